```python
import jax, jax.numpy as jnp
from jax import lax
import numpy as np

D_MODEL = 1024
BATCH = 8
SEQ = 4096
DEPTH = 2

D_MIX = D_MODEL
D_CONF = D_MIX // 2
D_SC = D_MIX - D_CONF
N_GROUPS_CONF = 8
N_GROUPS_SC = 8
CONF_KERNEL = 31
SC_KERNEL = 3
FFN_KERNEL = 3
D_FF = 2816
D_IN = 2 * D_CONF + 3 * D_SC
EPS = 1e-6

kernel_name = "hybrid_conformer_shortconv_convffn"


def rmsnorm(x, g):
    xf = x.astype(jnp.float32)
    y = xf * lax.rsqrt(jnp.mean(xf * xf, axis=-1, keepdims=True) + EPS)
    return (y * g.astype(jnp.float32)).astype(x.dtype)


def layernorm(x, g, b):
    xf = x.astype(jnp.float32)
    mu = jnp.mean(xf, axis=-1, keepdims=True)
    xc = xf - mu
    var = jnp.mean(xc * xc, axis=-1, keepdims=True)
    y = xc * lax.rsqrt(var + EPS)
    return (y * g.astype(jnp.float32) + b.astype(jnp.float32)).astype(x.dtype)


def causal_dwconv(x, w):
    k, c = w.shape
    return lax.conv_general_dilated(
        x, w[:, None, :].astype(x.dtype),
        window_strides=(1,), padding=[(k - 1, 0)],
        dimension_numbers=("NWC", "WIO", "NWC"),
        feature_group_count=c)


def _fwd_setup_inputs(seed: int = 0) -> dict:
    key = jax.random.key(seed)
    ks = jax.random.split(key, 16)
    f32 = jnp.float32
    x = jax.random.normal(ks[0], (BATCH, SEQ, D_MODEL), f32)
    mix_norm_g = 1.0 + 0.02 * jax.random.normal(ks[1], (DEPTH, D_MODEL), f32)
    w_in = jax.random.normal(ks[2], (DEPTH, D_MODEL, D_IN), f32) * D_MODEL ** -0.5
    b_in = 0.02 * jax.random.normal(ks[3], (DEPTH, D_IN), f32)
    conv_a_w = jax.random.normal(ks[4], (DEPTH, CONF_KERNEL, D_CONF), f32) * CONF_KERNEL ** -0.5
    conv_a_b = 0.02 * jax.random.normal(ks[5], (DEPTH, D_CONF), f32)
    ln_a_g = 1.0 + 0.02 * jax.random.normal(ks[6], (DEPTH, D_CONF), f32)
    ln_a_b = 0.02 * jax.random.normal(ks[7], (DEPTH, D_CONF), f32)
    conv_b_w = jax.random.normal(ks[8], (DEPTH, SC_KERNEL, D_SC), f32) * SC_KERNEL ** -0.5
    w_out = jax.random.normal(ks[9], (DEPTH, D_MIX, D_MODEL), f32) * D_MIX ** -0.5
    ffn_norm_g = 1.0 + 0.02 * jax.random.normal(ks[10], (DEPTH, D_MODEL), f32)
    w_up = jax.random.normal(ks[11], (DEPTH, D_MODEL, 2 * D_FF), f32) * D_MODEL ** -0.5
    conv_f_w = jax.random.normal(ks[12], (DEPTH, FFN_KERNEL, 2 * D_FF), f32) * FFN_KERNEL ** -0.5
    w_down = jax.random.normal(ks[13], (DEPTH, D_FF, D_MODEL), f32) * D_FF ** -0.5
    final_norm_g = 1.0 + 0.02 * jax.random.normal(ks[14], (D_MODEL,), f32)
    return {"x": x, "mix_norm_g": mix_norm_g, "w_in": w_in, "b_in": b_in,
            "conv_a_w": conv_a_w, "conv_a_b": conv_a_b, "ln_a_g": ln_a_g, "ln_a_b": ln_a_b,
            "conv_b_w": conv_b_w, "w_out": w_out, "ffn_norm_g": ffn_norm_g, "w_up": w_up,
            "conv_f_w": conv_f_w, "w_down": w_down, "final_norm_g": final_norm_g}


def token_mixer(h, w_in, b_in, conv_a_w, conv_a_b, ln_a_g, ln_a_b, conv_b_w, w_out):
    u = jnp.einsum("bsd,de->bse", h, w_in) + b_in.astype(h.dtype)
    a_val, a_gate, g_b, g_c, v_sc = jnp.split(
        u, [D_CONF, 2 * D_CONF, 2 * D_CONF + D_SC, 2 * D_CONF + 2 * D_SC], axis=-1)
    a = a_val * jax.nn.sigmoid(a_gate)
    a = causal_dwconv(a, conv_a_w) + conv_a_b.astype(a.dtype)
    a = jax.nn.silu(layernorm(a, ln_a_g, ln_a_b))
    s = g_b * causal_dwconv(g_c * v_sc, conv_b_w)
    y = jnp.concatenate([a, s], axis=-1)
    return jnp.einsum("bse,ed->bsd", y, w_out)


def conv_ffn(h, w_up, conv_f_w, w_down):
    u = jnp.einsum("bsd,df->bsf", h, w_up)
    u = causal_dwconv(u, conv_f_w)
    gate, val = jnp.split(u, 2, axis=-1)
    return jnp.einsum("bsf,fd->bsd", jax.nn.silu(gate) * val, w_down)


def _fwd_reference(x, mix_norm_g, w_in, b_in, conv_a_w, conv_a_b, ln_a_g, ln_a_b,
              conv_b_w, w_out, ffn_norm_g, w_up, conv_f_w, w_down, final_norm_g):
    for l in range(DEPTH):
        h = rmsnorm(x, mix_norm_g[l])
        x = x + token_mixer(h, w_in[l], b_in[l], conv_a_w[l], conv_a_b[l],
                            ln_a_g[l], ln_a_b[l], conv_b_w[l], w_out[l])
        h = rmsnorm(x, ffn_norm_g[l])
        x = x + conv_ffn(h, w_up[l], conv_f_w[l], w_down[l])
    return rmsnorm(x, final_norm_g)


import jax as _jax
import jax.numpy as _jnp

TWIN_FORMAT = 'train_step'
FWD_PARAMS = ['x', 'mix_norm_g', 'w_in', 'b_in', 'conv_a_w', 'conv_a_b', 'ln_a_g', 'ln_a_b', 'conv_b_w', 'w_out', 'ffn_norm_g', 'w_up', 'conv_f_w', 'w_down', 'final_norm_g']
TWIN_WEIGHTS = ['mix_norm_g', 'w_in', 'b_in', 'conv_a_w', 'conv_a_b', 'ln_a_g', 'ln_a_b', 'conv_b_w', 'w_out', 'ffn_norm_g', 'w_up', 'conv_f_w', 'w_down', 'final_norm_g']
TWIN_DIFF_INPUT = 'x'
TWIN_INPUTS = ['x', 'mix_norm_g', 'w_in', 'b_in', 'conv_a_w', 'conv_a_b', 'ln_a_g', 'ln_a_b', 'conv_b_w', 'w_out', 'ffn_norm_g', 'w_up', 'conv_f_w', 'w_down', 'final_norm_g', 'loss_target', 'm_mix_norm_g', 'm_w_in', 'm_b_in', 'm_conv_a_w', 'm_conv_a_b', 'm_ln_a_g', 'm_ln_a_b', 'm_conv_b_w', 'm_w_out', 'm_ffn_norm_g', 'm_w_up', 'm_conv_f_w', 'm_w_down', 'm_final_norm_g', 'v_mix_norm_g', 'v_w_in', 'v_b_in', 'v_conv_a_w', 'v_conv_a_b', 'v_ln_a_g', 'v_ln_a_b', 'v_conv_b_w', 'v_w_out', 'v_ffn_norm_g', 'v_w_up', 'v_conv_f_w', 'v_w_down', 'v_final_norm_g']
TWIN_OUTPUTS = ['loss', 'grad_x', 'grad_mix_norm_g', 'grad_w_in', 'grad_b_in', 'grad_conv_a_w', 'grad_conv_a_b', 'grad_ln_a_g', 'grad_ln_a_b', 'grad_conv_b_w', 'grad_w_out', 'grad_ffn_norm_g', 'grad_w_up', 'grad_conv_f_w', 'grad_w_down', 'grad_final_norm_g', 'delta_mix_norm_g', 'delta_w_in', 'delta_b_in', 'delta_conv_a_w', 'delta_conv_a_b', 'delta_ln_a_g', 'delta_ln_a_b', 'delta_conv_b_w', 'delta_w_out', 'delta_ffn_norm_g', 'delta_w_up', 'delta_conv_f_w', 'delta_w_down', 'delta_final_norm_g', 'new_m_mix_norm_g', 'new_m_w_in', 'new_m_b_in', 'new_m_conv_a_w', 'new_m_conv_a_b', 'new_m_ln_a_g', 'new_m_ln_a_b', 'new_m_conv_b_w', 'new_m_w_out', 'new_m_ffn_norm_g', 'new_m_w_up', 'new_m_conv_f_w', 'new_m_w_down', 'new_m_final_norm_g', 'new_v_mix_norm_g', 'new_v_w_in', 'new_v_b_in', 'new_v_conv_a_w', 'new_v_conv_a_b', 'new_v_ln_a_g', 'new_v_ln_a_b', 'new_v_conv_b_w', 'new_v_w_out', 'new_v_ffn_norm_g', 'new_v_w_up', 'new_v_conv_f_w', 'new_v_w_down', 'new_v_final_norm_g']
TWIN_LEAF_KINDS = {'loss': 'loss', 'grad_x': 'grad_x', 'grad_mix_norm_g': 'grad_w', 'grad_w_in': 'grad_w', 'grad_b_in': 'grad_w', 'grad_conv_a_w': 'grad_w', 'grad_conv_a_b': 'grad_w', 'grad_ln_a_g': 'grad_w', 'grad_ln_a_b': 'grad_w', 'grad_conv_b_w': 'grad_w', 'grad_w_out': 'grad_w', 'grad_ffn_norm_g': 'grad_w', 'grad_w_up': 'grad_w', 'grad_conv_f_w': 'grad_w', 'grad_w_down': 'grad_w', 'grad_final_norm_g': 'grad_w', 'delta_mix_norm_g': 'delta_w', 'delta_w_in': 'delta_w', 'delta_b_in': 'delta_w', 'delta_conv_a_w': 'delta_w', 'delta_conv_a_b': 'delta_w', 'delta_ln_a_g': 'delta_w', 'delta_ln_a_b': 'delta_w', 'delta_conv_b_w': 'delta_w', 'delta_w_out': 'delta_w', 'delta_ffn_norm_g': 'delta_w', 'delta_w_up': 'delta_w', 'delta_conv_f_w': 'delta_w', 'delta_w_down': 'delta_w', 'delta_final_norm_g': 'delta_w', 'new_m_mix_norm_g': 'new_m', 'new_m_w_in': 'new_m', 'new_m_b_in': 'new_m', 'new_m_conv_a_w': 'new_m', 'new_m_conv_a_b': 'new_m', 'new_m_ln_a_g': 'new_m', 'new_m_ln_a_b': 'new_m', 'new_m_conv_b_w': 'new_m', 'new_m_w_out': 'new_m', 'new_m_ffn_norm_g': 'new_m', 'new_m_w_up': 'new_m', 'new_m_conv_f_w': 'new_m', 'new_m_w_down': 'new_m', 'new_m_final_norm_g': 'new_m', 'new_v_mix_norm_g': 'new_v', 'new_v_w_in': 'new_v', 'new_v_b_in': 'new_v', 'new_v_conv_a_w': 'new_v', 'new_v_conv_a_b': 'new_v', 'new_v_ln_a_g': 'new_v', 'new_v_ln_a_b': 'new_v', 'new_v_conv_b_w': 'new_v', 'new_v_w_out': 'new_v', 'new_v_ffn_norm_g': 'new_v', 'new_v_w_up': 'new_v', 'new_v_conv_f_w': 'new_v', 'new_v_w_down': 'new_v', 'new_v_final_norm_g': 'new_v'}


def _forward(args):
    return _fwd_reference(*[args[k] for k in FWD_PARAMS])


def _output_shape():
    out = _jax.eval_shape(lambda: _forward(_fwd_setup_inputs(0)))
    return out.shape, out.dtype

N_MICROBATCH = 1
ADAM_LR = 0.001
ADAM_B1 = 0.9
ADAM_B2 = 0.999
ADAM_EPS = 1e-08
ADAM_WD = 0.01
ADAM_STEP = 10
PER_EXAMPLE_BATCH_AXIS = {'x': 0, 'loss_target': 0}
SHARED_INPUTS = []
_WEIGHT_DTYPES = {'mix_norm_g': _jnp.float32, 'w_in': _jnp.float32, 'b_in': _jnp.float32, 'conv_a_w': _jnp.float32, 'conv_a_b': _jnp.float32, 'ln_a_g': _jnp.float32, 'ln_a_b': _jnp.float32, 'conv_b_w': _jnp.float32, 'w_out': _jnp.float32, 'ffn_norm_g': _jnp.float32, 'w_up': _jnp.float32, 'conv_f_w': _jnp.float32, 'w_down': _jnp.float32, 'final_norm_g': _jnp.float32}
MOMENT_SCALE = {'mix_norm_g': 2.062344e-01, 'w_in': 1.290680e-01, 'b_in': 1.349227e-01, 'conv_a_w': 9.757971e-02, 'conv_a_b': 2.160581e-01, 'ln_a_g': 1.158070e-01, 'ln_a_b': 1.054620e-01, 'conv_b_w': 1.663941e-01, 'w_out': 1.278991e-01, 'ffn_norm_g': 1.156142e-01, 'w_up': 5.001528e-02, 'conv_f_w': 4.976509e-02, 'w_down': 8.175200e-02, 'final_norm_g': 3.203884e+01}


def _to_microbatches(a, axis):
    t = _jnp.moveaxis(a, axis, 0)
    t = t.reshape((N_MICROBATCH, t.shape[0] // N_MICROBATCH) + t.shape[1:])
    return _jnp.moveaxis(t, 1, axis + 1)


def setup_inputs(seed: int = 0) -> dict:
    inp = _fwd_setup_inputs(seed)
    key = _jax.random.fold_in(_jax.random.key(seed), 7919)
    shape, _ = _output_shape()
    out = dict(inp)
    out["loss_target"] = _jax.random.normal(_jax.random.fold_in(key, 0), shape, _jnp.float32)
    for i, name in enumerate(TWIN_WEIGHTS):
        w = inp[name].astype(_jnp.float32)
        if MOMENT_SCALE is None:
            s = _jnp.sqrt(_jnp.mean(_jnp.square(w)) + 1e-30)
        else:
            s = MOMENT_SCALE[name]
        km, kv = _jax.random.split(_jax.random.fold_in(key, i + 1))
        out[name] = w
        out["m_" + name] = s * _jax.random.normal(km, w.shape, _jnp.float32)
        out["v_" + name] = (s * s) * _jax.random.uniform(kv, w.shape, _jnp.float32, 0.5, 1.5)
    if N_MICROBATCH > 1:
        for name, axis in PER_EXAMPLE_BATCH_AXIS.items():
            out[name] = _to_microbatches(out[name], axis)
    return {'x': out['x'], 'mix_norm_g': out['mix_norm_g'], 'w_in': out['w_in'], 'b_in': out['b_in'], 'conv_a_w': out['conv_a_w'], 'conv_a_b': out['conv_a_b'], 'ln_a_g': out['ln_a_g'], 'ln_a_b': out['ln_a_b'], 'conv_b_w': out['conv_b_w'], 'w_out': out['w_out'], 'ffn_norm_g': out['ffn_norm_g'], 'w_up': out['w_up'], 'conv_f_w': out['conv_f_w'], 'w_down': out['w_down'], 'final_norm_g': out['final_norm_g'], 'loss_target': out['loss_target'], 'm_mix_norm_g': out['m_mix_norm_g'], 'm_w_in': out['m_w_in'], 'm_b_in': out['m_b_in'], 'm_conv_a_w': out['m_conv_a_w'], 'm_conv_a_b': out['m_conv_a_b'], 'm_ln_a_g': out['m_ln_a_g'], 'm_ln_a_b': out['m_ln_a_b'], 'm_conv_b_w': out['m_conv_b_w'], 'm_w_out': out['m_w_out'], 'm_ffn_norm_g': out['m_ffn_norm_g'], 'm_w_up': out['m_w_up'], 'm_conv_f_w': out['m_conv_f_w'], 'm_w_down': out['m_w_down'], 'm_final_norm_g': out['m_final_norm_g'], 'v_mix_norm_g': out['v_mix_norm_g'], 'v_w_in': out['v_w_in'], 'v_b_in': out['v_b_in'], 'v_conv_a_w': out['v_conv_a_w'], 'v_conv_a_b': out['v_conv_a_b'], 'v_ln_a_g': out['v_ln_a_g'], 'v_ln_a_b': out['v_ln_a_b'], 'v_conv_b_w': out['v_conv_b_w'], 'v_w_out': out['v_w_out'], 'v_ffn_norm_g': out['v_ffn_norm_g'], 'v_w_up': out['v_w_up'], 'v_conv_f_w': out['v_conv_f_w'], 'v_w_down': out['v_w_down'], 'v_final_norm_g': out['v_final_norm_g']}


def _loss(weights, diff, rest, loss_target):
    with _jax.named_scope("forward"):
        args = {**rest, TWIN_DIFF_INPUT: diff, **{k: w.astype(_WEIGHT_DTYPES[k]) for k, w in weights.items()}}
        y = _forward(args)
    with _jax.named_scope("loss_head"):
        err = _jnp.square(y.astype(_jnp.float32) - loss_target)
        return 0.5 * _jnp.sum(_jnp.mean(err, axis=-1)) if err.ndim else 0.5 * err


def _adamw(w, g, m, v):
    m = ADAM_B1 * m + (1.0 - ADAM_B1) * g
    v = ADAM_B2 * v + (1.0 - ADAM_B2) * _jnp.square(g)
    m_hat = m / (1.0 - ADAM_B1 ** ADAM_STEP)
    v_hat = v / (1.0 - ADAM_B2 ** ADAM_STEP)
    delta = -ADAM_LR * (m_hat / (_jnp.sqrt(v_hat) + ADAM_EPS) + ADAM_WD * w)
    return delta, m, v


def reference(x, mix_norm_g, w_in, b_in, conv_a_w, conv_a_b, ln_a_g, ln_a_b, conv_b_w, w_out, ffn_norm_g, w_up, conv_f_w, w_down, final_norm_g, loss_target, m_mix_norm_g, m_w_in, m_b_in, m_conv_a_w, m_conv_a_b, m_ln_a_g, m_ln_a_b, m_conv_b_w, m_w_out, m_ffn_norm_g, m_w_up, m_conv_f_w, m_w_down, m_final_norm_g, v_mix_norm_g, v_w_in, v_b_in, v_conv_a_w, v_conv_a_b, v_ln_a_g, v_ln_a_b, v_conv_b_w, v_w_out, v_ffn_norm_g, v_w_up, v_conv_f_w, v_w_down, v_final_norm_g):
    given = dict(x=x, mix_norm_g=mix_norm_g, w_in=w_in, b_in=b_in, conv_a_w=conv_a_w, conv_a_b=conv_a_b, ln_a_g=ln_a_g, ln_a_b=ln_a_b, conv_b_w=conv_b_w, w_out=w_out, ffn_norm_g=ffn_norm_g, w_up=w_up, conv_f_w=conv_f_w, w_down=w_down, final_norm_g=final_norm_g, loss_target=loss_target, m_mix_norm_g=m_mix_norm_g, m_w_in=m_w_in, m_b_in=m_b_in, m_conv_a_w=m_conv_a_w, m_conv_a_b=m_conv_a_b, m_ln_a_g=m_ln_a_g, m_ln_a_b=m_ln_a_b, m_conv_b_w=m_conv_b_w, m_w_out=m_w_out, m_ffn_norm_g=m_ffn_norm_g, m_w_up=m_w_up, m_conv_f_w=m_conv_f_w, m_w_down=m_w_down, m_final_norm_g=m_final_norm_g, v_mix_norm_g=v_mix_norm_g, v_w_in=v_w_in, v_b_in=v_b_in, v_conv_a_w=v_conv_a_w, v_conv_a_b=v_conv_a_b, v_ln_a_g=v_ln_a_g, v_ln_a_b=v_ln_a_b, v_conv_b_w=v_conv_b_w, v_w_out=v_w_out, v_ffn_norm_g=v_ffn_norm_g, v_w_up=v_w_up, v_conv_f_w=v_conv_f_w, v_w_down=v_w_down, v_final_norm_g=v_final_norm_g)
    weights = {n: given[n] for n in TWIN_WEIGHTS}
    shared = {n: given[n] for n in SHARED_INPUTS}
    per_example = {n: given[n] for n in ['x']}
    grad_fn = _jax.value_and_grad(_loss, argnums=(0, 1))

    def one_microbatch(ex, loss_target):
        ex = dict(ex)
        diff = ex.pop(TWIN_DIFF_INPUT)
        return grad_fn(weights, diff, {**shared, **ex}, loss_target)

    if N_MICROBATCH == 1:
        loss, (grad_w, grad_x) = one_microbatch(per_example, given["loss_target"])
    else:
        def body(carry, xs):
            loss_sum, grad_sum = carry
            l_k, (gw_k, gx_k) = one_microbatch(xs[0], xs[1])
            with _jax.named_scope("update"):
                return (loss_sum + l_k, _jax.tree.map(_jnp.add, grad_sum, gw_k)), gx_k

        init = (_jnp.zeros((), _jnp.float32), _jax.tree.map(_jnp.zeros_like, weights))
        (loss, grad_w), grad_x = _jax.lax.scan(body, init, (per_example, given["loss_target"]))
    with _jax.named_scope("update"):
        delta_w, new_m, new_v = {}, {}, {}
        for n in TWIN_WEIGHTS:
            delta_w[n], new_m[n], new_v[n] = _adamw(weights[n], grad_w[n], given["m_" + n], given["v_" + n])
    return (loss, grad_x, *[grad_w[n] for n in TWIN_WEIGHTS], *[delta_w[n] for n in TWIN_WEIGHTS],
            *[new_m[n] for n in TWIN_WEIGHTS], *[new_v[n] for n in TWIN_WEIGHTS])
```

```python
import functools

import jax
import jax.numpy as jnp
from jax import lax
from jax.experimental import pallas as pl
from jax.experimental.pallas import tpu as pltpu

F32 = jnp.float32
BF16 = jnp.bfloat16

N_DEV = 8
N_CHIP = 4
D_CONF = 512
CONF_K = 31
SHORT_K = 3
EPS = 1e-6
HALO = 32
HALO3 = 8
LANES = 128
SUB = 8
VMEM_LIMIT = 52 * 1024 * 1024

ADAM_LR = 0.001
ADAM_B1 = 0.9
ADAM_B2 = 0.999
ADAM_EPS = 1e-08
ADAM_WD = 0.01
ADAM_STEP = 10

MESH = pl.DeviceIdType.MESH
ANY = pl.BlockSpec(memory_space=pl.ANY)


def _params(*sem):
    return pltpu.CompilerParams(dimension_semantics=sem, vmem_limit_bytes=VMEM_LIMIT)


def _row_loop(n_rows, rb, fn):
    def body(i, carry):
        fn(pl.ds(pl.multiple_of(i * rb, rb), rb))
        return carry
    lax.fori_loop(0, n_rows // rb, body, 0)


def _rows8(v):
    acc = v[0:SUB]
    for k in range(1, v.shape[0] // SUB):
        acc = acc + v[k * SUB:(k + 1) * SUB]
    return acc


def _sigmoid(z):
    return 1.0 / (1.0 + jnp.exp(-z))


def _dot(a, b):
    return jnp.dot(a, b, preferred_element_type=F32)


def _dot_nt(a, b):
    return lax.dot_general(a, b, (((1,), (1,)), ((), ())), preferred_element_type=F32)


def _dot_tn(a, b):
    return lax.dot_general(a, b, (((0,), (0,)), ((), ())), preferred_element_type=F32)


def _conv_taps(win, w_ref, out, *, taps, n_rows, base, width, transposed=False, bias_ref=None):
    rb = min(64, n_rows)

    def lane_body(cb, carry):
        lanes = pl.ds(pl.multiple_of(cb * LANES, LANES), LANES)
        for r0 in range(0, n_rows, rb):
            acc = None
            for k in range(taps):
                off = (taps - 1 - k) if transposed else (k - (taps - 1))
                term = w_ref[pl.ds(k, 1), lanes] * win[pl.ds(base + r0 + off, rb), lanes]
                acc = term if acc is None else acc + term
            if bias_ref is not None:
                acc = acc + bias_ref[:, lanes]
            out[pl.ds(r0, rb), lanes] = acc.astype(out.dtype)
        return carry

    lax.fori_loop(0, width // LANES, lane_body, 0)


def _conv_wgrad(dy, win, dw_acc, *, taps, n_rows, base, width):
    rb = min(64, n_rows)

    def lane_body(cb, carry):
        lanes = pl.ds(pl.multiple_of(cb * LANES, LANES), LANES)
        for k in range(taps):
            acc = None
            for r0 in range(0, n_rows, rb):
                prod = dy[pl.ds(r0, rb), lanes] * win[pl.ds(base + r0 + k - (taps - 1), rb), lanes]
                acc = prod if acc is None else acc + prod
            dw_acc[pl.ds(k * SUB, SUB), lanes] += _rows8(acc)
        return carry

    lax.fori_loop(0, width // LANES, lane_body, 0)


def _fold8(acc_ref, taps):
    return jnp.concatenate(
        [jnp.sum(acc_ref[pl.ds(k * SUB, SUB), :], axis=0, keepdims=True) for k in range(taps)], axis=0)


def _seq_tile(s_len):
    return min(256, s_len)


def _ff_chunk(ff):
    best = LANES
    for c in range(LANES, 1408 + 1, LANES):
        if ff % c == 0:
            best = c
    return best


def _rms_matmul(x, g, w, b, *, name):
    s_len, d = x.shape
    n = w.shape[1]
    tm = min(512, s_len)
    tn = 512 if n % 512 == 0 else LANES
    has_bias = b is not None

    def body(*refs):
        if has_bias:
            x_ref, g_ref, w_ref, b_ref, o_ref, h_ref = refs
        else:
            x_ref, g_ref, w_ref, o_ref, h_ref = refs

        @pl.when(pl.program_id(1) == 0)
        def _():
            def blk(rows):
                xv = x_ref[rows, :]
                r = lax.rsqrt(jnp.mean(xv * xv, axis=-1, keepdims=True) + EPS)
                h_ref[rows, :] = ((xv * r) * g_ref[...]).astype(BF16)
            _row_loop(tm, 32, blk)

        acc = _dot(h_ref[...], w_ref[...])
        if has_bias:
            acc = acc + b_ref[...]
        o_ref[...] = acc

    in_specs = [pl.BlockSpec((tm, d), lambda i, j: (i, 0)), pl.BlockSpec((1, d), lambda i, j: (0, 0)),
                pl.BlockSpec((d, tn), lambda i, j: (0, j))]
    args = [x, g, w]
    if has_bias:
        in_specs.append(pl.BlockSpec((1, tn), lambda i, j: (0, j)))
        args.append(b)
    return pl.pallas_call(
        body, grid=(s_len // tm, n // tn), in_specs=in_specs,
        out_specs=pl.BlockSpec((tm, tn), lambda i, j: (i, j)),
        out_shape=jax.ShapeDtypeStruct((s_len, n), F32),
        scratch_shapes=[pltpu.VMEM((tm, d), BF16)],
        compiler_params=_params("parallel", "arbitrary"), name=name,
    )(*args)


def _mix_windows(u_ref, uh_ref, gw, pw, first, t):
    c = D_CONF
    hv = uh_ref[:, 0:c] * _sigmoid(uh_ref[:, c:2 * c])
    gw[0:HALO, :] = jnp.where(first, 0.0, hv)
    hp = uh_ref[HALO - HALO3:HALO, 3 * c:4 * c] * uh_ref[HALO - HALO3:HALO, 4 * c:5 * c]
    pw[0:HALO3, :] = jnp.where(first, 0.0, hp)

    def blk(rows):
        dst = pl.ds(pl.multiple_of(rows.start + HALO, SUB), rows.size)
        gw[dst, :] = u_ref[rows, 0:c] * _sigmoid(u_ref[rows, c:2 * c])
        dst3 = pl.ds(pl.multiple_of(rows.start + HALO3, SUB), rows.size)
        pw[dst3, :] = u_ref[rows, 3 * c:4 * c] * u_ref[rows, 4 * c:5 * c]
    _row_loop(t, 32, blk)


def _mix_fwd(u, x0, wa, ba, lg, lb, wb, w_out, *, name):
    s_len, d_in = u.shape
    d = x0.shape[1]
    c = D_CONF
    t = _seq_tile(s_len)
    per = t // HALO

    def body(u_ref, uh_ref, x0_ref, wa_ref, ba_ref, lg_ref, lb_ref, wb_ref, wo_ref, y_ref, x1_ref,
             gw, pw, ca, cb):
        first = pl.program_id(0) == 0
        _mix_windows(u_ref, uh_ref, gw, pw, first, t)
        _conv_taps(gw, wa_ref, ca, taps=CONF_K, n_rows=t, base=HALO, width=c, bias_ref=ba_ref)
        _conv_taps(pw, wb_ref, cb, taps=SHORT_K, n_rows=t, base=HALO3, width=c)

        def blk(rows):
            cv = ca[rows, :]
            mu = jnp.mean(cv, axis=-1, keepdims=True)
            xc = cv - mu
            var = jnp.mean(xc * xc, axis=-1, keepdims=True)
            ln = (xc * lax.rsqrt(var + EPS)) * lg_ref[...] + lb_ref[...]
            y_ref[rows, 0:c] = (ln * _sigmoid(ln)).astype(BF16)
            y_ref[rows, c:2 * c] = (u_ref[rows, 2 * c:3 * c] * cb[rows, :]).astype(BF16)
        _row_loop(t, 32, blk)
        x1_ref[...] = x0_ref[...] + _dot(y_ref[...], wo_ref[...])

    small = lambda r: pl.BlockSpec((r, c), lambda i: (0, 0))
    return pl.pallas_call(
        body, grid=(s_len // t,),
        in_specs=[pl.BlockSpec((t, d_in), lambda i: (i, 0)),
                  pl.BlockSpec((HALO, d_in), lambda i: (jnp.maximum(i * per - 1, 0), 0)),
                  pl.BlockSpec((t, d), lambda i: (i, 0)),
                  small(CONF_K), small(1), small(1), small(1), small(SHORT_K),
                  pl.BlockSpec((2 * c, d), lambda i: (0, 0))],
        out_specs=[pl.BlockSpec((t, 2 * c), lambda i: (i, 0)), pl.BlockSpec((t, d), lambda i: (i, 0))],
        out_shape=[jax.ShapeDtypeStruct((s_len, 2 * c), BF16), jax.ShapeDtypeStruct((s_len, d), F32)],
        scratch_shapes=[pltpu.VMEM((HALO + t, c), F32), pltpu.VMEM((HALO3 + t, c), F32),
                        pltpu.VMEM((t, c), F32), pltpu.VMEM((t, c), F32)],
        compiler_params=_params("arbitrary"), name=name,
    )(u, u, x0, wa, ba, lg, lb, wb, w_out)


def _ffn_windows(ug_ref, ugh_ref, uv_ref, uvh_ref, gwin, vwin, first, t):
    gwin[0:HALO3, :] = jnp.where(first, 0.0, ugh_ref[...])
    vwin[0:HALO3, :] = jnp.where(first, 0.0, uvh_ref[...])

    def blk(rows):
        dst = pl.ds(pl.multiple_of(rows.start + HALO3, SUB), rows.size)
        gwin[dst, :] = ug_ref[rows, :]
        vwin[dst, :] = uv_ref[rows, :]
    _row_loop(t, 32, blk)


def _ffn_fwd(uf, x1, wf, w_down, *, name):
    s_len, ff2 = uf.shape
    ff = ff2 // 2
    d = x1.shape[1]
    t = _seq_tile(s_len)
    fc = _ff_chunk(ff)
    nc = ff // fc
    per = t // HALO3

    def body(ug_ref, ugh_ref, uv_ref, uvh_ref, x1_ref, wfg_ref, wfv_ref, wd_ref, act_ref, x2_ref,
             gwin, vwin, cg, cv):
        first = pl.program_id(0) == 0
        _ffn_windows(ug_ref, ugh_ref, uv_ref, uvh_ref, gwin, vwin, first, t)
        _conv_taps(gwin, wfg_ref, cg, taps=SHORT_K, n_rows=t, base=HALO3, width=fc)
        _conv_taps(vwin, wfv_ref, cv, taps=SHORT_K, n_rows=t, base=HALO3, width=fc)

        def blk(rows):
            gv = cg[rows, :]
            act_ref[rows, :] = ((gv * _sigmoid(gv)) * cv[rows, :]).astype(BF16)
        _row_loop(t, 32, blk)

        @pl.when(pl.program_id(1) == 0)
        def _():
            x2_ref[...] = x1_ref[...]
        x2_ref[...] += _dot(act_ref[...], wd_ref[...])

    halo_map = lambda off: (lambda i, j: (jnp.maximum(i * per - 1, 0), j + off))
    return pl.pallas_call(
        body, grid=(s_len // t, nc),
        in_specs=[pl.BlockSpec((t, fc), lambda i, j: (i, j)), pl.BlockSpec((HALO3, fc), halo_map(0)),
                  pl.BlockSpec((t, fc), lambda i, j: (i, j + nc)), pl.BlockSpec((HALO3, fc), halo_map(nc)),
                  pl.BlockSpec((t, d), lambda i, j: (i, 0)),
                  pl.BlockSpec((SHORT_K, fc), lambda i, j: (0, j)),
                  pl.BlockSpec((SHORT_K, fc), lambda i, j: (0, j + nc)),
                  pl.BlockSpec((fc, d), lambda i, j: (j, 0))],
        out_specs=[pl.BlockSpec((t, fc), lambda i, j: (i, j)), pl.BlockSpec((t, d), lambda i, j: (i, 0))],
        out_shape=[jax.ShapeDtypeStruct((s_len, ff), BF16), jax.ShapeDtypeStruct((s_len, d), F32)],
        scratch_shapes=[pltpu.VMEM((HALO3 + t, fc), F32), pltpu.VMEM((HALO3 + t, fc), F32),
                        pltpu.VMEM((t, fc), F32), pltpu.VMEM((t, fc), F32)],
        compiler_params=_params("parallel", "arbitrary"), name=name,
    )(uf, uf, uf, uf, x1, wf, wf, w_down)


def _loss_bwd(x, g, target, *, name):
    s_len, d = x.shape
    t = _seq_tile(s_len)

    def body(x_ref, g_ref, t_ref, l_ref, dx_ref, dg_ref):
        @pl.when(pl.program_id(0) == 0)
        def _():
            l_ref[...] = jnp.zeros_like(l_ref)
            dg_ref[...] = jnp.zeros_like(dg_ref)

        def blk(rows):
            xv = x_ref[rows, :]
            r = lax.rsqrt(jnp.mean(xv * xv, axis=-1, keepdims=True) + EPS)
            xn = xv * r
            e = xn * g_ref[...] - t_ref[rows, :]
            l_ref[...] += _rows8(e * e)
            dy = e * (1.0 / d)
            dg_ref[...] += _rows8(dy * xn)
            dn = dy * g_ref[...]
            dx_ref[rows, :] = r * (dn - xn * jnp.mean(dn * xn, axis=-1, keepdims=True))
        _row_loop(t, 32, blk)

    return pl.pallas_call(
        body, grid=(s_len // t,),
        in_specs=[pl.BlockSpec((t, d), lambda i: (i, 0)), pl.BlockSpec((1, d), lambda i: (0, 0)),
                  pl.BlockSpec((t, d), lambda i: (i, 0))],
        out_specs=[pl.BlockSpec((SUB, d), lambda i: (0, 0)), pl.BlockSpec((t, d), lambda i: (i, 0)),
                   pl.BlockSpec((SUB, d), lambda i: (0, 0))],
        out_shape=[jax.ShapeDtypeStruct((SUB, d), F32), jax.ShapeDtypeStruct((s_len, d), F32),
                   jax.ShapeDtypeStruct((SUB, d), F32)],
        compiler_params=_params("arbitrary"), name=name,
    )(x, g, target)


def _ffn_bwd(dx2, uf, wf, w_down, *, name):
    s_len, ff2 = uf.shape
    ff = ff2 // 2
    d = dx2.shape[1]
    t = _seq_tile(s_len)
    n_t = s_len // t
    fc = _ff_chunk(ff)
    nc = ff // fc
    per = t // HALO3

    def body(dx_ref, ug_ref, ugh_ref, uv_ref, uvh_ref, wfg_ref, wfv_ref, wd_ref,
             dug_ref, duv_ref, dwg_ref, dwv_ref, gwin, vwin, cg, cv, dact, dgw, dvw, awg, awv):
        i = pl.program_id(1)
        first = i == n_t - 1
        _ffn_windows(ug_ref, ugh_ref, uv_ref, uvh_ref, gwin, vwin, first, t)
        _conv_taps(gwin, wfg_ref, cg, taps=SHORT_K, n_rows=t, base=HALO3, width=fc)
        _conv_taps(vwin, wfv_ref, cv, taps=SHORT_K, n_rows=t, base=HALO3, width=fc)
        dact[...] = _dot_nt(dx_ref[...].astype(BF16), wd_ref[...])

        @pl.when(i == 0)
        def _():
            dgw[t:t + HALO3, :] = jnp.zeros((HALO3, fc), F32)
            dvw[t:t + HALO3, :] = jnp.zeros((HALO3, fc), F32)
            awg[...] = jnp.zeros_like(awg)
            awv[...] = jnp.zeros_like(awv)

        def blk(rows):
            gv = cg[rows, :]
            sg = _sigmoid(gv)
            da = dact[rows, :]
            dgw[rows, :] = (da * cv[rows, :]) * (sg * (1.0 + gv * (1.0 - sg)))
            dvw[rows, :] = da * (gv * sg)
        _row_loop(t, 32, blk)

        _conv_taps(dgw, wfg_ref, dug_ref, taps=SHORT_K, n_rows=t, base=0, width=fc, transposed=True)
        _conv_taps(dvw, wfv_ref, duv_ref, taps=SHORT_K, n_rows=t, base=0, width=fc, transposed=True)
        _conv_wgrad(dgw, gwin, awg, taps=SHORT_K, n_rows=t, base=HALO3, width=fc)
        _conv_wgrad(dvw, vwin, awv, taps=SHORT_K, n_rows=t, base=HALO3, width=fc)
        dgw[t:t + HALO3, :] = dgw[0:HALO3, :]
        dvw[t:t + HALO3, :] = dvw[0:HALO3, :]

        @pl.when(i == n_t - 1)
        def _():
            dwg_ref[...] = _fold8(awg, SHORT_K)
            dwv_ref[...] = _fold8(awv, SHORT_K)

    rev = lambda i: n_t - 1 - i
    halo_map = lambda off: (lambda j, i: (jnp.maximum(rev(i) * per - 1, 0), j + off))
    return pl.pallas_call(
        body, grid=(nc, n_t),
        in_specs=[pl.BlockSpec((t, d), lambda j, i: (rev(i), 0)),
                  pl.BlockSpec((t, fc), lambda j, i: (rev(i), j)), pl.BlockSpec((HALO3, fc), halo_map(0)),
                  pl.BlockSpec((t, fc), lambda j, i: (rev(i), j + nc)), pl.BlockSpec((HALO3, fc), halo_map(nc)),
                  pl.BlockSpec((SHORT_K, fc), lambda j, i: (0, j)),
                  pl.BlockSpec((SHORT_K, fc), lambda j, i: (0, j + nc)),
                  pl.BlockSpec((fc, d), lambda j, i: (j, 0))],
        out_specs=[pl.BlockSpec((t, fc), lambda j, i: (rev(i), j)), pl.BlockSpec((t, fc), lambda j, i: (rev(i), j)),
                   pl.BlockSpec((SHORT_K, fc), lambda j, i: (0, j)), pl.BlockSpec((SHORT_K, fc), lambda j, i: (0, j))],
        out_shape=[jax.ShapeDtypeStruct((s_len, ff), BF16), jax.ShapeDtypeStruct((s_len, ff), BF16),
                   jax.ShapeDtypeStruct((SHORT_K, ff), F32), jax.ShapeDtypeStruct((SHORT_K, ff), F32)],
        scratch_shapes=[pltpu.VMEM((HALO3 + t, fc), F32), pltpu.VMEM((HALO3 + t, fc), F32),
                        pltpu.VMEM((t, fc), F32), pltpu.VMEM((t, fc), F32), pltpu.VMEM((t, fc), F32),
                        pltpu.VMEM((t + HALO3, fc), F32), pltpu.VMEM((t + HALO3, fc), F32),
                        pltpu.VMEM((SHORT_K * SUB, fc), F32), pltpu.VMEM((SHORT_K * SUB, fc), F32)],
        compiler_params=_params("arbitrary", "arbitrary"), name=name,
    )(dx2, uf, uf, uf, uf, wf, wf, w_down)


def _mix_bwd(dx1, u, wa, ba, lg, lb, wb, w_out, *, name):
    s_len, d_in = u.shape
    d = dx1.shape[1]
    c = D_CONF
    t = _seq_tile(s_len)
    n_t = s_len // t
    per = t // HALO

    def body(dx_ref, u_ref, uh_ref, wa_ref, ba_ref, lg_ref, lb_ref, wb_ref, wo_ref,
             du_ref, dwa_ref, dwb_ref, dba_ref, dlg_ref, dlb_ref, dbin_ref,
             gw, pw, ca, cb, dyc, dcaw, dcbw, dglu, dp, awa, awb):
        i = pl.program_id(0)
        first = i == n_t - 1
        _mix_windows(u_ref, uh_ref, gw, pw, first, t)
        _conv_taps(gw, wa_ref, ca, taps=CONF_K, n_rows=t, base=HALO, width=c, bias_ref=ba_ref)
        _conv_taps(pw, wb_ref, cb, taps=SHORT_K, n_rows=t, base=HALO3, width=c)
        dyc[...] = _dot_nt(dx_ref[...].astype(BF16), wo_ref[...])

        @pl.when(i == 0)
        def _():
            dcaw[t:t + HALO, :] = jnp.zeros((HALO, c), F32)
            dcbw[t:t + HALO3, :] = jnp.zeros((HALO3, c), F32)
            awa[...] = jnp.zeros_like(awa)
            awb[...] = jnp.zeros_like(awb)
            dba_ref[...] = jnp.zeros_like(dba_ref)
            dlg_ref[...] = jnp.zeros_like(dlg_ref)
            dlb_ref[...] = jnp.zeros_like(dlb_ref)
            dbin_ref[...] = jnp.zeros_like(dbin_ref)

        def blk1(rows):
            cv = ca[rows, :]
            mu = jnp.mean(cv, axis=-1, keepdims=True)
            xc = cv - mu
            rstd = lax.rsqrt(jnp.mean(xc * xc, axis=-1, keepdims=True) + EPS)
            nrm = xc * rstd
            ln = nrm * lg_ref[...] + lb_ref[...]
            sg = _sigmoid(ln)
            dln = dyc[rows, 0:c] * (sg * (1.0 + ln * (1.0 - sg)))
            dlg_ref[...] += _rows8(dln * nrm)
            dlb_ref[...] += _rows8(dln)
            dn = dln * lg_ref[...]
            dca = rstd * (dn - jnp.mean(dn, axis=-1, keepdims=True)
                          - nrm * jnp.mean(dn * nrm, axis=-1, keepdims=True))
            dcaw[rows, :] = dca
            dba_ref[...] += _rows8(dca)
            ds = dyc[rows, c:2 * c]
            dgb = ds * cb[rows, :]
            dcbw[rows, :] = ds * u_ref[rows, 2 * c:3 * c]
            du_ref[rows, 2 * c:3 * c] = dgb.astype(BF16)
            dbin_ref[:, 2 * c:3 * c] += _rows8(dgb)
        _row_loop(t, 32, blk1)

        _conv_taps(dcaw, wa_ref, dglu, taps=CONF_K, n_rows=t, base=0, width=c, transposed=True)
        _conv_taps(dcbw, wb_ref, dp, taps=SHORT_K, n_rows=t, base=0, width=c, transposed=True)
        _conv_wgrad(dcaw, gw, awa, taps=CONF_K, n_rows=t, base=HALO, width=c)
        _conv_wgrad(dcbw, pw, awb, taps=SHORT_K, n_rows=t, base=HALO3, width=c)
        dcaw[t:t + HALO, :] = dcaw[0:HALO, :]
        dcbw[t:t + HALO3, :] = dcbw[0:HALO3, :]

        def blk2(rows):
            av = u_ref[rows, 0:c]
            sg = _sigmoid(u_ref[rows, c:2 * c])
            dg = dglu[rows, :]
            d_av = dg * sg
            d_ag = (dg * av) * (sg * (1.0 - sg))
            dpv = dp[rows, :]
            d_gc = dpv * u_ref[rows, 4 * c:5 * c]
            d_vs = dpv * u_ref[rows, 3 * c:4 * c]
            du_ref[rows, 0:c] = d_av.astype(BF16)
            du_ref[rows, c:2 * c] = d_ag.astype(BF16)
            du_ref[rows, 3 * c:4 * c] = d_gc.astype(BF16)
            du_ref[rows, 4 * c:5 * c] = d_vs.astype(BF16)
            dbin_ref[:, 0:c] += _rows8(d_av)
            dbin_ref[:, c:2 * c] += _rows8(d_ag)
            dbin_ref[:, 3 * c:4 * c] += _rows8(d_gc)
            dbin_ref[:, 4 * c:5 * c] += _rows8(d_vs)
        _row_loop(t, 32, blk2)

        @pl.when(i == n_t - 1)
        def _():
            dwa_ref[...] = _fold8(awa, CONF_K)
            dwb_ref[...] = _fold8(awb, SHORT_K)

    rev = lambda i: n_t - 1 - i
    small = lambda r: pl.BlockSpec((r, c), lambda i: (0, 0))
    return pl.pallas_call(
        body, grid=(n_t,),
        in_specs=[pl.BlockSpec((t, d), lambda i: (rev(i), 0)),
                  pl.BlockSpec((t, d_in), lambda i: (rev(i), 0)),
                  pl.BlockSpec((HALO, d_in), lambda i: (jnp.maximum(rev(i) * per - 1, 0), 0)),
                  small(CONF_K), small(1), small(1), small(1), small(SHORT_K),
                  pl.BlockSpec((2 * c, d), lambda i: (0, 0))],
        out_specs=[pl.BlockSpec((t, d_in), lambda i: (rev(i), 0)),
                   small(CONF_K), small(SHORT_K), small(SUB), small(SUB), small(SUB),
                   pl.BlockSpec((SUB, d_in), lambda i: (0, 0))],
        out_shape=[jax.ShapeDtypeStruct((s_len, d_in), BF16),
                   jax.ShapeDtypeStruct((CONF_K, c), F32), jax.ShapeDtypeStruct((SHORT_K, c), F32),
                   jax.ShapeDtypeStruct((SUB, c), F32), jax.ShapeDtypeStruct((SUB, c), F32),
                   jax.ShapeDtypeStruct((SUB, c), F32), jax.ShapeDtypeStruct((SUB, d_in), F32)],
        scratch_shapes=[pltpu.VMEM((HALO + t, c), F32), pltpu.VMEM((HALO3 + t, c), F32),
                        pltpu.VMEM((t, c), F32), pltpu.VMEM((t, c), F32), pltpu.VMEM((t, 2 * c), F32),
                        pltpu.VMEM((t + HALO, c), F32), pltpu.VMEM((t + HALO3, c), F32),
                        pltpu.VMEM((t, c), F32), pltpu.VMEM((t, c), F32),
                        pltpu.VMEM((CONF_K * SUB, c), F32), pltpu.VMEM((SHORT_K * SUB, c), F32)],
        compiler_params=_params("arbitrary"), name=name,
    )(dx1, u, u, wa, ba, lg, lb, wb, w_out)


def _matmul_tn(a, b, *, name, gain=None):
    s_len, k = a.shape
    n = b.shape[1]
    ts = min(512, s_len)
    n_s = s_len // ts
    with_norm = gain is not None

    def body(*refs):
        if with_norm:
            a_ref, g_ref, b_ref, o_ref, acc, h_ref = refs
        else:
            a_ref, b_ref, o_ref, acc = refs
        s = pl.program_id(0)

        @pl.when(s == 0)
        def _():
            acc[...] = jnp.zeros_like(acc)

        if with_norm:
            def blk(rows):
                xv = a_ref[rows, :]
                r = lax.rsqrt(jnp.mean(xv * xv, axis=-1, keepdims=True) + EPS)
                h_ref[rows, :] = ((xv * r) * g_ref[...]).astype(BF16)
            _row_loop(ts, 32, blk)
            av = h_ref[...]
        else:
            av = a_ref[...]
        acc[...] += _dot_tn(av, b_ref[...].astype(BF16))

        @pl.when(s == n_s - 1)
        def _():
            o_ref[...] = acc[...].astype(BF16)

    in_specs = [pl.BlockSpec((ts, k), lambda s: (s, 0))]
    args = [a]
    scratch = [pltpu.VMEM((k, n), F32)]
    if with_norm:
        in_specs.append(pl.BlockSpec((1, k), lambda s: (0, 0)))
        args.append(gain)
        scratch.append(pltpu.VMEM((ts, k), BF16))
    in_specs.append(pl.BlockSpec((ts, n), lambda s: (s, 0)))
    args.append(b)
    return pl.pallas_call(
        body, grid=(n_s,), in_specs=in_specs,
        out_specs=pl.BlockSpec((k, n), lambda s: (0, 0)),
        out_shape=jax.ShapeDtypeStruct((k, n), BF16),
        scratch_shapes=scratch,
        compiler_params=_params("arbitrary"), name=name,
    )(*args)


def _matmul_nt_rmsbwd(dzs, w, x, g, dx_in, *, name):
    s_len, d = x.shape
    n_z = len(dzs)
    nj = dzs[0].shape[1]
    t = _seq_tile(s_len)

    def body(*refs):
        dz_refs = refs[0:n_z]
        w_refs = refs[n_z:2 * n_z]
        x_ref, g_ref, dxi_ref, dx_ref, dg_ref, dh = refs[2 * n_z:]

        @pl.when(pl.program_id(0) == 0)
        def _():
            dg_ref[...] = jnp.zeros_like(dg_ref)

        acc = _dot_nt(dz_refs[0][...], w_refs[0][...])
        for q in range(1, n_z):
            acc = acc + _dot_nt(dz_refs[q][...], w_refs[q][...])
        dh[...] = acc

        def blk(rows):
            xv = x_ref[rows, :]
            r = lax.rsqrt(jnp.mean(xv * xv, axis=-1, keepdims=True) + EPS)
            xn = xv * r
            dhv = dh[rows, :]
            dg_ref[...] += _rows8(dhv * xn)
            dn = dhv * g_ref[...]
            dx_ref[rows, :] = dxi_ref[rows, :] + r * (dn - xn * jnp.mean(dn * xn, axis=-1, keepdims=True))
        _row_loop(t, 32, blk)

    in_specs = [pl.BlockSpec((t, nj), lambda i: (i, 0)) for _ in range(n_z)]
    in_specs += [pl.BlockSpec((d, nj), functools.partial(lambda q, i: (0, q), q)) for q in range(n_z)]
    in_specs += [pl.BlockSpec((t, d), lambda i: (i, 0)), pl.BlockSpec((1, d), lambda i: (0, 0)),
                 pl.BlockSpec((t, d), lambda i: (i, 0))]
    return pl.pallas_call(
        body, grid=(s_len // t,), in_specs=in_specs,
        out_specs=[pl.BlockSpec((t, d), lambda i: (i, 0)), pl.BlockSpec((SUB, d), lambda i: (0, 0))],
        out_shape=[jax.ShapeDtypeStruct((s_len, d), F32), jax.ShapeDtypeStruct((SUB, d), F32)],
        scratch_shapes=[pltpu.VMEM((t, d), F32)],
        compiler_params=_params("arbitrary"), name=name,
    )(*dzs, *([w] * n_z), x, g, dx_in)


def _row(v):
    return v.reshape(1, -1)


def _layer_fwd(x0, p, tag):
    u = _rms_matmul(x0, _row(p["mix_norm_g"]), p["w_in"], _row(p["b_in"]), name=f"in_proj_{tag}")
    ycat, x1 = _mix_fwd(u, x0, p["conv_a_w"], _row(p["conv_a_b"]), _row(p["ln_a_g"]), _row(p["ln_a_b"]),
                        p["conv_b_w"], p["w_out"], name=f"mix_fwd_{tag}")
    uf = _rms_matmul(x1, _row(p["ffn_norm_g"]), p["w_up"], None, name=f"up_proj_{tag}")
    act, x2 = _ffn_fwd(uf, x1, p["conv_f_w"], p["w_down"], name=f"ffn_fwd_{tag}")
    return x2, dict(x0=x0, u=u, ycat=ycat, x1=x1, uf=uf, act=act)


def _layer_bwd(dx2, p, saved, tag):
    dug, duv, dwf_g, dwf_v = _ffn_bwd(dx2, saved["uf"], p["conv_f_w"], p["w_down"], name=f"ffn_bwd_{tag}")
    g_down = _matmul_tn(saved["act"], dx2, name=f"dw_down_{tag}")
    gain2 = _row(p["ffn_norm_g"])
    g_up = jnp.concatenate([_matmul_tn(saved["x1"], dug, gain=gain2, name=f"dw_up_g_{tag}"),
                            _matmul_tn(saved["x1"], duv, gain=gain2, name=f"dw_up_v_{tag}")], axis=1)
    dx1, dg2 = _matmul_nt_rmsbwd([dug, duv], p["w_up"], saved["x1"], gain2, dx2, name=f"dh_ffn_{tag}")
    du, dwa, dwb, dba, dlg, dlb, dbin = _mix_bwd(
        dx1, saved["u"], p["conv_a_w"], _row(p["conv_a_b"]), _row(p["ln_a_g"]), _row(p["ln_a_b"]),
        p["conv_b_w"], p["w_out"], name=f"mix_bwd_{tag}")
    g_out = _matmul_tn(saved["ycat"], dx1, name=f"dw_out_{tag}")
    gain1 = _row(p["mix_norm_g"])
    g_in = _matmul_tn(saved["x0"], du, gain=gain1, name=f"dw_in_{tag}")
    dx0, dg1 = _matmul_nt_rmsbwd([du], p["w_in"], saved["x0"], gain1, dx1, name=f"dh_mix_{tag}")
    big = dict(w_in=g_in, w_out=g_out, w_up=g_up, w_down=g_down)
    conv = dict(conv_a_w=dwa, conv_b_w=dwb, conv_f_w=jnp.concatenate([dwf_g, dwf_v], axis=1))
    rep = dict(mix_norm_g=dg1, b_in=dbin, conv_a_b=dba, ln_a_g=dlg, ln_a_b=dlb, ffn_norm_g=dg2)
    return dx0, big, conv, rep


def _place():
    return lax.axis_index("x"), lax.axis_index("y"), lax.axis_index("c")


def _all_gather(arrs, *, name):
    n_a = len(arrs)

    def body(*refs):
        ins = refs[0:n_a]
        outs = refs[n_a:2 * n_a]
        send_sems, recv_sems, local_sems = refs[2 * n_a:]
        x, y, c = _place()
        sibling = (x, y, 1 - c)
        chips = [(1 - x, y), (x, 1 - y), (1 - x, 1 - y)]

        def slot(a, px, py, pc):
            return outs[a].at[4 * px + 2 * py + pc]

        def copy(a, k, block, to, src=None):
            return pltpu.make_async_remote_copy(
                src_ref=slot(a, *block) if src is None else src, dst_ref=slot(a, *block),
                send_sem=send_sems.at[a, k], recv_sem=recv_sems.at[a, k],
                device_id=to, device_id_type=MESH)

        me = (x, y, c)
        mine = [pltpu.make_async_copy(ins[a], slot(a, *me), local_sems.at[a]) for a in range(n_a)]
        for cp in mine:
            cp.start()
        started = []
        for a in range(n_a):
            first = [copy(a, 0, me, sibling, src=ins[a])]
            first += [copy(a, 1 + j, me, (*chip, c), src=ins[a]) for j, chip in enumerate(chips)]
            for cp in first:
                cp.start()
            started += first
        for a in range(n_a):
            for j, chip in enumerate(chips):
                copy(a, 1 + j, (*chip, c), me).wait_recv()
                passed = copy(a, 4 + j, (*chip, c), sibling)
                passed.start()
                started.append(passed)
        for a in range(n_a):
            copy(a, 0, sibling, me).wait_recv()
            for j, chip in enumerate(chips):
                copy(a, 4 + j, (*chip, 1 - c), me).wait_recv()
        for cp in started:
            cp.wait_send()
        for cp in mine:
            cp.wait()

    return pl.pallas_call(
        body, in_specs=[ANY] * n_a, out_specs=[ANY] * n_a,
        out_shape=[jax.ShapeDtypeStruct((N_DEV, *a.shape), a.dtype) for a in arrs],
        scratch_shapes=[pltpu.SemaphoreType.DMA((n_a, 7)), pltpu.SemaphoreType.DMA((n_a, 7)),
                        pltpu.SemaphoreType.DMA((n_a,))],
        name=name,
    )(*arrs)


def _sibling_exchange(arrs, *, name):
    n_a = len(arrs)

    def body(*refs):
        ins = refs[0:n_a]
        outs = refs[n_a:2 * n_a]
        send_sems, recv_sems = refs[2 * n_a:]
        x, y, c = _place()
        copies = [pltpu.make_async_remote_copy(
            src_ref=ins[a].at[:, 1 - c], dst_ref=outs[a], send_sem=send_sems.at[a], recv_sem=recv_sems.at[a],
            device_id=(x, y, 1 - c), device_id_type=MESH) for a in range(n_a)]
        for cp in copies:
            cp.start()
        for cp in copies:
            cp.wait()

    return pl.pallas_call(
        body, in_specs=[ANY] * n_a, out_specs=[ANY] * n_a,
        out_shape=[jax.ShapeDtypeStruct((N_CHIP, *a.shape[2:]), a.dtype) for a in arrs],
        scratch_shapes=[pltpu.SemaphoreType.DMA((n_a,)), pltpu.SemaphoreType.DMA((n_a,))],
        name=name,
    )(*arrs)


def _pair_sum(mine, theirs, core, *, name):
    n_chip, _, r, c = mine.shape
    tr = _row_tile(r)

    def body(core_ref, a_ref, b_ref, o_ref):
        o_ref[...] = (a_ref[...].astype(F32) + b_ref[...].astype(F32)).astype(o_ref.dtype)

    return pl.pallas_call(
        body,
        grid_spec=pltpu.PrefetchScalarGridSpec(
            num_scalar_prefetch=1, grid=(n_chip, r // tr),
            in_specs=[pl.BlockSpec((None, None, tr, c), lambda q, i, core_ref: (q, core_ref[0], i, 0)),
                      pl.BlockSpec((None, tr, c), lambda q, i, core_ref: (q, i, 0))],
            out_specs=pl.BlockSpec((None, tr, c), lambda q, i, core_ref: (q, i, 0))),
        out_shape=jax.ShapeDtypeStruct((n_chip, r, c), mine.dtype),
        compiler_params=_params("parallel", "parallel"), name=name,
    )(core, mine, theirs)


def _chip_exchange(arrs, *, name):
    n_a = len(arrs)

    def body(*refs):
        ins = refs[0:n_a]
        outs = refs[n_a:2 * n_a]
        send_sems, recv_sems, local_sems = refs[2 * n_a:]
        x, y, c = _place()
        my_chip = 2 * x + y
        chips = [(1 - x, y), (x, 1 - y), (1 - x, 1 - y)]
        mine = [pltpu.make_async_copy(ins[a].at[my_chip], outs[a].at[my_chip], local_sems.at[a]) for a in range(n_a)]
        for cp in mine:
            cp.start()
        copies = []
        for a in range(n_a):
            for j, (px, py) in enumerate(chips):
                copies.append(pltpu.make_async_remote_copy(
                    src_ref=ins[a].at[2 * px + py], dst_ref=outs[a].at[my_chip],
                    send_sem=send_sems.at[a, j], recv_sem=recv_sems.at[a, j],
                    device_id=(px, py, c), device_id_type=MESH))
        for cp in copies:
            cp.start()
        for cp in copies:
            cp.wait()
        for cp in mine:
            cp.wait()

    return pl.pallas_call(
        body, in_specs=[ANY] * n_a, out_specs=[ANY] * n_a,
        out_shape=[jax.ShapeDtypeStruct(a.shape, a.dtype) for a in arrs],
        scratch_shapes=[pltpu.SemaphoreType.DMA((n_a, 3)), pltpu.SemaphoreType.DMA((n_a, 3)),
                        pltpu.SemaphoreType.DMA((n_a,))],
        name=name,
    )(*arrs)


def _row_tile(r):
    for tr in (512, 256, 128, 64, 32, 16, 8):
        if r % tr == 0:
            return tr
    return r


def _adamw_math(g, w, m, v):
    m = ADAM_B1 * m + (1.0 - ADAM_B1) * g
    v = ADAM_B2 * v + (1.0 - ADAM_B2) * (g * g)
    m_hat = m / (1.0 - ADAM_B1 ** ADAM_STEP)
    v_hat = v / (1.0 - ADAM_B2 ** ADAM_STEP)
    delta = -ADAM_LR * (m_hat / (jnp.sqrt(v_hat) + ADAM_EPS) + ADAM_WD * w)
    return delta, m, v


def _adamw_sharded(parts, w, m, v, *, name):
    n_chip, r, c = parts.shape
    tr = _row_tile(r)

    def body(p_ref, w_ref, m_ref, v_ref, g_out, d_out, m_out, v_out):
        g = p_ref[0].astype(F32)
        for q in range(1, n_chip):
            g = g + p_ref[q].astype(F32)
        delta, m_new, v_new = _adamw_math(g, w_ref[...], m_ref[...], v_ref[...])
        g_out[...] = g
        d_out[...] = delta
        m_out[...] = m_new
        v_out[...] = v_new

    blk = pl.BlockSpec((tr, c), lambda i: (i, 0))
    return pl.pallas_call(
        body, grid=(r // tr,),
        in_specs=[pl.BlockSpec((n_chip, tr, c), lambda i: (0, i, 0)), blk, blk, blk],
        out_specs=[blk] * 4, out_shape=[jax.ShapeDtypeStruct((r, c), F32)] * 4,
        compiler_params=_params("parallel"), name=name,
    )(parts, w, m, v)


def _adamw_replicated(parts, w, m, v, n_loss, *, name):
    n_dev, _, cl = parts.shape
    c = cl - n_loss

    def body(p_ref, w_ref, m_ref, v_ref, l_out, g_out, d_out, m_out, v_out):
        acc = p_ref[0]
        for q in range(1, n_dev):
            acc = acc + p_ref[q]
        tot = jnp.sum(acc, axis=0, keepdims=True)
        g = tot[:, 0:c]
        l_out[...] = (0.5 / n_loss) * jnp.sum(tot[:, c:cl], axis=-1, keepdims=True)
        delta, m_new, v_new = _adamw_math(g, w_ref[...], m_ref[...], v_ref[...])
        g_out[...] = g
        d_out[...] = delta
        m_out[...] = m_new
        v_out[...] = v_new

    return pl.pallas_call(
        body, out_shape=[jax.ShapeDtypeStruct((1, 1), F32)] + [jax.ShapeDtypeStruct((1, c), F32)] * 4,
        compiler_params=pltpu.CompilerParams(vmem_limit_bytes=VMEM_LIMIT), name=name,
    )(parts, w, m, v)


BIG = ("w_in", "w_out", "w_up", "w_down")
COL_SHARDED = ("w_in", "w_up")
CONV = ("conv_a_w", "conv_b_w", "conv_f_w")
REPLICATED = ("mix_norm_g", "b_in", "conv_a_b", "ln_a_g", "ln_a_b", "ffn_norm_g")
PACK_TILE = SUB * LANES


def _pack_rows(parts):
    flat = jnp.concatenate([p.reshape(-1) for p in parts])
    pad = (-flat.shape[0]) % PACK_TILE
    return jnp.pad(flat, (0, pad)).reshape(-1, LANES)


def _conv_pack(conv, n_layers):
    return _pack_rows([conv[n][l] for l in range(n_layers) for n in CONV])


def _conv_unpack(rows, like):
    flat = rows.reshape(-1)
    n_layers = like[CONV[0]][0]
    out = {n: [] for n in CONV}
    pos = 0
    for _ in range(n_layers):
        for n in CONV:
            _, k, c = like[n]
            out[n].append(flat[pos:pos + k * c].reshape(k, c))
            pos += k * c
    return {n: jnp.stack(v) for n, v in out.items()}


def _full_from_gathered(name, g):
    n_dev, n_layers, r, c = g.shape
    if name in COL_SHARDED:
        return g.transpose(1, 2, 0, 3).reshape(n_layers, r, n_dev * c)
    return g.transpose(1, 0, 2, 3).reshape(n_layers, n_dev * r, c)


def _slabs_from_full(name, per_layer):
    full = jnp.stack(per_layer)
    n_layers, r, c = full.shape
    if name in COL_SHARDED:
        s = full.reshape(n_layers, r, N_DEV, c // N_DEV).transpose(2, 0, 1, 3)
        return s.reshape(N_DEV, n_layers * r, c // N_DEV)
    s = full.reshape(n_layers, N_DEV, r // N_DEV, c).transpose(1, 0, 2, 3)
    return s.reshape(N_DEV, n_layers * (r // N_DEV), c)


def _conv_slabs(conv_grads, n_layers):
    slabs = []
    for k in range(N_DEV):
        part = {n: conv_grads[n][:, :, k * (conv_grads[n].shape[2] // N_DEV):(k + 1) * (conv_grads[n].shape[2] // N_DEV)]
                for n in CONV}
        slabs.append(_conv_pack(part, n_layers))
    return jnp.stack(slabs)


def kernel(x, mix_norm_g, w_in, b_in, conv_a_w, conv_a_b, ln_a_g, ln_a_b, conv_b_w, w_out, ffn_norm_g, w_up, conv_f_w, w_down, final_norm_g, loss_target, m_mix_norm_g, m_w_in, m_b_in, m_conv_a_w, m_conv_a_b, m_ln_a_g, m_ln_a_b, m_conv_b_w, m_w_out, m_ffn_norm_g, m_w_up, m_conv_f_w, m_w_down, m_final_norm_g, v_mix_norm_g, v_w_in, v_b_in, v_conv_a_w, v_conv_a_b, v_ln_a_g, v_ln_a_b, v_conv_b_w, v_w_out, v_ffn_norm_g, v_w_up, v_conv_f_w, v_w_down, v_final_norm_g):
    w = dict(mix_norm_g=mix_norm_g, w_in=w_in, b_in=b_in, conv_a_w=conv_a_w, conv_a_b=conv_a_b, ln_a_g=ln_a_g,
             ln_a_b=ln_a_b, conv_b_w=conv_b_w, w_out=w_out, ffn_norm_g=ffn_norm_g, w_up=w_up, conv_f_w=conv_f_w,
             w_down=w_down, final_norm_g=final_norm_g)
    m = dict(mix_norm_g=m_mix_norm_g, w_in=m_w_in, b_in=m_b_in, conv_a_w=m_conv_a_w, conv_a_b=m_conv_a_b,
             ln_a_g=m_ln_a_g, ln_a_b=m_ln_a_b, conv_b_w=m_conv_b_w, w_out=m_w_out, ffn_norm_g=m_ffn_norm_g,
             w_up=m_w_up, conv_f_w=m_conv_f_w, w_down=m_w_down, final_norm_g=m_final_norm_g)
    v = dict(mix_norm_g=v_mix_norm_g, w_in=v_w_in, b_in=v_b_in, conv_a_w=v_conv_a_w, conv_a_b=v_conv_a_b,
             ln_a_g=v_ln_a_g, ln_a_b=v_ln_a_b, conv_b_w=v_conv_b_w, w_out=v_w_out, ffn_norm_g=v_ffn_norm_g,
             w_up=v_w_up, conv_f_w=v_conv_f_w, w_down=v_w_down, final_norm_g=v_final_norm_g)
    order = list(w)
    n_layers = w_in.shape[0]
    xs = x[0]
    target = loss_target[0]
    core = lax.axis_index("c").astype(jnp.int32).reshape(1)

    conv_like = {n: w[n].shape for n in CONV}
    gathered = _all_gather([w[n].astype(BF16) for n in BIG] + [_conv_pack(w, n_layers)], name="gather_weights")
    full = {n: _full_from_gathered(n, g) for n, g in zip(BIG, gathered[:len(BIG)])}
    conv_parts = [_conv_unpack(gathered[-1][k], conv_like) for k in range(N_DEV)]
    for n in CONV:
        full[n] = jnp.concatenate([p[n] for p in conv_parts], axis=2)

    def layer_params(l):
        p = {n: full[n][l] for n in BIG + CONV}
        p.update({n: w[n][l] for n in REPLICATED})
        return p

    h = xs
    saved = []
    for l in range(n_layers):
        h, keep = _layer_fwd(h, layer_params(l), str(l))
        saved.append(keep)
    loss_sq, dh, dgf = _loss_bwd(h, _row(final_norm_g), target, name="loss")
    big_g = {n: [None] * n_layers for n in BIG}
    conv_g = {n: [None] * n_layers for n in CONV}
    rep_g = [None] * n_layers
    for l in reversed(range(n_layers)):
        dh, big, conv, rep = _layer_bwd(dh, layer_params(l), saved[l], str(l))
        for n in BIG:
            big_g[n][l] = big[n]
        for n in CONV:
            conv_g[n][l] = conv[n]
        rep_g[l] = rep

    slabs = [_slabs_from_full(n, big_g[n]) for n in BIG]
    slabs.append(_conv_slabs({n: jnp.stack(conv_g[n]) for n in CONV}, n_layers))
    slabs = [s.reshape(N_CHIP, 2, *s.shape[1:]) for s in slabs]
    theirs = _sibling_exchange(slabs, name="reduce_siblings")
    names = list(BIG) + ["conv"]
    pairs = [_pair_sum(a, b, core, name=f"pair_sum_{n}") for n, a, b in zip(names, slabs, theirs)]
    parts = _chip_exchange(pairs, name="reduce_chips")

    out = {k: {} for k in ("grad", "delta", "m", "v")}

    def put(name, res, shape):
        for k, r in zip(("grad", "delta", "m", "v"), res):
            out[k][name] = r.reshape(shape)

    for n, p in zip(BIG, parts[:len(BIG)]):
        two_d = p.shape[1:]
        put(n, _adamw_sharded(p, w[n].reshape(two_d), m[n].reshape(two_d), v[n].reshape(two_d), name=f"adamw_{n}"),
            w[n].shape)
    conv_res = _adamw_sharded(parts[-1], _conv_pack(w, n_layers), _conv_pack(m, n_layers), _conv_pack(v, n_layers),
                              name="adamw_conv")
    for k, r in zip(("grad", "delta", "m", "v"), conv_res):
        out[k].update(_conv_unpack(r, conv_like))

    rep_names = [(n, l) for l in range(n_layers) for n in REPLICATED]
    rep_cols = [rep_g[l][n] for n, l in rep_names] + [dgf, loss_sq]
    rep_all = _all_gather([jnp.concatenate(rep_cols, axis=1)], name="gather_small")[0]
    flat = lambda d: jnp.concatenate([d[n][l] for n, l in rep_names] + [d["final_norm_g"]]).reshape(1, -1)
    rep_res = _adamw_replicated(rep_all, flat(w), flat(m), flat(v), loss_sq.shape[1], name="adamw_small")
    loss = rep_res[0].reshape(())
    for k, r in zip(("grad", "delta", "m", "v"), rep_res[1:]):
        pos = 0
        pieces = {n: [None] * n_layers for n in REPLICATED}
        for n, l in rep_names:
            width = w[n].shape[1]
            pieces[n][l] = r[0, pos:pos + width]
            pos += width
        for n in REPLICATED:
            out[k][n] = jnp.stack(pieces[n])
        out[k]["final_norm_g"] = r[0, pos:pos + final_norm_g.shape[0]]

    grad_x = dh.reshape(x.shape)
    return (loss, grad_x, *[out["grad"][n] for n in order], *[out["delta"][n] for n in order],
            *[out["m"][n] for n in order], *[out["v"][n] for n in order])
```

```python
import functools

import jax
import jax.numpy as jnp
from jax import lax
from jax.experimental import pallas as pl
from jax.experimental.pallas import tpu as pltpu

F32 = jnp.float32
BF16 = jnp.bfloat16

N_DEV = 8
N_CHIP = 4
D_CONF = 512
CONF_K = 31
SHORT_K = 3
EPS = 1e-6
HALO = 32
HALO3 = 8
HALO3_BLK = 16
LANES = 128
SUB = 8
VMEM_LIMIT = 56 * 1024 * 1024

ADAM_LR = 0.001
ADAM_B1 = 0.9
ADAM_B2 = 0.999
ADAM_EPS = 1e-08
ADAM_WD = 0.01
ADAM_STEP = 10

MESH = pl.DeviceIdType.MESH
ANY = pl.BlockSpec(memory_space=pl.ANY)


def _params(*sem):
    return pltpu.CompilerParams(dimension_semantics=sem, vmem_limit_bytes=VMEM_LIMIT)


def _resident(shape, index_map):
    return pl.BlockSpec(shape, index_map, pipeline_mode=pl.Buffered(1))


def _row_loop(n_rows, rb, fn):
    def body(i, carry):
        fn(pl.ds(pl.multiple_of(i * rb, rb), rb))
        return carry
    lax.fori_loop(0, n_rows // rb, body, 0)


def _rows8(v):
    acc = v[0:SUB]
    for k in range(1, v.shape[0] // SUB):
        acc = acc + v[k * SUB:(k + 1) * SUB]
    return acc


def _sigmoid(z):
    return 1.0 / (1.0 + jnp.exp(-z))


def _dot(a, b):
    return jnp.dot(a, b, preferred_element_type=F32)


def _dot_tn(a, b):
    return lax.dot_general(a, b, (((0,), (0,)), ((), ())), preferred_element_type=F32)


def _conv_taps(win, w_ref, out, *, taps, n_rows, base, width, transposed=False, bias_ref=None):
    rb = min(64, n_rows)

    def lane_body(cb, carry):
        lanes = pl.ds(pl.multiple_of(cb * LANES, LANES), LANES)
        for r0 in range(0, n_rows, rb):
            acc = None
            for k in range(taps):
                off = (taps - 1 - k) if transposed else (k - (taps - 1))
                term = w_ref[pl.ds(k, 1), lanes] * win[pl.ds(base + r0 + off, rb), lanes]
                acc = term if acc is None else acc + term
            if bias_ref is not None:
                acc = acc + bias_ref[:, lanes]
            out[pl.ds(r0, rb), lanes] = acc.astype(out.dtype)
        return carry

    lax.fori_loop(0, width // LANES, lane_body, 0)


def _conv_wgrad(dy, win, dw_acc, *, taps, n_rows, base, width):
    rb = min(64, n_rows)

    def lane_body(cb, carry):
        lanes = pl.ds(pl.multiple_of(cb * LANES, LANES), LANES)
        for k in range(taps):
            acc = None
            for r0 in range(0, n_rows, rb):
                prod = dy[pl.ds(r0, rb), lanes] * win[pl.ds(base + r0 + k - (taps - 1), rb), lanes]
                acc = prod if acc is None else acc + prod
            dw_acc[pl.ds(k * SUB, SUB), lanes] += _rows8(acc)
        return carry

    lax.fori_loop(0, width // LANES, lane_body, 0)


def _fold8(acc_ref, taps):
    return jnp.concatenate(
        [jnp.sum(acc_ref[pl.ds(k * SUB, SUB), :], axis=0, keepdims=True) for k in range(taps)], axis=0)


def _seq_tile(s_len):
    return min(256, s_len)


def _mm_tile(s_len):
    return min(512, s_len)


def _ff_chunk(ff):
    best = LANES
    for c in range(LANES, 1408 + 1, LANES):
        if ff % c == 0:
            best = c
    return best


def _col_tile(n):
    for c in (512, 1408, 256, LANES):
        if n % c == 0:
            return c
    return n


def _rms_matmul(x, g, w, b, *, name):
    s_len, d = x.shape
    n = w.shape[1]
    tm = _mm_tile(s_len)
    cn = _col_tile(n)
    has_bias = b is not None

    def body(*refs):
        if has_bias:
            x_ref, g_ref, w_ref, b_ref, o_ref, h_ref = refs
        else:
            x_ref, g_ref, w_ref, o_ref, h_ref = refs

        def blk(rows):
            xv = x_ref[rows, :]
            r = lax.rsqrt(jnp.mean(xv * xv, axis=-1, keepdims=True) + EPS)
            h_ref[rows, :] = ((xv * r) * g_ref[...]).astype(BF16)
        _row_loop(tm, 32, blk)

        def chunk(j, carry):
            cols = pl.ds(pl.multiple_of(j * cn, cn), cn)
            acc = _dot(h_ref[...], w_ref[:, cols])
            if has_bias:
                acc = acc + b_ref[:, cols]
            o_ref[:, cols] = acc.astype(BF16)
            return carry
        lax.fori_loop(0, n // cn, chunk, 0)

    in_specs = [pl.BlockSpec((tm, d), lambda i: (i, 0)), _resident((1, d), lambda i: (0, 0)),
                _resident((d, n), lambda i: (0, 0))]
    args = [x, g, w]
    if has_bias:
        in_specs.append(_resident((1, n), lambda i: (0, 0)))
        args.append(b)
    return pl.pallas_call(
        body, grid=(s_len // tm,), in_specs=in_specs,
        out_specs=[pl.BlockSpec((tm, n), lambda i: (i, 0)), pl.BlockSpec((tm, d), lambda i: (i, 0))],
        out_shape=[jax.ShapeDtypeStruct((s_len, n), BF16), jax.ShapeDtypeStruct((s_len, d), BF16)],
        compiler_params=_params("parallel"), name=name,
    )(*args)


def _mix_windows(u_ref, uh_ref, gw, pw, first, t):
    c = D_CONF
    uh = uh_ref[...].astype(F32)
    gw[0:HALO, :] = jnp.where(first, 0.0, uh[:, 0:c] * _sigmoid(uh[:, c:2 * c]))
    pw[0:HALO3, :] = jnp.where(first, 0.0, uh[HALO - HALO3:HALO, 3 * c:4 * c] * uh[HALO - HALO3:HALO, 4 * c:5 * c])

    def blk(rows):
        dst = pl.ds(pl.multiple_of(rows.start + HALO, SUB), rows.size)
        gw[dst, :] = u_ref[rows, 0:c].astype(F32) * _sigmoid(u_ref[rows, c:2 * c].astype(F32))
        dst3 = pl.ds(pl.multiple_of(rows.start + HALO3, SUB), rows.size)
        pw[dst3, :] = u_ref[rows, 3 * c:4 * c].astype(F32) * u_ref[rows, 4 * c:5 * c].astype(F32)
    _row_loop(t, 32, blk)


def _mix_fwd(u, x0, wa, ba, lg, lb, wb, w_out, *, name):
    s_len, d_in = u.shape
    d = x0.shape[1]
    c = D_CONF
    t = _seq_tile(s_len)
    per = t // HALO

    def body(u_ref, uh_ref, x0_ref, wa_ref, ba_ref, lg_ref, lb_ref, wb_ref, wo_ref, y_ref, x1_ref, ca,
             gw, pw, cb):
        first = pl.program_id(0) == 0
        _mix_windows(u_ref, uh_ref, gw, pw, first, t)
        _conv_taps(gw, wa_ref, ca, taps=CONF_K, n_rows=t, base=HALO, width=c, bias_ref=ba_ref)
        _conv_taps(pw, wb_ref, cb, taps=SHORT_K, n_rows=t, base=HALO3, width=c)

        def blk(rows):
            cv = ca[rows, :]
            mu = jnp.mean(cv, axis=-1, keepdims=True)
            xc = cv - mu
            var = jnp.mean(xc * xc, axis=-1, keepdims=True)
            ln = (xc * lax.rsqrt(var + EPS)) * lg_ref[...] + lb_ref[...]
            y_ref[rows, 0:c] = (ln * _sigmoid(ln)).astype(BF16)
            y_ref[rows, c:2 * c] = (u_ref[rows, 2 * c:3 * c].astype(F32) * cb[rows, :]).astype(BF16)
        _row_loop(t, 32, blk)
        x1_ref[...] = x0_ref[...] + _dot(y_ref[...], wo_ref[...])

    small = lambda r: _resident((r, c), lambda i: (0, 0))
    return pl.pallas_call(
        body, grid=(s_len // t,),
        in_specs=[pl.BlockSpec((t, d_in), lambda i: (i, 0)),
                  pl.BlockSpec((HALO, d_in), lambda i: (jnp.maximum(i * per - 1, 0), 0)),
                  pl.BlockSpec((t, d), lambda i: (i, 0)),
                  small(CONF_K), small(1), small(1), small(1), small(SHORT_K),
                  _resident((2 * c, d), lambda i: (0, 0))],
        out_specs=[pl.BlockSpec((t, 2 * c), lambda i: (i, 0)), pl.BlockSpec((t, d), lambda i: (i, 0)),
                   pl.BlockSpec((t, c), lambda i: (i, 0))],
        out_shape=[jax.ShapeDtypeStruct((s_len, 2 * c), BF16), jax.ShapeDtypeStruct((s_len, d), F32),
                   jax.ShapeDtypeStruct((s_len, c), F32)],
        scratch_shapes=[pltpu.VMEM((HALO + t, c), F32), pltpu.VMEM((HALO3 + t, c), F32), pltpu.VMEM((t, c), F32)],
        compiler_params=_params("arbitrary"), name=name,
    )(u, u, x0, wa, ba, lg, lb, wb, w_out)


def _ffn_windows(ug_ref, ugh_ref, uv_ref, uvh_ref, gwin, vwin, first, t):
    lo = HALO3_BLK - HALO3
    gwin[0:HALO3, :] = jnp.where(first, 0.0, ugh_ref[...].astype(F32)[lo:HALO3_BLK])
    vwin[0:HALO3, :] = jnp.where(first, 0.0, uvh_ref[...].astype(F32)[lo:HALO3_BLK])

    def blk(rows):
        dst = pl.ds(pl.multiple_of(rows.start + HALO3, SUB), rows.size)
        gwin[dst, :] = ug_ref[rows, :].astype(F32)
        vwin[dst, :] = uv_ref[rows, :].astype(F32)
    _row_loop(t, 32, blk)


def _ffn_fwd(uf, x1, wf, w_down, *, name):
    s_len, ff2 = uf.shape
    ff = ff2 // 2
    d = x1.shape[1]
    t = _seq_tile(s_len)
    fc = _ff_chunk(ff)
    nc = ff // fc
    per = t // HALO3_BLK

    def body(ug_ref, ugh_ref, uv_ref, uvh_ref, x1_ref, wfg_ref, wfv_ref, wd_ref, act_ref, x2_ref,
             gwin, vwin, cg, cv):
        first = pl.program_id(0) == 0
        _ffn_windows(ug_ref, ugh_ref, uv_ref, uvh_ref, gwin, vwin, first, t)
        _conv_taps(gwin, wfg_ref, cg, taps=SHORT_K, n_rows=t, base=HALO3, width=fc)
        _conv_taps(vwin, wfv_ref, cv, taps=SHORT_K, n_rows=t, base=HALO3, width=fc)

        def blk(rows):
            gv = cg[rows, :]
            act_ref[rows, :] = ((gv * _sigmoid(gv)) * cv[rows, :]).astype(BF16)
        _row_loop(t, 32, blk)

        @pl.when(pl.program_id(1) == 0)
        def _():
            x2_ref[...] = x1_ref[...]
        x2_ref[...] += _dot(act_ref[...], wd_ref[...])

    halo_map = lambda off: (lambda i, j: (jnp.maximum(i * per - 1, 0), j + off))
    return pl.pallas_call(
        body, grid=(s_len // t, nc),
        in_specs=[pl.BlockSpec((t, fc), lambda i, j: (i, j)), pl.BlockSpec((HALO3_BLK, fc), halo_map(0)),
                  pl.BlockSpec((t, fc), lambda i, j: (i, j + nc)), pl.BlockSpec((HALO3_BLK, fc), halo_map(nc)),
                  pl.BlockSpec((t, d), lambda i, j: (i, 0)),
                  pl.BlockSpec((SHORT_K, fc), lambda i, j: (0, j)),
                  pl.BlockSpec((SHORT_K, fc), lambda i, j: (0, j + nc)),
                  pl.BlockSpec((fc, d), lambda i, j: (j, 0))],
        out_specs=[pl.BlockSpec((t, fc), lambda i, j: (i, j)), pl.BlockSpec((t, d), lambda i, j: (i, 0))],
        out_shape=[jax.ShapeDtypeStruct((s_len, ff), BF16), jax.ShapeDtypeStruct((s_len, d), F32)],
        scratch_shapes=[pltpu.VMEM((HALO3 + t, fc), F32), pltpu.VMEM((HALO3 + t, fc), F32),
                        pltpu.VMEM((t, fc), F32), pltpu.VMEM((t, fc), F32)],
        compiler_params=_params("parallel", "arbitrary"), name=name,
    )(uf, uf, uf, uf, x1, wf, wf, w_down)


def _loss_bwd(x, g, target, *, name):
    s_len, d = x.shape
    t = _seq_tile(s_len)

    def body(x_ref, g_ref, t_ref, l_ref, dx_ref, dxb_ref, dg_ref):
        @pl.when(pl.program_id(0) == 0)
        def _():
            l_ref[...] = jnp.zeros_like(l_ref)
            dg_ref[...] = jnp.zeros_like(dg_ref)

        def blk(rows):
            xv = x_ref[rows, :]
            r = lax.rsqrt(jnp.mean(xv * xv, axis=-1, keepdims=True) + EPS)
            xn = xv * r
            e = xn * g_ref[...] - t_ref[rows, :]
            l_ref[...] += _rows8(e * e)
            dy = e * (1.0 / d)
            dg_ref[...] += _rows8(dy * xn)
            dn = dy * g_ref[...]
            dx = r * (dn - xn * jnp.mean(dn * xn, axis=-1, keepdims=True))
            dx_ref[rows, :] = dx
            dxb_ref[rows, :] = dx.astype(BF16)
        _row_loop(t, 32, blk)

    row = pl.BlockSpec((t, d), lambda i: (i, 0))
    part = pl.BlockSpec((SUB, d), lambda i: (0, 0))
    return pl.pallas_call(
        body, grid=(s_len // t,),
        in_specs=[row, _resident((1, d), lambda i: (0, 0)), row],
        out_specs=[part, row, row, part],
        out_shape=[jax.ShapeDtypeStruct((SUB, d), F32), jax.ShapeDtypeStruct((s_len, d), F32),
                   jax.ShapeDtypeStruct((s_len, d), BF16), jax.ShapeDtypeStruct((SUB, d), F32)],
        compiler_params=_params("arbitrary"), name=name,
    )(x, g, target)


def _ffn_bwd(dx2, uf, wf, w_down_t, *, name):
    s_len, ff2 = uf.shape
    ff = ff2 // 2
    d = dx2.shape[1]
    t = _seq_tile(s_len)
    n_t = s_len // t
    fc = _ff_chunk(ff)
    nc = ff // fc
    per = t // HALO3_BLK

    def body(dx_ref, ug_ref, ugh_ref, uv_ref, uvh_ref, wfg_ref, wfv_ref, wd_ref,
             dug_ref, duv_ref, dwg_ref, dwv_ref, gwin, vwin, cg, cv, dact, dgw, dvw, awg, awv):
        i = pl.program_id(1)
        first = i == n_t - 1
        _ffn_windows(ug_ref, ugh_ref, uv_ref, uvh_ref, gwin, vwin, first, t)
        _conv_taps(gwin, wfg_ref, cg, taps=SHORT_K, n_rows=t, base=HALO3, width=fc)
        _conv_taps(vwin, wfv_ref, cv, taps=SHORT_K, n_rows=t, base=HALO3, width=fc)
        dact[...] = _dot(dx_ref[...], wd_ref[...])

        @pl.when(i == 0)
        def _():
            dgw[t:t + HALO3, :] = jnp.zeros((HALO3, fc), F32)
            dvw[t:t + HALO3, :] = jnp.zeros((HALO3, fc), F32)
            awg[...] = jnp.zeros_like(awg)
            awv[...] = jnp.zeros_like(awv)

        def blk(rows):
            gv = cg[rows, :]
            sg = _sigmoid(gv)
            da = dact[rows, :]
            dgw[rows, :] = (da * cv[rows, :]) * (sg * (1.0 + gv * (1.0 - sg)))
            dvw[rows, :] = da * (gv * sg)
        _row_loop(t, 32, blk)

        _conv_taps(dgw, wfg_ref, dug_ref, taps=SHORT_K, n_rows=t, base=0, width=fc, transposed=True)
        _conv_taps(dvw, wfv_ref, duv_ref, taps=SHORT_K, n_rows=t, base=0, width=fc, transposed=True)
        _conv_wgrad(dgw, gwin, awg, taps=SHORT_K, n_rows=t, base=HALO3, width=fc)
        _conv_wgrad(dvw, vwin, awv, taps=SHORT_K, n_rows=t, base=HALO3, width=fc)
        dgw[t:t + HALO3, :] = dgw[0:HALO3, :]
        dvw[t:t + HALO3, :] = dvw[0:HALO3, :]

        @pl.when(i == n_t - 1)
        def _():
            dwg_ref[...] = _fold8(awg, SHORT_K)
            dwv_ref[...] = _fold8(awv, SHORT_K)

    rev = lambda i: n_t - 1 - i
    halo_map = lambda off: (lambda j, i: (jnp.maximum(rev(i) * per - 1, 0), j + off))
    return pl.pallas_call(
        body, grid=(nc, n_t),
        in_specs=[pl.BlockSpec((t, d), lambda j, i: (rev(i), 0)),
                  pl.BlockSpec((t, fc), lambda j, i: (rev(i), j)), pl.BlockSpec((HALO3_BLK, fc), halo_map(0)),
                  pl.BlockSpec((t, fc), lambda j, i: (rev(i), j + nc)), pl.BlockSpec((HALO3_BLK, fc), halo_map(nc)),
                  pl.BlockSpec((SHORT_K, fc), lambda j, i: (0, j)),
                  pl.BlockSpec((SHORT_K, fc), lambda j, i: (0, j + nc)),
                  pl.BlockSpec((d, fc), lambda j, i: (0, j))],
        out_specs=[pl.BlockSpec((t, fc), lambda j, i: (rev(i), j)), pl.BlockSpec((t, fc), lambda j, i: (rev(i), j)),
                   pl.BlockSpec((SHORT_K, fc), lambda j, i: (0, j)), pl.BlockSpec((SHORT_K, fc), lambda j, i: (0, j))],
        out_shape=[jax.ShapeDtypeStruct((s_len, ff), BF16), jax.ShapeDtypeStruct((s_len, ff), BF16),
                   jax.ShapeDtypeStruct((SHORT_K, ff), F32), jax.ShapeDtypeStruct((SHORT_K, ff), F32)],
        scratch_shapes=[pltpu.VMEM((HALO3 + t, fc), F32), pltpu.VMEM((HALO3 + t, fc), F32),
                        pltpu.VMEM((t, fc), F32), pltpu.VMEM((t, fc), F32), pltpu.VMEM((t, fc), F32),
                        pltpu.VMEM((t + HALO3, fc), F32), pltpu.VMEM((t + HALO3, fc), F32),
                        pltpu.VMEM((SHORT_K * SUB, fc), F32), pltpu.VMEM((SHORT_K * SUB, fc), F32)],
        compiler_params=_params("arbitrary", "arbitrary"), name=name,
    )(dx2, uf, uf, uf, uf, wf, wf, w_down_t)


def _mix_bwd(dx1, u, ca, wa, lg, lb, wb, w_out_t, *, name):
    s_len, d_in = u.shape
    d = dx1.shape[1]
    c = D_CONF
    t = _seq_tile(s_len)
    n_t = s_len // t
    per = t // HALO

    def body(dx_ref, u_ref, uh_ref, ca_ref, wa_ref, lg_ref, lb_ref, wb_ref, wo_ref,
             du_ref, dwa_ref, dwb_ref, dba_ref, dlg_ref, dlb_ref, dbin_ref,
             gw, pw, cb, dyc, dcaw, dcbw, dglu, dp, awa, awb):
        i = pl.program_id(0)
        first = i == n_t - 1
        _mix_windows(u_ref, uh_ref, gw, pw, first, t)
        _conv_taps(pw, wb_ref, cb, taps=SHORT_K, n_rows=t, base=HALO3, width=c)
        dyc[...] = _dot(dx_ref[...], wo_ref[...])

        @pl.when(i == 0)
        def _():
            dcaw[t:t + HALO, :] = jnp.zeros((HALO, c), F32)
            dcbw[t:t + HALO3, :] = jnp.zeros((HALO3, c), F32)
            awa[...] = jnp.zeros_like(awa)
            awb[...] = jnp.zeros_like(awb)
            dba_ref[...] = jnp.zeros_like(dba_ref)
            dlg_ref[...] = jnp.zeros_like(dlg_ref)
            dlb_ref[...] = jnp.zeros_like(dlb_ref)
            dbin_ref[...] = jnp.zeros_like(dbin_ref)

        def blk1(rows):
            cv = ca_ref[rows, :]
            mu = jnp.mean(cv, axis=-1, keepdims=True)
            xc = cv - mu
            rstd = lax.rsqrt(jnp.mean(xc * xc, axis=-1, keepdims=True) + EPS)
            nrm = xc * rstd
            ln = nrm * lg_ref[...] + lb_ref[...]
            sg = _sigmoid(ln)
            dln = dyc[rows, 0:c] * (sg * (1.0 + ln * (1.0 - sg)))
            dlg_ref[...] += _rows8(dln * nrm)
            dlb_ref[...] += _rows8(dln)
            dn = dln * lg_ref[...]
            dca = rstd * (dn - jnp.mean(dn, axis=-1, keepdims=True)
                          - nrm * jnp.mean(dn * nrm, axis=-1, keepdims=True))
            dcaw[rows, :] = dca
            dba_ref[...] += _rows8(dca)
            ds = dyc[rows, c:2 * c]
            dgb = ds * cb[rows, :]
            dcbw[rows, :] = ds * u_ref[rows, 2 * c:3 * c].astype(F32)
            du_ref[rows, 2 * c:3 * c] = dgb.astype(BF16)
            dbin_ref[:, 2 * c:3 * c] += _rows8(dgb)
        _row_loop(t, 32, blk1)

        _conv_taps(dcaw, wa_ref, dglu, taps=CONF_K, n_rows=t, base=0, width=c, transposed=True)
        _conv_taps(dcbw, wb_ref, dp, taps=SHORT_K, n_rows=t, base=0, width=c, transposed=True)
        _conv_wgrad(dcaw, gw, awa, taps=CONF_K, n_rows=t, base=HALO, width=c)
        _conv_wgrad(dcbw, pw, awb, taps=SHORT_K, n_rows=t, base=HALO3, width=c)
        dcaw[t:t + HALO, :] = dcaw[0:HALO, :]
        dcbw[t:t + HALO3, :] = dcbw[0:HALO3, :]

        def blk2(rows):
            av = u_ref[rows, 0:c].astype(F32)
            sg = _sigmoid(u_ref[rows, c:2 * c].astype(F32))
            dg = dglu[rows, :]
            d_av = dg * sg
            d_ag = (dg * av) * (sg * (1.0 - sg))
            dpv = dp[rows, :]
            d_gc = dpv * u_ref[rows, 4 * c:5 * c].astype(F32)
            d_vs = dpv * u_ref[rows, 3 * c:4 * c].astype(F32)
            du_ref[rows, 0:c] = d_av.astype(BF16)
            du_ref[rows, c:2 * c] = d_ag.astype(BF16)
            du_ref[rows, 3 * c:4 * c] = d_gc.astype(BF16)
            du_ref[rows, 4 * c:5 * c] = d_vs.astype(BF16)
            dbin_ref[:, 0:c] += _rows8(d_av)
            dbin_ref[:, c:2 * c] += _rows8(d_ag)
            dbin_ref[:, 3 * c:4 * c] += _rows8(d_gc)
            dbin_ref[:, 4 * c:5 * c] += _rows8(d_vs)
        _row_loop(t, 32, blk2)

        @pl.when(i == n_t - 1)
        def _():
            dwa_ref[...] = _fold8(awa, CONF_K)
            dwb_ref[...] = _fold8(awb, SHORT_K)

    rev = lambda i: n_t - 1 - i
    small_in = lambda r: _resident((r, c), lambda i: (0, 0))
    small = lambda r: pl.BlockSpec((r, c), lambda i: (0, 0))
    return pl.pallas_call(
        body, grid=(n_t,),
        in_specs=[pl.BlockSpec((t, d), lambda i: (rev(i), 0)),
                  pl.BlockSpec((t, d_in), lambda i: (rev(i), 0)),
                  pl.BlockSpec((HALO, d_in), lambda i: (jnp.maximum(rev(i) * per - 1, 0), 0)),
                  pl.BlockSpec((t, c), lambda i: (rev(i), 0)),
                  small_in(CONF_K), small_in(1), small_in(1), small_in(SHORT_K),
                  _resident((d, 2 * c), lambda i: (0, 0))],
        out_specs=[pl.BlockSpec((t, d_in), lambda i: (rev(i), 0)),
                   small(CONF_K), small(SHORT_K), small(SUB), small(SUB), small(SUB),
                   pl.BlockSpec((SUB, d_in), lambda i: (0, 0))],
        out_shape=[jax.ShapeDtypeStruct((s_len, d_in), BF16),
                   jax.ShapeDtypeStruct((CONF_K, c), F32), jax.ShapeDtypeStruct((SHORT_K, c), F32),
                   jax.ShapeDtypeStruct((SUB, c), F32), jax.ShapeDtypeStruct((SUB, c), F32),
                   jax.ShapeDtypeStruct((SUB, c), F32), jax.ShapeDtypeStruct((SUB, d_in), F32)],
        scratch_shapes=[pltpu.VMEM((HALO + t, c), F32), pltpu.VMEM((HALO3 + t, c), F32),
                        pltpu.VMEM((t, c), F32), pltpu.VMEM((t, 2 * c), F32),
                        pltpu.VMEM((t + HALO, c), F32), pltpu.VMEM((t + HALO3, c), F32),
                        pltpu.VMEM((t, c), F32), pltpu.VMEM((t, c), F32),
                        pltpu.VMEM((CONF_K * SUB, c), F32), pltpu.VMEM((SHORT_K * SUB, c), F32)],
        compiler_params=_params("arbitrary"), name=name,
    )(dx1, u, u, ca, wa, lg, lb, wb, w_out_t)


def _matmul_tn(a, b, *, name):
    s_len, k = a.shape
    n = b.shape[1]
    if k <= n:
        tn = _col_tile(n)
        grid = (n // tn,)
        in_specs = [_resident((s_len, k), lambda j: (0, 0)), pl.BlockSpec((s_len, tn), lambda j: (0, j))]
        out_spec = pl.BlockSpec((k, tn), lambda j: (0, j))
    else:
        tk = _col_tile(k)
        grid = (k // tk,)
        in_specs = [pl.BlockSpec((s_len, tk), lambda j: (0, j)), _resident((s_len, n), lambda j: (0, 0))]
        out_spec = pl.BlockSpec((tk, n), lambda j: (j, 0))

    def body(a_ref, b_ref, o_ref):
        o_ref[...] = _dot_tn(a_ref[...], b_ref[...]).astype(BF16)

    return pl.pallas_call(
        body, grid=grid, in_specs=in_specs, out_specs=out_spec,
        out_shape=jax.ShapeDtypeStruct((k, n), BF16),
        compiler_params=_params("parallel"), name=name,
    )(a, b)


def _matmul_rmsbwd(dzs, wt, x, g, dx_in, *, name):
    s_len, d = x.shape
    n_z = len(dzs)
    nj = dzs[0].shape[1]
    t = _mm_tile(s_len)

    def body(*refs):
        dz_refs = refs[0:n_z]
        w_refs = refs[n_z:2 * n_z]
        x_ref, g_ref, dxi_ref, dx_ref, dxb_ref, dg_ref, dh = refs[2 * n_z:]

        @pl.when(pl.program_id(0) == 0)
        def _():
            dg_ref[...] = jnp.zeros_like(dg_ref)

        acc = _dot(dz_refs[0][...], w_refs[0][...])
        for q in range(1, n_z):
            acc = acc + _dot(dz_refs[q][...], w_refs[q][...])
        dh[...] = acc

        def blk(rows):
            xv = x_ref[rows, :]
            r = lax.rsqrt(jnp.mean(xv * xv, axis=-1, keepdims=True) + EPS)
            xn = xv * r
            dhv = dh[rows, :]
            dg_ref[...] += _rows8(dhv * xn)
            dn = dhv * g_ref[...]
            dx = dxi_ref[rows, :] + r * (dn - xn * jnp.mean(dn * xn, axis=-1, keepdims=True))
            dx_ref[rows, :] = dx
            dxb_ref[rows, :] = dx.astype(BF16)
        _row_loop(t, 32, blk)

    row = pl.BlockSpec((t, d), lambda i: (i, 0))
    in_specs = [pl.BlockSpec((t, nj), lambda i: (i, 0)) for _ in range(n_z)]
    in_specs += [_resident((nj, d), functools.partial(lambda q, i: (q, 0), q)) for q in range(n_z)]
    in_specs += [row, _resident((1, d), lambda i: (0, 0)), row]
    return pl.pallas_call(
        body, grid=(s_len // t,), in_specs=in_specs,
        out_specs=[row, row, pl.BlockSpec((SUB, d), lambda i: (0, 0))],
        out_shape=[jax.ShapeDtypeStruct((s_len, d), F32), jax.ShapeDtypeStruct((s_len, d), BF16),
                   jax.ShapeDtypeStruct((SUB, d), F32)],
        scratch_shapes=[pltpu.VMEM((t, d), F32)],
        compiler_params=_params("arbitrary"), name=name,
    )(*dzs, *([wt] * n_z), x, g, dx_in)


def _row(v):
    return v.reshape(1, -1)


def _layer_fwd(x0, p, tag):
    u, h1 = _rms_matmul(x0, _row(p["mix_norm_g"]), p["w_in"], _row(p["b_in"]), name=f"in_proj_{tag}")
    ycat, x1, ca = _mix_fwd(u, x0, p["conv_a_w"], _row(p["conv_a_b"]), _row(p["ln_a_g"]), _row(p["ln_a_b"]),
                            p["conv_b_w"], p["w_out"], name=f"mix_fwd_{tag}")
    uf, h2 = _rms_matmul(x1, _row(p["ffn_norm_g"]), p["w_up"], None, name=f"up_proj_{tag}")
    act, x2 = _ffn_fwd(uf, x1, p["conv_f_w"], p["w_down"], name=f"ffn_fwd_{tag}")
    return x2, dict(x0=x0, h1=h1, u=u, ca=ca, ycat=ycat, x1=x1, h2=h2, uf=uf, act=act)


def _layer_bwd(dx2, dx2_b, p, saved, tag):
    dug, duv, dwf_g, dwf_v = _ffn_bwd(dx2_b, saved["uf"], p["conv_f_w"], p["w_down_t"], name=f"ffn_bwd_{tag}")
    g_down = _matmul_tn(saved["act"], dx2_b, name=f"dw_down_{tag}")
    g_up = [_matmul_tn(saved["h2"], dug, name=f"dw_up_g_{tag}"), _matmul_tn(saved["h2"], duv, name=f"dw_up_v_{tag}")]
    dx1, dx1_b, dg2 = _matmul_rmsbwd([dug, duv], p["w_up_t"], saved["x1"], _row(p["ffn_norm_g"]), dx2,
                                     name=f"dh_ffn_{tag}")
    du, dwa, dwb, dba, dlg, dlb, dbin = _mix_bwd(
        dx1_b, saved["u"], saved["ca"], p["conv_a_w"], _row(p["ln_a_g"]), _row(p["ln_a_b"]),
        p["conv_b_w"], p["w_out_t"], name=f"mix_bwd_{tag}")
    g_out = _matmul_tn(saved["ycat"], dx1_b, name=f"dw_out_{tag}")
    g_in = _matmul_tn(saved["h1"], du, name=f"dw_in_{tag}")
    dx0, dx0_b, dg1 = _matmul_rmsbwd([du], p["w_in_t"], saved["x0"], _row(p["mix_norm_g"]), dx1,
                                     name=f"dh_mix_{tag}")
    big = dict(w_in=g_in, w_out=g_out, w_up=g_up, w_down=g_down)
    conv = dict(conv_a_w=dwa, conv_b_w=dwb, conv_f_w=jnp.concatenate([dwf_g, dwf_v], axis=1))
    rep = dict(mix_norm_g=dg1, b_in=dbin, conv_a_b=dba, ln_a_g=dlg, ln_a_b=dlb, ffn_norm_g=dg2)
    return dx0, dx0_b, big, conv, rep


def _place():
    return lax.axis_index("x"), lax.axis_index("y"), lax.axis_index("c")


def _all_gather(arrs, *, name):
    n_a = len(arrs)

    def body(*refs):
        ins = refs[0:n_a]
        outs = refs[n_a:2 * n_a]
        send_sems, recv_sems, local_sems = refs[2 * n_a:]
        x, y, c = _place()
        sibling = (x, y, 1 - c)
        chips = [(1 - x, y), (x, 1 - y), (1 - x, 1 - y)]

        def slot(a, px, py, pc):
            return outs[a].at[4 * px + 2 * py + pc]

        def copy(a, k, block, to, src=None):
            return pltpu.make_async_remote_copy(
                src_ref=slot(a, *block) if src is None else src, dst_ref=slot(a, *block),
                send_sem=send_sems.at[a, k], recv_sem=recv_sems.at[a, k],
                device_id=to, device_id_type=MESH)

        me = (x, y, c)
        mine = [pltpu.make_async_copy(ins[a], slot(a, *me), local_sems.at[a]) for a in range(n_a)]
        for cp in mine:
            cp.start()
        started = []
        for a in range(n_a):
            first = [copy(a, 0, me, sibling, src=ins[a])]
            first += [copy(a, 1 + j, me, (*chip, c), src=ins[a]) for j, chip in enumerate(chips)]
            for cp in first:
                cp.start()
            started += first
        for a in range(n_a):
            for j, chip in enumerate(chips):
                copy(a, 1 + j, (*chip, c), me).wait_recv()
                passed = copy(a, 4 + j, (*chip, c), sibling)
                passed.start()
                started.append(passed)
        for a in range(n_a):
            copy(a, 0, sibling, me).wait_recv()
            for j, chip in enumerate(chips):
                copy(a, 4 + j, (*chip, 1 - c), me).wait_recv()
        for cp in started:
            cp.wait_send()
        for cp in mine:
            cp.wait()

    return pl.pallas_call(
        body, in_specs=[ANY] * n_a, out_specs=[ANY] * n_a,
        out_shape=[jax.ShapeDtypeStruct((N_DEV, *a.shape), a.dtype) for a in arrs],
        scratch_shapes=[pltpu.SemaphoreType.DMA((n_a, 7)), pltpu.SemaphoreType.DMA((n_a, 7)),
                        pltpu.SemaphoreType.DMA((n_a,))],
        name=name,
    )(*arrs)


def _sibling_exchange(arrs, *, name):
    n_a = len(arrs)

    def body(*refs):
        ins = refs[0:n_a]
        outs = refs[n_a:2 * n_a]
        send_sems, recv_sems = refs[2 * n_a:]
        x, y, c = _place()
        copies = [pltpu.make_async_remote_copy(
            src_ref=ins[a].at[:, 1 - c], dst_ref=outs[a], send_sem=send_sems.at[a], recv_sem=recv_sems.at[a],
            device_id=(x, y, 1 - c), device_id_type=MESH) for a in range(n_a)]
        for cp in copies:
            cp.start()
        for cp in copies:
            cp.wait()

    return pl.pallas_call(
        body, in_specs=[ANY] * n_a, out_specs=[ANY] * n_a,
        out_shape=[jax.ShapeDtypeStruct((N_CHIP, *a.shape[2:]), a.dtype) for a in arrs],
        scratch_shapes=[pltpu.SemaphoreType.DMA((n_a,)), pltpu.SemaphoreType.DMA((n_a,))],
        name=name,
    )(*arrs)


def _row_tile(r):
    for tr in (512, 352, 256, 128, 64, 32, 16, 8):
        if r % tr == 0:
            return tr
    return r


def _pair_sum(mine, theirs, core, *, name):
    n_chip, _, r, c = mine.shape
    tr = _row_tile(r)

    def body(core_ref, a_ref, b_ref, o_ref):
        o_ref[...] = (a_ref[...].astype(F32) + b_ref[...].astype(F32)).astype(o_ref.dtype)

    return pl.pallas_call(
        body,
        grid_spec=pltpu.PrefetchScalarGridSpec(
            num_scalar_prefetch=1, grid=(n_chip, r // tr),
            in_specs=[pl.BlockSpec((None, None, tr, c), lambda q, i, core_ref: (q, core_ref[0], i, 0)),
                      pl.BlockSpec((None, tr, c), lambda q, i, core_ref: (q, i, 0))],
            out_specs=pl.BlockSpec((None, tr, c), lambda q, i, core_ref: (q, i, 0))),
        out_shape=jax.ShapeDtypeStruct((n_chip, r, c), mine.dtype),
        compiler_params=_params("parallel", "parallel"), name=name,
    )(core, mine, theirs)


def _chip_exchange(arrs, *, name):
    n_a = len(arrs)

    def body(*refs):
        ins = refs[0:n_a]
        outs = refs[n_a:2 * n_a]
        send_sems, recv_sems, local_sems = refs[2 * n_a:]
        x, y, c = _place()
        my_chip = 2 * x + y
        chips = [(1 - x, y), (x, 1 - y), (1 - x, 1 - y)]
        mine = [pltpu.make_async_copy(ins[a].at[my_chip], outs[a].at[my_chip], local_sems.at[a]) for a in range(n_a)]
        for cp in mine:
            cp.start()
        copies = []
        for a in range(n_a):
            for j, (px, py) in enumerate(chips):
                copies.append(pltpu.make_async_remote_copy(
                    src_ref=ins[a].at[2 * px + py], dst_ref=outs[a].at[my_chip],
                    send_sem=send_sems.at[a, j], recv_sem=recv_sems.at[a, j],
                    device_id=(px, py, c), device_id_type=MESH))
        for cp in copies:
            cp.start()
        for cp in copies:
            cp.wait()
        for cp in mine:
            cp.wait()

    return pl.pallas_call(
        body, in_specs=[ANY] * n_a, out_specs=[ANY] * n_a,
        out_shape=[jax.ShapeDtypeStruct(a.shape, a.dtype) for a in arrs],
        scratch_shapes=[pltpu.SemaphoreType.DMA((n_a, 3)), pltpu.SemaphoreType.DMA((n_a, 3)),
                        pltpu.SemaphoreType.DMA((n_a,))],
        name=name,
    )(*arrs)


def _adamw_math(g, w, m, v):
    m = ADAM_B1 * m + (1.0 - ADAM_B1) * g
    v = ADAM_B2 * v + (1.0 - ADAM_B2) * (g * g)
    m_hat = m / (1.0 - ADAM_B1 ** ADAM_STEP)
    v_hat = v / (1.0 - ADAM_B2 ** ADAM_STEP)
    delta = -ADAM_LR * (m_hat / (jnp.sqrt(v_hat) + ADAM_EPS) + ADAM_WD * w)
    return delta, m, v


def _adamw_sharded(parts, w, m, v, *, name):
    n_layers, r, c = w.shape
    n_chip = parts[0].shape[0]
    tr = _row_tile(r)
    n_i = r // tr

    def body(*refs):
        p_refs = refs[0:n_layers]
        w_ref, m_ref, v_ref, g_out, d_out, m_out, v_out = refs[n_layers:]
        layer = pl.program_id(0)
        for l in range(n_layers):
            @pl.when(layer == l)
            def _(l=l):
                g = p_refs[l][0].astype(F32)
                for q in range(1, n_chip):
                    g = g + p_refs[l][q].astype(F32)
                delta, m_new, v_new = _adamw_math(g, w_ref[...], m_ref[...], v_ref[...])
                g_out[...] = g
                d_out[...] = delta
                m_out[...] = m_new
                v_out[...] = v_new

    def part_map(l):
        return lambda layer, i: (0, jnp.where(layer == l, i, jnp.where(layer < l, 0, n_i - 1)), 0)

    blk = pl.BlockSpec((None, tr, c), lambda layer, i: (layer, i, 0))
    return pl.pallas_call(
        body, grid=(n_layers, n_i),
        in_specs=[pl.BlockSpec((n_chip, tr, c), part_map(l)) for l in range(n_layers)] + [blk, blk, blk],
        out_specs=[blk] * 4, out_shape=[jax.ShapeDtypeStruct((n_layers, r, c), F32)] * 4,
        compiler_params=_params("arbitrary", "arbitrary"), name=name,
    )(*parts, w, m, v)


def _adamw_replicated(parts, w, m, v, n_loss, *, name):
    n_dev, _, cl = parts.shape
    c = cl - n_loss

    def body(p_ref, w_ref, m_ref, v_ref, l_out, g_out, d_out, m_out, v_out):
        acc = p_ref[0]
        for q in range(1, n_dev):
            acc = acc + p_ref[q]
        tot = jnp.sum(acc, axis=0, keepdims=True)
        g = tot[:, 0:c]
        l_out[...] = (0.5 / n_loss) * jnp.sum(tot[:, c:cl], axis=-1, keepdims=True)
        delta, m_new, v_new = _adamw_math(g, w_ref[...], m_ref[...], v_ref[...])
        g_out[...] = g
        d_out[...] = delta
        m_out[...] = m_new
        v_out[...] = v_new

    return pl.pallas_call(
        body, out_shape=[jax.ShapeDtypeStruct((1, 1), F32)] + [jax.ShapeDtypeStruct((1, c), F32)] * 4,
        compiler_params=pltpu.CompilerParams(vmem_limit_bytes=VMEM_LIMIT), name=name,
    )(parts, w, m, v)


BIG = ("w_in", "w_out", "w_up", "w_down")
COL_SHARDED = ("w_in", "w_up")
CONV = ("conv_a_w", "conv_b_w", "conv_f_w")
REPLICATED = ("mix_norm_g", "b_in", "conv_a_b", "ln_a_g", "ln_a_b", "ffn_norm_g")
PACK_TILE = SUB * LANES


def _pack_rows(parts):
    flat = jnp.concatenate([p.reshape(-1) for p in parts])
    pad = (-flat.shape[0]) % PACK_TILE
    return jnp.pad(flat, (0, pad)).reshape(-1, LANES)


def _conv_pack(conv, n_layers):
    return _pack_rows([conv[n][l] for l in range(n_layers) for n in CONV])


def _conv_unpack(rows, like):
    flat = rows.reshape(-1)
    n_layers = like[CONV[0]][0]
    out = {n: [] for n in CONV}
    pos = 0
    for _ in range(n_layers):
        for n in CONV:
            _, k, c = like[n]
            out[n].append(flat[pos:pos + k * c].reshape(k, c))
            pos += k * c
    return {n: jnp.stack(v) for n, v in out.items()}


def _weights_from_gathered(name, g):
    n_dev, r, c = g.shape
    if name in COL_SHARDED:
        return g.transpose(1, 0, 2).reshape(r, n_dev * c), g.transpose(0, 2, 1).reshape(n_dev * c, r)
    full = g.reshape(n_dev * r, c)
    return full, full.T


def _slabs_from_full(name, grad):
    if name == "w_up":
        halves = [h.reshape(h.shape[0], N_DEV // 2, -1).transpose(1, 0, 2) for h in grad]
        return jnp.concatenate(halves, axis=0)
    r, c = grad.shape
    if name in COL_SHARDED:
        return grad.reshape(r, N_DEV, c // N_DEV).transpose(1, 0, 2)
    return grad.reshape(N_DEV, r // N_DEV, c)


def _conv_slabs(conv_grads, n_layers):
    slabs = []
    for k in range(N_DEV):
        part = {n: conv_grads[n][:, :, k * (conv_grads[n].shape[2] // N_DEV):(k + 1) * (conv_grads[n].shape[2] // N_DEV)]
                for n in CONV}
        slabs.append(_conv_pack(part, n_layers))
    return jnp.stack(slabs)


def _reduce_scatter(slabs, core, tag):
    slabs = [s.reshape(N_CHIP, 2, *s.shape[1:]) for s in slabs]
    theirs = _sibling_exchange(slabs, name=f"reduce_siblings_{tag}")
    pairs = [_pair_sum(a, b, core, name=f"pair_sum_{tag}_{q}") for q, (a, b) in enumerate(zip(slabs, theirs))]
    return _chip_exchange(pairs, name=f"reduce_chips_{tag}")


def kernel(x, mix_norm_g, w_in, b_in, conv_a_w, conv_a_b, ln_a_g, ln_a_b, conv_b_w, w_out, ffn_norm_g, w_up, conv_f_w, w_down, final_norm_g, loss_target, m_mix_norm_g, m_w_in, m_b_in, m_conv_a_w, m_conv_a_b, m_ln_a_g, m_ln_a_b, m_conv_b_w, m_w_out, m_ffn_norm_g, m_w_up, m_conv_f_w, m_w_down, m_final_norm_g, v_mix_norm_g, v_w_in, v_b_in, v_conv_a_w, v_conv_a_b, v_ln_a_g, v_ln_a_b, v_conv_b_w, v_w_out, v_ffn_norm_g, v_w_up, v_conv_f_w, v_w_down, v_final_norm_g):
    w = dict(mix_norm_g=mix_norm_g, w_in=w_in, b_in=b_in, conv_a_w=conv_a_w, conv_a_b=conv_a_b, ln_a_g=ln_a_g,
             ln_a_b=ln_a_b, conv_b_w=conv_b_w, w_out=w_out, ffn_norm_g=ffn_norm_g, w_up=w_up, conv_f_w=conv_f_w,
             w_down=w_down, final_norm_g=final_norm_g)
    m = dict(mix_norm_g=m_mix_norm_g, w_in=m_w_in, b_in=m_b_in, conv_a_w=m_conv_a_w, conv_a_b=m_conv_a_b,
             ln_a_g=m_ln_a_g, ln_a_b=m_ln_a_b, conv_b_w=m_conv_b_w, w_out=m_w_out, ffn_norm_g=m_ffn_norm_g,
             w_up=m_w_up, conv_f_w=m_conv_f_w, w_down=m_w_down, final_norm_g=m_final_norm_g)
    v = dict(mix_norm_g=v_mix_norm_g, w_in=v_w_in, b_in=v_b_in, conv_a_w=v_conv_a_w, conv_a_b=v_conv_a_b,
             ln_a_g=v_ln_a_g, ln_a_b=v_ln_a_b, conv_b_w=v_conv_b_w, w_out=v_w_out, ffn_norm_g=v_ffn_norm_g,
             w_up=v_w_up, conv_f_w=v_conv_f_w, w_down=v_w_down, final_norm_g=v_final_norm_g)
    order = list(w)
    n_layers = w_in.shape[0]
    xs = x[0]
    target = loss_target[0]
    core = lax.axis_index("c").astype(jnp.int32).reshape(1)

    conv_like = {n: w[n].shape for n in CONV}
    shards = [w[n][l].astype(BF16) for l in range(n_layers) for n in BIG]
    gathered = _all_gather(shards + [_conv_pack(w, n_layers)], name="gather_weights")
    conv_parts = [_conv_unpack(gathered[-1][k], conv_like) for k in range(N_DEV)]
    conv_full = {n: jnp.concatenate([p[n] for p in conv_parts], axis=2) for n in CONV}

    def layer_params(l):
        p = {n: conv_full[n][l] for n in CONV}
        for q, n in enumerate(BIG):
            p[n], p[n + "_t"] = _weights_from_gathered(n, gathered[l * len(BIG) + q])
        p.update({n: w[n][l] for n in REPLICATED})
        return p

    params = [layer_params(l) for l in range(n_layers)]

    h = xs
    saved = []
    for l in range(n_layers):
        h, keep = _layer_fwd(h, params[l], str(l))
        saved.append(keep)
    loss_sq, dh, dh_b, dgf = _loss_bwd(h, _row(final_norm_g), target, name="loss")
    big_g = [None] * n_layers
    conv_g = {n: [None] * n_layers for n in CONV}
    rep_g = [None] * n_layers
    for l in reversed(range(n_layers)):
        dh, dh_b, big_g[l], conv, rep_g[l] = _layer_bwd(dh, dh_b, params[l], saved[l], str(l))
        for n in CONV:
            conv_g[n][l] = conv[n]

    slabs = [_slabs_from_full(n, big_g[l][n]) for l in range(n_layers) for n in BIG]
    slabs.append(_conv_slabs({n: jnp.stack(conv_g[n]) for n in CONV}, n_layers))
    parts = _reduce_scatter(slabs, core, "all")

    out = {k: {} for k in ("grad", "delta", "m", "v")}
    kinds = ("grad", "delta", "m", "v")
    for q, n in enumerate(BIG):
        layer_parts = [parts[l * len(BIG) + q] for l in range(n_layers)]
        for k, r in zip(kinds, _adamw_sharded(layer_parts, w[n], m[n], v[n], name=f"adamw_{n}")):
            out[k][n] = r
    pack3 = lambda d: _conv_pack(d, n_layers)[None]
    for k, r in zip(kinds, _adamw_sharded([parts[-1]], pack3(w), pack3(m), pack3(v), name="adamw_conv")):
        out[k].update(_conv_unpack(r[0], conv_like))

    rep_names = [(n, l) for l in range(n_layers) for n in REPLICATED]
    rep_cols = [rep_g[l][n] for n, l in rep_names] + [dgf, loss_sq]
    rep_all = _all_gather([jnp.concatenate(rep_cols, axis=1)], name="gather_small")[0]
    flat = lambda d: jnp.concatenate([d[n][l] for n, l in rep_names] + [d["final_norm_g"]]).reshape(1, -1)
    rep_res = _adamw_replicated(rep_all, flat(w), flat(m), flat(v), loss_sq.shape[1], name="adamw_small")
    loss = rep_res[0].reshape(())
    for k, r in zip(kinds, rep_res[1:]):
        pos = 0
        pieces = {n: [None] * n_layers for n in REPLICATED}
        for n, l in rep_names:
            width = w[n].shape[1]
            pieces[n][l] = r[0, pos:pos + width]
            pos += width
        for n in REPLICATED:
            out[k][n] = jnp.stack(pieces[n])
        out[k]["final_norm_g"] = r[0, pos:pos + final_norm_g.shape[0]]

    grad_x = dh.reshape(x.shape)
    return (loss, grad_x, *[out["grad"][n] for n in order], *[out["delta"][n] for n in order],
            *[out["m"][n] for n in order], *[out["v"][n] for n in order])
```

```python
import functools

import jax
import jax.numpy as jnp
from jax import lax
from jax.experimental import pallas as pl
from jax.experimental.pallas import tpu as pltpu

F32 = jnp.float32
BF16 = jnp.bfloat16

N_DEV = 8
N_CHIP = 4
D_CONF = 512
CONF_K = 31
SHORT_K = 3
EPS = 1e-6
HALO = 32
HALO3 = 8
HALO3_BLK = 16
LANES = 128
SUB = 8
VMEM_LIMIT = 56 * 1024 * 1024

ADAM_LR = 0.001
ADAM_B1 = 0.9
ADAM_B2 = 0.999
ADAM_EPS = 1e-08
ADAM_WD = 0.01
ADAM_STEP = 10

MESH = pl.DeviceIdType.MESH
ANY = pl.BlockSpec(memory_space=pl.ANY)


def _params(*sem):
    return pltpu.CompilerParams(dimension_semantics=sem, vmem_limit_bytes=VMEM_LIMIT)


def _resident(shape, index_map):
    return pl.BlockSpec(shape, index_map, pipeline_mode=pl.Buffered(1))


def _row_loop(n_rows, rb, fn, unroll=1):
    rb = min(rb, n_rows)

    def body(i, carry):
        fn(pl.ds(pl.multiple_of(i * rb, rb), rb))
        return carry
    lax.fori_loop(0, n_rows // rb, body, 0, unroll=unroll)


def _rows8(v):
    acc = v[0:SUB]
    for k in range(1, v.shape[0] // SUB):
        acc = acc + v[k * SUB:(k + 1) * SUB]
    return acc


def _sigmoid(z):
    return 0.5 * jnp.tanh(0.5 * z) + 0.5


def _dot(a, b):
    return jnp.dot(a, b, preferred_element_type=F32)


def _dot_tn(a, b):
    return lax.dot_general(a, b, (((0,), (0,)), ((), ())), preferred_element_type=F32)


def _conv_taps(win, w_ref, out, *, taps, n_rows, base, width, transposed=False, bias_ref=None):
    rb = min(64, n_rows)

    def lane_body(cb, carry):
        lanes = pl.ds(pl.multiple_of(cb * LANES, LANES), LANES)
        for r0 in range(0, n_rows, rb):
            acc = None
            for k in range(taps):
                off = (taps - 1 - k) if transposed else (k - (taps - 1))
                term = w_ref[pl.ds(k, 1), lanes] * win[pl.ds(base + r0 + off, rb), lanes]
                acc = term if acc is None else acc + term
            if bias_ref is not None:
                acc = acc + bias_ref[:, lanes]
            out[pl.ds(r0, rb), lanes] = acc.astype(out.dtype)
        return carry

    lax.fori_loop(0, width // LANES, lane_body, 0)


def _conv_wgrad(dy, win, dw_acc, *, taps, n_rows, base, width):
    rb = min(64, n_rows)

    def lane_body(cb, carry):
        lanes = pl.ds(pl.multiple_of(cb * LANES, LANES), LANES)
        for k in range(taps):
            acc = None
            for r0 in range(0, n_rows, rb):
                prod = dy[pl.ds(r0, rb), lanes] * win[pl.ds(base + r0 + k - (taps - 1), rb), lanes]
                acc = prod if acc is None else acc + prod
            dw_acc[pl.ds(k * SUB, SUB), lanes] += _rows8(acc)
        return carry

    lax.fori_loop(0, width // LANES, lane_body, 0)


def _fold8(acc_ref, taps):
    return jnp.concatenate(
        [jnp.sum(acc_ref[pl.ds(k * SUB, SUB), :], axis=0, keepdims=True) for k in range(taps)], axis=0)


def _seq_tile(s_len):
    return min(256, s_len)


def _mm_tile(s_len):
    return min(512, s_len)


def _ff_chunk(ff):
    best = LANES
    for c in range(LANES, 1408 + 1, LANES):
        if ff % c == 0:
            best = c
    return best


def _col_tile(n):
    for c in (512, 1408, 256, LANES):
        if n % c == 0:
            return c
    return n


def _rms_matmul(x, g, w, b, *, name):
    s_len, d = x.shape
    n = w.shape[1]
    tm = _mm_tile(s_len)
    cn = _col_tile(n)
    has_bias = b is not None

    def body(*refs):
        if has_bias:
            x_ref, g_ref, w_ref, b_ref, o_ref, h_ref = refs
        else:
            x_ref, g_ref, w_ref, o_ref, h_ref = refs

        def blk(rows):
            xv = x_ref[rows, :]
            r = lax.rsqrt(jnp.mean(xv * xv, axis=-1, keepdims=True) + EPS)
            h_ref[rows, :] = ((xv * r) * g_ref[...]).astype(BF16)
        _row_loop(tm, 128, blk)

        def chunk(j, carry):
            cols = pl.ds(pl.multiple_of(j * cn, cn), cn)
            acc = _dot(h_ref[...], w_ref[:, cols])
            if has_bias:
                acc = acc + b_ref[:, cols]
            o_ref[:, cols] = acc.astype(BF16)
            return carry
        lax.fori_loop(0, n // cn, chunk, 0)

    in_specs = [pl.BlockSpec((tm, d), lambda i: (i, 0)), _resident((1, d), lambda i: (0, 0)),
                _resident((d, n), lambda i: (0, 0))]
    args = [x, g, w]
    if has_bias:
        in_specs.append(_resident((1, n), lambda i: (0, 0)))
        args.append(b)
    return pl.pallas_call(
        body, grid=(s_len // tm,), in_specs=in_specs,
        out_specs=[pl.BlockSpec((tm, n), lambda i: (i, 0)), pl.BlockSpec((tm, d), lambda i: (i, 0))],
        out_shape=[jax.ShapeDtypeStruct((s_len, n), BF16), jax.ShapeDtypeStruct((s_len, d), BF16)],
        compiler_params=_params("parallel"), name=name,
    )(*args)


def _mix_windows(u_ref, uh_ref, gw, pw, first, t):
    c = D_CONF
    uh = uh_ref[...].astype(F32)
    gw[0:HALO, :] = jnp.where(first, 0.0, uh[:, 0:c] * _sigmoid(uh[:, c:2 * c]))
    pw[0:HALO3, :] = jnp.where(first, 0.0, uh[HALO - HALO3:HALO, 3 * c:4 * c] * uh[HALO - HALO3:HALO, 4 * c:5 * c])

    def blk(rows):
        dst = pl.ds(pl.multiple_of(rows.start + HALO, SUB), rows.size)
        gw[dst, :] = u_ref[rows, 0:c].astype(F32) * _sigmoid(u_ref[rows, c:2 * c].astype(F32))
        dst3 = pl.ds(pl.multiple_of(rows.start + HALO3, SUB), rows.size)
        pw[dst3, :] = u_ref[rows, 3 * c:4 * c].astype(F32) * u_ref[rows, 4 * c:5 * c].astype(F32)
    _row_loop(t, 64, blk)


def _mix_fwd(u, x0, wa, ba, lg, lb, wb, w_out, *, name):
    s_len, d_in = u.shape
    d = x0.shape[1]
    c = D_CONF
    t = _seq_tile(s_len)
    per = t // HALO

    def body(u_ref, uh_ref, x0_ref, wa_ref, ba_ref, lg_ref, lb_ref, wb_ref, wo_ref, y_ref, x1_ref, ca,
             gw, pw, cb):
        first = pl.program_id(0) == 0
        _mix_windows(u_ref, uh_ref, gw, pw, first, t)
        _conv_taps(gw, wa_ref, ca, taps=CONF_K, n_rows=t, base=HALO, width=c, bias_ref=ba_ref)
        _conv_taps(pw, wb_ref, cb, taps=SHORT_K, n_rows=t, base=HALO3, width=c)

        def blk(rows):
            cv = ca[rows, :]
            mu = jnp.mean(cv, axis=-1, keepdims=True)
            xc = cv - mu
            var = jnp.mean(xc * xc, axis=-1, keepdims=True)
            ln = (xc * lax.rsqrt(var + EPS)) * lg_ref[...] + lb_ref[...]
            y_ref[rows, 0:c] = (ln * _sigmoid(ln)).astype(BF16)
            y_ref[rows, c:2 * c] = (u_ref[rows, 2 * c:3 * c].astype(F32) * cb[rows, :]).astype(BF16)
        _row_loop(t, 64, blk)
        x1_ref[...] = x0_ref[...] + _dot(y_ref[...], wo_ref[...])

    small = lambda r: _resident((r, c), lambda i: (0, 0))
    return pl.pallas_call(
        body, grid=(s_len // t,),
        in_specs=[pl.BlockSpec((t, d_in), lambda i: (i, 0)),
                  pl.BlockSpec((HALO, d_in), lambda i: (jnp.maximum(i * per - 1, 0), 0)),
                  pl.BlockSpec((t, d), lambda i: (i, 0)),
                  small(CONF_K), small(1), small(1), small(1), small(SHORT_K),
                  _resident((2 * c, d), lambda i: (0, 0))],
        out_specs=[pl.BlockSpec((t, 2 * c), lambda i: (i, 0)), pl.BlockSpec((t, d), lambda i: (i, 0)),
                   pl.BlockSpec((t, c), lambda i: (i, 0))],
        out_shape=[jax.ShapeDtypeStruct((s_len, 2 * c), BF16), jax.ShapeDtypeStruct((s_len, d), F32),
                   jax.ShapeDtypeStruct((s_len, c), F32)],
        scratch_shapes=[pltpu.VMEM((HALO + t, c), F32), pltpu.VMEM((HALO3 + t, c), F32), pltpu.VMEM((t, c), F32)],
        compiler_params=_params("arbitrary"), name=name,
    )(u, u, x0, wa, ba, lg, lb, wb, w_out)


def _ffn_windows(ug_ref, ugh_ref, uv_ref, uvh_ref, gwin, vwin, first, t):
    lo = HALO3_BLK - HALO3
    gwin[0:HALO3, :] = jnp.where(first, 0.0, ugh_ref[...].astype(F32)[lo:HALO3_BLK])
    vwin[0:HALO3, :] = jnp.where(first, 0.0, uvh_ref[...].astype(F32)[lo:HALO3_BLK])

    def blk(rows):
        dst = pl.ds(pl.multiple_of(rows.start + HALO3, SUB), rows.size)
        gwin[dst, :] = ug_ref[rows, :].astype(F32)
        vwin[dst, :] = uv_ref[rows, :].astype(F32)
    _row_loop(t, 64, blk)


def _ffn_fwd(uf, x1, wf, w_down, *, name):
    s_len, ff2 = uf.shape
    ff = ff2 // 2
    d = x1.shape[1]
    t = _seq_tile(s_len)
    fc = _ff_chunk(ff)
    nc = ff // fc
    per = t // HALO3_BLK

    def body(ug_ref, ugh_ref, uv_ref, uvh_ref, x1_ref, wfg_ref, wfv_ref, wd_ref, act_ref, x2_ref,
             gwin, vwin, cg, cv):
        first = pl.program_id(0) == 0
        _ffn_windows(ug_ref, ugh_ref, uv_ref, uvh_ref, gwin, vwin, first, t)
        _conv_taps(gwin, wfg_ref, cg, taps=SHORT_K, n_rows=t, base=HALO3, width=fc)
        _conv_taps(vwin, wfv_ref, cv, taps=SHORT_K, n_rows=t, base=HALO3, width=fc)

        def blk(rows):
            gv = cg[rows, :]
            act_ref[rows, :] = ((gv * _sigmoid(gv)) * cv[rows, :]).astype(BF16)
        _row_loop(t, 32, blk, unroll=2)

        @pl.when(pl.program_id(1) == 0)
        def _():
            x2_ref[...] = x1_ref[...]
        x2_ref[...] += _dot(act_ref[...], wd_ref[...])

    halo_map = lambda off: (lambda i, j: (jnp.maximum(i * per - 1, 0), j + off))
    return pl.pallas_call(
        body, grid=(s_len // t, nc),
        in_specs=[pl.BlockSpec((t, fc), lambda i, j: (i, j)), pl.BlockSpec((HALO3_BLK, fc), halo_map(0)),
                  pl.BlockSpec((t, fc), lambda i, j: (i, j + nc)), pl.BlockSpec((HALO3_BLK, fc), halo_map(nc)),
                  pl.BlockSpec((t, d), lambda i, j: (i, 0)),
                  pl.BlockSpec((SHORT_K, fc), lambda i, j: (0, j)),
                  pl.BlockSpec((SHORT_K, fc), lambda i, j: (0, j + nc)),
                  pl.BlockSpec((fc, d), lambda i, j: (j, 0))],
        out_specs=[pl.BlockSpec((t, fc), lambda i, j: (i, j)), pl.BlockSpec((t, d), lambda i, j: (i, 0))],
        out_shape=[jax.ShapeDtypeStruct((s_len, ff), BF16), jax.ShapeDtypeStruct((s_len, d), F32)],
        scratch_shapes=[pltpu.VMEM((HALO3 + t, fc), F32), pltpu.VMEM((HALO3 + t, fc), F32),
                        pltpu.VMEM((t, fc), F32), pltpu.VMEM((t, fc), F32)],
        compiler_params=_params("parallel", "arbitrary"), name=name,
    )(uf, uf, uf, uf, x1, wf, wf, w_down)


def _loss_bwd(x, g, target, *, name):
    s_len, d = x.shape
    t = _seq_tile(s_len)

    def body(x_ref, g_ref, t_ref, l_ref, dx_ref, dxb_ref, dg_ref):
        @pl.when(pl.program_id(0) == 0)
        def _():
            l_ref[...] = jnp.zeros_like(l_ref)
            dg_ref[...] = jnp.zeros_like(dg_ref)

        def blk(rows):
            xv = x_ref[rows, :]
            r = lax.rsqrt(jnp.mean(xv * xv, axis=-1, keepdims=True) + EPS)
            xn = xv * r
            e = xn * g_ref[...] - t_ref[rows, :]
            l_ref[...] += _rows8(e * e)
            dy = e * (1.0 / d)
            dg_ref[...] += _rows8(dy * xn)
            dn = dy * g_ref[...]
            dx = r * (dn - xn * jnp.mean(dn * xn, axis=-1, keepdims=True))
            dx_ref[rows, :] = dx
            dxb_ref[rows, :] = dx.astype(BF16)
        _row_loop(t, 64, blk)

    row = pl.BlockSpec((t, d), lambda i: (i, 0))
    part = pl.BlockSpec((SUB, d), lambda i: (0, 0))
    return pl.pallas_call(
        body, grid=(s_len // t,),
        in_specs=[row, _resident((1, d), lambda i: (0, 0)), row],
        out_specs=[part, row, row, part],
        out_shape=[jax.ShapeDtypeStruct((SUB, d), F32), jax.ShapeDtypeStruct((s_len, d), F32),
                   jax.ShapeDtypeStruct((s_len, d), BF16), jax.ShapeDtypeStruct((SUB, d), F32)],
        compiler_params=_params("arbitrary"), name=name,
    )(x, g, target)


def _ffn_bwd(dx2, uf, wf, w_down_t, *, name):
    s_len, ff2 = uf.shape
    ff = ff2 // 2
    d = dx2.shape[1]
    t = _seq_tile(s_len)
    n_t = s_len // t
    fc = _ff_chunk(ff)
    nc = ff // fc
    per = t // HALO3_BLK

    def body(dx_ref, ug_ref, ugh_ref, uv_ref, uvh_ref, wfg_ref, wfv_ref, wd_ref,
             dug_ref, duv_ref, dwg_ref, dwv_ref, gwin, vwin, cg, cv, dact, dgw, dvw, awg, awv):
        i = pl.program_id(1)
        first = i == n_t - 1
        _ffn_windows(ug_ref, ugh_ref, uv_ref, uvh_ref, gwin, vwin, first, t)
        _conv_taps(gwin, wfg_ref, cg, taps=SHORT_K, n_rows=t, base=HALO3, width=fc)
        _conv_taps(vwin, wfv_ref, cv, taps=SHORT_K, n_rows=t, base=HALO3, width=fc)
        dact[...] = _dot(dx_ref[...], wd_ref[...])

        @pl.when(i == 0)
        def _():
            dgw[t:t + HALO3, :] = jnp.zeros((HALO3, fc), F32)
            dvw[t:t + HALO3, :] = jnp.zeros((HALO3, fc), F32)
            awg[...] = jnp.zeros_like(awg)
            awv[...] = jnp.zeros_like(awv)

        def blk(rows):
            gv = cg[rows, :]
            sg = _sigmoid(gv)
            da = dact[rows, :]
            dgw[rows, :] = (da * cv[rows, :]) * (sg * (1.0 + gv * (1.0 - sg)))
            dvw[rows, :] = da * (gv * sg)
        _row_loop(t, 32, blk, unroll=2)

        _conv_taps(dgw, wfg_ref, dug_ref, taps=SHORT_K, n_rows=t, base=0, width=fc, transposed=True)
        _conv_taps(dvw, wfv_ref, duv_ref, taps=SHORT_K, n_rows=t, base=0, width=fc, transposed=True)
        _conv_wgrad(dgw, gwin, awg, taps=SHORT_K, n_rows=t, base=HALO3, width=fc)
        _conv_wgrad(dvw, vwin, awv, taps=SHORT_K, n_rows=t, base=HALO3, width=fc)
        dgw[t:t + HALO3, :] = dgw[0:HALO3, :]
        dvw[t:t + HALO3, :] = dvw[0:HALO3, :]

        @pl.when(i == n_t - 1)
        def _():
            dwg_ref[...] = _fold8(awg, SHORT_K)
            dwv_ref[...] = _fold8(awv, SHORT_K)

    rev = lambda i: n_t - 1 - i
    halo_map = lambda off: (lambda j, i: (jnp.maximum(rev(i) * per - 1, 0), j + off))
    return pl.pallas_call(
        body, grid=(nc, n_t),
        in_specs=[pl.BlockSpec((t, d), lambda j, i: (rev(i), 0)),
                  pl.BlockSpec((t, fc), lambda j, i: (rev(i), j)), pl.BlockSpec((HALO3_BLK, fc), halo_map(0)),
                  pl.BlockSpec((t, fc), lambda j, i: (rev(i), j + nc)), pl.BlockSpec((HALO3_BLK, fc), halo_map(nc)),
                  pl.BlockSpec((SHORT_K, fc), lambda j, i: (0, j)),
                  pl.BlockSpec((SHORT_K, fc), lambda j, i: (0, j + nc)),
                  pl.BlockSpec((d, fc), lambda j, i: (0, j))],
        out_specs=[pl.BlockSpec((t, fc), lambda j, i: (rev(i), j)), pl.BlockSpec((t, fc), lambda j, i: (rev(i), j)),
                   pl.BlockSpec((SHORT_K, fc), lambda j, i: (0, j)), pl.BlockSpec((SHORT_K, fc), lambda j, i: (0, j))],
        out_shape=[jax.ShapeDtypeStruct((s_len, ff), BF16), jax.ShapeDtypeStruct((s_len, ff), BF16),
                   jax.ShapeDtypeStruct((SHORT_K, ff), F32), jax.ShapeDtypeStruct((SHORT_K, ff), F32)],
        scratch_shapes=[pltpu.VMEM((HALO3 + t, fc), F32), pltpu.VMEM((HALO3 + t, fc), F32),
                        pltpu.VMEM((t, fc), F32), pltpu.VMEM((t, fc), F32), pltpu.VMEM((t, fc), F32),
                        pltpu.VMEM((t + HALO3, fc), F32), pltpu.VMEM((t + HALO3, fc), F32),
                        pltpu.VMEM((SHORT_K * SUB, fc), F32), pltpu.VMEM((SHORT_K * SUB, fc), F32)],
        compiler_params=_params("arbitrary", "arbitrary"), name=name,
    )(dx2, uf, uf, uf, uf, wf, wf, w_down_t)


def _mix_bwd(dx1, u, ca, wa, lg, lb, wb, w_out_t, *, name):
    s_len, d_in = u.shape
    d = dx1.shape[1]
    c = D_CONF
    t = _seq_tile(s_len)
    n_t = s_len // t
    per = t // HALO

    def body(dx_ref, u_ref, uh_ref, ca_ref, wa_ref, lg_ref, lb_ref, wb_ref, wo_ref,
             du_ref, dwa_ref, dwb_ref, dba_ref, dlg_ref, dlb_ref, dbin_ref,
             gw, pw, cb, dyc, dcaw, dcbw, dglu, dp, awa, awb):
        i = pl.program_id(0)
        first = i == n_t - 1
        _mix_windows(u_ref, uh_ref, gw, pw, first, t)
        _conv_taps(pw, wb_ref, cb, taps=SHORT_K, n_rows=t, base=HALO3, width=c)
        dyc[...] = _dot(dx_ref[...], wo_ref[...])

        @pl.when(i == 0)
        def _():
            dcaw[t:t + HALO, :] = jnp.zeros((HALO, c), F32)
            dcbw[t:t + HALO3, :] = jnp.zeros((HALO3, c), F32)
            awa[...] = jnp.zeros_like(awa)
            awb[...] = jnp.zeros_like(awb)
            dba_ref[...] = jnp.zeros_like(dba_ref)
            dlg_ref[...] = jnp.zeros_like(dlg_ref)
            dlb_ref[...] = jnp.zeros_like(dlb_ref)
            dbin_ref[...] = jnp.zeros_like(dbin_ref)

        def blk1(rows):
            cv = ca_ref[rows, :]
            mu = jnp.mean(cv, axis=-1, keepdims=True)
            xc = cv - mu
            rstd = lax.rsqrt(jnp.mean(xc * xc, axis=-1, keepdims=True) + EPS)
            nrm = xc * rstd
            ln = nrm * lg_ref[...] + lb_ref[...]
            sg = _sigmoid(ln)
            dln = dyc[rows, 0:c] * (sg * (1.0 + ln * (1.0 - sg)))
            dlg_ref[...] += _rows8(dln * nrm)
            dlb_ref[...] += _rows8(dln)
            dn = dln * lg_ref[...]
            dca = rstd * (dn - jnp.mean(dn, axis=-1, keepdims=True)
                          - nrm * jnp.mean(dn * nrm, axis=-1, keepdims=True))
            dcaw[rows, :] = dca
            dba_ref[...] += _rows8(dca)
            ds = dyc[rows, c:2 * c]
            dgb = ds * cb[rows, :]
            dcbw[rows, :] = ds * u_ref[rows, 2 * c:3 * c].astype(F32)
            du_ref[rows, 2 * c:3 * c] = dgb.astype(BF16)
            dbin_ref[:, 2 * c:3 * c] += _rows8(dgb)
        _row_loop(t, 64, blk1, unroll=2)

        _conv_taps(dcaw, wa_ref, dglu, taps=CONF_K, n_rows=t, base=0, width=c, transposed=True)
        _conv_taps(dcbw, wb_ref, dp, taps=SHORT_K, n_rows=t, base=0, width=c, transposed=True)
        _conv_wgrad(dcaw, gw, awa, taps=CONF_K, n_rows=t, base=HALO, width=c)
        _conv_wgrad(dcbw, pw, awb, taps=SHORT_K, n_rows=t, base=HALO3, width=c)
        dcaw[t:t + HALO, :] = dcaw[0:HALO, :]
        dcbw[t:t + HALO3, :] = dcbw[0:HALO3, :]

        def blk2(rows):
            av = u_ref[rows, 0:c].astype(F32)
            sg = _sigmoid(u_ref[rows, c:2 * c].astype(F32))
            dg = dglu[rows, :]
            d_av = dg * sg
            d_ag = (dg * av) * (sg * (1.0 - sg))
            dpv = dp[rows, :]
            d_gc = dpv * u_ref[rows, 4 * c:5 * c].astype(F32)
            d_vs = dpv * u_ref[rows, 3 * c:4 * c].astype(F32)
            du_ref[rows, 0:c] = d_av.astype(BF16)
            du_ref[rows, c:2 * c] = d_ag.astype(BF16)
            du_ref[rows, 3 * c:4 * c] = d_gc.astype(BF16)
            du_ref[rows, 4 * c:5 * c] = d_vs.astype(BF16)
            dbin_ref[:, 0:c] += _rows8(d_av)
            dbin_ref[:, c:2 * c] += _rows8(d_ag)
            dbin_ref[:, 3 * c:4 * c] += _rows8(d_gc)
            dbin_ref[:, 4 * c:5 * c] += _rows8(d_vs)
        _row_loop(t, 64, blk2)

        @pl.when(i == n_t - 1)
        def _():
            dwa_ref[...] = _fold8(awa, CONF_K)
            dwb_ref[...] = _fold8(awb, SHORT_K)

    rev = lambda i: n_t - 1 - i
    small_in = lambda r: _resident((r, c), lambda i: (0, 0))
    small = lambda r: pl.BlockSpec((r, c), lambda i: (0, 0))
    return pl.pallas_call(
        body, grid=(n_t,),
        in_specs=[pl.BlockSpec((t, d), lambda i: (rev(i), 0)),
                  pl.BlockSpec((t, d_in), lambda i: (rev(i), 0)),
                  pl.BlockSpec((HALO, d_in), lambda i: (jnp.maximum(rev(i) * per - 1, 0), 0)),
                  pl.BlockSpec((t, c), lambda i: (rev(i), 0)),
                  small_in(CONF_K), small_in(1), small_in(1), small_in(SHORT_K),
                  _resident((d, 2 * c), lambda i: (0, 0))],
        out_specs=[pl.BlockSpec((t, d_in), lambda i: (rev(i), 0)),
                   small(CONF_K), small(SHORT_K), small(SUB), small(SUB), small(SUB),
                   pl.BlockSpec((SUB, d_in), lambda i: (0, 0))],
        out_shape=[jax.ShapeDtypeStruct((s_len, d_in), BF16),
                   jax.ShapeDtypeStruct((CONF_K, c), F32), jax.ShapeDtypeStruct((SHORT_K, c), F32),
                   jax.ShapeDtypeStruct((SUB, c), F32), jax.ShapeDtypeStruct((SUB, c), F32),
                   jax.ShapeDtypeStruct((SUB, c), F32), jax.ShapeDtypeStruct((SUB, d_in), F32)],
        scratch_shapes=[pltpu.VMEM((HALO + t, c), F32), pltpu.VMEM((HALO3 + t, c), F32),
                        pltpu.VMEM((t, c), F32), pltpu.VMEM((t, 2 * c), F32),
                        pltpu.VMEM((t + HALO, c), F32), pltpu.VMEM((t + HALO3, c), F32),
                        pltpu.VMEM((t, c), F32), pltpu.VMEM((t, c), F32),
                        pltpu.VMEM((CONF_K * SUB, c), F32), pltpu.VMEM((SHORT_K * SUB, c), F32)],
        compiler_params=_params("arbitrary"), name=name,
    )(dx1, u, u, ca, wa, lg, lb, wb, w_out_t)


def _matmul_tn(a, b, *, name):
    s_len, k = a.shape
    n = b.shape[1]
    if k <= n:
        tn = _col_tile(n)
        grid = (n // tn,)
        in_specs = [_resident((s_len, k), lambda j: (0, 0)), pl.BlockSpec((s_len, tn), lambda j: (0, j))]
        out_spec = pl.BlockSpec((k, tn), lambda j: (0, j))
    else:
        tk = _col_tile(k)
        grid = (k // tk,)
        in_specs = [pl.BlockSpec((s_len, tk), lambda j: (0, j)), _resident((s_len, n), lambda j: (0, 0))]
        out_spec = pl.BlockSpec((tk, n), lambda j: (j, 0))

    def body(a_ref, b_ref, o_ref):
        o_ref[...] = _dot_tn(a_ref[...], b_ref[...]).astype(BF16)

    return pl.pallas_call(
        body, grid=grid, in_specs=in_specs, out_specs=out_spec,
        out_shape=jax.ShapeDtypeStruct((k, n), BF16),
        compiler_params=_params("parallel"), name=name,
    )(a, b)


def _matmul_rmsbwd(dzs, wt, x, g, dx_in, *, name):
    s_len, d = x.shape
    n_z = len(dzs)
    nj = dzs[0].shape[1]
    t = _mm_tile(s_len)

    def body(*refs):
        dz_refs = refs[0:n_z]
        w_refs = refs[n_z:2 * n_z]
        x_ref, g_ref, dxi_ref, dx_ref, dxb_ref, dg_ref, dh = refs[2 * n_z:]

        @pl.when(pl.program_id(0) == 0)
        def _():
            dg_ref[...] = jnp.zeros_like(dg_ref)

        acc = _dot(dz_refs[0][...], w_refs[0][...])
        for q in range(1, n_z):
            acc = acc + _dot(dz_refs[q][...], w_refs[q][...])
        dh[...] = acc

        def blk(rows):
            xv = x_ref[rows, :]
            r = lax.rsqrt(jnp.mean(xv * xv, axis=-1, keepdims=True) + EPS)
            xn = xv * r
            dhv = dh[rows, :]
            dg_ref[...] += _rows8(dhv * xn)
            dn = dhv * g_ref[...]
            dx = dxi_ref[rows, :] + r * (dn - xn * jnp.mean(dn * xn, axis=-1, keepdims=True))
            dx_ref[rows, :] = dx
            dxb_ref[rows, :] = dx.astype(BF16)
        _row_loop(t, 128, blk)

    row = pl.BlockSpec((t, d), lambda i: (i, 0))
    in_specs = [pl.BlockSpec((t, nj), lambda i: (i, 0)) for _ in range(n_z)]
    in_specs += [_resident((nj, d), functools.partial(lambda q, i: (q, 0), q)) for q in range(n_z)]
    in_specs += [row, _resident((1, d), lambda i: (0, 0)), row]
    return pl.pallas_call(
        body, grid=(s_len // t,), in_specs=in_specs,
        out_specs=[row, row, pl.BlockSpec((SUB, d), lambda i: (0, 0))],
        out_shape=[jax.ShapeDtypeStruct((s_len, d), F32), jax.ShapeDtypeStruct((s_len, d), BF16),
                   jax.ShapeDtypeStruct((SUB, d), F32)],
        scratch_shapes=[pltpu.VMEM((t, d), F32)],
        compiler_params=_params("arbitrary"), name=name,
    )(*dzs, *([wt] * n_z), x, g, dx_in)


def _row(v):
    return v.reshape(1, -1)


def _layer_fwd(x0, p, tag):
    u, h1 = _rms_matmul(x0, _row(p["mix_norm_g"]), p["w_in"], _row(p["b_in"]), name=f"in_proj_{tag}")
    ycat, x1, ca = _mix_fwd(u, x0, p["conv_a_w"], _row(p["conv_a_b"]), _row(p["ln_a_g"]), _row(p["ln_a_b"]),
                            p["conv_b_w"], p["w_out"], name=f"mix_fwd_{tag}")
    uf, h2 = _rms_matmul(x1, _row(p["ffn_norm_g"]), p["w_up"], None, name=f"up_proj_{tag}")
    act, x2 = _ffn_fwd(uf, x1, p["conv_f_w"], p["w_down"], name=f"ffn_fwd_{tag}")
    return x2, dict(x0=x0, h1=h1, u=u, ca=ca, ycat=ycat, x1=x1, h2=h2, uf=uf, act=act)


def _layer_bwd(dx2, dx2_b, p, saved, tag):
    dug, duv, dwf_g, dwf_v = _ffn_bwd(dx2_b, saved["uf"], p["conv_f_w"], p["w_down_t"], name=f"ffn_bwd_{tag}")
    g_down = _matmul_tn(saved["act"], dx2_b, name=f"dw_down_{tag}")
    g_up = [_matmul_tn(saved["h2"], dug, name=f"dw_up_g_{tag}"), _matmul_tn(saved["h2"], duv, name=f"dw_up_v_{tag}")]
    dx1, dx1_b, dg2 = _matmul_rmsbwd([dug, duv], p["w_up_t"], saved["x1"], _row(p["ffn_norm_g"]), dx2,
                                     name=f"dh_ffn_{tag}")
    du, dwa, dwb, dba, dlg, dlb, dbin = _mix_bwd(
        dx1_b, saved["u"], saved["ca"], p["conv_a_w"], _row(p["ln_a_g"]), _row(p["ln_a_b"]),
        p["conv_b_w"], p["w_out_t"], name=f"mix_bwd_{tag}")
    g_out = _matmul_tn(saved["ycat"], dx1_b, name=f"dw_out_{tag}")
    g_in = _matmul_tn(saved["h1"], du, name=f"dw_in_{tag}")
    dx0, dx0_b, dg1 = _matmul_rmsbwd([du], p["w_in_t"], saved["x0"], _row(p["mix_norm_g"]), dx1,
                                     name=f"dh_mix_{tag}")
    big = dict(w_in=g_in, w_out=g_out, w_up=g_up, w_down=g_down)
    conv = dict(conv_a_w=dwa, conv_b_w=dwb, conv_f_w=jnp.concatenate([dwf_g, dwf_v], axis=1))
    rep = dict(mix_norm_g=dg1, b_in=dbin, conv_a_b=dba, ln_a_g=dlg, ln_a_b=dlb, ffn_norm_g=dg2)
    return dx0, dx0_b, big, conv, rep


def _place():
    return lax.axis_index("x"), lax.axis_index("y"), lax.axis_index("c")


def _all_gather(arrs, *, name):
    n_a = len(arrs)

    def body(*refs):
        ins = refs[0:n_a]
        outs = refs[n_a:2 * n_a]
        send_sems, recv_sems, local_sems = refs[2 * n_a:]
        x, y, c = _place()
        sibling = (x, y, 1 - c)
        chips = [(1 - x, y), (x, 1 - y), (1 - x, 1 - y)]

        def slot(a, px, py, pc):
            return outs[a].at[4 * px + 2 * py + pc]

        def copy(a, k, block, to, src=None):
            return pltpu.make_async_remote_copy(
                src_ref=slot(a, *block) if src is None else src, dst_ref=slot(a, *block),
                send_sem=send_sems.at[a, k], recv_sem=recv_sems.at[a, k],
                device_id=to, device_id_type=MESH)

        me = (x, y, c)
        mine = [pltpu.make_async_copy(ins[a], slot(a, *me), local_sems.at[a]) for a in range(n_a)]
        for cp in mine:
            cp.start()
        started = []
        for a in range(n_a):
            first = [copy(a, 0, me, sibling, src=ins[a])]
            first += [copy(a, 1 + j, me, (*chip, c), src=ins[a]) for j, chip in enumerate(chips)]
            for cp in first:
                cp.start()
            started += first
        for a in range(n_a):
            for j, chip in enumerate(chips):
                copy(a, 1 + j, (*chip, c), me).wait_recv()
                passed = copy(a, 4 + j, (*chip, c), sibling)
                passed.start()
                started.append(passed)
        for a in range(n_a):
            copy(a, 0, sibling, me).wait_recv()
            for j, chip in enumerate(chips):
                copy(a, 4 + j, (*chip, 1 - c), me).wait_recv()
        for cp in started:
            cp.wait_send()
        for cp in mine:
            cp.wait()

    return pl.pallas_call(
        body, in_specs=[ANY] * n_a, out_specs=[ANY] * n_a,
        out_shape=[jax.ShapeDtypeStruct((N_DEV, *a.shape), a.dtype) for a in arrs],
        scratch_shapes=[pltpu.SemaphoreType.DMA((n_a, 7)), pltpu.SemaphoreType.DMA((n_a, 7)),
                        pltpu.SemaphoreType.DMA((n_a,))],
        name=name,
    )(*arrs)


def _sibling_exchange(arrs, *, name):
    n_a = len(arrs)

    def body(*refs):
        ins = refs[0:n_a]
        outs = refs[n_a:2 * n_a]
        send_sems, recv_sems = refs[2 * n_a:]
        x, y, c = _place()
        copies = [pltpu.make_async_remote_copy(
            src_ref=ins[a].at[:, 1 - c], dst_ref=outs[a], send_sem=send_sems.at[a], recv_sem=recv_sems.at[a],
            device_id=(x, y, 1 - c), device_id_type=MESH) for a in range(n_a)]
        for cp in copies:
            cp.start()
        for cp in copies:
            cp.wait()

    return pl.pallas_call(
        body, in_specs=[ANY] * n_a, out_specs=[ANY] * n_a,
        out_shape=[jax.ShapeDtypeStruct((N_CHIP, *a.shape[2:]), a.dtype) for a in arrs],
        scratch_shapes=[pltpu.SemaphoreType.DMA((n_a,)), pltpu.SemaphoreType.DMA((n_a,))],
        name=name,
    )(*arrs)


def _row_tile(r):
    for tr in (512, 352, 256, 128, 64, 32, 16, 8):
        if r % tr == 0:
            return tr
    return r


def _pair_sum(mine, theirs, core, *, name):
    n_chip, _, r, c = mine.shape
    tr = _row_tile(r)

    def body(core_ref, a_ref, b_ref, o_ref):
        o_ref[...] = (a_ref[...].astype(F32) + b_ref[...].astype(F32)).astype(o_ref.dtype)

    return pl.pallas_call(
        body,
        grid_spec=pltpu.PrefetchScalarGridSpec(
            num_scalar_prefetch=1, grid=(n_chip, r // tr),
            in_specs=[pl.BlockSpec((None, None, tr, c), lambda q, i, core_ref: (q, core_ref[0], i, 0)),
                      pl.BlockSpec((None, tr, c), lambda q, i, core_ref: (q, i, 0))],
            out_specs=pl.BlockSpec((None, tr, c), lambda q, i, core_ref: (q, i, 0))),
        out_shape=jax.ShapeDtypeStruct((n_chip, r, c), mine.dtype),
        compiler_params=_params("parallel", "parallel"), name=name,
    )(core, mine, theirs)


def _chip_exchange(arrs, *, name):
    n_a = len(arrs)

    def body(*refs):
        ins = refs[0:n_a]
        outs = refs[n_a:2 * n_a]
        send_sems, recv_sems, local_sems = refs[2 * n_a:]
        x, y, c = _place()
        my_chip = 2 * x + y
        chips = [(1 - x, y), (x, 1 - y), (1 - x, 1 - y)]
        mine = [pltpu.make_async_copy(ins[a].at[my_chip], outs[a].at[my_chip], local_sems.at[a]) for a in range(n_a)]
        for cp in mine:
            cp.start()
        copies = []
        for a in range(n_a):
            for j, (px, py) in enumerate(chips):
                copies.append(pltpu.make_async_remote_copy(
                    src_ref=ins[a].at[2 * px + py], dst_ref=outs[a].at[my_chip],
                    send_sem=send_sems.at[a, j], recv_sem=recv_sems.at[a, j],
                    device_id=(px, py, c), device_id_type=MESH))
        for cp in copies:
            cp.start()
        for cp in copies:
            cp.wait()
        for cp in mine:
            cp.wait()

    return pl.pallas_call(
        body, in_specs=[ANY] * n_a, out_specs=[ANY] * n_a,
        out_shape=[jax.ShapeDtypeStruct(a.shape, a.dtype) for a in arrs],
        scratch_shapes=[pltpu.SemaphoreType.DMA((n_a, 3)), pltpu.SemaphoreType.DMA((n_a, 3)),
                        pltpu.SemaphoreType.DMA((n_a,))],
        name=name,
    )(*arrs)


def _adamw_math(g, w, m, v):
    m = ADAM_B1 * m + (1.0 - ADAM_B1) * g
    v = ADAM_B2 * v + (1.0 - ADAM_B2) * (g * g)
    m_hat = m / (1.0 - ADAM_B1 ** ADAM_STEP)
    v_hat = v / (1.0 - ADAM_B2 ** ADAM_STEP)
    delta = -ADAM_LR * (m_hat / (jnp.sqrt(v_hat) + ADAM_EPS) + ADAM_WD * w)
    return delta, m, v


def _adamw_sharded(parts, w, m, v, *, name):
    n_layers, r, c = w.shape
    n_chip = parts[0].shape[0]
    tr = _row_tile(r)
    n_i = r // tr

    def body(*refs):
        p_refs = refs[0:n_layers]
        w_ref, m_ref, v_ref, g_out, d_out, m_out, v_out = refs[n_layers:]
        layer = pl.program_id(0)
        for l in range(n_layers):
            @pl.when(layer == l)
            def _(l=l):
                g = p_refs[l][0].astype(F32)
                for q in range(1, n_chip):
                    g = g + p_refs[l][q].astype(F32)
                delta, m_new, v_new = _adamw_math(g, w_ref[...], m_ref[...], v_ref[...])
                g_out[...] = g
                d_out[...] = delta
                m_out[...] = m_new
                v_out[...] = v_new

    def part_map(l):
        return lambda layer, i: (0, jnp.where(layer == l, i, jnp.where(layer < l, 0, n_i - 1)), 0)

    blk = pl.BlockSpec((None, tr, c), lambda layer, i: (layer, i, 0))
    return pl.pallas_call(
        body, grid=(n_layers, n_i),
        in_specs=[pl.BlockSpec((n_chip, tr, c), part_map(l)) for l in range(n_layers)] + [blk, blk, blk],
        out_specs=[blk] * 4, out_shape=[jax.ShapeDtypeStruct((n_layers, r, c), F32)] * 4,
        compiler_params=_params("arbitrary", "arbitrary"), name=name,
    )(*parts, w, m, v)


def _adamw_replicated(parts, names, w, m, v, n_loss, *, name):
    n_dev = parts.shape[0]
    n_layers = w[names[0]].shape[0]
    every = list(names) + ["final_norm_g"]
    n_p = len(every)

    def body(*refs):
        p_ref = refs[0]
        w_refs = dict(zip(every, refs[1:1 + n_p]))
        m_refs = dict(zip(every, refs[1 + n_p:1 + 2 * n_p]))
        v_refs = dict(zip(every, refs[1 + 2 * n_p:1 + 3 * n_p]))
        l_out = refs[1 + 3 * n_p]
        outs = refs[2 + 3 * n_p:]
        o_refs = {n: outs[4 * q:4 * q + 4] for q, n in enumerate(every)}
        acc = p_ref[0]
        for q in range(1, n_dev):
            acc = acc + p_ref[q]
        tot = jnp.sum(acc, axis=0, keepdims=True)
        pos = 0
        where = [(n, l) for l in range(n_layers) for n in names] + [("final_norm_g", 0)]
        for n, l in where:
            width = w_refs[n].shape[1]
            g = tot[:, pos:pos + width]
            pos += width
            row = pl.ds(l, 1)
            delta, m_new, v_new = _adamw_math(g, w_refs[n][row, :], m_refs[n][row, :], v_refs[n][row, :])
            for o, val in zip(o_refs[n], (g, delta, m_new, v_new)):
                o[row, :] = val
        l_out[...] = (0.5 / n_loss) * jnp.sum(tot[:, pos:pos + n_loss], axis=-1, keepdims=True)

    shapes = [jax.ShapeDtypeStruct((1, 1), F32)]
    for n in every:
        shapes += [jax.ShapeDtypeStruct(w[n].shape, F32)] * 4
    res = pl.pallas_call(
        body, out_shape=shapes,
        compiler_params=pltpu.CompilerParams(vmem_limit_bytes=VMEM_LIMIT), name=name,
    )(parts, *[w[n] for n in every], *[m[n] for n in every], *[v[n] for n in every])
    return res[0], {n: res[1 + 4 * q:5 + 4 * q] for q, n in enumerate(every)}


BIG = ("w_in", "w_out", "w_up", "w_down")
COL_SHARDED = ("w_in", "w_up")
CONV = ("conv_a_w", "conv_b_w", "conv_f_w")
REPLICATED = ("mix_norm_g", "b_in", "conv_a_b", "ln_a_g", "ln_a_b", "ffn_norm_g")
KINDS = ("grad", "delta", "m", "v")


def _weights_from_gathered(name, g):
    n_dev, r, c = g.shape
    if name in COL_SHARDED:
        return g.transpose(1, 0, 2).reshape(r, n_dev * c), g.transpose(0, 2, 1).reshape(n_dev * c, r)
    full = g.reshape(n_dev * r, c)
    return full, full.T


def _slabs_from_full(name, grad):
    if name == "w_up":
        halves = [h.reshape(h.shape[0], N_DEV // 2, -1).transpose(1, 0, 2) for h in grad]
        return jnp.concatenate(halves, axis=0)
    r, c = grad.shape
    if name in COL_SHARDED:
        return grad.reshape(r, N_DEV, c // N_DEV).transpose(1, 0, 2)
    return grad.reshape(N_DEV, r // N_DEV, c)


def _reduce_scatter(slabs, core, tag):
    slabs = [s.reshape(N_CHIP, 2, *s.shape[1:]) for s in slabs]
    theirs = _sibling_exchange(slabs, name=f"reduce_siblings_{tag}")
    pairs = [_pair_sum(a, b, core, name=f"pair_sum_{tag}_{q}") for q, (a, b) in enumerate(zip(slabs, theirs))]
    return _chip_exchange(pairs, name=f"reduce_chips_{tag}")


def kernel(x, mix_norm_g, w_in, b_in, conv_a_w, conv_a_b, ln_a_g, ln_a_b, conv_b_w, w_out, ffn_norm_g, w_up, conv_f_w, w_down, final_norm_g, loss_target, m_mix_norm_g, m_w_in, m_b_in, m_conv_a_w, m_conv_a_b, m_ln_a_g, m_ln_a_b, m_conv_b_w, m_w_out, m_ffn_norm_g, m_w_up, m_conv_f_w, m_w_down, m_final_norm_g, v_mix_norm_g, v_w_in, v_b_in, v_conv_a_w, v_conv_a_b, v_ln_a_g, v_ln_a_b, v_conv_b_w, v_w_out, v_ffn_norm_g, v_w_up, v_conv_f_w, v_w_down, v_final_norm_g):
    w = dict(mix_norm_g=mix_norm_g, w_in=w_in, b_in=b_in, conv_a_w=conv_a_w, conv_a_b=conv_a_b, ln_a_g=ln_a_g,
             ln_a_b=ln_a_b, conv_b_w=conv_b_w, w_out=w_out, ffn_norm_g=ffn_norm_g, w_up=w_up, conv_f_w=conv_f_w,
             w_down=w_down, final_norm_g=final_norm_g)
    m = dict(mix_norm_g=m_mix_norm_g, w_in=m_w_in, b_in=m_b_in, conv_a_w=m_conv_a_w, conv_a_b=m_conv_a_b,
             ln_a_g=m_ln_a_g, ln_a_b=m_ln_a_b, conv_b_w=m_conv_b_w, w_out=m_w_out, ffn_norm_g=m_ffn_norm_g,
             w_up=m_w_up, conv_f_w=m_conv_f_w, w_down=m_w_down, final_norm_g=m_final_norm_g)
    v = dict(mix_norm_g=v_mix_norm_g, w_in=v_w_in, b_in=v_b_in, conv_a_w=v_conv_a_w, conv_a_b=v_conv_a_b,
             ln_a_g=v_ln_a_g, ln_a_b=v_ln_a_b, conv_b_w=v_conv_b_w, w_out=v_w_out, ffn_norm_g=v_ffn_norm_g,
             w_up=v_w_up, conv_f_w=v_conv_f_w, w_down=v_w_down, final_norm_g=v_final_norm_g)
    order = list(w)
    n_layers = w_in.shape[0]
    n_big = len(BIG)
    xs = x[0]
    target = loss_target[0]
    core = lax.axis_index("c").astype(jnp.int32).reshape(1)

    shards = [w[n][l].astype(BF16) for l in range(n_layers) for n in BIG]
    gathered = _all_gather(shards + [w[n] for n in CONV], name="gather_weights")
    conv_full = {}
    for n, g in zip(CONV, gathered[n_layers * n_big:]):
        n_dev, _, taps, c = g.shape
        conv_full[n] = g.transpose(1, 2, 0, 3).reshape(n_layers, taps, n_dev * c)

    def layer_params(l):
        p = {n: conv_full[n][l] for n in CONV}
        for q, n in enumerate(BIG):
            p[n], p[n + "_t"] = _weights_from_gathered(n, gathered[l * n_big + q])
        p.update({n: w[n][l] for n in REPLICATED})
        return p

    params = [layer_params(l) for l in range(n_layers)]

    h = xs
    saved = []
    for l in range(n_layers):
        h, keep = _layer_fwd(h, params[l], str(l))
        saved.append(keep)
    loss_sq, dh, dh_b, dgf = _loss_bwd(h, _row(final_norm_g), target, name="loss")
    big_g = [None] * n_layers
    conv_g = {n: [None] * n_layers for n in CONV}
    rep_g = [None] * n_layers
    for l in reversed(range(n_layers)):
        dh, dh_b, big_g[l], conv, rep_g[l] = _layer_bwd(dh, dh_b, params[l], saved[l], str(l))
        for n in CONV:
            conv_g[n][l] = conv[n]

    slabs = [_slabs_from_full(n, big_g[l][n]) for l in range(n_layers) for n in BIG]
    for n in CONV:
        full = jnp.stack(conv_g[n])
        _, taps, c = full.shape
        slabs.append(full.reshape(n_layers, taps, N_DEV, c // N_DEV).transpose(2, 0, 1, 3)
                     .reshape(N_DEV, n_layers * taps, c // N_DEV))
    parts = _reduce_scatter(slabs, core, "all")

    out = {k: {} for k in KINDS}
    for q, n in enumerate(BIG):
        layer_parts = [parts[l * n_big + q] for l in range(n_layers)]
        for k, r in zip(KINDS, _adamw_sharded(layer_parts, w[n], m[n], v[n], name=f"adamw_{n}")):
            out[k][n] = r
    for n, p in zip(CONV, parts[n_layers * n_big:]):
        as_one = lambda a: a.reshape(1, *p.shape[1:])
        for k, r in zip(KINDS, _adamw_sharded([p], as_one(w[n]), as_one(m[n]), as_one(v[n]), name=f"adamw_{n}")):
            out[k][n] = r.reshape(w[n].shape)

    rep_cols = [rep_g[l][n] for l in range(n_layers) for n in REPLICATED] + [dgf, loss_sq]
    rep_all = _all_gather([jnp.concatenate(rep_cols, axis=1)], name="gather_small")[0]
    with_final = lambda d: {**{n: d[n] for n in REPLICATED}, "final_norm_g": _row(d["final_norm_g"])}
    loss, rep_res = _adamw_replicated(rep_all, REPLICATED, with_final(w), with_final(m), with_final(v),
                                      loss_sq.shape[1], name="adamw_small")
    for n, res in rep_res.items():
        for k, r in zip(KINDS, res):
            out[k][n] = r.reshape(w[n].shape)

    grad_x = dh.reshape(x.shape)
    return (loss.reshape(()), grad_x, *[out["grad"][n] for n in order], *[out["delta"][n] for n in order],
            *[out["m"][n] for n in order], *[out["v"][n] for n in order])
```

```python
import functools

import jax
import jax.numpy as jnp
from jax import lax
from jax.experimental import pallas as pl
from jax.experimental.pallas import tpu as pltpu

F32 = jnp.float32
BF16 = jnp.bfloat16

N_DEV = 8
N_CHIP = 4
D_CONF = 512
CONF_K = 31
SHORT_K = 3
EPS = 1e-6
HALO = 32
HALO3 = 8
HALO3_BLK = 16
LANES = 128
SUB = 8
VMEM_LIMIT = 56 * 1024 * 1024

ADAM_LR = 0.001
ADAM_B1 = 0.9
ADAM_B2 = 0.999
ADAM_EPS = 1e-08
ADAM_WD = 0.01
ADAM_STEP = 10

MESH = pl.DeviceIdType.MESH
ANY = pl.BlockSpec(memory_space=pl.ANY)


def _params(*sem):
    return pltpu.CompilerParams(dimension_semantics=sem, vmem_limit_bytes=VMEM_LIMIT)


def _resident(shape, index_map):
    return pl.BlockSpec(shape, index_map, pipeline_mode=pl.Buffered(1))


def _row_loop(n_rows, rb, fn, unroll=1):
    rb = min(rb, n_rows)

    def body(i, carry):
        fn(pl.ds(pl.multiple_of(i * rb, rb), rb))
        return carry
    lax.fori_loop(0, n_rows // rb, body, 0, unroll=unroll)


def _rows8(v):
    acc = v[0:SUB]
    for k in range(1, v.shape[0] // SUB):
        acc = acc + v[k * SUB:(k + 1) * SUB]
    return acc


def _sigmoid(z):
    return 0.5 * jnp.tanh(0.5 * z) + 0.5


def _dot(a, b):
    return jnp.dot(a, b, preferred_element_type=F32)


def _dot_tn(a, b):
    return lax.dot_general(a, b, (((0,), (0,)), ((), ())), preferred_element_type=F32)


def _conv_taps(win, w_ref, out, *, taps, n_rows, base, width, transposed=False, bias_ref=None):
    rb = min(64, n_rows)

    def lane_body(cb, carry):
        lanes = pl.ds(pl.multiple_of(cb * LANES, LANES), LANES)
        for r0 in range(0, n_rows, rb):
            acc = None
            for k in range(taps):
                off = (taps - 1 - k) if transposed else (k - (taps - 1))
                term = w_ref[pl.ds(k, 1), lanes] * win[pl.ds(base + r0 + off, rb), lanes]
                acc = term if acc is None else acc + term
            if bias_ref is not None:
                acc = acc + bias_ref[:, lanes]
            out[pl.ds(r0, rb), lanes] = acc.astype(out.dtype)
        return carry

    lax.fori_loop(0, width // LANES, lane_body, 0)


def _conv_wgrad(dy, win, dw_acc, *, taps, n_rows, base, width):
    rb = min(64, n_rows)

    def lane_body(cb, carry):
        lanes = pl.ds(pl.multiple_of(cb * LANES, LANES), LANES)
        for k in range(taps):
            acc = None
            for r0 in range(0, n_rows, rb):
                prod = dy[pl.ds(r0, rb), lanes] * win[pl.ds(base + r0 + k - (taps - 1), rb), lanes]
                acc = prod if acc is None else acc + prod
            dw_acc[pl.ds(k * SUB, SUB), lanes] += _rows8(acc)
        return carry

    lax.fori_loop(0, width // LANES, lane_body, 0)


def _fold8(acc_ref, taps):
    return jnp.concatenate(
        [jnp.sum(acc_ref[pl.ds(k * SUB, SUB), :], axis=0, keepdims=True) for k in range(taps)], axis=0)


def _seq_tile(s_len):
    return min(256, s_len)


def _mm_tile(s_len):
    return min(512, s_len)


def _ff_chunk(ff):
    best = LANES
    for c in range(LANES, 1408 + 1, LANES):
        if ff % c == 0:
            best = c
    return best


def _col_tile(n):
    for c in (512, 1408, 256, LANES):
        if n % c == 0:
            return c
    return n


def _rms_matmul(x, g, w, b, *, name, dep=None):
    s_len, d = x.shape
    n = w.shape[1]
    tm = _mm_tile(s_len)
    cn = _col_tile(n)
    has_bias = b is not None

    def body(*refs):
        x_ref, g_ref, w_ref = refs[0:3]
        b_ref = refs[3] if has_bias else None
        o_ref, h_ref = refs[-2:]

        def blk(rows):
            xv = x_ref[rows, :]
            r = lax.rsqrt(jnp.mean(xv * xv, axis=-1, keepdims=True) + EPS)
            h_ref[rows, :] = ((xv * r) * g_ref[...]).astype(BF16)
        _row_loop(tm, 128, blk)

        def chunk(j, carry):
            cols = pl.ds(pl.multiple_of(j * cn, cn), cn)
            acc = _dot(h_ref[...], w_ref[:, cols])
            if has_bias:
                acc = acc + b_ref[:, cols]
            o_ref[:, cols] = acc.astype(BF16)
            return carry
        lax.fori_loop(0, n // cn, chunk, 0)

    in_specs = [pl.BlockSpec((tm, d), lambda i: (i, 0)), _resident((1, d), lambda i: (0, 0)),
                _resident((d, n), lambda i: (0, 0))]
    args = [x, g, w]
    if has_bias:
        in_specs.append(_resident((1, n), lambda i: (0, 0)))
        args.append(b)
    in_specs.append(ANY)
    args.append(x if dep is None else dep)
    return pl.pallas_call(
        body, grid=(s_len // tm,), in_specs=in_specs,
        out_specs=[pl.BlockSpec((tm, n), lambda i: (i, 0)), pl.BlockSpec((tm, d), lambda i: (i, 0))],
        out_shape=[jax.ShapeDtypeStruct((s_len, n), BF16), jax.ShapeDtypeStruct((s_len, d), BF16)],
        compiler_params=_params("parallel"), name=name,
    )(*args)


def _mix_windows(u_ref, uh_ref, gw, pw, first, t):
    c = D_CONF
    uh = uh_ref[...].astype(F32)
    gw[0:HALO, :] = jnp.where(first, 0.0, uh[:, 0:c] * _sigmoid(uh[:, c:2 * c]))
    pw[0:HALO3, :] = jnp.where(first, 0.0, uh[HALO - HALO3:HALO, 3 * c:4 * c] * uh[HALO - HALO3:HALO, 4 * c:5 * c])

    def blk(rows):
        dst = pl.ds(pl.multiple_of(rows.start + HALO, SUB), rows.size)
        gw[dst, :] = u_ref[rows, 0:c].astype(F32) * _sigmoid(u_ref[rows, c:2 * c].astype(F32))
        dst3 = pl.ds(pl.multiple_of(rows.start + HALO3, SUB), rows.size)
        pw[dst3, :] = u_ref[rows, 3 * c:4 * c].astype(F32) * u_ref[rows, 4 * c:5 * c].astype(F32)
    _row_loop(t, 64, blk)


def _mix_fwd(u, x0, wa, ba, lg, lb, wb, w_out, *, name):
    s_len, d_in = u.shape
    d = x0.shape[1]
    c = D_CONF
    t = _seq_tile(s_len)
    per = t // HALO

    def body(u_ref, uh_ref, x0_ref, wa_ref, ba_ref, lg_ref, lb_ref, wb_ref, wo_ref, y_ref, x1_ref, ca,
             gw, pw, cb):
        first = pl.program_id(0) == 0
        _mix_windows(u_ref, uh_ref, gw, pw, first, t)
        _conv_taps(gw, wa_ref, ca, taps=CONF_K, n_rows=t, base=HALO, width=c, bias_ref=ba_ref)
        _conv_taps(pw, wb_ref, cb, taps=SHORT_K, n_rows=t, base=HALO3, width=c)

        def blk(rows):
            cv = ca[rows, :]
            mu = jnp.mean(cv, axis=-1, keepdims=True)
            xc = cv - mu
            var = jnp.mean(xc * xc, axis=-1, keepdims=True)
            ln = (xc * lax.rsqrt(var + EPS)) * lg_ref[...] + lb_ref[...]
            y_ref[rows, 0:c] = (ln * _sigmoid(ln)).astype(BF16)
            y_ref[rows, c:2 * c] = (u_ref[rows, 2 * c:3 * c].astype(F32) * cb[rows, :]).astype(BF16)
        _row_loop(t, 64, blk)
        x1_ref[...] = x0_ref[...] + _dot(y_ref[...], wo_ref[...])

    small = lambda r: _resident((r, c), lambda i: (0, 0))
    return pl.pallas_call(
        body, grid=(s_len // t,),
        in_specs=[pl.BlockSpec((t, d_in), lambda i: (i, 0)),
                  pl.BlockSpec((HALO, d_in), lambda i: (jnp.maximum(i * per - 1, 0), 0)),
                  pl.BlockSpec((t, d), lambda i: (i, 0)),
                  small(CONF_K), small(1), small(1), small(1), small(SHORT_K),
                  _resident((2 * c, d), lambda i: (0, 0))],
        out_specs=[pl.BlockSpec((t, 2 * c), lambda i: (i, 0)), pl.BlockSpec((t, d), lambda i: (i, 0)),
                   pl.BlockSpec((t, c), lambda i: (i, 0))],
        out_shape=[jax.ShapeDtypeStruct((s_len, 2 * c), BF16), jax.ShapeDtypeStruct((s_len, d), F32),
                   jax.ShapeDtypeStruct((s_len, c), F32)],
        scratch_shapes=[pltpu.VMEM((HALO + t, c), F32), pltpu.VMEM((HALO3 + t, c), F32), pltpu.VMEM((t, c), F32)],
        compiler_params=_params("arbitrary"), name=name,
    )(u, u, x0, wa, ba, lg, lb, wb, w_out)


def _ffn_windows(ug_ref, ugh_ref, uv_ref, uvh_ref, gwin, vwin, first, t):
    lo = HALO3_BLK - HALO3
    gwin[0:HALO3, :] = jnp.where(first, 0.0, ugh_ref[...].astype(F32)[lo:HALO3_BLK])
    vwin[0:HALO3, :] = jnp.where(first, 0.0, uvh_ref[...].astype(F32)[lo:HALO3_BLK])

    def blk(rows):
        dst = pl.ds(pl.multiple_of(rows.start + HALO3, SUB), rows.size)
        gwin[dst, :] = ug_ref[rows, :].astype(F32)
        vwin[dst, :] = uv_ref[rows, :].astype(F32)
    _row_loop(t, 64, blk)


def _ffn_fwd(uf, x1, wf, w_down, *, name, dep=None):
    s_len, ff2 = uf.shape
    ff = ff2 // 2
    d = x1.shape[1]
    t = _seq_tile(s_len)
    fc = _ff_chunk(ff)
    nc = ff // fc
    per = t // HALO3_BLK

    def body(ug_ref, ugh_ref, uv_ref, uvh_ref, x1_ref, wfg_ref, wfv_ref, wd_ref, dep_ref, act_ref, x2_ref,
             gwin, vwin, cg, cv):
        first = pl.program_id(0) == 0
        _ffn_windows(ug_ref, ugh_ref, uv_ref, uvh_ref, gwin, vwin, first, t)
        _conv_taps(gwin, wfg_ref, cg, taps=SHORT_K, n_rows=t, base=HALO3, width=fc)
        _conv_taps(vwin, wfv_ref, cv, taps=SHORT_K, n_rows=t, base=HALO3, width=fc)

        def blk(rows):
            gv = cg[rows, :]
            act_ref[rows, :] = ((gv * _sigmoid(gv)) * cv[rows, :]).astype(BF16)
        _row_loop(t, 32, blk, unroll=2)

        @pl.when(pl.program_id(1) == 0)
        def _():
            x2_ref[...] = x1_ref[...]
        x2_ref[...] += _dot(act_ref[...], wd_ref[...])

    halo_map = lambda off: (lambda i, j: (jnp.maximum(i * per - 1, 0), j + off))
    return pl.pallas_call(
        body, grid=(s_len // t, nc),
        in_specs=[pl.BlockSpec((t, fc), lambda i, j: (i, j)), pl.BlockSpec((HALO3_BLK, fc), halo_map(0)),
                  pl.BlockSpec((t, fc), lambda i, j: (i, j + nc)), pl.BlockSpec((HALO3_BLK, fc), halo_map(nc)),
                  pl.BlockSpec((t, d), lambda i, j: (i, 0)),
                  pl.BlockSpec((SHORT_K, fc), lambda i, j: (0, j)),
                  pl.BlockSpec((SHORT_K, fc), lambda i, j: (0, j + nc)),
                  pl.BlockSpec((fc, d), lambda i, j: (j, 0)), ANY],
        out_specs=[pl.BlockSpec((t, fc), lambda i, j: (i, j)), pl.BlockSpec((t, d), lambda i, j: (i, 0))],
        out_shape=[jax.ShapeDtypeStruct((s_len, ff), BF16), jax.ShapeDtypeStruct((s_len, d), F32)],
        scratch_shapes=[pltpu.VMEM((HALO3 + t, fc), F32), pltpu.VMEM((HALO3 + t, fc), F32),
                        pltpu.VMEM((t, fc), F32), pltpu.VMEM((t, fc), F32)],
        compiler_params=_params("parallel", "arbitrary"), name=name,
    )(uf, uf, uf, uf, x1, wf, wf, w_down, uf if dep is None else dep)


def _loss_bwd(x, g, target, *, name):
    s_len, d = x.shape
    t = _seq_tile(s_len)

    def body(x_ref, g_ref, t_ref, l_ref, dx_ref, dxb_ref, dg_ref):
        @pl.when(pl.program_id(0) == 0)
        def _():
            l_ref[...] = jnp.zeros_like(l_ref)
            dg_ref[...] = jnp.zeros_like(dg_ref)

        def blk(rows):
            xv = x_ref[rows, :]
            r = lax.rsqrt(jnp.mean(xv * xv, axis=-1, keepdims=True) + EPS)
            xn = xv * r
            e = xn * g_ref[...] - t_ref[rows, :]
            l_ref[...] += _rows8(e * e)
            dy = e * (1.0 / d)
            dg_ref[...] += _rows8(dy * xn)
            dn = dy * g_ref[...]
            dx = r * (dn - xn * jnp.mean(dn * xn, axis=-1, keepdims=True))
            dx_ref[rows, :] = dx
            dxb_ref[rows, :] = dx.astype(BF16)
        _row_loop(t, 64, blk)

    row = pl.BlockSpec((t, d), lambda i: (i, 0))
    part = pl.BlockSpec((SUB, d), lambda i: (0, 0))
    return pl.pallas_call(
        body, grid=(s_len // t,),
        in_specs=[row, _resident((1, d), lambda i: (0, 0)), row],
        out_specs=[part, row, row, part],
        out_shape=[jax.ShapeDtypeStruct((SUB, d), F32), jax.ShapeDtypeStruct((s_len, d), F32),
                   jax.ShapeDtypeStruct((s_len, d), BF16), jax.ShapeDtypeStruct((SUB, d), F32)],
        compiler_params=_params("arbitrary"), name=name,
    )(x, g, target)


def _ffn_bwd(dx2, uf, wf, w_down_t, *, name, dep=None):
    s_len, ff2 = uf.shape
    ff = ff2 // 2
    d = dx2.shape[1]
    t = _seq_tile(s_len)
    n_t = s_len // t
    fc = _ff_chunk(ff)
    nc = ff // fc
    per = t // HALO3_BLK

    def body(dx_ref, ug_ref, ugh_ref, uv_ref, uvh_ref, wfg_ref, wfv_ref, wd_ref, dep_ref,
             dug_ref, duv_ref, dwg_ref, dwv_ref, gwin, vwin, cg, cv, dact, dgw, dvw, awg, awv):
        i = pl.program_id(1)
        first = i == n_t - 1
        _ffn_windows(ug_ref, ugh_ref, uv_ref, uvh_ref, gwin, vwin, first, t)
        _conv_taps(gwin, wfg_ref, cg, taps=SHORT_K, n_rows=t, base=HALO3, width=fc)
        _conv_taps(vwin, wfv_ref, cv, taps=SHORT_K, n_rows=t, base=HALO3, width=fc)
        dact[...] = _dot(dx_ref[...], wd_ref[...])

        @pl.when(i == 0)
        def _():
            dgw[t:t + HALO3, :] = jnp.zeros((HALO3, fc), F32)
            dvw[t:t + HALO3, :] = jnp.zeros((HALO3, fc), F32)
            awg[...] = jnp.zeros_like(awg)
            awv[...] = jnp.zeros_like(awv)

        def blk(rows):
            gv = cg[rows, :]
            sg = _sigmoid(gv)
            da = dact[rows, :]
            dgw[rows, :] = (da * cv[rows, :]) * (sg * (1.0 + gv * (1.0 - sg)))
            dvw[rows, :] = da * (gv * sg)
        _row_loop(t, 32, blk, unroll=2)

        _conv_taps(dgw, wfg_ref, dug_ref, taps=SHORT_K, n_rows=t, base=0, width=fc, transposed=True)
        _conv_taps(dvw, wfv_ref, duv_ref, taps=SHORT_K, n_rows=t, base=0, width=fc, transposed=True)
        _conv_wgrad(dgw, gwin, awg, taps=SHORT_K, n_rows=t, base=HALO3, width=fc)
        _conv_wgrad(dvw, vwin, awv, taps=SHORT_K, n_rows=t, base=HALO3, width=fc)
        dgw[t:t + HALO3, :] = dgw[0:HALO3, :]
        dvw[t:t + HALO3, :] = dvw[0:HALO3, :]

        @pl.when(i == n_t - 1)
        def _():
            dwg_ref[...] = _fold8(awg, SHORT_K)
            dwv_ref[...] = _fold8(awv, SHORT_K)

    rev = lambda i: n_t - 1 - i
    halo_map = lambda off: (lambda j, i: (jnp.maximum(rev(i) * per - 1, 0), j + off))
    return pl.pallas_call(
        body, grid=(nc, n_t),
        in_specs=[pl.BlockSpec((t, d), lambda j, i: (rev(i), 0)),
                  pl.BlockSpec((t, fc), lambda j, i: (rev(i), j)), pl.BlockSpec((HALO3_BLK, fc), halo_map(0)),
                  pl.BlockSpec((t, fc), lambda j, i: (rev(i), j + nc)), pl.BlockSpec((HALO3_BLK, fc), halo_map(nc)),
                  pl.BlockSpec((SHORT_K, fc), lambda j, i: (0, j)),
                  pl.BlockSpec((SHORT_K, fc), lambda j, i: (0, j + nc)),
                  pl.BlockSpec((d, fc), lambda j, i: (0, j)), ANY],
        out_specs=[pl.BlockSpec((t, fc), lambda j, i: (rev(i), j)), pl.BlockSpec((t, fc), lambda j, i: (rev(i), j)),
                   pl.BlockSpec((SHORT_K, fc), lambda j, i: (0, j)), pl.BlockSpec((SHORT_K, fc), lambda j, i: (0, j))],
        out_shape=[jax.ShapeDtypeStruct((s_len, ff), BF16), jax.ShapeDtypeStruct((s_len, ff), BF16),
                   jax.ShapeDtypeStruct((SHORT_K, ff), F32), jax.ShapeDtypeStruct((SHORT_K, ff), F32)],
        scratch_shapes=[pltpu.VMEM((HALO3 + t, fc), F32), pltpu.VMEM((HALO3 + t, fc), F32),
                        pltpu.VMEM((t, fc), F32), pltpu.VMEM((t, fc), F32), pltpu.VMEM((t, fc), F32),
                        pltpu.VMEM((t + HALO3, fc), F32), pltpu.VMEM((t + HALO3, fc), F32),
                        pltpu.VMEM((SHORT_K * SUB, fc), F32), pltpu.VMEM((SHORT_K * SUB, fc), F32)],
        compiler_params=_params("arbitrary", "arbitrary"), name=name,
    )(dx2, uf, uf, uf, uf, wf, wf, w_down_t, uf if dep is None else dep)


def _mix_bwd(dx1, u, ca, wa, lg, lb, wb, w_out_t, *, name):
    s_len, d_in = u.shape
    d = dx1.shape[1]
    c = D_CONF
    t = _seq_tile(s_len)
    n_t = s_len // t
    per = t // HALO

    def body(dx_ref, u_ref, uh_ref, ca_ref, wa_ref, lg_ref, lb_ref, wb_ref, wo_ref,
             du_ref, dwa_ref, dwb_ref, dba_ref, dlg_ref, dlb_ref, dbin_ref,
             gw, pw, cb, dyc, dcaw, dcbw, dglu, dp, awa, awb):
        i = pl.program_id(0)
        first = i == n_t - 1
        _mix_windows(u_ref, uh_ref, gw, pw, first, t)
        _conv_taps(pw, wb_ref, cb, taps=SHORT_K, n_rows=t, base=HALO3, width=c)
        dyc[...] = _dot(dx_ref[...], wo_ref[...])

        @pl.when(i == 0)
        def _():
            dcaw[t:t + HALO, :] = jnp.zeros((HALO, c), F32)
            dcbw[t:t + HALO3, :] = jnp.zeros((HALO3, c), F32)
            awa[...] = jnp.zeros_like(awa)
            awb[...] = jnp.zeros_like(awb)
            dba_ref[...] = jnp.zeros_like(dba_ref)
            dlg_ref[...] = jnp.zeros_like(dlg_ref)
            dlb_ref[...] = jnp.zeros_like(dlb_ref)
            dbin_ref[...] = jnp.zeros_like(dbin_ref)

        def blk1(rows):
            cv = ca_ref[rows, :]
            mu = jnp.mean(cv, axis=-1, keepdims=True)
            xc = cv - mu
            rstd = lax.rsqrt(jnp.mean(xc * xc, axis=-1, keepdims=True) + EPS)
            nrm = xc * rstd
            ln = nrm * lg_ref[...] + lb_ref[...]
            sg = _sigmoid(ln)
            dln = dyc[rows, 0:c] * (sg * (1.0 + ln * (1.0 - sg)))
            dlg_ref[...] += _rows8(dln * nrm)
            dlb_ref[...] += _rows8(dln)
            dn = dln * lg_ref[...]
            dca = rstd * (dn - jnp.mean(dn, axis=-1, keepdims=True)
                          - nrm * jnp.mean(dn * nrm, axis=-1, keepdims=True))
            dcaw[rows, :] = dca
            dba_ref[...] += _rows8(dca)
            ds = dyc[rows, c:2 * c]
            dgb = ds * cb[rows, :]
            dcbw[rows, :] = ds * u_ref[rows, 2 * c:3 * c].astype(F32)
            du_ref[rows, 2 * c:3 * c] = dgb.astype(BF16)
            dbin_ref[:, 2 * c:3 * c] += _rows8(dgb)
        _row_loop(t, 64, blk1, unroll=2)

        _conv_taps(dcaw, wa_ref, dglu, taps=CONF_K, n_rows=t, base=0, width=c, transposed=True)
        _conv_taps(dcbw, wb_ref, dp, taps=SHORT_K, n_rows=t, base=0, width=c, transposed=True)
        _conv_wgrad(dcaw, gw, awa, taps=CONF_K, n_rows=t, base=HALO, width=c)
        _conv_wgrad(dcbw, pw, awb, taps=SHORT_K, n_rows=t, base=HALO3, width=c)
        dcaw[t:t + HALO, :] = dcaw[0:HALO, :]
        dcbw[t:t + HALO3, :] = dcbw[0:HALO3, :]

        def blk2(rows):
            av = u_ref[rows, 0:c].astype(F32)
            sg = _sigmoid(u_ref[rows, c:2 * c].astype(F32))
            dg = dglu[rows, :]
            d_av = dg * sg
            d_ag = (dg * av) * (sg * (1.0 - sg))
            dpv = dp[rows, :]
            d_gc = dpv * u_ref[rows, 4 * c:5 * c].astype(F32)
            d_vs = dpv * u_ref[rows, 3 * c:4 * c].astype(F32)
            du_ref[rows, 0:c] = d_av.astype(BF16)
            du_ref[rows, c:2 * c] = d_ag.astype(BF16)
            du_ref[rows, 3 * c:4 * c] = d_gc.astype(BF16)
            du_ref[rows, 4 * c:5 * c] = d_vs.astype(BF16)
            dbin_ref[:, 0:c] += _rows8(d_av)
            dbin_ref[:, c:2 * c] += _rows8(d_ag)
            dbin_ref[:, 3 * c:4 * c] += _rows8(d_gc)
            dbin_ref[:, 4 * c:5 * c] += _rows8(d_vs)
        _row_loop(t, 64, blk2)

        @pl.when(i == n_t - 1)
        def _():
            dwa_ref[...] = _fold8(awa, CONF_K)
            dwb_ref[...] = _fold8(awb, SHORT_K)

    rev = lambda i: n_t - 1 - i
    small_in = lambda r: _resident((r, c), lambda i: (0, 0))
    small = lambda r: pl.BlockSpec((r, c), lambda i: (0, 0))
    return pl.pallas_call(
        body, grid=(n_t,),
        in_specs=[pl.BlockSpec((t, d), lambda i: (rev(i), 0)),
                  pl.BlockSpec((t, d_in), lambda i: (rev(i), 0)),
                  pl.BlockSpec((HALO, d_in), lambda i: (jnp.maximum(rev(i) * per - 1, 0), 0)),
                  pl.BlockSpec((t, c), lambda i: (rev(i), 0)),
                  small_in(CONF_K), small_in(1), small_in(1), small_in(SHORT_K),
                  _resident((d, 2 * c), lambda i: (0, 0))],
        out_specs=[pl.BlockSpec((t, d_in), lambda i: (rev(i), 0)),
                   small(CONF_K), small(SHORT_K), small(SUB), small(SUB), small(SUB),
                   pl.BlockSpec((SUB, d_in), lambda i: (0, 0))],
        out_shape=[jax.ShapeDtypeStruct((s_len, d_in), BF16),
                   jax.ShapeDtypeStruct((CONF_K, c), F32), jax.ShapeDtypeStruct((SHORT_K, c), F32),
                   jax.ShapeDtypeStruct((SUB, c), F32), jax.ShapeDtypeStruct((SUB, c), F32),
                   jax.ShapeDtypeStruct((SUB, c), F32), jax.ShapeDtypeStruct((SUB, d_in), F32)],
        scratch_shapes=[pltpu.VMEM((HALO + t, c), F32), pltpu.VMEM((HALO3 + t, c), F32),
                        pltpu.VMEM((t, c), F32), pltpu.VMEM((t, 2 * c), F32),
                        pltpu.VMEM((t + HALO, c), F32), pltpu.VMEM((t + HALO3, c), F32),
                        pltpu.VMEM((t, c), F32), pltpu.VMEM((t, c), F32),
                        pltpu.VMEM((CONF_K * SUB, c), F32), pltpu.VMEM((SHORT_K * SUB, c), F32)],
        compiler_params=_params("arbitrary"), name=name,
    )(dx1, u, u, ca, wa, lg, lb, wb, w_out_t)


def _matmul_tn(a, b, *, name):
    s_len, k = a.shape
    n = b.shape[1]
    if k <= n:
        tn = _col_tile(n)
        grid = (n // tn,)
        in_specs = [_resident((s_len, k), lambda j: (0, 0)), pl.BlockSpec((s_len, tn), lambda j: (0, j))]
        out_spec = pl.BlockSpec((k, tn), lambda j: (0, j))
    else:
        tk = _col_tile(k)
        grid = (k // tk,)
        in_specs = [pl.BlockSpec((s_len, tk), lambda j: (0, j)), _resident((s_len, n), lambda j: (0, 0))]
        out_spec = pl.BlockSpec((tk, n), lambda j: (j, 0))

    def body(a_ref, b_ref, o_ref):
        o_ref[...] = _dot_tn(a_ref[...], b_ref[...]).astype(BF16)

    return pl.pallas_call(
        body, grid=grid, in_specs=in_specs, out_specs=out_spec,
        out_shape=jax.ShapeDtypeStruct((k, n), BF16),
        compiler_params=_params("parallel"), name=name,
    )(a, b)


def _matmul_rmsbwd(dzs, wt, x, g, dx_in, *, name):
    s_len, d = x.shape
    n_z = len(dzs)
    nj = dzs[0].shape[1]
    t = _mm_tile(s_len)

    def body(*refs):
        dz_refs = refs[0:n_z]
        w_refs = refs[n_z:2 * n_z]
        x_ref, g_ref, dxi_ref, dx_ref, dxb_ref, dg_ref, dh = refs[2 * n_z:]

        @pl.when(pl.program_id(0) == 0)
        def _():
            dg_ref[...] = jnp.zeros_like(dg_ref)

        acc = _dot(dz_refs[0][...], w_refs[0][...])
        for q in range(1, n_z):
            acc = acc + _dot(dz_refs[q][...], w_refs[q][...])
        dh[...] = acc

        def blk(rows):
            xv = x_ref[rows, :]
            r = lax.rsqrt(jnp.mean(xv * xv, axis=-1, keepdims=True) + EPS)
            xn = xv * r
            dhv = dh[rows, :]
            dg_ref[...] += _rows8(dhv * xn)
            dn = dhv * g_ref[...]
            dx = dxi_ref[rows, :] + r * (dn - xn * jnp.mean(dn * xn, axis=-1, keepdims=True))
            dx_ref[rows, :] = dx
            dxb_ref[rows, :] = dx.astype(BF16)
        _row_loop(t, 128, blk)

    row = pl.BlockSpec((t, d), lambda i: (i, 0))
    in_specs = [pl.BlockSpec((t, nj), lambda i: (i, 0)) for _ in range(n_z)]
    in_specs += [_resident((nj, d), functools.partial(lambda q, i: (q, 0), q)) for q in range(n_z)]
    in_specs += [row, _resident((1, d), lambda i: (0, 0)), row]
    return pl.pallas_call(
        body, grid=(s_len // t,), in_specs=in_specs,
        out_specs=[row, row, pl.BlockSpec((SUB, d), lambda i: (0, 0))],
        out_shape=[jax.ShapeDtypeStruct((s_len, d), F32), jax.ShapeDtypeStruct((s_len, d), BF16),
                   jax.ShapeDtypeStruct((SUB, d), F32)],
        scratch_shapes=[pltpu.VMEM((t, d), F32)],
        compiler_params=_params("arbitrary"), name=name,
    )(*dzs, *([wt] * n_z), x, g, dx_in)


def _row(v):
    return v.reshape(1, -1)


def _layer_fwd(x0, p, tag, dep=None, before_ffn=None):
    u, h1 = _rms_matmul(x0, _row(p["mix_norm_g"]), p["w_in"], _row(p["b_in"]), name=f"in_proj_{tag}", dep=dep)
    ycat, x1, ca = _mix_fwd(u, x0, p["conv_a_w"], _row(p["conv_a_b"]), _row(p["ln_a_g"]), _row(p["ln_a_b"]),
                            p["conv_b_w"], p["w_out"], name=f"mix_fwd_{tag}")
    uf, h2 = _rms_matmul(x1, _row(p["ffn_norm_g"]), p["w_up"], None, name=f"up_proj_{tag}")
    dep_ffn = None if before_ffn is None else before_ffn(uf)
    act, x2 = _ffn_fwd(uf, x1, p["conv_f_w"], p["w_down"], name=f"ffn_fwd_{tag}", dep=dep_ffn)
    return x2, dict(x0=x0, h1=h1, u=u, ca=ca, ycat=ycat, x1=x1, h2=h2, uf=uf, act=act)


def _layer_bwd(dx2, dx2_b, p, saved, tag, dep=None):
    dug, duv, dwf_g, dwf_v = _ffn_bwd(dx2_b, saved["uf"], p["conv_f_w"], p["w_down_t"], name=f"ffn_bwd_{tag}",
                                      dep=dep)
    g_down = _matmul_tn(saved["act"], dx2_b, name=f"dw_down_{tag}")
    g_up = [_matmul_tn(saved["h2"], dug, name=f"dw_up_g_{tag}"), _matmul_tn(saved["h2"], duv, name=f"dw_up_v_{tag}")]
    dx1, dx1_b, dg2 = _matmul_rmsbwd([dug, duv], p["w_up_t"], saved["x1"], _row(p["ffn_norm_g"]), dx2,
                                     name=f"dh_ffn_{tag}")
    du, dwa, dwb, dba, dlg, dlb, dbin = _mix_bwd(
        dx1_b, saved["u"], saved["ca"], p["conv_a_w"], _row(p["ln_a_g"]), _row(p["ln_a_b"]),
        p["conv_b_w"], p["w_out_t"], name=f"mix_bwd_{tag}")
    g_out = _matmul_tn(saved["ycat"], dx1_b, name=f"dw_out_{tag}")
    g_in = _matmul_tn(saved["h1"], du, name=f"dw_in_{tag}")
    dx0, dx0_b, dg1 = _matmul_rmsbwd([du], p["w_in_t"], saved["x0"], _row(p["mix_norm_g"]), dx1,
                                     name=f"dh_mix_{tag}")
    big = dict(w_in=g_in, w_out=g_out, w_up=g_up, w_down=g_down)
    conv = dict(conv_a_w=dwa, conv_b_w=dwb, conv_f_w=jnp.concatenate([dwf_g, dwf_v], axis=1))
    rep = dict(mix_norm_g=dg1, b_in=dbin, conv_a_b=dba, ln_a_g=dlg, ln_a_b=dlb, ffn_norm_g=dg2)
    return dx0, dx0_b, big, conv, rep


def _place():
    return lax.axis_index("x"), lax.axis_index("y"), lax.axis_index("c")


def _all_gather(arrs, *, name):
    n_a = len(arrs)

    def body(*refs):
        ins = refs[0:n_a]
        outs = refs[n_a:2 * n_a]
        send_sems, recv_sems, local_sems = refs[2 * n_a:]
        x, y, c = _place()
        sibling = (x, y, 1 - c)
        chips = [(1 - x, y), (x, 1 - y), (1 - x, 1 - y)]

        def slot(a, px, py, pc):
            return outs[a].at[4 * px + 2 * py + pc]

        def copy(a, k, block, to, src=None):
            return pltpu.make_async_remote_copy(
                src_ref=slot(a, *block) if src is None else src, dst_ref=slot(a, *block),
                send_sem=send_sems.at[a, k], recv_sem=recv_sems.at[a, k],
                device_id=to, device_id_type=MESH)

        me = (x, y, c)
        mine = [pltpu.make_async_copy(ins[a], slot(a, *me), local_sems.at[a]) for a in range(n_a)]
        for cp in mine:
            cp.start()
        started = []
        for a in range(n_a):
            first = [copy(a, 0, me, sibling, src=ins[a])]
            first += [copy(a, 1 + j, me, (*chip, c), src=ins[a]) for j, chip in enumerate(chips)]
            for cp in first:
                cp.start()
            started += first
        for a in range(n_a):
            for j, chip in enumerate(chips):
                copy(a, 1 + j, (*chip, c), me).wait_recv()
                passed = copy(a, 4 + j, (*chip, c), sibling)
                passed.start()
                started.append(passed)
        for a in range(n_a):
            copy(a, 0, sibling, me).wait_recv()
            for j, chip in enumerate(chips):
                copy(a, 4 + j, (*chip, 1 - c), me).wait_recv()
        for cp in started:
            cp.wait_send()
        for cp in mine:
            cp.wait()

    return pl.pallas_call(
        body, in_specs=[ANY] * n_a, out_specs=[ANY] * n_a,
        out_shape=[jax.ShapeDtypeStruct((N_DEV, *a.shape), a.dtype) for a in arrs],
        scratch_shapes=[pltpu.SemaphoreType.DMA((n_a, 7)), pltpu.SemaphoreType.DMA((n_a, 7)),
                        pltpu.SemaphoreType.DMA((n_a,))],
        name=name,
    )(*arrs)


def _sibling_exchange(arrs, *, name):
    n_a = len(arrs)

    def body(*refs):
        ins = refs[0:n_a]
        outs = refs[n_a:2 * n_a]
        send_sems, recv_sems = refs[2 * n_a:]
        x, y, c = _place()
        copies = [pltpu.make_async_remote_copy(
            src_ref=ins[a].at[:, 1 - c], dst_ref=outs[a], send_sem=send_sems.at[a], recv_sem=recv_sems.at[a],
            device_id=(x, y, 1 - c), device_id_type=MESH) for a in range(n_a)]
        for cp in copies:
            cp.start()
        for cp in copies:
            cp.wait()

    return pl.pallas_call(
        body, in_specs=[ANY] * n_a, out_specs=[ANY] * n_a,
        out_shape=[jax.ShapeDtypeStruct((N_CHIP, *a.shape[2:]), a.dtype) for a in arrs],
        scratch_shapes=[pltpu.SemaphoreType.DMA((n_a,)), pltpu.SemaphoreType.DMA((n_a,))],
        name=name,
    )(*arrs)


def _row_tile(r):
    for tr in (512, 352, 256, 128, 64, 32, 16, 8):
        if r % tr == 0:
            return tr
    return r


def _pair_sum(mine, theirs, core, *, name):
    n_chip, _, r, c = mine.shape
    tr = _row_tile(r)

    def body(core_ref, a_ref, b_ref, o_ref):
        o_ref[...] = (a_ref[...].astype(F32) + b_ref[...].astype(F32)).astype(o_ref.dtype)

    return pl.pallas_call(
        body,
        grid_spec=pltpu.PrefetchScalarGridSpec(
            num_scalar_prefetch=1, grid=(n_chip, r // tr),
            in_specs=[pl.BlockSpec((None, None, tr, c), lambda q, i, core_ref: (q, core_ref[0], i, 0)),
                      pl.BlockSpec((None, tr, c), lambda q, i, core_ref: (q, i, 0))],
            out_specs=pl.BlockSpec((None, tr, c), lambda q, i, core_ref: (q, i, 0))),
        out_shape=jax.ShapeDtypeStruct((n_chip, r, c), mine.dtype),
        compiler_params=_params("parallel", "parallel"), name=name,
    )(core, mine, theirs)


def _chip_exchange(arrs, *, name):
    n_a = len(arrs)

    def body(*refs):
        ins = refs[0:n_a]
        outs = refs[n_a:2 * n_a]
        send_sems, recv_sems, local_sems = refs[2 * n_a:]
        x, y, c = _place()
        my_chip = 2 * x + y
        chips = [(1 - x, y), (x, 1 - y), (1 - x, 1 - y)]
        mine = [pltpu.make_async_copy(ins[a].at[my_chip], outs[a].at[my_chip], local_sems.at[a]) for a in range(n_a)]
        for cp in mine:
            cp.start()
        copies = []
        for a in range(n_a):
            for j, (px, py) in enumerate(chips):
                copies.append(pltpu.make_async_remote_copy(
                    src_ref=ins[a].at[2 * px + py], dst_ref=outs[a].at[my_chip],
                    send_sem=send_sems.at[a, j], recv_sem=recv_sems.at[a, j],
                    device_id=(px, py, c), device_id_type=MESH))
        for cp in copies:
            cp.start()
        for cp in copies:
            cp.wait()
        for cp in mine:
            cp.wait()

    return pl.pallas_call(
        body, in_specs=[ANY] * n_a, out_specs=[ANY] * n_a,
        out_shape=[jax.ShapeDtypeStruct(a.shape, a.dtype) for a in arrs],
        scratch_shapes=[pltpu.SemaphoreType.DMA((n_a, 3)), pltpu.SemaphoreType.DMA((n_a, 3)),
                        pltpu.SemaphoreType.DMA((n_a,))],
        name=name,
    )(*arrs)


HBM = pl.BlockSpec(memory_space=pltpu.HBM)
SEM = pl.BlockSpec(memory_space=pltpu.SEMAPHORE)
EFFECT = pltpu.SideEffectType.DATAFLOW_SIDE_EFFECTING


def _in_hbm(a):
    return pltpu.with_memory_space_constraint(a, pltpu.HBM)


def _split_start(srcs, lands, plan, n_copies, *, name):
    n_s, n_l = len(srcs), len(lands)

    def body(*refs):
        src_refs = refs[0:n_s]
        land_refs = refs[n_s:n_s + n_l]
        send_sems, recv_sems = refs[n_s + n_l], refs[n_s + n_l + 1]
        token = refs[-1]
        for cp in plan(src_refs, land_refs, send_sems, recv_sems):
            cp.start()
        token[...] = jnp.zeros_like(token)

    thru = [pltpu.HBM(a.shape, a.dtype) for a in list(srcs) + list(lands)]
    res = pl.pallas_call(
        body, name=name,
        out_shape=(pltpu.SemaphoreType.DMA((n_copies,)), pltpu.SemaphoreType.DMA((n_copies,)), *thru,
                   jax.ShapeDtypeStruct((SUB, LANES), F32)),
        in_specs=[HBM] * (n_s + n_l),
        out_specs=(SEM, SEM, *([HBM] * (n_s + n_l)), pl.BlockSpec(memory_space=pltpu.VMEM)),
        input_output_aliases={i: 2 + i for i in range(n_s + n_l)},
        compiler_params=pltpu.CompilerParams(has_side_effects=EFFECT),
    )(*[_in_hbm(a) for a in srcs], *[_in_hbm(a) for a in lands])
    return res[0], res[1], list(res[2:2 + n_s]), list(res[2 + n_s:2 + n_s + n_l]), res[-1]


def _split_wait(send_sems, recv_sems, srcs, lands, after, plan, *, name):
    n_s, n_l = len(srcs), len(lands)

    def body(*refs):
        src_refs = refs[0:n_s]
        land_refs = refs[n_s:n_s + n_l]
        send, recv = refs[n_s + n_l], refs[n_s + n_l + 1]
        for cp in plan(src_refs, land_refs, send, recv):
            cp.wait_send()
            cp.wait_recv()

    res = pl.pallas_call(
        body, name=name,
        out_shape=tuple(pltpu.HBM(a.shape, a.dtype) for a in list(srcs) + list(lands)),
        in_specs=[HBM] * (n_s + n_l) + [SEM, SEM, ANY],
        out_specs=tuple([HBM] * (n_s + n_l)),
        input_output_aliases={i: i for i in range(n_s + n_l)},
        compiler_params=pltpu.CompilerParams(has_side_effects=EFFECT),
    )(*srcs, *lands, send_sems, recv_sems, after)
    return list(res[n_s:])


def _remote(src, dst, send_sems, recv_sems, k, to):
    return pltpu.make_async_remote_copy(src_ref=src, dst_ref=dst, send_sem=send_sems.at[k], recv_sem=recv_sems.at[k],
                                        device_id=to, device_id_type=MESH)


def _gather_plan_first(src_refs, land_refs, send_sems, recv_sems):
    x, y, c = _place()
    me = 4 * x + 2 * y + c
    peers = [(x, y, 1 - c), (1 - x, y, c), (x, 1 - y, c), (1 - x, 1 - y, c)]
    return [_remote(src, land.at[me], send_sems, recv_sems, 4 * a + k, to)
            for a, (src, land) in enumerate(zip(src_refs, land_refs)) for k, to in enumerate(peers)]


def _gather_plan_second(src_refs, land_refs, send_sems, recv_sems):
    x, y, c = _place()
    chips = [(1 - x, y), (x, 1 - y), (1 - x, 1 - y)]
    out = []
    for a, land in enumerate(land_refs):
        for j, (px, py) in enumerate(chips):
            slot = land.at[4 * px + 2 * py + c]
            out.append(_remote(slot, slot, send_sems, recv_sems, 3 * a + j, (x, y, 1 - c)))
    return out


def _chips_plan(src_refs, land_refs, send_sems, recv_sems):
    x, y, c = _place()
    my_chip = 2 * x + y
    chips = [(1 - x, y), (x, 1 - y), (1 - x, 1 - y)]
    return [_remote(src.at[2 * px + py], land.at[my_chip], send_sems, recv_sems, 3 * a + j, (px, py, c))
            for a, (src, land) in enumerate(zip(src_refs, land_refs)) for j, (px, py) in enumerate(chips)]


def _landing(like_shape, dtype, own, index):
    return lax.dynamic_update_index_in_dim(lax.empty(like_shape, dtype), own, index, 0)


def _adamw_math(g, w, m, v):
    m = ADAM_B1 * m + (1.0 - ADAM_B1) * g
    v = ADAM_B2 * v + (1.0 - ADAM_B2) * (g * g)
    m_hat = m / (1.0 - ADAM_B1 ** ADAM_STEP)
    v_hat = v / (1.0 - ADAM_B2 ** ADAM_STEP)
    delta = -ADAM_LR * (m_hat / (jnp.sqrt(v_hat) + ADAM_EPS) + ADAM_WD * w)
    return delta, m, v


def _adamw_sharded(parts, w, m, v, *, name):
    n_layers, r, c = w.shape
    n_chip = parts[0].shape[0]
    tr = _row_tile(r)
    n_i = r // tr

    def body(*refs):
        p_refs = refs[0:n_layers]
        w_ref, m_ref, v_ref, g_out, d_out, m_out, v_out = refs[n_layers:]
        layer = pl.program_id(0)
        for l in range(n_layers):
            @pl.when(layer == l)
            def _(l=l):
                g = p_refs[l][0].astype(F32)
                for q in range(1, n_chip):
                    g = g + p_refs[l][q].astype(F32)
                delta, m_new, v_new = _adamw_math(g, w_ref[...], m_ref[...], v_ref[...])
                g_out[...] = g
                d_out[...] = delta
                m_out[...] = m_new
                v_out[...] = v_new

    def part_map(l):
        return lambda layer, i: (0, jnp.where(layer == l, i, jnp.where(layer < l, 0, n_i - 1)), 0)

    blk = pl.BlockSpec((None, tr, c), lambda layer, i: (layer, i, 0))
    return pl.pallas_call(
        body, grid=(n_layers, n_i),
        in_specs=[pl.BlockSpec((n_chip, tr, c), part_map(l)) for l in range(n_layers)] + [blk, blk, blk],
        out_specs=[blk] * 4, out_shape=[jax.ShapeDtypeStruct((n_layers, r, c), F32)] * 4,
        compiler_params=_params("arbitrary", "arbitrary"), name=name,
    )(*parts, w, m, v)


def _adamw_replicated(parts, names, w, m, v, n_loss, *, name):
    n_dev = parts.shape[0]
    n_layers = w[names[0]].shape[0]
    every = list(names) + ["final_norm_g"]
    n_p = len(every)

    def body(*refs):
        p_ref = refs[0]
        w_refs = dict(zip(every, refs[1:1 + n_p]))
        m_refs = dict(zip(every, refs[1 + n_p:1 + 2 * n_p]))
        v_refs = dict(zip(every, refs[1 + 2 * n_p:1 + 3 * n_p]))
        l_out = refs[1 + 3 * n_p]
        outs = refs[2 + 3 * n_p:]
        o_refs = {n: outs[4 * q:4 * q + 4] for q, n in enumerate(every)}
        acc = p_ref[0]
        for q in range(1, n_dev):
            acc = acc + p_ref[q]
        tot = jnp.sum(acc, axis=0, keepdims=True)
        pos = 0
        where = [(n, l) for l in range(n_layers) for n in names] + [("final_norm_g", 0)]
        for n, l in where:
            width = w_refs[n].shape[1]
            g = tot[:, pos:pos + width]
            pos += width
            row = pl.ds(l, 1)
            delta, m_new, v_new = _adamw_math(g, w_refs[n][row, :], m_refs[n][row, :], v_refs[n][row, :])
            for o, val in zip(o_refs[n], (g, delta, m_new, v_new)):
                o[row, :] = val
        l_out[...] = (0.5 / n_loss) * jnp.sum(tot[:, pos:pos + n_loss], axis=-1, keepdims=True)

    shapes = [jax.ShapeDtypeStruct((1, 1), F32)]
    for n in every:
        shapes += [jax.ShapeDtypeStruct(w[n].shape, F32)] * 4
    res = pl.pallas_call(
        body, out_shape=shapes,
        compiler_params=pltpu.CompilerParams(vmem_limit_bytes=VMEM_LIMIT), name=name,
    )(parts, *[w[n] for n in every], *[m[n] for n in every], *[v[n] for n in every])
    return res[0], {n: res[1 + 4 * q:5 + 4 * q] for q, n in enumerate(every)}


BIG = ("w_in", "w_out", "w_up", "w_down")
COL_SHARDED = ("w_in", "w_up")
CONV = ("conv_a_w", "conv_b_w", "conv_f_w")
REPLICATED = ("mix_norm_g", "b_in", "conv_a_b", "ln_a_g", "ln_a_b", "ffn_norm_g")
KINDS = ("grad", "delta", "m", "v")


def _weights_from_gathered(name, g):
    n_dev, r, c = g.shape
    if name in COL_SHARDED:
        return g.transpose(1, 0, 2).reshape(r, n_dev * c), g.transpose(0, 2, 1).reshape(n_dev * c, r)
    full = g.reshape(n_dev * r, c)
    return full, full.T


def _slabs_from_full(name, grad):
    if name == "w_up":
        halves = [h.reshape(h.shape[0], N_DEV // 2, -1).transpose(1, 0, 2) for h in grad]
        return jnp.concatenate(halves, axis=0)
    r, c = grad.shape
    if name in COL_SHARDED:
        return grad.reshape(r, N_DEV, c // N_DEV).transpose(1, 0, 2)
    return grad.reshape(N_DEV, r // N_DEV, c)


def _pair_sums(slabs, core, tag):
    slabs = [s.reshape(N_CHIP, 2, *s.shape[1:]) for s in slabs]
    theirs = _sibling_exchange(slabs, name=f"reduce_siblings_{tag}")
    return [_pair_sum(a, b, core, name=f"pair_sum_{tag}_{q}") for q, (a, b) in enumerate(zip(slabs, theirs))]


def kernel(x, mix_norm_g, w_in, b_in, conv_a_w, conv_a_b, ln_a_g, ln_a_b, conv_b_w, w_out, ffn_norm_g, w_up, conv_f_w, w_down, final_norm_g, loss_target, m_mix_norm_g, m_w_in, m_b_in, m_conv_a_w, m_conv_a_b, m_ln_a_g, m_ln_a_b, m_conv_b_w, m_w_out, m_ffn_norm_g, m_w_up, m_conv_f_w, m_w_down, m_final_norm_g, v_mix_norm_g, v_w_in, v_b_in, v_conv_a_w, v_conv_a_b, v_ln_a_g, v_ln_a_b, v_conv_b_w, v_w_out, v_ffn_norm_g, v_w_up, v_conv_f_w, v_w_down, v_final_norm_g):
    w = dict(mix_norm_g=mix_norm_g, w_in=w_in, b_in=b_in, conv_a_w=conv_a_w, conv_a_b=conv_a_b, ln_a_g=ln_a_g,
             ln_a_b=ln_a_b, conv_b_w=conv_b_w, w_out=w_out, ffn_norm_g=ffn_norm_g, w_up=w_up, conv_f_w=conv_f_w,
             w_down=w_down, final_norm_g=final_norm_g)
    m = dict(mix_norm_g=m_mix_norm_g, w_in=m_w_in, b_in=m_b_in, conv_a_w=m_conv_a_w, conv_a_b=m_conv_a_b,
             ln_a_g=m_ln_a_g, ln_a_b=m_ln_a_b, conv_b_w=m_conv_b_w, w_out=m_w_out, ffn_norm_g=m_ffn_norm_g,
             w_up=m_w_up, conv_f_w=m_conv_f_w, w_down=m_w_down, final_norm_g=m_final_norm_g)
    v = dict(mix_norm_g=v_mix_norm_g, w_in=v_w_in, b_in=v_b_in, conv_a_w=v_conv_a_w, conv_a_b=v_conv_a_b,
             ln_a_g=v_ln_a_g, ln_a_b=v_ln_a_b, conv_b_w=v_conv_b_w, w_out=v_w_out, ffn_norm_g=v_ffn_norm_g,
             w_up=v_w_up, conv_f_w=v_conv_f_w, w_down=v_w_down, final_norm_g=v_final_norm_g)
    order = list(w)
    n_layers = w_in.shape[0]
    n_big = len(BIG)
    xs = x[0]
    target = loss_target[0]
    px, py, pc = _place()
    core = pc.astype(jnp.int32).reshape(1)
    me = 4 * px + 2 * py + pc
    my_chip = 2 * px + py

    gathered = _all_gather([w[n][0].astype(BF16) for n in BIG] + [w[n] for n in CONV], name="gather_weights_0")
    conv_full = {}
    for n, g in zip(CONV, gathered[n_big:]):
        n_dev, _, taps, c = g.shape
        conv_full[n] = g.transpose(1, 2, 0, 3).reshape(n_layers, taps, n_dev * c)

    def layer_params(l, big):
        p = {n: conv_full[n][l] for n in CONV}
        for n, g in zip(BIG, big):
            p[n], p[n + "_t"] = _weights_from_gathered(n, g)
        p.update({n: w[n][l] for n in REPLICATED})
        return p

    params = [layer_params(0, gathered[:n_big])]
    h = xs
    saved = []
    for l in range(n_layers):
        if l + 1 == n_layers:
            h, keep = _layer_fwd(h, params[l], str(l))
            saved.append(keep)
            break
        shards = [w[n][l + 1].astype(BF16) for n in BIG]
        lands = [_landing((N_DEV, *s.shape), s.dtype, s, me) for s in shards]
        first = _split_start(shards, lands, _gather_plan_first, 4 * n_big, name=f"gather_first_start_{l + 1}")
        second = []

        def before_ffn(uf, first=first, second=second, tag=l + 1):
            arrived = _split_wait(first[0], first[1], first[2], first[3], uf, _gather_plan_first,
                                  name=f"gather_first_wait_{tag}")
            second.extend(_split_start([], arrived, _gather_plan_second, 3 * n_big, name=f"gather_second_start_{tag}"))
            return second[4]

        h, keep = _layer_fwd(h, params[l], str(l), dep=first[4], before_ffn=before_ffn)
        saved.append(keep)
        big = _split_wait(second[0], second[1], [], second[3], h, _gather_plan_second, name=f"gather_second_wait_{l + 1}")
        params.append(layer_params(l + 1, big))

    loss_sq, dh, dh_b, dgf = _loss_bwd(h, _row(final_norm_g), target, name="loss")
    conv_g = {n: [None] * n_layers for n in CONV}
    rep_g = [None] * n_layers
    parts = [None] * n_layers
    flying = None
    for l in reversed(range(n_layers)):
        dep = None if flying is None else flying[4]
        dh, dh_b, big_g, conv, rep_g[l] = _layer_bwd(dh, dh_b, params[l], saved[l], str(l), dep=dep)
        for n in CONV:
            conv_g[n][l] = conv[n]
        if flying is not None:
            parts[l + 1] = _split_wait(flying[0], flying[1], flying[2], flying[3], dh, _chips_plan,
                                       name=f"reduce_chips_wait_{l + 1}")
        slabs = [_slabs_from_full(n, big_g[n]) for n in BIG]
        if l > 0:
            pairs = _pair_sums(slabs, core, str(l))
            lands = [_landing(p.shape, p.dtype, lax.dynamic_index_in_dim(p, my_chip, 0, keepdims=False), my_chip)
                     for p in pairs]
            flying = _split_start(pairs, lands, _chips_plan, 3 * n_big, name=f"reduce_chips_start_{l}")
        else:
            for n in CONV:
                full = jnp.stack(conv_g[n])
                _, taps, c = full.shape
                slabs.append(full.reshape(n_layers, taps, N_DEV, c // N_DEV).transpose(2, 0, 1, 3)
                             .reshape(N_DEV, n_layers * taps, c // N_DEV))
            last = _chip_exchange(_pair_sums(slabs, core, "0"), name="reduce_chips_0")
            parts[0] = last[:n_big]
            conv_parts = last[n_big:]

    out = {k: {} for k in KINDS}
    for q, n in enumerate(BIG):
        layer_parts = [parts[l][q] for l in range(n_layers)]
        for k, r in zip(KINDS, _adamw_sharded(layer_parts, w[n], m[n], v[n], name=f"adamw_{n}")):
            out[k][n] = r
    for n, p in zip(CONV, conv_parts):
        as_one = lambda a: a.reshape(1, *p.shape[1:])
        for k, r in zip(KINDS, _adamw_sharded([p], as_one(w[n]), as_one(m[n]), as_one(v[n]), name=f"adamw_{n}")):
            out[k][n] = r.reshape(w[n].shape)

    rep_cols = [rep_g[l][n] for l in range(n_layers) for n in REPLICATED] + [dgf, loss_sq]
    rep_all = _all_gather([jnp.concatenate(rep_cols, axis=1)], name="gather_small")[0]
    with_final = lambda d: {**{n: d[n] for n in REPLICATED}, "final_norm_g": _row(d["final_norm_g"])}
    loss, rep_res = _adamw_replicated(rep_all, REPLICATED, with_final(w), with_final(m), with_final(v),
                                      loss_sq.shape[1], name="adamw_small")
    for n, res in rep_res.items():
        for k, r in zip(KINDS, res):
            out[k][n] = r.reshape(w[n].shape)

    grad_x = dh.reshape(x.shape)
    return (loss.reshape(()), grad_x, *[out["grad"][n] for n in order], *[out["delta"][n] for n in order],
            *[out["m"][n] for n in order], *[out["v"][n] for n in order])
```

```python
import functools

import jax
import jax.numpy as jnp
from jax import lax
from jax.experimental import pallas as pl
from jax.experimental.pallas import tpu as pltpu

F32 = jnp.float32
BF16 = jnp.bfloat16

N_DEV = 8
N_CHIP = 4
D_CONF = 512
CONF_K = 31
SHORT_K = 3
EPS = 1e-6
HALO = 32
HALO3 = 8
HALO3_BLK = 16
LANES = 128
SUB = 8
VMEM_LIMIT = 56 * 1024 * 1024

ADAM_LR = 0.001
ADAM_B1 = 0.9
ADAM_B2 = 0.999
ADAM_EPS = 1e-08
ADAM_WD = 0.01
ADAM_STEP = 10

MESH = pl.DeviceIdType.MESH
ANY = pl.BlockSpec(memory_space=pl.ANY)


def _params(*sem):
    return pltpu.CompilerParams(dimension_semantics=sem, vmem_limit_bytes=VMEM_LIMIT)


def _resident(shape, index_map):
    return pl.BlockSpec(shape, index_map, pipeline_mode=pl.Buffered(1))


def _row_loop(n_rows, rb, fn, unroll=1):
    rb = min(rb, n_rows)

    def body(i, carry):
        fn(pl.ds(pl.multiple_of(i * rb, rb), rb))
        return carry
    lax.fori_loop(0, n_rows // rb, body, 0, unroll=unroll)


def _rows8(v):
    acc = v[0:SUB]
    for k in range(1, v.shape[0] // SUB):
        acc = acc + v[k * SUB:(k + 1) * SUB]
    return acc


def _sigmoid(z):
    return 0.5 * jnp.tanh(0.5 * z) + 0.5


def _dot(a, b):
    return jnp.dot(a, b, preferred_element_type=F32)


def _dot_tn(a, b):
    return lax.dot_general(a, b, (((0,), (0,)), ((), ())), preferred_element_type=F32)


def _conv_taps(win, w_ref, out, *, taps, n_rows, base, width, transposed=False, bias_ref=None):
    rb = min(64, n_rows)

    def lane_body(cb, carry):
        lanes = pl.ds(pl.multiple_of(cb * LANES, LANES), LANES)
        for r0 in range(0, n_rows, rb):
            acc = None
            for k in range(taps):
                off = (taps - 1 - k) if transposed else (k - (taps - 1))
                term = w_ref[pl.ds(k, 1), lanes] * win[pl.ds(base + r0 + off, rb), lanes]
                acc = term if acc is None else acc + term
            if bias_ref is not None:
                acc = acc + bias_ref[:, lanes]
            out[pl.ds(r0, rb), lanes] = acc.astype(out.dtype)
        return carry

    lax.fori_loop(0, width // LANES, lane_body, 0)


def _conv_wgrad(dy, win, dw_acc, *, taps, n_rows, base, width):
    rb = min(64, n_rows)

    def lane_body(cb, carry):
        lanes = pl.ds(pl.multiple_of(cb * LANES, LANES), LANES)
        for k in range(taps):
            acc = None
            for r0 in range(0, n_rows, rb):
                prod = dy[pl.ds(r0, rb), lanes] * win[pl.ds(base + r0 + k - (taps - 1), rb), lanes]
                acc = prod if acc is None else acc + prod
            dw_acc[pl.ds(k * SUB, SUB), lanes] += _rows8(acc)
        return carry

    lax.fori_loop(0, width // LANES, lane_body, 0)


def _fold8(acc_ref, taps):
    return jnp.concatenate(
        [jnp.sum(acc_ref[pl.ds(k * SUB, SUB), :], axis=0, keepdims=True) for k in range(taps)], axis=0)


def _seq_tile(s_len):
    return min(256, s_len)


def _mm_tile(s_len):
    return min(512, s_len)


def _ff_chunk(ff):
    best = LANES
    for c in range(LANES, 1408 + 1, LANES):
        if ff % c == 0:
            best = c
    return best


def _col_tile(n):
    for c in (512, 1408, 256, LANES):
        if n % c == 0:
            return c
    return n


def _rms_matmul(x, g, w, b, *, name, dep=None):
    s_len, d = x.shape
    n = w.shape[1]
    tm = _mm_tile(s_len)
    cn = _col_tile(n)
    has_bias = b is not None

    def body(*refs):
        x_ref, g_ref, w_ref = refs[0:3]
        b_ref = refs[3] if has_bias else None
        o_ref, h_ref = refs[-2:]

        def blk(rows):
            xv = x_ref[rows, :]
            r = lax.rsqrt(jnp.mean(xv * xv, axis=-1, keepdims=True) + EPS)
            h_ref[rows, :] = ((xv * r) * g_ref[...]).astype(BF16)
        _row_loop(tm, 128, blk)

        def chunk(j, carry):
            cols = pl.ds(pl.multiple_of(j * cn, cn), cn)
            acc = _dot(h_ref[...], w_ref[:, cols])
            if has_bias:
                acc = acc + b_ref[:, cols]
            o_ref[:, cols] = acc.astype(BF16)
            return carry
        lax.fori_loop(0, n // cn, chunk, 0)

    in_specs = [pl.BlockSpec((tm, d), lambda i: (i, 0)), _resident((1, d), lambda i: (0, 0)),
                _resident((d, n), lambda i: (0, 0))]
    args = [x, g, w]
    if has_bias:
        in_specs.append(_resident((1, n), lambda i: (0, 0)))
        args.append(b)
    in_specs.append(ANY)
    args.append(x if dep is None else dep)
    return pl.pallas_call(
        body, grid=(s_len // tm,), in_specs=in_specs,
        out_specs=[pl.BlockSpec((tm, n), lambda i: (i, 0)), pl.BlockSpec((tm, d), lambda i: (i, 0))],
        out_shape=[jax.ShapeDtypeStruct((s_len, n), BF16), jax.ShapeDtypeStruct((s_len, d), BF16)],
        compiler_params=_params("parallel"), name=name,
    )(*args)


def _mix_windows(u_ref, uh_ref, gw, pw, first, t):
    c = D_CONF
    uh = uh_ref[...].astype(F32)
    gw[0:HALO, :] = jnp.where(first, 0.0, uh[:, 0:c] * _sigmoid(uh[:, c:2 * c]))
    pw[0:HALO3, :] = jnp.where(first, 0.0, uh[HALO - HALO3:HALO, 3 * c:4 * c] * uh[HALO - HALO3:HALO, 4 * c:5 * c])

    def blk(rows):
        dst = pl.ds(pl.multiple_of(rows.start + HALO, SUB), rows.size)
        gw[dst, :] = u_ref[rows, 0:c].astype(F32) * _sigmoid(u_ref[rows, c:2 * c].astype(F32))
        dst3 = pl.ds(pl.multiple_of(rows.start + HALO3, SUB), rows.size)
        pw[dst3, :] = u_ref[rows, 3 * c:4 * c].astype(F32) * u_ref[rows, 4 * c:5 * c].astype(F32)
    _row_loop(t, 64, blk)


def _mix_fwd(u, x0, wa, ba, lg, lb, wb, w_out, *, name):
    s_len, d_in = u.shape
    d = x0.shape[1]
    c = D_CONF
    t = _seq_tile(s_len)
    per = t // HALO

    def body(u_ref, uh_ref, x0_ref, wa_ref, ba_ref, lg_ref, lb_ref, wb_ref, wo_ref, y_ref, x1_ref, ca,
             gw, pw, cb):
        first = pl.program_id(0) == 0
        _mix_windows(u_ref, uh_ref, gw, pw, first, t)
        _conv_taps(gw, wa_ref, ca, taps=CONF_K, n_rows=t, base=HALO, width=c, bias_ref=ba_ref)
        _conv_taps(pw, wb_ref, cb, taps=SHORT_K, n_rows=t, base=HALO3, width=c)

        def blk(rows):
            cv = ca[rows, :]
            mu = jnp.mean(cv, axis=-1, keepdims=True)
            xc = cv - mu
            var = jnp.mean(xc * xc, axis=-1, keepdims=True)
            ln = (xc * lax.rsqrt(var + EPS)) * lg_ref[...] + lb_ref[...]
            y_ref[rows, 0:c] = (ln * _sigmoid(ln)).astype(BF16)
            y_ref[rows, c:2 * c] = (u_ref[rows, 2 * c:3 * c].astype(F32) * cb[rows, :]).astype(BF16)
        _row_loop(t, 64, blk)
        x1_ref[...] = x0_ref[...] + _dot(y_ref[...], wo_ref[...])

    small = lambda r: _resident((r, c), lambda i: (0, 0))
    return pl.pallas_call(
        body, grid=(s_len // t,),
        in_specs=[pl.BlockSpec((t, d_in), lambda i: (i, 0)),
                  pl.BlockSpec((HALO, d_in), lambda i: (jnp.maximum(i * per - 1, 0), 0)),
                  pl.BlockSpec((t, d), lambda i: (i, 0)),
                  small(CONF_K), small(1), small(1), small(1), small(SHORT_K),
                  _resident((2 * c, d), lambda i: (0, 0))],
        out_specs=[pl.BlockSpec((t, 2 * c), lambda i: (i, 0)), pl.BlockSpec((t, d), lambda i: (i, 0)),
                   pl.BlockSpec((t, c), lambda i: (i, 0))],
        out_shape=[jax.ShapeDtypeStruct((s_len, 2 * c), BF16), jax.ShapeDtypeStruct((s_len, d), F32),
                   jax.ShapeDtypeStruct((s_len, c), F32)],
        scratch_shapes=[pltpu.VMEM((HALO + t, c), F32), pltpu.VMEM((HALO3 + t, c), F32), pltpu.VMEM((t, c), F32)],
        compiler_params=_params("arbitrary"), name=name,
    )(u, u, x0, wa, ba, lg, lb, wb, w_out)


def _ffn_windows(ug_ref, ugh_ref, uv_ref, uvh_ref, gwin, vwin, first, t):
    lo = HALO3_BLK - HALO3
    gwin[0:HALO3, :] = jnp.where(first, 0.0, ugh_ref[...].astype(F32)[lo:HALO3_BLK])
    vwin[0:HALO3, :] = jnp.where(first, 0.0, uvh_ref[...].astype(F32)[lo:HALO3_BLK])

    def blk(rows):
        dst = pl.ds(pl.multiple_of(rows.start + HALO3, SUB), rows.size)
        gwin[dst, :] = ug_ref[rows, :].astype(F32)
        vwin[dst, :] = uv_ref[rows, :].astype(F32)
    _row_loop(t, 64, blk)


def _ffn_fwd(uf, x1, wf, w_down, *, name, dep=None):
    s_len, ff2 = uf.shape
    ff = ff2 // 2
    d = x1.shape[1]
    t = _seq_tile(s_len)
    fc = _ff_chunk(ff)
    nc = ff // fc
    per = t // HALO3_BLK

    def body(ug_ref, ugh_ref, uv_ref, uvh_ref, x1_ref, wfg_ref, wfv_ref, wd_ref, dep_ref, act_ref, x2_ref,
             gwin, vwin, cg, cv):
        first = pl.program_id(0) == 0
        _ffn_windows(ug_ref, ugh_ref, uv_ref, uvh_ref, gwin, vwin, first, t)
        _conv_taps(gwin, wfg_ref, cg, taps=SHORT_K, n_rows=t, base=HALO3, width=fc)
        _conv_taps(vwin, wfv_ref, cv, taps=SHORT_K, n_rows=t, base=HALO3, width=fc)

        def blk(rows):
            gv = cg[rows, :]
            act_ref[rows, :] = ((gv * _sigmoid(gv)) * cv[rows, :]).astype(BF16)
        _row_loop(t, 32, blk, unroll=2)

        @pl.when(pl.program_id(1) == 0)
        def _():
            x2_ref[...] = x1_ref[...]
        x2_ref[...] += _dot(act_ref[...], wd_ref[...])

    halo_map = lambda off: (lambda i, j: (jnp.maximum(i * per - 1, 0), j + off))
    return pl.pallas_call(
        body, grid=(s_len // t, nc),
        in_specs=[pl.BlockSpec((t, fc), lambda i, j: (i, j)), pl.BlockSpec((HALO3_BLK, fc), halo_map(0)),
                  pl.BlockSpec((t, fc), lambda i, j: (i, j + nc)), pl.BlockSpec((HALO3_BLK, fc), halo_map(nc)),
                  pl.BlockSpec((t, d), lambda i, j: (i, 0)),
                  pl.BlockSpec((SHORT_K, fc), lambda i, j: (0, j)),
                  pl.BlockSpec((SHORT_K, fc), lambda i, j: (0, j + nc)),
                  pl.BlockSpec((fc, d), lambda i, j: (j, 0)), ANY],
        out_specs=[pl.BlockSpec((t, fc), lambda i, j: (i, j)), pl.BlockSpec((t, d), lambda i, j: (i, 0))],
        out_shape=[jax.ShapeDtypeStruct((s_len, ff), BF16), jax.ShapeDtypeStruct((s_len, d), F32)],
        scratch_shapes=[pltpu.VMEM((HALO3 + t, fc), F32), pltpu.VMEM((HALO3 + t, fc), F32),
                        pltpu.VMEM((t, fc), F32), pltpu.VMEM((t, fc), F32)],
        compiler_params=_params("parallel", "arbitrary"), name=name,
    )(uf, uf, uf, uf, x1, wf, wf, w_down, uf if dep is None else dep)


def _loss_bwd(x, g, target, *, name):
    s_len, d = x.shape
    t = _seq_tile(s_len)

    def body(x_ref, g_ref, t_ref, l_ref, dx_ref, dxb_ref, dg_ref):
        @pl.when(pl.program_id(0) == 0)
        def _():
            l_ref[...] = jnp.zeros_like(l_ref)
            dg_ref[...] = jnp.zeros_like(dg_ref)

        def blk(rows):
            xv = x_ref[rows, :]
            r = lax.rsqrt(jnp.mean(xv * xv, axis=-1, keepdims=True) + EPS)
            xn = xv * r
            e = xn * g_ref[...] - t_ref[rows, :]
            l_ref[...] += _rows8(e * e)
            dy = e * (1.0 / d)
            dg_ref[...] += _rows8(dy * xn)
            dn = dy * g_ref[...]
            dx = r * (dn - xn * jnp.mean(dn * xn, axis=-1, keepdims=True))
            dx_ref[rows, :] = dx
            dxb_ref[rows, :] = dx.astype(BF16)
        _row_loop(t, 64, blk)

    row = pl.BlockSpec((t, d), lambda i: (i, 0))
    part = pl.BlockSpec((SUB, d), lambda i: (0, 0))
    return pl.pallas_call(
        body, grid=(s_len // t,),
        in_specs=[row, _resident((1, d), lambda i: (0, 0)), row],
        out_specs=[part, row, row, part],
        out_shape=[jax.ShapeDtypeStruct((SUB, d), F32), jax.ShapeDtypeStruct((s_len, d), F32),
                   jax.ShapeDtypeStruct((s_len, d), BF16), jax.ShapeDtypeStruct((SUB, d), F32)],
        compiler_params=_params("arbitrary"), name=name,
    )(x, g, target)


def _ffn_bwd(dx2, uf, wf, w_down_t, *, name, dep=None):
    s_len, ff2 = uf.shape
    ff = ff2 // 2
    d = dx2.shape[1]
    t = _seq_tile(s_len)
    n_t = s_len // t
    fc = _ff_chunk(ff)
    nc = ff // fc
    per = t // HALO3_BLK

    def body(dx_ref, ug_ref, ugh_ref, uv_ref, uvh_ref, wfg_ref, wfv_ref, wd_ref, dep_ref,
             dug_ref, duv_ref, dwg_ref, dwv_ref, gwin, vwin, cg, cv, dact, dgw, dvw, awg, awv):
        i = pl.program_id(1)
        first = i == n_t - 1
        _ffn_windows(ug_ref, ugh_ref, uv_ref, uvh_ref, gwin, vwin, first, t)
        _conv_taps(gwin, wfg_ref, cg, taps=SHORT_K, n_rows=t, base=HALO3, width=fc)
        _conv_taps(vwin, wfv_ref, cv, taps=SHORT_K, n_rows=t, base=HALO3, width=fc)
        dact[...] = _dot(dx_ref[...], wd_ref[...])

        @pl.when(i == 0)
        def _():
            dgw[t:t + HALO3, :] = jnp.zeros((HALO3, fc), F32)
            dvw[t:t + HALO3, :] = jnp.zeros((HALO3, fc), F32)
            awg[...] = jnp.zeros_like(awg)
            awv[...] = jnp.zeros_like(awv)

        def blk(rows):
            gv = cg[rows, :]
            sg = _sigmoid(gv)
            da = dact[rows, :]
            dgw[rows, :] = (da * cv[rows, :]) * (sg * (1.0 + gv * (1.0 - sg)))
            dvw[rows, :] = da * (gv * sg)
        _row_loop(t, 32, blk, unroll=2)

        _conv_taps(dgw, wfg_ref, dug_ref, taps=SHORT_K, n_rows=t, base=0, width=fc, transposed=True)
        _conv_taps(dvw, wfv_ref, duv_ref, taps=SHORT_K, n_rows=t, base=0, width=fc, transposed=True)
        _conv_wgrad(dgw, gwin, awg, taps=SHORT_K, n_rows=t, base=HALO3, width=fc)
        _conv_wgrad(dvw, vwin, awv, taps=SHORT_K, n_rows=t, base=HALO3, width=fc)
        dgw[t:t + HALO3, :] = dgw[0:HALO3, :]
        dvw[t:t + HALO3, :] = dvw[0:HALO3, :]

        @pl.when(i == n_t - 1)
        def _():
            dwg_ref[...] = _fold8(awg, SHORT_K)
            dwv_ref[...] = _fold8(awv, SHORT_K)

    rev = lambda i: n_t - 1 - i
    halo_map = lambda off: (lambda j, i: (jnp.maximum(rev(i) * per - 1, 0), j + off))
    return pl.pallas_call(
        body, grid=(nc, n_t),
        in_specs=[pl.BlockSpec((t, d), lambda j, i: (rev(i), 0)),
                  pl.BlockSpec((t, fc), lambda j, i: (rev(i), j)), pl.BlockSpec((HALO3_BLK, fc), halo_map(0)),
                  pl.BlockSpec((t, fc), lambda j, i: (rev(i), j + nc)), pl.BlockSpec((HALO3_BLK, fc), halo_map(nc)),
                  pl.BlockSpec((SHORT_K, fc), lambda j, i: (0, j)),
                  pl.BlockSpec((SHORT_K, fc), lambda j, i: (0, j + nc)),
                  pl.BlockSpec((d, fc), lambda j, i: (0, j)), ANY],
        out_specs=[pl.BlockSpec((t, fc), lambda j, i: (rev(i), j)), pl.BlockSpec((t, fc), lambda j, i: (rev(i), j)),
                   pl.BlockSpec((SHORT_K, fc), lambda j, i: (0, j)), pl.BlockSpec((SHORT_K, fc), lambda j, i: (0, j))],
        out_shape=[jax.ShapeDtypeStruct((s_len, ff), BF16), jax.ShapeDtypeStruct((s_len, ff), BF16),
                   jax.ShapeDtypeStruct((SHORT_K, ff), F32), jax.ShapeDtypeStruct((SHORT_K, ff), F32)],
        scratch_shapes=[pltpu.VMEM((HALO3 + t, fc), F32), pltpu.VMEM((HALO3 + t, fc), F32),
                        pltpu.VMEM((t, fc), F32), pltpu.VMEM((t, fc), F32), pltpu.VMEM((t, fc), F32),
                        pltpu.VMEM((t + HALO3, fc), F32), pltpu.VMEM((t + HALO3, fc), F32),
                        pltpu.VMEM((SHORT_K * SUB, fc), F32), pltpu.VMEM((SHORT_K * SUB, fc), F32)],
        compiler_params=_params("arbitrary", "arbitrary"), name=name,
    )(dx2, uf, uf, uf, uf, wf, wf, w_down_t, uf if dep is None else dep)


def _mix_bwd(dx1, u, ca, wa, lg, lb, wb, w_out_t, *, name):
    s_len, d_in = u.shape
    d = dx1.shape[1]
    c = D_CONF
    t = _seq_tile(s_len)
    n_t = s_len // t
    per = t // HALO

    def body(dx_ref, u_ref, uh_ref, ca_ref, wa_ref, lg_ref, lb_ref, wb_ref, wo_ref,
             du_ref, dwa_ref, dwb_ref, dba_ref, dlg_ref, dlb_ref, dbin_ref,
             gw, pw, cb, dyc, dcaw, dcbw, dglu, dp, awa, awb):
        i = pl.program_id(0)
        first = i == n_t - 1
        _mix_windows(u_ref, uh_ref, gw, pw, first, t)
        _conv_taps(pw, wb_ref, cb, taps=SHORT_K, n_rows=t, base=HALO3, width=c)
        dyc[...] = _dot(dx_ref[...], wo_ref[...])

        @pl.when(i == 0)
        def _():
            dcaw[t:t + HALO, :] = jnp.zeros((HALO, c), F32)
            dcbw[t:t + HALO3, :] = jnp.zeros((HALO3, c), F32)
            awa[...] = jnp.zeros_like(awa)
            awb[...] = jnp.zeros_like(awb)
            dba_ref[...] = jnp.zeros_like(dba_ref)
            dlg_ref[...] = jnp.zeros_like(dlg_ref)
            dlb_ref[...] = jnp.zeros_like(dlb_ref)
            dbin_ref[...] = jnp.zeros_like(dbin_ref)

        def blk1(rows):
            cv = ca_ref[rows, :]
            mu = jnp.mean(cv, axis=-1, keepdims=True)
            xc = cv - mu
            rstd = lax.rsqrt(jnp.mean(xc * xc, axis=-1, keepdims=True) + EPS)
            nrm = xc * rstd
            ln = nrm * lg_ref[...] + lb_ref[...]
            sg = _sigmoid(ln)
            dln = dyc[rows, 0:c] * (sg * (1.0 + ln * (1.0 - sg)))
            dlg_ref[...] += _rows8(dln * nrm)
            dlb_ref[...] += _rows8(dln)
            dn = dln * lg_ref[...]
            dca = rstd * (dn - jnp.mean(dn, axis=-1, keepdims=True)
                          - nrm * jnp.mean(dn * nrm, axis=-1, keepdims=True))
            dcaw[rows, :] = dca
            dba_ref[...] += _rows8(dca)
            ds = dyc[rows, c:2 * c]
            dgb = ds * cb[rows, :]
            dcbw[rows, :] = ds * u_ref[rows, 2 * c:3 * c].astype(F32)
            du_ref[rows, 2 * c:3 * c] = dgb.astype(BF16)
            dbin_ref[:, 2 * c:3 * c] += _rows8(dgb)
        _row_loop(t, 64, blk1, unroll=2)

        _conv_taps(dcaw, wa_ref, dglu, taps=CONF_K, n_rows=t, base=0, width=c, transposed=True)
        _conv_taps(dcbw, wb_ref, dp, taps=SHORT_K, n_rows=t, base=0, width=c, transposed=True)
        _conv_wgrad(dcaw, gw, awa, taps=CONF_K, n_rows=t, base=HALO, width=c)
        _conv_wgrad(dcbw, pw, awb, taps=SHORT_K, n_rows=t, base=HALO3, width=c)
        dcaw[t:t + HALO, :] = dcaw[0:HALO, :]
        dcbw[t:t + HALO3, :] = dcbw[0:HALO3, :]

        def blk2(rows):
            av = u_ref[rows, 0:c].astype(F32)
            sg = _sigmoid(u_ref[rows, c:2 * c].astype(F32))
            dg = dglu[rows, :]
            d_av = dg * sg
            d_ag = (dg * av) * (sg * (1.0 - sg))
            dpv = dp[rows, :]
            d_gc = dpv * u_ref[rows, 4 * c:5 * c].astype(F32)
            d_vs = dpv * u_ref[rows, 3 * c:4 * c].astype(F32)
            du_ref[rows, 0:c] = d_av.astype(BF16)
            du_ref[rows, c:2 * c] = d_ag.astype(BF16)
            du_ref[rows, 3 * c:4 * c] = d_gc.astype(BF16)
            du_ref[rows, 4 * c:5 * c] = d_vs.astype(BF16)
            dbin_ref[:, 0:c] += _rows8(d_av)
            dbin_ref[:, c:2 * c] += _rows8(d_ag)
            dbin_ref[:, 3 * c:4 * c] += _rows8(d_gc)
            dbin_ref[:, 4 * c:5 * c] += _rows8(d_vs)
        _row_loop(t, 64, blk2)

        @pl.when(i == n_t - 1)
        def _():
            dwa_ref[...] = _fold8(awa, CONF_K)
            dwb_ref[...] = _fold8(awb, SHORT_K)

    rev = lambda i: n_t - 1 - i
    small_in = lambda r: _resident((r, c), lambda i: (0, 0))
    small = lambda r: pl.BlockSpec((r, c), lambda i: (0, 0))
    return pl.pallas_call(
        body, grid=(n_t,),
        in_specs=[pl.BlockSpec((t, d), lambda i: (rev(i), 0)),
                  pl.BlockSpec((t, d_in), lambda i: (rev(i), 0)),
                  pl.BlockSpec((HALO, d_in), lambda i: (jnp.maximum(rev(i) * per - 1, 0), 0)),
                  pl.BlockSpec((t, c), lambda i: (rev(i), 0)),
                  small_in(CONF_K), small_in(1), small_in(1), small_in(SHORT_K),
                  _resident((d, 2 * c), lambda i: (0, 0))],
        out_specs=[pl.BlockSpec((t, d_in), lambda i: (rev(i), 0)),
                   small(CONF_K), small(SHORT_K), small(SUB), small(SUB), small(SUB),
                   pl.BlockSpec((SUB, d_in), lambda i: (0, 0))],
        out_shape=[jax.ShapeDtypeStruct((s_len, d_in), BF16),
                   jax.ShapeDtypeStruct((CONF_K, c), F32), jax.ShapeDtypeStruct((SHORT_K, c), F32),
                   jax.ShapeDtypeStruct((SUB, c), F32), jax.ShapeDtypeStruct((SUB, c), F32),
                   jax.ShapeDtypeStruct((SUB, c), F32), jax.ShapeDtypeStruct((SUB, d_in), F32)],
        scratch_shapes=[pltpu.VMEM((HALO + t, c), F32), pltpu.VMEM((HALO3 + t, c), F32),
                        pltpu.VMEM((t, c), F32), pltpu.VMEM((t, 2 * c), F32),
                        pltpu.VMEM((t + HALO, c), F32), pltpu.VMEM((t + HALO3, c), F32),
                        pltpu.VMEM((t, c), F32), pltpu.VMEM((t, c), F32),
                        pltpu.VMEM((CONF_K * SUB, c), F32), pltpu.VMEM((SHORT_K * SUB, c), F32)],
        compiler_params=_params("arbitrary"), name=name,
    )(dx1, u, u, ca, wa, lg, lb, wb, w_out_t)


def _matmul_tn(a, b, *, name, into=None, part=0, n_parts=1):
    s_len, k = a.shape
    n = b.shape[1]
    tk = _col_tile(k)
    per = k // tk

    def body(*refs):
        a_ref, b_ref = refs[0], refs[1]
        o_ref = refs[-1]
        o_ref[...] = _dot_tn(a_ref[...], b_ref[...]).astype(BF16)

    in_specs = [pl.BlockSpec((s_len, tk), lambda j: (0, j)), _resident((s_len, n), lambda j: (0, 0))]
    args = [a, b]
    aliases = {}
    if into is not None:
        in_specs.append(ANY)
        args.append(into)
        aliases = {2: 0}
    return pl.pallas_call(
        body, grid=(per,), in_specs=in_specs,
        out_specs=pl.BlockSpec((tk, n), lambda j: (part * per + j, 0)),
        out_shape=jax.ShapeDtypeStruct((n_parts * k, n), BF16),
        input_output_aliases=aliases,
        compiler_params=_params("parallel"), name=name,
    )(*args)


def _matmul_rmsbwd(dzs, wt, x, g, dx_in, *, name):
    s_len, d = x.shape
    n_z = len(dzs)
    nj = dzs[0].shape[1]
    t = _mm_tile(s_len)

    def body(*refs):
        dz_refs = refs[0:n_z]
        w_refs = refs[n_z:2 * n_z]
        x_ref, g_ref, dxi_ref, dx_ref, dxb_ref, dg_ref, dh = refs[2 * n_z:]

        @pl.when(pl.program_id(0) == 0)
        def _():
            dg_ref[...] = jnp.zeros_like(dg_ref)

        acc = _dot(dz_refs[0][...], w_refs[0][...])
        for q in range(1, n_z):
            acc = acc + _dot(dz_refs[q][...], w_refs[q][...])
        dh[...] = acc

        def blk(rows):
            xv = x_ref[rows, :]
            r = lax.rsqrt(jnp.mean(xv * xv, axis=-1, keepdims=True) + EPS)
            xn = xv * r
            dhv = dh[rows, :]
            dg_ref[...] += _rows8(dhv * xn)
            dn = dhv * g_ref[...]
            dx = dxi_ref[rows, :] + r * (dn - xn * jnp.mean(dn * xn, axis=-1, keepdims=True))
            dx_ref[rows, :] = dx
            dxb_ref[rows, :] = dx.astype(BF16)
        _row_loop(t, 128, blk)

    row = pl.BlockSpec((t, d), lambda i: (i, 0))
    in_specs = [pl.BlockSpec((t, nj), lambda i: (i, 0)) for _ in range(n_z)]
    in_specs += [_resident((nj, d), functools.partial(lambda q, i: (q, 0), q)) for q in range(n_z)]
    in_specs += [row, _resident((1, d), lambda i: (0, 0)), row]
    return pl.pallas_call(
        body, grid=(s_len // t,), in_specs=in_specs,
        out_specs=[row, row, pl.BlockSpec((SUB, d), lambda i: (0, 0))],
        out_shape=[jax.ShapeDtypeStruct((s_len, d), F32), jax.ShapeDtypeStruct((s_len, d), BF16),
                   jax.ShapeDtypeStruct((SUB, d), F32)],
        scratch_shapes=[pltpu.VMEM((t, d), F32)],
        compiler_params=_params("arbitrary"), name=name,
    )(*dzs, *([wt] * n_z), x, g, dx_in)


def _row(v):
    return v.reshape(1, -1)


def _layer_fwd(x0, p, tag, dep=None, before_ffn=None):
    u, h1 = _rms_matmul(x0, _row(p["mix_norm_g"]), p["w_in"], _row(p["b_in"]), name=f"in_proj_{tag}", dep=dep)
    ycat, x1, ca = _mix_fwd(u, x0, p["conv_a_w"], _row(p["conv_a_b"]), _row(p["ln_a_g"]), _row(p["ln_a_b"]),
                            p["conv_b_w"], p["w_out"], name=f"mix_fwd_{tag}")
    uf, h2 = _rms_matmul(x1, _row(p["ffn_norm_g"]), p["w_up"], None, name=f"up_proj_{tag}")
    dep_ffn = None if before_ffn is None else before_ffn(uf)
    act, x2 = _ffn_fwd(uf, x1, p["conv_f_w"], p["w_down"], name=f"ffn_fwd_{tag}", dep=dep_ffn)
    return x2, dict(x0=x0, h1=h1, u=u, ca=ca, ycat=ycat, x1=x1, h2=h2, uf=uf, act=act)


def _layer_bwd(dx2, dx2_b, p, saved, tag, dep=None):
    dug, duv, dwf_g, dwf_v = _ffn_bwd(dx2_b, saved["uf"], p["conv_f_w"], p["w_down_t"], name=f"ffn_bwd_{tag}",
                                      dep=dep)
    g_down = _matmul_tn(saved["act"], dx2_b, name=f"dw_down_{tag}")
    g_up = _matmul_tn(dug, saved["h2"], name=f"dw_up_g_{tag}", n_parts=2)
    g_up = _matmul_tn(duv, saved["h2"], name=f"dw_up_v_{tag}", into=g_up, part=1, n_parts=2)
    dx1, dx1_b, dg2 = _matmul_rmsbwd([dug, duv], p["w_up_t"], saved["x1"], _row(p["ffn_norm_g"]), dx2,
                                     name=f"dh_ffn_{tag}")
    du, dwa, dwb, dba, dlg, dlb, dbin = _mix_bwd(
        dx1_b, saved["u"], saved["ca"], p["conv_a_w"], _row(p["ln_a_g"]), _row(p["ln_a_b"]),
        p["conv_b_w"], p["w_out_t"], name=f"mix_bwd_{tag}")
    g_out = _matmul_tn(saved["ycat"], dx1_b, name=f"dw_out_{tag}")
    g_in = _matmul_tn(du, saved["h1"], name=f"dw_in_{tag}")
    dx0, dx0_b, dg1 = _matmul_rmsbwd([du], p["w_in_t"], saved["x0"], _row(p["mix_norm_g"]), dx1,
                                     name=f"dh_mix_{tag}")
    big = dict(w_in=g_in, w_out=g_out, w_up=g_up, w_down=g_down)
    conv = dict(conv_a_w=dwa, conv_b_w=dwb, conv_f_w=jnp.concatenate([dwf_g, dwf_v], axis=1))
    rep = dict(mix_norm_g=dg1, b_in=dbin, conv_a_b=dba, ln_a_g=dlg, ln_a_b=dlb, ffn_norm_g=dg2)
    return dx0, dx0_b, big, conv, rep


def _place():
    return lax.axis_index("x"), lax.axis_index("y"), lax.axis_index("c")


def _all_gather(arrs, *, name):
    n_a = len(arrs)

    def body(*refs):
        ins = refs[0:n_a]
        outs = refs[n_a:2 * n_a]
        send_sems, recv_sems, local_sems = refs[2 * n_a:]
        x, y, c = _place()
        sibling = (x, y, 1 - c)
        chips = [(1 - x, y), (x, 1 - y), (1 - x, 1 - y)]

        def slot(a, px, py, pc):
            return outs[a].at[4 * px + 2 * py + pc]

        def copy(a, k, block, to, src=None):
            return pltpu.make_async_remote_copy(
                src_ref=slot(a, *block) if src is None else src, dst_ref=slot(a, *block),
                send_sem=send_sems.at[a, k], recv_sem=recv_sems.at[a, k],
                device_id=to, device_id_type=MESH)

        me = (x, y, c)
        mine = [pltpu.make_async_copy(ins[a], slot(a, *me), local_sems.at[a]) for a in range(n_a)]
        for cp in mine:
            cp.start()
        started = []
        for a in range(n_a):
            first = [copy(a, 0, me, sibling, src=ins[a])]
            first += [copy(a, 1 + j, me, (*chip, c), src=ins[a]) for j, chip in enumerate(chips)]
            for cp in first:
                cp.start()
            started += first
        for a in range(n_a):
            for j, chip in enumerate(chips):
                copy(a, 1 + j, (*chip, c), me).wait_recv()
                passed = copy(a, 4 + j, (*chip, c), sibling)
                passed.start()
                started.append(passed)
        for a in range(n_a):
            copy(a, 0, sibling, me).wait_recv()
            for j, chip in enumerate(chips):
                copy(a, 4 + j, (*chip, 1 - c), me).wait_recv()
        for cp in started:
            cp.wait_send()
        for cp in mine:
            cp.wait()

    return pl.pallas_call(
        body, in_specs=[ANY] * n_a, out_specs=[ANY] * n_a,
        out_shape=[jax.ShapeDtypeStruct((N_DEV, *a.shape), a.dtype) for a in arrs],
        scratch_shapes=[pltpu.SemaphoreType.DMA((n_a, 7)), pltpu.SemaphoreType.DMA((n_a, 7)),
                        pltpu.SemaphoreType.DMA((n_a,))],
        name=name,
    )(*arrs)


def _sibling_exchange(arrs, *, name):
    n_a = len(arrs)

    def body(*refs):
        ins = refs[0:n_a]
        outs = refs[n_a:2 * n_a]
        send_sems, recv_sems = refs[2 * n_a:]
        x, y, c = _place()
        copies = [pltpu.make_async_remote_copy(
            src_ref=ins[a].at[:, 1 - c], dst_ref=outs[a], send_sem=send_sems.at[a], recv_sem=recv_sems.at[a],
            device_id=(x, y, 1 - c), device_id_type=MESH) for a in range(n_a)]
        for cp in copies:
            cp.start()
        for cp in copies:
            cp.wait()

    return pl.pallas_call(
        body, in_specs=[ANY] * n_a, out_specs=[ANY] * n_a,
        out_shape=[jax.ShapeDtypeStruct((N_CHIP, *a.shape[2:]), a.dtype) for a in arrs],
        scratch_shapes=[pltpu.SemaphoreType.DMA((n_a,)), pltpu.SemaphoreType.DMA((n_a,))],
        name=name,
    )(*arrs)


def _row_tile(r):
    for tr in (512, 352, 256, 128, 64, 32, 16, 8):
        if r % tr == 0:
            return tr
    return r


def _pair_sum(mine, theirs, core, *, name):
    n_chip, _, r, c = mine.shape
    tr = _row_tile(r)

    def body(core_ref, a_ref, b_ref, o_ref):
        o_ref[...] = (a_ref[...].astype(F32) + b_ref[...].astype(F32)).astype(o_ref.dtype)

    return pl.pallas_call(
        body,
        grid_spec=pltpu.PrefetchScalarGridSpec(
            num_scalar_prefetch=1, grid=(n_chip, r // tr),
            in_specs=[pl.BlockSpec((None, None, tr, c), lambda q, i, core_ref: (q, core_ref[0], i, 0)),
                      pl.BlockSpec((None, tr, c), lambda q, i, core_ref: (q, i, 0))],
            out_specs=pl.BlockSpec((None, tr, c), lambda q, i, core_ref: (q, i, 0))),
        out_shape=jax.ShapeDtypeStruct((n_chip, r, c), mine.dtype),
        compiler_params=_params("parallel", "parallel"), name=name,
    )(core, mine, theirs)


def _chip_exchange(arrs, *, name):
    n_a = len(arrs)

    def body(*refs):
        ins = refs[0:n_a]
        outs = refs[n_a:2 * n_a]
        send_sems, recv_sems, local_sems = refs[2 * n_a:]
        x, y, c = _place()
        my_chip = 2 * x + y
        chips = [(1 - x, y), (x, 1 - y), (1 - x, 1 - y)]
        mine = [pltpu.make_async_copy(ins[a].at[my_chip], outs[a].at[my_chip], local_sems.at[a]) for a in range(n_a)]
        for cp in mine:
            cp.start()
        copies = []
        for a in range(n_a):
            for j, (px, py) in enumerate(chips):
                copies.append(pltpu.make_async_remote_copy(
                    src_ref=ins[a].at[2 * px + py], dst_ref=outs[a].at[my_chip],
                    send_sem=send_sems.at[a, j], recv_sem=recv_sems.at[a, j],
                    device_id=(px, py, c), device_id_type=MESH))
        for cp in copies:
            cp.start()
        for cp in copies:
            cp.wait()
        for cp in mine:
            cp.wait()

    return pl.pallas_call(
        body, in_specs=[ANY] * n_a, out_specs=[ANY] * n_a,
        out_shape=[jax.ShapeDtypeStruct(a.shape, a.dtype) for a in arrs],
        scratch_shapes=[pltpu.SemaphoreType.DMA((n_a, 3)), pltpu.SemaphoreType.DMA((n_a, 3)),
                        pltpu.SemaphoreType.DMA((n_a,))],
        name=name,
    )(*arrs)


HBM = pl.BlockSpec(memory_space=pltpu.HBM)
SEM = pl.BlockSpec(memory_space=pltpu.SEMAPHORE)
EFFECT = pltpu.SideEffectType.DATAFLOW_SIDE_EFFECTING


def _in_hbm(a):
    return pltpu.with_memory_space_constraint(a, pltpu.HBM)


def _split_start(srcs, lands, plan, n_copies, *, name):
    n_s, n_l = len(srcs), len(lands)

    def body(*refs):
        src_refs = refs[0:n_s]
        land_refs = refs[n_s:n_s + n_l]
        send_sems, recv_sems = refs[n_s + n_l], refs[n_s + n_l + 1]
        token = refs[-1]
        for cp in plan(src_refs, land_refs, send_sems, recv_sems):
            cp.start()
        token[...] = jnp.zeros_like(token)

    thru = [pltpu.HBM(a.shape, a.dtype) for a in list(srcs) + list(lands)]
    res = pl.pallas_call(
        body, name=name,
        out_shape=(pltpu.SemaphoreType.DMA((n_copies,)), pltpu.SemaphoreType.DMA((n_copies,)), *thru,
                   jax.ShapeDtypeStruct((SUB, LANES), F32)),
        in_specs=[HBM] * (n_s + n_l),
        out_specs=(SEM, SEM, *([HBM] * (n_s + n_l)), pl.BlockSpec(memory_space=pltpu.VMEM)),
        input_output_aliases={i: 2 + i for i in range(n_s + n_l)},
        compiler_params=pltpu.CompilerParams(has_side_effects=EFFECT),
    )(*[_in_hbm(a) for a in srcs], *[_in_hbm(a) for a in lands])
    return res[0], res[1], list(res[2:2 + n_s]), list(res[2 + n_s:2 + n_s + n_l]), res[-1]


def _split_wait(send_sems, recv_sems, srcs, lands, after, plan, *, name):
    n_s, n_l = len(srcs), len(lands)

    def body(*refs):
        src_refs = refs[0:n_s]
        land_refs = refs[n_s:n_s + n_l]
        send, recv = refs[n_s + n_l], refs[n_s + n_l + 1]
        for cp in plan(src_refs, land_refs, send, recv):
            cp.wait_send()
            cp.wait_recv()

    res = pl.pallas_call(
        body, name=name,
        out_shape=tuple(pltpu.HBM(a.shape, a.dtype) for a in list(srcs) + list(lands)),
        in_specs=[HBM] * (n_s + n_l) + [SEM, SEM, ANY],
        out_specs=tuple([HBM] * (n_s + n_l)),
        input_output_aliases={i: i for i in range(n_s + n_l)},
        compiler_params=pltpu.CompilerParams(has_side_effects=EFFECT),
    )(*srcs, *lands, send_sems, recv_sems, after)
    return list(res[n_s:])


def _remote(src, dst, send_sems, recv_sems, k, to):
    return pltpu.make_async_remote_copy(src_ref=src, dst_ref=dst, send_sem=send_sems.at[k], recv_sem=recv_sems.at[k],
                                        device_id=to, device_id_type=MESH)


def _gather_plan_first(src_refs, land_refs, send_sems, recv_sems):
    x, y, c = _place()
    me = 4 * x + 2 * y + c
    peers = [(x, y, 1 - c), (1 - x, y, c), (x, 1 - y, c), (1 - x, 1 - y, c)]
    return [_remote(src, land.at[me], send_sems, recv_sems, 4 * a + k, to)
            for a, (src, land) in enumerate(zip(src_refs, land_refs)) for k, to in enumerate(peers)]


def _gather_plan_second(src_refs, land_refs, send_sems, recv_sems):
    x, y, c = _place()
    chips = [(1 - x, y), (x, 1 - y), (1 - x, 1 - y)]
    out = []
    for a, land in enumerate(land_refs):
        for j, (px, py) in enumerate(chips):
            slot = land.at[4 * px + 2 * py + c]
            out.append(_remote(slot, slot, send_sems, recv_sems, 3 * a + j, (x, y, 1 - c)))
    return out


def _chips_plan(src_refs, land_refs, send_sems, recv_sems):
    x, y, c = _place()
    my_chip = 2 * x + y
    chips = [(1 - x, y), (x, 1 - y), (1 - x, 1 - y)]
    return [_remote(src.at[2 * px + py], land.at[my_chip], send_sems, recv_sems, 3 * a + j, (px, py, c))
            for a, (src, land) in enumerate(zip(src_refs, land_refs)) for j, (px, py) in enumerate(chips)]


def _landing(like_shape, dtype, own, index):
    return lax.dynamic_update_index_in_dim(lax.empty(like_shape, dtype), own, index, 0)


def _adamw_math(g, w, m, v):
    m = ADAM_B1 * m + (1.0 - ADAM_B1) * g
    v = ADAM_B2 * v + (1.0 - ADAM_B2) * (g * g)
    m_hat = m / (1.0 - ADAM_B1 ** ADAM_STEP)
    v_hat = v / (1.0 - ADAM_B2 ** ADAM_STEP)
    delta = -ADAM_LR * (m_hat / (jnp.sqrt(v_hat) + ADAM_EPS) + ADAM_WD * w)
    return delta, m, v


def _adamw_sharded(parts, w, m, v, *, name):
    n_layers, r, c = w.shape
    n_chip = parts[0].shape[0]
    tr = _row_tile(r)
    n_i = r // tr

    def body(*refs):
        p_refs = refs[0:n_layers]
        w_ref, m_ref, v_ref, g_out, d_out, m_out, v_out = refs[n_layers:]
        layer = pl.program_id(0)
        for l in range(n_layers):
            @pl.when(layer == l)
            def _(l=l):
                g = p_refs[l][0].astype(F32)
                for q in range(1, n_chip):
                    g = g + p_refs[l][q].astype(F32)
                delta, m_new, v_new = _adamw_math(g, w_ref[...], m_ref[...], v_ref[...])
                g_out[...] = g
                d_out[...] = delta
                m_out[...] = m_new
                v_out[...] = v_new

    def part_map(l):
        return lambda layer, i: (0, jnp.where(layer == l, i, jnp.where(layer < l, 0, n_i - 1)), 0)

    blk = pl.BlockSpec((None, tr, c), lambda layer, i: (layer, i, 0))
    return pl.pallas_call(
        body, grid=(n_layers, n_i),
        in_specs=[pl.BlockSpec((n_chip, tr, c), part_map(l)) for l in range(n_layers)] + [blk, blk, blk],
        out_specs=[blk] * 4, out_shape=[jax.ShapeDtypeStruct((n_layers, r, c), F32)] * 4,
        compiler_params=_params("arbitrary", "arbitrary"), name=name,
    )(*parts, w, m, v)


def _fold_partials(cols, *, name):
    widths = [c.shape[1] for c in cols]

    def body(*refs):
        o_ref = refs[-1]
        pos = 0
        for ref, width in zip(refs[:-1], widths):
            o_ref[:, pos:pos + width] = jnp.sum(ref[...], axis=0, keepdims=True)
            pos += width

    return pl.pallas_call(body, out_shape=jax.ShapeDtypeStruct((1, sum(widths)), F32), name=name)(*cols)


def _adamw_replicated(parts, names, w, m, v, n_loss, *, name):
    n_dev = parts.shape[0]
    n_layers = w[names[0]].shape[0]
    every = list(names) + ["final_norm_g"]
    n_p = len(every)

    def body(*refs):
        p_ref = refs[0]
        w_refs = dict(zip(every, refs[1:1 + n_p]))
        m_refs = dict(zip(every, refs[1 + n_p:1 + 2 * n_p]))
        v_refs = dict(zip(every, refs[1 + 2 * n_p:1 + 3 * n_p]))
        l_out = refs[1 + 3 * n_p]
        outs = refs[2 + 3 * n_p:]
        o_refs = {n: outs[4 * q:4 * q + 4] for q, n in enumerate(every)}
        acc = p_ref[0]
        for q in range(1, n_dev):
            acc = acc + p_ref[q]
        tot = jnp.sum(acc, axis=0, keepdims=True)
        pos = 0
        where = [(n, l) for l in range(n_layers) for n in names] + [("final_norm_g", 0)]
        for n, l in where:
            width = w_refs[n].shape[1]
            g = tot[:, pos:pos + width]
            pos += width
            row = pl.ds(l, 1)
            delta, m_new, v_new = _adamw_math(g, w_refs[n][row, :], m_refs[n][row, :], v_refs[n][row, :])
            for o, val in zip(o_refs[n], (g, delta, m_new, v_new)):
                o[row, :] = val
        l_out[...] = (0.5 / n_loss) * jnp.sum(tot[:, pos:pos + n_loss], axis=-1, keepdims=True)

    shapes = [jax.ShapeDtypeStruct((1, 1), F32)]
    for n in every:
        shapes += [jax.ShapeDtypeStruct(w[n].shape, F32)] * 4
    res = pl.pallas_call(
        body, out_shape=shapes,
        compiler_params=pltpu.CompilerParams(vmem_limit_bytes=VMEM_LIMIT), name=name,
    )(parts, *[w[n] for n in every], *[m[n] for n in every], *[v[n] for n in every])
    return res[0], {n: res[1 + 4 * q:5 + 4 * q] for q, n in enumerate(every)}


BIG = ("w_in", "w_out", "w_up", "w_down")
COL_SHARDED = ("w_in", "w_up")
CONV = ("conv_a_w", "conv_b_w", "conv_f_w")
REPLICATED = ("mix_norm_g", "b_in", "conv_a_b", "ln_a_g", "ln_a_b", "ffn_norm_g")
KINDS = ("grad", "delta", "m", "v")


def _weights_from_gathered(name, g):
    n_dev, r, c = g.shape
    flat = g.reshape(n_dev * r, c)
    return (flat.T, flat) if name in COL_SHARDED else (flat, flat.T)


def _slabs_from_full(grad):
    return grad.reshape(N_DEV, grad.shape[0] // N_DEV, grad.shape[1])


def _pair_sums(slabs, core, tag):
    slabs = [s.reshape(N_CHIP, 2, *s.shape[1:]) for s in slabs]
    theirs = _sibling_exchange(slabs, name=f"reduce_siblings_{tag}")
    return [_pair_sum(a, b, core, name=f"pair_sum_{tag}_{q}") for q, (a, b) in enumerate(zip(slabs, theirs))]


def kernel(x, mix_norm_g, w_in, b_in, conv_a_w, conv_a_b, ln_a_g, ln_a_b, conv_b_w, w_out, ffn_norm_g, w_up, conv_f_w, w_down, final_norm_g, loss_target, m_mix_norm_g, m_w_in, m_b_in, m_conv_a_w, m_conv_a_b, m_ln_a_g, m_ln_a_b, m_conv_b_w, m_w_out, m_ffn_norm_g, m_w_up, m_conv_f_w, m_w_down, m_final_norm_g, v_mix_norm_g, v_w_in, v_b_in, v_conv_a_w, v_conv_a_b, v_ln_a_g, v_ln_a_b, v_conv_b_w, v_w_out, v_ffn_norm_g, v_w_up, v_conv_f_w, v_w_down, v_final_norm_g):
    w = dict(mix_norm_g=mix_norm_g, w_in=w_in, b_in=b_in, conv_a_w=conv_a_w, conv_a_b=conv_a_b, ln_a_g=ln_a_g,
             ln_a_b=ln_a_b, conv_b_w=conv_b_w, w_out=w_out, ffn_norm_g=ffn_norm_g, w_up=w_up, conv_f_w=conv_f_w,
             w_down=w_down, final_norm_g=final_norm_g)
    m = dict(mix_norm_g=m_mix_norm_g, w_in=m_w_in, b_in=m_b_in, conv_a_w=m_conv_a_w, conv_a_b=m_conv_a_b,
             ln_a_g=m_ln_a_g, ln_a_b=m_ln_a_b, conv_b_w=m_conv_b_w, w_out=m_w_out, ffn_norm_g=m_ffn_norm_g,
             w_up=m_w_up, conv_f_w=m_conv_f_w, w_down=m_w_down, final_norm_g=m_final_norm_g)
    v = dict(mix_norm_g=v_mix_norm_g, w_in=v_w_in, b_in=v_b_in, conv_a_w=v_conv_a_w, conv_a_b=v_conv_a_b,
             ln_a_g=v_ln_a_g, ln_a_b=v_ln_a_b, conv_b_w=v_conv_b_w, w_out=v_w_out, ffn_norm_g=v_ffn_norm_g,
             w_up=v_w_up, conv_f_w=v_conv_f_w, w_down=v_w_down, final_norm_g=v_final_norm_g)
    order = list(w)
    n_layers = w_in.shape[0]
    n_big = len(BIG)
    xs = x[0]
    target = loss_target[0]
    flip = lambda a: jnp.transpose(a, (0, 2, 1))
    wt, mt, vt = ({n: flip(d[n]) if n in COL_SHARDED else d[n] for n in BIG} for d in (w, m, v))
    px, py, pc = _place()
    core = pc.astype(jnp.int32).reshape(1)
    me = 4 * px + 2 * py + pc
    my_chip = 2 * px + py

    gathered = _all_gather([wt[n][0].astype(BF16) for n in BIG] + [w[n] for n in CONV], name="gather_weights_0")
    conv_full = {}
    for n, g in zip(CONV, gathered[n_big:]):
        n_dev, _, taps, c = g.shape
        conv_full[n] = g.transpose(1, 2, 0, 3).reshape(n_layers, taps, n_dev * c)

    def layer_params(l, big):
        p = {n: conv_full[n][l] for n in CONV}
        for n, g in zip(BIG, big):
            p[n], p[n + "_t"] = _weights_from_gathered(n, g)
        p.update({n: w[n][l] for n in REPLICATED})
        return p

    params = [layer_params(0, gathered[:n_big])]
    h = xs
    saved = []
    for l in range(n_layers):
        if l + 1 == n_layers:
            h, keep = _layer_fwd(h, params[l], str(l))
            saved.append(keep)
            break
        shards = [wt[n][l + 1].astype(BF16) for n in BIG]
        lands = [_landing((N_DEV, *s.shape), s.dtype, s, me) for s in shards]
        first = _split_start(shards, lands, _gather_plan_first, 4 * n_big, name=f"gather_first_start_{l + 1}")
        second = []

        def before_ffn(uf, first=first, second=second, tag=l + 1):
            arrived = _split_wait(first[0], first[1], first[2], first[3], uf, _gather_plan_first,
                                  name=f"gather_first_wait_{tag}")
            second.extend(_split_start([], arrived, _gather_plan_second, 3 * n_big, name=f"gather_second_start_{tag}"))
            return second[4]

        h, keep = _layer_fwd(h, params[l], str(l), dep=first[4], before_ffn=before_ffn)
        saved.append(keep)
        big = _split_wait(second[0], second[1], [], second[3], h, _gather_plan_second, name=f"gather_second_wait_{l + 1}")
        params.append(layer_params(l + 1, big))

    loss_sq, dh, dh_b, dgf = _loss_bwd(h, _row(final_norm_g), target, name="loss")
    conv_g = {n: [None] * n_layers for n in CONV}
    rep_g = [None] * n_layers
    parts = [None] * n_layers
    flying = None
    for l in reversed(range(n_layers)):
        dep = None if flying is None else flying[4]
        dh, dh_b, big_g, conv, rep_g[l] = _layer_bwd(dh, dh_b, params[l], saved[l], str(l), dep=dep)
        for n in CONV:
            conv_g[n][l] = conv[n]
        if flying is not None:
            parts[l + 1] = _split_wait(flying[0], flying[1], flying[2], flying[3], dh, _chips_plan,
                                       name=f"reduce_chips_wait_{l + 1}")
        slabs = [_slabs_from_full(big_g[n]) for n in BIG]
        if l > 0:
            pairs = _pair_sums(slabs, core, str(l))
            lands = [_landing(p.shape, p.dtype, lax.dynamic_index_in_dim(p, my_chip, 0, keepdims=False), my_chip)
                     for p in pairs]
            flying = _split_start(pairs, lands, _chips_plan, 3 * n_big, name=f"reduce_chips_start_{l}")
        else:
            for n in CONV:
                full = jnp.stack(conv_g[n])
                _, taps, c = full.shape
                slabs.append(full.reshape(n_layers, taps, N_DEV, c // N_DEV).transpose(2, 0, 1, 3)
                             .reshape(N_DEV, n_layers * taps, c // N_DEV))
            last = _chip_exchange(_pair_sums(slabs, core, "0"), name="reduce_chips_0")
            parts[0] = last[:n_big]
            conv_parts = last[n_big:]

    out = {k: {} for k in KINDS}
    for q, n in enumerate(BIG):
        layer_parts = [parts[l][q] for l in range(n_layers)]
        for k, r in zip(KINDS, _adamw_sharded(layer_parts, wt[n], mt[n], vt[n], name=f"adamw_{n}")):
            out[k][n] = flip(r) if n in COL_SHARDED else r
    for n, p in zip(CONV, conv_parts):
        as_one = lambda a: a.reshape(1, *p.shape[1:])
        for k, r in zip(KINDS, _adamw_sharded([p], as_one(w[n]), as_one(m[n]), as_one(v[n]), name=f"adamw_{n}")):
            out[k][n] = r.reshape(w[n].shape)

    rep_cols = [rep_g[l][n] for l in range(n_layers) for n in REPLICATED] + [dgf, loss_sq]
    rep_all = _all_gather([_fold_partials(rep_cols, name="fold_small")], name="gather_small")[0]
    with_final = lambda d: {**{n: d[n] for n in REPLICATED}, "final_norm_g": _row(d["final_norm_g"])}
    loss, rep_res = _adamw_replicated(rep_all, REPLICATED, with_final(w), with_final(m), with_final(v),
                                      loss_sq.shape[1], name="adamw_small")
    for n, res in rep_res.items():
        for k, r in zip(KINDS, res):
            out[k][n] = r.reshape(w[n].shape)

    grad_x = dh.reshape(x.shape)
    return (loss.reshape(()), grad_x, *[out["grad"][n] for n in order], *[out["delta"][n] for n in order],
            *[out["m"][n] for n in order], *[out["v"][n] for n in order])
```

```python
import functools

import jax
import jax.numpy as jnp
from jax import lax
from jax.experimental import pallas as pl
from jax.experimental.pallas import tpu as pltpu

F32 = jnp.float32
BF16 = jnp.bfloat16

N_DEV = 8
N_CHIP = 4
D_CONF = 512
CONF_K = 31
SHORT_K = 3
EPS = 1e-6
HALO = 32
HALO3 = 8
HALO3_BLK = 16
LANES = 128
SUB = 8
VMEM_LIMIT = 56 * 1024 * 1024

ADAM_LR = 0.001
ADAM_B1 = 0.9
ADAM_B2 = 0.999
ADAM_EPS = 1e-08
ADAM_WD = 0.01
ADAM_STEP = 10

MESH = pl.DeviceIdType.MESH
ANY = pl.BlockSpec(memory_space=pl.ANY)


def _params(*sem):
    return pltpu.CompilerParams(dimension_semantics=sem, vmem_limit_bytes=VMEM_LIMIT)


def _resident(shape, index_map):
    return pl.BlockSpec(shape, index_map, pipeline_mode=pl.Buffered(1))


def _row_loop(n_rows, rb, fn, unroll=1):
    rb = min(rb, n_rows)

    def body(i, carry):
        fn(pl.ds(pl.multiple_of(i * rb, rb), rb))
        return carry
    lax.fori_loop(0, n_rows // rb, body, 0, unroll=unroll)


def _rows8(v):
    acc = v[0:SUB]
    for k in range(1, v.shape[0] // SUB):
        acc = acc + v[k * SUB:(k + 1) * SUB]
    return acc


def _sigmoid(z):
    return 0.5 * jnp.tanh(0.5 * z) + 0.5


def _dot(a, b):
    return jnp.dot(a, b, preferred_element_type=F32)


def _dot_tn(a, b):
    return lax.dot_general(a, b, (((0,), (0,)), ((), ())), preferred_element_type=F32)


def _conv_taps(win, w_ref, out, *, taps, n_rows, base, width, transposed=False, bias_ref=None):
    rb = min(64, n_rows)

    def lane_body(cb, carry):
        lanes = pl.ds(pl.multiple_of(cb * LANES, LANES), LANES)
        for r0 in range(0, n_rows, rb):
            acc = None
            for k in range(taps):
                off = (taps - 1 - k) if transposed else (k - (taps - 1))
                term = w_ref[pl.ds(k, 1), lanes] * win[pl.ds(base + r0 + off, rb), lanes]
                acc = term if acc is None else acc + term
            if bias_ref is not None:
                acc = acc + bias_ref[:, lanes]
            out[pl.ds(r0, rb), lanes] = acc.astype(out.dtype)
        return carry

    lax.fori_loop(0, width // LANES, lane_body, 0)


def _conv_wgrad(dy, win, dw_acc, *, taps, n_rows, base, width):
    rb = min(64, n_rows)

    def lane_body(cb, carry):
        lanes = pl.ds(pl.multiple_of(cb * LANES, LANES), LANES)
        for k in range(taps):
            acc = None
            for r0 in range(0, n_rows, rb):
                prod = dy[pl.ds(r0, rb), lanes] * win[pl.ds(base + r0 + k - (taps - 1), rb), lanes]
                acc = prod if acc is None else acc + prod
            dw_acc[pl.ds(k * SUB, SUB), lanes] += _rows8(acc)
        return carry

    lax.fori_loop(0, width // LANES, lane_body, 0)


def _fold8(acc_ref, taps):
    return jnp.concatenate(
        [jnp.sum(acc_ref[pl.ds(k * SUB, SUB), :], axis=0, keepdims=True) for k in range(taps)], axis=0)


def _seq_tile(s_len):
    return min(256, s_len)


def _mm_tile(s_len):
    return min(512, s_len)


def _ff_chunk(ff):
    best = LANES
    for c in range(LANES, 1408 + 1, LANES):
        if ff % c == 0:
            best = c
    return best


def _col_tile(n):
    for c in (512, 1408, 256, LANES):
        if n % c == 0:
            return c
    return n


def _rms_matmul(x, g, w, b, *, name, dep=None):
    s_len, d = x.shape
    n = w.shape[1]
    tm = _mm_tile(s_len)
    cn = _col_tile(n)
    has_bias = b is not None

    def body(*refs):
        x_ref, g_ref, w_ref = refs[0:3]
        b_ref = refs[3] if has_bias else None
        o_ref, h_ref = refs[-2:]

        def blk(rows):
            xv = x_ref[rows, :]
            r = lax.rsqrt(jnp.mean(xv * xv, axis=-1, keepdims=True) + EPS)
            h_ref[rows, :] = ((xv * r) * g_ref[...]).astype(BF16)
        _row_loop(tm, 128, blk)

        def chunk(j, carry):
            cols = pl.ds(pl.multiple_of(j * cn, cn), cn)
            acc = _dot(h_ref[...], w_ref[:, cols])
            if has_bias:
                acc = acc + b_ref[:, cols]
            o_ref[:, cols] = acc.astype(BF16)
            return carry
        lax.fori_loop(0, n // cn, chunk, 0)

    in_specs = [pl.BlockSpec((tm, d), lambda i: (i, 0)), _resident((1, d), lambda i: (0, 0)),
                _resident((d, n), lambda i: (0, 0))]
    args = [x, g, w]
    if has_bias:
        in_specs.append(_resident((1, n), lambda i: (0, 0)))
        args.append(b)
    in_specs.append(ANY)
    args.append(x if dep is None else dep)
    return pl.pallas_call(
        body, grid=(s_len // tm,), in_specs=in_specs,
        out_specs=[pl.BlockSpec((tm, n), lambda i: (i, 0)), pl.BlockSpec((tm, d), lambda i: (i, 0))],
        out_shape=[jax.ShapeDtypeStruct((s_len, n), BF16), jax.ShapeDtypeStruct((s_len, d), BF16)],
        compiler_params=_params("parallel"), name=name,
    )(*args)


def _mix_windows(u_ref, uh_ref, gw, pw, first, t):
    c = D_CONF
    uh = uh_ref[...].astype(F32)
    gw[0:HALO, :] = jnp.where(first, 0.0, uh[:, 0:c] * _sigmoid(uh[:, c:2 * c]))
    pw[0:HALO3, :] = jnp.where(first, 0.0, uh[HALO - HALO3:HALO, 3 * c:4 * c] * uh[HALO - HALO3:HALO, 4 * c:5 * c])

    def blk(rows):
        dst = pl.ds(pl.multiple_of(rows.start + HALO, SUB), rows.size)
        gw[dst, :] = u_ref[rows, 0:c].astype(F32) * _sigmoid(u_ref[rows, c:2 * c].astype(F32))
        dst3 = pl.ds(pl.multiple_of(rows.start + HALO3, SUB), rows.size)
        pw[dst3, :] = u_ref[rows, 3 * c:4 * c].astype(F32) * u_ref[rows, 4 * c:5 * c].astype(F32)
    _row_loop(t, 64, blk)


def _mix_fwd(u, x0, wa, ba, lg, lb, wb, w_out, *, name):
    s_len, d_in = u.shape
    d = x0.shape[1]
    c = D_CONF
    t = _seq_tile(s_len)
    per = t // HALO

    def body(u_ref, uh_ref, x0_ref, wa_ref, ba_ref, lg_ref, lb_ref, wb_ref, wo_ref, y_ref, x1_ref, ca,
             gw, pw, cb):
        first = pl.program_id(0) == 0
        _mix_windows(u_ref, uh_ref, gw, pw, first, t)
        _conv_taps(gw, wa_ref, ca, taps=CONF_K, n_rows=t, base=HALO, width=c, bias_ref=ba_ref)
        _conv_taps(pw, wb_ref, cb, taps=SHORT_K, n_rows=t, base=HALO3, width=c)

        def blk(rows):
            cv = ca[rows, :]
            mu = jnp.mean(cv, axis=-1, keepdims=True)
            xc = cv - mu
            var = jnp.mean(xc * xc, axis=-1, keepdims=True)
            ln = (xc * lax.rsqrt(var + EPS)) * lg_ref[...] + lb_ref[...]
            y_ref[rows, 0:c] = (ln * _sigmoid(ln)).astype(BF16)
            y_ref[rows, c:2 * c] = (u_ref[rows, 2 * c:3 * c].astype(F32) * cb[rows, :]).astype(BF16)
        _row_loop(t, 64, blk)
        x1_ref[...] = x0_ref[...] + _dot(y_ref[...], wo_ref[...])

    small = lambda r: _resident((r, c), lambda i: (0, 0))
    return pl.pallas_call(
        body, grid=(s_len // t,),
        in_specs=[pl.BlockSpec((t, d_in), lambda i: (i, 0)),
                  pl.BlockSpec((HALO, d_in), lambda i: (jnp.maximum(i * per - 1, 0), 0)),
                  pl.BlockSpec((t, d), lambda i: (i, 0)),
                  small(CONF_K), small(1), small(1), small(1), small(SHORT_K),
                  _resident((2 * c, d), lambda i: (0, 0))],
        out_specs=[pl.BlockSpec((t, 2 * c), lambda i: (i, 0)), pl.BlockSpec((t, d), lambda i: (i, 0)),
                   pl.BlockSpec((t, c), lambda i: (i, 0))],
        out_shape=[jax.ShapeDtypeStruct((s_len, 2 * c), BF16), jax.ShapeDtypeStruct((s_len, d), F32),
                   jax.ShapeDtypeStruct((s_len, c), F32)],
        scratch_shapes=[pltpu.VMEM((HALO + t, c), F32), pltpu.VMEM((HALO3 + t, c), F32), pltpu.VMEM((t, c), F32)],
        compiler_params=_params("arbitrary"), name=name,
    )(u, u, x0, wa, ba, lg, lb, wb, w_out)


def _ffn_windows(ug_ref, ugh_ref, uv_ref, uvh_ref, gwin, vwin, first, t):
    lo = HALO3_BLK - HALO3
    gwin[0:HALO3, :] = jnp.where(first, 0.0, ugh_ref[...].astype(F32)[lo:HALO3_BLK])
    vwin[0:HALO3, :] = jnp.where(first, 0.0, uvh_ref[...].astype(F32)[lo:HALO3_BLK])

    def blk(rows):
        dst = pl.ds(pl.multiple_of(rows.start + HALO3, SUB), rows.size)
        gwin[dst, :] = ug_ref[rows, :].astype(F32)
        vwin[dst, :] = uv_ref[rows, :].astype(F32)
    _row_loop(t, 64, blk)


def _ffn_fwd(uf, x1, wf, w_down, *, name, dep=None):
    s_len, ff2 = uf.shape
    ff = ff2 // 2
    d = x1.shape[1]
    t = _seq_tile(s_len)
    fc = _ff_chunk(ff)
    nc = ff // fc
    per = t // HALO3_BLK

    def body(ug_ref, ugh_ref, uv_ref, uvh_ref, x1_ref, wfg_ref, wfv_ref, wd_ref, dep_ref, act_ref, x2_ref,
             gwin, vwin, cg, cv):
        first = pl.program_id(0) == 0
        _ffn_windows(ug_ref, ugh_ref, uv_ref, uvh_ref, gwin, vwin, first, t)
        _conv_taps(gwin, wfg_ref, cg, taps=SHORT_K, n_rows=t, base=HALO3, width=fc)
        _conv_taps(vwin, wfv_ref, cv, taps=SHORT_K, n_rows=t, base=HALO3, width=fc)

        def blk(rows):
            gv = cg[rows, :]
            act_ref[rows, :] = ((gv * _sigmoid(gv)) * cv[rows, :]).astype(BF16)
        _row_loop(t, 32, blk, unroll=2)

        @pl.when(pl.program_id(1) == 0)
        def _():
            x2_ref[...] = x1_ref[...]
        x2_ref[...] += _dot(act_ref[...], wd_ref[...])

    halo_map = lambda off: (lambda i, j: (jnp.maximum(i * per - 1, 0), j + off))
    return pl.pallas_call(
        body, grid=(s_len // t, nc),
        in_specs=[pl.BlockSpec((t, fc), lambda i, j: (i, j)), pl.BlockSpec((HALO3_BLK, fc), halo_map(0)),
                  pl.BlockSpec((t, fc), lambda i, j: (i, j + nc)), pl.BlockSpec((HALO3_BLK, fc), halo_map(nc)),
                  pl.BlockSpec((t, d), lambda i, j: (i, 0)),
                  pl.BlockSpec((SHORT_K, fc), lambda i, j: (0, j)),
                  pl.BlockSpec((SHORT_K, fc), lambda i, j: (0, j + nc)),
                  pl.BlockSpec((fc, d), lambda i, j: (j, 0)), ANY],
        out_specs=[pl.BlockSpec((t, fc), lambda i, j: (i, j)), pl.BlockSpec((t, d), lambda i, j: (i, 0))],
        out_shape=[jax.ShapeDtypeStruct((s_len, ff), BF16), jax.ShapeDtypeStruct((s_len, d), F32)],
        scratch_shapes=[pltpu.VMEM((HALO3 + t, fc), F32), pltpu.VMEM((HALO3 + t, fc), F32),
                        pltpu.VMEM((t, fc), F32), pltpu.VMEM((t, fc), F32)],
        compiler_params=_params("parallel", "arbitrary"), name=name,
    )(uf, uf, uf, uf, x1, wf, wf, w_down, uf if dep is None else dep)


def _loss_bwd(x, g, target, *, name):
    s_len, d = x.shape
    t = _seq_tile(s_len)

    def body(x_ref, g_ref, t_ref, l_ref, dx_ref, dxb_ref, dg_ref):
        @pl.when(pl.program_id(0) == 0)
        def _():
            l_ref[...] = jnp.zeros_like(l_ref)
            dg_ref[...] = jnp.zeros_like(dg_ref)

        def blk(rows):
            xv = x_ref[rows, :]
            r = lax.rsqrt(jnp.mean(xv * xv, axis=-1, keepdims=True) + EPS)
            xn = xv * r
            e = xn * g_ref[...] - t_ref[rows, :]
            l_ref[...] += _rows8(e * e)
            dy = e * (1.0 / d)
            dg_ref[...] += _rows8(dy * xn)
            dn = dy * g_ref[...]
            dx = r * (dn - xn * jnp.mean(dn * xn, axis=-1, keepdims=True))
            dx_ref[rows, :] = dx
            dxb_ref[rows, :] = dx.astype(BF16)
        _row_loop(t, 64, blk)

    row = pl.BlockSpec((t, d), lambda i: (i, 0))
    part = pl.BlockSpec((SUB, d), lambda i: (0, 0))
    return pl.pallas_call(
        body, grid=(s_len // t,),
        in_specs=[row, _resident((1, d), lambda i: (0, 0)), row],
        out_specs=[part, row, row, part],
        out_shape=[jax.ShapeDtypeStruct((SUB, d), F32), jax.ShapeDtypeStruct((s_len, d), F32),
                   jax.ShapeDtypeStruct((s_len, d), BF16), jax.ShapeDtypeStruct((SUB, d), F32)],
        compiler_params=_params("arbitrary"), name=name,
    )(x, g, target)


def _ffn_bwd(dx2, uf, wf, w_down_t, *, name, dep=None):
    s_len, ff2 = uf.shape
    ff = ff2 // 2
    d = dx2.shape[1]
    t = _seq_tile(s_len)
    n_t = s_len // t
    fc = _ff_chunk(ff)
    nc = ff // fc
    per = t // HALO3_BLK

    def body(dx_ref, ug_ref, ugh_ref, uv_ref, uvh_ref, wfg_ref, wfv_ref, wd_ref, dep_ref,
             dug_ref, duv_ref, dwg_ref, dwv_ref, gwin, vwin, cg, cv, dact, dgw, dvw, awg, awv):
        i = pl.program_id(1)
        first = i == n_t - 1
        _ffn_windows(ug_ref, ugh_ref, uv_ref, uvh_ref, gwin, vwin, first, t)
        _conv_taps(gwin, wfg_ref, cg, taps=SHORT_K, n_rows=t, base=HALO3, width=fc)
        _conv_taps(vwin, wfv_ref, cv, taps=SHORT_K, n_rows=t, base=HALO3, width=fc)
        dact[...] = _dot(dx_ref[...], wd_ref[...])

        @pl.when(i == 0)
        def _():
            dgw[t:t + HALO3, :] = jnp.zeros((HALO3, fc), F32)
            dvw[t:t + HALO3, :] = jnp.zeros((HALO3, fc), F32)
            awg[...] = jnp.zeros_like(awg)
            awv[...] = jnp.zeros_like(awv)

        def blk(rows):
            gv = cg[rows, :]
            sg = _sigmoid(gv)
            da = dact[rows, :]
            dgw[rows, :] = (da * cv[rows, :]) * (sg * (1.0 + gv * (1.0 - sg)))
            dvw[rows, :] = da * (gv * sg)
        _row_loop(t, 32, blk, unroll=2)

        _conv_taps(dgw, wfg_ref, dug_ref, taps=SHORT_K, n_rows=t, base=0, width=fc, transposed=True)
        _conv_taps(dvw, wfv_ref, duv_ref, taps=SHORT_K, n_rows=t, base=0, width=fc, transposed=True)
        _conv_wgrad(dgw, gwin, awg, taps=SHORT_K, n_rows=t, base=HALO3, width=fc)
        _conv_wgrad(dvw, vwin, awv, taps=SHORT_K, n_rows=t, base=HALO3, width=fc)
        dgw[t:t + HALO3, :] = dgw[0:HALO3, :]
        dvw[t:t + HALO3, :] = dvw[0:HALO3, :]

        @pl.when(i == n_t - 1)
        def _():
            dwg_ref[...] = _fold8(awg, SHORT_K)
            dwv_ref[...] = _fold8(awv, SHORT_K)

    rev = lambda i: n_t - 1 - i
    halo_map = lambda off: (lambda j, i: (jnp.maximum(rev(i) * per - 1, 0), j + off))
    return pl.pallas_call(
        body, grid=(nc, n_t),
        in_specs=[pl.BlockSpec((t, d), lambda j, i: (rev(i), 0)),
                  pl.BlockSpec((t, fc), lambda j, i: (rev(i), j)), pl.BlockSpec((HALO3_BLK, fc), halo_map(0)),
                  pl.BlockSpec((t, fc), lambda j, i: (rev(i), j + nc)), pl.BlockSpec((HALO3_BLK, fc), halo_map(nc)),
                  pl.BlockSpec((SHORT_K, fc), lambda j, i: (0, j)),
                  pl.BlockSpec((SHORT_K, fc), lambda j, i: (0, j + nc)),
                  pl.BlockSpec((d, fc), lambda j, i: (0, j)), ANY],
        out_specs=[pl.BlockSpec((t, fc), lambda j, i: (rev(i), j)), pl.BlockSpec((t, fc), lambda j, i: (rev(i), j)),
                   pl.BlockSpec((SHORT_K, fc), lambda j, i: (0, j)), pl.BlockSpec((SHORT_K, fc), lambda j, i: (0, j))],
        out_shape=[jax.ShapeDtypeStruct((s_len, ff), BF16), jax.ShapeDtypeStruct((s_len, ff), BF16),
                   jax.ShapeDtypeStruct((SHORT_K, ff), F32), jax.ShapeDtypeStruct((SHORT_K, ff), F32)],
        scratch_shapes=[pltpu.VMEM((HALO3 + t, fc), F32), pltpu.VMEM((HALO3 + t, fc), F32),
                        pltpu.VMEM((t, fc), F32), pltpu.VMEM((t, fc), F32), pltpu.VMEM((t, fc), F32),
                        pltpu.VMEM((t + HALO3, fc), F32), pltpu.VMEM((t + HALO3, fc), F32),
                        pltpu.VMEM((SHORT_K * SUB, fc), F32), pltpu.VMEM((SHORT_K * SUB, fc), F32)],
        compiler_params=_params("arbitrary", "arbitrary"), name=name,
    )(dx2, uf, uf, uf, uf, wf, wf, w_down_t, uf if dep is None else dep)


def _mix_bwd(dx1, u, ca, wa, lg, lb, wb, w_out_t, *, name, dep=None):
    s_len, d_in = u.shape
    d = dx1.shape[1]
    c = D_CONF
    t = _seq_tile(s_len)
    n_t = s_len // t
    per = t // HALO

    def body(dx_ref, u_ref, uh_ref, ca_ref, wa_ref, lg_ref, lb_ref, wb_ref, wo_ref, dep_ref,
             du_ref, dwa_ref, dwb_ref, dba_ref, dlg_ref, dlb_ref, dbin_ref,
             gw, pw, cb, dyc, dcaw, dcbw, dglu, dp, awa, awb):
        i = pl.program_id(0)
        first = i == n_t - 1
        _mix_windows(u_ref, uh_ref, gw, pw, first, t)
        _conv_taps(pw, wb_ref, cb, taps=SHORT_K, n_rows=t, base=HALO3, width=c)
        dyc[...] = _dot(dx_ref[...], wo_ref[...])

        @pl.when(i == 0)
        def _():
            dcaw[t:t + HALO, :] = jnp.zeros((HALO, c), F32)
            dcbw[t:t + HALO3, :] = jnp.zeros((HALO3, c), F32)
            awa[...] = jnp.zeros_like(awa)
            awb[...] = jnp.zeros_like(awb)
            dba_ref[...] = jnp.zeros_like(dba_ref)
            dlg_ref[...] = jnp.zeros_like(dlg_ref)
            dlb_ref[...] = jnp.zeros_like(dlb_ref)
            dbin_ref[...] = jnp.zeros_like(dbin_ref)

        def blk1(rows):
            cv = ca_ref[rows, :]
            mu = jnp.mean(cv, axis=-1, keepdims=True)
            xc = cv - mu
            rstd = lax.rsqrt(jnp.mean(xc * xc, axis=-1, keepdims=True) + EPS)
            nrm = xc * rstd
            ln = nrm * lg_ref[...] + lb_ref[...]
            sg = _sigmoid(ln)
            dln = dyc[rows, 0:c] * (sg * (1.0 + ln * (1.0 - sg)))
            dlg_ref[...] += _rows8(dln * nrm)
            dlb_ref[...] += _rows8(dln)
            dn = dln * lg_ref[...]
            dca = rstd * (dn - jnp.mean(dn, axis=-1, keepdims=True)
                          - nrm * jnp.mean(dn * nrm, axis=-1, keepdims=True))
            dcaw[rows, :] = dca
            dba_ref[...] += _rows8(dca)
            ds = dyc[rows, c:2 * c]
            dgb = ds * cb[rows, :]
            dcbw[rows, :] = ds * u_ref[rows, 2 * c:3 * c].astype(F32)
            du_ref[rows, 2 * c:3 * c] = dgb.astype(BF16)
            dbin_ref[:, 2 * c:3 * c] += _rows8(dgb)
        _row_loop(t, 64, blk1, unroll=2)

        _conv_taps(dcaw, wa_ref, dglu, taps=CONF_K, n_rows=t, base=0, width=c, transposed=True)
        _conv_taps(dcbw, wb_ref, dp, taps=SHORT_K, n_rows=t, base=0, width=c, transposed=True)
        _conv_wgrad(dcaw, gw, awa, taps=CONF_K, n_rows=t, base=HALO, width=c)
        _conv_wgrad(dcbw, pw, awb, taps=SHORT_K, n_rows=t, base=HALO3, width=c)
        dcaw[t:t + HALO, :] = dcaw[0:HALO, :]
        dcbw[t:t + HALO3, :] = dcbw[0:HALO3, :]

        def blk2(rows):
            av = u_ref[rows, 0:c].astype(F32)
            sg = _sigmoid(u_ref[rows, c:2 * c].astype(F32))
            dg = dglu[rows, :]
            d_av = dg * sg
            d_ag = (dg * av) * (sg * (1.0 - sg))
            dpv = dp[rows, :]
            d_gc = dpv * u_ref[rows, 4 * c:5 * c].astype(F32)
            d_vs = dpv * u_ref[rows, 3 * c:4 * c].astype(F32)
            du_ref[rows, 0:c] = d_av.astype(BF16)
            du_ref[rows, c:2 * c] = d_ag.astype(BF16)
            du_ref[rows, 3 * c:4 * c] = d_gc.astype(BF16)
            du_ref[rows, 4 * c:5 * c] = d_vs.astype(BF16)
            dbin_ref[:, 0:c] += _rows8(d_av)
            dbin_ref[:, c:2 * c] += _rows8(d_ag)
            dbin_ref[:, 3 * c:4 * c] += _rows8(d_gc)
            dbin_ref[:, 4 * c:5 * c] += _rows8(d_vs)
        _row_loop(t, 64, blk2)

        @pl.when(i == n_t - 1)
        def _():
            dwa_ref[...] = _fold8(awa, CONF_K)
            dwb_ref[...] = _fold8(awb, SHORT_K)

    rev = lambda i: n_t - 1 - i
    small_in = lambda r: _resident((r, c), lambda i: (0, 0))
    small = lambda r: pl.BlockSpec((r, c), lambda i: (0, 0))
    return pl.pallas_call(
        body, grid=(n_t,),
        in_specs=[pl.BlockSpec((t, d), lambda i: (rev(i), 0)),
                  pl.BlockSpec((t, d_in), lambda i: (rev(i), 0)),
                  pl.BlockSpec((HALO, d_in), lambda i: (jnp.maximum(rev(i) * per - 1, 0), 0)),
                  pl.BlockSpec((t, c), lambda i: (rev(i), 0)),
                  small_in(CONF_K), small_in(1), small_in(1), small_in(SHORT_K),
                  _resident((d, 2 * c), lambda i: (0, 0)), ANY],
        out_specs=[pl.BlockSpec((t, d_in), lambda i: (rev(i), 0)),
                   small(CONF_K), small(SHORT_K), small(SUB), small(SUB), small(SUB),
                   pl.BlockSpec((SUB, d_in), lambda i: (0, 0))],
        out_shape=[jax.ShapeDtypeStruct((s_len, d_in), BF16),
                   jax.ShapeDtypeStruct((CONF_K, c), F32), jax.ShapeDtypeStruct((SHORT_K, c), F32),
                   jax.ShapeDtypeStruct((SUB, c), F32), jax.ShapeDtypeStruct((SUB, c), F32),
                   jax.ShapeDtypeStruct((SUB, c), F32), jax.ShapeDtypeStruct((SUB, d_in), F32)],
        scratch_shapes=[pltpu.VMEM((HALO + t, c), F32), pltpu.VMEM((HALO3 + t, c), F32),
                        pltpu.VMEM((t, c), F32), pltpu.VMEM((t, 2 * c), F32),
                        pltpu.VMEM((t + HALO, c), F32), pltpu.VMEM((t + HALO3, c), F32),
                        pltpu.VMEM((t, c), F32), pltpu.VMEM((t, c), F32),
                        pltpu.VMEM((CONF_K * SUB, c), F32), pltpu.VMEM((SHORT_K * SUB, c), F32)],
        compiler_params=_params("arbitrary"), name=name,
    )(dx1, u, u, ca, wa, lg, lb, wb, w_out_t, u if dep is None else dep)


def _matmul_tn(a, b, *, name, into=None, part=0, n_parts=1):
    s_len, k = a.shape
    n = b.shape[1]
    tk = _col_tile(k)
    per = k // tk

    def body(*refs):
        a_ref, b_ref = refs[0], refs[1]
        o_ref = refs[-1]
        o_ref[...] = _dot_tn(a_ref[...], b_ref[...]).astype(BF16)

    in_specs = [pl.BlockSpec((s_len, tk), lambda j: (0, j)), _resident((s_len, n), lambda j: (0, 0))]
    args = [a, b]
    aliases = {}
    if into is not None:
        in_specs.append(ANY)
        args.append(into)
        aliases = {2: 0}
    return pl.pallas_call(
        body, grid=(per,), in_specs=in_specs,
        out_specs=pl.BlockSpec((tk, n), lambda j: (part * per + j, 0)),
        out_shape=jax.ShapeDtypeStruct((n_parts * k, n), BF16),
        input_output_aliases=aliases,
        compiler_params=_params("parallel"), name=name,
    )(*args)


def _matmul_rmsbwd(dzs, wt, x, g, dx_in, *, name):
    s_len, d = x.shape
    n_z = len(dzs)
    nj = dzs[0].shape[1]
    t = _mm_tile(s_len)

    def body(*refs):
        dz_refs = refs[0:n_z]
        w_refs = refs[n_z:2 * n_z]
        x_ref, g_ref, dxi_ref, dx_ref, dxb_ref, dg_ref, dh = refs[2 * n_z:]

        @pl.when(pl.program_id(0) == 0)
        def _():
            dg_ref[...] = jnp.zeros_like(dg_ref)

        acc = _dot(dz_refs[0][...], w_refs[0][...])
        for q in range(1, n_z):
            acc = acc + _dot(dz_refs[q][...], w_refs[q][...])
        dh[...] = acc

        def blk(rows):
            xv = x_ref[rows, :]
            r = lax.rsqrt(jnp.mean(xv * xv, axis=-1, keepdims=True) + EPS)
            xn = xv * r
            dhv = dh[rows, :]
            dg_ref[...] += _rows8(dhv * xn)
            dn = dhv * g_ref[...]
            dx = dxi_ref[rows, :] + r * (dn - xn * jnp.mean(dn * xn, axis=-1, keepdims=True))
            dx_ref[rows, :] = dx
            dxb_ref[rows, :] = dx.astype(BF16)
        _row_loop(t, 128, blk)

    row = pl.BlockSpec((t, d), lambda i: (i, 0))
    in_specs = [pl.BlockSpec((t, nj), lambda i: (i, 0)) for _ in range(n_z)]
    in_specs += [_resident((nj, d), functools.partial(lambda q, i: (q, 0), q)) for q in range(n_z)]
    in_specs += [row, _resident((1, d), lambda i: (0, 0)), row]
    return pl.pallas_call(
        body, grid=(s_len // t,), in_specs=in_specs,
        out_specs=[row, row, pl.BlockSpec((SUB, d), lambda i: (0, 0))],
        out_shape=[jax.ShapeDtypeStruct((s_len, d), F32), jax.ShapeDtypeStruct((s_len, d), BF16),
                   jax.ShapeDtypeStruct((SUB, d), F32)],
        scratch_shapes=[pltpu.VMEM((t, d), F32)],
        compiler_params=_params("arbitrary"), name=name,
    )(*dzs, *([wt] * n_z), x, g, dx_in)


def _row(v):
    return v.reshape(1, -1)


def _layer_fwd(x0, p, tag, dep=None, before_ffn=None):
    u, h1 = _rms_matmul(x0, _row(p["mix_norm_g"]), p["w_in"], _row(p["b_in"]), name=f"in_proj_{tag}", dep=dep)
    ycat, x1, ca = _mix_fwd(u, x0, p["conv_a_w"], _row(p["conv_a_b"]), _row(p["ln_a_g"]), _row(p["ln_a_b"]),
                            p["conv_b_w"], p["w_out"], name=f"mix_fwd_{tag}")
    uf, h2 = _rms_matmul(x1, _row(p["ffn_norm_g"]), p["w_up"], None, name=f"up_proj_{tag}")
    dep_ffn = None if before_ffn is None else before_ffn(uf)
    act, x2 = _ffn_fwd(uf, x1, p["conv_f_w"], p["w_down"], name=f"ffn_fwd_{tag}", dep=dep_ffn)
    return x2, dict(x0=x0, h1=h1, u=u, ca=ca, ycat=ycat, x1=x1, h2=h2, uf=uf, act=act)


def _layer_bwd(dx2, dx2_b, p, saved, tag, dep=None, after_ffn=None):
    dug, duv, dwf_g, dwf_v = _ffn_bwd(dx2_b, saved["uf"], p["conv_f_w"], p["w_down_t"], name=f"ffn_bwd_{tag}",
                                      dep=dep)
    g_down = _matmul_tn(saved["act"], dx2_b, name=f"dw_down_{tag}")
    g_up = _matmul_tn(dug, saved["h2"], name=f"dw_up_g_{tag}", n_parts=2)
    g_up = _matmul_tn(duv, saved["h2"], name=f"dw_up_v_{tag}", into=g_up, part=1, n_parts=2)
    dx1, dx1_b, dg2 = _matmul_rmsbwd([dug, duv], p["w_up_t"], saved["x1"], _row(p["ffn_norm_g"]), dx2,
                                     name=f"dh_ffn_{tag}")
    dep_mix = None if after_ffn is None else after_ffn(dict(w_up=g_up, w_down=g_down))
    du, dwa, dwb, dba, dlg, dlb, dbin = _mix_bwd(
        dx1_b, saved["u"], saved["ca"], p["conv_a_w"], _row(p["ln_a_g"]), _row(p["ln_a_b"]),
        p["conv_b_w"], p["w_out_t"], name=f"mix_bwd_{tag}", dep=dep_mix)
    g_out = _matmul_tn(saved["ycat"], dx1_b, name=f"dw_out_{tag}")
    g_in = _matmul_tn(du, saved["h1"], name=f"dw_in_{tag}")
    dx0, dx0_b, dg1 = _matmul_rmsbwd([du], p["w_in_t"], saved["x0"], _row(p["mix_norm_g"]), dx1,
                                     name=f"dh_mix_{tag}")
    big = dict(w_in=g_in, w_out=g_out, w_up=g_up, w_down=g_down)
    conv = dict(conv_a_w=dwa, conv_b_w=dwb, conv_f_w=jnp.concatenate([dwf_g, dwf_v], axis=1))
    rep = dict(mix_norm_g=dg1, b_in=dbin, conv_a_b=dba, ln_a_g=dlg, ln_a_b=dlb, ffn_norm_g=dg2)
    return dx0, dx0_b, big, conv, rep


def _place():
    return lax.axis_index("x"), lax.axis_index("y"), lax.axis_index("c")


def _all_gather(arrs, *, name):
    n_a = len(arrs)

    def body(*refs):
        ins = refs[0:n_a]
        outs = refs[n_a:2 * n_a]
        send_sems, recv_sems, local_sems = refs[2 * n_a:]
        x, y, c = _place()
        sibling = (x, y, 1 - c)
        chips = [(1 - x, y), (x, 1 - y), (1 - x, 1 - y)]

        def slot(a, px, py, pc):
            return outs[a].at[4 * px + 2 * py + pc]

        def copy(a, k, block, to, src=None):
            return pltpu.make_async_remote_copy(
                src_ref=slot(a, *block) if src is None else src, dst_ref=slot(a, *block),
                send_sem=send_sems.at[a, k], recv_sem=recv_sems.at[a, k],
                device_id=to, device_id_type=MESH)

        me = (x, y, c)
        mine = [pltpu.make_async_copy(ins[a], slot(a, *me), local_sems.at[a]) for a in range(n_a)]
        for cp in mine:
            cp.start()
        started = []
        for a in range(n_a):
            first = [copy(a, 0, me, sibling, src=ins[a])]
            first += [copy(a, 1 + j, me, (*chip, c), src=ins[a]) for j, chip in enumerate(chips)]
            for cp in first:
                cp.start()
            started += first
        for a in range(n_a):
            for j, chip in enumerate(chips):
                copy(a, 1 + j, (*chip, c), me).wait_recv()
                passed = copy(a, 4 + j, (*chip, c), sibling)
                passed.start()
                started.append(passed)
        for a in range(n_a):
            copy(a, 0, sibling, me).wait_recv()
            for j, chip in enumerate(chips):
                copy(a, 4 + j, (*chip, 1 - c), me).wait_recv()
        for cp in started:
            cp.wait_send()
        for cp in mine:
            cp.wait()

    return pl.pallas_call(
        body, in_specs=[ANY] * n_a, out_specs=[ANY] * n_a,
        out_shape=[jax.ShapeDtypeStruct((N_DEV, *a.shape), a.dtype) for a in arrs],
        scratch_shapes=[pltpu.SemaphoreType.DMA((n_a, 7)), pltpu.SemaphoreType.DMA((n_a, 7)),
                        pltpu.SemaphoreType.DMA((n_a,))],
        name=name,
    )(*arrs)


def _sibling_exchange(arrs, *, name):
    n_a = len(arrs)

    def body(*refs):
        ins = refs[0:n_a]
        outs = refs[n_a:2 * n_a]
        send_sems, recv_sems = refs[2 * n_a:]
        x, y, c = _place()
        copies = [pltpu.make_async_remote_copy(
            src_ref=ins[a].at[:, 1 - c], dst_ref=outs[a], send_sem=send_sems.at[a], recv_sem=recv_sems.at[a],
            device_id=(x, y, 1 - c), device_id_type=MESH) for a in range(n_a)]
        for cp in copies:
            cp.start()
        for cp in copies:
            cp.wait()

    return pl.pallas_call(
        body, in_specs=[ANY] * n_a, out_specs=[ANY] * n_a,
        out_shape=[jax.ShapeDtypeStruct((N_CHIP, *a.shape[2:]), a.dtype) for a in arrs],
        scratch_shapes=[pltpu.SemaphoreType.DMA((n_a,)), pltpu.SemaphoreType.DMA((n_a,))],
        name=name,
    )(*arrs)


def _row_tile(r):
    for tr in (512, 352, 256, 128, 64, 32, 16, 8):
        if r % tr == 0:
            return tr
    return r


def _pair_sum(mine, theirs, core, *, name):
    n_chip, _, r, c = mine.shape
    tr = _row_tile(r)

    def body(core_ref, a_ref, b_ref, o_ref):
        o_ref[...] = (a_ref[...].astype(F32) + b_ref[...].astype(F32)).astype(o_ref.dtype)

    return pl.pallas_call(
        body,
        grid_spec=pltpu.PrefetchScalarGridSpec(
            num_scalar_prefetch=1, grid=(n_chip, r // tr),
            in_specs=[pl.BlockSpec((None, None, tr, c), lambda q, i, core_ref: (q, core_ref[0], i, 0)),
                      pl.BlockSpec((None, tr, c), lambda q, i, core_ref: (q, i, 0))],
            out_specs=pl.BlockSpec((None, tr, c), lambda q, i, core_ref: (q, i, 0))),
        out_shape=jax.ShapeDtypeStruct((n_chip, r, c), mine.dtype),
        compiler_params=_params("parallel", "parallel"), name=name,
    )(core, mine, theirs)


def _chip_exchange(arrs, *, name):
    n_a = len(arrs)

    def body(*refs):
        ins = refs[0:n_a]
        outs = refs[n_a:2 * n_a]
        send_sems, recv_sems, local_sems = refs[2 * n_a:]
        x, y, c = _place()
        my_chip = 2 * x + y
        chips = [(1 - x, y), (x, 1 - y), (1 - x, 1 - y)]
        mine = [pltpu.make_async_copy(ins[a].at[my_chip], outs[a].at[my_chip], local_sems.at[a]) for a in range(n_a)]
        for cp in mine:
            cp.start()
        copies = []
        for a in range(n_a):
            for j, (px, py) in enumerate(chips):
                copies.append(pltpu.make_async_remote_copy(
                    src_ref=ins[a].at[2 * px + py], dst_ref=outs[a].at[my_chip],
                    send_sem=send_sems.at[a, j], recv_sem=recv_sems.at[a, j],
                    device_id=(px, py, c), device_id_type=MESH))
        for cp in copies:
            cp.start()
        for cp in copies:
            cp.wait()
        for cp in mine:
            cp.wait()

    return pl.pallas_call(
        body, in_specs=[ANY] * n_a, out_specs=[ANY] * n_a,
        out_shape=[jax.ShapeDtypeStruct(a.shape, a.dtype) for a in arrs],
        scratch_shapes=[pltpu.SemaphoreType.DMA((n_a, 3)), pltpu.SemaphoreType.DMA((n_a, 3)),
                        pltpu.SemaphoreType.DMA((n_a,))],
        name=name,
    )(*arrs)


HBM = pl.BlockSpec(memory_space=pltpu.HBM)
SEM = pl.BlockSpec(memory_space=pltpu.SEMAPHORE)
EFFECT = pltpu.SideEffectType.DATAFLOW_SIDE_EFFECTING


def _in_hbm(a):
    return pltpu.with_memory_space_constraint(a, pltpu.HBM)


def _split_start(srcs, lands, plan, n_copies, *, name):
    n_s, n_l = len(srcs), len(lands)

    def body(*refs):
        src_refs = refs[0:n_s]
        land_refs = refs[n_s:n_s + n_l]
        send_sems, recv_sems = refs[n_s + n_l], refs[n_s + n_l + 1]
        token = refs[-1]
        for cp in plan(src_refs, land_refs, send_sems, recv_sems):
            cp.start()
        token[...] = jnp.zeros_like(token)

    thru = [pltpu.HBM(a.shape, a.dtype) for a in list(srcs) + list(lands)]
    res = pl.pallas_call(
        body, name=name,
        out_shape=(pltpu.SemaphoreType.DMA((n_copies,)), pltpu.SemaphoreType.DMA((n_copies,)), *thru,
                   jax.ShapeDtypeStruct((SUB, LANES), F32)),
        in_specs=[HBM] * (n_s + n_l),
        out_specs=(SEM, SEM, *([HBM] * (n_s + n_l)), pl.BlockSpec(memory_space=pltpu.VMEM)),
        input_output_aliases={i: 2 + i for i in range(n_s + n_l)},
        compiler_params=pltpu.CompilerParams(has_side_effects=EFFECT),
    )(*[_in_hbm(a) for a in srcs], *[_in_hbm(a) for a in lands])
    return res[0], res[1], list(res[2:2 + n_s]), list(res[2 + n_s:2 + n_s + n_l]), res[-1]


def _split_wait(send_sems, recv_sems, srcs, lands, after, plan, *, name):
    n_s, n_l = len(srcs), len(lands)

    def body(*refs):
        src_refs = refs[0:n_s]
        land_refs = refs[n_s:n_s + n_l]
        send, recv = refs[n_s + n_l], refs[n_s + n_l + 1]
        for cp in plan(src_refs, land_refs, send, recv):
            cp.wait_send()
            cp.wait_recv()

    res = pl.pallas_call(
        body, name=name,
        out_shape=tuple(pltpu.HBM(a.shape, a.dtype) for a in list(srcs) + list(lands)),
        in_specs=[HBM] * (n_s + n_l) + [SEM, SEM, ANY],
        out_specs=tuple([HBM] * (n_s + n_l)),
        input_output_aliases={i: i for i in range(n_s + n_l)},
        compiler_params=pltpu.CompilerParams(has_side_effects=EFFECT),
    )(*srcs, *lands, send_sems, recv_sems, _in_hbm(after))
    return list(res[n_s:])


def _remote(src, dst, send_sems, recv_sems, k, to):
    return pltpu.make_async_remote_copy(src_ref=src, dst_ref=dst, send_sem=send_sems.at[k], recv_sem=recv_sems.at[k],
                                        device_id=to, device_id_type=MESH)


def _gather_plan_first(src_refs, land_refs, send_sems, recv_sems):
    x, y, c = _place()
    me = 4 * x + 2 * y + c
    peers = [(x, y, 1 - c), (1 - x, y, c), (x, 1 - y, c), (1 - x, 1 - y, c)]
    return [_remote(src, land.at[me], send_sems, recv_sems, 4 * a + k, to)
            for a, (src, land) in enumerate(zip(src_refs, land_refs)) for k, to in enumerate(peers)]


def _gather_plan_second(src_refs, land_refs, send_sems, recv_sems):
    x, y, c = _place()
    chips = [(1 - x, y), (x, 1 - y), (1 - x, 1 - y)]
    out = []
    for a, land in enumerate(land_refs):
        for j, (px, py) in enumerate(chips):
            slot = land.at[4 * px + 2 * py + c]
            out.append(_remote(slot, slot, send_sems, recv_sems, 3 * a + j, (x, y, 1 - c)))
    return out


def _chips_plan(src_refs, land_refs, send_sems, recv_sems):
    x, y, c = _place()
    my_chip = 2 * x + y
    chips = [(1 - x, y), (x, 1 - y), (1 - x, 1 - y)]
    return [_remote(src.at[2 * px + py], land.at[my_chip], send_sems, recv_sems, 3 * a + j, (px, py, c))
            for a, (src, land) in enumerate(zip(src_refs, land_refs)) for j, (px, py) in enumerate(chips)]


def _landing(like_shape, dtype, own, index):
    return lax.dynamic_update_index_in_dim(lax.empty(like_shape, dtype), own, index, 0)


def _adamw_math(g, w, m, v):
    m = ADAM_B1 * m + (1.0 - ADAM_B1) * g
    v = ADAM_B2 * v + (1.0 - ADAM_B2) * (g * g)
    m_hat = m / (1.0 - ADAM_B1 ** ADAM_STEP)
    v_hat = v / (1.0 - ADAM_B2 ** ADAM_STEP)
    delta = -ADAM_LR * (m_hat / (jnp.sqrt(v_hat) + ADAM_EPS) + ADAM_WD * w)
    return delta, m, v


def _adamw_sharded(parts, w, m, v, *, name, dep=None):
    n_layers, r, c = w.shape
    n_chip = parts[0].shape[0]
    tr = _row_tile(r)
    n_i = r // tr

    def body(*refs):
        p_refs = refs[0:n_layers]
        w_ref, m_ref, v_ref, _, g_out, d_out, m_out, v_out = refs[n_layers:]
        layer = pl.program_id(0)
        for l in range(n_layers):
            @pl.when(layer == l)
            def _(l=l):
                g = p_refs[l][0].astype(F32)
                for q in range(1, n_chip):
                    g = g + p_refs[l][q].astype(F32)
                delta, m_new, v_new = _adamw_math(g, w_ref[...], m_ref[...], v_ref[...])
                g_out[...] = g
                d_out[...] = delta
                m_out[...] = m_new
                v_out[...] = v_new

    def part_map(l):
        return lambda layer, i: (0, jnp.where(layer == l, i, jnp.where(layer < l, 0, n_i - 1)), 0)

    blk = pl.BlockSpec((None, tr, c), lambda layer, i: (layer, i, 0))
    return pl.pallas_call(
        body, grid=(n_layers, n_i),
        in_specs=[pl.BlockSpec((n_chip, tr, c), part_map(l)) for l in range(n_layers)] + [blk, blk, blk, ANY],
        out_specs=[blk] * 4, out_shape=[jax.ShapeDtypeStruct((n_layers, r, c), F32)] * 4,
        compiler_params=_params("arbitrary", "arbitrary"), name=name,
    )(*parts, w, m, v, w if dep is None else dep)


def _fold_partials(cols, *, name):
    widths = [c.shape[1] for c in cols]

    def body(*refs):
        o_ref = refs[-1]
        pos = 0
        for ref, width in zip(refs[:-1], widths):
            o_ref[:, pos:pos + width] = jnp.sum(ref[...], axis=0, keepdims=True)
            pos += width

    return pl.pallas_call(body, out_shape=jax.ShapeDtypeStruct((1, sum(widths)), F32), name=name)(*cols)


def _adamw_replicated(parts, names, w, m, v, n_loss, *, name):
    n_dev = parts.shape[0]
    n_layers = w[names[0]].shape[0]
    every = list(names) + ["final_norm_g"]
    n_p = len(every)

    def body(*refs):
        p_ref = refs[0]
        w_refs = dict(zip(every, refs[1:1 + n_p]))
        m_refs = dict(zip(every, refs[1 + n_p:1 + 2 * n_p]))
        v_refs = dict(zip(every, refs[1 + 2 * n_p:1 + 3 * n_p]))
        l_out = refs[1 + 3 * n_p]
        outs = refs[2 + 3 * n_p:]
        o_refs = {n: outs[4 * q:4 * q + 4] for q, n in enumerate(every)}
        acc = p_ref[0]
        for q in range(1, n_dev):
            acc = acc + p_ref[q]
        tot = jnp.sum(acc, axis=0, keepdims=True)
        pos = 0
        where = [(n, l) for l in range(n_layers) for n in names] + [("final_norm_g", 0)]
        for n, l in where:
            width = w_refs[n].shape[1]
            g = tot[:, pos:pos + width]
            pos += width
            row = pl.ds(l, 1)
            delta, m_new, v_new = _adamw_math(g, w_refs[n][row, :], m_refs[n][row, :], v_refs[n][row, :])
            for o, val in zip(o_refs[n], (g, delta, m_new, v_new)):
                o[row, :] = val
        l_out[...] = (0.5 / n_loss) * jnp.sum(tot[:, pos:pos + n_loss], axis=-1, keepdims=True)

    shapes = [jax.ShapeDtypeStruct((1, 1), F32)]
    for n in every:
        shapes += [jax.ShapeDtypeStruct(w[n].shape, F32)] * 4
    res = pl.pallas_call(
        body, out_shape=shapes,
        compiler_params=pltpu.CompilerParams(vmem_limit_bytes=VMEM_LIMIT), name=name,
    )(parts, *[w[n] for n in every], *[m[n] for n in every], *[v[n] for n in every])
    return res[0], {n: res[1 + 4 * q:5 + 4 * q] for q, n in enumerate(every)}


BIG = ("w_in", "w_out", "w_up", "w_down")
COL_SHARDED = ("w_in", "w_up")
CONV = ("conv_a_w", "conv_b_w", "conv_f_w")
REPLICATED = ("mix_norm_g", "b_in", "conv_a_b", "ln_a_g", "ln_a_b", "ffn_norm_g")
KINDS = ("grad", "delta", "m", "v")
FFN_PART = ("w_up", "w_down")
MIX_PART = ("w_in", "w_out")


def _weights_from_gathered(name, g):
    n_dev, r, c = g.shape
    flat = g.reshape(n_dev * r, c)
    return (flat.T, flat) if name in COL_SHARDED else (flat, flat.T)


def _slabs_from_full(grad):
    return grad.reshape(N_DEV, grad.shape[0] // N_DEV, grad.shape[1])


def _pair_sums(slabs, core, tag):
    slabs = [s.reshape(N_CHIP, 2, *s.shape[1:]) for s in slabs]
    theirs = _sibling_exchange(slabs, name=f"reduce_siblings_{tag}")
    return [_pair_sum(a, b, core, name=f"pair_sum_{tag}_{q}") for q, (a, b) in enumerate(zip(slabs, theirs))]


def kernel(x, mix_norm_g, w_in, b_in, conv_a_w, conv_a_b, ln_a_g, ln_a_b, conv_b_w, w_out, ffn_norm_g, w_up, conv_f_w, w_down, final_norm_g, loss_target, m_mix_norm_g, m_w_in, m_b_in, m_conv_a_w, m_conv_a_b, m_ln_a_g, m_ln_a_b, m_conv_b_w, m_w_out, m_ffn_norm_g, m_w_up, m_conv_f_w, m_w_down, m_final_norm_g, v_mix_norm_g, v_w_in, v_b_in, v_conv_a_w, v_conv_a_b, v_ln_a_g, v_ln_a_b, v_conv_b_w, v_w_out, v_ffn_norm_g, v_w_up, v_conv_f_w, v_w_down, v_final_norm_g):
    w = dict(mix_norm_g=mix_norm_g, w_in=w_in, b_in=b_in, conv_a_w=conv_a_w, conv_a_b=conv_a_b, ln_a_g=ln_a_g,
             ln_a_b=ln_a_b, conv_b_w=conv_b_w, w_out=w_out, ffn_norm_g=ffn_norm_g, w_up=w_up, conv_f_w=conv_f_w,
             w_down=w_down, final_norm_g=final_norm_g)
    m = dict(mix_norm_g=m_mix_norm_g, w_in=m_w_in, b_in=m_b_in, conv_a_w=m_conv_a_w, conv_a_b=m_conv_a_b,
             ln_a_g=m_ln_a_g, ln_a_b=m_ln_a_b, conv_b_w=m_conv_b_w, w_out=m_w_out, ffn_norm_g=m_ffn_norm_g,
             w_up=m_w_up, conv_f_w=m_conv_f_w, w_down=m_w_down, final_norm_g=m_final_norm_g)
    v = dict(mix_norm_g=v_mix_norm_g, w_in=v_w_in, b_in=v_b_in, conv_a_w=v_conv_a_w, conv_a_b=v_conv_a_b,
             ln_a_g=v_ln_a_g, ln_a_b=v_ln_a_b, conv_b_w=v_conv_b_w, w_out=v_w_out, ffn_norm_g=v_ffn_norm_g,
             w_up=v_w_up, conv_f_w=v_conv_f_w, w_down=v_w_down, final_norm_g=v_final_norm_g)
    order = list(w)
    n_layers = w_in.shape[0]
    n_big = len(BIG)
    xs = x[0]
    target = loss_target[0]
    flip = lambda a: jnp.transpose(a, (0, 2, 1))
    wt, mt, vt = ({n: flip(d[n]) if n in COL_SHARDED else d[n] for n in BIG} for d in (w, m, v))
    px, py, pc = _place()
    core = pc.astype(jnp.int32).reshape(1)
    me = 4 * px + 2 * py + pc
    my_chip = 2 * px + py

    gathered = _all_gather([wt[n][0].astype(BF16) for n in BIG] + [w[n] for n in CONV], name="gather_weights_0")
    conv_full = {}
    for n, g in zip(CONV, gathered[n_big:]):
        n_dev, _, taps, c = g.shape
        conv_full[n] = g.transpose(1, 2, 0, 3).reshape(n_layers, taps, n_dev * c)

    def layer_params(l, big):
        p = {n: conv_full[n][l] for n in CONV}
        for n, g in zip(BIG, big):
            p[n], p[n + "_t"] = _weights_from_gathered(n, g)
        p.update({n: w[n][l] for n in REPLICATED})
        return p

    params = [layer_params(0, gathered[:n_big])]
    h = xs
    saved = []
    for l in range(n_layers):
        if l + 1 == n_layers:
            h, keep = _layer_fwd(h, params[l], str(l))
            saved.append(keep)
            break
        shards = [wt[n][l + 1].astype(BF16) for n in BIG]
        lands = [_landing((N_DEV, *s.shape), s.dtype, s, me) for s in shards]
        first = _split_start(shards, lands, _gather_plan_first, 4 * n_big, name=f"gather_first_start_{l + 1}")
        second = []

        def before_ffn(uf, first=first, second=second, tag=l + 1):
            arrived = _split_wait(first[0], first[1], first[2], first[3], uf, _gather_plan_first,
                                  name=f"gather_first_wait_{tag}")
            second.extend(_split_start([], arrived, _gather_plan_second, 3 * n_big, name=f"gather_second_start_{tag}"))
            return second[4]

        h, keep = _layer_fwd(h, params[l], str(l), dep=first[4], before_ffn=before_ffn)
        saved.append(keep)
        big = _split_wait(second[0], second[1], [], second[3], h, _gather_plan_second, name=f"gather_second_wait_{l + 1}")
        params.append(layer_params(l + 1, big))

    def start_reduce(slabs, tag):
        pairs = _pair_sums(slabs, core, tag)
        lands = [_landing(p.shape, p.dtype, lax.dynamic_index_in_dim(p, my_chip, 0, keepdims=False), my_chip)
                 for p in pairs]
        return _split_start(pairs, lands, _chips_plan, 3 * len(pairs), name=f"reduce_chips_start_{tag}")

    def finish_reduce(fly, after, tag):
        return _split_wait(fly[0], fly[1], fly[2], fly[3], after, _chips_plan, name=f"reduce_chips_wait_{tag}")

    loss_sq, dh, dh_b, dgf = _loss_bwd(h, _row(final_norm_g), target, name="loss")
    conv_g = {n: [None] * n_layers for n in CONV}
    rep_g = [None] * n_layers
    flights = {}
    token = None
    for l in reversed(range(n_layers)):
        def after_ffn(g, l=l):
            flights[l, "ffn"] = start_reduce([_slabs_from_full(g[n]) for n in FFN_PART], f"{l}_ffn")
            return flights[l, "ffn"][4]

        dh, dh_b, big_g, conv, rep_g[l] = _layer_bwd(dh, dh_b, params[l], saved[l], str(l), dep=token,
                                                     after_ffn=after_ffn)
        for n in CONV:
            conv_g[n][l] = conv[n]
        slabs = [_slabs_from_full(big_g[n]) for n in MIX_PART]
        if l == 0:
            for n in CONV:
                full = jnp.stack(conv_g[n])
                _, taps, c = full.shape
                slabs.append(full.reshape(n_layers, taps, N_DEV, c // N_DEV).transpose(2, 0, 1, 3)
                             .reshape(N_DEV, n_layers * taps, c // N_DEV))
        flights[l, "mix"] = start_reduce(slabs, f"{l}_mix")
        token = flights[l, "mix"][4]

    sums = {key: finish_reduce(fly, dh, f"{key[0]}_{key[1]}") for key, fly in flights.items() if key != (0, "mix")}
    out = {k: {} for k in KINDS}

    def adamw_big(names, part, dep):
        for q, n in enumerate(names):
            layer_parts = [sums[l, part][q] for l in range(n_layers)]
            res = _adamw_sharded(layer_parts, wt[n], mt[n], vt[n], name=f"adamw_{n}", dep=dep)
            for k, r in zip(KINDS, res):
                out[k][n] = flip(r) if n in COL_SHARDED else r

    adamw_big(FFN_PART, "ffn", token)

    rep_cols = [rep_g[l][n] for l in range(n_layers) for n in REPLICATED] + [dgf, loss_sq]
    rep_all = _all_gather([_fold_partials(rep_cols, name="fold_small")], name="gather_small")[0]
    with_final = lambda d: {**{n: d[n] for n in REPLICATED}, "final_norm_g": _row(d["final_norm_g"])}
    loss, rep_res = _adamw_replicated(rep_all, REPLICATED, with_final(w), with_final(m), with_final(v),
                                      loss_sq.shape[1], name="adamw_small")
    for n, res in rep_res.items():
        for k, r in zip(KINDS, res):
            out[k][n] = r.reshape(w[n].shape)

    last = finish_reduce(flights[0, "mix"], rep_res["b_in"][0], "0_mix")
    sums[0, "mix"] = last[:len(MIX_PART)]
    adamw_big(MIX_PART, "mix", None)
    for n, p in zip(CONV, last[len(MIX_PART):]):
        as_one = lambda a: a.reshape(1, *p.shape[1:])
        for k, r in zip(KINDS, _adamw_sharded([p], as_one(w[n]), as_one(m[n]), as_one(v[n]), name=f"adamw_{n}")):
            out[k][n] = r.reshape(w[n].shape)

    grad_x = dh.reshape(x.shape)
    return (loss.reshape(()), grad_x, *[out["grad"][n] for n in order], *[out["delta"][n] for n in order],
            *[out["m"][n] for n in order], *[out["v"][n] for n in order])
```

```python
import functools

import jax
import jax.numpy as jnp
from jax import lax
from jax.experimental import pallas as pl
from jax.experimental.pallas import tpu as pltpu

F32 = jnp.float32
BF16 = jnp.bfloat16

N_DEV = 8
N_CHIP = 4
D_CONF = 512
CONF_K = 31
SHORT_K = 3
EPS = 1e-6
HALO = 32
HALO3 = 8
HALO3_BLK = 16
LANES = 128
SUB = 8
VMEM_LIMIT = 56 * 1024 * 1024

ADAM_LR = 0.001
ADAM_B1 = 0.9
ADAM_B2 = 0.999
ADAM_EPS = 1e-08
ADAM_WD = 0.01
ADAM_STEP = 10

MESH = pl.DeviceIdType.MESH
ANY = pl.BlockSpec(memory_space=pl.ANY)


def _params(*sem):
    return pltpu.CompilerParams(dimension_semantics=sem, vmem_limit_bytes=VMEM_LIMIT)


def _resident(shape, index_map):
    return pl.BlockSpec(shape, index_map, pipeline_mode=pl.Buffered(1))


def _row_loop(n_rows, rb, fn, unroll=1):
    rb = min(rb, n_rows)

    def body(i, carry):
        fn(pl.ds(pl.multiple_of(i * rb, rb), rb))
        return carry
    lax.fori_loop(0, n_rows // rb, body, 0, unroll=unroll)


def _rows8(v):
    acc = v[0:SUB]
    for k in range(1, v.shape[0] // SUB):
        acc = acc + v[k * SUB:(k + 1) * SUB]
    return acc


def _sigmoid(z):
    return 0.5 * jnp.tanh(0.5 * z) + 0.5


def _dot(a, b):
    return jnp.dot(a, b, preferred_element_type=F32)


def _dot_tn(a, b):
    return lax.dot_general(a, b, (((0,), (0,)), ((), ())), preferred_element_type=F32)


def _replicate_taps(w_ref, wrep, taps):
    for k in range(taps):
        wrep[pl.ds(k * SUB, SUB), :] = jnp.broadcast_to(w_ref[pl.ds(k, 1), :], (SUB, w_ref.shape[1]))


def _shift_copies(win, shf, lanes):
    span = win.shape[0] - SUB
    for r in range(1, SUB):
        for j0 in range(0, span, 64):
            n = min(64, span - j0)
            shf[r - 1, pl.ds(j0, n), lanes] = win[pl.ds(j0 + r, n), lanes]


def _rows_at(win, shf, off, rb, lanes):
    if shf is None or off % SUB == 0:
        return win[pl.ds(off, rb), lanes]
    return shf[off % SUB - 1, pl.ds(off - off % SUB, rb), lanes]


def _conv_taps(win, wrep, out, *, taps, n_rows, base, width, transposed=False, bias_ref=None, shf=None):
    rb = min(64, n_rows)

    def lane_body(cb, carry):
        lanes = pl.ds(pl.multiple_of(cb * LANES, LANES), LANES)
        if shf is not None:
            _shift_copies(win, shf, lanes)
        for r0 in range(0, n_rows, rb):
            acc = None
            for k in range(taps):
                off = (taps - 1 - k) if transposed else (k - (taps - 1))
                wk = jnp.tile(wrep[pl.ds(k * SUB, SUB), lanes], (rb // SUB, 1))
                term = wk * _rows_at(win, shf, base + r0 + off, rb, lanes)
                acc = term if acc is None else acc + term
            if bias_ref is not None:
                acc = acc + bias_ref[:, lanes]
            out[pl.ds(r0, rb), lanes] = acc.astype(out.dtype)
        return carry

    lax.fori_loop(0, width // LANES, lane_body, 0)


def _conv_bwd_taps(win, wrep, x_cur, dx_out, dw_acc, *, taps, n_rows, width, shf=None):
    rb = min(32 if taps > 8 else 64, n_rows)

    def lane_body(cb, carry):
        lanes = pl.ds(pl.multiple_of(cb * LANES, LANES), LANES)
        if shf is not None:
            _shift_copies(win, shf, lanes)
        sums = [None] * taps
        for r0 in range(0, n_rows, rb):
            xv = x_cur[pl.ds(r0, rb), lanes].astype(F32)
            acc = None
            for k in range(taps):
                shifted = _rows_at(win, shf, r0 + taps - 1 - k, rb, lanes)
                term = jnp.tile(wrep[pl.ds(k * SUB, SUB), lanes], (rb // SUB, 1)) * shifted
                acc = term if acc is None else acc + term
                part = _rows8(xv * shifted)
                sums[k] = part if sums[k] is None else sums[k] + part
            dx_out[pl.ds(r0, rb), lanes] = acc.astype(dx_out.dtype)
        for k in range(taps):
            dw_acc[pl.ds(k * SUB, SUB), lanes] += sums[k]
        return carry

    lax.fori_loop(0, width // LANES, lane_body, 0)


def _fold8(acc_ref, taps):
    return jnp.concatenate(
        [jnp.sum(acc_ref[pl.ds(k * SUB, SUB), :], axis=0, keepdims=True) for k in range(taps)], axis=0)


def _seq_tile(s_len):
    return min(256, s_len)


def _mm_tile(s_len):
    return min(512, s_len)


def _ff_chunk(ff):
    best = LANES
    for c in range(LANES, 1408 + 1, LANES):
        if ff % c == 0:
            best = c
    return best


def _col_tile(n):
    for c in (512, 1408, 256, LANES):
        if n % c == 0:
            return c
    return n


def _rms_matmul(x, g, w, b, *, name, dep=None):
    s_len, d = x.shape
    n = w.shape[1]
    tm = _mm_tile(s_len)
    cn = _col_tile(n)
    has_bias = b is not None

    def body(*refs):
        x_ref, g_ref, w_ref = refs[0:3]
        b_ref = refs[3] if has_bias else None
        o_ref, h_ref = refs[-2:]

        def blk(rows):
            xv = x_ref[rows, :]
            r = lax.rsqrt(jnp.mean(xv * xv, axis=-1, keepdims=True) + EPS)
            h_ref[rows, :] = ((xv * r) * g_ref[...]).astype(BF16)
        _row_loop(tm, 128, blk)

        def chunk(j, carry):
            cols = pl.ds(pl.multiple_of(j * cn, cn), cn)
            acc = _dot(h_ref[...], w_ref[:, cols])
            if has_bias:
                acc = acc + b_ref[:, cols]
            o_ref[:, cols] = acc.astype(BF16)
            return carry
        lax.fori_loop(0, n // cn, chunk, 0)

    in_specs = [pl.BlockSpec((tm, d), lambda i: (i, 0)), _resident((1, d), lambda i: (0, 0)),
                _resident((d, n), lambda i: (0, 0))]
    args = [x, g, w]
    if has_bias:
        in_specs.append(_resident((1, n), lambda i: (0, 0)))
        args.append(b)
    in_specs.append(ANY)
    args.append(x if dep is None else dep)
    return pl.pallas_call(
        body, grid=(s_len // tm,), in_specs=in_specs,
        out_specs=[pl.BlockSpec((tm, n), lambda i: (i, 0)), pl.BlockSpec((tm, d), lambda i: (i, 0))],
        out_shape=[jax.ShapeDtypeStruct((s_len, n), BF16), jax.ShapeDtypeStruct((s_len, d), BF16)],
        compiler_params=_params("parallel"), name=name,
    )(*args)


def _mix_windows(u_ref, uh_ref, gw, pw, first, t):
    c = D_CONF
    uh = uh_ref[...].astype(F32)
    gw[0:HALO, :] = jnp.where(first, 0.0, uh[:, 0:c] * _sigmoid(uh[:, c:2 * c]))
    pw[0:HALO3, :] = jnp.where(first, 0.0, uh[HALO - HALO3:HALO, 3 * c:4 * c] * uh[HALO - HALO3:HALO, 4 * c:5 * c])

    def blk(rows):
        dst = pl.ds(pl.multiple_of(rows.start + HALO, SUB), rows.size)
        gw[dst, :] = u_ref[rows, 0:c].astype(F32) * _sigmoid(u_ref[rows, c:2 * c].astype(F32))
        dst3 = pl.ds(pl.multiple_of(rows.start + HALO3, SUB), rows.size)
        pw[dst3, :] = u_ref[rows, 3 * c:4 * c].astype(F32) * u_ref[rows, 4 * c:5 * c].astype(F32)
    _row_loop(t, 64, blk)


def _mix_fwd(u, x0, wa, ba, lg, lb, wb, w_out, *, name):
    s_len, d_in = u.shape
    d = x0.shape[1]
    c = D_CONF
    t = _seq_tile(s_len)
    per = t // HALO

    def body(u_ref, uh_ref, x0_ref, wa_ref, ba_ref, lg_ref, lb_ref, wb_ref, wo_ref, y_ref, x1_ref, ca, cb,
             gw, pw, wrep_a, wrep_b, shf):
        first = pl.program_id(0) == 0
        _mix_windows(u_ref, uh_ref, gw, pw, first, t)
        _replicate_taps(wa_ref, wrep_a, CONF_K)
        _replicate_taps(wb_ref, wrep_b, SHORT_K)
        _conv_taps(gw, wrep_a, ca, taps=CONF_K, n_rows=t, base=HALO, width=c, bias_ref=ba_ref, shf=shf)
        _conv_taps(pw, wrep_b, cb, taps=SHORT_K, n_rows=t, base=HALO3, width=c)

        def blk(rows):
            cv = ca[rows, :]
            mu = jnp.mean(cv, axis=-1, keepdims=True)
            xc = cv - mu
            var = jnp.mean(xc * xc, axis=-1, keepdims=True)
            ln = (xc * lax.rsqrt(var + EPS)) * lg_ref[...] + lb_ref[...]
            y_ref[rows, 0:c] = (ln * _sigmoid(ln)).astype(BF16)
            y_ref[rows, c:2 * c] = (u_ref[rows, 2 * c:3 * c].astype(F32) * cb[rows, :]).astype(BF16)
        _row_loop(t, 64, blk)
        x1_ref[...] = x0_ref[...] + _dot(y_ref[...], wo_ref[...])

    small = lambda r: _resident((r, c), lambda i: (0, 0))
    return pl.pallas_call(
        body, grid=(s_len // t,),
        in_specs=[pl.BlockSpec((t, d_in), lambda i: (i, 0)),
                  pl.BlockSpec((HALO, d_in), lambda i: (jnp.maximum(i * per - 1, 0), 0)),
                  pl.BlockSpec((t, d), lambda i: (i, 0)),
                  small(CONF_K), small(1), small(1), small(1), small(SHORT_K),
                  _resident((2 * c, d), lambda i: (0, 0))],
        out_specs=[pl.BlockSpec((t, 2 * c), lambda i: (i, 0)), pl.BlockSpec((t, d), lambda i: (i, 0)),
                   pl.BlockSpec((t, c), lambda i: (i, 0)), pl.BlockSpec((t, c), lambda i: (i, 0))],
        out_shape=[jax.ShapeDtypeStruct((s_len, 2 * c), BF16), jax.ShapeDtypeStruct((s_len, d), F32),
                   jax.ShapeDtypeStruct((s_len, c), F32), jax.ShapeDtypeStruct((s_len, c), F32)],
        scratch_shapes=[pltpu.VMEM((HALO + t, c), F32), pltpu.VMEM((HALO3 + t, c), F32),
                        pltpu.VMEM((CONF_K * SUB, c), F32), pltpu.VMEM((SHORT_K * SUB, c), F32),
                        pltpu.VMEM((SUB - 1, HALO + t, c), F32)],
        compiler_params=_params("arbitrary"), name=name,
    )(u, u, x0, wa, ba, lg, lb, wb, w_out)


def _ffn_windows(ug_ref, ugh_ref, uv_ref, uvh_ref, gwin, vwin, first, t):
    lo = HALO3_BLK - HALO3
    gwin[0:HALO3, :] = jnp.where(first, 0.0, ugh_ref[...].astype(F32)[lo:HALO3_BLK])
    vwin[0:HALO3, :] = jnp.where(first, 0.0, uvh_ref[...].astype(F32)[lo:HALO3_BLK])

    def blk(rows):
        dst = pl.ds(pl.multiple_of(rows.start + HALO3, SUB), rows.size)
        gwin[dst, :] = ug_ref[rows, :].astype(F32)
        vwin[dst, :] = uv_ref[rows, :].astype(F32)
    _row_loop(t, 64, blk)


def _ffn_fwd(uf, x1, wf, w_down, *, name, dep=None):
    s_len, ff2 = uf.shape
    ff = ff2 // 2
    d = x1.shape[1]
    t = _seq_tile(s_len)
    fc = _ff_chunk(ff)
    nc = ff // fc
    per = t // HALO3_BLK

    def body(ug_ref, ugh_ref, uv_ref, uvh_ref, x1_ref, wfg_ref, wfv_ref, wd_ref, dep_ref,
             act_ref, x2_ref, cg_ref, cv_ref, gwin, vwin, cg, cv, wrep_g, wrep_v):
        first = pl.program_id(0) == 0
        _ffn_windows(ug_ref, ugh_ref, uv_ref, uvh_ref, gwin, vwin, first, t)
        _replicate_taps(wfg_ref, wrep_g, SHORT_K)
        _replicate_taps(wfv_ref, wrep_v, SHORT_K)
        _conv_taps(gwin, wrep_g, cg, taps=SHORT_K, n_rows=t, base=HALO3, width=fc)
        _conv_taps(vwin, wrep_v, cv, taps=SHORT_K, n_rows=t, base=HALO3, width=fc)

        def blk(rows):
            gv = cg[rows, :]
            vv = cv[rows, :]
            cg_ref[rows, :] = gv.astype(BF16)
            cv_ref[rows, :] = vv.astype(BF16)
            act_ref[rows, :] = ((gv * _sigmoid(gv)) * vv).astype(BF16)
        _row_loop(t, 32, blk, unroll=2)

        @pl.when(pl.program_id(1) == 0)
        def _():
            x2_ref[...] = x1_ref[...]
        x2_ref[...] += _dot(act_ref[...], wd_ref[...])

    halo_map = lambda off: (lambda i, j: (jnp.maximum(i * per - 1, 0), j + off))
    return pl.pallas_call(
        body, grid=(s_len // t, nc),
        in_specs=[pl.BlockSpec((t, fc), lambda i, j: (i, j)), pl.BlockSpec((HALO3_BLK, fc), halo_map(0)),
                  pl.BlockSpec((t, fc), lambda i, j: (i, j + nc)), pl.BlockSpec((HALO3_BLK, fc), halo_map(nc)),
                  pl.BlockSpec((t, d), lambda i, j: (i, 0)),
                  pl.BlockSpec((SHORT_K, fc), lambda i, j: (0, j)),
                  pl.BlockSpec((SHORT_K, fc), lambda i, j: (0, j + nc)),
                  pl.BlockSpec((fc, d), lambda i, j: (j, 0)), ANY],
        out_specs=[pl.BlockSpec((t, fc), lambda i, j: (i, j)), pl.BlockSpec((t, d), lambda i, j: (i, 0)),
                   pl.BlockSpec((t, fc), lambda i, j: (i, j)), pl.BlockSpec((t, fc), lambda i, j: (i, j))],
        out_shape=[jax.ShapeDtypeStruct((s_len, ff), BF16), jax.ShapeDtypeStruct((s_len, d), F32),
                   jax.ShapeDtypeStruct((s_len, ff), BF16), jax.ShapeDtypeStruct((s_len, ff), BF16)],
        scratch_shapes=[pltpu.VMEM((HALO3 + t, fc), F32), pltpu.VMEM((HALO3 + t, fc), F32),
                        pltpu.VMEM((t, fc), F32), pltpu.VMEM((t, fc), F32),
                        pltpu.VMEM((SHORT_K * SUB, fc), F32), pltpu.VMEM((SHORT_K * SUB, fc), F32)],
        compiler_params=_params("parallel", "arbitrary"), name=name,
    )(uf, uf, uf, uf, x1, wf, wf, w_down, uf if dep is None else dep)


def _loss_bwd(x, g, target, *, name):
    s_len, d = x.shape
    t = _seq_tile(s_len)

    def body(x_ref, g_ref, t_ref, l_ref, dx_ref, dxb_ref, dg_ref):
        @pl.when(pl.program_id(0) == 0)
        def _():
            l_ref[...] = jnp.zeros_like(l_ref)
            dg_ref[...] = jnp.zeros_like(dg_ref)

        def blk(rows):
            xv = x_ref[rows, :]
            r = lax.rsqrt(jnp.mean(xv * xv, axis=-1, keepdims=True) + EPS)
            xn = xv * r
            e = xn * g_ref[...] - t_ref[rows, :]
            l_ref[...] += _rows8(e * e)
            dy = e * (1.0 / d)
            dg_ref[...] += _rows8(dy * xn)
            dn = dy * g_ref[...]
            dx = r * (dn - xn * jnp.mean(dn * xn, axis=-1, keepdims=True))
            dx_ref[rows, :] = dx
            dxb_ref[rows, :] = dx.astype(BF16)
        _row_loop(t, 64, blk)

    row = pl.BlockSpec((t, d), lambda i: (i, 0))
    part = pl.BlockSpec((SUB, d), lambda i: (0, 0))
    return pl.pallas_call(
        body, grid=(s_len // t,),
        in_specs=[row, _resident((1, d), lambda i: (0, 0)), row],
        out_specs=[part, row, row, part],
        out_shape=[jax.ShapeDtypeStruct((SUB, d), F32), jax.ShapeDtypeStruct((s_len, d), F32),
                   jax.ShapeDtypeStruct((s_len, d), BF16), jax.ShapeDtypeStruct((SUB, d), F32)],
        compiler_params=_params("arbitrary"), name=name,
    )(x, g, target)


def _ffn_bwd(dx2, uf, cg, cv, wf, w_down_t, *, name, dep=None):
    s_len, ff2 = uf.shape
    ff = ff2 // 2
    d = dx2.shape[1]
    t = _seq_tile(s_len)
    n_t = s_len // t
    fc = _ff_chunk(ff)
    nc = ff // fc

    def body(dx_ref, ug_ref, uv_ref, cg_ref, cv_ref, wfg_ref, wfv_ref, wd_ref, dep_ref,
             dug_ref, duv_ref, dwg_ref, dwv_ref, dact, dgw, dvw, awg, awv, wrep_g, wrep_v):
        i = pl.program_id(1)
        dact[...] = _dot(dx_ref[...], wd_ref[...])
        _replicate_taps(wfg_ref, wrep_g, SHORT_K)
        _replicate_taps(wfv_ref, wrep_v, SHORT_K)

        @pl.when(i == 0)
        def _():
            dgw[t:t + HALO3, :] = jnp.zeros((HALO3, fc), F32)
            dvw[t:t + HALO3, :] = jnp.zeros((HALO3, fc), F32)
            awg[...] = jnp.zeros_like(awg)
            awv[...] = jnp.zeros_like(awv)

        def blk(rows):
            gv = cg_ref[rows, :].astype(F32)
            sg = _sigmoid(gv)
            da = dact[rows, :]
            dgw[rows, :] = (da * cv_ref[rows, :].astype(F32)) * (sg * (1.0 + gv * (1.0 - sg)))
            dvw[rows, :] = da * (gv * sg)
        _row_loop(t, 32, blk, unroll=2)

        _conv_bwd_taps(dgw, wrep_g, ug_ref, dug_ref, awg, taps=SHORT_K, n_rows=t, width=fc)
        _conv_bwd_taps(dvw, wrep_v, uv_ref, duv_ref, awv, taps=SHORT_K, n_rows=t, width=fc)
        dgw[t:t + HALO3, :] = dgw[0:HALO3, :]
        dvw[t:t + HALO3, :] = dvw[0:HALO3, :]

        @pl.when(i == n_t - 1)
        def _():
            dwg_ref[...] = _fold8(awg, SHORT_K)
            dwv_ref[...] = _fold8(awv, SHORT_K)

    rev = lambda i: n_t - 1 - i
    gate = pl.BlockSpec((t, fc), lambda j, i: (rev(i), j))
    value = pl.BlockSpec((t, fc), lambda j, i: (rev(i), j + nc))
    return pl.pallas_call(
        body, grid=(nc, n_t),
        in_specs=[pl.BlockSpec((t, d), lambda j, i: (rev(i), 0)), gate, value, gate, gate,
                  pl.BlockSpec((SHORT_K, fc), lambda j, i: (0, j)),
                  pl.BlockSpec((SHORT_K, fc), lambda j, i: (0, j + nc)),
                  pl.BlockSpec((d, fc), lambda j, i: (0, j)), ANY],
        out_specs=[gate, gate,
                   pl.BlockSpec((SHORT_K, fc), lambda j, i: (0, j)), pl.BlockSpec((SHORT_K, fc), lambda j, i: (0, j))],
        out_shape=[jax.ShapeDtypeStruct((s_len, ff), BF16), jax.ShapeDtypeStruct((s_len, ff), BF16),
                   jax.ShapeDtypeStruct((SHORT_K, ff), F32), jax.ShapeDtypeStruct((SHORT_K, ff), F32)],
        scratch_shapes=[pltpu.VMEM((t, fc), F32),
                        pltpu.VMEM((t + HALO3, fc), F32), pltpu.VMEM((t + HALO3, fc), F32),
                        pltpu.VMEM((SHORT_K * SUB, fc), F32), pltpu.VMEM((SHORT_K * SUB, fc), F32),
                        pltpu.VMEM((SHORT_K * SUB, fc), F32), pltpu.VMEM((SHORT_K * SUB, fc), F32)],
        compiler_params=_params("arbitrary", "arbitrary"), name=name,
    )(dx2, uf, uf, cg, cv, wf, wf, w_down_t, uf if dep is None else dep)


def _mix_bwd(dx1, u, ca, cb, wa, lg, lb, wb, w_out_t, *, name):
    s_len, d_in = u.shape
    d = dx1.shape[1]
    c = D_CONF
    t = _seq_tile(s_len)
    n_t = s_len // t

    def body(dx_ref, u_ref, ca_ref, cb_ref, wa_ref, lg_ref, lb_ref, wb_ref, wo_ref,
             du_ref, dwa_ref, dwb_ref, dba_ref, dlg_ref, dlb_ref, dbin_ref,
             glu, prod, dyc, dcaw, dcbw, dglu, dp, awa, awb, wrep_a, wrep_b, shf):
        i = pl.program_id(0)
        dyc[...] = _dot(dx_ref[...], wo_ref[...])
        _replicate_taps(wa_ref, wrep_a, CONF_K)
        _replicate_taps(wb_ref, wrep_b, SHORT_K)

        @pl.when(i == 0)
        def _():
            dcaw[t:t + HALO, :] = jnp.zeros((HALO, c), F32)
            dcbw[t:t + HALO3, :] = jnp.zeros((HALO3, c), F32)
            awa[...] = jnp.zeros_like(awa)
            awb[...] = jnp.zeros_like(awb)
            dba_ref[...] = jnp.zeros_like(dba_ref)
            dlg_ref[...] = jnp.zeros_like(dlg_ref)
            dlb_ref[...] = jnp.zeros_like(dlb_ref)
            dbin_ref[...] = jnp.zeros_like(dbin_ref)

        def blk1(rows):
            cv = ca_ref[rows, :]
            mu = jnp.mean(cv, axis=-1, keepdims=True)
            xc = cv - mu
            rstd = lax.rsqrt(jnp.mean(xc * xc, axis=-1, keepdims=True) + EPS)
            nrm = xc * rstd
            ln = nrm * lg_ref[...] + lb_ref[...]
            sg = _sigmoid(ln)
            dln = dyc[rows, 0:c] * (sg * (1.0 + ln * (1.0 - sg)))
            dlg_ref[...] += _rows8(dln * nrm)
            dlb_ref[...] += _rows8(dln)
            dn = dln * lg_ref[...]
            dca = rstd * (dn - jnp.mean(dn, axis=-1, keepdims=True)
                          - nrm * jnp.mean(dn * nrm, axis=-1, keepdims=True))
            dcaw[rows, :] = dca
            dba_ref[...] += _rows8(dca)
            ds = dyc[rows, c:2 * c]
            dgb = ds * cb_ref[rows, :]
            dcbw[rows, :] = ds * u_ref[rows, 2 * c:3 * c].astype(F32)
            du_ref[rows, 2 * c:3 * c] = dgb.astype(BF16)
            dbin_ref[:, 2 * c:3 * c] += _rows8(dgb)
            glu[rows, :] = u_ref[rows, 0:c].astype(F32) * _sigmoid(u_ref[rows, c:2 * c].astype(F32))
            prod[rows, :] = u_ref[rows, 3 * c:4 * c].astype(F32) * u_ref[rows, 4 * c:5 * c].astype(F32)
        _row_loop(t, 64, blk1, unroll=2)

        _conv_bwd_taps(dcaw, wrep_a, glu, dglu, awa, taps=CONF_K, n_rows=t, width=c, shf=shf)
        _conv_bwd_taps(dcbw, wrep_b, prod, dp, awb, taps=SHORT_K, n_rows=t, width=c)
        dcaw[t:t + HALO, :] = dcaw[0:HALO, :]
        dcbw[t:t + HALO3, :] = dcbw[0:HALO3, :]

        def blk2(rows):
            av = u_ref[rows, 0:c].astype(F32)
            sg = _sigmoid(u_ref[rows, c:2 * c].astype(F32))
            dg = dglu[rows, :]
            d_av = dg * sg
            d_ag = (dg * av) * (sg * (1.0 - sg))
            dpv = dp[rows, :]
            d_gc = dpv * u_ref[rows, 4 * c:5 * c].astype(F32)
            d_vs = dpv * u_ref[rows, 3 * c:4 * c].astype(F32)
            du_ref[rows, 0:c] = d_av.astype(BF16)
            du_ref[rows, c:2 * c] = d_ag.astype(BF16)
            du_ref[rows, 3 * c:4 * c] = d_gc.astype(BF16)
            du_ref[rows, 4 * c:5 * c] = d_vs.astype(BF16)
            dbin_ref[:, 0:c] += _rows8(d_av)
            dbin_ref[:, c:2 * c] += _rows8(d_ag)
            dbin_ref[:, 3 * c:4 * c] += _rows8(d_gc)
            dbin_ref[:, 4 * c:5 * c] += _rows8(d_vs)
        _row_loop(t, 64, blk2)

        @pl.when(i == n_t - 1)
        def _():
            dwa_ref[...] = _fold8(awa, CONF_K)
            dwb_ref[...] = _fold8(awb, SHORT_K)

    rev = lambda i: n_t - 1 - i
    small_in = lambda r: _resident((r, c), lambda i: (0, 0))
    small = lambda r: pl.BlockSpec((r, c), lambda i: (0, 0))
    return pl.pallas_call(
        body, grid=(n_t,),
        in_specs=[pl.BlockSpec((t, d), lambda i: (rev(i), 0)),
                  pl.BlockSpec((t, d_in), lambda i: (rev(i), 0)),
                  pl.BlockSpec((t, c), lambda i: (rev(i), 0)), pl.BlockSpec((t, c), lambda i: (rev(i), 0)),
                  small_in(CONF_K), small_in(1), small_in(1), small_in(SHORT_K),
                  _resident((d, 2 * c), lambda i: (0, 0))],
        out_specs=[pl.BlockSpec((t, d_in), lambda i: (rev(i), 0)),
                   small(CONF_K), small(SHORT_K), small(SUB), small(SUB), small(SUB),
                   pl.BlockSpec((SUB, d_in), lambda i: (0, 0))],
        out_shape=[jax.ShapeDtypeStruct((s_len, d_in), BF16),
                   jax.ShapeDtypeStruct((CONF_K, c), F32), jax.ShapeDtypeStruct((SHORT_K, c), F32),
                   jax.ShapeDtypeStruct((SUB, c), F32), jax.ShapeDtypeStruct((SUB, c), F32),
                   jax.ShapeDtypeStruct((SUB, c), F32), jax.ShapeDtypeStruct((SUB, d_in), F32)],
        scratch_shapes=[pltpu.VMEM((t, c), F32), pltpu.VMEM((t, c), F32), pltpu.VMEM((t, 2 * c), F32),
                        pltpu.VMEM((t + HALO, c), F32), pltpu.VMEM((t + HALO3, c), F32),
                        pltpu.VMEM((t, c), F32), pltpu.VMEM((t, c), F32),
                        pltpu.VMEM((CONF_K * SUB, c), F32), pltpu.VMEM((SHORT_K * SUB, c), F32),
                        pltpu.VMEM((CONF_K * SUB, c), F32), pltpu.VMEM((SHORT_K * SUB, c), F32),
                        pltpu.VMEM((SUB - 1, t + HALO, c), F32)],
        compiler_params=_params("arbitrary"), name=name,
    )(dx1, u, ca, cb, wa, lg, lb, wb, w_out_t)


def _matmul_tn(a, b, *, name, into=None, part=0, n_parts=1):
    s_len, k = a.shape
    n = b.shape[1]
    tk = _col_tile(k)
    per = k // tk

    def body(*refs):
        a_ref, b_ref = refs[0], refs[1]
        o_ref = refs[-1]
        o_ref[...] = _dot_tn(a_ref[...], b_ref[...]).astype(BF16)

    in_specs = [pl.BlockSpec((s_len, tk), lambda j: (0, j)), _resident((s_len, n), lambda j: (0, 0))]
    args = [a, b]
    aliases = {}
    if into is not None:
        in_specs.append(ANY)
        args.append(into)
        aliases = {2: 0}
    return pl.pallas_call(
        body, grid=(per,), in_specs=in_specs,
        out_specs=pl.BlockSpec((tk, n), lambda j: (part * per + j, 0)),
        out_shape=jax.ShapeDtypeStruct((n_parts * k, n), BF16),
        input_output_aliases=aliases,
        compiler_params=_params("parallel"), name=name,
    )(*args)


def _matmul_rmsbwd(dzs, wt, x, g, dx_in, *, name, dep=None):
    s_len, d = x.shape
    n_z = len(dzs)
    nj = dzs[0].shape[1]
    t = _mm_tile(s_len)

    def body(*refs):
        dz_refs = refs[0:n_z]
        w_refs = refs[n_z:2 * n_z]
        x_ref, g_ref, dxi_ref, _, dx_ref, dxb_ref, dg_ref, dh = refs[2 * n_z:]

        @pl.when(pl.program_id(0) == 0)
        def _():
            dg_ref[...] = jnp.zeros_like(dg_ref)

        acc = _dot(dz_refs[0][...], w_refs[0][...])
        for q in range(1, n_z):
            acc = acc + _dot(dz_refs[q][...], w_refs[q][...])
        dh[...] = acc

        def blk(rows):
            xv = x_ref[rows, :]
            r = lax.rsqrt(jnp.mean(xv * xv, axis=-1, keepdims=True) + EPS)
            xn = xv * r
            dhv = dh[rows, :]
            dg_ref[...] += _rows8(dhv * xn)
            dn = dhv * g_ref[...]
            dx = dxi_ref[rows, :] + r * (dn - xn * jnp.mean(dn * xn, axis=-1, keepdims=True))
            dx_ref[rows, :] = dx
            dxb_ref[rows, :] = dx.astype(BF16)
        _row_loop(t, 128, blk)

    row = pl.BlockSpec((t, d), lambda i: (i, 0))
    in_specs = [pl.BlockSpec((t, nj), lambda i: (i, 0)) for _ in range(n_z)]
    in_specs += [_resident((nj, d), functools.partial(lambda q, i: (q, 0), q)) for q in range(n_z)]
    in_specs += [row, _resident((1, d), lambda i: (0, 0)), row, ANY]
    return pl.pallas_call(
        body, grid=(s_len // t,), in_specs=in_specs,
        out_specs=[row, row, pl.BlockSpec((SUB, d), lambda i: (0, 0))],
        out_shape=[jax.ShapeDtypeStruct((s_len, d), F32), jax.ShapeDtypeStruct((s_len, d), BF16),
                   jax.ShapeDtypeStruct((SUB, d), F32)],
        scratch_shapes=[pltpu.VMEM((t, d), F32)],
        compiler_params=_params("arbitrary"), name=name,
    )(*dzs, *([wt] * n_z), x, g, dx_in, x if dep is None else dep)


def _row(v):
    return v.reshape(1, -1)


def _layer_fwd(x0, p, tag, dep=None, before_ffn=None):
    u, h1 = _rms_matmul(x0, _row(p["mix_norm_g"]), p["w_in"], _row(p["b_in"]), name=f"in_proj_{tag}", dep=dep)
    ycat, x1, ca, cb = _mix_fwd(u, x0, p["conv_a_w"], _row(p["conv_a_b"]), _row(p["ln_a_g"]), _row(p["ln_a_b"]),
                            p["conv_b_w"], p["w_out"], name=f"mix_fwd_{tag}")
    uf, h2 = _rms_matmul(x1, _row(p["ffn_norm_g"]), p["w_up"], None, name=f"up_proj_{tag}")
    dep_ffn = None if before_ffn is None else before_ffn(uf)
    act, x2, cg, cv = _ffn_fwd(uf, x1, p["conv_f_w"], p["w_down"], name=f"ffn_fwd_{tag}", dep=dep_ffn)
    return x2, dict(x0=x0, h1=h1, u=u, ca=ca, cb=cb, ycat=ycat, x1=x1, h2=h2, uf=uf, cg=cg, cv=cv, act=act)


def _layer_bwd(dx2, dx2_b, p, saved, tag, after_ffn, after_mix, dep=None):
    dug, duv, dwf_g, dwf_v = _ffn_bwd(dx2_b, saved["uf"], saved["cg"], saved["cv"], p["conv_f_w"], p["w_down_t"],
                                      name=f"ffn_bwd_{tag}", dep=dep)
    g_down = _matmul_tn(saved["act"], dx2_b, name=f"dw_down_{tag}")
    g_up = _matmul_tn(dug, saved["h2"], name=f"dw_up_g_{tag}", n_parts=2)
    g_up = _matmul_tn(duv, saved["h2"], name=f"dw_up_v_{tag}", into=g_up, part=1, n_parts=2)
    dep_ffn = after_ffn(dict(w_up=g_up, w_down=g_down))
    dx1, dx1_b, dg2 = _matmul_rmsbwd([dug, duv], p["w_up_t"], saved["x1"], _row(p["ffn_norm_g"]), dx2,
                                     name=f"dh_ffn_{tag}", dep=dep_ffn)
    du, dwa, dwb, dba, dlg, dlb, dbin = _mix_bwd(
        dx1_b, saved["u"], saved["ca"], saved["cb"], p["conv_a_w"], _row(p["ln_a_g"]), _row(p["ln_a_b"]),
        p["conv_b_w"], p["w_out_t"], name=f"mix_bwd_{tag}")
    g_out = _matmul_tn(saved["ycat"], dx1_b, name=f"dw_out_{tag}")
    g_in = _matmul_tn(du, saved["h1"], name=f"dw_in_{tag}")
    conv = dict(conv_a_w=dwa, conv_b_w=dwb, conv_f_w=jnp.concatenate([dwf_g, dwf_v], axis=1))
    dep_mix = after_mix(dict(w_in=g_in, w_out=g_out), conv)
    dx0, dx0_b, dg1 = _matmul_rmsbwd([du], p["w_in_t"], saved["x0"], _row(p["mix_norm_g"]), dx1,
                                     name=f"dh_mix_{tag}", dep=dep_mix)
    rep = dict(mix_norm_g=dg1, b_in=dbin, conv_a_b=dba, ln_a_g=dlg, ln_a_b=dlb, ffn_norm_g=dg2)
    return dx0, dx0_b, rep


def _place():
    return lax.axis_index("x"), lax.axis_index("y"), lax.axis_index("c")


def _all_gather(arrs, *, name):
    n_a = len(arrs)

    def body(*refs):
        ins = refs[0:n_a]
        outs = refs[n_a:2 * n_a]
        send_sems, recv_sems, local_sems = refs[2 * n_a:]
        x, y, c = _place()
        sibling = (x, y, 1 - c)
        chips = [(1 - x, y), (x, 1 - y), (1 - x, 1 - y)]

        def slot(a, px, py, pc):
            return outs[a].at[4 * px + 2 * py + pc]

        def copy(a, k, block, to, src=None):
            return pltpu.make_async_remote_copy(
                src_ref=slot(a, *block) if src is None else src, dst_ref=slot(a, *block),
                send_sem=send_sems.at[a, k], recv_sem=recv_sems.at[a, k],
                device_id=to, device_id_type=MESH)

        me = (x, y, c)
        mine = [pltpu.make_async_copy(ins[a], slot(a, *me), local_sems.at[a]) for a in range(n_a)]
        for cp in mine:
            cp.start()
        started = []
        for a in range(n_a):
            first = [copy(a, 0, me, sibling, src=ins[a])]
            first += [copy(a, 1 + j, me, (*chip, c), src=ins[a]) for j, chip in enumerate(chips)]
            for cp in first:
                cp.start()
            started += first
        for a in range(n_a):
            for j, chip in enumerate(chips):
                copy(a, 1 + j, (*chip, c), me).wait_recv()
                passed = copy(a, 4 + j, (*chip, c), sibling)
                passed.start()
                started.append(passed)
        for a in range(n_a):
            copy(a, 0, sibling, me).wait_recv()
            for j, chip in enumerate(chips):
                copy(a, 4 + j, (*chip, 1 - c), me).wait_recv()
        for cp in started:
            cp.wait_send()
        for cp in mine:
            cp.wait()

    return pl.pallas_call(
        body, in_specs=[ANY] * n_a, out_specs=[ANY] * n_a,
        out_shape=[jax.ShapeDtypeStruct((N_DEV, *a.shape), a.dtype) for a in arrs],
        scratch_shapes=[pltpu.SemaphoreType.DMA((n_a, 7)), pltpu.SemaphoreType.DMA((n_a, 7)),
                        pltpu.SemaphoreType.DMA((n_a,))],
        name=name,
    )(*arrs)


def _sibling_exchange(arrs, *, name):
    n_a = len(arrs)

    def body(*refs):
        ins = refs[0:n_a]
        outs = refs[n_a:2 * n_a]
        send_sems, recv_sems = refs[2 * n_a:]
        x, y, c = _place()
        copies = [pltpu.make_async_remote_copy(
            src_ref=ins[a].at[:, 1 - c], dst_ref=outs[a], send_sem=send_sems.at[a], recv_sem=recv_sems.at[a],
            device_id=(x, y, 1 - c), device_id_type=MESH) for a in range(n_a)]
        for cp in copies:
            cp.start()
        for cp in copies:
            cp.wait()

    return pl.pallas_call(
        body, in_specs=[ANY] * n_a, out_specs=[ANY] * n_a,
        out_shape=[jax.ShapeDtypeStruct((N_CHIP, *a.shape[2:]), a.dtype) for a in arrs],
        scratch_shapes=[pltpu.SemaphoreType.DMA((n_a,)), pltpu.SemaphoreType.DMA((n_a,))],
        name=name,
    )(*arrs)


def _row_tile(r):
    for tr in (512, 352, 256, 128, 64, 32, 16, 8):
        if r % tr == 0:
            return tr
    return r


def _pair_sum(mine, theirs, core, *, name):
    n_chip, _, r, c = mine.shape
    tr = _row_tile(r)

    def body(core_ref, a_ref, b_ref, o_ref):
        o_ref[...] = (a_ref[...].astype(F32) + b_ref[...].astype(F32)).astype(o_ref.dtype)

    return pl.pallas_call(
        body,
        grid_spec=pltpu.PrefetchScalarGridSpec(
            num_scalar_prefetch=1, grid=(n_chip, r // tr),
            in_specs=[pl.BlockSpec((None, None, tr, c), lambda q, i, core_ref: (q, core_ref[0], i, 0)),
                      pl.BlockSpec((None, tr, c), lambda q, i, core_ref: (q, i, 0))],
            out_specs=pl.BlockSpec((None, tr, c), lambda q, i, core_ref: (q, i, 0))),
        out_shape=jax.ShapeDtypeStruct((n_chip, r, c), mine.dtype),
        compiler_params=_params("parallel", "parallel"), name=name,
    )(core, mine, theirs)


def _chip_exchange(arrs, *, name):
    n_a = len(arrs)

    def body(*refs):
        ins = refs[0:n_a]
        outs = refs[n_a:2 * n_a]
        send_sems, recv_sems, local_sems = refs[2 * n_a:]
        x, y, c = _place()
        my_chip = 2 * x + y
        chips = [(1 - x, y), (x, 1 - y), (1 - x, 1 - y)]
        mine = [pltpu.make_async_copy(ins[a].at[my_chip], outs[a].at[my_chip], local_sems.at[a]) for a in range(n_a)]
        for cp in mine:
            cp.start()
        copies = []
        for a in range(n_a):
            for j, (px, py) in enumerate(chips):
                copies.append(pltpu.make_async_remote_copy(
                    src_ref=ins[a].at[2 * px + py], dst_ref=outs[a].at[my_chip],
                    send_sem=send_sems.at[a, j], recv_sem=recv_sems.at[a, j],
                    device_id=(px, py, c), device_id_type=MESH))
        for cp in copies:
            cp.start()
        for cp in copies:
            cp.wait()
        for cp in mine:
            cp.wait()

    return pl.pallas_call(
        body, in_specs=[ANY] * n_a, out_specs=[ANY] * n_a,
        out_shape=[jax.ShapeDtypeStruct(a.shape, a.dtype) for a in arrs],
        scratch_shapes=[pltpu.SemaphoreType.DMA((n_a, 3)), pltpu.SemaphoreType.DMA((n_a, 3)),
                        pltpu.SemaphoreType.DMA((n_a,))],
        name=name,
    )(*arrs)


HBM = pl.BlockSpec(memory_space=pltpu.HBM)
SEM = pl.BlockSpec(memory_space=pltpu.SEMAPHORE)
EFFECT = pltpu.SideEffectType.DATAFLOW_SIDE_EFFECTING


def _in_hbm(a):
    return pltpu.with_memory_space_constraint(a, pltpu.HBM)


def _split_start(srcs, lands, plan, n_copies, *, name):
    n_s, n_l = len(srcs), len(lands)

    def body(*refs):
        src_refs = refs[0:n_s]
        land_refs = refs[n_s:n_s + n_l]
        send_sems, recv_sems = refs[n_s + n_l], refs[n_s + n_l + 1]
        token = refs[-1]
        for cp in plan(src_refs, land_refs, send_sems, recv_sems):
            cp.start()
        token[...] = jnp.zeros_like(token)

    thru = [pltpu.HBM(a.shape, a.dtype) for a in list(srcs) + list(lands)]
    res = pl.pallas_call(
        body, name=name,
        out_shape=(pltpu.SemaphoreType.DMA((n_copies,)), pltpu.SemaphoreType.DMA((n_copies,)), *thru,
                   jax.ShapeDtypeStruct((SUB, LANES), F32)),
        in_specs=[HBM] * (n_s + n_l),
        out_specs=(SEM, SEM, *([HBM] * (n_s + n_l)), pl.BlockSpec(memory_space=pltpu.VMEM)),
        input_output_aliases={i: 2 + i for i in range(n_s + n_l)},
        compiler_params=pltpu.CompilerParams(has_side_effects=EFFECT),
    )(*[_in_hbm(a) for a in srcs], *[_in_hbm(a) for a in lands])
    return res[0], res[1], list(res[2:2 + n_s]), list(res[2 + n_s:2 + n_s + n_l]), res[-1]


def _split_wait(send_sems, recv_sems, srcs, lands, after, plan, *, name):
    n_s, n_l = len(srcs), len(lands)

    def body(*refs):
        src_refs = refs[0:n_s]
        land_refs = refs[n_s:n_s + n_l]
        send, recv = refs[n_s + n_l], refs[n_s + n_l + 1]
        for cp in plan(src_refs, land_refs, send, recv):
            cp.wait_send()
            cp.wait_recv()

    res = pl.pallas_call(
        body, name=name,
        out_shape=tuple(pltpu.HBM(a.shape, a.dtype) for a in list(srcs) + list(lands)),
        in_specs=[HBM] * (n_s + n_l) + [SEM, SEM, ANY],
        out_specs=tuple([HBM] * (n_s + n_l)),
        input_output_aliases={i: i for i in range(n_s + n_l)},
        compiler_params=pltpu.CompilerParams(has_side_effects=EFFECT),
    )(*srcs, *lands, send_sems, recv_sems, _in_hbm(after))
    return list(res[n_s:])


def _remote(src, dst, send_sems, recv_sems, k, to):
    return pltpu.make_async_remote_copy(src_ref=src, dst_ref=dst, send_sem=send_sems.at[k], recv_sem=recv_sems.at[k],
                                        device_id=to, device_id_type=MESH)


def _gather_plan_first(src_refs, land_refs, send_sems, recv_sems):
    x, y, c = _place()
    me = 4 * x + 2 * y + c
    peers = [(x, y, 1 - c), (1 - x, y, c), (x, 1 - y, c), (1 - x, 1 - y, c)]
    return [_remote(src, land.at[me], send_sems, recv_sems, 4 * a + k, to)
            for a, (src, land) in enumerate(zip(src_refs, land_refs)) for k, to in enumerate(peers)]


def _gather_plan_second(src_refs, land_refs, send_sems, recv_sems):
    x, y, c = _place()
    chips = [(1 - x, y), (x, 1 - y), (1 - x, 1 - y)]
    out = []
    for a, land in enumerate(land_refs):
        for j, (px, py) in enumerate(chips):
            slot = land.at[4 * px + 2 * py + c]
            out.append(_remote(slot, slot, send_sems, recv_sems, 3 * a + j, (x, y, 1 - c)))
    return out


def _chips_plan(src_refs, land_refs, send_sems, recv_sems):
    x, y, c = _place()
    my_chip = 2 * x + y
    chips = [(1 - x, y), (x, 1 - y), (1 - x, 1 - y)]
    return [_remote(src.at[2 * px + py], land.at[my_chip], send_sems, recv_sems, 3 * a + j, (px, py, c))
            for a, (src, land) in enumerate(zip(src_refs, land_refs)) for j, (px, py) in enumerate(chips)]


def _landing(like_shape, dtype, own, index):
    return lax.dynamic_update_index_in_dim(lax.empty(like_shape, dtype), own, index, 0)


def _adamw_math(g, w, m, v):
    m = ADAM_B1 * m + (1.0 - ADAM_B1) * g
    v = ADAM_B2 * v + (1.0 - ADAM_B2) * (g * g)
    m_hat = m / (1.0 - ADAM_B1 ** ADAM_STEP)
    v_hat = v / (1.0 - ADAM_B2 ** ADAM_STEP)
    delta = -ADAM_LR * (m_hat / (jnp.sqrt(v_hat) + ADAM_EPS) + ADAM_WD * w)
    return delta, m, v


def _adamw_sharded(parts, w, m, v, *, name, dep=None):
    n_layers, r, c = w.shape
    n_chip = parts[0].shape[0]
    tr = _row_tile(r)
    n_i = r // tr

    def body(*refs):
        p_refs = refs[0:n_layers]
        w_ref, m_ref, v_ref, _, g_out, d_out, m_out, v_out = refs[n_layers:]
        layer = pl.program_id(0)
        for l in range(n_layers):
            @pl.when(layer == l)
            def _(l=l):
                g = p_refs[l][0].astype(F32)
                for q in range(1, n_chip):
                    g = g + p_refs[l][q].astype(F32)
                delta, m_new, v_new = _adamw_math(g, w_ref[...], m_ref[...], v_ref[...])
                g_out[...] = g
                d_out[...] = delta
                m_out[...] = m_new
                v_out[...] = v_new

    def part_map(l):
        return lambda layer, i: (0, jnp.where(layer == l, i, jnp.where(layer < l, 0, n_i - 1)), 0)

    blk = pl.BlockSpec((None, tr, c), lambda layer, i: (layer, i, 0))
    return pl.pallas_call(
        body, grid=(n_layers, n_i),
        in_specs=[pl.BlockSpec((n_chip, tr, c), part_map(l)) for l in range(n_layers)] + [blk, blk, blk, ANY],
        out_specs=[blk] * 4, out_shape=[jax.ShapeDtypeStruct((n_layers, r, c), F32)] * 4,
        compiler_params=_params("arbitrary", "arbitrary"), name=name,
    )(*parts, w, m, v, w if dep is None else dep)


def _fold_partials(cols, *, name):
    widths = [c.shape[1] for c in cols]

    def body(*refs):
        o_ref = refs[-1]
        pos = 0
        for ref, width in zip(refs[:-1], widths):
            o_ref[:, pos:pos + width] = jnp.sum(ref[...], axis=0, keepdims=True)
            pos += width

    return pl.pallas_call(body, out_shape=jax.ShapeDtypeStruct((1, sum(widths)), F32), name=name)(*cols)


def _adamw_replicated(parts, names, w, m, v, n_loss, *, name):
    n_dev = parts.shape[0]
    n_layers = w[names[0]].shape[0]
    every = list(names) + ["final_norm_g"]
    n_p = len(every)

    def body(*refs):
        p_ref = refs[0]
        w_refs = dict(zip(every, refs[1:1 + n_p]))
        m_refs = dict(zip(every, refs[1 + n_p:1 + 2 * n_p]))
        v_refs = dict(zip(every, refs[1 + 2 * n_p:1 + 3 * n_p]))
        l_out = refs[1 + 3 * n_p]
        outs = refs[2 + 3 * n_p:]
        o_refs = {n: outs[4 * q:4 * q + 4] for q, n in enumerate(every)}
        acc = p_ref[0]
        for q in range(1, n_dev):
            acc = acc + p_ref[q]
        tot = jnp.sum(acc, axis=0, keepdims=True)
        pos = 0
        where = [(n, l) for l in range(n_layers) for n in names] + [("final_norm_g", 0)]
        for n, l in where:
            width = w_refs[n].shape[1]
            g = tot[:, pos:pos + width]
            pos += width
            row = pl.ds(l, 1)
            delta, m_new, v_new = _adamw_math(g, w_refs[n][row, :], m_refs[n][row, :], v_refs[n][row, :])
            for o, val in zip(o_refs[n], (g, delta, m_new, v_new)):
                o[row, :] = val
        l_out[...] = (0.5 / n_loss) * jnp.sum(tot[:, pos:pos + n_loss], axis=-1, keepdims=True)

    shapes = [jax.ShapeDtypeStruct((1, 1), F32)]
    for n in every:
        shapes += [jax.ShapeDtypeStruct(w[n].shape, F32)] * 4
    res = pl.pallas_call(
        body, out_shape=shapes,
        compiler_params=pltpu.CompilerParams(vmem_limit_bytes=VMEM_LIMIT), name=name,
    )(parts, *[w[n] for n in every], *[m[n] for n in every], *[v[n] for n in every])
    return res[0], {n: res[1 + 4 * q:5 + 4 * q] for q, n in enumerate(every)}


BIG = ("w_in", "w_out", "w_up", "w_down")
COL_SHARDED = ("w_in", "w_up")
CONV = ("conv_a_w", "conv_b_w", "conv_f_w")
REPLICATED = ("mix_norm_g", "b_in", "conv_a_b", "ln_a_g", "ln_a_b", "ffn_norm_g")
KINDS = ("grad", "delta", "m", "v")
FFN_PART = ("w_up", "w_down")
MIX_PART = ("w_in", "w_out")


def _weights_from_gathered(name, g):
    n_dev, r, c = g.shape
    flat = g.reshape(n_dev * r, c)
    return (flat.T, flat) if name in COL_SHARDED else (flat, flat.T)


def _slabs_from_full(grad):
    return grad.reshape(N_DEV, grad.shape[0] // N_DEV, grad.shape[1])


def _pair_sums(slabs, core, tag):
    slabs = [s.reshape(N_CHIP, 2, *s.shape[1:]) for s in slabs]
    theirs = _sibling_exchange(slabs, name=f"reduce_siblings_{tag}")
    return [_pair_sum(a, b, core, name=f"pair_sum_{tag}_{q}") for q, (a, b) in enumerate(zip(slabs, theirs))]


def kernel(x, mix_norm_g, w_in, b_in, conv_a_w, conv_a_b, ln_a_g, ln_a_b, conv_b_w, w_out, ffn_norm_g, w_up, conv_f_w, w_down, final_norm_g, loss_target, m_mix_norm_g, m_w_in, m_b_in, m_conv_a_w, m_conv_a_b, m_ln_a_g, m_ln_a_b, m_conv_b_w, m_w_out, m_ffn_norm_g, m_w_up, m_conv_f_w, m_w_down, m_final_norm_g, v_mix_norm_g, v_w_in, v_b_in, v_conv_a_w, v_conv_a_b, v_ln_a_g, v_ln_a_b, v_conv_b_w, v_w_out, v_ffn_norm_g, v_w_up, v_conv_f_w, v_w_down, v_final_norm_g):
    w = dict(mix_norm_g=mix_norm_g, w_in=w_in, b_in=b_in, conv_a_w=conv_a_w, conv_a_b=conv_a_b, ln_a_g=ln_a_g,
             ln_a_b=ln_a_b, conv_b_w=conv_b_w, w_out=w_out, ffn_norm_g=ffn_norm_g, w_up=w_up, conv_f_w=conv_f_w,
             w_down=w_down, final_norm_g=final_norm_g)
    m = dict(mix_norm_g=m_mix_norm_g, w_in=m_w_in, b_in=m_b_in, conv_a_w=m_conv_a_w, conv_a_b=m_conv_a_b,
             ln_a_g=m_ln_a_g, ln_a_b=m_ln_a_b, conv_b_w=m_conv_b_w, w_out=m_w_out, ffn_norm_g=m_ffn_norm_g,
             w_up=m_w_up, conv_f_w=m_conv_f_w, w_down=m_w_down, final_norm_g=m_final_norm_g)
    v = dict(mix_norm_g=v_mix_norm_g, w_in=v_w_in, b_in=v_b_in, conv_a_w=v_conv_a_w, conv_a_b=v_conv_a_b,
             ln_a_g=v_ln_a_g, ln_a_b=v_ln_a_b, conv_b_w=v_conv_b_w, w_out=v_w_out, ffn_norm_g=v_ffn_norm_g,
             w_up=v_w_up, conv_f_w=v_conv_f_w, w_down=v_w_down, final_norm_g=v_final_norm_g)
    order = list(w)
    n_layers = w_in.shape[0]
    n_big = len(BIG)
    xs = x[0]
    target = loss_target[0]
    flip = lambda a: jnp.transpose(a, (0, 2, 1))
    wt, mt, vt = ({n: flip(d[n]) if n in COL_SHARDED else d[n] for n in BIG} for d in (w, m, v))
    px, py, pc = _place()
    core = pc.astype(jnp.int32).reshape(1)
    me = 4 * px + 2 * py + pc
    my_chip = 2 * px + py

    gathered = _all_gather([wt[n][0].astype(BF16) for n in BIG] + [w[n] for n in CONV], name="gather_weights_0")
    conv_full = {}
    for n, g in zip(CONV, gathered[n_big:]):
        n_dev, _, taps, c = g.shape
        conv_full[n] = g.transpose(1, 2, 0, 3).reshape(n_layers, taps, n_dev * c)

    def layer_params(l, big):
        p = {n: conv_full[n][l] for n in CONV}
        for n, g in zip(BIG, big):
            p[n], p[n + "_t"] = _weights_from_gathered(n, g)
        p.update({n: w[n][l] for n in REPLICATED})
        return p

    params = [layer_params(0, gathered[:n_big])]
    h = xs
    saved = []
    for l in range(n_layers):
        if l + 1 == n_layers:
            h, keep = _layer_fwd(h, params[l], str(l))
            saved.append(keep)
            break
        shards = [wt[n][l + 1].astype(BF16) for n in BIG]
        lands = [_landing((N_DEV, *s.shape), s.dtype, s, me) for s in shards]
        first = _split_start(shards, lands, _gather_plan_first, 4 * n_big, name=f"gather_first_start_{l + 1}")
        second = []

        def before_ffn(uf, first=first, second=second, tag=l + 1):
            arrived = _split_wait(first[0], first[1], first[2], first[3], uf, _gather_plan_first,
                                  name=f"gather_first_wait_{tag}")
            second.extend(_split_start([], arrived, _gather_plan_second, 3 * n_big, name=f"gather_second_start_{tag}"))
            return second[4]

        h, keep = _layer_fwd(h, params[l], str(l), dep=first[4], before_ffn=before_ffn)
        saved.append(keep)
        big = _split_wait(second[0], second[1], [], second[3], h, _gather_plan_second, name=f"gather_second_wait_{l + 1}")
        params.append(layer_params(l + 1, big))

    def start_reduce(slabs, tag):
        pairs = _pair_sums(slabs, core, tag)
        lands = [_landing(p.shape, p.dtype, lax.dynamic_index_in_dim(p, my_chip, 0, keepdims=False), my_chip)
                 for p in pairs]
        return _split_start(pairs, lands, _chips_plan, 3 * len(pairs), name=f"reduce_chips_start_{tag}")

    def finish_reduce(fly, after, tag):
        return _split_wait(fly[0], fly[1], fly[2], fly[3], after, _chips_plan, name=f"reduce_chips_wait_{tag}")

    loss_sq, dh, dh_b, dgf = _loss_bwd(h, _row(final_norm_g), target, name="loss")
    conv_g = {n: [None] * n_layers for n in CONV}
    rep_g = [None] * n_layers
    flights = {}
    token = None
    for l in reversed(range(n_layers)):
        def after_ffn(g, l=l):
            flights[l, "ffn"] = start_reduce([_slabs_from_full(g[n]) for n in FFN_PART], f"{l}_ffn")
            return flights[l, "ffn"][4]

        def after_mix(g, conv, l=l):
            for n in CONV:
                conv_g[n][l] = conv[n]
            slabs = [_slabs_from_full(g[n]) for n in MIX_PART]
            if l == 0:
                for n in CONV:
                    full = jnp.stack(conv_g[n])
                    _, taps, c = full.shape
                    slabs.append(full.reshape(n_layers, taps, N_DEV, c // N_DEV).transpose(2, 0, 1, 3)
                                 .reshape(N_DEV, n_layers * taps, c // N_DEV))
            flights[l, "mix"] = start_reduce(slabs, f"{l}_mix")
            return flights[l, "mix"][4]

        dh, dh_b, rep_g[l] = _layer_bwd(dh, dh_b, params[l], saved[l], str(l), after_ffn, after_mix, dep=token)
        token = flights[l, "mix"][4]

    sums = {key: finish_reduce(fly, dh, f"{key[0]}_{key[1]}") for key, fly in flights.items() if key != (0, "mix")}
    out = {k: {} for k in KINDS}

    def adamw_big(names, part, dep):
        for q, n in enumerate(names):
            layer_parts = [sums[l, part][q] for l in range(n_layers)]
            res = _adamw_sharded(layer_parts, wt[n], mt[n], vt[n], name=f"adamw_{n}", dep=dep)
            for k, r in zip(KINDS, res):
                out[k][n] = flip(r) if n in COL_SHARDED else r

    adamw_big(FFN_PART, "ffn", token)

    rep_cols = [rep_g[l][n] for l in range(n_layers) for n in REPLICATED] + [dgf, loss_sq]
    rep_all = _all_gather([_fold_partials(rep_cols, name="fold_small")], name="gather_small")[0]
    with_final = lambda d: {**{n: d[n] for n in REPLICATED}, "final_norm_g": _row(d["final_norm_g"])}
    loss, rep_res = _adamw_replicated(rep_all, REPLICATED, with_final(w), with_final(m), with_final(v),
                                      loss_sq.shape[1], name="adamw_small")
    for n, res in rep_res.items():
        for k, r in zip(KINDS, res):
            out[k][n] = r.reshape(w[n].shape)

    last = finish_reduce(flights[0, "mix"], rep_res["b_in"][0], "0_mix")
    sums[0, "mix"] = last[:len(MIX_PART)]
    adamw_big(MIX_PART, "mix", None)
    for n, p in zip(CONV, last[len(MIX_PART):]):
        as_one = lambda a: a.reshape(1, *p.shape[1:])
        for k, r in zip(KINDS, _adamw_sharded([p], as_one(w[n]), as_one(m[n]), as_one(v[n]), name=f"adamw_{n}")):
            out[k][n] = r.reshape(w[n].shape)

    grad_x = dh.reshape(x.shape)
    return (loss.reshape(()), grad_x, *[out["grad"][n] for n in order], *[out["delta"][n] for n in order],
            *[out["m"][n] for n in order], *[out["v"][n] for n in order])
```

```python
import functools

import jax
import jax.numpy as jnp
from jax import lax
from jax.experimental import pallas as pl
from jax.experimental.pallas import tpu as pltpu

F32 = jnp.float32
BF16 = jnp.bfloat16

N_DEV = 8
N_CHIP = 4
D_CONF = 512
CONF_K = 31
SHORT_K = 3
EPS = 1e-6
HALO = 32
HALO3 = 8
HALO3_BLK = 16
LANES = 128
SUB = 8
VMEM_LIMIT = 56 * 1024 * 1024

ADAM_LR = 0.001
ADAM_B1 = 0.9
ADAM_B2 = 0.999
ADAM_EPS = 1e-08
ADAM_WD = 0.01
ADAM_STEP = 10

MESH = pl.DeviceIdType.MESH
ANY = pl.BlockSpec(memory_space=pl.ANY)


def _params(*sem):
    return pltpu.CompilerParams(dimension_semantics=sem, vmem_limit_bytes=VMEM_LIMIT)


def _resident(shape, index_map):
    return pl.BlockSpec(shape, index_map, pipeline_mode=pl.Buffered(1))


def _row_loop(n_rows, rb, fn, unroll=1):
    rb = min(rb, n_rows)

    def body(i, carry):
        fn(pl.ds(pl.multiple_of(i * rb, rb), rb))
        return carry
    lax.fori_loop(0, n_rows // rb, body, 0, unroll=unroll)


def _rows8(v):
    acc = v[0:SUB]
    for k in range(1, v.shape[0] // SUB):
        acc = acc + v[k * SUB:(k + 1) * SUB]
    return acc


def _sigmoid(z):
    return 0.5 * jnp.tanh(0.5 * z) + 0.5


def _dot(a, b):
    return jnp.dot(a, b, preferred_element_type=F32)


def _dot_nt(a, b):
    return lax.dot_general(a, b, (((1,), (1,)), ((), ())), preferred_element_type=F32)


def _dot_tn(a, b):
    return lax.dot_general(a, b, (((0,), (0,)), ((), ())), preferred_element_type=F32)


def _replicate_taps(w_ref, wrep, taps):
    for k in range(taps):
        wrep[pl.ds(k * SUB, SUB), :] = jnp.broadcast_to(w_ref[pl.ds(k, 1), :], (SUB, w_ref.shape[1]))


def _shift_copies(win, shf, lanes):
    span = win.shape[0] - SUB
    for r in range(1, SUB):
        for j0 in range(0, span, 64):
            n = min(64, span - j0)
            shf[r - 1, pl.ds(j0, n), lanes] = win[pl.ds(j0 + r, n), lanes]


def _rows_at(win, shf, off, rb, lanes):
    if shf is None or off % SUB == 0:
        return win[pl.ds(off, rb), lanes]
    return shf[off % SUB - 1, pl.ds(off - off % SUB, rb), lanes]


def _conv_taps(win, wrep, out, *, taps, n_rows, base, width, transposed=False, bias_ref=None, shf=None):
    rb = min(64, n_rows)

    def lane_body(cb, carry):
        lanes = pl.ds(pl.multiple_of(cb * LANES, LANES), LANES)
        if shf is not None:
            _shift_copies(win, shf, lanes)
        for r0 in range(0, n_rows, rb):
            acc = None
            for k in range(taps):
                off = (taps - 1 - k) if transposed else (k - (taps - 1))
                wk = jnp.tile(wrep[pl.ds(k * SUB, SUB), lanes], (rb // SUB, 1))
                term = wk * _rows_at(win, shf, base + r0 + off, rb, lanes)
                acc = term if acc is None else acc + term
            if bias_ref is not None:
                acc = acc + bias_ref[:, lanes]
            out[pl.ds(r0, rb), lanes] = acc.astype(out.dtype)
        return carry

    lax.fori_loop(0, width // LANES, lane_body, 0)


def _conv_bwd_taps(win, wrep, x_cur, dx_out, dw_acc, *, taps, n_rows, width, shf=None):
    rb = min(32 if taps > 8 else 64, n_rows)

    def lane_body(cb, carry):
        lanes = pl.ds(pl.multiple_of(cb * LANES, LANES), LANES)
        if shf is not None:
            _shift_copies(win, shf, lanes)
        sums = [None] * taps
        for r0 in range(0, n_rows, rb):
            xv = x_cur[pl.ds(r0, rb), lanes].astype(F32)
            acc = None
            for k in range(taps):
                shifted = _rows_at(win, shf, r0 + taps - 1 - k, rb, lanes)
                term = jnp.tile(wrep[pl.ds(k * SUB, SUB), lanes], (rb // SUB, 1)) * shifted
                acc = term if acc is None else acc + term
                part = _rows8(xv * shifted)
                sums[k] = part if sums[k] is None else sums[k] + part
            dx_out[pl.ds(r0, rb), lanes] = acc.astype(dx_out.dtype)
        for k in range(taps):
            dw_acc[pl.ds(k * SUB, SUB), lanes] += sums[k]
        return carry

    lax.fori_loop(0, width // LANES, lane_body, 0)


def _fold8(acc_ref, taps):
    return jnp.concatenate(
        [jnp.sum(acc_ref[pl.ds(k * SUB, SUB), :], axis=0, keepdims=True) for k in range(taps)], axis=0)


def _seq_tile(s_len):
    return min(256, s_len)


def _mm_tile(s_len):
    return min(512, s_len)


def _ff_chunk(ff):
    best = LANES
    for c in range(LANES, 1408 + 1, LANES):
        if ff % c == 0:
            best = c
    return best


def _col_tile(n):
    for c in (512, 1408, 256, LANES):
        if n % c == 0:
            return c
    return n


def _rms_matmul(x, g, wt, b, *, name, dep=None):
    s_len, d = x.shape
    n = wt.shape[0]
    tm = _mm_tile(s_len)
    cn = _col_tile(n)
    has_bias = b is not None

    def body(*refs):
        x_ref, g_ref, w_ref = refs[0:3]
        b_ref = refs[3] if has_bias else None
        o_ref, h_ref = refs[-2:]

        def blk(rows):
            xv = x_ref[rows, :]
            r = lax.rsqrt(jnp.mean(xv * xv, axis=-1, keepdims=True) + EPS)
            h_ref[rows, :] = ((xv * r) * g_ref[...]).astype(BF16)
        _row_loop(tm, 128, blk)

        for j in range(n // cn):
            acc = _dot_nt(h_ref[...], w_ref[j * cn:(j + 1) * cn, :])
            if has_bias:
                acc = acc + b_ref[:, j * cn:(j + 1) * cn]
            o_ref[:, j * cn:(j + 1) * cn] = acc.astype(BF16)

    in_specs = [pl.BlockSpec((tm, d), lambda i: (i, 0)), _resident((1, d), lambda i: (0, 0)),
                _resident((n, d), lambda i: (0, 0))]
    args = [x, g, wt]
    if has_bias:
        in_specs.append(_resident((1, n), lambda i: (0, 0)))
        args.append(b)
    in_specs.append(ANY)
    args.append(x if dep is None else dep)
    return pl.pallas_call(
        body, grid=(s_len // tm,), in_specs=in_specs,
        out_specs=[pl.BlockSpec((tm, n), lambda i: (i, 0)), pl.BlockSpec((tm, d), lambda i: (i, 0))],
        out_shape=[jax.ShapeDtypeStruct((s_len, n), BF16), jax.ShapeDtypeStruct((s_len, d), BF16)],
        compiler_params=_params("parallel"), name=name,
    )(*args)


def _mix_windows(u_ref, uh_ref, gw, pw, first, t):
    c = D_CONF
    uh = uh_ref[...].astype(F32)
    gw[0:HALO, :] = jnp.where(first, 0.0, uh[:, 0:c] * _sigmoid(uh[:, c:2 * c]))
    pw[0:HALO3, :] = jnp.where(first, 0.0, uh[HALO - HALO3:HALO, 3 * c:4 * c] * uh[HALO - HALO3:HALO, 4 * c:5 * c])

    def blk(rows):
        dst = pl.ds(pl.multiple_of(rows.start + HALO, SUB), rows.size)
        gw[dst, :] = u_ref[rows, 0:c].astype(F32) * _sigmoid(u_ref[rows, c:2 * c].astype(F32))
        dst3 = pl.ds(pl.multiple_of(rows.start + HALO3, SUB), rows.size)
        pw[dst3, :] = u_ref[rows, 3 * c:4 * c].astype(F32) * u_ref[rows, 4 * c:5 * c].astype(F32)
    _row_loop(t, 64, blk)


def _mix_fwd(u, x0, wa, ba, lg, lb, wb, w_out, *, name):
    s_len, d_in = u.shape
    d = x0.shape[1]
    c = D_CONF
    t = _seq_tile(s_len)
    per = t // HALO

    def body(u_ref, uh_ref, x0_ref, wa_ref, ba_ref, lg_ref, lb_ref, wb_ref, wo_ref, y_ref, x1_ref, ca, cb,
             gw, pw, wrep_a, wrep_b, shf):
        first = pl.program_id(0) == 0
        _mix_windows(u_ref, uh_ref, gw, pw, first, t)
        _replicate_taps(wa_ref, wrep_a, CONF_K)
        _replicate_taps(wb_ref, wrep_b, SHORT_K)
        _conv_taps(gw, wrep_a, ca, taps=CONF_K, n_rows=t, base=HALO, width=c, bias_ref=ba_ref, shf=shf)
        _conv_taps(pw, wrep_b, cb, taps=SHORT_K, n_rows=t, base=HALO3, width=c)

        def blk(rows):
            cv = ca[rows, :]
            mu = jnp.mean(cv, axis=-1, keepdims=True)
            xc = cv - mu
            var = jnp.mean(xc * xc, axis=-1, keepdims=True)
            ln = (xc * lax.rsqrt(var + EPS)) * lg_ref[...] + lb_ref[...]
            y_ref[rows, 0:c] = (ln * _sigmoid(ln)).astype(BF16)
            y_ref[rows, c:2 * c] = (u_ref[rows, 2 * c:3 * c].astype(F32) * cb[rows, :]).astype(BF16)
        _row_loop(t, 64, blk)
        x1_ref[...] = x0_ref[...] + _dot(y_ref[...], wo_ref[...])

    small = lambda r: _resident((r, c), lambda i: (0, 0))
    return pl.pallas_call(
        body, grid=(s_len // t,),
        in_specs=[pl.BlockSpec((t, d_in), lambda i: (i, 0)),
                  pl.BlockSpec((HALO, d_in), lambda i: (jnp.maximum(i * per - 1, 0), 0)),
                  pl.BlockSpec((t, d), lambda i: (i, 0)),
                  small(CONF_K), small(1), small(1), small(1), small(SHORT_K),
                  _resident((2 * c, d), lambda i: (0, 0))],
        out_specs=[pl.BlockSpec((t, 2 * c), lambda i: (i, 0)), pl.BlockSpec((t, d), lambda i: (i, 0)),
                   pl.BlockSpec((t, c), lambda i: (i, 0)), pl.BlockSpec((t, c), lambda i: (i, 0))],
        out_shape=[jax.ShapeDtypeStruct((s_len, 2 * c), BF16), jax.ShapeDtypeStruct((s_len, d), F32),
                   jax.ShapeDtypeStruct((s_len, c), F32), jax.ShapeDtypeStruct((s_len, c), F32)],
        scratch_shapes=[pltpu.VMEM((HALO + t, c), F32), pltpu.VMEM((HALO3 + t, c), F32),
                        pltpu.VMEM((CONF_K * SUB, c), F32), pltpu.VMEM((SHORT_K * SUB, c), F32),
                        pltpu.VMEM((SUB - 1, HALO + t, c), F32)],
        compiler_params=_params("arbitrary"), name=name,
    )(u, u, x0, wa, ba, lg, lb, wb, w_out)


def _ffn_windows(ug_ref, ugh_ref, uv_ref, uvh_ref, gwin, vwin, first, t):
    lo = HALO3_BLK - HALO3
    gwin[0:HALO3, :] = jnp.where(first, 0.0, ugh_ref[...].astype(F32)[lo:HALO3_BLK])
    vwin[0:HALO3, :] = jnp.where(first, 0.0, uvh_ref[...].astype(F32)[lo:HALO3_BLK])

    def blk(rows):
        dst = pl.ds(pl.multiple_of(rows.start + HALO3, SUB), rows.size)
        gwin[dst, :] = ug_ref[rows, :].astype(F32)
        vwin[dst, :] = uv_ref[rows, :].astype(F32)
    _row_loop(t, 64, blk)


def _ffn_fwd(uf, x1, wf, w_down, *, name, dep=None):
    s_len, ff2 = uf.shape
    ff = ff2 // 2
    d = x1.shape[1]
    t = _seq_tile(s_len)
    fc = _ff_chunk(ff)
    nc = ff // fc
    per = t // HALO3_BLK

    def body(ug_ref, ugh_ref, uv_ref, uvh_ref, x1_ref, wfg_ref, wfv_ref, wd_ref, dep_ref,
             act_ref, x2_ref, cg_ref, cv_ref, gwin, vwin, cg, cv, wrep_g, wrep_v):
        first = pl.program_id(0) == 0
        _ffn_windows(ug_ref, ugh_ref, uv_ref, uvh_ref, gwin, vwin, first, t)
        _replicate_taps(wfg_ref, wrep_g, SHORT_K)
        _replicate_taps(wfv_ref, wrep_v, SHORT_K)
        _conv_taps(gwin, wrep_g, cg, taps=SHORT_K, n_rows=t, base=HALO3, width=fc)
        _conv_taps(vwin, wrep_v, cv, taps=SHORT_K, n_rows=t, base=HALO3, width=fc)

        def blk(rows):
            gv = cg[rows, :]
            vv = cv[rows, :]
            cg_ref[rows, :] = gv.astype(BF16)
            cv_ref[rows, :] = vv.astype(BF16)
            act_ref[rows, :] = ((gv * _sigmoid(gv)) * vv).astype(BF16)
        _row_loop(t, 32, blk, unroll=2)

        @pl.when(pl.program_id(1) == 0)
        def _():
            x2_ref[...] = x1_ref[...]
        x2_ref[...] += _dot(act_ref[...], wd_ref[...])

    halo_map = lambda off: (lambda i, j: (jnp.maximum(i * per - 1, 0), j + off))
    return pl.pallas_call(
        body, grid=(s_len // t, nc),
        in_specs=[pl.BlockSpec((t, fc), lambda i, j: (i, j)), pl.BlockSpec((HALO3_BLK, fc), halo_map(0)),
                  pl.BlockSpec((t, fc), lambda i, j: (i, j + nc)), pl.BlockSpec((HALO3_BLK, fc), halo_map(nc)),
                  pl.BlockSpec((t, d), lambda i, j: (i, 0)),
                  pl.BlockSpec((SHORT_K, fc), lambda i, j: (0, j)),
                  pl.BlockSpec((SHORT_K, fc), lambda i, j: (0, j + nc)),
                  pl.BlockSpec((fc, d), lambda i, j: (j, 0)), ANY],
        out_specs=[pl.BlockSpec((t, fc), lambda i, j: (i, j)), pl.BlockSpec((t, d), lambda i, j: (i, 0)),
                   pl.BlockSpec((t, fc), lambda i, j: (i, j)), pl.BlockSpec((t, fc), lambda i, j: (i, j))],
        out_shape=[jax.ShapeDtypeStruct((s_len, ff), BF16), jax.ShapeDtypeStruct((s_len, d), F32),
                   jax.ShapeDtypeStruct((s_len, ff), BF16), jax.ShapeDtypeStruct((s_len, ff), BF16)],
        scratch_shapes=[pltpu.VMEM((HALO3 + t, fc), F32), pltpu.VMEM((HALO3 + t, fc), F32),
                        pltpu.VMEM((t, fc), F32), pltpu.VMEM((t, fc), F32),
                        pltpu.VMEM((SHORT_K * SUB, fc), F32), pltpu.VMEM((SHORT_K * SUB, fc), F32)],
        compiler_params=_params("parallel", "arbitrary"), name=name,
    )(uf, uf, uf, uf, x1, wf, wf, w_down, uf if dep is None else dep)


def _loss_bwd(x, g, target, *, name):
    s_len, d = x.shape
    t = _seq_tile(s_len)

    def body(x_ref, g_ref, t_ref, l_ref, dx_ref, dxb_ref, dg_ref):
        @pl.when(pl.program_id(0) == 0)
        def _():
            l_ref[...] = jnp.zeros_like(l_ref)
            dg_ref[...] = jnp.zeros_like(dg_ref)

        def blk(rows):
            xv = x_ref[rows, :]
            r = lax.rsqrt(jnp.mean(xv * xv, axis=-1, keepdims=True) + EPS)
            xn = xv * r
            e = xn * g_ref[...] - t_ref[rows, :]
            l_ref[...] += _rows8(e * e)
            dy = e * (1.0 / d)
            dg_ref[...] += _rows8(dy * xn)
            dn = dy * g_ref[...]
            dx = r * (dn - xn * jnp.mean(dn * xn, axis=-1, keepdims=True))
            dx_ref[rows, :] = dx
            dxb_ref[rows, :] = dx.astype(BF16)
        _row_loop(t, 64, blk)

    row = pl.BlockSpec((t, d), lambda i: (i, 0))
    part = pl.BlockSpec((SUB, d), lambda i: (0, 0))
    return pl.pallas_call(
        body, grid=(s_len // t,),
        in_specs=[row, _resident((1, d), lambda i: (0, 0)), row],
        out_specs=[part, row, row, part],
        out_shape=[jax.ShapeDtypeStruct((SUB, d), F32), jax.ShapeDtypeStruct((s_len, d), F32),
                   jax.ShapeDtypeStruct((s_len, d), BF16), jax.ShapeDtypeStruct((SUB, d), F32)],
        compiler_params=_params("arbitrary"), name=name,
    )(x, g, target)


def _ffn_bwd(dx2, uf, cg, cv, wf, w_down, *, name, dep=None):
    s_len, ff2 = uf.shape
    ff = ff2 // 2
    d = dx2.shape[1]
    t = _seq_tile(s_len)
    n_t = s_len // t
    fc = _ff_chunk(ff)
    nc = ff // fc

    def body(dx_ref, ug_ref, uv_ref, cg_ref, cv_ref, wfg_ref, wfv_ref, wd_ref, dep_ref,
             dug_ref, duv_ref, dwg_ref, dwv_ref, dact, dgw, dvw, awg, awv, wrep_g, wrep_v):
        i = pl.program_id(1)
        dact[...] = _dot_nt(dx_ref[...], wd_ref[...])
        _replicate_taps(wfg_ref, wrep_g, SHORT_K)
        _replicate_taps(wfv_ref, wrep_v, SHORT_K)

        @pl.when(i == 0)
        def _():
            dgw[t:t + HALO3, :] = jnp.zeros((HALO3, fc), F32)
            dvw[t:t + HALO3, :] = jnp.zeros((HALO3, fc), F32)
            awg[...] = jnp.zeros_like(awg)
            awv[...] = jnp.zeros_like(awv)

        def blk(rows):
            gv = cg_ref[rows, :].astype(F32)
            sg = _sigmoid(gv)
            da = dact[rows, :]
            dgw[rows, :] = (da * cv_ref[rows, :].astype(F32)) * (sg * (1.0 + gv * (1.0 - sg)))
            dvw[rows, :] = da * (gv * sg)
        _row_loop(t, 32, blk, unroll=2)

        _conv_bwd_taps(dgw, wrep_g, ug_ref, dug_ref, awg, taps=SHORT_K, n_rows=t, width=fc)
        _conv_bwd_taps(dvw, wrep_v, uv_ref, duv_ref, awv, taps=SHORT_K, n_rows=t, width=fc)
        dgw[t:t + HALO3, :] = dgw[0:HALO3, :]
        dvw[t:t + HALO3, :] = dvw[0:HALO3, :]

        @pl.when(i == n_t - 1)
        def _():
            dwg_ref[...] = _fold8(awg, SHORT_K)
            dwv_ref[...] = _fold8(awv, SHORT_K)

    rev = lambda i: n_t - 1 - i
    gate = pl.BlockSpec((t, fc), lambda j, i: (rev(i), j))
    value = pl.BlockSpec((t, fc), lambda j, i: (rev(i), j + nc))
    return pl.pallas_call(
        body, grid=(nc, n_t),
        in_specs=[pl.BlockSpec((t, d), lambda j, i: (rev(i), 0)), gate, value, gate, gate,
                  pl.BlockSpec((SHORT_K, fc), lambda j, i: (0, j)),
                  pl.BlockSpec((SHORT_K, fc), lambda j, i: (0, j + nc)),
                  pl.BlockSpec((fc, d), lambda j, i: (j, 0)), ANY],
        out_specs=[gate, gate,
                   pl.BlockSpec((SHORT_K, fc), lambda j, i: (0, j)), pl.BlockSpec((SHORT_K, fc), lambda j, i: (0, j))],
        out_shape=[jax.ShapeDtypeStruct((s_len, ff), BF16), jax.ShapeDtypeStruct((s_len, ff), BF16),
                   jax.ShapeDtypeStruct((SHORT_K, ff), F32), jax.ShapeDtypeStruct((SHORT_K, ff), F32)],
        scratch_shapes=[pltpu.VMEM((t, fc), F32),
                        pltpu.VMEM((t + HALO3, fc), F32), pltpu.VMEM((t + HALO3, fc), F32),
                        pltpu.VMEM((SHORT_K * SUB, fc), F32), pltpu.VMEM((SHORT_K * SUB, fc), F32),
                        pltpu.VMEM((SHORT_K * SUB, fc), F32), pltpu.VMEM((SHORT_K * SUB, fc), F32)],
        compiler_params=_params("arbitrary", "arbitrary"), name=name,
    )(dx2, uf, uf, cg, cv, wf, wf, w_down, uf if dep is None else dep)


def _mix_bwd(dx1, u, ca, cb, wa, lg, lb, wb, w_out, *, name):
    s_len, d_in = u.shape
    d = dx1.shape[1]
    c = D_CONF
    t = _seq_tile(s_len)
    n_t = s_len // t

    def body(dx_ref, u_ref, ca_ref, cb_ref, wa_ref, lg_ref, lb_ref, wb_ref, wo_ref,
             du_ref, dwa_ref, dwb_ref, dba_ref, dlg_ref, dlb_ref, dbin_ref,
             glu, prod, dyc, dcaw, dcbw, dglu, dp, awa, awb, wrep_a, wrep_b, shf):
        i = pl.program_id(0)
        dyc[...] = _dot_nt(dx_ref[...], wo_ref[...])
        _replicate_taps(wa_ref, wrep_a, CONF_K)
        _replicate_taps(wb_ref, wrep_b, SHORT_K)

        @pl.when(i == 0)
        def _():
            dcaw[t:t + HALO, :] = jnp.zeros((HALO, c), F32)
            dcbw[t:t + HALO3, :] = jnp.zeros((HALO3, c), F32)
            awa[...] = jnp.zeros_like(awa)
            awb[...] = jnp.zeros_like(awb)
            dba_ref[...] = jnp.zeros_like(dba_ref)
            dlg_ref[...] = jnp.zeros_like(dlg_ref)
            dlb_ref[...] = jnp.zeros_like(dlb_ref)
            dbin_ref[...] = jnp.zeros_like(dbin_ref)

        def blk1(rows):
            cv = ca_ref[rows, :]
            mu = jnp.mean(cv, axis=-1, keepdims=True)
            xc = cv - mu
            rstd = lax.rsqrt(jnp.mean(xc * xc, axis=-1, keepdims=True) + EPS)
            nrm = xc * rstd
            ln = nrm * lg_ref[...] + lb_ref[...]
            sg = _sigmoid(ln)
            dln = dyc[rows, 0:c] * (sg * (1.0 + ln * (1.0 - sg)))
            dlg_ref[...] += _rows8(dln * nrm)
            dlb_ref[...] += _rows8(dln)
            dn = dln * lg_ref[...]
            dca = rstd * (dn - jnp.mean(dn, axis=-1, keepdims=True)
                          - nrm * jnp.mean(dn * nrm, axis=-1, keepdims=True))
            dcaw[rows, :] = dca
            dba_ref[...] += _rows8(dca)
            ds = dyc[rows, c:2 * c]
            dgb = ds * cb_ref[rows, :]
            dcbw[rows, :] = ds * u_ref[rows, 2 * c:3 * c].astype(F32)
            du_ref[rows, 2 * c:3 * c] = dgb.astype(BF16)
            dbin_ref[:, 2 * c:3 * c] += _rows8(dgb)
            glu[rows, :] = u_ref[rows, 0:c].astype(F32) * _sigmoid(u_ref[rows, c:2 * c].astype(F32))
            prod[rows, :] = u_ref[rows, 3 * c:4 * c].astype(F32) * u_ref[rows, 4 * c:5 * c].astype(F32)
        _row_loop(t, 64, blk1, unroll=2)

        _conv_bwd_taps(dcaw, wrep_a, glu, dglu, awa, taps=CONF_K, n_rows=t, width=c, shf=shf)
        _conv_bwd_taps(dcbw, wrep_b, prod, dp, awb, taps=SHORT_K, n_rows=t, width=c)
        dcaw[t:t + HALO, :] = dcaw[0:HALO, :]
        dcbw[t:t + HALO3, :] = dcbw[0:HALO3, :]

        def blk2(rows):
            av = u_ref[rows, 0:c].astype(F32)
            sg = _sigmoid(u_ref[rows, c:2 * c].astype(F32))
            dg = dglu[rows, :]
            d_av = dg * sg
            d_ag = (dg * av) * (sg * (1.0 - sg))
            dpv = dp[rows, :]
            d_gc = dpv * u_ref[rows, 4 * c:5 * c].astype(F32)
            d_vs = dpv * u_ref[rows, 3 * c:4 * c].astype(F32)
            du_ref[rows, 0:c] = d_av.astype(BF16)
            du_ref[rows, c:2 * c] = d_ag.astype(BF16)
            du_ref[rows, 3 * c:4 * c] = d_gc.astype(BF16)
            du_ref[rows, 4 * c:5 * c] = d_vs.astype(BF16)
            dbin_ref[:, 0:c] += _rows8(d_av)
            dbin_ref[:, c:2 * c] += _rows8(d_ag)
            dbin_ref[:, 3 * c:4 * c] += _rows8(d_gc)
            dbin_ref[:, 4 * c:5 * c] += _rows8(d_vs)
        _row_loop(t, 64, blk2)

        @pl.when(i == n_t - 1)
        def _():
            dwa_ref[...] = _fold8(awa, CONF_K)
            dwb_ref[...] = _fold8(awb, SHORT_K)

    rev = lambda i: n_t - 1 - i
    small_in = lambda r: _resident((r, c), lambda i: (0, 0))
    small = lambda r: pl.BlockSpec((r, c), lambda i: (0, 0))
    return pl.pallas_call(
        body, grid=(n_t,),
        in_specs=[pl.BlockSpec((t, d), lambda i: (rev(i), 0)),
                  pl.BlockSpec((t, d_in), lambda i: (rev(i), 0)),
                  pl.BlockSpec((t, c), lambda i: (rev(i), 0)), pl.BlockSpec((t, c), lambda i: (rev(i), 0)),
                  small_in(CONF_K), small_in(1), small_in(1), small_in(SHORT_K),
                  _resident((2 * c, d), lambda i: (0, 0))],
        out_specs=[pl.BlockSpec((t, d_in), lambda i: (rev(i), 0)),
                   small(CONF_K), small(SHORT_K), small(SUB), small(SUB), small(SUB),
                   pl.BlockSpec((SUB, d_in), lambda i: (0, 0))],
        out_shape=[jax.ShapeDtypeStruct((s_len, d_in), BF16),
                   jax.ShapeDtypeStruct((CONF_K, c), F32), jax.ShapeDtypeStruct((SHORT_K, c), F32),
                   jax.ShapeDtypeStruct((SUB, c), F32), jax.ShapeDtypeStruct((SUB, c), F32),
                   jax.ShapeDtypeStruct((SUB, c), F32), jax.ShapeDtypeStruct((SUB, d_in), F32)],
        scratch_shapes=[pltpu.VMEM((t, c), F32), pltpu.VMEM((t, c), F32), pltpu.VMEM((t, 2 * c), F32),
                        pltpu.VMEM((t + HALO, c), F32), pltpu.VMEM((t + HALO3, c), F32),
                        pltpu.VMEM((t, c), F32), pltpu.VMEM((t, c), F32),
                        pltpu.VMEM((CONF_K * SUB, c), F32), pltpu.VMEM((SHORT_K * SUB, c), F32),
                        pltpu.VMEM((CONF_K * SUB, c), F32), pltpu.VMEM((SHORT_K * SUB, c), F32),
                        pltpu.VMEM((SUB - 1, t + HALO, c), F32)],
        compiler_params=_params("arbitrary"), name=name,
    )(dx1, u, ca, cb, wa, lg, lb, wb, w_out)


def _matmul_tn(a, b, *, name, into=None, part=0, n_parts=1):
    s_len, k = a.shape
    n = b.shape[1]
    tk = _col_tile(k)
    per = k // tk

    def body(*refs):
        a_ref, b_ref = refs[0], refs[1]
        o_ref = refs[-1]
        o_ref[...] = _dot_tn(a_ref[...], b_ref[...]).astype(BF16)

    in_specs = [pl.BlockSpec((s_len, tk), lambda j: (0, j)), _resident((s_len, n), lambda j: (0, 0))]
    args = [a, b]
    aliases = {}
    if into is not None:
        in_specs.append(ANY)
        args.append(into)
        aliases = {2: 0}
    return pl.pallas_call(
        body, grid=(per,), in_specs=in_specs,
        out_specs=pl.BlockSpec((tk, n), lambda j: (part * per + j, 0)),
        out_shape=jax.ShapeDtypeStruct((n_parts * k, n), BF16),
        input_output_aliases=aliases,
        compiler_params=_params("parallel"), name=name,
    )(*args)


def _matmul_rmsbwd(dzs, wt, x, g, dx_in, *, name, dep=None):
    s_len, d = x.shape
    n_z = len(dzs)
    nj = dzs[0].shape[1]
    t = _mm_tile(s_len)

    def body(*refs):
        dz_refs = refs[0:n_z]
        w_refs = refs[n_z:2 * n_z]
        x_ref, g_ref, dxi_ref, _, dx_ref, dxb_ref, dg_ref, dh = refs[2 * n_z:]

        @pl.when(pl.program_id(0) == 0)
        def _():
            dg_ref[...] = jnp.zeros_like(dg_ref)

        acc = _dot(dz_refs[0][...], w_refs[0][...])
        for q in range(1, n_z):
            acc = acc + _dot(dz_refs[q][...], w_refs[q][...])
        dh[...] = acc

        def blk(rows):
            xv = x_ref[rows, :]
            r = lax.rsqrt(jnp.mean(xv * xv, axis=-1, keepdims=True) + EPS)
            xn = xv * r
            dhv = dh[rows, :]
            dg_ref[...] += _rows8(dhv * xn)
            dn = dhv * g_ref[...]
            dx = dxi_ref[rows, :] + r * (dn - xn * jnp.mean(dn * xn, axis=-1, keepdims=True))
            dx_ref[rows, :] = dx
            dxb_ref[rows, :] = dx.astype(BF16)
        _row_loop(t, 128, blk)

    row = pl.BlockSpec((t, d), lambda i: (i, 0))
    in_specs = [pl.BlockSpec((t, nj), lambda i: (i, 0)) for _ in range(n_z)]
    in_specs += [_resident((nj, d), functools.partial(lambda q, i: (q, 0), q)) for q in range(n_z)]
    in_specs += [row, _resident((1, d), lambda i: (0, 0)), row, ANY]
    return pl.pallas_call(
        body, grid=(s_len // t,), in_specs=in_specs,
        out_specs=[row, row, pl.BlockSpec((SUB, d), lambda i: (0, 0))],
        out_shape=[jax.ShapeDtypeStruct((s_len, d), F32), jax.ShapeDtypeStruct((s_len, d), BF16),
                   jax.ShapeDtypeStruct((SUB, d), F32)],
        scratch_shapes=[pltpu.VMEM((t, d), F32)],
        compiler_params=_params("arbitrary"), name=name,
    )(*dzs, *([wt] * n_z), x, g, dx_in, x if dep is None else dep)


def _row(v):
    return v.reshape(1, -1)


def _layer_fwd(x0, p, tag, dep=None, before_ffn=None):
    u, h1 = _rms_matmul(x0, _row(p["mix_norm_g"]), p["w_in_t"], _row(p["b_in"]), name=f"in_proj_{tag}", dep=dep)
    ycat, x1, ca, cb = _mix_fwd(u, x0, p["conv_a_w"], _row(p["conv_a_b"]), _row(p["ln_a_g"]), _row(p["ln_a_b"]),
                            p["conv_b_w"], p["w_out"], name=f"mix_fwd_{tag}")
    uf, h2 = _rms_matmul(x1, _row(p["ffn_norm_g"]), p["w_up_t"], None, name=f"up_proj_{tag}")
    dep_ffn = None if before_ffn is None else before_ffn(uf)
    act, x2, cg, cv = _ffn_fwd(uf, x1, p["conv_f_w"], p["w_down"], name=f"ffn_fwd_{tag}", dep=dep_ffn)
    return x2, dict(x0=x0, h1=h1, u=u, ca=ca, cb=cb, ycat=ycat, x1=x1, h2=h2, uf=uf, cg=cg, cv=cv, act=act)


def _layer_bwd(dx2, dx2_b, p, saved, tag, after_ffn, after_mix, dep=None):
    dug, duv, dwf_g, dwf_v = _ffn_bwd(dx2_b, saved["uf"], saved["cg"], saved["cv"], p["conv_f_w"], p["w_down"],
                                      name=f"ffn_bwd_{tag}", dep=dep)
    g_down = _matmul_tn(saved["act"], dx2_b, name=f"dw_down_{tag}")
    g_up = _matmul_tn(dug, saved["h2"], name=f"dw_up_g_{tag}", n_parts=2)
    g_up = _matmul_tn(duv, saved["h2"], name=f"dw_up_v_{tag}", into=g_up, part=1, n_parts=2)
    dep_ffn = after_ffn(dict(w_up=g_up, w_down=g_down))
    dx1, dx1_b, dg2 = _matmul_rmsbwd([dug, duv], p["w_up_t"], saved["x1"], _row(p["ffn_norm_g"]), dx2,
                                     name=f"dh_ffn_{tag}", dep=dep_ffn)
    du, dwa, dwb, dba, dlg, dlb, dbin = _mix_bwd(
        dx1_b, saved["u"], saved["ca"], saved["cb"], p["conv_a_w"], _row(p["ln_a_g"]), _row(p["ln_a_b"]),
        p["conv_b_w"], p["w_out"], name=f"mix_bwd_{tag}")
    g_out = _matmul_tn(saved["ycat"], dx1_b, name=f"dw_out_{tag}")
    g_in = _matmul_tn(du, saved["h1"], name=f"dw_in_{tag}")
    conv = dict(conv_a_w=dwa, conv_b_w=dwb, conv_f_w=jnp.concatenate([dwf_g, dwf_v], axis=1))
    dep_mix = after_mix(dict(w_in=g_in, w_out=g_out), conv)
    dx0, dx0_b, dg1 = _matmul_rmsbwd([du], p["w_in_t"], saved["x0"], _row(p["mix_norm_g"]), dx1,
                                     name=f"dh_mix_{tag}", dep=dep_mix)
    rep = dict(mix_norm_g=dg1, b_in=dbin, conv_a_b=dba, ln_a_g=dlg, ln_a_b=dlb, ffn_norm_g=dg2)
    return dx0, dx0_b, rep


def _place():
    return lax.axis_index("x"), lax.axis_index("y"), lax.axis_index("c")


def _all_gather(arrs, *, name):
    n_a = len(arrs)

    def body(*refs):
        ins = refs[0:n_a]
        outs = refs[n_a:2 * n_a]
        send_sems, recv_sems, local_sems = refs[2 * n_a:]
        x, y, c = _place()
        sibling = (x, y, 1 - c)
        chips = [(1 - x, y), (x, 1 - y), (1 - x, 1 - y)]

        def slot(a, px, py, pc):
            return outs[a].at[4 * px + 2 * py + pc]

        def copy(a, k, block, to, src=None):
            return pltpu.make_async_remote_copy(
                src_ref=slot(a, *block) if src is None else src, dst_ref=slot(a, *block),
                send_sem=send_sems.at[a, k], recv_sem=recv_sems.at[a, k],
                device_id=to, device_id_type=MESH)

        me = (x, y, c)
        mine = [pltpu.make_async_copy(ins[a], slot(a, *me), local_sems.at[a]) for a in range(n_a)]
        for cp in mine:
            cp.start()
        started = []
        for a in range(n_a):
            first = [copy(a, 0, me, sibling, src=ins[a])]
            first += [copy(a, 1 + j, me, (*chip, c), src=ins[a]) for j, chip in enumerate(chips)]
            for cp in first:
                cp.start()
            started += first
        for a in range(n_a):
            for j, chip in enumerate(chips):
                copy(a, 1 + j, (*chip, c), me).wait_recv()
                passed = copy(a, 4 + j, (*chip, c), sibling)
                passed.start()
                started.append(passed)
        for a in range(n_a):
            copy(a, 0, sibling, me).wait_recv()
            for j, chip in enumerate(chips):
                copy(a, 4 + j, (*chip, 1 - c), me).wait_recv()
        for cp in started:
            cp.wait_send()
        for cp in mine:
            cp.wait()

    return pl.pallas_call(
        body, in_specs=[ANY] * n_a, out_specs=[ANY] * n_a,
        out_shape=[jax.ShapeDtypeStruct((N_DEV, *a.shape), a.dtype) for a in arrs],
        scratch_shapes=[pltpu.SemaphoreType.DMA((n_a, 7)), pltpu.SemaphoreType.DMA((n_a, 7)),
                        pltpu.SemaphoreType.DMA((n_a,))],
        name=name,
    )(*arrs)


def _sibling_exchange(arrs, *, name):
    n_a = len(arrs)

    def body(*refs):
        ins = refs[0:n_a]
        outs = refs[n_a:2 * n_a]
        send_sems, recv_sems = refs[2 * n_a:]
        x, y, c = _place()
        copies = [pltpu.make_async_remote_copy(
            src_ref=ins[a].at[:, 1 - c], dst_ref=outs[a], send_sem=send_sems.at[a], recv_sem=recv_sems.at[a],
            device_id=(x, y, 1 - c), device_id_type=MESH) for a in range(n_a)]
        for cp in copies:
            cp.start()
        for cp in copies:
            cp.wait()

    return pl.pallas_call(
        body, in_specs=[ANY] * n_a, out_specs=[ANY] * n_a,
        out_shape=[jax.ShapeDtypeStruct((N_CHIP, *a.shape[2:]), a.dtype) for a in arrs],
        scratch_shapes=[pltpu.SemaphoreType.DMA((n_a,)), pltpu.SemaphoreType.DMA((n_a,))],
        name=name,
    )(*arrs)


def _row_tile(r, cap):
    for tr in range(min(cap, r) // 16 * 16, 0, -16):
        if r % tr == 0:
            return tr
    return r


def _pair_sum(mine, theirs, core, *, name):
    n_chip, _, r, c = mine.shape
    tr = _row_tile(r, 1024)

    def body(core_ref, a_ref, b_ref, o_ref):
        o_ref[...] = (a_ref[...].astype(F32) + b_ref[...].astype(F32)).astype(o_ref.dtype)

    return pl.pallas_call(
        body,
        grid_spec=pltpu.PrefetchScalarGridSpec(
            num_scalar_prefetch=1, grid=(n_chip, r // tr),
            in_specs=[pl.BlockSpec((None, None, tr, c), lambda q, i, core_ref: (q, core_ref[0], i, 0)),
                      pl.BlockSpec((None, tr, c), lambda q, i, core_ref: (q, i, 0))],
            out_specs=pl.BlockSpec((None, tr, c), lambda q, i, core_ref: (q, i, 0))),
        out_shape=jax.ShapeDtypeStruct((n_chip, r, c), mine.dtype),
        compiler_params=_params("parallel", "parallel"), name=name,
    )(core, mine, theirs)


def _chip_exchange(arrs, *, name):
    n_a = len(arrs)

    def body(*refs):
        ins = refs[0:n_a]
        outs = refs[n_a:2 * n_a]
        send_sems, recv_sems, local_sems = refs[2 * n_a:]
        x, y, c = _place()
        my_chip = 2 * x + y
        chips = [(1 - x, y), (x, 1 - y), (1 - x, 1 - y)]
        mine = [pltpu.make_async_copy(ins[a].at[my_chip], outs[a].at[my_chip], local_sems.at[a]) for a in range(n_a)]
        for cp in mine:
            cp.start()
        copies = []
        for a in range(n_a):
            for j, (px, py) in enumerate(chips):
                copies.append(pltpu.make_async_remote_copy(
                    src_ref=ins[a].at[2 * px + py], dst_ref=outs[a].at[my_chip],
                    send_sem=send_sems.at[a, j], recv_sem=recv_sems.at[a, j],
                    device_id=(px, py, c), device_id_type=MESH))
        for cp in copies:
            cp.start()
        for cp in copies:
            cp.wait()
        for cp in mine:
            cp.wait()

    return pl.pallas_call(
        body, in_specs=[ANY] * n_a, out_specs=[ANY] * n_a,
        out_shape=[jax.ShapeDtypeStruct(a.shape, a.dtype) for a in arrs],
        scratch_shapes=[pltpu.SemaphoreType.DMA((n_a, 3)), pltpu.SemaphoreType.DMA((n_a, 3)),
                        pltpu.SemaphoreType.DMA((n_a,))],
        name=name,
    )(*arrs)


HBM = pl.BlockSpec(memory_space=pltpu.HBM)
SEM = pl.BlockSpec(memory_space=pltpu.SEMAPHORE)
EFFECT = pltpu.SideEffectType.DATAFLOW_SIDE_EFFECTING


def _in_hbm(a):
    return pltpu.with_memory_space_constraint(a, pltpu.HBM)


def _split_start(srcs, lands, plan, n_copies, *, name):
    n_s, n_l = len(srcs), len(lands)

    def body(*refs):
        src_refs = refs[0:n_s]
        land_refs = refs[n_s:n_s + n_l]
        send_sems, recv_sems = refs[n_s + n_l], refs[n_s + n_l + 1]
        token = refs[-1]
        for cp in plan(src_refs, land_refs, send_sems, recv_sems):
            cp.start()
        token[...] = jnp.zeros_like(token)

    thru = [pltpu.HBM(a.shape, a.dtype) for a in list(srcs) + list(lands)]
    res = pl.pallas_call(
        body, name=name,
        out_shape=(pltpu.SemaphoreType.DMA((n_copies,)), pltpu.SemaphoreType.DMA((n_copies,)), *thru,
                   jax.ShapeDtypeStruct((SUB, LANES), F32)),
        in_specs=[HBM] * (n_s + n_l),
        out_specs=(SEM, SEM, *([HBM] * (n_s + n_l)), pl.BlockSpec(memory_space=pltpu.VMEM)),
        input_output_aliases={i: 2 + i for i in range(n_s + n_l)},
        compiler_params=pltpu.CompilerParams(has_side_effects=EFFECT),
    )(*[_in_hbm(a) for a in srcs], *[_in_hbm(a) for a in lands])
    return res[0], res[1], list(res[2:2 + n_s]), list(res[2 + n_s:2 + n_s + n_l]), res[-1]


def _split_wait(send_sems, recv_sems, srcs, lands, after, plan, *, name):
    n_s, n_l = len(srcs), len(lands)

    def body(*refs):
        src_refs = refs[0:n_s]
        land_refs = refs[n_s:n_s + n_l]
        send, recv = refs[n_s + n_l], refs[n_s + n_l + 1]
        for cp in plan(src_refs, land_refs, send, recv):
            cp.wait_send()
            cp.wait_recv()

    res = pl.pallas_call(
        body, name=name,
        out_shape=tuple(pltpu.HBM(a.shape, a.dtype) for a in list(srcs) + list(lands)),
        in_specs=[HBM] * (n_s + n_l) + [SEM, SEM, ANY],
        out_specs=tuple([HBM] * (n_s + n_l)),
        input_output_aliases={i: i for i in range(n_s + n_l)},
        compiler_params=pltpu.CompilerParams(has_side_effects=EFFECT),
    )(*srcs, *lands, send_sems, recv_sems, _in_hbm(after))
    return list(res[n_s:])


def _remote(src, dst, send_sems, recv_sems, k, to):
    return pltpu.make_async_remote_copy(src_ref=src, dst_ref=dst, send_sem=send_sems.at[k], recv_sem=recv_sems.at[k],
                                        device_id=to, device_id_type=MESH)


def _gather_plan_first(src_refs, land_refs, send_sems, recv_sems):
    x, y, c = _place()
    me = 4 * x + 2 * y + c
    peers = [(x, y, 1 - c), (1 - x, y, c), (x, 1 - y, c), (1 - x, 1 - y, c)]
    return [_remote(src, land.at[me], send_sems, recv_sems, 4 * a + k, to)
            for a, (src, land) in enumerate(zip(src_refs, land_refs)) for k, to in enumerate(peers)]


def _gather_plan_second(src_refs, land_refs, send_sems, recv_sems):
    x, y, c = _place()
    chips = [(1 - x, y), (x, 1 - y), (1 - x, 1 - y)]
    out = []
    for a, land in enumerate(land_refs):
        for j, (px, py) in enumerate(chips):
            slot = land.at[4 * px + 2 * py + c]
            out.append(_remote(slot, slot, send_sems, recv_sems, 3 * a + j, (x, y, 1 - c)))
    return out


def _chips_plan(src_refs, land_refs, send_sems, recv_sems):
    x, y, c = _place()
    my_chip = 2 * x + y
    chips = [(1 - x, y), (x, 1 - y), (1 - x, 1 - y)]
    return [_remote(src.at[2 * px + py], land.at[my_chip], send_sems, recv_sems, 3 * a + j, (px, py, c))
            for a, (src, land) in enumerate(zip(src_refs, land_refs)) for j, (px, py) in enumerate(chips)]


def _landing(like_shape, dtype, own, index):
    return lax.dynamic_update_index_in_dim(lax.empty(like_shape, dtype), own, index, 0)


def _adamw_math(g, w, m, v):
    m = ADAM_B1 * m + (1.0 - ADAM_B1) * g
    v = ADAM_B2 * v + (1.0 - ADAM_B2) * (g * g)
    m_hat = m / (1.0 - ADAM_B1 ** ADAM_STEP)
    v_hat = v / (1.0 - ADAM_B2 ** ADAM_STEP)
    delta = -ADAM_LR * (m_hat / (jnp.sqrt(v_hat) + ADAM_EPS) + ADAM_WD * w)
    return delta, m, v


def _adamw_sharded(parts, w, m, v, *, name, dep=None):
    n_layers, r, c = w.shape
    n_chip = parts[0].shape[0]
    tr = _row_tile(r, 384)
    n_i = r // tr

    def body(*refs):
        p_refs = refs[0:n_layers]
        w_ref, m_ref, v_ref, _, g_out, d_out, m_out, v_out = refs[n_layers:]
        layer = pl.program_id(0)
        for l in range(n_layers):
            @pl.when(layer == l)
            def _(l=l):
                g = p_refs[l][0].astype(F32)
                for q in range(1, n_chip):
                    g = g + p_refs[l][q].astype(F32)
                delta, m_new, v_new = _adamw_math(g, w_ref[...], m_ref[...], v_ref[...])
                g_out[...] = g
                d_out[...] = delta
                m_out[...] = m_new
                v_out[...] = v_new

    def part_map(l):
        return lambda layer, i: (0, jnp.where(layer == l, i, jnp.where(layer < l, 0, n_i - 1)), 0)

    blk = pl.BlockSpec((None, tr, c), lambda layer, i: (layer, i, 0))
    return pl.pallas_call(
        body, grid=(n_layers, n_i),
        in_specs=[pl.BlockSpec((n_chip, tr, c), part_map(l)) for l in range(n_layers)] + [blk, blk, blk, ANY],
        out_specs=[blk] * 4, out_shape=[jax.ShapeDtypeStruct((n_layers, r, c), F32)] * 4,
        compiler_params=_params("arbitrary", "arbitrary"), name=name,
    )(*parts, w, m, v, w if dep is None else dep)


def _fold_partials(cols, *, name):
    widths = [c.shape[1] for c in cols]

    def body(*refs):
        o_ref = refs[-1]
        pos = 0
        for ref, width in zip(refs[:-1], widths):
            o_ref[:, pos:pos + width] = jnp.sum(ref[...], axis=0, keepdims=True)
            pos += width

    return pl.pallas_call(body, out_shape=jax.ShapeDtypeStruct((1, sum(widths)), F32), name=name)(*cols)


def _adamw_replicated(parts, names, w, m, v, n_loss, *, name):
    n_dev = parts.shape[0]
    n_layers = w[names[0]].shape[0]
    every = list(names) + ["final_norm_g"]
    n_p = len(every)

    def body(*refs):
        p_ref = refs[0]
        w_refs = dict(zip(every, refs[1:1 + n_p]))
        m_refs = dict(zip(every, refs[1 + n_p:1 + 2 * n_p]))
        v_refs = dict(zip(every, refs[1 + 2 * n_p:1 + 3 * n_p]))
        l_out = refs[1 + 3 * n_p]
        outs = refs[2 + 3 * n_p:]
        o_refs = {n: outs[4 * q:4 * q + 4] for q, n in enumerate(every)}
        acc = p_ref[0]
        for q in range(1, n_dev):
            acc = acc + p_ref[q]
        tot = jnp.sum(acc, axis=0, keepdims=True)
        pos = 0
        where = [(n, l) for l in range(n_layers) for n in names] + [("final_norm_g", 0)]
        for n, l in where:
            width = w_refs[n].shape[1]
            g = tot[:, pos:pos + width]
            pos += width
            row = pl.ds(l, 1)
            delta, m_new, v_new = _adamw_math(g, w_refs[n][row, :], m_refs[n][row, :], v_refs[n][row, :])
            for o, val in zip(o_refs[n], (g, delta, m_new, v_new)):
                o[row, :] = val
        l_out[...] = (0.5 / n_loss) * jnp.sum(tot[:, pos:pos + n_loss], axis=-1, keepdims=True)

    shapes = [jax.ShapeDtypeStruct((1, 1), F32)]
    for n in every:
        shapes += [jax.ShapeDtypeStruct(w[n].shape, F32)] * 4
    res = pl.pallas_call(
        body, out_shape=shapes,
        compiler_params=pltpu.CompilerParams(vmem_limit_bytes=VMEM_LIMIT), name=name,
    )(parts, *[w[n] for n in every], *[m[n] for n in every], *[v[n] for n in every])
    return res[0], {n: res[1 + 4 * q:5 + 4 * q] for q, n in enumerate(every)}


BIG = ("w_in", "w_out", "w_up", "w_down")
COL_SHARDED = ("w_in", "w_up")
CONV = ("conv_a_w", "conv_b_w", "conv_f_w")
REPLICATED = ("mix_norm_g", "b_in", "conv_a_b", "ln_a_g", "ln_a_b", "ffn_norm_g")
KINDS = ("grad", "delta", "m", "v")
FFN_PART = ("w_up", "w_down")
MIX_PART = ("w_in", "w_out")


def _weights_from_gathered(g):
    n_dev, r, c = g.shape
    return g.reshape(n_dev * r, c)


def _slabs_from_full(grad):
    return grad.reshape(N_DEV, grad.shape[0] // N_DEV, grad.shape[1])


def _pair_sums(slabs, core, tag):
    slabs = [s.reshape(N_CHIP, 2, *s.shape[1:]) for s in slabs]
    theirs = _sibling_exchange(slabs, name=f"reduce_siblings_{tag}")
    return [_pair_sum(a, b, core, name=f"pair_sum_{tag}_{q}") for q, (a, b) in enumerate(zip(slabs, theirs))]


def kernel(x, mix_norm_g, w_in, b_in, conv_a_w, conv_a_b, ln_a_g, ln_a_b, conv_b_w, w_out, ffn_norm_g, w_up, conv_f_w, w_down, final_norm_g, loss_target, m_mix_norm_g, m_w_in, m_b_in, m_conv_a_w, m_conv_a_b, m_ln_a_g, m_ln_a_b, m_conv_b_w, m_w_out, m_ffn_norm_g, m_w_up, m_conv_f_w, m_w_down, m_final_norm_g, v_mix_norm_g, v_w_in, v_b_in, v_conv_a_w, v_conv_a_b, v_ln_a_g, v_ln_a_b, v_conv_b_w, v_w_out, v_ffn_norm_g, v_w_up, v_conv_f_w, v_w_down, v_final_norm_g):
    w = dict(mix_norm_g=mix_norm_g, w_in=w_in, b_in=b_in, conv_a_w=conv_a_w, conv_a_b=conv_a_b, ln_a_g=ln_a_g,
             ln_a_b=ln_a_b, conv_b_w=conv_b_w, w_out=w_out, ffn_norm_g=ffn_norm_g, w_up=w_up, conv_f_w=conv_f_w,
             w_down=w_down, final_norm_g=final_norm_g)
    m = dict(mix_norm_g=m_mix_norm_g, w_in=m_w_in, b_in=m_b_in, conv_a_w=m_conv_a_w, conv_a_b=m_conv_a_b,
             ln_a_g=m_ln_a_g, ln_a_b=m_ln_a_b, conv_b_w=m_conv_b_w, w_out=m_w_out, ffn_norm_g=m_ffn_norm_g,
             w_up=m_w_up, conv_f_w=m_conv_f_w, w_down=m_w_down, final_norm_g=m_final_norm_g)
    v = dict(mix_norm_g=v_mix_norm_g, w_in=v_w_in, b_in=v_b_in, conv_a_w=v_conv_a_w, conv_a_b=v_conv_a_b,
             ln_a_g=v_ln_a_g, ln_a_b=v_ln_a_b, conv_b_w=v_conv_b_w, w_out=v_w_out, ffn_norm_g=v_ffn_norm_g,
             w_up=v_w_up, conv_f_w=v_conv_f_w, w_down=v_w_down, final_norm_g=v_final_norm_g)
    order = list(w)
    n_layers = w_in.shape[0]
    n_big = len(BIG)
    xs = x[0]
    target = loss_target[0]
    flip = lambda a: jnp.transpose(a, (0, 2, 1))
    wt, mt, vt = ({n: flip(d[n]) if n in COL_SHARDED else d[n] for n in BIG} for d in (w, m, v))
    px, py, pc = _place()
    core = pc.astype(jnp.int32).reshape(1)
    me = 4 * px + 2 * py + pc
    my_chip = 2 * px + py

    gathered = _all_gather([wt[n][0].astype(BF16) for n in BIG] + [w[n] for n in CONV], name="gather_weights_0")
    conv_full = {}
    for n, g in zip(CONV, gathered[n_big:]):
        n_dev, _, taps, c = g.shape
        conv_full[n] = g.transpose(1, 2, 0, 3).reshape(n_layers, taps, n_dev * c)

    def layer_params(l, big):
        p = {n: conv_full[n][l] for n in CONV}
        for n, g in zip(BIG, big):
            p[n + "_t" if n in COL_SHARDED else n] = _weights_from_gathered(g)
        p.update({n: w[n][l] for n in REPLICATED})
        return p

    params = [layer_params(0, gathered[:n_big])]
    h = xs
    saved = []
    for l in range(n_layers):
        if l + 1 == n_layers:
            h, keep = _layer_fwd(h, params[l], str(l))
            saved.append(keep)
            break
        shards = [wt[n][l + 1].astype(BF16) for n in BIG]
        lands = [_landing((N_DEV, *s.shape), s.dtype, s, me) for s in shards]
        first = _split_start(shards, lands, _gather_plan_first, 4 * n_big, name=f"gather_first_start_{l + 1}")
        second = []

        def before_ffn(uf, first=first, second=second, tag=l + 1):
            arrived = _split_wait(first[0], first[1], first[2], first[3], uf, _gather_plan_first,
                                  name=f"gather_first_wait_{tag}")
            second.extend(_split_start([], arrived, _gather_plan_second, 3 * n_big, name=f"gather_second_start_{tag}"))
            return second[4]

        h, keep = _layer_fwd(h, params[l], str(l), dep=first[4], before_ffn=before_ffn)
        saved.append(keep)
        big = _split_wait(second[0], second[1], [], second[3], h, _gather_plan_second, name=f"gather_second_wait_{l + 1}")
        params.append(layer_params(l + 1, big))

    def start_reduce(slabs, tag):
        pairs = _pair_sums(slabs, core, tag)
        lands = [_landing(p.shape, p.dtype, lax.dynamic_index_in_dim(p, my_chip, 0, keepdims=False), my_chip)
                 for p in pairs]
        return _split_start(pairs, lands, _chips_plan, 3 * len(pairs), name=f"reduce_chips_start_{tag}")

    def finish_reduce(fly, after, tag):
        return _split_wait(fly[0], fly[1], fly[2], fly[3], after, _chips_plan, name=f"reduce_chips_wait_{tag}")

    loss_sq, dh, dh_b, dgf = _loss_bwd(h, _row(final_norm_g), target, name="loss")
    conv_g = {n: [None] * n_layers for n in CONV}
    rep_g = [None] * n_layers
    flights = {}
    token = None
    for l in reversed(range(n_layers)):
        def after_ffn(g, l=l):
            flights[l, "ffn"] = start_reduce([_slabs_from_full(g[n]) for n in FFN_PART], f"{l}_ffn")
            return flights[l, "ffn"][4]

        def after_mix(g, conv, l=l):
            for n in CONV:
                conv_g[n][l] = conv[n]
            slabs = [_slabs_from_full(g[n]) for n in MIX_PART]
            if l == 0:
                for n in CONV:
                    full = jnp.stack(conv_g[n])
                    _, taps, c = full.shape
                    slabs.append(full.reshape(n_layers, taps, N_DEV, c // N_DEV).transpose(2, 0, 1, 3)
                                 .reshape(N_DEV, n_layers * taps, c // N_DEV))
            flights[l, "mix"] = start_reduce(slabs, f"{l}_mix")
            return flights[l, "mix"][4]

        dh, dh_b, rep_g[l] = _layer_bwd(dh, dh_b, params[l], saved[l], str(l), after_ffn, after_mix, dep=token)
        token = flights[l, "mix"][4]

    sums = {key: finish_reduce(fly, dh, f"{key[0]}_{key[1]}") for key, fly in flights.items() if key != (0, "mix")}
    out = {k: {} for k in KINDS}

    def adamw_big(names, part, dep):
        for q, n in enumerate(names):
            layer_parts = [sums[l, part][q] for l in range(n_layers)]
            res = _adamw_sharded(layer_parts, wt[n], mt[n], vt[n], name=f"adamw_{n}", dep=dep)
            for k, r in zip(KINDS, res):
                out[k][n] = flip(r) if n in COL_SHARDED else r

    adamw_big(FFN_PART, "ffn", token)

    rep_cols = [rep_g[l][n] for l in range(n_layers) for n in REPLICATED] + [dgf, loss_sq]
    rep_all = _all_gather([_fold_partials(rep_cols, name="fold_small")], name="gather_small")[0]
    with_final = lambda d: {**{n: d[n] for n in REPLICATED}, "final_norm_g": _row(d["final_norm_g"])}
    loss, rep_res = _adamw_replicated(rep_all, REPLICATED, with_final(w), with_final(m), with_final(v),
                                      loss_sq.shape[1], name="adamw_small")
    for n, res in rep_res.items():
        for k, r in zip(KINDS, res):
            out[k][n] = r.reshape(w[n].shape)

    last = finish_reduce(flights[0, "mix"], rep_res["b_in"][0], "0_mix")
    sums[0, "mix"] = last[:len(MIX_PART)]
    adamw_big(MIX_PART, "mix", None)
    for n, p in zip(CONV, last[len(MIX_PART):]):
        as_one = lambda a: a.reshape(1, *p.shape[1:])
        for k, r in zip(KINDS, _adamw_sharded([p], as_one(w[n]), as_one(m[n]), as_one(v[n]), name=f"adamw_{n}")):
            out[k][n] = r.reshape(w[n].shape)

    grad_x = dh.reshape(x.shape)
    return (loss.reshape(()), grad_x, *[out["grad"][n] for n in order], *[out["delta"][n] for n in order],
            *[out["m"][n] for n in order], *[out["v"][n] for n in order])
```

```python
import functools

import jax
import jax.numpy as jnp
from jax import lax
from jax.experimental import pallas as pl
from jax.experimental.pallas import tpu as pltpu

F32 = jnp.float32
BF16 = jnp.bfloat16

N_DEV = 8
N_CHIP = 4
D_CONF = 512
CONF_K = 31
SHORT_K = 3
EPS = 1e-6
HALO = 32
HALO3 = 8
HALO3_BLK = 16
LANES = 128
SUB = 8
VMEM_LIMIT = 56 * 1024 * 1024

ADAM_LR = 0.001
ADAM_B1 = 0.9
ADAM_B2 = 0.999
ADAM_EPS = 1e-08
ADAM_WD = 0.01
ADAM_STEP = 10

MESH = pl.DeviceIdType.MESH
ANY = pl.BlockSpec(memory_space=pl.ANY)


def _params(*sem):
    return pltpu.CompilerParams(dimension_semantics=sem, vmem_limit_bytes=VMEM_LIMIT)


def _resident(shape, index_map):
    return pl.BlockSpec(shape, index_map, pipeline_mode=pl.Buffered(1))


def _row_loop(n_rows, rb, fn, unroll=1):
    rb = min(rb, n_rows)

    def body(i, carry):
        fn(pl.ds(pl.multiple_of(i * rb, rb), rb))
        return carry
    lax.fori_loop(0, n_rows // rb, body, 0, unroll=unroll)


def _rows8(v):
    acc = v[0:SUB]
    for k in range(1, v.shape[0] // SUB):
        acc = acc + v[k * SUB:(k + 1) * SUB]
    return acc


def _sigmoid(z):
    return 0.5 * jnp.tanh(0.5 * z) + 0.5


def _dot(a, b):
    return jnp.dot(a, b, preferred_element_type=F32)


def _dot_nt(a, b):
    return lax.dot_general(a, b, (((1,), (1,)), ((), ())), preferred_element_type=F32)


def _dot_tn(a, b):
    return lax.dot_general(a, b, (((0,), (0,)), ((), ())), preferred_element_type=F32)


def _replicate_taps(w_ref, wrep, taps):
    for k in range(taps):
        wrep[pl.ds(k * SUB, SUB), :] = jnp.broadcast_to(w_ref[pl.ds(k, 1), :], (SUB, w_ref.shape[1]))


def _shift_copies(win, shf, lanes):
    span = win.shape[0] - SUB
    for r in range(1, SUB):
        for j0 in range(0, span, 64):
            n = min(64, span - j0)
            shf[r - 1, pl.ds(j0, n), lanes] = win[pl.ds(j0 + r, n), lanes]


def _rows_at(win, shf, off, rb, lanes):
    if shf is None or off % SUB == 0:
        return win[pl.ds(off, rb), lanes]
    return shf[off % SUB - 1, pl.ds(off - off % SUB, rb), lanes]


def _conv_taps(win, wrep, out, *, taps, n_rows, base, width, transposed=False, bias_ref=None, shf=None):
    rb = min(64, n_rows)

    def lane_body(cb, carry):
        lanes = pl.ds(pl.multiple_of(cb * LANES, LANES), LANES)
        if shf is not None:
            _shift_copies(win, shf, lanes)
        for r0 in range(0, n_rows, rb):
            acc = None
            for k in range(taps):
                off = (taps - 1 - k) if transposed else (k - (taps - 1))
                wk = jnp.tile(wrep[pl.ds(k * SUB, SUB), lanes], (rb // SUB, 1))
                term = wk * _rows_at(win, shf, base + r0 + off, rb, lanes)
                acc = term if acc is None else acc + term
            if bias_ref is not None:
                acc = acc + bias_ref[:, lanes]
            out[pl.ds(r0, rb), lanes] = acc.astype(out.dtype)
        return carry

    lax.fori_loop(0, width // LANES, lane_body, 0)


def _conv_bwd_taps(win, wrep, x_cur, dx_out, dw_acc, *, taps, n_rows, width, shf=None):
    rb = min(32 if taps > 8 else 64, n_rows)

    def lane_body(cb, carry):
        lanes = pl.ds(pl.multiple_of(cb * LANES, LANES), LANES)
        if shf is not None:
            _shift_copies(win, shf, lanes)
        sums = [None] * taps
        for r0 in range(0, n_rows, rb):
            xv = x_cur[pl.ds(r0, rb), lanes].astype(F32)
            acc = None
            for k in range(taps):
                shifted = _rows_at(win, shf, r0 + taps - 1 - k, rb, lanes)
                term = jnp.tile(wrep[pl.ds(k * SUB, SUB), lanes], (rb // SUB, 1)) * shifted
                acc = term if acc is None else acc + term
                part = _rows8(xv * shifted)
                sums[k] = part if sums[k] is None else sums[k] + part
            dx_out[pl.ds(r0, rb), lanes] = acc.astype(dx_out.dtype)
        for k in range(taps):
            dw_acc[pl.ds(k * SUB, SUB), lanes] += sums[k]
        return carry

    lax.fori_loop(0, width // LANES, lane_body, 0)


def _fold8(acc_ref, taps):
    return jnp.concatenate(
        [jnp.sum(acc_ref[pl.ds(k * SUB, SUB), :], axis=0, keepdims=True) for k in range(taps)], axis=0)


def _seq_tile(s_len):
    return min(256, s_len)


def _mm_tile(s_len):
    return min(512, s_len)


def _ff_chunk(ff):
    best = LANES
    for c in range(LANES, 1408 + 1, LANES):
        if ff % c == 0:
            best = c
    return best


def _col_tile(n):
    for c in (512, 1408, 256, LANES):
        if n % c == 0:
            return c
    return n


def _rms_matmul(x, g, wt, b, *, name, dep=None):
    s_len, d = x.shape
    n = wt.shape[0]
    tm = _mm_tile(s_len)
    cn = _col_tile(n)
    has_bias = b is not None

    def body(*refs):
        x_ref, g_ref, w_ref = refs[0:3]
        b_ref = refs[3] if has_bias else None
        o_ref, h_ref = refs[-2:]

        def blk(rows):
            xv = x_ref[rows, :]
            r = lax.rsqrt(jnp.mean(xv * xv, axis=-1, keepdims=True) + EPS)
            h_ref[rows, :] = ((xv * r) * g_ref[...]).astype(BF16)
        _row_loop(tm, 128, blk)

        for j in range(n // cn):
            acc = _dot_nt(h_ref[...], w_ref[j * cn:(j + 1) * cn, :])
            if has_bias:
                acc = acc + b_ref[:, j * cn:(j + 1) * cn]
            o_ref[:, j * cn:(j + 1) * cn] = acc.astype(BF16)

    in_specs = [pl.BlockSpec((tm, d), lambda i: (i, 0)), _resident((1, d), lambda i: (0, 0)),
                _resident((n, d), lambda i: (0, 0))]
    args = [x, g, wt]
    if has_bias:
        in_specs.append(_resident((1, n), lambda i: (0, 0)))
        args.append(b)
    in_specs.append(ANY)
    args.append(x if dep is None else dep)
    return pl.pallas_call(
        body, grid=(s_len // tm,), in_specs=in_specs,
        out_specs=[pl.BlockSpec((tm, n), lambda i: (i, 0)), pl.BlockSpec((tm, d), lambda i: (i, 0))],
        out_shape=[jax.ShapeDtypeStruct((s_len, n), BF16), jax.ShapeDtypeStruct((s_len, d), BF16)],
        compiler_params=_params("parallel"), name=name,
    )(*args)


def _mix_windows(u_ref, uh_ref, gw, pw, first, t):
    c = D_CONF
    uh = uh_ref[...].astype(F32)
    gw[0:HALO, :] = jnp.where(first, 0.0, uh[:, 0:c] * _sigmoid(uh[:, c:2 * c]))
    pw[0:HALO3, :] = jnp.where(first, 0.0, uh[HALO - HALO3:HALO, 3 * c:4 * c] * uh[HALO - HALO3:HALO, 4 * c:5 * c])

    def blk(rows):
        dst = pl.ds(pl.multiple_of(rows.start + HALO, SUB), rows.size)
        gw[dst, :] = u_ref[rows, 0:c].astype(F32) * _sigmoid(u_ref[rows, c:2 * c].astype(F32))
        dst3 = pl.ds(pl.multiple_of(rows.start + HALO3, SUB), rows.size)
        pw[dst3, :] = u_ref[rows, 3 * c:4 * c].astype(F32) * u_ref[rows, 4 * c:5 * c].astype(F32)
    _row_loop(t, 64, blk)


def _mix_fwd(u, x0, wa, ba, lg, lb, wb, w_out, *, name):
    s_len, d_in = u.shape
    d = x0.shape[1]
    c = D_CONF
    t = _seq_tile(s_len)
    per = t // HALO

    def body(u_ref, uh_ref, x0_ref, wa_ref, ba_ref, lg_ref, lb_ref, wb_ref, wo_ref, y_ref, x1_ref, ca, cb,
             gw, pw, wrep_a, wrep_b, shf):
        first = pl.program_id(0) == 0
        _mix_windows(u_ref, uh_ref, gw, pw, first, t)
        _replicate_taps(wa_ref, wrep_a, CONF_K)
        _replicate_taps(wb_ref, wrep_b, SHORT_K)
        _conv_taps(gw, wrep_a, ca, taps=CONF_K, n_rows=t, base=HALO, width=c, bias_ref=ba_ref, shf=shf)
        _conv_taps(pw, wrep_b, cb, taps=SHORT_K, n_rows=t, base=HALO3, width=c)

        def blk(rows):
            cv = ca[rows, :]
            mu = jnp.mean(cv, axis=-1, keepdims=True)
            xc = cv - mu
            var = jnp.mean(xc * xc, axis=-1, keepdims=True)
            ln = (xc * lax.rsqrt(var + EPS)) * lg_ref[...] + lb_ref[...]
            y_ref[rows, 0:c] = (ln * _sigmoid(ln)).astype(BF16)
            y_ref[rows, c:2 * c] = (u_ref[rows, 2 * c:3 * c].astype(F32) * cb[rows, :]).astype(BF16)
        _row_loop(t, 64, blk)
        x1_ref[...] = x0_ref[...] + _dot(y_ref[...], wo_ref[...])

    small = lambda r: _resident((r, c), lambda i: (0, 0))
    return pl.pallas_call(
        body, grid=(s_len // t,),
        in_specs=[pl.BlockSpec((t, d_in), lambda i: (i, 0)),
                  pl.BlockSpec((HALO, d_in), lambda i: (jnp.maximum(i * per - 1, 0), 0)),
                  pl.BlockSpec((t, d), lambda i: (i, 0)),
                  small(CONF_K), small(1), small(1), small(1), small(SHORT_K),
                  _resident((2 * c, d), lambda i: (0, 0))],
        out_specs=[pl.BlockSpec((t, 2 * c), lambda i: (i, 0)), pl.BlockSpec((t, d), lambda i: (i, 0)),
                   pl.BlockSpec((t, c), lambda i: (i, 0)), pl.BlockSpec((t, c), lambda i: (i, 0))],
        out_shape=[jax.ShapeDtypeStruct((s_len, 2 * c), BF16), jax.ShapeDtypeStruct((s_len, d), F32),
                   jax.ShapeDtypeStruct((s_len, c), F32), jax.ShapeDtypeStruct((s_len, c), F32)],
        scratch_shapes=[pltpu.VMEM((HALO + t, c), F32), pltpu.VMEM((HALO3 + t, c), F32),
                        pltpu.VMEM((CONF_K * SUB, c), F32), pltpu.VMEM((SHORT_K * SUB, c), F32),
                        pltpu.VMEM((SUB - 1, HALO + t, c), F32)],
        compiler_params=_params("arbitrary"), name=name,
    )(u, u, x0, wa, ba, lg, lb, wb, w_out)


def _ffn_windows(ug_ref, ugh_ref, uv_ref, uvh_ref, gwin, vwin, first, t):
    lo = HALO3_BLK - HALO3
    gwin[0:HALO3, :] = jnp.where(first, 0.0, ugh_ref[...].astype(F32)[lo:HALO3_BLK])
    vwin[0:HALO3, :] = jnp.where(first, 0.0, uvh_ref[...].astype(F32)[lo:HALO3_BLK])

    def blk(rows):
        dst = pl.ds(pl.multiple_of(rows.start + HALO3, SUB), rows.size)
        gwin[dst, :] = ug_ref[rows, :].astype(F32)
        vwin[dst, :] = uv_ref[rows, :].astype(F32)
    _row_loop(t, 64, blk)


def _ffn_fwd(uf, x1, wf, w_down, *, name, dep=None):
    s_len, ff2 = uf.shape
    ff = ff2 // 2
    d = x1.shape[1]
    t = _seq_tile(s_len)
    fc = _ff_chunk(ff)
    nc = ff // fc
    per = t // HALO3_BLK

    def body(ug_ref, ugh_ref, uv_ref, uvh_ref, x1_ref, wfg_ref, wfv_ref, wd_ref, dep_ref,
             act_ref, x2_ref, cg_ref, cv_ref, gwin, vwin, cg, cv, wrep_g, wrep_v):
        first = pl.program_id(0) == 0
        _ffn_windows(ug_ref, ugh_ref, uv_ref, uvh_ref, gwin, vwin, first, t)
        _replicate_taps(wfg_ref, wrep_g, SHORT_K)
        _replicate_taps(wfv_ref, wrep_v, SHORT_K)
        _conv_taps(gwin, wrep_g, cg, taps=SHORT_K, n_rows=t, base=HALO3, width=fc)
        _conv_taps(vwin, wrep_v, cv, taps=SHORT_K, n_rows=t, base=HALO3, width=fc)

        def blk(rows):
            gv = cg[rows, :]
            vv = cv[rows, :]
            cg_ref[rows, :] = gv.astype(BF16)
            cv_ref[rows, :] = vv.astype(BF16)
            act_ref[rows, :] = ((gv * _sigmoid(gv)) * vv).astype(BF16)
        _row_loop(t, 32, blk, unroll=2)

        @pl.when(pl.program_id(1) == 0)
        def _():
            x2_ref[...] = x1_ref[...]
        x2_ref[...] += _dot(act_ref[...], wd_ref[...])

    halo_map = lambda off: (lambda i, j: (jnp.maximum(i * per - 1, 0), j + off))
    return pl.pallas_call(
        body, grid=(s_len // t, nc),
        in_specs=[pl.BlockSpec((t, fc), lambda i, j: (i, j)), pl.BlockSpec((HALO3_BLK, fc), halo_map(0)),
                  pl.BlockSpec((t, fc), lambda i, j: (i, j + nc)), pl.BlockSpec((HALO3_BLK, fc), halo_map(nc)),
                  pl.BlockSpec((t, d), lambda i, j: (i, 0)),
                  pl.BlockSpec((SHORT_K, fc), lambda i, j: (0, j)),
                  pl.BlockSpec((SHORT_K, fc), lambda i, j: (0, j + nc)),
                  pl.BlockSpec((fc, d), lambda i, j: (j, 0)), ANY],
        out_specs=[pl.BlockSpec((t, fc), lambda i, j: (i, j)), pl.BlockSpec((t, d), lambda i, j: (i, 0)),
                   pl.BlockSpec((t, fc), lambda i, j: (i, j)), pl.BlockSpec((t, fc), lambda i, j: (i, j))],
        out_shape=[jax.ShapeDtypeStruct((s_len, ff), BF16), jax.ShapeDtypeStruct((s_len, d), F32),
                   jax.ShapeDtypeStruct((s_len, ff), BF16), jax.ShapeDtypeStruct((s_len, ff), BF16)],
        scratch_shapes=[pltpu.VMEM((HALO3 + t, fc), F32), pltpu.VMEM((HALO3 + t, fc), F32),
                        pltpu.VMEM((t, fc), F32), pltpu.VMEM((t, fc), F32),
                        pltpu.VMEM((SHORT_K * SUB, fc), F32), pltpu.VMEM((SHORT_K * SUB, fc), F32)],
        compiler_params=_params("parallel", "arbitrary"), name=name,
    )(uf, uf, uf, uf, x1, wf, wf, w_down, uf if dep is None else dep)


def _loss_bwd(x, g, target, *, name):
    s_len, d = x.shape
    t = _seq_tile(s_len)

    def body(x_ref, g_ref, t_ref, l_ref, dx_ref, dxb_ref, dg_ref):
        @pl.when(pl.program_id(0) == 0)
        def _():
            l_ref[...] = jnp.zeros_like(l_ref)
            dg_ref[...] = jnp.zeros_like(dg_ref)

        def blk(rows):
            xv = x_ref[rows, :]
            r = lax.rsqrt(jnp.mean(xv * xv, axis=-1, keepdims=True) + EPS)
            xn = xv * r
            e = xn * g_ref[...] - t_ref[rows, :]
            l_ref[...] += _rows8(e * e)
            dy = e * (1.0 / d)
            dg_ref[...] += _rows8(dy * xn)
            dn = dy * g_ref[...]
            dx = r * (dn - xn * jnp.mean(dn * xn, axis=-1, keepdims=True))
            dx_ref[rows, :] = dx
            dxb_ref[rows, :] = dx.astype(BF16)
        _row_loop(t, 64, blk)

    row = pl.BlockSpec((t, d), lambda i: (i, 0))
    part = pl.BlockSpec((SUB, d), lambda i: (0, 0))
    return pl.pallas_call(
        body, grid=(s_len // t,),
        in_specs=[row, _resident((1, d), lambda i: (0, 0)), row],
        out_specs=[part, row, row, part],
        out_shape=[jax.ShapeDtypeStruct((SUB, d), F32), jax.ShapeDtypeStruct((s_len, d), F32),
                   jax.ShapeDtypeStruct((s_len, d), BF16), jax.ShapeDtypeStruct((SUB, d), F32)],
        compiler_params=_params("arbitrary"), name=name,
    )(x, g, target)


def _ffn_bwd(dx2, uf, cg, cv, wf, w_down, *, name, dep=None):
    s_len, ff2 = uf.shape
    ff = ff2 // 2
    d = dx2.shape[1]
    t = _seq_tile(s_len)
    n_t = s_len // t
    fc = _ff_chunk(ff)
    nc = ff // fc

    def body(dx_ref, ug_ref, uv_ref, cg_ref, cv_ref, wfg_ref, wfv_ref, wd_ref, dep_ref,
             dug_ref, duv_ref, dwg_ref, dwv_ref, dact, dgw, dvw, awg, awv, wrep_g, wrep_v):
        i = pl.program_id(1)
        dact[...] = _dot_nt(dx_ref[...], wd_ref[...])
        _replicate_taps(wfg_ref, wrep_g, SHORT_K)
        _replicate_taps(wfv_ref, wrep_v, SHORT_K)

        @pl.when(i == 0)
        def _():
            dgw[t:t + HALO3, :] = jnp.zeros((HALO3, fc), F32)
            dvw[t:t + HALO3, :] = jnp.zeros((HALO3, fc), F32)
            awg[...] = jnp.zeros_like(awg)
            awv[...] = jnp.zeros_like(awv)

        def blk(rows):
            gv = cg_ref[rows, :].astype(F32)
            sg = _sigmoid(gv)
            da = dact[rows, :]
            dgw[rows, :] = (da * cv_ref[rows, :].astype(F32)) * (sg * (1.0 + gv * (1.0 - sg)))
            dvw[rows, :] = da * (gv * sg)
        _row_loop(t, 32, blk, unroll=2)

        _conv_bwd_taps(dgw, wrep_g, ug_ref, dug_ref, awg, taps=SHORT_K, n_rows=t, width=fc)
        _conv_bwd_taps(dvw, wrep_v, uv_ref, duv_ref, awv, taps=SHORT_K, n_rows=t, width=fc)
        dgw[t:t + HALO3, :] = dgw[0:HALO3, :]
        dvw[t:t + HALO3, :] = dvw[0:HALO3, :]

        @pl.when(i == n_t - 1)
        def _():
            dwg_ref[...] = _fold8(awg, SHORT_K)
            dwv_ref[...] = _fold8(awv, SHORT_K)

    rev = lambda i: n_t - 1 - i
    gate = pl.BlockSpec((t, fc), lambda j, i: (rev(i), j))
    value = pl.BlockSpec((t, fc), lambda j, i: (rev(i), j + nc))
    return pl.pallas_call(
        body, grid=(nc, n_t),
        in_specs=[pl.BlockSpec((t, d), lambda j, i: (rev(i), 0)), gate, value, gate, gate,
                  pl.BlockSpec((SHORT_K, fc), lambda j, i: (0, j)),
                  pl.BlockSpec((SHORT_K, fc), lambda j, i: (0, j + nc)),
                  pl.BlockSpec((fc, d), lambda j, i: (j, 0)), ANY],
        out_specs=[gate, gate,
                   pl.BlockSpec((SHORT_K, fc), lambda j, i: (0, j)), pl.BlockSpec((SHORT_K, fc), lambda j, i: (0, j))],
        out_shape=[jax.ShapeDtypeStruct((s_len, ff), BF16), jax.ShapeDtypeStruct((s_len, ff), BF16),
                   jax.ShapeDtypeStruct((SHORT_K, ff), F32), jax.ShapeDtypeStruct((SHORT_K, ff), F32)],
        scratch_shapes=[pltpu.VMEM((t, fc), F32),
                        pltpu.VMEM((t + HALO3, fc), F32), pltpu.VMEM((t + HALO3, fc), F32),
                        pltpu.VMEM((SHORT_K * SUB, fc), F32), pltpu.VMEM((SHORT_K * SUB, fc), F32),
                        pltpu.VMEM((SHORT_K * SUB, fc), F32), pltpu.VMEM((SHORT_K * SUB, fc), F32)],
        compiler_params=_params("arbitrary", "arbitrary"), name=name,
    )(dx2, uf, uf, cg, cv, wf, wf, w_down, uf if dep is None else dep)


def _mix_bwd(dx1, u, ca, cb, wa, lg, lb, wb, w_out, *, name):
    s_len, d_in = u.shape
    d = dx1.shape[1]
    c = D_CONF
    t = _seq_tile(s_len)
    n_t = s_len // t

    def body(dx_ref, u_ref, ca_ref, cb_ref, wa_ref, lg_ref, lb_ref, wb_ref, wo_ref,
             du_ref, dwa_ref, dwb_ref, dba_ref, dlg_ref, dlb_ref, dbin_ref,
             glu, prod, dyc, dcaw, dcbw, dglu, dp, awa, awb, wrep_a, wrep_b, shf):
        i = pl.program_id(0)
        dyc[...] = _dot_nt(dx_ref[...], wo_ref[...])
        _replicate_taps(wa_ref, wrep_a, CONF_K)
        _replicate_taps(wb_ref, wrep_b, SHORT_K)

        @pl.when(i == 0)
        def _():
            dcaw[t:t + HALO, :] = jnp.zeros((HALO, c), F32)
            dcbw[t:t + HALO3, :] = jnp.zeros((HALO3, c), F32)
            awa[...] = jnp.zeros_like(awa)
            awb[...] = jnp.zeros_like(awb)
            dba_ref[...] = jnp.zeros_like(dba_ref)
            dlg_ref[...] = jnp.zeros_like(dlg_ref)
            dlb_ref[...] = jnp.zeros_like(dlb_ref)
            dbin_ref[...] = jnp.zeros_like(dbin_ref)

        def blk1(rows):
            cv = ca_ref[rows, :]
            mu = jnp.mean(cv, axis=-1, keepdims=True)
            xc = cv - mu
            rstd = lax.rsqrt(jnp.mean(xc * xc, axis=-1, keepdims=True) + EPS)
            nrm = xc * rstd
            ln = nrm * lg_ref[...] + lb_ref[...]
            sg = _sigmoid(ln)
            dln = dyc[rows, 0:c] * (sg * (1.0 + ln * (1.0 - sg)))
            dlg_ref[...] += _rows8(dln * nrm)
            dlb_ref[...] += _rows8(dln)
            dn = dln * lg_ref[...]
            dca = rstd * (dn - jnp.mean(dn, axis=-1, keepdims=True)
                          - nrm * jnp.mean(dn * nrm, axis=-1, keepdims=True))
            dcaw[rows, :] = dca
            dba_ref[...] += _rows8(dca)
            ds = dyc[rows, c:2 * c]
            dgb = ds * cb_ref[rows, :]
            dcbw[rows, :] = ds * u_ref[rows, 2 * c:3 * c].astype(F32)
            du_ref[rows, 2 * c:3 * c] = dgb.astype(BF16)
            dbin_ref[:, 2 * c:3 * c] += _rows8(dgb)
            glu[rows, :] = u_ref[rows, 0:c].astype(F32) * _sigmoid(u_ref[rows, c:2 * c].astype(F32))
            prod[rows, :] = u_ref[rows, 3 * c:4 * c].astype(F32) * u_ref[rows, 4 * c:5 * c].astype(F32)
        _row_loop(t, 64, blk1, unroll=2)

        _conv_bwd_taps(dcaw, wrep_a, glu, dglu, awa, taps=CONF_K, n_rows=t, width=c, shf=shf)
        _conv_bwd_taps(dcbw, wrep_b, prod, dp, awb, taps=SHORT_K, n_rows=t, width=c)
        dcaw[t:t + HALO, :] = dcaw[0:HALO, :]
        dcbw[t:t + HALO3, :] = dcbw[0:HALO3, :]

        def blk2(rows):
            av = u_ref[rows, 0:c].astype(F32)
            sg = _sigmoid(u_ref[rows, c:2 * c].astype(F32))
            dg = dglu[rows, :]
            d_av = dg * sg
            d_ag = (dg * av) * (sg * (1.0 - sg))
            dpv = dp[rows, :]
            d_gc = dpv * u_ref[rows, 4 * c:5 * c].astype(F32)
            d_vs = dpv * u_ref[rows, 3 * c:4 * c].astype(F32)
            du_ref[rows, 0:c] = d_av.astype(BF16)
            du_ref[rows, c:2 * c] = d_ag.astype(BF16)
            du_ref[rows, 3 * c:4 * c] = d_gc.astype(BF16)
            du_ref[rows, 4 * c:5 * c] = d_vs.astype(BF16)
            dbin_ref[:, 0:c] += _rows8(d_av)
            dbin_ref[:, c:2 * c] += _rows8(d_ag)
            dbin_ref[:, 3 * c:4 * c] += _rows8(d_gc)
            dbin_ref[:, 4 * c:5 * c] += _rows8(d_vs)
        _row_loop(t, 64, blk2)

        @pl.when(i == n_t - 1)
        def _():
            dwa_ref[...] = _fold8(awa, CONF_K)
            dwb_ref[...] = _fold8(awb, SHORT_K)

    rev = lambda i: n_t - 1 - i
    small_in = lambda r: _resident((r, c), lambda i: (0, 0))
    small = lambda r: pl.BlockSpec((r, c), lambda i: (0, 0))
    return pl.pallas_call(
        body, grid=(n_t,),
        in_specs=[pl.BlockSpec((t, d), lambda i: (rev(i), 0)),
                  pl.BlockSpec((t, d_in), lambda i: (rev(i), 0)),
                  pl.BlockSpec((t, c), lambda i: (rev(i), 0)), pl.BlockSpec((t, c), lambda i: (rev(i), 0)),
                  small_in(CONF_K), small_in(1), small_in(1), small_in(SHORT_K),
                  _resident((2 * c, d), lambda i: (0, 0))],
        out_specs=[pl.BlockSpec((t, d_in), lambda i: (rev(i), 0)),
                   small(CONF_K), small(SHORT_K), small(SUB), small(SUB), small(SUB),
                   pl.BlockSpec((SUB, d_in), lambda i: (0, 0))],
        out_shape=[jax.ShapeDtypeStruct((s_len, d_in), BF16),
                   jax.ShapeDtypeStruct((CONF_K, c), F32), jax.ShapeDtypeStruct((SHORT_K, c), F32),
                   jax.ShapeDtypeStruct((SUB, c), F32), jax.ShapeDtypeStruct((SUB, c), F32),
                   jax.ShapeDtypeStruct((SUB, c), F32), jax.ShapeDtypeStruct((SUB, d_in), F32)],
        scratch_shapes=[pltpu.VMEM((t, c), F32), pltpu.VMEM((t, c), F32), pltpu.VMEM((t, 2 * c), F32),
                        pltpu.VMEM((t + HALO, c), F32), pltpu.VMEM((t + HALO3, c), F32),
                        pltpu.VMEM((t, c), F32), pltpu.VMEM((t, c), F32),
                        pltpu.VMEM((CONF_K * SUB, c), F32), pltpu.VMEM((SHORT_K * SUB, c), F32),
                        pltpu.VMEM((CONF_K * SUB, c), F32), pltpu.VMEM((SHORT_K * SUB, c), F32),
                        pltpu.VMEM((SUB - 1, t + HALO, c), F32)],
        compiler_params=_params("arbitrary"), name=name,
    )(dx1, u, ca, cb, wa, lg, lb, wb, w_out)


def _matmul_tn(a, b, *, name, into=None, part=0, n_parts=1):
    s_len, k = a.shape
    n = b.shape[1]
    tk = _col_tile(k)
    per = k // tk

    def body(*refs):
        a_ref, b_ref = refs[0], refs[1]
        o_ref = refs[-1]
        o_ref[...] = _dot_tn(a_ref[...], b_ref[...]).astype(BF16)

    in_specs = [pl.BlockSpec((s_len, tk), lambda j: (0, j)), _resident((s_len, n), lambda j: (0, 0))]
    args = [a, b]
    aliases = {}
    if into is not None:
        in_specs.append(ANY)
        args.append(into)
        aliases = {2: 0}
    return pl.pallas_call(
        body, grid=(per,), in_specs=in_specs,
        out_specs=pl.BlockSpec((tk, n), lambda j: (part * per + j, 0)),
        out_shape=jax.ShapeDtypeStruct((n_parts * k, n), BF16),
        input_output_aliases=aliases,
        compiler_params=_params("parallel"), name=name,
    )(*args)


def _matmul_rmsbwd(dzs, wt, x, g, dx_in, *, name, dep=None):
    s_len, d = x.shape
    n_z = len(dzs)
    nj = dzs[0].shape[1]
    t = _mm_tile(s_len)

    def body(*refs):
        dz_refs = refs[0:n_z]
        w_refs = refs[n_z:2 * n_z]
        x_ref, g_ref, dxi_ref, _, dx_ref, dxb_ref, dg_ref, dh = refs[2 * n_z:]

        @pl.when(pl.program_id(0) == 0)
        def _():
            dg_ref[...] = jnp.zeros_like(dg_ref)

        acc = _dot(dz_refs[0][...], w_refs[0][...])
        for q in range(1, n_z):
            acc = acc + _dot(dz_refs[q][...], w_refs[q][...])
        dh[...] = acc

        def blk(rows):
            xv = x_ref[rows, :]
            r = lax.rsqrt(jnp.mean(xv * xv, axis=-1, keepdims=True) + EPS)
            xn = xv * r
            dhv = dh[rows, :]
            dg_ref[...] += _rows8(dhv * xn)
            dn = dhv * g_ref[...]
            dx = dxi_ref[rows, :] + r * (dn - xn * jnp.mean(dn * xn, axis=-1, keepdims=True))
            dx_ref[rows, :] = dx
            dxb_ref[rows, :] = dx.astype(BF16)
        _row_loop(t, 128, blk)

    row = pl.BlockSpec((t, d), lambda i: (i, 0))
    in_specs = [pl.BlockSpec((t, nj), lambda i: (i, 0)) for _ in range(n_z)]
    in_specs += [_resident((nj, d), functools.partial(lambda q, i: (q, 0), q)) for q in range(n_z)]
    in_specs += [row, _resident((1, d), lambda i: (0, 0)), row, ANY]
    return pl.pallas_call(
        body, grid=(s_len // t,), in_specs=in_specs,
        out_specs=[row, row, pl.BlockSpec((SUB, d), lambda i: (0, 0))],
        out_shape=[jax.ShapeDtypeStruct((s_len, d), F32), jax.ShapeDtypeStruct((s_len, d), BF16),
                   jax.ShapeDtypeStruct((SUB, d), F32)],
        scratch_shapes=[pltpu.VMEM((t, d), F32)],
        compiler_params=_params("arbitrary"), name=name,
    )(*dzs, *([wt] * n_z), x, g, dx_in, x if dep is None else dep)


def _row(v):
    return v.reshape(1, -1)


def _layer_fwd(x0, p, tag, dep=None, before_up=None):
    u, h1 = _rms_matmul(x0, _row(p["mix_norm_g"]), p["w_in_t"], _row(p["b_in"]), name=f"in_proj_{tag}", dep=dep)
    ycat, x1, ca, cb = _mix_fwd(u, x0, p["conv_a_w"], _row(p["conv_a_b"]), _row(p["ln_a_g"]), _row(p["ln_a_b"]),
                            p["conv_b_w"], p["w_out"], name=f"mix_fwd_{tag}")
    if before_up is not None:
        before_up(x1)
    uf, h2 = _rms_matmul(x1, _row(p["ffn_norm_g"]), p["w_up_t"], None, name=f"up_proj_{tag}")
    act, x2, cg, cv = _ffn_fwd(uf, x1, p["conv_f_w"], p["w_down"], name=f"ffn_fwd_{tag}")
    return x2, dict(x0=x0, h1=h1, u=u, ca=ca, cb=cb, ycat=ycat, x1=x1, h2=h2, uf=uf, cg=cg, cv=cv, act=act)


def _layer_bwd(dx2, dx2_b, p, saved, tag, after_ffn, after_mix, dep=None):
    dug, duv, dwf_g, dwf_v = _ffn_bwd(dx2_b, saved["uf"], saved["cg"], saved["cv"], p["conv_f_w"], p["w_down"],
                                      name=f"ffn_bwd_{tag}", dep=dep)
    g_down = _matmul_tn(saved["act"], dx2_b, name=f"dw_down_{tag}")
    g_up = _matmul_tn(dug, saved["h2"], name=f"dw_up_g_{tag}", n_parts=2)
    g_up = _matmul_tn(duv, saved["h2"], name=f"dw_up_v_{tag}", into=g_up, part=1, n_parts=2)
    dep_ffn = after_ffn(dict(w_up=g_up, w_down=g_down))
    dx1, dx1_b, dg2 = _matmul_rmsbwd([dug, duv], p["w_up_t"], saved["x1"], _row(p["ffn_norm_g"]), dx2,
                                     name=f"dh_ffn_{tag}", dep=dep_ffn)
    du, dwa, dwb, dba, dlg, dlb, dbin = _mix_bwd(
        dx1_b, saved["u"], saved["ca"], saved["cb"], p["conv_a_w"], _row(p["ln_a_g"]), _row(p["ln_a_b"]),
        p["conv_b_w"], p["w_out"], name=f"mix_bwd_{tag}")
    g_out = _matmul_tn(saved["ycat"], dx1_b, name=f"dw_out_{tag}")
    g_in = _matmul_tn(du, saved["h1"], name=f"dw_in_{tag}")
    conv = dict(conv_a_w=dwa, conv_b_w=dwb, conv_f_w=jnp.concatenate([dwf_g, dwf_v], axis=1))
    dep_mix = after_mix(dict(w_in=g_in, w_out=g_out), conv)
    dx0, dx0_b, dg1 = _matmul_rmsbwd([du], p["w_in_t"], saved["x0"], _row(p["mix_norm_g"]), dx1,
                                     name=f"dh_mix_{tag}", dep=dep_mix)
    rep = dict(mix_norm_g=dg1, b_in=dbin, conv_a_b=dba, ln_a_g=dlg, ln_a_b=dlb, ffn_norm_g=dg2)
    return dx0, dx0_b, rep


def _place():
    return lax.axis_index("x"), lax.axis_index("y"), lax.axis_index("c")


def _all_gather(arrs, *, name):
    n_a = len(arrs)

    def body(*refs):
        ins = refs[0:n_a]
        outs = refs[n_a:2 * n_a]
        send_sems, recv_sems, local_sems = refs[2 * n_a:]
        x, y, c = _place()
        sibling = (x, y, 1 - c)
        chips = [(1 - x, y), (x, 1 - y), (1 - x, 1 - y)]

        def slot(a, px, py, pc):
            return outs[a].at[4 * px + 2 * py + pc]

        def copy(a, k, block, to, src=None):
            return pltpu.make_async_remote_copy(
                src_ref=slot(a, *block) if src is None else src, dst_ref=slot(a, *block),
                send_sem=send_sems.at[a, k], recv_sem=recv_sems.at[a, k],
                device_id=to, device_id_type=MESH)

        me = (x, y, c)
        mine = [pltpu.make_async_copy(ins[a], slot(a, *me), local_sems.at[a]) for a in range(n_a)]
        for cp in mine:
            cp.start()
        started = []
        for a in range(n_a):
            first = [copy(a, 0, me, sibling, src=ins[a])]
            first += [copy(a, 1 + j, me, (*chip, c), src=ins[a]) for j, chip in enumerate(chips)]
            for cp in first:
                cp.start()
            started += first
        for a in range(n_a):
            for j, chip in enumerate(chips):
                copy(a, 1 + j, (*chip, c), me).wait_recv()
                passed = copy(a, 4 + j, (*chip, c), sibling)
                passed.start()
                started.append(passed)
        for a in range(n_a):
            copy(a, 0, sibling, me).wait_recv()
            for j, chip in enumerate(chips):
                copy(a, 4 + j, (*chip, 1 - c), me).wait_recv()
        for cp in started:
            cp.wait_send()
        for cp in mine:
            cp.wait()

    return pl.pallas_call(
        body, in_specs=[ANY] * n_a, out_specs=[ANY] * n_a,
        out_shape=[jax.ShapeDtypeStruct((N_DEV, *a.shape), a.dtype) for a in arrs],
        scratch_shapes=[pltpu.SemaphoreType.DMA((n_a, 7)), pltpu.SemaphoreType.DMA((n_a, 7)),
                        pltpu.SemaphoreType.DMA((n_a,))],
        name=name,
    )(*arrs)


def _sibling_exchange(arrs, *, name):
    n_a = len(arrs)

    def body(*refs):
        ins = refs[0:n_a]
        outs = refs[n_a:2 * n_a]
        send_sems, recv_sems = refs[2 * n_a:]
        x, y, c = _place()
        copies = [pltpu.make_async_remote_copy(
            src_ref=ins[a].at[:, 1 - c], dst_ref=outs[a], send_sem=send_sems.at[a], recv_sem=recv_sems.at[a],
            device_id=(x, y, 1 - c), device_id_type=MESH) for a in range(n_a)]
        for cp in copies:
            cp.start()
        for cp in copies:
            cp.wait()

    return pl.pallas_call(
        body, in_specs=[ANY] * n_a, out_specs=[ANY] * n_a,
        out_shape=[jax.ShapeDtypeStruct((N_CHIP, *a.shape[2:]), a.dtype) for a in arrs],
        scratch_shapes=[pltpu.SemaphoreType.DMA((n_a,)), pltpu.SemaphoreType.DMA((n_a,))],
        name=name,
    )(*arrs)


def _row_tile(r, cap):
    for tr in range(min(cap, r) // 16 * 16, 0, -16):
        if r % tr == 0:
            return tr
    return r


def _pair_sum(mine, theirs, core, *, name):
    n_chip, _, r, c = mine.shape
    tr = _row_tile(r, 1024)

    def body(core_ref, a_ref, b_ref, o_ref):
        o_ref[...] = (a_ref[...].astype(F32) + b_ref[...].astype(F32)).astype(o_ref.dtype)

    return pl.pallas_call(
        body,
        grid_spec=pltpu.PrefetchScalarGridSpec(
            num_scalar_prefetch=1, grid=(n_chip, r // tr),
            in_specs=[pl.BlockSpec((None, None, tr, c), lambda q, i, core_ref: (q, core_ref[0], i, 0)),
                      pl.BlockSpec((None, tr, c), lambda q, i, core_ref: (q, i, 0))],
            out_specs=pl.BlockSpec((None, tr, c), lambda q, i, core_ref: (q, i, 0))),
        out_shape=jax.ShapeDtypeStruct((n_chip, r, c), mine.dtype),
        compiler_params=_params("parallel", "parallel"), name=name,
    )(core, mine, theirs)


def _chip_exchange(arrs, *, name):
    n_a = len(arrs)

    def body(*refs):
        ins = refs[0:n_a]
        outs = refs[n_a:2 * n_a]
        send_sems, recv_sems, local_sems = refs[2 * n_a:]
        x, y, c = _place()
        my_chip = 2 * x + y
        chips = [(1 - x, y), (x, 1 - y), (1 - x, 1 - y)]
        mine = [pltpu.make_async_copy(ins[a].at[my_chip], outs[a].at[my_chip], local_sems.at[a]) for a in range(n_a)]
        for cp in mine:
            cp.start()
        copies = []
        for a in range(n_a):
            for j, (px, py) in enumerate(chips):
                copies.append(pltpu.make_async_remote_copy(
                    src_ref=ins[a].at[2 * px + py], dst_ref=outs[a].at[my_chip],
                    send_sem=send_sems.at[a, j], recv_sem=recv_sems.at[a, j],
                    device_id=(px, py, c), device_id_type=MESH))
        for cp in copies:
            cp.start()
        for cp in copies:
            cp.wait()
        for cp in mine:
            cp.wait()

    return pl.pallas_call(
        body, in_specs=[ANY] * n_a, out_specs=[ANY] * n_a,
        out_shape=[jax.ShapeDtypeStruct(a.shape, a.dtype) for a in arrs],
        scratch_shapes=[pltpu.SemaphoreType.DMA((n_a, 3)), pltpu.SemaphoreType.DMA((n_a, 3)),
                        pltpu.SemaphoreType.DMA((n_a,))],
        name=name,
    )(*arrs)


HBM = pl.BlockSpec(memory_space=pltpu.HBM)
SEM = pl.BlockSpec(memory_space=pltpu.SEMAPHORE)
EFFECT = pltpu.SideEffectType.DATAFLOW_SIDE_EFFECTING


def _in_hbm(a):
    return pltpu.with_memory_space_constraint(a, pltpu.HBM)


def _split_start(srcs, lands, plan, n_copies, after, *, name):
    n_s, n_l = len(srcs), len(lands)

    def body(*refs):
        src_refs = refs[0:n_s]
        land_refs = refs[n_s:n_s + n_l]
        send_sems, recv_sems = refs[n_s + n_l + 1], refs[n_s + n_l + 2]
        token = refs[-1]
        for cp in plan(src_refs, land_refs, send_sems, recv_sems):
            cp.start()
        token[...] = jnp.zeros_like(token)

    thru = [pltpu.HBM(a.shape, a.dtype) for a in list(srcs) + list(lands)]
    res = pl.pallas_call(
        body, name=name,
        out_shape=(pltpu.SemaphoreType.DMA((n_copies,)), pltpu.SemaphoreType.DMA((n_copies,)), *thru,
                   jax.ShapeDtypeStruct((SUB, LANES), F32)),
        in_specs=[HBM] * (n_s + n_l) + [ANY],
        out_specs=(SEM, SEM, *([HBM] * (n_s + n_l)), pl.BlockSpec(memory_space=pltpu.VMEM)),
        input_output_aliases={i: 2 + i for i in range(n_s + n_l)},
        compiler_params=pltpu.CompilerParams(has_side_effects=EFFECT),
    )(*[_in_hbm(a) for a in srcs], *[_in_hbm(a) for a in lands], _in_hbm(after))
    return res[0], res[1], list(res[2:2 + n_s]), list(res[2 + n_s:2 + n_s + n_l]), res[-1]


def _split_wait(send_sems, recv_sems, srcs, lands, after, plan, *, name):
    n_s, n_l = len(srcs), len(lands)

    def body(*refs):
        src_refs = refs[0:n_s]
        land_refs = refs[n_s:n_s + n_l]
        send, recv = refs[n_s + n_l], refs[n_s + n_l + 1]
        for cp in plan(src_refs, land_refs, send, recv):
            cp.wait_send()
            cp.wait_recv()

    res = pl.pallas_call(
        body, name=name,
        out_shape=tuple(pltpu.HBM(a.shape, a.dtype) for a in list(srcs) + list(lands)),
        in_specs=[HBM] * (n_s + n_l) + [SEM, SEM, ANY],
        out_specs=tuple([HBM] * (n_s + n_l)),
        input_output_aliases={i: i for i in range(n_s + n_l)},
        compiler_params=pltpu.CompilerParams(has_side_effects=EFFECT),
    )(*srcs, *lands, send_sems, recv_sems, _in_hbm(after))
    return list(res[n_s:])


def _remote(src, dst, send_sems, recv_sems, k, to):
    return pltpu.make_async_remote_copy(src_ref=src, dst_ref=dst, send_sem=send_sems.at[k], recv_sem=recv_sems.at[k],
                                        device_id=to, device_id_type=MESH)


def _gather_plan_first(src_refs, land_refs, send_sems, recv_sems):
    x, y, c = _place()
    me = 4 * x + 2 * y + c
    peers = [(x, y, 1 - c), (1 - x, y, c), (x, 1 - y, c), (1 - x, 1 - y, c)]
    return [_remote(src, land.at[me], send_sems, recv_sems, 4 * a + k, to)
            for a, (src, land) in enumerate(zip(src_refs, land_refs)) for k, to in enumerate(peers)]


def _gather_plan_second(src_refs, land_refs, send_sems, recv_sems):
    x, y, c = _place()
    chips = [(1 - x, y), (x, 1 - y), (1 - x, 1 - y)]
    out = []
    for a, land in enumerate(land_refs):
        for j, (px, py) in enumerate(chips):
            slot = land.at[4 * px + 2 * py + c]
            out.append(_remote(slot, slot, send_sems, recv_sems, 3 * a + j, (x, y, 1 - c)))
    return out


def _chips_plan(src_refs, land_refs, send_sems, recv_sems):
    x, y, c = _place()
    my_chip = 2 * x + y
    chips = [(1 - x, y), (x, 1 - y), (1 - x, 1 - y)]
    return [_remote(src.at[2 * px + py], land.at[my_chip], send_sems, recv_sems, 3 * a + j, (px, py, c))
            for a, (src, land) in enumerate(zip(src_refs, land_refs)) for j, (px, py) in enumerate(chips)]


def _landing(like_shape, dtype, own, index):
    return lax.dynamic_update_index_in_dim(lax.empty(like_shape, dtype), own, index, 0)


def _adamw_math(g, w, m, v):
    m = ADAM_B1 * m + (1.0 - ADAM_B1) * g
    v = ADAM_B2 * v + (1.0 - ADAM_B2) * (g * g)
    m_hat = m / (1.0 - ADAM_B1 ** ADAM_STEP)
    v_hat = v / (1.0 - ADAM_B2 ** ADAM_STEP)
    delta = -ADAM_LR * (m_hat / (jnp.sqrt(v_hat) + ADAM_EPS) + ADAM_WD * w)
    return delta, m, v


def _adamw_sharded(parts, w, m, v, *, name, dep=None):
    n_layers, r, c = w.shape
    n_chip = parts[0].shape[0]
    tr = _row_tile(r, 384)
    n_i = r // tr

    def body(*refs):
        p_refs = refs[0:n_layers]
        w_ref, m_ref, v_ref, _, g_out, d_out, m_out, v_out = refs[n_layers:]
        layer = pl.program_id(0)
        for l in range(n_layers):
            @pl.when(layer == l)
            def _(l=l):
                g = p_refs[l][0].astype(F32)
                for q in range(1, n_chip):
                    g = g + p_refs[l][q].astype(F32)
                delta, m_new, v_new = _adamw_math(g, w_ref[...], m_ref[...], v_ref[...])
                g_out[...] = g
                d_out[...] = delta
                m_out[...] = m_new
                v_out[...] = v_new

    def part_map(l):
        return lambda layer, i: (0, jnp.where(layer == l, i, jnp.where(layer < l, 0, n_i - 1)), 0)

    blk = pl.BlockSpec((None, tr, c), lambda layer, i: (layer, i, 0))
    return pl.pallas_call(
        body, grid=(n_layers, n_i),
        in_specs=[pl.BlockSpec((n_chip, tr, c), part_map(l)) for l in range(n_layers)] + [blk, blk, blk, ANY],
        out_specs=[blk] * 4, out_shape=[jax.ShapeDtypeStruct((n_layers, r, c), F32)] * 4,
        compiler_params=_params("arbitrary", "arbitrary"), name=name,
    )(*parts, w, m, v, w if dep is None else dep)


def _fold_partials(cols, *, name):
    widths = [c.shape[1] for c in cols]

    def body(*refs):
        o_ref = refs[-1]
        pos = 0
        for ref, width in zip(refs[:-1], widths):
            o_ref[:, pos:pos + width] = jnp.sum(ref[...], axis=0, keepdims=True)
            pos += width

    return pl.pallas_call(body, out_shape=jax.ShapeDtypeStruct((1, sum(widths)), F32), name=name)(*cols)


def _adamw_replicated(parts, names, w, m, v, n_loss, *, name):
    n_dev = parts.shape[0]
    n_layers = w[names[0]].shape[0]
    every = list(names) + ["final_norm_g"]
    n_p = len(every)

    def body(*refs):
        p_ref = refs[0]
        w_refs = dict(zip(every, refs[1:1 + n_p]))
        m_refs = dict(zip(every, refs[1 + n_p:1 + 2 * n_p]))
        v_refs = dict(zip(every, refs[1 + 2 * n_p:1 + 3 * n_p]))
        l_out = refs[1 + 3 * n_p]
        outs = refs[2 + 3 * n_p:]
        o_refs = {n: outs[4 * q:4 * q + 4] for q, n in enumerate(every)}
        acc = p_ref[0]
        for q in range(1, n_dev):
            acc = acc + p_ref[q]
        tot = jnp.sum(acc, axis=0, keepdims=True)
        pos = 0
        where = [(n, l) for l in range(n_layers) for n in names] + [("final_norm_g", 0)]
        for n, l in where:
            width = w_refs[n].shape[1]
            g = tot[:, pos:pos + width]
            pos += width
            row = pl.ds(l, 1)
            delta, m_new, v_new = _adamw_math(g, w_refs[n][row, :], m_refs[n][row, :], v_refs[n][row, :])
            for o, val in zip(o_refs[n], (g, delta, m_new, v_new)):
                o[row, :] = val
        l_out[...] = (0.5 / n_loss) * jnp.sum(tot[:, pos:pos + n_loss], axis=-1, keepdims=True)

    shapes = [jax.ShapeDtypeStruct((1, 1), F32)]
    for n in every:
        shapes += [jax.ShapeDtypeStruct(w[n].shape, F32)] * 4
    res = pl.pallas_call(
        body, out_shape=shapes,
        compiler_params=pltpu.CompilerParams(vmem_limit_bytes=VMEM_LIMIT), name=name,
    )(parts, *[w[n] for n in every], *[m[n] for n in every], *[v[n] for n in every])
    return res[0], {n: res[1 + 4 * q:5 + 4 * q] for q, n in enumerate(every)}


BIG = ("w_in", "w_out", "w_up", "w_down")
COL_SHARDED = ("w_in", "w_up")
CONV = ("conv_a_w", "conv_b_w", "conv_f_w")
REPLICATED = ("mix_norm_g", "b_in", "conv_a_b", "ln_a_g", "ln_a_b", "ffn_norm_g")
KINDS = ("grad", "delta", "m", "v")
FFN_PART = ("w_up", "w_down")
MIX_PART = ("w_in", "w_out")


def _weights_from_gathered(g):
    n_dev, r, c = g.shape
    return g.reshape(n_dev * r, c)


def _slabs_from_full(grad):
    return grad.reshape(N_DEV, grad.shape[0] // N_DEV, grad.shape[1])


def _pair_sums(slabs, core, tag):
    slabs = [s.reshape(N_CHIP, 2, *s.shape[1:]) for s in slabs]
    theirs = _sibling_exchange(slabs, name=f"reduce_siblings_{tag}")
    return [_pair_sum(a, b, core, name=f"pair_sum_{tag}_{q}") for q, (a, b) in enumerate(zip(slabs, theirs))]


def kernel(x, mix_norm_g, w_in, b_in, conv_a_w, conv_a_b, ln_a_g, ln_a_b, conv_b_w, w_out, ffn_norm_g, w_up, conv_f_w, w_down, final_norm_g, loss_target, m_mix_norm_g, m_w_in, m_b_in, m_conv_a_w, m_conv_a_b, m_ln_a_g, m_ln_a_b, m_conv_b_w, m_w_out, m_ffn_norm_g, m_w_up, m_conv_f_w, m_w_down, m_final_norm_g, v_mix_norm_g, v_w_in, v_b_in, v_conv_a_w, v_conv_a_b, v_ln_a_g, v_ln_a_b, v_conv_b_w, v_w_out, v_ffn_norm_g, v_w_up, v_conv_f_w, v_w_down, v_final_norm_g):
    w = dict(mix_norm_g=mix_norm_g, w_in=w_in, b_in=b_in, conv_a_w=conv_a_w, conv_a_b=conv_a_b, ln_a_g=ln_a_g,
             ln_a_b=ln_a_b, conv_b_w=conv_b_w, w_out=w_out, ffn_norm_g=ffn_norm_g, w_up=w_up, conv_f_w=conv_f_w,
             w_down=w_down, final_norm_g=final_norm_g)
    m = dict(mix_norm_g=m_mix_norm_g, w_in=m_w_in, b_in=m_b_in, conv_a_w=m_conv_a_w, conv_a_b=m_conv_a_b,
             ln_a_g=m_ln_a_g, ln_a_b=m_ln_a_b, conv_b_w=m_conv_b_w, w_out=m_w_out, ffn_norm_g=m_ffn_norm_g,
             w_up=m_w_up, conv_f_w=m_conv_f_w, w_down=m_w_down, final_norm_g=m_final_norm_g)
    v = dict(mix_norm_g=v_mix_norm_g, w_in=v_w_in, b_in=v_b_in, conv_a_w=v_conv_a_w, conv_a_b=v_conv_a_b,
             ln_a_g=v_ln_a_g, ln_a_b=v_ln_a_b, conv_b_w=v_conv_b_w, w_out=v_w_out, ffn_norm_g=v_ffn_norm_g,
             w_up=v_w_up, conv_f_w=v_conv_f_w, w_down=v_w_down, final_norm_g=v_final_norm_g)
    order = list(w)
    n_layers = w_in.shape[0]
    n_big = len(BIG)
    xs = x[0]
    target = loss_target[0]
    flip = lambda a: jnp.transpose(a, (0, 2, 1))
    wt, mt, vt = ({n: flip(d[n]) if n in COL_SHARDED else d[n] for n in BIG} for d in (w, m, v))
    px, py, pc = _place()
    core = pc.astype(jnp.int32).reshape(1)
    me = 4 * px + 2 * py + pc
    my_chip = 2 * px + py

    assert BIG == MIX_PART + FFN_PART
    key = lambda n: n + "_t" if n in COL_SHARDED else n
    shard = lambda n, l: wt[n][l].astype(BF16)

    def gather_start(names, l, after, tag):
        shards = [shard(n, l) for n in names]
        lands = [_landing((N_DEV, *s.shape), s.dtype, s, me) for s in shards]
        return _split_start(shards, lands, _gather_plan_first, 4 * len(shards), after, name=f"gather_first_start_{tag}")

    def gather_mid(first, after, tag):
        return _split_wait(first[0], first[1], first[2], first[3], after, _gather_plan_first,
                           name=f"gather_first_wait_{tag}")

    def forward_start(lands, after, tag):
        return _split_start([], lands, _gather_plan_second, 3 * len(lands), after, name=f"gather_second_start_{tag}")

    def forward_finish(second, after, tag):
        return _split_wait(second[0], second[1], [], second[3], after, _gather_plan_second,
                           name=f"gather_second_wait_{tag}")

    gathered = _all_gather([shard(n, 0) for n in MIX_PART] + [w[n] for n in CONV], name="gather_weights_0")
    params = [{n: w[n][l] for n in REPLICATED} for l in range(n_layers)]
    for n, g in zip(CONV, gathered[len(MIX_PART):]):
        n_dev, _, taps, c = g.shape
        full = g.transpose(1, 2, 0, 3).reshape(n_layers, taps, n_dev * c)
        for l in range(n_layers):
            params[l][n] = full[l]
    for n, g in zip(MIX_PART, gathered):
        params[0][key(n)] = _weights_from_gathered(g)
    ffn_first = gather_start(FFN_PART, 0, gathered[0], "0_ffn")
    pending = {}

    h = xs
    saved = []
    for l in range(n_layers):
        nxt = l + 1 if l + 1 < n_layers else None

        def before_up(x1, l=l, nxt=nxt):
            if l == 0:
                second = forward_start(gather_mid(ffn_first, x1, "0_ffn"), x1, "0_ffn")
                after = second[4]
            else:
                second = pending[l]["ffn"]
                after = x1
            if nxt is not None:
                pending[nxt] = dict(first=gather_start(BIG, nxt, after, str(nxt)))
                after = pending[nxt]["first"][4]
            for n, g in zip(FFN_PART, forward_finish(second, after, f"{l}_ffn")):
                params[l][key(n)] = _weights_from_gathered(g)

        h, keep = _layer_fwd(h, params[l], str(l), dep=ffn_first[4] if l == 0 else None, before_up=before_up)
        saved.append(keep)
        if nxt is not None:
            arrived = gather_mid(pending[nxt]["first"], h, str(nxt))
            mix_second = forward_start(arrived[:len(MIX_PART)], h, f"{nxt}_mix")
            pending[nxt]["ffn"] = forward_start(arrived[len(MIX_PART):], mix_second[4], f"{nxt}_ffn")
            for n, g in zip(MIX_PART, forward_finish(mix_second, pending[nxt]["ffn"][4], f"{nxt}_mix")):
                params[nxt][key(n)] = _weights_from_gathered(g)

    def start_reduce(slabs, tag):
        pairs = _pair_sums(slabs, core, tag)
        lands = [_landing(p.shape, p.dtype, lax.dynamic_index_in_dim(p, my_chip, 0, keepdims=False), my_chip)
                 for p in pairs]
        return _split_start(pairs, lands, _chips_plan, 3 * len(pairs), pairs[0], name=f"reduce_chips_start_{tag}")

    def finish_reduce(fly, after, tag):
        return _split_wait(fly[0], fly[1], fly[2], fly[3], after, _chips_plan, name=f"reduce_chips_wait_{tag}")

    loss_sq, dh, dh_b, dgf = _loss_bwd(h, _row(final_norm_g), target, name="loss")
    conv_g = {n: [None] * n_layers for n in CONV}
    rep_g = [None] * n_layers
    flights = {}
    token = None
    for l in reversed(range(n_layers)):
        def after_ffn(g, l=l):
            flights[l, "ffn"] = start_reduce([_slabs_from_full(g[n]) for n in FFN_PART], f"{l}_ffn")
            return flights[l, "ffn"][4]

        def after_mix(g, conv, l=l):
            for n in CONV:
                conv_g[n][l] = conv[n]
            slabs = [_slabs_from_full(g[n]) for n in MIX_PART]
            if l == 0:
                for n in CONV:
                    full = jnp.stack(conv_g[n])
                    _, taps, c = full.shape
                    slabs.append(full.reshape(n_layers, taps, N_DEV, c // N_DEV).transpose(2, 0, 1, 3)
                                 .reshape(N_DEV, n_layers * taps, c // N_DEV))
            flights[l, "mix"] = start_reduce(slabs, f"{l}_mix")
            return flights[l, "mix"][4]

        dh, dh_b, rep_g[l] = _layer_bwd(dh, dh_b, params[l], saved[l], str(l), after_ffn, after_mix, dep=token)
        token = flights[l, "mix"][4]

    sums = {key: finish_reduce(fly, dh, f"{key[0]}_{key[1]}") for key, fly in flights.items() if key != (0, "mix")}
    out = {k: {} for k in KINDS}

    def adamw_big(names, part, dep):
        for q, n in enumerate(names):
            layer_parts = [sums[l, part][q] for l in range(n_layers)]
            res = _adamw_sharded(layer_parts, wt[n], mt[n], vt[n], name=f"adamw_{n}", dep=dep)
            for k, r in zip(KINDS, res):
                out[k][n] = flip(r) if n in COL_SHARDED else r

    adamw_big(FFN_PART, "ffn", token)

    rep_cols = [rep_g[l][n] for l in range(n_layers) for n in REPLICATED] + [dgf, loss_sq]
    rep_all = _all_gather([_fold_partials(rep_cols, name="fold_small")], name="gather_small")[0]
    with_final = lambda d: {**{n: d[n] for n in REPLICATED}, "final_norm_g": _row(d["final_norm_g"])}
    loss, rep_res = _adamw_replicated(rep_all, REPLICATED, with_final(w), with_final(m), with_final(v),
                                      loss_sq.shape[1], name="adamw_small")
    for n, res in rep_res.items():
        for k, r in zip(KINDS, res):
            out[k][n] = r.reshape(w[n].shape)

    last = finish_reduce(flights[0, "mix"], rep_res["b_in"][0], "0_mix")
    sums[0, "mix"] = last[:len(MIX_PART)]
    adamw_big(MIX_PART, "mix", None)
    for n, p in zip(CONV, last[len(MIX_PART):]):
        as_one = lambda a: a.reshape(1, *p.shape[1:])
        for k, r in zip(KINDS, _adamw_sharded([p], as_one(w[n]), as_one(m[n]), as_one(v[n]), name=f"adamw_{n}")):
            out[k][n] = r.reshape(w[n].shape)

    grad_x = dh.reshape(x.shape)
    return (loss.reshape(()), grad_x, *[out["grad"][n] for n in order], *[out["delta"][n] for n in order],
            *[out["m"][n] for n in order], *[out["v"][n] for n in order])
```

```python
import functools

import jax
import jax.numpy as jnp
from jax import lax
from jax.experimental import pallas as pl
from jax.experimental.pallas import tpu as pltpu

F32 = jnp.float32
BF16 = jnp.bfloat16

N_DEV = 8
N_CHIP = 4
D_CONF = 512
CONF_K = 31
SHORT_K = 3
EPS = 1e-6
HALO = 32
HALO3 = 8
HALO3_BLK = 16
LANES = 128
SUB = 8
VMEM_LIMIT = 56 * 1024 * 1024

ADAM_LR = 0.001
ADAM_B1 = 0.9
ADAM_B2 = 0.999
ADAM_EPS = 1e-08
ADAM_WD = 0.01
ADAM_STEP = 10

MESH = pl.DeviceIdType.MESH
ANY = pl.BlockSpec(memory_space=pl.ANY)


def _params(*sem):
    return pltpu.CompilerParams(dimension_semantics=sem, vmem_limit_bytes=VMEM_LIMIT)


def _resident(shape, index_map):
    return pl.BlockSpec(shape, index_map, pipeline_mode=pl.Buffered(1))


def _row_loop(n_rows, rb, fn, unroll=1):
    rb = min(rb, n_rows)

    def body(i, carry):
        fn(pl.ds(pl.multiple_of(i * rb, rb), rb))
        return carry
    lax.fori_loop(0, n_rows // rb, body, 0, unroll=unroll)


def _rows8(v):
    acc = v[0:SUB]
    for k in range(1, v.shape[0] // SUB):
        acc = acc + v[k * SUB:(k + 1) * SUB]
    return acc


def _sigmoid(z):
    return 0.5 * jnp.tanh(0.5 * z) + 0.5


def _dot(a, b):
    return jnp.dot(a, b, preferred_element_type=F32)


def _dot_nt(a, b):
    return lax.dot_general(a, b, (((1,), (1,)), ((), ())), preferred_element_type=F32)


def _dot_tn(a, b):
    return lax.dot_general(a, b, (((0,), (0,)), ((), ())), preferred_element_type=F32)


def _replicate_taps(w_ref, wrep, taps):
    for k in range(taps):
        wrep[pl.ds(k * SUB, SUB), :] = jnp.broadcast_to(w_ref[pl.ds(k, 1), :], (SUB, w_ref.shape[1]))


def _shift_copies(win, shf, lanes):
    span = win.shape[0] - SUB
    for r in range(1, SUB):
        for j0 in range(0, span, 64):
            n = min(64, span - j0)
            shf[r - 1, pl.ds(j0, n), lanes] = win[pl.ds(j0 + r, n), lanes]


def _rows_at(win, shf, off, rb, lanes):
    if shf is None or off % SUB == 0:
        return win[pl.ds(off, rb), lanes]
    return shf[off % SUB - 1, pl.ds(off - off % SUB, rb), lanes]


def _conv_taps(win, wrep, out, *, taps, n_rows, base, width, transposed=False, bias_ref=None, shf=None):
    rb = min(64, n_rows)

    def lane_body(cb, carry):
        lanes = pl.ds(pl.multiple_of(cb * LANES, LANES), LANES)
        if shf is not None:
            _shift_copies(win, shf, lanes)
        for r0 in range(0, n_rows, rb):
            acc = None
            for k in range(taps):
                off = (taps - 1 - k) if transposed else (k - (taps - 1))
                wk = jnp.tile(wrep[pl.ds(k * SUB, SUB), lanes], (rb // SUB, 1))
                term = wk * _rows_at(win, shf, base + r0 + off, rb, lanes)
                acc = term if acc is None else acc + term
            if bias_ref is not None:
                acc = acc + bias_ref[:, lanes]
            out[pl.ds(r0, rb), lanes] = acc.astype(out.dtype)
        return carry

    lax.fori_loop(0, width // LANES, lane_body, 0)


def _conv_bwd_taps(win, wrep, x_cur, dx_out, dw_acc, *, taps, n_rows, width, shf=None):
    rb = min(32 if taps > 8 else 64, n_rows)

    def lane_body(cb, carry):
        lanes = pl.ds(pl.multiple_of(cb * LANES, LANES), LANES)
        if shf is not None:
            _shift_copies(win, shf, lanes)
        sums = [None] * taps
        for r0 in range(0, n_rows, rb):
            xv = x_cur[pl.ds(r0, rb), lanes].astype(F32)
            acc = None
            for k in range(taps):
                shifted = _rows_at(win, shf, r0 + taps - 1 - k, rb, lanes)
                term = jnp.tile(wrep[pl.ds(k * SUB, SUB), lanes], (rb // SUB, 1)) * shifted
                acc = term if acc is None else acc + term
                part = _rows8(xv * shifted)
                sums[k] = part if sums[k] is None else sums[k] + part
            dx_out[pl.ds(r0, rb), lanes] = acc.astype(dx_out.dtype)
        for k in range(taps):
            dw_acc[pl.ds(k * SUB, SUB), lanes] += sums[k]
        return carry

    lax.fori_loop(0, width // LANES, lane_body, 0)


def _fold8(acc_ref, taps):
    return jnp.concatenate(
        [jnp.sum(acc_ref[pl.ds(k * SUB, SUB), :], axis=0, keepdims=True) for k in range(taps)], axis=0)


def _seq_tile(s_len):
    return min(256, s_len)


def _mm_tile(s_len):
    return min(512, s_len)


def _ff_chunk(ff):
    best = LANES
    for c in range(LANES, 1408 + 1, LANES):
        if ff % c == 0:
            best = c
    return best


def _col_tile(n):
    for c in (512, 1408, 256, LANES):
        if n % c == 0:
            return c
    return n


def _rms_matmul(x, g, wt, b, *, name, dep=None):
    s_len, d = x.shape
    n = wt.shape[0]
    tm = _mm_tile(s_len)
    cn = _col_tile(n)
    has_bias = b is not None

    def body(*refs):
        x_ref, g_ref, w_ref = refs[0:3]
        b_ref = refs[3] if has_bias else None
        o_ref, h_ref = refs[-2:]

        def blk(rows):
            xv = x_ref[rows, :]
            r = lax.rsqrt(jnp.mean(xv * xv, axis=-1, keepdims=True) + EPS)
            h_ref[rows, :] = ((xv * r) * g_ref[...]).astype(BF16)

        rb = min(128, tm)
        for r0 in range(0, tm, rb):
            blk(pl.ds(r0, rb))
        for j in range(n // cn):
            acc = _dot_nt(h_ref[...], w_ref[j * cn:(j + 1) * cn, :])
            if has_bias:
                acc = acc + b_ref[:, j * cn:(j + 1) * cn]
            o_ref[:, j * cn:(j + 1) * cn] = acc.astype(BF16)

    in_specs = [pl.BlockSpec((tm, d), lambda i: (i, 0)), _resident((1, d), lambda i: (0, 0)),
                _resident((n, d), lambda i: (0, 0))]
    args = [x, g, wt]
    if has_bias:
        in_specs.append(_resident((1, n), lambda i: (0, 0)))
        args.append(b)
    in_specs.append(ANY)
    args.append(x if dep is None else dep)
    return pl.pallas_call(
        body, grid=(s_len // tm,), in_specs=in_specs,
        out_specs=[pl.BlockSpec((tm, n), lambda i: (i, 0)), pl.BlockSpec((tm, d), lambda i: (i, 0))],
        out_shape=[jax.ShapeDtypeStruct((s_len, n), BF16), jax.ShapeDtypeStruct((s_len, d), BF16)],
        compiler_params=_params("parallel"), name=name,
    )(*args)


def _mix_windows(u_ref, uh_ref, gw, pw, first, t):
    c = D_CONF
    uh = uh_ref[...].astype(F32)
    gw[0:HALO, :] = jnp.where(first, 0.0, uh[:, 0:c] * _sigmoid(uh[:, c:2 * c]))
    pw[0:HALO3, :] = jnp.where(first, 0.0, uh[HALO - HALO3:HALO, 3 * c:4 * c] * uh[HALO - HALO3:HALO, 4 * c:5 * c])

    def blk(rows):
        dst = pl.ds(pl.multiple_of(rows.start + HALO, SUB), rows.size)
        gw[dst, :] = u_ref[rows, 0:c].astype(F32) * _sigmoid(u_ref[rows, c:2 * c].astype(F32))
        dst3 = pl.ds(pl.multiple_of(rows.start + HALO3, SUB), rows.size)
        pw[dst3, :] = u_ref[rows, 3 * c:4 * c].astype(F32) * u_ref[rows, 4 * c:5 * c].astype(F32)
    _row_loop(t, 64, blk)


def _mix_fwd(u, x0, wa, ba, lg, lb, wb, w_out, *, name):
    s_len, d_in = u.shape
    d = x0.shape[1]
    c = D_CONF
    t = _seq_tile(s_len)
    per = t // HALO

    def body(u_ref, uh_ref, x0_ref, wa_ref, ba_ref, lg_ref, lb_ref, wb_ref, wo_ref, y_ref, x1_ref, ca, cb,
             gw, pw, wrep_a, wrep_b, shf):
        first = pl.program_id(0) == 0
        _mix_windows(u_ref, uh_ref, gw, pw, first, t)
        _replicate_taps(wa_ref, wrep_a, CONF_K)
        _replicate_taps(wb_ref, wrep_b, SHORT_K)
        _conv_taps(gw, wrep_a, ca, taps=CONF_K, n_rows=t, base=HALO, width=c, bias_ref=ba_ref, shf=shf)
        _conv_taps(pw, wrep_b, cb, taps=SHORT_K, n_rows=t, base=HALO3, width=c)

        def blk(rows):
            cv = ca[rows, :]
            mu = jnp.mean(cv, axis=-1, keepdims=True)
            xc = cv - mu
            var = jnp.mean(xc * xc, axis=-1, keepdims=True)
            ln = (xc * lax.rsqrt(var + EPS)) * lg_ref[...] + lb_ref[...]
            y_ref[rows, 0:c] = (ln * _sigmoid(ln)).astype(BF16)
            y_ref[rows, c:2 * c] = (u_ref[rows, 2 * c:3 * c].astype(F32) * cb[rows, :]).astype(BF16)
        _row_loop(t, 64, blk)
        x1_ref[...] = x0_ref[...] + _dot(y_ref[...], wo_ref[...])

    small = lambda r: _resident((r, c), lambda i: (0, 0))
    return pl.pallas_call(
        body, grid=(s_len // t,),
        in_specs=[pl.BlockSpec((t, d_in), lambda i: (i, 0)),
                  pl.BlockSpec((HALO, d_in), lambda i: (jnp.maximum(i * per - 1, 0), 0)),
                  pl.BlockSpec((t, d), lambda i: (i, 0)),
                  small(CONF_K), small(1), small(1), small(1), small(SHORT_K),
                  _resident((2 * c, d), lambda i: (0, 0))],
        out_specs=[pl.BlockSpec((t, 2 * c), lambda i: (i, 0)), pl.BlockSpec((t, d), lambda i: (i, 0)),
                   pl.BlockSpec((t, c), lambda i: (i, 0)), pl.BlockSpec((t, c), lambda i: (i, 0))],
        out_shape=[jax.ShapeDtypeStruct((s_len, 2 * c), BF16), jax.ShapeDtypeStruct((s_len, d), F32),
                   jax.ShapeDtypeStruct((s_len, c), F32), jax.ShapeDtypeStruct((s_len, c), F32)],
        scratch_shapes=[pltpu.VMEM((HALO + t, c), F32), pltpu.VMEM((HALO3 + t, c), F32),
                        pltpu.VMEM((CONF_K * SUB, c), F32), pltpu.VMEM((SHORT_K * SUB, c), F32),
                        pltpu.VMEM((SUB - 1, HALO + t, c), F32)],
        compiler_params=_params("arbitrary"), name=name,
    )(u, u, x0, wa, ba, lg, lb, wb, w_out)


def _ffn_windows(ug_ref, ugh_ref, uv_ref, uvh_ref, gwin, vwin, first, t):
    lo = HALO3_BLK - HALO3
    gwin[0:HALO3, :] = jnp.where(first, 0.0, ugh_ref[...].astype(F32)[lo:HALO3_BLK])
    vwin[0:HALO3, :] = jnp.where(first, 0.0, uvh_ref[...].astype(F32)[lo:HALO3_BLK])

    def blk(rows):
        dst = pl.ds(pl.multiple_of(rows.start + HALO3, SUB), rows.size)
        gwin[dst, :] = ug_ref[rows, :].astype(F32)
        vwin[dst, :] = uv_ref[rows, :].astype(F32)
    _row_loop(t, 64, blk)


def _ffn_fwd(uf, x1, wf, w_down, *, name, dep=None):
    s_len, ff2 = uf.shape
    ff = ff2 // 2
    d = x1.shape[1]
    t = _seq_tile(s_len)
    fc = _ff_chunk(ff)
    nc = ff // fc
    per = t // HALO3_BLK

    def body(ug_ref, ugh_ref, uv_ref, uvh_ref, x1_ref, wfg_ref, wfv_ref, wd_ref, dep_ref,
             act_ref, x2_ref, cg_ref, cv_ref, gwin, vwin, cg, cv, wrep_g, wrep_v):
        first = pl.program_id(0) == 0
        _ffn_windows(ug_ref, ugh_ref, uv_ref, uvh_ref, gwin, vwin, first, t)
        _replicate_taps(wfg_ref, wrep_g, SHORT_K)
        _replicate_taps(wfv_ref, wrep_v, SHORT_K)
        _conv_taps(gwin, wrep_g, cg, taps=SHORT_K, n_rows=t, base=HALO3, width=fc)
        _conv_taps(vwin, wrep_v, cv, taps=SHORT_K, n_rows=t, base=HALO3, width=fc)

        def blk(rows):
            gv = cg[rows, :]
            vv = cv[rows, :]
            cg_ref[rows, :] = gv.astype(BF16)
            cv_ref[rows, :] = vv.astype(BF16)
            act_ref[rows, :] = ((gv * _sigmoid(gv)) * vv).astype(BF16)
        _row_loop(t, 32, blk, unroll=2)

        @pl.when(pl.program_id(1) == 0)
        def _():
            x2_ref[...] = x1_ref[...]
        x2_ref[...] += _dot(act_ref[...], wd_ref[...])

    halo_map = lambda off: (lambda i, j: (jnp.maximum(i * per - 1, 0), j + off))
    return pl.pallas_call(
        body, grid=(s_len // t, nc),
        in_specs=[pl.BlockSpec((t, fc), lambda i, j: (i, j)), pl.BlockSpec((HALO3_BLK, fc), halo_map(0)),
                  pl.BlockSpec((t, fc), lambda i, j: (i, j + nc)), pl.BlockSpec((HALO3_BLK, fc), halo_map(nc)),
                  pl.BlockSpec((t, d), lambda i, j: (i, 0)),
                  pl.BlockSpec((SHORT_K, fc), lambda i, j: (0, j)),
                  pl.BlockSpec((SHORT_K, fc), lambda i, j: (0, j + nc)),
                  pl.BlockSpec((fc, d), lambda i, j: (j, 0)), ANY],
        out_specs=[pl.BlockSpec((t, fc), lambda i, j: (i, j)), pl.BlockSpec((t, d), lambda i, j: (i, 0)),
                   pl.BlockSpec((t, fc), lambda i, j: (i, j)), pl.BlockSpec((t, fc), lambda i, j: (i, j))],
        out_shape=[jax.ShapeDtypeStruct((s_len, ff), BF16), jax.ShapeDtypeStruct((s_len, d), F32),
                   jax.ShapeDtypeStruct((s_len, ff), BF16), jax.ShapeDtypeStruct((s_len, ff), BF16)],
        scratch_shapes=[pltpu.VMEM((HALO3 + t, fc), F32), pltpu.VMEM((HALO3 + t, fc), F32),
                        pltpu.VMEM((t, fc), F32), pltpu.VMEM((t, fc), F32),
                        pltpu.VMEM((SHORT_K * SUB, fc), F32), pltpu.VMEM((SHORT_K * SUB, fc), F32)],
        compiler_params=_params("parallel", "arbitrary"), name=name,
    )(uf, uf, uf, uf, x1, wf, wf, w_down, uf if dep is None else dep)


def _loss_bwd(x, g, target, *, name):
    s_len, d = x.shape
    t = _seq_tile(s_len)

    def body(x_ref, g_ref, t_ref, l_ref, dx_ref, dxb_ref, dg_ref):
        @pl.when(pl.program_id(0) == 0)
        def _():
            l_ref[...] = jnp.zeros_like(l_ref)
            dg_ref[...] = jnp.zeros_like(dg_ref)

        def blk(rows):
            xv = x_ref[rows, :]
            r = lax.rsqrt(jnp.mean(xv * xv, axis=-1, keepdims=True) + EPS)
            xn = xv * r
            e = xn * g_ref[...] - t_ref[rows, :]
            l_ref[...] += _rows8(e * e)
            dy = e * (1.0 / d)
            dg_ref[...] += _rows8(dy * xn)
            dn = dy * g_ref[...]
            dx = r * (dn - xn * jnp.mean(dn * xn, axis=-1, keepdims=True))
            dx_ref[rows, :] = dx
            dxb_ref[rows, :] = dx.astype(BF16)
        _row_loop(t, 64, blk)

    row = pl.BlockSpec((t, d), lambda i: (i, 0))
    part = pl.BlockSpec((SUB, d), lambda i: (0, 0))
    return pl.pallas_call(
        body, grid=(s_len // t,),
        in_specs=[row, _resident((1, d), lambda i: (0, 0)), row],
        out_specs=[part, row, row, part],
        out_shape=[jax.ShapeDtypeStruct((SUB, d), F32), jax.ShapeDtypeStruct((s_len, d), F32),
                   jax.ShapeDtypeStruct((s_len, d), BF16), jax.ShapeDtypeStruct((SUB, d), F32)],
        compiler_params=_params("arbitrary"), name=name,
    )(x, g, target)


def _ffn_bwd(dx2, uf, cg, cv, wf, w_down, *, name, dep=None):
    s_len, ff2 = uf.shape
    ff = ff2 // 2
    d = dx2.shape[1]
    t = _seq_tile(s_len)
    n_t = s_len // t
    fc = _ff_chunk(ff)
    nc = ff // fc

    def body(dx_ref, ug_ref, uv_ref, cg_ref, cv_ref, wfg_ref, wfv_ref, wd_ref, dep_ref,
             dug_ref, duv_ref, dwg_ref, dwv_ref, dact, dgw, dvw, awg, awv, wrep_g, wrep_v):
        i = pl.program_id(1)

        @pl.when(i == 0)
        def _():
            dgw[t:t + HALO3, :] = jnp.zeros((HALO3, fc), F32)
            dvw[t:t + HALO3, :] = jnp.zeros((HALO3, fc), F32)
            awg[...] = jnp.zeros_like(awg)
            awv[...] = jnp.zeros_like(awv)

        _replicate_taps(wfg_ref, wrep_g, SHORT_K)
        _replicate_taps(wfv_ref, wrep_v, SHORT_K)

        def blk(rows):
            gv = cg_ref[rows, :].astype(F32)
            sg = _sigmoid(gv)
            da = dact[rows, :]
            dgw[rows, :] = (da * cv_ref[rows, :].astype(F32)) * (sg * (1.0 + gv * (1.0 - sg)))
            dvw[rows, :] = da * (gv * sg)

        half = t // 2
        rb = min(64, half)
        for lo in range(0, t, half):
            dact[lo:lo + half, :] = _dot_nt(dx_ref[lo:lo + half, :], wd_ref[...])
            for r0 in range(lo, lo + half, rb):
                blk(pl.ds(r0, rb))

        _conv_bwd_taps(dgw, wrep_g, ug_ref, dug_ref, awg, taps=SHORT_K, n_rows=t, width=fc)
        _conv_bwd_taps(dvw, wrep_v, uv_ref, duv_ref, awv, taps=SHORT_K, n_rows=t, width=fc)
        dgw[t:t + HALO3, :] = dgw[0:HALO3, :]
        dvw[t:t + HALO3, :] = dvw[0:HALO3, :]

        @pl.when(i == n_t - 1)
        def _():
            dwg_ref[...] = _fold8(awg, SHORT_K)
            dwv_ref[...] = _fold8(awv, SHORT_K)

    rev = lambda i: n_t - 1 - i
    gate = pl.BlockSpec((t, fc), lambda j, i: (rev(i), j))
    value = pl.BlockSpec((t, fc), lambda j, i: (rev(i), j + nc))
    return pl.pallas_call(
        body, grid=(nc, n_t),
        in_specs=[pl.BlockSpec((t, d), lambda j, i: (rev(i), 0)), gate, value, gate, gate,
                  pl.BlockSpec((SHORT_K, fc), lambda j, i: (0, j)),
                  pl.BlockSpec((SHORT_K, fc), lambda j, i: (0, j + nc)),
                  pl.BlockSpec((fc, d), lambda j, i: (j, 0)), ANY],
        out_specs=[gate, gate,
                   pl.BlockSpec((SHORT_K, fc), lambda j, i: (0, j)), pl.BlockSpec((SHORT_K, fc), lambda j, i: (0, j))],
        out_shape=[jax.ShapeDtypeStruct((s_len, ff), BF16), jax.ShapeDtypeStruct((s_len, ff), BF16),
                   jax.ShapeDtypeStruct((SHORT_K, ff), F32), jax.ShapeDtypeStruct((SHORT_K, ff), F32)],
        scratch_shapes=[pltpu.VMEM((t, fc), F32),
                        pltpu.VMEM((t + HALO3, fc), F32), pltpu.VMEM((t + HALO3, fc), F32),
                        pltpu.VMEM((SHORT_K * SUB, fc), F32), pltpu.VMEM((SHORT_K * SUB, fc), F32),
                        pltpu.VMEM((SHORT_K * SUB, fc), F32), pltpu.VMEM((SHORT_K * SUB, fc), F32)],
        compiler_params=_params("arbitrary", "arbitrary"), name=name,
    )(dx2, uf, uf, cg, cv, wf, wf, w_down, uf if dep is None else dep)


def _mix_bwd(dx1, u, ca, cb, wa, lg, lb, wb, w_out, *, name):
    s_len, d_in = u.shape
    d = dx1.shape[1]
    c = D_CONF
    t = _seq_tile(s_len)
    n_t = s_len // t

    def body(dx_ref, u_ref, ca_ref, cb_ref, wa_ref, lg_ref, lb_ref, wb_ref, wo_ref,
             du_ref, dwa_ref, dwb_ref, dba_ref, dlg_ref, dlb_ref, dbin_ref,
             glu, prod, dyc, dcaw, dcbw, dglu, dp, awa, awb, wrep_a, wrep_b, shf):
        i = pl.program_id(0)
        dyc[...] = _dot_nt(dx_ref[...], wo_ref[...])
        _replicate_taps(wa_ref, wrep_a, CONF_K)
        _replicate_taps(wb_ref, wrep_b, SHORT_K)

        @pl.when(i == 0)
        def _():
            dcaw[t:t + HALO, :] = jnp.zeros((HALO, c), F32)
            dcbw[t:t + HALO3, :] = jnp.zeros((HALO3, c), F32)
            awa[...] = jnp.zeros_like(awa)
            awb[...] = jnp.zeros_like(awb)
            dba_ref[...] = jnp.zeros_like(dba_ref)
            dlg_ref[...] = jnp.zeros_like(dlg_ref)
            dlb_ref[...] = jnp.zeros_like(dlb_ref)
            dbin_ref[...] = jnp.zeros_like(dbin_ref)

        def blk1(rows):
            cv = ca_ref[rows, :]
            mu = jnp.mean(cv, axis=-1, keepdims=True)
            xc = cv - mu
            rstd = lax.rsqrt(jnp.mean(xc * xc, axis=-1, keepdims=True) + EPS)
            nrm = xc * rstd
            ln = nrm * lg_ref[...] + lb_ref[...]
            sg = _sigmoid(ln)
            dln = dyc[rows, 0:c] * (sg * (1.0 + ln * (1.0 - sg)))
            dlg_ref[...] += _rows8(dln * nrm)
            dlb_ref[...] += _rows8(dln)
            dn = dln * lg_ref[...]
            dca = rstd * (dn - jnp.mean(dn, axis=-1, keepdims=True)
                          - nrm * jnp.mean(dn * nrm, axis=-1, keepdims=True))
            dcaw[rows, :] = dca
            dba_ref[...] += _rows8(dca)
            ds = dyc[rows, c:2 * c]
            dgb = ds * cb_ref[rows, :]
            dcbw[rows, :] = ds * u_ref[rows, 2 * c:3 * c].astype(F32)
            du_ref[rows, 2 * c:3 * c] = dgb.astype(BF16)
            dbin_ref[:, 2 * c:3 * c] += _rows8(dgb)
            glu[rows, :] = u_ref[rows, 0:c].astype(F32) * _sigmoid(u_ref[rows, c:2 * c].astype(F32))
            prod[rows, :] = u_ref[rows, 3 * c:4 * c].astype(F32) * u_ref[rows, 4 * c:5 * c].astype(F32)
        _row_loop(t, 64, blk1, unroll=2)

        _conv_bwd_taps(dcaw, wrep_a, glu, dglu, awa, taps=CONF_K, n_rows=t, width=c, shf=shf)
        _conv_bwd_taps(dcbw, wrep_b, prod, dp, awb, taps=SHORT_K, n_rows=t, width=c)
        dcaw[t:t + HALO, :] = dcaw[0:HALO, :]
        dcbw[t:t + HALO3, :] = dcbw[0:HALO3, :]

        def blk2(rows):
            av = u_ref[rows, 0:c].astype(F32)
            sg = _sigmoid(u_ref[rows, c:2 * c].astype(F32))
            dg = dglu[rows, :]
            d_av = dg * sg
            d_ag = (dg * av) * (sg * (1.0 - sg))
            dpv = dp[rows, :]
            d_gc = dpv * u_ref[rows, 4 * c:5 * c].astype(F32)
            d_vs = dpv * u_ref[rows, 3 * c:4 * c].astype(F32)
            du_ref[rows, 0:c] = d_av.astype(BF16)
            du_ref[rows, c:2 * c] = d_ag.astype(BF16)
            du_ref[rows, 3 * c:4 * c] = d_gc.astype(BF16)
            du_ref[rows, 4 * c:5 * c] = d_vs.astype(BF16)
            dbin_ref[:, 0:c] += _rows8(d_av)
            dbin_ref[:, c:2 * c] += _rows8(d_ag)
            dbin_ref[:, 3 * c:4 * c] += _rows8(d_gc)
            dbin_ref[:, 4 * c:5 * c] += _rows8(d_vs)
        _row_loop(t, 64, blk2)

        @pl.when(i == n_t - 1)
        def _():
            dwa_ref[...] = _fold8(awa, CONF_K)
            dwb_ref[...] = _fold8(awb, SHORT_K)

    rev = lambda i: n_t - 1 - i
    small_in = lambda r: _resident((r, c), lambda i: (0, 0))
    small = lambda r: pl.BlockSpec((r, c), lambda i: (0, 0))
    return pl.pallas_call(
        body, grid=(n_t,),
        in_specs=[pl.BlockSpec((t, d), lambda i: (rev(i), 0)),
                  pl.BlockSpec((t, d_in), lambda i: (rev(i), 0)),
                  pl.BlockSpec((t, c), lambda i: (rev(i), 0)), pl.BlockSpec((t, c), lambda i: (rev(i), 0)),
                  small_in(CONF_K), small_in(1), small_in(1), small_in(SHORT_K),
                  _resident((2 * c, d), lambda i: (0, 0))],
        out_specs=[pl.BlockSpec((t, d_in), lambda i: (rev(i), 0)),
                   small(CONF_K), small(SHORT_K), small(SUB), small(SUB), small(SUB),
                   pl.BlockSpec((SUB, d_in), lambda i: (0, 0))],
        out_shape=[jax.ShapeDtypeStruct((s_len, d_in), BF16),
                   jax.ShapeDtypeStruct((CONF_K, c), F32), jax.ShapeDtypeStruct((SHORT_K, c), F32),
                   jax.ShapeDtypeStruct((SUB, c), F32), jax.ShapeDtypeStruct((SUB, c), F32),
                   jax.ShapeDtypeStruct((SUB, c), F32), jax.ShapeDtypeStruct((SUB, d_in), F32)],
        scratch_shapes=[pltpu.VMEM((t, c), F32), pltpu.VMEM((t, c), F32), pltpu.VMEM((t, 2 * c), F32),
                        pltpu.VMEM((t + HALO, c), F32), pltpu.VMEM((t + HALO3, c), F32),
                        pltpu.VMEM((t, c), F32), pltpu.VMEM((t, c), F32),
                        pltpu.VMEM((CONF_K * SUB, c), F32), pltpu.VMEM((SHORT_K * SUB, c), F32),
                        pltpu.VMEM((CONF_K * SUB, c), F32), pltpu.VMEM((SHORT_K * SUB, c), F32),
                        pltpu.VMEM((SUB - 1, t + HALO, c), F32)],
        compiler_params=_params("arbitrary"), name=name,
    )(dx1, u, ca, cb, wa, lg, lb, wb, w_out)


def _matmul_tn(a, b, *, name, into=None, part=0, n_parts=1):
    s_len, k = a.shape
    n = b.shape[1]
    tk = _col_tile(k)
    per = k // tk

    def body(*refs):
        a_ref, b_ref = refs[0], refs[1]
        o_ref = refs[-1]
        o_ref[...] = _dot_tn(a_ref[...], b_ref[...]).astype(BF16)

    in_specs = [pl.BlockSpec((s_len, tk), lambda j: (0, j)), _resident((s_len, n), lambda j: (0, 0))]
    args = [a, b]
    aliases = {}
    if into is not None:
        in_specs.append(ANY)
        args.append(into)
        aliases = {2: 0}
    return pl.pallas_call(
        body, grid=(per,), in_specs=in_specs,
        out_specs=pl.BlockSpec((tk, n), lambda j: (part * per + j, 0)),
        out_shape=jax.ShapeDtypeStruct((n_parts * k, n), BF16),
        input_output_aliases=aliases,
        compiler_params=_params("parallel"), name=name,
    )(*args)


def _matmul_rmsbwd(dzs, wt, x, g, dx_in, *, name, dep=None):
    s_len, d = x.shape
    n_z = len(dzs)
    nj = dzs[0].shape[1]
    t = _mm_tile(s_len)

    def body(*refs):
        dz_refs = refs[0:n_z]
        w_refs = refs[n_z:2 * n_z]
        x_ref, g_ref, dxi_ref, _, dx_ref, dxb_ref, dg_ref, dh = refs[2 * n_z:]

        @pl.when(pl.program_id(0) == 0)
        def _():
            dg_ref[...] = jnp.zeros_like(dg_ref)

        def blk(rows):
            xv = x_ref[rows, :]
            r = lax.rsqrt(jnp.mean(xv * xv, axis=-1, keepdims=True) + EPS)
            xn = xv * r
            dhv = dh[rows, :]
            dg_ref[...] += _rows8(dhv * xn)
            dn = dhv * g_ref[...]
            dx = dxi_ref[rows, :] + r * (dn - xn * jnp.mean(dn * xn, axis=-1, keepdims=True))
            dx_ref[rows, :] = dx
            dxb_ref[rows, :] = dx.astype(BF16)

        half = t // 2
        rb = min(128, half)
        for lo in range(0, t, half):
            acc = _dot(dz_refs[0][lo:lo + half, :], w_refs[0][...])
            for q in range(1, n_z):
                acc = acc + _dot(dz_refs[q][lo:lo + half, :], w_refs[q][...])
            dh[lo:lo + half, :] = acc
            for r0 in range(lo, lo + half, rb):
                blk(pl.ds(r0, rb))

    row = pl.BlockSpec((t, d), lambda i: (i, 0))
    in_specs = [pl.BlockSpec((t, nj), lambda i: (i, 0)) for _ in range(n_z)]
    in_specs += [_resident((nj, d), functools.partial(lambda q, i: (q, 0), q)) for q in range(n_z)]
    in_specs += [row, _resident((1, d), lambda i: (0, 0)), row, ANY]
    return pl.pallas_call(
        body, grid=(s_len // t,), in_specs=in_specs,
        out_specs=[row, row, pl.BlockSpec((SUB, d), lambda i: (0, 0))],
        out_shape=[jax.ShapeDtypeStruct((s_len, d), F32), jax.ShapeDtypeStruct((s_len, d), BF16),
                   jax.ShapeDtypeStruct((SUB, d), F32)],
        scratch_shapes=[pltpu.VMEM((t, d), F32)],
        compiler_params=_params("arbitrary"), name=name,
    )(*dzs, *([wt] * n_z), x, g, dx_in, x if dep is None else dep)


def _row(v):
    return v.reshape(1, -1)


def _layer_fwd(x0, p, tag, dep=None, before_up=None):
    u, h1 = _rms_matmul(x0, _row(p["mix_norm_g"]), p["w_in_t"], _row(p["b_in"]), name=f"in_proj_{tag}", dep=dep)
    ycat, x1, ca, cb = _mix_fwd(u, x0, p["conv_a_w"], _row(p["conv_a_b"]), _row(p["ln_a_g"]), _row(p["ln_a_b"]),
                            p["conv_b_w"], p["w_out"], name=f"mix_fwd_{tag}")
    if before_up is not None:
        before_up(x1)
    uf, h2 = _rms_matmul(x1, _row(p["ffn_norm_g"]), p["w_up_t"], None, name=f"up_proj_{tag}")
    act, x2, cg, cv = _ffn_fwd(uf, x1, p["conv_f_w"], p["w_down"], name=f"ffn_fwd_{tag}")
    return x2, dict(x0=x0, h1=h1, u=u, ca=ca, cb=cb, ycat=ycat, x1=x1, h2=h2, uf=uf, cg=cg, cv=cv, act=act)


def _layer_bwd(dx2, dx2_b, p, saved, tag, after_ffn, after_mix, dep=None):
    dug, duv, dwf_g, dwf_v = _ffn_bwd(dx2_b, saved["uf"], saved["cg"], saved["cv"], p["conv_f_w"], p["w_down"],
                                      name=f"ffn_bwd_{tag}", dep=dep)
    g_down = _matmul_tn(saved["act"], dx2_b, name=f"dw_down_{tag}")
    g_up = _matmul_tn(dug, saved["h2"], name=f"dw_up_g_{tag}", n_parts=2)
    g_up = _matmul_tn(duv, saved["h2"], name=f"dw_up_v_{tag}", into=g_up, part=1, n_parts=2)
    dep_ffn = after_ffn(dict(w_up=g_up, w_down=g_down))
    dx1, dx1_b, dg2 = _matmul_rmsbwd([dug, duv], p["w_up_t"], saved["x1"], _row(p["ffn_norm_g"]), dx2,
                                     name=f"dh_ffn_{tag}", dep=dep_ffn)
    du, dwa, dwb, dba, dlg, dlb, dbin = _mix_bwd(
        dx1_b, saved["u"], saved["ca"], saved["cb"], p["conv_a_w"], _row(p["ln_a_g"]), _row(p["ln_a_b"]),
        p["conv_b_w"], p["w_out"], name=f"mix_bwd_{tag}")
    g_out = _matmul_tn(saved["ycat"], dx1_b, name=f"dw_out_{tag}")
    g_in = _matmul_tn(du, saved["h1"], name=f"dw_in_{tag}")
    conv = dict(conv_a_w=dwa, conv_b_w=dwb, conv_f_w=jnp.concatenate([dwf_g, dwf_v], axis=1))
    dep_mix = after_mix(dict(w_in=g_in, w_out=g_out), conv)
    dx0, dx0_b, dg1 = _matmul_rmsbwd([du], p["w_in_t"], saved["x0"], _row(p["mix_norm_g"]), dx1,
                                     name=f"dh_mix_{tag}", dep=dep_mix)
    rep = dict(mix_norm_g=dg1, b_in=dbin, conv_a_b=dba, ln_a_g=dlg, ln_a_b=dlb, ffn_norm_g=dg2)
    return dx0, dx0_b, rep


def _place():
    return lax.axis_index("x"), lax.axis_index("y"), lax.axis_index("c")


def _all_gather(arrs, *, name):
    n_a = len(arrs)

    def body(*refs):
        ins = refs[0:n_a]
        outs = refs[n_a:2 * n_a]
        send_sems, recv_sems, local_sems = refs[2 * n_a:]
        x, y, c = _place()
        sibling = (x, y, 1 - c)
        chips = [(1 - x, y), (x, 1 - y), (1 - x, 1 - y)]

        def slot(a, px, py, pc):
            return outs[a].at[4 * px + 2 * py + pc]

        def copy(a, k, block, to, src=None):
            return pltpu.make_async_remote_copy(
                src_ref=slot(a, *block) if src is None else src, dst_ref=slot(a, *block),
                send_sem=send_sems.at[a, k], recv_sem=recv_sems.at[a, k],
                device_id=to, device_id_type=MESH)

        me = (x, y, c)
        mine = [pltpu.make_async_copy(ins[a], slot(a, *me), local_sems.at[a]) for a in range(n_a)]
        for cp in mine:
            cp.start()
        started = []
        for a in range(n_a):
            first = [copy(a, 0, me, sibling, src=ins[a])]
            first += [copy(a, 1 + j, me, (*chip, c), src=ins[a]) for j, chip in enumerate(chips)]
            for cp in first:
                cp.start()
            started += first
        for a in range(n_a):
            for j, chip in enumerate(chips):
                copy(a, 1 + j, (*chip, c), me).wait_recv()
                passed = copy(a, 4 + j, (*chip, c), sibling)
                passed.start()
                started.append(passed)
        for a in range(n_a):
            copy(a, 0, sibling, me).wait_recv()
            for j, chip in enumerate(chips):
                copy(a, 4 + j, (*chip, 1 - c), me).wait_recv()
        for cp in started:
            cp.wait_send()
        for cp in mine:
            cp.wait()

    return pl.pallas_call(
        body, in_specs=[ANY] * n_a, out_specs=[ANY] * n_a,
        out_shape=[jax.ShapeDtypeStruct((N_DEV, *a.shape), a.dtype) for a in arrs],
        scratch_shapes=[pltpu.SemaphoreType.DMA((n_a, 7)), pltpu.SemaphoreType.DMA((n_a, 7)),
                        pltpu.SemaphoreType.DMA((n_a,))],
        name=name,
    )(*arrs)


def _sibling_exchange(arrs, *, name):
    n_a = len(arrs)

    def body(*refs):
        ins = refs[0:n_a]
        outs = refs[n_a:2 * n_a]
        send_sems, recv_sems = refs[2 * n_a:]
        x, y, c = _place()
        copies = [pltpu.make_async_remote_copy(
            src_ref=ins[a].at[:, 1 - c], dst_ref=outs[a], send_sem=send_sems.at[a], recv_sem=recv_sems.at[a],
            device_id=(x, y, 1 - c), device_id_type=MESH) for a in range(n_a)]
        for cp in copies:
            cp.start()
        for cp in copies:
            cp.wait()

    return pl.pallas_call(
        body, in_specs=[ANY] * n_a, out_specs=[ANY] * n_a,
        out_shape=[jax.ShapeDtypeStruct((N_CHIP, *a.shape[2:]), a.dtype) for a in arrs],
        scratch_shapes=[pltpu.SemaphoreType.DMA((n_a,)), pltpu.SemaphoreType.DMA((n_a,))],
        name=name,
    )(*arrs)


def _row_tile(r, cap):
    for tr in range(min(cap, r) // 16 * 16, 0, -16):
        if r % tr == 0:
            return tr
    return r


def _pair_sum(mine, theirs, core, *, name):
    n_chip, _, r, c = mine.shape
    tr = _row_tile(r, 1024)

    def body(core_ref, a_ref, b_ref, o_ref):
        o_ref[...] = (a_ref[...].astype(F32) + b_ref[...].astype(F32)).astype(o_ref.dtype)

    return pl.pallas_call(
        body,
        grid_spec=pltpu.PrefetchScalarGridSpec(
            num_scalar_prefetch=1, grid=(n_chip, r // tr),
            in_specs=[pl.BlockSpec((None, None, tr, c), lambda q, i, core_ref: (q, core_ref[0], i, 0)),
                      pl.BlockSpec((None, tr, c), lambda q, i, core_ref: (q, i, 0))],
            out_specs=pl.BlockSpec((None, tr, c), lambda q, i, core_ref: (q, i, 0))),
        out_shape=jax.ShapeDtypeStruct((n_chip, r, c), mine.dtype),
        compiler_params=_params("parallel", "parallel"), name=name,
    )(core, mine, theirs)


def _chip_exchange(arrs, *, name):
    n_a = len(arrs)

    def body(*refs):
        ins = refs[0:n_a]
        outs = refs[n_a:2 * n_a]
        send_sems, recv_sems, local_sems = refs[2 * n_a:]
        x, y, c = _place()
        my_chip = 2 * x + y
        chips = [(1 - x, y), (x, 1 - y), (1 - x, 1 - y)]
        mine = [pltpu.make_async_copy(ins[a].at[my_chip], outs[a].at[my_chip], local_sems.at[a]) for a in range(n_a)]
        for cp in mine:
            cp.start()
        copies = []
        for a in range(n_a):
            for j, (px, py) in enumerate(chips):
                copies.append(pltpu.make_async_remote_copy(
                    src_ref=ins[a].at[2 * px + py], dst_ref=outs[a].at[my_chip],
                    send_sem=send_sems.at[a, j], recv_sem=recv_sems.at[a, j],
                    device_id=(px, py, c), device_id_type=MESH))
        for cp in copies:
            cp.start()
        for cp in copies:
            cp.wait()
        for cp in mine:
            cp.wait()

    return pl.pallas_call(
        body, in_specs=[ANY] * n_a, out_specs=[ANY] * n_a,
        out_shape=[jax.ShapeDtypeStruct(a.shape, a.dtype) for a in arrs],
        scratch_shapes=[pltpu.SemaphoreType.DMA((n_a, 3)), pltpu.SemaphoreType.DMA((n_a, 3)),
                        pltpu.SemaphoreType.DMA((n_a,))],
        name=name,
    )(*arrs)


HBM = pl.BlockSpec(memory_space=pltpu.HBM)
SEM = pl.BlockSpec(memory_space=pltpu.SEMAPHORE)
EFFECT = pltpu.SideEffectType.DATAFLOW_SIDE_EFFECTING


def _in_hbm(a):
    return pltpu.with_memory_space_constraint(a, pltpu.HBM)


def _split_start(srcs, lands, plan, n_copies, after, *, name):
    n_s, n_l = len(srcs), len(lands)

    def body(*refs):
        src_refs = refs[0:n_s]
        land_refs = refs[n_s:n_s + n_l]
        send_sems, recv_sems = refs[n_s + n_l + 1], refs[n_s + n_l + 2]
        token = refs[-1]
        for cp in plan(src_refs, land_refs, send_sems, recv_sems):
            cp.start()
        token[...] = jnp.zeros_like(token)

    thru = [pltpu.HBM(a.shape, a.dtype) for a in list(srcs) + list(lands)]
    res = pl.pallas_call(
        body, name=name,
        out_shape=(pltpu.SemaphoreType.DMA((n_copies,)), pltpu.SemaphoreType.DMA((n_copies,)), *thru,
                   jax.ShapeDtypeStruct((SUB, LANES), F32)),
        in_specs=[HBM] * (n_s + n_l) + [ANY],
        out_specs=(SEM, SEM, *([HBM] * (n_s + n_l)), pl.BlockSpec(memory_space=pltpu.VMEM)),
        input_output_aliases={i: 2 + i for i in range(n_s + n_l)},
        compiler_params=pltpu.CompilerParams(has_side_effects=EFFECT),
    )(*[_in_hbm(a) for a in srcs], *[_in_hbm(a) for a in lands], _in_hbm(after))
    return res[0], res[1], list(res[2:2 + n_s]), list(res[2 + n_s:2 + n_s + n_l]), res[-1]


def _split_wait(send_sems, recv_sems, srcs, lands, after, plan, *, name):
    n_s, n_l = len(srcs), len(lands)

    def body(*refs):
        src_refs = refs[0:n_s]
        land_refs = refs[n_s:n_s + n_l]
        send, recv = refs[n_s + n_l], refs[n_s + n_l + 1]
        for cp in plan(src_refs, land_refs, send, recv):
            cp.wait_send()
            cp.wait_recv()

    res = pl.pallas_call(
        body, name=name,
        out_shape=tuple(pltpu.HBM(a.shape, a.dtype) for a in list(srcs) + list(lands)),
        in_specs=[HBM] * (n_s + n_l) + [SEM, SEM, ANY],
        out_specs=tuple([HBM] * (n_s + n_l)),
        input_output_aliases={i: i for i in range(n_s + n_l)},
        compiler_params=pltpu.CompilerParams(has_side_effects=EFFECT),
    )(*srcs, *lands, send_sems, recv_sems, _in_hbm(after))
    return list(res[n_s:])


def _remote(src, dst, send_sems, recv_sems, k, to):
    return pltpu.make_async_remote_copy(src_ref=src, dst_ref=dst, send_sem=send_sems.at[k], recv_sem=recv_sems.at[k],
                                        device_id=to, device_id_type=MESH)


def _gather_plan_first(src_refs, land_refs, send_sems, recv_sems):
    x, y, c = _place()
    me = 4 * x + 2 * y + c
    peers = [(x, y, 1 - c), (1 - x, y, c), (x, 1 - y, c), (1 - x, 1 - y, c)]
    return [_remote(src, land.at[me], send_sems, recv_sems, 4 * a + k, to)
            for a, (src, land) in enumerate(zip(src_refs, land_refs)) for k, to in enumerate(peers)]


def _gather_plan_second(src_refs, land_refs, send_sems, recv_sems):
    x, y, c = _place()
    chips = [(1 - x, y), (x, 1 - y), (1 - x, 1 - y)]
    out = []
    for a, land in enumerate(land_refs):
        for j, (px, py) in enumerate(chips):
            slot = land.at[4 * px + 2 * py + c]
            out.append(_remote(slot, slot, send_sems, recv_sems, 3 * a + j, (x, y, 1 - c)))
    return out


def _chips_plan(src_refs, land_refs, send_sems, recv_sems):
    x, y, c = _place()
    my_chip = 2 * x + y
    chips = [(1 - x, y), (x, 1 - y), (1 - x, 1 - y)]
    return [_remote(src.at[2 * px + py], land.at[my_chip], send_sems, recv_sems, 3 * a + j, (px, py, c))
            for a, (src, land) in enumerate(zip(src_refs, land_refs)) for j, (px, py) in enumerate(chips)]


def _landing(like_shape, dtype, own, index):
    return lax.dynamic_update_index_in_dim(lax.empty(like_shape, dtype), own, index, 0)


def _adamw_math(g, w, m, v):
    m = ADAM_B1 * m + (1.0 - ADAM_B1) * g
    v = ADAM_B2 * v + (1.0 - ADAM_B2) * (g * g)
    m_hat = m / (1.0 - ADAM_B1 ** ADAM_STEP)
    v_hat = v / (1.0 - ADAM_B2 ** ADAM_STEP)
    delta = -ADAM_LR * (m_hat / (jnp.sqrt(v_hat) + ADAM_EPS) + ADAM_WD * w)
    return delta, m, v


def _adamw_sharded(parts, w, m, v, *, name, dep=None):
    n_layers, r, c = w.shape
    n_chip = parts[0].shape[0]
    tr = _row_tile(r, 384)
    n_i = r // tr

    def body(*refs):
        p_refs = refs[0:n_layers]
        w_ref, m_ref, v_ref, _, g_out, d_out, m_out, v_out = refs[n_layers:]
        layer = pl.program_id(0)
        for l in range(n_layers):
            @pl.when(layer == l)
            def _(l=l):
                g = p_refs[l][0].astype(F32)
                for q in range(1, n_chip):
                    g = g + p_refs[l][q].astype(F32)
                delta, m_new, v_new = _adamw_math(g, w_ref[...], m_ref[...], v_ref[...])
                g_out[...] = g
                d_out[...] = delta
                m_out[...] = m_new
                v_out[...] = v_new

    def part_map(l):
        return lambda layer, i: (0, jnp.where(layer == l, i, jnp.where(layer < l, 0, n_i - 1)), 0)

    blk = pl.BlockSpec((None, tr, c), lambda layer, i: (layer, i, 0))
    return pl.pallas_call(
        body, grid=(n_layers, n_i),
        in_specs=[pl.BlockSpec((n_chip, tr, c), part_map(l)) for l in range(n_layers)] + [blk, blk, blk, ANY],
        out_specs=[blk] * 4, out_shape=[jax.ShapeDtypeStruct((n_layers, r, c), F32)] * 4,
        compiler_params=_params("arbitrary", "arbitrary"), name=name,
    )(*parts, w, m, v, w if dep is None else dep)


def _fold_partials(cols, *, name):
    widths = [c.shape[1] for c in cols]

    def body(*refs):
        o_ref = refs[-1]
        pos = 0
        for ref, width in zip(refs[:-1], widths):
            o_ref[:, pos:pos + width] = jnp.sum(ref[...], axis=0, keepdims=True)
            pos += width

    return pl.pallas_call(body, out_shape=jax.ShapeDtypeStruct((1, sum(widths)), F32), name=name)(*cols)


def _adamw_replicated(parts, names, w, m, v, n_loss, *, name):
    n_dev = parts.shape[0]
    n_layers = w[names[0]].shape[0]
    every = list(names) + ["final_norm_g"]
    n_p = len(every)

    def body(*refs):
        p_ref = refs[0]
        w_refs = dict(zip(every, refs[1:1 + n_p]))
        m_refs = dict(zip(every, refs[1 + n_p:1 + 2 * n_p]))
        v_refs = dict(zip(every, refs[1 + 2 * n_p:1 + 3 * n_p]))
        l_out = refs[1 + 3 * n_p]
        outs = refs[2 + 3 * n_p:]
        o_refs = {n: outs[4 * q:4 * q + 4] for q, n in enumerate(every)}
        acc = p_ref[0]
        for q in range(1, n_dev):
            acc = acc + p_ref[q]
        tot = jnp.sum(acc, axis=0, keepdims=True)
        pos = 0
        where = [(n, l) for l in range(n_layers) for n in names] + [("final_norm_g", 0)]
        for n, l in where:
            width = w_refs[n].shape[1]
            g = tot[:, pos:pos + width]
            pos += width
            row = pl.ds(l, 1)
            delta, m_new, v_new = _adamw_math(g, w_refs[n][row, :], m_refs[n][row, :], v_refs[n][row, :])
            for o, val in zip(o_refs[n], (g, delta, m_new, v_new)):
                o[row, :] = val
        l_out[...] = (0.5 / n_loss) * jnp.sum(tot[:, pos:pos + n_loss], axis=-1, keepdims=True)

    shapes = [jax.ShapeDtypeStruct((1, 1), F32)]
    for n in every:
        shapes += [jax.ShapeDtypeStruct(w[n].shape, F32)] * 4
    res = pl.pallas_call(
        body, out_shape=shapes,
        compiler_params=pltpu.CompilerParams(vmem_limit_bytes=VMEM_LIMIT), name=name,
    )(parts, *[w[n] for n in every], *[m[n] for n in every], *[v[n] for n in every])
    return res[0], {n: res[1 + 4 * q:5 + 4 * q] for q, n in enumerate(every)}


BIG = ("w_in", "w_out", "w_up", "w_down")
COL_SHARDED = ("w_in", "w_up")
CONV = ("conv_a_w", "conv_b_w", "conv_f_w")
REPLICATED = ("mix_norm_g", "b_in", "conv_a_b", "ln_a_g", "ln_a_b", "ffn_norm_g")
KINDS = ("grad", "delta", "m", "v")
FFN_PART = ("w_up", "w_down")
MIX_PART = ("w_in", "w_out")


def _weights_from_gathered(g):
    n_dev, r, c = g.shape
    return g.reshape(n_dev * r, c)


def _slabs_from_full(grad):
    return grad.reshape(N_DEV, grad.shape[0] // N_DEV, grad.shape[1])


def _pair_sums(slabs, core, tag):
    slabs = [s.reshape(N_CHIP, 2, *s.shape[1:]) for s in slabs]
    theirs = _sibling_exchange(slabs, name=f"reduce_siblings_{tag}")
    return [_pair_sum(a, b, core, name=f"pair_sum_{tag}_{q}") for q, (a, b) in enumerate(zip(slabs, theirs))]


def kernel(x, mix_norm_g, w_in, b_in, conv_a_w, conv_a_b, ln_a_g, ln_a_b, conv_b_w, w_out, ffn_norm_g, w_up, conv_f_w, w_down, final_norm_g, loss_target, m_mix_norm_g, m_w_in, m_b_in, m_conv_a_w, m_conv_a_b, m_ln_a_g, m_ln_a_b, m_conv_b_w, m_w_out, m_ffn_norm_g, m_w_up, m_conv_f_w, m_w_down, m_final_norm_g, v_mix_norm_g, v_w_in, v_b_in, v_conv_a_w, v_conv_a_b, v_ln_a_g, v_ln_a_b, v_conv_b_w, v_w_out, v_ffn_norm_g, v_w_up, v_conv_f_w, v_w_down, v_final_norm_g):
    w = dict(mix_norm_g=mix_norm_g, w_in=w_in, b_in=b_in, conv_a_w=conv_a_w, conv_a_b=conv_a_b, ln_a_g=ln_a_g,
             ln_a_b=ln_a_b, conv_b_w=conv_b_w, w_out=w_out, ffn_norm_g=ffn_norm_g, w_up=w_up, conv_f_w=conv_f_w,
             w_down=w_down, final_norm_g=final_norm_g)
    m = dict(mix_norm_g=m_mix_norm_g, w_in=m_w_in, b_in=m_b_in, conv_a_w=m_conv_a_w, conv_a_b=m_conv_a_b,
             ln_a_g=m_ln_a_g, ln_a_b=m_ln_a_b, conv_b_w=m_conv_b_w, w_out=m_w_out, ffn_norm_g=m_ffn_norm_g,
             w_up=m_w_up, conv_f_w=m_conv_f_w, w_down=m_w_down, final_norm_g=m_final_norm_g)
    v = dict(mix_norm_g=v_mix_norm_g, w_in=v_w_in, b_in=v_b_in, conv_a_w=v_conv_a_w, conv_a_b=v_conv_a_b,
             ln_a_g=v_ln_a_g, ln_a_b=v_ln_a_b, conv_b_w=v_conv_b_w, w_out=v_w_out, ffn_norm_g=v_ffn_norm_g,
             w_up=v_w_up, conv_f_w=v_conv_f_w, w_down=v_w_down, final_norm_g=v_final_norm_g)
    order = list(w)
    n_layers = w_in.shape[0]
    n_big = len(BIG)
    xs = x[0]
    target = loss_target[0]
    flip = lambda a: jnp.transpose(a, (0, 2, 1))
    wt, mt, vt = ({n: flip(d[n]) if n in COL_SHARDED else d[n] for n in BIG} for d in (w, m, v))
    px, py, pc = _place()
    core = pc.astype(jnp.int32).reshape(1)
    me = 4 * px + 2 * py + pc
    my_chip = 2 * px + py

    assert BIG == MIX_PART + FFN_PART
    key = lambda n: n + "_t" if n in COL_SHARDED else n
    shard = lambda n, l: wt[n][l].astype(BF16)

    def gather_start(names, l, after, tag):
        shards = [shard(n, l) for n in names]
        lands = [_landing((N_DEV, *s.shape), s.dtype, s, me) for s in shards]
        return _split_start(shards, lands, _gather_plan_first, 4 * len(shards), after, name=f"gather_first_start_{tag}")

    def gather_mid(first, after, tag):
        return _split_wait(first[0], first[1], first[2], first[3], after, _gather_plan_first,
                           name=f"gather_first_wait_{tag}")

    def forward_start(lands, after, tag):
        return _split_start([], lands, _gather_plan_second, 3 * len(lands), after, name=f"gather_second_start_{tag}")

    def forward_finish(second, after, tag):
        return _split_wait(second[0], second[1], [], second[3], after, _gather_plan_second,
                           name=f"gather_second_wait_{tag}")

    gathered = _all_gather([shard(n, 0) for n in MIX_PART] + [w[n] for n in CONV], name="gather_weights_0")
    params = [{n: w[n][l] for n in REPLICATED} for l in range(n_layers)]
    for n, g in zip(CONV, gathered[len(MIX_PART):]):
        n_dev, _, taps, c = g.shape
        full = g.transpose(1, 2, 0, 3).reshape(n_layers, taps, n_dev * c)
        for l in range(n_layers):
            params[l][n] = full[l]
    for n, g in zip(MIX_PART, gathered):
        params[0][key(n)] = _weights_from_gathered(g)
    ffn_first = gather_start(FFN_PART, 0, gathered[0], "0_ffn")
    pending = {}

    h = xs
    saved = []
    for l in range(n_layers):
        nxt = l + 1 if l + 1 < n_layers else None

        def before_up(x1, l=l, nxt=nxt):
            if l == 0:
                second = forward_start(gather_mid(ffn_first, x1, "0_ffn"), x1, "0_ffn")
                after = second[4]
            else:
                second = pending[l]["ffn"]
                after = x1
            if nxt is not None:
                pending[nxt] = dict(first=gather_start(BIG, nxt, after, str(nxt)))
                after = pending[nxt]["first"][4]
            for n, g in zip(FFN_PART, forward_finish(second, after, f"{l}_ffn")):
                params[l][key(n)] = _weights_from_gathered(g)

        h, keep = _layer_fwd(h, params[l], str(l), dep=ffn_first[4] if l == 0 else None, before_up=before_up)
        saved.append(keep)
        if nxt is not None:
            arrived = gather_mid(pending[nxt]["first"], h, str(nxt))
            mix_second = forward_start(arrived[:len(MIX_PART)], h, f"{nxt}_mix")
            pending[nxt]["ffn"] = forward_start(arrived[len(MIX_PART):], mix_second[4], f"{nxt}_ffn")
            for n, g in zip(MIX_PART, forward_finish(mix_second, pending[nxt]["ffn"][4], f"{nxt}_mix")):
                params[nxt][key(n)] = _weights_from_gathered(g)

    def start_reduce(slabs, tag):
        pairs = _pair_sums(slabs, core, tag)
        lands = [_landing(p.shape, p.dtype, lax.dynamic_index_in_dim(p, my_chip, 0, keepdims=False), my_chip)
                 for p in pairs]
        return _split_start(pairs, lands, _chips_plan, 3 * len(pairs), pairs[0], name=f"reduce_chips_start_{tag}")

    def finish_reduce(fly, after, tag):
        return _split_wait(fly[0], fly[1], fly[2], fly[3], after, _chips_plan, name=f"reduce_chips_wait_{tag}")

    loss_sq, dh, dh_b, dgf = _loss_bwd(h, _row(final_norm_g), target, name="loss")
    conv_g = {n: [None] * n_layers for n in CONV}
    rep_g = [None] * n_layers
    flights = {}
    token = None
    for l in reversed(range(n_layers)):
        def after_ffn(g, l=l):
            flights[l, "ffn"] = start_reduce([_slabs_from_full(g[n]) for n in FFN_PART], f"{l}_ffn")
            return flights[l, "ffn"][4]

        def after_mix(g, conv, l=l):
            for n in CONV:
                conv_g[n][l] = conv[n]
            slabs = [_slabs_from_full(g[n]) for n in MIX_PART]
            if l == 0:
                for n in CONV:
                    full = jnp.stack(conv_g[n])
                    _, taps, c = full.shape
                    slabs.append(full.reshape(n_layers, taps, N_DEV, c // N_DEV).transpose(2, 0, 1, 3)
                                 .reshape(N_DEV, n_layers * taps, c // N_DEV))
            flights[l, "mix"] = start_reduce(slabs, f"{l}_mix")
            return flights[l, "mix"][4]

        dh, dh_b, rep_g[l] = _layer_bwd(dh, dh_b, params[l], saved[l], str(l), after_ffn, after_mix, dep=token)
        token = flights[l, "mix"][4]

    sums = {key: finish_reduce(fly, dh, f"{key[0]}_{key[1]}") for key, fly in flights.items() if key != (0, "mix")}
    out = {k: {} for k in KINDS}

    def adamw_big(names, part, dep):
        for q, n in enumerate(names):
            layer_parts = [sums[l, part][q] for l in range(n_layers)]
            res = _adamw_sharded(layer_parts, wt[n], mt[n], vt[n], name=f"adamw_{n}", dep=dep)
            for k, r in zip(KINDS, res):
                out[k][n] = flip(r) if n in COL_SHARDED else r

    adamw_big(FFN_PART, "ffn", token)

    rep_cols = [rep_g[l][n] for l in range(n_layers) for n in REPLICATED] + [dgf, loss_sq]
    rep_all = _all_gather([_fold_partials(rep_cols, name="fold_small")], name="gather_small")[0]
    with_final = lambda d: {**{n: d[n] for n in REPLICATED}, "final_norm_g": _row(d["final_norm_g"])}
    loss, rep_res = _adamw_replicated(rep_all, REPLICATED, with_final(w), with_final(m), with_final(v),
                                      loss_sq.shape[1], name="adamw_small")
    for n, res in rep_res.items():
        for k, r in zip(KINDS, res):
            out[k][n] = r.reshape(w[n].shape)

    last = finish_reduce(flights[0, "mix"], rep_res["b_in"][0], "0_mix")
    sums[0, "mix"] = last[:len(MIX_PART)]
    adamw_big(MIX_PART, "mix", None)
    for n, p in zip(CONV, last[len(MIX_PART):]):
        as_one = lambda a: a.reshape(1, *p.shape[1:])
        for k, r in zip(KINDS, _adamw_sharded([p], as_one(w[n]), as_one(m[n]), as_one(v[n]), name=f"adamw_{n}")):
            out[k][n] = r.reshape(w[n].shape)

    grad_x = dh.reshape(x.shape)
    return (loss.reshape(()), grad_x, *[out["grad"][n] for n in order], *[out["delta"][n] for n in order],
            *[out["m"][n] for n in order], *[out["v"][n] for n in order])
```

```python
import functools

import jax
import jax.numpy as jnp
from jax import lax
from jax.experimental import pallas as pl
from jax.experimental.pallas import tpu as pltpu

F32 = jnp.float32
BF16 = jnp.bfloat16

N_DEV = 8
N_CHIP = 4
D_CONF = 512
CONF_K = 31
SHORT_K = 3
EPS = 1e-6
HALO = 32
HALO3 = 8
HALO3_BLK = 16
LANES = 128
SUB = 8
VMEM_LIMIT = 56 * 1024 * 1024

ADAM_LR = 0.001
ADAM_B1 = 0.9
ADAM_B2 = 0.999
ADAM_EPS = 1e-08
ADAM_WD = 0.01
ADAM_STEP = 10

MESH = pl.DeviceIdType.MESH
ANY = pl.BlockSpec(memory_space=pl.ANY)


def _params(*sem):
    return pltpu.CompilerParams(dimension_semantics=sem, vmem_limit_bytes=VMEM_LIMIT)


def _resident(shape, index_map):
    return pl.BlockSpec(shape, index_map, pipeline_mode=pl.Buffered(1))


def _row_loop(n_rows, rb, fn, unroll=1):
    rb = min(rb, n_rows)

    def body(i, carry):
        fn(pl.ds(pl.multiple_of(i * rb, rb), rb))
        return carry
    lax.fori_loop(0, n_rows // rb, body, 0, unroll=unroll)


def _rows8(v):
    acc = v[0:SUB]
    for k in range(1, v.shape[0] // SUB):
        acc = acc + v[k * SUB:(k + 1) * SUB]
    return acc


def _sigmoid(z):
    return 0.5 * jnp.tanh(0.5 * z) + 0.5


def _dot(a, b):
    return jnp.dot(a, b, preferred_element_type=F32)


def _dot_nt(a, b):
    return lax.dot_general(a, b, (((1,), (1,)), ((), ())), preferred_element_type=F32)


def _dot_tn(a, b):
    return lax.dot_general(a, b, (((0,), (0,)), ((), ())), preferred_element_type=F32)


def _replicate_taps(w_ref, wrep, taps):
    for k in range(taps):
        wrep[pl.ds(k * SUB, SUB), :] = jnp.broadcast_to(w_ref[pl.ds(k, 1), :], (SUB, w_ref.shape[1]))


def _shift_copies(win, shf, lanes):
    span = win.shape[0] - SUB
    for r in range(1, SUB):
        for j0 in range(0, span, 64):
            n = min(64, span - j0)
            shf[r - 1, pl.ds(j0, n), lanes] = win[pl.ds(j0 + r, n), lanes]


def _rows_at(win, shf, off, rb, lanes):
    if shf is None or off % SUB == 0:
        return win[pl.ds(off, rb), lanes]
    return shf[off % SUB - 1, pl.ds(off - off % SUB, rb), lanes]


def _conv_taps(win, wrep, out, *, taps, n_rows, base, width, transposed=False, bias_ref=None, shf=None):
    rb = min(64, n_rows)

    def lane_body(cb, carry):
        lanes = pl.ds(pl.multiple_of(cb * LANES, LANES), LANES)
        if shf is not None:
            _shift_copies(win, shf, lanes)
        for r0 in range(0, n_rows, rb):
            acc = None
            for k in range(taps):
                off = (taps - 1 - k) if transposed else (k - (taps - 1))
                wk = jnp.tile(wrep[pl.ds(k * SUB, SUB), lanes], (rb // SUB, 1))
                term = wk * _rows_at(win, shf, base + r0 + off, rb, lanes)
                acc = term if acc is None else acc + term
            if bias_ref is not None:
                acc = acc + bias_ref[:, lanes]
            out[pl.ds(r0, rb), lanes] = acc.astype(out.dtype)
        return carry

    lax.fori_loop(0, width // LANES, lane_body, 0)


def _conv_bwd_taps(win, wrep, x_cur, dx_out, dw_acc, *, taps, n_rows, width, shf=None):
    rb = min(32 if taps > 8 else 64, n_rows)

    def lane_body(cb, carry):
        lanes = pl.ds(pl.multiple_of(cb * LANES, LANES), LANES)
        if shf is not None:
            _shift_copies(win, shf, lanes)
        sums = [None] * taps
        for r0 in range(0, n_rows, rb):
            xv = x_cur[pl.ds(r0, rb), lanes].astype(F32)
            acc = None
            for k in range(taps):
                shifted = _rows_at(win, shf, r0 + taps - 1 - k, rb, lanes)
                term = jnp.tile(wrep[pl.ds(k * SUB, SUB), lanes], (rb // SUB, 1)) * shifted
                acc = term if acc is None else acc + term
                part = _rows8(xv * shifted)
                sums[k] = part if sums[k] is None else sums[k] + part
            dx_out[pl.ds(r0, rb), lanes] = acc.astype(dx_out.dtype)
        for k in range(taps):
            dw_acc[pl.ds(k * SUB, SUB), lanes] += sums[k]
        return carry

    lax.fori_loop(0, width // LANES, lane_body, 0)


def _fold8(acc_ref, taps):
    return jnp.concatenate(
        [jnp.sum(acc_ref[pl.ds(k * SUB, SUB), :], axis=0, keepdims=True) for k in range(taps)], axis=0)


def _seq_tile(s_len):
    return min(512, s_len)


def _mm_tile(s_len):
    return min(512, s_len)


def _ff_chunk(ff):
    best = LANES
    for c in range(LANES, 1408 + 1, LANES):
        if ff % c == 0:
            best = c
    return best


def _col_tile(n):
    for c in (512, 1408, 256, LANES):
        if n % c == 0:
            return c
    return n


def _rms_matmul(x, g, wt, b, *, name, dep=None):
    s_len, d = x.shape
    n = wt.shape[0]
    tm = _mm_tile(s_len)
    cn = _col_tile(n)
    has_bias = b is not None

    def body(*refs):
        x_ref, g_ref, w_ref = refs[0:3]
        b_ref = refs[3] if has_bias else None
        o_ref, h_ref = refs[-2:]

        def blk(rows):
            xv = x_ref[rows, :]
            r = lax.rsqrt(jnp.mean(xv * xv, axis=-1, keepdims=True) + EPS)
            h_ref[rows, :] = ((xv * r) * g_ref[...]).astype(BF16)

        rb = min(128, tm)
        for r0 in range(0, tm, rb):
            blk(pl.ds(r0, rb))
        for j in range(n // cn):
            acc = _dot_nt(h_ref[...], w_ref[j * cn:(j + 1) * cn, :])
            if has_bias:
                acc = acc + b_ref[:, j * cn:(j + 1) * cn]
            o_ref[:, j * cn:(j + 1) * cn] = acc.astype(BF16)

    in_specs = [pl.BlockSpec((tm, d), lambda i: (i, 0)), _resident((1, d), lambda i: (0, 0)),
                _resident((n, d), lambda i: (0, 0))]
    args = [x, g, wt]
    if has_bias:
        in_specs.append(_resident((1, n), lambda i: (0, 0)))
        args.append(b)
    in_specs.append(ANY)
    args.append(x if dep is None else dep)
    return pl.pallas_call(
        body, grid=(s_len // tm,), in_specs=in_specs,
        out_specs=[pl.BlockSpec((tm, n), lambda i: (i, 0)), pl.BlockSpec((tm, d), lambda i: (i, 0))],
        out_shape=[jax.ShapeDtypeStruct((s_len, n), BF16), jax.ShapeDtypeStruct((s_len, d), BF16)],
        compiler_params=_params("parallel"), name=name,
    )(*args)


def _mix_windows(u_ref, uh_ref, gw, pw, first, t):
    c = D_CONF
    uh = uh_ref[...].astype(F32)
    gw[0:HALO, :] = jnp.where(first, 0.0, uh[:, 0:c] * _sigmoid(uh[:, c:2 * c]))
    pw[0:HALO3, :] = jnp.where(first, 0.0, uh[HALO - HALO3:HALO, 3 * c:4 * c] * uh[HALO - HALO3:HALO, 4 * c:5 * c])

    def blk(rows):
        dst = pl.ds(pl.multiple_of(rows.start + HALO, SUB), rows.size)
        gw[dst, :] = u_ref[rows, 0:c].astype(F32) * _sigmoid(u_ref[rows, c:2 * c].astype(F32))
        dst3 = pl.ds(pl.multiple_of(rows.start + HALO3, SUB), rows.size)
        pw[dst3, :] = u_ref[rows, 3 * c:4 * c].astype(F32) * u_ref[rows, 4 * c:5 * c].astype(F32)
    _row_loop(t, 64, blk)


def _mix_fwd(u, x0, wa, ba, lg, lb, wb, w_out, *, name):
    s_len, d_in = u.shape
    d = x0.shape[1]
    c = D_CONF
    t = _seq_tile(s_len)
    per = t // HALO

    def body(u_ref, uh_ref, x0_ref, wa_ref, ba_ref, lg_ref, lb_ref, wb_ref, wo_ref, y_ref, x1_ref, ca, cb,
             gw, pw, wrep_a, wrep_b, shf):
        first = pl.program_id(0) == 0
        _mix_windows(u_ref, uh_ref, gw, pw, first, t)
        _replicate_taps(wa_ref, wrep_a, CONF_K)
        _replicate_taps(wb_ref, wrep_b, SHORT_K)
        _conv_taps(gw, wrep_a, ca, taps=CONF_K, n_rows=t, base=HALO, width=c, bias_ref=ba_ref, shf=shf)
        _conv_taps(pw, wrep_b, cb, taps=SHORT_K, n_rows=t, base=HALO3, width=c)

        def blk(rows):
            cv = ca[rows, :]
            mu = jnp.mean(cv, axis=-1, keepdims=True)
            xc = cv - mu
            var = jnp.mean(xc * xc, axis=-1, keepdims=True)
            ln = (xc * lax.rsqrt(var + EPS)) * lg_ref[...] + lb_ref[...]
            y_ref[rows, 0:c] = (ln * _sigmoid(ln)).astype(BF16)
            y_ref[rows, c:2 * c] = (u_ref[rows, 2 * c:3 * c].astype(F32) * cb[rows, :]).astype(BF16)
        _row_loop(t, 64, blk)
        x1_ref[...] = x0_ref[...] + _dot(y_ref[...], wo_ref[...])

    small = lambda r: _resident((r, c), lambda i: (0, 0))
    return pl.pallas_call(
        body, grid=(s_len // t,),
        in_specs=[pl.BlockSpec((t, d_in), lambda i: (i, 0)),
                  pl.BlockSpec((HALO, d_in), lambda i: (jnp.maximum(i * per - 1, 0), 0)),
                  pl.BlockSpec((t, d), lambda i: (i, 0)),
                  small(CONF_K), small(1), small(1), small(1), small(SHORT_K),
                  _resident((2 * c, d), lambda i: (0, 0))],
        out_specs=[pl.BlockSpec((t, 2 * c), lambda i: (i, 0)), pl.BlockSpec((t, d), lambda i: (i, 0)),
                   pl.BlockSpec((t, c), lambda i: (i, 0)), pl.BlockSpec((t, c), lambda i: (i, 0))],
        out_shape=[jax.ShapeDtypeStruct((s_len, 2 * c), BF16), jax.ShapeDtypeStruct((s_len, d), F32),
                   jax.ShapeDtypeStruct((s_len, c), F32), jax.ShapeDtypeStruct((s_len, c), F32)],
        scratch_shapes=[pltpu.VMEM((HALO + t, c), F32), pltpu.VMEM((HALO3 + t, c), F32),
                        pltpu.VMEM((CONF_K * SUB, c), F32), pltpu.VMEM((SHORT_K * SUB, c), F32),
                        pltpu.VMEM((SUB - 1, HALO + t, c), F32)],
        compiler_params=_params("arbitrary"), name=name,
    )(u, u, x0, wa, ba, lg, lb, wb, w_out)


def _ffn_windows(ug_ref, ugh_ref, uv_ref, uvh_ref, gwin, vwin, first, t):
    lo = HALO3_BLK - HALO3
    gwin[0:HALO3, :] = jnp.where(first, 0.0, ugh_ref[...].astype(F32)[lo:HALO3_BLK])
    vwin[0:HALO3, :] = jnp.where(first, 0.0, uvh_ref[...].astype(F32)[lo:HALO3_BLK])

    def blk(rows):
        dst = pl.ds(pl.multiple_of(rows.start + HALO3, SUB), rows.size)
        gwin[dst, :] = ug_ref[rows, :].astype(F32)
        vwin[dst, :] = uv_ref[rows, :].astype(F32)
    _row_loop(t, 64, blk)


def _ffn_fwd(uf, x1, wf, w_down, *, name, dep=None):
    s_len, ff2 = uf.shape
    ff = ff2 // 2
    d = x1.shape[1]
    t = _seq_tile(s_len)
    fc = _ff_chunk(ff)
    nc = ff // fc
    per = t // HALO3_BLK

    def body(ug_ref, ugh_ref, uv_ref, uvh_ref, x1_ref, wfg_ref, wfv_ref, wd_ref, dep_ref,
             act_ref, x2_ref, cg_ref, cv_ref, gwin, vwin, cg, cv, wrep_g, wrep_v):
        first = pl.program_id(0) == 0
        _ffn_windows(ug_ref, ugh_ref, uv_ref, uvh_ref, gwin, vwin, first, t)
        _replicate_taps(wfg_ref, wrep_g, SHORT_K)
        _replicate_taps(wfv_ref, wrep_v, SHORT_K)
        _conv_taps(gwin, wrep_g, cg, taps=SHORT_K, n_rows=t, base=HALO3, width=fc)
        _conv_taps(vwin, wrep_v, cv, taps=SHORT_K, n_rows=t, base=HALO3, width=fc)

        def blk(rows):
            gv = cg[rows, :]
            vv = cv[rows, :]
            cg_ref[rows, :] = gv.astype(BF16)
            cv_ref[rows, :] = vv.astype(BF16)
            act_ref[rows, :] = ((gv * _sigmoid(gv)) * vv).astype(BF16)
        _row_loop(t, 32, blk, unroll=2)

        @pl.when(pl.program_id(1) == 0)
        def _():
            x2_ref[...] = x1_ref[...]
        x2_ref[...] += _dot(act_ref[...], wd_ref[...])

    halo_map = lambda off: (lambda i, j: (jnp.maximum(i * per - 1, 0), j + off))
    return pl.pallas_call(
        body, grid=(s_len // t, nc),
        in_specs=[pl.BlockSpec((t, fc), lambda i, j: (i, j)), pl.BlockSpec((HALO3_BLK, fc), halo_map(0)),
                  pl.BlockSpec((t, fc), lambda i, j: (i, j + nc)), pl.BlockSpec((HALO3_BLK, fc), halo_map(nc)),
                  pl.BlockSpec((t, d), lambda i, j: (i, 0)),
                  pl.BlockSpec((SHORT_K, fc), lambda i, j: (0, j)),
                  pl.BlockSpec((SHORT_K, fc), lambda i, j: (0, j + nc)),
                  pl.BlockSpec((fc, d), lambda i, j: (j, 0)), ANY],
        out_specs=[pl.BlockSpec((t, fc), lambda i, j: (i, j)), pl.BlockSpec((t, d), lambda i, j: (i, 0)),
                   pl.BlockSpec((t, fc), lambda i, j: (i, j)), pl.BlockSpec((t, fc), lambda i, j: (i, j))],
        out_shape=[jax.ShapeDtypeStruct((s_len, ff), BF16), jax.ShapeDtypeStruct((s_len, d), F32),
                   jax.ShapeDtypeStruct((s_len, ff), BF16), jax.ShapeDtypeStruct((s_len, ff), BF16)],
        scratch_shapes=[pltpu.VMEM((HALO3 + t, fc), F32), pltpu.VMEM((HALO3 + t, fc), F32),
                        pltpu.VMEM((t, fc), F32), pltpu.VMEM((t, fc), F32),
                        pltpu.VMEM((SHORT_K * SUB, fc), F32), pltpu.VMEM((SHORT_K * SUB, fc), F32)],
        compiler_params=_params("parallel", "arbitrary"), name=name,
    )(uf, uf, uf, uf, x1, wf, wf, w_down, uf if dep is None else dep)


def _loss_bwd(x, g, target, *, name):
    s_len, d = x.shape
    t = _seq_tile(s_len)

    def body(x_ref, g_ref, t_ref, l_ref, dx_ref, dxb_ref, dg_ref):
        @pl.when(pl.program_id(0) == 0)
        def _():
            l_ref[...] = jnp.zeros_like(l_ref)
            dg_ref[...] = jnp.zeros_like(dg_ref)

        def blk(rows):
            xv = x_ref[rows, :]
            r = lax.rsqrt(jnp.mean(xv * xv, axis=-1, keepdims=True) + EPS)
            xn = xv * r
            e = xn * g_ref[...] - t_ref[rows, :]
            l_ref[...] += _rows8(e * e)
            dy = e * (1.0 / d)
            dg_ref[...] += _rows8(dy * xn)
            dn = dy * g_ref[...]
            dx = r * (dn - xn * jnp.mean(dn * xn, axis=-1, keepdims=True))
            dx_ref[rows, :] = dx
            dxb_ref[rows, :] = dx.astype(BF16)
        _row_loop(t, 64, blk)

    row = pl.BlockSpec((t, d), lambda i: (i, 0))
    part = pl.BlockSpec((SUB, d), lambda i: (0, 0))
    return pl.pallas_call(
        body, grid=(s_len // t,),
        in_specs=[row, _resident((1, d), lambda i: (0, 0)), row],
        out_specs=[part, row, row, part],
        out_shape=[jax.ShapeDtypeStruct((SUB, d), F32), jax.ShapeDtypeStruct((s_len, d), F32),
                   jax.ShapeDtypeStruct((s_len, d), BF16), jax.ShapeDtypeStruct((SUB, d), F32)],
        compiler_params=_params("arbitrary"), name=name,
    )(x, g, target)


def _ffn_bwd(dx2, uf, cg, cv, wf, w_down, *, name, dep=None):
    s_len, ff2 = uf.shape
    ff = ff2 // 2
    d = dx2.shape[1]
    t = _seq_tile(s_len)
    n_t = s_len // t
    fc = _ff_chunk(ff)
    nc = ff // fc

    def body(dx_ref, ug_ref, uv_ref, cg_ref, cv_ref, wfg_ref, wfv_ref, wd_ref, dep_ref,
             dug_ref, duv_ref, dwg_ref, dwv_ref, dact, dgw, dvw, awg, awv, wrep_g, wrep_v):
        i = pl.program_id(1)

        @pl.when(i == 0)
        def _():
            dgw[t:t + HALO3, :] = jnp.zeros((HALO3, fc), F32)
            dvw[t:t + HALO3, :] = jnp.zeros((HALO3, fc), F32)
            awg[...] = jnp.zeros_like(awg)
            awv[...] = jnp.zeros_like(awv)

        _replicate_taps(wfg_ref, wrep_g, SHORT_K)
        _replicate_taps(wfv_ref, wrep_v, SHORT_K)

        def blk(rows):
            gv = cg_ref[rows, :].astype(F32)
            sg = _sigmoid(gv)
            da = dact[rows, :]
            dgw[rows, :] = (da * cv_ref[rows, :].astype(F32)) * (sg * (1.0 + gv * (1.0 - sg)))
            dvw[rows, :] = da * (gv * sg)

        dact[...] = _dot_nt(dx_ref[...], wd_ref[...])
        _row_loop(t, 32, blk, unroll=2)

        _conv_bwd_taps(dgw, wrep_g, ug_ref, dug_ref, awg, taps=SHORT_K, n_rows=t, width=fc)
        _conv_bwd_taps(dvw, wrep_v, uv_ref, duv_ref, awv, taps=SHORT_K, n_rows=t, width=fc)
        dgw[t:t + HALO3, :] = dgw[0:HALO3, :]
        dvw[t:t + HALO3, :] = dvw[0:HALO3, :]

        @pl.when(i == n_t - 1)
        def _():
            dwg_ref[...] = _fold8(awg, SHORT_K)
            dwv_ref[...] = _fold8(awv, SHORT_K)

    rev = lambda i: n_t - 1 - i
    gate = pl.BlockSpec((t, fc), lambda j, i: (rev(i), j))
    value = pl.BlockSpec((t, fc), lambda j, i: (rev(i), j + nc))
    return pl.pallas_call(
        body, grid=(nc, n_t),
        in_specs=[pl.BlockSpec((t, d), lambda j, i: (rev(i), 0)), gate, value, gate, gate,
                  pl.BlockSpec((SHORT_K, fc), lambda j, i: (0, j)),
                  pl.BlockSpec((SHORT_K, fc), lambda j, i: (0, j + nc)),
                  pl.BlockSpec((fc, d), lambda j, i: (j, 0)), ANY],
        out_specs=[gate, gate,
                   pl.BlockSpec((SHORT_K, fc), lambda j, i: (0, j)), pl.BlockSpec((SHORT_K, fc), lambda j, i: (0, j))],
        out_shape=[jax.ShapeDtypeStruct((s_len, ff), BF16), jax.ShapeDtypeStruct((s_len, ff), BF16),
                   jax.ShapeDtypeStruct((SHORT_K, ff), F32), jax.ShapeDtypeStruct((SHORT_K, ff), F32)],
        scratch_shapes=[pltpu.VMEM((t, fc), F32),
                        pltpu.VMEM((t + HALO3, fc), F32), pltpu.VMEM((t + HALO3, fc), F32),
                        pltpu.VMEM((SHORT_K * SUB, fc), F32), pltpu.VMEM((SHORT_K * SUB, fc), F32),
                        pltpu.VMEM((SHORT_K * SUB, fc), F32), pltpu.VMEM((SHORT_K * SUB, fc), F32)],
        compiler_params=_params("arbitrary", "arbitrary"), name=name,
    )(dx2, uf, uf, cg, cv, wf, wf, w_down, uf if dep is None else dep)


def _mix_bwd(dx1, u, ca, cb, wa, lg, lb, wb, w_out, *, name):
    s_len, d_in = u.shape
    d = dx1.shape[1]
    c = D_CONF
    t = _seq_tile(s_len)
    n_t = s_len // t

    def body(dx_ref, u_ref, ca_ref, cb_ref, wa_ref, lg_ref, lb_ref, wb_ref, wo_ref,
             du_ref, dwa_ref, dwb_ref, dba_ref, dlg_ref, dlb_ref, dbin_ref,
             glu, prod, dyc, dcaw, dcbw, dglu, dp, awa, awb, wrep_a, wrep_b, shf):
        i = pl.program_id(0)
        dyc[...] = _dot_nt(dx_ref[...], wo_ref[...])
        _replicate_taps(wa_ref, wrep_a, CONF_K)
        _replicate_taps(wb_ref, wrep_b, SHORT_K)

        @pl.when(i == 0)
        def _():
            dcaw[t:t + HALO, :] = jnp.zeros((HALO, c), F32)
            dcbw[t:t + HALO3, :] = jnp.zeros((HALO3, c), F32)
            awa[...] = jnp.zeros_like(awa)
            awb[...] = jnp.zeros_like(awb)
            dba_ref[...] = jnp.zeros_like(dba_ref)
            dlg_ref[...] = jnp.zeros_like(dlg_ref)
            dlb_ref[...] = jnp.zeros_like(dlb_ref)
            dbin_ref[...] = jnp.zeros_like(dbin_ref)

        def blk1(rows):
            cv = ca_ref[rows, :]
            mu = jnp.mean(cv, axis=-1, keepdims=True)
            xc = cv - mu
            rstd = lax.rsqrt(jnp.mean(xc * xc, axis=-1, keepdims=True) + EPS)
            nrm = xc * rstd
            ln = nrm * lg_ref[...] + lb_ref[...]
            sg = _sigmoid(ln)
            dln = dyc[rows, 0:c] * (sg * (1.0 + ln * (1.0 - sg)))
            dlg_ref[...] += _rows8(dln * nrm)
            dlb_ref[...] += _rows8(dln)
            dn = dln * lg_ref[...]
            dca = rstd * (dn - jnp.mean(dn, axis=-1, keepdims=True)
                          - nrm * jnp.mean(dn * nrm, axis=-1, keepdims=True))
            dcaw[rows, :] = dca
            dba_ref[...] += _rows8(dca)
            ds = dyc[rows, c:2 * c]
            dgb = ds * cb_ref[rows, :]
            dcbw[rows, :] = ds * u_ref[rows, 2 * c:3 * c].astype(F32)
            du_ref[rows, 2 * c:3 * c] = dgb.astype(BF16)
            dbin_ref[:, 2 * c:3 * c] += _rows8(dgb)
            glu[rows, :] = u_ref[rows, 0:c].astype(F32) * _sigmoid(u_ref[rows, c:2 * c].astype(F32))
            prod[rows, :] = u_ref[rows, 3 * c:4 * c].astype(F32) * u_ref[rows, 4 * c:5 * c].astype(F32)
        _row_loop(t, 64, blk1, unroll=2)

        _conv_bwd_taps(dcaw, wrep_a, glu, dglu, awa, taps=CONF_K, n_rows=t, width=c, shf=shf)
        _conv_bwd_taps(dcbw, wrep_b, prod, dp, awb, taps=SHORT_K, n_rows=t, width=c)
        dcaw[t:t + HALO, :] = dcaw[0:HALO, :]
        dcbw[t:t + HALO3, :] = dcbw[0:HALO3, :]

        def blk2(rows):
            av = u_ref[rows, 0:c].astype(F32)
            sg = _sigmoid(u_ref[rows, c:2 * c].astype(F32))
            dg = dglu[rows, :]
            d_av = dg * sg
            d_ag = (dg * av) * (sg * (1.0 - sg))
            dpv = dp[rows, :]
            d_gc = dpv * u_ref[rows, 4 * c:5 * c].astype(F32)
            d_vs = dpv * u_ref[rows, 3 * c:4 * c].astype(F32)
            du_ref[rows, 0:c] = d_av.astype(BF16)
            du_ref[rows, c:2 * c] = d_ag.astype(BF16)
            du_ref[rows, 3 * c:4 * c] = d_gc.astype(BF16)
            du_ref[rows, 4 * c:5 * c] = d_vs.astype(BF16)
            dbin_ref[:, 0:c] += _rows8(d_av)
            dbin_ref[:, c:2 * c] += _rows8(d_ag)
            dbin_ref[:, 3 * c:4 * c] += _rows8(d_gc)
            dbin_ref[:, 4 * c:5 * c] += _rows8(d_vs)
        _row_loop(t, 64, blk2)

        @pl.when(i == n_t - 1)
        def _():
            dwa_ref[...] = _fold8(awa, CONF_K)
            dwb_ref[...] = _fold8(awb, SHORT_K)

    rev = lambda i: n_t - 1 - i
    small_in = lambda r: _resident((r, c), lambda i: (0, 0))
    small = lambda r: pl.BlockSpec((r, c), lambda i: (0, 0))
    return pl.pallas_call(
        body, grid=(n_t,),
        in_specs=[pl.BlockSpec((t, d), lambda i: (rev(i), 0)),
                  pl.BlockSpec((t, d_in), lambda i: (rev(i), 0)),
                  pl.BlockSpec((t, c), lambda i: (rev(i), 0)), pl.BlockSpec((t, c), lambda i: (rev(i), 0)),
                  small_in(CONF_K), small_in(1), small_in(1), small_in(SHORT_K),
                  _resident((2 * c, d), lambda i: (0, 0))],
        out_specs=[pl.BlockSpec((t, d_in), lambda i: (rev(i), 0)),
                   small(CONF_K), small(SHORT_K), small(SUB), small(SUB), small(SUB),
                   pl.BlockSpec((SUB, d_in), lambda i: (0, 0))],
        out_shape=[jax.ShapeDtypeStruct((s_len, d_in), BF16),
                   jax.ShapeDtypeStruct((CONF_K, c), F32), jax.ShapeDtypeStruct((SHORT_K, c), F32),
                   jax.ShapeDtypeStruct((SUB, c), F32), jax.ShapeDtypeStruct((SUB, c), F32),
                   jax.ShapeDtypeStruct((SUB, c), F32), jax.ShapeDtypeStruct((SUB, d_in), F32)],
        scratch_shapes=[pltpu.VMEM((t, c), F32), pltpu.VMEM((t, c), F32), pltpu.VMEM((t, 2 * c), F32),
                        pltpu.VMEM((t + HALO, c), F32), pltpu.VMEM((t + HALO3, c), F32),
                        pltpu.VMEM((t, c), F32), pltpu.VMEM((t, c), F32),
                        pltpu.VMEM((CONF_K * SUB, c), F32), pltpu.VMEM((SHORT_K * SUB, c), F32),
                        pltpu.VMEM((CONF_K * SUB, c), F32), pltpu.VMEM((SHORT_K * SUB, c), F32),
                        pltpu.VMEM((SUB - 1, t + HALO, c), F32)],
        compiler_params=_params("arbitrary"), name=name,
    )(dx1, u, ca, cb, wa, lg, lb, wb, w_out)


def _matmul_tn(a, b, *, name, into=None, part=0, n_parts=1):
    s_len, k = a.shape
    n = b.shape[1]
    tk = _col_tile(k)
    per = k // tk

    def body(*refs):
        a_ref, b_ref = refs[0], refs[1]
        o_ref = refs[-1]
        o_ref[...] = _dot_tn(a_ref[...], b_ref[...]).astype(BF16)

    in_specs = [pl.BlockSpec((s_len, tk), lambda j: (0, j)), _resident((s_len, n), lambda j: (0, 0))]
    args = [a, b]
    aliases = {}
    if into is not None:
        in_specs.append(ANY)
        args.append(into)
        aliases = {2: 0}
    return pl.pallas_call(
        body, grid=(per,), in_specs=in_specs,
        out_specs=pl.BlockSpec((tk, n), lambda j: (part * per + j, 0)),
        out_shape=jax.ShapeDtypeStruct((n_parts * k, n), BF16),
        input_output_aliases=aliases,
        compiler_params=_params("parallel"), name=name,
    )(*args)


def _matmul_rmsbwd(dzs, wt, x, g, dx_in, *, name, dep=None):
    s_len, d = x.shape
    n_z = len(dzs)
    nj = dzs[0].shape[1]
    t = _mm_tile(s_len)

    def body(*refs):
        dz_refs = refs[0:n_z]
        w_refs = refs[n_z:2 * n_z]
        x_ref, g_ref, dxi_ref, _, dx_ref, dxb_ref, dg_ref, dh = refs[2 * n_z:]

        @pl.when(pl.program_id(0) == 0)
        def _():
            dg_ref[...] = jnp.zeros_like(dg_ref)

        def blk(rows):
            xv = x_ref[rows, :]
            r = lax.rsqrt(jnp.mean(xv * xv, axis=-1, keepdims=True) + EPS)
            xn = xv * r
            dhv = dh[rows, :]
            dg_ref[...] += _rows8(dhv * xn)
            dn = dhv * g_ref[...]
            dx = dxi_ref[rows, :] + r * (dn - xn * jnp.mean(dn * xn, axis=-1, keepdims=True))
            dx_ref[rows, :] = dx
            dxb_ref[rows, :] = dx.astype(BF16)

        half = t // 2
        rb = min(128, half)
        for lo in range(0, t, half):
            acc = _dot(dz_refs[0][lo:lo + half, :], w_refs[0][...])
            for q in range(1, n_z):
                acc = acc + _dot(dz_refs[q][lo:lo + half, :], w_refs[q][...])
            dh[lo:lo + half, :] = acc
            for r0 in range(lo, lo + half, rb):
                blk(pl.ds(r0, rb))

    row = pl.BlockSpec((t, d), lambda i: (i, 0))
    in_specs = [pl.BlockSpec((t, nj), lambda i: (i, 0)) for _ in range(n_z)]
    in_specs += [_resident((nj, d), functools.partial(lambda q, i: (q, 0), q)) for q in range(n_z)]
    in_specs += [row, _resident((1, d), lambda i: (0, 0)), row, ANY]
    return pl.pallas_call(
        body, grid=(s_len // t,), in_specs=in_specs,
        out_specs=[row, row, pl.BlockSpec((SUB, d), lambda i: (0, 0))],
        out_shape=[jax.ShapeDtypeStruct((s_len, d), F32), jax.ShapeDtypeStruct((s_len, d), BF16),
                   jax.ShapeDtypeStruct((SUB, d), F32)],
        scratch_shapes=[pltpu.VMEM((t, d), F32)],
        compiler_params=_params("arbitrary"), name=name,
    )(*dzs, *([wt] * n_z), x, g, dx_in, x if dep is None else dep)


def _row(v):
    return v.reshape(1, -1)


def _layer_fwd(x0, p, tag, dep=None, before_up=None):
    u, h1 = _rms_matmul(x0, _row(p["mix_norm_g"]), p["w_in_t"], _row(p["b_in"]), name=f"in_proj_{tag}", dep=dep)
    ycat, x1, ca, cb = _mix_fwd(u, x0, p["conv_a_w"], _row(p["conv_a_b"]), _row(p["ln_a_g"]), _row(p["ln_a_b"]),
                            p["conv_b_w"], p["w_out"], name=f"mix_fwd_{tag}")
    if before_up is not None:
        before_up(x1)
    uf, h2 = _rms_matmul(x1, _row(p["ffn_norm_g"]), p["w_up_t"], None, name=f"up_proj_{tag}")
    act, x2, cg, cv = _ffn_fwd(uf, x1, p["conv_f_w"], p["w_down"], name=f"ffn_fwd_{tag}")
    return x2, dict(x0=x0, h1=h1, u=u, ca=ca, cb=cb, ycat=ycat, x1=x1, h2=h2, uf=uf, cg=cg, cv=cv, act=act)


def _layer_bwd(dx2, dx2_b, p, saved, tag, after_ffn, after_mix, dep=None):
    dug, duv, dwf_g, dwf_v = _ffn_bwd(dx2_b, saved["uf"], saved["cg"], saved["cv"], p["conv_f_w"], p["w_down"],
                                      name=f"ffn_bwd_{tag}", dep=dep)
    g_down = _matmul_tn(saved["act"], dx2_b, name=f"dw_down_{tag}")
    g_up = _matmul_tn(dug, saved["h2"], name=f"dw_up_g_{tag}", n_parts=2)
    g_up = _matmul_tn(duv, saved["h2"], name=f"dw_up_v_{tag}", into=g_up, part=1, n_parts=2)
    dep_ffn = after_ffn(dict(w_up=g_up, w_down=g_down))
    dx1, dx1_b, dg2 = _matmul_rmsbwd([dug, duv], p["w_up_t"], saved["x1"], _row(p["ffn_norm_g"]), dx2,
                                     name=f"dh_ffn_{tag}", dep=dep_ffn)
    du, dwa, dwb, dba, dlg, dlb, dbin = _mix_bwd(
        dx1_b, saved["u"], saved["ca"], saved["cb"], p["conv_a_w"], _row(p["ln_a_g"]), _row(p["ln_a_b"]),
        p["conv_b_w"], p["w_out"], name=f"mix_bwd_{tag}")
    g_out = _matmul_tn(saved["ycat"], dx1_b, name=f"dw_out_{tag}")
    g_in = _matmul_tn(du, saved["h1"], name=f"dw_in_{tag}")
    conv = dict(conv_a_w=dwa, conv_b_w=dwb, conv_f_w=jnp.concatenate([dwf_g, dwf_v], axis=1))
    dep_mix = after_mix(dict(w_in=g_in, w_out=g_out), conv)
    dx0, dx0_b, dg1 = _matmul_rmsbwd([du], p["w_in_t"], saved["x0"], _row(p["mix_norm_g"]), dx1,
                                     name=f"dh_mix_{tag}", dep=dep_mix)
    rep = dict(mix_norm_g=dg1, b_in=dbin, conv_a_b=dba, ln_a_g=dlg, ln_a_b=dlb, ffn_norm_g=dg2)
    return dx0, dx0_b, rep


def _place():
    return lax.axis_index("x"), lax.axis_index("y"), lax.axis_index("c")


def _all_gather(arrs, *, name):
    n_a = len(arrs)

    def body(*refs):
        ins = refs[0:n_a]
        outs = refs[n_a:2 * n_a]
        send_sems, recv_sems, local_sems = refs[2 * n_a:]
        x, y, c = _place()
        sibling = (x, y, 1 - c)
        chips = [(1 - x, y), (x, 1 - y), (1 - x, 1 - y)]

        def slot(a, px, py, pc):
            return outs[a].at[4 * px + 2 * py + pc]

        def copy(a, k, block, to, src=None):
            return pltpu.make_async_remote_copy(
                src_ref=slot(a, *block) if src is None else src, dst_ref=slot(a, *block),
                send_sem=send_sems.at[a, k], recv_sem=recv_sems.at[a, k],
                device_id=to, device_id_type=MESH)

        me = (x, y, c)
        mine = [pltpu.make_async_copy(ins[a], slot(a, *me), local_sems.at[a]) for a in range(n_a)]
        for cp in mine:
            cp.start()
        started = []
        for a in range(n_a):
            first = [copy(a, 0, me, sibling, src=ins[a])]
            first += [copy(a, 1 + j, me, (*chip, c), src=ins[a]) for j, chip in enumerate(chips)]
            for cp in first:
                cp.start()
            started += first
        for a in range(n_a):
            for j, chip in enumerate(chips):
                copy(a, 1 + j, (*chip, c), me).wait_recv()
                passed = copy(a, 4 + j, (*chip, c), sibling)
                passed.start()
                started.append(passed)
        for a in range(n_a):
            copy(a, 0, sibling, me).wait_recv()
            for j, chip in enumerate(chips):
                copy(a, 4 + j, (*chip, 1 - c), me).wait_recv()
        for cp in started:
            cp.wait_send()
        for cp in mine:
            cp.wait()

    return pl.pallas_call(
        body, in_specs=[ANY] * n_a, out_specs=[ANY] * n_a,
        out_shape=[jax.ShapeDtypeStruct((N_DEV, *a.shape), a.dtype) for a in arrs],
        scratch_shapes=[pltpu.SemaphoreType.DMA((n_a, 7)), pltpu.SemaphoreType.DMA((n_a, 7)),
                        pltpu.SemaphoreType.DMA((n_a,))],
        name=name,
    )(*arrs)


def _sibling_exchange(arrs, *, name):
    n_a = len(arrs)

    def body(*refs):
        ins = refs[0:n_a]
        outs = refs[n_a:2 * n_a]
        send_sems, recv_sems = refs[2 * n_a:]
        x, y, c = _place()
        copies = [pltpu.make_async_remote_copy(
            src_ref=ins[a].at[:, 1 - c], dst_ref=outs[a], send_sem=send_sems.at[a], recv_sem=recv_sems.at[a],
            device_id=(x, y, 1 - c), device_id_type=MESH) for a in range(n_a)]
        for cp in copies:
            cp.start()
        for cp in copies:
            cp.wait()

    return pl.pallas_call(
        body, in_specs=[ANY] * n_a, out_specs=[ANY] * n_a,
        out_shape=[jax.ShapeDtypeStruct((N_CHIP, *a.shape[2:]), a.dtype) for a in arrs],
        scratch_shapes=[pltpu.SemaphoreType.DMA((n_a,)), pltpu.SemaphoreType.DMA((n_a,))],
        name=name,
    )(*arrs)


def _row_tile(r, cap):
    for tr in range(min(cap, r) // 16 * 16, 0, -16):
        if r % tr == 0:
            return tr
    return r


def _pair_sum(mine, theirs, core, *, name):
    n_chip, _, r, c = mine.shape
    tr = _row_tile(r, 1024)

    def body(core_ref, a_ref, b_ref, o_ref):
        o_ref[...] = (a_ref[...].astype(F32) + b_ref[...].astype(F32)).astype(o_ref.dtype)

    return pl.pallas_call(
        body,
        grid_spec=pltpu.PrefetchScalarGridSpec(
            num_scalar_prefetch=1, grid=(n_chip, r // tr),
            in_specs=[pl.BlockSpec((None, None, tr, c), lambda q, i, core_ref: (q, core_ref[0], i, 0)),
                      pl.BlockSpec((None, tr, c), lambda q, i, core_ref: (q, i, 0))],
            out_specs=pl.BlockSpec((None, tr, c), lambda q, i, core_ref: (q, i, 0))),
        out_shape=jax.ShapeDtypeStruct((n_chip, r, c), mine.dtype),
        compiler_params=_params("parallel", "parallel"), name=name,
    )(core, mine, theirs)


def _chip_exchange(arrs, *, name):
    n_a = len(arrs)

    def body(*refs):
        ins = refs[0:n_a]
        outs = refs[n_a:2 * n_a]
        send_sems, recv_sems, local_sems = refs[2 * n_a:]
        x, y, c = _place()
        my_chip = 2 * x + y
        chips = [(1 - x, y), (x, 1 - y), (1 - x, 1 - y)]
        mine = [pltpu.make_async_copy(ins[a].at[my_chip], outs[a].at[my_chip], local_sems.at[a]) for a in range(n_a)]
        for cp in mine:
            cp.start()
        copies = []
        for a in range(n_a):
            for j, (px, py) in enumerate(chips):
                copies.append(pltpu.make_async_remote_copy(
                    src_ref=ins[a].at[2 * px + py], dst_ref=outs[a].at[my_chip],
                    send_sem=send_sems.at[a, j], recv_sem=recv_sems.at[a, j],
                    device_id=(px, py, c), device_id_type=MESH))
        for cp in copies:
            cp.start()
        for cp in copies:
            cp.wait()
        for cp in mine:
            cp.wait()

    return pl.pallas_call(
        body, in_specs=[ANY] * n_a, out_specs=[ANY] * n_a,
        out_shape=[jax.ShapeDtypeStruct(a.shape, a.dtype) for a in arrs],
        scratch_shapes=[pltpu.SemaphoreType.DMA((n_a, 3)), pltpu.SemaphoreType.DMA((n_a, 3)),
                        pltpu.SemaphoreType.DMA((n_a,))],
        name=name,
    )(*arrs)


HBM = pl.BlockSpec(memory_space=pltpu.HBM)
SEM = pl.BlockSpec(memory_space=pltpu.SEMAPHORE)
EFFECT = pltpu.SideEffectType.DATAFLOW_SIDE_EFFECTING


def _in_hbm(a):
    return pltpu.with_memory_space_constraint(a, pltpu.HBM)


def _split_start(srcs, lands, plan, n_copies, after, *, name):
    n_s, n_l = len(srcs), len(lands)

    def body(*refs):
        src_refs = refs[0:n_s]
        land_refs = refs[n_s:n_s + n_l]
        send_sems, recv_sems = refs[n_s + n_l + 1], refs[n_s + n_l + 2]
        token = refs[-1]
        for cp in plan(src_refs, land_refs, send_sems, recv_sems):
            cp.start()
        token[...] = jnp.zeros_like(token)

    thru = [pltpu.HBM(a.shape, a.dtype) for a in list(srcs) + list(lands)]
    res = pl.pallas_call(
        body, name=name,
        out_shape=(pltpu.SemaphoreType.DMA((n_copies,)), pltpu.SemaphoreType.DMA((n_copies,)), *thru,
                   jax.ShapeDtypeStruct((SUB, LANES), F32)),
        in_specs=[HBM] * (n_s + n_l) + [ANY],
        out_specs=(SEM, SEM, *([HBM] * (n_s + n_l)), pl.BlockSpec(memory_space=pltpu.VMEM)),
        input_output_aliases={i: 2 + i for i in range(n_s + n_l)},
        compiler_params=pltpu.CompilerParams(has_side_effects=EFFECT),
    )(*[_in_hbm(a) for a in srcs], *[_in_hbm(a) for a in lands], _in_hbm(after))
    return res[0], res[1], list(res[2:2 + n_s]), list(res[2 + n_s:2 + n_s + n_l]), res[-1]


def _split_wait(send_sems, recv_sems, srcs, lands, after, plan, *, name):
    n_s, n_l = len(srcs), len(lands)

    def body(*refs):
        src_refs = refs[0:n_s]
        land_refs = refs[n_s:n_s + n_l]
        send, recv = refs[n_s + n_l], refs[n_s + n_l + 1]
        for cp in plan(src_refs, land_refs, send, recv):
            cp.wait_send()
            cp.wait_recv()

    res = pl.pallas_call(
        body, name=name,
        out_shape=tuple(pltpu.HBM(a.shape, a.dtype) for a in list(srcs) + list(lands)),
        in_specs=[HBM] * (n_s + n_l) + [SEM, SEM, ANY],
        out_specs=tuple([HBM] * (n_s + n_l)),
        input_output_aliases={i: i for i in range(n_s + n_l)},
        compiler_params=pltpu.CompilerParams(has_side_effects=EFFECT),
    )(*srcs, *lands, send_sems, recv_sems, _in_hbm(after))
    return list(res[n_s:])


def _remote(src, dst, send_sems, recv_sems, k, to):
    return pltpu.make_async_remote_copy(src_ref=src, dst_ref=dst, send_sem=send_sems.at[k], recv_sem=recv_sems.at[k],
                                        device_id=to, device_id_type=MESH)


def _gather_plan_first(src_refs, land_refs, send_sems, recv_sems):
    x, y, c = _place()
    me = 4 * x + 2 * y + c
    peers = [(x, y, 1 - c), (1 - x, y, c), (x, 1 - y, c), (1 - x, 1 - y, c)]
    return [_remote(src, land.at[me], send_sems, recv_sems, 4 * a + k, to)
            for a, (src, land) in enumerate(zip(src_refs, land_refs)) for k, to in enumerate(peers)]


def _gather_plan_second(src_refs, land_refs, send_sems, recv_sems):
    x, y, c = _place()
    chips = [(1 - x, y), (x, 1 - y), (1 - x, 1 - y)]
    out = []
    for a, land in enumerate(land_refs):
        for j, (px, py) in enumerate(chips):
            slot = land.at[4 * px + 2 * py + c]
            out.append(_remote(slot, slot, send_sems, recv_sems, 3 * a + j, (x, y, 1 - c)))
    return out


def _chips_plan(src_refs, land_refs, send_sems, recv_sems):
    x, y, c = _place()
    my_chip = 2 * x + y
    chips = [(1 - x, y), (x, 1 - y), (1 - x, 1 - y)]
    return [_remote(src.at[2 * px + py], land.at[my_chip], send_sems, recv_sems, 3 * a + j, (px, py, c))
            for a, (src, land) in enumerate(zip(src_refs, land_refs)) for j, (px, py) in enumerate(chips)]


def _landing(like_shape, dtype, own, index):
    return lax.dynamic_update_index_in_dim(lax.empty(like_shape, dtype), own, index, 0)


def _adamw_math(g, w, m, v):
    m = ADAM_B1 * m + (1.0 - ADAM_B1) * g
    v = ADAM_B2 * v + (1.0 - ADAM_B2) * (g * g)
    m_hat = m / (1.0 - ADAM_B1 ** ADAM_STEP)
    v_hat = v / (1.0 - ADAM_B2 ** ADAM_STEP)
    delta = -ADAM_LR * (m_hat / (jnp.sqrt(v_hat) + ADAM_EPS) + ADAM_WD * w)
    return delta, m, v


def _adamw_sharded(parts, w, m, v, *, name, dep=None):
    n_layers, r, c = w.shape
    n_chip = parts[0].shape[0]
    tr = _row_tile(r, 384)
    n_i = r // tr

    def body(*refs):
        p_refs = refs[0:n_layers]
        w_ref, m_ref, v_ref, _, g_out, d_out, m_out, v_out = refs[n_layers:]
        layer = pl.program_id(0)
        for l in range(n_layers):
            @pl.when(layer == l)
            def _(l=l):
                g = p_refs[l][0].astype(F32)
                for q in range(1, n_chip):
                    g = g + p_refs[l][q].astype(F32)
                delta, m_new, v_new = _adamw_math(g, w_ref[...], m_ref[...], v_ref[...])
                g_out[...] = g
                d_out[...] = delta
                m_out[...] = m_new
                v_out[...] = v_new

    def part_map(l):
        return lambda layer, i: (0, jnp.where(layer == l, i, jnp.where(layer < l, 0, n_i - 1)), 0)

    blk = pl.BlockSpec((None, tr, c), lambda layer, i: (layer, i, 0))
    return pl.pallas_call(
        body, grid=(n_layers, n_i),
        in_specs=[pl.BlockSpec((n_chip, tr, c), part_map(l)) for l in range(n_layers)] + [blk, blk, blk, ANY],
        out_specs=[blk] * 4, out_shape=[jax.ShapeDtypeStruct((n_layers, r, c), F32)] * 4,
        compiler_params=_params("arbitrary", "arbitrary"), name=name,
    )(*parts, w, m, v, w if dep is None else dep)


def _fold_partials(cols, *, name):
    widths = [c.shape[1] for c in cols]

    def body(*refs):
        o_ref = refs[-1]
        pos = 0
        for ref, width in zip(refs[:-1], widths):
            o_ref[:, pos:pos + width] = jnp.sum(ref[...], axis=0, keepdims=True)
            pos += width

    return pl.pallas_call(body, out_shape=jax.ShapeDtypeStruct((1, sum(widths)), F32), name=name)(*cols)


def _adamw_replicated(parts, names, w, m, v, n_loss, *, name):
    n_dev = parts.shape[0]
    n_layers = w[names[0]].shape[0]
    every = list(names) + ["final_norm_g"]
    n_p = len(every)

    def body(*refs):
        p_ref = refs[0]
        w_refs = dict(zip(every, refs[1:1 + n_p]))
        m_refs = dict(zip(every, refs[1 + n_p:1 + 2 * n_p]))
        v_refs = dict(zip(every, refs[1 + 2 * n_p:1 + 3 * n_p]))
        l_out = refs[1 + 3 * n_p]
        outs = refs[2 + 3 * n_p:]
        o_refs = {n: outs[4 * q:4 * q + 4] for q, n in enumerate(every)}
        acc = p_ref[0]
        for q in range(1, n_dev):
            acc = acc + p_ref[q]
        tot = jnp.sum(acc, axis=0, keepdims=True)
        pos = 0
        where = [(n, l) for l in range(n_layers) for n in names] + [("final_norm_g", 0)]
        for n, l in where:
            width = w_refs[n].shape[1]
            g = tot[:, pos:pos + width]
            pos += width
            row = pl.ds(l, 1)
            delta, m_new, v_new = _adamw_math(g, w_refs[n][row, :], m_refs[n][row, :], v_refs[n][row, :])
            for o, val in zip(o_refs[n], (g, delta, m_new, v_new)):
                o[row, :] = val
        l_out[...] = (0.5 / n_loss) * jnp.sum(tot[:, pos:pos + n_loss], axis=-1, keepdims=True)

    shapes = [jax.ShapeDtypeStruct((1, 1), F32)]
    for n in every:
        shapes += [jax.ShapeDtypeStruct(w[n].shape, F32)] * 4
    res = pl.pallas_call(
        body, out_shape=shapes,
        compiler_params=pltpu.CompilerParams(vmem_limit_bytes=VMEM_LIMIT), name=name,
    )(parts, *[w[n] for n in every], *[m[n] for n in every], *[v[n] for n in every])
    return res[0], {n: res[1 + 4 * q:5 + 4 * q] for q, n in enumerate(every)}


BIG = ("w_in", "w_out", "w_up", "w_down")
COL_SHARDED = ("w_in", "w_up")
CONV = ("conv_a_w", "conv_b_w", "conv_f_w")
REPLICATED = ("mix_norm_g", "b_in", "conv_a_b", "ln_a_g", "ln_a_b", "ffn_norm_g")
KINDS = ("grad", "delta", "m", "v")
FFN_PART = ("w_up", "w_down")
MIX_PART = ("w_in", "w_out")


def _weights_from_gathered(g):
    n_dev, r, c = g.shape
    return g.reshape(n_dev * r, c)


def _slabs_from_full(grad):
    return grad.reshape(N_DEV, grad.shape[0] // N_DEV, grad.shape[1])


def _pair_sums(slabs, core, tag):
    slabs = [s.reshape(N_CHIP, 2, *s.shape[1:]) for s in slabs]
    theirs = _sibling_exchange(slabs, name=f"reduce_siblings_{tag}")
    return [_pair_sum(a, b, core, name=f"pair_sum_{tag}_{q}") for q, (a, b) in enumerate(zip(slabs, theirs))]


def kernel(x, mix_norm_g, w_in, b_in, conv_a_w, conv_a_b, ln_a_g, ln_a_b, conv_b_w, w_out, ffn_norm_g, w_up, conv_f_w, w_down, final_norm_g, loss_target, m_mix_norm_g, m_w_in, m_b_in, m_conv_a_w, m_conv_a_b, m_ln_a_g, m_ln_a_b, m_conv_b_w, m_w_out, m_ffn_norm_g, m_w_up, m_conv_f_w, m_w_down, m_final_norm_g, v_mix_norm_g, v_w_in, v_b_in, v_conv_a_w, v_conv_a_b, v_ln_a_g, v_ln_a_b, v_conv_b_w, v_w_out, v_ffn_norm_g, v_w_up, v_conv_f_w, v_w_down, v_final_norm_g):
    w = dict(mix_norm_g=mix_norm_g, w_in=w_in, b_in=b_in, conv_a_w=conv_a_w, conv_a_b=conv_a_b, ln_a_g=ln_a_g,
             ln_a_b=ln_a_b, conv_b_w=conv_b_w, w_out=w_out, ffn_norm_g=ffn_norm_g, w_up=w_up, conv_f_w=conv_f_w,
             w_down=w_down, final_norm_g=final_norm_g)
    m = dict(mix_norm_g=m_mix_norm_g, w_in=m_w_in, b_in=m_b_in, conv_a_w=m_conv_a_w, conv_a_b=m_conv_a_b,
             ln_a_g=m_ln_a_g, ln_a_b=m_ln_a_b, conv_b_w=m_conv_b_w, w_out=m_w_out, ffn_norm_g=m_ffn_norm_g,
             w_up=m_w_up, conv_f_w=m_conv_f_w, w_down=m_w_down, final_norm_g=m_final_norm_g)
    v = dict(mix_norm_g=v_mix_norm_g, w_in=v_w_in, b_in=v_b_in, conv_a_w=v_conv_a_w, conv_a_b=v_conv_a_b,
             ln_a_g=v_ln_a_g, ln_a_b=v_ln_a_b, conv_b_w=v_conv_b_w, w_out=v_w_out, ffn_norm_g=v_ffn_norm_g,
             w_up=v_w_up, conv_f_w=v_conv_f_w, w_down=v_w_down, final_norm_g=v_final_norm_g)
    order = list(w)
    n_layers = w_in.shape[0]
    n_big = len(BIG)
    xs = x[0]
    target = loss_target[0]
    flip = lambda a: jnp.transpose(a, (0, 2, 1))
    wt, mt, vt = ({n: flip(d[n]) if n in COL_SHARDED else d[n] for n in BIG} for d in (w, m, v))
    px, py, pc = _place()
    core = pc.astype(jnp.int32).reshape(1)
    me = 4 * px + 2 * py + pc
    my_chip = 2 * px + py

    assert BIG == MIX_PART + FFN_PART
    key = lambda n: n + "_t" if n in COL_SHARDED else n
    shard = lambda n, l: wt[n][l].astype(BF16)

    def gather_start(names, l, after, tag):
        shards = [shard(n, l) for n in names]
        lands = [_landing((N_DEV, *s.shape), s.dtype, s, me) for s in shards]
        return _split_start(shards, lands, _gather_plan_first, 4 * len(shards), after, name=f"gather_first_start_{tag}")

    def gather_mid(first, after, tag):
        return _split_wait(first[0], first[1], first[2], first[3], after, _gather_plan_first,
                           name=f"gather_first_wait_{tag}")

    def forward_start(lands, after, tag):
        return _split_start([], lands, _gather_plan_second, 3 * len(lands), after, name=f"gather_second_start_{tag}")

    def forward_finish(second, after, tag):
        return _split_wait(second[0], second[1], [], second[3], after, _gather_plan_second,
                           name=f"gather_second_wait_{tag}")

    gathered = _all_gather([shard(n, 0) for n in MIX_PART] + [w[n] for n in CONV], name="gather_weights_0")
    params = [{n: w[n][l] for n in REPLICATED} for l in range(n_layers)]
    for n, g in zip(CONV, gathered[len(MIX_PART):]):
        n_dev, _, taps, c = g.shape
        full = g.transpose(1, 2, 0, 3).reshape(n_layers, taps, n_dev * c)
        for l in range(n_layers):
            params[l][n] = full[l]
    for n, g in zip(MIX_PART, gathered):
        params[0][key(n)] = _weights_from_gathered(g)
    ffn_first = gather_start(FFN_PART, 0, gathered[0], "0_ffn")
    pending = {}

    h = xs
    saved = []
    for l in range(n_layers):
        nxt = l + 1 if l + 1 < n_layers else None

        def before_up(x1, l=l, nxt=nxt):
            if l == 0:
                second = forward_start(gather_mid(ffn_first, x1, "0_ffn"), x1, "0_ffn")
                after = second[4]
            else:
                second = pending[l]["ffn"]
                after = x1
            if nxt is not None:
                pending[nxt] = dict(first=gather_start(BIG, nxt, after, str(nxt)))
                after = pending[nxt]["first"][4]
            for n, g in zip(FFN_PART, forward_finish(second, after, f"{l}_ffn")):
                params[l][key(n)] = _weights_from_gathered(g)

        h, keep = _layer_fwd(h, params[l], str(l), dep=ffn_first[4] if l == 0 else None, before_up=before_up)
        saved.append(keep)
        if nxt is not None:
            arrived = gather_mid(pending[nxt]["first"], h, str(nxt))
            mix_second = forward_start(arrived[:len(MIX_PART)], h, f"{nxt}_mix")
            pending[nxt]["ffn"] = forward_start(arrived[len(MIX_PART):], mix_second[4], f"{nxt}_ffn")
            for n, g in zip(MIX_PART, forward_finish(mix_second, pending[nxt]["ffn"][4], f"{nxt}_mix")):
                params[nxt][key(n)] = _weights_from_gathered(g)

    def start_reduce(slabs, tag):
        pairs = _pair_sums(slabs, core, tag)
        lands = [_landing(p.shape, p.dtype, lax.dynamic_index_in_dim(p, my_chip, 0, keepdims=False), my_chip)
                 for p in pairs]
        return _split_start(pairs, lands, _chips_plan, 3 * len(pairs), pairs[0], name=f"reduce_chips_start_{tag}")

    def finish_reduce(fly, after, tag):
        return _split_wait(fly[0], fly[1], fly[2], fly[3], after, _chips_plan, name=f"reduce_chips_wait_{tag}")

    loss_sq, dh, dh_b, dgf = _loss_bwd(h, _row(final_norm_g), target, name="loss")
    conv_g = {n: [None] * n_layers for n in CONV}
    rep_g = [None] * n_layers
    flights = {}
    token = None
    for l in reversed(range(n_layers)):
        def after_ffn(g, l=l):
            flights[l, "ffn"] = start_reduce([_slabs_from_full(g[n]) for n in FFN_PART], f"{l}_ffn")
            return flights[l, "ffn"][4]

        def after_mix(g, conv, l=l):
            for n in CONV:
                conv_g[n][l] = conv[n]
            slabs = [_slabs_from_full(g[n]) for n in MIX_PART]
            if l == 0:
                for n in CONV:
                    full = jnp.stack(conv_g[n])
                    _, taps, c = full.shape
                    slabs.append(full.reshape(n_layers, taps, N_DEV, c // N_DEV).transpose(2, 0, 1, 3)
                                 .reshape(N_DEV, n_layers * taps, c // N_DEV))
            flights[l, "mix"] = start_reduce(slabs, f"{l}_mix")
            return flights[l, "mix"][4]

        dh, dh_b, rep_g[l] = _layer_bwd(dh, dh_b, params[l], saved[l], str(l), after_ffn, after_mix, dep=token)
        token = flights[l, "mix"][4]

    sums = {key: finish_reduce(fly, dh, f"{key[0]}_{key[1]}") for key, fly in flights.items() if key != (0, "mix")}
    out = {k: {} for k in KINDS}

    def adamw_big(names, part, dep):
        for q, n in enumerate(names):
            layer_parts = [sums[l, part][q] for l in range(n_layers)]
            res = _adamw_sharded(layer_parts, wt[n], mt[n], vt[n], name=f"adamw_{n}", dep=dep)
            for k, r in zip(KINDS, res):
                out[k][n] = flip(r) if n in COL_SHARDED else r

    adamw_big(FFN_PART, "ffn", token)

    rep_cols = [rep_g[l][n] for l in range(n_layers) for n in REPLICATED] + [dgf, loss_sq]
    rep_all = _all_gather([_fold_partials(rep_cols, name="fold_small")], name="gather_small")[0]
    with_final = lambda d: {**{n: d[n] for n in REPLICATED}, "final_norm_g": _row(d["final_norm_g"])}
    loss, rep_res = _adamw_replicated(rep_all, REPLICATED, with_final(w), with_final(m), with_final(v),
                                      loss_sq.shape[1], name="adamw_small")
    for n, res in rep_res.items():
        for k, r in zip(KINDS, res):
            out[k][n] = r.reshape(w[n].shape)

    last = finish_reduce(flights[0, "mix"], rep_res["b_in"][0], "0_mix")
    sums[0, "mix"] = last[:len(MIX_PART)]
    adamw_big(MIX_PART, "mix", None)
    for n, p in zip(CONV, last[len(MIX_PART):]):
        as_one = lambda a: a.reshape(1, *p.shape[1:])
        for k, r in zip(KINDS, _adamw_sharded([p], as_one(w[n]), as_one(m[n]), as_one(v[n]), name=f"adamw_{n}")):
            out[k][n] = r.reshape(w[n].shape)

    grad_x = dh.reshape(x.shape)
    return (loss.reshape(()), grad_x, *[out["grad"][n] for n in order], *[out["delta"][n] for n in order],
            *[out["m"][n] for n in order], *[out["v"][n] for n in order])
```

```python
import functools

import jax
import jax.numpy as jnp
from jax import lax
from jax.experimental import pallas as pl
from jax.experimental.pallas import tpu as pltpu

F32 = jnp.float32
BF16 = jnp.bfloat16

N_DEV = 8
N_CHIP = 4
D_CONF = 512
CONF_K = 31
SHORT_K = 3
EPS = 1e-6
HALO = 32
HALO3 = 8
HALO3_BLK = 16
LANES = 128
SUB = 8
VMEM_LIMIT = 56 * 1024 * 1024

ADAM_LR = 0.001
ADAM_B1 = 0.9
ADAM_B2 = 0.999
ADAM_EPS = 1e-08
ADAM_WD = 0.01
ADAM_STEP = 10

MESH = pl.DeviceIdType.MESH
ANY = pl.BlockSpec(memory_space=pl.ANY)


def _params(*sem):
    return pltpu.CompilerParams(dimension_semantics=sem, vmem_limit_bytes=VMEM_LIMIT)


def _resident(shape, index_map):
    return pl.BlockSpec(shape, index_map, pipeline_mode=pl.Buffered(1))


def _row_loop(n_rows, rb, fn, unroll=1):
    rb = min(rb, n_rows)

    def body(i, carry):
        fn(pl.ds(pl.multiple_of(i * rb, rb), rb))
        return carry
    lax.fori_loop(0, n_rows // rb, body, 0, unroll=unroll)


def _rows8(v):
    acc = v[0:SUB]
    for k in range(1, v.shape[0] // SUB):
        acc = acc + v[k * SUB:(k + 1) * SUB]
    return acc


def _sigmoid(z):
    return 0.5 * jnp.tanh(0.5 * z) + 0.5


def _dot(a, b):
    return jnp.dot(a, b, preferred_element_type=F32)


def _dot_nt(a, b):
    return lax.dot_general(a, b, (((1,), (1,)), ((), ())), preferred_element_type=F32)


def _dot_tn(a, b):
    return lax.dot_general(a, b, (((0,), (0,)), ((), ())), preferred_element_type=F32)


def _replicate_taps(w_ref, wrep, taps):
    for k in range(taps):
        wrep[pl.ds(k * SUB, SUB), :] = jnp.broadcast_to(w_ref[pl.ds(k, 1), :], (SUB, w_ref.shape[1]))


def _shift_copies(win, shf, lanes):
    span = win.shape[0] - SUB
    for r in range(1, SUB):
        for j0 in range(0, span, 64):
            n = min(64, span - j0)
            shf[r - 1, pl.ds(j0, n), lanes] = win[pl.ds(j0 + r, n), lanes]


def _rows_at(win, shf, off, rb, lanes):
    if shf is None or off % SUB == 0:
        return win[pl.ds(off, rb), lanes]
    return shf[off % SUB - 1, pl.ds(off - off % SUB, rb), lanes]


def _conv_taps(win, wrep, out, *, taps, n_rows, base, width, transposed=False, bias_ref=None, shf=None):
    rb = min(64, n_rows)

    def lane_body(cb, carry):
        lanes = pl.ds(pl.multiple_of(cb * LANES, LANES), LANES)
        if shf is not None:
            _shift_copies(win, shf, lanes)
        for r0 in range(0, n_rows, rb):
            acc = None
            for k in range(taps):
                off = (taps - 1 - k) if transposed else (k - (taps - 1))
                wk = jnp.tile(wrep[pl.ds(k * SUB, SUB), lanes], (rb // SUB, 1))
                term = wk * _rows_at(win, shf, base + r0 + off, rb, lanes)
                acc = term if acc is None else acc + term
            if bias_ref is not None:
                acc = acc + bias_ref[:, lanes]
            out[pl.ds(r0, rb), lanes] = acc.astype(out.dtype)
        return carry

    lax.fori_loop(0, width // LANES, lane_body, 0)


def _conv_bwd_taps(win, wrep, x_cur, dx_out, dw_acc, *, taps, n_rows, width, shf=None):
    rb = min(32 if taps > 8 else 64, n_rows)

    def lane_body(cb, carry):
        lanes = pl.ds(pl.multiple_of(cb * LANES, LANES), LANES)
        if shf is not None:
            _shift_copies(win, shf, lanes)
        sums = [None] * taps
        for r0 in range(0, n_rows, rb):
            xv = x_cur[pl.ds(r0, rb), lanes].astype(F32)
            acc = None
            for k in range(taps):
                shifted = _rows_at(win, shf, r0 + taps - 1 - k, rb, lanes)
                term = jnp.tile(wrep[pl.ds(k * SUB, SUB), lanes], (rb // SUB, 1)) * shifted
                acc = term if acc is None else acc + term
                part = _rows8(xv * shifted)
                sums[k] = part if sums[k] is None else sums[k] + part
            dx_out[pl.ds(r0, rb), lanes] = acc.astype(dx_out.dtype)
        for k in range(taps):
            dw_acc[pl.ds(k * SUB, SUB), lanes] += sums[k]
        return carry

    lax.fori_loop(0, width // LANES, lane_body, 0)


def _fold8(acc_ref, taps):
    return jnp.concatenate(
        [jnp.sum(acc_ref[pl.ds(k * SUB, SUB), :], axis=0, keepdims=True) for k in range(taps)], axis=0)


def _seq_tile(s_len):
    return min(512, s_len)


def _mm_tile(s_len):
    return min(512, s_len)


def _ff_chunk(ff):
    best = LANES
    for c in range(LANES, 1408 + 1, LANES):
        if ff % c == 0:
            best = c
    return best


def _col_tile(n):
    for c in (512, 1408, 256, LANES):
        if n % c == 0:
            return c
    return n


def _rms_matmul(x, g, wt, b, *, name, dep=None):
    s_len, d = x.shape
    n = wt.shape[0]
    tm = _mm_tile(s_len)
    cn = _col_tile(n)
    has_bias = b is not None

    def body(*refs):
        x_ref, g_ref, w_ref = refs[0:3]
        b_ref = refs[3] if has_bias else None
        o_ref, h_ref = refs[-2:]

        def blk(rows):
            xv = x_ref[rows, :]
            r = lax.rsqrt(jnp.mean(xv * xv, axis=-1, keepdims=True) + EPS)
            h_ref[rows, :] = ((xv * r) * g_ref[...]).astype(BF16)

        rb = min(128, tm)
        for r0 in range(0, tm, rb):
            blk(pl.ds(r0, rb))
        for j in range(n // cn):
            acc = _dot_nt(h_ref[...], w_ref[j * cn:(j + 1) * cn, :])
            if has_bias:
                acc = acc + b_ref[:, j * cn:(j + 1) * cn]
            o_ref[:, j * cn:(j + 1) * cn] = acc.astype(BF16)

    in_specs = [pl.BlockSpec((tm, d), lambda i: (i, 0)), _resident((1, d), lambda i: (0, 0)),
                _resident((n, d), lambda i: (0, 0))]
    args = [x, g, wt]
    if has_bias:
        in_specs.append(_resident((1, n), lambda i: (0, 0)))
        args.append(b)
    in_specs.append(ANY)
    args.append(x if dep is None else dep)
    return pl.pallas_call(
        body, grid=(s_len // tm,), in_specs=in_specs,
        out_specs=[pl.BlockSpec((tm, n), lambda i: (i, 0)), pl.BlockSpec((tm, d), lambda i: (i, 0))],
        out_shape=[jax.ShapeDtypeStruct((s_len, n), BF16), jax.ShapeDtypeStruct((s_len, d), BF16)],
        compiler_params=_params("parallel"), name=name,
    )(*args)


def _mix_windows(u_ref, uh_ref, gw, pw, first, t):
    c = D_CONF
    uh = uh_ref[...].astype(F32)
    gw[0:HALO, :] = jnp.where(first, 0.0, uh[:, 0:c] * _sigmoid(uh[:, c:2 * c]))
    pw[0:HALO3, :] = jnp.where(first, 0.0, uh[HALO - HALO3:HALO, 3 * c:4 * c] * uh[HALO - HALO3:HALO, 4 * c:5 * c])

    def blk(rows):
        dst = pl.ds(pl.multiple_of(rows.start + HALO, SUB), rows.size)
        gw[dst, :] = u_ref[rows, 0:c].astype(F32) * _sigmoid(u_ref[rows, c:2 * c].astype(F32))
        dst3 = pl.ds(pl.multiple_of(rows.start + HALO3, SUB), rows.size)
        pw[dst3, :] = u_ref[rows, 3 * c:4 * c].astype(F32) * u_ref[rows, 4 * c:5 * c].astype(F32)
    _row_loop(t, 64, blk)


def _mix_fwd(u, x0, wa, ba, lg, lb, wb, w_out, *, name):
    s_len, d_in = u.shape
    d = x0.shape[1]
    c = D_CONF
    t = _seq_tile(s_len)
    per = t // HALO

    def body(u_ref, uh_ref, x0_ref, wa_ref, ba_ref, lg_ref, lb_ref, wb_ref, wo_ref, y_ref, x1_ref, ca, cb,
             gw, pw, wrep_a, wrep_b, shf):
        first = pl.program_id(0) == 0
        _mix_windows(u_ref, uh_ref, gw, pw, first, t)
        _replicate_taps(wa_ref, wrep_a, CONF_K)
        _replicate_taps(wb_ref, wrep_b, SHORT_K)
        _conv_taps(gw, wrep_a, ca, taps=CONF_K, n_rows=t, base=HALO, width=c, bias_ref=ba_ref, shf=shf)
        _conv_taps(pw, wrep_b, cb, taps=SHORT_K, n_rows=t, base=HALO3, width=c)

        def blk(rows):
            cv = ca[rows, :]
            mu = jnp.mean(cv, axis=-1, keepdims=True)
            xc = cv - mu
            var = jnp.mean(xc * xc, axis=-1, keepdims=True)
            ln = (xc * lax.rsqrt(var + EPS)) * lg_ref[...] + lb_ref[...]
            y_ref[rows, 0:c] = (ln * _sigmoid(ln)).astype(BF16)
            y_ref[rows, c:2 * c] = (u_ref[rows, 2 * c:3 * c].astype(F32) * cb[rows, :]).astype(BF16)
        _row_loop(t, 64, blk)
        x1_ref[...] = x0_ref[...] + _dot(y_ref[...], wo_ref[...])

    small = lambda r: _resident((r, c), lambda i: (0, 0))
    return pl.pallas_call(
        body, grid=(s_len // t,),
        in_specs=[pl.BlockSpec((t, d_in), lambda i: (i, 0)),
                  pl.BlockSpec((HALO, d_in), lambda i: (jnp.maximum(i * per - 1, 0), 0)),
                  pl.BlockSpec((t, d), lambda i: (i, 0)),
                  small(CONF_K), small(1), small(1), small(1), small(SHORT_K),
                  _resident((2 * c, d), lambda i: (0, 0))],
        out_specs=[pl.BlockSpec((t, 2 * c), lambda i: (i, 0)), pl.BlockSpec((t, d), lambda i: (i, 0)),
                   pl.BlockSpec((t, c), lambda i: (i, 0)), pl.BlockSpec((t, c), lambda i: (i, 0))],
        out_shape=[jax.ShapeDtypeStruct((s_len, 2 * c), BF16), jax.ShapeDtypeStruct((s_len, d), F32),
                   jax.ShapeDtypeStruct((s_len, c), F32), jax.ShapeDtypeStruct((s_len, c), F32)],
        scratch_shapes=[pltpu.VMEM((HALO + t, c), F32), pltpu.VMEM((HALO3 + t, c), F32),
                        pltpu.VMEM((CONF_K * SUB, c), F32), pltpu.VMEM((SHORT_K * SUB, c), F32),
                        pltpu.VMEM((SUB - 1, HALO + t, c), F32)],
        compiler_params=_params("arbitrary"), name=name,
    )(u, u, x0, wa, ba, lg, lb, wb, w_out)


def _ffn_windows(ug_ref, ugh_ref, uv_ref, uvh_ref, gwin, vwin, first, t):
    lo = HALO3_BLK - HALO3
    gwin[0:HALO3, :] = jnp.where(first, 0.0, ugh_ref[...].astype(F32)[lo:HALO3_BLK])
    vwin[0:HALO3, :] = jnp.where(first, 0.0, uvh_ref[...].astype(F32)[lo:HALO3_BLK])

    def blk(rows):
        dst = pl.ds(pl.multiple_of(rows.start + HALO3, SUB), rows.size)
        gwin[dst, :] = ug_ref[rows, :].astype(F32)
        vwin[dst, :] = uv_ref[rows, :].astype(F32)
    _row_loop(t, 64, blk)


def _ffn_fwd(uf, x1, wf, w_down, *, name, dep=None):
    s_len, ff2 = uf.shape
    ff = ff2 // 2
    d = x1.shape[1]
    t = _seq_tile(s_len)
    fc = _ff_chunk(ff)
    nc = ff // fc
    per = t // HALO3_BLK

    def body(ug_ref, ugh_ref, uv_ref, uvh_ref, x1_ref, wfg_ref, wfv_ref, wd_ref, dep_ref,
             act_ref, x2_ref, cg_ref, cv_ref, gwin, vwin, cg, cv, wrep_g, wrep_v):
        first = pl.program_id(0) == 0
        _ffn_windows(ug_ref, ugh_ref, uv_ref, uvh_ref, gwin, vwin, first, t)
        _replicate_taps(wfg_ref, wrep_g, SHORT_K)
        _replicate_taps(wfv_ref, wrep_v, SHORT_K)
        _conv_taps(gwin, wrep_g, cg, taps=SHORT_K, n_rows=t, base=HALO3, width=fc)
        _conv_taps(vwin, wrep_v, cv, taps=SHORT_K, n_rows=t, base=HALO3, width=fc)

        def blk(rows):
            gv = cg[rows, :]
            vv = cv[rows, :]
            cg_ref[rows, :] = gv.astype(BF16)
            cv_ref[rows, :] = vv.astype(BF16)
            act_ref[rows, :] = ((gv * _sigmoid(gv)) * vv).astype(BF16)
        _row_loop(t, 32, blk, unroll=2)

        @pl.when(pl.program_id(1) == 0)
        def _():
            x2_ref[...] = x1_ref[...]
        x2_ref[...] += _dot(act_ref[...], wd_ref[...])

    halo_map = lambda off: (lambda i, j: (jnp.maximum(i * per - 1, 0), j + off))
    return pl.pallas_call(
        body, grid=(s_len // t, nc),
        in_specs=[pl.BlockSpec((t, fc), lambda i, j: (i, j)), pl.BlockSpec((HALO3_BLK, fc), halo_map(0)),
                  pl.BlockSpec((t, fc), lambda i, j: (i, j + nc)), pl.BlockSpec((HALO3_BLK, fc), halo_map(nc)),
                  pl.BlockSpec((t, d), lambda i, j: (i, 0)),
                  pl.BlockSpec((SHORT_K, fc), lambda i, j: (0, j)),
                  pl.BlockSpec((SHORT_K, fc), lambda i, j: (0, j + nc)),
                  pl.BlockSpec((fc, d), lambda i, j: (j, 0)), ANY],
        out_specs=[pl.BlockSpec((t, fc), lambda i, j: (i, j)), pl.BlockSpec((t, d), lambda i, j: (i, 0)),
                   pl.BlockSpec((t, fc), lambda i, j: (i, j)), pl.BlockSpec((t, fc), lambda i, j: (i, j))],
        out_shape=[jax.ShapeDtypeStruct((s_len, ff), BF16), jax.ShapeDtypeStruct((s_len, d), F32),
                   jax.ShapeDtypeStruct((s_len, ff), BF16), jax.ShapeDtypeStruct((s_len, ff), BF16)],
        scratch_shapes=[pltpu.VMEM((HALO3 + t, fc), F32), pltpu.VMEM((HALO3 + t, fc), F32),
                        pltpu.VMEM((t, fc), F32), pltpu.VMEM((t, fc), F32),
                        pltpu.VMEM((SHORT_K * SUB, fc), F32), pltpu.VMEM((SHORT_K * SUB, fc), F32)],
        compiler_params=_params("parallel", "arbitrary"), name=name,
    )(uf, uf, uf, uf, x1, wf, wf, w_down, uf if dep is None else dep)


def _loss_bwd(x, g, target, *, name):
    s_len, d = x.shape
    t = _seq_tile(s_len)

    def body(x_ref, g_ref, t_ref, l_ref, dx_ref, dxb_ref, dg_ref):
        @pl.when(pl.program_id(0) == 0)
        def _():
            l_ref[...] = jnp.zeros_like(l_ref)
            dg_ref[...] = jnp.zeros_like(dg_ref)

        def blk(rows):
            xv = x_ref[rows, :]
            r = lax.rsqrt(jnp.mean(xv * xv, axis=-1, keepdims=True) + EPS)
            xn = xv * r
            e = xn * g_ref[...] - t_ref[rows, :]
            l_ref[...] += _rows8(e * e)
            dy = e * (1.0 / d)
            dg_ref[...] += _rows8(dy * xn)
            dn = dy * g_ref[...]
            dx = r * (dn - xn * jnp.mean(dn * xn, axis=-1, keepdims=True))
            dx_ref[rows, :] = dx
            dxb_ref[rows, :] = dx.astype(BF16)
        _row_loop(t, 64, blk)

    row = pl.BlockSpec((t, d), lambda i: (i, 0))
    part = pl.BlockSpec((SUB, d), lambda i: (0, 0))
    return pl.pallas_call(
        body, grid=(s_len // t,),
        in_specs=[row, _resident((1, d), lambda i: (0, 0)), row],
        out_specs=[part, row, row, part],
        out_shape=[jax.ShapeDtypeStruct((SUB, d), F32), jax.ShapeDtypeStruct((s_len, d), F32),
                   jax.ShapeDtypeStruct((s_len, d), BF16), jax.ShapeDtypeStruct((SUB, d), F32)],
        compiler_params=_params("arbitrary"), name=name,
    )(x, g, target)


def _ffn_bwd(dx2, uf, cg, cv, wf, w_down, *, name, dep=None):
    s_len, ff2 = uf.shape
    ff = ff2 // 2
    d = dx2.shape[1]
    t = _seq_tile(s_len)
    n_t = s_len // t
    fc = _ff_chunk(ff)
    nc = ff // fc

    def body(dx_ref, ug_ref, uv_ref, cg_ref, cv_ref, wfg_ref, wfv_ref, wd_ref, dep_ref,
             dug_ref, duv_ref, dwg_ref, dwv_ref, dact, dgw, dvw, awg, awv, wrep_g, wrep_v):
        i = pl.program_id(1)

        @pl.when(i == 0)
        def _():
            dgw[t:t + HALO3, :] = jnp.zeros((HALO3, fc), F32)
            dvw[t:t + HALO3, :] = jnp.zeros((HALO3, fc), F32)
            awg[...] = jnp.zeros_like(awg)
            awv[...] = jnp.zeros_like(awv)

        _replicate_taps(wfg_ref, wrep_g, SHORT_K)
        _replicate_taps(wfv_ref, wrep_v, SHORT_K)

        def blk(rows):
            gv = cg_ref[rows, :].astype(F32)
            sg = _sigmoid(gv)
            da = dact[rows, :]
            dgw[rows, :] = (da * cv_ref[rows, :].astype(F32)) * (sg * (1.0 + gv * (1.0 - sg)))
            dvw[rows, :] = da * (gv * sg)

        dact[...] = _dot_nt(dx_ref[...], wd_ref[...])
        _row_loop(t, 32, blk, unroll=2)

        _conv_bwd_taps(dgw, wrep_g, ug_ref, dug_ref, awg, taps=SHORT_K, n_rows=t, width=fc)
        _conv_bwd_taps(dvw, wrep_v, uv_ref, duv_ref, awv, taps=SHORT_K, n_rows=t, width=fc)
        dgw[t:t + HALO3, :] = dgw[0:HALO3, :]
        dvw[t:t + HALO3, :] = dvw[0:HALO3, :]

        @pl.when(i == n_t - 1)
        def _():
            dwg_ref[...] = _fold8(awg, SHORT_K)
            dwv_ref[...] = _fold8(awv, SHORT_K)

    rev = lambda i: n_t - 1 - i
    gate = pl.BlockSpec((t, fc), lambda j, i: (rev(i), j))
    value = pl.BlockSpec((t, fc), lambda j, i: (rev(i), j + nc))
    return pl.pallas_call(
        body, grid=(nc, n_t),
        in_specs=[pl.BlockSpec((t, d), lambda j, i: (rev(i), 0)), gate, value, gate, gate,
                  pl.BlockSpec((SHORT_K, fc), lambda j, i: (0, j)),
                  pl.BlockSpec((SHORT_K, fc), lambda j, i: (0, j + nc)),
                  pl.BlockSpec((fc, d), lambda j, i: (j, 0)), ANY],
        out_specs=[gate, gate,
                   pl.BlockSpec((SHORT_K, fc), lambda j, i: (0, j)), pl.BlockSpec((SHORT_K, fc), lambda j, i: (0, j))],
        out_shape=[jax.ShapeDtypeStruct((s_len, ff), BF16), jax.ShapeDtypeStruct((s_len, ff), BF16),
                   jax.ShapeDtypeStruct((SHORT_K, ff), F32), jax.ShapeDtypeStruct((SHORT_K, ff), F32)],
        scratch_shapes=[pltpu.VMEM((t, fc), F32),
                        pltpu.VMEM((t + HALO3, fc), F32), pltpu.VMEM((t + HALO3, fc), F32),
                        pltpu.VMEM((SHORT_K * SUB, fc), F32), pltpu.VMEM((SHORT_K * SUB, fc), F32),
                        pltpu.VMEM((SHORT_K * SUB, fc), F32), pltpu.VMEM((SHORT_K * SUB, fc), F32)],
        compiler_params=_params("arbitrary", "arbitrary"), name=name,
    )(dx2, uf, uf, cg, cv, wf, wf, w_down, uf if dep is None else dep)


def _mix_bwd(dx1, u, ca, cb, wa, lg, lb, wb, w_out, *, name):
    s_len, d_in = u.shape
    d = dx1.shape[1]
    c = D_CONF
    t = _seq_tile(s_len)
    n_t = s_len // t

    def body(dx_ref, u_ref, ca_ref, cb_ref, wa_ref, lg_ref, lb_ref, wb_ref, wo_ref,
             du_ref, dwa_ref, dwb_ref, dba_ref, dlg_ref, dlb_ref, dbin_ref,
             glu, prod, dyc, dcaw, dcbw, dglu, dp, awa, awb, wrep_a, wrep_b, shf):
        i = pl.program_id(0)
        dyc[...] = _dot_nt(dx_ref[...], wo_ref[...])
        _replicate_taps(wa_ref, wrep_a, CONF_K)
        _replicate_taps(wb_ref, wrep_b, SHORT_K)

        @pl.when(i == 0)
        def _():
            dcaw[t:t + HALO, :] = jnp.zeros((HALO, c), F32)
            dcbw[t:t + HALO3, :] = jnp.zeros((HALO3, c), F32)
            awa[...] = jnp.zeros_like(awa)
            awb[...] = jnp.zeros_like(awb)
            dba_ref[...] = jnp.zeros_like(dba_ref)
            dlg_ref[...] = jnp.zeros_like(dlg_ref)
            dlb_ref[...] = jnp.zeros_like(dlb_ref)
            dbin_ref[...] = jnp.zeros_like(dbin_ref)

        def blk1(rows):
            cv = ca_ref[rows, :]
            mu = jnp.mean(cv, axis=-1, keepdims=True)
            xc = cv - mu
            rstd = lax.rsqrt(jnp.mean(xc * xc, axis=-1, keepdims=True) + EPS)
            nrm = xc * rstd
            ln = nrm * lg_ref[...] + lb_ref[...]
            sg = _sigmoid(ln)
            dln = dyc[rows, 0:c] * (sg * (1.0 + ln * (1.0 - sg)))
            dlg_ref[...] += _rows8(dln * nrm)
            dlb_ref[...] += _rows8(dln)
            dn = dln * lg_ref[...]
            dca = rstd * (dn - jnp.mean(dn, axis=-1, keepdims=True)
                          - nrm * jnp.mean(dn * nrm, axis=-1, keepdims=True))
            dcaw[rows, :] = dca
            dba_ref[...] += _rows8(dca)
            ds = dyc[rows, c:2 * c]
            dgb = ds * cb_ref[rows, :]
            dcbw[rows, :] = ds * u_ref[rows, 2 * c:3 * c].astype(F32)
            du_ref[rows, 2 * c:3 * c] = dgb.astype(BF16)
            dbin_ref[:, 2 * c:3 * c] += _rows8(dgb)
            glu[rows, :] = u_ref[rows, 0:c].astype(F32) * _sigmoid(u_ref[rows, c:2 * c].astype(F32))
            prod[rows, :] = u_ref[rows, 3 * c:4 * c].astype(F32) * u_ref[rows, 4 * c:5 * c].astype(F32)
        _row_loop(t, 64, blk1, unroll=2)

        _conv_bwd_taps(dcaw, wrep_a, glu, dglu, awa, taps=CONF_K, n_rows=t, width=c, shf=shf)
        _conv_bwd_taps(dcbw, wrep_b, prod, dp, awb, taps=SHORT_K, n_rows=t, width=c)
        dcaw[t:t + HALO, :] = dcaw[0:HALO, :]
        dcbw[t:t + HALO3, :] = dcbw[0:HALO3, :]

        def blk2(rows):
            av = u_ref[rows, 0:c].astype(F32)
            sg = _sigmoid(u_ref[rows, c:2 * c].astype(F32))
            dg = dglu[rows, :]
            d_av = dg * sg
            d_ag = (dg * av) * (sg * (1.0 - sg))
            dpv = dp[rows, :]
            d_gc = dpv * u_ref[rows, 4 * c:5 * c].astype(F32)
            d_vs = dpv * u_ref[rows, 3 * c:4 * c].astype(F32)
            du_ref[rows, 0:c] = d_av.astype(BF16)
            du_ref[rows, c:2 * c] = d_ag.astype(BF16)
            du_ref[rows, 3 * c:4 * c] = d_gc.astype(BF16)
            du_ref[rows, 4 * c:5 * c] = d_vs.astype(BF16)
            dbin_ref[:, 0:c] += _rows8(d_av)
            dbin_ref[:, c:2 * c] += _rows8(d_ag)
            dbin_ref[:, 3 * c:4 * c] += _rows8(d_gc)
            dbin_ref[:, 4 * c:5 * c] += _rows8(d_vs)
        _row_loop(t, 64, blk2)

        @pl.when(i == n_t - 1)
        def _():
            dwa_ref[...] = _fold8(awa, CONF_K)
            dwb_ref[...] = _fold8(awb, SHORT_K)

    rev = lambda i: n_t - 1 - i
    small_in = lambda r: _resident((r, c), lambda i: (0, 0))
    small = lambda r: pl.BlockSpec((r, c), lambda i: (0, 0))
    return pl.pallas_call(
        body, grid=(n_t,),
        in_specs=[pl.BlockSpec((t, d), lambda i: (rev(i), 0)),
                  pl.BlockSpec((t, d_in), lambda i: (rev(i), 0)),
                  pl.BlockSpec((t, c), lambda i: (rev(i), 0)), pl.BlockSpec((t, c), lambda i: (rev(i), 0)),
                  small_in(CONF_K), small_in(1), small_in(1), small_in(SHORT_K),
                  _resident((2 * c, d), lambda i: (0, 0))],
        out_specs=[pl.BlockSpec((t, d_in), lambda i: (rev(i), 0)),
                   small(CONF_K), small(SHORT_K), small(SUB), small(SUB), small(SUB),
                   pl.BlockSpec((SUB, d_in), lambda i: (0, 0))],
        out_shape=[jax.ShapeDtypeStruct((s_len, d_in), BF16),
                   jax.ShapeDtypeStruct((CONF_K, c), F32), jax.ShapeDtypeStruct((SHORT_K, c), F32),
                   jax.ShapeDtypeStruct((SUB, c), F32), jax.ShapeDtypeStruct((SUB, c), F32),
                   jax.ShapeDtypeStruct((SUB, c), F32), jax.ShapeDtypeStruct((SUB, d_in), F32)],
        scratch_shapes=[pltpu.VMEM((t, c), F32), pltpu.VMEM((t, c), F32), pltpu.VMEM((t, 2 * c), F32),
                        pltpu.VMEM((t + HALO, c), F32), pltpu.VMEM((t + HALO3, c), F32),
                        pltpu.VMEM((t, c), F32), pltpu.VMEM((t, c), F32),
                        pltpu.VMEM((CONF_K * SUB, c), F32), pltpu.VMEM((SHORT_K * SUB, c), F32),
                        pltpu.VMEM((CONF_K * SUB, c), F32), pltpu.VMEM((SHORT_K * SUB, c), F32),
                        pltpu.VMEM((SUB - 1, t + HALO, c), F32)],
        compiler_params=_params("arbitrary"), name=name,
    )(dx1, u, ca, cb, wa, lg, lb, wb, w_out)


def _matmul_tn(a, b, *, name, into=None, part=0, n_parts=1):
    s_len, k = a.shape
    n = b.shape[1]
    tk = _col_tile(k)
    per = k // tk

    def body(*refs):
        a_ref, b_ref = refs[0], refs[1]
        o_ref = refs[-1]
        o_ref[...] = _dot_tn(a_ref[...], b_ref[...]).astype(BF16)

    in_specs = [pl.BlockSpec((s_len, tk), lambda j: (0, j)), _resident((s_len, n), lambda j: (0, 0))]
    args = [a, b]
    aliases = {}
    if into is not None:
        in_specs.append(ANY)
        args.append(into)
        aliases = {2: 0}
    return pl.pallas_call(
        body, grid=(per,), in_specs=in_specs,
        out_specs=pl.BlockSpec((tk, n), lambda j: (part * per + j, 0)),
        out_shape=jax.ShapeDtypeStruct((n_parts * k, n), BF16),
        input_output_aliases=aliases,
        compiler_params=_params("parallel"), name=name,
    )(*args)


def _matmul_rmsbwd(dzs, wt, x, g, dx_in, *, name, dep=None):
    s_len, d = x.shape
    n_z = len(dzs)
    nj = dzs[0].shape[1]
    t = _mm_tile(s_len)

    def body(*refs):
        dz_refs = refs[0:n_z]
        w_refs = refs[n_z:2 * n_z]
        x_ref, g_ref, dxi_ref, _, dx_ref, dxb_ref, dg_ref, dh = refs[2 * n_z:]

        @pl.when(pl.program_id(0) == 0)
        def _():
            dg_ref[...] = jnp.zeros_like(dg_ref)

        def blk(rows):
            xv = x_ref[rows, :]
            r = lax.rsqrt(jnp.mean(xv * xv, axis=-1, keepdims=True) + EPS)
            xn = xv * r
            dhv = dh[rows, :]
            dg_ref[...] += _rows8(dhv * xn)
            dn = dhv * g_ref[...]
            dx = dxi_ref[rows, :] + r * (dn - xn * jnp.mean(dn * xn, axis=-1, keepdims=True))
            dx_ref[rows, :] = dx
            dxb_ref[rows, :] = dx.astype(BF16)

        half = t // 2
        rb = min(128, half)
        for lo in range(0, t, half):
            acc = _dot(dz_refs[0][lo:lo + half, :], w_refs[0][...])
            for q in range(1, n_z):
                acc = acc + _dot(dz_refs[q][lo:lo + half, :], w_refs[q][...])
            dh[lo:lo + half, :] = acc
            for r0 in range(lo, lo + half, rb):
                blk(pl.ds(r0, rb))

    row = pl.BlockSpec((t, d), lambda i: (i, 0))
    in_specs = [pl.BlockSpec((t, nj), lambda i: (i, 0)) for _ in range(n_z)]
    in_specs += [_resident((nj, d), functools.partial(lambda q, i: (q, 0), q)) for q in range(n_z)]
    in_specs += [row, _resident((1, d), lambda i: (0, 0)), row, ANY]
    return pl.pallas_call(
        body, grid=(s_len // t,), in_specs=in_specs,
        out_specs=[row, row, pl.BlockSpec((SUB, d), lambda i: (0, 0))],
        out_shape=[jax.ShapeDtypeStruct((s_len, d), F32), jax.ShapeDtypeStruct((s_len, d), BF16),
                   jax.ShapeDtypeStruct((SUB, d), F32)],
        scratch_shapes=[pltpu.VMEM((t, d), F32)],
        compiler_params=_params("arbitrary"), name=name,
    )(*dzs, *([wt] * n_z), x, g, dx_in, x if dep is None else dep)


def _row(v):
    return v.reshape(1, -1)


def _layer_fwd(x0, p, tag, dep=None, before_up=None):
    u, h1 = _rms_matmul(x0, _row(p["mix_norm_g"]), p["w_in_t"], _row(p["b_in"]), name=f"in_proj_{tag}", dep=dep)
    ycat, x1, ca, cb = _mix_fwd(u, x0, p["conv_a_w"], _row(p["conv_a_b"]), _row(p["ln_a_g"]), _row(p["ln_a_b"]),
                            p["conv_b_w"], p["w_out"], name=f"mix_fwd_{tag}")
    if before_up is not None:
        before_up(x1)
    uf, h2 = _rms_matmul(x1, _row(p["ffn_norm_g"]), p["w_up_t"], None, name=f"up_proj_{tag}")
    act, x2, cg, cv = _ffn_fwd(uf, x1, p["conv_f_w"], p["w_down"], name=f"ffn_fwd_{tag}")
    return x2, dict(x0=x0, h1=h1, u=u, ca=ca, cb=cb, ycat=ycat, x1=x1, h2=h2, uf=uf, cg=cg, cv=cv, act=act)


def _layer_bwd(dx2, dx2_b, p, saved, tag, after_ffn, after_mix, dep=None):
    dug, duv, dwf_g, dwf_v = _ffn_bwd(dx2_b, saved["uf"], saved["cg"], saved["cv"], p["conv_f_w"], p["w_down"],
                                      name=f"ffn_bwd_{tag}", dep=dep)
    g_down = _matmul_tn(saved["act"], dx2_b, name=f"dw_down_{tag}")
    g_up = _matmul_tn(dug, saved["h2"], name=f"dw_up_g_{tag}", n_parts=2)
    g_up = _matmul_tn(duv, saved["h2"], name=f"dw_up_v_{tag}", into=g_up, part=1, n_parts=2)
    dep_ffn = after_ffn(dict(w_up=g_up, w_down=g_down))
    dx1, dx1_b, dg2 = _matmul_rmsbwd([dug, duv], p["w_up_t"], saved["x1"], _row(p["ffn_norm_g"]), dx2,
                                     name=f"dh_ffn_{tag}", dep=dep_ffn)
    du, dwa, dwb, dba, dlg, dlb, dbin = _mix_bwd(
        dx1_b, saved["u"], saved["ca"], saved["cb"], p["conv_a_w"], _row(p["ln_a_g"]), _row(p["ln_a_b"]),
        p["conv_b_w"], p["w_out"], name=f"mix_bwd_{tag}")
    g_out = _matmul_tn(saved["ycat"], dx1_b, name=f"dw_out_{tag}")
    g_in = _matmul_tn(du, saved["h1"], name=f"dw_in_{tag}")
    conv = dict(conv_a_w=dwa, conv_b_w=dwb, conv_f_w=jnp.concatenate([dwf_g, dwf_v], axis=1))
    dep_mix = after_mix(dict(w_in=g_in, w_out=g_out), conv)
    dx0, dx0_b, dg1 = _matmul_rmsbwd([du], p["w_in_t"], saved["x0"], _row(p["mix_norm_g"]), dx1,
                                     name=f"dh_mix_{tag}", dep=dep_mix)
    rep = dict(mix_norm_g=dg1, b_in=dbin, conv_a_b=dba, ln_a_g=dlg, ln_a_b=dlb, ffn_norm_g=dg2)
    return dx0, dx0_b, rep


def _place():
    return lax.axis_index("x"), lax.axis_index("y"), lax.axis_index("c")


def _all_gather(arrs, *, name):
    n_a = len(arrs)

    def body(*refs):
        ins = refs[0:n_a]
        outs = refs[n_a:2 * n_a]
        send_sems, recv_sems, local_sems = refs[2 * n_a:]
        x, y, c = _place()
        sibling = (x, y, 1 - c)
        chips = [(1 - x, y), (x, 1 - y), (1 - x, 1 - y)]

        def slot(a, px, py, pc):
            return outs[a].at[4 * px + 2 * py + pc]

        def copy(a, k, block, to, src=None):
            return pltpu.make_async_remote_copy(
                src_ref=slot(a, *block) if src is None else src, dst_ref=slot(a, *block),
                send_sem=send_sems.at[a, k], recv_sem=recv_sems.at[a, k],
                device_id=to, device_id_type=MESH)

        me = (x, y, c)
        mine = [pltpu.make_async_copy(ins[a], slot(a, *me), local_sems.at[a]) for a in range(n_a)]
        for cp in mine:
            cp.start()
        started = []
        for a in range(n_a):
            first = [copy(a, 0, me, sibling, src=ins[a])]
            first += [copy(a, 1 + j, me, (*chip, c), src=ins[a]) for j, chip in enumerate(chips)]
            for cp in first:
                cp.start()
            started += first
        for a in range(n_a):
            for j, chip in enumerate(chips):
                copy(a, 1 + j, (*chip, c), me).wait_recv()
                passed = copy(a, 4 + j, (*chip, c), sibling)
                passed.start()
                started.append(passed)
        for a in range(n_a):
            copy(a, 0, sibling, me).wait_recv()
            for j, chip in enumerate(chips):
                copy(a, 4 + j, (*chip, 1 - c), me).wait_recv()
        for cp in started:
            cp.wait_send()
        for cp in mine:
            cp.wait()

    return pl.pallas_call(
        body, in_specs=[ANY] * n_a, out_specs=[ANY] * n_a,
        out_shape=[jax.ShapeDtypeStruct((N_DEV, *a.shape), a.dtype) for a in arrs],
        scratch_shapes=[pltpu.SemaphoreType.DMA((n_a, 7)), pltpu.SemaphoreType.DMA((n_a, 7)),
                        pltpu.SemaphoreType.DMA((n_a,))],
        name=name,
    )(*arrs)


def _sibling_exchange(arrs, *, name):
    n_a = len(arrs)

    def body(*refs):
        ins = refs[0:n_a]
        outs = refs[n_a:2 * n_a]
        send_sems, recv_sems = refs[2 * n_a:]
        x, y, c = _place()
        copies = [pltpu.make_async_remote_copy(
            src_ref=ins[a].at[:, 1 - c], dst_ref=outs[a], send_sem=send_sems.at[a], recv_sem=recv_sems.at[a],
            device_id=(x, y, 1 - c), device_id_type=MESH) for a in range(n_a)]
        for cp in copies:
            cp.start()
        for cp in copies:
            cp.wait()

    return pl.pallas_call(
        body, in_specs=[ANY] * n_a, out_specs=[ANY] * n_a,
        out_shape=[jax.ShapeDtypeStruct((N_CHIP, *a.shape[2:]), a.dtype) for a in arrs],
        scratch_shapes=[pltpu.SemaphoreType.DMA((n_a,)), pltpu.SemaphoreType.DMA((n_a,))],
        name=name,
    )(*arrs)


def _row_tile(r, cap):
    for tr in range(min(cap, r) // 16 * 16, 0, -16):
        if r % tr == 0:
            return tr
    return r


def _pair_sum(mines, theirs, where, *, name):
    n_a = len(mines)
    n_chip = mines[0].shape[0]

    def body(where_ref, *refs):
        a_refs = refs[0:n_a]
        b_refs = refs[n_a:2 * n_a]
        p_refs = refs[2 * n_a:3 * n_a]
        l_refs = refs[3 * n_a:4 * n_a]
        q = pl.program_id(0)
        for a in range(n_a):
            p_refs[a][...] = (a_refs[a][...].astype(F32) + b_refs[a][...].astype(F32)).astype(p_refs[a].dtype)

        @pl.when(q == where_ref[1])
        def _():
            for a in range(n_a):
                l_refs[a][...] = p_refs[a][...]

    in_specs, out_p, out_l, shapes = [], [], [], []
    for m in mines:
        _, _, r, c = m.shape
        in_specs.append(pl.BlockSpec((None, None, r, c), lambda q, where_ref: (q, where_ref[0], 0, 0)))
    for m in mines:
        _, _, r, c = m.shape
        in_specs.append(pl.BlockSpec((None, r, c), lambda q, where_ref: (q, 0, 0)))
        out_p.append(pl.BlockSpec((None, r, c), lambda q, where_ref: (q, 0, 0)))
        out_l.append(pl.BlockSpec((None, r, c), lambda q, where_ref: (where_ref[1], 0, 0)))
        shapes.append(jax.ShapeDtypeStruct((n_chip, r, c), m.dtype))
    res = pl.pallas_call(
        body,
        grid_spec=pltpu.PrefetchScalarGridSpec(num_scalar_prefetch=1, grid=(n_chip,), in_specs=in_specs,
                                               out_specs=out_p + out_l),
        out_shape=shapes + shapes,
        compiler_params=_params("arbitrary"), name=name,
    )(where, *mines, *theirs)
    return list(res[:n_a]), list(res[n_a:])


HBM = pl.BlockSpec(memory_space=pltpu.HBM)
SEM = pl.BlockSpec(memory_space=pltpu.SEMAPHORE)
EFFECT = pltpu.SideEffectType.DATAFLOW_SIDE_EFFECTING


def _in_hbm(a):
    return pltpu.with_memory_space_constraint(a, pltpu.HBM)


def _split_start(srcs, lands, plan, n_copies, after, *, name):
    n_s, n_l = len(srcs), len(lands)

    def body(*refs):
        src_refs = refs[0:n_s]
        land_refs = refs[n_s:n_s + n_l]
        send_sems, recv_sems = refs[n_s + n_l + 1], refs[n_s + n_l + 2]
        token = refs[-1]
        for cp in plan(src_refs, land_refs, send_sems, recv_sems):
            cp.start()
        token[...] = jnp.zeros_like(token)

    thru = [pltpu.HBM(a.shape, a.dtype) for a in list(srcs) + list(lands)]
    res = pl.pallas_call(
        body, name=name,
        out_shape=(pltpu.SemaphoreType.DMA((n_copies,)), pltpu.SemaphoreType.DMA((n_copies,)), *thru,
                   jax.ShapeDtypeStruct((SUB, LANES), F32)),
        in_specs=[HBM] * (n_s + n_l) + [ANY],
        out_specs=(SEM, SEM, *([HBM] * (n_s + n_l)), pl.BlockSpec(memory_space=pltpu.VMEM)),
        input_output_aliases={i: 2 + i for i in range(n_s + n_l)},
        compiler_params=pltpu.CompilerParams(has_side_effects=EFFECT),
    )(*[_in_hbm(a) for a in srcs], *[_in_hbm(a) for a in lands], _in_hbm(after))
    return res[0], res[1], list(res[2:2 + n_s]), list(res[2 + n_s:2 + n_s + n_l]), res[-1]


def _split_wait(send_sems, recv_sems, srcs, lands, after, plan, *, name):
    n_s, n_l = len(srcs), len(lands)

    def body(*refs):
        src_refs = refs[0:n_s]
        land_refs = refs[n_s:n_s + n_l]
        send, recv = refs[n_s + n_l], refs[n_s + n_l + 1]
        for cp in plan(src_refs, land_refs, send, recv):
            cp.wait_send()
            cp.wait_recv()

    res = pl.pallas_call(
        body, name=name,
        out_shape=tuple(pltpu.HBM(a.shape, a.dtype) for a in list(srcs) + list(lands)),
        in_specs=[HBM] * (n_s + n_l) + [SEM, SEM, ANY],
        out_specs=tuple([HBM] * (n_s + n_l)),
        input_output_aliases={i: i for i in range(n_s + n_l)},
        compiler_params=pltpu.CompilerParams(has_side_effects=EFFECT),
    )(*srcs, *lands, send_sems, recv_sems, _in_hbm(after))
    return list(res[n_s:])


def _remote(src, dst, send_sems, recv_sems, k, to):
    return pltpu.make_async_remote_copy(src_ref=src, dst_ref=dst, send_sem=send_sems.at[k], recv_sem=recv_sems.at[k],
                                        device_id=to, device_id_type=MESH)


def _gather_plan_first(src_refs, land_refs, send_sems, recv_sems):
    x, y, c = _place()
    me = 4 * x + 2 * y + c
    peers = [(x, y, 1 - c), (1 - x, y, c), (x, 1 - y, c), (1 - x, 1 - y, c)]
    return [_remote(src, land.at[me], send_sems, recv_sems, 4 * a + k, to)
            for a, (src, land) in enumerate(zip(src_refs, land_refs)) for k, to in enumerate(peers)]


def _gather_plan_second(src_refs, land_refs, send_sems, recv_sems):
    x, y, c = _place()
    chips = [(1 - x, y), (x, 1 - y), (1 - x, 1 - y)]
    out = []
    for a, land in enumerate(land_refs):
        for j, (px, py) in enumerate(chips):
            slot = land.at[4 * px + 2 * py + c]
            out.append(_remote(slot, slot, send_sems, recv_sems, 3 * a + j, (x, y, 1 - c)))
    return out


def _chips_plan(src_refs, land_refs, send_sems, recv_sems):
    x, y, c = _place()
    my_chip = 2 * x + y
    chips = [(1 - x, y), (x, 1 - y), (1 - x, 1 - y)]
    return [_remote(src.at[2 * px + py], land.at[my_chip], send_sems, recv_sems, 3 * a + j, (px, py, c))
            for a, (src, land) in enumerate(zip(src_refs, land_refs)) for j, (px, py) in enumerate(chips)]


def _gather_landings(shards, *, name):
    n_a = len(shards)

    def body(*refs):
        ins = refs[0:n_a]
        outs = refs[n_a:2 * n_a]
        sems = refs[2 * n_a]
        x, y, c = _place()
        copies = [pltpu.make_async_copy(ins[a], outs[a].at[4 * x + 2 * y + c], sems.at[a]) for a in range(n_a)]
        for cp in copies:
            cp.start()
        for cp in copies:
            cp.wait()

    return pl.pallas_call(
        body, in_specs=[ANY] * n_a, out_specs=[ANY] * n_a,
        out_shape=[jax.ShapeDtypeStruct((N_DEV, *s.shape), s.dtype) for s in shards],
        scratch_shapes=[pltpu.SemaphoreType.DMA((n_a,))], name=name,
    )(*shards)


def _adamw_math(g, w, m, v):
    m = ADAM_B1 * m + (1.0 - ADAM_B1) * g
    v = ADAM_B2 * v + (1.0 - ADAM_B2) * (g * g)
    m_hat = m / (1.0 - ADAM_B1 ** ADAM_STEP)
    v_hat = v / (1.0 - ADAM_B2 ** ADAM_STEP)
    delta = -ADAM_LR * (m_hat / (jnp.sqrt(v_hat) + ADAM_EPS) + ADAM_WD * w)
    return delta, m, v


def _adamw_sharded(parts, w, m, v, *, name, dep=None):
    n_layers, r, c = w.shape
    n_chip = parts[0].shape[0]
    tr = _row_tile(r, 384)
    n_i = r // tr

    def body(*refs):
        p_refs = refs[0:n_layers]
        w_ref, m_ref, v_ref, _, g_out, d_out, m_out, v_out = refs[n_layers:]
        layer = pl.program_id(0)
        for l in range(n_layers):
            @pl.when(layer == l)
            def _(l=l):
                g = p_refs[l][0].astype(F32)
                for q in range(1, n_chip):
                    g = g + p_refs[l][q].astype(F32)
                delta, m_new, v_new = _adamw_math(g, w_ref[...], m_ref[...], v_ref[...])
                g_out[...] = g
                d_out[...] = delta
                m_out[...] = m_new
                v_out[...] = v_new

    def part_map(l):
        return lambda layer, i: (0, jnp.where(layer == l, i, jnp.where(layer < l, 0, n_i - 1)), 0)

    blk = pl.BlockSpec((None, tr, c), lambda layer, i: (layer, i, 0))
    return pl.pallas_call(
        body, grid=(n_layers, n_i),
        in_specs=[pl.BlockSpec((n_chip, tr, c), part_map(l)) for l in range(n_layers)] + [blk, blk, blk, ANY],
        out_specs=[blk] * 4, out_shape=[jax.ShapeDtypeStruct((n_layers, r, c), F32)] * 4,
        compiler_params=_params("arbitrary", "arbitrary"), name=name,
    )(*parts, w, m, v, w if dep is None else dep)


def _fold_partials(cols, *, name):
    widths = [c.shape[1] for c in cols]

    def body(*refs):
        o_ref = refs[-1]
        pos = 0
        for ref, width in zip(refs[:-1], widths):
            o_ref[:, pos:pos + width] = jnp.sum(ref[...], axis=0, keepdims=True)
            pos += width

    return pl.pallas_call(body, out_shape=jax.ShapeDtypeStruct((1, sum(widths)), F32), name=name)(*cols)


def _adamw_replicated(parts, names, w, m, v, n_loss, *, name):
    n_dev = parts.shape[0]
    n_layers = w[names[0]].shape[0]
    every = list(names) + ["final_norm_g"]
    n_p = len(every)

    def body(*refs):
        p_ref = refs[0]
        w_refs = dict(zip(every, refs[1:1 + n_p]))
        m_refs = dict(zip(every, refs[1 + n_p:1 + 2 * n_p]))
        v_refs = dict(zip(every, refs[1 + 2 * n_p:1 + 3 * n_p]))
        l_out = refs[1 + 3 * n_p]
        outs = refs[2 + 3 * n_p:]
        o_refs = {n: outs[4 * q:4 * q + 4] for q, n in enumerate(every)}
        acc = p_ref[0]
        for q in range(1, n_dev):
            acc = acc + p_ref[q]
        tot = jnp.sum(acc, axis=0, keepdims=True)
        pos = 0
        where = [(n, l) for l in range(n_layers) for n in names] + [("final_norm_g", 0)]
        for n, l in where:
            width = w_refs[n].shape[1]
            g = tot[:, pos:pos + width]
            pos += width
            row = pl.ds(l, 1)
            delta, m_new, v_new = _adamw_math(g, w_refs[n][row, :], m_refs[n][row, :], v_refs[n][row, :])
            for o, val in zip(o_refs[n], (g, delta, m_new, v_new)):
                o[row, :] = val
        l_out[...] = (0.5 / n_loss) * jnp.sum(tot[:, pos:pos + n_loss], axis=-1, keepdims=True)

    shapes = [jax.ShapeDtypeStruct((1, 1), F32)]
    for n in every:
        shapes += [jax.ShapeDtypeStruct(w[n].shape, F32)] * 4
    res = pl.pallas_call(
        body, out_shape=shapes,
        compiler_params=pltpu.CompilerParams(vmem_limit_bytes=VMEM_LIMIT), name=name,
    )(parts, *[w[n] for n in every], *[m[n] for n in every], *[v[n] for n in every])
    return res[0], {n: res[1 + 4 * q:5 + 4 * q] for q, n in enumerate(every)}


BIG = ("w_in", "w_out", "w_up", "w_down")
COL_SHARDED = ("w_in", "w_up")
CONV = ("conv_a_w", "conv_b_w", "conv_f_w")
REPLICATED = ("mix_norm_g", "b_in", "conv_a_b", "ln_a_g", "ln_a_b", "ffn_norm_g")
KINDS = ("grad", "delta", "m", "v")
FFN_PART = ("w_up", "w_down")
MIX_PART = ("w_in", "w_out")


def _weights_from_gathered(g):
    n_dev, r, c = g.shape
    return g.reshape(n_dev * r, c)


def _slabs_from_full(grad):
    return grad.reshape(N_DEV, grad.shape[0] // N_DEV, grad.shape[1])


def _pair_sums(slabs, where, tag):
    slabs = [s.reshape(N_CHIP, 2, *s.shape[1:]) for s in slabs]
    theirs = _sibling_exchange(slabs, name=f"reduce_siblings_{tag}")
    return _pair_sum(slabs, theirs, where, name=f"pair_sum_{tag}")


def kernel(x, mix_norm_g, w_in, b_in, conv_a_w, conv_a_b, ln_a_g, ln_a_b, conv_b_w, w_out, ffn_norm_g, w_up, conv_f_w, w_down, final_norm_g, loss_target, m_mix_norm_g, m_w_in, m_b_in, m_conv_a_w, m_conv_a_b, m_ln_a_g, m_ln_a_b, m_conv_b_w, m_w_out, m_ffn_norm_g, m_w_up, m_conv_f_w, m_w_down, m_final_norm_g, v_mix_norm_g, v_w_in, v_b_in, v_conv_a_w, v_conv_a_b, v_ln_a_g, v_ln_a_b, v_conv_b_w, v_w_out, v_ffn_norm_g, v_w_up, v_conv_f_w, v_w_down, v_final_norm_g):
    w = dict(mix_norm_g=mix_norm_g, w_in=w_in, b_in=b_in, conv_a_w=conv_a_w, conv_a_b=conv_a_b, ln_a_g=ln_a_g,
             ln_a_b=ln_a_b, conv_b_w=conv_b_w, w_out=w_out, ffn_norm_g=ffn_norm_g, w_up=w_up, conv_f_w=conv_f_w,
             w_down=w_down, final_norm_g=final_norm_g)
    m = dict(mix_norm_g=m_mix_norm_g, w_in=m_w_in, b_in=m_b_in, conv_a_w=m_conv_a_w, conv_a_b=m_conv_a_b,
             ln_a_g=m_ln_a_g, ln_a_b=m_ln_a_b, conv_b_w=m_conv_b_w, w_out=m_w_out, ffn_norm_g=m_ffn_norm_g,
             w_up=m_w_up, conv_f_w=m_conv_f_w, w_down=m_w_down, final_norm_g=m_final_norm_g)
    v = dict(mix_norm_g=v_mix_norm_g, w_in=v_w_in, b_in=v_b_in, conv_a_w=v_conv_a_w, conv_a_b=v_conv_a_b,
             ln_a_g=v_ln_a_g, ln_a_b=v_ln_a_b, conv_b_w=v_conv_b_w, w_out=v_w_out, ffn_norm_g=v_ffn_norm_g,
             w_up=v_w_up, conv_f_w=v_conv_f_w, w_down=v_w_down, final_norm_g=v_final_norm_g)
    order = list(w)
    n_layers = w_in.shape[0]
    xs = x[0]
    target = loss_target[0]
    flip = lambda a: jnp.transpose(a, (0, 2, 1))
    wt, mt, vt = ({n: flip(d[n]) if n in COL_SHARDED else d[n] for n in BIG} for d in (w, m, v))
    px, py, pc = _place()
    where = jnp.stack([pc, 2 * px + py]).astype(jnp.int32)

    assert BIG == MIX_PART + FFN_PART
    key = lambda n: n + "_t" if n in COL_SHARDED else n
    shard = lambda n, l: wt[n][l].astype(BF16)

    def gather_start(names, l, after, tag):
        shards = [shard(n, l) for n in names]
        lands = _gather_landings(shards, name=f"gather_landings_{tag}")
        return _split_start(shards, lands, _gather_plan_first, 4 * len(shards), after, name=f"gather_first_start_{tag}")

    def gather_mid(first, after, tag):
        return _split_wait(first[0], first[1], first[2], first[3], after, _gather_plan_first,
                           name=f"gather_first_wait_{tag}")

    def forward_start(lands, after, tag):
        return _split_start([], lands, _gather_plan_second, 3 * len(lands), after, name=f"gather_second_start_{tag}")

    def forward_finish(second, after, tag):
        return _split_wait(second[0], second[1], [], second[3], after, _gather_plan_second,
                           name=f"gather_second_wait_{tag}")

    gathered = _all_gather([shard(n, 0) for n in MIX_PART] + [w[n] for n in CONV], name="gather_weights_0")
    params = [{n: w[n][l] for n in REPLICATED} for l in range(n_layers)]
    for n, g in zip(CONV, gathered[len(MIX_PART):]):
        n_dev, _, taps, c = g.shape
        full = g.transpose(1, 2, 0, 3).reshape(n_layers, taps, n_dev * c)
        for l in range(n_layers):
            params[l][n] = full[l]
    for n, g in zip(MIX_PART, gathered):
        params[0][key(n)] = _weights_from_gathered(g)
    ffn_first = gather_start(FFN_PART, 0, gathered[0], "0_ffn")
    pending = {}

    h = xs
    saved = []
    for l in range(n_layers):
        nxt = l + 1 if l + 1 < n_layers else None

        def before_up(x1, l=l, nxt=nxt):
            if l == 0:
                second = forward_start(gather_mid(ffn_first, x1, "0_ffn"), x1, "0_ffn")
                after = second[4]
            else:
                second = pending[l]["ffn"]
                after = x1
            if nxt is not None:
                pending[nxt] = dict(first=gather_start(BIG, nxt, after, str(nxt)))
                after = pending[nxt]["first"][4]
            for n, g in zip(FFN_PART, forward_finish(second, after, f"{l}_ffn")):
                params[l][key(n)] = _weights_from_gathered(g)

        h, keep = _layer_fwd(h, params[l], str(l), dep=ffn_first[4] if l == 0 else None, before_up=before_up)
        saved.append(keep)
        if nxt is not None:
            arrived = gather_mid(pending[nxt]["first"], h, str(nxt))
            mix_second = forward_start(arrived[:len(MIX_PART)], h, f"{nxt}_mix")
            pending[nxt]["ffn"] = forward_start(arrived[len(MIX_PART):], mix_second[4], f"{nxt}_ffn")
            for n, g in zip(MIX_PART, forward_finish(mix_second, pending[nxt]["ffn"][4], f"{nxt}_mix")):
                params[nxt][key(n)] = _weights_from_gathered(g)

    def start_reduce(slabs, tag):
        pairs, lands = _pair_sums(slabs, where, tag)
        return _split_start(pairs, lands, _chips_plan, 3 * len(pairs), pairs[0], name=f"reduce_chips_start_{tag}")

    def finish_reduce(fly, after, tag):
        return _split_wait(fly[0], fly[1], fly[2], fly[3], after, _chips_plan, name=f"reduce_chips_wait_{tag}")

    loss_sq, dh, dh_b, dgf = _loss_bwd(h, _row(final_norm_g), target, name="loss")
    conv_g = {n: [None] * n_layers for n in CONV}
    rep_g = [None] * n_layers
    flights = {}
    token = None
    for l in reversed(range(n_layers)):
        def after_ffn(g, l=l):
            flights[l, "ffn"] = start_reduce([_slabs_from_full(g[n]) for n in FFN_PART], f"{l}_ffn")
            return flights[l, "ffn"][4]

        def after_mix(g, conv, l=l):
            for n in CONV:
                conv_g[n][l] = conv[n]
            slabs = [_slabs_from_full(g[n]) for n in MIX_PART]
            if l == 0:
                for n in CONV:
                    full = jnp.stack(conv_g[n])
                    _, taps, c = full.shape
                    slabs.append(full.reshape(n_layers, taps, N_DEV, c // N_DEV).transpose(2, 0, 1, 3)
                                 .reshape(N_DEV, n_layers * taps, c // N_DEV))
            flights[l, "mix"] = start_reduce(slabs, f"{l}_mix")
            return flights[l, "mix"][4]

        dh, dh_b, rep_g[l] = _layer_bwd(dh, dh_b, params[l], saved[l], str(l), after_ffn, after_mix, dep=token)
        token = flights[l, "mix"][4]

    sums = {key: finish_reduce(fly, dh, f"{key[0]}_{key[1]}") for key, fly in flights.items() if key != (0, "mix")}
    out = {k: {} for k in KINDS}

    def adamw_big(names, part, dep):
        for q, n in enumerate(names):
            layer_parts = [sums[l, part][q] for l in range(n_layers)]
            res = _adamw_sharded(layer_parts, wt[n], mt[n], vt[n], name=f"adamw_{n}", dep=dep)
            for k, r in zip(KINDS, res):
                out[k][n] = flip(r) if n in COL_SHARDED else r

    adamw_big(FFN_PART, "ffn", token)

    rep_cols = [rep_g[l][n] for l in range(n_layers) for n in REPLICATED] + [dgf, loss_sq]
    rep_all = _all_gather([_fold_partials(rep_cols, name="fold_small")], name="gather_small")[0]
    with_final = lambda d: {**{n: d[n] for n in REPLICATED}, "final_norm_g": _row(d["final_norm_g"])}
    loss, rep_res = _adamw_replicated(rep_all, REPLICATED, with_final(w), with_final(m), with_final(v),
                                      loss_sq.shape[1], name="adamw_small")
    for n, res in rep_res.items():
        for k, r in zip(KINDS, res):
            out[k][n] = r.reshape(w[n].shape)

    last = finish_reduce(flights[0, "mix"], rep_res["b_in"][0], "0_mix")
    sums[0, "mix"] = last[:len(MIX_PART)]
    adamw_big(MIX_PART, "mix", None)
    for n, p in zip(CONV, last[len(MIX_PART):]):
        as_one = lambda a: a.reshape(1, *p.shape[1:])
        for k, r in zip(KINDS, _adamw_sharded([p], as_one(w[n]), as_one(m[n]), as_one(v[n]), name=f"adamw_{n}")):
            out[k][n] = r.reshape(w[n].shape)

    grad_x = dh.reshape(x.shape)
    return (loss.reshape(()), grad_x, *[out["grad"][n] for n in order], *[out["delta"][n] for n in order],
            *[out["m"][n] for n in order], *[out["v"][n] for n in order])
```

```python
import functools

import jax
import jax.numpy as jnp
from jax import lax
from jax.experimental import pallas as pl
from jax.experimental.pallas import tpu as pltpu

F32 = jnp.float32
BF16 = jnp.bfloat16

N_DEV = 8
N_CHIP = 4
D_CONF = 512
CONF_K = 31
SHORT_K = 3
EPS = 1e-6
HALO = 32
HALO3 = 8
HALO3_BLK = 16
LANES = 128
SUB = 8
VMEM_LIMIT = 56 * 1024 * 1024

ADAM_LR = 0.001
ADAM_B1 = 0.9
ADAM_B2 = 0.999
ADAM_EPS = 1e-08
ADAM_WD = 0.01
ADAM_STEP = 10

MESH = pl.DeviceIdType.MESH
ANY = pl.BlockSpec(memory_space=pl.ANY)


def _params(*sem):
    return pltpu.CompilerParams(dimension_semantics=sem, vmem_limit_bytes=VMEM_LIMIT)


def _resident(shape, index_map):
    return pl.BlockSpec(shape, index_map, pipeline_mode=pl.Buffered(1))


def _row_loop(n_rows, rb, fn, unroll=1):
    rb = min(rb, n_rows)

    def body(i, carry):
        fn(pl.ds(pl.multiple_of(i * rb, rb), rb))
        return carry
    lax.fori_loop(0, n_rows // rb, body, 0, unroll=unroll)


def _rows8(v):
    acc = v[0:SUB]
    for k in range(1, v.shape[0] // SUB):
        acc = acc + v[k * SUB:(k + 1) * SUB]
    return acc


def _sigmoid(z):
    return 0.5 * jnp.tanh(0.5 * z) + 0.5


def _dot(a, b):
    return jnp.dot(a, b, preferred_element_type=F32)


def _dot_nt(a, b):
    return lax.dot_general(a, b, (((1,), (1,)), ((), ())), preferred_element_type=F32)


def _dot_tn(a, b):
    return lax.dot_general(a, b, (((0,), (0,)), ((), ())), preferred_element_type=F32)


def _replicate_taps(w_ref, wrep, taps):
    for k in range(taps):
        wrep[pl.ds(k * SUB, SUB), :] = jnp.broadcast_to(w_ref[pl.ds(k, 1), :], (SUB, w_ref.shape[1]))


def _shift_copies(win, shf, lanes):
    span = win.shape[0] - SUB
    for r in range(1, SUB):
        for j0 in range(0, span, 64):
            n = min(64, span - j0)
            shf[r - 1, pl.ds(j0, n), lanes] = win[pl.ds(j0 + r, n), lanes]


def _rows_at(win, shf, off, rb, lanes):
    if shf is None or off % SUB == 0:
        return win[pl.ds(off, rb), lanes]
    return shf[off % SUB - 1, pl.ds(off - off % SUB, rb), lanes]


def _conv_taps(win, wrep, out, *, taps, n_rows, base, width, transposed=False, bias_ref=None, shf=None):
    rb = min(64, n_rows)

    def lane_body(cb, carry):
        lanes = pl.ds(pl.multiple_of(cb * LANES, LANES), LANES)
        if shf is not None:
            _shift_copies(win, shf, lanes)
        for r0 in range(0, n_rows, rb):
            acc = None
            for k in range(taps):
                off = (taps - 1 - k) if transposed else (k - (taps - 1))
                wk = jnp.tile(wrep[pl.ds(k * SUB, SUB), lanes], (rb // SUB, 1))
                term = wk * _rows_at(win, shf, base + r0 + off, rb, lanes)
                acc = term if acc is None else acc + term
            if bias_ref is not None:
                acc = acc + bias_ref[:, lanes]
            out[pl.ds(r0, rb), lanes] = acc.astype(out.dtype)
        return carry

    lax.fori_loop(0, width // LANES, lane_body, 0)


def _conv_bwd_taps(win, wrep, x_cur, dx_out, dw_acc, *, taps, n_rows, width, shf=None):
    rb = min(32 if taps > 8 else 64, n_rows)

    def lane_body(cb, carry):
        lanes = pl.ds(pl.multiple_of(cb * LANES, LANES), LANES)
        if shf is not None:
            _shift_copies(win, shf, lanes)
        sums = [None] * taps
        for r0 in range(0, n_rows, rb):
            xv = x_cur[pl.ds(r0, rb), lanes].astype(F32)
            acc = None
            for k in range(taps):
                shifted = _rows_at(win, shf, r0 + taps - 1 - k, rb, lanes)
                term = jnp.tile(wrep[pl.ds(k * SUB, SUB), lanes], (rb // SUB, 1)) * shifted
                acc = term if acc is None else acc + term
                part = _rows8(xv * shifted)
                sums[k] = part if sums[k] is None else sums[k] + part
            dx_out[pl.ds(r0, rb), lanes] = acc.astype(dx_out.dtype)
        for k in range(taps):
            dw_acc[pl.ds(k * SUB, SUB), lanes] += sums[k]
        return carry

    lax.fori_loop(0, width // LANES, lane_body, 0)


def _fold8(acc_ref, taps):
    return jnp.concatenate(
        [jnp.sum(acc_ref[pl.ds(k * SUB, SUB), :], axis=0, keepdims=True) for k in range(taps)], axis=0)


def _seq_tile(s_len):
    return min(512, s_len)


def _mm_tile(s_len):
    return min(512, s_len)


def _ff_chunk(ff):
    best = LANES
    for c in range(LANES, 1408 + 1, LANES):
        if ff % c == 0:
            best = c
    return best


def _col_tile(n):
    for c in (512, 1408, 256, LANES):
        if n % c == 0:
            return c
    return n


def _rms_matmul(x, g, wt, b, *, name, dep=None):
    s_len, d = x.shape
    n = wt.shape[0]
    tm = _mm_tile(s_len)
    cn = _col_tile(n)
    has_bias = b is not None

    def body(*refs):
        x_ref, g_ref, w_ref = refs[0:3]
        b_ref = refs[3] if has_bias else None
        o_ref, h_ref = refs[-2:]

        def blk(rows):
            xv = x_ref[rows, :]
            r = lax.rsqrt(jnp.mean(xv * xv, axis=-1, keepdims=True) + EPS)
            h_ref[rows, :] = ((xv * r) * g_ref[...]).astype(BF16)

        rb = min(128, tm)
        for r0 in range(0, tm, rb):
            blk(pl.ds(r0, rb))
        for j in range(n // cn):
            acc = _dot_nt(h_ref[...], w_ref[j * cn:(j + 1) * cn, :])
            if has_bias:
                acc = acc + b_ref[:, j * cn:(j + 1) * cn]
            o_ref[:, j * cn:(j + 1) * cn] = acc.astype(BF16)

    in_specs = [pl.BlockSpec((tm, d), lambda i: (i, 0)), _resident((1, d), lambda i: (0, 0)),
                _resident((n, d), lambda i: (0, 0))]
    args = [x, g, wt]
    if has_bias:
        in_specs.append(_resident((1, n), lambda i: (0, 0)))
        args.append(b)
    in_specs.append(ANY)
    args.append(x if dep is None else dep)
    return pl.pallas_call(
        body, grid=(s_len // tm,), in_specs=in_specs,
        out_specs=[pl.BlockSpec((tm, n), lambda i: (i, 0)), pl.BlockSpec((tm, d), lambda i: (i, 0))],
        out_shape=[jax.ShapeDtypeStruct((s_len, n), BF16), jax.ShapeDtypeStruct((s_len, d), BF16)],
        compiler_params=_params("parallel"), name=name,
    )(*args)


def _mix_windows(u_ref, uh_ref, gw, pw, first, t):
    c = D_CONF
    uh = uh_ref[...].astype(F32)
    gw[0:HALO, :] = jnp.where(first, 0.0, uh[:, 0:c] * _sigmoid(uh[:, c:2 * c]))
    pw[0:HALO3, :] = jnp.where(first, 0.0, uh[HALO - HALO3:HALO, 3 * c:4 * c] * uh[HALO - HALO3:HALO, 4 * c:5 * c])

    def blk(rows):
        dst = pl.ds(pl.multiple_of(rows.start + HALO, SUB), rows.size)
        gw[dst, :] = u_ref[rows, 0:c].astype(F32) * _sigmoid(u_ref[rows, c:2 * c].astype(F32))
        dst3 = pl.ds(pl.multiple_of(rows.start + HALO3, SUB), rows.size)
        pw[dst3, :] = u_ref[rows, 3 * c:4 * c].astype(F32) * u_ref[rows, 4 * c:5 * c].astype(F32)
    _row_loop(t, 64, blk)


def _mix_fwd(u, x0, wa, ba, lg, lb, wb, w_out, *, name):
    s_len, d_in = u.shape
    d = x0.shape[1]
    c = D_CONF
    t = _seq_tile(s_len)
    per = t // HALO

    def body(u_ref, uh_ref, x0_ref, wa_ref, ba_ref, lg_ref, lb_ref, wb_ref, wo_ref, y_ref, x1_ref, ca, cb,
             gw, pw, wrep_a, wrep_b, shf):
        first = pl.program_id(0) == 0
        _mix_windows(u_ref, uh_ref, gw, pw, first, t)
        _replicate_taps(wa_ref, wrep_a, CONF_K)
        _replicate_taps(wb_ref, wrep_b, SHORT_K)
        _conv_taps(gw, wrep_a, ca, taps=CONF_K, n_rows=t, base=HALO, width=c, bias_ref=ba_ref, shf=shf)
        _conv_taps(pw, wrep_b, cb, taps=SHORT_K, n_rows=t, base=HALO3, width=c)

        def blk(rows):
            cv = ca[rows, :]
            mu = jnp.mean(cv, axis=-1, keepdims=True)
            xc = cv - mu
            var = jnp.mean(xc * xc, axis=-1, keepdims=True)
            ln = (xc * lax.rsqrt(var + EPS)) * lg_ref[...] + lb_ref[...]
            y_ref[rows, 0:c] = (ln * _sigmoid(ln)).astype(BF16)
            y_ref[rows, c:2 * c] = (u_ref[rows, 2 * c:3 * c].astype(F32) * cb[rows, :]).astype(BF16)
        _row_loop(t, 64, blk)
        x1_ref[...] = x0_ref[...] + _dot(y_ref[...], wo_ref[...])

    small = lambda r: _resident((r, c), lambda i: (0, 0))
    return pl.pallas_call(
        body, grid=(s_len // t,),
        in_specs=[pl.BlockSpec((t, d_in), lambda i: (i, 0)),
                  pl.BlockSpec((HALO, d_in), lambda i: (jnp.maximum(i * per - 1, 0), 0)),
                  pl.BlockSpec((t, d), lambda i: (i, 0)),
                  small(CONF_K), small(1), small(1), small(1), small(SHORT_K),
                  _resident((2 * c, d), lambda i: (0, 0))],
        out_specs=[pl.BlockSpec((t, 2 * c), lambda i: (i, 0)), pl.BlockSpec((t, d), lambda i: (i, 0)),
                   pl.BlockSpec((t, c), lambda i: (i, 0)), pl.BlockSpec((t, c), lambda i: (i, 0))],
        out_shape=[jax.ShapeDtypeStruct((s_len, 2 * c), BF16), jax.ShapeDtypeStruct((s_len, d), F32),
                   jax.ShapeDtypeStruct((s_len, c), F32), jax.ShapeDtypeStruct((s_len, c), F32)],
        scratch_shapes=[pltpu.VMEM((HALO + t, c), F32), pltpu.VMEM((HALO3 + t, c), F32),
                        pltpu.VMEM((CONF_K * SUB, c), F32), pltpu.VMEM((SHORT_K * SUB, c), F32),
                        pltpu.VMEM((SUB - 1, HALO + t, c), F32)],
        compiler_params=_params("arbitrary"), name=name,
    )(u, u, x0, wa, ba, lg, lb, wb, w_out)


def _ffn_windows(ug_ref, ugh_ref, uv_ref, uvh_ref, gwin, vwin, first, t):
    lo = HALO3_BLK - HALO3
    gwin[0:HALO3, :] = jnp.where(first, 0.0, ugh_ref[...].astype(F32)[lo:HALO3_BLK])
    vwin[0:HALO3, :] = jnp.where(first, 0.0, uvh_ref[...].astype(F32)[lo:HALO3_BLK])

    def blk(rows):
        dst = pl.ds(pl.multiple_of(rows.start + HALO3, SUB), rows.size)
        gwin[dst, :] = ug_ref[rows, :].astype(F32)
        vwin[dst, :] = uv_ref[rows, :].astype(F32)
    _row_loop(t, 64, blk)


def _ffn_fwd(uf, x1, wf, w_down, *, name, dep=None):
    s_len, ff2 = uf.shape
    ff = ff2 // 2
    d = x1.shape[1]
    t = _seq_tile(s_len)
    fc = _ff_chunk(ff)
    nc = ff // fc
    per = t // HALO3_BLK

    def body(ug_ref, ugh_ref, uv_ref, uvh_ref, x1_ref, wfg_ref, wfv_ref, wd_ref, dep_ref,
             act_ref, x2_ref, cg_ref, cv_ref, gwin, vwin, cg, cv, wrep_g, wrep_v):
        first = pl.program_id(0) == 0
        _ffn_windows(ug_ref, ugh_ref, uv_ref, uvh_ref, gwin, vwin, first, t)
        _replicate_taps(wfg_ref, wrep_g, SHORT_K)
        _replicate_taps(wfv_ref, wrep_v, SHORT_K)
        _conv_taps(gwin, wrep_g, cg, taps=SHORT_K, n_rows=t, base=HALO3, width=fc)
        _conv_taps(vwin, wrep_v, cv, taps=SHORT_K, n_rows=t, base=HALO3, width=fc)

        def blk(rows):
            gv = cg[rows, :]
            vv = cv[rows, :]
            cg_ref[rows, :] = gv.astype(BF16)
            cv_ref[rows, :] = vv.astype(BF16)
            act_ref[rows, :] = ((gv * _sigmoid(gv)) * vv).astype(BF16)
        _row_loop(t, 32, blk, unroll=2)

        @pl.when(pl.program_id(1) == 0)
        def _():
            x2_ref[...] = x1_ref[...]
        x2_ref[...] += _dot(act_ref[...], wd_ref[...])

    halo_map = lambda off: (lambda i, j: (jnp.maximum(i * per - 1, 0), j + off))
    return pl.pallas_call(
        body, grid=(s_len // t, nc),
        in_specs=[pl.BlockSpec((t, fc), lambda i, j: (i, j)), pl.BlockSpec((HALO3_BLK, fc), halo_map(0)),
                  pl.BlockSpec((t, fc), lambda i, j: (i, j + nc)), pl.BlockSpec((HALO3_BLK, fc), halo_map(nc)),
                  pl.BlockSpec((t, d), lambda i, j: (i, 0)),
                  pl.BlockSpec((SHORT_K, fc), lambda i, j: (0, j)),
                  pl.BlockSpec((SHORT_K, fc), lambda i, j: (0, j + nc)),
                  pl.BlockSpec((fc, d), lambda i, j: (j, 0)), ANY],
        out_specs=[pl.BlockSpec((t, fc), lambda i, j: (i, j)), pl.BlockSpec((t, d), lambda i, j: (i, 0)),
                   pl.BlockSpec((t, fc), lambda i, j: (i, j)), pl.BlockSpec((t, fc), lambda i, j: (i, j))],
        out_shape=[jax.ShapeDtypeStruct((s_len, ff), BF16), jax.ShapeDtypeStruct((s_len, d), F32),
                   jax.ShapeDtypeStruct((s_len, ff), BF16), jax.ShapeDtypeStruct((s_len, ff), BF16)],
        scratch_shapes=[pltpu.VMEM((HALO3 + t, fc), F32), pltpu.VMEM((HALO3 + t, fc), F32),
                        pltpu.VMEM((t, fc), F32), pltpu.VMEM((t, fc), F32),
                        pltpu.VMEM((SHORT_K * SUB, fc), F32), pltpu.VMEM((SHORT_K * SUB, fc), F32)],
        compiler_params=_params("parallel", "arbitrary"), name=name,
    )(uf, uf, uf, uf, x1, wf, wf, w_down, uf if dep is None else dep)


def _loss_bwd(x, g, target, *, name):
    s_len, d = x.shape
    t = _seq_tile(s_len)

    def body(x_ref, g_ref, t_ref, l_ref, dx_ref, dxb_ref, dg_ref):
        @pl.when(pl.program_id(0) == 0)
        def _():
            l_ref[...] = jnp.zeros_like(l_ref)
            dg_ref[...] = jnp.zeros_like(dg_ref)

        def blk(rows):
            xv = x_ref[rows, :]
            r = lax.rsqrt(jnp.mean(xv * xv, axis=-1, keepdims=True) + EPS)
            xn = xv * r
            e = xn * g_ref[...] - t_ref[rows, :]
            l_ref[...] += _rows8(e * e)
            dy = e * (1.0 / d)
            dg_ref[...] += _rows8(dy * xn)
            dn = dy * g_ref[...]
            dx = r * (dn - xn * jnp.mean(dn * xn, axis=-1, keepdims=True))
            dx_ref[rows, :] = dx
            dxb_ref[rows, :] = dx.astype(BF16)
        _row_loop(t, 64, blk)

    row = pl.BlockSpec((t, d), lambda i: (i, 0))
    part = pl.BlockSpec((SUB, d), lambda i: (0, 0))
    return pl.pallas_call(
        body, grid=(s_len // t,),
        in_specs=[row, _resident((1, d), lambda i: (0, 0)), row],
        out_specs=[part, row, row, part],
        out_shape=[jax.ShapeDtypeStruct((SUB, d), F32), jax.ShapeDtypeStruct((s_len, d), F32),
                   jax.ShapeDtypeStruct((s_len, d), BF16), jax.ShapeDtypeStruct((SUB, d), F32)],
        compiler_params=_params("arbitrary"), name=name,
    )(x, g, target)


def _ffn_bwd(dx2, uf, cg, cv, wf, w_down, *, name, dep=None):
    s_len, ff2 = uf.shape
    ff = ff2 // 2
    d = dx2.shape[1]
    t = _seq_tile(s_len)
    n_t = s_len // t
    fc = _ff_chunk(ff)
    nc = ff // fc

    def body(dx_ref, ug_ref, uv_ref, cg_ref, cv_ref, wfg_ref, wfv_ref, wd_ref, dep_ref,
             dug_ref, duv_ref, dwg_ref, dwv_ref, dact, dgw, dvw, awg, awv, wrep_g, wrep_v):
        i = pl.program_id(1)

        @pl.when(i == 0)
        def _():
            dgw[t:t + HALO3, :] = jnp.zeros((HALO3, fc), F32)
            dvw[t:t + HALO3, :] = jnp.zeros((HALO3, fc), F32)
            awg[...] = jnp.zeros_like(awg)
            awv[...] = jnp.zeros_like(awv)

        _replicate_taps(wfg_ref, wrep_g, SHORT_K)
        _replicate_taps(wfv_ref, wrep_v, SHORT_K)

        def blk(rows):
            gv = cg_ref[rows, :].astype(F32)
            sg = _sigmoid(gv)
            da = dact[rows, :]
            dgw[rows, :] = (da * cv_ref[rows, :].astype(F32)) * (sg * (1.0 + gv * (1.0 - sg)))
            dvw[rows, :] = da * (gv * sg)

        dact[...] = _dot_nt(dx_ref[...], wd_ref[...])
        _row_loop(t, 32, blk, unroll=2)

        _conv_bwd_taps(dgw, wrep_g, ug_ref, dug_ref, awg, taps=SHORT_K, n_rows=t, width=fc)
        _conv_bwd_taps(dvw, wrep_v, uv_ref, duv_ref, awv, taps=SHORT_K, n_rows=t, width=fc)
        dgw[t:t + HALO3, :] = dgw[0:HALO3, :]
        dvw[t:t + HALO3, :] = dvw[0:HALO3, :]

        @pl.when(i == n_t - 1)
        def _():
            dwg_ref[...] = _fold8(awg, SHORT_K)
            dwv_ref[...] = _fold8(awv, SHORT_K)

    rev = lambda i: n_t - 1 - i
    gate = pl.BlockSpec((t, fc), lambda j, i: (rev(i), j))
    value = pl.BlockSpec((t, fc), lambda j, i: (rev(i), j + nc))
    return pl.pallas_call(
        body, grid=(nc, n_t),
        in_specs=[pl.BlockSpec((t, d), lambda j, i: (rev(i), 0)), gate, value, gate, gate,
                  pl.BlockSpec((SHORT_K, fc), lambda j, i: (0, j)),
                  pl.BlockSpec((SHORT_K, fc), lambda j, i: (0, j + nc)),
                  pl.BlockSpec((fc, d), lambda j, i: (j, 0)), ANY],
        out_specs=[gate, gate,
                   pl.BlockSpec((SHORT_K, fc), lambda j, i: (0, j)), pl.BlockSpec((SHORT_K, fc), lambda j, i: (0, j))],
        out_shape=[jax.ShapeDtypeStruct((s_len, ff), BF16), jax.ShapeDtypeStruct((s_len, ff), BF16),
                   jax.ShapeDtypeStruct((SHORT_K, ff), F32), jax.ShapeDtypeStruct((SHORT_K, ff), F32)],
        scratch_shapes=[pltpu.VMEM((t, fc), F32),
                        pltpu.VMEM((t + HALO3, fc), F32), pltpu.VMEM((t + HALO3, fc), F32),
                        pltpu.VMEM((SHORT_K * SUB, fc), F32), pltpu.VMEM((SHORT_K * SUB, fc), F32),
                        pltpu.VMEM((SHORT_K * SUB, fc), F32), pltpu.VMEM((SHORT_K * SUB, fc), F32)],
        compiler_params=_params("arbitrary", "arbitrary"), name=name,
    )(dx2, uf, uf, cg, cv, wf, wf, w_down, uf if dep is None else dep)


def _mix_bwd(dx1, u, ca, cb, wa, lg, lb, wb, w_out, *, name):
    s_len, d_in = u.shape
    d = dx1.shape[1]
    c = D_CONF
    t = _seq_tile(s_len)
    n_t = s_len // t

    def body(dx_ref, u_ref, ca_ref, cb_ref, wa_ref, lg_ref, lb_ref, wb_ref, wo_ref,
             du_ref, dwa_ref, dwb_ref, dba_ref, dlg_ref, dlb_ref, dbin_ref,
             glu, prod, dyc, dcaw, dcbw, dglu, dp, awa, awb, wrep_a, wrep_b, shf):
        i = pl.program_id(0)
        dyc[...] = _dot_nt(dx_ref[...], wo_ref[...])
        _replicate_taps(wa_ref, wrep_a, CONF_K)
        _replicate_taps(wb_ref, wrep_b, SHORT_K)

        @pl.when(i == 0)
        def _():
            dcaw[t:t + HALO, :] = jnp.zeros((HALO, c), F32)
            dcbw[t:t + HALO3, :] = jnp.zeros((HALO3, c), F32)
            awa[...] = jnp.zeros_like(awa)
            awb[...] = jnp.zeros_like(awb)
            dba_ref[...] = jnp.zeros_like(dba_ref)
            dlg_ref[...] = jnp.zeros_like(dlg_ref)
            dlb_ref[...] = jnp.zeros_like(dlb_ref)
            dbin_ref[...] = jnp.zeros_like(dbin_ref)

        def blk1(rows):
            cv = ca_ref[rows, :]
            mu = jnp.mean(cv, axis=-1, keepdims=True)
            xc = cv - mu
            rstd = lax.rsqrt(jnp.mean(xc * xc, axis=-1, keepdims=True) + EPS)
            nrm = xc * rstd
            ln = nrm * lg_ref[...] + lb_ref[...]
            sg = _sigmoid(ln)
            dln = dyc[rows, 0:c] * (sg * (1.0 + ln * (1.0 - sg)))
            dlg_ref[...] += _rows8(dln * nrm)
            dlb_ref[...] += _rows8(dln)
            dn = dln * lg_ref[...]
            dca = rstd * (dn - jnp.mean(dn, axis=-1, keepdims=True)
                          - nrm * jnp.mean(dn * nrm, axis=-1, keepdims=True))
            dcaw[rows, :] = dca
            dba_ref[...] += _rows8(dca)
            ds = dyc[rows, c:2 * c]
            dgb = ds * cb_ref[rows, :]
            dcbw[rows, :] = ds * u_ref[rows, 2 * c:3 * c].astype(F32)
            du_ref[rows, 2 * c:3 * c] = dgb.astype(BF16)
            dbin_ref[:, 2 * c:3 * c] += _rows8(dgb)
            glu[rows, :] = u_ref[rows, 0:c].astype(F32) * _sigmoid(u_ref[rows, c:2 * c].astype(F32))
            prod[rows, :] = u_ref[rows, 3 * c:4 * c].astype(F32) * u_ref[rows, 4 * c:5 * c].astype(F32)
        _row_loop(t, 64, blk1, unroll=2)

        _conv_bwd_taps(dcaw, wrep_a, glu, dglu, awa, taps=CONF_K, n_rows=t, width=c, shf=shf)
        _conv_bwd_taps(dcbw, wrep_b, prod, dp, awb, taps=SHORT_K, n_rows=t, width=c)
        dcaw[t:t + HALO, :] = dcaw[0:HALO, :]
        dcbw[t:t + HALO3, :] = dcbw[0:HALO3, :]

        def blk2(rows):
            av = u_ref[rows, 0:c].astype(F32)
            sg = _sigmoid(u_ref[rows, c:2 * c].astype(F32))
            dg = dglu[rows, :]
            d_av = dg * sg
            d_ag = (dg * av) * (sg * (1.0 - sg))
            dpv = dp[rows, :]
            d_gc = dpv * u_ref[rows, 4 * c:5 * c].astype(F32)
            d_vs = dpv * u_ref[rows, 3 * c:4 * c].astype(F32)
            du_ref[rows, 0:c] = d_av.astype(BF16)
            du_ref[rows, c:2 * c] = d_ag.astype(BF16)
            du_ref[rows, 3 * c:4 * c] = d_gc.astype(BF16)
            du_ref[rows, 4 * c:5 * c] = d_vs.astype(BF16)
            dbin_ref[:, 0:c] += _rows8(d_av)
            dbin_ref[:, c:2 * c] += _rows8(d_ag)
            dbin_ref[:, 3 * c:4 * c] += _rows8(d_gc)
            dbin_ref[:, 4 * c:5 * c] += _rows8(d_vs)
        _row_loop(t, 64, blk2)

        @pl.when(i == n_t - 1)
        def _():
            dwa_ref[...] = _fold8(awa, CONF_K)
            dwb_ref[...] = _fold8(awb, SHORT_K)

    rev = lambda i: n_t - 1 - i
    small_in = lambda r: _resident((r, c), lambda i: (0, 0))
    small = lambda r: pl.BlockSpec((r, c), lambda i: (0, 0))
    return pl.pallas_call(
        body, grid=(n_t,),
        in_specs=[pl.BlockSpec((t, d), lambda i: (rev(i), 0)),
                  pl.BlockSpec((t, d_in), lambda i: (rev(i), 0)),
                  pl.BlockSpec((t, c), lambda i: (rev(i), 0)), pl.BlockSpec((t, c), lambda i: (rev(i), 0)),
                  small_in(CONF_K), small_in(1), small_in(1), small_in(SHORT_K),
                  _resident((2 * c, d), lambda i: (0, 0))],
        out_specs=[pl.BlockSpec((t, d_in), lambda i: (rev(i), 0)),
                   small(CONF_K), small(SHORT_K), small(SUB), small(SUB), small(SUB),
                   pl.BlockSpec((SUB, d_in), lambda i: (0, 0))],
        out_shape=[jax.ShapeDtypeStruct((s_len, d_in), BF16),
                   jax.ShapeDtypeStruct((CONF_K, c), F32), jax.ShapeDtypeStruct((SHORT_K, c), F32),
                   jax.ShapeDtypeStruct((SUB, c), F32), jax.ShapeDtypeStruct((SUB, c), F32),
                   jax.ShapeDtypeStruct((SUB, c), F32), jax.ShapeDtypeStruct((SUB, d_in), F32)],
        scratch_shapes=[pltpu.VMEM((t, c), F32), pltpu.VMEM((t, c), F32), pltpu.VMEM((t, 2 * c), F32),
                        pltpu.VMEM((t + HALO, c), F32), pltpu.VMEM((t + HALO3, c), F32),
                        pltpu.VMEM((t, c), F32), pltpu.VMEM((t, c), F32),
                        pltpu.VMEM((CONF_K * SUB, c), F32), pltpu.VMEM((SHORT_K * SUB, c), F32),
                        pltpu.VMEM((CONF_K * SUB, c), F32), pltpu.VMEM((SHORT_K * SUB, c), F32),
                        pltpu.VMEM((SUB - 1, t + HALO, c), F32)],
        compiler_params=_params("arbitrary"), name=name,
    )(dx1, u, ca, cb, wa, lg, lb, wb, w_out)


def _matmul_tn(a, b, *, name, into=None, part=0, n_parts=1):
    s_len, k = a.shape
    n = b.shape[1]
    tk = _col_tile(k)
    per = k // tk

    def body(*refs):
        a_ref, b_ref = refs[0], refs[1]
        o_ref = refs[-1]
        o_ref[...] = _dot_tn(a_ref[...], b_ref[...]).astype(BF16)

    in_specs = [pl.BlockSpec((s_len, tk), lambda j: (0, j)), _resident((s_len, n), lambda j: (0, 0))]
    args = [a, b]
    aliases = {}
    if into is not None:
        in_specs.append(ANY)
        args.append(into)
        aliases = {2: 0}
    return pl.pallas_call(
        body, grid=(per,), in_specs=in_specs,
        out_specs=pl.BlockSpec((tk, n), lambda j: (part * per + j, 0)),
        out_shape=jax.ShapeDtypeStruct((n_parts * k, n), BF16),
        input_output_aliases=aliases,
        compiler_params=_params("parallel"), name=name,
    )(*args)


def _matmul_rmsbwd(dzs, wt, x, g, dx_in, *, name, dep=None):
    s_len, d = x.shape
    n_z = len(dzs)
    nj = dzs[0].shape[1]
    t = _mm_tile(s_len)

    def body(*refs):
        dz_refs = refs[0:n_z]
        w_refs = refs[n_z:2 * n_z]
        x_ref, g_ref, dxi_ref, _, dx_ref, dxb_ref, dg_ref, dh = refs[2 * n_z:]

        @pl.when(pl.program_id(0) == 0)
        def _():
            dg_ref[...] = jnp.zeros_like(dg_ref)

        def blk(rows):
            xv = x_ref[rows, :]
            r = lax.rsqrt(jnp.mean(xv * xv, axis=-1, keepdims=True) + EPS)
            xn = xv * r
            dhv = dh[rows, :]
            dg_ref[...] += _rows8(dhv * xn)
            dn = dhv * g_ref[...]
            dx = dxi_ref[rows, :] + r * (dn - xn * jnp.mean(dn * xn, axis=-1, keepdims=True))
            dx_ref[rows, :] = dx
            dxb_ref[rows, :] = dx.astype(BF16)

        half = t // 2
        rb = min(128, half)
        for lo in range(0, t, half):
            acc = _dot(dz_refs[0][lo:lo + half, :], w_refs[0][...])
            for q in range(1, n_z):
                acc = acc + _dot(dz_refs[q][lo:lo + half, :], w_refs[q][...])
            dh[lo:lo + half, :] = acc
            for r0 in range(lo, lo + half, rb):
                blk(pl.ds(r0, rb))

    row = pl.BlockSpec((t, d), lambda i: (i, 0))
    in_specs = [pl.BlockSpec((t, nj), lambda i: (i, 0)) for _ in range(n_z)]
    in_specs += [_resident((nj, d), functools.partial(lambda q, i: (q, 0), q)) for q in range(n_z)]
    in_specs += [row, _resident((1, d), lambda i: (0, 0)), row, ANY]
    return pl.pallas_call(
        body, grid=(s_len // t,), in_specs=in_specs,
        out_specs=[row, row, pl.BlockSpec((SUB, d), lambda i: (0, 0))],
        out_shape=[jax.ShapeDtypeStruct((s_len, d), F32), jax.ShapeDtypeStruct((s_len, d), BF16),
                   jax.ShapeDtypeStruct((SUB, d), F32)],
        scratch_shapes=[pltpu.VMEM((t, d), F32)],
        compiler_params=_params("arbitrary"), name=name,
    )(*dzs, *([wt] * n_z), x, g, dx_in, x if dep is None else dep)


def _row(v):
    return v.reshape(1, -1)


def _layer_fwd(x0, p, tag, dep=None, before_up=None):
    u, h1 = _rms_matmul(x0, _row(p["mix_norm_g"]), p["w_in_t"], _row(p["b_in"]), name=f"in_proj_{tag}", dep=dep)
    ycat, x1, ca, cb = _mix_fwd(u, x0, p["conv_a_w"], _row(p["conv_a_b"]), _row(p["ln_a_g"]), _row(p["ln_a_b"]),
                            p["conv_b_w"], p["w_out"], name=f"mix_fwd_{tag}")
    if before_up is not None:
        before_up(x1)
    uf, h2 = _rms_matmul(x1, _row(p["ffn_norm_g"]), p["w_up_t"], None, name=f"up_proj_{tag}")
    act, x2, cg, cv = _ffn_fwd(uf, x1, p["conv_f_w"], p["w_down"], name=f"ffn_fwd_{tag}")
    return x2, dict(x0=x0, h1=h1, u=u, ca=ca, cb=cb, ycat=ycat, x1=x1, h2=h2, uf=uf, cg=cg, cv=cv, act=act)


def _layer_bwd(dx2, dx2_b, p, saved, tag, after_ffn, after_mix, dep=None):
    dug, duv, dwf_g, dwf_v = _ffn_bwd(dx2_b, saved["uf"], saved["cg"], saved["cv"], p["conv_f_w"], p["w_down"],
                                      name=f"ffn_bwd_{tag}", dep=dep)
    g_down = _matmul_tn(saved["act"], dx2_b, name=f"dw_down_{tag}")
    g_up = _matmul_tn(dug, saved["h2"], name=f"dw_up_g_{tag}", n_parts=2)
    g_up = _matmul_tn(duv, saved["h2"], name=f"dw_up_v_{tag}", into=g_up, part=1, n_parts=2)
    dep_ffn = after_ffn(dict(w_up=g_up, w_down=g_down))
    dx1, dx1_b, dg2 = _matmul_rmsbwd([dug, duv], p["w_up_t"], saved["x1"], _row(p["ffn_norm_g"]), dx2,
                                     name=f"dh_ffn_{tag}", dep=dep_ffn)
    du, dwa, dwb, dba, dlg, dlb, dbin = _mix_bwd(
        dx1_b, saved["u"], saved["ca"], saved["cb"], p["conv_a_w"], _row(p["ln_a_g"]), _row(p["ln_a_b"]),
        p["conv_b_w"], p["w_out"], name=f"mix_bwd_{tag}")
    g_out = _matmul_tn(saved["ycat"], dx1_b, name=f"dw_out_{tag}")
    g_in = _matmul_tn(du, saved["h1"], name=f"dw_in_{tag}")
    conv = dict(conv_a_w=dwa, conv_b_w=dwb, conv_f_w=jnp.concatenate([dwf_g, dwf_v], axis=1))
    dep_mix = after_mix(dict(w_in=g_in, w_out=g_out), conv)
    dx0, dx0_b, dg1 = _matmul_rmsbwd([du], p["w_in_t"], saved["x0"], _row(p["mix_norm_g"]), dx1,
                                     name=f"dh_mix_{tag}", dep=dep_mix)
    rep = dict(mix_norm_g=dg1, b_in=dbin, conv_a_b=dba, ln_a_g=dlg, ln_a_b=dlb, ffn_norm_g=dg2)
    return dx0, dx0_b, rep


def _place():
    return lax.axis_index("x"), lax.axis_index("y"), lax.axis_index("c")


def _all_gather(arrs, *, name):
    n_a = len(arrs)

    def body(*refs):
        ins = refs[0:n_a]
        outs = refs[n_a:2 * n_a]
        send_sems, recv_sems, local_sems = refs[2 * n_a:]
        x, y, c = _place()
        sibling = (x, y, 1 - c)
        chips = [(1 - x, y), (x, 1 - y), (1 - x, 1 - y)]

        def slot(a, px, py, pc):
            return outs[a].at[4 * px + 2 * py + pc]

        def copy(a, k, block, to, src=None):
            return pltpu.make_async_remote_copy(
                src_ref=slot(a, *block) if src is None else src, dst_ref=slot(a, *block),
                send_sem=send_sems.at[a, k], recv_sem=recv_sems.at[a, k],
                device_id=to, device_id_type=MESH)

        me = (x, y, c)
        mine = [pltpu.make_async_copy(ins[a], slot(a, *me), local_sems.at[a]) for a in range(n_a)]
        for cp in mine:
            cp.start()
        started = []
        for a in range(n_a):
            first = [copy(a, 0, me, sibling, src=ins[a])]
            first += [copy(a, 1 + j, me, (*chip, c), src=ins[a]) for j, chip in enumerate(chips)]
            for cp in first:
                cp.start()
            started += first
        for a in range(n_a):
            for j, chip in enumerate(chips):
                copy(a, 1 + j, (*chip, c), me).wait_recv()
                passed = copy(a, 4 + j, (*chip, c), sibling)
                passed.start()
                started.append(passed)
        for a in range(n_a):
            copy(a, 0, sibling, me).wait_recv()
            for j, chip in enumerate(chips):
                copy(a, 4 + j, (*chip, 1 - c), me).wait_recv()
        for cp in started:
            cp.wait_send()
        for cp in mine:
            cp.wait()

    return pl.pallas_call(
        body, in_specs=[ANY] * n_a, out_specs=[ANY] * n_a,
        out_shape=[jax.ShapeDtypeStruct((N_DEV, *a.shape), a.dtype) for a in arrs],
        scratch_shapes=[pltpu.SemaphoreType.DMA((n_a, 7)), pltpu.SemaphoreType.DMA((n_a, 7)),
                        pltpu.SemaphoreType.DMA((n_a,))],
        name=name,
    )(*arrs)


def _sibling_exchange(arrs, *, name):
    n_a = len(arrs)

    def body(*refs):
        ins = refs[0:n_a]
        outs = refs[n_a:2 * n_a]
        send_sems, recv_sems = refs[2 * n_a:]
        x, y, c = _place()
        copies = [pltpu.make_async_remote_copy(
            src_ref=ins[a].at[:, 1 - c], dst_ref=outs[a], send_sem=send_sems.at[a], recv_sem=recv_sems.at[a],
            device_id=(x, y, 1 - c), device_id_type=MESH) for a in range(n_a)]
        for cp in copies:
            cp.start()
        for cp in copies:
            cp.wait()

    return pl.pallas_call(
        body, in_specs=[ANY] * n_a, out_specs=[ANY] * n_a,
        out_shape=[jax.ShapeDtypeStruct((N_CHIP, *a.shape[2:]), a.dtype) for a in arrs],
        scratch_shapes=[pltpu.SemaphoreType.DMA((n_a,)), pltpu.SemaphoreType.DMA((n_a,))],
        name=name,
    )(*arrs)


def _row_tile(r, cap):
    for tr in range(min(cap, r) // 16 * 16, 0, -16):
        if r % tr == 0:
            return tr
    return r


def _pair_sum(mines, theirs, where, *, name):
    n_a = len(mines)
    n_chip = mines[0].shape[0]

    def body(where_ref, *refs):
        a_refs = refs[0:n_a]
        b_refs = refs[n_a:2 * n_a]
        p_refs = refs[2 * n_a:3 * n_a]
        l_refs = refs[3 * n_a:4 * n_a]
        q = pl.program_id(0)
        for a in range(n_a):
            p_refs[a][...] = (a_refs[a][...].astype(F32) + b_refs[a][...].astype(F32)).astype(p_refs[a].dtype)

        @pl.when(q == where_ref[1])
        def _():
            for a in range(n_a):
                l_refs[a][...] = p_refs[a][...]

    in_specs, out_p, out_l, shapes = [], [], [], []
    for m in mines:
        _, _, r, c = m.shape
        in_specs.append(pl.BlockSpec((None, None, r, c), lambda q, where_ref: (q, where_ref[0], 0, 0)))
    for m in mines:
        _, _, r, c = m.shape
        in_specs.append(pl.BlockSpec((None, r, c), lambda q, where_ref: (q, 0, 0)))
        out_p.append(pl.BlockSpec((None, r, c), lambda q, where_ref: (q, 0, 0)))
        out_l.append(pl.BlockSpec((None, r, c), lambda q, where_ref: (where_ref[1], 0, 0)))
        shapes.append(jax.ShapeDtypeStruct((n_chip, r, c), m.dtype))
    res = pl.pallas_call(
        body,
        grid_spec=pltpu.PrefetchScalarGridSpec(num_scalar_prefetch=1, grid=(n_chip,), in_specs=in_specs,
                                               out_specs=out_p + out_l),
        out_shape=shapes + shapes,
        compiler_params=_params("arbitrary"), name=name,
    )(where, *mines, *theirs)
    return list(res[:n_a]), list(res[n_a:])


HBM = pl.BlockSpec(memory_space=pltpu.HBM)
SEM = pl.BlockSpec(memory_space=pltpu.SEMAPHORE)
EFFECT = pltpu.SideEffectType.DATAFLOW_SIDE_EFFECTING


def _in_hbm(a):
    return pltpu.with_memory_space_constraint(a, pltpu.HBM)


def _split_start(srcs, lands, plan, n_copies, after, *, name):
    n_s, n_l = len(srcs), len(lands)

    def body(*refs):
        src_refs = refs[0:n_s]
        land_refs = refs[n_s:n_s + n_l]
        send_sems, recv_sems = refs[n_s + n_l + 1], refs[n_s + n_l + 2]
        token = refs[-1]
        for cp in plan(src_refs, land_refs, send_sems, recv_sems):
            cp.start()
        token[...] = jnp.zeros_like(token)

    thru = [pltpu.HBM(a.shape, a.dtype) for a in list(srcs) + list(lands)]
    res = pl.pallas_call(
        body, name=name,
        out_shape=(pltpu.SemaphoreType.DMA((n_copies,)), pltpu.SemaphoreType.DMA((n_copies,)), *thru,
                   jax.ShapeDtypeStruct((SUB, LANES), F32)),
        in_specs=[HBM] * (n_s + n_l) + [ANY],
        out_specs=(SEM, SEM, *([HBM] * (n_s + n_l)), pl.BlockSpec(memory_space=pltpu.VMEM)),
        input_output_aliases={i: 2 + i for i in range(n_s + n_l)},
        compiler_params=pltpu.CompilerParams(has_side_effects=EFFECT),
    )(*[_in_hbm(a) for a in srcs], *[_in_hbm(a) for a in lands], _in_hbm(after))
    return res[0], res[1], list(res[2:2 + n_s]), list(res[2 + n_s:2 + n_s + n_l]), res[-1]


def _split_wait(send_sems, recv_sems, srcs, lands, after, plan, *, name):
    n_s, n_l = len(srcs), len(lands)

    def body(*refs):
        src_refs = refs[0:n_s]
        land_refs = refs[n_s:n_s + n_l]
        send, recv = refs[n_s + n_l], refs[n_s + n_l + 1]
        for cp in plan(src_refs, land_refs, send, recv):
            cp.wait_send()
            cp.wait_recv()

    res = pl.pallas_call(
        body, name=name,
        out_shape=tuple(pltpu.HBM(a.shape, a.dtype) for a in list(srcs) + list(lands)),
        in_specs=[HBM] * (n_s + n_l) + [SEM, SEM, ANY],
        out_specs=tuple([HBM] * (n_s + n_l)),
        input_output_aliases={i: i for i in range(n_s + n_l)},
        compiler_params=pltpu.CompilerParams(has_side_effects=EFFECT),
    )(*srcs, *lands, send_sems, recv_sems, _in_hbm(after))
    return list(res[n_s:])


def _remote(src, dst, send_sems, recv_sems, k, to):
    return pltpu.make_async_remote_copy(src_ref=src, dst_ref=dst, send_sem=send_sems.at[k], recv_sem=recv_sems.at[k],
                                        device_id=to, device_id_type=MESH)


def _gather_plan_first(src_refs, land_refs, send_sems, recv_sems):
    x, y, c = _place()
    me = 4 * x + 2 * y + c
    peers = [(x, y, 1 - c), (1 - x, y, c), (x, 1 - y, c), (1 - x, 1 - y, c)]
    return [_remote(src, land.at[me], send_sems, recv_sems, 4 * a + k, to)
            for a, (src, land) in enumerate(zip(src_refs, land_refs)) for k, to in enumerate(peers)]


def _gather_plan_second(src_refs, land_refs, send_sems, recv_sems):
    x, y, c = _place()
    chips = [(1 - x, y), (x, 1 - y), (1 - x, 1 - y)]
    out = []
    for a, land in enumerate(land_refs):
        for j, (px, py) in enumerate(chips):
            slot = land.at[4 * px + 2 * py + c]
            out.append(_remote(slot, slot, send_sems, recv_sems, 3 * a + j, (x, y, 1 - c)))
    return out


def _chips_plan(src_refs, land_refs, send_sems, recv_sems):
    x, y, c = _place()
    my_chip = 2 * x + y
    chips = [(1 - x, y), (x, 1 - y), (1 - x, 1 - y)]
    return [_remote(src.at[2 * px + py], land.at[my_chip], send_sems, recv_sems, 3 * a + j, (px, py, c))
            for a, (src, land) in enumerate(zip(src_refs, land_refs)) for j, (px, py) in enumerate(chips)]


def _gather_landing(shard, me):
    return lax.dynamic_update_index_in_dim(lax.empty((N_DEV, *shard.shape), shard.dtype), shard, me, 0)


def _adamw_math(g, w, m, v):
    m = ADAM_B1 * m + (1.0 - ADAM_B1) * g
    v = ADAM_B2 * v + (1.0 - ADAM_B2) * (g * g)
    m_hat = m / (1.0 - ADAM_B1 ** ADAM_STEP)
    v_hat = v / (1.0 - ADAM_B2 ** ADAM_STEP)
    delta = -ADAM_LR * (m_hat / (jnp.sqrt(v_hat) + ADAM_EPS) + ADAM_WD * w)
    return delta, m, v


def _adamw_sharded(parts, w, m, v, *, name, dep=None):
    n_layers, r, c = w.shape
    n_chip = parts[0].shape[0]
    tr = _row_tile(r, 384)
    n_i = r // tr

    def body(*refs):
        p_refs = refs[0:n_layers]
        w_ref, m_ref, v_ref, _, g_out, d_out, m_out, v_out = refs[n_layers:]
        layer = pl.program_id(0)
        for l in range(n_layers):
            @pl.when(layer == l)
            def _(l=l):
                g = p_refs[l][0].astype(F32)
                for q in range(1, n_chip):
                    g = g + p_refs[l][q].astype(F32)
                delta, m_new, v_new = _adamw_math(g, w_ref[...], m_ref[...], v_ref[...])
                g_out[...] = g
                d_out[...] = delta
                m_out[...] = m_new
                v_out[...] = v_new

    def part_map(l):
        return lambda layer, i: (0, jnp.where(layer == l, i, jnp.where(layer < l, 0, n_i - 1)), 0)

    blk = pl.BlockSpec((None, tr, c), lambda layer, i: (layer, i, 0))
    return pl.pallas_call(
        body, grid=(n_layers, n_i),
        in_specs=[pl.BlockSpec((n_chip, tr, c), part_map(l)) for l in range(n_layers)] + [blk, blk, blk, ANY],
        out_specs=[blk] * 4, out_shape=[jax.ShapeDtypeStruct((n_layers, r, c), F32)] * 4,
        compiler_params=_params("arbitrary", "arbitrary"), name=name,
    )(*parts, w, m, v, w if dep is None else dep)


def _fold_partials(cols, *, name):
    widths = [c.shape[1] for c in cols]

    def body(*refs):
        o_ref = refs[-1]
        pos = 0
        for ref, width in zip(refs[:-1], widths):
            o_ref[:, pos:pos + width] = jnp.sum(ref[...], axis=0, keepdims=True)
            pos += width

    return pl.pallas_call(body, out_shape=jax.ShapeDtypeStruct((1, sum(widths)), F32), name=name)(*cols)


def _adamw_replicated(parts, names, w, m, v, n_loss, *, name):
    n_dev = parts.shape[0]
    n_layers = w[names[0]].shape[0]
    every = list(names) + ["final_norm_g"]
    n_p = len(every)

    def body(*refs):
        p_ref = refs[0]
        w_refs = dict(zip(every, refs[1:1 + n_p]))
        m_refs = dict(zip(every, refs[1 + n_p:1 + 2 * n_p]))
        v_refs = dict(zip(every, refs[1 + 2 * n_p:1 + 3 * n_p]))
        l_out = refs[1 + 3 * n_p]
        outs = refs[2 + 3 * n_p:]
        o_refs = {n: outs[4 * q:4 * q + 4] for q, n in enumerate(every)}
        acc = p_ref[0]
        for q in range(1, n_dev):
            acc = acc + p_ref[q]
        tot = jnp.sum(acc, axis=0, keepdims=True)
        pos = 0
        where = [(n, l) for l in range(n_layers) for n in names] + [("final_norm_g", 0)]
        for n, l in where:
            width = w_refs[n].shape[1]
            g = tot[:, pos:pos + width]
            pos += width
            row = pl.ds(l, 1)
            delta, m_new, v_new = _adamw_math(g, w_refs[n][row, :], m_refs[n][row, :], v_refs[n][row, :])
            for o, val in zip(o_refs[n], (g, delta, m_new, v_new)):
                o[row, :] = val
        l_out[...] = (0.5 / n_loss) * jnp.sum(tot[:, pos:pos + n_loss], axis=-1, keepdims=True)

    shapes = [jax.ShapeDtypeStruct((1, 1), F32)]
    for n in every:
        shapes += [jax.ShapeDtypeStruct(w[n].shape, F32)] * 4
    res = pl.pallas_call(
        body, out_shape=shapes,
        compiler_params=pltpu.CompilerParams(vmem_limit_bytes=VMEM_LIMIT), name=name,
    )(parts, *[w[n] for n in every], *[m[n] for n in every], *[v[n] for n in every])
    return res[0], {n: res[1 + 4 * q:5 + 4 * q] for q, n in enumerate(every)}


BIG = ("w_in", "w_out", "w_up", "w_down")
COL_SHARDED = ("w_in", "w_up")
CONV = ("conv_a_w", "conv_b_w", "conv_f_w")
REPLICATED = ("mix_norm_g", "b_in", "conv_a_b", "ln_a_g", "ln_a_b", "ffn_norm_g")
KINDS = ("grad", "delta", "m", "v")
FFN_PART = ("w_up", "w_down")
MIX_PART = ("w_in", "w_out")


def _weights_from_gathered(g):
    n_dev, r, c = g.shape
    return g.reshape(n_dev * r, c)


def _slabs_from_full(grad):
    return grad.reshape(N_DEV, grad.shape[0] // N_DEV, grad.shape[1])


def _pair_sums(slabs, where, tag):
    slabs = [s.reshape(N_CHIP, 2, *s.shape[1:]) for s in slabs]
    theirs = _sibling_exchange(slabs, name=f"reduce_siblings_{tag}")
    return _pair_sum(slabs, theirs, where, name=f"pair_sum_{tag}")


def kernel(x, mix_norm_g, w_in, b_in, conv_a_w, conv_a_b, ln_a_g, ln_a_b, conv_b_w, w_out, ffn_norm_g, w_up, conv_f_w, w_down, final_norm_g, loss_target, m_mix_norm_g, m_w_in, m_b_in, m_conv_a_w, m_conv_a_b, m_ln_a_g, m_ln_a_b, m_conv_b_w, m_w_out, m_ffn_norm_g, m_w_up, m_conv_f_w, m_w_down, m_final_norm_g, v_mix_norm_g, v_w_in, v_b_in, v_conv_a_w, v_conv_a_b, v_ln_a_g, v_ln_a_b, v_conv_b_w, v_w_out, v_ffn_norm_g, v_w_up, v_conv_f_w, v_w_down, v_final_norm_g):
    w = dict(mix_norm_g=mix_norm_g, w_in=w_in, b_in=b_in, conv_a_w=conv_a_w, conv_a_b=conv_a_b, ln_a_g=ln_a_g,
             ln_a_b=ln_a_b, conv_b_w=conv_b_w, w_out=w_out, ffn_norm_g=ffn_norm_g, w_up=w_up, conv_f_w=conv_f_w,
             w_down=w_down, final_norm_g=final_norm_g)
    m = dict(mix_norm_g=m_mix_norm_g, w_in=m_w_in, b_in=m_b_in, conv_a_w=m_conv_a_w, conv_a_b=m_conv_a_b,
             ln_a_g=m_ln_a_g, ln_a_b=m_ln_a_b, conv_b_w=m_conv_b_w, w_out=m_w_out, ffn_norm_g=m_ffn_norm_g,
             w_up=m_w_up, conv_f_w=m_conv_f_w, w_down=m_w_down, final_norm_g=m_final_norm_g)
    v = dict(mix_norm_g=v_mix_norm_g, w_in=v_w_in, b_in=v_b_in, conv_a_w=v_conv_a_w, conv_a_b=v_conv_a_b,
             ln_a_g=v_ln_a_g, ln_a_b=v_ln_a_b, conv_b_w=v_conv_b_w, w_out=v_w_out, ffn_norm_g=v_ffn_norm_g,
             w_up=v_w_up, conv_f_w=v_conv_f_w, w_down=v_w_down, final_norm_g=v_final_norm_g)
    order = list(w)
    n_layers = w_in.shape[0]
    xs = x[0]
    target = loss_target[0]
    flip = lambda a: jnp.transpose(a, (0, 2, 1))
    wt, mt, vt = ({n: flip(d[n]) if n in COL_SHARDED else d[n] for n in BIG} for d in (w, m, v))
    px, py, pc = _place()
    where = jnp.stack([pc, 2 * px + py]).astype(jnp.int32)
    me = 4 * px + 2 * py + pc

    assert BIG == MIX_PART + FFN_PART
    key = lambda n: n + "_t" if n in COL_SHARDED else n
    shard = lambda n, l: wt[n][l].astype(BF16)

    def gather_start(names, l, after, tag):
        shards = [shard(n, l) for n in names]
        lands = [_gather_landing(s, me) for s in shards]
        return _split_start(shards, lands, _gather_plan_first, 4 * len(shards), after, name=f"gather_first_start_{tag}")

    def gather_mid(first, after, tag):
        return _split_wait(first[0], first[1], first[2], first[3], after, _gather_plan_first,
                           name=f"gather_first_wait_{tag}")

    def forward_start(lands, after, tag):
        return _split_start([], lands, _gather_plan_second, 3 * len(lands), after, name=f"gather_second_start_{tag}")

    def forward_finish(second, after, tag):
        return _split_wait(second[0], second[1], [], second[3], after, _gather_plan_second,
                           name=f"gather_second_wait_{tag}")

    gathered = _all_gather([shard(n, 0) for n in MIX_PART] + [w[n] for n in CONV], name="gather_weights_0")
    params = [{n: w[n][l] for n in REPLICATED} for l in range(n_layers)]
    for n, g in zip(CONV, gathered[len(MIX_PART):]):
        n_dev, _, taps, c = g.shape
        full = g.transpose(1, 2, 0, 3).reshape(n_layers, taps, n_dev * c)
        for l in range(n_layers):
            params[l][n] = full[l]
    for n, g in zip(MIX_PART, gathered):
        params[0][key(n)] = _weights_from_gathered(g)
    ffn_first = gather_start(FFN_PART, 0, gathered[0], "0_ffn")
    pending = {}

    h = xs
    saved = []
    for l in range(n_layers):
        nxt = l + 1 if l + 1 < n_layers else None

        def before_up(x1, l=l, nxt=nxt):
            if l == 0:
                second = forward_start(gather_mid(ffn_first, x1, "0_ffn"), x1, "0_ffn")
                after = second[4]
            else:
                second = pending[l]["ffn"]
                after = x1
            if nxt is not None:
                pending[nxt] = dict(first=gather_start(BIG, nxt, after, str(nxt)))
                after = pending[nxt]["first"][4]
            for n, g in zip(FFN_PART, forward_finish(second, after, f"{l}_ffn")):
                params[l][key(n)] = _weights_from_gathered(g)

        h, keep = _layer_fwd(h, params[l], str(l), dep=ffn_first[4] if l == 0 else None, before_up=before_up)
        saved.append(keep)
        if nxt is not None:
            arrived = gather_mid(pending[nxt]["first"], h, str(nxt))
            mix_second = forward_start(arrived[:len(MIX_PART)], h, f"{nxt}_mix")
            pending[nxt]["ffn"] = forward_start(arrived[len(MIX_PART):], mix_second[4], f"{nxt}_ffn")
            for n, g in zip(MIX_PART, forward_finish(mix_second, pending[nxt]["ffn"][4], f"{nxt}_mix")):
                params[nxt][key(n)] = _weights_from_gathered(g)

    def start_reduce(slabs, tag):
        pairs, lands = _pair_sums(slabs, where, tag)
        return _split_start(pairs, lands, _chips_plan, 3 * len(pairs), pairs[0], name=f"reduce_chips_start_{tag}")

    def finish_reduce(fly, after, tag):
        return _split_wait(fly[0], fly[1], fly[2], fly[3], after, _chips_plan, name=f"reduce_chips_wait_{tag}")

    loss_sq, dh, dh_b, dgf = _loss_bwd(h, _row(final_norm_g), target, name="loss")
    conv_g = {n: [None] * n_layers for n in CONV}
    rep_g = [None] * n_layers
    flights = {}
    token = None
    for l in reversed(range(n_layers)):
        def after_ffn(g, l=l):
            flights[l, "ffn"] = start_reduce([_slabs_from_full(g[n]) for n in FFN_PART], f"{l}_ffn")
            return flights[l, "ffn"][4]

        def after_mix(g, conv, l=l):
            for n in CONV:
                conv_g[n][l] = conv[n]
            slabs = [_slabs_from_full(g[n]) for n in MIX_PART]
            if l == 0:
                for n in CONV:
                    full = jnp.stack(conv_g[n])
                    _, taps, c = full.shape
                    slabs.append(full.reshape(n_layers, taps, N_DEV, c // N_DEV).transpose(2, 0, 1, 3)
                                 .reshape(N_DEV, n_layers * taps, c // N_DEV))
            flights[l, "mix"] = start_reduce(slabs, f"{l}_mix")
            return flights[l, "mix"][4]

        dh, dh_b, rep_g[l] = _layer_bwd(dh, dh_b, params[l], saved[l], str(l), after_ffn, after_mix, dep=token)
        token = flights[l, "mix"][4]

    sums = {key: finish_reduce(fly, dh, f"{key[0]}_{key[1]}") for key, fly in flights.items() if key != (0, "mix")}
    out = {k: {} for k in KINDS}

    def adamw_big(names, part, dep):
        for q, n in enumerate(names):
            layer_parts = [sums[l, part][q] for l in range(n_layers)]
            res = _adamw_sharded(layer_parts, wt[n], mt[n], vt[n], name=f"adamw_{n}", dep=dep)
            for k, r in zip(KINDS, res):
                out[k][n] = flip(r) if n in COL_SHARDED else r

    adamw_big(FFN_PART, "ffn", token)

    rep_cols = [rep_g[l][n] for l in range(n_layers) for n in REPLICATED] + [dgf, loss_sq]
    rep_all = _all_gather([_fold_partials(rep_cols, name="fold_small")], name="gather_small")[0]
    with_final = lambda d: {**{n: d[n] for n in REPLICATED}, "final_norm_g": _row(d["final_norm_g"])}
    loss, rep_res = _adamw_replicated(rep_all, REPLICATED, with_final(w), with_final(m), with_final(v),
                                      loss_sq.shape[1], name="adamw_small")
    for n, res in rep_res.items():
        for k, r in zip(KINDS, res):
            out[k][n] = r.reshape(w[n].shape)

    last = finish_reduce(flights[0, "mix"], rep_res["b_in"][0], "0_mix")
    sums[0, "mix"] = last[:len(MIX_PART)]
    adamw_big(MIX_PART, "mix", None)
    for n, p in zip(CONV, last[len(MIX_PART):]):
        as_one = lambda a: a.reshape(1, *p.shape[1:])
        for k, r in zip(KINDS, _adamw_sharded([p], as_one(w[n]), as_one(m[n]), as_one(v[n]), name=f"adamw_{n}")):
            out[k][n] = r.reshape(w[n].shape)

    grad_x = dh.reshape(x.shape)
    return (loss.reshape(()), grad_x, *[out["grad"][n] for n in order], *[out["delta"][n] for n in order],
            *[out["m"][n] for n in order], *[out["v"][n] for n in order])
```

```python
import functools

import jax
import jax.numpy as jnp
from jax import lax
from jax.experimental import pallas as pl
from jax.experimental.pallas import tpu as pltpu

F32 = jnp.float32
BF16 = jnp.bfloat16

N_DEV = 8
N_CHIP = 4
D_CONF = 512
CONF_K = 31
SHORT_K = 3
EPS = 1e-6
HALO = 32
HALO3 = 8
HALO3_BLK = 16
LANES = 128
SUB = 8
VMEM_LIMIT = 56 * 1024 * 1024

ADAM_LR = 0.001
ADAM_B1 = 0.9
ADAM_B2 = 0.999
ADAM_EPS = 1e-08
ADAM_WD = 0.01
ADAM_STEP = 10

MESH = pl.DeviceIdType.MESH
ANY = pl.BlockSpec(memory_space=pl.ANY)


def _params(*sem):
    return pltpu.CompilerParams(dimension_semantics=sem, vmem_limit_bytes=VMEM_LIMIT)


def _resident(shape, index_map):
    return pl.BlockSpec(shape, index_map, pipeline_mode=pl.Buffered(1))


def _row_loop(n_rows, rb, fn, unroll=1):
    rb = min(rb, n_rows)

    def body(i, carry):
        fn(pl.ds(pl.multiple_of(i * rb, rb), rb))
        return carry
    lax.fori_loop(0, n_rows // rb, body, 0, unroll=unroll)


def _rows8(v):
    acc = v[0:SUB]
    for k in range(1, v.shape[0] // SUB):
        acc = acc + v[k * SUB:(k + 1) * SUB]
    return acc


def _sigmoid(z):
    return 0.5 * jnp.tanh(0.5 * z) + 0.5


def _dot(a, b):
    return jnp.dot(a, b, preferred_element_type=F32)


def _dot_nt(a, b):
    return lax.dot_general(a, b, (((1,), (1,)), ((), ())), preferred_element_type=F32)


def _dot_tn(a, b):
    return lax.dot_general(a, b, (((0,), (0,)), ((), ())), preferred_element_type=F32)


def _replicate_taps(w_ref, wrep, taps):
    for k in range(taps):
        wrep[pl.ds(k * SUB, SUB), :] = jnp.broadcast_to(w_ref[pl.ds(k, 1), :], (SUB, w_ref.shape[1]))


def _shift_copies(win, shf, lanes):
    span = win.shape[0] - SUB
    for r in range(1, SUB):
        for j0 in range(0, span, 64):
            n = min(64, span - j0)
            shf[r - 1, pl.ds(j0, n), lanes] = win[pl.ds(j0 + r, n), lanes]


def _rows_at(win, shf, off, rb, lanes):
    if shf is None or off % SUB == 0:
        return win[pl.ds(off, rb), lanes]
    return shf[off % SUB - 1, pl.ds(off - off % SUB, rb), lanes]


def _conv_taps(win, wrep, out, *, taps, n_rows, base, width, transposed=False, bias_ref=None, shf=None):
    rb = min(64, n_rows)

    def lane_body(cb, carry):
        lanes = pl.ds(pl.multiple_of(cb * LANES, LANES), LANES)
        if shf is not None:
            _shift_copies(win, shf, lanes)
        for r0 in range(0, n_rows, rb):
            acc = None
            for k in range(taps):
                off = (taps - 1 - k) if transposed else (k - (taps - 1))
                wk = jnp.tile(wrep[pl.ds(k * SUB, SUB), lanes], (rb // SUB, 1))
                term = wk * _rows_at(win, shf, base + r0 + off, rb, lanes)
                acc = term if acc is None else acc + term
            if bias_ref is not None:
                acc = acc + bias_ref[:, lanes]
            out[pl.ds(r0, rb), lanes] = acc.astype(out.dtype)
        return carry

    lax.fori_loop(0, width // LANES, lane_body, 0)


def _conv_bwd_taps(win, wrep, x_cur, dx_out, dw_acc, *, taps, n_rows, width, shf=None):
    rb = min(32 if taps > 8 else 64, n_rows)

    def lane_body(cb, carry):
        lanes = pl.ds(pl.multiple_of(cb * LANES, LANES), LANES)
        if shf is not None:
            _shift_copies(win, shf, lanes)
        sums = [None] * taps
        for r0 in range(0, n_rows, rb):
            xv = x_cur[pl.ds(r0, rb), lanes].astype(F32)
            acc = None
            for k in range(taps):
                shifted = _rows_at(win, shf, r0 + taps - 1 - k, rb, lanes)
                term = jnp.tile(wrep[pl.ds(k * SUB, SUB), lanes], (rb // SUB, 1)) * shifted
                acc = term if acc is None else acc + term
                part = _rows8(xv * shifted)
                sums[k] = part if sums[k] is None else sums[k] + part
            dx_out[pl.ds(r0, rb), lanes] = acc.astype(dx_out.dtype)
        for k in range(taps):
            dw_acc[pl.ds(k * SUB, SUB), lanes] += sums[k]
        return carry

    lax.fori_loop(0, width // LANES, lane_body, 0)


def _fold8(acc_ref, taps):
    return jnp.concatenate(
        [jnp.sum(acc_ref[pl.ds(k * SUB, SUB), :], axis=0, keepdims=True) for k in range(taps)], axis=0)


def _seq_tile(s_len):
    return min(512, s_len)


def _mm_tile(s_len):
    return min(512, s_len)


def _ff_chunk(ff):
    best = LANES
    for c in range(LANES, 1408 + 1, LANES):
        if ff % c == 0:
            best = c
    return best


def _col_tile(n):
    for c in (512, 1408, 256, LANES):
        if n % c == 0:
            return c
    return n


def _rms_matmul(x, g, wt, b, *, name, dep=None):
    s_len, d = x.shape
    n = wt.shape[0]
    tm = _mm_tile(s_len)
    cn = _col_tile(n)
    has_bias = b is not None

    def body(*refs):
        x_ref, g_ref, w_ref = refs[0:3]
        b_ref = refs[3] if has_bias else None
        o_ref, h_ref = refs[-2:]

        def blk(rows):
            xv = x_ref[rows, :]
            r = lax.rsqrt(jnp.mean(xv * xv, axis=-1, keepdims=True) + EPS)
            h_ref[rows, :] = ((xv * r) * g_ref[...]).astype(BF16)

        rb = min(128, tm)
        for r0 in range(0, tm, rb):
            blk(pl.ds(r0, rb))
        for j in range(n // cn):
            acc = _dot_nt(h_ref[...], w_ref[j * cn:(j + 1) * cn, :])
            if has_bias:
                acc = acc + b_ref[:, j * cn:(j + 1) * cn]
            o_ref[:, j * cn:(j + 1) * cn] = acc.astype(BF16)

    in_specs = [pl.BlockSpec((tm, d), lambda i: (i, 0)), _resident((1, d), lambda i: (0, 0)),
                _resident((n, d), lambda i: (0, 0))]
    args = [x, g, wt]
    if has_bias:
        in_specs.append(_resident((1, n), lambda i: (0, 0)))
        args.append(b)
    in_specs.append(ANY)
    args.append(x if dep is None else dep)
    return pl.pallas_call(
        body, grid=(s_len // tm,), in_specs=in_specs,
        out_specs=[pl.BlockSpec((tm, n), lambda i: (i, 0)), pl.BlockSpec((tm, d), lambda i: (i, 0))],
        out_shape=[jax.ShapeDtypeStruct((s_len, n), BF16), jax.ShapeDtypeStruct((s_len, d), BF16)],
        compiler_params=_params("parallel"), name=name,
    )(*args)


def _mix_windows(u_ref, uh_ref, gw, pw, first, t):
    c = D_CONF
    uh = uh_ref[...].astype(F32)
    gw[0:HALO, :] = jnp.where(first, 0.0, uh[:, 0:c] * _sigmoid(uh[:, c:2 * c]))
    pw[0:HALO3, :] = jnp.where(first, 0.0, uh[HALO - HALO3:HALO, 3 * c:4 * c] * uh[HALO - HALO3:HALO, 4 * c:5 * c])

    def blk(rows):
        dst = pl.ds(pl.multiple_of(rows.start + HALO, SUB), rows.size)
        gw[dst, :] = u_ref[rows, 0:c].astype(F32) * _sigmoid(u_ref[rows, c:2 * c].astype(F32))
        dst3 = pl.ds(pl.multiple_of(rows.start + HALO3, SUB), rows.size)
        pw[dst3, :] = u_ref[rows, 3 * c:4 * c].astype(F32) * u_ref[rows, 4 * c:5 * c].astype(F32)
    _row_loop(t, 64, blk)


def _mix_fwd(u, x0, wa, ba, lg, lb, wb, w_out, *, name):
    s_len, d_in = u.shape
    d = x0.shape[1]
    c = D_CONF
    t = _seq_tile(s_len)
    per = t // HALO

    def body(u_ref, uh_ref, x0_ref, wa_ref, ba_ref, lg_ref, lb_ref, wb_ref, wo_ref, y_ref, x1_ref, ca, cb,
             gw, pw, wrep_a, wrep_b, shf):
        first = pl.program_id(0) == 0
        _mix_windows(u_ref, uh_ref, gw, pw, first, t)
        _replicate_taps(wa_ref, wrep_a, CONF_K)
        _replicate_taps(wb_ref, wrep_b, SHORT_K)
        _conv_taps(gw, wrep_a, ca, taps=CONF_K, n_rows=t, base=HALO, width=c, bias_ref=ba_ref, shf=shf)
        _conv_taps(pw, wrep_b, cb, taps=SHORT_K, n_rows=t, base=HALO3, width=c)

        def blk(rows):
            cv = ca[rows, :]
            mu = jnp.mean(cv, axis=-1, keepdims=True)
            xc = cv - mu
            var = jnp.mean(xc * xc, axis=-1, keepdims=True)
            ln = (xc * lax.rsqrt(var + EPS)) * lg_ref[...] + lb_ref[...]
            y_ref[rows, 0:c] = (ln * _sigmoid(ln)).astype(BF16)
            y_ref[rows, c:2 * c] = (u_ref[rows, 2 * c:3 * c].astype(F32) * cb[rows, :]).astype(BF16)
        _row_loop(t, 64, blk)
        x1_ref[...] = x0_ref[...] + _dot(y_ref[...], wo_ref[...])

    small = lambda r: _resident((r, c), lambda i: (0, 0))
    return pl.pallas_call(
        body, grid=(s_len // t,),
        in_specs=[pl.BlockSpec((t, d_in), lambda i: (i, 0)),
                  pl.BlockSpec((HALO, d_in), lambda i: (jnp.maximum(i * per - 1, 0), 0)),
                  pl.BlockSpec((t, d), lambda i: (i, 0)),
                  small(CONF_K), small(1), small(1), small(1), small(SHORT_K),
                  _resident((2 * c, d), lambda i: (0, 0))],
        out_specs=[pl.BlockSpec((t, 2 * c), lambda i: (i, 0)), pl.BlockSpec((t, d), lambda i: (i, 0)),
                   pl.BlockSpec((t, c), lambda i: (i, 0)), pl.BlockSpec((t, c), lambda i: (i, 0))],
        out_shape=[jax.ShapeDtypeStruct((s_len, 2 * c), BF16), jax.ShapeDtypeStruct((s_len, d), F32),
                   jax.ShapeDtypeStruct((s_len, c), F32), jax.ShapeDtypeStruct((s_len, c), F32)],
        scratch_shapes=[pltpu.VMEM((HALO + t, c), F32), pltpu.VMEM((HALO3 + t, c), F32),
                        pltpu.VMEM((CONF_K * SUB, c), F32), pltpu.VMEM((SHORT_K * SUB, c), F32),
                        pltpu.VMEM((SUB - 1, HALO + t, c), F32)],
        compiler_params=_params("arbitrary"), name=name,
    )(u, u, x0, wa, ba, lg, lb, wb, w_out)


def _ffn_windows(ug_ref, ugh_ref, uv_ref, uvh_ref, gwin, vwin, first, t):
    lo = HALO3_BLK - HALO3
    gwin[0:HALO3, :] = jnp.where(first, 0.0, ugh_ref[...].astype(F32)[lo:HALO3_BLK])
    vwin[0:HALO3, :] = jnp.where(first, 0.0, uvh_ref[...].astype(F32)[lo:HALO3_BLK])

    def blk(rows):
        dst = pl.ds(pl.multiple_of(rows.start + HALO3, SUB), rows.size)
        gwin[dst, :] = ug_ref[rows, :].astype(F32)
        vwin[dst, :] = uv_ref[rows, :].astype(F32)
    _row_loop(t, 64, blk)


def _ffn_fwd(uf, x1, wf, w_down, *, name, dep=None):
    s_len, ff2 = uf.shape
    ff = ff2 // 2
    d = x1.shape[1]
    t = _seq_tile(s_len)
    fc = _ff_chunk(ff)
    nc = ff // fc
    per = t // HALO3_BLK

    def body(ug_ref, ugh_ref, uv_ref, uvh_ref, x1_ref, wfg_ref, wfv_ref, wd_ref, dep_ref,
             act_ref, x2_ref, cg_ref, cv_ref, gwin, vwin, cg, cv, wrep_g, wrep_v):
        first = pl.program_id(0) == 0
        _ffn_windows(ug_ref, ugh_ref, uv_ref, uvh_ref, gwin, vwin, first, t)
        _replicate_taps(wfg_ref, wrep_g, SHORT_K)
        _replicate_taps(wfv_ref, wrep_v, SHORT_K)
        _conv_taps(gwin, wrep_g, cg, taps=SHORT_K, n_rows=t, base=HALO3, width=fc)
        _conv_taps(vwin, wrep_v, cv, taps=SHORT_K, n_rows=t, base=HALO3, width=fc)

        def blk(rows):
            gv = cg[rows, :]
            vv = cv[rows, :]
            cg_ref[rows, :] = gv.astype(BF16)
            cv_ref[rows, :] = vv.astype(BF16)
            act_ref[rows, :] = ((gv * _sigmoid(gv)) * vv).astype(BF16)
        _row_loop(t, 32, blk, unroll=2)

        @pl.when(pl.program_id(1) == 0)
        def _():
            x2_ref[...] = x1_ref[...]
        x2_ref[...] += _dot(act_ref[...], wd_ref[...])

    halo_map = lambda off: (lambda i, j: (jnp.maximum(i * per - 1, 0), j + off))
    return pl.pallas_call(
        body, grid=(s_len // t, nc),
        in_specs=[pl.BlockSpec((t, fc), lambda i, j: (i, j)), pl.BlockSpec((HALO3_BLK, fc), halo_map(0)),
                  pl.BlockSpec((t, fc), lambda i, j: (i, j + nc)), pl.BlockSpec((HALO3_BLK, fc), halo_map(nc)),
                  pl.BlockSpec((t, d), lambda i, j: (i, 0)),
                  pl.BlockSpec((SHORT_K, fc), lambda i, j: (0, j)),
                  pl.BlockSpec((SHORT_K, fc), lambda i, j: (0, j + nc)),
                  pl.BlockSpec((fc, d), lambda i, j: (j, 0)), ANY],
        out_specs=[pl.BlockSpec((t, fc), lambda i, j: (i, j)), pl.BlockSpec((t, d), lambda i, j: (i, 0)),
                   pl.BlockSpec((t, fc), lambda i, j: (i, j)), pl.BlockSpec((t, fc), lambda i, j: (i, j))],
        out_shape=[jax.ShapeDtypeStruct((s_len, ff), BF16), jax.ShapeDtypeStruct((s_len, d), F32),
                   jax.ShapeDtypeStruct((s_len, ff), BF16), jax.ShapeDtypeStruct((s_len, ff), BF16)],
        scratch_shapes=[pltpu.VMEM((HALO3 + t, fc), F32), pltpu.VMEM((HALO3 + t, fc), F32),
                        pltpu.VMEM((t, fc), F32), pltpu.VMEM((t, fc), F32),
                        pltpu.VMEM((SHORT_K * SUB, fc), F32), pltpu.VMEM((SHORT_K * SUB, fc), F32)],
        compiler_params=_params("parallel", "arbitrary"), name=name,
    )(uf, uf, uf, uf, x1, wf, wf, w_down, uf if dep is None else dep)


def _loss_bwd(x, g, target, *, name):
    s_len, d = x.shape
    t = _seq_tile(s_len)

    def body(x_ref, g_ref, t_ref, l_ref, dx_ref, dxb_ref, dg_ref):
        @pl.when(pl.program_id(0) == 0)
        def _():
            l_ref[...] = jnp.zeros_like(l_ref)
            dg_ref[...] = jnp.zeros_like(dg_ref)

        def blk(rows):
            xv = x_ref[rows, :]
            r = lax.rsqrt(jnp.mean(xv * xv, axis=-1, keepdims=True) + EPS)
            xn = xv * r
            e = xn * g_ref[...] - t_ref[rows, :]
            l_ref[...] += _rows8(e * e)
            dy = e * (1.0 / d)
            dg_ref[...] += _rows8(dy * xn)
            dn = dy * g_ref[...]
            dx = r * (dn - xn * jnp.mean(dn * xn, axis=-1, keepdims=True))
            dx_ref[rows, :] = dx
            dxb_ref[rows, :] = dx.astype(BF16)
        _row_loop(t, 64, blk)

    row = pl.BlockSpec((t, d), lambda i: (i, 0))
    part = pl.BlockSpec((SUB, d), lambda i: (0, 0))
    return pl.pallas_call(
        body, grid=(s_len // t,),
        in_specs=[row, _resident((1, d), lambda i: (0, 0)), row],
        out_specs=[part, row, row, part],
        out_shape=[jax.ShapeDtypeStruct((SUB, d), F32), jax.ShapeDtypeStruct((s_len, d), F32),
                   jax.ShapeDtypeStruct((s_len, d), BF16), jax.ShapeDtypeStruct((SUB, d), F32)],
        compiler_params=_params("arbitrary"), name=name,
    )(x, g, target)


def _ffn_bwd(dx2, uf, cg, cv, wf, w_down, *, name, dep=None):
    s_len, ff2 = uf.shape
    ff = ff2 // 2
    d = dx2.shape[1]
    t = _seq_tile(s_len)
    n_t = s_len // t
    fc = _ff_chunk(ff)
    nc = ff // fc

    def body(dx_ref, ug_ref, uv_ref, cg_ref, cv_ref, wfg_ref, wfv_ref, wd_ref, dep_ref,
             dug_ref, duv_ref, dwg_ref, dwv_ref, dact, dgw, dvw, awg, awv, wrep_g, wrep_v):
        i = pl.program_id(1)

        @pl.when(i == 0)
        def _():
            dgw[t:t + HALO3, :] = jnp.zeros((HALO3, fc), F32)
            dvw[t:t + HALO3, :] = jnp.zeros((HALO3, fc), F32)
            awg[...] = jnp.zeros_like(awg)
            awv[...] = jnp.zeros_like(awv)

        _replicate_taps(wfg_ref, wrep_g, SHORT_K)
        _replicate_taps(wfv_ref, wrep_v, SHORT_K)

        def blk(rows):
            gv = cg_ref[rows, :].astype(F32)
            sg = _sigmoid(gv)
            da = dact[rows, :]
            dgw[rows, :] = (da * cv_ref[rows, :].astype(F32)) * (sg * (1.0 + gv * (1.0 - sg)))
            dvw[rows, :] = da * (gv * sg)

        dact[...] = _dot_nt(dx_ref[...], wd_ref[...])
        _row_loop(t, 32, blk, unroll=2)

        _conv_bwd_taps(dgw, wrep_g, ug_ref, dug_ref, awg, taps=SHORT_K, n_rows=t, width=fc)
        _conv_bwd_taps(dvw, wrep_v, uv_ref, duv_ref, awv, taps=SHORT_K, n_rows=t, width=fc)
        dgw[t:t + HALO3, :] = dgw[0:HALO3, :]
        dvw[t:t + HALO3, :] = dvw[0:HALO3, :]

        @pl.when(i == n_t - 1)
        def _():
            dwg_ref[...] = _fold8(awg, SHORT_K)
            dwv_ref[...] = _fold8(awv, SHORT_K)

    rev = lambda i: n_t - 1 - i
    gate = pl.BlockSpec((t, fc), lambda j, i: (rev(i), j))
    value = pl.BlockSpec((t, fc), lambda j, i: (rev(i), j + nc))
    return pl.pallas_call(
        body, grid=(nc, n_t),
        in_specs=[pl.BlockSpec((t, d), lambda j, i: (rev(i), 0)), gate, value, gate, gate,
                  pl.BlockSpec((SHORT_K, fc), lambda j, i: (0, j)),
                  pl.BlockSpec((SHORT_K, fc), lambda j, i: (0, j + nc)),
                  pl.BlockSpec((fc, d), lambda j, i: (j, 0)), ANY],
        out_specs=[gate, gate,
                   pl.BlockSpec((SHORT_K, fc), lambda j, i: (0, j)), pl.BlockSpec((SHORT_K, fc), lambda j, i: (0, j))],
        out_shape=[jax.ShapeDtypeStruct((s_len, ff), BF16), jax.ShapeDtypeStruct((s_len, ff), BF16),
                   jax.ShapeDtypeStruct((SHORT_K, ff), F32), jax.ShapeDtypeStruct((SHORT_K, ff), F32)],
        scratch_shapes=[pltpu.VMEM((t, fc), F32),
                        pltpu.VMEM((t + HALO3, fc), F32), pltpu.VMEM((t + HALO3, fc), F32),
                        pltpu.VMEM((SHORT_K * SUB, fc), F32), pltpu.VMEM((SHORT_K * SUB, fc), F32),
                        pltpu.VMEM((SHORT_K * SUB, fc), F32), pltpu.VMEM((SHORT_K * SUB, fc), F32)],
        compiler_params=_params("arbitrary", "arbitrary"), name=name,
    )(dx2, uf, uf, cg, cv, wf, wf, w_down, uf if dep is None else dep)


def _mix_bwd(dx1, u, ca, cb, wa, lg, lb, wb, w_out, *, name, dep=None):
    s_len, d_in = u.shape
    d = dx1.shape[1]
    c = D_CONF
    t = _seq_tile(s_len)
    n_t = s_len // t

    def body(dx_ref, u_ref, ca_ref, cb_ref, wa_ref, lg_ref, lb_ref, wb_ref, wo_ref, dep_ref,
             du_ref, dwa_ref, dwb_ref, dba_ref, dlg_ref, dlb_ref, dbin_ref,
             glu, prod, dyc, dcaw, dcbw, dglu, dp, awa, awb, wrep_a, wrep_b, shf):
        i = pl.program_id(0)
        dyc[...] = _dot_nt(dx_ref[...], wo_ref[...])
        _replicate_taps(wa_ref, wrep_a, CONF_K)
        _replicate_taps(wb_ref, wrep_b, SHORT_K)

        @pl.when(i == 0)
        def _():
            dcaw[t:t + HALO, :] = jnp.zeros((HALO, c), F32)
            dcbw[t:t + HALO3, :] = jnp.zeros((HALO3, c), F32)
            awa[...] = jnp.zeros_like(awa)
            awb[...] = jnp.zeros_like(awb)
            dba_ref[...] = jnp.zeros_like(dba_ref)
            dlg_ref[...] = jnp.zeros_like(dlg_ref)
            dlb_ref[...] = jnp.zeros_like(dlb_ref)
            dbin_ref[...] = jnp.zeros_like(dbin_ref)

        def blk1(rows):
            cv = ca_ref[rows, :]
            mu = jnp.mean(cv, axis=-1, keepdims=True)
            xc = cv - mu
            rstd = lax.rsqrt(jnp.mean(xc * xc, axis=-1, keepdims=True) + EPS)
            nrm = xc * rstd
            ln = nrm * lg_ref[...] + lb_ref[...]
            sg = _sigmoid(ln)
            dln = dyc[rows, 0:c] * (sg * (1.0 + ln * (1.0 - sg)))
            dlg_ref[...] += _rows8(dln * nrm)
            dlb_ref[...] += _rows8(dln)
            dn = dln * lg_ref[...]
            dca = rstd * (dn - jnp.mean(dn, axis=-1, keepdims=True)
                          - nrm * jnp.mean(dn * nrm, axis=-1, keepdims=True))
            dcaw[rows, :] = dca
            dba_ref[...] += _rows8(dca)
            ds = dyc[rows, c:2 * c]
            dgb = ds * cb_ref[rows, :]
            dcbw[rows, :] = ds * u_ref[rows, 2 * c:3 * c].astype(F32)
            du_ref[rows, 2 * c:3 * c] = dgb.astype(BF16)
            dbin_ref[:, 2 * c:3 * c] += _rows8(dgb)
            glu[rows, :] = u_ref[rows, 0:c].astype(F32) * _sigmoid(u_ref[rows, c:2 * c].astype(F32))
            prod[rows, :] = u_ref[rows, 3 * c:4 * c].astype(F32) * u_ref[rows, 4 * c:5 * c].astype(F32)
        _row_loop(t, 64, blk1, unroll=2)

        _conv_bwd_taps(dcaw, wrep_a, glu, dglu, awa, taps=CONF_K, n_rows=t, width=c, shf=shf)
        _conv_bwd_taps(dcbw, wrep_b, prod, dp, awb, taps=SHORT_K, n_rows=t, width=c)
        dcaw[t:t + HALO, :] = dcaw[0:HALO, :]
        dcbw[t:t + HALO3, :] = dcbw[0:HALO3, :]

        def blk2(rows):
            av = u_ref[rows, 0:c].astype(F32)
            sg = _sigmoid(u_ref[rows, c:2 * c].astype(F32))
            dg = dglu[rows, :]
            d_av = dg * sg
            d_ag = (dg * av) * (sg * (1.0 - sg))
            dpv = dp[rows, :]
            d_gc = dpv * u_ref[rows, 4 * c:5 * c].astype(F32)
            d_vs = dpv * u_ref[rows, 3 * c:4 * c].astype(F32)
            du_ref[rows, 0:c] = d_av.astype(BF16)
            du_ref[rows, c:2 * c] = d_ag.astype(BF16)
            du_ref[rows, 3 * c:4 * c] = d_gc.astype(BF16)
            du_ref[rows, 4 * c:5 * c] = d_vs.astype(BF16)
            dbin_ref[:, 0:c] += _rows8(d_av)
            dbin_ref[:, c:2 * c] += _rows8(d_ag)
            dbin_ref[:, 3 * c:4 * c] += _rows8(d_gc)
            dbin_ref[:, 4 * c:5 * c] += _rows8(d_vs)
        _row_loop(t, 64, blk2)

        @pl.when(i == n_t - 1)
        def _():
            dwa_ref[...] = _fold8(awa, CONF_K)
            dwb_ref[...] = _fold8(awb, SHORT_K)

    rev = lambda i: n_t - 1 - i
    small_in = lambda r: _resident((r, c), lambda i: (0, 0))
    small = lambda r: pl.BlockSpec((r, c), lambda i: (0, 0))
    return pl.pallas_call(
        body, grid=(n_t,),
        in_specs=[pl.BlockSpec((t, d), lambda i: (rev(i), 0)),
                  pl.BlockSpec((t, d_in), lambda i: (rev(i), 0)),
                  pl.BlockSpec((t, c), lambda i: (rev(i), 0)), pl.BlockSpec((t, c), lambda i: (rev(i), 0)),
                  small_in(CONF_K), small_in(1), small_in(1), small_in(SHORT_K),
                  _resident((2 * c, d), lambda i: (0, 0)), ANY],
        out_specs=[pl.BlockSpec((t, d_in), lambda i: (rev(i), 0)),
                   small(CONF_K), small(SHORT_K), small(SUB), small(SUB), small(SUB),
                   pl.BlockSpec((SUB, d_in), lambda i: (0, 0))],
        out_shape=[jax.ShapeDtypeStruct((s_len, d_in), BF16),
                   jax.ShapeDtypeStruct((CONF_K, c), F32), jax.ShapeDtypeStruct((SHORT_K, c), F32),
                   jax.ShapeDtypeStruct((SUB, c), F32), jax.ShapeDtypeStruct((SUB, c), F32),
                   jax.ShapeDtypeStruct((SUB, c), F32), jax.ShapeDtypeStruct((SUB, d_in), F32)],
        scratch_shapes=[pltpu.VMEM((t, c), F32), pltpu.VMEM((t, c), F32), pltpu.VMEM((t, 2 * c), F32),
                        pltpu.VMEM((t + HALO, c), F32), pltpu.VMEM((t + HALO3, c), F32),
                        pltpu.VMEM((t, c), F32), pltpu.VMEM((t, c), F32),
                        pltpu.VMEM((CONF_K * SUB, c), F32), pltpu.VMEM((SHORT_K * SUB, c), F32),
                        pltpu.VMEM((CONF_K * SUB, c), F32), pltpu.VMEM((SHORT_K * SUB, c), F32),
                        pltpu.VMEM((SUB - 1, t + HALO, c), F32)],
        compiler_params=_params("arbitrary"), name=name,
    )(dx1, u, ca, cb, wa, lg, lb, wb, w_out, u if dep is None else dep)


def _matmul_tn(a, b, *, name, into=None, part=0, n_parts=1):
    s_len, k = a.shape
    n = b.shape[1]
    tk = _col_tile(k)
    per = k // tk

    def body(*refs):
        a_ref, b_ref = refs[0], refs[1]
        o_ref = refs[-1]
        o_ref[...] = _dot_tn(a_ref[...], b_ref[...]).astype(BF16)

    in_specs = [pl.BlockSpec((s_len, tk), lambda j: (0, j)), _resident((s_len, n), lambda j: (0, 0))]
    args = [a, b]
    aliases = {}
    if into is not None:
        in_specs.append(ANY)
        args.append(into)
        aliases = {2: 0}
    return pl.pallas_call(
        body, grid=(per,), in_specs=in_specs,
        out_specs=pl.BlockSpec((tk, n), lambda j: (part * per + j, 0)),
        out_shape=jax.ShapeDtypeStruct((n_parts * k, n), BF16),
        input_output_aliases=aliases,
        compiler_params=_params("parallel"), name=name,
    )(*args)


def _matmul_rmsbwd(dzs, wt, x, g, dx_in, *, name, dep=None):
    s_len, d = x.shape
    n_z = len(dzs)
    nj = dzs[0].shape[1]
    t = _mm_tile(s_len)

    def body(*refs):
        dz_refs = refs[0:n_z]
        w_refs = refs[n_z:2 * n_z]
        x_ref, g_ref, dxi_ref, _, dx_ref, dxb_ref, dg_ref, dh = refs[2 * n_z:]

        @pl.when(pl.program_id(0) == 0)
        def _():
            dg_ref[...] = jnp.zeros_like(dg_ref)

        def blk(rows):
            xv = x_ref[rows, :]
            r = lax.rsqrt(jnp.mean(xv * xv, axis=-1, keepdims=True) + EPS)
            xn = xv * r
            dhv = dh[rows, :]
            dg_ref[...] += _rows8(dhv * xn)
            dn = dhv * g_ref[...]
            dx = dxi_ref[rows, :] + r * (dn - xn * jnp.mean(dn * xn, axis=-1, keepdims=True))
            dx_ref[rows, :] = dx
            dxb_ref[rows, :] = dx.astype(BF16)

        half = t // 2
        rb = min(128, half)
        for lo in range(0, t, half):
            acc = _dot(dz_refs[0][lo:lo + half, :], w_refs[0][...])
            for q in range(1, n_z):
                acc = acc + _dot(dz_refs[q][lo:lo + half, :], w_refs[q][...])
            dh[lo:lo + half, :] = acc
            for r0 in range(lo, lo + half, rb):
                blk(pl.ds(r0, rb))

    row = pl.BlockSpec((t, d), lambda i: (i, 0))
    in_specs = [pl.BlockSpec((t, nj), lambda i: (i, 0)) for _ in range(n_z)]
    in_specs += [_resident((nj, d), functools.partial(lambda q, i: (q, 0), q)) for q in range(n_z)]
    in_specs += [row, _resident((1, d), lambda i: (0, 0)), row, ANY]
    return pl.pallas_call(
        body, grid=(s_len // t,), in_specs=in_specs,
        out_specs=[row, row, pl.BlockSpec((SUB, d), lambda i: (0, 0))],
        out_shape=[jax.ShapeDtypeStruct((s_len, d), F32), jax.ShapeDtypeStruct((s_len, d), BF16),
                   jax.ShapeDtypeStruct((SUB, d), F32)],
        scratch_shapes=[pltpu.VMEM((t, d), F32)],
        compiler_params=_params("arbitrary"), name=name,
    )(*dzs, *([wt] * n_z), x, g, dx_in, x if dep is None else dep)


def _row(v):
    return v.reshape(1, -1)


def _layer_fwd(x0, p, tag, dep=None, before_up=None):
    u, h1 = _rms_matmul(x0, _row(p["mix_norm_g"]), p["w_in_t"], _row(p["b_in"]), name=f"in_proj_{tag}", dep=dep)
    ycat, x1, ca, cb = _mix_fwd(u, x0, p["conv_a_w"], _row(p["conv_a_b"]), _row(p["ln_a_g"]), _row(p["ln_a_b"]),
                            p["conv_b_w"], p["w_out"], name=f"mix_fwd_{tag}")
    if before_up is not None:
        before_up(x1)
    uf, h2 = _rms_matmul(x1, _row(p["ffn_norm_g"]), p["w_up_t"], None, name=f"up_proj_{tag}")
    act, x2, cg, cv = _ffn_fwd(uf, x1, p["conv_f_w"], p["w_down"], name=f"ffn_fwd_{tag}")
    return x2, dict(x0=x0, h1=h1, u=u, ca=ca, cb=cb, ycat=ycat, x1=x1, h2=h2, uf=uf, cg=cg, cv=cv, act=act)


def _layer_bwd(dx2, dx2_b, p, saved, tag, ffn_grads, ffn_sent, mix_grads, mix_sent, dep=None):
    dug, duv, dwf_g, dwf_v = _ffn_bwd(dx2_b, saved["uf"], saved["cg"], saved["cv"], p["conv_f_w"], p["w_down"],
                                      name=f"ffn_bwd_{tag}", dep=dep)
    g_down = _matmul_tn(saved["act"], dx2_b, name=f"dw_down_{tag}")
    g_up = _matmul_tn(dug, saved["h2"], name=f"dw_up_g_{tag}", n_parts=2)
    g_up = _matmul_tn(duv, saved["h2"], name=f"dw_up_v_{tag}", into=g_up, part=1, n_parts=2)
    dep_ffn = ffn_grads(dict(w_up=g_up, w_down=g_down), dx2_b)
    dx1, dx1_b, dg2 = _matmul_rmsbwd([dug, duv], p["w_up_t"], saved["x1"], _row(p["ffn_norm_g"]), dx2,
                                     name=f"dh_ffn_{tag}", dep=dep_ffn)
    du, dwa, dwb, dba, dlg, dlb, dbin = _mix_bwd(
        dx1_b, saved["u"], saved["ca"], saved["cb"], p["conv_a_w"], _row(p["ln_a_g"]), _row(p["ln_a_b"]),
        p["conv_b_w"], p["w_out"], name=f"mix_bwd_{tag}", dep=ffn_sent(dx1_b))
    g_out = _matmul_tn(saved["ycat"], dx1_b, name=f"dw_out_{tag}")
    g_in = _matmul_tn(du, saved["h1"], name=f"dw_in_{tag}")
    conv = dict(conv_a_w=dwa, conv_b_w=dwb, conv_f_w=jnp.concatenate([dwf_g, dwf_v], axis=1))
    dep_mix = mix_grads(dict(w_in=g_in, w_out=g_out), conv, dx1_b)
    dx0, dx0_b, dg1 = _matmul_rmsbwd([du], p["w_in_t"], saved["x0"], _row(p["mix_norm_g"]), dx1,
                                     name=f"dh_mix_{tag}", dep=dep_mix)
    rep = dict(mix_norm_g=dg1, b_in=dbin, conv_a_b=dba, ln_a_g=dlg, ln_a_b=dlb, ffn_norm_g=dg2)
    return dx0, dx0_b, rep, mix_sent(dx0_b)


def _place():
    return lax.axis_index("x"), lax.axis_index("y"), lax.axis_index("c")


def _all_gather(arrs, *, name):
    n_a = len(arrs)

    def body(*refs):
        ins = refs[0:n_a]
        outs = refs[n_a:2 * n_a]
        send_sems, recv_sems, local_sems = refs[2 * n_a:]
        x, y, c = _place()
        sibling = (x, y, 1 - c)
        chips = [(1 - x, y), (x, 1 - y), (1 - x, 1 - y)]

        def slot(a, px, py, pc):
            return outs[a].at[4 * px + 2 * py + pc]

        def copy(a, k, block, to, src=None):
            return pltpu.make_async_remote_copy(
                src_ref=slot(a, *block) if src is None else src, dst_ref=slot(a, *block),
                send_sem=send_sems.at[a, k], recv_sem=recv_sems.at[a, k],
                device_id=to, device_id_type=MESH)

        me = (x, y, c)
        mine = [pltpu.make_async_copy(ins[a], slot(a, *me), local_sems.at[a]) for a in range(n_a)]
        for cp in mine:
            cp.start()
        started = []
        for a in range(n_a):
            first = [copy(a, 0, me, sibling, src=ins[a])]
            first += [copy(a, 1 + j, me, (*chip, c), src=ins[a]) for j, chip in enumerate(chips)]
            for cp in first:
                cp.start()
            started += first
        for a in range(n_a):
            for j, chip in enumerate(chips):
                copy(a, 1 + j, (*chip, c), me).wait_recv()
                passed = copy(a, 4 + j, (*chip, c), sibling)
                passed.start()
                started.append(passed)
        for a in range(n_a):
            copy(a, 0, sibling, me).wait_recv()
            for j, chip in enumerate(chips):
                copy(a, 4 + j, (*chip, 1 - c), me).wait_recv()
        for cp in started:
            cp.wait_send()
        for cp in mine:
            cp.wait()

    return pl.pallas_call(
        body, in_specs=[ANY] * n_a, out_specs=[ANY] * n_a,
        out_shape=[jax.ShapeDtypeStruct((N_DEV, *a.shape), a.dtype) for a in arrs],
        scratch_shapes=[pltpu.SemaphoreType.DMA((n_a, 7)), pltpu.SemaphoreType.DMA((n_a, 7)),
                        pltpu.SemaphoreType.DMA((n_a,))],
        name=name,
    )(*arrs)


def _row_tile(r, cap):
    for tr in range(min(cap, r) // 16 * 16, 0, -16):
        if r % tr == 0:
            return tr
    return r


def _pair_sum(mines, theirs, where, *, name):
    n_a = len(mines)
    n_chip = mines[0].shape[0]

    def body(where_ref, *refs):
        a_refs = refs[0:n_a]
        b_refs = refs[n_a:2 * n_a]
        p_refs = refs[2 * n_a:3 * n_a]
        l_refs = refs[3 * n_a:4 * n_a]
        q = pl.program_id(0)
        for a in range(n_a):
            p_refs[a][...] = (a_refs[a][...].astype(F32) + b_refs[a][...].astype(F32)).astype(p_refs[a].dtype)

        @pl.when(q == where_ref[1])
        def _():
            for a in range(n_a):
                l_refs[a][...] = p_refs[a][...]

    in_specs, out_p, out_l, shapes = [], [], [], []
    for m in mines:
        _, _, r, c = m.shape
        in_specs.append(pl.BlockSpec((None, None, r, c), lambda q, where_ref: (q, where_ref[0], 0, 0)))
    for m in mines:
        _, _, r, c = m.shape
        in_specs.append(pl.BlockSpec((None, r, c), lambda q, where_ref: (q, 0, 0)))
        out_p.append(pl.BlockSpec((None, r, c), lambda q, where_ref: (q, 0, 0)))
        out_l.append(pl.BlockSpec((None, r, c), lambda q, where_ref: (where_ref[1], 0, 0)))
        shapes.append(jax.ShapeDtypeStruct((n_chip, r, c), m.dtype))
    res = pl.pallas_call(
        body,
        grid_spec=pltpu.PrefetchScalarGridSpec(num_scalar_prefetch=1, grid=(n_chip,), in_specs=in_specs,
                                               out_specs=out_p + out_l),
        out_shape=shapes + shapes,
        compiler_params=_params("arbitrary"), name=name,
    )(where, *mines, *theirs)
    return list(res[:n_a]), list(res[n_a:])


HBM = pl.BlockSpec(memory_space=pltpu.HBM)
SEM = pl.BlockSpec(memory_space=pltpu.SEMAPHORE)
EFFECT = pltpu.SideEffectType.DATAFLOW_SIDE_EFFECTING


def _in_hbm(a):
    return pltpu.with_memory_space_constraint(a, pltpu.HBM)


def _split_start(srcs, lands, plan, n_copies, after, *, name):
    n_s, n_l = len(srcs), len(lands)

    def body(*refs):
        src_refs = refs[0:n_s]
        land_refs = refs[n_s:n_s + n_l]
        send_sems, recv_sems = refs[n_s + n_l + 1], refs[n_s + n_l + 2]
        token = refs[-1]
        for cp in plan(src_refs, land_refs, send_sems, recv_sems):
            cp.start()
        token[...] = jnp.zeros_like(token)

    thru = [pltpu.HBM(a.shape, a.dtype) for a in list(srcs) + list(lands)]
    res = pl.pallas_call(
        body, name=name,
        out_shape=(pltpu.SemaphoreType.DMA((n_copies,)), pltpu.SemaphoreType.DMA((n_copies,)), *thru,
                   jax.ShapeDtypeStruct((SUB, LANES), F32)),
        in_specs=[HBM] * (n_s + n_l) + [ANY],
        out_specs=(SEM, SEM, *([HBM] * (n_s + n_l)), pl.BlockSpec(memory_space=pltpu.VMEM)),
        input_output_aliases={i: 2 + i for i in range(n_s + n_l)},
        compiler_params=pltpu.CompilerParams(has_side_effects=EFFECT),
    )(*[_in_hbm(a) for a in srcs], *[_in_hbm(a) for a in lands], _in_hbm(after))
    return res[0], res[1], list(res[2:2 + n_s]), list(res[2 + n_s:2 + n_s + n_l]), res[-1]


def _split_wait(send_sems, recv_sems, srcs, lands, after, plan, *, name):
    n_s, n_l = len(srcs), len(lands)

    def body(*refs):
        src_refs = refs[0:n_s]
        land_refs = refs[n_s:n_s + n_l]
        send, recv = refs[n_s + n_l], refs[n_s + n_l + 1]
        for cp in plan(src_refs, land_refs, send, recv):
            cp.wait_send()
            cp.wait_recv()

    res = pl.pallas_call(
        body, name=name,
        out_shape=tuple(pltpu.HBM(a.shape, a.dtype) for a in list(srcs) + list(lands)),
        in_specs=[HBM] * (n_s + n_l) + [SEM, SEM, ANY],
        out_specs=tuple([HBM] * (n_s + n_l)),
        input_output_aliases={i: i for i in range(n_s + n_l)},
        compiler_params=pltpu.CompilerParams(has_side_effects=EFFECT),
    )(*srcs, *lands, send_sems, recv_sems, _in_hbm(after))
    return list(res[:n_s]), list(res[n_s:])


def _remote(src, dst, send_sems, recv_sems, k, to):
    return pltpu.make_async_remote_copy(src_ref=src, dst_ref=dst, send_sem=send_sems.at[k], recv_sem=recv_sems.at[k],
                                        device_id=to, device_id_type=MESH)


def _gather_plan_first(src_refs, land_refs, send_sems, recv_sems):
    x, y, c = _place()
    me = 4 * x + 2 * y + c
    peers = [(x, y, 1 - c), (1 - x, y, c), (x, 1 - y, c), (1 - x, 1 - y, c)]
    return [_remote(src, land.at[me], send_sems, recv_sems, 4 * a + k, to)
            for a, (src, land) in enumerate(zip(src_refs, land_refs)) for k, to in enumerate(peers)]


def _gather_plan_second(src_refs, land_refs, send_sems, recv_sems):
    x, y, c = _place()
    chips = [(1 - x, y), (x, 1 - y), (1 - x, 1 - y)]
    out = []
    for a, land in enumerate(land_refs):
        for j, (px, py) in enumerate(chips):
            slot = land.at[4 * px + 2 * py + c]
            out.append(_remote(slot, slot, send_sems, recv_sems, 3 * a + j, (x, y, 1 - c)))
    return out


def _siblings_plan(src_refs, land_refs, send_sems, recv_sems):
    x, y, c = _place()
    return [_remote(src.at[:, 1 - c], land, send_sems, recv_sems, a, (x, y, 1 - c))
            for a, (src, land) in enumerate(zip(src_refs, land_refs))]


def _chips_plan(src_refs, land_refs, send_sems, recv_sems):
    x, y, c = _place()
    my_chip = 2 * x + y
    chips = [(1 - x, y), (x, 1 - y), (1 - x, 1 - y)]
    return [_remote(src.at[2 * px + py], land.at[my_chip], send_sems, recv_sems, 3 * a + j, (px, py, c))
            for a, (src, land) in enumerate(zip(src_refs, land_refs)) for j, (px, py) in enumerate(chips)]


def _gather_landing(shard, me):
    return lax.dynamic_update_index_in_dim(lax.empty((N_DEV, *shard.shape), shard.dtype), shard, me, 0)


def _adamw_math(g, w, m, v):
    m = ADAM_B1 * m + (1.0 - ADAM_B1) * g
    v = ADAM_B2 * v + (1.0 - ADAM_B2) * (g * g)
    m_hat = m / (1.0 - ADAM_B1 ** ADAM_STEP)
    v_hat = v / (1.0 - ADAM_B2 ** ADAM_STEP)
    delta = -ADAM_LR * (m_hat / (jnp.sqrt(v_hat) + ADAM_EPS) + ADAM_WD * w)
    return delta, m, v


def _adamw_sharded(parts, w, m, v, *, name, dep=None):
    n_layers, r, c = w.shape
    n_chip = parts[0].shape[0]
    tr = _row_tile(r, 384)
    n_i = r // tr

    def body(*refs):
        p_refs = refs[0:n_layers]
        w_ref, m_ref, v_ref, _, g_out, d_out, m_out, v_out = refs[n_layers:]
        layer = pl.program_id(0)
        for l in range(n_layers):
            @pl.when(layer == l)
            def _(l=l):
                g = p_refs[l][0].astype(F32)
                for q in range(1, n_chip):
                    g = g + p_refs[l][q].astype(F32)
                delta, m_new, v_new = _adamw_math(g, w_ref[...], m_ref[...], v_ref[...])
                g_out[...] = g
                d_out[...] = delta
                m_out[...] = m_new
                v_out[...] = v_new

    def part_map(l):
        return lambda layer, i: (0, jnp.where(layer == l, i, jnp.where(layer < l, 0, n_i - 1)), 0)

    blk = pl.BlockSpec((None, tr, c), lambda layer, i: (layer, i, 0))
    return pl.pallas_call(
        body, grid=(n_layers, n_i),
        in_specs=[pl.BlockSpec((n_chip, tr, c), part_map(l)) for l in range(n_layers)] + [blk, blk, blk, ANY],
        out_specs=[blk] * 4, out_shape=[jax.ShapeDtypeStruct((n_layers, r, c), F32)] * 4,
        compiler_params=_params("arbitrary", "arbitrary"), name=name,
    )(*parts, w, m, v, w if dep is None else dep)


def _fold_partials(cols, *, name):
    widths = [c.shape[1] for c in cols]

    def body(*refs):
        o_ref = refs[-1]
        pos = 0
        for ref, width in zip(refs[:-1], widths):
            o_ref[:, pos:pos + width] = jnp.sum(ref[...], axis=0, keepdims=True)
            pos += width

    return pl.pallas_call(body, out_shape=jax.ShapeDtypeStruct((1, sum(widths)), F32), name=name)(*cols)


def _adamw_replicated(parts, names, w, m, v, n_loss, *, name):
    n_dev = parts.shape[0]
    n_layers = w[names[0]].shape[0]
    every = list(names) + ["final_norm_g"]
    n_p = len(every)

    def body(*refs):
        p_ref = refs[0]
        w_refs = dict(zip(every, refs[1:1 + n_p]))
        m_refs = dict(zip(every, refs[1 + n_p:1 + 2 * n_p]))
        v_refs = dict(zip(every, refs[1 + 2 * n_p:1 + 3 * n_p]))
        l_out = refs[1 + 3 * n_p]
        outs = refs[2 + 3 * n_p:]
        o_refs = {n: outs[4 * q:4 * q + 4] for q, n in enumerate(every)}
        acc = p_ref[0]
        for q in range(1, n_dev):
            acc = acc + p_ref[q]
        tot = jnp.sum(acc, axis=0, keepdims=True)
        pos = 0
        where = [(n, l) for l in range(n_layers) for n in names] + [("final_norm_g", 0)]
        for n, l in where:
            width = w_refs[n].shape[1]
            g = tot[:, pos:pos + width]
            pos += width
            row = pl.ds(l, 1)
            delta, m_new, v_new = _adamw_math(g, w_refs[n][row, :], m_refs[n][row, :], v_refs[n][row, :])
            for o, val in zip(o_refs[n], (g, delta, m_new, v_new)):
                o[row, :] = val
        l_out[...] = (0.5 / n_loss) * jnp.sum(tot[:, pos:pos + n_loss], axis=-1, keepdims=True)

    shapes = [jax.ShapeDtypeStruct((1, 1), F32)]
    for n in every:
        shapes += [jax.ShapeDtypeStruct(w[n].shape, F32)] * 4
    res = pl.pallas_call(
        body, out_shape=shapes,
        compiler_params=pltpu.CompilerParams(vmem_limit_bytes=VMEM_LIMIT), name=name,
    )(parts, *[w[n] for n in every], *[m[n] for n in every], *[v[n] for n in every])
    return res[0], {n: res[1 + 4 * q:5 + 4 * q] for q, n in enumerate(every)}


BIG = ("w_in", "w_out", "w_up", "w_down")
COL_SHARDED = ("w_in", "w_up")
CONV = ("conv_a_w", "conv_b_w", "conv_f_w")
REPLICATED = ("mix_norm_g", "b_in", "conv_a_b", "ln_a_g", "ln_a_b", "ffn_norm_g")
KINDS = ("grad", "delta", "m", "v")
FFN_PART = ("w_up", "w_down")
MIX_PART = ("w_in", "w_out")


def _weights_from_gathered(g):
    n_dev, r, c = g.shape
    return g.reshape(n_dev * r, c)


def _slabs_from_full(grad):
    return grad.reshape(N_DEV, grad.shape[0] // N_DEV, grad.shape[1])


def _unwritten(like, *, name):
    return pl.pallas_call(lambda *refs: None, out_specs=[ANY] * len(like), out_shape=list(like), name=name)()


def kernel(x, mix_norm_g, w_in, b_in, conv_a_w, conv_a_b, ln_a_g, ln_a_b, conv_b_w, w_out, ffn_norm_g, w_up, conv_f_w, w_down, final_norm_g, loss_target, m_mix_norm_g, m_w_in, m_b_in, m_conv_a_w, m_conv_a_b, m_ln_a_g, m_ln_a_b, m_conv_b_w, m_w_out, m_ffn_norm_g, m_w_up, m_conv_f_w, m_w_down, m_final_norm_g, v_mix_norm_g, v_w_in, v_b_in, v_conv_a_w, v_conv_a_b, v_ln_a_g, v_ln_a_b, v_conv_b_w, v_w_out, v_ffn_norm_g, v_w_up, v_conv_f_w, v_w_down, v_final_norm_g):
    w = dict(mix_norm_g=mix_norm_g, w_in=w_in, b_in=b_in, conv_a_w=conv_a_w, conv_a_b=conv_a_b, ln_a_g=ln_a_g,
             ln_a_b=ln_a_b, conv_b_w=conv_b_w, w_out=w_out, ffn_norm_g=ffn_norm_g, w_up=w_up, conv_f_w=conv_f_w,
             w_down=w_down, final_norm_g=final_norm_g)
    m = dict(mix_norm_g=m_mix_norm_g, w_in=m_w_in, b_in=m_b_in, conv_a_w=m_conv_a_w, conv_a_b=m_conv_a_b,
             ln_a_g=m_ln_a_g, ln_a_b=m_ln_a_b, conv_b_w=m_conv_b_w, w_out=m_w_out, ffn_norm_g=m_ffn_norm_g,
             w_up=m_w_up, conv_f_w=m_conv_f_w, w_down=m_w_down, final_norm_g=m_final_norm_g)
    v = dict(mix_norm_g=v_mix_norm_g, w_in=v_w_in, b_in=v_b_in, conv_a_w=v_conv_a_w, conv_a_b=v_conv_a_b,
             ln_a_g=v_ln_a_g, ln_a_b=v_ln_a_b, conv_b_w=v_conv_b_w, w_out=v_w_out, ffn_norm_g=v_ffn_norm_g,
             w_up=v_w_up, conv_f_w=v_conv_f_w, w_down=v_w_down, final_norm_g=v_final_norm_g)
    order = list(w)
    n_layers = w_in.shape[0]
    xs = x[0]
    target = loss_target[0]
    flip = lambda a: jnp.transpose(a, (0, 2, 1))
    wt, mt, vt = ({n: flip(d[n]) if n in COL_SHARDED else d[n] for n in BIG} for d in (w, m, v))
    px, py, pc = _place()
    where = jnp.stack([pc, 2 * px + py]).astype(jnp.int32)
    me = 4 * px + 2 * py + pc

    assert BIG == MIX_PART + FFN_PART
    key = lambda n: n + "_t" if n in COL_SHARDED else n
    shard = lambda n, l: wt[n][l].astype(BF16)

    def gather_start(names, l, after, tag):
        shards = [shard(n, l) for n in names]
        lands = [_gather_landing(s, me) for s in shards]
        return _split_start(shards, lands, _gather_plan_first, 4 * len(shards), after, name=f"gather_first_start_{tag}")

    def gather_mid(first, after, tag):
        return _split_wait(first[0], first[1], first[2], first[3], after, _gather_plan_first,
                           name=f"gather_first_wait_{tag}")[1]

    def forward_start(lands, after, tag):
        return _split_start([], lands, _gather_plan_second, 3 * len(lands), after, name=f"gather_second_start_{tag}")

    def forward_finish(second, after, tag):
        return _split_wait(second[0], second[1], [], second[3], after, _gather_plan_second,
                           name=f"gather_second_wait_{tag}")[1]

    gathered = _all_gather([shard(n, 0) for n in MIX_PART] + [w[n] for n in CONV], name="gather_weights_0")
    params = [{n: w[n][l] for n in REPLICATED} for l in range(n_layers)]
    for n, g in zip(CONV, gathered[len(MIX_PART):]):
        n_dev, _, taps, c = g.shape
        full = g.transpose(1, 2, 0, 3).reshape(n_layers, taps, n_dev * c)
        for l in range(n_layers):
            params[l][n] = full[l]
    for n, g in zip(MIX_PART, gathered):
        params[0][key(n)] = _weights_from_gathered(g)
    ffn_first = gather_start(FFN_PART, 0, gathered[0], "0_ffn")
    pending = {}

    h = xs
    saved = []
    for l in range(n_layers):
        nxt = l + 1 if l + 1 < n_layers else None

        def before_up(x1, l=l, nxt=nxt):
            if l == 0:
                second = forward_start(gather_mid(ffn_first, x1, "0_ffn"), x1, "0_ffn")
                after = second[4]
            else:
                second = pending[l]["ffn"]
                after = x1
            if nxt is not None:
                pending[nxt] = dict(first=gather_start(BIG, nxt, after, str(nxt)))
                after = pending[nxt]["first"][4]
            for n, g in zip(FFN_PART, forward_finish(second, after, f"{l}_ffn")):
                params[l][key(n)] = _weights_from_gathered(g)

        h, keep = _layer_fwd(h, params[l], str(l), dep=ffn_first[4] if l == 0 else None, before_up=before_up)
        saved.append(keep)
        if nxt is not None:
            arrived = gather_mid(pending[nxt]["first"], h, str(nxt))
            mix_second = forward_start(arrived[:len(MIX_PART)], h, f"{nxt}_mix")
            pending[nxt]["ffn"] = forward_start(arrived[len(MIX_PART):], mix_second[4], f"{nxt}_ffn")
            for n, g in zip(MIX_PART, forward_finish(mix_second, pending[nxt]["ffn"][4], f"{nxt}_mix")):
                params[nxt][key(n)] = _weights_from_gathered(g)

    def start_siblings(slabs, after, tag):
        mines = [s.reshape(N_CHIP, 2, *s.shape[1:]) for s in slabs]
        lands = _unwritten([jax.ShapeDtypeStruct((N_CHIP, *m.shape[2:]), m.dtype) for m in mines],
                           name=f"reduce_siblings_landing_{tag}")
        return _split_start(mines, lands, _siblings_plan, len(mines), after, name=f"reduce_siblings_start_{tag}")

    def start_chips(sib, after, tag):
        mines, theirs = _split_wait(sib[0], sib[1], sib[2], sib[3], after, _siblings_plan,
                                    name=f"reduce_siblings_wait_{tag}")
        pairs, lands = _pair_sum(mines, theirs, where, name=f"pair_sum_{tag}")
        return _split_start(pairs, lands, _chips_plan, 3 * len(pairs), after, name=f"reduce_chips_start_{tag}")

    def finish_reduce(fly, after, tag):
        return _split_wait(fly[0], fly[1], fly[2], fly[3], after, _chips_plan, name=f"reduce_chips_wait_{tag}")[1]

    loss_sq, dh, dh_b, dgf = _loss_bwd(h, _row(final_norm_g), target, name="loss")
    conv_g = {n: [None] * n_layers for n in CONV}
    rep_g = [None] * n_layers
    siblings = {}
    flights = {}
    token = None
    for l in reversed(range(n_layers)):
        def ffn_grads(g, after, l=l):
            siblings[l, "ffn"] = start_siblings([_slabs_from_full(g[n]) for n in FFN_PART], after, f"{l}_ffn")
            return siblings[l, "ffn"][4]

        def ffn_sent(after, l=l):
            flights[l, "ffn"] = start_chips(siblings[l, "ffn"], after, f"{l}_ffn")
            return flights[l, "ffn"][4]

        def mix_grads(g, conv, after, l=l):
            for n in CONV:
                conv_g[n][l] = conv[n]
            slabs = [_slabs_from_full(g[n]) for n in MIX_PART]
            if l == 0:
                for n in CONV:
                    full = jnp.stack(conv_g[n])
                    _, taps, c = full.shape
                    slabs.append(full.reshape(n_layers, taps, N_DEV, c // N_DEV).transpose(2, 0, 1, 3)
                                 .reshape(N_DEV, n_layers * taps, c // N_DEV))
            siblings[l, "mix"] = start_siblings(slabs, after, f"{l}_mix")
            return siblings[l, "mix"][4]

        def mix_sent(after, l=l):
            flights[l, "mix"] = start_chips(siblings[l, "mix"], after, f"{l}_mix")
            return flights[l, "mix"][4]

        dh, dh_b, rep_g[l], token = _layer_bwd(dh, dh_b, params[l], saved[l], str(l), ffn_grads, ffn_sent,
                                               mix_grads, mix_sent, dep=token)

    sums = {key: finish_reduce(fly, dh, f"{key[0]}_{key[1]}") for key, fly in flights.items() if key != (0, "mix")}
    out = {k: {} for k in KINDS}

    def adamw_big(names, part, dep):
        for q, n in enumerate(names):
            layer_parts = [sums[l, part][q] for l in range(n_layers)]
            res = _adamw_sharded(layer_parts, wt[n], mt[n], vt[n], name=f"adamw_{n}", dep=dep)
            for k, r in zip(KINDS, res):
                out[k][n] = flip(r) if n in COL_SHARDED else r

    adamw_big(FFN_PART, "ffn", token)

    rep_cols = [rep_g[l][n] for l in range(n_layers) for n in REPLICATED] + [dgf, loss_sq]
    rep_all = _all_gather([_fold_partials(rep_cols, name="fold_small")], name="gather_small")[0]
    with_final = lambda d: {**{n: d[n] for n in REPLICATED}, "final_norm_g": _row(d["final_norm_g"])}
    loss, rep_res = _adamw_replicated(rep_all, REPLICATED, with_final(w), with_final(m), with_final(v),
                                      loss_sq.shape[1], name="adamw_small")
    for n, res in rep_res.items():
        for k, r in zip(KINDS, res):
            out[k][n] = r.reshape(w[n].shape)

    last = finish_reduce(flights[0, "mix"], rep_res["b_in"][0], "0_mix")
    sums[0, "mix"] = last[:len(MIX_PART)]
    adamw_big(MIX_PART, "mix", None)
    for n, p in zip(CONV, last[len(MIX_PART):]):
        as_one = lambda a: a.reshape(1, *p.shape[1:])
        for k, r in zip(KINDS, _adamw_sharded([p], as_one(w[n]), as_one(m[n]), as_one(v[n]), name=f"adamw_{n}")):
            out[k][n] = r.reshape(w[n].shape)

    grad_x = dh.reshape(x.shape)
    return (loss.reshape(()), grad_x, *[out["grad"][n] for n in order], *[out["delta"][n] for n in order],
            *[out["m"][n] for n in order], *[out["v"][n] for n in order])
```

```python
import functools

import jax
import jax.numpy as jnp
from jax import lax
from jax.experimental import pallas as pl
from jax.experimental.pallas import tpu as pltpu

F32 = jnp.float32
BF16 = jnp.bfloat16

N_DEV = 8
N_CHIP = 4
D_CONF = 512
CONF_K = 31
SHORT_K = 3
EPS = 1e-6
HALO = 32
HALO3 = 8
HALO3_BLK = 16
LANES = 128
SUB = 8
VMEM_LIMIT = 56 * 1024 * 1024

ADAM_LR = 0.001
ADAM_B1 = 0.9
ADAM_B2 = 0.999
ADAM_EPS = 1e-08
ADAM_WD = 0.01
ADAM_STEP = 10

MESH = pl.DeviceIdType.MESH
ANY = pl.BlockSpec(memory_space=pl.ANY)


def _params(*sem):
    return pltpu.CompilerParams(dimension_semantics=sem, vmem_limit_bytes=VMEM_LIMIT)


def _resident(shape, index_map):
    return pl.BlockSpec(shape, index_map, pipeline_mode=pl.Buffered(1))


def _row_loop(n_rows, rb, fn, unroll=1):
    rb = min(rb, n_rows)

    def body(i, carry):
        fn(pl.ds(pl.multiple_of(i * rb, rb), rb))
        return carry
    lax.fori_loop(0, n_rows // rb, body, 0, unroll=unroll)


def _rows8(v):
    acc = v[0:SUB]
    for k in range(1, v.shape[0] // SUB):
        acc = acc + v[k * SUB:(k + 1) * SUB]
    return acc


def _sigmoid(z):
    return 0.5 * jnp.tanh(0.5 * z) + 0.5


def _dot(a, b):
    return jnp.dot(a, b, preferred_element_type=F32)


def _dot_nt(a, b):
    return lax.dot_general(a, b, (((1,), (1,)), ((), ())), preferred_element_type=F32)


def _dot_tn(a, b):
    return lax.dot_general(a, b, (((0,), (0,)), ((), ())), preferred_element_type=F32)


def _replicate_taps(w_ref, wrep, taps):
    for k in range(taps):
        wrep[pl.ds(k * SUB, SUB), :] = jnp.broadcast_to(w_ref[pl.ds(k, 1), :], (SUB, w_ref.shape[1]))


def _shift_copies(win, shf, lanes):
    span = win.shape[0] - SUB
    for r in range(1, SUB):
        for j0 in range(0, span, 64):
            n = min(64, span - j0)
            shf[r - 1, pl.ds(j0, n), lanes] = win[pl.ds(j0 + r, n), lanes]


def _rows_at(win, shf, off, rb, lanes):
    if shf is None or off % SUB == 0:
        return win[pl.ds(off, rb), lanes]
    return shf[off % SUB - 1, pl.ds(off - off % SUB, rb), lanes]


def _conv_taps(win, wrep, out, *, taps, n_rows, base, width, transposed=False, bias_ref=None, shf=None):
    rb = min(64, n_rows)

    def lane_body(cb, carry):
        lanes = pl.ds(pl.multiple_of(cb * LANES, LANES), LANES)
        if shf is not None:
            _shift_copies(win, shf, lanes)
        for r0 in range(0, n_rows, rb):
            acc = None
            for k in range(taps):
                off = (taps - 1 - k) if transposed else (k - (taps - 1))
                wk = jnp.tile(wrep[pl.ds(k * SUB, SUB), lanes], (rb // SUB, 1))
                term = wk * _rows_at(win, shf, base + r0 + off, rb, lanes)
                acc = term if acc is None else acc + term
            if bias_ref is not None:
                acc = acc + bias_ref[:, lanes]
            out[pl.ds(r0, rb), lanes] = acc.astype(out.dtype)
        return carry

    lax.fori_loop(0, width // LANES, lane_body, 0)


def _conv_bwd_taps(win, wrep, x_cur, dx_out, dw_acc, *, taps, n_rows, width, shf=None):
    rb = min(32 if taps > 8 else 64, n_rows)

    def lane_body(cb, carry):
        lanes = pl.ds(pl.multiple_of(cb * LANES, LANES), LANES)
        if shf is not None:
            _shift_copies(win, shf, lanes)
        sums = [None] * taps
        for r0 in range(0, n_rows, rb):
            xv = x_cur[pl.ds(r0, rb), lanes].astype(F32)
            acc = None
            for k in range(taps):
                shifted = _rows_at(win, shf, r0 + taps - 1 - k, rb, lanes)
                term = jnp.tile(wrep[pl.ds(k * SUB, SUB), lanes], (rb // SUB, 1)) * shifted
                acc = term if acc is None else acc + term
                part = _rows8(xv * shifted)
                sums[k] = part if sums[k] is None else sums[k] + part
            dx_out[pl.ds(r0, rb), lanes] = acc.astype(dx_out.dtype)
        for k in range(taps):
            dw_acc[pl.ds(k * SUB, SUB), lanes] += sums[k]
        return carry

    lax.fori_loop(0, width // LANES, lane_body, 0)


def _fold8(acc_ref, taps):
    return jnp.concatenate(
        [jnp.sum(acc_ref[pl.ds(k * SUB, SUB), :], axis=0, keepdims=True) for k in range(taps)], axis=0)


def _seq_tile(s_len):
    return min(512, s_len)


def _mm_tile(s_len):
    return min(512, s_len)


def _ff_chunk(ff):
    best = LANES
    for c in range(LANES, 1408 + 1, LANES):
        if ff % c == 0:
            best = c
    return best


def _col_tile(n):
    for c in (512, 1408, 256, LANES):
        if n % c == 0:
            return c
    return n


def _rms_matmul(x, g, wt, b, *, name, dep=None):
    s_len, d = x.shape
    n = wt.shape[0]
    tm = _mm_tile(s_len)
    cn = _col_tile(n)
    has_bias = b is not None

    def body(*refs):
        x_ref, g_ref, w_ref = refs[0:3]
        b_ref = refs[3] if has_bias else None
        o_ref, h_ref = refs[-2:]

        def blk(rows):
            xv = x_ref[rows, :]
            r = lax.rsqrt(jnp.mean(xv * xv, axis=-1, keepdims=True) + EPS)
            h_ref[rows, :] = ((xv * r) * g_ref[...]).astype(BF16)

        rb = min(128, tm)
        for r0 in range(0, tm, rb):
            blk(pl.ds(r0, rb))
        for j in range(n // cn):
            acc = _dot_nt(h_ref[...], w_ref[j * cn:(j + 1) * cn, :])
            if has_bias:
                acc = acc + b_ref[:, j * cn:(j + 1) * cn]
            o_ref[:, j * cn:(j + 1) * cn] = acc.astype(BF16)

    in_specs = [pl.BlockSpec((tm, d), lambda i: (i, 0)), _resident((1, d), lambda i: (0, 0)),
                _resident((n, d), lambda i: (0, 0))]
    args = [x, g, wt]
    if has_bias:
        in_specs.append(_resident((1, n), lambda i: (0, 0)))
        args.append(b)
    in_specs.append(ANY)
    args.append(x if dep is None else dep)
    return pl.pallas_call(
        body, grid=(s_len // tm,), in_specs=in_specs,
        out_specs=[pl.BlockSpec((tm, n), lambda i: (i, 0)), pl.BlockSpec((tm, d), lambda i: (i, 0))],
        out_shape=[jax.ShapeDtypeStruct((s_len, n), BF16), jax.ShapeDtypeStruct((s_len, d), BF16)],
        compiler_params=_params("parallel"), name=name,
    )(*args)


def _mix_windows(u_ref, uh_ref, gw, pw, first, t):
    c = D_CONF
    uh = uh_ref[...].astype(F32)
    gw[0:HALO, :] = jnp.where(first, 0.0, uh[:, 0:c] * _sigmoid(uh[:, c:2 * c]))
    pw[0:HALO3, :] = jnp.where(first, 0.0, uh[HALO - HALO3:HALO, 3 * c:4 * c] * uh[HALO - HALO3:HALO, 4 * c:5 * c])

    def blk(rows):
        dst = pl.ds(pl.multiple_of(rows.start + HALO, SUB), rows.size)
        gw[dst, :] = u_ref[rows, 0:c].astype(F32) * _sigmoid(u_ref[rows, c:2 * c].astype(F32))
        dst3 = pl.ds(pl.multiple_of(rows.start + HALO3, SUB), rows.size)
        pw[dst3, :] = u_ref[rows, 3 * c:4 * c].astype(F32) * u_ref[rows, 4 * c:5 * c].astype(F32)
    _row_loop(t, 64, blk)


def _mix_fwd(u, x0, wa, ba, lg, lb, wb, w_out, *, name):
    s_len, d_in = u.shape
    d = x0.shape[1]
    c = D_CONF
    t = _seq_tile(s_len)
    per = t // HALO

    def body(u_ref, uh_ref, x0_ref, wa_ref, ba_ref, lg_ref, lb_ref, wb_ref, wo_ref, y_ref, x1_ref, ca, cb,
             gw, pw, wrep_a, wrep_b, shf):
        first = pl.program_id(0) == 0
        _mix_windows(u_ref, uh_ref, gw, pw, first, t)
        _replicate_taps(wa_ref, wrep_a, CONF_K)
        _replicate_taps(wb_ref, wrep_b, SHORT_K)
        _conv_taps(gw, wrep_a, ca, taps=CONF_K, n_rows=t, base=HALO, width=c, bias_ref=ba_ref, shf=shf)
        _conv_taps(pw, wrep_b, cb, taps=SHORT_K, n_rows=t, base=HALO3, width=c)

        def blk(rows):
            cv = ca[rows, :]
            mu = jnp.mean(cv, axis=-1, keepdims=True)
            xc = cv - mu
            var = jnp.mean(xc * xc, axis=-1, keepdims=True)
            ln = (xc * lax.rsqrt(var + EPS)) * lg_ref[...] + lb_ref[...]
            y_ref[rows, 0:c] = (ln * _sigmoid(ln)).astype(BF16)
            y_ref[rows, c:2 * c] = (u_ref[rows, 2 * c:3 * c].astype(F32) * cb[rows, :]).astype(BF16)
        _row_loop(t, 64, blk)
        x1_ref[...] = x0_ref[...] + _dot(y_ref[...], wo_ref[...])

    small = lambda r: _resident((r, c), lambda i: (0, 0))
    return pl.pallas_call(
        body, grid=(s_len // t,),
        in_specs=[pl.BlockSpec((t, d_in), lambda i: (i, 0)),
                  pl.BlockSpec((HALO, d_in), lambda i: (jnp.maximum(i * per - 1, 0), 0)),
                  pl.BlockSpec((t, d), lambda i: (i, 0)),
                  small(CONF_K), small(1), small(1), small(1), small(SHORT_K),
                  _resident((2 * c, d), lambda i: (0, 0))],
        out_specs=[pl.BlockSpec((t, 2 * c), lambda i: (i, 0)), pl.BlockSpec((t, d), lambda i: (i, 0)),
                   pl.BlockSpec((t, c), lambda i: (i, 0)), pl.BlockSpec((t, c), lambda i: (i, 0))],
        out_shape=[jax.ShapeDtypeStruct((s_len, 2 * c), BF16), jax.ShapeDtypeStruct((s_len, d), F32),
                   jax.ShapeDtypeStruct((s_len, c), F32), jax.ShapeDtypeStruct((s_len, c), F32)],
        scratch_shapes=[pltpu.VMEM((HALO + t, c), F32), pltpu.VMEM((HALO3 + t, c), F32),
                        pltpu.VMEM((CONF_K * SUB, c), F32), pltpu.VMEM((SHORT_K * SUB, c), F32),
                        pltpu.VMEM((SUB - 1, HALO + t, c), F32)],
        compiler_params=_params("arbitrary"), name=name,
    )(u, u, x0, wa, ba, lg, lb, wb, w_out)


def _ffn_windows(ug_ref, ugh_ref, uv_ref, uvh_ref, gwin, vwin, first, t):
    lo = HALO3_BLK - HALO3
    gwin[0:HALO3, :] = jnp.where(first, 0.0, ugh_ref[...].astype(F32)[lo:HALO3_BLK])
    vwin[0:HALO3, :] = jnp.where(first, 0.0, uvh_ref[...].astype(F32)[lo:HALO3_BLK])

    def blk(rows):
        dst = pl.ds(pl.multiple_of(rows.start + HALO3, SUB), rows.size)
        gwin[dst, :] = ug_ref[rows, :].astype(F32)
        vwin[dst, :] = uv_ref[rows, :].astype(F32)
    _row_loop(t, 64, blk)


def _ffn_fwd(uf, x1, wf, w_down, *, name, dep=None):
    s_len, ff2 = uf.shape
    ff = ff2 // 2
    d = x1.shape[1]
    t = _seq_tile(s_len)
    fc = _ff_chunk(ff)
    nc = ff // fc
    per = t // HALO3_BLK

    def body(ug_ref, ugh_ref, uv_ref, uvh_ref, x1_ref, wfg_ref, wfv_ref, wd_ref, dep_ref,
             act_ref, x2_ref, cg_ref, cv_ref, gwin, vwin, cg, cv, wrep_g, wrep_v):
        first = pl.program_id(0) == 0
        _ffn_windows(ug_ref, ugh_ref, uv_ref, uvh_ref, gwin, vwin, first, t)
        _replicate_taps(wfg_ref, wrep_g, SHORT_K)
        _replicate_taps(wfv_ref, wrep_v, SHORT_K)
        _conv_taps(gwin, wrep_g, cg, taps=SHORT_K, n_rows=t, base=HALO3, width=fc)
        _conv_taps(vwin, wrep_v, cv, taps=SHORT_K, n_rows=t, base=HALO3, width=fc)

        def blk(rows):
            gv = cg[rows, :]
            vv = cv[rows, :]
            cg_ref[rows, :] = gv.astype(BF16)
            cv_ref[rows, :] = vv.astype(BF16)
            act_ref[rows, :] = ((gv * _sigmoid(gv)) * vv).astype(BF16)
        _row_loop(t, 32, blk, unroll=2)

        @pl.when(pl.program_id(1) == 0)
        def _():
            x2_ref[...] = x1_ref[...]
        x2_ref[...] += _dot(act_ref[...], wd_ref[...])

    halo_map = lambda off: (lambda i, j: (jnp.maximum(i * per - 1, 0), j + off))
    return pl.pallas_call(
        body, grid=(s_len // t, nc),
        in_specs=[pl.BlockSpec((t, fc), lambda i, j: (i, j)), pl.BlockSpec((HALO3_BLK, fc), halo_map(0)),
                  pl.BlockSpec((t, fc), lambda i, j: (i, j + nc)), pl.BlockSpec((HALO3_BLK, fc), halo_map(nc)),
                  pl.BlockSpec((t, d), lambda i, j: (i, 0)),
                  pl.BlockSpec((SHORT_K, fc), lambda i, j: (0, j)),
                  pl.BlockSpec((SHORT_K, fc), lambda i, j: (0, j + nc)),
                  pl.BlockSpec((fc, d), lambda i, j: (j, 0)), ANY],
        out_specs=[pl.BlockSpec((t, fc), lambda i, j: (i, j)), pl.BlockSpec((t, d), lambda i, j: (i, 0)),
                   pl.BlockSpec((t, fc), lambda i, j: (i, j)), pl.BlockSpec((t, fc), lambda i, j: (i, j))],
        out_shape=[jax.ShapeDtypeStruct((s_len, ff), BF16), jax.ShapeDtypeStruct((s_len, d), F32),
                   jax.ShapeDtypeStruct((s_len, ff), BF16), jax.ShapeDtypeStruct((s_len, ff), BF16)],
        scratch_shapes=[pltpu.VMEM((HALO3 + t, fc), F32), pltpu.VMEM((HALO3 + t, fc), F32),
                        pltpu.VMEM((t, fc), F32), pltpu.VMEM((t, fc), F32),
                        pltpu.VMEM((SHORT_K * SUB, fc), F32), pltpu.VMEM((SHORT_K * SUB, fc), F32)],
        compiler_params=_params("parallel", "arbitrary"), name=name,
    )(uf, uf, uf, uf, x1, wf, wf, w_down, uf if dep is None else dep)


def _loss_bwd(x, g, target, *, name):
    s_len, d = x.shape
    t = _seq_tile(s_len)

    def body(x_ref, g_ref, t_ref, l_ref, dx_ref, dxb_ref, dg_ref):
        @pl.when(pl.program_id(0) == 0)
        def _():
            l_ref[...] = jnp.zeros_like(l_ref)
            dg_ref[...] = jnp.zeros_like(dg_ref)

        def blk(rows):
            xv = x_ref[rows, :]
            r = lax.rsqrt(jnp.mean(xv * xv, axis=-1, keepdims=True) + EPS)
            xn = xv * r
            e = xn * g_ref[...] - t_ref[rows, :]
            l_ref[...] += _rows8(e * e)
            dy = e * (1.0 / d)
            dg_ref[...] += _rows8(dy * xn)
            dn = dy * g_ref[...]
            dx = r * (dn - xn * jnp.mean(dn * xn, axis=-1, keepdims=True))
            dx_ref[rows, :] = dx
            dxb_ref[rows, :] = dx.astype(BF16)
        _row_loop(t, 64, blk)

    row = pl.BlockSpec((t, d), lambda i: (i, 0))
    part = pl.BlockSpec((SUB, d), lambda i: (0, 0))
    return pl.pallas_call(
        body, grid=(s_len // t,),
        in_specs=[row, _resident((1, d), lambda i: (0, 0)), row],
        out_specs=[part, row, row, part],
        out_shape=[jax.ShapeDtypeStruct((SUB, d), F32), jax.ShapeDtypeStruct((s_len, d), F32),
                   jax.ShapeDtypeStruct((s_len, d), BF16), jax.ShapeDtypeStruct((SUB, d), F32)],
        compiler_params=_params("arbitrary"), name=name,
    )(x, g, target)


def _ffn_bwd(dx2, uf, cg, cv, wf, w_down, *, name, dep=None):
    s_len, ff2 = uf.shape
    ff = ff2 // 2
    d = dx2.shape[1]
    t = _seq_tile(s_len)
    n_t = s_len // t
    fc = _ff_chunk(ff)
    nc = ff // fc

    def body(dx_ref, ug_ref, uv_ref, cg_ref, cv_ref, wfg_ref, wfv_ref, wd_ref, dep_ref,
             dug_ref, duv_ref, dwg_ref, dwv_ref, dact, dgw, dvw, awg, awv, wrep_g, wrep_v):
        i = pl.program_id(1)

        @pl.when(i == 0)
        def _():
            dgw[t:t + HALO3, :] = jnp.zeros((HALO3, fc), F32)
            dvw[t:t + HALO3, :] = jnp.zeros((HALO3, fc), F32)
            awg[...] = jnp.zeros_like(awg)
            awv[...] = jnp.zeros_like(awv)

        _replicate_taps(wfg_ref, wrep_g, SHORT_K)
        _replicate_taps(wfv_ref, wrep_v, SHORT_K)

        def blk(rows):
            gv = cg_ref[rows, :].astype(F32)
            sg = _sigmoid(gv)
            da = dact[rows, :]
            dgw[rows, :] = (da * cv_ref[rows, :].astype(F32)) * (sg * (1.0 + gv * (1.0 - sg)))
            dvw[rows, :] = da * (gv * sg)

        dact[...] = _dot_nt(dx_ref[...], wd_ref[...])
        _row_loop(t, 32, blk, unroll=2)

        _conv_bwd_taps(dgw, wrep_g, ug_ref, dug_ref, awg, taps=SHORT_K, n_rows=t, width=fc)
        _conv_bwd_taps(dvw, wrep_v, uv_ref, duv_ref, awv, taps=SHORT_K, n_rows=t, width=fc)
        dgw[t:t + HALO3, :] = dgw[0:HALO3, :]
        dvw[t:t + HALO3, :] = dvw[0:HALO3, :]

        @pl.when(i == n_t - 1)
        def _():
            dwg_ref[...] = _fold8(awg, SHORT_K)
            dwv_ref[...] = _fold8(awv, SHORT_K)

    rev = lambda i: n_t - 1 - i
    gate = pl.BlockSpec((t, fc), lambda j, i: (rev(i), j))
    value = pl.BlockSpec((t, fc), lambda j, i: (rev(i), j + nc))
    return pl.pallas_call(
        body, grid=(nc, n_t),
        in_specs=[pl.BlockSpec((t, d), lambda j, i: (rev(i), 0)), gate, value, gate, gate,
                  pl.BlockSpec((SHORT_K, fc), lambda j, i: (0, j)),
                  pl.BlockSpec((SHORT_K, fc), lambda j, i: (0, j + nc)),
                  pl.BlockSpec((fc, d), lambda j, i: (j, 0)), ANY],
        out_specs=[gate, gate,
                   pl.BlockSpec((SHORT_K, fc), lambda j, i: (0, j)), pl.BlockSpec((SHORT_K, fc), lambda j, i: (0, j))],
        out_shape=[jax.ShapeDtypeStruct((s_len, ff), BF16), jax.ShapeDtypeStruct((s_len, ff), BF16),
                   jax.ShapeDtypeStruct((SHORT_K, ff), F32), jax.ShapeDtypeStruct((SHORT_K, ff), F32)],
        scratch_shapes=[pltpu.VMEM((t, fc), F32),
                        pltpu.VMEM((t + HALO3, fc), F32), pltpu.VMEM((t + HALO3, fc), F32),
                        pltpu.VMEM((SHORT_K * SUB, fc), F32), pltpu.VMEM((SHORT_K * SUB, fc), F32),
                        pltpu.VMEM((SHORT_K * SUB, fc), F32), pltpu.VMEM((SHORT_K * SUB, fc), F32)],
        compiler_params=_params("arbitrary", "arbitrary"), name=name,
    )(dx2, uf, uf, cg, cv, wf, wf, w_down, uf if dep is None else dep)


def _mix_bwd(dx1, u, ca, cb, wa, lg, lb, wb, w_out, *, name, dep=None):
    s_len, d_in = u.shape
    d = dx1.shape[1]
    c = D_CONF
    t = _seq_tile(s_len)
    n_t = s_len // t

    def body(dx_ref, u_ref, ca_ref, cb_ref, wa_ref, lg_ref, lb_ref, wb_ref, wo_ref, dep_ref,
             du_ref, dwa_ref, dwb_ref, dba_ref, dlg_ref, dlb_ref, dbin_ref,
             glu, prod, dyc, dcaw, dcbw, dglu, dp, awa, awb, wrep_a, wrep_b, shf):
        i = pl.program_id(0)
        dyc[...] = _dot_nt(dx_ref[...], wo_ref[...])
        _replicate_taps(wa_ref, wrep_a, CONF_K)
        _replicate_taps(wb_ref, wrep_b, SHORT_K)

        @pl.when(i == 0)
        def _():
            dcaw[t:t + HALO, :] = jnp.zeros((HALO, c), F32)
            dcbw[t:t + HALO3, :] = jnp.zeros((HALO3, c), F32)
            awa[...] = jnp.zeros_like(awa)
            awb[...] = jnp.zeros_like(awb)
            dba_ref[...] = jnp.zeros_like(dba_ref)
            dlg_ref[...] = jnp.zeros_like(dlg_ref)
            dlb_ref[...] = jnp.zeros_like(dlb_ref)
            dbin_ref[...] = jnp.zeros_like(dbin_ref)

        def blk1(rows):
            cv = ca_ref[rows, :]
            mu = jnp.mean(cv, axis=-1, keepdims=True)
            xc = cv - mu
            rstd = lax.rsqrt(jnp.mean(xc * xc, axis=-1, keepdims=True) + EPS)
            nrm = xc * rstd
            ln = nrm * lg_ref[...] + lb_ref[...]
            sg = _sigmoid(ln)
            dln = dyc[rows, 0:c] * (sg * (1.0 + ln * (1.0 - sg)))
            dlg_ref[...] += _rows8(dln * nrm)
            dlb_ref[...] += _rows8(dln)
            dn = dln * lg_ref[...]
            dca = rstd * (dn - jnp.mean(dn, axis=-1, keepdims=True)
                          - nrm * jnp.mean(dn * nrm, axis=-1, keepdims=True))
            dcaw[rows, :] = dca
            dba_ref[...] += _rows8(dca)
            ds = dyc[rows, c:2 * c]
            dgb = ds * cb_ref[rows, :]
            dcbw[rows, :] = ds * u_ref[rows, 2 * c:3 * c].astype(F32)
            du_ref[rows, 2 * c:3 * c] = dgb.astype(BF16)
            dbin_ref[:, 2 * c:3 * c] += _rows8(dgb)
            glu[rows, :] = u_ref[rows, 0:c].astype(F32) * _sigmoid(u_ref[rows, c:2 * c].astype(F32))
            prod[rows, :] = u_ref[rows, 3 * c:4 * c].astype(F32) * u_ref[rows, 4 * c:5 * c].astype(F32)
        _row_loop(t, 64, blk1, unroll=2)

        _conv_bwd_taps(dcaw, wrep_a, glu, dglu, awa, taps=CONF_K, n_rows=t, width=c, shf=shf)
        _conv_bwd_taps(dcbw, wrep_b, prod, dp, awb, taps=SHORT_K, n_rows=t, width=c)
        dcaw[t:t + HALO, :] = dcaw[0:HALO, :]
        dcbw[t:t + HALO3, :] = dcbw[0:HALO3, :]

        def blk2(rows):
            av = u_ref[rows, 0:c].astype(F32)
            sg = _sigmoid(u_ref[rows, c:2 * c].astype(F32))
            dg = dglu[rows, :]
            d_av = dg * sg
            d_ag = (dg * av) * (sg * (1.0 - sg))
            dpv = dp[rows, :]
            d_gc = dpv * u_ref[rows, 4 * c:5 * c].astype(F32)
            d_vs = dpv * u_ref[rows, 3 * c:4 * c].astype(F32)
            du_ref[rows, 0:c] = d_av.astype(BF16)
            du_ref[rows, c:2 * c] = d_ag.astype(BF16)
            du_ref[rows, 3 * c:4 * c] = d_gc.astype(BF16)
            du_ref[rows, 4 * c:5 * c] = d_vs.astype(BF16)
            dbin_ref[:, 0:c] += _rows8(d_av)
            dbin_ref[:, c:2 * c] += _rows8(d_ag)
            dbin_ref[:, 3 * c:4 * c] += _rows8(d_gc)
            dbin_ref[:, 4 * c:5 * c] += _rows8(d_vs)
        _row_loop(t, 64, blk2)

        @pl.when(i == n_t - 1)
        def _():
            dwa_ref[...] = _fold8(awa, CONF_K)
            dwb_ref[...] = _fold8(awb, SHORT_K)

    rev = lambda i: n_t - 1 - i
    small_in = lambda r: _resident((r, c), lambda i: (0, 0))
    small = lambda r: pl.BlockSpec((r, c), lambda i: (0, 0))
    return pl.pallas_call(
        body, grid=(n_t,),
        in_specs=[pl.BlockSpec((t, d), lambda i: (rev(i), 0)),
                  pl.BlockSpec((t, d_in), lambda i: (rev(i), 0)),
                  pl.BlockSpec((t, c), lambda i: (rev(i), 0)), pl.BlockSpec((t, c), lambda i: (rev(i), 0)),
                  small_in(CONF_K), small_in(1), small_in(1), small_in(SHORT_K),
                  _resident((2 * c, d), lambda i: (0, 0)), ANY],
        out_specs=[pl.BlockSpec((t, d_in), lambda i: (rev(i), 0)),
                   small(CONF_K), small(SHORT_K), small(SUB), small(SUB), small(SUB),
                   pl.BlockSpec((SUB, d_in), lambda i: (0, 0))],
        out_shape=[jax.ShapeDtypeStruct((s_len, d_in), BF16),
                   jax.ShapeDtypeStruct((CONF_K, c), F32), jax.ShapeDtypeStruct((SHORT_K, c), F32),
                   jax.ShapeDtypeStruct((SUB, c), F32), jax.ShapeDtypeStruct((SUB, c), F32),
                   jax.ShapeDtypeStruct((SUB, c), F32), jax.ShapeDtypeStruct((SUB, d_in), F32)],
        scratch_shapes=[pltpu.VMEM((t, c), F32), pltpu.VMEM((t, c), F32), pltpu.VMEM((t, 2 * c), F32),
                        pltpu.VMEM((t + HALO, c), F32), pltpu.VMEM((t + HALO3, c), F32),
                        pltpu.VMEM((t, c), F32), pltpu.VMEM((t, c), F32),
                        pltpu.VMEM((CONF_K * SUB, c), F32), pltpu.VMEM((SHORT_K * SUB, c), F32),
                        pltpu.VMEM((CONF_K * SUB, c), F32), pltpu.VMEM((SHORT_K * SUB, c), F32),
                        pltpu.VMEM((SUB - 1, t + HALO, c), F32)],
        compiler_params=_params("arbitrary"), name=name,
    )(dx1, u, ca, cb, wa, lg, lb, wb, w_out, u if dep is None else dep)


def _matmul_tn(a, b, *, name, into=None, part=0, n_parts=1):
    s_len, k = a.shape
    n = b.shape[1]
    tk = _col_tile(k)
    per = k // tk

    def body(*refs):
        a_ref, b_ref = refs[0], refs[1]
        o_ref = refs[-1]
        o_ref[...] = _dot_tn(a_ref[...], b_ref[...]).astype(BF16)

    in_specs = [pl.BlockSpec((s_len, tk), lambda j: (0, j)), _resident((s_len, n), lambda j: (0, 0))]
    args = [a, b]
    aliases = {}
    if into is not None:
        in_specs.append(ANY)
        args.append(into)
        aliases = {2: 0}
    return pl.pallas_call(
        body, grid=(per,), in_specs=in_specs,
        out_specs=pl.BlockSpec((tk, n), lambda j: (part * per + j, 0)),
        out_shape=jax.ShapeDtypeStruct((n_parts * k, n), BF16),
        input_output_aliases=aliases,
        compiler_params=_params("parallel"), name=name,
    )(*args)


def _matmul_rmsbwd(dzs, wt, x, g, dx_in, *, name, dep=None):
    s_len, d = x.shape
    n_z = len(dzs)
    nj = dzs[0].shape[1]
    t = _mm_tile(s_len)

    def body(*refs):
        dz_refs = refs[0:n_z]
        w_refs = refs[n_z:2 * n_z]
        x_ref, g_ref, dxi_ref, _, dx_ref, dxb_ref, dg_ref, dh = refs[2 * n_z:]

        @pl.when(pl.program_id(0) == 0)
        def _():
            dg_ref[...] = jnp.zeros_like(dg_ref)

        def blk(rows):
            xv = x_ref[rows, :]
            r = lax.rsqrt(jnp.mean(xv * xv, axis=-1, keepdims=True) + EPS)
            xn = xv * r
            dhv = dh[rows, :]
            dg_ref[...] += _rows8(dhv * xn)
            dn = dhv * g_ref[...]
            dx = dxi_ref[rows, :] + r * (dn - xn * jnp.mean(dn * xn, axis=-1, keepdims=True))
            dx_ref[rows, :] = dx
            dxb_ref[rows, :] = dx.astype(BF16)

        half = t // 2
        rb = min(128, half)
        for lo in range(0, t, half):
            acc = _dot(dz_refs[0][lo:lo + half, :], w_refs[0][...])
            for q in range(1, n_z):
                acc = acc + _dot(dz_refs[q][lo:lo + half, :], w_refs[q][...])
            dh[lo:lo + half, :] = acc
            for r0 in range(lo, lo + half, rb):
                blk(pl.ds(r0, rb))

    row = pl.BlockSpec((t, d), lambda i: (i, 0))
    in_specs = [pl.BlockSpec((t, nj), lambda i: (i, 0)) for _ in range(n_z)]
    in_specs += [_resident((nj, d), functools.partial(lambda q, i: (q, 0), q)) for q in range(n_z)]
    in_specs += [row, _resident((1, d), lambda i: (0, 0)), row, ANY]
    return pl.pallas_call(
        body, grid=(s_len // t,), in_specs=in_specs,
        out_specs=[row, row, pl.BlockSpec((SUB, d), lambda i: (0, 0))],
        out_shape=[jax.ShapeDtypeStruct((s_len, d), F32), jax.ShapeDtypeStruct((s_len, d), BF16),
                   jax.ShapeDtypeStruct((SUB, d), F32)],
        scratch_shapes=[pltpu.VMEM((t, d), F32)],
        compiler_params=_params("arbitrary"), name=name,
    )(*dzs, *([wt] * n_z), x, g, dx_in, x if dep is None else dep)


def _row(v):
    return v.reshape(1, -1)


def _layer_fwd(x0, p, tag, dep=None, before_up=None):
    u, h1 = _rms_matmul(x0, _row(p["mix_norm_g"]), p["w_in_t"], _row(p["b_in"]), name=f"in_proj_{tag}", dep=dep)
    ycat, x1, ca, cb = _mix_fwd(u, x0, p["conv_a_w"], _row(p["conv_a_b"]), _row(p["ln_a_g"]), _row(p["ln_a_b"]),
                            p["conv_b_w"], p["w_out"], name=f"mix_fwd_{tag}")
    if before_up is not None:
        before_up(x1)
    uf, h2 = _rms_matmul(x1, _row(p["ffn_norm_g"]), p["w_up_t"], None, name=f"up_proj_{tag}")
    act, x2, cg, cv = _ffn_fwd(uf, x1, p["conv_f_w"], p["w_down"], name=f"ffn_fwd_{tag}")
    return x2, dict(x0=x0, h1=h1, u=u, ca=ca, cb=cb, ycat=ycat, x1=x1, h2=h2, uf=uf, cg=cg, cv=cv, act=act)


def _layer_bwd(dx2, dx2_b, p, saved, tag, ffn_grads, ffn_sent, mix_grads, mix_sent, dep=None):
    dug, duv, dwf_g, dwf_v = _ffn_bwd(dx2_b, saved["uf"], saved["cg"], saved["cv"], p["conv_f_w"], p["w_down"],
                                      name=f"ffn_bwd_{tag}", dep=dep)
    g_down = _matmul_tn(saved["act"], dx2_b, name=f"dw_down_{tag}")
    g_up = _matmul_tn(dug, saved["h2"], name=f"dw_up_g_{tag}", n_parts=2)
    g_up = _matmul_tn(duv, saved["h2"], name=f"dw_up_v_{tag}", into=g_up, part=1, n_parts=2)
    dep_ffn = ffn_grads(dict(w_up=g_up, w_down=g_down), dx2_b)
    dx1, dx1_b, dg2 = _matmul_rmsbwd([dug, duv], p["w_up_t"], saved["x1"], _row(p["ffn_norm_g"]), dx2,
                                     name=f"dh_ffn_{tag}", dep=dep_ffn)
    du, dwa, dwb, dba, dlg, dlb, dbin = _mix_bwd(
        dx1_b, saved["u"], saved["ca"], saved["cb"], p["conv_a_w"], _row(p["ln_a_g"]), _row(p["ln_a_b"]),
        p["conv_b_w"], p["w_out"], name=f"mix_bwd_{tag}", dep=ffn_sent(dx1_b))
    g_out = _matmul_tn(saved["ycat"], dx1_b, name=f"dw_out_{tag}")
    g_in = _matmul_tn(du, saved["h1"], name=f"dw_in_{tag}")
    conv = dict(conv_a_w=dwa, conv_b_w=dwb, conv_f_w=jnp.concatenate([dwf_g, dwf_v], axis=1))
    dep_mix = mix_grads(dict(w_in=g_in, w_out=g_out), conv, dx1_b)
    dx0, dx0_b, dg1 = _matmul_rmsbwd([du], p["w_in_t"], saved["x0"], _row(p["mix_norm_g"]), dx1,
                                     name=f"dh_mix_{tag}", dep=dep_mix)
    rep = dict(mix_norm_g=dg1, b_in=dbin, conv_a_b=dba, ln_a_g=dlg, ln_a_b=dlb, ffn_norm_g=dg2)
    return dx0, dx0_b, rep, mix_sent(dx0_b)


def _place():
    return lax.axis_index("x"), lax.axis_index("y"), lax.axis_index("c")


def _all_gather_small(arr, *, name):
    def body(in_ref, out_ref, send_sems, recv_sems, local_sem):
        x, y, c = _place()
        me = 4 * x + 2 * y + c
        mine = pltpu.make_async_copy(in_ref, out_ref.at[me], local_sem)
        mine.start()
        copies = []
        for k in range(1, N_DEV):
            to = (1 - x if k & 4 else x, 1 - y if k & 2 else y, 1 - c if k & 1 else c)
            copies.append(pltpu.make_async_remote_copy(
                src_ref=in_ref, dst_ref=out_ref.at[me], send_sem=send_sems.at[k], recv_sem=recv_sems.at[k],
                device_id=to, device_id_type=MESH))
        for cp in copies:
            cp.start()
        for cp in copies:
            cp.wait()
        mine.wait()

    return pl.pallas_call(
        body, in_specs=[ANY], out_specs=ANY,
        out_shape=jax.ShapeDtypeStruct((N_DEV, *arr.shape), arr.dtype),
        scratch_shapes=[pltpu.SemaphoreType.DMA((N_DEV,)), pltpu.SemaphoreType.DMA((N_DEV,)),
                        pltpu.SemaphoreType.DMA],
        name=name,
    )(arr)


def _all_gather(arrs, *, name):
    n_a = len(arrs)

    def body(*refs):
        ins = refs[0:n_a]
        outs = refs[n_a:2 * n_a]
        send_sems, recv_sems, local_sems = refs[2 * n_a:]
        x, y, c = _place()
        sibling = (x, y, 1 - c)
        chips = [(1 - x, y), (x, 1 - y), (1 - x, 1 - y)]

        def slot(a, px, py, pc):
            return outs[a].at[4 * px + 2 * py + pc]

        def copy(a, k, block, to, src=None):
            return pltpu.make_async_remote_copy(
                src_ref=slot(a, *block) if src is None else src, dst_ref=slot(a, *block),
                send_sem=send_sems.at[a, k], recv_sem=recv_sems.at[a, k],
                device_id=to, device_id_type=MESH)

        me = (x, y, c)
        mine = [pltpu.make_async_copy(ins[a], slot(a, *me), local_sems.at[a]) for a in range(n_a)]
        for cp in mine:
            cp.start()
        started = []
        for a in range(n_a):
            first = [copy(a, 0, me, sibling, src=ins[a])]
            first += [copy(a, 1 + j, me, (*chip, c), src=ins[a]) for j, chip in enumerate(chips)]
            for cp in first:
                cp.start()
            started += first
        for a in range(n_a):
            for j, chip in enumerate(chips):
                copy(a, 1 + j, (*chip, c), me).wait_recv()
                passed = copy(a, 4 + j, (*chip, c), sibling)
                passed.start()
                started.append(passed)
        for a in range(n_a):
            copy(a, 0, sibling, me).wait_recv()
            for j, chip in enumerate(chips):
                copy(a, 4 + j, (*chip, 1 - c), me).wait_recv()
        for cp in started:
            cp.wait_send()
        for cp in mine:
            cp.wait()

    return pl.pallas_call(
        body, in_specs=[ANY] * n_a, out_specs=[ANY] * n_a,
        out_shape=[jax.ShapeDtypeStruct((N_DEV, *a.shape), a.dtype) for a in arrs],
        scratch_shapes=[pltpu.SemaphoreType.DMA((n_a, 7)), pltpu.SemaphoreType.DMA((n_a, 7)),
                        pltpu.SemaphoreType.DMA((n_a,))],
        name=name,
    )(*arrs)


def _row_tile(r, cap):
    for tr in range(min(cap, r) // 16 * 16, 0, -16):
        if r % tr == 0:
            return tr
    return r


def _pair_sum(mines, theirs, where, *, name):
    n_a = len(mines)
    n_chip = mines[0].shape[0]

    def body(where_ref, *refs):
        a_refs = refs[0:n_a]
        b_refs = refs[n_a:2 * n_a]
        p_refs = refs[2 * n_a:3 * n_a]
        l_refs = refs[3 * n_a:4 * n_a]
        q = pl.program_id(0)
        for a in range(n_a):
            p_refs[a][...] = (a_refs[a][...].astype(F32) + b_refs[a][...].astype(F32)).astype(p_refs[a].dtype)

        @pl.when(q == where_ref[1])
        def _():
            for a in range(n_a):
                l_refs[a][...] = p_refs[a][...]

    in_specs, out_p, out_l, shapes = [], [], [], []
    for m in mines:
        _, _, r, c = m.shape
        in_specs.append(pl.BlockSpec((None, None, r, c), lambda q, where_ref: (q, where_ref[0], 0, 0)))
    for m in mines:
        _, _, r, c = m.shape
        in_specs.append(pl.BlockSpec((None, r, c), lambda q, where_ref: (q, 0, 0)))
        out_p.append(pl.BlockSpec((None, r, c), lambda q, where_ref: (q, 0, 0)))
        out_l.append(pl.BlockSpec((None, r, c), lambda q, where_ref: (where_ref[1], 0, 0)))
        shapes.append(jax.ShapeDtypeStruct((n_chip, r, c), m.dtype))
    res = pl.pallas_call(
        body,
        grid_spec=pltpu.PrefetchScalarGridSpec(num_scalar_prefetch=1, grid=(n_chip,), in_specs=in_specs,
                                               out_specs=out_p + out_l),
        out_shape=shapes + shapes,
        compiler_params=_params("arbitrary"), name=name,
    )(where, *mines, *theirs)
    return list(res[:n_a]), list(res[n_a:])


HBM = pl.BlockSpec(memory_space=pltpu.HBM)
SEM = pl.BlockSpec(memory_space=pltpu.SEMAPHORE)
EFFECT = pltpu.SideEffectType.DATAFLOW_SIDE_EFFECTING


def _in_hbm(a):
    return pltpu.with_memory_space_constraint(a, pltpu.HBM)


def _split_start(srcs, lands, plan, n_copies, after, *, name):
    n_s, n_l = len(srcs), len(lands)

    def body(*refs):
        src_refs = refs[0:n_s]
        land_refs = refs[n_s:n_s + n_l]
        send_sems, recv_sems = refs[n_s + n_l + 1], refs[n_s + n_l + 2]
        token = refs[-1]
        for cp in plan(src_refs, land_refs, send_sems, recv_sems):
            cp.start()
        token[...] = jnp.zeros_like(token)

    thru = [pltpu.HBM(a.shape, a.dtype) for a in list(srcs) + list(lands)]
    res = pl.pallas_call(
        body, name=name,
        out_shape=(pltpu.SemaphoreType.DMA((n_copies,)), pltpu.SemaphoreType.DMA((n_copies,)), *thru,
                   jax.ShapeDtypeStruct((SUB, LANES), F32)),
        in_specs=[HBM] * (n_s + n_l) + [ANY],
        out_specs=(SEM, SEM, *([HBM] * (n_s + n_l)), pl.BlockSpec(memory_space=pltpu.VMEM)),
        input_output_aliases={i: 2 + i for i in range(n_s + n_l)},
        compiler_params=pltpu.CompilerParams(has_side_effects=EFFECT),
    )(*[_in_hbm(a) for a in srcs], *[_in_hbm(a) for a in lands], _in_hbm(after))
    return res[0], res[1], list(res[2:2 + n_s]), list(res[2 + n_s:2 + n_s + n_l]), res[-1]


def _split_wait(send_sems, recv_sems, srcs, lands, after, plan, *, name):
    n_s, n_l = len(srcs), len(lands)

    def body(*refs):
        src_refs = refs[0:n_s]
        land_refs = refs[n_s:n_s + n_l]
        send, recv = refs[n_s + n_l], refs[n_s + n_l + 1]
        for cp in plan(src_refs, land_refs, send, recv):
            cp.wait_send()
            cp.wait_recv()

    res = pl.pallas_call(
        body, name=name,
        out_shape=tuple(pltpu.HBM(a.shape, a.dtype) for a in list(srcs) + list(lands)),
        in_specs=[HBM] * (n_s + n_l) + [SEM, SEM, ANY],
        out_specs=tuple([HBM] * (n_s + n_l)),
        input_output_aliases={i: i for i in range(n_s + n_l)},
        compiler_params=pltpu.CompilerParams(has_side_effects=EFFECT),
    )(*srcs, *lands, send_sems, recv_sems, _in_hbm(after))
    return list(res[:n_s]), list(res[n_s:])


def _remote(src, dst, send_sems, recv_sems, k, to):
    return pltpu.make_async_remote_copy(src_ref=src, dst_ref=dst, send_sem=send_sems.at[k], recv_sem=recv_sems.at[k],
                                        device_id=to, device_id_type=MESH)


def _gather_plan_first(src_refs, land_refs, send_sems, recv_sems):
    x, y, c = _place()
    me = 4 * x + 2 * y + c
    peers = [(x, y, 1 - c), (1 - x, y, c), (x, 1 - y, c), (1 - x, 1 - y, c)]
    return [_remote(src, land.at[me], send_sems, recv_sems, 4 * a + k, to)
            for a, (src, land) in enumerate(zip(src_refs, land_refs)) for k, to in enumerate(peers)]


def _gather_plan_second(src_refs, land_refs, send_sems, recv_sems):
    x, y, c = _place()
    chips = [(1 - x, y), (x, 1 - y), (1 - x, 1 - y)]
    out = []
    for a, land in enumerate(land_refs):
        for j, (px, py) in enumerate(chips):
            slot = land.at[4 * px + 2 * py + c]
            out.append(_remote(slot, slot, send_sems, recv_sems, 3 * a + j, (x, y, 1 - c)))
    return out


def _siblings_plan(src_refs, land_refs, send_sems, recv_sems):
    x, y, c = _place()
    return [_remote(src.at[:, 1 - c], land, send_sems, recv_sems, a, (x, y, 1 - c))
            for a, (src, land) in enumerate(zip(src_refs, land_refs))]


def _chips_plan(src_refs, land_refs, send_sems, recv_sems):
    x, y, c = _place()
    my_chip = 2 * x + y
    chips = [(1 - x, y), (x, 1 - y), (1 - x, 1 - y)]
    return [_remote(src.at[2 * px + py], land.at[my_chip], send_sems, recv_sems, 3 * a + j, (px, py, c))
            for a, (src, land) in enumerate(zip(src_refs, land_refs)) for j, (px, py) in enumerate(chips)]


def _gather_landing(shard, me):
    return lax.dynamic_update_index_in_dim(lax.empty((N_DEV, *shard.shape), shard.dtype), shard, me, 0)


def _adamw_math(g, w, m, v):
    m = ADAM_B1 * m + (1.0 - ADAM_B1) * g
    v = ADAM_B2 * v + (1.0 - ADAM_B2) * (g * g)
    m_hat = m / (1.0 - ADAM_B1 ** ADAM_STEP)
    v_hat = v / (1.0 - ADAM_B2 ** ADAM_STEP)
    delta = -ADAM_LR * (m_hat / (jnp.sqrt(v_hat) + ADAM_EPS) + ADAM_WD * w)
    return delta, m, v


def _adamw_sharded(parts, w, m, v, *, name, dep=None):
    n_layers, r, c = w.shape
    n_chip = parts[0].shape[0]
    tr = _row_tile(r, 384)
    n_i = r // tr

    def body(*refs):
        p_refs = refs[0:n_layers]
        w_ref, m_ref, v_ref, _, g_out, d_out, m_out, v_out = refs[n_layers:]
        layer = pl.program_id(0)
        for l in range(n_layers):
            @pl.when(layer == l)
            def _(l=l):
                g = p_refs[l][0].astype(F32)
                for q in range(1, n_chip):
                    g = g + p_refs[l][q].astype(F32)
                delta, m_new, v_new = _adamw_math(g, w_ref[...], m_ref[...], v_ref[...])
                g_out[...] = g
                d_out[...] = delta
                m_out[...] = m_new
                v_out[...] = v_new

    def part_map(l):
        return lambda layer, i: (0, jnp.where(layer == l, i, jnp.where(layer < l, 0, n_i - 1)), 0)

    blk = pl.BlockSpec((None, tr, c), lambda layer, i: (layer, i, 0))
    return pl.pallas_call(
        body, grid=(n_layers, n_i),
        in_specs=[pl.BlockSpec((n_chip, tr, c), part_map(l)) for l in range(n_layers)] + [blk, blk, blk, ANY],
        out_specs=[blk] * 4, out_shape=[jax.ShapeDtypeStruct((n_layers, r, c), F32)] * 4,
        compiler_params=_params("arbitrary", "arbitrary"), name=name,
    )(*parts, w, m, v, w if dep is None else dep)


def _fold_partials(cols, *, name):
    widths = [c.shape[1] for c in cols]

    def body(*refs):
        o_ref = refs[-1]
        pos = 0
        for ref, width in zip(refs[:-1], widths):
            o_ref[:, pos:pos + width] = jnp.sum(ref[...], axis=0, keepdims=True)
            pos += width

    return pl.pallas_call(body, out_shape=jax.ShapeDtypeStruct((1, sum(widths)), F32), name=name)(*cols)


def _adamw_replicated(parts, names, w, m, v, n_loss, *, name):
    n_dev = parts.shape[0]
    n_layers = w[names[0]].shape[0]
    every = list(names) + ["final_norm_g"]
    n_p = len(every)

    def body(*refs):
        p_ref = refs[0]
        w_refs = dict(zip(every, refs[1:1 + n_p]))
        m_refs = dict(zip(every, refs[1 + n_p:1 + 2 * n_p]))
        v_refs = dict(zip(every, refs[1 + 2 * n_p:1 + 3 * n_p]))
        l_out = refs[1 + 3 * n_p]
        outs = refs[2 + 3 * n_p:]
        o_refs = {n: outs[4 * q:4 * q + 4] for q, n in enumerate(every)}
        acc = p_ref[0]
        for q in range(1, n_dev):
            acc = acc + p_ref[q]
        tot = jnp.sum(acc, axis=0, keepdims=True)
        pos = 0
        where = [(n, l) for l in range(n_layers) for n in names] + [("final_norm_g", 0)]
        for n, l in where:
            width = w_refs[n].shape[1]
            g = tot[:, pos:pos + width]
            pos += width
            row = pl.ds(l, 1)
            delta, m_new, v_new = _adamw_math(g, w_refs[n][row, :], m_refs[n][row, :], v_refs[n][row, :])
            for o, val in zip(o_refs[n], (g, delta, m_new, v_new)):
                o[row, :] = val
        l_out[...] = (0.5 / n_loss) * jnp.sum(tot[:, pos:pos + n_loss], axis=-1, keepdims=True)

    shapes = [jax.ShapeDtypeStruct((1, 1), F32)]
    for n in every:
        shapes += [jax.ShapeDtypeStruct(w[n].shape, F32)] * 4
    res = pl.pallas_call(
        body, out_shape=shapes,
        compiler_params=pltpu.CompilerParams(vmem_limit_bytes=VMEM_LIMIT), name=name,
    )(parts, *[w[n] for n in every], *[m[n] for n in every], *[v[n] for n in every])
    return res[0], {n: res[1 + 4 * q:5 + 4 * q] for q, n in enumerate(every)}


BIG = ("w_in", "w_out", "w_up", "w_down")
COL_SHARDED = ("w_in", "w_up")
CONV = ("conv_a_w", "conv_b_w", "conv_f_w")
REPLICATED = ("mix_norm_g", "b_in", "conv_a_b", "ln_a_g", "ln_a_b", "ffn_norm_g")
KINDS = ("grad", "delta", "m", "v")
FFN_PART = ("w_up", "w_down")
MIX_PART = ("w_in", "w_out")


def _weights_from_gathered(g):
    n_dev, r, c = g.shape
    return g.reshape(n_dev * r, c)


def _slabs_from_full(grad):
    return grad.reshape(N_DEV, grad.shape[0] // N_DEV, grad.shape[1])


def _unwritten(like, *, name):
    return pl.pallas_call(lambda *refs: None, out_specs=[ANY] * len(like), out_shape=list(like), name=name)()


def kernel(x, mix_norm_g, w_in, b_in, conv_a_w, conv_a_b, ln_a_g, ln_a_b, conv_b_w, w_out, ffn_norm_g, w_up, conv_f_w, w_down, final_norm_g, loss_target, m_mix_norm_g, m_w_in, m_b_in, m_conv_a_w, m_conv_a_b, m_ln_a_g, m_ln_a_b, m_conv_b_w, m_w_out, m_ffn_norm_g, m_w_up, m_conv_f_w, m_w_down, m_final_norm_g, v_mix_norm_g, v_w_in, v_b_in, v_conv_a_w, v_conv_a_b, v_ln_a_g, v_ln_a_b, v_conv_b_w, v_w_out, v_ffn_norm_g, v_w_up, v_conv_f_w, v_w_down, v_final_norm_g):
    w = dict(mix_norm_g=mix_norm_g, w_in=w_in, b_in=b_in, conv_a_w=conv_a_w, conv_a_b=conv_a_b, ln_a_g=ln_a_g,
             ln_a_b=ln_a_b, conv_b_w=conv_b_w, w_out=w_out, ffn_norm_g=ffn_norm_g, w_up=w_up, conv_f_w=conv_f_w,
             w_down=w_down, final_norm_g=final_norm_g)
    m = dict(mix_norm_g=m_mix_norm_g, w_in=m_w_in, b_in=m_b_in, conv_a_w=m_conv_a_w, conv_a_b=m_conv_a_b,
             ln_a_g=m_ln_a_g, ln_a_b=m_ln_a_b, conv_b_w=m_conv_b_w, w_out=m_w_out, ffn_norm_g=m_ffn_norm_g,
             w_up=m_w_up, conv_f_w=m_conv_f_w, w_down=m_w_down, final_norm_g=m_final_norm_g)
    v = dict(mix_norm_g=v_mix_norm_g, w_in=v_w_in, b_in=v_b_in, conv_a_w=v_conv_a_w, conv_a_b=v_conv_a_b,
             ln_a_g=v_ln_a_g, ln_a_b=v_ln_a_b, conv_b_w=v_conv_b_w, w_out=v_w_out, ffn_norm_g=v_ffn_norm_g,
             w_up=v_w_up, conv_f_w=v_conv_f_w, w_down=v_w_down, final_norm_g=v_final_norm_g)
    order = list(w)
    n_layers = w_in.shape[0]
    xs = x[0]
    target = loss_target[0]
    flip = lambda a: jnp.transpose(a, (0, 2, 1))
    wt, mt, vt = ({n: flip(d[n]) if n in COL_SHARDED else d[n] for n in BIG} for d in (w, m, v))
    px, py, pc = _place()
    where = jnp.stack([pc, 2 * px + py]).astype(jnp.int32)
    me = 4 * px + 2 * py + pc

    assert BIG == MIX_PART + FFN_PART
    key = lambda n: n + "_t" if n in COL_SHARDED else n
    shard = lambda n, l: wt[n][l].astype(BF16)

    def gather_start(names, l, after, tag):
        shards = [shard(n, l) for n in names]
        lands = [_gather_landing(s, me) for s in shards]
        return _split_start(shards, lands, _gather_plan_first, 4 * len(shards), after, name=f"gather_first_start_{tag}")

    def gather_mid(first, after, tag):
        return _split_wait(first[0], first[1], first[2], first[3], after, _gather_plan_first,
                           name=f"gather_first_wait_{tag}")[1]

    def forward_start(lands, after, tag):
        return _split_start([], lands, _gather_plan_second, 3 * len(lands), after, name=f"gather_second_start_{tag}")

    def forward_finish(second, after, tag):
        return _split_wait(second[0], second[1], [], second[3], after, _gather_plan_second,
                           name=f"gather_second_wait_{tag}")[1]

    gathered = _all_gather([shard(n, 0) for n in MIX_PART] + [w[n] for n in CONV], name="gather_weights_0")
    params = [{n: w[n][l] for n in REPLICATED} for l in range(n_layers)]
    for n, g in zip(CONV, gathered[len(MIX_PART):]):
        n_dev, _, taps, c = g.shape
        full = g.transpose(1, 2, 0, 3).reshape(n_layers, taps, n_dev * c)
        for l in range(n_layers):
            params[l][n] = full[l]
    for n, g in zip(MIX_PART, gathered):
        params[0][key(n)] = _weights_from_gathered(g)
    ffn_first = gather_start(FFN_PART, 0, gathered[0], "0_ffn")
    pending = {}

    h = xs
    saved = []
    for l in range(n_layers):
        nxt = l + 1 if l + 1 < n_layers else None

        def before_up(x1, l=l, nxt=nxt):
            if l == 0:
                second = forward_start(gather_mid(ffn_first, x1, "0_ffn"), x1, "0_ffn")
                after = second[4]
            else:
                second = pending[l]["ffn"]
                after = x1
            if nxt is not None:
                pending[nxt] = dict(first=gather_start(BIG, nxt, after, str(nxt)))
                after = pending[nxt]["first"][4]
            for n, g in zip(FFN_PART, forward_finish(second, after, f"{l}_ffn")):
                params[l][key(n)] = _weights_from_gathered(g)

        h, keep = _layer_fwd(h, params[l], str(l), dep=ffn_first[4] if l == 0 else None, before_up=before_up)
        saved.append(keep)
        if nxt is not None:
            arrived = gather_mid(pending[nxt]["first"], h, str(nxt))
            mix_second = forward_start(arrived[:len(MIX_PART)], h, f"{nxt}_mix")
            pending[nxt]["ffn"] = forward_start(arrived[len(MIX_PART):], mix_second[4], f"{nxt}_ffn")
            for n, g in zip(MIX_PART, forward_finish(mix_second, pending[nxt]["ffn"][4], f"{nxt}_mix")):
                params[nxt][key(n)] = _weights_from_gathered(g)

    def start_siblings(slabs, after, tag):
        mines = [s.reshape(N_CHIP, 2, *s.shape[1:]) for s in slabs]
        lands = _unwritten([jax.ShapeDtypeStruct((N_CHIP, *m.shape[2:]), m.dtype) for m in mines],
                           name=f"reduce_siblings_landing_{tag}")
        return _split_start(mines, lands, _siblings_plan, len(mines), after, name=f"reduce_siblings_start_{tag}")

    def start_chips(sib, after, tag):
        mines, theirs = _split_wait(sib[0], sib[1], sib[2], sib[3], after, _siblings_plan,
                                    name=f"reduce_siblings_wait_{tag}")
        pairs, lands = _pair_sum(mines, theirs, where, name=f"pair_sum_{tag}")
        return _split_start(pairs, lands, _chips_plan, 3 * len(pairs), after, name=f"reduce_chips_start_{tag}")

    def finish_reduce(fly, after, tag):
        return _split_wait(fly[0], fly[1], fly[2], fly[3], after, _chips_plan, name=f"reduce_chips_wait_{tag}")[1]

    loss_sq, dh, dh_b, dgf = _loss_bwd(h, _row(final_norm_g), target, name="loss")
    conv_g = {n: [None] * n_layers for n in CONV}
    rep_g = [None] * n_layers
    siblings = {}
    flights = {}
    token = None
    for l in reversed(range(n_layers)):
        def ffn_grads(g, after, l=l):
            siblings[l, "ffn"] = start_siblings([_slabs_from_full(g[n]) for n in FFN_PART], after, f"{l}_ffn")
            return siblings[l, "ffn"][4]

        def ffn_sent(after, l=l):
            flights[l, "ffn"] = start_chips(siblings[l, "ffn"], after, f"{l}_ffn")
            return flights[l, "ffn"][4]

        def mix_grads(g, conv, after, l=l):
            for n in CONV:
                conv_g[n][l] = conv[n]
            slabs = [_slabs_from_full(g[n]) for n in MIX_PART]
            if l == 0:
                for n in CONV:
                    full = jnp.stack(conv_g[n])
                    _, taps, c = full.shape
                    slabs.append(full.reshape(n_layers, taps, N_DEV, c // N_DEV).transpose(2, 0, 1, 3)
                                 .reshape(N_DEV, n_layers * taps, c // N_DEV))
            siblings[l, "mix"] = start_siblings(slabs, after, f"{l}_mix")
            return siblings[l, "mix"][4]

        def mix_sent(after, l=l):
            flights[l, "mix"] = start_chips(siblings[l, "mix"], after, f"{l}_mix")
            return flights[l, "mix"][4]

        dh, dh_b, rep_g[l], token = _layer_bwd(dh, dh_b, params[l], saved[l], str(l), ffn_grads, ffn_sent,
                                               mix_grads, mix_sent, dep=token)

    sums = {key: finish_reduce(fly, dh, f"{key[0]}_{key[1]}") for key, fly in flights.items() if key != (0, "mix")}
    out = {k: {} for k in KINDS}

    def adamw_big(names, part, dep):
        for q, n in enumerate(names):
            layer_parts = [sums[l, part][q] for l in range(n_layers)]
            res = _adamw_sharded(layer_parts, wt[n], mt[n], vt[n], name=f"adamw_{n}", dep=dep)
            for k, r in zip(KINDS, res):
                out[k][n] = flip(r) if n in COL_SHARDED else r

    adamw_big(FFN_PART, "ffn", token)

    rep_cols = [rep_g[l][n] for l in range(n_layers) for n in REPLICATED] + [dgf, loss_sq]
    rep_all = _all_gather_small(_fold_partials(rep_cols, name="fold_small"), name="gather_small")
    with_final = lambda d: {**{n: d[n] for n in REPLICATED}, "final_norm_g": _row(d["final_norm_g"])}
    loss, rep_res = _adamw_replicated(rep_all, REPLICATED, with_final(w), with_final(m), with_final(v),
                                      loss_sq.shape[1], name="adamw_small")
    for n, res in rep_res.items():
        for k, r in zip(KINDS, res):
            out[k][n] = r.reshape(w[n].shape)

    last = finish_reduce(flights[0, "mix"], rep_res["b_in"][0], "0_mix")
    sums[0, "mix"] = last[:len(MIX_PART)]
    adamw_big(MIX_PART, "mix", None)
    for n, p in zip(CONV, last[len(MIX_PART):]):
        as_one = lambda a: a.reshape(1, *p.shape[1:])
        for k, r in zip(KINDS, _adamw_sharded([p], as_one(w[n]), as_one(m[n]), as_one(v[n]), name=f"adamw_{n}")):
            out[k][n] = r.reshape(w[n].shape)

    grad_x = dh.reshape(x.shape)
    return (loss.reshape(()), grad_x, *[out["grad"][n] for n in order], *[out["delta"][n] for n in order],
            *[out["m"][n] for n in order], *[out["v"][n] for n in order])
```

```python
import functools

import jax
import jax.numpy as jnp
from jax import lax
from jax.experimental import pallas as pl
from jax.experimental.pallas import tpu as pltpu

F32 = jnp.float32
BF16 = jnp.bfloat16

N_DEV = 8
N_CHIP = 4
D_CONF = 512
CONF_K = 31
SHORT_K = 3
EPS = 1e-6
HALO = 32
HALO3 = 8
HALO3_BLK = 16
LANES = 128
SUB = 8
VMEM_LIMIT = 56 * 1024 * 1024

ADAM_LR = 0.001
ADAM_B1 = 0.9
ADAM_B2 = 0.999
ADAM_EPS = 1e-08
ADAM_WD = 0.01
ADAM_STEP = 10

MESH = pl.DeviceIdType.MESH
ANY = pl.BlockSpec(memory_space=pl.ANY)


def _params(*sem):
    return pltpu.CompilerParams(dimension_semantics=sem, vmem_limit_bytes=VMEM_LIMIT)


def _resident(shape, index_map):
    return pl.BlockSpec(shape, index_map, pipeline_mode=pl.Buffered(1))


def _row_loop(n_rows, rb, fn, unroll=1):
    rb = min(rb, n_rows)

    def body(i, carry):
        fn(pl.ds(pl.multiple_of(i * rb, rb), rb))
        return carry
    lax.fori_loop(0, n_rows // rb, body, 0, unroll=unroll)


def _rows8(v):
    acc = v[0:SUB]
    for k in range(1, v.shape[0] // SUB):
        acc = acc + v[k * SUB:(k + 1) * SUB]
    return acc


def _sigmoid(z):
    return 0.5 * jnp.tanh(0.5 * z) + 0.5


def _dot(a, b):
    return jnp.dot(a, b, preferred_element_type=F32)


def _dot_nt(a, b):
    return lax.dot_general(a, b, (((1,), (1,)), ((), ())), preferred_element_type=F32)


def _dot_tn(a, b):
    return lax.dot_general(a, b, (((0,), (0,)), ((), ())), preferred_element_type=F32)


def _replicate_taps(w_ref, wrep, taps):
    for k in range(taps):
        wrep[pl.ds(k * SUB, SUB), :] = jnp.broadcast_to(w_ref[pl.ds(k, 1), :], (SUB, w_ref.shape[1]))


def _shift_copies(win, shf, lanes):
    span = win.shape[0] - SUB
    for r in range(1, SUB):
        for j0 in range(0, span, 64):
            n = min(64, span - j0)
            shf[r - 1, pl.ds(j0, n), lanes] = win[pl.ds(j0 + r, n), lanes]


def _rows_at(win, shf, off, rb, lanes):
    if shf is None or off % SUB == 0:
        return win[pl.ds(off, rb), lanes]
    return shf[off % SUB - 1, pl.ds(off - off % SUB, rb), lanes]


def _conv_taps(win, wrep, out, *, taps, n_rows, base, width, transposed=False, bias_ref=None, shf=None):
    rb = min(64, n_rows)

    def lane_body(cb, carry):
        lanes = pl.ds(pl.multiple_of(cb * LANES, LANES), LANES)
        if shf is not None:
            _shift_copies(win, shf, lanes)
        for r0 in range(0, n_rows, rb):
            acc = None
            for k in range(taps):
                off = (taps - 1 - k) if transposed else (k - (taps - 1))
                wk = jnp.tile(wrep[pl.ds(k * SUB, SUB), lanes], (rb // SUB, 1))
                term = wk * _rows_at(win, shf, base + r0 + off, rb, lanes)
                acc = term if acc is None else acc + term
            if bias_ref is not None:
                acc = acc + bias_ref[:, lanes]
            out[pl.ds(r0, rb), lanes] = acc.astype(out.dtype)
        return carry

    lax.fori_loop(0, width // LANES, lane_body, 0)


def _conv_bwd_taps(win, wrep, x_cur, dx_out, dw_acc, *, taps, n_rows, width, shf=None):
    rb = min(32 if taps > 8 else 64, n_rows)

    def lane_body(cb, carry):
        lanes = pl.ds(pl.multiple_of(cb * LANES, LANES), LANES)
        if shf is not None:
            _shift_copies(win, shf, lanes)
        sums = [None] * taps
        for r0 in range(0, n_rows, rb):
            xv = x_cur[pl.ds(r0, rb), lanes].astype(F32)
            acc = None
            for k in range(taps):
                shifted = _rows_at(win, shf, r0 + taps - 1 - k, rb, lanes)
                term = jnp.tile(wrep[pl.ds(k * SUB, SUB), lanes], (rb // SUB, 1)) * shifted
                acc = term if acc is None else acc + term
                part = _rows8(xv * shifted)
                sums[k] = part if sums[k] is None else sums[k] + part
            dx_out[pl.ds(r0, rb), lanes] = acc.astype(dx_out.dtype)
        for k in range(taps):
            dw_acc[pl.ds(k * SUB, SUB), lanes] += sums[k]
        return carry

    lax.fori_loop(0, width // LANES, lane_body, 0)


def _fold8(acc_ref, taps):
    return jnp.concatenate(
        [jnp.sum(acc_ref[pl.ds(k * SUB, SUB), :], axis=0, keepdims=True) for k in range(taps)], axis=0)


def _seq_tile(s_len):
    return min(512, s_len)


def _mm_tile(s_len):
    return min(512, s_len)


def _ff_chunk(ff):
    best = LANES
    for c in range(LANES, 1408 + 1, LANES):
        if ff % c == 0:
            best = c
    return best


def _col_tile(n):
    for c in (512, 1408, 256, LANES):
        if n % c == 0:
            return c
    return n


def _rms_matmul(x, g, wt, b, *, name, dep=None):
    s_len, d = x.shape
    n = wt.shape[0]
    tm = _mm_tile(s_len)
    cn = _col_tile(n)
    has_bias = b is not None

    def body(*refs):
        x_ref, g_ref, w_ref = refs[0:3]
        b_ref = refs[3] if has_bias else None
        o_ref, h_ref = refs[-2:]

        def blk(rows):
            xv = x_ref[rows, :]
            r = lax.rsqrt(jnp.mean(xv * xv, axis=-1, keepdims=True) + EPS)
            h_ref[rows, :] = ((xv * r) * g_ref[...]).astype(BF16)

        rb = min(128, tm)
        for r0 in range(0, tm, rb):
            blk(pl.ds(r0, rb))
        for j in range(n // cn):
            acc = _dot_nt(h_ref[...], w_ref[j * cn:(j + 1) * cn, :])
            if has_bias:
                acc = acc + b_ref[:, j * cn:(j + 1) * cn]
            o_ref[:, j * cn:(j + 1) * cn] = acc.astype(BF16)

    in_specs = [pl.BlockSpec((tm, d), lambda i: (i, 0)), _resident((1, d), lambda i: (0, 0)),
                _resident((n, d), lambda i: (0, 0))]
    args = [x, g, wt]
    if has_bias:
        in_specs.append(_resident((1, n), lambda i: (0, 0)))
        args.append(b)
    in_specs.append(ANY)
    args.append(x if dep is None else dep)
    return pl.pallas_call(
        body, grid=(s_len // tm,), in_specs=in_specs,
        out_specs=[pl.BlockSpec((tm, n), lambda i: (i, 0)), pl.BlockSpec((tm, d), lambda i: (i, 0))],
        out_shape=[jax.ShapeDtypeStruct((s_len, n), BF16), jax.ShapeDtypeStruct((s_len, d), BF16)],
        compiler_params=_params("parallel"), name=name,
    )(*args)


def _mix_windows(u_ref, uh_ref, gw, pw, first, t):
    c = D_CONF
    uh = uh_ref[...].astype(F32)
    gw[0:HALO, :] = jnp.where(first, 0.0, uh[:, 0:c] * _sigmoid(uh[:, c:2 * c]))
    pw[0:HALO3, :] = jnp.where(first, 0.0, uh[HALO - HALO3:HALO, 3 * c:4 * c] * uh[HALO - HALO3:HALO, 4 * c:5 * c])

    def blk(rows):
        dst = pl.ds(pl.multiple_of(rows.start + HALO, SUB), rows.size)
        gw[dst, :] = u_ref[rows, 0:c].astype(F32) * _sigmoid(u_ref[rows, c:2 * c].astype(F32))
        dst3 = pl.ds(pl.multiple_of(rows.start + HALO3, SUB), rows.size)
        pw[dst3, :] = u_ref[rows, 3 * c:4 * c].astype(F32) * u_ref[rows, 4 * c:5 * c].astype(F32)
    _row_loop(t, 64, blk)


def _mix_fwd(u, x0, wa, ba, lg, lb, wb, w_out, *, name):
    s_len, d_in = u.shape
    d = x0.shape[1]
    c = D_CONF
    t = _seq_tile(s_len)
    per = t // HALO

    def body(u_ref, uh_ref, x0_ref, wa_ref, ba_ref, lg_ref, lb_ref, wb_ref, wo_ref, y_ref, x1_ref, ca, cb,
             gw, pw, wrep_a, wrep_b, shf):
        first = pl.program_id(0) == 0
        _mix_windows(u_ref, uh_ref, gw, pw, first, t)
        _replicate_taps(wa_ref, wrep_a, CONF_K)
        _replicate_taps(wb_ref, wrep_b, SHORT_K)
        _conv_taps(gw, wrep_a, ca, taps=CONF_K, n_rows=t, base=HALO, width=c, bias_ref=ba_ref, shf=shf)
        _conv_taps(pw, wrep_b, cb, taps=SHORT_K, n_rows=t, base=HALO3, width=c)

        def blk(rows):
            cv = ca[rows, :]
            mu = jnp.mean(cv, axis=-1, keepdims=True)
            xc = cv - mu
            var = jnp.mean(xc * xc, axis=-1, keepdims=True)
            ln = (xc * lax.rsqrt(var + EPS)) * lg_ref[...] + lb_ref[...]
            y_ref[rows, 0:c] = (ln * _sigmoid(ln)).astype(BF16)
            y_ref[rows, c:2 * c] = (u_ref[rows, 2 * c:3 * c].astype(F32) * cb[rows, :]).astype(BF16)
        _row_loop(t, 64, blk)
        x1_ref[...] = x0_ref[...] + _dot(y_ref[...], wo_ref[...])

    small = lambda r: _resident((r, c), lambda i: (0, 0))
    return pl.pallas_call(
        body, grid=(s_len // t,),
        in_specs=[pl.BlockSpec((t, d_in), lambda i: (i, 0)),
                  pl.BlockSpec((HALO, d_in), lambda i: (jnp.maximum(i * per - 1, 0), 0)),
                  pl.BlockSpec((t, d), lambda i: (i, 0)),
                  small(CONF_K), small(1), small(1), small(1), small(SHORT_K),
                  _resident((2 * c, d), lambda i: (0, 0))],
        out_specs=[pl.BlockSpec((t, 2 * c), lambda i: (i, 0)), pl.BlockSpec((t, d), lambda i: (i, 0)),
                   pl.BlockSpec((t, c), lambda i: (i, 0)), pl.BlockSpec((t, c), lambda i: (i, 0))],
        out_shape=[jax.ShapeDtypeStruct((s_len, 2 * c), BF16), jax.ShapeDtypeStruct((s_len, d), F32),
                   jax.ShapeDtypeStruct((s_len, c), F32), jax.ShapeDtypeStruct((s_len, c), F32)],
        scratch_shapes=[pltpu.VMEM((HALO + t, c), F32), pltpu.VMEM((HALO3 + t, c), F32),
                        pltpu.VMEM((CONF_K * SUB, c), F32), pltpu.VMEM((SHORT_K * SUB, c), F32),
                        pltpu.VMEM((SUB - 1, HALO + t, c), F32)],
        compiler_params=_params("arbitrary"), name=name,
    )(u, u, x0, wa, ba, lg, lb, wb, w_out)


def _ffn_windows(ug_ref, ugh_ref, uv_ref, uvh_ref, gwin, vwin, first, t):
    lo = HALO3_BLK - HALO3
    gwin[0:HALO3, :] = jnp.where(first, 0.0, ugh_ref[...].astype(F32)[lo:HALO3_BLK])
    vwin[0:HALO3, :] = jnp.where(first, 0.0, uvh_ref[...].astype(F32)[lo:HALO3_BLK])

    def blk(rows):
        dst = pl.ds(pl.multiple_of(rows.start + HALO3, SUB), rows.size)
        gwin[dst, :] = ug_ref[rows, :].astype(F32)
        vwin[dst, :] = uv_ref[rows, :].astype(F32)
    _row_loop(t, 64, blk)


def _ffn_fwd(uf, x1, wf, w_down, *, name):
    s_len, ff2 = uf.shape
    ff = ff2 // 2
    d = x1.shape[1]
    t = _seq_tile(s_len)
    fc = _ff_chunk(ff)
    nc = ff // fc
    per = t // HALO3_BLK

    def body(ug_ref, ugh_ref, uv_ref, uvh_ref, x1_ref, wfg_ref, wfv_ref, wd_ref,
             act_ref, x2_ref, cg_ref, cv_ref, gwin, vwin, cg, cv, wrep_g, wrep_v):
        first = pl.program_id(0) == 0
        _ffn_windows(ug_ref, ugh_ref, uv_ref, uvh_ref, gwin, vwin, first, t)
        _replicate_taps(wfg_ref, wrep_g, SHORT_K)
        _replicate_taps(wfv_ref, wrep_v, SHORT_K)
        _conv_taps(gwin, wrep_g, cg, taps=SHORT_K, n_rows=t, base=HALO3, width=fc)
        _conv_taps(vwin, wrep_v, cv, taps=SHORT_K, n_rows=t, base=HALO3, width=fc)

        def blk(rows):
            gv = cg[rows, :]
            vv = cv[rows, :]
            cg_ref[rows, :] = gv.astype(BF16)
            cv_ref[rows, :] = vv.astype(BF16)
            act_ref[rows, :] = ((gv * _sigmoid(gv)) * vv).astype(BF16)
        _row_loop(t, 32, blk, unroll=2)

        @pl.when(pl.program_id(1) == 0)
        def _():
            x2_ref[...] = x1_ref[...]
        x2_ref[...] += _dot(act_ref[...], wd_ref[...])

    halo_map = lambda off: (lambda i, j: (jnp.maximum(i * per - 1, 0), j + off))
    return pl.pallas_call(
        body, grid=(s_len // t, nc),
        in_specs=[pl.BlockSpec((t, fc), lambda i, j: (i, j)), pl.BlockSpec((HALO3_BLK, fc), halo_map(0)),
                  pl.BlockSpec((t, fc), lambda i, j: (i, j + nc)), pl.BlockSpec((HALO3_BLK, fc), halo_map(nc)),
                  pl.BlockSpec((t, d), lambda i, j: (i, 0)),
                  pl.BlockSpec((SHORT_K, fc), lambda i, j: (0, j)),
                  pl.BlockSpec((SHORT_K, fc), lambda i, j: (0, j + nc)),
                  pl.BlockSpec((fc, d), lambda i, j: (j, 0))],
        out_specs=[pl.BlockSpec((t, fc), lambda i, j: (i, j)), pl.BlockSpec((t, d), lambda i, j: (i, 0)),
                   pl.BlockSpec((t, fc), lambda i, j: (i, j)), pl.BlockSpec((t, fc), lambda i, j: (i, j))],
        out_shape=[jax.ShapeDtypeStruct((s_len, ff), BF16), jax.ShapeDtypeStruct((s_len, d), F32),
                   jax.ShapeDtypeStruct((s_len, ff), BF16), jax.ShapeDtypeStruct((s_len, ff), BF16)],
        scratch_shapes=[pltpu.VMEM((HALO3 + t, fc), F32), pltpu.VMEM((HALO3 + t, fc), F32),
                        pltpu.VMEM((t, fc), F32), pltpu.VMEM((t, fc), F32),
                        pltpu.VMEM((SHORT_K * SUB, fc), F32), pltpu.VMEM((SHORT_K * SUB, fc), F32)],
        compiler_params=_params("parallel", "arbitrary"), name=name,
    )(uf, uf, uf, uf, x1, wf, wf, w_down)


def _loss_bwd(x, g, target, *, name):
    s_len, d = x.shape
    t = _seq_tile(s_len)

    def body(x_ref, g_ref, t_ref, l_ref, dx_ref, dxb_ref, dg_ref):
        @pl.when(pl.program_id(0) == 0)
        def _():
            l_ref[...] = jnp.zeros_like(l_ref)
            dg_ref[...] = jnp.zeros_like(dg_ref)

        def blk(rows):
            xv = x_ref[rows, :]
            r = lax.rsqrt(jnp.mean(xv * xv, axis=-1, keepdims=True) + EPS)
            xn = xv * r
            e = xn * g_ref[...] - t_ref[rows, :]
            l_ref[...] += _rows8(e * e)
            dy = e * (1.0 / d)
            dg_ref[...] += _rows8(dy * xn)
            dn = dy * g_ref[...]
            dx = r * (dn - xn * jnp.mean(dn * xn, axis=-1, keepdims=True))
            dx_ref[rows, :] = dx
            dxb_ref[rows, :] = dx.astype(BF16)
        _row_loop(t, 64, blk)

    row = pl.BlockSpec((t, d), lambda i: (i, 0))
    part = pl.BlockSpec((SUB, d), lambda i: (0, 0))
    return pl.pallas_call(
        body, grid=(s_len // t,),
        in_specs=[row, _resident((1, d), lambda i: (0, 0)), row],
        out_specs=[part, row, row, part],
        out_shape=[jax.ShapeDtypeStruct((SUB, d), F32), jax.ShapeDtypeStruct((s_len, d), F32),
                   jax.ShapeDtypeStruct((s_len, d), BF16), jax.ShapeDtypeStruct((SUB, d), F32)],
        compiler_params=_params("arbitrary"), name=name,
    )(x, g, target)


def _ffn_bwd(dx2, uf, cg, cv, wf, w_down, *, name, dep=None):
    s_len, ff2 = uf.shape
    ff = ff2 // 2
    d = dx2.shape[1]
    t = _seq_tile(s_len)
    n_t = s_len // t
    fc = _ff_chunk(ff)
    nc = ff // fc

    def body(dx_ref, ug_ref, uv_ref, cg_ref, cv_ref, wfg_ref, wfv_ref, wd_ref, dep_ref,
             duf_ref, dwg_ref, dwv_ref, dact, dgw, dvw, awg, awv, wrep_g, wrep_v):
        i = pl.program_id(1)

        @pl.when(i == 0)
        def _():
            dgw[t:t + HALO3, :] = jnp.zeros((HALO3, fc), F32)
            dvw[t:t + HALO3, :] = jnp.zeros((HALO3, fc), F32)
            awg[...] = jnp.zeros_like(awg)
            awv[...] = jnp.zeros_like(awv)

        _replicate_taps(wfg_ref, wrep_g, SHORT_K)
        _replicate_taps(wfv_ref, wrep_v, SHORT_K)

        def blk(rows):
            gv = cg_ref[rows, :].astype(F32)
            sg = _sigmoid(gv)
            da = dact[rows, :]
            dgw[rows, :] = (da * cv_ref[rows, :].astype(F32)) * (sg * (1.0 + gv * (1.0 - sg)))
            dvw[rows, :] = da * (gv * sg)

        dact[...] = _dot_nt(dx_ref[...], wd_ref[...])
        _row_loop(t, 32, blk, unroll=2)

        _conv_bwd_taps(dgw, wrep_g, ug_ref, duf_ref.at[0], awg, taps=SHORT_K, n_rows=t, width=fc)
        _conv_bwd_taps(dvw, wrep_v, uv_ref, duf_ref.at[1], awv, taps=SHORT_K, n_rows=t, width=fc)
        dgw[t:t + HALO3, :] = dgw[0:HALO3, :]
        dvw[t:t + HALO3, :] = dvw[0:HALO3, :]

        @pl.when(i == n_t - 1)
        def _():
            dwg_ref[...] = _fold8(awg, SHORT_K)
            dwv_ref[...] = _fold8(awv, SHORT_K)

    rev = lambda i: n_t - 1 - i
    gate = pl.BlockSpec((t, fc), lambda j, i: (rev(i), j))
    value = pl.BlockSpec((t, fc), lambda j, i: (rev(i), j + nc))
    return pl.pallas_call(
        body, grid=(nc, n_t),
        in_specs=[pl.BlockSpec((t, d), lambda j, i: (rev(i), 0)), gate, value, gate, gate,
                  pl.BlockSpec((SHORT_K, fc), lambda j, i: (0, j)),
                  pl.BlockSpec((SHORT_K, fc), lambda j, i: (0, j + nc)),
                  pl.BlockSpec((fc, d), lambda j, i: (j, 0)), ANY],
        out_specs=[pl.BlockSpec((2, t, fc), lambda j, i: (0, rev(i), j)),
                   pl.BlockSpec((SHORT_K, fc), lambda j, i: (0, j)), pl.BlockSpec((SHORT_K, fc), lambda j, i: (0, j))],
        out_shape=[jax.ShapeDtypeStruct((2, s_len, ff), BF16),
                   jax.ShapeDtypeStruct((SHORT_K, ff), F32), jax.ShapeDtypeStruct((SHORT_K, ff), F32)],
        scratch_shapes=[pltpu.VMEM((t, fc), F32),
                        pltpu.VMEM((t + HALO3, fc), F32), pltpu.VMEM((t + HALO3, fc), F32),
                        pltpu.VMEM((SHORT_K * SUB, fc), F32), pltpu.VMEM((SHORT_K * SUB, fc), F32),
                        pltpu.VMEM((SHORT_K * SUB, fc), F32), pltpu.VMEM((SHORT_K * SUB, fc), F32)],
        compiler_params=_params("arbitrary", "arbitrary"), name=name,
    )(dx2, uf, uf, cg, cv, wf, wf, w_down, uf if dep is None else dep)


def _mix_bwd(dx1, u, ca, cb, wa, lg, lb, wb, w_out, *, name, dep=None):
    s_len, d_in = u.shape
    d = dx1.shape[1]
    c = D_CONF
    t = _seq_tile(s_len)
    n_t = s_len // t

    def body(dx_ref, u_ref, ca_ref, cb_ref, wa_ref, lg_ref, lb_ref, wb_ref, wo_ref, dep_ref,
             du_ref, dwa_ref, dwb_ref, dba_ref, dlg_ref, dlb_ref, dbin_ref,
             glu, prod, dyc, dcaw, dcbw, dglu, dp, awa, awb, wrep_a, wrep_b, shf):
        i = pl.program_id(0)
        dyc[...] = _dot_nt(dx_ref[...], wo_ref[...])
        _replicate_taps(wa_ref, wrep_a, CONF_K)
        _replicate_taps(wb_ref, wrep_b, SHORT_K)

        @pl.when(i == 0)
        def _():
            dcaw[t:t + HALO, :] = jnp.zeros((HALO, c), F32)
            dcbw[t:t + HALO3, :] = jnp.zeros((HALO3, c), F32)
            awa[...] = jnp.zeros_like(awa)
            awb[...] = jnp.zeros_like(awb)
            dba_ref[...] = jnp.zeros_like(dba_ref)
            dlg_ref[...] = jnp.zeros_like(dlg_ref)
            dlb_ref[...] = jnp.zeros_like(dlb_ref)
            dbin_ref[...] = jnp.zeros_like(dbin_ref)

        def blk1(rows):
            cv = ca_ref[rows, :]
            mu = jnp.mean(cv, axis=-1, keepdims=True)
            xc = cv - mu
            rstd = lax.rsqrt(jnp.mean(xc * xc, axis=-1, keepdims=True) + EPS)
            nrm = xc * rstd
            ln = nrm * lg_ref[...] + lb_ref[...]
            sg = _sigmoid(ln)
            dln = dyc[rows, 0:c] * (sg * (1.0 + ln * (1.0 - sg)))
            dlg_ref[...] += _rows8(dln * nrm)
            dlb_ref[...] += _rows8(dln)
            dn = dln * lg_ref[...]
            dca = rstd * (dn - jnp.mean(dn, axis=-1, keepdims=True)
                          - nrm * jnp.mean(dn * nrm, axis=-1, keepdims=True))
            dcaw[rows, :] = dca
            dba_ref[...] += _rows8(dca)
            ds = dyc[rows, c:2 * c]
            dgb = ds * cb_ref[rows, :]
            dcbw[rows, :] = ds * u_ref[rows, 2 * c:3 * c].astype(F32)
            du_ref[rows, 2 * c:3 * c] = dgb.astype(BF16)
            dbin_ref[:, 2 * c:3 * c] += _rows8(dgb)
            glu[rows, :] = u_ref[rows, 0:c].astype(F32) * _sigmoid(u_ref[rows, c:2 * c].astype(F32))
            prod[rows, :] = u_ref[rows, 3 * c:4 * c].astype(F32) * u_ref[rows, 4 * c:5 * c].astype(F32)
        _row_loop(t, 64, blk1, unroll=2)

        _conv_bwd_taps(dcaw, wrep_a, glu, dglu, awa, taps=CONF_K, n_rows=t, width=c, shf=shf)
        _conv_bwd_taps(dcbw, wrep_b, prod, dp, awb, taps=SHORT_K, n_rows=t, width=c)
        dcaw[t:t + HALO, :] = dcaw[0:HALO, :]
        dcbw[t:t + HALO3, :] = dcbw[0:HALO3, :]

        def blk2(rows):
            av = u_ref[rows, 0:c].astype(F32)
            sg = _sigmoid(u_ref[rows, c:2 * c].astype(F32))
            dg = dglu[rows, :]
            d_av = dg * sg
            d_ag = (dg * av) * (sg * (1.0 - sg))
            dpv = dp[rows, :]
            d_gc = dpv * u_ref[rows, 4 * c:5 * c].astype(F32)
            d_vs = dpv * u_ref[rows, 3 * c:4 * c].astype(F32)
            du_ref[rows, 0:c] = d_av.astype(BF16)
            du_ref[rows, c:2 * c] = d_ag.astype(BF16)
            du_ref[rows, 3 * c:4 * c] = d_gc.astype(BF16)
            du_ref[rows, 4 * c:5 * c] = d_vs.astype(BF16)
            dbin_ref[:, 0:c] += _rows8(d_av)
            dbin_ref[:, c:2 * c] += _rows8(d_ag)
            dbin_ref[:, 3 * c:4 * c] += _rows8(d_gc)
            dbin_ref[:, 4 * c:5 * c] += _rows8(d_vs)
        _row_loop(t, 64, blk2)

        @pl.when(i == n_t - 1)
        def _():
            dwa_ref[...] = _fold8(awa, CONF_K)
            dwb_ref[...] = _fold8(awb, SHORT_K)

    rev = lambda i: n_t - 1 - i
    small_in = lambda r: _resident((r, c), lambda i: (0, 0))
    small = lambda r: pl.BlockSpec((r, c), lambda i: (0, 0))
    return pl.pallas_call(
        body, grid=(n_t,),
        in_specs=[pl.BlockSpec((t, d), lambda i: (rev(i), 0)),
                  pl.BlockSpec((t, d_in), lambda i: (rev(i), 0)),
                  pl.BlockSpec((t, c), lambda i: (rev(i), 0)), pl.BlockSpec((t, c), lambda i: (rev(i), 0)),
                  small_in(CONF_K), small_in(1), small_in(1), small_in(SHORT_K),
                  _resident((2 * c, d), lambda i: (0, 0)), ANY],
        out_specs=[pl.BlockSpec((t, d_in), lambda i: (rev(i), 0)),
                   small(CONF_K), small(SHORT_K), small(SUB), small(SUB), small(SUB),
                   pl.BlockSpec((SUB, d_in), lambda i: (0, 0))],
        out_shape=[jax.ShapeDtypeStruct((s_len, d_in), BF16),
                   jax.ShapeDtypeStruct((CONF_K, c), F32), jax.ShapeDtypeStruct((SHORT_K, c), F32),
                   jax.ShapeDtypeStruct((SUB, c), F32), jax.ShapeDtypeStruct((SUB, c), F32),
                   jax.ShapeDtypeStruct((SUB, c), F32), jax.ShapeDtypeStruct((SUB, d_in), F32)],
        scratch_shapes=[pltpu.VMEM((t, c), F32), pltpu.VMEM((t, c), F32), pltpu.VMEM((t, 2 * c), F32),
                        pltpu.VMEM((t + HALO, c), F32), pltpu.VMEM((t + HALO3, c), F32),
                        pltpu.VMEM((t, c), F32), pltpu.VMEM((t, c), F32),
                        pltpu.VMEM((CONF_K * SUB, c), F32), pltpu.VMEM((SHORT_K * SUB, c), F32),
                        pltpu.VMEM((CONF_K * SUB, c), F32), pltpu.VMEM((SHORT_K * SUB, c), F32),
                        pltpu.VMEM((SUB - 1, t + HALO, c), F32)],
        compiler_params=_params("arbitrary"), name=name,
    )(dx1, u, ca, cb, wa, lg, lb, wb, w_out, u if dep is None else dep)


def _matmul_tn(a, b, *, name):
    n_p, s_len, k = a.shape
    n = b.shape[1]
    tk = _col_tile(k)
    per = k // tk

    def body(a_ref, b_ref, o_ref):
        o_ref[...] = _dot_tn(a_ref[...], b_ref[...]).astype(BF16)

    return pl.pallas_call(
        body, grid=(n_p, per),
        in_specs=[pl.BlockSpec((None, s_len, tk), lambda p, j: (p, 0, j)), _resident((s_len, n), lambda p, j: (0, 0))],
        out_specs=pl.BlockSpec((tk, n), lambda p, j: (p * per + j, 0)),
        out_shape=jax.ShapeDtypeStruct((n_p * k, n), BF16),
        compiler_params=_params("parallel", "parallel"), name=name,
    )(a, b)


def _matmul_rmsbwd(dzs, wt, x, g, dx_in, *, name, dep=None):
    s_len, d = x.shape
    n_z, _, nj = dzs.shape
    t = _mm_tile(s_len)

    def body(*refs):
        dz_refs = refs[0:n_z]
        w_refs = refs[n_z:2 * n_z]
        x_ref, g_ref, dxi_ref, _, dx_ref, dxb_ref, dg_ref, dh = refs[2 * n_z:]

        @pl.when(pl.program_id(0) == 0)
        def _():
            dg_ref[...] = jnp.zeros_like(dg_ref)

        def blk(rows):
            xv = x_ref[rows, :]
            r = lax.rsqrt(jnp.mean(xv * xv, axis=-1, keepdims=True) + EPS)
            xn = xv * r
            dhv = dh[rows, :]
            dg_ref[...] += _rows8(dhv * xn)
            dn = dhv * g_ref[...]
            dx = dxi_ref[rows, :] + r * (dn - xn * jnp.mean(dn * xn, axis=-1, keepdims=True))
            dx_ref[rows, :] = dx
            dxb_ref[rows, :] = dx.astype(BF16)

        half = t // 2
        rb = min(128, half)
        for lo in range(0, t, half):
            acc = _dot(dz_refs[0][lo:lo + half, :], w_refs[0][...])
            for q in range(1, n_z):
                acc = acc + _dot(dz_refs[q][lo:lo + half, :], w_refs[q][...])
            dh[lo:lo + half, :] = acc
            for r0 in range(lo, lo + half, rb):
                blk(pl.ds(r0, rb))

    row = pl.BlockSpec((t, d), lambda i: (i, 0))
    in_specs = [pl.BlockSpec((None, t, nj), functools.partial(lambda q, i: (q, i, 0), q)) for q in range(n_z)]
    in_specs += [_resident((nj, d), functools.partial(lambda q, i: (q, 0), q)) for q in range(n_z)]
    in_specs += [row, _resident((1, d), lambda i: (0, 0)), row, ANY]
    return pl.pallas_call(
        body, grid=(s_len // t,), in_specs=in_specs,
        out_specs=[row, row, pl.BlockSpec((SUB, d), lambda i: (0, 0))],
        out_shape=[jax.ShapeDtypeStruct((s_len, d), F32), jax.ShapeDtypeStruct((s_len, d), BF16),
                   jax.ShapeDtypeStruct((SUB, d), F32)],
        scratch_shapes=[pltpu.VMEM((t, d), F32)],
        compiler_params=_params("arbitrary"), name=name,
    )(*([dzs] * n_z), *([wt] * n_z), x, g, dx_in, x if dep is None else dep)


def _row(v):
    return v.reshape(1, -1)


def _layer_fwd(x0, p, tag, dep=None, before_up=None):
    u, h1 = _rms_matmul(x0, _row(p["mix_norm_g"]), p["w_in_t"], _row(p["b_in"]), name=f"in_proj_{tag}", dep=dep)
    ycat, x1, ca, cb = _mix_fwd(u, x0, p["conv_a_w"], _row(p["conv_a_b"]), _row(p["ln_a_g"]), _row(p["ln_a_b"]),
                            p["conv_b_w"], p["w_out"], name=f"mix_fwd_{tag}")
    if before_up is not None:
        before_up(x1)
    uf, h2 = _rms_matmul(x1, _row(p["ffn_norm_g"]), p["w_up_t"], None, name=f"up_proj_{tag}")
    act, x2, cg, cv = _ffn_fwd(uf, x1, p["conv_f_w"], p["w_down"], name=f"ffn_fwd_{tag}")
    return x2, dict(x0=x0, h1=h1, u=u, ca=ca, cb=cb, ycat=ycat, x1=x1, h2=h2, uf=uf, cg=cg, cv=cv, act=act)


def _layer_bwd(dx2, dx2_b, p, saved, tag, ffn_grads, ffn_sent, mix_grads, mix_sent, dep=None):
    d_uf, dwf_g, dwf_v = _ffn_bwd(dx2_b, saved["uf"], saved["cg"], saved["cv"], p["conv_f_w"], p["w_down"],
                                  name=f"ffn_bwd_{tag}", dep=dep)
    g_down = _matmul_tn(saved["act"][None], dx2_b, name=f"dw_down_{tag}")
    g_up = _matmul_tn(d_uf, saved["h2"], name=f"dw_up_{tag}")
    dep_ffn = ffn_grads(dict(w_up=g_up, w_down=g_down), dx2_b)
    dx1, dx1_b, dg2 = _matmul_rmsbwd(d_uf, p["w_up_t"], saved["x1"], _row(p["ffn_norm_g"]), dx2,
                                     name=f"dh_ffn_{tag}", dep=dep_ffn)
    du, dwa, dwb, dba, dlg, dlb, dbin = _mix_bwd(
        dx1_b, saved["u"], saved["ca"], saved["cb"], p["conv_a_w"], _row(p["ln_a_g"]), _row(p["ln_a_b"]),
        p["conv_b_w"], p["w_out"], name=f"mix_bwd_{tag}", dep=ffn_sent(dx1_b))
    g_out = _matmul_tn(saved["ycat"][None], dx1_b, name=f"dw_out_{tag}")
    g_in = _matmul_tn(du[None], saved["h1"], name=f"dw_in_{tag}")
    conv = dict(conv_a_w=dwa, conv_b_w=dwb, conv_f_w=jnp.concatenate([dwf_g, dwf_v], axis=1))
    dep_mix = mix_grads(dict(w_in=g_in, w_out=g_out), conv, dx1_b)
    dx0, dx0_b, dg1 = _matmul_rmsbwd(du[None], p["w_in_t"], saved["x0"], _row(p["mix_norm_g"]), dx1,
                                     name=f"dh_mix_{tag}", dep=dep_mix)
    rep = dict(mix_norm_g=dg1, b_in=dbin, conv_a_b=dba, ln_a_g=dlg, ln_a_b=dlb, ffn_norm_g=dg2)
    return dx0, dx0_b, rep, mix_sent(dx0_b)


def _place():
    return lax.axis_index("x"), lax.axis_index("y"), lax.axis_index("c")


def _all_gather(arrs, *, name):
    n_a = len(arrs)

    def body(*refs):
        ins = refs[0:n_a]
        outs = refs[n_a:2 * n_a]
        send_sems, recv_sems, local_sems = refs[2 * n_a:]
        x, y, c = _place()
        sibling = (x, y, 1 - c)
        chips = [(1 - x, y), (x, 1 - y), (1 - x, 1 - y)]

        def slot(a, px, py, pc):
            return outs[a].at[4 * px + 2 * py + pc]

        def copy(a, k, block, to, src=None):
            return pltpu.make_async_remote_copy(
                src_ref=slot(a, *block) if src is None else src, dst_ref=slot(a, *block),
                send_sem=send_sems.at[a, k], recv_sem=recv_sems.at[a, k],
                device_id=to, device_id_type=MESH)

        me = (x, y, c)
        mine = [pltpu.make_async_copy(ins[a], slot(a, *me), local_sems.at[a]) for a in range(n_a)]
        for cp in mine:
            cp.start()
        started = []
        for a in range(n_a):
            first = [copy(a, 0, me, sibling, src=ins[a])]
            first += [copy(a, 1 + j, me, (*chip, c), src=ins[a]) for j, chip in enumerate(chips)]
            for cp in first:
                cp.start()
            started += first
        for a in range(n_a):
            for j, chip in enumerate(chips):
                copy(a, 1 + j, (*chip, c), me).wait_recv()
                passed = copy(a, 4 + j, (*chip, c), sibling)
                passed.start()
                started.append(passed)
        for a in range(n_a):
            copy(a, 0, sibling, me).wait_recv()
            for j, chip in enumerate(chips):
                copy(a, 4 + j, (*chip, 1 - c), me).wait_recv()
        for cp in started:
            cp.wait_send()
        for cp in mine:
            cp.wait()

    return pl.pallas_call(
        body, in_specs=[ANY] * n_a, out_specs=[ANY] * n_a,
        out_shape=[jax.ShapeDtypeStruct((N_DEV, *a.shape), a.dtype) for a in arrs],
        scratch_shapes=[pltpu.SemaphoreType.DMA((n_a, 7)), pltpu.SemaphoreType.DMA((n_a, 7)),
                        pltpu.SemaphoreType.DMA((n_a,))],
        name=name,
    )(*arrs)


def _row_tile(r, cap):
    for tr in range(min(cap, r) // 16 * 16, 0, -16):
        if r % tr == 0:
            return tr
    return r


def _pair_sum(mines, theirs, where, *, name):
    n_a = len(mines)
    n_chip = mines[0].shape[0]

    def body(where_ref, *refs):
        a_refs = refs[0:n_a]
        b_refs = refs[n_a:2 * n_a]
        p_refs = refs[2 * n_a:3 * n_a]
        l_refs = refs[3 * n_a:4 * n_a]
        q = pl.program_id(0)
        for a in range(n_a):
            p_refs[a][...] = (a_refs[a][...].astype(F32) + b_refs[a][...].astype(F32)).astype(p_refs[a].dtype)

        @pl.when(q == where_ref[1])
        def _():
            for a in range(n_a):
                l_refs[a][...] = p_refs[a][...]

    in_specs, out_p, out_l, shapes = [], [], [], []
    for m in mines:
        _, _, r, c = m.shape
        in_specs.append(pl.BlockSpec((None, None, r, c), lambda q, where_ref: (q, where_ref[0], 0, 0)))
    for m in mines:
        _, _, r, c = m.shape
        in_specs.append(pl.BlockSpec((None, r, c), lambda q, where_ref: (q, 0, 0)))
        out_p.append(pl.BlockSpec((None, r, c), lambda q, where_ref: (q, 0, 0)))
        out_l.append(pl.BlockSpec((None, r, c), lambda q, where_ref: (where_ref[1], 0, 0)))
        shapes.append(jax.ShapeDtypeStruct((n_chip, r, c), m.dtype))
    res = pl.pallas_call(
        body,
        grid_spec=pltpu.PrefetchScalarGridSpec(num_scalar_prefetch=1, grid=(n_chip,), in_specs=in_specs,
                                               out_specs=out_p + out_l),
        out_shape=shapes + shapes,
        compiler_params=_params("arbitrary"), name=name,
    )(where, *mines, *theirs)
    return list(res[:n_a]), list(res[n_a:])


HBM = pl.BlockSpec(memory_space=pltpu.HBM)
SEM = pl.BlockSpec(memory_space=pltpu.SEMAPHORE)
EFFECT = pltpu.SideEffectType.DATAFLOW_SIDE_EFFECTING


def _in_hbm(a):
    return pltpu.with_memory_space_constraint(a, pltpu.HBM)


def _split_start(srcs, lands, plan, n_copies, after, *, name):
    n_s, n_l = len(srcs), len(lands)

    def body(*refs):
        src_refs = refs[0:n_s]
        land_refs = refs[n_s:n_s + n_l]
        send_sems, recv_sems = refs[n_s + n_l + 1], refs[n_s + n_l + 2]
        token = refs[-1]
        for cp in plan(src_refs, land_refs, send_sems, recv_sems):
            cp.start()
        token[...] = jnp.zeros_like(token)

    thru = [pltpu.HBM(a.shape, a.dtype) for a in list(srcs) + list(lands)]
    res = pl.pallas_call(
        body, name=name,
        out_shape=(pltpu.SemaphoreType.DMA((n_copies,)), pltpu.SemaphoreType.DMA((n_copies,)), *thru,
                   jax.ShapeDtypeStruct((SUB, LANES), F32)),
        in_specs=[HBM] * (n_s + n_l) + [ANY],
        out_specs=(SEM, SEM, *([HBM] * (n_s + n_l)), pl.BlockSpec(memory_space=pltpu.VMEM)),
        input_output_aliases={i: 2 + i for i in range(n_s + n_l)},
        compiler_params=pltpu.CompilerParams(has_side_effects=EFFECT),
    )(*[_in_hbm(a) for a in srcs], *[_in_hbm(a) for a in lands], _in_hbm(after))
    return res[0], res[1], list(res[2:2 + n_s]), list(res[2 + n_s:2 + n_s + n_l]), res[-1]


def _split_wait(send_sems, recv_sems, srcs, lands, after, plan, *, name):
    n_s, n_l = len(srcs), len(lands)

    def body(*refs):
        src_refs = refs[0:n_s]
        land_refs = refs[n_s:n_s + n_l]
        send, recv = refs[n_s + n_l], refs[n_s + n_l + 1]
        for cp in plan(src_refs, land_refs, send, recv):
            cp.wait_send()
            cp.wait_recv()

    res = pl.pallas_call(
        body, name=name,
        out_shape=tuple(pltpu.HBM(a.shape, a.dtype) for a in list(srcs) + list(lands)),
        in_specs=[HBM] * (n_s + n_l) + [SEM, SEM, ANY],
        out_specs=tuple([HBM] * (n_s + n_l)),
        input_output_aliases={i: i for i in range(n_s + n_l)},
        compiler_params=pltpu.CompilerParams(has_side_effects=EFFECT),
    )(*srcs, *lands, send_sems, recv_sems, _in_hbm(after))
    return list(res[:n_s]), list(res[n_s:])


def _remote(src, dst, send_sems, recv_sems, k, to):
    return pltpu.make_async_remote_copy(src_ref=src, dst_ref=dst, send_sem=send_sems.at[k], recv_sem=recv_sems.at[k],
                                        device_id=to, device_id_type=MESH)


def _gather_plan_first(src_refs, land_refs, send_sems, recv_sems):
    x, y, c = _place()
    me = 4 * x + 2 * y + c
    peers = [(x, y, 1 - c), (1 - x, y, c), (x, 1 - y, c), (1 - x, 1 - y, c)]
    return [_remote(src, land.at[me], send_sems, recv_sems, 4 * a + k, to)
            for a, (src, land) in enumerate(zip(src_refs, land_refs)) for k, to in enumerate(peers)]


def _gather_plan_second(src_refs, land_refs, send_sems, recv_sems):
    x, y, c = _place()
    chips = [(1 - x, y), (x, 1 - y), (1 - x, 1 - y)]
    out = []
    for a, land in enumerate(land_refs):
        for j, (px, py) in enumerate(chips):
            slot = land.at[4 * px + 2 * py + c]
            out.append(_remote(slot, slot, send_sems, recv_sems, 3 * a + j, (x, y, 1 - c)))
    return out


def _siblings_plan(src_refs, land_refs, send_sems, recv_sems):
    x, y, c = _place()
    return [_remote(src.at[:, 1 - c], land, send_sems, recv_sems, a, (x, y, 1 - c))
            for a, (src, land) in enumerate(zip(src_refs, land_refs))]


def _chips_plan(src_refs, land_refs, send_sems, recv_sems):
    x, y, c = _place()
    my_chip = 2 * x + y
    chips = [(1 - x, y), (x, 1 - y), (1 - x, 1 - y)]
    return [_remote(src.at[2 * px + py], land.at[my_chip], send_sems, recv_sems, 3 * a + j, (px, py, c))
            for a, (src, land) in enumerate(zip(src_refs, land_refs)) for j, (px, py) in enumerate(chips)]


def _gather_landings(shards, me, *, name):
    blank = _unwritten([jax.ShapeDtypeStruct((N_DEV, *s.shape), s.dtype) for s in shards], name=name)
    return [lax.dynamic_update_index_in_dim(b, s, me, 0) for b, s in zip(blank, shards)]


def _adamw_math(g, w, m, v):
    m = ADAM_B1 * m + (1.0 - ADAM_B1) * g
    v = ADAM_B2 * v + (1.0 - ADAM_B2) * (g * g)
    m_hat = m / (1.0 - ADAM_B1 ** ADAM_STEP)
    v_hat = v / (1.0 - ADAM_B2 ** ADAM_STEP)
    delta = -ADAM_LR * (m_hat / (jnp.sqrt(v_hat) + ADAM_EPS) + ADAM_WD * w)
    return delta, m, v


def _adamw_sharded(parts, w, m, v, *, name, dep=None):
    n_layers, r, c = w.shape
    n_chip = parts[0].shape[0]
    tr = _row_tile(r, 384)
    n_i = r // tr

    def body(*refs):
        p_refs = refs[0:n_layers]
        w_ref, m_ref, v_ref, _, g_out, d_out, m_out, v_out = refs[n_layers:]
        layer = pl.program_id(0)
        for l in range(n_layers):
            @pl.when(layer == l)
            def _(l=l):
                g = p_refs[l][0].astype(F32)
                for q in range(1, n_chip):
                    g = g + p_refs[l][q].astype(F32)
                delta, m_new, v_new = _adamw_math(g, w_ref[...], m_ref[...], v_ref[...])
                g_out[...] = g
                d_out[...] = delta
                m_out[...] = m_new
                v_out[...] = v_new

    def part_map(l):
        return lambda layer, i: (0, jnp.where(layer == l, i, jnp.where(layer < l, 0, n_i - 1)), 0)

    blk = pl.BlockSpec((None, tr, c), lambda layer, i: (layer, i, 0))
    return pl.pallas_call(
        body, grid=(n_layers, n_i),
        in_specs=[pl.BlockSpec((n_chip, tr, c), part_map(l)) for l in range(n_layers)] + [blk, blk, blk, ANY],
        out_specs=[blk] * 4, out_shape=[jax.ShapeDtypeStruct((n_layers, r, c), F32)] * 4,
        compiler_params=_params("arbitrary", "arbitrary"), name=name,
    )(*parts, w, m, v, w if dep is None else dep)


def _fold_partials(cols, *, name):
    widths = [c.shape[1] for c in cols]

    def body(*refs):
        o_ref = refs[-1]
        pos = 0
        for ref, width in zip(refs[:-1], widths):
            o_ref[:, pos:pos + width] = jnp.sum(ref[...], axis=0, keepdims=True)
            pos += width

    return pl.pallas_call(body, out_shape=jax.ShapeDtypeStruct((1, sum(widths)), F32), name=name)(*cols)


def _adamw_replicated(parts, names, w, m, v, n_loss, *, name):
    n_dev = parts.shape[0]
    n_layers = w[names[0]].shape[0]
    every = list(names) + ["final_norm_g"]
    n_p = len(every)

    def body(*refs):
        p_ref = refs[0]
        w_refs = dict(zip(every, refs[1:1 + n_p]))
        m_refs = dict(zip(every, refs[1 + n_p:1 + 2 * n_p]))
        v_refs = dict(zip(every, refs[1 + 2 * n_p:1 + 3 * n_p]))
        l_out = refs[1 + 3 * n_p]
        outs = refs[2 + 3 * n_p:]
        o_refs = {n: outs[4 * q:4 * q + 4] for q, n in enumerate(every)}
        acc = p_ref[0]
        for q in range(1, n_dev):
            acc = acc + p_ref[q]
        tot = jnp.sum(acc, axis=0, keepdims=True)
        pos = 0
        where = [(n, l) for l in range(n_layers) for n in names] + [("final_norm_g", 0)]
        for n, l in where:
            width = w_refs[n].shape[1]
            g = tot[:, pos:pos + width]
            pos += width
            row = pl.ds(l, 1)
            delta, m_new, v_new = _adamw_math(g, w_refs[n][row, :], m_refs[n][row, :], v_refs[n][row, :])
            for o, val in zip(o_refs[n], (g, delta, m_new, v_new)):
                o[row, :] = val
        l_out[...] = (0.5 / n_loss) * jnp.sum(tot[:, pos:pos + n_loss], axis=-1, keepdims=True)

    shapes = [jax.ShapeDtypeStruct((1, 1), F32)]
    for n in every:
        shapes += [jax.ShapeDtypeStruct(w[n].shape, F32)] * 4
    res = pl.pallas_call(
        body, out_shape=shapes,
        compiler_params=pltpu.CompilerParams(vmem_limit_bytes=VMEM_LIMIT), name=name,
    )(parts, *[w[n] for n in every], *[m[n] for n in every], *[v[n] for n in every])
    return res[0], {n: res[1 + 4 * q:5 + 4 * q] for q, n in enumerate(every)}


BIG = ("w_in", "w_out", "w_up", "w_down")
COL_SHARDED = ("w_in", "w_up")
CONV = ("conv_a_w", "conv_b_w", "conv_f_w")
REPLICATED = ("mix_norm_g", "b_in", "conv_a_b", "ln_a_g", "ln_a_b", "ffn_norm_g")
KINDS = ("grad", "delta", "m", "v")
FFN_PART = ("w_up", "w_down")
MIX_PART = ("w_in", "w_out")


def _weights_from_gathered(g):
    n_dev, r, c = g.shape
    return g.reshape(n_dev * r, c)


def _slabs_from_full(grad):
    return grad.reshape(N_DEV, grad.shape[0] // N_DEV, grad.shape[1])


def _unwritten(like, *, name):
    return pl.pallas_call(lambda *refs: None, out_specs=[ANY] * len(like), out_shape=list(like), name=name)()


def kernel(x, mix_norm_g, w_in, b_in, conv_a_w, conv_a_b, ln_a_g, ln_a_b, conv_b_w, w_out, ffn_norm_g, w_up, conv_f_w, w_down, final_norm_g, loss_target, m_mix_norm_g, m_w_in, m_b_in, m_conv_a_w, m_conv_a_b, m_ln_a_g, m_ln_a_b, m_conv_b_w, m_w_out, m_ffn_norm_g, m_w_up, m_conv_f_w, m_w_down, m_final_norm_g, v_mix_norm_g, v_w_in, v_b_in, v_conv_a_w, v_conv_a_b, v_ln_a_g, v_ln_a_b, v_conv_b_w, v_w_out, v_ffn_norm_g, v_w_up, v_conv_f_w, v_w_down, v_final_norm_g):
    w = dict(mix_norm_g=mix_norm_g, w_in=w_in, b_in=b_in, conv_a_w=conv_a_w, conv_a_b=conv_a_b, ln_a_g=ln_a_g,
             ln_a_b=ln_a_b, conv_b_w=conv_b_w, w_out=w_out, ffn_norm_g=ffn_norm_g, w_up=w_up, conv_f_w=conv_f_w,
             w_down=w_down, final_norm_g=final_norm_g)
    m = dict(mix_norm_g=m_mix_norm_g, w_in=m_w_in, b_in=m_b_in, conv_a_w=m_conv_a_w, conv_a_b=m_conv_a_b,
             ln_a_g=m_ln_a_g, ln_a_b=m_ln_a_b, conv_b_w=m_conv_b_w, w_out=m_w_out, ffn_norm_g=m_ffn_norm_g,
             w_up=m_w_up, conv_f_w=m_conv_f_w, w_down=m_w_down, final_norm_g=m_final_norm_g)
    v = dict(mix_norm_g=v_mix_norm_g, w_in=v_w_in, b_in=v_b_in, conv_a_w=v_conv_a_w, conv_a_b=v_conv_a_b,
             ln_a_g=v_ln_a_g, ln_a_b=v_ln_a_b, conv_b_w=v_conv_b_w, w_out=v_w_out, ffn_norm_g=v_ffn_norm_g,
             w_up=v_w_up, conv_f_w=v_conv_f_w, w_down=v_w_down, final_norm_g=v_final_norm_g)
    order = list(w)
    n_layers = w_in.shape[0]
    xs = x[0]
    target = loss_target[0]
    flip = lambda a: jnp.transpose(a, (0, 2, 1))
    wt, mt, vt = ({n: flip(d[n]) if n in COL_SHARDED else d[n] for n in BIG} for d in (w, m, v))
    px, py, pc = _place()
    where = jnp.stack([pc, 2 * px + py]).astype(jnp.int32)
    me = 4 * px + 2 * py + pc

    assert BIG == MIX_PART + FFN_PART
    key = lambda n: n + "_t" if n in COL_SHARDED else n
    shard = lambda n, l: wt[n][l].astype(BF16)

    def gather_start(names, l, after, tag):
        shards = [shard(n, l) for n in names]
        lands = _gather_landings(shards, me, name=f"gather_landing_{tag}")
        return _split_start(shards, lands, _gather_plan_first, 4 * len(shards), after, name=f"gather_first_start_{tag}")

    def gather_mid(first, after, tag):
        return _split_wait(first[0], first[1], first[2], first[3], after, _gather_plan_first,
                           name=f"gather_first_wait_{tag}")[1]

    def forward_start(lands, after, tag):
        return _split_start([], lands, _gather_plan_second, 3 * len(lands), after, name=f"gather_second_start_{tag}")

    def forward_finish(second, after, tag):
        return _split_wait(second[0], second[1], [], second[3], after, _gather_plan_second,
                           name=f"gather_second_wait_{tag}")[1]

    gathered = _all_gather([shard(n, 0) for n in MIX_PART] + [w[n] for n in CONV], name="gather_weights_0")
    params = [{n: w[n][l] for n in REPLICATED} for l in range(n_layers)]
    for n, g in zip(CONV, gathered[len(MIX_PART):]):
        n_dev, _, taps, c = g.shape
        full = g.transpose(1, 2, 0, 3).reshape(n_layers, taps, n_dev * c)
        for l in range(n_layers):
            params[l][n] = full[l]
    for n, g in zip(MIX_PART, gathered):
        params[0][key(n)] = _weights_from_gathered(g)
    ffn_first = gather_start(FFN_PART, 0, gathered[0], "0_ffn")
    pending = {}

    h = xs
    saved = []
    for l in range(n_layers):
        nxt = l + 1 if l + 1 < n_layers else None

        def before_up(x1, l=l, nxt=nxt):
            if l == 0:
                second = forward_start(gather_mid(ffn_first, x1, "0_ffn"), x1, "0_ffn")
                after = second[4]
            else:
                second = pending[l]["ffn"]
                after = x1
            if nxt is not None:
                pending[nxt] = dict(first=gather_start(BIG, nxt, after, str(nxt)))
                after = pending[nxt]["first"][4]
            for n, g in zip(FFN_PART, forward_finish(second, after, f"{l}_ffn")):
                params[l][key(n)] = _weights_from_gathered(g)

        h, keep = _layer_fwd(h, params[l], str(l), dep=ffn_first[4] if l == 0 else None, before_up=before_up)
        saved.append(keep)
        if nxt is not None:
            arrived = gather_mid(pending[nxt]["first"], h, str(nxt))
            mix_second = forward_start(arrived[:len(MIX_PART)], h, f"{nxt}_mix")
            pending[nxt]["ffn"] = forward_start(arrived[len(MIX_PART):], mix_second[4], f"{nxt}_ffn")
            for n, g in zip(MIX_PART, forward_finish(mix_second, pending[nxt]["ffn"][4], f"{nxt}_mix")):
                params[nxt][key(n)] = _weights_from_gathered(g)

    def start_siblings(slabs, after, tag):
        mines = [s.reshape(N_CHIP, 2, *s.shape[1:]) for s in slabs]
        lands = _unwritten([jax.ShapeDtypeStruct((N_CHIP, *m.shape[2:]), m.dtype) for m in mines],
                           name=f"reduce_siblings_landing_{tag}")
        return _split_start(mines, lands, _siblings_plan, len(mines), after, name=f"reduce_siblings_start_{tag}")

    def start_chips(sib, after, tag):
        mines, theirs = _split_wait(sib[0], sib[1], sib[2], sib[3], after, _siblings_plan,
                                    name=f"reduce_siblings_wait_{tag}")
        pairs, lands = _pair_sum(mines, theirs, where, name=f"pair_sum_{tag}")
        return _split_start(pairs, lands, _chips_plan, 3 * len(pairs), after, name=f"reduce_chips_start_{tag}")

    def finish_reduce(fly, after, tag):
        return _split_wait(fly[0], fly[1], fly[2], fly[3], after, _chips_plan, name=f"reduce_chips_wait_{tag}")[1]

    loss_sq, dh, dh_b, dgf = _loss_bwd(h, _row(final_norm_g), target, name="loss")
    conv_g = {n: [None] * n_layers for n in CONV}
    rep_g = [None] * n_layers
    siblings = {}
    flights = {}
    token = None
    for l in reversed(range(n_layers)):
        def ffn_grads(g, after, l=l):
            siblings[l, "ffn"] = start_siblings([_slabs_from_full(g[n]) for n in FFN_PART], after, f"{l}_ffn")
            return siblings[l, "ffn"][4]

        def ffn_sent(after, l=l):
            flights[l, "ffn"] = start_chips(siblings[l, "ffn"], after, f"{l}_ffn")
            return flights[l, "ffn"][4]

        def mix_grads(g, conv, after, l=l):
            for n in CONV:
                conv_g[n][l] = conv[n]
            slabs = [_slabs_from_full(g[n]) for n in MIX_PART]
            if l == 0:
                for n in CONV:
                    full = jnp.stack(conv_g[n])
                    _, taps, c = full.shape
                    slabs.append(full.reshape(n_layers, taps, N_DEV, c // N_DEV).transpose(2, 0, 1, 3)
                                 .reshape(N_DEV, n_layers * taps, c // N_DEV))
            siblings[l, "mix"] = start_siblings(slabs, after, f"{l}_mix")
            return siblings[l, "mix"][4]

        def mix_sent(after, l=l):
            flights[l, "mix"] = start_chips(siblings[l, "mix"], after, f"{l}_mix")
            return flights[l, "mix"][4]

        dh, dh_b, rep_g[l], token = _layer_bwd(dh, dh_b, params[l], saved[l], str(l), ffn_grads, ffn_sent,
                                               mix_grads, mix_sent, dep=token)

    sums = {key: finish_reduce(fly, dh, f"{key[0]}_{key[1]}") for key, fly in flights.items() if key != (0, "mix")}
    out = {k: {} for k in KINDS}

    def adamw_big(names, part, dep):
        for q, n in enumerate(names):
            layer_parts = [sums[l, part][q] for l in range(n_layers)]
            res = _adamw_sharded(layer_parts, wt[n], mt[n], vt[n], name=f"adamw_{n}", dep=dep)
            for k, r in zip(KINDS, res):
                out[k][n] = flip(r) if n in COL_SHARDED else r

    adamw_big(FFN_PART, "ffn", token)

    rep_cols = [rep_g[l][n] for l in range(n_layers) for n in REPLICATED] + [dgf, loss_sq]
    rep_all = _all_gather([_fold_partials(rep_cols, name="fold_small")], name="gather_small")[0]
    with_final = lambda d: {**{n: d[n] for n in REPLICATED}, "final_norm_g": _row(d["final_norm_g"])}
    loss, rep_res = _adamw_replicated(rep_all, REPLICATED, with_final(w), with_final(m), with_final(v),
                                      loss_sq.shape[1], name="adamw_small")
    for n, res in rep_res.items():
        for k, r in zip(KINDS, res):
            out[k][n] = r.reshape(w[n].shape)

    last = finish_reduce(flights[0, "mix"], rep_res["b_in"][0], "0_mix")
    sums[0, "mix"] = last[:len(MIX_PART)]
    adamw_big(MIX_PART, "mix", None)
    for n, p in zip(CONV, last[len(MIX_PART):]):
        as_one = lambda a: a.reshape(1, *p.shape[1:])
        for k, r in zip(KINDS, _adamw_sharded([p], as_one(w[n]), as_one(m[n]), as_one(v[n]), name=f"adamw_{n}")):
            out[k][n] = r.reshape(w[n].shape)

    grad_x = dh.reshape(x.shape)
    return (loss.reshape(()), grad_x, *[out["grad"][n] for n in order], *[out["delta"][n] for n in order],
            *[out["m"][n] for n in order], *[out["v"][n] for n in order])
```

```python
import functools

import jax
import jax.numpy as jnp
from jax import lax
from jax.experimental import pallas as pl
from jax.experimental.pallas import tpu as pltpu

F32 = jnp.float32
BF16 = jnp.bfloat16

N_DEV = 8
N_CHIP = 4
D_CONF = 512
CONF_K = 31
SHORT_K = 3
EPS = 1e-6
HALO = 32
HALO3 = 8
HALO3_BLK = 16
LANES = 128
SUB = 8
VMEM_LIMIT = 56 * 1024 * 1024

ADAM_LR = 0.001
ADAM_B1 = 0.9
ADAM_B2 = 0.999
ADAM_EPS = 1e-08
ADAM_WD = 0.01
ADAM_STEP = 10

MESH = pl.DeviceIdType.MESH
ANY = pl.BlockSpec(memory_space=pl.ANY)


def _params(*sem):
    return pltpu.CompilerParams(dimension_semantics=sem, vmem_limit_bytes=VMEM_LIMIT)


def _resident(shape, index_map):
    return pl.BlockSpec(shape, index_map, pipeline_mode=pl.Buffered(1))


def _row_loop(n_rows, rb, fn, unroll=1):
    rb = min(rb, n_rows)

    def body(i, carry):
        fn(pl.ds(pl.multiple_of(i * rb, rb), rb))
        return carry
    lax.fori_loop(0, n_rows // rb, body, 0, unroll=unroll)


def _rows8(v):
    acc = v[0:SUB]
    for k in range(1, v.shape[0] // SUB):
        acc = acc + v[k * SUB:(k + 1) * SUB]
    return acc


def _sigmoid(z):
    return 0.5 * jnp.tanh(0.5 * z) + 0.5


def _dot(a, b):
    return jnp.dot(a, b, preferred_element_type=F32)


def _dot_nt(a, b):
    return lax.dot_general(a, b, (((1,), (1,)), ((), ())), preferred_element_type=F32)


def _dot_tn(a, b):
    return lax.dot_general(a, b, (((0,), (0,)), ((), ())), preferred_element_type=F32)


def _replicate_taps(w_ref, wrep, taps):
    for k in range(taps):
        wrep[pl.ds(k * SUB, SUB), :] = jnp.broadcast_to(w_ref[pl.ds(k, 1), :], (SUB, w_ref.shape[1]))


def _shift_copies(win, shf, lanes):
    span = win.shape[0] - SUB
    for r in range(1, SUB):
        for j0 in range(0, span, 64):
            n = min(64, span - j0)
            shf[r - 1, pl.ds(j0, n), lanes] = win[pl.ds(j0 + r, n), lanes]


def _rows_at(win, shf, off, rb, lanes):
    if shf is None or off % SUB == 0:
        return win[pl.ds(off, rb), lanes]
    return shf[off % SUB - 1, pl.ds(off - off % SUB, rb), lanes]


def _conv_taps(win, wrep, out, *, taps, n_rows, base, width, transposed=False, bias_ref=None, shf=None):
    rb = min(64, n_rows)

    def lane_body(cb, carry):
        lanes = pl.ds(pl.multiple_of(cb * LANES, LANES), LANES)
        if shf is not None:
            _shift_copies(win, shf, lanes)
        for r0 in range(0, n_rows, rb):
            acc = None
            for k in range(taps):
                off = (taps - 1 - k) if transposed else (k - (taps - 1))
                wk = jnp.tile(wrep[pl.ds(k * SUB, SUB), lanes], (rb // SUB, 1))
                term = wk * _rows_at(win, shf, base + r0 + off, rb, lanes)
                acc = term if acc is None else acc + term
            if bias_ref is not None:
                acc = acc + bias_ref[:, lanes]
            out[pl.ds(r0, rb), lanes] = acc.astype(out.dtype)
        return carry

    lax.fori_loop(0, width // LANES, lane_body, 0)


def _conv_bwd_taps(win, wrep, x_cur, dx_out, dw_acc, *, taps, n_rows, width, shf=None):
    rb = min(32 if taps > 8 else 64, n_rows)

    def lane_body(cb, carry):
        lanes = pl.ds(pl.multiple_of(cb * LANES, LANES), LANES)
        if shf is not None:
            _shift_copies(win, shf, lanes)
        sums = [None] * taps
        for r0 in range(0, n_rows, rb):
            xv = x_cur[pl.ds(r0, rb), lanes].astype(F32)
            acc = None
            for k in range(taps):
                shifted = _rows_at(win, shf, r0 + taps - 1 - k, rb, lanes)
                term = jnp.tile(wrep[pl.ds(k * SUB, SUB), lanes], (rb // SUB, 1)) * shifted
                acc = term if acc is None else acc + term
                part = _rows8(xv * shifted)
                sums[k] = part if sums[k] is None else sums[k] + part
            dx_out[pl.ds(r0, rb), lanes] = acc.astype(dx_out.dtype)
        for k in range(taps):
            dw_acc[pl.ds(k * SUB, SUB), lanes] += sums[k]
        return carry

    lax.fori_loop(0, width // LANES, lane_body, 0)


def _fold8(acc_ref, taps):
    return jnp.concatenate(
        [jnp.sum(acc_ref[pl.ds(k * SUB, SUB), :], axis=0, keepdims=True) for k in range(taps)], axis=0)


def _seq_tile(s_len):
    return min(512, s_len)


def _mm_tile(s_len):
    return min(512, s_len)


def _ff_chunk(ff):
    best = LANES
    for c in range(LANES, 1408 + 1, LANES):
        if ff % c == 0:
            best = c
    return best


def _col_tile(n):
    for c in (512, 1408, 256, LANES):
        if n % c == 0:
            return c
    return n


def _rms_matmul(x, g, wt, b, *, name, dep=None):
    s_len, d = x.shape
    n = wt.shape[0]
    tm = _mm_tile(s_len)
    cn = _col_tile(n)
    has_bias = b is not None

    def body(*refs):
        x_ref, g_ref, w_ref = refs[0:3]
        b_ref = refs[3] if has_bias else None
        o_ref, h_ref = refs[-2:]

        def blk(rows):
            xv = x_ref[rows, :]
            r = lax.rsqrt(jnp.mean(xv * xv, axis=-1, keepdims=True) + EPS)
            h_ref[rows, :] = ((xv * r) * g_ref[...]).astype(BF16)

        rb = min(128, tm)
        for r0 in range(0, tm, rb):
            blk(pl.ds(r0, rb))
        for j in range(n // cn):
            acc = _dot_nt(h_ref[...], w_ref[j * cn:(j + 1) * cn, :])
            if has_bias:
                acc = acc + b_ref[:, j * cn:(j + 1) * cn]
            o_ref[:, j * cn:(j + 1) * cn] = acc.astype(BF16)

    in_specs = [pl.BlockSpec((tm, d), lambda i: (i, 0)), _resident((1, d), lambda i: (0, 0)),
                _resident((n, d), lambda i: (0, 0))]
    args = [x, g, wt]
    if has_bias:
        in_specs.append(_resident((1, n), lambda i: (0, 0)))
        args.append(b)
    in_specs.append(ANY)
    args.append(x if dep is None else dep)
    return pl.pallas_call(
        body, grid=(s_len // tm,), in_specs=in_specs,
        out_specs=[pl.BlockSpec((tm, n), lambda i: (i, 0)), pl.BlockSpec((tm, d), lambda i: (i, 0))],
        out_shape=[jax.ShapeDtypeStruct((s_len, n), BF16), jax.ShapeDtypeStruct((s_len, d), BF16)],
        compiler_params=_params("parallel"), name=name,
    )(*args)


def _mix_windows(u_ref, uh_ref, gw, pw, first, t):
    c = D_CONF
    uh = uh_ref[...].astype(F32)
    gw[0:HALO, :] = jnp.where(first, 0.0, uh[:, 0:c] * _sigmoid(uh[:, c:2 * c]))
    pw[0:HALO3, :] = jnp.where(first, 0.0, uh[HALO - HALO3:HALO, 3 * c:4 * c] * uh[HALO - HALO3:HALO, 4 * c:5 * c])

    def blk(rows):
        dst = pl.ds(pl.multiple_of(rows.start + HALO, SUB), rows.size)
        gw[dst, :] = u_ref[rows, 0:c].astype(F32) * _sigmoid(u_ref[rows, c:2 * c].astype(F32))
        dst3 = pl.ds(pl.multiple_of(rows.start + HALO3, SUB), rows.size)
        pw[dst3, :] = u_ref[rows, 3 * c:4 * c].astype(F32) * u_ref[rows, 4 * c:5 * c].astype(F32)
    _row_loop(t, 64, blk)


def _mix_fwd(u, x0, wa, ba, lg, lb, wb, w_out, *, name):
    s_len, d_in = u.shape
    d = x0.shape[1]
    c = D_CONF
    t = _seq_tile(s_len)
    per = t // HALO

    def body(u_ref, uh_ref, x0_ref, wa_ref, ba_ref, lg_ref, lb_ref, wb_ref, wo_ref, y_ref, x1_ref, ca, cb,
             gw, pw, wrep_a, wrep_b, shf):
        first = pl.program_id(0) == 0
        _mix_windows(u_ref, uh_ref, gw, pw, first, t)
        _replicate_taps(wa_ref, wrep_a, CONF_K)
        _replicate_taps(wb_ref, wrep_b, SHORT_K)
        _conv_taps(gw, wrep_a, ca, taps=CONF_K, n_rows=t, base=HALO, width=c, bias_ref=ba_ref, shf=shf)
        _conv_taps(pw, wrep_b, cb, taps=SHORT_K, n_rows=t, base=HALO3, width=c)

        def blk(rows):
            cv = ca[rows, :]
            mu = jnp.mean(cv, axis=-1, keepdims=True)
            xc = cv - mu
            var = jnp.mean(xc * xc, axis=-1, keepdims=True)
            ln = (xc * lax.rsqrt(var + EPS)) * lg_ref[...] + lb_ref[...]
            y_ref[rows, 0:c] = (ln * _sigmoid(ln)).astype(BF16)
            y_ref[rows, c:2 * c] = (u_ref[rows, 2 * c:3 * c].astype(F32) * cb[rows, :]).astype(BF16)
        _row_loop(t, 64, blk)
        x1_ref[...] = x0_ref[...] + _dot(y_ref[...], wo_ref[...])

    small = lambda r: _resident((r, c), lambda i: (0, 0))
    return pl.pallas_call(
        body, grid=(s_len // t,),
        in_specs=[pl.BlockSpec((t, d_in), lambda i: (i, 0)),
                  pl.BlockSpec((HALO, d_in), lambda i: (jnp.maximum(i * per - 1, 0), 0)),
                  pl.BlockSpec((t, d), lambda i: (i, 0)),
                  small(CONF_K), small(1), small(1), small(1), small(SHORT_K),
                  _resident((2 * c, d), lambda i: (0, 0))],
        out_specs=[pl.BlockSpec((t, 2 * c), lambda i: (i, 0)), pl.BlockSpec((t, d), lambda i: (i, 0)),
                   pl.BlockSpec((t, c), lambda i: (i, 0)), pl.BlockSpec((t, c), lambda i: (i, 0))],
        out_shape=[jax.ShapeDtypeStruct((s_len, 2 * c), BF16), jax.ShapeDtypeStruct((s_len, d), F32),
                   jax.ShapeDtypeStruct((s_len, c), F32), jax.ShapeDtypeStruct((s_len, c), F32)],
        scratch_shapes=[pltpu.VMEM((HALO + t, c), F32), pltpu.VMEM((HALO3 + t, c), F32),
                        pltpu.VMEM((CONF_K * SUB, c), F32), pltpu.VMEM((SHORT_K * SUB, c), F32),
                        pltpu.VMEM((SUB - 1, HALO + t, c), F32)],
        compiler_params=_params("arbitrary"), name=name,
    )(u, u, x0, wa, ba, lg, lb, wb, w_out)


def _ffn_fwd(uf, x1, wf, w_down, *, name):
    s_len, ff2 = uf.shape
    ff = ff2 // 2
    d = x1.shape[1]
    t = _seq_tile(s_len)
    fc = _ff_chunk(ff)
    nc = ff // fc
    per = t // HALO3_BLK
    half = t // 2
    rb = min(64, half)

    def body(ug_ref, ugh_ref, uv_ref, uvh_ref, x1_ref, wfg_ref, wfv_ref, wd_ref,
             act_ref, x2_ref, cg_ref, cv_ref, gwin, vwin, wrep_g, wrep_v):
        first = pl.program_id(0) == 0
        first_chunk = pl.program_id(1) == 0
        lo8 = HALO3_BLK - HALO3
        gwin[0:HALO3, :] = jnp.where(first, 0.0, ugh_ref[...].astype(F32)[lo8:HALO3_BLK])
        vwin[0:HALO3, :] = jnp.where(first, 0.0, uvh_ref[...].astype(F32)[lo8:HALO3_BLK])
        _replicate_taps(wfg_ref, wrep_g, SHORT_K)
        _replicate_taps(wfv_ref, wrep_v, SHORT_K)

        def conv(win, wrep, r0, lanes):
            acc = None
            for k in range(SHORT_K):
                wk = jnp.tile(wrep[k * SUB:(k + 1) * SUB, lanes], (rb // SUB, 1))
                off = HALO3 + r0 + k - (SHORT_K - 1)
                term = wk * win[off:off + rb, lanes]
                acc = term if acc is None else acc + term
            return acc

        for lo in range(0, t, half):
            for r0 in range(lo, lo + half, rb):
                gwin[HALO3 + r0:HALO3 + r0 + rb, :] = ug_ref[r0:r0 + rb, :].astype(F32)
                vwin[HALO3 + r0:HALO3 + r0 + rb, :] = uv_ref[r0:r0 + rb, :].astype(F32)
            for cb in range(fc // LANES):
                lanes = slice(cb * LANES, (cb + 1) * LANES)
                for r0 in range(lo, lo + half, rb):
                    gv = conv(gwin, wrep_g, r0, lanes)
                    vv = conv(vwin, wrep_v, r0, lanes)
                    cg_ref[r0:r0 + rb, lanes] = gv.astype(BF16)
                    cv_ref[r0:r0 + rb, lanes] = vv.astype(BF16)
                    act_ref[r0:r0 + rb, lanes] = ((gv * _sigmoid(gv)) * vv).astype(BF16)
            base = jnp.where(first_chunk, x1_ref[lo:lo + half, :], x2_ref[lo:lo + half, :])
            x2_ref[lo:lo + half, :] = base + _dot(act_ref[lo:lo + half, :], wd_ref[...])

    halo_map = lambda off: (lambda i, j: (jnp.maximum(i * per - 1, 0), j + off))
    return pl.pallas_call(
        body, grid=(s_len // t, nc),
        in_specs=[pl.BlockSpec((t, fc), lambda i, j: (i, j)), pl.BlockSpec((HALO3_BLK, fc), halo_map(0)),
                  pl.BlockSpec((t, fc), lambda i, j: (i, j + nc)), pl.BlockSpec((HALO3_BLK, fc), halo_map(nc)),
                  pl.BlockSpec((t, d), lambda i, j: (i, 0)),
                  pl.BlockSpec((SHORT_K, fc), lambda i, j: (0, j)),
                  pl.BlockSpec((SHORT_K, fc), lambda i, j: (0, j + nc)),
                  pl.BlockSpec((fc, d), lambda i, j: (j, 0))],
        out_specs=[pl.BlockSpec((t, fc), lambda i, j: (i, j)), pl.BlockSpec((t, d), lambda i, j: (i, 0)),
                   pl.BlockSpec((t, fc), lambda i, j: (i, j)), pl.BlockSpec((t, fc), lambda i, j: (i, j))],
        out_shape=[jax.ShapeDtypeStruct((s_len, ff), BF16), jax.ShapeDtypeStruct((s_len, d), F32),
                   jax.ShapeDtypeStruct((s_len, ff), BF16), jax.ShapeDtypeStruct((s_len, ff), BF16)],
        scratch_shapes=[pltpu.VMEM((HALO3 + t, fc), F32), pltpu.VMEM((HALO3 + t, fc), F32),
                        pltpu.VMEM((SHORT_K * SUB, fc), F32), pltpu.VMEM((SHORT_K * SUB, fc), F32)],
        compiler_params=_params("parallel", "arbitrary"), name=name,
    )(uf, uf, uf, uf, x1, wf, wf, w_down)


def _loss_bwd(x, g, target, *, name):
    s_len, d = x.shape
    t = _seq_tile(s_len)

    def body(x_ref, g_ref, t_ref, l_ref, dx_ref, dxb_ref, dg_ref):
        @pl.when(pl.program_id(0) == 0)
        def _():
            l_ref[...] = jnp.zeros_like(l_ref)
            dg_ref[...] = jnp.zeros_like(dg_ref)

        def blk(rows):
            xv = x_ref[rows, :]
            r = lax.rsqrt(jnp.mean(xv * xv, axis=-1, keepdims=True) + EPS)
            xn = xv * r
            e = xn * g_ref[...] - t_ref[rows, :]
            l_ref[...] += _rows8(e * e)
            dy = e * (1.0 / d)
            dg_ref[...] += _rows8(dy * xn)
            dn = dy * g_ref[...]
            dx = r * (dn - xn * jnp.mean(dn * xn, axis=-1, keepdims=True))
            dx_ref[rows, :] = dx
            dxb_ref[rows, :] = dx.astype(BF16)
        _row_loop(t, 64, blk)

    row = pl.BlockSpec((t, d), lambda i: (i, 0))
    part = pl.BlockSpec((SUB, d), lambda i: (0, 0))
    return pl.pallas_call(
        body, grid=(s_len // t,),
        in_specs=[row, _resident((1, d), lambda i: (0, 0)), row],
        out_specs=[part, row, row, part],
        out_shape=[jax.ShapeDtypeStruct((SUB, d), F32), jax.ShapeDtypeStruct((s_len, d), F32),
                   jax.ShapeDtypeStruct((s_len, d), BF16), jax.ShapeDtypeStruct((SUB, d), F32)],
        compiler_params=_params("arbitrary"), name=name,
    )(x, g, target)


def _ffn_bwd(dx2, uf, cg, cv, wf, w_down, *, name, dep=None):
    s_len, ff2 = uf.shape
    ff = ff2 // 2
    d = dx2.shape[1]
    t = _seq_tile(s_len)
    n_t = s_len // t
    fc = _ff_chunk(ff)
    nc = ff // fc

    def body(dx_ref, ug_ref, uv_ref, cg_ref, cv_ref, wfg_ref, wfv_ref, wd_ref, dep_ref,
             duf_ref, dwg_ref, dwv_ref, dact, dgw, dvw, awg, awv, wrep_g, wrep_v):
        i = pl.program_id(1)

        @pl.when(i == 0)
        def _():
            dgw[t:t + HALO3, :] = jnp.zeros((HALO3, fc), F32)
            dvw[t:t + HALO3, :] = jnp.zeros((HALO3, fc), F32)
            awg[...] = jnp.zeros_like(awg)
            awv[...] = jnp.zeros_like(awv)

        _replicate_taps(wfg_ref, wrep_g, SHORT_K)
        _replicate_taps(wfv_ref, wrep_v, SHORT_K)

        def blk(rows):
            gv = cg_ref[rows, :].astype(F32)
            sg = _sigmoid(gv)
            da = dact[rows, :]
            dgw[rows, :] = (da * cv_ref[rows, :].astype(F32)) * (sg * (1.0 + gv * (1.0 - sg)))
            dvw[rows, :] = da * (gv * sg)

        dact[...] = _dot_nt(dx_ref[...], wd_ref[...])
        _row_loop(t, 32, blk, unroll=2)

        _conv_bwd_taps(dgw, wrep_g, ug_ref, duf_ref.at[0], awg, taps=SHORT_K, n_rows=t, width=fc)
        _conv_bwd_taps(dvw, wrep_v, uv_ref, duf_ref.at[1], awv, taps=SHORT_K, n_rows=t, width=fc)
        dgw[t:t + HALO3, :] = dgw[0:HALO3, :]
        dvw[t:t + HALO3, :] = dvw[0:HALO3, :]

        @pl.when(i == n_t - 1)
        def _():
            dwg_ref[...] = _fold8(awg, SHORT_K)
            dwv_ref[...] = _fold8(awv, SHORT_K)

    rev = lambda i: n_t - 1 - i
    gate = pl.BlockSpec((t, fc), lambda j, i: (rev(i), j))
    value = pl.BlockSpec((t, fc), lambda j, i: (rev(i), j + nc))
    return pl.pallas_call(
        body, grid=(nc, n_t),
        in_specs=[pl.BlockSpec((t, d), lambda j, i: (rev(i), 0)), gate, value, gate, gate,
                  pl.BlockSpec((SHORT_K, fc), lambda j, i: (0, j)),
                  pl.BlockSpec((SHORT_K, fc), lambda j, i: (0, j + nc)),
                  pl.BlockSpec((fc, d), lambda j, i: (j, 0)), ANY],
        out_specs=[pl.BlockSpec((2, t, fc), lambda j, i: (0, rev(i), j)),
                   pl.BlockSpec((SHORT_K, fc), lambda j, i: (0, j)), pl.BlockSpec((SHORT_K, fc), lambda j, i: (0, j))],
        out_shape=[jax.ShapeDtypeStruct((2, s_len, ff), BF16),
                   jax.ShapeDtypeStruct((SHORT_K, ff), F32), jax.ShapeDtypeStruct((SHORT_K, ff), F32)],
        scratch_shapes=[pltpu.VMEM((t, fc), F32),
                        pltpu.VMEM((t + HALO3, fc), F32), pltpu.VMEM((t + HALO3, fc), F32),
                        pltpu.VMEM((SHORT_K * SUB, fc), F32), pltpu.VMEM((SHORT_K * SUB, fc), F32),
                        pltpu.VMEM((SHORT_K * SUB, fc), F32), pltpu.VMEM((SHORT_K * SUB, fc), F32)],
        compiler_params=_params("arbitrary", "arbitrary"), name=name,
    )(dx2, uf, uf, cg, cv, wf, wf, w_down, uf if dep is None else dep)


def _mix_bwd(dx1, u, ca, cb, wa, lg, lb, wb, w_out, *, name, dep=None):
    s_len, d_in = u.shape
    d = dx1.shape[1]
    c = D_CONF
    t = _seq_tile(s_len)
    n_t = s_len // t

    def body(dx_ref, u_ref, ca_ref, cb_ref, wa_ref, lg_ref, lb_ref, wb_ref, wo_ref, dep_ref,
             du_ref, dwa_ref, dwb_ref, dba_ref, dlg_ref, dlb_ref, dbin_ref,
             glu, prod, dyc, dcaw, dcbw, dglu, dp, awa, awb, wrep_a, wrep_b, shf):
        i = pl.program_id(0)
        dyc[...] = _dot_nt(dx_ref[...], wo_ref[...])
        _replicate_taps(wa_ref, wrep_a, CONF_K)
        _replicate_taps(wb_ref, wrep_b, SHORT_K)

        @pl.when(i == 0)
        def _():
            dcaw[t:t + HALO, :] = jnp.zeros((HALO, c), F32)
            dcbw[t:t + HALO3, :] = jnp.zeros((HALO3, c), F32)
            awa[...] = jnp.zeros_like(awa)
            awb[...] = jnp.zeros_like(awb)
            dba_ref[...] = jnp.zeros_like(dba_ref)
            dlg_ref[...] = jnp.zeros_like(dlg_ref)
            dlb_ref[...] = jnp.zeros_like(dlb_ref)
            dbin_ref[...] = jnp.zeros_like(dbin_ref)

        def blk1(rows):
            cv = ca_ref[rows, :]
            mu = jnp.mean(cv, axis=-1, keepdims=True)
            xc = cv - mu
            rstd = lax.rsqrt(jnp.mean(xc * xc, axis=-1, keepdims=True) + EPS)
            nrm = xc * rstd
            ln = nrm * lg_ref[...] + lb_ref[...]
            sg = _sigmoid(ln)
            dln = dyc[rows, 0:c] * (sg * (1.0 + ln * (1.0 - sg)))
            dlg_ref[...] += _rows8(dln * nrm)
            dlb_ref[...] += _rows8(dln)
            dn = dln * lg_ref[...]
            dca = rstd * (dn - jnp.mean(dn, axis=-1, keepdims=True)
                          - nrm * jnp.mean(dn * nrm, axis=-1, keepdims=True))
            dcaw[rows, :] = dca
            dba_ref[...] += _rows8(dca)
            ds = dyc[rows, c:2 * c]
            dgb = ds * cb_ref[rows, :]
            dcbw[rows, :] = ds * u_ref[rows, 2 * c:3 * c].astype(F32)
            du_ref[rows, 2 * c:3 * c] = dgb.astype(BF16)
            dbin_ref[:, 2 * c:3 * c] += _rows8(dgb)
            glu[rows, :] = u_ref[rows, 0:c].astype(F32) * _sigmoid(u_ref[rows, c:2 * c].astype(F32))
            prod[rows, :] = u_ref[rows, 3 * c:4 * c].astype(F32) * u_ref[rows, 4 * c:5 * c].astype(F32)
        _row_loop(t, 64, blk1, unroll=2)

        _conv_bwd_taps(dcaw, wrep_a, glu, dglu, awa, taps=CONF_K, n_rows=t, width=c, shf=shf)
        _conv_bwd_taps(dcbw, wrep_b, prod, dp, awb, taps=SHORT_K, n_rows=t, width=c)
        dcaw[t:t + HALO, :] = dcaw[0:HALO, :]
        dcbw[t:t + HALO3, :] = dcbw[0:HALO3, :]

        def blk2(rows):
            av = u_ref[rows, 0:c].astype(F32)
            sg = _sigmoid(u_ref[rows, c:2 * c].astype(F32))
            dg = dglu[rows, :]
            d_av = dg * sg
            d_ag = (dg * av) * (sg * (1.0 - sg))
            dpv = dp[rows, :]
            d_gc = dpv * u_ref[rows, 4 * c:5 * c].astype(F32)
            d_vs = dpv * u_ref[rows, 3 * c:4 * c].astype(F32)
            du_ref[rows, 0:c] = d_av.astype(BF16)
            du_ref[rows, c:2 * c] = d_ag.astype(BF16)
            du_ref[rows, 3 * c:4 * c] = d_gc.astype(BF16)
            du_ref[rows, 4 * c:5 * c] = d_vs.astype(BF16)
            dbin_ref[:, 0:c] += _rows8(d_av)
            dbin_ref[:, c:2 * c] += _rows8(d_ag)
            dbin_ref[:, 3 * c:4 * c] += _rows8(d_gc)
            dbin_ref[:, 4 * c:5 * c] += _rows8(d_vs)
        _row_loop(t, 64, blk2)

        @pl.when(i == n_t - 1)
        def _():
            dwa_ref[...] = _fold8(awa, CONF_K)
            dwb_ref[...] = _fold8(awb, SHORT_K)

    rev = lambda i: n_t - 1 - i
    small_in = lambda r: _resident((r, c), lambda i: (0, 0))
    small = lambda r: pl.BlockSpec((r, c), lambda i: (0, 0))
    return pl.pallas_call(
        body, grid=(n_t,),
        in_specs=[pl.BlockSpec((t, d), lambda i: (rev(i), 0)),
                  pl.BlockSpec((t, d_in), lambda i: (rev(i), 0)),
                  pl.BlockSpec((t, c), lambda i: (rev(i), 0)), pl.BlockSpec((t, c), lambda i: (rev(i), 0)),
                  small_in(CONF_K), small_in(1), small_in(1), small_in(SHORT_K),
                  _resident((2 * c, d), lambda i: (0, 0)), ANY],
        out_specs=[pl.BlockSpec((t, d_in), lambda i: (rev(i), 0)),
                   small(CONF_K), small(SHORT_K), small(SUB), small(SUB), small(SUB),
                   pl.BlockSpec((SUB, d_in), lambda i: (0, 0))],
        out_shape=[jax.ShapeDtypeStruct((s_len, d_in), BF16),
                   jax.ShapeDtypeStruct((CONF_K, c), F32), jax.ShapeDtypeStruct((SHORT_K, c), F32),
                   jax.ShapeDtypeStruct((SUB, c), F32), jax.ShapeDtypeStruct((SUB, c), F32),
                   jax.ShapeDtypeStruct((SUB, c), F32), jax.ShapeDtypeStruct((SUB, d_in), F32)],
        scratch_shapes=[pltpu.VMEM((t, c), F32), pltpu.VMEM((t, c), F32), pltpu.VMEM((t, 2 * c), F32),
                        pltpu.VMEM((t + HALO, c), F32), pltpu.VMEM((t + HALO3, c), F32),
                        pltpu.VMEM((t, c), F32), pltpu.VMEM((t, c), F32),
                        pltpu.VMEM((CONF_K * SUB, c), F32), pltpu.VMEM((SHORT_K * SUB, c), F32),
                        pltpu.VMEM((CONF_K * SUB, c), F32), pltpu.VMEM((SHORT_K * SUB, c), F32),
                        pltpu.VMEM((SUB - 1, t + HALO, c), F32)],
        compiler_params=_params("arbitrary"), name=name,
    )(dx1, u, ca, cb, wa, lg, lb, wb, w_out, u if dep is None else dep)


def _matmul_tn(a, b, *, name):
    n_p, s_len, k = a.shape
    n = b.shape[1]
    tk = _col_tile(k)
    per = k // tk

    def body(a_ref, b_ref, o_ref):
        o_ref[...] = _dot_tn(a_ref[...], b_ref[...]).astype(BF16)

    return pl.pallas_call(
        body, grid=(n_p, per),
        in_specs=[pl.BlockSpec((None, s_len, tk), lambda p, j: (p, 0, j)), _resident((s_len, n), lambda p, j: (0, 0))],
        out_specs=pl.BlockSpec((tk, n), lambda p, j: (p * per + j, 0)),
        out_shape=jax.ShapeDtypeStruct((n_p * k, n), BF16),
        compiler_params=_params("parallel", "parallel"), name=name,
    )(a, b)


def _matmul_rmsbwd(dzs, wt, x, g, dx_in, *, name, dep=None):
    s_len, d = x.shape
    n_z, _, nj = dzs.shape
    t = _mm_tile(s_len)

    def body(*refs):
        dz_refs = refs[0:n_z]
        w_refs = refs[n_z:2 * n_z]
        x_ref, g_ref, dxi_ref, _, dx_ref, dxb_ref, dg_ref, dh = refs[2 * n_z:]

        @pl.when(pl.program_id(0) == 0)
        def _():
            dg_ref[...] = jnp.zeros_like(dg_ref)

        def blk(rows):
            xv = x_ref[rows, :]
            r = lax.rsqrt(jnp.mean(xv * xv, axis=-1, keepdims=True) + EPS)
            xn = xv * r
            dhv = dh[rows, :]
            dg_ref[...] += _rows8(dhv * xn)
            dn = dhv * g_ref[...]
            dx = dxi_ref[rows, :] + r * (dn - xn * jnp.mean(dn * xn, axis=-1, keepdims=True))
            dx_ref[rows, :] = dx
            dxb_ref[rows, :] = dx.astype(BF16)

        half = t // 2
        rb = min(128, half)
        for lo in range(0, t, half):
            acc = _dot(dz_refs[0][lo:lo + half, :], w_refs[0][...])
            for q in range(1, n_z):
                acc = acc + _dot(dz_refs[q][lo:lo + half, :], w_refs[q][...])
            dh[lo:lo + half, :] = acc
            for r0 in range(lo, lo + half, rb):
                blk(pl.ds(r0, rb))

    row = pl.BlockSpec((t, d), lambda i: (i, 0))
    in_specs = [pl.BlockSpec((None, t, nj), functools.partial(lambda q, i: (q, i, 0), q)) for q in range(n_z)]
    in_specs += [_resident((nj, d), functools.partial(lambda q, i: (q, 0), q)) for q in range(n_z)]
    in_specs += [row, _resident((1, d), lambda i: (0, 0)), row, ANY]
    return pl.pallas_call(
        body, grid=(s_len // t,), in_specs=in_specs,
        out_specs=[row, row, pl.BlockSpec((SUB, d), lambda i: (0, 0))],
        out_shape=[jax.ShapeDtypeStruct((s_len, d), F32), jax.ShapeDtypeStruct((s_len, d), BF16),
                   jax.ShapeDtypeStruct((SUB, d), F32)],
        scratch_shapes=[pltpu.VMEM((t, d), F32)],
        compiler_params=_params("arbitrary"), name=name,
    )(*([dzs] * n_z), *([wt] * n_z), x, g, dx_in, x if dep is None else dep)


def _row(v):
    return v.reshape(1, -1)


def _layer_fwd(x0, p, tag, dep=None, before_up=None):
    u, h1 = _rms_matmul(x0, _row(p["mix_norm_g"]), p["w_in_t"], _row(p["b_in"]), name=f"in_proj_{tag}", dep=dep)
    ycat, x1, ca, cb = _mix_fwd(u, x0, p["conv_a_w"], _row(p["conv_a_b"]), _row(p["ln_a_g"]), _row(p["ln_a_b"]),
                            p["conv_b_w"], p["w_out"], name=f"mix_fwd_{tag}")
    if before_up is not None:
        before_up(x1)
    uf, h2 = _rms_matmul(x1, _row(p["ffn_norm_g"]), p["w_up_t"], None, name=f"up_proj_{tag}")
    act, x2, cg, cv = _ffn_fwd(uf, x1, p["conv_f_w"], p["w_down"], name=f"ffn_fwd_{tag}")
    return x2, dict(x0=x0, h1=h1, u=u, ca=ca, cb=cb, ycat=ycat, x1=x1, h2=h2, uf=uf, cg=cg, cv=cv, act=act)


def _layer_bwd(dx2, dx2_b, p, saved, tag, ffn_grads, ffn_sent, mix_grads, mix_sent, dep=None):
    d_uf, dwf_g, dwf_v = _ffn_bwd(dx2_b, saved["uf"], saved["cg"], saved["cv"], p["conv_f_w"], p["w_down"],
                                  name=f"ffn_bwd_{tag}", dep=dep)
    g_down = _matmul_tn(saved["act"][None], dx2_b, name=f"dw_down_{tag}")
    g_up = _matmul_tn(d_uf, saved["h2"], name=f"dw_up_{tag}")
    dep_ffn = ffn_grads(dict(w_up=g_up, w_down=g_down), dx2_b)
    dx1, dx1_b, dg2 = _matmul_rmsbwd(d_uf, p["w_up_t"], saved["x1"], _row(p["ffn_norm_g"]), dx2,
                                     name=f"dh_ffn_{tag}", dep=dep_ffn)
    du, dwa, dwb, dba, dlg, dlb, dbin = _mix_bwd(
        dx1_b, saved["u"], saved["ca"], saved["cb"], p["conv_a_w"], _row(p["ln_a_g"]), _row(p["ln_a_b"]),
        p["conv_b_w"], p["w_out"], name=f"mix_bwd_{tag}", dep=ffn_sent(dx1_b))
    g_out = _matmul_tn(saved["ycat"][None], dx1_b, name=f"dw_out_{tag}")
    g_in = _matmul_tn(du[None], saved["h1"], name=f"dw_in_{tag}")
    conv = dict(conv_a_w=dwa, conv_b_w=dwb, conv_f_w=jnp.concatenate([dwf_g, dwf_v], axis=1))
    dep_mix = mix_grads(dict(w_in=g_in, w_out=g_out), conv, dx1_b)
    dx0, dx0_b, dg1 = _matmul_rmsbwd(du[None], p["w_in_t"], saved["x0"], _row(p["mix_norm_g"]), dx1,
                                     name=f"dh_mix_{tag}", dep=dep_mix)
    rep = dict(mix_norm_g=dg1, b_in=dbin, conv_a_b=dba, ln_a_g=dlg, ln_a_b=dlb, ffn_norm_g=dg2)
    return dx0, dx0_b, rep, mix_sent(dx0_b)


def _place():
    return lax.axis_index("x"), lax.axis_index("y"), lax.axis_index("c")


def _all_gather(arrs, *, name):
    n_a = len(arrs)

    def body(*refs):
        ins = refs[0:n_a]
        outs = refs[n_a:2 * n_a]
        send_sems, recv_sems, local_sems = refs[2 * n_a:]
        x, y, c = _place()
        sibling = (x, y, 1 - c)
        chips = [(1 - x, y), (x, 1 - y), (1 - x, 1 - y)]

        def slot(a, px, py, pc):
            return outs[a].at[4 * px + 2 * py + pc]

        def copy(a, k, block, to, src=None):
            return pltpu.make_async_remote_copy(
                src_ref=slot(a, *block) if src is None else src, dst_ref=slot(a, *block),
                send_sem=send_sems.at[a, k], recv_sem=recv_sems.at[a, k],
                device_id=to, device_id_type=MESH)

        me = (x, y, c)
        mine = [pltpu.make_async_copy(ins[a], slot(a, *me), local_sems.at[a]) for a in range(n_a)]
        for cp in mine:
            cp.start()
        started = []
        for a in range(n_a):
            first = [copy(a, 0, me, sibling, src=ins[a])]
            first += [copy(a, 1 + j, me, (*chip, c), src=ins[a]) for j, chip in enumerate(chips)]
            for cp in first:
                cp.start()
            started += first
        for a in range(n_a):
            for j, chip in enumerate(chips):
                copy(a, 1 + j, (*chip, c), me).wait_recv()
                passed = copy(a, 4 + j, (*chip, c), sibling)
                passed.start()
                started.append(passed)
        for a in range(n_a):
            copy(a, 0, sibling, me).wait_recv()
            for j, chip in enumerate(chips):
                copy(a, 4 + j, (*chip, 1 - c), me).wait_recv()
        for cp in started:
            cp.wait_send()
        for cp in mine:
            cp.wait()

    return pl.pallas_call(
        body, in_specs=[ANY] * n_a, out_specs=[ANY] * n_a,
        out_shape=[jax.ShapeDtypeStruct((N_DEV, *a.shape), a.dtype) for a in arrs],
        scratch_shapes=[pltpu.SemaphoreType.DMA((n_a, 7)), pltpu.SemaphoreType.DMA((n_a, 7)),
                        pltpu.SemaphoreType.DMA((n_a,))],
        name=name,
    )(*arrs)


def _row_tile(r, cap):
    for tr in range(min(cap, r) // 16 * 16, 0, -16):
        if r % tr == 0:
            return tr
    return r


def _pair_sum(mines, theirs, where, *, name):
    n_a = len(mines)
    n_chip = mines[0].shape[0]

    def body(where_ref, *refs):
        a_refs = refs[0:n_a]
        b_refs = refs[n_a:2 * n_a]
        p_refs = refs[2 * n_a:3 * n_a]
        l_refs = refs[3 * n_a:4 * n_a]
        q = pl.program_id(0)
        for a in range(n_a):
            p_refs[a][...] = (a_refs[a][...].astype(F32) + b_refs[a][...].astype(F32)).astype(p_refs[a].dtype)

        @pl.when(q == where_ref[1])
        def _():
            for a in range(n_a):
                l_refs[a][...] = p_refs[a][...]

    in_specs, out_p, out_l, shapes = [], [], [], []
    for m in mines:
        _, _, r, c = m.shape
        in_specs.append(pl.BlockSpec((None, None, r, c), lambda q, where_ref: (q, where_ref[0], 0, 0)))
    for m in mines:
        _, _, r, c = m.shape
        in_specs.append(pl.BlockSpec((None, r, c), lambda q, where_ref: (q, 0, 0)))
        out_p.append(pl.BlockSpec((None, r, c), lambda q, where_ref: (q, 0, 0)))
        out_l.append(pl.BlockSpec((None, r, c), lambda q, where_ref: (where_ref[1], 0, 0)))
        shapes.append(jax.ShapeDtypeStruct((n_chip, r, c), m.dtype))
    res = pl.pallas_call(
        body,
        grid_spec=pltpu.PrefetchScalarGridSpec(num_scalar_prefetch=1, grid=(n_chip,), in_specs=in_specs,
                                               out_specs=out_p + out_l),
        out_shape=shapes + shapes,
        compiler_params=_params("arbitrary"), name=name,
    )(where, *mines, *theirs)
    return list(res[:n_a]), list(res[n_a:])


HBM = pl.BlockSpec(memory_space=pltpu.HBM)
SEM = pl.BlockSpec(memory_space=pltpu.SEMAPHORE)
EFFECT = pltpu.SideEffectType.DATAFLOW_SIDE_EFFECTING


def _in_hbm(a):
    return pltpu.with_memory_space_constraint(a, pltpu.HBM)


def _split_start(srcs, lands, plan, n_copies, after, *, name):
    n_s, n_l = len(srcs), len(lands)

    def body(*refs):
        src_refs = refs[0:n_s]
        land_refs = refs[n_s:n_s + n_l]
        send_sems, recv_sems = refs[n_s + n_l + 1], refs[n_s + n_l + 2]
        token = refs[-1]
        for cp in plan(src_refs, land_refs, send_sems, recv_sems):
            cp.start()
        token[...] = jnp.zeros_like(token)

    thru = [pltpu.HBM(a.shape, a.dtype) for a in list(srcs) + list(lands)]
    res = pl.pallas_call(
        body, name=name,
        out_shape=(pltpu.SemaphoreType.DMA((n_copies,)), pltpu.SemaphoreType.DMA((n_copies,)), *thru,
                   jax.ShapeDtypeStruct((SUB, LANES), F32)),
        in_specs=[HBM] * (n_s + n_l) + [ANY],
        out_specs=(SEM, SEM, *([HBM] * (n_s + n_l)), pl.BlockSpec(memory_space=pltpu.VMEM)),
        input_output_aliases={i: 2 + i for i in range(n_s + n_l)},
        compiler_params=pltpu.CompilerParams(has_side_effects=EFFECT),
    )(*[_in_hbm(a) for a in srcs], *[_in_hbm(a) for a in lands], _in_hbm(after))
    return res[0], res[1], list(res[2:2 + n_s]), list(res[2 + n_s:2 + n_s + n_l]), res[-1]


def _split_wait(send_sems, recv_sems, srcs, lands, after, plan, *, name):
    n_s, n_l = len(srcs), len(lands)

    def body(*refs):
        src_refs = refs[0:n_s]
        land_refs = refs[n_s:n_s + n_l]
        send, recv = refs[n_s + n_l], refs[n_s + n_l + 1]
        for cp in plan(src_refs, land_refs, send, recv):
            cp.wait_send()
            cp.wait_recv()

    res = pl.pallas_call(
        body, name=name,
        out_shape=tuple(pltpu.HBM(a.shape, a.dtype) for a in list(srcs) + list(lands)),
        in_specs=[HBM] * (n_s + n_l) + [SEM, SEM, ANY],
        out_specs=tuple([HBM] * (n_s + n_l)),
        input_output_aliases={i: i for i in range(n_s + n_l)},
        compiler_params=pltpu.CompilerParams(has_side_effects=EFFECT),
    )(*srcs, *lands, send_sems, recv_sems, _in_hbm(after))
    return list(res[:n_s]), list(res[n_s:])


def _remote(src, dst, send_sems, recv_sems, k, to):
    return pltpu.make_async_remote_copy(src_ref=src, dst_ref=dst, send_sem=send_sems.at[k], recv_sem=recv_sems.at[k],
                                        device_id=to, device_id_type=MESH)


def _gather_plan_first(src_refs, land_refs, send_sems, recv_sems):
    x, y, c = _place()
    me = 4 * x + 2 * y + c
    peers = [(x, y, 1 - c), (1 - x, y, c), (x, 1 - y, c), (1 - x, 1 - y, c)]
    return [_remote(src, land.at[me], send_sems, recv_sems, 4 * a + k, to)
            for a, (src, land) in enumerate(zip(src_refs, land_refs)) for k, to in enumerate(peers)]


def _gather_plan_second(src_refs, land_refs, send_sems, recv_sems):
    x, y, c = _place()
    chips = [(1 - x, y), (x, 1 - y), (1 - x, 1 - y)]
    out = []
    for a, land in enumerate(land_refs):
        for j, (px, py) in enumerate(chips):
            slot = land.at[4 * px + 2 * py + c]
            out.append(_remote(slot, slot, send_sems, recv_sems, 3 * a + j, (x, y, 1 - c)))
    return out


def _siblings_plan(src_refs, land_refs, send_sems, recv_sems):
    x, y, c = _place()
    return [_remote(src.at[:, 1 - c], land, send_sems, recv_sems, a, (x, y, 1 - c))
            for a, (src, land) in enumerate(zip(src_refs, land_refs))]


def _chips_plan(src_refs, land_refs, send_sems, recv_sems):
    x, y, c = _place()
    my_chip = 2 * x + y
    chips = [(1 - x, y), (x, 1 - y), (1 - x, 1 - y)]
    return [_remote(src.at[2 * px + py], land.at[my_chip], send_sems, recv_sems, 3 * a + j, (px, py, c))
            for a, (src, land) in enumerate(zip(src_refs, land_refs)) for j, (px, py) in enumerate(chips)]


def _gather_landings(shards, me, *, name):
    blank = _unwritten([jax.ShapeDtypeStruct((N_DEV, *s.shape), s.dtype) for s in shards], name=name)
    return [lax.dynamic_update_index_in_dim(b, s, me, 0) for b, s in zip(blank, shards)]


def _adamw_math(g, w, m, v):
    m = ADAM_B1 * m + (1.0 - ADAM_B1) * g
    v = ADAM_B2 * v + (1.0 - ADAM_B2) * (g * g)
    m_hat = m / (1.0 - ADAM_B1 ** ADAM_STEP)
    v_hat = v / (1.0 - ADAM_B2 ** ADAM_STEP)
    delta = -ADAM_LR * (m_hat / (jnp.sqrt(v_hat) + ADAM_EPS) + ADAM_WD * w)
    return delta, m, v


def _adamw_sharded(parts, w, m, v, *, name, dep=None):
    n_layers, r, c = w.shape
    n_chip = parts[0].shape[0]
    tr = _row_tile(r, 384)
    n_i = r // tr

    def body(*refs):
        p_refs = refs[0:n_layers]
        w_ref, m_ref, v_ref, _, g_out, d_out, m_out, v_out = refs[n_layers:]
        layer = pl.program_id(0)
        for l in range(n_layers):
            @pl.when(layer == l)
            def _(l=l):
                g = p_refs[l][0].astype(F32)
                for q in range(1, n_chip):
                    g = g + p_refs[l][q].astype(F32)
                delta, m_new, v_new = _adamw_math(g, w_ref[...], m_ref[...], v_ref[...])
                g_out[...] = g
                d_out[...] = delta
                m_out[...] = m_new
                v_out[...] = v_new

    def part_map(l):
        return lambda layer, i: (0, jnp.where(layer == l, i, jnp.where(layer < l, 0, n_i - 1)), 0)

    blk = pl.BlockSpec((None, tr, c), lambda layer, i: (layer, i, 0))
    return pl.pallas_call(
        body, grid=(n_layers, n_i),
        in_specs=[pl.BlockSpec((n_chip, tr, c), part_map(l)) for l in range(n_layers)] + [blk, blk, blk, ANY],
        out_specs=[blk] * 4, out_shape=[jax.ShapeDtypeStruct((n_layers, r, c), F32)] * 4,
        compiler_params=_params("arbitrary", "arbitrary"), name=name,
    )(*parts, w, m, v, w if dep is None else dep)


def _fold_partials(cols, *, name):
    widths = [c.shape[1] for c in cols]

    def body(*refs):
        o_ref = refs[-1]
        pos = 0
        for ref, width in zip(refs[:-1], widths):
            o_ref[:, pos:pos + width] = jnp.sum(ref[...], axis=0, keepdims=True)
            pos += width

    return pl.pallas_call(body, out_shape=jax.ShapeDtypeStruct((1, sum(widths)), F32), name=name)(*cols)


def _adamw_replicated(parts, names, w, m, v, n_loss, *, name):
    n_dev = parts.shape[0]
    n_layers = w[names[0]].shape[0]
    every = list(names) + ["final_norm_g"]
    n_p = len(every)

    def body(*refs):
        p_ref = refs[0]
        w_refs = dict(zip(every, refs[1:1 + n_p]))
        m_refs = dict(zip(every, refs[1 + n_p:1 + 2 * n_p]))
        v_refs = dict(zip(every, refs[1 + 2 * n_p:1 + 3 * n_p]))
        l_out = refs[1 + 3 * n_p]
        outs = refs[2 + 3 * n_p:]
        o_refs = {n: outs[4 * q:4 * q + 4] for q, n in enumerate(every)}
        acc = p_ref[0]
        for q in range(1, n_dev):
            acc = acc + p_ref[q]
        tot = jnp.sum(acc, axis=0, keepdims=True)
        pos = 0
        where = [(n, l) for l in range(n_layers) for n in names] + [("final_norm_g", 0)]
        for n, l in where:
            width = w_refs[n].shape[1]
            g = tot[:, pos:pos + width]
            pos += width
            row = pl.ds(l, 1)
            delta, m_new, v_new = _adamw_math(g, w_refs[n][row, :], m_refs[n][row, :], v_refs[n][row, :])
            for o, val in zip(o_refs[n], (g, delta, m_new, v_new)):
                o[row, :] = val
        l_out[...] = (0.5 / n_loss) * jnp.sum(tot[:, pos:pos + n_loss], axis=-1, keepdims=True)

    shapes = [jax.ShapeDtypeStruct((1, 1), F32)]
    for n in every:
        shapes += [jax.ShapeDtypeStruct(w[n].shape, F32)] * 4
    res = pl.pallas_call(
        body, out_shape=shapes,
        compiler_params=pltpu.CompilerParams(vmem_limit_bytes=VMEM_LIMIT), name=name,
    )(parts, *[w[n] for n in every], *[m[n] for n in every], *[v[n] for n in every])
    return res[0], {n: res[1 + 4 * q:5 + 4 * q] for q, n in enumerate(every)}


BIG = ("w_in", "w_out", "w_up", "w_down")
COL_SHARDED = ("w_in", "w_up")
CONV = ("conv_a_w", "conv_b_w", "conv_f_w")
REPLICATED = ("mix_norm_g", "b_in", "conv_a_b", "ln_a_g", "ln_a_b", "ffn_norm_g")
KINDS = ("grad", "delta", "m", "v")
FFN_PART = ("w_up", "w_down")
MIX_PART = ("w_in", "w_out")


def _weights_from_gathered(g):
    n_dev, r, c = g.shape
    return g.reshape(n_dev * r, c)


def _slabs_from_full(grad):
    return grad.reshape(N_DEV, grad.shape[0] // N_DEV, grad.shape[1])


def _unwritten(like, *, name):
    return pl.pallas_call(lambda *refs: None, out_specs=[ANY] * len(like), out_shape=list(like), name=name)()


def kernel(x, mix_norm_g, w_in, b_in, conv_a_w, conv_a_b, ln_a_g, ln_a_b, conv_b_w, w_out, ffn_norm_g, w_up, conv_f_w, w_down, final_norm_g, loss_target, m_mix_norm_g, m_w_in, m_b_in, m_conv_a_w, m_conv_a_b, m_ln_a_g, m_ln_a_b, m_conv_b_w, m_w_out, m_ffn_norm_g, m_w_up, m_conv_f_w, m_w_down, m_final_norm_g, v_mix_norm_g, v_w_in, v_b_in, v_conv_a_w, v_conv_a_b, v_ln_a_g, v_ln_a_b, v_conv_b_w, v_w_out, v_ffn_norm_g, v_w_up, v_conv_f_w, v_w_down, v_final_norm_g):
    w = dict(mix_norm_g=mix_norm_g, w_in=w_in, b_in=b_in, conv_a_w=conv_a_w, conv_a_b=conv_a_b, ln_a_g=ln_a_g,
             ln_a_b=ln_a_b, conv_b_w=conv_b_w, w_out=w_out, ffn_norm_g=ffn_norm_g, w_up=w_up, conv_f_w=conv_f_w,
             w_down=w_down, final_norm_g=final_norm_g)
    m = dict(mix_norm_g=m_mix_norm_g, w_in=m_w_in, b_in=m_b_in, conv_a_w=m_conv_a_w, conv_a_b=m_conv_a_b,
             ln_a_g=m_ln_a_g, ln_a_b=m_ln_a_b, conv_b_w=m_conv_b_w, w_out=m_w_out, ffn_norm_g=m_ffn_norm_g,
             w_up=m_w_up, conv_f_w=m_conv_f_w, w_down=m_w_down, final_norm_g=m_final_norm_g)
    v = dict(mix_norm_g=v_mix_norm_g, w_in=v_w_in, b_in=v_b_in, conv_a_w=v_conv_a_w, conv_a_b=v_conv_a_b,
             ln_a_g=v_ln_a_g, ln_a_b=v_ln_a_b, conv_b_w=v_conv_b_w, w_out=v_w_out, ffn_norm_g=v_ffn_norm_g,
             w_up=v_w_up, conv_f_w=v_conv_f_w, w_down=v_w_down, final_norm_g=v_final_norm_g)
    order = list(w)
    n_layers = w_in.shape[0]
    xs = x[0]
    target = loss_target[0]
    flip = lambda a: jnp.transpose(a, (0, 2, 1))
    wt, mt, vt = ({n: flip(d[n]) if n in COL_SHARDED else d[n] for n in BIG} for d in (w, m, v))
    px, py, pc = _place()
    where = jnp.stack([pc, 2 * px + py]).astype(jnp.int32)
    me = 4 * px + 2 * py + pc

    assert BIG == MIX_PART + FFN_PART
    key = lambda n: n + "_t" if n in COL_SHARDED else n
    shard = lambda n, l: wt[n][l].astype(BF16)

    def gather_start(names, l, after, tag):
        shards = [shard(n, l) for n in names]
        lands = _gather_landings(shards, me, name=f"gather_landing_{tag}")
        return _split_start(shards, lands, _gather_plan_first, 4 * len(shards), after, name=f"gather_first_start_{tag}")

    def gather_mid(first, after, tag):
        return _split_wait(first[0], first[1], first[2], first[3], after, _gather_plan_first,
                           name=f"gather_first_wait_{tag}")[1]

    def forward_start(lands, after, tag):
        return _split_start([], lands, _gather_plan_second, 3 * len(lands), after, name=f"gather_second_start_{tag}")

    def forward_finish(second, after, tag):
        return _split_wait(second[0], second[1], [], second[3], after, _gather_plan_second,
                           name=f"gather_second_wait_{tag}")[1]

    gathered = _all_gather([shard(n, 0) for n in MIX_PART] + [w[n] for n in CONV], name="gather_weights_0")
    params = [{n: w[n][l] for n in REPLICATED} for l in range(n_layers)]
    for n, g in zip(CONV, gathered[len(MIX_PART):]):
        n_dev, _, taps, c = g.shape
        full = g.transpose(1, 2, 0, 3).reshape(n_layers, taps, n_dev * c)
        for l in range(n_layers):
            params[l][n] = full[l]
    for n, g in zip(MIX_PART, gathered):
        params[0][key(n)] = _weights_from_gathered(g)
    ffn_first = gather_start(FFN_PART, 0, gathered[0], "0_ffn")
    pending = {}

    h = xs
    saved = []
    for l in range(n_layers):
        nxt = l + 1 if l + 1 < n_layers else None

        def before_up(x1, l=l, nxt=nxt):
            if l == 0:
                second = forward_start(gather_mid(ffn_first, x1, "0_ffn"), x1, "0_ffn")
                after = second[4]
            else:
                second = pending[l]["ffn"]
                after = x1
            if nxt is not None:
                pending[nxt] = dict(first=gather_start(BIG, nxt, after, str(nxt)))
                after = pending[nxt]["first"][4]
            for n, g in zip(FFN_PART, forward_finish(second, after, f"{l}_ffn")):
                params[l][key(n)] = _weights_from_gathered(g)

        h, keep = _layer_fwd(h, params[l], str(l), dep=ffn_first[4] if l == 0 else None, before_up=before_up)
        saved.append(keep)
        if nxt is not None:
            arrived = gather_mid(pending[nxt]["first"], h, str(nxt))
            mix_second = forward_start(arrived[:len(MIX_PART)], h, f"{nxt}_mix")
            pending[nxt]["ffn"] = forward_start(arrived[len(MIX_PART):], mix_second[4], f"{nxt}_ffn")
            for n, g in zip(MIX_PART, forward_finish(mix_second, pending[nxt]["ffn"][4], f"{nxt}_mix")):
                params[nxt][key(n)] = _weights_from_gathered(g)

    def start_siblings(slabs, after, tag):
        mines = [s.reshape(N_CHIP, 2, *s.shape[1:]) for s in slabs]
        lands = _unwritten([jax.ShapeDtypeStruct((N_CHIP, *m.shape[2:]), m.dtype) for m in mines],
                           name=f"reduce_siblings_landing_{tag}")
        return _split_start(mines, lands, _siblings_plan, len(mines), after, name=f"reduce_siblings_start_{tag}")

    def start_chips(sib, after, tag):
        mines, theirs = _split_wait(sib[0], sib[1], sib[2], sib[3], after, _siblings_plan,
                                    name=f"reduce_siblings_wait_{tag}")
        pairs, lands = _pair_sum(mines, theirs, where, name=f"pair_sum_{tag}")
        return _split_start(pairs, lands, _chips_plan, 3 * len(pairs), after, name=f"reduce_chips_start_{tag}")

    def finish_reduce(fly, after, tag):
        return _split_wait(fly[0], fly[1], fly[2], fly[3], after, _chips_plan, name=f"reduce_chips_wait_{tag}")[1]

    loss_sq, dh, dh_b, dgf = _loss_bwd(h, _row(final_norm_g), target, name="loss")
    conv_g = {n: [None] * n_layers for n in CONV}
    rep_g = [None] * n_layers
    siblings = {}
    flights = {}
    token = None
    for l in reversed(range(n_layers)):
        def ffn_grads(g, after, l=l):
            siblings[l, "ffn"] = start_siblings([_slabs_from_full(g[n]) for n in FFN_PART], after, f"{l}_ffn")
            return siblings[l, "ffn"][4]

        def ffn_sent(after, l=l):
            flights[l, "ffn"] = start_chips(siblings[l, "ffn"], after, f"{l}_ffn")
            return flights[l, "ffn"][4]

        def mix_grads(g, conv, after, l=l):
            for n in CONV:
                conv_g[n][l] = conv[n]
            slabs = [_slabs_from_full(g[n]) for n in MIX_PART]
            if l == 0:
                for n in CONV:
                    full = jnp.stack(conv_g[n])
                    _, taps, c = full.shape
                    slabs.append(full.reshape(n_layers, taps, N_DEV, c // N_DEV).transpose(2, 0, 1, 3)
                                 .reshape(N_DEV, n_layers * taps, c // N_DEV))
            siblings[l, "mix"] = start_siblings(slabs, after, f"{l}_mix")
            return siblings[l, "mix"][4]

        def mix_sent(after, l=l):
            flights[l, "mix"] = start_chips(siblings[l, "mix"], after, f"{l}_mix")
            return flights[l, "mix"][4]

        dh, dh_b, rep_g[l], token = _layer_bwd(dh, dh_b, params[l], saved[l], str(l), ffn_grads, ffn_sent,
                                               mix_grads, mix_sent, dep=token)

    sums = {key: finish_reduce(fly, dh, f"{key[0]}_{key[1]}") for key, fly in flights.items() if key != (0, "mix")}
    out = {k: {} for k in KINDS}

    def adamw_big(names, part, dep):
        for q, n in enumerate(names):
            layer_parts = [sums[l, part][q] for l in range(n_layers)]
            res = _adamw_sharded(layer_parts, wt[n], mt[n], vt[n], name=f"adamw_{n}", dep=dep)
            for k, r in zip(KINDS, res):
                out[k][n] = flip(r) if n in COL_SHARDED else r

    adamw_big(FFN_PART, "ffn", token)

    rep_cols = [rep_g[l][n] for l in range(n_layers) for n in REPLICATED] + [dgf, loss_sq]
    rep_all = _all_gather([_fold_partials(rep_cols, name="fold_small")], name="gather_small")[0]
    with_final = lambda d: {**{n: d[n] for n in REPLICATED}, "final_norm_g": _row(d["final_norm_g"])}
    loss, rep_res = _adamw_replicated(rep_all, REPLICATED, with_final(w), with_final(m), with_final(v),
                                      loss_sq.shape[1], name="adamw_small")
    for n, res in rep_res.items():
        for k, r in zip(KINDS, res):
            out[k][n] = r.reshape(w[n].shape)

    last = finish_reduce(flights[0, "mix"], rep_res["b_in"][0], "0_mix")
    sums[0, "mix"] = last[:len(MIX_PART)]
    adamw_big(MIX_PART, "mix", None)
    for n, p in zip(CONV, last[len(MIX_PART):]):
        as_one = lambda a: a.reshape(1, *p.shape[1:])
        for k, r in zip(KINDS, _adamw_sharded([p], as_one(w[n]), as_one(m[n]), as_one(v[n]), name=f"adamw_{n}")):
            out[k][n] = r.reshape(w[n].shape)

    grad_x = dh.reshape(x.shape)
    return (loss.reshape(()), grad_x, *[out["grad"][n] for n in order], *[out["delta"][n] for n in order],
            *[out["m"][n] for n in order], *[out["v"][n] for n in order])
```

```python
import functools

import jax
import jax.numpy as jnp
from jax import lax
from jax.experimental import pallas as pl
from jax.experimental.pallas import tpu as pltpu

F32 = jnp.float32
BF16 = jnp.bfloat16

N_DEV = 8
N_CHIP = 4
D_CONF = 512
CONF_K = 31
SHORT_K = 3
EPS = 1e-6
HALO = 32
HALO3 = 8
HALO3_BLK = 16
LANES = 128
SUB = 8
VMEM_LIMIT = 56 * 1024 * 1024

ADAM_LR = 0.001
ADAM_B1 = 0.9
ADAM_B2 = 0.999
ADAM_EPS = 1e-08
ADAM_WD = 0.01
ADAM_STEP = 10

MESH = pl.DeviceIdType.MESH
ANY = pl.BlockSpec(memory_space=pl.ANY)


def _params(*sem):
    return pltpu.CompilerParams(dimension_semantics=sem, vmem_limit_bytes=VMEM_LIMIT)


def _resident(shape, index_map):
    return pl.BlockSpec(shape, index_map, pipeline_mode=pl.Buffered(1))


def _row_loop(n_rows, rb, fn, unroll=1):
    rb = min(rb, n_rows)

    def body(i, carry):
        fn(pl.ds(pl.multiple_of(i * rb, rb), rb))
        return carry
    lax.fori_loop(0, n_rows // rb, body, 0, unroll=unroll)


def _rows8(v):
    acc = v[0:SUB]
    for k in range(1, v.shape[0] // SUB):
        acc = acc + v[k * SUB:(k + 1) * SUB]
    return acc


def _sigmoid(z):
    return 0.5 * jnp.tanh(0.5 * z) + 0.5


def _dot(a, b):
    return jnp.dot(a, b, preferred_element_type=F32)


def _dot_nt(a, b):
    return lax.dot_general(a, b, (((1,), (1,)), ((), ())), preferred_element_type=F32)


def _dot_tn(a, b):
    return lax.dot_general(a, b, (((0,), (0,)), ((), ())), preferred_element_type=F32)


def _replicate_taps(w_ref, wrep, taps):
    for k in range(taps):
        wrep[pl.ds(k * SUB, SUB), :] = jnp.broadcast_to(w_ref[pl.ds(k, 1), :], (SUB, w_ref.shape[1]))


def _shift_copies(win, shf, lanes):
    span = win.shape[0] - SUB
    for r in range(1, SUB):
        for j0 in range(0, span, 64):
            n = min(64, span - j0)
            shf[r - 1, pl.ds(j0, n), lanes] = win[pl.ds(j0 + r, n), lanes]


def _rows_at(win, shf, off, rb, lanes):
    if shf is None or off % SUB == 0:
        return win[pl.ds(off, rb), lanes]
    return shf[off % SUB - 1, pl.ds(off - off % SUB, rb), lanes]


def _conv_taps(win, wrep, out, *, taps, n_rows, base, width, transposed=False, bias_ref=None, shf=None):
    rb = min(64, n_rows)

    def lane_body(cb, carry):
        lanes = pl.ds(pl.multiple_of(cb * LANES, LANES), LANES)
        if shf is not None:
            _shift_copies(win, shf, lanes)
        for r0 in range(0, n_rows, rb):
            acc = None
            for k in range(taps):
                off = (taps - 1 - k) if transposed else (k - (taps - 1))
                wk = jnp.tile(wrep[pl.ds(k * SUB, SUB), lanes], (rb // SUB, 1))
                term = wk * _rows_at(win, shf, base + r0 + off, rb, lanes)
                acc = term if acc is None else acc + term
            if bias_ref is not None:
                acc = acc + bias_ref[:, lanes]
            out[pl.ds(r0, rb), lanes] = acc.astype(out.dtype)
        return carry

    lax.fori_loop(0, width // LANES, lane_body, 0)


def _conv_bwd_taps(win, wrep, x_cur, dx_out, dw_acc, *, taps, n_rows, width, shf=None):
    rb = min(32 if taps > 8 else 64, n_rows)

    def lane_body(cb, carry):
        lanes = pl.ds(pl.multiple_of(cb * LANES, LANES), LANES)
        if shf is not None:
            _shift_copies(win, shf, lanes)
        sums = [None] * taps
        for r0 in range(0, n_rows, rb):
            xv = x_cur[pl.ds(r0, rb), lanes].astype(F32)
            acc = None
            for k in range(taps):
                shifted = _rows_at(win, shf, r0 + taps - 1 - k, rb, lanes)
                term = jnp.tile(wrep[pl.ds(k * SUB, SUB), lanes], (rb // SUB, 1)) * shifted
                acc = term if acc is None else acc + term
                part = _rows8(xv * shifted)
                sums[k] = part if sums[k] is None else sums[k] + part
            dx_out[pl.ds(r0, rb), lanes] = acc.astype(dx_out.dtype)
        for k in range(taps):
            dw_acc[pl.ds(k * SUB, SUB), lanes] += sums[k]
        return carry

    lax.fori_loop(0, width // LANES, lane_body, 0)


def _fold8(acc_ref, taps):
    return jnp.concatenate(
        [jnp.sum(acc_ref[pl.ds(k * SUB, SUB), :], axis=0, keepdims=True) for k in range(taps)], axis=0)


def _seq_tile(s_len):
    return min(512, s_len)


def _mm_tile(s_len):
    return min(512, s_len)


def _ff_chunk(ff):
    best = LANES
    for c in range(LANES, 1408 + 1, LANES):
        if ff % c == 0:
            best = c
    return best


def _col_tile(n):
    for c in (512, 1408, 256, LANES):
        if n % c == 0:
            return c
    return n


def _rms_matmul(x, g, wt, b, *, name, dep=None):
    s_len, d = x.shape
    n = wt.shape[0]
    tm = _mm_tile(s_len)
    cn = _col_tile(n)
    has_bias = b is not None

    def body(*refs):
        x_ref, g_ref, w_ref = refs[0:3]
        b_ref = refs[3] if has_bias else None
        o_ref, h_ref = refs[-2:]

        def blk(rows):
            xv = x_ref[rows, :]
            r = lax.rsqrt(jnp.mean(xv * xv, axis=-1, keepdims=True) + EPS)
            h_ref[rows, :] = ((xv * r) * g_ref[...]).astype(BF16)

        rb = min(128, tm)
        for r0 in range(0, tm, rb):
            blk(pl.ds(r0, rb))
        for j in range(n // cn):
            acc = _dot_nt(h_ref[...], w_ref[j * cn:(j + 1) * cn, :])
            if has_bias:
                acc = acc + b_ref[:, j * cn:(j + 1) * cn]
            o_ref[:, j * cn:(j + 1) * cn] = acc.astype(BF16)

    in_specs = [pl.BlockSpec((tm, d), lambda i: (i, 0)), _resident((1, d), lambda i: (0, 0)),
                _resident((n, d), lambda i: (0, 0))]
    args = [x, g, wt]
    if has_bias:
        in_specs.append(_resident((1, n), lambda i: (0, 0)))
        args.append(b)
    in_specs.append(ANY)
    args.append(x if dep is None else dep)
    return pl.pallas_call(
        body, grid=(s_len // tm,), in_specs=in_specs,
        out_specs=[pl.BlockSpec((tm, n), lambda i: (i, 0)), pl.BlockSpec((tm, d), lambda i: (i, 0))],
        out_shape=[jax.ShapeDtypeStruct((s_len, n), BF16), jax.ShapeDtypeStruct((s_len, d), BF16)],
        compiler_params=_params("parallel"), name=name,
    )(*args)


def _mix_windows(u_ref, uh_ref, gw, pw, first, t):
    c = D_CONF
    uh = uh_ref[...].astype(F32)
    gw[0:HALO, :] = jnp.where(first, 0.0, uh[:, 0:c] * _sigmoid(uh[:, c:2 * c]))
    pw[0:HALO3, :] = jnp.where(first, 0.0, uh[HALO - HALO3:HALO, 3 * c:4 * c] * uh[HALO - HALO3:HALO, 4 * c:5 * c])

    def blk(rows):
        dst = pl.ds(pl.multiple_of(rows.start + HALO, SUB), rows.size)
        gw[dst, :] = u_ref[rows, 0:c].astype(F32) * _sigmoid(u_ref[rows, c:2 * c].astype(F32))
        dst3 = pl.ds(pl.multiple_of(rows.start + HALO3, SUB), rows.size)
        pw[dst3, :] = u_ref[rows, 3 * c:4 * c].astype(F32) * u_ref[rows, 4 * c:5 * c].astype(F32)
    _row_loop(t, 64, blk)


def _mix_fwd(u, x0, wa, ba, lg, lb, wb, w_out, *, name):
    s_len, d_in = u.shape
    d = x0.shape[1]
    c = D_CONF
    t = _seq_tile(s_len)
    per = t // HALO

    def body(u_ref, uh_ref, x0_ref, wa_ref, ba_ref, lg_ref, lb_ref, wb_ref, wo_ref, y_ref, x1_ref, ca, cb,
             gw, pw, wrep_a, wrep_b, shf):
        first = pl.program_id(0) == 0
        _mix_windows(u_ref, uh_ref, gw, pw, first, t)
        _replicate_taps(wa_ref, wrep_a, CONF_K)
        _replicate_taps(wb_ref, wrep_b, SHORT_K)
        _conv_taps(gw, wrep_a, ca, taps=CONF_K, n_rows=t, base=HALO, width=c, bias_ref=ba_ref, shf=shf)
        _conv_taps(pw, wrep_b, cb, taps=SHORT_K, n_rows=t, base=HALO3, width=c)

        def blk(rows):
            cv = ca[rows, :]
            mu = jnp.mean(cv, axis=-1, keepdims=True)
            xc = cv - mu
            var = jnp.mean(xc * xc, axis=-1, keepdims=True)
            ln = (xc * lax.rsqrt(var + EPS)) * lg_ref[...] + lb_ref[...]
            y_ref[rows, 0:c] = (ln * _sigmoid(ln)).astype(BF16)
            y_ref[rows, c:2 * c] = (u_ref[rows, 2 * c:3 * c].astype(F32) * cb[rows, :]).astype(BF16)
        half = t // 2
        rb = min(64, half)
        for lo in range(0, t, half):
            for r0 in range(lo, lo + half, rb):
                blk(pl.ds(r0, rb))
            x1_ref[lo:lo + half, :] = x0_ref[lo:lo + half, :] + _dot(y_ref[lo:lo + half, :], wo_ref[...])

    small = lambda r: _resident((r, c), lambda i: (0, 0))
    return pl.pallas_call(
        body, grid=(s_len // t,),
        in_specs=[pl.BlockSpec((t, d_in), lambda i: (i, 0)),
                  pl.BlockSpec((HALO, d_in), lambda i: (jnp.maximum(i * per - 1, 0), 0)),
                  pl.BlockSpec((t, d), lambda i: (i, 0)),
                  small(CONF_K), small(1), small(1), small(1), small(SHORT_K),
                  _resident((2 * c, d), lambda i: (0, 0))],
        out_specs=[pl.BlockSpec((t, 2 * c), lambda i: (i, 0)), pl.BlockSpec((t, d), lambda i: (i, 0)),
                   pl.BlockSpec((t, c), lambda i: (i, 0)), pl.BlockSpec((t, c), lambda i: (i, 0))],
        out_shape=[jax.ShapeDtypeStruct((s_len, 2 * c), BF16), jax.ShapeDtypeStruct((s_len, d), F32),
                   jax.ShapeDtypeStruct((s_len, c), F32), jax.ShapeDtypeStruct((s_len, c), F32)],
        scratch_shapes=[pltpu.VMEM((HALO + t, c), F32), pltpu.VMEM((HALO3 + t, c), F32),
                        pltpu.VMEM((CONF_K * SUB, c), F32), pltpu.VMEM((SHORT_K * SUB, c), F32),
                        pltpu.VMEM((SUB - 1, HALO + t, c), F32)],
        compiler_params=_params("arbitrary"), name=name,
    )(u, u, x0, wa, ba, lg, lb, wb, w_out)


def _ffn_fwd(uf, x1, wf, w_down, *, name):
    s_len, ff2 = uf.shape
    ff = ff2 // 2
    d = x1.shape[1]
    t = _seq_tile(s_len)
    fc = _ff_chunk(ff)
    nc = ff // fc
    per = t // HALO3_BLK
    half = t // 2
    rb = min(64, half)

    def body(ug_ref, ugh_ref, uv_ref, uvh_ref, x1_ref, wfg_ref, wfv_ref, wd_ref,
             act_ref, x2_ref, cg_ref, cv_ref, gwin, vwin, wrep_g, wrep_v):
        first = pl.program_id(0) == 0
        first_chunk = pl.program_id(1) == 0
        lo8 = HALO3_BLK - HALO3
        gwin[0:HALO3, :] = jnp.where(first, 0.0, ugh_ref[...].astype(F32)[lo8:HALO3_BLK])
        vwin[0:HALO3, :] = jnp.where(first, 0.0, uvh_ref[...].astype(F32)[lo8:HALO3_BLK])
        _replicate_taps(wfg_ref, wrep_g, SHORT_K)
        _replicate_taps(wfv_ref, wrep_v, SHORT_K)
        chunk_rows = pl.ds(pl.multiple_of(pl.program_id(1) * fc, fc), fc)

        def conv(win, wrep, r0, lanes):
            acc = None
            for k in range(SHORT_K):
                wk = jnp.tile(wrep[k * SUB:(k + 1) * SUB, lanes], (rb // SUB, 1))
                off = HALO3 + r0 + k - (SHORT_K - 1)
                term = wk * win[off:off + rb, lanes]
                acc = term if acc is None else acc + term
            return acc

        for lo in range(0, t, half):
            for r0 in range(lo, lo + half, rb):
                gwin[HALO3 + r0:HALO3 + r0 + rb, :] = ug_ref[r0:r0 + rb, :].astype(F32)
                vwin[HALO3 + r0:HALO3 + r0 + rb, :] = uv_ref[r0:r0 + rb, :].astype(F32)
            for cb in range(fc // LANES):
                lanes = slice(cb * LANES, (cb + 1) * LANES)
                for r0 in range(lo, lo + half, rb):
                    gv = conv(gwin, wrep_g, r0, lanes)
                    vv = conv(vwin, wrep_v, r0, lanes)
                    cg_ref[r0:r0 + rb, lanes] = gv.astype(BF16)
                    cv_ref[r0:r0 + rb, lanes] = vv.astype(BF16)
                    act_ref[r0:r0 + rb, lanes] = ((gv * _sigmoid(gv)) * vv).astype(BF16)
            base = jnp.where(first_chunk, x1_ref[lo:lo + half, :], x2_ref[lo:lo + half, :])
            x2_ref[lo:lo + half, :] = base + _dot(act_ref[lo:lo + half, :], wd_ref[chunk_rows, :])

    halo_map = lambda off: (lambda i, j: (jnp.maximum(i * per - 1, 0), j + off))
    return pl.pallas_call(
        body, grid=(s_len // t, nc),
        in_specs=[pl.BlockSpec((t, fc), lambda i, j: (i, j)), pl.BlockSpec((HALO3_BLK, fc), halo_map(0)),
                  pl.BlockSpec((t, fc), lambda i, j: (i, j + nc)), pl.BlockSpec((HALO3_BLK, fc), halo_map(nc)),
                  pl.BlockSpec((t, d), lambda i, j: (i, 0)),
                  pl.BlockSpec((SHORT_K, fc), lambda i, j: (0, j)),
                  pl.BlockSpec((SHORT_K, fc), lambda i, j: (0, j + nc)),
                  _resident((ff, d), lambda i, j: (0, 0))],
        out_specs=[pl.BlockSpec((t, fc), lambda i, j: (i, j)), pl.BlockSpec((t, d), lambda i, j: (i, 0)),
                   pl.BlockSpec((t, fc), lambda i, j: (i, j)), pl.BlockSpec((t, fc), lambda i, j: (i, j))],
        out_shape=[jax.ShapeDtypeStruct((s_len, ff), BF16), jax.ShapeDtypeStruct((s_len, d), F32),
                   jax.ShapeDtypeStruct((s_len, ff), BF16), jax.ShapeDtypeStruct((s_len, ff), BF16)],
        scratch_shapes=[pltpu.VMEM((HALO3 + t, fc), F32), pltpu.VMEM((HALO3 + t, fc), F32),
                        pltpu.VMEM((SHORT_K * SUB, fc), F32), pltpu.VMEM((SHORT_K * SUB, fc), F32)],
        compiler_params=_params("parallel", "arbitrary"), name=name,
    )(uf, uf, uf, uf, x1, wf, wf, w_down)


def _loss_bwd(x, g, target, *, name):
    s_len, d = x.shape
    t = _seq_tile(s_len)

    def body(x_ref, g_ref, t_ref, l_ref, dx_ref, dxb_ref, dg_ref):
        @pl.when(pl.program_id(0) == 0)
        def _():
            l_ref[...] = jnp.zeros_like(l_ref)
            dg_ref[...] = jnp.zeros_like(dg_ref)

        def blk(rows):
            xv = x_ref[rows, :]
            r = lax.rsqrt(jnp.mean(xv * xv, axis=-1, keepdims=True) + EPS)
            xn = xv * r
            e = xn * g_ref[...] - t_ref[rows, :]
            l_ref[...] += _rows8(e * e)
            dy = e * (1.0 / d)
            dg_ref[...] += _rows8(dy * xn)
            dn = dy * g_ref[...]
            dx = r * (dn - xn * jnp.mean(dn * xn, axis=-1, keepdims=True))
            dx_ref[rows, :] = dx
            dxb_ref[rows, :] = dx.astype(BF16)
        _row_loop(t, 64, blk)

    row = pl.BlockSpec((t, d), lambda i: (i, 0))
    part = pl.BlockSpec((SUB, d), lambda i: (0, 0))
    return pl.pallas_call(
        body, grid=(s_len // t,),
        in_specs=[row, _resident((1, d), lambda i: (0, 0)), row],
        out_specs=[part, row, row, part],
        out_shape=[jax.ShapeDtypeStruct((SUB, d), F32), jax.ShapeDtypeStruct((s_len, d), F32),
                   jax.ShapeDtypeStruct((s_len, d), BF16), jax.ShapeDtypeStruct((SUB, d), F32)],
        compiler_params=_params("arbitrary"), name=name,
    )(x, g, target)


def _ffn_bwd(dx2, uf, cg, cv, wf, w_down, *, name, dep=None):
    s_len, ff2 = uf.shape
    ff = ff2 // 2
    d = dx2.shape[1]
    t = _seq_tile(s_len)
    n_t = s_len // t
    fc = _ff_chunk(ff)
    nc = ff // fc

    def body(dx_ref, ug_ref, uv_ref, cg_ref, cv_ref, wfg_ref, wfv_ref, wd_ref, dep_ref,
             duf_ref, dwg_ref, dwv_ref, dact, dgw, dvw, awg, awv, wrep_g, wrep_v):
        i = pl.program_id(1)

        @pl.when(i == 0)
        def _():
            dgw[t:t + HALO3, :] = jnp.zeros((HALO3, fc), F32)
            dvw[t:t + HALO3, :] = jnp.zeros((HALO3, fc), F32)
            awg[...] = jnp.zeros_like(awg)
            awv[...] = jnp.zeros_like(awv)

        _replicate_taps(wfg_ref, wrep_g, SHORT_K)
        _replicate_taps(wfv_ref, wrep_v, SHORT_K)

        def blk(rows):
            gv = cg_ref[rows, :].astype(F32)
            sg = _sigmoid(gv)
            da = dact[rows, :]
            dgw[rows, :] = (da * cv_ref[rows, :].astype(F32)) * (sg * (1.0 + gv * (1.0 - sg)))
            dvw[rows, :] = da * (gv * sg)

        dact[...] = _dot_nt(dx_ref[...], wd_ref[...])
        _row_loop(t, 32, blk, unroll=2)

        _conv_bwd_taps(dgw, wrep_g, ug_ref, duf_ref.at[0], awg, taps=SHORT_K, n_rows=t, width=fc)
        _conv_bwd_taps(dvw, wrep_v, uv_ref, duf_ref.at[1], awv, taps=SHORT_K, n_rows=t, width=fc)
        dgw[t:t + HALO3, :] = dgw[0:HALO3, :]
        dvw[t:t + HALO3, :] = dvw[0:HALO3, :]

        @pl.when(i == n_t - 1)
        def _():
            dwg_ref[...] = _fold8(awg, SHORT_K)
            dwv_ref[...] = _fold8(awv, SHORT_K)

    rev = lambda i: n_t - 1 - i
    gate = pl.BlockSpec((t, fc), lambda j, i: (rev(i), j))
    value = pl.BlockSpec((t, fc), lambda j, i: (rev(i), j + nc))
    return pl.pallas_call(
        body, grid=(nc, n_t),
        in_specs=[pl.BlockSpec((t, d), lambda j, i: (rev(i), 0)), gate, value, gate, gate,
                  pl.BlockSpec((SHORT_K, fc), lambda j, i: (0, j)),
                  pl.BlockSpec((SHORT_K, fc), lambda j, i: (0, j + nc)),
                  pl.BlockSpec((fc, d), lambda j, i: (j, 0)), ANY],
        out_specs=[pl.BlockSpec((2, t, fc), lambda j, i: (0, rev(i), j)),
                   pl.BlockSpec((SHORT_K, fc), lambda j, i: (0, j)), pl.BlockSpec((SHORT_K, fc), lambda j, i: (0, j))],
        out_shape=[jax.ShapeDtypeStruct((2, s_len, ff), BF16),
                   jax.ShapeDtypeStruct((SHORT_K, ff), F32), jax.ShapeDtypeStruct((SHORT_K, ff), F32)],
        scratch_shapes=[pltpu.VMEM((t, fc), F32),
                        pltpu.VMEM((t + HALO3, fc), F32), pltpu.VMEM((t + HALO3, fc), F32),
                        pltpu.VMEM((SHORT_K * SUB, fc), F32), pltpu.VMEM((SHORT_K * SUB, fc), F32),
                        pltpu.VMEM((SHORT_K * SUB, fc), F32), pltpu.VMEM((SHORT_K * SUB, fc), F32)],
        compiler_params=_params("arbitrary", "arbitrary"), name=name,
    )(dx2, uf, uf, cg, cv, wf, wf, w_down, uf if dep is None else dep)


def _mix_bwd(dx1, u, ca, cb, wa, lg, lb, wb, w_out, *, name, dep=None):
    s_len, d_in = u.shape
    d = dx1.shape[1]
    c = D_CONF
    t = _seq_tile(s_len)
    n_t = s_len // t

    def body(dx_ref, u_ref, ca_ref, cb_ref, wa_ref, lg_ref, lb_ref, wb_ref, wo_ref, dep_ref,
             du_ref, dwa_ref, dwb_ref, dba_ref, dlg_ref, dlb_ref, dbin_ref,
             glu, prod, dyc, dcaw, dcbw, dglu, dp, awa, awb, wrep_a, wrep_b, shf):
        i = pl.program_id(0)
        _replicate_taps(wa_ref, wrep_a, CONF_K)
        _replicate_taps(wb_ref, wrep_b, SHORT_K)

        @pl.when(i == 0)
        def _():
            dcaw[t:t + HALO, :] = jnp.zeros((HALO, c), F32)
            dcbw[t:t + HALO3, :] = jnp.zeros((HALO3, c), F32)
            awa[...] = jnp.zeros_like(awa)
            awb[...] = jnp.zeros_like(awb)
            dba_ref[...] = jnp.zeros_like(dba_ref)
            dlg_ref[...] = jnp.zeros_like(dlg_ref)
            dlb_ref[...] = jnp.zeros_like(dlb_ref)
            dbin_ref[...] = jnp.zeros_like(dbin_ref)

        def blk1(rows):
            cv = ca_ref[rows, :]
            mu = jnp.mean(cv, axis=-1, keepdims=True)
            xc = cv - mu
            rstd = lax.rsqrt(jnp.mean(xc * xc, axis=-1, keepdims=True) + EPS)
            nrm = xc * rstd
            ln = nrm * lg_ref[...] + lb_ref[...]
            sg = _sigmoid(ln)
            dln = dyc[rows, 0:c] * (sg * (1.0 + ln * (1.0 - sg)))
            dlg_ref[...] += _rows8(dln * nrm)
            dlb_ref[...] += _rows8(dln)
            dn = dln * lg_ref[...]
            dca = rstd * (dn - jnp.mean(dn, axis=-1, keepdims=True)
                          - nrm * jnp.mean(dn * nrm, axis=-1, keepdims=True))
            dcaw[rows, :] = dca
            dba_ref[...] += _rows8(dca)
            ds = dyc[rows, c:2 * c]
            dgb = ds * cb_ref[rows, :]
            dcbw[rows, :] = ds * u_ref[rows, 2 * c:3 * c].astype(F32)
            du_ref[rows, 2 * c:3 * c] = dgb.astype(BF16)
            dbin_ref[:, 2 * c:3 * c] += _rows8(dgb)
            glu[rows, :] = u_ref[rows, 0:c].astype(F32) * _sigmoid(u_ref[rows, c:2 * c].astype(F32))
            prod[rows, :] = u_ref[rows, 3 * c:4 * c].astype(F32) * u_ref[rows, 4 * c:5 * c].astype(F32)
        half = t // 2
        rb = min(64, half)
        for lo in range(0, t, half):
            dyc[lo:lo + half, :] = _dot_nt(dx_ref[lo:lo + half, :], wo_ref[...])
            for r0 in range(lo, lo + half, rb):
                blk1(pl.ds(r0, rb))

        _conv_bwd_taps(dcaw, wrep_a, glu, dglu, awa, taps=CONF_K, n_rows=t, width=c, shf=shf)
        _conv_bwd_taps(dcbw, wrep_b, prod, dp, awb, taps=SHORT_K, n_rows=t, width=c)
        dcaw[t:t + HALO, :] = dcaw[0:HALO, :]
        dcbw[t:t + HALO3, :] = dcbw[0:HALO3, :]

        def blk2(rows):
            av = u_ref[rows, 0:c].astype(F32)
            sg = _sigmoid(u_ref[rows, c:2 * c].astype(F32))
            dg = dglu[rows, :]
            d_av = dg * sg
            d_ag = (dg * av) * (sg * (1.0 - sg))
            dpv = dp[rows, :]
            d_gc = dpv * u_ref[rows, 4 * c:5 * c].astype(F32)
            d_vs = dpv * u_ref[rows, 3 * c:4 * c].astype(F32)
            du_ref[rows, 0:c] = d_av.astype(BF16)
            du_ref[rows, c:2 * c] = d_ag.astype(BF16)
            du_ref[rows, 3 * c:4 * c] = d_gc.astype(BF16)
            du_ref[rows, 4 * c:5 * c] = d_vs.astype(BF16)
            dbin_ref[:, 0:c] += _rows8(d_av)
            dbin_ref[:, c:2 * c] += _rows8(d_ag)
            dbin_ref[:, 3 * c:4 * c] += _rows8(d_gc)
            dbin_ref[:, 4 * c:5 * c] += _rows8(d_vs)
        _row_loop(t, 64, blk2)

        @pl.when(i == n_t - 1)
        def _():
            dwa_ref[...] = _fold8(awa, CONF_K)
            dwb_ref[...] = _fold8(awb, SHORT_K)

    rev = lambda i: n_t - 1 - i
    small_in = lambda r: _resident((r, c), lambda i: (0, 0))
    small = lambda r: pl.BlockSpec((r, c), lambda i: (0, 0))
    return pl.pallas_call(
        body, grid=(n_t,),
        in_specs=[pl.BlockSpec((t, d), lambda i: (rev(i), 0)),
                  pl.BlockSpec((t, d_in), lambda i: (rev(i), 0)),
                  pl.BlockSpec((t, c), lambda i: (rev(i), 0)), pl.BlockSpec((t, c), lambda i: (rev(i), 0)),
                  small_in(CONF_K), small_in(1), small_in(1), small_in(SHORT_K),
                  _resident((2 * c, d), lambda i: (0, 0)), ANY],
        out_specs=[pl.BlockSpec((t, d_in), lambda i: (rev(i), 0)),
                   small(CONF_K), small(SHORT_K), small(SUB), small(SUB), small(SUB),
                   pl.BlockSpec((SUB, d_in), lambda i: (0, 0))],
        out_shape=[jax.ShapeDtypeStruct((s_len, d_in), BF16),
                   jax.ShapeDtypeStruct((CONF_K, c), F32), jax.ShapeDtypeStruct((SHORT_K, c), F32),
                   jax.ShapeDtypeStruct((SUB, c), F32), jax.ShapeDtypeStruct((SUB, c), F32),
                   jax.ShapeDtypeStruct((SUB, c), F32), jax.ShapeDtypeStruct((SUB, d_in), F32)],
        scratch_shapes=[pltpu.VMEM((t, c), F32), pltpu.VMEM((t, c), F32), pltpu.VMEM((t, 2 * c), F32),
                        pltpu.VMEM((t + HALO, c), F32), pltpu.VMEM((t + HALO3, c), F32),
                        pltpu.VMEM((t, c), F32), pltpu.VMEM((t, c), F32),
                        pltpu.VMEM((CONF_K * SUB, c), F32), pltpu.VMEM((SHORT_K * SUB, c), F32),
                        pltpu.VMEM((CONF_K * SUB, c), F32), pltpu.VMEM((SHORT_K * SUB, c), F32),
                        pltpu.VMEM((SUB - 1, t + HALO, c), F32)],
        compiler_params=_params("arbitrary"), name=name,
    )(dx1, u, ca, cb, wa, lg, lb, wb, w_out, u if dep is None else dep)


def _matmul_tn(a, b, *, name):
    n_p, s_len, k = a.shape
    n = b.shape[1]
    tk = _col_tile(k)
    per = k // tk

    def body(a_ref, b_ref, o_ref):
        o_ref[...] = _dot_tn(a_ref[...], b_ref[...]).astype(BF16)

    return pl.pallas_call(
        body, grid=(n_p, per),
        in_specs=[pl.BlockSpec((None, s_len, tk), lambda p, j: (p, 0, j)), _resident((s_len, n), lambda p, j: (0, 0))],
        out_specs=pl.BlockSpec((tk, n), lambda p, j: (p * per + j, 0)),
        out_shape=jax.ShapeDtypeStruct((n_p * k, n), BF16),
        compiler_params=_params("parallel", "parallel"), name=name,
    )(a, b)


def _matmul_rmsbwd(dzs, wt, x, g, dx_in, *, name, dep=None):
    s_len, d = x.shape
    n_z, _, nj = dzs.shape
    t = _mm_tile(s_len)

    def body(*refs):
        dz_refs = refs[0:n_z]
        w_refs = refs[n_z:2 * n_z]
        x_ref, g_ref, dxi_ref, _, dx_ref, dxb_ref, dg_ref, dh = refs[2 * n_z:]

        @pl.when(pl.program_id(0) == 0)
        def _():
            dg_ref[...] = jnp.zeros_like(dg_ref)

        def blk(rows):
            xv = x_ref[rows, :]
            r = lax.rsqrt(jnp.mean(xv * xv, axis=-1, keepdims=True) + EPS)
            xn = xv * r
            dhv = dh[rows, :]
            dg_ref[...] += _rows8(dhv * xn)
            dn = dhv * g_ref[...]
            dx = dxi_ref[rows, :] + r * (dn - xn * jnp.mean(dn * xn, axis=-1, keepdims=True))
            dx_ref[rows, :] = dx
            dxb_ref[rows, :] = dx.astype(BF16)

        half = t // 2
        rb = min(128, half)
        for lo in range(0, t, half):
            acc = _dot(dz_refs[0][lo:lo + half, :], w_refs[0][...])
            for q in range(1, n_z):
                acc = acc + _dot(dz_refs[q][lo:lo + half, :], w_refs[q][...])
            dh[lo:lo + half, :] = acc
            for r0 in range(lo, lo + half, rb):
                blk(pl.ds(r0, rb))

    row = pl.BlockSpec((t, d), lambda i: (i, 0))
    in_specs = [pl.BlockSpec((None, t, nj), functools.partial(lambda q, i: (q, i, 0), q)) for q in range(n_z)]
    in_specs += [_resident((nj, d), functools.partial(lambda q, i: (q, 0), q)) for q in range(n_z)]
    in_specs += [row, _resident((1, d), lambda i: (0, 0)), row, ANY]
    return pl.pallas_call(
        body, grid=(s_len // t,), in_specs=in_specs,
        out_specs=[row, row, pl.BlockSpec((SUB, d), lambda i: (0, 0))],
        out_shape=[jax.ShapeDtypeStruct((s_len, d), F32), jax.ShapeDtypeStruct((s_len, d), BF16),
                   jax.ShapeDtypeStruct((SUB, d), F32)],
        scratch_shapes=[pltpu.VMEM((t, d), F32)],
        compiler_params=_params("arbitrary"), name=name,
    )(*([dzs] * n_z), *([wt] * n_z), x, g, dx_in, x if dep is None else dep)


def _row(v):
    return v.reshape(1, -1)


def _layer_fwd(x0, p, tag, dep=None, before_up=None):
    u, h1 = _rms_matmul(x0, _row(p["mix_norm_g"]), p["w_in_t"], _row(p["b_in"]), name=f"in_proj_{tag}", dep=dep)
    ycat, x1, ca, cb = _mix_fwd(u, x0, p["conv_a_w"], _row(p["conv_a_b"]), _row(p["ln_a_g"]), _row(p["ln_a_b"]),
                            p["conv_b_w"], p["w_out"], name=f"mix_fwd_{tag}")
    if before_up is not None:
        before_up(x1)
    uf, h2 = _rms_matmul(x1, _row(p["ffn_norm_g"]), p["w_up_t"], None, name=f"up_proj_{tag}")
    act, x2, cg, cv = _ffn_fwd(uf, x1, p["conv_f_w"], p["w_down"], name=f"ffn_fwd_{tag}")
    return x2, dict(x0=x0, h1=h1, u=u, ca=ca, cb=cb, ycat=ycat, x1=x1, h2=h2, uf=uf, cg=cg, cv=cv, act=act)


def _layer_bwd(dx2, dx2_b, p, saved, tag, ffn_grads, ffn_sent, mix_grads, mix_sent, dep=None):
    d_uf, dwf_g, dwf_v = _ffn_bwd(dx2_b, saved["uf"], saved["cg"], saved["cv"], p["conv_f_w"], p["w_down"],
                                  name=f"ffn_bwd_{tag}", dep=dep)
    g_down = _matmul_tn(saved["act"][None], dx2_b, name=f"dw_down_{tag}")
    g_up = _matmul_tn(d_uf, saved["h2"], name=f"dw_up_{tag}")
    dep_ffn = ffn_grads(dict(w_up=g_up, w_down=g_down), dx2_b)
    dx1, dx1_b, dg2 = _matmul_rmsbwd(d_uf, p["w_up_t"], saved["x1"], _row(p["ffn_norm_g"]), dx2,
                                     name=f"dh_ffn_{tag}", dep=dep_ffn)
    du, dwa, dwb, dba, dlg, dlb, dbin = _mix_bwd(
        dx1_b, saved["u"], saved["ca"], saved["cb"], p["conv_a_w"], _row(p["ln_a_g"]), _row(p["ln_a_b"]),
        p["conv_b_w"], p["w_out"], name=f"mix_bwd_{tag}", dep=ffn_sent(dx1_b))
    g_out = _matmul_tn(saved["ycat"][None], dx1_b, name=f"dw_out_{tag}")
    g_in = _matmul_tn(du[None], saved["h1"], name=f"dw_in_{tag}")
    conv = dict(conv_a_w=dwa, conv_b_w=dwb, conv_f_w=jnp.concatenate([dwf_g, dwf_v], axis=1))
    dep_mix = mix_grads(dict(w_in=g_in, w_out=g_out), conv, dx1_b)
    dx0, dx0_b, dg1 = _matmul_rmsbwd(du[None], p["w_in_t"], saved["x0"], _row(p["mix_norm_g"]), dx1,
                                     name=f"dh_mix_{tag}", dep=dep_mix)
    rep = dict(mix_norm_g=dg1, b_in=dbin, conv_a_b=dba, ln_a_g=dlg, ln_a_b=dlb, ffn_norm_g=dg2)
    return dx0, dx0_b, rep, mix_sent(dx0_b)


def _place():
    return lax.axis_index("x"), lax.axis_index("y"), lax.axis_index("c")


def _all_gather(arrs, *, name):
    n_a = len(arrs)

    def body(*refs):
        ins = refs[0:n_a]
        outs = refs[n_a:2 * n_a]
        send_sems, recv_sems, local_sems = refs[2 * n_a:]
        x, y, c = _place()
        sibling = (x, y, 1 - c)
        chips = [(1 - x, y), (x, 1 - y), (1 - x, 1 - y)]

        def slot(a, px, py, pc):
            return outs[a].at[4 * px + 2 * py + pc]

        def copy(a, k, block, to, src=None):
            return pltpu.make_async_remote_copy(
                src_ref=slot(a, *block) if src is None else src, dst_ref=slot(a, *block),
                send_sem=send_sems.at[a, k], recv_sem=recv_sems.at[a, k],
                device_id=to, device_id_type=MESH)

        me = (x, y, c)
        mine = [pltpu.make_async_copy(ins[a], slot(a, *me), local_sems.at[a]) for a in range(n_a)]
        for cp in mine:
            cp.start()
        started = []
        for a in range(n_a):
            first = [copy(a, 0, me, sibling, src=ins[a])]
            first += [copy(a, 1 + j, me, (*chip, c), src=ins[a]) for j, chip in enumerate(chips)]
            for cp in first:
                cp.start()
            started += first
        for a in range(n_a):
            for j, chip in enumerate(chips):
                copy(a, 1 + j, (*chip, c), me).wait_recv()
                passed = copy(a, 4 + j, (*chip, c), sibling)
                passed.start()
                started.append(passed)
        for a in range(n_a):
            copy(a, 0, sibling, me).wait_recv()
            for j, chip in enumerate(chips):
                copy(a, 4 + j, (*chip, 1 - c), me).wait_recv()
        for cp in started:
            cp.wait_send()
        for cp in mine:
            cp.wait()

    return pl.pallas_call(
        body, in_specs=[ANY] * n_a, out_specs=[ANY] * n_a,
        out_shape=[jax.ShapeDtypeStruct((N_DEV, *a.shape), a.dtype) for a in arrs],
        scratch_shapes=[pltpu.SemaphoreType.DMA((n_a, 7)), pltpu.SemaphoreType.DMA((n_a, 7)),
                        pltpu.SemaphoreType.DMA((n_a,))],
        name=name,
    )(*arrs)


def _row_tile(r, cap):
    for tr in range(min(cap, r) // 16 * 16, 0, -16):
        if r % tr == 0:
            return tr
    return r


def _pair_sum(mines, theirs, where, *, name):
    n_a = len(mines)
    n_chip = mines[0].shape[0]

    def body(where_ref, *refs):
        a_refs = refs[0:n_a]
        b_refs = refs[n_a:2 * n_a]
        p_refs = refs[2 * n_a:3 * n_a]
        l_refs = refs[3 * n_a:4 * n_a]
        q = pl.program_id(0)
        for a in range(n_a):
            p_refs[a][...] = (a_refs[a][...].astype(F32) + b_refs[a][...].astype(F32)).astype(p_refs[a].dtype)

        @pl.when(q == where_ref[1])
        def _():
            for a in range(n_a):
                l_refs[a][...] = p_refs[a][...]

    in_specs, out_p, out_l, shapes = [], [], [], []
    for m in mines:
        _, _, r, c = m.shape
        in_specs.append(pl.BlockSpec((None, None, r, c), lambda q, where_ref: (q, where_ref[0], 0, 0)))
    for m in mines:
        _, _, r, c = m.shape
        in_specs.append(pl.BlockSpec((None, r, c), lambda q, where_ref: (q, 0, 0)))
        out_p.append(pl.BlockSpec((None, r, c), lambda q, where_ref: (q, 0, 0)))
        out_l.append(pl.BlockSpec((None, r, c), lambda q, where_ref: (where_ref[1], 0, 0)))
        shapes.append(jax.ShapeDtypeStruct((n_chip, r, c), m.dtype))
    res = pl.pallas_call(
        body,
        grid_spec=pltpu.PrefetchScalarGridSpec(num_scalar_prefetch=1, grid=(n_chip,), in_specs=in_specs,
                                               out_specs=out_p + out_l),
        out_shape=shapes + shapes,
        compiler_params=_params("arbitrary"), name=name,
    )(where, *mines, *theirs)
    return list(res[:n_a]), list(res[n_a:])


HBM = pl.BlockSpec(memory_space=pltpu.HBM)
SEM = pl.BlockSpec(memory_space=pltpu.SEMAPHORE)
EFFECT = pltpu.SideEffectType.DATAFLOW_SIDE_EFFECTING


def _in_hbm(a):
    return pltpu.with_memory_space_constraint(a, pltpu.HBM)


def _split_start(srcs, lands, plan, n_copies, after, *, name):
    n_s, n_l = len(srcs), len(lands)

    def body(*refs):
        src_refs = refs[0:n_s]
        land_refs = refs[n_s:n_s + n_l]
        send_sems, recv_sems = refs[n_s + n_l + 1], refs[n_s + n_l + 2]
        token = refs[-1]
        for cp in plan(src_refs, land_refs, send_sems, recv_sems):
            cp.start()
        token[...] = jnp.zeros_like(token)

    thru = [pltpu.HBM(a.shape, a.dtype) for a in list(srcs) + list(lands)]
    res = pl.pallas_call(
        body, name=name,
        out_shape=(pltpu.SemaphoreType.DMA((n_copies,)), pltpu.SemaphoreType.DMA((n_copies,)), *thru,
                   jax.ShapeDtypeStruct((SUB, LANES), F32)),
        in_specs=[HBM] * (n_s + n_l) + [ANY],
        out_specs=(SEM, SEM, *([HBM] * (n_s + n_l)), pl.BlockSpec(memory_space=pltpu.VMEM)),
        input_output_aliases={i: 2 + i for i in range(n_s + n_l)},
        compiler_params=pltpu.CompilerParams(has_side_effects=EFFECT),
    )(*[_in_hbm(a) for a in srcs], *[_in_hbm(a) for a in lands], _in_hbm(after))
    return res[0], res[1], list(res[2:2 + n_s]), list(res[2 + n_s:2 + n_s + n_l]), res[-1]


def _split_wait(send_sems, recv_sems, srcs, lands, after, plan, *, name):
    n_s, n_l = len(srcs), len(lands)

    def body(*refs):
        src_refs = refs[0:n_s]
        land_refs = refs[n_s:n_s + n_l]
        send, recv = refs[n_s + n_l], refs[n_s + n_l + 1]
        for cp in plan(src_refs, land_refs, send, recv):
            cp.wait_send()
            cp.wait_recv()

    res = pl.pallas_call(
        body, name=name,
        out_shape=tuple(pltpu.HBM(a.shape, a.dtype) for a in list(srcs) + list(lands)),
        in_specs=[HBM] * (n_s + n_l) + [SEM, SEM, ANY],
        out_specs=tuple([HBM] * (n_s + n_l)),
        input_output_aliases={i: i for i in range(n_s + n_l)},
        compiler_params=pltpu.CompilerParams(has_side_effects=EFFECT),
    )(*srcs, *lands, send_sems, recv_sems, _in_hbm(after))
    return list(res[:n_s]), list(res[n_s:])


def _remote(src, dst, send_sems, recv_sems, k, to):
    return pltpu.make_async_remote_copy(src_ref=src, dst_ref=dst, send_sem=send_sems.at[k], recv_sem=recv_sems.at[k],
                                        device_id=to, device_id_type=MESH)


def _gather_plan_first(src_refs, land_refs, send_sems, recv_sems):
    x, y, c = _place()
    me = 4 * x + 2 * y + c
    peers = [(x, y, 1 - c), (1 - x, y, c), (x, 1 - y, c), (1 - x, 1 - y, c)]
    return [_remote(src, land.at[me], send_sems, recv_sems, 4 * a + k, to)
            for a, (src, land) in enumerate(zip(src_refs, land_refs)) for k, to in enumerate(peers)]


def _gather_plan_second(src_refs, land_refs, send_sems, recv_sems):
    x, y, c = _place()
    chips = [(1 - x, y), (x, 1 - y), (1 - x, 1 - y)]
    out = []
    for a, land in enumerate(land_refs):
        for j, (px, py) in enumerate(chips):
            slot = land.at[4 * px + 2 * py + c]
            out.append(_remote(slot, slot, send_sems, recv_sems, 3 * a + j, (x, y, 1 - c)))
    return out


def _siblings_plan(src_refs, land_refs, send_sems, recv_sems):
    x, y, c = _place()
    return [_remote(src.at[:, 1 - c], land, send_sems, recv_sems, a, (x, y, 1 - c))
            for a, (src, land) in enumerate(zip(src_refs, land_refs))]


def _chips_plan(src_refs, land_refs, send_sems, recv_sems):
    x, y, c = _place()
    my_chip = 2 * x + y
    chips = [(1 - x, y), (x, 1 - y), (1 - x, 1 - y)]
    return [_remote(src.at[2 * px + py], land.at[my_chip], send_sems, recv_sems, 3 * a + j, (px, py, c))
            for a, (src, land) in enumerate(zip(src_refs, land_refs)) for j, (px, py) in enumerate(chips)]


def _gather_landings(shards, me, *, name):
    blank = _unwritten([jax.ShapeDtypeStruct((N_DEV, *s.shape), s.dtype) for s in shards], name=name)
    return [lax.dynamic_update_index_in_dim(b, s, me, 0) for b, s in zip(blank, shards)]


def _adamw_math(g, w, m, v):
    m = ADAM_B1 * m + (1.0 - ADAM_B1) * g
    v = ADAM_B2 * v + (1.0 - ADAM_B2) * (g * g)
    m_hat = m / (1.0 - ADAM_B1 ** ADAM_STEP)
    v_hat = v / (1.0 - ADAM_B2 ** ADAM_STEP)
    delta = -ADAM_LR * (m_hat / (jnp.sqrt(v_hat) + ADAM_EPS) + ADAM_WD * w)
    return delta, m, v


def _adamw_sharded(parts, w, m, v, *, name, dep=None):
    n_layers, r, c = w.shape
    n_chip = parts[0].shape[0]
    tr = _row_tile(r, 384)
    n_i = r // tr

    def body(*refs):
        p_refs = refs[0:n_layers]
        w_ref, m_ref, v_ref, _, g_out, d_out, m_out, v_out = refs[n_layers:]
        layer = pl.program_id(0)
        for l in range(n_layers):
            @pl.when(layer == l)
            def _(l=l):
                g = p_refs[l][0].astype(F32)
                for q in range(1, n_chip):
                    g = g + p_refs[l][q].astype(F32)
                delta, m_new, v_new = _adamw_math(g, w_ref[...], m_ref[...], v_ref[...])
                g_out[...] = g
                d_out[...] = delta
                m_out[...] = m_new
                v_out[...] = v_new

    def part_map(l):
        return lambda layer, i: (0, jnp.where(layer == l, i, jnp.where(layer < l, 0, n_i - 1)), 0)

    blk = pl.BlockSpec((None, tr, c), lambda layer, i: (layer, i, 0))
    return pl.pallas_call(
        body, grid=(n_layers, n_i),
        in_specs=[pl.BlockSpec((n_chip, tr, c), part_map(l)) for l in range(n_layers)] + [blk, blk, blk, ANY],
        out_specs=[blk] * 4, out_shape=[jax.ShapeDtypeStruct((n_layers, r, c), F32)] * 4,
        compiler_params=_params("arbitrary", "arbitrary"), name=name,
    )(*parts, w, m, v, w if dep is None else dep)


def _fold_partials(cols, *, name):
    widths = [c.shape[1] for c in cols]

    def body(*refs):
        o_ref = refs[-1]
        pos = 0
        for ref, width in zip(refs[:-1], widths):
            o_ref[:, pos:pos + width] = jnp.sum(ref[...], axis=0, keepdims=True)
            pos += width

    return pl.pallas_call(body, out_shape=jax.ShapeDtypeStruct((1, sum(widths)), F32), name=name)(*cols)


def _adamw_replicated(parts, names, w, m, v, n_loss, *, name):
    n_dev = parts.shape[0]
    n_layers = w[names[0]].shape[0]
    every = list(names) + ["final_norm_g"]
    n_p = len(every)

    def body(*refs):
        p_ref = refs[0]
        w_refs = dict(zip(every, refs[1:1 + n_p]))
        m_refs = dict(zip(every, refs[1 + n_p:1 + 2 * n_p]))
        v_refs = dict(zip(every, refs[1 + 2 * n_p:1 + 3 * n_p]))
        l_out = refs[1 + 3 * n_p]
        outs = refs[2 + 3 * n_p:]
        o_refs = {n: outs[4 * q:4 * q + 4] for q, n in enumerate(every)}
        acc = p_ref[0]
        for q in range(1, n_dev):
            acc = acc + p_ref[q]
        tot = jnp.sum(acc, axis=0, keepdims=True)
        pos = 0
        where = [(n, l) for l in range(n_layers) for n in names] + [("final_norm_g", 0)]
        for n, l in where:
            width = w_refs[n].shape[1]
            g = tot[:, pos:pos + width]
            pos += width
            row = pl.ds(l, 1)
            delta, m_new, v_new = _adamw_math(g, w_refs[n][row, :], m_refs[n][row, :], v_refs[n][row, :])
            for o, val in zip(o_refs[n], (g, delta, m_new, v_new)):
                o[row, :] = val
        l_out[...] = (0.5 / n_loss) * jnp.sum(tot[:, pos:pos + n_loss], axis=-1, keepdims=True)

    shapes = [jax.ShapeDtypeStruct((1, 1), F32)]
    for n in every:
        shapes += [jax.ShapeDtypeStruct(w[n].shape, F32)] * 4
    res = pl.pallas_call(
        body, out_shape=shapes,
        compiler_params=pltpu.CompilerParams(vmem_limit_bytes=VMEM_LIMIT), name=name,
    )(parts, *[w[n] for n in every], *[m[n] for n in every], *[v[n] for n in every])
    return res[0], {n: res[1 + 4 * q:5 + 4 * q] for q, n in enumerate(every)}


BIG = ("w_in", "w_out", "w_up", "w_down")
COL_SHARDED = ("w_in", "w_up")
CONV = ("conv_a_w", "conv_b_w", "conv_f_w")
REPLICATED = ("mix_norm_g", "b_in", "conv_a_b", "ln_a_g", "ln_a_b", "ffn_norm_g")
KINDS = ("grad", "delta", "m", "v")
FFN_PART = ("w_up", "w_down")
MIX_PART = ("w_in", "w_out")


def _weights_from_gathered(g):
    n_dev, r, c = g.shape
    return g.reshape(n_dev * r, c)


def _slabs_from_full(grad):
    return grad.reshape(N_DEV, grad.shape[0] // N_DEV, grad.shape[1])


def _unwritten(like, *, name):
    return pl.pallas_call(lambda *refs: None, out_specs=[ANY] * len(like), out_shape=list(like), name=name)()


def kernel(x, mix_norm_g, w_in, b_in, conv_a_w, conv_a_b, ln_a_g, ln_a_b, conv_b_w, w_out, ffn_norm_g, w_up, conv_f_w, w_down, final_norm_g, loss_target, m_mix_norm_g, m_w_in, m_b_in, m_conv_a_w, m_conv_a_b, m_ln_a_g, m_ln_a_b, m_conv_b_w, m_w_out, m_ffn_norm_g, m_w_up, m_conv_f_w, m_w_down, m_final_norm_g, v_mix_norm_g, v_w_in, v_b_in, v_conv_a_w, v_conv_a_b, v_ln_a_g, v_ln_a_b, v_conv_b_w, v_w_out, v_ffn_norm_g, v_w_up, v_conv_f_w, v_w_down, v_final_norm_g):
    w = dict(mix_norm_g=mix_norm_g, w_in=w_in, b_in=b_in, conv_a_w=conv_a_w, conv_a_b=conv_a_b, ln_a_g=ln_a_g,
             ln_a_b=ln_a_b, conv_b_w=conv_b_w, w_out=w_out, ffn_norm_g=ffn_norm_g, w_up=w_up, conv_f_w=conv_f_w,
             w_down=w_down, final_norm_g=final_norm_g)
    m = dict(mix_norm_g=m_mix_norm_g, w_in=m_w_in, b_in=m_b_in, conv_a_w=m_conv_a_w, conv_a_b=m_conv_a_b,
             ln_a_g=m_ln_a_g, ln_a_b=m_ln_a_b, conv_b_w=m_conv_b_w, w_out=m_w_out, ffn_norm_g=m_ffn_norm_g,
             w_up=m_w_up, conv_f_w=m_conv_f_w, w_down=m_w_down, final_norm_g=m_final_norm_g)
    v = dict(mix_norm_g=v_mix_norm_g, w_in=v_w_in, b_in=v_b_in, conv_a_w=v_conv_a_w, conv_a_b=v_conv_a_b,
             ln_a_g=v_ln_a_g, ln_a_b=v_ln_a_b, conv_b_w=v_conv_b_w, w_out=v_w_out, ffn_norm_g=v_ffn_norm_g,
             w_up=v_w_up, conv_f_w=v_conv_f_w, w_down=v_w_down, final_norm_g=v_final_norm_g)
    order = list(w)
    n_layers = w_in.shape[0]
    xs = x[0]
    target = loss_target[0]
    flip = lambda a: jnp.transpose(a, (0, 2, 1))
    wt, mt, vt = ({n: flip(d[n]) if n in COL_SHARDED else d[n] for n in BIG} for d in (w, m, v))
    px, py, pc = _place()
    where = jnp.stack([pc, 2 * px + py]).astype(jnp.int32)
    me = 4 * px + 2 * py + pc

    assert BIG == MIX_PART + FFN_PART
    key = lambda n: n + "_t" if n in COL_SHARDED else n
    shard = lambda n, l: wt[n][l].astype(BF16)

    def gather_start(names, l, after, tag):
        shards = [shard(n, l) for n in names]
        lands = _gather_landings(shards, me, name=f"gather_landing_{tag}")
        return _split_start(shards, lands, _gather_plan_first, 4 * len(shards), after, name=f"gather_first_start_{tag}")

    def gather_mid(first, after, tag):
        return _split_wait(first[0], first[1], first[2], first[3], after, _gather_plan_first,
                           name=f"gather_first_wait_{tag}")[1]

    def forward_start(lands, after, tag):
        return _split_start([], lands, _gather_plan_second, 3 * len(lands), after, name=f"gather_second_start_{tag}")

    def forward_finish(second, after, tag):
        return _split_wait(second[0], second[1], [], second[3], after, _gather_plan_second,
                           name=f"gather_second_wait_{tag}")[1]

    gathered = _all_gather([shard(n, 0) for n in MIX_PART] + [w[n] for n in CONV], name="gather_weights_0")
    params = [{n: w[n][l] for n in REPLICATED} for l in range(n_layers)]
    for n, g in zip(CONV, gathered[len(MIX_PART):]):
        n_dev, _, taps, c = g.shape
        full = g.transpose(1, 2, 0, 3).reshape(n_layers, taps, n_dev * c)
        for l in range(n_layers):
            params[l][n] = full[l]
    for n, g in zip(MIX_PART, gathered):
        params[0][key(n)] = _weights_from_gathered(g)
    ffn_first = gather_start(FFN_PART, 0, gathered[0], "0_ffn")
    pending = {}

    h = xs
    saved = []
    for l in range(n_layers):
        nxt = l + 1 if l + 1 < n_layers else None

        def before_up(x1, l=l, nxt=nxt):
            if l == 0:
                second = forward_start(gather_mid(ffn_first, x1, "0_ffn"), x1, "0_ffn")
                after = second[4]
            else:
                second = pending[l]["ffn"]
                after = x1
            if nxt is not None:
                pending[nxt] = dict(first=gather_start(BIG, nxt, after, str(nxt)))
                after = pending[nxt]["first"][4]
            for n, g in zip(FFN_PART, forward_finish(second, after, f"{l}_ffn")):
                params[l][key(n)] = _weights_from_gathered(g)

        h, keep = _layer_fwd(h, params[l], str(l), dep=ffn_first[4] if l == 0 else None, before_up=before_up)
        saved.append(keep)
        if nxt is not None:
            arrived = gather_mid(pending[nxt]["first"], h, str(nxt))
            mix_second = forward_start(arrived[:len(MIX_PART)], h, f"{nxt}_mix")
            pending[nxt]["ffn"] = forward_start(arrived[len(MIX_PART):], mix_second[4], f"{nxt}_ffn")
            for n, g in zip(MIX_PART, forward_finish(mix_second, pending[nxt]["ffn"][4], f"{nxt}_mix")):
                params[nxt][key(n)] = _weights_from_gathered(g)

    def start_siblings(slabs, after, tag):
        mines = [s.reshape(N_CHIP, 2, *s.shape[1:]) for s in slabs]
        lands = _unwritten([jax.ShapeDtypeStruct((N_CHIP, *m.shape[2:]), m.dtype) for m in mines],
                           name=f"reduce_siblings_landing_{tag}")
        return _split_start(mines, lands, _siblings_plan, len(mines), after, name=f"reduce_siblings_start_{tag}")

    def start_chips(sib, after, tag):
        mines, theirs = _split_wait(sib[0], sib[1], sib[2], sib[3], after, _siblings_plan,
                                    name=f"reduce_siblings_wait_{tag}")
        pairs, lands = _pair_sum(mines, theirs, where, name=f"pair_sum_{tag}")
        return _split_start(pairs, lands, _chips_plan, 3 * len(pairs), after, name=f"reduce_chips_start_{tag}")

    def finish_reduce(fly, after, tag):
        return _split_wait(fly[0], fly[1], fly[2], fly[3], after, _chips_plan, name=f"reduce_chips_wait_{tag}")[1]

    loss_sq, dh, dh_b, dgf = _loss_bwd(h, _row(final_norm_g), target, name="loss")
    conv_g = {n: [None] * n_layers for n in CONV}
    rep_g = [None] * n_layers
    siblings = {}
    flights = {}
    token = None
    for l in reversed(range(n_layers)):
        def ffn_grads(g, after, l=l):
            siblings[l, "ffn"] = start_siblings([_slabs_from_full(g[n]) for n in FFN_PART], after, f"{l}_ffn")
            return siblings[l, "ffn"][4]

        def ffn_sent(after, l=l):
            flights[l, "ffn"] = start_chips(siblings[l, "ffn"], after, f"{l}_ffn")
            return flights[l, "ffn"][4]

        def mix_grads(g, conv, after, l=l):
            for n in CONV:
                conv_g[n][l] = conv[n]
            slabs = [_slabs_from_full(g[n]) for n in MIX_PART]
            if l == 0:
                for n in CONV:
                    full = jnp.stack(conv_g[n])
                    _, taps, c = full.shape
                    slabs.append(full.reshape(n_layers, taps, N_DEV, c // N_DEV).transpose(2, 0, 1, 3)
                                 .reshape(N_DEV, n_layers * taps, c // N_DEV))
            siblings[l, "mix"] = start_siblings(slabs, after, f"{l}_mix")
            return siblings[l, "mix"][4]

        def mix_sent(after, l=l):
            flights[l, "mix"] = start_chips(siblings[l, "mix"], after, f"{l}_mix")
            return flights[l, "mix"][4]

        dh, dh_b, rep_g[l], token = _layer_bwd(dh, dh_b, params[l], saved[l], str(l), ffn_grads, ffn_sent,
                                               mix_grads, mix_sent, dep=token)

    sums = {key: finish_reduce(fly, dh, f"{key[0]}_{key[1]}") for key, fly in flights.items() if key != (0, "mix")}
    out = {k: {} for k in KINDS}

    def adamw_big(names, part, dep):
        for q, n in enumerate(names):
            layer_parts = [sums[l, part][q] for l in range(n_layers)]
            res = _adamw_sharded(layer_parts, wt[n], mt[n], vt[n], name=f"adamw_{n}", dep=dep)
            for k, r in zip(KINDS, res):
                out[k][n] = flip(r) if n in COL_SHARDED else r

    adamw_big(FFN_PART, "ffn", token)

    rep_cols = [rep_g[l][n] for l in range(n_layers) for n in REPLICATED] + [dgf, loss_sq]
    rep_all = _all_gather([_fold_partials(rep_cols, name="fold_small")], name="gather_small")[0]
    with_final = lambda d: {**{n: d[n] for n in REPLICATED}, "final_norm_g": _row(d["final_norm_g"])}
    loss, rep_res = _adamw_replicated(rep_all, REPLICATED, with_final(w), with_final(m), with_final(v),
                                      loss_sq.shape[1], name="adamw_small")
    for n, res in rep_res.items():
        for k, r in zip(KINDS, res):
            out[k][n] = r.reshape(w[n].shape)

    last = finish_reduce(flights[0, "mix"], rep_res["b_in"][0], "0_mix")
    sums[0, "mix"] = last[:len(MIX_PART)]
    adamw_big(MIX_PART, "mix", None)
    for n, p in zip(CONV, last[len(MIX_PART):]):
        as_one = lambda a: a.reshape(1, *p.shape[1:])
        for k, r in zip(KINDS, _adamw_sharded([p], as_one(w[n]), as_one(m[n]), as_one(v[n]), name=f"adamw_{n}")):
            out[k][n] = r.reshape(w[n].shape)

    grad_x = dh.reshape(x.shape)
    return (loss.reshape(()), grad_x, *[out["grad"][n] for n in order], *[out["delta"][n] for n in order],
            *[out["m"][n] for n in order], *[out["v"][n] for n in order])
```

```python
import functools

import jax
import jax.numpy as jnp
from jax import lax
from jax.experimental import pallas as pl
from jax.experimental.pallas import tpu as pltpu

F32 = jnp.float32
BF16 = jnp.bfloat16

N_DEV = 8
N_CHIP = 4
D_CONF = 512
CONF_K = 31
SHORT_K = 3
EPS = 1e-6
HALO = 32
HALO3 = 8
HALO3_BLK = 16
LANES = 128
SUB = 8
VMEM_LIMIT = 56 * 1024 * 1024

ADAM_LR = 0.001
ADAM_B1 = 0.9
ADAM_B2 = 0.999
ADAM_EPS = 1e-08
ADAM_WD = 0.01
ADAM_STEP = 10

MESH = pl.DeviceIdType.MESH
ANY = pl.BlockSpec(memory_space=pl.ANY)


def _params(*sem):
    return pltpu.CompilerParams(dimension_semantics=sem, vmem_limit_bytes=VMEM_LIMIT)


def _resident(shape, index_map):
    return pl.BlockSpec(shape, index_map, pipeline_mode=pl.Buffered(1))


def _row_loop(n_rows, rb, fn, unroll=1):
    rb = min(rb, n_rows)

    def body(i, carry):
        fn(pl.ds(pl.multiple_of(i * rb, rb), rb))
        return carry
    lax.fori_loop(0, n_rows // rb, body, 0, unroll=unroll)


def _rows8(v):
    acc = v[0:SUB]
    for k in range(1, v.shape[0] // SUB):
        acc = acc + v[k * SUB:(k + 1) * SUB]
    return acc


def _sigmoid(z):
    return 0.5 * jnp.tanh(0.5 * z) + 0.5


def _dot(a, b):
    return jnp.dot(a, b, preferred_element_type=F32)


def _dot_nt(a, b):
    return lax.dot_general(a, b, (((1,), (1,)), ((), ())), preferred_element_type=F32)


def _dot_tn(a, b):
    return lax.dot_general(a, b, (((0,), (0,)), ((), ())), preferred_element_type=F32)


def _replicate_taps(w_ref, wrep, taps):
    for k in range(taps):
        wrep[pl.ds(k * SUB, SUB), :] = jnp.broadcast_to(w_ref[pl.ds(k, 1), :], (SUB, w_ref.shape[1]))


def _shift_copies(win, shf, lanes):
    span = win.shape[0] - SUB
    for r in range(1, SUB):
        for j0 in range(0, span, 64):
            n = min(64, span - j0)
            shf[r - 1, pl.ds(j0, n), lanes] = win[pl.ds(j0 + r, n), lanes]


def _rows_at(win, shf, off, rb, lanes):
    if shf is None or off % SUB == 0:
        return win[pl.ds(off, rb), lanes]
    return shf[off % SUB - 1, pl.ds(off - off % SUB, rb), lanes]


def _conv_taps(win, wrep, out, *, taps, n_rows, base, width, transposed=False, bias_ref=None, shf=None):
    rb = min(64, n_rows)

    def lane_body(cb, carry):
        lanes = pl.ds(pl.multiple_of(cb * LANES, LANES), LANES)
        if shf is not None:
            _shift_copies(win, shf, lanes)
        for r0 in range(0, n_rows, rb):
            acc = None
            for k in range(taps):
                off = (taps - 1 - k) if transposed else (k - (taps - 1))
                wk = jnp.tile(wrep[pl.ds(k * SUB, SUB), lanes], (rb // SUB, 1))
                term = wk * _rows_at(win, shf, base + r0 + off, rb, lanes)
                acc = term if acc is None else acc + term
            if bias_ref is not None:
                acc = acc + bias_ref[:, lanes]
            out[pl.ds(r0, rb), lanes] = acc.astype(out.dtype)
        return carry

    lax.fori_loop(0, width // LANES, lane_body, 0)


def _conv_bwd_taps(win, wrep, x_cur, dx_out, dw_acc, *, taps, n_rows, width, shf=None):
    rb = min(32 if taps > 8 else 64, n_rows)

    def lane_body(cb, carry):
        lanes = pl.ds(pl.multiple_of(cb * LANES, LANES), LANES)
        if shf is not None:
            _shift_copies(win, shf, lanes)
        sums = [None] * taps
        for r0 in range(0, n_rows, rb):
            xv = x_cur[pl.ds(r0, rb), lanes].astype(F32)
            acc = None
            for k in range(taps):
                shifted = _rows_at(win, shf, r0 + taps - 1 - k, rb, lanes)
                term = jnp.tile(wrep[pl.ds(k * SUB, SUB), lanes], (rb // SUB, 1)) * shifted
                acc = term if acc is None else acc + term
                part = _rows8(xv * shifted)
                sums[k] = part if sums[k] is None else sums[k] + part
            dx_out[pl.ds(r0, rb), lanes] = acc.astype(dx_out.dtype)
        for k in range(taps):
            dw_acc[pl.ds(k * SUB, SUB), lanes] += sums[k]
        return carry

    lax.fori_loop(0, width // LANES, lane_body, 0)


def _fold8(acc_ref, taps):
    return jnp.concatenate(
        [jnp.sum(acc_ref[pl.ds(k * SUB, SUB), :], axis=0, keepdims=True) for k in range(taps)], axis=0)


def _seq_tile(s_len):
    return min(512, s_len)


def _mm_tile(s_len):
    return min(512, s_len)


def _ff_chunk(ff):
    best = LANES
    for c in range(LANES, 1408 + 1, LANES):
        if ff % c == 0:
            best = c
    return best


def _col_tile(n):
    for c in (512, 1408, 256, LANES):
        if n % c == 0:
            return c
    return n


def _rms_matmul(x, g, wt, b, *, name, dep=None):
    s_len, d = x.shape
    n = wt.shape[0]
    tm = _mm_tile(s_len)
    cn = _col_tile(n)
    has_bias = b is not None

    def body(*refs):
        x_ref, g_ref, w_ref = refs[0:3]
        b_ref = refs[3] if has_bias else None
        o_ref, h_ref = refs[-2:]

        def blk(rows):
            xv = x_ref[rows, :]
            r = lax.rsqrt(jnp.mean(xv * xv, axis=-1, keepdims=True) + EPS)
            h_ref[rows, :] = ((xv * r) * g_ref[...]).astype(BF16)

        rb = min(128, tm)
        for r0 in range(0, tm, rb):
            blk(pl.ds(r0, rb))
        for j in range(n // cn):
            acc = _dot_nt(h_ref[...], w_ref[j * cn:(j + 1) * cn, :])
            if has_bias:
                acc = acc + b_ref[:, j * cn:(j + 1) * cn]
            o_ref[:, j * cn:(j + 1) * cn] = acc.astype(BF16)

    in_specs = [pl.BlockSpec((tm, d), lambda i: (i, 0)), _resident((1, d), lambda i: (0, 0)),
                _resident((n, d), lambda i: (0, 0))]
    args = [x, g, wt]
    if has_bias:
        in_specs.append(_resident((1, n), lambda i: (0, 0)))
        args.append(b)
    in_specs.append(ANY)
    args.append(x if dep is None else dep)
    return pl.pallas_call(
        body, grid=(s_len // tm,), in_specs=in_specs,
        out_specs=[pl.BlockSpec((tm, n), lambda i: (i, 0)), pl.BlockSpec((tm, d), lambda i: (i, 0))],
        out_shape=[jax.ShapeDtypeStruct((s_len, n), BF16), jax.ShapeDtypeStruct((s_len, d), BF16)],
        compiler_params=_params("parallel"), name=name,
    )(*args)


def _mix_windows(u_ref, uh_ref, gw, pw, first, t):
    c = D_CONF
    uh = uh_ref[...].astype(F32)
    gw[0:HALO, :] = jnp.where(first, 0.0, uh[:, 0:c] * _sigmoid(uh[:, c:2 * c]))
    pw[0:HALO3, :] = jnp.where(first, 0.0, uh[HALO - HALO3:HALO, 3 * c:4 * c] * uh[HALO - HALO3:HALO, 4 * c:5 * c])

    def blk(rows):
        dst = pl.ds(pl.multiple_of(rows.start + HALO, SUB), rows.size)
        gw[dst, :] = u_ref[rows, 0:c].astype(F32) * _sigmoid(u_ref[rows, c:2 * c].astype(F32))
        dst3 = pl.ds(pl.multiple_of(rows.start + HALO3, SUB), rows.size)
        pw[dst3, :] = u_ref[rows, 3 * c:4 * c].astype(F32) * u_ref[rows, 4 * c:5 * c].astype(F32)
    _row_loop(t, 64, blk)


def _mix_fwd(u, x0, wa, ba, lg, lb, wb, w_out, *, name):
    s_len, d_in = u.shape
    d = x0.shape[1]
    c = D_CONF
    t = _seq_tile(s_len)
    per = t // HALO

    def body(u_ref, uh_ref, x0_ref, wa_ref, ba_ref, lg_ref, lb_ref, wb_ref, wo_ref, y_ref, x1_ref, ca, cb,
             gw, pw, wrep_a, wrep_b, shf):
        first = pl.program_id(0) == 0
        _mix_windows(u_ref, uh_ref, gw, pw, first, t)
        _replicate_taps(wa_ref, wrep_a, CONF_K)
        _replicate_taps(wb_ref, wrep_b, SHORT_K)
        _conv_taps(gw, wrep_a, ca, taps=CONF_K, n_rows=t, base=HALO, width=c, bias_ref=ba_ref, shf=shf)
        _conv_taps(pw, wrep_b, cb, taps=SHORT_K, n_rows=t, base=HALO3, width=c)

        def blk(rows):
            cv = ca[rows, :]
            mu = jnp.mean(cv, axis=-1, keepdims=True)
            xc = cv - mu
            var = jnp.mean(xc * xc, axis=-1, keepdims=True)
            ln = (xc * lax.rsqrt(var + EPS)) * lg_ref[...] + lb_ref[...]
            y_ref[rows, 0:c] = (ln * _sigmoid(ln)).astype(BF16)
            y_ref[rows, c:2 * c] = (u_ref[rows, 2 * c:3 * c].astype(F32) * cb[rows, :]).astype(BF16)
        half = t // 2
        rb = min(64, half)
        for lo in range(0, t, half):
            for r0 in range(lo, lo + half, rb):
                blk(pl.ds(r0, rb))
            x1_ref[lo:lo + half, :] = x0_ref[lo:lo + half, :] + _dot(y_ref[lo:lo + half, :], wo_ref[...])

    small = lambda r: _resident((r, c), lambda i: (0, 0))
    return pl.pallas_call(
        body, grid=(s_len // t,),
        in_specs=[pl.BlockSpec((t, d_in), lambda i: (i, 0)),
                  pl.BlockSpec((HALO, d_in), lambda i: (jnp.maximum(i * per - 1, 0), 0)),
                  pl.BlockSpec((t, d), lambda i: (i, 0)),
                  small(CONF_K), small(1), small(1), small(1), small(SHORT_K),
                  _resident((2 * c, d), lambda i: (0, 0))],
        out_specs=[pl.BlockSpec((t, 2 * c), lambda i: (i, 0)), pl.BlockSpec((t, d), lambda i: (i, 0)),
                   pl.BlockSpec((t, c), lambda i: (i, 0)), pl.BlockSpec((t, c), lambda i: (i, 0))],
        out_shape=[jax.ShapeDtypeStruct((s_len, 2 * c), BF16), jax.ShapeDtypeStruct((s_len, d), F32),
                   jax.ShapeDtypeStruct((s_len, c), F32), jax.ShapeDtypeStruct((s_len, c), F32)],
        scratch_shapes=[pltpu.VMEM((HALO + t, c), F32), pltpu.VMEM((HALO3 + t, c), F32),
                        pltpu.VMEM((CONF_K * SUB, c), F32), pltpu.VMEM((SHORT_K * SUB, c), F32),
                        pltpu.VMEM((SUB - 1, HALO + t, c), F32)],
        compiler_params=_params("arbitrary"), name=name,
    )(u, u, x0, wa, ba, lg, lb, wb, w_out)


def _ffn_fwd(uf, x1, wf, w_down, *, name):
    s_len, ff2 = uf.shape
    ff = ff2 // 2
    d = x1.shape[1]
    t = _seq_tile(s_len)
    fc = _ff_chunk(ff)
    nc = ff // fc
    per = t // HALO3_BLK
    half = t // 2
    rb = min(64, half)

    def body(ug_ref, ugh_ref, uv_ref, uvh_ref, x1_ref, wfg_ref, wfv_ref, wd_ref,
             act_ref, x2_ref, cg_ref, cv_ref, gwin, vwin, wrep_g, wrep_v):
        first = pl.program_id(0) == 0
        first_chunk = pl.program_id(1) == 0
        lo8 = HALO3_BLK - HALO3
        gwin[0:HALO3, :] = jnp.where(first, 0.0, ugh_ref[...].astype(F32)[lo8:HALO3_BLK])
        vwin[0:HALO3, :] = jnp.where(first, 0.0, uvh_ref[...].astype(F32)[lo8:HALO3_BLK])
        _replicate_taps(wfg_ref, wrep_g, SHORT_K)
        _replicate_taps(wfv_ref, wrep_v, SHORT_K)
        chunk_rows = pl.ds(pl.multiple_of(pl.program_id(1) * fc, fc), fc)

        def conv(win, wrep, r0, lanes):
            acc = None
            for k in range(SHORT_K):
                wk = jnp.tile(wrep[k * SUB:(k + 1) * SUB, lanes], (rb // SUB, 1))
                off = HALO3 + r0 + k - (SHORT_K - 1)
                term = wk * win[off:off + rb, lanes]
                acc = term if acc is None else acc + term
            return acc

        for lo in range(0, t, half):
            for r0 in range(lo, lo + half, rb):
                gwin[HALO3 + r0:HALO3 + r0 + rb, :] = ug_ref[r0:r0 + rb, :].astype(F32)
                vwin[HALO3 + r0:HALO3 + r0 + rb, :] = uv_ref[r0:r0 + rb, :].astype(F32)
            for cb in range(fc // LANES):
                lanes = slice(cb * LANES, (cb + 1) * LANES)
                for r0 in range(lo, lo + half, rb):
                    gv = conv(gwin, wrep_g, r0, lanes)
                    vv = conv(vwin, wrep_v, r0, lanes)
                    cg_ref[r0:r0 + rb, lanes] = gv.astype(BF16)
                    cv_ref[r0:r0 + rb, lanes] = vv.astype(BF16)
                    act_ref[r0:r0 + rb, lanes] = ((gv * _sigmoid(gv)) * vv).astype(BF16)
            base = jnp.where(first_chunk, x1_ref[lo:lo + half, :], x2_ref[lo:lo + half, :])
            x2_ref[lo:lo + half, :] = base + _dot(act_ref[lo:lo + half, :], wd_ref[chunk_rows, :])

    halo_map = lambda off: (lambda i, j: (jnp.maximum(i * per - 1, 0), j + off))
    return pl.pallas_call(
        body, grid=(s_len // t, nc),
        in_specs=[pl.BlockSpec((t, fc), lambda i, j: (i, j)), pl.BlockSpec((HALO3_BLK, fc), halo_map(0)),
                  pl.BlockSpec((t, fc), lambda i, j: (i, j + nc)), pl.BlockSpec((HALO3_BLK, fc), halo_map(nc)),
                  pl.BlockSpec((t, d), lambda i, j: (i, 0)),
                  pl.BlockSpec((SHORT_K, fc), lambda i, j: (0, j)),
                  pl.BlockSpec((SHORT_K, fc), lambda i, j: (0, j + nc)),
                  _resident((ff, d), lambda i, j: (0, 0))],
        out_specs=[pl.BlockSpec((t, fc), lambda i, j: (i, j)), pl.BlockSpec((t, d), lambda i, j: (i, 0)),
                   pl.BlockSpec((t, fc), lambda i, j: (i, j)), pl.BlockSpec((t, fc), lambda i, j: (i, j))],
        out_shape=[jax.ShapeDtypeStruct((s_len, ff), BF16), jax.ShapeDtypeStruct((s_len, d), F32),
                   jax.ShapeDtypeStruct((s_len, ff), BF16), jax.ShapeDtypeStruct((s_len, ff), BF16)],
        scratch_shapes=[pltpu.VMEM((HALO3 + t, fc), F32), pltpu.VMEM((HALO3 + t, fc), F32),
                        pltpu.VMEM((SHORT_K * SUB, fc), F32), pltpu.VMEM((SHORT_K * SUB, fc), F32)],
        compiler_params=_params("parallel", "arbitrary"), name=name,
    )(uf, uf, uf, uf, x1, wf, wf, w_down)


def _loss_bwd(x, g, target, *, name):
    s_len, d = x.shape
    t = _seq_tile(s_len)

    def body(x_ref, g_ref, t_ref, l_ref, dx_ref, dxb_ref, dg_ref):
        @pl.when(pl.program_id(0) == 0)
        def _():
            l_ref[...] = jnp.zeros_like(l_ref)
            dg_ref[...] = jnp.zeros_like(dg_ref)

        def blk(rows):
            xv = x_ref[rows, :]
            r = lax.rsqrt(jnp.mean(xv * xv, axis=-1, keepdims=True) + EPS)
            xn = xv * r
            e = xn * g_ref[...] - t_ref[rows, :]
            l_ref[...] += _rows8(e * e)
            dy = e * (1.0 / d)
            dg_ref[...] += _rows8(dy * xn)
            dn = dy * g_ref[...]
            dx = r * (dn - xn * jnp.mean(dn * xn, axis=-1, keepdims=True))
            dx_ref[rows, :] = dx
            dxb_ref[rows, :] = dx.astype(BF16)
        _row_loop(t, 64, blk)

    row = pl.BlockSpec((t, d), lambda i: (i, 0))
    part = pl.BlockSpec((SUB, d), lambda i: (0, 0))
    return pl.pallas_call(
        body, grid=(s_len // t,),
        in_specs=[row, _resident((1, d), lambda i: (0, 0)), row],
        out_specs=[part, row, row, part],
        out_shape=[jax.ShapeDtypeStruct((SUB, d), F32), jax.ShapeDtypeStruct((s_len, d), F32),
                   jax.ShapeDtypeStruct((s_len, d), BF16), jax.ShapeDtypeStruct((SUB, d), F32)],
        compiler_params=_params("arbitrary"), name=name,
    )(x, g, target)


def _ffn_bwd(dx2, uf, cg, cv, wf, w_down, *, name, dep=None):
    s_len, ff2 = uf.shape
    ff = ff2 // 2
    d = dx2.shape[1]
    t = _seq_tile(s_len)
    n_t = s_len // t
    fc = _ff_chunk(ff)
    nc = ff // fc
    half = t // 2
    rb = min(64, half)

    def body(dx_ref, ug_ref, uv_ref, cg_ref, cv_ref, wfg_ref, wfv_ref, wd_ref, dep_ref,
             duf_ref, dwg_ref, dwv_ref, dact, dgw, dvw, awg, awv, wrep_g, wrep_v):
        i = pl.program_id(1)

        @pl.when(i == 0)
        def _():
            dgw[t:t + HALO3, :] = jnp.zeros((HALO3, fc), F32)
            dvw[t:t + HALO3, :] = jnp.zeros((HALO3, fc), F32)
            awg[...] = jnp.zeros_like(awg)
            awv[...] = jnp.zeros_like(awv)

        _replicate_taps(wfg_ref, wrep_g, SHORT_K)
        _replicate_taps(wfv_ref, wrep_v, SHORT_K)

        def gate(r0, lanes):
            gv = cg_ref[r0:r0 + rb, lanes].astype(F32)
            sg = _sigmoid(gv)
            da = dact[r0:r0 + rb, lanes]
            dgw[r0:r0 + rb, lanes] = (da * cv_ref[r0:r0 + rb, lanes].astype(F32)) * (sg * (1.0 + gv * (1.0 - sg)))
            dvw[r0:r0 + rb, lanes] = da * (gv * sg)

        def conv_bwd(win, wrep, x_ref, out, acc_ref, lo, lanes):
            sums = [None] * SHORT_K
            for r0 in range(lo, lo + half, rb):
                xv = x_ref[r0:r0 + rb, lanes].astype(F32)
                acc = None
                for k in range(SHORT_K):
                    off = r0 + SHORT_K - 1 - k
                    shifted = win[off:off + rb, lanes]
                    term = jnp.tile(wrep[k * SUB:(k + 1) * SUB, lanes], (rb // SUB, 1)) * shifted
                    acc = term if acc is None else acc + term
                    part = _rows8(xv * shifted)
                    sums[k] = part if sums[k] is None else sums[k] + part
                out[r0:r0 + rb, lanes] = acc.astype(out.dtype)
            for k in range(SHORT_K):
                acc_ref[k * SUB:(k + 1) * SUB, lanes] += sums[k]

        for lo in reversed(range(0, t, half)):
            dact[lo:lo + half, :] = _dot_nt(dx_ref[lo:lo + half, :], wd_ref[...])
            for cb in range(fc // LANES):
                lanes = slice(cb * LANES, (cb + 1) * LANES)
                for r0 in range(lo, lo + half, rb):
                    gate(r0, lanes)
                conv_bwd(dgw, wrep_g, ug_ref, duf_ref.at[0], awg, lo, lanes)
                conv_bwd(dvw, wrep_v, uv_ref, duf_ref.at[1], awv, lo, lanes)
        dgw[t:t + HALO3, :] = dgw[0:HALO3, :]
        dvw[t:t + HALO3, :] = dvw[0:HALO3, :]

        @pl.when(i == n_t - 1)
        def _():
            dwg_ref[...] = _fold8(awg, SHORT_K)
            dwv_ref[...] = _fold8(awv, SHORT_K)

    rev = lambda i: n_t - 1 - i
    gate = pl.BlockSpec((t, fc), lambda j, i: (rev(i), j))
    value = pl.BlockSpec((t, fc), lambda j, i: (rev(i), j + nc))
    return pl.pallas_call(
        body, grid=(nc, n_t),
        in_specs=[pl.BlockSpec((t, d), lambda j, i: (rev(i), 0)), gate, value, gate, gate,
                  pl.BlockSpec((SHORT_K, fc), lambda j, i: (0, j)),
                  pl.BlockSpec((SHORT_K, fc), lambda j, i: (0, j + nc)),
                  pl.BlockSpec((fc, d), lambda j, i: (j, 0)), ANY],
        out_specs=[pl.BlockSpec((2, t, fc), lambda j, i: (0, rev(i), j)),
                   pl.BlockSpec((SHORT_K, fc), lambda j, i: (0, j)), pl.BlockSpec((SHORT_K, fc), lambda j, i: (0, j))],
        out_shape=[jax.ShapeDtypeStruct((2, s_len, ff), BF16),
                   jax.ShapeDtypeStruct((SHORT_K, ff), F32), jax.ShapeDtypeStruct((SHORT_K, ff), F32)],
        scratch_shapes=[pltpu.VMEM((t, fc), F32),
                        pltpu.VMEM((t + HALO3, fc), F32), pltpu.VMEM((t + HALO3, fc), F32),
                        pltpu.VMEM((SHORT_K * SUB, fc), F32), pltpu.VMEM((SHORT_K * SUB, fc), F32),
                        pltpu.VMEM((SHORT_K * SUB, fc), F32), pltpu.VMEM((SHORT_K * SUB, fc), F32)],
        compiler_params=_params("arbitrary", "arbitrary"), name=name,
    )(dx2, uf, uf, cg, cv, wf, wf, w_down, uf if dep is None else dep)


def _mix_bwd(dx1, u, ca, cb, wa, lg, lb, wb, w_out, *, name, dep=None):
    s_len, d_in = u.shape
    d = dx1.shape[1]
    c = D_CONF
    t = _seq_tile(s_len)
    n_t = s_len // t

    def body(dx_ref, u_ref, ca_ref, cb_ref, wa_ref, lg_ref, lb_ref, wb_ref, wo_ref, dep_ref,
             du_ref, dwa_ref, dwb_ref, dba_ref, dlg_ref, dlb_ref, dbin_ref,
             glu, prod, dyc, dcaw, dcbw, dglu, dp, awa, awb, wrep_a, wrep_b, shf):
        i = pl.program_id(0)
        _replicate_taps(wa_ref, wrep_a, CONF_K)
        _replicate_taps(wb_ref, wrep_b, SHORT_K)

        @pl.when(i == 0)
        def _():
            dcaw[t:t + HALO, :] = jnp.zeros((HALO, c), F32)
            dcbw[t:t + HALO3, :] = jnp.zeros((HALO3, c), F32)
            awa[...] = jnp.zeros_like(awa)
            awb[...] = jnp.zeros_like(awb)
            dba_ref[...] = jnp.zeros_like(dba_ref)
            dlg_ref[...] = jnp.zeros_like(dlg_ref)
            dlb_ref[...] = jnp.zeros_like(dlb_ref)
            dbin_ref[...] = jnp.zeros_like(dbin_ref)

        def blk1(rows):
            cv = ca_ref[rows, :]
            mu = jnp.mean(cv, axis=-1, keepdims=True)
            xc = cv - mu
            rstd = lax.rsqrt(jnp.mean(xc * xc, axis=-1, keepdims=True) + EPS)
            nrm = xc * rstd
            ln = nrm * lg_ref[...] + lb_ref[...]
            sg = _sigmoid(ln)
            dln = dyc[rows, 0:c] * (sg * (1.0 + ln * (1.0 - sg)))
            dlg_ref[...] += _rows8(dln * nrm)
            dlb_ref[...] += _rows8(dln)
            dn = dln * lg_ref[...]
            dca = rstd * (dn - jnp.mean(dn, axis=-1, keepdims=True)
                          - nrm * jnp.mean(dn * nrm, axis=-1, keepdims=True))
            dcaw[rows, :] = dca
            dba_ref[...] += _rows8(dca)
            ds = dyc[rows, c:2 * c]
            dgb = ds * cb_ref[rows, :]
            dcbw[rows, :] = ds * u_ref[rows, 2 * c:3 * c].astype(F32)
            du_ref[rows, 2 * c:3 * c] = dgb.astype(BF16)
            dbin_ref[:, 2 * c:3 * c] += _rows8(dgb)
            glu[rows, :] = u_ref[rows, 0:c].astype(F32) * _sigmoid(u_ref[rows, c:2 * c].astype(F32))
            prod[rows, :] = u_ref[rows, 3 * c:4 * c].astype(F32) * u_ref[rows, 4 * c:5 * c].astype(F32)
        half = t // 2
        rb = min(64, half)
        for lo in range(0, t, half):
            dyc[lo:lo + half, :] = _dot_nt(dx_ref[lo:lo + half, :], wo_ref[...])
            for r0 in range(lo, lo + half, rb):
                blk1(pl.ds(r0, rb))

        _conv_bwd_taps(dcaw, wrep_a, glu, dglu, awa, taps=CONF_K, n_rows=t, width=c, shf=shf)
        _conv_bwd_taps(dcbw, wrep_b, prod, dp, awb, taps=SHORT_K, n_rows=t, width=c)
        dcaw[t:t + HALO, :] = dcaw[0:HALO, :]
        dcbw[t:t + HALO3, :] = dcbw[0:HALO3, :]

        def blk2(rows):
            av = u_ref[rows, 0:c].astype(F32)
            sg = _sigmoid(u_ref[rows, c:2 * c].astype(F32))
            dg = dglu[rows, :]
            d_av = dg * sg
            d_ag = (dg * av) * (sg * (1.0 - sg))
            dpv = dp[rows, :]
            d_gc = dpv * u_ref[rows, 4 * c:5 * c].astype(F32)
            d_vs = dpv * u_ref[rows, 3 * c:4 * c].astype(F32)
            du_ref[rows, 0:c] = d_av.astype(BF16)
            du_ref[rows, c:2 * c] = d_ag.astype(BF16)
            du_ref[rows, 3 * c:4 * c] = d_gc.astype(BF16)
            du_ref[rows, 4 * c:5 * c] = d_vs.astype(BF16)
            dbin_ref[:, 0:c] += _rows8(d_av)
            dbin_ref[:, c:2 * c] += _rows8(d_ag)
            dbin_ref[:, 3 * c:4 * c] += _rows8(d_gc)
            dbin_ref[:, 4 * c:5 * c] += _rows8(d_vs)
        _row_loop(t, 64, blk2)

        @pl.when(i == n_t - 1)
        def _():
            dwa_ref[...] = _fold8(awa, CONF_K)
            dwb_ref[...] = _fold8(awb, SHORT_K)

    rev = lambda i: n_t - 1 - i
    small_in = lambda r: _resident((r, c), lambda i: (0, 0))
    small = lambda r: pl.BlockSpec((r, c), lambda i: (0, 0))
    return pl.pallas_call(
        body, grid=(n_t,),
        in_specs=[pl.BlockSpec((t, d), lambda i: (rev(i), 0)),
                  pl.BlockSpec((t, d_in), lambda i: (rev(i), 0)),
                  pl.BlockSpec((t, c), lambda i: (rev(i), 0)), pl.BlockSpec((t, c), lambda i: (rev(i), 0)),
                  small_in(CONF_K), small_in(1), small_in(1), small_in(SHORT_K),
                  _resident((2 * c, d), lambda i: (0, 0)), ANY],
        out_specs=[pl.BlockSpec((t, d_in), lambda i: (rev(i), 0)),
                   small(CONF_K), small(SHORT_K), small(SUB), small(SUB), small(SUB),
                   pl.BlockSpec((SUB, d_in), lambda i: (0, 0))],
        out_shape=[jax.ShapeDtypeStruct((s_len, d_in), BF16),
                   jax.ShapeDtypeStruct((CONF_K, c), F32), jax.ShapeDtypeStruct((SHORT_K, c), F32),
                   jax.ShapeDtypeStruct((SUB, c), F32), jax.ShapeDtypeStruct((SUB, c), F32),
                   jax.ShapeDtypeStruct((SUB, c), F32), jax.ShapeDtypeStruct((SUB, d_in), F32)],
        scratch_shapes=[pltpu.VMEM((t, c), F32), pltpu.VMEM((t, c), F32), pltpu.VMEM((t, 2 * c), F32),
                        pltpu.VMEM((t + HALO, c), F32), pltpu.VMEM((t + HALO3, c), F32),
                        pltpu.VMEM((t, c), F32), pltpu.VMEM((t, c), F32),
                        pltpu.VMEM((CONF_K * SUB, c), F32), pltpu.VMEM((SHORT_K * SUB, c), F32),
                        pltpu.VMEM((CONF_K * SUB, c), F32), pltpu.VMEM((SHORT_K * SUB, c), F32),
                        pltpu.VMEM((SUB - 1, t + HALO, c), F32)],
        compiler_params=_params("arbitrary"), name=name,
    )(dx1, u, ca, cb, wa, lg, lb, wb, w_out, u if dep is None else dep)


def _matmul_tn(a, b, *, name):
    n_p, s_len, k = a.shape
    n = b.shape[1]
    tk = _col_tile(k)
    per = k // tk

    def body(a_ref, b_ref, o_ref):
        o_ref[...] = _dot_tn(a_ref[...], b_ref[...]).astype(BF16)

    return pl.pallas_call(
        body, grid=(n_p, per),
        in_specs=[pl.BlockSpec((None, s_len, tk), lambda p, j: (p, 0, j)), _resident((s_len, n), lambda p, j: (0, 0))],
        out_specs=pl.BlockSpec((tk, n), lambda p, j: (p * per + j, 0)),
        out_shape=jax.ShapeDtypeStruct((n_p * k, n), BF16),
        compiler_params=_params("parallel", "parallel"), name=name,
    )(a, b)


def _matmul_rmsbwd(dzs, wt, x, g, dx_in, *, name, dep=None):
    s_len, d = x.shape
    n_z, _, nj = dzs.shape
    t = _mm_tile(s_len)

    def body(*refs):
        dz_refs = refs[0:n_z]
        w_refs = refs[n_z:2 * n_z]
        x_ref, g_ref, dxi_ref, _, dx_ref, dxb_ref, dg_ref, dh = refs[2 * n_z:]

        @pl.when(pl.program_id(0) == 0)
        def _():
            dg_ref[...] = jnp.zeros_like(dg_ref)

        def blk(rows):
            xv = x_ref[rows, :]
            r = lax.rsqrt(jnp.mean(xv * xv, axis=-1, keepdims=True) + EPS)
            xn = xv * r
            dhv = dh[rows, :]
            dg_ref[...] += _rows8(dhv * xn)
            dn = dhv * g_ref[...]
            dx = dxi_ref[rows, :] + r * (dn - xn * jnp.mean(dn * xn, axis=-1, keepdims=True))
            dx_ref[rows, :] = dx
            dxb_ref[rows, :] = dx.astype(BF16)

        half = t // 2
        rb = min(128, half)
        for lo in range(0, t, half):
            acc = _dot(dz_refs[0][lo:lo + half, :], w_refs[0][...])
            for q in range(1, n_z):
                acc = acc + _dot(dz_refs[q][lo:lo + half, :], w_refs[q][...])
            dh[lo:lo + half, :] = acc
            for r0 in range(lo, lo + half, rb):
                blk(pl.ds(r0, rb))

    row = pl.BlockSpec((t, d), lambda i: (i, 0))
    in_specs = [pl.BlockSpec((None, t, nj), functools.partial(lambda q, i: (q, i, 0), q)) for q in range(n_z)]
    in_specs += [_resident((nj, d), functools.partial(lambda q, i: (q, 0), q)) for q in range(n_z)]
    in_specs += [row, _resident((1, d), lambda i: (0, 0)), row, ANY]
    return pl.pallas_call(
        body, grid=(s_len // t,), in_specs=in_specs,
        out_specs=[row, row, pl.BlockSpec((SUB, d), lambda i: (0, 0))],
        out_shape=[jax.ShapeDtypeStruct((s_len, d), F32), jax.ShapeDtypeStruct((s_len, d), BF16),
                   jax.ShapeDtypeStruct((SUB, d), F32)],
        scratch_shapes=[pltpu.VMEM((t, d), F32)],
        compiler_params=_params("arbitrary"), name=name,
    )(*([dzs] * n_z), *([wt] * n_z), x, g, dx_in, x if dep is None else dep)


def _row(v):
    return v.reshape(1, -1)


def _layer_fwd(x0, p, tag, dep=None, before_up=None):
    u, h1 = _rms_matmul(x0, _row(p["mix_norm_g"]), p["w_in_t"], _row(p["b_in"]), name=f"in_proj_{tag}", dep=dep)
    ycat, x1, ca, cb = _mix_fwd(u, x0, p["conv_a_w"], _row(p["conv_a_b"]), _row(p["ln_a_g"]), _row(p["ln_a_b"]),
                            p["conv_b_w"], p["w_out"], name=f"mix_fwd_{tag}")
    if before_up is not None:
        before_up(x1)
    uf, h2 = _rms_matmul(x1, _row(p["ffn_norm_g"]), p["w_up_t"], None, name=f"up_proj_{tag}")
    act, x2, cg, cv = _ffn_fwd(uf, x1, p["conv_f_w"], p["w_down"], name=f"ffn_fwd_{tag}")
    return x2, dict(x0=x0, h1=h1, u=u, ca=ca, cb=cb, ycat=ycat, x1=x1, h2=h2, uf=uf, cg=cg, cv=cv, act=act)


def _layer_bwd(dx2, dx2_b, p, saved, tag, ffn_grads, ffn_sent, mix_grads, mix_sent, dep=None):
    d_uf, dwf_g, dwf_v = _ffn_bwd(dx2_b, saved["uf"], saved["cg"], saved["cv"], p["conv_f_w"], p["w_down"],
                                  name=f"ffn_bwd_{tag}", dep=dep)
    g_down = _matmul_tn(saved["act"][None], dx2_b, name=f"dw_down_{tag}")
    g_up = _matmul_tn(d_uf, saved["h2"], name=f"dw_up_{tag}")
    dep_ffn = ffn_grads(dict(w_up=g_up, w_down=g_down), dx2_b)
    dx1, dx1_b, dg2 = _matmul_rmsbwd(d_uf, p["w_up_t"], saved["x1"], _row(p["ffn_norm_g"]), dx2,
                                     name=f"dh_ffn_{tag}", dep=dep_ffn)
    du, dwa, dwb, dba, dlg, dlb, dbin = _mix_bwd(
        dx1_b, saved["u"], saved["ca"], saved["cb"], p["conv_a_w"], _row(p["ln_a_g"]), _row(p["ln_a_b"]),
        p["conv_b_w"], p["w_out"], name=f"mix_bwd_{tag}", dep=ffn_sent(dx1_b))
    g_out = _matmul_tn(saved["ycat"][None], dx1_b, name=f"dw_out_{tag}")
    g_in = _matmul_tn(du[None], saved["h1"], name=f"dw_in_{tag}")
    conv = dict(conv_a_w=dwa, conv_b_w=dwb, conv_f_w=jnp.concatenate([dwf_g, dwf_v], axis=1))
    dep_mix = mix_grads(dict(w_in=g_in, w_out=g_out), conv, dx1_b)
    dx0, dx0_b, dg1 = _matmul_rmsbwd(du[None], p["w_in_t"], saved["x0"], _row(p["mix_norm_g"]), dx1,
                                     name=f"dh_mix_{tag}", dep=dep_mix)
    rep = dict(mix_norm_g=dg1, b_in=dbin, conv_a_b=dba, ln_a_g=dlg, ln_a_b=dlb, ffn_norm_g=dg2)
    return dx0, dx0_b, rep, mix_sent(dx0_b)


def _place():
    return lax.axis_index("x"), lax.axis_index("y"), lax.axis_index("c")


def _all_gather(arrs, *, name):
    n_a = len(arrs)

    def body(*refs):
        ins = refs[0:n_a]
        outs = refs[n_a:2 * n_a]
        send_sems, recv_sems, local_sems = refs[2 * n_a:]
        x, y, c = _place()
        sibling = (x, y, 1 - c)
        chips = [(1 - x, y), (x, 1 - y), (1 - x, 1 - y)]

        def slot(a, px, py, pc):
            return outs[a].at[4 * px + 2 * py + pc]

        def copy(a, k, block, to, src=None):
            return pltpu.make_async_remote_copy(
                src_ref=slot(a, *block) if src is None else src, dst_ref=slot(a, *block),
                send_sem=send_sems.at[a, k], recv_sem=recv_sems.at[a, k],
                device_id=to, device_id_type=MESH)

        me = (x, y, c)
        mine = [pltpu.make_async_copy(ins[a], slot(a, *me), local_sems.at[a]) for a in range(n_a)]
        for cp in mine:
            cp.start()
        started = []
        for a in range(n_a):
            first = [copy(a, 0, me, sibling, src=ins[a])]
            first += [copy(a, 1 + j, me, (*chip, c), src=ins[a]) for j, chip in enumerate(chips)]
            for cp in first:
                cp.start()
            started += first
        for a in range(n_a):
            for j, chip in enumerate(chips):
                copy(a, 1 + j, (*chip, c), me).wait_recv()
                passed = copy(a, 4 + j, (*chip, c), sibling)
                passed.start()
                started.append(passed)
        for a in range(n_a):
            copy(a, 0, sibling, me).wait_recv()
            for j, chip in enumerate(chips):
                copy(a, 4 + j, (*chip, 1 - c), me).wait_recv()
        for cp in started:
            cp.wait_send()
        for cp in mine:
            cp.wait()

    return pl.pallas_call(
        body, in_specs=[ANY] * n_a, out_specs=[ANY] * n_a,
        out_shape=[jax.ShapeDtypeStruct((N_DEV, *a.shape), a.dtype) for a in arrs],
        scratch_shapes=[pltpu.SemaphoreType.DMA((n_a, 7)), pltpu.SemaphoreType.DMA((n_a, 7)),
                        pltpu.SemaphoreType.DMA((n_a,))],
        name=name,
    )(*arrs)


def _row_tile(r, cap):
    for tr in range(min(cap, r) // 16 * 16, 0, -16):
        if r % tr == 0:
            return tr
    return r


def _pair_sum(mines, theirs, where, *, name):
    n_a = len(mines)
    n_chip = mines[0].shape[0]

    def body(where_ref, *refs):
        a_refs = refs[0:n_a]
        b_refs = refs[n_a:2 * n_a]
        p_refs = refs[2 * n_a:3 * n_a]
        l_refs = refs[3 * n_a:4 * n_a]
        q = pl.program_id(0)
        for a in range(n_a):
            p_refs[a][...] = (a_refs[a][...].astype(F32) + b_refs[a][...].astype(F32)).astype(p_refs[a].dtype)

        @pl.when(q == where_ref[1])
        def _():
            for a in range(n_a):
                l_refs[a][...] = p_refs[a][...]

    in_specs, out_p, out_l, shapes = [], [], [], []
    for m in mines:
        _, _, r, c = m.shape
        in_specs.append(pl.BlockSpec((None, None, r, c), lambda q, where_ref: (q, where_ref[0], 0, 0)))
    for m in mines:
        _, _, r, c = m.shape
        in_specs.append(pl.BlockSpec((None, r, c), lambda q, where_ref: (q, 0, 0)))
        out_p.append(pl.BlockSpec((None, r, c), lambda q, where_ref: (q, 0, 0)))
        out_l.append(pl.BlockSpec((None, r, c), lambda q, where_ref: (where_ref[1], 0, 0)))
        shapes.append(jax.ShapeDtypeStruct((n_chip, r, c), m.dtype))
    res = pl.pallas_call(
        body,
        grid_spec=pltpu.PrefetchScalarGridSpec(num_scalar_prefetch=1, grid=(n_chip,), in_specs=in_specs,
                                               out_specs=out_p + out_l),
        out_shape=shapes + shapes,
        compiler_params=_params("arbitrary"), name=name,
    )(where, *mines, *theirs)
    return list(res[:n_a]), list(res[n_a:])


HBM = pl.BlockSpec(memory_space=pltpu.HBM)
SEM = pl.BlockSpec(memory_space=pltpu.SEMAPHORE)
EFFECT = pltpu.SideEffectType.DATAFLOW_SIDE_EFFECTING


def _in_hbm(a):
    return pltpu.with_memory_space_constraint(a, pltpu.HBM)


def _split_start(srcs, lands, plan, n_copies, after, *, name):
    n_s, n_l = len(srcs), len(lands)

    def body(*refs):
        src_refs = refs[0:n_s]
        land_refs = refs[n_s:n_s + n_l]
        send_sems, recv_sems = refs[n_s + n_l + 1], refs[n_s + n_l + 2]
        token = refs[-1]
        for cp in plan(src_refs, land_refs, send_sems, recv_sems):
            cp.start()
        token[...] = jnp.zeros_like(token)

    thru = [pltpu.HBM(a.shape, a.dtype) for a in list(srcs) + list(lands)]
    res = pl.pallas_call(
        body, name=name,
        out_shape=(pltpu.SemaphoreType.DMA((n_copies,)), pltpu.SemaphoreType.DMA((n_copies,)), *thru,
                   jax.ShapeDtypeStruct((SUB, LANES), F32)),
        in_specs=[HBM] * (n_s + n_l) + [ANY],
        out_specs=(SEM, SEM, *([HBM] * (n_s + n_l)), pl.BlockSpec(memory_space=pltpu.VMEM)),
        input_output_aliases={i: 2 + i for i in range(n_s + n_l)},
        compiler_params=pltpu.CompilerParams(has_side_effects=EFFECT),
    )(*[_in_hbm(a) for a in srcs], *[_in_hbm(a) for a in lands], _in_hbm(after))
    return res[0], res[1], list(res[2:2 + n_s]), list(res[2 + n_s:2 + n_s + n_l]), res[-1]


def _split_wait(send_sems, recv_sems, srcs, lands, after, plan, *, name):
    n_s, n_l = len(srcs), len(lands)

    def body(*refs):
        src_refs = refs[0:n_s]
        land_refs = refs[n_s:n_s + n_l]
        send, recv = refs[n_s + n_l], refs[n_s + n_l + 1]
        for cp in plan(src_refs, land_refs, send, recv):
            cp.wait_send()
            cp.wait_recv()

    res = pl.pallas_call(
        body, name=name,
        out_shape=tuple(pltpu.HBM(a.shape, a.dtype) for a in list(srcs) + list(lands)),
        in_specs=[HBM] * (n_s + n_l) + [SEM, SEM, ANY],
        out_specs=tuple([HBM] * (n_s + n_l)),
        input_output_aliases={i: i for i in range(n_s + n_l)},
        compiler_params=pltpu.CompilerParams(has_side_effects=EFFECT),
    )(*srcs, *lands, send_sems, recv_sems, _in_hbm(after))
    return list(res[:n_s]), list(res[n_s:])


def _remote(src, dst, send_sems, recv_sems, k, to):
    return pltpu.make_async_remote_copy(src_ref=src, dst_ref=dst, send_sem=send_sems.at[k], recv_sem=recv_sems.at[k],
                                        device_id=to, device_id_type=MESH)


def _gather_plan_first(src_refs, land_refs, send_sems, recv_sems):
    x, y, c = _place()
    me = 4 * x + 2 * y + c
    peers = [(x, y, 1 - c), (1 - x, y, c), (x, 1 - y, c), (1 - x, 1 - y, c)]
    return [_remote(src, land.at[me], send_sems, recv_sems, 4 * a + k, to)
            for a, (src, land) in enumerate(zip(src_refs, land_refs)) for k, to in enumerate(peers)]


def _gather_plan_second(src_refs, land_refs, send_sems, recv_sems):
    x, y, c = _place()
    chips = [(1 - x, y), (x, 1 - y), (1 - x, 1 - y)]
    out = []
    for a, land in enumerate(land_refs):
        for j, (px, py) in enumerate(chips):
            slot = land.at[4 * px + 2 * py + c]
            out.append(_remote(slot, slot, send_sems, recv_sems, 3 * a + j, (x, y, 1 - c)))
    return out


def _siblings_plan(src_refs, land_refs, send_sems, recv_sems):
    x, y, c = _place()
    return [_remote(src.at[:, 1 - c], land, send_sems, recv_sems, a, (x, y, 1 - c))
            for a, (src, land) in enumerate(zip(src_refs, land_refs))]


def _chips_plan(src_refs, land_refs, send_sems, recv_sems):
    x, y, c = _place()
    my_chip = 2 * x + y
    chips = [(1 - x, y), (x, 1 - y), (1 - x, 1 - y)]
    return [_remote(src.at[2 * px + py], land.at[my_chip], send_sems, recv_sems, 3 * a + j, (px, py, c))
            for a, (src, land) in enumerate(zip(src_refs, land_refs)) for j, (px, py) in enumerate(chips)]


def _gather_landings(shards, me, *, name):
    blank = _unwritten([jax.ShapeDtypeStruct((N_DEV, *s.shape), s.dtype) for s in shards], name=name)
    return [lax.dynamic_update_index_in_dim(b, s, me, 0) for b, s in zip(blank, shards)]


def _adamw_math(g, w, m, v):
    m = ADAM_B1 * m + (1.0 - ADAM_B1) * g
    v = ADAM_B2 * v + (1.0 - ADAM_B2) * (g * g)
    m_hat = m / (1.0 - ADAM_B1 ** ADAM_STEP)
    v_hat = v / (1.0 - ADAM_B2 ** ADAM_STEP)
    delta = -ADAM_LR * (m_hat / (jnp.sqrt(v_hat) + ADAM_EPS) + ADAM_WD * w)
    return delta, m, v


def _adamw_sharded(parts, w, m, v, *, name, dep=None):
    n_layers, r, c = w.shape
    n_chip = parts[0].shape[0]
    tr = _row_tile(r, 384)
    n_i = r // tr

    def body(*refs):
        p_refs = refs[0:n_layers]
        w_ref, m_ref, v_ref, _, g_out, d_out, m_out, v_out = refs[n_layers:]
        layer = pl.program_id(0)
        for l in range(n_layers):
            @pl.when(layer == l)
            def _(l=l):
                g = p_refs[l][0].astype(F32)
                for q in range(1, n_chip):
                    g = g + p_refs[l][q].astype(F32)
                delta, m_new, v_new = _adamw_math(g, w_ref[...], m_ref[...], v_ref[...])
                g_out[...] = g
                d_out[...] = delta
                m_out[...] = m_new
                v_out[...] = v_new

    def part_map(l):
        return lambda layer, i: (0, jnp.where(layer == l, i, jnp.where(layer < l, 0, n_i - 1)), 0)

    blk = pl.BlockSpec((None, tr, c), lambda layer, i: (layer, i, 0))
    return pl.pallas_call(
        body, grid=(n_layers, n_i),
        in_specs=[pl.BlockSpec((n_chip, tr, c), part_map(l)) for l in range(n_layers)] + [blk, blk, blk, ANY],
        out_specs=[blk] * 4, out_shape=[jax.ShapeDtypeStruct((n_layers, r, c), F32)] * 4,
        compiler_params=_params("arbitrary", "arbitrary"), name=name,
    )(*parts, w, m, v, w if dep is None else dep)


def _fold_partials(cols, *, name):
    widths = [c.shape[1] for c in cols]

    def body(*refs):
        o_ref = refs[-1]
        pos = 0
        for ref, width in zip(refs[:-1], widths):
            o_ref[:, pos:pos + width] = jnp.sum(ref[...], axis=0, keepdims=True)
            pos += width

    return pl.pallas_call(body, out_shape=jax.ShapeDtypeStruct((1, sum(widths)), F32), name=name)(*cols)


def _adamw_replicated(parts, names, w, m, v, n_loss, *, name):
    n_dev = parts.shape[0]
    n_layers = w[names[0]].shape[0]
    every = list(names) + ["final_norm_g"]
    n_p = len(every)

    def body(*refs):
        p_ref = refs[0]
        w_refs = dict(zip(every, refs[1:1 + n_p]))
        m_refs = dict(zip(every, refs[1 + n_p:1 + 2 * n_p]))
        v_refs = dict(zip(every, refs[1 + 2 * n_p:1 + 3 * n_p]))
        l_out = refs[1 + 3 * n_p]
        outs = refs[2 + 3 * n_p:]
        o_refs = {n: outs[4 * q:4 * q + 4] for q, n in enumerate(every)}
        acc = p_ref[0]
        for q in range(1, n_dev):
            acc = acc + p_ref[q]
        tot = jnp.sum(acc, axis=0, keepdims=True)
        pos = 0
        where = [(n, l) for l in range(n_layers) for n in names] + [("final_norm_g", 0)]
        for n, l in where:
            width = w_refs[n].shape[1]
            g = tot[:, pos:pos + width]
            pos += width
            row = pl.ds(l, 1)
            delta, m_new, v_new = _adamw_math(g, w_refs[n][row, :], m_refs[n][row, :], v_refs[n][row, :])
            for o, val in zip(o_refs[n], (g, delta, m_new, v_new)):
                o[row, :] = val
        l_out[...] = (0.5 / n_loss) * jnp.sum(tot[:, pos:pos + n_loss], axis=-1, keepdims=True)

    shapes = [jax.ShapeDtypeStruct((1, 1), F32)]
    for n in every:
        shapes += [jax.ShapeDtypeStruct(w[n].shape, F32)] * 4
    res = pl.pallas_call(
        body, out_shape=shapes,
        compiler_params=pltpu.CompilerParams(vmem_limit_bytes=VMEM_LIMIT), name=name,
    )(parts, *[w[n] for n in every], *[m[n] for n in every], *[v[n] for n in every])
    return res[0], {n: res[1 + 4 * q:5 + 4 * q] for q, n in enumerate(every)}


BIG = ("w_in", "w_out", "w_up", "w_down")
COL_SHARDED = ("w_in", "w_up")
CONV = ("conv_a_w", "conv_b_w", "conv_f_w")
REPLICATED = ("mix_norm_g", "b_in", "conv_a_b", "ln_a_g", "ln_a_b", "ffn_norm_g")
KINDS = ("grad", "delta", "m", "v")
FFN_PART = ("w_up", "w_down")
MIX_PART = ("w_in", "w_out")


def _weights_from_gathered(g):
    n_dev, r, c = g.shape
    return g.reshape(n_dev * r, c)


def _slabs_from_full(grad):
    return grad.reshape(N_DEV, grad.shape[0] // N_DEV, grad.shape[1])


def _unwritten(like, *, name):
    return pl.pallas_call(lambda *refs: None, out_specs=[ANY] * len(like), out_shape=list(like), name=name)()


def kernel(x, mix_norm_g, w_in, b_in, conv_a_w, conv_a_b, ln_a_g, ln_a_b, conv_b_w, w_out, ffn_norm_g, w_up, conv_f_w, w_down, final_norm_g, loss_target, m_mix_norm_g, m_w_in, m_b_in, m_conv_a_w, m_conv_a_b, m_ln_a_g, m_ln_a_b, m_conv_b_w, m_w_out, m_ffn_norm_g, m_w_up, m_conv_f_w, m_w_down, m_final_norm_g, v_mix_norm_g, v_w_in, v_b_in, v_conv_a_w, v_conv_a_b, v_ln_a_g, v_ln_a_b, v_conv_b_w, v_w_out, v_ffn_norm_g, v_w_up, v_conv_f_w, v_w_down, v_final_norm_g):
    w = dict(mix_norm_g=mix_norm_g, w_in=w_in, b_in=b_in, conv_a_w=conv_a_w, conv_a_b=conv_a_b, ln_a_g=ln_a_g,
             ln_a_b=ln_a_b, conv_b_w=conv_b_w, w_out=w_out, ffn_norm_g=ffn_norm_g, w_up=w_up, conv_f_w=conv_f_w,
             w_down=w_down, final_norm_g=final_norm_g)
    m = dict(mix_norm_g=m_mix_norm_g, w_in=m_w_in, b_in=m_b_in, conv_a_w=m_conv_a_w, conv_a_b=m_conv_a_b,
             ln_a_g=m_ln_a_g, ln_a_b=m_ln_a_b, conv_b_w=m_conv_b_w, w_out=m_w_out, ffn_norm_g=m_ffn_norm_g,
             w_up=m_w_up, conv_f_w=m_conv_f_w, w_down=m_w_down, final_norm_g=m_final_norm_g)
    v = dict(mix_norm_g=v_mix_norm_g, w_in=v_w_in, b_in=v_b_in, conv_a_w=v_conv_a_w, conv_a_b=v_conv_a_b,
             ln_a_g=v_ln_a_g, ln_a_b=v_ln_a_b, conv_b_w=v_conv_b_w, w_out=v_w_out, ffn_norm_g=v_ffn_norm_g,
             w_up=v_w_up, conv_f_w=v_conv_f_w, w_down=v_w_down, final_norm_g=v_final_norm_g)
    order = list(w)
    n_layers = w_in.shape[0]
    xs = x[0]
    target = loss_target[0]
    flip = lambda a: jnp.transpose(a, (0, 2, 1))
    wt, mt, vt = ({n: flip(d[n]) if n in COL_SHARDED else d[n] for n in BIG} for d in (w, m, v))
    px, py, pc = _place()
    where = jnp.stack([pc, 2 * px + py]).astype(jnp.int32)
    me = 4 * px + 2 * py + pc

    assert BIG == MIX_PART + FFN_PART
    key = lambda n: n + "_t" if n in COL_SHARDED else n
    shard = lambda n, l: wt[n][l].astype(BF16)

    def gather_start(names, l, after, tag):
        shards = [shard(n, l) for n in names]
        lands = _gather_landings(shards, me, name=f"gather_landing_{tag}")
        return _split_start(shards, lands, _gather_plan_first, 4 * len(shards), after, name=f"gather_first_start_{tag}")

    def gather_mid(first, after, tag):
        return _split_wait(first[0], first[1], first[2], first[3], after, _gather_plan_first,
                           name=f"gather_first_wait_{tag}")[1]

    def forward_start(lands, after, tag):
        return _split_start([], lands, _gather_plan_second, 3 * len(lands), after, name=f"gather_second_start_{tag}")

    def forward_finish(second, after, tag):
        return _split_wait(second[0], second[1], [], second[3], after, _gather_plan_second,
                           name=f"gather_second_wait_{tag}")[1]

    gathered = _all_gather([shard(n, 0) for n in MIX_PART] + [w[n] for n in CONV], name="gather_weights_0")
    params = [{n: w[n][l] for n in REPLICATED} for l in range(n_layers)]
    for n, g in zip(CONV, gathered[len(MIX_PART):]):
        n_dev, _, taps, c = g.shape
        full = g.transpose(1, 2, 0, 3).reshape(n_layers, taps, n_dev * c)
        for l in range(n_layers):
            params[l][n] = full[l]
    for n, g in zip(MIX_PART, gathered):
        params[0][key(n)] = _weights_from_gathered(g)
    ffn_first = gather_start(FFN_PART, 0, gathered[0], "0_ffn")
    pending = {}

    h = xs
    saved = []
    for l in range(n_layers):
        nxt = l + 1 if l + 1 < n_layers else None

        def before_up(x1, l=l, nxt=nxt):
            if l == 0:
                second = forward_start(gather_mid(ffn_first, x1, "0_ffn"), x1, "0_ffn")
                after = second[4]
            else:
                second = pending[l]["ffn"]
                after = x1
            if nxt is not None:
                pending[nxt] = dict(first=gather_start(BIG, nxt, after, str(nxt)))
                after = pending[nxt]["first"][4]
            for n, g in zip(FFN_PART, forward_finish(second, after, f"{l}_ffn")):
                params[l][key(n)] = _weights_from_gathered(g)

        h, keep = _layer_fwd(h, params[l], str(l), dep=ffn_first[4] if l == 0 else None, before_up=before_up)
        saved.append(keep)
        if nxt is not None:
            arrived = gather_mid(pending[nxt]["first"], h, str(nxt))
            mix_second = forward_start(arrived[:len(MIX_PART)], h, f"{nxt}_mix")
            pending[nxt]["ffn"] = forward_start(arrived[len(MIX_PART):], mix_second[4], f"{nxt}_ffn")
            for n, g in zip(MIX_PART, forward_finish(mix_second, pending[nxt]["ffn"][4], f"{nxt}_mix")):
                params[nxt][key(n)] = _weights_from_gathered(g)

    def start_siblings(slabs, after, tag):
        mines = [s.reshape(N_CHIP, 2, *s.shape[1:]) for s in slabs]
        lands = _unwritten([jax.ShapeDtypeStruct((N_CHIP, *m.shape[2:]), m.dtype) for m in mines],
                           name=f"reduce_siblings_landing_{tag}")
        return _split_start(mines, lands, _siblings_plan, len(mines), after, name=f"reduce_siblings_start_{tag}")

    def start_chips(sib, after, tag):
        mines, theirs = _split_wait(sib[0], sib[1], sib[2], sib[3], after, _siblings_plan,
                                    name=f"reduce_siblings_wait_{tag}")
        pairs, lands = _pair_sum(mines, theirs, where, name=f"pair_sum_{tag}")
        return _split_start(pairs, lands, _chips_plan, 3 * len(pairs), after, name=f"reduce_chips_start_{tag}")

    def finish_reduce(fly, after, tag):
        return _split_wait(fly[0], fly[1], fly[2], fly[3], after, _chips_plan, name=f"reduce_chips_wait_{tag}")[1]

    loss_sq, dh, dh_b, dgf = _loss_bwd(h, _row(final_norm_g), target, name="loss")
    conv_g = {n: [None] * n_layers for n in CONV}
    rep_g = [None] * n_layers
    siblings = {}
    flights = {}
    token = None
    for l in reversed(range(n_layers)):
        def ffn_grads(g, after, l=l):
            siblings[l, "ffn"] = start_siblings([_slabs_from_full(g[n]) for n in FFN_PART], after, f"{l}_ffn")
            return siblings[l, "ffn"][4]

        def ffn_sent(after, l=l):
            flights[l, "ffn"] = start_chips(siblings[l, "ffn"], after, f"{l}_ffn")
            return flights[l, "ffn"][4]

        def mix_grads(g, conv, after, l=l):
            for n in CONV:
                conv_g[n][l] = conv[n]
            slabs = [_slabs_from_full(g[n]) for n in MIX_PART]
            if l == 0:
                for n in CONV:
                    full = jnp.stack(conv_g[n])
                    _, taps, c = full.shape
                    slabs.append(full.reshape(n_layers, taps, N_DEV, c // N_DEV).transpose(2, 0, 1, 3)
                                 .reshape(N_DEV, n_layers * taps, c // N_DEV))
            siblings[l, "mix"] = start_siblings(slabs, after, f"{l}_mix")
            return siblings[l, "mix"][4]

        def mix_sent(after, l=l):
            flights[l, "mix"] = start_chips(siblings[l, "mix"], after, f"{l}_mix")
            return flights[l, "mix"][4]

        dh, dh_b, rep_g[l], token = _layer_bwd(dh, dh_b, params[l], saved[l], str(l), ffn_grads, ffn_sent,
                                               mix_grads, mix_sent, dep=token)

    sums = {key: finish_reduce(fly, dh, f"{key[0]}_{key[1]}") for key, fly in flights.items() if key != (0, "mix")}
    out = {k: {} for k in KINDS}

    def adamw_big(names, part, dep):
        for q, n in enumerate(names):
            layer_parts = [sums[l, part][q] for l in range(n_layers)]
            res = _adamw_sharded(layer_parts, wt[n], mt[n], vt[n], name=f"adamw_{n}", dep=dep)
            for k, r in zip(KINDS, res):
                out[k][n] = flip(r) if n in COL_SHARDED else r

    adamw_big(FFN_PART, "ffn", token)

    rep_cols = [rep_g[l][n] for l in range(n_layers) for n in REPLICATED] + [dgf, loss_sq]
    rep_all = _all_gather([_fold_partials(rep_cols, name="fold_small")], name="gather_small")[0]
    with_final = lambda d: {**{n: d[n] for n in REPLICATED}, "final_norm_g": _row(d["final_norm_g"])}
    loss, rep_res = _adamw_replicated(rep_all, REPLICATED, with_final(w), with_final(m), with_final(v),
                                      loss_sq.shape[1], name="adamw_small")
    for n, res in rep_res.items():
        for k, r in zip(KINDS, res):
            out[k][n] = r.reshape(w[n].shape)

    last = finish_reduce(flights[0, "mix"], rep_res["b_in"][0], "0_mix")
    sums[0, "mix"] = last[:len(MIX_PART)]
    adamw_big(MIX_PART, "mix", None)
    for n, p in zip(CONV, last[len(MIX_PART):]):
        as_one = lambda a: a.reshape(1, *p.shape[1:])
        for k, r in zip(KINDS, _adamw_sharded([p], as_one(w[n]), as_one(m[n]), as_one(v[n]), name=f"adamw_{n}")):
            out[k][n] = r.reshape(w[n].shape)

    grad_x = dh.reshape(x.shape)
    return (loss.reshape(()), grad_x, *[out["grad"][n] for n in order], *[out["delta"][n] for n in order],
            *[out["m"][n] for n in order], *[out["v"][n] for n in order])
```

```python
import functools

import jax
import jax.numpy as jnp
from jax import lax
from jax.experimental import pallas as pl
from jax.experimental.pallas import tpu as pltpu

F32 = jnp.float32
BF16 = jnp.bfloat16

N_DEV = 8
N_CHIP = 4
D_CONF = 512
CONF_K = 31
SHORT_K = 3
EPS = 1e-6
HALO = 32
HALO3 = 8
HALO3_BLK = 16
LANES = 128
SUB = 8
VMEM_LIMIT = 56 * 1024 * 1024

ADAM_LR = 0.001
ADAM_B1 = 0.9
ADAM_B2 = 0.999
ADAM_EPS = 1e-08
ADAM_WD = 0.01
ADAM_STEP = 10

MESH = pl.DeviceIdType.MESH
ANY = pl.BlockSpec(memory_space=pl.ANY)


def _params(*sem):
    return pltpu.CompilerParams(dimension_semantics=sem, vmem_limit_bytes=VMEM_LIMIT)


def _resident(shape, index_map):
    return pl.BlockSpec(shape, index_map, pipeline_mode=pl.Buffered(1))


def _row_loop(n_rows, rb, fn, unroll=1):
    rb = min(rb, n_rows)

    def body(i, carry):
        fn(pl.ds(pl.multiple_of(i * rb, rb), rb))
        return carry
    lax.fori_loop(0, n_rows // rb, body, 0, unroll=unroll)


def _rows8(v):
    acc = v[0:SUB]
    for k in range(1, v.shape[0] // SUB):
        acc = acc + v[k * SUB:(k + 1) * SUB]
    return acc


def _sigmoid(z):
    return 0.5 * jnp.tanh(0.5 * z) + 0.5


def _dot(a, b):
    return jnp.dot(a, b, preferred_element_type=F32)


def _dot_nt(a, b):
    return lax.dot_general(a, b, (((1,), (1,)), ((), ())), preferred_element_type=F32)


def _dot_tn(a, b):
    return lax.dot_general(a, b, (((0,), (0,)), ((), ())), preferred_element_type=F32)


def _replicate_taps(w_ref, wrep, taps):
    for k in range(taps):
        wrep[pl.ds(k * SUB, SUB), :] = jnp.broadcast_to(w_ref[pl.ds(k, 1), :], (SUB, w_ref.shape[1]))


def _shift_copies(win, shf, lanes):
    span = win.shape[0] - SUB
    for r in range(1, SUB):
        for j0 in range(0, span, 64):
            n = min(64, span - j0)
            shf[r - 1, pl.ds(j0, n), lanes] = win[pl.ds(j0 + r, n), lanes]


def _rows_at(win, shf, off, rb, lanes):
    if shf is None or off % SUB == 0:
        return win[pl.ds(off, rb), lanes]
    return shf[off % SUB - 1, pl.ds(off - off % SUB, rb), lanes]


def _conv_taps(win, wrep, out, *, taps, n_rows, base, width, transposed=False, bias_ref=None, shf=None):
    rb = min(64, n_rows)

    def lane_body(cb, carry):
        lanes = pl.ds(pl.multiple_of(cb * LANES, LANES), LANES)
        if shf is not None:
            _shift_copies(win, shf, lanes)
        for r0 in range(0, n_rows, rb):
            acc = None
            for k in range(taps):
                off = (taps - 1 - k) if transposed else (k - (taps - 1))
                wk = jnp.tile(wrep[pl.ds(k * SUB, SUB), lanes], (rb // SUB, 1))
                term = wk * _rows_at(win, shf, base + r0 + off, rb, lanes)
                acc = term if acc is None else acc + term
            if bias_ref is not None:
                acc = acc + bias_ref[:, lanes]
            out[pl.ds(r0, rb), lanes] = acc.astype(out.dtype)
        return carry

    lax.fori_loop(0, width // LANES, lane_body, 0)


def _conv_bwd_taps(win, wrep, x_cur, dx_out, dw_acc, *, taps, n_rows, width, shf=None):
    rb = min(32 if taps > 8 else 64, n_rows)

    def lane_body(cb, carry):
        lanes = pl.ds(pl.multiple_of(cb * LANES, LANES), LANES)
        if shf is not None:
            _shift_copies(win, shf, lanes)
        sums = [None] * taps
        for r0 in range(0, n_rows, rb):
            xv = x_cur[pl.ds(r0, rb), lanes].astype(F32)
            acc = None
            for k in range(taps):
                shifted = _rows_at(win, shf, r0 + taps - 1 - k, rb, lanes)
                term = jnp.tile(wrep[pl.ds(k * SUB, SUB), lanes], (rb // SUB, 1)) * shifted
                acc = term if acc is None else acc + term
                part = _rows8(xv * shifted)
                sums[k] = part if sums[k] is None else sums[k] + part
            dx_out[pl.ds(r0, rb), lanes] = acc.astype(dx_out.dtype)
        for k in range(taps):
            dw_acc[pl.ds(k * SUB, SUB), lanes] += sums[k]
        return carry

    lax.fori_loop(0, width // LANES, lane_body, 0)


def _fold8(acc_ref, taps):
    return jnp.concatenate(
        [jnp.sum(acc_ref[pl.ds(k * SUB, SUB), :], axis=0, keepdims=True) for k in range(taps)], axis=0)


def _seq_tile(s_len):
    return min(512, s_len)


def _mm_tile(s_len):
    return min(512, s_len)


def _ff_chunk(ff):
    best = LANES
    for c in range(LANES, 1408 + 1, LANES):
        if ff % c == 0:
            best = c
    return best


def _col_tile(n):
    for c in (512, 1408, 256, LANES):
        if n % c == 0:
            return c
    return n


def _rms_matmul(x, g, wt, b, *, name, dep=None):
    s_len, d = x.shape
    n = wt.shape[0]
    tm = _mm_tile(s_len)
    cn = _col_tile(n)
    has_bias = b is not None

    def body(*refs):
        x_ref, g_ref, w_ref = refs[0:3]
        b_ref = refs[3] if has_bias else None
        o_ref, h_ref = refs[-2:]

        def blk(rows):
            xv = x_ref[rows, :]
            r = lax.rsqrt(jnp.mean(xv * xv, axis=-1, keepdims=True) + EPS)
            h_ref[rows, :] = ((xv * r) * g_ref[...]).astype(BF16)

        rb = min(128, tm)
        for r0 in range(0, tm, rb):
            blk(pl.ds(r0, rb))
        for j in range(n // cn):
            acc = _dot_nt(h_ref[...], w_ref[j * cn:(j + 1) * cn, :])
            if has_bias:
                acc = acc + b_ref[:, j * cn:(j + 1) * cn]
            o_ref[:, j * cn:(j + 1) * cn] = acc.astype(BF16)

    in_specs = [pl.BlockSpec((tm, d), lambda i: (i, 0)), _resident((1, d), lambda i: (0, 0)),
                _resident((n, d), lambda i: (0, 0))]
    args = [x, g, wt]
    if has_bias:
        in_specs.append(_resident((1, n), lambda i: (0, 0)))
        args.append(b)
    in_specs.append(ANY)
    args.append(x if dep is None else dep)
    return pl.pallas_call(
        body, grid=(s_len // tm,), in_specs=in_specs,
        out_specs=[pl.BlockSpec((tm, n), lambda i: (i, 0)), pl.BlockSpec((tm, d), lambda i: (i, 0))],
        out_shape=[jax.ShapeDtypeStruct((s_len, n), BF16), jax.ShapeDtypeStruct((s_len, d), BF16)],
        compiler_params=_params("parallel"), name=name,
    )(*args)


def _mix_windows(u_ref, uh_ref, gw, pw, first, t):
    c = D_CONF
    uh = uh_ref[...].astype(F32)
    gw[0:HALO, :] = jnp.where(first, 0.0, uh[:, 0:c] * _sigmoid(uh[:, c:2 * c]))
    pw[0:HALO3, :] = jnp.where(first, 0.0, uh[HALO - HALO3:HALO, 3 * c:4 * c] * uh[HALO - HALO3:HALO, 4 * c:5 * c])

    def blk(rows):
        dst = pl.ds(pl.multiple_of(rows.start + HALO, SUB), rows.size)
        gw[dst, :] = u_ref[rows, 0:c].astype(F32) * _sigmoid(u_ref[rows, c:2 * c].astype(F32))
        dst3 = pl.ds(pl.multiple_of(rows.start + HALO3, SUB), rows.size)
        pw[dst3, :] = u_ref[rows, 3 * c:4 * c].astype(F32) * u_ref[rows, 4 * c:5 * c].astype(F32)
    _row_loop(t, 64, blk)


def _mix_fwd(u, x0, wa, ba, lg, lb, wb, w_out, *, name):
    s_len, d_in = u.shape
    d = x0.shape[1]
    c = D_CONF
    t = _seq_tile(s_len)
    per = t // HALO

    def body(u_ref, uh_ref, x0_ref, wa_ref, ba_ref, lg_ref, lb_ref, wb_ref, wo_ref, y_ref, x1_ref, ca, cb,
             gw, pw, wrep_a, wrep_b, shf):
        first = pl.program_id(0) == 0
        _mix_windows(u_ref, uh_ref, gw, pw, first, t)
        _replicate_taps(wa_ref, wrep_a, CONF_K)
        _replicate_taps(wb_ref, wrep_b, SHORT_K)
        _conv_taps(gw, wrep_a, ca, taps=CONF_K, n_rows=t, base=HALO, width=c, bias_ref=ba_ref, shf=shf)
        _conv_taps(pw, wrep_b, cb, taps=SHORT_K, n_rows=t, base=HALO3, width=c)

        def blk(rows):
            cv = ca[rows, :]
            mu = jnp.mean(cv, axis=-1, keepdims=True)
            xc = cv - mu
            var = jnp.mean(xc * xc, axis=-1, keepdims=True)
            ln = (xc * lax.rsqrt(var + EPS)) * lg_ref[...] + lb_ref[...]
            y_ref[rows, 0:c] = (ln * _sigmoid(ln)).astype(BF16)
            y_ref[rows, c:2 * c] = (u_ref[rows, 2 * c:3 * c].astype(F32) * cb[rows, :]).astype(BF16)
        half = t // 2
        rb = min(64, half)
        for lo in range(0, t, half):
            for r0 in range(lo, lo + half, rb):
                blk(pl.ds(r0, rb))
            x1_ref[lo:lo + half, :] = x0_ref[lo:lo + half, :] + _dot(y_ref[lo:lo + half, :], wo_ref[...])

    small = lambda r: _resident((r, c), lambda i: (0, 0))
    return pl.pallas_call(
        body, grid=(s_len // t,),
        in_specs=[pl.BlockSpec((t, d_in), lambda i: (i, 0)),
                  pl.BlockSpec((HALO, d_in), lambda i: (jnp.maximum(i * per - 1, 0), 0)),
                  pl.BlockSpec((t, d), lambda i: (i, 0)),
                  small(CONF_K), small(1), small(1), small(1), small(SHORT_K),
                  _resident((2 * c, d), lambda i: (0, 0))],
        out_specs=[pl.BlockSpec((t, 2 * c), lambda i: (i, 0)), pl.BlockSpec((t, d), lambda i: (i, 0)),
                   pl.BlockSpec((t, c), lambda i: (i, 0)), pl.BlockSpec((t, c), lambda i: (i, 0))],
        out_shape=[jax.ShapeDtypeStruct((s_len, 2 * c), BF16), jax.ShapeDtypeStruct((s_len, d), F32),
                   jax.ShapeDtypeStruct((s_len, c), F32), jax.ShapeDtypeStruct((s_len, c), F32)],
        scratch_shapes=[pltpu.VMEM((HALO + t, c), F32), pltpu.VMEM((HALO3 + t, c), F32),
                        pltpu.VMEM((CONF_K * SUB, c), F32), pltpu.VMEM((SHORT_K * SUB, c), F32),
                        pltpu.VMEM((SUB - 1, HALO + t, c), F32)],
        compiler_params=_params("arbitrary"), name=name,
    )(u, u, x0, wa, ba, lg, lb, wb, w_out)


def _ffn_fwd(uf, x1, wf, w_down, *, name):
    s_len, ff2 = uf.shape
    ff = ff2 // 2
    d = x1.shape[1]
    t = _seq_tile(s_len)
    fc = _ff_chunk(ff)
    nc = ff // fc
    per = t // HALO3_BLK
    half = t // 2
    rb = min(64, half)

    def body(ug_ref, ugh_ref, uv_ref, uvh_ref, x1_ref, wfg_ref, wfv_ref, wd_ref,
             act_ref, x2_ref, cg_ref, cv_ref, gwin, vwin, wrep_g, wrep_v):
        first = pl.program_id(0) == 0
        first_chunk = pl.program_id(1) == 0
        lo8 = HALO3_BLK - HALO3
        gwin[0:HALO3, :] = jnp.where(first, 0.0, ugh_ref[...].astype(F32)[lo8:HALO3_BLK])
        vwin[0:HALO3, :] = jnp.where(first, 0.0, uvh_ref[...].astype(F32)[lo8:HALO3_BLK])
        _replicate_taps(wfg_ref, wrep_g, SHORT_K)
        _replicate_taps(wfv_ref, wrep_v, SHORT_K)
        chunk_rows = pl.ds(pl.multiple_of(pl.program_id(1) * fc, fc), fc)

        def conv(win, wrep, r0, lanes):
            acc = None
            for k in range(SHORT_K):
                wk = jnp.tile(wrep[k * SUB:(k + 1) * SUB, lanes], (rb // SUB, 1))
                off = HALO3 + r0 + k - (SHORT_K - 1)
                term = wk * win[off:off + rb, lanes]
                acc = term if acc is None else acc + term
            return acc

        for lo in range(0, t, half):
            for r0 in range(lo, lo + half, rb):
                gwin[HALO3 + r0:HALO3 + r0 + rb, :] = ug_ref[r0:r0 + rb, :].astype(F32)
                vwin[HALO3 + r0:HALO3 + r0 + rb, :] = uv_ref[r0:r0 + rb, :].astype(F32)
            for cb in range(fc // LANES):
                lanes = slice(cb * LANES, (cb + 1) * LANES)
                for r0 in range(lo, lo + half, rb):
                    gv = conv(gwin, wrep_g, r0, lanes)
                    vv = conv(vwin, wrep_v, r0, lanes)
                    cg_ref[r0:r0 + rb, lanes] = gv.astype(BF16)
                    cv_ref[r0:r0 + rb, lanes] = vv.astype(BF16)
                    act_ref[r0:r0 + rb, lanes] = ((gv * _sigmoid(gv)) * vv).astype(BF16)
            base = jnp.where(first_chunk, x1_ref[lo:lo + half, :], x2_ref[lo:lo + half, :])
            x2_ref[lo:lo + half, :] = base + _dot(act_ref[lo:lo + half, :], wd_ref[chunk_rows, :])

    halo_map = lambda off: (lambda i, j: (jnp.maximum(i * per - 1, 0), j + off))
    return pl.pallas_call(
        body, grid=(s_len // t, nc),
        in_specs=[pl.BlockSpec((t, fc), lambda i, j: (i, j)), pl.BlockSpec((HALO3_BLK, fc), halo_map(0)),
                  pl.BlockSpec((t, fc), lambda i, j: (i, j + nc)), pl.BlockSpec((HALO3_BLK, fc), halo_map(nc)),
                  pl.BlockSpec((t, d), lambda i, j: (i, 0)),
                  pl.BlockSpec((SHORT_K, fc), lambda i, j: (0, j)),
                  pl.BlockSpec((SHORT_K, fc), lambda i, j: (0, j + nc)),
                  _resident((ff, d), lambda i, j: (0, 0))],
        out_specs=[pl.BlockSpec((t, fc), lambda i, j: (i, j)), pl.BlockSpec((t, d), lambda i, j: (i, 0)),
                   pl.BlockSpec((t, fc), lambda i, j: (i, j)), pl.BlockSpec((t, fc), lambda i, j: (i, j))],
        out_shape=[jax.ShapeDtypeStruct((s_len, ff), BF16), jax.ShapeDtypeStruct((s_len, d), F32),
                   jax.ShapeDtypeStruct((s_len, ff), BF16), jax.ShapeDtypeStruct((s_len, ff), BF16)],
        scratch_shapes=[pltpu.VMEM((HALO3 + t, fc), F32), pltpu.VMEM((HALO3 + t, fc), F32),
                        pltpu.VMEM((SHORT_K * SUB, fc), F32), pltpu.VMEM((SHORT_K * SUB, fc), F32)],
        compiler_params=_params("parallel", "arbitrary"), name=name,
    )(uf, uf, uf, uf, x1, wf, wf, w_down)


def _loss_bwd(x, g, target, *, name):
    s_len, d = x.shape
    t = _seq_tile(s_len)

    def body(x_ref, g_ref, t_ref, l_ref, dx_ref, dxb_ref, dg_ref):
        @pl.when(pl.program_id(0) == 0)
        def _():
            l_ref[...] = jnp.zeros_like(l_ref)
            dg_ref[...] = jnp.zeros_like(dg_ref)

        def blk(rows):
            xv = x_ref[rows, :]
            r = lax.rsqrt(jnp.mean(xv * xv, axis=-1, keepdims=True) + EPS)
            xn = xv * r
            e = xn * g_ref[...] - t_ref[rows, :]
            l_ref[...] += _rows8(e * e)
            dy = e * (1.0 / d)
            dg_ref[...] += _rows8(dy * xn)
            dn = dy * g_ref[...]
            dx = r * (dn - xn * jnp.mean(dn * xn, axis=-1, keepdims=True))
            dx_ref[rows, :] = dx
            dxb_ref[rows, :] = dx.astype(BF16)
        _row_loop(t, 64, blk)

    row = pl.BlockSpec((t, d), lambda i: (i, 0))
    part = pl.BlockSpec((SUB, d), lambda i: (0, 0))
    return pl.pallas_call(
        body, grid=(s_len // t,),
        in_specs=[row, _resident((1, d), lambda i: (0, 0)), row],
        out_specs=[part, row, row, part],
        out_shape=[jax.ShapeDtypeStruct((SUB, d), F32), jax.ShapeDtypeStruct((s_len, d), F32),
                   jax.ShapeDtypeStruct((s_len, d), BF16), jax.ShapeDtypeStruct((SUB, d), F32)],
        compiler_params=_params("arbitrary"), name=name,
    )(x, g, target)


def _ffn_bwd(dx2, uf, cg, cv, wf, w_down, *, name, dep=None):
    s_len, ff2 = uf.shape
    ff = ff2 // 2
    d = dx2.shape[1]
    t = _seq_tile(s_len)
    n_t = s_len // t
    fc = _ff_chunk(ff)
    nc = ff // fc

    def body(dx_ref, ug_ref, uv_ref, cg_ref, cv_ref, wfg_ref, wfv_ref, wd_ref, dep_ref,
             duf_ref, dwg_ref, dwv_ref, dact, dgw, dvw, awg, awv, wrep_g, wrep_v):
        i = pl.program_id(1)

        @pl.when(i == 0)
        def _():
            dgw[t:t + HALO3, :] = jnp.zeros((HALO3, fc), F32)
            dvw[t:t + HALO3, :] = jnp.zeros((HALO3, fc), F32)
            awg[...] = jnp.zeros_like(awg)
            awv[...] = jnp.zeros_like(awv)

        _replicate_taps(wfg_ref, wrep_g, SHORT_K)
        _replicate_taps(wfv_ref, wrep_v, SHORT_K)

        def blk(rows):
            gv = cg_ref[rows, :].astype(F32)
            sg = _sigmoid(gv)
            da = dact[rows, :]
            dgw[rows, :] = (da * cv_ref[rows, :].astype(F32)) * (sg * (1.0 + gv * (1.0 - sg)))
            dvw[rows, :] = da * (gv * sg)

        dact[...] = _dot_nt(dx_ref[...], wd_ref[...])
        _row_loop(t, 32, blk, unroll=2)

        _conv_bwd_taps(dgw, wrep_g, ug_ref, duf_ref.at[0], awg, taps=SHORT_K, n_rows=t, width=fc)
        _conv_bwd_taps(dvw, wrep_v, uv_ref, duf_ref.at[1], awv, taps=SHORT_K, n_rows=t, width=fc)
        dgw[t:t + HALO3, :] = dgw[0:HALO3, :]
        dvw[t:t + HALO3, :] = dvw[0:HALO3, :]

        @pl.when(i == n_t - 1)
        def _():
            dwg_ref[...] = _fold8(awg, SHORT_K)
            dwv_ref[...] = _fold8(awv, SHORT_K)

    rev = lambda i: n_t - 1 - i
    gate = pl.BlockSpec((t, fc), lambda j, i: (rev(i), j))
    value = pl.BlockSpec((t, fc), lambda j, i: (rev(i), j + nc))
    return pl.pallas_call(
        body, grid=(nc, n_t),
        in_specs=[pl.BlockSpec((t, d), lambda j, i: (rev(i), 0)), gate, value, gate, gate,
                  pl.BlockSpec((SHORT_K, fc), lambda j, i: (0, j)),
                  pl.BlockSpec((SHORT_K, fc), lambda j, i: (0, j + nc)),
                  pl.BlockSpec((fc, d), lambda j, i: (j, 0)), ANY],
        out_specs=[pl.BlockSpec((2, t, fc), lambda j, i: (0, rev(i), j)),
                   pl.BlockSpec((SHORT_K, fc), lambda j, i: (0, j)), pl.BlockSpec((SHORT_K, fc), lambda j, i: (0, j))],
        out_shape=[jax.ShapeDtypeStruct((2, s_len, ff), BF16),
                   jax.ShapeDtypeStruct((SHORT_K, ff), F32), jax.ShapeDtypeStruct((SHORT_K, ff), F32)],
        scratch_shapes=[pltpu.VMEM((t, fc), F32),
                        pltpu.VMEM((t + HALO3, fc), F32), pltpu.VMEM((t + HALO3, fc), F32),
                        pltpu.VMEM((SHORT_K * SUB, fc), F32), pltpu.VMEM((SHORT_K * SUB, fc), F32),
                        pltpu.VMEM((SHORT_K * SUB, fc), F32), pltpu.VMEM((SHORT_K * SUB, fc), F32)],
        compiler_params=_params("arbitrary", "arbitrary"), name=name,
    )(dx2, uf, uf, cg, cv, wf, wf, w_down, uf if dep is None else dep)


def _mix_bwd(dx1, u, ca, cb, wa, lg, lb, wb, w_out, *, name, dep=None):
    s_len, d_in = u.shape
    d = dx1.shape[1]
    c = D_CONF
    t = _seq_tile(s_len)
    n_t = s_len // t

    def body(dx_ref, u_ref, ca_ref, cb_ref, wa_ref, lg_ref, lb_ref, wb_ref, wo_ref, dep_ref,
             du_ref, dwa_ref, dwb_ref, dba_ref, dlg_ref, dlb_ref, dbin_ref,
             glu, prod, dyc, dcaw, dcbw, dglu, dp, awa, awb, wrep_a, wrep_b, shf):
        i = pl.program_id(0)
        _replicate_taps(wa_ref, wrep_a, CONF_K)
        _replicate_taps(wb_ref, wrep_b, SHORT_K)

        @pl.when(i == 0)
        def _():
            dcaw[t:t + HALO, :] = jnp.zeros((HALO, c), F32)
            dcbw[t:t + HALO3, :] = jnp.zeros((HALO3, c), F32)
            awa[...] = jnp.zeros_like(awa)
            awb[...] = jnp.zeros_like(awb)
            dba_ref[...] = jnp.zeros_like(dba_ref)
            dlg_ref[...] = jnp.zeros_like(dlg_ref)
            dlb_ref[...] = jnp.zeros_like(dlb_ref)
            dbin_ref[...] = jnp.zeros_like(dbin_ref)

        def blk1(rows):
            cv = ca_ref[rows, :]
            mu = jnp.mean(cv, axis=-1, keepdims=True)
            xc = cv - mu
            rstd = lax.rsqrt(jnp.mean(xc * xc, axis=-1, keepdims=True) + EPS)
            nrm = xc * rstd
            ln = nrm * lg_ref[...] + lb_ref[...]
            sg = _sigmoid(ln)
            dln = dyc[rows, 0:c] * (sg * (1.0 + ln * (1.0 - sg)))
            dlg_ref[...] += _rows8(dln * nrm)
            dlb_ref[...] += _rows8(dln)
            dn = dln * lg_ref[...]
            dca = rstd * (dn - jnp.mean(dn, axis=-1, keepdims=True)
                          - nrm * jnp.mean(dn * nrm, axis=-1, keepdims=True))
            dcaw[rows, :] = dca
            dba_ref[...] += _rows8(dca)
            ds = dyc[rows, c:2 * c]
            dgb = ds * cb_ref[rows, :]
            dcbw[rows, :] = ds * u_ref[rows, 2 * c:3 * c].astype(F32)
            du_ref[rows, 2 * c:3 * c] = dgb.astype(BF16)
            dbin_ref[:, 2 * c:3 * c] += _rows8(dgb)
            glu[rows, :] = u_ref[rows, 0:c].astype(F32) * _sigmoid(u_ref[rows, c:2 * c].astype(F32))
            prod[rows, :] = u_ref[rows, 3 * c:4 * c].astype(F32) * u_ref[rows, 4 * c:5 * c].astype(F32)
        half = t // 2
        rb = min(64, half)
        for lo in range(0, t, half):
            dyc[lo:lo + half, :] = _dot_nt(dx_ref[lo:lo + half, :], wo_ref[...])
            for r0 in range(lo, lo + half, rb):
                blk1(pl.ds(r0, rb))

        _conv_bwd_taps(dcaw, wrep_a, glu, dglu, awa, taps=CONF_K, n_rows=t, width=c, shf=shf)
        _conv_bwd_taps(dcbw, wrep_b, prod, dp, awb, taps=SHORT_K, n_rows=t, width=c)
        dcaw[t:t + HALO, :] = dcaw[0:HALO, :]
        dcbw[t:t + HALO3, :] = dcbw[0:HALO3, :]

        def blk2(rows):
            av = u_ref[rows, 0:c].astype(F32)
            sg = _sigmoid(u_ref[rows, c:2 * c].astype(F32))
            dg = dglu[rows, :]
            d_av = dg * sg
            d_ag = (dg * av) * (sg * (1.0 - sg))
            dpv = dp[rows, :]
            d_gc = dpv * u_ref[rows, 4 * c:5 * c].astype(F32)
            d_vs = dpv * u_ref[rows, 3 * c:4 * c].astype(F32)
            du_ref[rows, 0:c] = d_av.astype(BF16)
            du_ref[rows, c:2 * c] = d_ag.astype(BF16)
            du_ref[rows, 3 * c:4 * c] = d_gc.astype(BF16)
            du_ref[rows, 4 * c:5 * c] = d_vs.astype(BF16)
            dbin_ref[:, 0:c] += _rows8(d_av)
            dbin_ref[:, c:2 * c] += _rows8(d_ag)
            dbin_ref[:, 3 * c:4 * c] += _rows8(d_gc)
            dbin_ref[:, 4 * c:5 * c] += _rows8(d_vs)
        _row_loop(t, 64, blk2)

        @pl.when(i == n_t - 1)
        def _():
            dwa_ref[...] = _fold8(awa, CONF_K)
            dwb_ref[...] = _fold8(awb, SHORT_K)

    rev = lambda i: n_t - 1 - i
    small_in = lambda r: _resident((r, c), lambda i: (0, 0))
    small = lambda r: pl.BlockSpec((r, c), lambda i: (0, 0))
    return pl.pallas_call(
        body, grid=(n_t,),
        in_specs=[pl.BlockSpec((t, d), lambda i: (rev(i), 0)),
                  pl.BlockSpec((t, d_in), lambda i: (rev(i), 0)),
                  pl.BlockSpec((t, c), lambda i: (rev(i), 0)), pl.BlockSpec((t, c), lambda i: (rev(i), 0)),
                  small_in(CONF_K), small_in(1), small_in(1), small_in(SHORT_K),
                  _resident((2 * c, d), lambda i: (0, 0)), ANY],
        out_specs=[pl.BlockSpec((t, d_in), lambda i: (rev(i), 0)),
                   small(CONF_K), small(SHORT_K), small(SUB), small(SUB), small(SUB),
                   pl.BlockSpec((SUB, d_in), lambda i: (0, 0))],
        out_shape=[jax.ShapeDtypeStruct((s_len, d_in), BF16),
                   jax.ShapeDtypeStruct((CONF_K, c), F32), jax.ShapeDtypeStruct((SHORT_K, c), F32),
                   jax.ShapeDtypeStruct((SUB, c), F32), jax.ShapeDtypeStruct((SUB, c), F32),
                   jax.ShapeDtypeStruct((SUB, c), F32), jax.ShapeDtypeStruct((SUB, d_in), F32)],
        scratch_shapes=[pltpu.VMEM((t, c), F32), pltpu.VMEM((t, c), F32), pltpu.VMEM((t, 2 * c), F32),
                        pltpu.VMEM((t + HALO, c), F32), pltpu.VMEM((t + HALO3, c), F32),
                        pltpu.VMEM((t, c), F32), pltpu.VMEM((t, c), F32),
                        pltpu.VMEM((CONF_K * SUB, c), F32), pltpu.VMEM((SHORT_K * SUB, c), F32),
                        pltpu.VMEM((CONF_K * SUB, c), F32), pltpu.VMEM((SHORT_K * SUB, c), F32),
                        pltpu.VMEM((SUB - 1, t + HALO, c), F32)],
        compiler_params=_params("arbitrary"), name=name,
    )(dx1, u, ca, cb, wa, lg, lb, wb, w_out, u if dep is None else dep)


def _matmul_tn(a, b, *, name):
    n_p, s_len, k = a.shape
    n = b.shape[1]
    tk = _col_tile(k)
    per = k // tk

    def body(a_ref, b_ref, o_ref):
        o_ref[...] = _dot_tn(a_ref[...], b_ref[...]).astype(BF16)

    return pl.pallas_call(
        body, grid=(n_p, per),
        in_specs=[pl.BlockSpec((None, s_len, tk), lambda p, j: (p, 0, j)), _resident((s_len, n), lambda p, j: (0, 0))],
        out_specs=pl.BlockSpec((tk, n), lambda p, j: (p * per + j, 0)),
        out_shape=jax.ShapeDtypeStruct((n_p * k, n), BF16),
        compiler_params=_params("parallel", "parallel"), name=name,
    )(a, b)


def _matmul_rmsbwd(dzs, wt, x, g, dx_in, *, name, dep=None):
    s_len, d = x.shape
    n_z, _, nj = dzs.shape
    t = _mm_tile(s_len)

    def body(*refs):
        dz_refs = refs[0:n_z]
        w_refs = refs[n_z:2 * n_z]
        x_ref, g_ref, dxi_ref, _, dx_ref, dxb_ref, dg_ref, dh = refs[2 * n_z:]

        @pl.when(pl.program_id(0) == 0)
        def _():
            dg_ref[...] = jnp.zeros_like(dg_ref)

        def blk(rows):
            xv = x_ref[rows, :]
            r = lax.rsqrt(jnp.mean(xv * xv, axis=-1, keepdims=True) + EPS)
            xn = xv * r
            dhv = dh[rows, :]
            dg_ref[...] += _rows8(dhv * xn)
            dn = dhv * g_ref[...]
            dx = dxi_ref[rows, :] + r * (dn - xn * jnp.mean(dn * xn, axis=-1, keepdims=True))
            dx_ref[rows, :] = dx
            dxb_ref[rows, :] = dx.astype(BF16)

        half = t // 2
        rb = min(128, half)
        for lo in range(0, t, half):
            acc = _dot(dz_refs[0][lo:lo + half, :], w_refs[0][...])
            for q in range(1, n_z):
                acc = acc + _dot(dz_refs[q][lo:lo + half, :], w_refs[q][...])
            dh[lo:lo + half, :] = acc
            for r0 in range(lo, lo + half, rb):
                blk(pl.ds(r0, rb))

    row = pl.BlockSpec((t, d), lambda i: (i, 0))
    in_specs = [pl.BlockSpec((None, t, nj), functools.partial(lambda q, i: (q, i, 0), q)) for q in range(n_z)]
    in_specs += [_resident((nj, d), functools.partial(lambda q, i: (q, 0), q)) for q in range(n_z)]
    in_specs += [row, _resident((1, d), lambda i: (0, 0)), row, ANY]
    return pl.pallas_call(
        body, grid=(s_len // t,), in_specs=in_specs,
        out_specs=[row, row, pl.BlockSpec((SUB, d), lambda i: (0, 0))],
        out_shape=[jax.ShapeDtypeStruct((s_len, d), F32), jax.ShapeDtypeStruct((s_len, d), BF16),
                   jax.ShapeDtypeStruct((SUB, d), F32)],
        scratch_shapes=[pltpu.VMEM((t, d), F32)],
        compiler_params=_params("arbitrary"), name=name,
    )(*([dzs] * n_z), *([wt] * n_z), x, g, dx_in, x if dep is None else dep)


def _row(v):
    return v.reshape(1, -1)


def _layer_fwd(x0, p, tag, dep=None, before_up=None):
    u, h1 = _rms_matmul(x0, _row(p["mix_norm_g"]), p["w_in_t"], _row(p["b_in"]), name=f"in_proj_{tag}", dep=dep)
    ycat, x1, ca, cb = _mix_fwd(u, x0, p["conv_a_w"], _row(p["conv_a_b"]), _row(p["ln_a_g"]), _row(p["ln_a_b"]),
                            p["conv_b_w"], p["w_out"], name=f"mix_fwd_{tag}")
    if before_up is not None:
        before_up(x1)
    uf, h2 = _rms_matmul(x1, _row(p["ffn_norm_g"]), p["w_up_t"], None, name=f"up_proj_{tag}")
    act, x2, cg, cv = _ffn_fwd(uf, x1, p["conv_f_w"], p["w_down"], name=f"ffn_fwd_{tag}")
    return x2, dict(x0=x0, h1=h1, u=u, ca=ca, cb=cb, ycat=ycat, x1=x1, h2=h2, uf=uf, cg=cg, cv=cv, act=act)


def _layer_bwd(dx2, dx2_b, p, saved, tag, ffn_grads, ffn_sent, mix_grads, mix_sent, dep=None):
    d_uf, dwf_g, dwf_v = _ffn_bwd(dx2_b, saved["uf"], saved["cg"], saved["cv"], p["conv_f_w"], p["w_down"],
                                  name=f"ffn_bwd_{tag}", dep=dep)
    g_down = _matmul_tn(saved["act"][None], dx2_b, name=f"dw_down_{tag}")
    g_up = _matmul_tn(d_uf, saved["h2"], name=f"dw_up_{tag}")
    dep_ffn = ffn_grads(dict(w_up=g_up, w_down=g_down), dx2_b)
    dx1, dx1_b, dg2 = _matmul_rmsbwd(d_uf, p["w_up_t"], saved["x1"], _row(p["ffn_norm_g"]), dx2,
                                     name=f"dh_ffn_{tag}", dep=dep_ffn)
    du, dwa, dwb, dba, dlg, dlb, dbin = _mix_bwd(
        dx1_b, saved["u"], saved["ca"], saved["cb"], p["conv_a_w"], _row(p["ln_a_g"]), _row(p["ln_a_b"]),
        p["conv_b_w"], p["w_out"], name=f"mix_bwd_{tag}", dep=ffn_sent(dx1_b))
    g_out = _matmul_tn(saved["ycat"][None], dx1_b, name=f"dw_out_{tag}")
    g_in = _matmul_tn(du[None], saved["h1"], name=f"dw_in_{tag}")
    conv = dict(conv_a_w=dwa, conv_b_w=dwb, conv_f_w=jnp.concatenate([dwf_g, dwf_v], axis=1))
    dep_mix = mix_grads(dict(w_in=g_in, w_out=g_out), conv, dx1_b)
    dx0, dx0_b, dg1 = _matmul_rmsbwd(du[None], p["w_in_t"], saved["x0"], _row(p["mix_norm_g"]), dx1,
                                     name=f"dh_mix_{tag}", dep=dep_mix)
    rep = dict(mix_norm_g=dg1, b_in=dbin, conv_a_b=dba, ln_a_g=dlg, ln_a_b=dlb, ffn_norm_g=dg2)
    return dx0, dx0_b, rep, mix_sent(dx0_b)


def _place():
    return lax.axis_index("x"), lax.axis_index("y"), lax.axis_index("c")


def _all_gather(arrs, *, name):
    n_a = len(arrs)

    def body(*refs):
        ins = refs[0:n_a]
        outs = refs[n_a:2 * n_a]
        send_sems, recv_sems, local_sems = refs[2 * n_a:]
        x, y, c = _place()
        sibling = (x, y, 1 - c)
        chips = [(1 - x, y), (x, 1 - y), (1 - x, 1 - y)]

        def slot(a, px, py, pc):
            return outs[a].at[4 * px + 2 * py + pc]

        def copy(a, k, block, to, src=None):
            return pltpu.make_async_remote_copy(
                src_ref=slot(a, *block) if src is None else src, dst_ref=slot(a, *block),
                send_sem=send_sems.at[a, k], recv_sem=recv_sems.at[a, k],
                device_id=to, device_id_type=MESH)

        me = (x, y, c)
        mine = [pltpu.make_async_copy(ins[a], slot(a, *me), local_sems.at[a]) for a in range(n_a)]
        for cp in mine:
            cp.start()
        started = []
        for a in range(n_a):
            first = [copy(a, 0, me, sibling, src=ins[a])]
            first += [copy(a, 1 + j, me, (*chip, c), src=ins[a]) for j, chip in enumerate(chips)]
            for cp in first:
                cp.start()
            started += first
        for a in range(n_a):
            for j, chip in enumerate(chips):
                copy(a, 1 + j, (*chip, c), me).wait_recv()
                passed = copy(a, 4 + j, (*chip, c), sibling)
                passed.start()
                started.append(passed)
        for a in range(n_a):
            copy(a, 0, sibling, me).wait_recv()
            for j, chip in enumerate(chips):
                copy(a, 4 + j, (*chip, 1 - c), me).wait_recv()
        for cp in started:
            cp.wait_send()
        for cp in mine:
            cp.wait()

    return pl.pallas_call(
        body, in_specs=[ANY] * n_a, out_specs=[ANY] * n_a,
        out_shape=[jax.ShapeDtypeStruct((N_DEV, *a.shape), a.dtype) for a in arrs],
        scratch_shapes=[pltpu.SemaphoreType.DMA((n_a, 7)), pltpu.SemaphoreType.DMA((n_a, 7)),
                        pltpu.SemaphoreType.DMA((n_a,))],
        name=name,
    )(*arrs)


def _row_tile(r, cap):
    for tr in range(min(cap, r) // 16 * 16, 0, -16):
        if r % tr == 0:
            return tr
    return r


def _pair_sum(mines, theirs, where, *, name):
    n_a = len(mines)
    n_chip = mines[0].shape[0]

    def body(where_ref, *refs):
        a_refs = refs[0:n_a]
        b_refs = refs[n_a:2 * n_a]
        p_refs = refs[2 * n_a:3 * n_a]
        l_refs = refs[3 * n_a:4 * n_a]
        q = pl.program_id(0)
        for a in range(n_a):
            p_refs[a][...] = (a_refs[a][...].astype(F32) + b_refs[a][...].astype(F32)).astype(p_refs[a].dtype)

        @pl.when(q == where_ref[1])
        def _():
            for a in range(n_a):
                l_refs[a][...] = p_refs[a][...]

    in_specs, out_p, out_l, shapes = [], [], [], []
    for m in mines:
        _, _, r, c = m.shape
        in_specs.append(pl.BlockSpec((None, None, r, c), lambda q, where_ref: (q, where_ref[0], 0, 0)))
    for m in mines:
        _, _, r, c = m.shape
        in_specs.append(pl.BlockSpec((None, r, c), lambda q, where_ref: (q, 0, 0)))
        out_p.append(pl.BlockSpec((None, r, c), lambda q, where_ref: (q, 0, 0)))
        out_l.append(pl.BlockSpec((None, r, c), lambda q, where_ref: (where_ref[1], 0, 0)))
        shapes.append(jax.ShapeDtypeStruct((n_chip, r, c), m.dtype))
    res = pl.pallas_call(
        body,
        grid_spec=pltpu.PrefetchScalarGridSpec(num_scalar_prefetch=1, grid=(n_chip,), in_specs=in_specs,
                                               out_specs=out_p + out_l),
        out_shape=shapes + shapes,
        compiler_params=_params("arbitrary"), name=name,
    )(where, *mines, *theirs)
    return list(res[:n_a]), list(res[n_a:])


HBM = pl.BlockSpec(memory_space=pltpu.HBM)
SEM = pl.BlockSpec(memory_space=pltpu.SEMAPHORE)
EFFECT = pltpu.SideEffectType.DATAFLOW_SIDE_EFFECTING


def _in_hbm(a):
    return pltpu.with_memory_space_constraint(a, pltpu.HBM)


def _split_start(srcs, lands, plan, n_copies, after, *, name):
    n_s, n_l = len(srcs), len(lands)

    def body(*refs):
        src_refs = refs[0:n_s]
        land_refs = refs[n_s:n_s + n_l]
        send_sems, recv_sems = refs[n_s + n_l + 1], refs[n_s + n_l + 2]
        token = refs[-1]
        for cp in plan(src_refs, land_refs, send_sems, recv_sems):
            cp.start()
        token[...] = jnp.zeros_like(token)

    thru = [pltpu.HBM(a.shape, a.dtype) for a in list(srcs) + list(lands)]
    res = pl.pallas_call(
        body, name=name,
        out_shape=(pltpu.SemaphoreType.DMA((n_copies,)), pltpu.SemaphoreType.DMA((n_copies,)), *thru,
                   jax.ShapeDtypeStruct((SUB, LANES), F32)),
        in_specs=[HBM] * (n_s + n_l) + [ANY],
        out_specs=(SEM, SEM, *([HBM] * (n_s + n_l)), pl.BlockSpec(memory_space=pltpu.VMEM)),
        input_output_aliases={i: 2 + i for i in range(n_s + n_l)},
        compiler_params=pltpu.CompilerParams(has_side_effects=EFFECT),
    )(*[_in_hbm(a) for a in srcs], *[_in_hbm(a) for a in lands], _in_hbm(after))
    return res[0], res[1], list(res[2:2 + n_s]), list(res[2 + n_s:2 + n_s + n_l]), res[-1]


def _split_wait(send_sems, recv_sems, srcs, lands, after, plan, *, name):
    n_s, n_l = len(srcs), len(lands)

    def body(*refs):
        src_refs = refs[0:n_s]
        land_refs = refs[n_s:n_s + n_l]
        send, recv = refs[n_s + n_l], refs[n_s + n_l + 1]
        for cp in plan(src_refs, land_refs, send, recv):
            cp.wait_send()
            cp.wait_recv()

    res = pl.pallas_call(
        body, name=name,
        out_shape=tuple(pltpu.HBM(a.shape, a.dtype) for a in list(srcs) + list(lands)),
        in_specs=[HBM] * (n_s + n_l) + [SEM, SEM, ANY],
        out_specs=tuple([HBM] * (n_s + n_l)),
        input_output_aliases={i: i for i in range(n_s + n_l)},
        compiler_params=pltpu.CompilerParams(has_side_effects=EFFECT),
    )(*srcs, *lands, send_sems, recv_sems, _in_hbm(after))
    return list(res[:n_s]), list(res[n_s:])


def _remote(src, dst, send_sems, recv_sems, k, to):
    return pltpu.make_async_remote_copy(src_ref=src, dst_ref=dst, send_sem=send_sems.at[k], recv_sem=recv_sems.at[k],
                                        device_id=to, device_id_type=MESH)


def _gather_plan_first(src_refs, land_refs, send_sems, recv_sems):
    x, y, c = _place()
    me = 4 * x + 2 * y + c
    peers = [(x, y, 1 - c), (1 - x, y, c), (x, 1 - y, c), (1 - x, 1 - y, c)]
    return [_remote(src, land.at[me], send_sems, recv_sems, 4 * a + k, to)
            for a, (src, land) in enumerate(zip(src_refs, land_refs)) for k, to in enumerate(peers)]


def _gather_plan_second(src_refs, land_refs, send_sems, recv_sems):
    x, y, c = _place()
    chips = [(1 - x, y), (x, 1 - y), (1 - x, 1 - y)]
    out = []
    for a, land in enumerate(land_refs):
        for j, (px, py) in enumerate(chips):
            slot = land.at[4 * px + 2 * py + c]
            out.append(_remote(slot, slot, send_sems, recv_sems, 3 * a + j, (x, y, 1 - c)))
    return out


def _gather_plan_all(src_refs, land_refs, send_sems, recv_sems):
    x, y, c = _place()
    me = 4 * x + 2 * y + c
    peers = [(px, py, pc) for px in (x, 1 - x) for py in (y, 1 - y) for pc in (c, 1 - c)][1:]
    return [_remote(src, land.at[me], send_sems, recv_sems, len(peers) * a + k, to)
            for a, (src, land) in enumerate(zip(src_refs, land_refs)) for k, to in enumerate(peers)]


def _siblings_plan(src_refs, land_refs, send_sems, recv_sems):
    x, y, c = _place()
    return [_remote(src.at[:, 1 - c], land, send_sems, recv_sems, a, (x, y, 1 - c))
            for a, (src, land) in enumerate(zip(src_refs, land_refs))]


def _chips_plan(src_refs, land_refs, send_sems, recv_sems):
    x, y, c = _place()
    my_chip = 2 * x + y
    chips = [(1 - x, y), (x, 1 - y), (1 - x, 1 - y)]
    return [_remote(src.at[2 * px + py], land.at[my_chip], send_sems, recv_sems, 3 * a + j, (px, py, c))
            for a, (src, land) in enumerate(zip(src_refs, land_refs)) for j, (px, py) in enumerate(chips)]


def _gather_landings(shards, me, *, name):
    blank = _unwritten([jax.ShapeDtypeStruct((N_DEV, *s.shape), s.dtype) for s in shards], name=name)
    return [lax.dynamic_update_index_in_dim(b, s, me, 0) for b, s in zip(blank, shards)]


def _adamw_math(g, w, m, v):
    m = ADAM_B1 * m + (1.0 - ADAM_B1) * g
    v = ADAM_B2 * v + (1.0 - ADAM_B2) * (g * g)
    m_hat = m / (1.0 - ADAM_B1 ** ADAM_STEP)
    v_hat = v / (1.0 - ADAM_B2 ** ADAM_STEP)
    delta = -ADAM_LR * (m_hat / (jnp.sqrt(v_hat) + ADAM_EPS) + ADAM_WD * w)
    return delta, m, v


def _adamw_sharded(parts, w, m, v, *, name, dep=None):
    n_layers, r, c = w.shape
    n_chip = parts[0].shape[0]
    tr = _row_tile(r, 384)
    n_i = r // tr

    def body(*refs):
        p_refs = refs[0:n_layers]
        w_ref, m_ref, v_ref, _, g_out, d_out, m_out, v_out = refs[n_layers:]
        layer = pl.program_id(0)
        for l in range(n_layers):
            @pl.when(layer == l)
            def _(l=l):
                g = p_refs[l][0].astype(F32)
                for q in range(1, n_chip):
                    g = g + p_refs[l][q].astype(F32)
                delta, m_new, v_new = _adamw_math(g, w_ref[...], m_ref[...], v_ref[...])
                g_out[...] = g
                d_out[...] = delta
                m_out[...] = m_new
                v_out[...] = v_new

    def part_map(l):
        return lambda layer, i: (0, jnp.where(layer == l, i, jnp.where(layer < l, 0, n_i - 1)), 0)

    blk = pl.BlockSpec((None, tr, c), lambda layer, i: (layer, i, 0))
    return pl.pallas_call(
        body, grid=(n_layers, n_i),
        in_specs=[pl.BlockSpec((n_chip, tr, c), part_map(l)) for l in range(n_layers)] + [blk, blk, blk, ANY],
        out_specs=[blk] * 4, out_shape=[jax.ShapeDtypeStruct((n_layers, r, c), F32)] * 4,
        compiler_params=_params("arbitrary", "arbitrary"), name=name,
    )(*parts, w, m, v, w if dep is None else dep)


def _fold_partials(cols, *, name):
    widths = [c.shape[1] for c in cols]

    def body(*refs):
        o_ref = refs[-1]
        pos = 0
        for ref, width in zip(refs[:-1], widths):
            o_ref[:, pos:pos + width] = jnp.sum(ref[...], axis=0, keepdims=True)
            pos += width

    return pl.pallas_call(body, out_shape=jax.ShapeDtypeStruct((1, sum(widths)), F32), name=name)(*cols)


def _adamw_replicated(parts, names, w, m, v, n_loss, *, name):
    n_dev = parts.shape[0]
    n_layers = w[names[0]].shape[0]
    every = list(names) + ["final_norm_g"]
    n_p = len(every)

    def body(*refs):
        p_ref = refs[0]
        w_refs = dict(zip(every, refs[1:1 + n_p]))
        m_refs = dict(zip(every, refs[1 + n_p:1 + 2 * n_p]))
        v_refs = dict(zip(every, refs[1 + 2 * n_p:1 + 3 * n_p]))
        l_out = refs[1 + 3 * n_p]
        outs = refs[2 + 3 * n_p:]
        o_refs = {n: outs[4 * q:4 * q + 4] for q, n in enumerate(every)}
        acc = p_ref[0]
        for q in range(1, n_dev):
            acc = acc + p_ref[q]
        tot = jnp.sum(acc, axis=0, keepdims=True)
        pos = 0
        where = [(n, l) for l in range(n_layers) for n in names] + [("final_norm_g", 0)]
        for n, l in where:
            width = w_refs[n].shape[1]
            g = tot[:, pos:pos + width]
            pos += width
            row = pl.ds(l, 1)
            delta, m_new, v_new = _adamw_math(g, w_refs[n][row, :], m_refs[n][row, :], v_refs[n][row, :])
            for o, val in zip(o_refs[n], (g, delta, m_new, v_new)):
                o[row, :] = val
        l_out[...] = (0.5 / n_loss) * jnp.sum(tot[:, pos:pos + n_loss], axis=-1, keepdims=True)

    shapes = [jax.ShapeDtypeStruct((1, 1), F32)]
    for n in every:
        shapes += [jax.ShapeDtypeStruct(w[n].shape, F32)] * 4
    res = pl.pallas_call(
        body, out_shape=shapes,
        compiler_params=pltpu.CompilerParams(vmem_limit_bytes=VMEM_LIMIT), name=name,
    )(parts, *[w[n] for n in every], *[m[n] for n in every], *[v[n] for n in every])
    return res[0], {n: res[1 + 4 * q:5 + 4 * q] for q, n in enumerate(every)}


BIG = ("w_in", "w_out", "w_up", "w_down")
COL_SHARDED = ("w_in", "w_up")
CONV = ("conv_a_w", "conv_b_w", "conv_f_w")
REPLICATED = ("mix_norm_g", "b_in", "conv_a_b", "ln_a_g", "ln_a_b", "ffn_norm_g")
KINDS = ("grad", "delta", "m", "v")
FFN_PART = ("w_up", "w_down")
MIX_PART = ("w_in", "w_out")


def _weights_from_gathered(g):
    n_dev, r, c = g.shape
    return g.reshape(n_dev * r, c)


def _slabs_from_full(grad):
    return grad.reshape(N_DEV, grad.shape[0] // N_DEV, grad.shape[1])


def _unwritten(like, *, name):
    return pl.pallas_call(lambda *refs: None, out_specs=[ANY] * len(like), out_shape=list(like), name=name)()


def kernel(x, mix_norm_g, w_in, b_in, conv_a_w, conv_a_b, ln_a_g, ln_a_b, conv_b_w, w_out, ffn_norm_g, w_up, conv_f_w, w_down, final_norm_g, loss_target, m_mix_norm_g, m_w_in, m_b_in, m_conv_a_w, m_conv_a_b, m_ln_a_g, m_ln_a_b, m_conv_b_w, m_w_out, m_ffn_norm_g, m_w_up, m_conv_f_w, m_w_down, m_final_norm_g, v_mix_norm_g, v_w_in, v_b_in, v_conv_a_w, v_conv_a_b, v_ln_a_g, v_ln_a_b, v_conv_b_w, v_w_out, v_ffn_norm_g, v_w_up, v_conv_f_w, v_w_down, v_final_norm_g):
    w = dict(mix_norm_g=mix_norm_g, w_in=w_in, b_in=b_in, conv_a_w=conv_a_w, conv_a_b=conv_a_b, ln_a_g=ln_a_g,
             ln_a_b=ln_a_b, conv_b_w=conv_b_w, w_out=w_out, ffn_norm_g=ffn_norm_g, w_up=w_up, conv_f_w=conv_f_w,
             w_down=w_down, final_norm_g=final_norm_g)
    m = dict(mix_norm_g=m_mix_norm_g, w_in=m_w_in, b_in=m_b_in, conv_a_w=m_conv_a_w, conv_a_b=m_conv_a_b,
             ln_a_g=m_ln_a_g, ln_a_b=m_ln_a_b, conv_b_w=m_conv_b_w, w_out=m_w_out, ffn_norm_g=m_ffn_norm_g,
             w_up=m_w_up, conv_f_w=m_conv_f_w, w_down=m_w_down, final_norm_g=m_final_norm_g)
    v = dict(mix_norm_g=v_mix_norm_g, w_in=v_w_in, b_in=v_b_in, conv_a_w=v_conv_a_w, conv_a_b=v_conv_a_b,
             ln_a_g=v_ln_a_g, ln_a_b=v_ln_a_b, conv_b_w=v_conv_b_w, w_out=v_w_out, ffn_norm_g=v_ffn_norm_g,
             w_up=v_w_up, conv_f_w=v_conv_f_w, w_down=v_w_down, final_norm_g=v_final_norm_g)
    order = list(w)
    n_layers = w_in.shape[0]
    xs = x[0]
    target = loss_target[0]
    flip = lambda a: jnp.transpose(a, (0, 2, 1))
    wt, mt, vt = ({n: flip(d[n]) if n in COL_SHARDED else d[n] for n in BIG} for d in (w, m, v))
    px, py, pc = _place()
    where = jnp.stack([pc, 2 * px + py]).astype(jnp.int32)
    me = 4 * px + 2 * py + pc

    assert BIG == MIX_PART + FFN_PART
    key = lambda n: n + "_t" if n in COL_SHARDED else n
    shard = lambda n, l: wt[n][l].astype(BF16)

    def gather_start(names, l, after, tag):
        shards = [shard(n, l) for n in names]
        lands = _gather_landings(shards, me, name=f"gather_landing_{tag}")
        return _split_start(shards, lands, _gather_plan_first, 4 * len(shards), after, name=f"gather_first_start_{tag}")

    def gather_mid(first, after, tag):
        return _split_wait(first[0], first[1], first[2], first[3], after, _gather_plan_first,
                           name=f"gather_first_wait_{tag}")[1]

    def forward_start(lands, after, tag):
        return _split_start([], lands, _gather_plan_second, 3 * len(lands), after, name=f"gather_second_start_{tag}")

    def forward_finish(second, after, tag):
        return _split_wait(second[0], second[1], [], second[3], after, _gather_plan_second,
                           name=f"gather_second_wait_{tag}")[1]

    gathered = _all_gather([shard(n, 0) for n in MIX_PART] + [w[n] for n in CONV], name="gather_weights_0")
    params = [{n: w[n][l] for n in REPLICATED} for l in range(n_layers)]
    for n, g in zip(CONV, gathered[len(MIX_PART):]):
        n_dev, _, taps, c = g.shape
        full = g.transpose(1, 2, 0, 3).reshape(n_layers, taps, n_dev * c)
        for l in range(n_layers):
            params[l][n] = full[l]
    for n, g in zip(MIX_PART, gathered):
        params[0][key(n)] = _weights_from_gathered(g)
    ffn_first = gather_start(FFN_PART, 0, gathered[0], "0_ffn")
    pending = {}

    h = xs
    saved = []
    for l in range(n_layers):
        nxt = l + 1 if l + 1 < n_layers else None

        def before_up(x1, l=l, nxt=nxt):
            if l == 0:
                second = forward_start(gather_mid(ffn_first, x1, "0_ffn"), x1, "0_ffn")
                after = second[4]
            else:
                second = pending[l]["ffn"]
                after = x1
            if nxt is not None:
                pending[nxt] = dict(first=gather_start(BIG, nxt, after, str(nxt)))
                after = pending[nxt]["first"][4]
            for n, g in zip(FFN_PART, forward_finish(second, after, f"{l}_ffn")):
                params[l][key(n)] = _weights_from_gathered(g)

        h, keep = _layer_fwd(h, params[l], str(l), dep=ffn_first[4] if l == 0 else None, before_up=before_up)
        saved.append(keep)
        if nxt is not None:
            arrived = gather_mid(pending[nxt]["first"], h, str(nxt))
            mix_second = forward_start(arrived[:len(MIX_PART)], h, f"{nxt}_mix")
            pending[nxt]["ffn"] = forward_start(arrived[len(MIX_PART):], mix_second[4], f"{nxt}_ffn")
            for n, g in zip(MIX_PART, forward_finish(mix_second, pending[nxt]["ffn"][4], f"{nxt}_mix")):
                params[nxt][key(n)] = _weights_from_gathered(g)

    def start_siblings(slabs, after, tag):
        mines = [s.reshape(N_CHIP, 2, *s.shape[1:]) for s in slabs]
        lands = _unwritten([jax.ShapeDtypeStruct((N_CHIP, *m.shape[2:]), m.dtype) for m in mines],
                           name=f"reduce_siblings_landing_{tag}")
        return _split_start(mines, lands, _siblings_plan, len(mines), after, name=f"reduce_siblings_start_{tag}")

    def start_chips(sib, after, tag):
        mines, theirs = _split_wait(sib[0], sib[1], sib[2], sib[3], after, _siblings_plan,
                                    name=f"reduce_siblings_wait_{tag}")
        pairs, lands = _pair_sum(mines, theirs, where, name=f"pair_sum_{tag}")
        return _split_start(pairs, lands, _chips_plan, 3 * len(pairs), after, name=f"reduce_chips_start_{tag}")

    def finish_reduce(fly, after, tag):
        return _split_wait(fly[0], fly[1], fly[2], fly[3], after, _chips_plan, name=f"reduce_chips_wait_{tag}")[1]

    loss_sq, dh, dh_b, dgf = _loss_bwd(h, _row(final_norm_g), target, name="loss")
    conv_g = {n: [None] * n_layers for n in CONV}
    rep_g = [None] * n_layers
    siblings = {}
    flights = {}
    token = None
    for l in reversed(range(n_layers)):
        def ffn_grads(g, after, l=l):
            siblings[l, "ffn"] = start_siblings([_slabs_from_full(g[n]) for n in FFN_PART], after, f"{l}_ffn")
            return siblings[l, "ffn"][4]

        def ffn_sent(after, l=l):
            flights[l, "ffn"] = start_chips(siblings[l, "ffn"], after, f"{l}_ffn")
            return flights[l, "ffn"][4]

        def mix_grads(g, conv, after, l=l):
            for n in CONV:
                conv_g[n][l] = conv[n]
            slabs = [_slabs_from_full(g[n]) for n in MIX_PART]
            if l == 0:
                for n in CONV:
                    full = jnp.stack(conv_g[n])
                    _, taps, c = full.shape
                    slabs.append(full.reshape(n_layers, taps, N_DEV, c // N_DEV).transpose(2, 0, 1, 3)
                                 .reshape(N_DEV, n_layers * taps, c // N_DEV))
            siblings[l, "mix"] = start_siblings(slabs, after, f"{l}_mix")
            return siblings[l, "mix"][4]

        def mix_sent(after, l=l):
            flights[l, "mix"] = start_chips(siblings[l, "mix"], after, f"{l}_mix")
            return flights[l, "mix"][4]

        dh, dh_b, rep_g[l], token = _layer_bwd(dh, dh_b, params[l], saved[l], str(l), ffn_grads, ffn_sent,
                                               mix_grads, mix_sent, dep=token)

    rep_cols = [rep_g[l][n] for l in range(n_layers) for n in REPLICATED] + [dgf, loss_sq]
    folded = _fold_partials(rep_cols, name="fold_small")
    small = _split_start([folded], _gather_landings([folded], me, name="gather_small_landing"), _gather_plan_all,
                         N_DEV - 1, token, name="gather_small_start")
    sums = {key: finish_reduce(fly, dh, f"{key[0]}_{key[1]}") for key, fly in flights.items() if key != (0, "mix")}
    out = {k: {} for k in KINDS}

    def adamw_big(names, part, dep):
        for q, n in enumerate(names):
            layer_parts = [sums[l, part][q] for l in range(n_layers)]
            res = _adamw_sharded(layer_parts, wt[n], mt[n], vt[n], name=f"adamw_{n}", dep=dep)
            for k, r in zip(KINDS, res):
                out[k][n] = flip(r) if n in COL_SHARDED else r

    adamw_big(FFN_PART, "ffn", small[4])

    rep_all = _split_wait(small[0], small[1], small[2], small[3], out["m"][FFN_PART[-1]], _gather_plan_all,
                          name="gather_small_wait")[1][0]
    with_final = lambda d: {**{n: d[n] for n in REPLICATED}, "final_norm_g": _row(d["final_norm_g"])}
    loss, rep_res = _adamw_replicated(rep_all, REPLICATED, with_final(w), with_final(m), with_final(v),
                                      loss_sq.shape[1], name="adamw_small")
    for n, res in rep_res.items():
        for k, r in zip(KINDS, res):
            out[k][n] = r.reshape(w[n].shape)

    last = finish_reduce(flights[0, "mix"], rep_res["b_in"][0], "0_mix")
    sums[0, "mix"] = last[:len(MIX_PART)]
    adamw_big(MIX_PART, "mix", None)
    for n, p in zip(CONV, last[len(MIX_PART):]):
        as_one = lambda a: a.reshape(1, *p.shape[1:])
        for k, r in zip(KINDS, _adamw_sharded([p], as_one(w[n]), as_one(m[n]), as_one(v[n]), name=f"adamw_{n}")):
            out[k][n] = r.reshape(w[n].shape)

    grad_x = dh.reshape(x.shape)
    return (loss.reshape(()), grad_x, *[out["grad"][n] for n in order], *[out["delta"][n] for n in order],
            *[out["m"][n] for n in order], *[out["v"][n] for n in order])
```

```python
import functools

import jax
import jax.numpy as jnp
from jax import lax
from jax.experimental import pallas as pl
from jax.experimental.pallas import tpu as pltpu

F32 = jnp.float32
BF16 = jnp.bfloat16

N_DEV = 8
N_CHIP = 4
D_CONF = 512
CONF_K = 31
SHORT_K = 3
EPS = 1e-6
HALO = 32
HALO3 = 8
HALO3_BLK = 16
LANES = 128
SUB = 8
VMEM_LIMIT = 56 * 1024 * 1024

ADAM_LR = 0.001
ADAM_B1 = 0.9
ADAM_B2 = 0.999
ADAM_EPS = 1e-08
ADAM_WD = 0.01
ADAM_STEP = 10

MESH = pl.DeviceIdType.MESH
ANY = pl.BlockSpec(memory_space=pl.ANY)


def _params(*sem):
    return pltpu.CompilerParams(dimension_semantics=sem, vmem_limit_bytes=VMEM_LIMIT)


def _resident(shape, index_map):
    return pl.BlockSpec(shape, index_map, pipeline_mode=pl.Buffered(1))


def _row_loop(n_rows, rb, fn, unroll=1):
    rb = min(rb, n_rows)

    def body(i, carry):
        fn(pl.ds(pl.multiple_of(i * rb, rb), rb))
        return carry
    lax.fori_loop(0, n_rows // rb, body, 0, unroll=unroll)


def _rows8(v):
    acc = v[0:SUB]
    for k in range(1, v.shape[0] // SUB):
        acc = acc + v[k * SUB:(k + 1) * SUB]
    return acc


def _sigmoid(z):
    return 0.5 * jnp.tanh(0.5 * z) + 0.5


def _dot(a, b):
    return jnp.dot(a, b, preferred_element_type=F32)


def _dot_nt(a, b):
    return lax.dot_general(a, b, (((1,), (1,)), ((), ())), preferred_element_type=F32)


def _dot_tn(a, b):
    return lax.dot_general(a, b, (((0,), (0,)), ((), ())), preferred_element_type=F32)


def _replicate_taps(w_ref, wrep, taps):
    for k in range(taps):
        wrep[pl.ds(k * SUB, SUB), :] = jnp.broadcast_to(w_ref[pl.ds(k, 1), :], (SUB, w_ref.shape[1]))


def _shift_copies(win, shf, lanes):
    span = win.shape[0] - SUB
    for r in range(1, SUB):
        for j0 in range(0, span, 64):
            n = min(64, span - j0)
            shf[r - 1, pl.ds(j0, n), lanes] = win[pl.ds(j0 + r, n), lanes]


def _rows_at(win, shf, off, rb, lanes):
    if shf is None or off % SUB == 0:
        return win[pl.ds(off, rb), lanes]
    return shf[off % SUB - 1, pl.ds(off - off % SUB, rb), lanes]


def _conv_taps(win, wrep, out, *, taps, n_rows, base, width, transposed=False, bias_ref=None, shf=None):
    rb = min(64, n_rows)

    def lane_body(cb, carry):
        lanes = pl.ds(pl.multiple_of(cb * LANES, LANES), LANES)
        if shf is not None:
            _shift_copies(win, shf, lanes)
        for r0 in range(0, n_rows, rb):
            acc = None
            for k in range(taps):
                off = (taps - 1 - k) if transposed else (k - (taps - 1))
                wk = jnp.tile(wrep[pl.ds(k * SUB, SUB), lanes], (rb // SUB, 1))
                term = wk * _rows_at(win, shf, base + r0 + off, rb, lanes)
                acc = term if acc is None else acc + term
            if bias_ref is not None:
                acc = acc + bias_ref[:, lanes]
            out[pl.ds(r0, rb), lanes] = acc.astype(out.dtype)
        return carry

    lax.fori_loop(0, width // LANES, lane_body, 0)


def _conv_bwd_taps(win, wrep, x_cur, dx_out, dw_acc, *, taps, n_rows, width, shf=None):
    rb = min(32 if taps > 8 else 64, n_rows)

    def lane_body(cb, carry):
        lanes = pl.ds(pl.multiple_of(cb * LANES, LANES), LANES)
        if shf is not None:
            _shift_copies(win, shf, lanes)
        sums = [None] * taps
        for r0 in range(0, n_rows, rb):
            xv = x_cur[pl.ds(r0, rb), lanes].astype(F32)
            acc = None
            for k in range(taps):
                shifted = _rows_at(win, shf, r0 + taps - 1 - k, rb, lanes)
                term = jnp.tile(wrep[pl.ds(k * SUB, SUB), lanes], (rb // SUB, 1)) * shifted
                acc = term if acc is None else acc + term
                part = _rows8(xv * shifted)
                sums[k] = part if sums[k] is None else sums[k] + part
            dx_out[pl.ds(r0, rb), lanes] = acc.astype(dx_out.dtype)
        for k in range(taps):
            dw_acc[pl.ds(k * SUB, SUB), lanes] += sums[k]
        return carry

    lax.fori_loop(0, width // LANES, lane_body, 0)


def _fold8(acc_ref, taps):
    return jnp.concatenate(
        [jnp.sum(acc_ref[pl.ds(k * SUB, SUB), :], axis=0, keepdims=True) for k in range(taps)], axis=0)


def _seq_tile(s_len):
    return min(512, s_len)


def _mm_tile(s_len):
    return min(512, s_len)


def _ff_chunk(ff):
    best = LANES
    for c in range(LANES, 1408 + 1, LANES):
        if ff % c == 0:
            best = c
    return best


def _col_tile(n):
    for c in (512, 1408, 256, LANES):
        if n % c == 0:
            return c
    return n


def _rms_matmul(x, g, wt, b, *, name, dep=None):
    s_len, d = x.shape
    n = wt.shape[0]
    tm = _mm_tile(s_len)
    cn = _col_tile(n)
    has_bias = b is not None

    def body(*refs):
        x_ref, g_ref, w_ref = refs[0:3]
        b_ref = refs[3] if has_bias else None
        o_ref, h_ref = refs[-2:]

        def blk(rows):
            xv = x_ref[rows, :]
            r = lax.rsqrt(jnp.mean(xv * xv, axis=-1, keepdims=True) + EPS)
            h_ref[rows, :] = ((xv * r) * g_ref[...]).astype(BF16)

        rb = min(128, tm)
        for r0 in range(0, tm, rb):
            blk(pl.ds(r0, rb))
        for j in range(n // cn):
            acc = _dot_nt(h_ref[...], w_ref[j * cn:(j + 1) * cn, :])
            if has_bias:
                acc = acc + b_ref[:, j * cn:(j + 1) * cn]
            o_ref[:, j * cn:(j + 1) * cn] = acc.astype(BF16)

    in_specs = [pl.BlockSpec((tm, d), lambda i: (i, 0)), _resident((1, d), lambda i: (0, 0)),
                _resident((n, d), lambda i: (0, 0))]
    args = [x, g, wt]
    if has_bias:
        in_specs.append(_resident((1, n), lambda i: (0, 0)))
        args.append(b)
    in_specs.append(ANY)
    args.append(x if dep is None else dep)
    return pl.pallas_call(
        body, grid=(s_len // tm,), in_specs=in_specs,
        out_specs=[pl.BlockSpec((tm, n), lambda i: (i, 0)), pl.BlockSpec((tm, d), lambda i: (i, 0))],
        out_shape=[jax.ShapeDtypeStruct((s_len, n), BF16), jax.ShapeDtypeStruct((s_len, d), BF16)],
        compiler_params=_params("parallel"), name=name,
    )(*args)


def _mix_windows(u_ref, uh_ref, gw, pw, first, t):
    c = D_CONF
    uh = uh_ref[...].astype(F32)
    gw[0:HALO, :] = jnp.where(first, 0.0, uh[:, 0:c] * _sigmoid(uh[:, c:2 * c]))
    pw[0:HALO3, :] = jnp.where(first, 0.0, uh[HALO - HALO3:HALO, 3 * c:4 * c] * uh[HALO - HALO3:HALO, 4 * c:5 * c])

    def blk(rows):
        dst = pl.ds(pl.multiple_of(rows.start + HALO, SUB), rows.size)
        gw[dst, :] = u_ref[rows, 0:c].astype(F32) * _sigmoid(u_ref[rows, c:2 * c].astype(F32))
        dst3 = pl.ds(pl.multiple_of(rows.start + HALO3, SUB), rows.size)
        pw[dst3, :] = u_ref[rows, 3 * c:4 * c].astype(F32) * u_ref[rows, 4 * c:5 * c].astype(F32)
    _row_loop(t, 64, blk)


def _mix_fwd(u, x0, wa, ba, lg, lb, wb, w_out, *, name):
    s_len, d_in = u.shape
    d = x0.shape[1]
    c = D_CONF
    t = _seq_tile(s_len)
    per = t // HALO

    def body(u_ref, uh_ref, x0_ref, wa_ref, ba_ref, lg_ref, lb_ref, wb_ref, wo_ref, y_ref, x1_ref, ca, cb,
             gw, pw, wrep_a, wrep_b, shf):
        first = pl.program_id(0) == 0
        _mix_windows(u_ref, uh_ref, gw, pw, first, t)
        _replicate_taps(wa_ref, wrep_a, CONF_K)
        _replicate_taps(wb_ref, wrep_b, SHORT_K)
        _conv_taps(gw, wrep_a, ca, taps=CONF_K, n_rows=t, base=HALO, width=c, bias_ref=ba_ref, shf=shf)
        _conv_taps(pw, wrep_b, cb, taps=SHORT_K, n_rows=t, base=HALO3, width=c)

        def blk(rows):
            cv = ca[rows, :]
            mu = jnp.mean(cv, axis=-1, keepdims=True)
            xc = cv - mu
            var = jnp.mean(xc * xc, axis=-1, keepdims=True)
            ln = (xc * lax.rsqrt(var + EPS)) * lg_ref[...] + lb_ref[...]
            y_ref[rows, 0:c] = (ln * _sigmoid(ln)).astype(BF16)
            y_ref[rows, c:2 * c] = (u_ref[rows, 2 * c:3 * c].astype(F32) * cb[rows, :]).astype(BF16)
        half = t // 2
        rb = min(64, half)
        for lo in range(0, t, half):
            for r0 in range(lo, lo + half, rb):
                blk(pl.ds(r0, rb))
            x1_ref[lo:lo + half, :] = x0_ref[lo:lo + half, :] + _dot(y_ref[lo:lo + half, :], wo_ref[...])

    small = lambda r: _resident((r, c), lambda i: (0, 0))
    return pl.pallas_call(
        body, grid=(s_len // t,),
        in_specs=[pl.BlockSpec((t, d_in), lambda i: (i, 0)),
                  pl.BlockSpec((HALO, d_in), lambda i: (jnp.maximum(i * per - 1, 0), 0)),
                  pl.BlockSpec((t, d), lambda i: (i, 0)),
                  small(CONF_K), small(1), small(1), small(1), small(SHORT_K),
                  _resident((2 * c, d), lambda i: (0, 0))],
        out_specs=[pl.BlockSpec((t, 2 * c), lambda i: (i, 0)), pl.BlockSpec((t, d), lambda i: (i, 0)),
                   pl.BlockSpec((t, c), lambda i: (i, 0)), pl.BlockSpec((t, c), lambda i: (i, 0))],
        out_shape=[jax.ShapeDtypeStruct((s_len, 2 * c), BF16), jax.ShapeDtypeStruct((s_len, d), F32),
                   jax.ShapeDtypeStruct((s_len, c), F32), jax.ShapeDtypeStruct((s_len, c), F32)],
        scratch_shapes=[pltpu.VMEM((HALO + t, c), F32), pltpu.VMEM((HALO3 + t, c), F32),
                        pltpu.VMEM((CONF_K * SUB, c), F32), pltpu.VMEM((SHORT_K * SUB, c), F32),
                        pltpu.VMEM((SUB - 1, HALO + t, c), F32)],
        compiler_params=_params("arbitrary"), name=name,
    )(u, u, x0, wa, ba, lg, lb, wb, w_out)


def _ffn_fwd(uf, x1, wf, w_down, *, name):
    s_len, ff2 = uf.shape
    ff = ff2 // 2
    d = x1.shape[1]
    t = _seq_tile(s_len)
    fc = _ff_chunk(ff)
    nc = ff // fc
    per = t // HALO3_BLK
    half = t // 2
    rb = min(64, half)

    def body(ug_ref, ugh_ref, uv_ref, uvh_ref, x1_ref, wfg_ref, wfv_ref, wd_ref,
             act_ref, x2_ref, cg_ref, cv_ref, gwin, vwin, wrep_g, wrep_v):
        first = pl.program_id(0) == 0
        first_chunk = pl.program_id(1) == 0
        lo8 = HALO3_BLK - HALO3
        gwin[0:HALO3, :] = jnp.where(first, 0.0, ugh_ref[...].astype(F32)[lo8:HALO3_BLK])
        vwin[0:HALO3, :] = jnp.where(first, 0.0, uvh_ref[...].astype(F32)[lo8:HALO3_BLK])
        _replicate_taps(wfg_ref, wrep_g, SHORT_K)
        _replicate_taps(wfv_ref, wrep_v, SHORT_K)
        chunk_rows = pl.ds(pl.multiple_of(pl.program_id(1) * fc, fc), fc)

        def conv(win, wrep, r0, lanes):
            acc = None
            for k in range(SHORT_K):
                wk = jnp.tile(wrep[k * SUB:(k + 1) * SUB, lanes], (rb // SUB, 1))
                off = HALO3 + r0 + k - (SHORT_K - 1)
                term = wk * win[off:off + rb, lanes]
                acc = term if acc is None else acc + term
            return acc

        for lo in range(0, t, half):
            for r0 in range(lo, lo + half, rb):
                gwin[HALO3 + r0:HALO3 + r0 + rb, :] = ug_ref[r0:r0 + rb, :].astype(F32)
                vwin[HALO3 + r0:HALO3 + r0 + rb, :] = uv_ref[r0:r0 + rb, :].astype(F32)
            for cb in range(fc // LANES):
                lanes = slice(cb * LANES, (cb + 1) * LANES)
                for r0 in range(lo, lo + half, rb):
                    gv = conv(gwin, wrep_g, r0, lanes).astype(BF16)
                    vv = conv(vwin, wrep_v, r0, lanes).astype(BF16)
                    cg_ref[r0:r0 + rb, lanes] = gv
                    cv_ref[r0:r0 + rb, lanes] = vv
                    act_ref[r0:r0 + rb, lanes] = (gv * _sigmoid(gv)) * vv
            base = jnp.where(first_chunk, x1_ref[lo:lo + half, :], x2_ref[lo:lo + half, :])
            x2_ref[lo:lo + half, :] = base + _dot(act_ref[lo:lo + half, :], wd_ref[chunk_rows, :])

    halo_map = lambda off: (lambda i, j: (jnp.maximum(i * per - 1, 0), j + off))
    return pl.pallas_call(
        body, grid=(s_len // t, nc),
        in_specs=[pl.BlockSpec((t, fc), lambda i, j: (i, j)), pl.BlockSpec((HALO3_BLK, fc), halo_map(0)),
                  pl.BlockSpec((t, fc), lambda i, j: (i, j + nc)), pl.BlockSpec((HALO3_BLK, fc), halo_map(nc)),
                  pl.BlockSpec((t, d), lambda i, j: (i, 0)),
                  pl.BlockSpec((SHORT_K, fc), lambda i, j: (0, j)),
                  pl.BlockSpec((SHORT_K, fc), lambda i, j: (0, j + nc)),
                  _resident((ff, d), lambda i, j: (0, 0))],
        out_specs=[pl.BlockSpec((t, fc), lambda i, j: (i, j)), pl.BlockSpec((t, d), lambda i, j: (i, 0)),
                   pl.BlockSpec((t, fc), lambda i, j: (i, j)), pl.BlockSpec((t, fc), lambda i, j: (i, j))],
        out_shape=[jax.ShapeDtypeStruct((s_len, ff), BF16), jax.ShapeDtypeStruct((s_len, d), F32),
                   jax.ShapeDtypeStruct((s_len, ff), BF16), jax.ShapeDtypeStruct((s_len, ff), BF16)],
        scratch_shapes=[pltpu.VMEM((HALO3 + t, fc), F32), pltpu.VMEM((HALO3 + t, fc), F32),
                        pltpu.VMEM((SHORT_K * SUB, fc), F32), pltpu.VMEM((SHORT_K * SUB, fc), F32)],
        compiler_params=_params("parallel", "arbitrary"), name=name,
    )(uf, uf, uf, uf, x1, wf, wf, w_down)


def _loss_bwd(x, g, target, *, name):
    s_len, d = x.shape
    t = _seq_tile(s_len)

    def body(x_ref, g_ref, t_ref, l_ref, dx_ref, dxb_ref, dg_ref):
        @pl.when(pl.program_id(0) == 0)
        def _():
            l_ref[...] = jnp.zeros_like(l_ref)
            dg_ref[...] = jnp.zeros_like(dg_ref)

        def blk(rows):
            xv = x_ref[rows, :]
            r = lax.rsqrt(jnp.mean(xv * xv, axis=-1, keepdims=True) + EPS)
            xn = xv * r
            e = xn * g_ref[...] - t_ref[rows, :]
            l_ref[...] += _rows8(e * e)
            dy = e * (1.0 / d)
            dg_ref[...] += _rows8(dy * xn)
            dn = dy * g_ref[...]
            dx = r * (dn - xn * jnp.mean(dn * xn, axis=-1, keepdims=True))
            dx_ref[rows, :] = dx
            dxb_ref[rows, :] = dx.astype(BF16)
        _row_loop(t, 64, blk)

    row = pl.BlockSpec((t, d), lambda i: (i, 0))
    part = pl.BlockSpec((SUB, d), lambda i: (0, 0))
    return pl.pallas_call(
        body, grid=(s_len // t,),
        in_specs=[row, _resident((1, d), lambda i: (0, 0)), row],
        out_specs=[part, row, row, part],
        out_shape=[jax.ShapeDtypeStruct((SUB, d), F32), jax.ShapeDtypeStruct((s_len, d), F32),
                   jax.ShapeDtypeStruct((s_len, d), BF16), jax.ShapeDtypeStruct((SUB, d), F32)],
        compiler_params=_params("arbitrary"), name=name,
    )(x, g, target)


def _ffn_bwd(dx2, uf, cg, cv, wf, w_down, *, name, dep=None):
    s_len, ff2 = uf.shape
    ff = ff2 // 2
    d = dx2.shape[1]
    t = _seq_tile(s_len)
    n_t = s_len // t
    fc = _ff_chunk(ff)
    nc = ff // fc

    def body(dx_ref, ug_ref, uv_ref, cg_ref, cv_ref, wfg_ref, wfv_ref, wd_ref, dep_ref,
             duf_ref, dwg_ref, dwv_ref, dact, dgw, dvw, awg, awv, wrep_g, wrep_v):
        i = pl.program_id(1)

        @pl.when(i == 0)
        def _():
            dgw[t:t + HALO3, :] = jnp.zeros((HALO3, fc), F32)
            dvw[t:t + HALO3, :] = jnp.zeros((HALO3, fc), F32)
            awg[...] = jnp.zeros_like(awg)
            awv[...] = jnp.zeros_like(awv)

        _replicate_taps(wfg_ref, wrep_g, SHORT_K)
        _replicate_taps(wfv_ref, wrep_v, SHORT_K)

        def blk(rows):
            gv = cg_ref[rows, :]
            sg = _sigmoid(gv)
            da = dact[rows, :].astype(BF16)
            dgw[rows, :] = ((da * cv_ref[rows, :]) * (sg * (1.0 + gv * (1.0 - sg)))).astype(F32)
            dvw[rows, :] = (da * (gv * sg)).astype(F32)

        dact[...] = _dot_nt(dx_ref[...], wd_ref[...])
        _row_loop(t, 32, blk, unroll=2)

        _conv_bwd_taps(dgw, wrep_g, ug_ref, duf_ref.at[0], awg, taps=SHORT_K, n_rows=t, width=fc)
        _conv_bwd_taps(dvw, wrep_v, uv_ref, duf_ref.at[1], awv, taps=SHORT_K, n_rows=t, width=fc)
        dgw[t:t + HALO3, :] = dgw[0:HALO3, :]
        dvw[t:t + HALO3, :] = dvw[0:HALO3, :]

        @pl.when(i == n_t - 1)
        def _():
            dwg_ref[...] = _fold8(awg, SHORT_K)
            dwv_ref[...] = _fold8(awv, SHORT_K)

    rev = lambda i: n_t - 1 - i
    gate = pl.BlockSpec((t, fc), lambda j, i: (rev(i), j))
    value = pl.BlockSpec((t, fc), lambda j, i: (rev(i), j + nc))
    return pl.pallas_call(
        body, grid=(nc, n_t),
        in_specs=[pl.BlockSpec((t, d), lambda j, i: (rev(i), 0)), gate, value, gate, gate,
                  pl.BlockSpec((SHORT_K, fc), lambda j, i: (0, j)),
                  pl.BlockSpec((SHORT_K, fc), lambda j, i: (0, j + nc)),
                  pl.BlockSpec((fc, d), lambda j, i: (j, 0)), ANY],
        out_specs=[pl.BlockSpec((2, t, fc), lambda j, i: (0, rev(i), j)),
                   pl.BlockSpec((SHORT_K, fc), lambda j, i: (0, j)), pl.BlockSpec((SHORT_K, fc), lambda j, i: (0, j))],
        out_shape=[jax.ShapeDtypeStruct((2, s_len, ff), BF16),
                   jax.ShapeDtypeStruct((SHORT_K, ff), F32), jax.ShapeDtypeStruct((SHORT_K, ff), F32)],
        scratch_shapes=[pltpu.VMEM((t, fc), F32),
                        pltpu.VMEM((t + HALO3, fc), F32), pltpu.VMEM((t + HALO3, fc), F32),
                        pltpu.VMEM((SHORT_K * SUB, fc), F32), pltpu.VMEM((SHORT_K * SUB, fc), F32),
                        pltpu.VMEM((SHORT_K * SUB, fc), F32), pltpu.VMEM((SHORT_K * SUB, fc), F32)],
        compiler_params=_params("arbitrary", "arbitrary"), name=name,
    )(dx2, uf, uf, cg, cv, wf, wf, w_down, uf if dep is None else dep)


def _mix_bwd(dx1, u, ca, cb, wa, lg, lb, wb, w_out, *, name, dep=None):
    s_len, d_in = u.shape
    d = dx1.shape[1]
    c = D_CONF
    t = _seq_tile(s_len)
    n_t = s_len // t

    def body(dx_ref, u_ref, ca_ref, cb_ref, wa_ref, lg_ref, lb_ref, wb_ref, wo_ref, dep_ref,
             du_ref, dwa_ref, dwb_ref, dba_ref, dlg_ref, dlb_ref, dbin_ref,
             glu, prod, dyc, dcaw, dcbw, dglu, dp, awa, awb, wrep_a, wrep_b, shf):
        i = pl.program_id(0)
        _replicate_taps(wa_ref, wrep_a, CONF_K)
        _replicate_taps(wb_ref, wrep_b, SHORT_K)

        @pl.when(i == 0)
        def _():
            dcaw[t:t + HALO, :] = jnp.zeros((HALO, c), F32)
            dcbw[t:t + HALO3, :] = jnp.zeros((HALO3, c), F32)
            awa[...] = jnp.zeros_like(awa)
            awb[...] = jnp.zeros_like(awb)
            dba_ref[...] = jnp.zeros_like(dba_ref)
            dlg_ref[...] = jnp.zeros_like(dlg_ref)
            dlb_ref[...] = jnp.zeros_like(dlb_ref)
            dbin_ref[...] = jnp.zeros_like(dbin_ref)

        def blk1(rows):
            cv = ca_ref[rows, :]
            mu = jnp.mean(cv, axis=-1, keepdims=True)
            xc = cv - mu
            rstd = lax.rsqrt(jnp.mean(xc * xc, axis=-1, keepdims=True) + EPS)
            nrm = xc * rstd
            ln = nrm * lg_ref[...] + lb_ref[...]
            sg = _sigmoid(ln)
            dln = dyc[rows, 0:c] * (sg * (1.0 + ln * (1.0 - sg)))
            dlg_ref[...] += _rows8(dln * nrm)
            dlb_ref[...] += _rows8(dln)
            dn = dln * lg_ref[...]
            dca = rstd * (dn - jnp.mean(dn, axis=-1, keepdims=True)
                          - nrm * jnp.mean(dn * nrm, axis=-1, keepdims=True))
            dcaw[rows, :] = dca
            dba_ref[...] += _rows8(dca)
            ds = dyc[rows, c:2 * c]
            dgb = ds * cb_ref[rows, :]
            dcbw[rows, :] = ds * u_ref[rows, 2 * c:3 * c].astype(F32)
            du_ref[rows, 2 * c:3 * c] = dgb.astype(BF16)
            dbin_ref[:, 2 * c:3 * c] += _rows8(dgb)
            glu[rows, :] = u_ref[rows, 0:c].astype(F32) * _sigmoid(u_ref[rows, c:2 * c].astype(F32))
            prod[rows, :] = u_ref[rows, 3 * c:4 * c].astype(F32) * u_ref[rows, 4 * c:5 * c].astype(F32)
        half = t // 2
        rb = min(64, half)
        for lo in range(0, t, half):
            dyc[lo:lo + half, :] = _dot_nt(dx_ref[lo:lo + half, :], wo_ref[...])
            for r0 in range(lo, lo + half, rb):
                blk1(pl.ds(r0, rb))

        _conv_bwd_taps(dcaw, wrep_a, glu, dglu, awa, taps=CONF_K, n_rows=t, width=c, shf=shf)
        _conv_bwd_taps(dcbw, wrep_b, prod, dp, awb, taps=SHORT_K, n_rows=t, width=c)
        dcaw[t:t + HALO, :] = dcaw[0:HALO, :]
        dcbw[t:t + HALO3, :] = dcbw[0:HALO3, :]

        def blk2(rows):
            av = u_ref[rows, 0:c].astype(F32)
            sg = _sigmoid(u_ref[rows, c:2 * c].astype(F32))
            dg = dglu[rows, :]
            d_av = dg * sg
            d_ag = (dg * av) * (sg * (1.0 - sg))
            dpv = dp[rows, :]
            d_gc = dpv * u_ref[rows, 4 * c:5 * c].astype(F32)
            d_vs = dpv * u_ref[rows, 3 * c:4 * c].astype(F32)
            du_ref[rows, 0:c] = d_av.astype(BF16)
            du_ref[rows, c:2 * c] = d_ag.astype(BF16)
            du_ref[rows, 3 * c:4 * c] = d_gc.astype(BF16)
            du_ref[rows, 4 * c:5 * c] = d_vs.astype(BF16)
            dbin_ref[:, 0:c] += _rows8(d_av)
            dbin_ref[:, c:2 * c] += _rows8(d_ag)
            dbin_ref[:, 3 * c:4 * c] += _rows8(d_gc)
            dbin_ref[:, 4 * c:5 * c] += _rows8(d_vs)
        _row_loop(t, 64, blk2)

        @pl.when(i == n_t - 1)
        def _():
            dwa_ref[...] = _fold8(awa, CONF_K)
            dwb_ref[...] = _fold8(awb, SHORT_K)

    rev = lambda i: n_t - 1 - i
    small_in = lambda r: _resident((r, c), lambda i: (0, 0))
    small = lambda r: pl.BlockSpec((r, c), lambda i: (0, 0))
    return pl.pallas_call(
        body, grid=(n_t,),
        in_specs=[pl.BlockSpec((t, d), lambda i: (rev(i), 0)),
                  pl.BlockSpec((t, d_in), lambda i: (rev(i), 0)),
                  pl.BlockSpec((t, c), lambda i: (rev(i), 0)), pl.BlockSpec((t, c), lambda i: (rev(i), 0)),
                  small_in(CONF_K), small_in(1), small_in(1), small_in(SHORT_K),
                  _resident((2 * c, d), lambda i: (0, 0)), ANY],
        out_specs=[pl.BlockSpec((t, d_in), lambda i: (rev(i), 0)),
                   small(CONF_K), small(SHORT_K), small(SUB), small(SUB), small(SUB),
                   pl.BlockSpec((SUB, d_in), lambda i: (0, 0))],
        out_shape=[jax.ShapeDtypeStruct((s_len, d_in), BF16),
                   jax.ShapeDtypeStruct((CONF_K, c), F32), jax.ShapeDtypeStruct((SHORT_K, c), F32),
                   jax.ShapeDtypeStruct((SUB, c), F32), jax.ShapeDtypeStruct((SUB, c), F32),
                   jax.ShapeDtypeStruct((SUB, c), F32), jax.ShapeDtypeStruct((SUB, d_in), F32)],
        scratch_shapes=[pltpu.VMEM((t, c), F32), pltpu.VMEM((t, c), F32), pltpu.VMEM((t, 2 * c), F32),
                        pltpu.VMEM((t + HALO, c), F32), pltpu.VMEM((t + HALO3, c), F32),
                        pltpu.VMEM((t, c), F32), pltpu.VMEM((t, c), F32),
                        pltpu.VMEM((CONF_K * SUB, c), F32), pltpu.VMEM((SHORT_K * SUB, c), F32),
                        pltpu.VMEM((CONF_K * SUB, c), F32), pltpu.VMEM((SHORT_K * SUB, c), F32),
                        pltpu.VMEM((SUB - 1, t + HALO, c), F32)],
        compiler_params=_params("arbitrary"), name=name,
    )(dx1, u, ca, cb, wa, lg, lb, wb, w_out, u if dep is None else dep)


def _matmul_tn(a, b, *, name):
    n_p, s_len, k = a.shape
    n = b.shape[1]
    tk = _col_tile(k)
    per = k // tk

    def body(a_ref, b_ref, o_ref):
        o_ref[...] = _dot_tn(a_ref[...], b_ref[...]).astype(BF16)

    return pl.pallas_call(
        body, grid=(n_p, per),
        in_specs=[pl.BlockSpec((None, s_len, tk), lambda p, j: (p, 0, j)), _resident((s_len, n), lambda p, j: (0, 0))],
        out_specs=pl.BlockSpec((tk, n), lambda p, j: (p * per + j, 0)),
        out_shape=jax.ShapeDtypeStruct((n_p * k, n), BF16),
        compiler_params=_params("parallel", "parallel"), name=name,
    )(a, b)


def _matmul_rmsbwd(dzs, wt, x, g, dx_in, *, name, dep=None):
    s_len, d = x.shape
    n_z, _, nj = dzs.shape
    t = _mm_tile(s_len)

    def body(*refs):
        dz_refs = refs[0:n_z]
        w_refs = refs[n_z:2 * n_z]
        x_ref, g_ref, dxi_ref, _, dx_ref, dxb_ref, dg_ref, dh = refs[2 * n_z:]

        @pl.when(pl.program_id(0) == 0)
        def _():
            dg_ref[...] = jnp.zeros_like(dg_ref)

        def blk(rows):
            xv = x_ref[rows, :]
            r = lax.rsqrt(jnp.mean(xv * xv, axis=-1, keepdims=True) + EPS)
            xn = xv * r
            dhv = dh[rows, :]
            dg_ref[...] += _rows8(dhv * xn)
            dn = dhv * g_ref[...]
            dx = dxi_ref[rows, :] + r * (dn - xn * jnp.mean(dn * xn, axis=-1, keepdims=True))
            dx_ref[rows, :] = dx
            dxb_ref[rows, :] = dx.astype(BF16)

        half = t // 2
        rb = min(128, half)
        for lo in range(0, t, half):
            acc = _dot(dz_refs[0][lo:lo + half, :], w_refs[0][...])
            for q in range(1, n_z):
                acc = acc + _dot(dz_refs[q][lo:lo + half, :], w_refs[q][...])
            dh[lo:lo + half, :] = acc
            for r0 in range(lo, lo + half, rb):
                blk(pl.ds(r0, rb))

    row = pl.BlockSpec((t, d), lambda i: (i, 0))
    in_specs = [pl.BlockSpec((None, t, nj), functools.partial(lambda q, i: (q, i, 0), q)) for q in range(n_z)]
    in_specs += [_resident((nj, d), functools.partial(lambda q, i: (q, 0), q)) for q in range(n_z)]
    in_specs += [row, _resident((1, d), lambda i: (0, 0)), row, ANY]
    return pl.pallas_call(
        body, grid=(s_len // t,), in_specs=in_specs,
        out_specs=[row, row, pl.BlockSpec((SUB, d), lambda i: (0, 0))],
        out_shape=[jax.ShapeDtypeStruct((s_len, d), F32), jax.ShapeDtypeStruct((s_len, d), BF16),
                   jax.ShapeDtypeStruct((SUB, d), F32)],
        scratch_shapes=[pltpu.VMEM((t, d), F32)],
        compiler_params=_params("arbitrary"), name=name,
    )(*([dzs] * n_z), *([wt] * n_z), x, g, dx_in, x if dep is None else dep)


def _row(v):
    return v.reshape(1, -1)


def _layer_fwd(x0, p, tag, dep=None, before_up=None):
    u, h1 = _rms_matmul(x0, _row(p["mix_norm_g"]), p["w_in_t"], _row(p["b_in"]), name=f"in_proj_{tag}", dep=dep)
    ycat, x1, ca, cb = _mix_fwd(u, x0, p["conv_a_w"], _row(p["conv_a_b"]), _row(p["ln_a_g"]), _row(p["ln_a_b"]),
                            p["conv_b_w"], p["w_out"], name=f"mix_fwd_{tag}")
    if before_up is not None:
        before_up(x1)
    uf, h2 = _rms_matmul(x1, _row(p["ffn_norm_g"]), p["w_up_t"], None, name=f"up_proj_{tag}")
    act, x2, cg, cv = _ffn_fwd(uf, x1, p["conv_f_w"], p["w_down"], name=f"ffn_fwd_{tag}")
    return x2, dict(x0=x0, h1=h1, u=u, ca=ca, cb=cb, ycat=ycat, x1=x1, h2=h2, uf=uf, cg=cg, cv=cv, act=act)


def _layer_bwd(dx2, dx2_b, p, saved, tag, ffn_grads, ffn_sent, mix_grads, mix_sent, dep=None):
    d_uf, dwf_g, dwf_v = _ffn_bwd(dx2_b, saved["uf"], saved["cg"], saved["cv"], p["conv_f_w"], p["w_down"],
                                  name=f"ffn_bwd_{tag}", dep=dep)
    g_down = _matmul_tn(saved["act"][None], dx2_b, name=f"dw_down_{tag}")
    g_up = _matmul_tn(d_uf, saved["h2"], name=f"dw_up_{tag}")
    dep_ffn = ffn_grads(dict(w_up=g_up, w_down=g_down), dx2_b)
    dx1, dx1_b, dg2 = _matmul_rmsbwd(d_uf, p["w_up_t"], saved["x1"], _row(p["ffn_norm_g"]), dx2,
                                     name=f"dh_ffn_{tag}", dep=dep_ffn)
    du, dwa, dwb, dba, dlg, dlb, dbin = _mix_bwd(
        dx1_b, saved["u"], saved["ca"], saved["cb"], p["conv_a_w"], _row(p["ln_a_g"]), _row(p["ln_a_b"]),
        p["conv_b_w"], p["w_out"], name=f"mix_bwd_{tag}", dep=ffn_sent(dx1_b))
    g_out = _matmul_tn(saved["ycat"][None], dx1_b, name=f"dw_out_{tag}")
    g_in = _matmul_tn(du[None], saved["h1"], name=f"dw_in_{tag}")
    conv = dict(conv_a_w=dwa, conv_b_w=dwb, conv_f_w=jnp.concatenate([dwf_g, dwf_v], axis=1))
    dep_mix = mix_grads(dict(w_in=g_in, w_out=g_out), conv, dx1_b)
    dx0, dx0_b, dg1 = _matmul_rmsbwd(du[None], p["w_in_t"], saved["x0"], _row(p["mix_norm_g"]), dx1,
                                     name=f"dh_mix_{tag}", dep=dep_mix)
    rep = dict(mix_norm_g=dg1, b_in=dbin, conv_a_b=dba, ln_a_g=dlg, ln_a_b=dlb, ffn_norm_g=dg2)
    return dx0, dx0_b, rep, mix_sent(dx0_b)


def _place():
    return lax.axis_index("x"), lax.axis_index("y"), lax.axis_index("c")


def _all_gather(arrs, *, name):
    n_a = len(arrs)

    def body(*refs):
        ins = refs[0:n_a]
        outs = refs[n_a:2 * n_a]
        send_sems, recv_sems, local_sems = refs[2 * n_a:]
        x, y, c = _place()
        sibling = (x, y, 1 - c)
        chips = [(1 - x, y), (x, 1 - y), (1 - x, 1 - y)]

        def slot(a, px, py, pc):
            return outs[a].at[4 * px + 2 * py + pc]

        def copy(a, k, block, to, src=None):
            return pltpu.make_async_remote_copy(
                src_ref=slot(a, *block) if src is None else src, dst_ref=slot(a, *block),
                send_sem=send_sems.at[a, k], recv_sem=recv_sems.at[a, k],
                device_id=to, device_id_type=MESH)

        me = (x, y, c)
        mine = [pltpu.make_async_copy(ins[a], slot(a, *me), local_sems.at[a]) for a in range(n_a)]
        for cp in mine:
            cp.start()
        started = []
        for a in range(n_a):
            first = [copy(a, 0, me, sibling, src=ins[a])]
            first += [copy(a, 1 + j, me, (*chip, c), src=ins[a]) for j, chip in enumerate(chips)]
            for cp in first:
                cp.start()
            started += first
        for a in range(n_a):
            for j, chip in enumerate(chips):
                copy(a, 1 + j, (*chip, c), me).wait_recv()
                passed = copy(a, 4 + j, (*chip, c), sibling)
                passed.start()
                started.append(passed)
        for a in range(n_a):
            copy(a, 0, sibling, me).wait_recv()
            for j, chip in enumerate(chips):
                copy(a, 4 + j, (*chip, 1 - c), me).wait_recv()
        for cp in started:
            cp.wait_send()
        for cp in mine:
            cp.wait()

    return pl.pallas_call(
        body, in_specs=[ANY] * n_a, out_specs=[ANY] * n_a,
        out_shape=[jax.ShapeDtypeStruct((N_DEV, *a.shape), a.dtype) for a in arrs],
        scratch_shapes=[pltpu.SemaphoreType.DMA((n_a, 7)), pltpu.SemaphoreType.DMA((n_a, 7)),
                        pltpu.SemaphoreType.DMA((n_a,))],
        name=name,
    )(*arrs)


def _row_tile(r, cap):
    for tr in range(min(cap, r) // 16 * 16, 0, -16):
        if r % tr == 0:
            return tr
    return r


def _pair_sum(mines, theirs, where, *, name):
    n_a = len(mines)
    n_chip = mines[0].shape[0]

    def body(where_ref, *refs):
        a_refs = refs[0:n_a]
        b_refs = refs[n_a:2 * n_a]
        p_refs = refs[2 * n_a:3 * n_a]
        l_refs = refs[3 * n_a:4 * n_a]
        q = pl.program_id(0)
        for a in range(n_a):
            p_refs[a][...] = (a_refs[a][...].astype(F32) + b_refs[a][...].astype(F32)).astype(p_refs[a].dtype)

        @pl.when(q == where_ref[1])
        def _():
            for a in range(n_a):
                l_refs[a][...] = p_refs[a][...]

    in_specs, out_p, out_l, shapes = [], [], [], []
    for m in mines:
        _, _, r, c = m.shape
        in_specs.append(pl.BlockSpec((None, None, r, c), lambda q, where_ref: (q, where_ref[0], 0, 0)))
    for m in mines:
        _, _, r, c = m.shape
        in_specs.append(pl.BlockSpec((None, r, c), lambda q, where_ref: (q, 0, 0)))
        out_p.append(pl.BlockSpec((None, r, c), lambda q, where_ref: (q, 0, 0)))
        out_l.append(pl.BlockSpec((None, r, c), lambda q, where_ref: (where_ref[1], 0, 0)))
        shapes.append(jax.ShapeDtypeStruct((n_chip, r, c), m.dtype))
    res = pl.pallas_call(
        body,
        grid_spec=pltpu.PrefetchScalarGridSpec(num_scalar_prefetch=1, grid=(n_chip,), in_specs=in_specs,
                                               out_specs=out_p + out_l),
        out_shape=shapes + shapes,
        compiler_params=_params("arbitrary"), name=name,
    )(where, *mines, *theirs)
    return list(res[:n_a]), list(res[n_a:])


HBM = pl.BlockSpec(memory_space=pltpu.HBM)
SEM = pl.BlockSpec(memory_space=pltpu.SEMAPHORE)
EFFECT = pltpu.SideEffectType.DATAFLOW_SIDE_EFFECTING


def _in_hbm(a):
    return pltpu.with_memory_space_constraint(a, pltpu.HBM)


def _split_start(srcs, lands, plan, n_copies, after, *, name):
    n_s, n_l = len(srcs), len(lands)

    def body(*refs):
        src_refs = refs[0:n_s]
        land_refs = refs[n_s:n_s + n_l]
        send_sems, recv_sems = refs[n_s + n_l + 1], refs[n_s + n_l + 2]
        token = refs[-1]
        for cp in plan(src_refs, land_refs, send_sems, recv_sems):
            cp.start()
        token[...] = jnp.zeros_like(token)

    thru = [pltpu.HBM(a.shape, a.dtype) for a in list(srcs) + list(lands)]
    res = pl.pallas_call(
        body, name=name,
        out_shape=(pltpu.SemaphoreType.DMA((n_copies,)), pltpu.SemaphoreType.DMA((n_copies,)), *thru,
                   jax.ShapeDtypeStruct((SUB, LANES), F32)),
        in_specs=[HBM] * (n_s + n_l) + [ANY],
        out_specs=(SEM, SEM, *([HBM] * (n_s + n_l)), pl.BlockSpec(memory_space=pltpu.VMEM)),
        input_output_aliases={i: 2 + i for i in range(n_s + n_l)},
        compiler_params=pltpu.CompilerParams(has_side_effects=EFFECT),
    )(*[_in_hbm(a) for a in srcs], *[_in_hbm(a) for a in lands], _in_hbm(after))
    return res[0], res[1], list(res[2:2 + n_s]), list(res[2 + n_s:2 + n_s + n_l]), res[-1]


def _split_wait(send_sems, recv_sems, srcs, lands, after, plan, *, name):
    n_s, n_l = len(srcs), len(lands)

    def body(*refs):
        src_refs = refs[0:n_s]
        land_refs = refs[n_s:n_s + n_l]
        send, recv = refs[n_s + n_l], refs[n_s + n_l + 1]
        for cp in plan(src_refs, land_refs, send, recv):
            cp.wait_send()
            cp.wait_recv()

    res = pl.pallas_call(
        body, name=name,
        out_shape=tuple(pltpu.HBM(a.shape, a.dtype) for a in list(srcs) + list(lands)),
        in_specs=[HBM] * (n_s + n_l) + [SEM, SEM, ANY],
        out_specs=tuple([HBM] * (n_s + n_l)),
        input_output_aliases={i: i for i in range(n_s + n_l)},
        compiler_params=pltpu.CompilerParams(has_side_effects=EFFECT),
    )(*srcs, *lands, send_sems, recv_sems, _in_hbm(after))
    return list(res[:n_s]), list(res[n_s:])


def _remote(src, dst, send_sems, recv_sems, k, to):
    return pltpu.make_async_remote_copy(src_ref=src, dst_ref=dst, send_sem=send_sems.at[k], recv_sem=recv_sems.at[k],
                                        device_id=to, device_id_type=MESH)


def _gather_plan_first(src_refs, land_refs, send_sems, recv_sems):
    x, y, c = _place()
    me = 4 * x + 2 * y + c
    peers = [(x, y, 1 - c), (1 - x, y, c), (x, 1 - y, c), (1 - x, 1 - y, c)]
    return [_remote(src, land.at[me], send_sems, recv_sems, 4 * a + k, to)
            for a, (src, land) in enumerate(zip(src_refs, land_refs)) for k, to in enumerate(peers)]


def _gather_plan_second(src_refs, land_refs, send_sems, recv_sems):
    x, y, c = _place()
    chips = [(1 - x, y), (x, 1 - y), (1 - x, 1 - y)]
    out = []
    for a, land in enumerate(land_refs):
        for j, (px, py) in enumerate(chips):
            slot = land.at[4 * px + 2 * py + c]
            out.append(_remote(slot, slot, send_sems, recv_sems, 3 * a + j, (x, y, 1 - c)))
    return out


def _siblings_plan(src_refs, land_refs, send_sems, recv_sems):
    x, y, c = _place()
    return [_remote(src.at[:, 1 - c], land, send_sems, recv_sems, a, (x, y, 1 - c))
            for a, (src, land) in enumerate(zip(src_refs, land_refs))]


def _chips_plan(src_refs, land_refs, send_sems, recv_sems):
    x, y, c = _place()
    my_chip = 2 * x + y
    chips = [(1 - x, y), (x, 1 - y), (1 - x, 1 - y)]
    return [_remote(src.at[2 * px + py], land.at[my_chip], send_sems, recv_sems, 3 * a + j, (px, py, c))
            for a, (src, land) in enumerate(zip(src_refs, land_refs)) for j, (px, py) in enumerate(chips)]


def _gather_landings(shards, me, *, name):
    blank = _unwritten([jax.ShapeDtypeStruct((N_DEV, *s.shape), s.dtype) for s in shards], name=name)
    return [lax.dynamic_update_index_in_dim(b, s, me, 0) for b, s in zip(blank, shards)]


def _adamw_math(g, w, m, v):
    m = ADAM_B1 * m + (1.0 - ADAM_B1) * g
    v = ADAM_B2 * v + (1.0 - ADAM_B2) * (g * g)
    m_hat = m / (1.0 - ADAM_B1 ** ADAM_STEP)
    v_hat = v / (1.0 - ADAM_B2 ** ADAM_STEP)
    delta = -ADAM_LR * (m_hat / (jnp.sqrt(v_hat) + ADAM_EPS) + ADAM_WD * w)
    return delta, m, v


def _adamw_sharded(parts, w, m, v, *, name, dep=None):
    n_layers, r, c = w.shape
    n_chip = parts[0].shape[0]
    tr = _row_tile(r, 384)
    n_i = r // tr

    def body(*refs):
        p_refs = refs[0:n_layers]
        w_ref, m_ref, v_ref, _, g_out, d_out, m_out, v_out = refs[n_layers:]
        layer = pl.program_id(0)
        for l in range(n_layers):
            @pl.when(layer == l)
            def _(l=l):
                g = p_refs[l][0].astype(F32)
                for q in range(1, n_chip):
                    g = g + p_refs[l][q].astype(F32)
                delta, m_new, v_new = _adamw_math(g, w_ref[...], m_ref[...], v_ref[...])
                g_out[...] = g
                d_out[...] = delta
                m_out[...] = m_new
                v_out[...] = v_new

    def part_map(l):
        return lambda layer, i: (0, jnp.where(layer == l, i, jnp.where(layer < l, 0, n_i - 1)), 0)

    blk = pl.BlockSpec((None, tr, c), lambda layer, i: (layer, i, 0))
    return pl.pallas_call(
        body, grid=(n_layers, n_i),
        in_specs=[pl.BlockSpec((n_chip, tr, c), part_map(l)) for l in range(n_layers)] + [blk, blk, blk, ANY],
        out_specs=[blk] * 4, out_shape=[jax.ShapeDtypeStruct((n_layers, r, c), F32)] * 4,
        compiler_params=_params("arbitrary", "arbitrary"), name=name,
    )(*parts, w, m, v, w if dep is None else dep)


def _fold_partials(cols, *, name):
    widths = [c.shape[1] for c in cols]

    def body(*refs):
        o_ref = refs[-1]
        pos = 0
        for ref, width in zip(refs[:-1], widths):
            o_ref[:, pos:pos + width] = jnp.sum(ref[...], axis=0, keepdims=True)
            pos += width

    return pl.pallas_call(body, out_shape=jax.ShapeDtypeStruct((1, sum(widths)), F32), name=name)(*cols)


def _adamw_replicated(parts, names, w, m, v, n_loss, *, name):
    n_dev = parts.shape[0]
    n_layers = w[names[0]].shape[0]
    every = list(names) + ["final_norm_g"]
    n_p = len(every)

    def body(*refs):
        p_ref = refs[0]
        w_refs = dict(zip(every, refs[1:1 + n_p]))
        m_refs = dict(zip(every, refs[1 + n_p:1 + 2 * n_p]))
        v_refs = dict(zip(every, refs[1 + 2 * n_p:1 + 3 * n_p]))
        l_out = refs[1 + 3 * n_p]
        outs = refs[2 + 3 * n_p:]
        o_refs = {n: outs[4 * q:4 * q + 4] for q, n in enumerate(every)}
        acc = p_ref[0]
        for q in range(1, n_dev):
            acc = acc + p_ref[q]
        tot = jnp.sum(acc, axis=0, keepdims=True)
        pos = 0
        where = [(n, l) for l in range(n_layers) for n in names] + [("final_norm_g", 0)]
        for n, l in where:
            width = w_refs[n].shape[1]
            g = tot[:, pos:pos + width]
            pos += width
            row = pl.ds(l, 1)
            delta, m_new, v_new = _adamw_math(g, w_refs[n][row, :], m_refs[n][row, :], v_refs[n][row, :])
            for o, val in zip(o_refs[n], (g, delta, m_new, v_new)):
                o[row, :] = val
        l_out[...] = (0.5 / n_loss) * jnp.sum(tot[:, pos:pos + n_loss], axis=-1, keepdims=True)

    shapes = [jax.ShapeDtypeStruct((1, 1), F32)]
    for n in every:
        shapes += [jax.ShapeDtypeStruct(w[n].shape, F32)] * 4
    res = pl.pallas_call(
        body, out_shape=shapes,
        compiler_params=pltpu.CompilerParams(vmem_limit_bytes=VMEM_LIMIT), name=name,
    )(parts, *[w[n] for n in every], *[m[n] for n in every], *[v[n] for n in every])
    return res[0], {n: res[1 + 4 * q:5 + 4 * q] for q, n in enumerate(every)}


BIG = ("w_in", "w_out", "w_up", "w_down")
COL_SHARDED = ("w_in", "w_up")
CONV = ("conv_a_w", "conv_b_w", "conv_f_w")
REPLICATED = ("mix_norm_g", "b_in", "conv_a_b", "ln_a_g", "ln_a_b", "ffn_norm_g")
KINDS = ("grad", "delta", "m", "v")
FFN_PART = ("w_up", "w_down")
MIX_PART = ("w_in", "w_out")


def _weights_from_gathered(g):
    n_dev, r, c = g.shape
    return g.reshape(n_dev * r, c)


def _slabs_from_full(grad):
    return grad.reshape(N_DEV, grad.shape[0] // N_DEV, grad.shape[1])


def _unwritten(like, *, name):
    return pl.pallas_call(lambda *refs: None, out_specs=[ANY] * len(like), out_shape=list(like), name=name)()


def kernel(x, mix_norm_g, w_in, b_in, conv_a_w, conv_a_b, ln_a_g, ln_a_b, conv_b_w, w_out, ffn_norm_g, w_up, conv_f_w, w_down, final_norm_g, loss_target, m_mix_norm_g, m_w_in, m_b_in, m_conv_a_w, m_conv_a_b, m_ln_a_g, m_ln_a_b, m_conv_b_w, m_w_out, m_ffn_norm_g, m_w_up, m_conv_f_w, m_w_down, m_final_norm_g, v_mix_norm_g, v_w_in, v_b_in, v_conv_a_w, v_conv_a_b, v_ln_a_g, v_ln_a_b, v_conv_b_w, v_w_out, v_ffn_norm_g, v_w_up, v_conv_f_w, v_w_down, v_final_norm_g):
    w = dict(mix_norm_g=mix_norm_g, w_in=w_in, b_in=b_in, conv_a_w=conv_a_w, conv_a_b=conv_a_b, ln_a_g=ln_a_g,
             ln_a_b=ln_a_b, conv_b_w=conv_b_w, w_out=w_out, ffn_norm_g=ffn_norm_g, w_up=w_up, conv_f_w=conv_f_w,
             w_down=w_down, final_norm_g=final_norm_g)
    m = dict(mix_norm_g=m_mix_norm_g, w_in=m_w_in, b_in=m_b_in, conv_a_w=m_conv_a_w, conv_a_b=m_conv_a_b,
             ln_a_g=m_ln_a_g, ln_a_b=m_ln_a_b, conv_b_w=m_conv_b_w, w_out=m_w_out, ffn_norm_g=m_ffn_norm_g,
             w_up=m_w_up, conv_f_w=m_conv_f_w, w_down=m_w_down, final_norm_g=m_final_norm_g)
    v = dict(mix_norm_g=v_mix_norm_g, w_in=v_w_in, b_in=v_b_in, conv_a_w=v_conv_a_w, conv_a_b=v_conv_a_b,
             ln_a_g=v_ln_a_g, ln_a_b=v_ln_a_b, conv_b_w=v_conv_b_w, w_out=v_w_out, ffn_norm_g=v_ffn_norm_g,
             w_up=v_w_up, conv_f_w=v_conv_f_w, w_down=v_w_down, final_norm_g=v_final_norm_g)
    order = list(w)
    n_layers = w_in.shape[0]
    xs = x[0]
    target = loss_target[0]
    flip = lambda a: jnp.transpose(a, (0, 2, 1))
    wt, mt, vt = ({n: flip(d[n]) if n in COL_SHARDED else d[n] for n in BIG} for d in (w, m, v))
    px, py, pc = _place()
    where = jnp.stack([pc, 2 * px + py]).astype(jnp.int32)
    me = 4 * px + 2 * py + pc

    assert BIG == MIX_PART + FFN_PART
    key = lambda n: n + "_t" if n in COL_SHARDED else n
    shard = lambda n, l: wt[n][l].astype(BF16)

    def gather_start(names, l, after, tag):
        shards = [shard(n, l) for n in names]
        lands = _gather_landings(shards, me, name=f"gather_landing_{tag}")
        return _split_start(shards, lands, _gather_plan_first, 4 * len(shards), after, name=f"gather_first_start_{tag}")

    def gather_mid(first, after, tag):
        return _split_wait(first[0], first[1], first[2], first[3], after, _gather_plan_first,
                           name=f"gather_first_wait_{tag}")[1]

    def forward_start(lands, after, tag):
        return _split_start([], lands, _gather_plan_second, 3 * len(lands), after, name=f"gather_second_start_{tag}")

    def forward_finish(second, after, tag):
        return _split_wait(second[0], second[1], [], second[3], after, _gather_plan_second,
                           name=f"gather_second_wait_{tag}")[1]

    gathered = _all_gather([shard(n, 0) for n in MIX_PART] + [w[n] for n in CONV], name="gather_weights_0")
    params = [{n: w[n][l] for n in REPLICATED} for l in range(n_layers)]
    for n, g in zip(CONV, gathered[len(MIX_PART):]):
        n_dev, _, taps, c = g.shape
        full = g.transpose(1, 2, 0, 3).reshape(n_layers, taps, n_dev * c)
        for l in range(n_layers):
            params[l][n] = full[l]
    for n, g in zip(MIX_PART, gathered):
        params[0][key(n)] = _weights_from_gathered(g)
    ffn_first = gather_start(FFN_PART, 0, gathered[0], "0_ffn")
    pending = {}

    h = xs
    saved = []
    for l in range(n_layers):
        nxt = l + 1 if l + 1 < n_layers else None

        def before_up(x1, l=l, nxt=nxt):
            if l == 0:
                second = forward_start(gather_mid(ffn_first, x1, "0_ffn"), x1, "0_ffn")
                after = second[4]
            else:
                second = pending[l]["ffn"]
                after = x1
            if nxt is not None:
                pending[nxt] = dict(first=gather_start(BIG, nxt, after, str(nxt)))
                after = pending[nxt]["first"][4]
            for n, g in zip(FFN_PART, forward_finish(second, after, f"{l}_ffn")):
                params[l][key(n)] = _weights_from_gathered(g)

        h, keep = _layer_fwd(h, params[l], str(l), dep=ffn_first[4] if l == 0 else None, before_up=before_up)
        saved.append(keep)
        if nxt is not None:
            arrived = gather_mid(pending[nxt]["first"], h, str(nxt))
            mix_second = forward_start(arrived[:len(MIX_PART)], h, f"{nxt}_mix")
            pending[nxt]["ffn"] = forward_start(arrived[len(MIX_PART):], mix_second[4], f"{nxt}_ffn")
            for n, g in zip(MIX_PART, forward_finish(mix_second, pending[nxt]["ffn"][4], f"{nxt}_mix")):
                params[nxt][key(n)] = _weights_from_gathered(g)

    def start_siblings(slabs, after, tag):
        mines = [s.reshape(N_CHIP, 2, *s.shape[1:]) for s in slabs]
        lands = _unwritten([jax.ShapeDtypeStruct((N_CHIP, *m.shape[2:]), m.dtype) for m in mines],
                           name=f"reduce_siblings_landing_{tag}")
        return _split_start(mines, lands, _siblings_plan, len(mines), after, name=f"reduce_siblings_start_{tag}")

    def start_chips(sib, after, tag):
        mines, theirs = _split_wait(sib[0], sib[1], sib[2], sib[3], after, _siblings_plan,
                                    name=f"reduce_siblings_wait_{tag}")
        pairs, lands = _pair_sum(mines, theirs, where, name=f"pair_sum_{tag}")
        return _split_start(pairs, lands, _chips_plan, 3 * len(pairs), after, name=f"reduce_chips_start_{tag}")

    def finish_reduce(fly, after, tag):
        return _split_wait(fly[0], fly[1], fly[2], fly[3], after, _chips_plan, name=f"reduce_chips_wait_{tag}")[1]

    loss_sq, dh, dh_b, dgf = _loss_bwd(h, _row(final_norm_g), target, name="loss")
    conv_g = {n: [None] * n_layers for n in CONV}
    rep_g = [None] * n_layers
    siblings = {}
    flights = {}
    token = None
    for l in reversed(range(n_layers)):
        def ffn_grads(g, after, l=l):
            siblings[l, "ffn"] = start_siblings([_slabs_from_full(g[n]) for n in FFN_PART], after, f"{l}_ffn")
            return siblings[l, "ffn"][4]

        def ffn_sent(after, l=l):
            flights[l, "ffn"] = start_chips(siblings[l, "ffn"], after, f"{l}_ffn")
            return flights[l, "ffn"][4]

        def mix_grads(g, conv, after, l=l):
            for n in CONV:
                conv_g[n][l] = conv[n]
            slabs = [_slabs_from_full(g[n]) for n in MIX_PART]
            if l == 0:
                for n in CONV:
                    full = jnp.stack(conv_g[n])
                    _, taps, c = full.shape
                    slabs.append(full.reshape(n_layers, taps, N_DEV, c // N_DEV).transpose(2, 0, 1, 3)
                                 .reshape(N_DEV, n_layers * taps, c // N_DEV))
            siblings[l, "mix"] = start_siblings(slabs, after, f"{l}_mix")
            return siblings[l, "mix"][4]

        def mix_sent(after, l=l):
            flights[l, "mix"] = start_chips(siblings[l, "mix"], after, f"{l}_mix")
            return flights[l, "mix"][4]

        dh, dh_b, rep_g[l], token = _layer_bwd(dh, dh_b, params[l], saved[l], str(l), ffn_grads, ffn_sent,
                                               mix_grads, mix_sent, dep=token)

    sums = {key: finish_reduce(fly, dh, f"{key[0]}_{key[1]}") for key, fly in flights.items() if key != (0, "mix")}
    out = {k: {} for k in KINDS}

    def adamw_big(names, part, dep):
        for q, n in enumerate(names):
            layer_parts = [sums[l, part][q] for l in range(n_layers)]
            res = _adamw_sharded(layer_parts, wt[n], mt[n], vt[n], name=f"adamw_{n}", dep=dep)
            for k, r in zip(KINDS, res):
                out[k][n] = flip(r) if n in COL_SHARDED else r

    adamw_big(FFN_PART, "ffn", token)

    rep_cols = [rep_g[l][n] for l in range(n_layers) for n in REPLICATED] + [dgf, loss_sq]
    rep_all = _all_gather([_fold_partials(rep_cols, name="fold_small")], name="gather_small")[0]
    with_final = lambda d: {**{n: d[n] for n in REPLICATED}, "final_norm_g": _row(d["final_norm_g"])}
    loss, rep_res = _adamw_replicated(rep_all, REPLICATED, with_final(w), with_final(m), with_final(v),
                                      loss_sq.shape[1], name="adamw_small")
    for n, res in rep_res.items():
        for k, r in zip(KINDS, res):
            out[k][n] = r.reshape(w[n].shape)

    last = finish_reduce(flights[0, "mix"], rep_res["b_in"][0], "0_mix")
    sums[0, "mix"] = last[:len(MIX_PART)]
    adamw_big(MIX_PART, "mix", None)
    for n, p in zip(CONV, last[len(MIX_PART):]):
        as_one = lambda a: a.reshape(1, *p.shape[1:])
        for k, r in zip(KINDS, _adamw_sharded([p], as_one(w[n]), as_one(m[n]), as_one(v[n]), name=f"adamw_{n}")):
            out[k][n] = r.reshape(w[n].shape)

    grad_x = dh.reshape(x.shape)
    return (loss.reshape(()), grad_x, *[out["grad"][n] for n in order], *[out["delta"][n] for n in order],
            *[out["m"][n] for n in order], *[out["v"][n] for n in order])
```

```python
import functools

import jax
import jax.numpy as jnp
from jax import lax
from jax.experimental import pallas as pl
from jax.experimental.pallas import tpu as pltpu

F32 = jnp.float32
BF16 = jnp.bfloat16

N_DEV = 8
N_CHIP = 4
D_CONF = 512
CONF_K = 31
SHORT_K = 3
EPS = 1e-6
HALO = 32
HALO3 = 8
HALO3_BLK = 16
LANES = 128
SUB = 8
VMEM_LIMIT = 56 * 1024 * 1024

ADAM_LR = 0.001
ADAM_B1 = 0.9
ADAM_B2 = 0.999
ADAM_EPS = 1e-08
ADAM_WD = 0.01
ADAM_STEP = 10

MESH = pl.DeviceIdType.MESH
ANY = pl.BlockSpec(memory_space=pl.ANY)


def _params(*sem):
    return pltpu.CompilerParams(dimension_semantics=sem, vmem_limit_bytes=VMEM_LIMIT)


def _resident(shape, index_map):
    return pl.BlockSpec(shape, index_map, pipeline_mode=pl.Buffered(1))


def _row_loop(n_rows, rb, fn, unroll=1):
    rb = min(rb, n_rows)

    def body(i, carry):
        fn(pl.ds(pl.multiple_of(i * rb, rb), rb))
        return carry
    lax.fori_loop(0, n_rows // rb, body, 0, unroll=unroll)


def _rows8(v):
    acc = v[0:SUB]
    for k in range(1, v.shape[0] // SUB):
        acc = acc + v[k * SUB:(k + 1) * SUB]
    return acc


def _sigmoid(z):
    return 0.5 * jnp.tanh(0.5 * z) + 0.5


def _dot(a, b):
    return jnp.dot(a, b, preferred_element_type=F32)


def _dot_nt(a, b):
    return lax.dot_general(a, b, (((1,), (1,)), ((), ())), preferred_element_type=F32)


def _dot_tn(a, b):
    return lax.dot_general(a, b, (((0,), (0,)), ((), ())), preferred_element_type=F32)


def _replicate_taps(w_ref, wrep, taps):
    for k in range(taps):
        wrep[pl.ds(k * SUB, SUB), :] = jnp.broadcast_to(w_ref[pl.ds(k, 1), :], (SUB, w_ref.shape[1]))


def _shift_copies(win, shf, lanes):
    span = win.shape[0] - SUB
    for r in range(1, SUB):
        for j0 in range(0, span, 64):
            n = min(64, span - j0)
            shf[r - 1, pl.ds(j0, n), lanes] = win[pl.ds(j0 + r, n), lanes]


def _rows_at(win, shf, off, rb, lanes):
    if shf is None or off % SUB == 0:
        return win[pl.ds(off, rb), lanes]
    return shf[off % SUB - 1, pl.ds(off - off % SUB, rb), lanes]


def _conv_taps(win, wrep, out, *, taps, n_rows, base, width, transposed=False, bias_ref=None, shf=None):
    rb = min(64, n_rows)

    def lane_body(cb, carry):
        lanes = pl.ds(pl.multiple_of(cb * LANES, LANES), LANES)
        if shf is not None:
            _shift_copies(win, shf, lanes)
        for r0 in range(0, n_rows, rb):
            acc = None
            for k in range(taps):
                off = (taps - 1 - k) if transposed else (k - (taps - 1))
                wk = jnp.tile(wrep[pl.ds(k * SUB, SUB), lanes], (rb // SUB, 1))
                term = wk * _rows_at(win, shf, base + r0 + off, rb, lanes)
                acc = term if acc is None else acc + term
            if bias_ref is not None:
                acc = acc + bias_ref[:, lanes]
            out[pl.ds(r0, rb), lanes] = acc.astype(out.dtype)
        return carry

    lax.fori_loop(0, width // LANES, lane_body, 0)


def _conv_bwd_taps(win, wrep, x_cur, dx_out, dw_acc, *, taps, n_rows, width, shf=None):
    rb = min(32 if taps > 8 else 64, n_rows)

    def lane_body(cb, carry):
        lanes = pl.ds(pl.multiple_of(cb * LANES, LANES), LANES)
        if shf is not None:
            _shift_copies(win, shf, lanes)
        sums = [None] * taps
        for r0 in range(0, n_rows, rb):
            xv = x_cur[pl.ds(r0, rb), lanes].astype(F32)
            acc = None
            for k in range(taps):
                shifted = _rows_at(win, shf, r0 + taps - 1 - k, rb, lanes)
                term = jnp.tile(wrep[pl.ds(k * SUB, SUB), lanes], (rb // SUB, 1)) * shifted
                acc = term if acc is None else acc + term
                part = _rows8(xv * shifted)
                sums[k] = part if sums[k] is None else sums[k] + part
            dx_out[pl.ds(r0, rb), lanes] = acc.astype(dx_out.dtype)
        for k in range(taps):
            dw_acc[pl.ds(k * SUB, SUB), lanes] += sums[k]
        return carry

    lax.fori_loop(0, width // LANES, lane_body, 0)


def _fold8(acc_ref, taps):
    return jnp.concatenate(
        [jnp.sum(acc_ref[pl.ds(k * SUB, SUB), :], axis=0, keepdims=True) for k in range(taps)], axis=0)


def _seq_tile(s_len):
    return min(512, s_len)


def _mm_tile(s_len):
    return min(512, s_len)


def _ff_chunk(ff):
    best = LANES
    for c in range(LANES, 1408 + 1, LANES):
        if ff % c == 0:
            best = c
    return best


def _col_tile(n):
    for c in (512, 1408, 256, LANES):
        if n % c == 0:
            return c
    return n


def _rms_matmul(x, g, wt, b, *, name, dep=None):
    s_len, d = x.shape
    n = wt.shape[0]
    tm = _mm_tile(s_len)
    cn = _col_tile(n)
    has_bias = b is not None

    def body(*refs):
        x_ref, g_ref, w_ref = refs[0:3]
        b_ref = refs[3] if has_bias else None
        o_ref, h_ref = refs[-2:]

        def blk(rows):
            xv = x_ref[rows, :]
            r = lax.rsqrt(jnp.mean(xv * xv, axis=-1, keepdims=True) + EPS)
            h_ref[rows, :] = ((xv * r) * g_ref[...]).astype(BF16)

        rb = min(128, tm)
        for r0 in range(0, tm, rb):
            blk(pl.ds(r0, rb))
        for j in range(n // cn):
            acc = _dot_nt(h_ref[...], w_ref[j * cn:(j + 1) * cn, :])
            if has_bias:
                acc = acc + b_ref[:, j * cn:(j + 1) * cn]
            o_ref[:, j * cn:(j + 1) * cn] = acc.astype(BF16)

    in_specs = [pl.BlockSpec((tm, d), lambda i: (i, 0)), _resident((1, d), lambda i: (0, 0)),
                _resident((n, d), lambda i: (0, 0))]
    args = [x, g, wt]
    if has_bias:
        in_specs.append(_resident((1, n), lambda i: (0, 0)))
        args.append(b)
    in_specs.append(ANY)
    args.append(x if dep is None else dep)
    return pl.pallas_call(
        body, grid=(s_len // tm,), in_specs=in_specs,
        out_specs=[pl.BlockSpec((tm, n), lambda i: (i, 0)), pl.BlockSpec((tm, d), lambda i: (i, 0))],
        out_shape=[jax.ShapeDtypeStruct((s_len, n), BF16), jax.ShapeDtypeStruct((s_len, d), BF16)],
        compiler_params=_params("parallel"), name=name,
    )(*args)


def _mix_windows(u_ref, uh_ref, gw, pw, first, t):
    c = D_CONF
    uh = uh_ref[...].astype(F32)
    gw[0:HALO, :] = jnp.where(first, 0.0, uh[:, 0:c] * _sigmoid(uh[:, c:2 * c]))
    pw[0:HALO3, :] = jnp.where(first, 0.0, uh[HALO - HALO3:HALO, 3 * c:4 * c] * uh[HALO - HALO3:HALO, 4 * c:5 * c])

    def blk(rows):
        dst = pl.ds(pl.multiple_of(rows.start + HALO, SUB), rows.size)
        gw[dst, :] = u_ref[rows, 0:c].astype(F32) * _sigmoid(u_ref[rows, c:2 * c].astype(F32))
        dst3 = pl.ds(pl.multiple_of(rows.start + HALO3, SUB), rows.size)
        pw[dst3, :] = u_ref[rows, 3 * c:4 * c].astype(F32) * u_ref[rows, 4 * c:5 * c].astype(F32)
    _row_loop(t, 64, blk)


def _mix_fwd(u, x0, wa, ba, lg, lb, wb, w_out, *, name):
    s_len, d_in = u.shape
    d = x0.shape[1]
    c = D_CONF
    t = _seq_tile(s_len)
    per = t // HALO

    def body(u_ref, uh_ref, x0_ref, wa_ref, ba_ref, lg_ref, lb_ref, wb_ref, wo_ref, y_ref, x1_ref, ca, cb,
             gw, pw, wrep_a, wrep_b, shf):
        first = pl.program_id(0) == 0
        _mix_windows(u_ref, uh_ref, gw, pw, first, t)
        _replicate_taps(wa_ref, wrep_a, CONF_K)
        _replicate_taps(wb_ref, wrep_b, SHORT_K)
        _conv_taps(gw, wrep_a, ca, taps=CONF_K, n_rows=t, base=HALO, width=c, bias_ref=ba_ref, shf=shf)
        _conv_taps(pw, wrep_b, cb, taps=SHORT_K, n_rows=t, base=HALO3, width=c)

        def blk(rows):
            cv = ca[rows, :]
            mu = jnp.mean(cv, axis=-1, keepdims=True)
            xc = cv - mu
            var = jnp.mean(xc * xc, axis=-1, keepdims=True)
            ln = (xc * lax.rsqrt(var + EPS)) * lg_ref[...] + lb_ref[...]
            y_ref[rows, 0:c] = (ln * _sigmoid(ln)).astype(BF16)
            y_ref[rows, c:2 * c] = (u_ref[rows, 2 * c:3 * c].astype(F32) * cb[rows, :]).astype(BF16)
        half = t // 2
        rb = min(64, half)
        for lo in range(0, t, half):
            for r0 in range(lo, lo + half, rb):
                blk(pl.ds(r0, rb))
            x1_ref[lo:lo + half, :] = x0_ref[lo:lo + half, :] + _dot(y_ref[lo:lo + half, :], wo_ref[...])

    small = lambda r: _resident((r, c), lambda i: (0, 0))
    return pl.pallas_call(
        body, grid=(s_len // t,),
        in_specs=[pl.BlockSpec((t, d_in), lambda i: (i, 0)),
                  pl.BlockSpec((HALO, d_in), lambda i: (jnp.maximum(i * per - 1, 0), 0)),
                  pl.BlockSpec((t, d), lambda i: (i, 0)),
                  small(CONF_K), small(1), small(1), small(1), small(SHORT_K),
                  _resident((2 * c, d), lambda i: (0, 0))],
        out_specs=[pl.BlockSpec((t, 2 * c), lambda i: (i, 0)), pl.BlockSpec((t, d), lambda i: (i, 0)),
                   pl.BlockSpec((t, c), lambda i: (i, 0)), pl.BlockSpec((t, c), lambda i: (i, 0))],
        out_shape=[jax.ShapeDtypeStruct((s_len, 2 * c), BF16), jax.ShapeDtypeStruct((s_len, d), F32),
                   jax.ShapeDtypeStruct((s_len, c), F32), jax.ShapeDtypeStruct((s_len, c), F32)],
        scratch_shapes=[pltpu.VMEM((HALO + t, c), F32), pltpu.VMEM((HALO3 + t, c), F32),
                        pltpu.VMEM((CONF_K * SUB, c), F32), pltpu.VMEM((SHORT_K * SUB, c), F32),
                        pltpu.VMEM((SUB - 1, HALO + t, c), F32)],
        compiler_params=_params("arbitrary"), name=name,
    )(u, u, x0, wa, ba, lg, lb, wb, w_out)


def _ffn_fwd(uf, x1, wf, w_down, *, name):
    s_len, ff2 = uf.shape
    ff = ff2 // 2
    d = x1.shape[1]
    t = _seq_tile(s_len)
    fc = _ff_chunk(ff)
    nc = ff // fc
    per = t // HALO3_BLK
    half = t // 2
    rb = min(64, half)

    def body(ug_ref, ugh_ref, uv_ref, uvh_ref, x1_ref, wfg_ref, wfv_ref, wd_ref,
             act_ref, x2_ref, cg_ref, cv_ref, gwin, vwin, wrep_g, wrep_v):
        first = pl.program_id(0) == 0
        first_chunk = pl.program_id(1) == 0
        lo8 = HALO3_BLK - HALO3
        gwin[0:HALO3, :] = jnp.where(first, 0.0, ugh_ref[...].astype(F32)[lo8:HALO3_BLK])
        vwin[0:HALO3, :] = jnp.where(first, 0.0, uvh_ref[...].astype(F32)[lo8:HALO3_BLK])
        _replicate_taps(wfg_ref, wrep_g, SHORT_K)
        _replicate_taps(wfv_ref, wrep_v, SHORT_K)
        chunk_rows = pl.ds(pl.multiple_of(pl.program_id(1) * fc, fc), fc)

        def conv(win, wrep, r0, lanes):
            acc = None
            for k in range(SHORT_K):
                wk = jnp.tile(wrep[k * SUB:(k + 1) * SUB, lanes], (rb // SUB, 1))
                off = HALO3 + r0 + k - (SHORT_K - 1)
                term = wk * win[off:off + rb, lanes]
                acc = term if acc is None else acc + term
            return acc

        for lo in range(0, t, half):
            for r0 in range(lo, lo + half, rb):
                gwin[HALO3 + r0:HALO3 + r0 + rb, :] = ug_ref[r0:r0 + rb, :].astype(F32)
                vwin[HALO3 + r0:HALO3 + r0 + rb, :] = uv_ref[r0:r0 + rb, :].astype(F32)
            for cb in range(fc // LANES):
                lanes = slice(cb * LANES, (cb + 1) * LANES)
                for r0 in range(lo, lo + half, rb):
                    gv = conv(gwin, wrep_g, r0, lanes).astype(BF16)
                    vv = conv(vwin, wrep_v, r0, lanes).astype(BF16)
                    cg_ref[r0:r0 + rb, lanes] = gv
                    cv_ref[r0:r0 + rb, lanes] = vv
                    act_ref[r0:r0 + rb, lanes] = (gv * _sigmoid(gv)) * vv
            base = jnp.where(first_chunk, x1_ref[lo:lo + half, :], x2_ref[lo:lo + half, :])
            x2_ref[lo:lo + half, :] = base + _dot(act_ref[lo:lo + half, :], wd_ref[chunk_rows, :])

    halo_map = lambda off: (lambda i, j: (jnp.maximum(i * per - 1, 0), j + off))
    return pl.pallas_call(
        body, grid=(s_len // t, nc),
        in_specs=[pl.BlockSpec((t, fc), lambda i, j: (i, j)), pl.BlockSpec((HALO3_BLK, fc), halo_map(0)),
                  pl.BlockSpec((t, fc), lambda i, j: (i, j + nc)), pl.BlockSpec((HALO3_BLK, fc), halo_map(nc)),
                  pl.BlockSpec((t, d), lambda i, j: (i, 0)),
                  pl.BlockSpec((SHORT_K, fc), lambda i, j: (0, j)),
                  pl.BlockSpec((SHORT_K, fc), lambda i, j: (0, j + nc)),
                  _resident((ff, d), lambda i, j: (0, 0))],
        out_specs=[pl.BlockSpec((t, fc), lambda i, j: (i, j)), pl.BlockSpec((t, d), lambda i, j: (i, 0)),
                   pl.BlockSpec((t, fc), lambda i, j: (i, j)), pl.BlockSpec((t, fc), lambda i, j: (i, j))],
        out_shape=[jax.ShapeDtypeStruct((s_len, ff), BF16), jax.ShapeDtypeStruct((s_len, d), F32),
                   jax.ShapeDtypeStruct((s_len, ff), BF16), jax.ShapeDtypeStruct((s_len, ff), BF16)],
        scratch_shapes=[pltpu.VMEM((HALO3 + t, fc), F32), pltpu.VMEM((HALO3 + t, fc), F32),
                        pltpu.VMEM((SHORT_K * SUB, fc), F32), pltpu.VMEM((SHORT_K * SUB, fc), F32)],
        compiler_params=_params("parallel", "arbitrary"), name=name,
    )(uf, uf, uf, uf, x1, wf, wf, w_down)


def _loss_bwd(x, g, target, *, name):
    s_len, d = x.shape
    t = _seq_tile(s_len)

    def body(x_ref, g_ref, t_ref, l_ref, dx_ref, dxb_ref, dg_ref):
        @pl.when(pl.program_id(0) == 0)
        def _():
            l_ref[...] = jnp.zeros_like(l_ref)
            dg_ref[...] = jnp.zeros_like(dg_ref)

        def blk(rows):
            xv = x_ref[rows, :]
            r = lax.rsqrt(jnp.mean(xv * xv, axis=-1, keepdims=True) + EPS)
            xn = xv * r
            e = xn * g_ref[...] - t_ref[rows, :]
            l_ref[...] += _rows8(e * e)
            dy = e * (1.0 / d)
            dg_ref[...] += _rows8(dy * xn)
            dn = dy * g_ref[...]
            dx = r * (dn - xn * jnp.mean(dn * xn, axis=-1, keepdims=True))
            dx_ref[rows, :] = dx
            dxb_ref[rows, :] = dx.astype(BF16)
        _row_loop(t, 64, blk)

    row = pl.BlockSpec((t, d), lambda i: (i, 0))
    part = pl.BlockSpec((SUB, d), lambda i: (0, 0))
    return pl.pallas_call(
        body, grid=(s_len // t,),
        in_specs=[row, _resident((1, d), lambda i: (0, 0)), row],
        out_specs=[part, row, row, part],
        out_shape=[jax.ShapeDtypeStruct((SUB, d), F32), jax.ShapeDtypeStruct((s_len, d), F32),
                   jax.ShapeDtypeStruct((s_len, d), BF16), jax.ShapeDtypeStruct((SUB, d), F32)],
        compiler_params=_params("arbitrary"), name=name,
    )(x, g, target)


def _ffn_bwd(dx2, uf, cg, cv, wf, w_down, *, name, dep=None):
    s_len, ff2 = uf.shape
    ff = ff2 // 2
    d = dx2.shape[1]
    t = _seq_tile(s_len)
    n_t = s_len // t
    fc = _ff_chunk(ff)
    nc = ff // fc

    def body(dx_ref, ug_ref, uv_ref, cg_ref, cv_ref, wfg_ref, wfv_ref, wd_ref, dep_ref,
             duf_ref, dwg_ref, dwv_ref, dact, dgw, dvw, awg, awv, wrep_g, wrep_v):
        i = pl.program_id(1)

        @pl.when(i == 0)
        def _():
            dgw[t:t + HALO3, :] = jnp.zeros((HALO3, fc), F32)
            dvw[t:t + HALO3, :] = jnp.zeros((HALO3, fc), F32)
            awg[...] = jnp.zeros_like(awg)
            awv[...] = jnp.zeros_like(awv)

        _replicate_taps(wfg_ref, wrep_g, SHORT_K)
        _replicate_taps(wfv_ref, wrep_v, SHORT_K)

        def blk(rows):
            gv = cg_ref[rows, :]
            sg = _sigmoid(gv)
            da = dact[rows, :].astype(BF16)
            dgw[rows, :] = ((da * cv_ref[rows, :]) * (sg * (1.0 + gv * (1.0 - sg)))).astype(F32)
            dvw[rows, :] = (da * (gv * sg)).astype(F32)

        dact[...] = _dot_nt(dx_ref[...], wd_ref[...])
        _row_loop(t, 32, blk, unroll=2)

        _conv_bwd_taps(dgw, wrep_g, ug_ref, duf_ref.at[0], awg, taps=SHORT_K, n_rows=t, width=fc)
        _conv_bwd_taps(dvw, wrep_v, uv_ref, duf_ref.at[1], awv, taps=SHORT_K, n_rows=t, width=fc)
        dgw[t:t + HALO3, :] = dgw[0:HALO3, :]
        dvw[t:t + HALO3, :] = dvw[0:HALO3, :]

        @pl.when(i == n_t - 1)
        def _():
            dwg_ref[...] = _fold8(awg, SHORT_K)
            dwv_ref[...] = _fold8(awv, SHORT_K)

    rev = lambda i: n_t - 1 - i
    gate = pl.BlockSpec((t, fc), lambda j, i: (rev(i), j))
    value = pl.BlockSpec((t, fc), lambda j, i: (rev(i), j + nc))
    return pl.pallas_call(
        body, grid=(nc, n_t),
        in_specs=[pl.BlockSpec((t, d), lambda j, i: (rev(i), 0)), gate, value, gate, gate,
                  pl.BlockSpec((SHORT_K, fc), lambda j, i: (0, j)),
                  pl.BlockSpec((SHORT_K, fc), lambda j, i: (0, j + nc)),
                  pl.BlockSpec((fc, d), lambda j, i: (j, 0)), ANY],
        out_specs=[pl.BlockSpec((2, t, fc), lambda j, i: (0, rev(i), j)),
                   pl.BlockSpec((SHORT_K, fc), lambda j, i: (0, j)), pl.BlockSpec((SHORT_K, fc), lambda j, i: (0, j))],
        out_shape=[jax.ShapeDtypeStruct((2, s_len, ff), BF16),
                   jax.ShapeDtypeStruct((SHORT_K, ff), F32), jax.ShapeDtypeStruct((SHORT_K, ff), F32)],
        scratch_shapes=[pltpu.VMEM((t, fc), F32),
                        pltpu.VMEM((t + HALO3, fc), F32), pltpu.VMEM((t + HALO3, fc), F32),
                        pltpu.VMEM((SHORT_K * SUB, fc), F32), pltpu.VMEM((SHORT_K * SUB, fc), F32),
                        pltpu.VMEM((SHORT_K * SUB, fc), F32), pltpu.VMEM((SHORT_K * SUB, fc), F32)],
        compiler_params=_params("arbitrary", "arbitrary"), name=name,
    )(dx2, uf, uf, cg, cv, wf, wf, w_down, uf if dep is None else dep)


def _mix_bwd(dx1, u, ca, cb, wa, lg, lb, wb, w_out, *, name, dep=None):
    s_len, d_in = u.shape
    d = dx1.shape[1]
    c = D_CONF
    t = _seq_tile(s_len)
    n_t = s_len // t

    def body(dx_ref, u_ref, ca_ref, cb_ref, wa_ref, lg_ref, lb_ref, wb_ref, wo_ref, dep_ref,
             du_ref, dwa_ref, dwb_ref, dba_ref, dlg_ref, dlb_ref, dbin_ref,
             glu, prod, dyc, dcaw, dcbw, dglu, dp, awa, awb, wrep_a, wrep_b, shf):
        i = pl.program_id(0)
        _replicate_taps(wa_ref, wrep_a, CONF_K)
        _replicate_taps(wb_ref, wrep_b, SHORT_K)

        @pl.when(i == 0)
        def _():
            dcaw[t:t + HALO, :] = jnp.zeros((HALO, c), F32)
            dcbw[t:t + HALO3, :] = jnp.zeros((HALO3, c), F32)
            awa[...] = jnp.zeros_like(awa)
            awb[...] = jnp.zeros_like(awb)
            dba_ref[...] = jnp.zeros_like(dba_ref)
            dlg_ref[...] = jnp.zeros_like(dlg_ref)
            dlb_ref[...] = jnp.zeros_like(dlb_ref)
            dbin_ref[...] = jnp.zeros_like(dbin_ref)

        def blk1(rows):
            cv = ca_ref[rows, :]
            mu = jnp.mean(cv, axis=-1, keepdims=True)
            xc = cv - mu
            rstd = lax.rsqrt(jnp.mean(xc * xc, axis=-1, keepdims=True) + EPS)
            nrm = xc * rstd
            ln = nrm * lg_ref[...] + lb_ref[...]
            sg = _sigmoid(ln)
            dln = dyc[rows, 0:c] * (sg * (1.0 + ln * (1.0 - sg)))
            dlg_ref[...] += _rows8(dln * nrm)
            dlb_ref[...] += _rows8(dln)
            dn = dln * lg_ref[...]
            dca = rstd * (dn - jnp.mean(dn, axis=-1, keepdims=True)
                          - nrm * jnp.mean(dn * nrm, axis=-1, keepdims=True))
            dcaw[rows, :] = dca
            dba_ref[...] += _rows8(dca)
            ds = dyc[rows, c:2 * c]
            dgb = ds * cb_ref[rows, :]
            dcbw[rows, :] = ds * u_ref[rows, 2 * c:3 * c].astype(F32)
            du_ref[rows, 2 * c:3 * c] = dgb.astype(BF16)
            dbin_ref[:, 2 * c:3 * c] += _rows8(dgb)
            glu[rows, :] = u_ref[rows, 0:c].astype(F32) * _sigmoid(u_ref[rows, c:2 * c].astype(F32))
            prod[rows, :] = u_ref[rows, 3 * c:4 * c].astype(F32) * u_ref[rows, 4 * c:5 * c].astype(F32)
        half = t // 2
        rb = min(64, half)
        for lo in range(0, t, half):
            dyc[lo:lo + half, :] = _dot_nt(dx_ref[lo:lo + half, :], wo_ref[...])
            for r0 in range(lo, lo + half, rb):
                blk1(pl.ds(r0, rb))

        _conv_bwd_taps(dcaw, wrep_a, glu, dglu, awa, taps=CONF_K, n_rows=t, width=c, shf=shf)
        _conv_bwd_taps(dcbw, wrep_b, prod, dp, awb, taps=SHORT_K, n_rows=t, width=c)
        dcaw[t:t + HALO, :] = dcaw[0:HALO, :]
        dcbw[t:t + HALO3, :] = dcbw[0:HALO3, :]

        def blk2(rows):
            av = u_ref[rows, 0:c].astype(F32)
            sg = _sigmoid(u_ref[rows, c:2 * c].astype(F32))
            dg = dglu[rows, :]
            d_av = dg * sg
            d_ag = (dg * av) * (sg * (1.0 - sg))
            dpv = dp[rows, :]
            d_gc = dpv * u_ref[rows, 4 * c:5 * c].astype(F32)
            d_vs = dpv * u_ref[rows, 3 * c:4 * c].astype(F32)
            du_ref[rows, 0:c] = d_av.astype(BF16)
            du_ref[rows, c:2 * c] = d_ag.astype(BF16)
            du_ref[rows, 3 * c:4 * c] = d_gc.astype(BF16)
            du_ref[rows, 4 * c:5 * c] = d_vs.astype(BF16)
            dbin_ref[:, 0:c] += _rows8(d_av)
            dbin_ref[:, c:2 * c] += _rows8(d_ag)
            dbin_ref[:, 3 * c:4 * c] += _rows8(d_gc)
            dbin_ref[:, 4 * c:5 * c] += _rows8(d_vs)
        _row_loop(t, 64, blk2)

        @pl.when(i == n_t - 1)
        def _():
            dwa_ref[...] = _fold8(awa, CONF_K)
            dwb_ref[...] = _fold8(awb, SHORT_K)

    rev = lambda i: n_t - 1 - i
    small_in = lambda r: _resident((r, c), lambda i: (0, 0))
    small = lambda r: pl.BlockSpec((r, c), lambda i: (0, 0))
    return pl.pallas_call(
        body, grid=(n_t,),
        in_specs=[pl.BlockSpec((t, d), lambda i: (rev(i), 0)),
                  pl.BlockSpec((t, d_in), lambda i: (rev(i), 0)),
                  pl.BlockSpec((t, c), lambda i: (rev(i), 0)), pl.BlockSpec((t, c), lambda i: (rev(i), 0)),
                  small_in(CONF_K), small_in(1), small_in(1), small_in(SHORT_K),
                  _resident((2 * c, d), lambda i: (0, 0)), ANY],
        out_specs=[pl.BlockSpec((t, d_in), lambda i: (rev(i), 0)),
                   small(CONF_K), small(SHORT_K), small(SUB), small(SUB), small(SUB),
                   pl.BlockSpec((SUB, d_in), lambda i: (0, 0))],
        out_shape=[jax.ShapeDtypeStruct((s_len, d_in), BF16),
                   jax.ShapeDtypeStruct((CONF_K, c), F32), jax.ShapeDtypeStruct((SHORT_K, c), F32),
                   jax.ShapeDtypeStruct((SUB, c), F32), jax.ShapeDtypeStruct((SUB, c), F32),
                   jax.ShapeDtypeStruct((SUB, c), F32), jax.ShapeDtypeStruct((SUB, d_in), F32)],
        scratch_shapes=[pltpu.VMEM((t, c), F32), pltpu.VMEM((t, c), F32), pltpu.VMEM((t, 2 * c), F32),
                        pltpu.VMEM((t + HALO, c), F32), pltpu.VMEM((t + HALO3, c), F32),
                        pltpu.VMEM((t, c), F32), pltpu.VMEM((t, c), F32),
                        pltpu.VMEM((CONF_K * SUB, c), F32), pltpu.VMEM((SHORT_K * SUB, c), F32),
                        pltpu.VMEM((CONF_K * SUB, c), F32), pltpu.VMEM((SHORT_K * SUB, c), F32),
                        pltpu.VMEM((SUB - 1, t + HALO, c), F32)],
        compiler_params=_params("arbitrary"), name=name,
    )(dx1, u, ca, cb, wa, lg, lb, wb, w_out, u if dep is None else dep)


def _matmul_tn(a, b, *, name):
    n_p, s_len, k = a.shape
    n = b.shape[1]
    tk = _col_tile(k)
    per = k // tk
    if per > 2:
        def body(a_ref, b_ref, o_ref):
            o_ref[...] = _dot_tn(a_ref[...], b_ref[...]).astype(BF16)

        return pl.pallas_call(
            body, grid=(n_p, per),
            in_specs=[pl.BlockSpec((None, s_len, tk), lambda p, j: (p, 0, j)),
                      _resident((s_len, n), lambda p, j: (0, 0))],
            out_specs=pl.BlockSpec((tk, n), lambda p, j: (p * per + j, 0)),
            out_shape=jax.ShapeDtypeStruct((n_p * k, n), BF16),
            compiler_params=_params("parallel", "parallel"), name=name,
        )(a, b)

    half = s_len // 2

    def body_halves(a_ref, b_ref, o_ref, acc):
        @pl.when(pl.program_id(2) == 0)
        def _():
            acc[...] = _dot_tn(a_ref[...], b_ref[...])

        @pl.when(pl.program_id(2) == 1)
        def _():
            o_ref[...] = (acc[...] + _dot_tn(a_ref[...], b_ref[...])).astype(BF16)

    return pl.pallas_call(
        body_halves, grid=(n_p, per, 2),
        in_specs=[pl.BlockSpec((None, half, tk), lambda p, j, q: (p, q, j)),
                  pl.BlockSpec((half, n), lambda p, j, q: (q, 0))],
        out_specs=pl.BlockSpec((tk, n), lambda p, j, q: (p * per + j, 0)),
        out_shape=jax.ShapeDtypeStruct((n_p * k, n), BF16),
        scratch_shapes=[pltpu.VMEM((tk, n), F32)],
        compiler_params=_params("parallel", "parallel", "arbitrary"), name=name,
    )(a, b)


def _matmul_rmsbwd(dzs, wt, x, g, dx_in, *, name, dep=None):
    s_len, d = x.shape
    n_z, _, nj = dzs.shape
    t = _mm_tile(s_len)

    def body(*refs):
        dz_refs = refs[0:n_z]
        w_refs = refs[n_z:2 * n_z]
        x_ref, g_ref, dxi_ref, _, dx_ref, dxb_ref, dg_ref, dh = refs[2 * n_z:]

        @pl.when(pl.program_id(0) == 0)
        def _():
            dg_ref[...] = jnp.zeros_like(dg_ref)

        def blk(rows):
            xv = x_ref[rows, :]
            r = lax.rsqrt(jnp.mean(xv * xv, axis=-1, keepdims=True) + EPS)
            xn = xv * r
            dhv = dh[rows, :]
            dg_ref[...] += _rows8(dhv * xn)
            dn = dhv * g_ref[...]
            dx = dxi_ref[rows, :] + r * (dn - xn * jnp.mean(dn * xn, axis=-1, keepdims=True))
            dx_ref[rows, :] = dx
            dxb_ref[rows, :] = dx.astype(BF16)

        half = t // 2
        rb = min(128, half)
        for lo in range(0, t, half):
            acc = _dot(dz_refs[0][lo:lo + half, :], w_refs[0][...])
            for q in range(1, n_z):
                acc = acc + _dot(dz_refs[q][lo:lo + half, :], w_refs[q][...])
            dh[lo:lo + half, :] = acc
            for r0 in range(lo, lo + half, rb):
                blk(pl.ds(r0, rb))

    row = pl.BlockSpec((t, d), lambda i: (i, 0))
    in_specs = [pl.BlockSpec((None, t, nj), functools.partial(lambda q, i: (q, i, 0), q)) for q in range(n_z)]
    in_specs += [_resident((nj, d), functools.partial(lambda q, i: (q, 0), q)) for q in range(n_z)]
    in_specs += [row, _resident((1, d), lambda i: (0, 0)), row, ANY]
    return pl.pallas_call(
        body, grid=(s_len // t,), in_specs=in_specs,
        out_specs=[row, row, pl.BlockSpec((SUB, d), lambda i: (0, 0))],
        out_shape=[jax.ShapeDtypeStruct((s_len, d), F32), jax.ShapeDtypeStruct((s_len, d), BF16),
                   jax.ShapeDtypeStruct((SUB, d), F32)],
        scratch_shapes=[pltpu.VMEM((t, d), F32)],
        compiler_params=_params("arbitrary"), name=name,
    )(*([dzs] * n_z), *([wt] * n_z), x, g, dx_in, x if dep is None else dep)


def _row(v):
    return v.reshape(1, -1)


def _layer_fwd(x0, p, tag, dep=None, before_up=None):
    u, h1 = _rms_matmul(x0, _row(p["mix_norm_g"]), p["w_in_t"], _row(p["b_in"]), name=f"in_proj_{tag}", dep=dep)
    ycat, x1, ca, cb = _mix_fwd(u, x0, p["conv_a_w"], _row(p["conv_a_b"]), _row(p["ln_a_g"]), _row(p["ln_a_b"]),
                            p["conv_b_w"], p["w_out"], name=f"mix_fwd_{tag}")
    if before_up is not None:
        before_up(x1)
    uf, h2 = _rms_matmul(x1, _row(p["ffn_norm_g"]), p["w_up_t"], None, name=f"up_proj_{tag}")
    act, x2, cg, cv = _ffn_fwd(uf, x1, p["conv_f_w"], p["w_down"], name=f"ffn_fwd_{tag}")
    return x2, dict(x0=x0, h1=h1, u=u, ca=ca, cb=cb, ycat=ycat, x1=x1, h2=h2, uf=uf, cg=cg, cv=cv, act=act)


def _layer_bwd(dx2, dx2_b, p, saved, tag, ffn_grads, ffn_sent, mix_grads, mix_sent, dep=None):
    d_uf, dwf_g, dwf_v = _ffn_bwd(dx2_b, saved["uf"], saved["cg"], saved["cv"], p["conv_f_w"], p["w_down"],
                                  name=f"ffn_bwd_{tag}", dep=dep)
    g_down = _matmul_tn(saved["act"][None], dx2_b, name=f"dw_down_{tag}")
    g_up = _matmul_tn(d_uf, saved["h2"], name=f"dw_up_{tag}")
    dep_ffn = ffn_grads(dict(w_up=g_up, w_down=g_down), dx2_b)
    dx1, dx1_b, dg2 = _matmul_rmsbwd(d_uf, p["w_up_t"], saved["x1"], _row(p["ffn_norm_g"]), dx2,
                                     name=f"dh_ffn_{tag}", dep=dep_ffn)
    du, dwa, dwb, dba, dlg, dlb, dbin = _mix_bwd(
        dx1_b, saved["u"], saved["ca"], saved["cb"], p["conv_a_w"], _row(p["ln_a_g"]), _row(p["ln_a_b"]),
        p["conv_b_w"], p["w_out"], name=f"mix_bwd_{tag}", dep=ffn_sent(dx1_b))
    g_out = _matmul_tn(saved["ycat"][None], dx1_b, name=f"dw_out_{tag}")
    g_in = _matmul_tn(du[None], saved["h1"], name=f"dw_in_{tag}")
    conv = dict(conv_a_w=dwa, conv_b_w=dwb, conv_f_w=jnp.concatenate([dwf_g, dwf_v], axis=1))
    dep_mix = mix_grads(dict(w_in=g_in, w_out=g_out), conv, dx1_b)
    dx0, dx0_b, dg1 = _matmul_rmsbwd(du[None], p["w_in_t"], saved["x0"], _row(p["mix_norm_g"]), dx1,
                                     name=f"dh_mix_{tag}", dep=dep_mix)
    rep = dict(mix_norm_g=dg1, b_in=dbin, conv_a_b=dba, ln_a_g=dlg, ln_a_b=dlb, ffn_norm_g=dg2)
    return dx0, dx0_b, rep, mix_sent(dx0_b)


def _place():
    return lax.axis_index("x"), lax.axis_index("y"), lax.axis_index("c")


def _all_gather(arrs, *, name):
    n_a = len(arrs)

    def body(*refs):
        ins = refs[0:n_a]
        outs = refs[n_a:2 * n_a]
        send_sems, recv_sems, local_sems = refs[2 * n_a:]
        x, y, c = _place()
        sibling = (x, y, 1 - c)
        chips = [(1 - x, y), (x, 1 - y), (1 - x, 1 - y)]

        def slot(a, px, py, pc):
            return outs[a].at[4 * px + 2 * py + pc]

        def copy(a, k, block, to, src=None):
            return pltpu.make_async_remote_copy(
                src_ref=slot(a, *block) if src is None else src, dst_ref=slot(a, *block),
                send_sem=send_sems.at[a, k], recv_sem=recv_sems.at[a, k],
                device_id=to, device_id_type=MESH)

        me = (x, y, c)
        mine = [pltpu.make_async_copy(ins[a], slot(a, *me), local_sems.at[a]) for a in range(n_a)]
        for cp in mine:
            cp.start()
        started = []
        for a in range(n_a):
            first = [copy(a, 0, me, sibling, src=ins[a])]
            first += [copy(a, 1 + j, me, (*chip, c), src=ins[a]) for j, chip in enumerate(chips)]
            for cp in first:
                cp.start()
            started += first
        for a in range(n_a):
            for j, chip in enumerate(chips):
                copy(a, 1 + j, (*chip, c), me).wait_recv()
                passed = copy(a, 4 + j, (*chip, c), sibling)
                passed.start()
                started.append(passed)
        for a in range(n_a):
            copy(a, 0, sibling, me).wait_recv()
            for j, chip in enumerate(chips):
                copy(a, 4 + j, (*chip, 1 - c), me).wait_recv()
        for cp in started:
            cp.wait_send()
        for cp in mine:
            cp.wait()

    return pl.pallas_call(
        body, in_specs=[ANY] * n_a, out_specs=[ANY] * n_a,
        out_shape=[jax.ShapeDtypeStruct((N_DEV, *a.shape), a.dtype) for a in arrs],
        scratch_shapes=[pltpu.SemaphoreType.DMA((n_a, 7)), pltpu.SemaphoreType.DMA((n_a, 7)),
                        pltpu.SemaphoreType.DMA((n_a,))],
        name=name,
    )(*arrs)


def _row_tile(r, cap):
    for tr in range(min(cap, r) // 16 * 16, 0, -16):
        if r % tr == 0:
            return tr
    return r


def _pair_sum(mines, theirs, where, *, name):
    n_a = len(mines)
    n_chip = mines[0].shape[0]

    def body(where_ref, *refs):
        a_refs = refs[0:n_a]
        b_refs = refs[n_a:2 * n_a]
        p_refs = refs[2 * n_a:3 * n_a]
        l_refs = refs[3 * n_a:4 * n_a]
        q = pl.program_id(0)
        for a in range(n_a):
            p_refs[a][...] = (a_refs[a][...].astype(F32) + b_refs[a][...].astype(F32)).astype(p_refs[a].dtype)

        @pl.when(q == where_ref[1])
        def _():
            for a in range(n_a):
                l_refs[a][...] = p_refs[a][...]

    in_specs, out_p, out_l, shapes = [], [], [], []
    for m in mines:
        _, _, r, c = m.shape
        in_specs.append(pl.BlockSpec((None, None, r, c), lambda q, where_ref: (q, where_ref[0], 0, 0)))
    for m in mines:
        _, _, r, c = m.shape
        in_specs.append(pl.BlockSpec((None, r, c), lambda q, where_ref: (q, 0, 0)))
        out_p.append(pl.BlockSpec((None, r, c), lambda q, where_ref: (q, 0, 0)))
        out_l.append(pl.BlockSpec((None, r, c), lambda q, where_ref: (where_ref[1], 0, 0)))
        shapes.append(jax.ShapeDtypeStruct((n_chip, r, c), m.dtype))
    res = pl.pallas_call(
        body,
        grid_spec=pltpu.PrefetchScalarGridSpec(num_scalar_prefetch=1, grid=(n_chip,), in_specs=in_specs,
                                               out_specs=out_p + out_l),
        out_shape=shapes + shapes,
        compiler_params=_params("arbitrary"), name=name,
    )(where, *mines, *theirs)
    return list(res[:n_a]), list(res[n_a:])


HBM = pl.BlockSpec(memory_space=pltpu.HBM)
SEM = pl.BlockSpec(memory_space=pltpu.SEMAPHORE)
EFFECT = pltpu.SideEffectType.DATAFLOW_SIDE_EFFECTING


def _in_hbm(a):
    return pltpu.with_memory_space_constraint(a, pltpu.HBM)


def _split_start(srcs, lands, plan, n_copies, after, *, name):
    n_s, n_l = len(srcs), len(lands)

    def body(*refs):
        src_refs = refs[0:n_s]
        land_refs = refs[n_s:n_s + n_l]
        send_sems, recv_sems = refs[n_s + n_l + 1], refs[n_s + n_l + 2]
        token = refs[-1]
        for cp in plan(src_refs, land_refs, send_sems, recv_sems):
            cp.start()
        token[...] = jnp.zeros_like(token)

    thru = [pltpu.HBM(a.shape, a.dtype) for a in list(srcs) + list(lands)]
    res = pl.pallas_call(
        body, name=name,
        out_shape=(pltpu.SemaphoreType.DMA((n_copies,)), pltpu.SemaphoreType.DMA((n_copies,)), *thru,
                   jax.ShapeDtypeStruct((SUB, LANES), F32)),
        in_specs=[HBM] * (n_s + n_l) + [ANY],
        out_specs=(SEM, SEM, *([HBM] * (n_s + n_l)), pl.BlockSpec(memory_space=pltpu.VMEM)),
        input_output_aliases={i: 2 + i for i in range(n_s + n_l)},
        compiler_params=pltpu.CompilerParams(has_side_effects=EFFECT),
    )(*[_in_hbm(a) for a in srcs], *[_in_hbm(a) for a in lands], _in_hbm(after))
    return res[0], res[1], list(res[2:2 + n_s]), list(res[2 + n_s:2 + n_s + n_l]), res[-1]


def _split_wait(send_sems, recv_sems, srcs, lands, after, plan, *, name):
    n_s, n_l = len(srcs), len(lands)

    def body(*refs):
        src_refs = refs[0:n_s]
        land_refs = refs[n_s:n_s + n_l]
        send, recv = refs[n_s + n_l], refs[n_s + n_l + 1]
        for cp in plan(src_refs, land_refs, send, recv):
            cp.wait_send()
            cp.wait_recv()

    res = pl.pallas_call(
        body, name=name,
        out_shape=tuple(pltpu.HBM(a.shape, a.dtype) for a in list(srcs) + list(lands)),
        in_specs=[HBM] * (n_s + n_l) + [SEM, SEM, ANY],
        out_specs=tuple([HBM] * (n_s + n_l)),
        input_output_aliases={i: i for i in range(n_s + n_l)},
        compiler_params=pltpu.CompilerParams(has_side_effects=EFFECT),
    )(*srcs, *lands, send_sems, recv_sems, _in_hbm(after))
    return list(res[:n_s]), list(res[n_s:])


def _remote(src, dst, send_sems, recv_sems, k, to):
    return pltpu.make_async_remote_copy(src_ref=src, dst_ref=dst, send_sem=send_sems.at[k], recv_sem=recv_sems.at[k],
                                        device_id=to, device_id_type=MESH)


def _gather_plan_first(src_refs, land_refs, send_sems, recv_sems):
    x, y, c = _place()
    me = 4 * x + 2 * y + c
    peers = [(x, y, 1 - c), (1 - x, y, c), (x, 1 - y, c), (1 - x, 1 - y, c)]
    return [_remote(src, land.at[me], send_sems, recv_sems, 4 * a + k, to)
            for a, (src, land) in enumerate(zip(src_refs, land_refs)) for k, to in enumerate(peers)]


def _gather_plan_second(src_refs, land_refs, send_sems, recv_sems):
    x, y, c = _place()
    chips = [(1 - x, y), (x, 1 - y), (1 - x, 1 - y)]
    out = []
    for a, land in enumerate(land_refs):
        for j, (px, py) in enumerate(chips):
            slot = land.at[4 * px + 2 * py + c]
            out.append(_remote(slot, slot, send_sems, recv_sems, 3 * a + j, (x, y, 1 - c)))
    return out


def _siblings_plan(src_refs, land_refs, send_sems, recv_sems):
    x, y, c = _place()
    return [_remote(src.at[:, 1 - c], land, send_sems, recv_sems, a, (x, y, 1 - c))
            for a, (src, land) in enumerate(zip(src_refs, land_refs))]


def _chips_plan(src_refs, land_refs, send_sems, recv_sems):
    x, y, c = _place()
    my_chip = 2 * x + y
    chips = [(1 - x, y), (x, 1 - y), (1 - x, 1 - y)]
    return [_remote(src.at[2 * px + py], land.at[my_chip], send_sems, recv_sems, 3 * a + j, (px, py, c))
            for a, (src, land) in enumerate(zip(src_refs, land_refs)) for j, (px, py) in enumerate(chips)]


def _gather_landings(shards, me, *, name):
    blank = _unwritten([jax.ShapeDtypeStruct((N_DEV, *s.shape), s.dtype) for s in shards], name=name)
    return [lax.dynamic_update_index_in_dim(b, s, me, 0) for b, s in zip(blank, shards)]


def _adamw_math(g, w, m, v):
    m = ADAM_B1 * m + (1.0 - ADAM_B1) * g
    v = ADAM_B2 * v + (1.0 - ADAM_B2) * (g * g)
    m_hat = m / (1.0 - ADAM_B1 ** ADAM_STEP)
    v_hat = v / (1.0 - ADAM_B2 ** ADAM_STEP)
    delta = -ADAM_LR * (m_hat / (jnp.sqrt(v_hat) + ADAM_EPS) + ADAM_WD * w)
    return delta, m, v


def _adamw_sharded(parts, w, m, v, *, name, dep=None):
    n_layers, r, c = w.shape
    n_chip = parts[0].shape[0]
    tr = _row_tile(r, 384)
    n_i = r // tr

    def body(*refs):
        p_refs = refs[0:n_layers]
        w_ref, m_ref, v_ref, _, g_out, d_out, m_out, v_out = refs[n_layers:]
        layer = pl.program_id(0)
        for l in range(n_layers):
            @pl.when(layer == l)
            def _(l=l):
                g = p_refs[l][0].astype(F32)
                for q in range(1, n_chip):
                    g = g + p_refs[l][q].astype(F32)
                delta, m_new, v_new = _adamw_math(g, w_ref[...], m_ref[...], v_ref[...])
                g_out[...] = g
                d_out[...] = delta
                m_out[...] = m_new
                v_out[...] = v_new

    def part_map(l):
        return lambda layer, i: (0, jnp.where(layer == l, i, jnp.where(layer < l, 0, n_i - 1)), 0)

    blk = pl.BlockSpec((None, tr, c), lambda layer, i: (layer, i, 0))
    return pl.pallas_call(
        body, grid=(n_layers, n_i),
        in_specs=[pl.BlockSpec((n_chip, tr, c), part_map(l)) for l in range(n_layers)] + [blk, blk, blk, ANY],
        out_specs=[blk] * 4, out_shape=[jax.ShapeDtypeStruct((n_layers, r, c), F32)] * 4,
        compiler_params=_params("arbitrary", "arbitrary"), name=name,
    )(*parts, w, m, v, w if dep is None else dep)


def _fold_partials(cols, *, name):
    widths = [c.shape[1] for c in cols]

    def body(*refs):
        o_ref = refs[-1]
        pos = 0
        for ref, width in zip(refs[:-1], widths):
            o_ref[:, pos:pos + width] = jnp.sum(ref[...], axis=0, keepdims=True)
            pos += width

    return pl.pallas_call(body, out_shape=jax.ShapeDtypeStruct((1, sum(widths)), F32), name=name)(*cols)


def _adamw_replicated(parts, names, w, m, v, n_loss, *, name):
    n_dev = parts.shape[0]
    n_layers = w[names[0]].shape[0]
    every = list(names) + ["final_norm_g"]
    n_p = len(every)

    def body(*refs):
        p_ref = refs[0]
        w_refs = dict(zip(every, refs[1:1 + n_p]))
        m_refs = dict(zip(every, refs[1 + n_p:1 + 2 * n_p]))
        v_refs = dict(zip(every, refs[1 + 2 * n_p:1 + 3 * n_p]))
        l_out = refs[1 + 3 * n_p]
        outs = refs[2 + 3 * n_p:]
        o_refs = {n: outs[4 * q:4 * q + 4] for q, n in enumerate(every)}
        acc = p_ref[0]
        for q in range(1, n_dev):
            acc = acc + p_ref[q]
        tot = jnp.sum(acc, axis=0, keepdims=True)
        pos = 0
        where = [(n, l) for l in range(n_layers) for n in names] + [("final_norm_g", 0)]
        for n, l in where:
            width = w_refs[n].shape[1]
            g = tot[:, pos:pos + width]
            pos += width
            row = pl.ds(l, 1)
            delta, m_new, v_new = _adamw_math(g, w_refs[n][row, :], m_refs[n][row, :], v_refs[n][row, :])
            for o, val in zip(o_refs[n], (g, delta, m_new, v_new)):
                o[row, :] = val
        l_out[...] = (0.5 / n_loss) * jnp.sum(tot[:, pos:pos + n_loss], axis=-1, keepdims=True)

    shapes = [jax.ShapeDtypeStruct((1, 1), F32)]
    for n in every:
        shapes += [jax.ShapeDtypeStruct(w[n].shape, F32)] * 4
    res = pl.pallas_call(
        body, out_shape=shapes,
        compiler_params=pltpu.CompilerParams(vmem_limit_bytes=VMEM_LIMIT), name=name,
    )(parts, *[w[n] for n in every], *[m[n] for n in every], *[v[n] for n in every])
    return res[0], {n: res[1 + 4 * q:5 + 4 * q] for q, n in enumerate(every)}


BIG = ("w_in", "w_out", "w_up", "w_down")
COL_SHARDED = ("w_in", "w_up")
CONV = ("conv_a_w", "conv_b_w", "conv_f_w")
REPLICATED = ("mix_norm_g", "b_in", "conv_a_b", "ln_a_g", "ln_a_b", "ffn_norm_g")
KINDS = ("grad", "delta", "m", "v")
FFN_PART = ("w_up", "w_down")
MIX_PART = ("w_in", "w_out")


def _weights_from_gathered(g):
    n_dev, r, c = g.shape
    return g.reshape(n_dev * r, c)


def _slabs_from_full(grad):
    return grad.reshape(N_DEV, grad.shape[0] // N_DEV, grad.shape[1])


def _unwritten(like, *, name):
    return pl.pallas_call(lambda *refs: None, out_specs=[ANY] * len(like), out_shape=list(like), name=name)()


def kernel(x, mix_norm_g, w_in, b_in, conv_a_w, conv_a_b, ln_a_g, ln_a_b, conv_b_w, w_out, ffn_norm_g, w_up, conv_f_w, w_down, final_norm_g, loss_target, m_mix_norm_g, m_w_in, m_b_in, m_conv_a_w, m_conv_a_b, m_ln_a_g, m_ln_a_b, m_conv_b_w, m_w_out, m_ffn_norm_g, m_w_up, m_conv_f_w, m_w_down, m_final_norm_g, v_mix_norm_g, v_w_in, v_b_in, v_conv_a_w, v_conv_a_b, v_ln_a_g, v_ln_a_b, v_conv_b_w, v_w_out, v_ffn_norm_g, v_w_up, v_conv_f_w, v_w_down, v_final_norm_g):
    w = dict(mix_norm_g=mix_norm_g, w_in=w_in, b_in=b_in, conv_a_w=conv_a_w, conv_a_b=conv_a_b, ln_a_g=ln_a_g,
             ln_a_b=ln_a_b, conv_b_w=conv_b_w, w_out=w_out, ffn_norm_g=ffn_norm_g, w_up=w_up, conv_f_w=conv_f_w,
             w_down=w_down, final_norm_g=final_norm_g)
    m = dict(mix_norm_g=m_mix_norm_g, w_in=m_w_in, b_in=m_b_in, conv_a_w=m_conv_a_w, conv_a_b=m_conv_a_b,
             ln_a_g=m_ln_a_g, ln_a_b=m_ln_a_b, conv_b_w=m_conv_b_w, w_out=m_w_out, ffn_norm_g=m_ffn_norm_g,
             w_up=m_w_up, conv_f_w=m_conv_f_w, w_down=m_w_down, final_norm_g=m_final_norm_g)
    v = dict(mix_norm_g=v_mix_norm_g, w_in=v_w_in, b_in=v_b_in, conv_a_w=v_conv_a_w, conv_a_b=v_conv_a_b,
             ln_a_g=v_ln_a_g, ln_a_b=v_ln_a_b, conv_b_w=v_conv_b_w, w_out=v_w_out, ffn_norm_g=v_ffn_norm_g,
             w_up=v_w_up, conv_f_w=v_conv_f_w, w_down=v_w_down, final_norm_g=v_final_norm_g)
    order = list(w)
    n_layers = w_in.shape[0]
    xs = x[0]
    target = loss_target[0]
    flip = lambda a: jnp.transpose(a, (0, 2, 1))
    wt, mt, vt = ({n: flip(d[n]) if n in COL_SHARDED else d[n] for n in BIG} for d in (w, m, v))
    px, py, pc = _place()
    where = jnp.stack([pc, 2 * px + py]).astype(jnp.int32)
    me = 4 * px + 2 * py + pc

    assert BIG == MIX_PART + FFN_PART
    key = lambda n: n + "_t" if n in COL_SHARDED else n
    shard = lambda n, l: wt[n][l].astype(BF16)

    def gather_start(names, l, after, tag):
        shards = [shard(n, l) for n in names]
        lands = _gather_landings(shards, me, name=f"gather_landing_{tag}")
        return _split_start(shards, lands, _gather_plan_first, 4 * len(shards), after, name=f"gather_first_start_{tag}")

    def gather_mid(first, after, tag):
        return _split_wait(first[0], first[1], first[2], first[3], after, _gather_plan_first,
                           name=f"gather_first_wait_{tag}")[1]

    def forward_start(lands, after, tag):
        return _split_start([], lands, _gather_plan_second, 3 * len(lands), after, name=f"gather_second_start_{tag}")

    def forward_finish(second, after, tag):
        return _split_wait(second[0], second[1], [], second[3], after, _gather_plan_second,
                           name=f"gather_second_wait_{tag}")[1]

    gathered = _all_gather([shard(n, 0) for n in MIX_PART] + [w[n] for n in CONV], name="gather_weights_0")
    params = [{n: w[n][l] for n in REPLICATED} for l in range(n_layers)]
    for n, g in zip(CONV, gathered[len(MIX_PART):]):
        n_dev, _, taps, c = g.shape
        full = g.transpose(1, 2, 0, 3).reshape(n_layers, taps, n_dev * c)
        for l in range(n_layers):
            params[l][n] = full[l]
    for n, g in zip(MIX_PART, gathered):
        params[0][key(n)] = _weights_from_gathered(g)
    ffn_first = gather_start(FFN_PART, 0, gathered[0], "0_ffn")
    pending = {}

    h = xs
    saved = []
    for l in range(n_layers):
        nxt = l + 1 if l + 1 < n_layers else None

        def before_up(x1, l=l, nxt=nxt):
            if l == 0:
                second = forward_start(gather_mid(ffn_first, x1, "0_ffn"), x1, "0_ffn")
                after = second[4]
            else:
                second = pending[l]["ffn"]
                after = x1
            if nxt is not None:
                pending[nxt] = dict(first=gather_start(BIG, nxt, after, str(nxt)))
                after = pending[nxt]["first"][4]
            for n, g in zip(FFN_PART, forward_finish(second, after, f"{l}_ffn")):
                params[l][key(n)] = _weights_from_gathered(g)

        h, keep = _layer_fwd(h, params[l], str(l), dep=ffn_first[4] if l == 0 else None, before_up=before_up)
        saved.append(keep)
        if nxt is not None:
            arrived = gather_mid(pending[nxt]["first"], h, str(nxt))
            mix_second = forward_start(arrived[:len(MIX_PART)], h, f"{nxt}_mix")
            pending[nxt]["ffn"] = forward_start(arrived[len(MIX_PART):], mix_second[4], f"{nxt}_ffn")
            for n, g in zip(MIX_PART, forward_finish(mix_second, pending[nxt]["ffn"][4], f"{nxt}_mix")):
                params[nxt][key(n)] = _weights_from_gathered(g)

    def start_siblings(slabs, after, tag):
        mines = [s.reshape(N_CHIP, 2, *s.shape[1:]) for s in slabs]
        lands = _unwritten([jax.ShapeDtypeStruct((N_CHIP, *m.shape[2:]), m.dtype) for m in mines],
                           name=f"reduce_siblings_landing_{tag}")
        return _split_start(mines, lands, _siblings_plan, len(mines), after, name=f"reduce_siblings_start_{tag}")

    def start_chips(sib, after, tag):
        mines, theirs = _split_wait(sib[0], sib[1], sib[2], sib[3], after, _siblings_plan,
                                    name=f"reduce_siblings_wait_{tag}")
        pairs, lands = _pair_sum(mines, theirs, where, name=f"pair_sum_{tag}")
        return _split_start(pairs, lands, _chips_plan, 3 * len(pairs), after, name=f"reduce_chips_start_{tag}")

    def finish_reduce(fly, after, tag):
        return _split_wait(fly[0], fly[1], fly[2], fly[3], after, _chips_plan, name=f"reduce_chips_wait_{tag}")[1]

    loss_sq, dh, dh_b, dgf = _loss_bwd(h, _row(final_norm_g), target, name="loss")
    conv_g = {n: [None] * n_layers for n in CONV}
    rep_g = [None] * n_layers
    siblings = {}
    flights = {}
    token = None
    for l in reversed(range(n_layers)):
        def ffn_grads(g, after, l=l):
            siblings[l, "ffn"] = start_siblings([_slabs_from_full(g[n]) for n in FFN_PART], after, f"{l}_ffn")
            return siblings[l, "ffn"][4]

        def ffn_sent(after, l=l):
            flights[l, "ffn"] = start_chips(siblings[l, "ffn"], after, f"{l}_ffn")
            return flights[l, "ffn"][4]

        def mix_grads(g, conv, after, l=l):
            for n in CONV:
                conv_g[n][l] = conv[n]
            slabs = [_slabs_from_full(g[n]) for n in MIX_PART]
            if l == 0:
                for n in CONV:
                    full = jnp.stack(conv_g[n])
                    _, taps, c = full.shape
                    slabs.append(full.reshape(n_layers, taps, N_DEV, c // N_DEV).transpose(2, 0, 1, 3)
                                 .reshape(N_DEV, n_layers * taps, c // N_DEV))
            siblings[l, "mix"] = start_siblings(slabs, after, f"{l}_mix")
            return siblings[l, "mix"][4]

        def mix_sent(after, l=l):
            flights[l, "mix"] = start_chips(siblings[l, "mix"], after, f"{l}_mix")
            return flights[l, "mix"][4]

        dh, dh_b, rep_g[l], token = _layer_bwd(dh, dh_b, params[l], saved[l], str(l), ffn_grads, ffn_sent,
                                               mix_grads, mix_sent, dep=token)

    sums = {key: finish_reduce(fly, dh, f"{key[0]}_{key[1]}") for key, fly in flights.items() if key != (0, "mix")}
    out = {k: {} for k in KINDS}

    def adamw_big(names, part, dep):
        for q, n in enumerate(names):
            layer_parts = [sums[l, part][q] for l in range(n_layers)]
            res = _adamw_sharded(layer_parts, wt[n], mt[n], vt[n], name=f"adamw_{n}", dep=dep)
            for k, r in zip(KINDS, res):
                out[k][n] = flip(r) if n in COL_SHARDED else r

    adamw_big(FFN_PART, "ffn", token)

    rep_cols = [rep_g[l][n] for l in range(n_layers) for n in REPLICATED] + [dgf, loss_sq]
    rep_all = _all_gather([_fold_partials(rep_cols, name="fold_small")], name="gather_small")[0]
    with_final = lambda d: {**{n: d[n] for n in REPLICATED}, "final_norm_g": _row(d["final_norm_g"])}
    loss, rep_res = _adamw_replicated(rep_all, REPLICATED, with_final(w), with_final(m), with_final(v),
                                      loss_sq.shape[1], name="adamw_small")
    for n, res in rep_res.items():
        for k, r in zip(KINDS, res):
            out[k][n] = r.reshape(w[n].shape)

    last = finish_reduce(flights[0, "mix"], rep_res["b_in"][0], "0_mix")
    sums[0, "mix"] = last[:len(MIX_PART)]
    adamw_big(MIX_PART, "mix", None)
    for n, p in zip(CONV, last[len(MIX_PART):]):
        as_one = lambda a: a.reshape(1, *p.shape[1:])
        for k, r in zip(KINDS, _adamw_sharded([p], as_one(w[n]), as_one(m[n]), as_one(v[n]), name=f"adamw_{n}")):
            out[k][n] = r.reshape(w[n].shape)

    grad_x = dh.reshape(x.shape)
    return (loss.reshape(()), grad_x, *[out["grad"][n] for n in order], *[out["delta"][n] for n in order],
            *[out["m"][n] for n in order], *[out["v"][n] for n in order])
```

```python
import functools

import jax
import jax.numpy as jnp
from jax import lax
from jax.experimental import pallas as pl
from jax.experimental.pallas import tpu as pltpu

F32 = jnp.float32
BF16 = jnp.bfloat16

N_DEV = 8
N_CHIP = 4
D_CONF = 512
CONF_K = 31
SHORT_K = 3
EPS = 1e-6
HALO = 32
HALO3 = 8
HALO3_BLK = 16
LANES = 128
SUB = 8
VMEM_LIMIT = 56 * 1024 * 1024

ADAM_LR = 0.001
ADAM_B1 = 0.9
ADAM_B2 = 0.999
ADAM_EPS = 1e-08
ADAM_WD = 0.01
ADAM_STEP = 10

MESH = pl.DeviceIdType.MESH
ANY = pl.BlockSpec(memory_space=pl.ANY)


def _params(*sem):
    return pltpu.CompilerParams(dimension_semantics=sem, vmem_limit_bytes=VMEM_LIMIT)


def _resident(shape, index_map):
    return pl.BlockSpec(shape, index_map, pipeline_mode=pl.Buffered(1))


def _row_loop(n_rows, rb, fn, unroll=1):
    rb = min(rb, n_rows)

    def body(i, carry):
        fn(pl.ds(pl.multiple_of(i * rb, rb), rb))
        return carry
    lax.fori_loop(0, n_rows // rb, body, 0, unroll=unroll)


def _rows8(v):
    acc = v[0:SUB]
    for k in range(1, v.shape[0] // SUB):
        acc = acc + v[k * SUB:(k + 1) * SUB]
    return acc


def _sigmoid(z):
    return 0.5 * jnp.tanh(0.5 * z) + 0.5


def _dot(a, b):
    return jnp.dot(a, b, preferred_element_type=F32)


def _dot_nt(a, b):
    return lax.dot_general(a, b, (((1,), (1,)), ((), ())), preferred_element_type=F32)


def _dot_tn(a, b):
    return lax.dot_general(a, b, (((0,), (0,)), ((), ())), preferred_element_type=F32)


def _replicate_taps(w_ref, wrep, taps):
    for k in range(taps):
        wrep[pl.ds(k * SUB, SUB), :] = jnp.broadcast_to(w_ref[pl.ds(k, 1), :], (SUB, w_ref.shape[1]))


def _shift_copies(win, shf, lanes):
    span = win.shape[0] - SUB
    for r in range(1, SUB):
        for j0 in range(0, span, 64):
            n = min(64, span - j0)
            shf[r - 1, pl.ds(j0, n), lanes] = win[pl.ds(j0 + r, n), lanes]


def _rows_at(win, shf, off, rb, lanes):
    if shf is None or off % SUB == 0:
        return win[pl.ds(off, rb), lanes]
    return shf[off % SUB - 1, pl.ds(off - off % SUB, rb), lanes]


def _conv_taps(win, wrep, out, *, taps, n_rows, base, width, transposed=False, bias_ref=None, shf=None):
    rb = min(64, n_rows)

    def lane_body(cb, carry):
        lanes = pl.ds(pl.multiple_of(cb * LANES, LANES), LANES)
        if shf is not None:
            _shift_copies(win, shf, lanes)
        for r0 in range(0, n_rows, rb):
            acc = None
            for k in range(taps):
                off = (taps - 1 - k) if transposed else (k - (taps - 1))
                wk = jnp.tile(wrep[pl.ds(k * SUB, SUB), lanes], (rb // SUB, 1))
                term = wk * _rows_at(win, shf, base + r0 + off, rb, lanes)
                acc = term if acc is None else acc + term
            if bias_ref is not None:
                acc = acc + bias_ref[:, lanes]
            out[pl.ds(r0, rb), lanes] = acc.astype(out.dtype)
        return carry

    lax.fori_loop(0, width // LANES, lane_body, 0)


def _conv_bwd_taps(win, wrep, x_cur, dx_out, dw_acc, *, taps, n_rows, width, shf=None):
    rb = min(32 if taps > 8 else 64, n_rows)

    def lane_body(cb, carry):
        lanes = pl.ds(pl.multiple_of(cb * LANES, LANES), LANES)
        if shf is not None:
            _shift_copies(win, shf, lanes)
        sums = [None] * taps
        for r0 in range(0, n_rows, rb):
            xv = x_cur[pl.ds(r0, rb), lanes].astype(F32)
            acc = None
            for k in range(taps):
                shifted = _rows_at(win, shf, r0 + taps - 1 - k, rb, lanes)
                term = jnp.tile(wrep[pl.ds(k * SUB, SUB), lanes], (rb // SUB, 1)) * shifted
                acc = term if acc is None else acc + term
                part = _rows8(xv * shifted)
                sums[k] = part if sums[k] is None else sums[k] + part
            dx_out[pl.ds(r0, rb), lanes] = acc.astype(dx_out.dtype)
        for k in range(taps):
            dw_acc[pl.ds(k * SUB, SUB), lanes] += sums[k]
        return carry

    lax.fori_loop(0, width // LANES, lane_body, 0)


def _fold8(acc_ref, taps):
    return jnp.concatenate(
        [jnp.sum(acc_ref[pl.ds(k * SUB, SUB), :], axis=0, keepdims=True) for k in range(taps)], axis=0)


def _seq_tile(s_len):
    return min(512, s_len)


def _mm_tile(s_len):
    return min(512, s_len)


def _ff_chunk(ff):
    best = LANES
    for c in range(LANES, 1408 + 1, LANES):
        if ff % c == 0:
            best = c
    return best


def _col_tile(n):
    for c in (512, 1408, 256, LANES):
        if n % c == 0:
            return c
    return n


def _rms_matmul(x, g, wt, b, *, name, dep=None):
    s_len, d = x.shape
    n = wt.shape[0]
    tm = _mm_tile(s_len)
    cn = _col_tile(n)
    has_bias = b is not None

    def body(*refs):
        x_ref, g_ref, w_ref = refs[0:3]
        b_ref = refs[3] if has_bias else None
        o_ref, h_ref = refs[-2:]

        def blk(rows):
            xv = x_ref[rows, :]
            r = lax.rsqrt(jnp.mean(xv * xv, axis=-1, keepdims=True) + EPS)
            h_ref[rows, :] = ((xv * r) * g_ref[...]).astype(BF16)

        rb = min(128, tm)
        for r0 in range(0, tm, rb):
            blk(pl.ds(r0, rb))
        for j in range(n // cn):
            acc = _dot_nt(h_ref[...], w_ref[j * cn:(j + 1) * cn, :])
            if has_bias:
                acc = acc + b_ref[:, j * cn:(j + 1) * cn]
            o_ref[:, j * cn:(j + 1) * cn] = acc.astype(BF16)

    in_specs = [pl.BlockSpec((tm, d), lambda i: (i, 0)), _resident((1, d), lambda i: (0, 0)),
                _resident((n, d), lambda i: (0, 0))]
    args = [x, g, wt]
    if has_bias:
        in_specs.append(_resident((1, n), lambda i: (0, 0)))
        args.append(b)
    in_specs.append(ANY)
    args.append(x if dep is None else dep)
    return pl.pallas_call(
        body, grid=(s_len // tm,), in_specs=in_specs,
        out_specs=[pl.BlockSpec((tm, n), lambda i: (i, 0)), pl.BlockSpec((tm, d), lambda i: (i, 0))],
        out_shape=[jax.ShapeDtypeStruct((s_len, n), BF16), jax.ShapeDtypeStruct((s_len, d), BF16)],
        compiler_params=_params("parallel"), name=name,
    )(*args)


def _mix_windows(u_ref, uh_ref, gw, pw, first, t):
    c = D_CONF
    uh = uh_ref[...].astype(F32)
    gw[0:HALO, :] = jnp.where(first, 0.0, uh[:, 0:c] * _sigmoid(uh[:, c:2 * c]))
    pw[0:HALO3, :] = jnp.where(first, 0.0, uh[HALO - HALO3:HALO, 3 * c:4 * c] * uh[HALO - HALO3:HALO, 4 * c:5 * c])

    def blk(rows):
        dst = pl.ds(pl.multiple_of(rows.start + HALO, SUB), rows.size)
        gw[dst, :] = u_ref[rows, 0:c].astype(F32) * _sigmoid(u_ref[rows, c:2 * c].astype(F32))
        dst3 = pl.ds(pl.multiple_of(rows.start + HALO3, SUB), rows.size)
        pw[dst3, :] = u_ref[rows, 3 * c:4 * c].astype(F32) * u_ref[rows, 4 * c:5 * c].astype(F32)
    _row_loop(t, 64, blk)


def _mix_fwd(u, x0, wa, ba, lg, lb, wb, w_out, *, name):
    s_len, d_in = u.shape
    d = x0.shape[1]
    c = D_CONF
    t = _seq_tile(s_len)
    per = t // HALO

    def body(u_ref, uh_ref, x0_ref, wa_ref, ba_ref, lg_ref, lb_ref, wb_ref, wo_ref, y_ref, x1_ref, ca, cb,
             gw, pw, wrep_a, wrep_b, shf):
        first = pl.program_id(0) == 0
        _mix_windows(u_ref, uh_ref, gw, pw, first, t)
        _replicate_taps(wa_ref, wrep_a, CONF_K)
        _replicate_taps(wb_ref, wrep_b, SHORT_K)
        _conv_taps(gw, wrep_a, ca, taps=CONF_K, n_rows=t, base=HALO, width=c, bias_ref=ba_ref, shf=shf)
        _conv_taps(pw, wrep_b, cb, taps=SHORT_K, n_rows=t, base=HALO3, width=c)

        def blk(rows):
            cv = ca[rows, :]
            mu = jnp.mean(cv, axis=-1, keepdims=True)
            xc = cv - mu
            var = jnp.mean(xc * xc, axis=-1, keepdims=True)
            ln = (xc * lax.rsqrt(var + EPS)) * lg_ref[...] + lb_ref[...]
            y_ref[rows, 0:c] = (ln * _sigmoid(ln)).astype(BF16)
            y_ref[rows, c:2 * c] = (u_ref[rows, 2 * c:3 * c].astype(F32) * cb[rows, :]).astype(BF16)
        half = t // 2
        rb = min(64, half)
        for lo in range(0, t, half):
            for r0 in range(lo, lo + half, rb):
                blk(pl.ds(r0, rb))
            x1_ref[lo:lo + half, :] = x0_ref[lo:lo + half, :] + _dot(y_ref[lo:lo + half, :], wo_ref[...])

    small = lambda r: _resident((r, c), lambda i: (0, 0))
    return pl.pallas_call(
        body, grid=(s_len // t,),
        in_specs=[pl.BlockSpec((t, d_in), lambda i: (i, 0)),
                  pl.BlockSpec((HALO, d_in), lambda i: (jnp.maximum(i * per - 1, 0), 0)),
                  pl.BlockSpec((t, d), lambda i: (i, 0)),
                  small(CONF_K), small(1), small(1), small(1), small(SHORT_K),
                  _resident((2 * c, d), lambda i: (0, 0))],
        out_specs=[pl.BlockSpec((t, 2 * c), lambda i: (i, 0)), pl.BlockSpec((t, d), lambda i: (i, 0)),
                   pl.BlockSpec((t, c), lambda i: (i, 0)), pl.BlockSpec((t, c), lambda i: (i, 0))],
        out_shape=[jax.ShapeDtypeStruct((s_len, 2 * c), BF16), jax.ShapeDtypeStruct((s_len, d), F32),
                   jax.ShapeDtypeStruct((s_len, c), F32), jax.ShapeDtypeStruct((s_len, c), F32)],
        scratch_shapes=[pltpu.VMEM((HALO + t, c), F32), pltpu.VMEM((HALO3 + t, c), F32),
                        pltpu.VMEM((CONF_K * SUB, c), F32), pltpu.VMEM((SHORT_K * SUB, c), F32),
                        pltpu.VMEM((SUB - 1, HALO + t, c), F32)],
        compiler_params=_params("arbitrary"), name=name,
    )(u, u, x0, wa, ba, lg, lb, wb, w_out)


def _ffn_fwd(uf, x1, wf, w_down, *, name):
    s_len, ff2 = uf.shape
    ff = ff2 // 2
    d = x1.shape[1]
    t = _seq_tile(s_len)
    fc = _ff_chunk(ff)
    nc = ff // fc
    per = t // HALO3_BLK
    half = t // 2
    rb = min(64, half)

    def body(ug_ref, ugh_ref, uv_ref, uvh_ref, x1_ref, wfg_ref, wfv_ref, wd_ref,
             act_ref, x2_ref, cg_ref, cv_ref, gwin, vwin, wrep_g, wrep_v):
        first = pl.program_id(0) == 0
        first_chunk = pl.program_id(1) == 0
        lo8 = HALO3_BLK - HALO3
        gwin[0:HALO3, :] = jnp.where(first, 0.0, ugh_ref[...].astype(F32)[lo8:HALO3_BLK])
        vwin[0:HALO3, :] = jnp.where(first, 0.0, uvh_ref[...].astype(F32)[lo8:HALO3_BLK])
        _replicate_taps(wfg_ref, wrep_g, SHORT_K)
        _replicate_taps(wfv_ref, wrep_v, SHORT_K)
        chunk_rows = pl.ds(pl.multiple_of(pl.program_id(1) * fc, fc), fc)

        def conv(win, wrep, r0, lanes):
            acc = None
            for k in range(SHORT_K):
                wk = jnp.tile(wrep[k * SUB:(k + 1) * SUB, lanes], (rb // SUB, 1))
                off = HALO3 + r0 + k - (SHORT_K - 1)
                term = wk * win[off:off + rb, lanes]
                acc = term if acc is None else acc + term
            return acc

        for lo in range(0, t, half):
            for r0 in range(lo, lo + half, rb):
                gwin[HALO3 + r0:HALO3 + r0 + rb, :] = ug_ref[r0:r0 + rb, :].astype(F32)
                vwin[HALO3 + r0:HALO3 + r0 + rb, :] = uv_ref[r0:r0 + rb, :].astype(F32)
            for cb in range(fc // LANES):
                lanes = slice(cb * LANES, (cb + 1) * LANES)
                for r0 in range(lo, lo + half, rb):
                    gv = conv(gwin, wrep_g, r0, lanes).astype(BF16)
                    vv = conv(vwin, wrep_v, r0, lanes).astype(BF16)
                    cg_ref[r0:r0 + rb, lanes] = gv
                    cv_ref[r0:r0 + rb, lanes] = vv
                    act_ref[r0:r0 + rb, lanes] = (gv * _sigmoid(gv)) * vv
            base = jnp.where(first_chunk, x1_ref[lo:lo + half, :], x2_ref[lo:lo + half, :])
            x2_ref[lo:lo + half, :] = base + _dot(act_ref[lo:lo + half, :], wd_ref[chunk_rows, :])

    halo_map = lambda off: (lambda i, j: (jnp.maximum(i * per - 1, 0), j + off))
    return pl.pallas_call(
        body, grid=(s_len // t, nc),
        in_specs=[pl.BlockSpec((t, fc), lambda i, j: (i, j)), pl.BlockSpec((HALO3_BLK, fc), halo_map(0)),
                  pl.BlockSpec((t, fc), lambda i, j: (i, j + nc)), pl.BlockSpec((HALO3_BLK, fc), halo_map(nc)),
                  pl.BlockSpec((t, d), lambda i, j: (i, 0)),
                  pl.BlockSpec((SHORT_K, fc), lambda i, j: (0, j)),
                  pl.BlockSpec((SHORT_K, fc), lambda i, j: (0, j + nc)),
                  _resident((ff, d), lambda i, j: (0, 0))],
        out_specs=[pl.BlockSpec((t, fc), lambda i, j: (i, j)), pl.BlockSpec((t, d), lambda i, j: (i, 0)),
                   pl.BlockSpec((t, fc), lambda i, j: (i, j)), pl.BlockSpec((t, fc), lambda i, j: (i, j))],
        out_shape=[jax.ShapeDtypeStruct((s_len, ff), BF16), jax.ShapeDtypeStruct((s_len, d), F32),
                   jax.ShapeDtypeStruct((s_len, ff), BF16), jax.ShapeDtypeStruct((s_len, ff), BF16)],
        scratch_shapes=[pltpu.VMEM((HALO3 + t, fc), F32), pltpu.VMEM((HALO3 + t, fc), F32),
                        pltpu.VMEM((SHORT_K * SUB, fc), F32), pltpu.VMEM((SHORT_K * SUB, fc), F32)],
        compiler_params=_params("parallel", "arbitrary"), name=name,
    )(uf, uf, uf, uf, x1, wf, wf, w_down)


def _loss_bwd(x, g, target, *, name):
    s_len, d = x.shape
    t = _seq_tile(s_len)

    def body(x_ref, g_ref, t_ref, l_ref, dx_ref, dxb_ref, dg_ref):
        @pl.when(pl.program_id(0) == 0)
        def _():
            l_ref[...] = jnp.zeros_like(l_ref)
            dg_ref[...] = jnp.zeros_like(dg_ref)

        def blk(rows):
            xv = x_ref[rows, :]
            r = lax.rsqrt(jnp.mean(xv * xv, axis=-1, keepdims=True) + EPS)
            xn = xv * r
            e = xn * g_ref[...] - t_ref[rows, :]
            l_ref[...] += _rows8(e * e)
            dy = e * (1.0 / d)
            dg_ref[...] += _rows8(dy * xn)
            dn = dy * g_ref[...]
            dx = r * (dn - xn * jnp.mean(dn * xn, axis=-1, keepdims=True))
            dx_ref[rows, :] = dx
            dxb_ref[rows, :] = dx.astype(BF16)
        _row_loop(t, 64, blk)

    row = pl.BlockSpec((t, d), lambda i: (i, 0))
    part = pl.BlockSpec((SUB, d), lambda i: (0, 0))
    return pl.pallas_call(
        body, grid=(s_len // t,),
        in_specs=[row, _resident((1, d), lambda i: (0, 0)), row],
        out_specs=[part, row, row, part],
        out_shape=[jax.ShapeDtypeStruct((SUB, d), F32), jax.ShapeDtypeStruct((s_len, d), F32),
                   jax.ShapeDtypeStruct((s_len, d), BF16), jax.ShapeDtypeStruct((SUB, d), F32)],
        compiler_params=_params("arbitrary"), name=name,
    )(x, g, target)


def _ffn_bwd(dx2, uf, cg, cv, wf, w_down, *, name, dep=None):
    s_len, ff2 = uf.shape
    ff = ff2 // 2
    d = dx2.shape[1]
    t = _seq_tile(s_len)
    n_t = s_len // t
    fc = _ff_chunk(ff)
    nc = ff // fc

    def body(dx_ref, ug_ref, uv_ref, cg_ref, cv_ref, wfg_ref, wfv_ref, wd_ref, dep_ref,
             duf_ref, dwg_ref, dwv_ref, dact, dgw, dvw, awg, awv, wrep_g, wrep_v):
        i = pl.program_id(1)

        @pl.when(i == 0)
        def _():
            dgw[t:t + HALO3, :] = jnp.zeros((HALO3, fc), F32)
            dvw[t:t + HALO3, :] = jnp.zeros((HALO3, fc), F32)
            awg[...] = jnp.zeros_like(awg)
            awv[...] = jnp.zeros_like(awv)

        _replicate_taps(wfg_ref, wrep_g, SHORT_K)
        _replicate_taps(wfv_ref, wrep_v, SHORT_K)

        def blk(rows):
            gv = cg_ref[rows, :]
            sg = _sigmoid(gv)
            da = dact[rows, :].astype(BF16)
            dgw[rows, :] = ((da * cv_ref[rows, :]) * (sg * (1.0 + gv * (1.0 - sg)))).astype(F32)
            dvw[rows, :] = (da * (gv * sg)).astype(F32)

        dact[...] = _dot_nt(dx_ref[...], wd_ref[...])
        _row_loop(t, 64, blk)

        _conv_bwd_taps(dgw, wrep_g, ug_ref, duf_ref.at[0], awg, taps=SHORT_K, n_rows=t, width=fc)
        _conv_bwd_taps(dvw, wrep_v, uv_ref, duf_ref.at[1], awv, taps=SHORT_K, n_rows=t, width=fc)
        dgw[t:t + HALO3, :] = dgw[0:HALO3, :]
        dvw[t:t + HALO3, :] = dvw[0:HALO3, :]

        @pl.when(i == n_t - 1)
        def _():
            dwg_ref[...] = _fold8(awg, SHORT_K)
            dwv_ref[...] = _fold8(awv, SHORT_K)

    rev = lambda i: n_t - 1 - i
    gate = pl.BlockSpec((t, fc), lambda j, i: (rev(i), j))
    value = pl.BlockSpec((t, fc), lambda j, i: (rev(i), j + nc))
    return pl.pallas_call(
        body, grid=(nc, n_t),
        in_specs=[pl.BlockSpec((t, d), lambda j, i: (rev(i), 0)), gate, value, gate, gate,
                  pl.BlockSpec((SHORT_K, fc), lambda j, i: (0, j)),
                  pl.BlockSpec((SHORT_K, fc), lambda j, i: (0, j + nc)),
                  pl.BlockSpec((fc, d), lambda j, i: (j, 0)), ANY],
        out_specs=[pl.BlockSpec((2, t, fc), lambda j, i: (0, rev(i), j)),
                   pl.BlockSpec((SHORT_K, fc), lambda j, i: (0, j)), pl.BlockSpec((SHORT_K, fc), lambda j, i: (0, j))],
        out_shape=[jax.ShapeDtypeStruct((2, s_len, ff), BF16),
                   jax.ShapeDtypeStruct((SHORT_K, ff), F32), jax.ShapeDtypeStruct((SHORT_K, ff), F32)],
        scratch_shapes=[pltpu.VMEM((t, fc), F32),
                        pltpu.VMEM((t + HALO3, fc), F32), pltpu.VMEM((t + HALO3, fc), F32),
                        pltpu.VMEM((SHORT_K * SUB, fc), F32), pltpu.VMEM((SHORT_K * SUB, fc), F32),
                        pltpu.VMEM((SHORT_K * SUB, fc), F32), pltpu.VMEM((SHORT_K * SUB, fc), F32)],
        compiler_params=_params("arbitrary", "arbitrary"), name=name,
    )(dx2, uf, uf, cg, cv, wf, wf, w_down, uf if dep is None else dep)


def _mix_bwd(dx1, u, ca, cb, wa, lg, lb, wb, w_out, *, name, dep=None):
    s_len, d_in = u.shape
    d = dx1.shape[1]
    c = D_CONF
    t = _seq_tile(s_len)
    n_t = s_len // t

    def body(dx_ref, u_ref, ca_ref, cb_ref, wa_ref, lg_ref, lb_ref, wb_ref, wo_ref, dep_ref,
             du_ref, dwa_ref, dwb_ref, dba_ref, dlg_ref, dlb_ref, dbin_ref,
             glu, prod, dyc, dcaw, dcbw, dglu, dp, awa, awb, wrep_a, wrep_b, shf):
        i = pl.program_id(0)
        _replicate_taps(wa_ref, wrep_a, CONF_K)
        _replicate_taps(wb_ref, wrep_b, SHORT_K)

        @pl.when(i == 0)
        def _():
            dcaw[t:t + HALO, :] = jnp.zeros((HALO, c), F32)
            dcbw[t:t + HALO3, :] = jnp.zeros((HALO3, c), F32)
            awa[...] = jnp.zeros_like(awa)
            awb[...] = jnp.zeros_like(awb)
            dba_ref[...] = jnp.zeros_like(dba_ref)
            dlg_ref[...] = jnp.zeros_like(dlg_ref)
            dlb_ref[...] = jnp.zeros_like(dlb_ref)
            dbin_ref[...] = jnp.zeros_like(dbin_ref)

        def blk1(rows):
            cv = ca_ref[rows, :]
            mu = jnp.mean(cv, axis=-1, keepdims=True)
            xc = cv - mu
            rstd = lax.rsqrt(jnp.mean(xc * xc, axis=-1, keepdims=True) + EPS)
            nrm = xc * rstd
            ln = nrm * lg_ref[...] + lb_ref[...]
            sg = _sigmoid(ln)
            dln = dyc[rows, 0:c] * (sg * (1.0 + ln * (1.0 - sg)))
            dlg_ref[...] += _rows8(dln * nrm)
            dlb_ref[...] += _rows8(dln)
            dn = dln * lg_ref[...]
            dca = rstd * (dn - jnp.mean(dn, axis=-1, keepdims=True)
                          - nrm * jnp.mean(dn * nrm, axis=-1, keepdims=True))
            dcaw[rows, :] = dca
            dba_ref[...] += _rows8(dca)
            ds = dyc[rows, c:2 * c]
            dgb = ds * cb_ref[rows, :]
            dcbw[rows, :] = ds * u_ref[rows, 2 * c:3 * c].astype(F32)
            du_ref[rows, 2 * c:3 * c] = dgb.astype(BF16)
            dbin_ref[:, 2 * c:3 * c] += _rows8(dgb)
            glu[rows, :] = u_ref[rows, 0:c].astype(F32) * _sigmoid(u_ref[rows, c:2 * c].astype(F32))
            prod[rows, :] = u_ref[rows, 3 * c:4 * c].astype(F32) * u_ref[rows, 4 * c:5 * c].astype(F32)
        half = t // 2
        rb = min(64, half)
        for lo in range(0, t, half):
            dyc[lo:lo + half, :] = _dot_nt(dx_ref[lo:lo + half, :], wo_ref[...])
            for r0 in range(lo, lo + half, rb):
                blk1(pl.ds(r0, rb))

        _conv_bwd_taps(dcaw, wrep_a, glu, dglu, awa, taps=CONF_K, n_rows=t, width=c, shf=shf)
        _conv_bwd_taps(dcbw, wrep_b, prod, dp, awb, taps=SHORT_K, n_rows=t, width=c)
        dcaw[t:t + HALO, :] = dcaw[0:HALO, :]
        dcbw[t:t + HALO3, :] = dcbw[0:HALO3, :]

        def blk2(rows):
            av = u_ref[rows, 0:c].astype(F32)
            sg = _sigmoid(u_ref[rows, c:2 * c].astype(F32))
            dg = dglu[rows, :]
            d_av = dg * sg
            d_ag = (dg * av) * (sg * (1.0 - sg))
            dpv = dp[rows, :]
            d_gc = dpv * u_ref[rows, 4 * c:5 * c].astype(F32)
            d_vs = dpv * u_ref[rows, 3 * c:4 * c].astype(F32)
            du_ref[rows, 0:c] = d_av.astype(BF16)
            du_ref[rows, c:2 * c] = d_ag.astype(BF16)
            du_ref[rows, 3 * c:4 * c] = d_gc.astype(BF16)
            du_ref[rows, 4 * c:5 * c] = d_vs.astype(BF16)
            dbin_ref[:, 0:c] += _rows8(d_av)
            dbin_ref[:, c:2 * c] += _rows8(d_ag)
            dbin_ref[:, 3 * c:4 * c] += _rows8(d_gc)
            dbin_ref[:, 4 * c:5 * c] += _rows8(d_vs)
        _row_loop(t, 64, blk2)

        @pl.when(i == n_t - 1)
        def _():
            dwa_ref[...] = _fold8(awa, CONF_K)
            dwb_ref[...] = _fold8(awb, SHORT_K)

    rev = lambda i: n_t - 1 - i
    small_in = lambda r: _resident((r, c), lambda i: (0, 0))
    small = lambda r: pl.BlockSpec((r, c), lambda i: (0, 0))
    return pl.pallas_call(
        body, grid=(n_t,),
        in_specs=[pl.BlockSpec((t, d), lambda i: (rev(i), 0)),
                  pl.BlockSpec((t, d_in), lambda i: (rev(i), 0)),
                  pl.BlockSpec((t, c), lambda i: (rev(i), 0)), pl.BlockSpec((t, c), lambda i: (rev(i), 0)),
                  small_in(CONF_K), small_in(1), small_in(1), small_in(SHORT_K),
                  _resident((2 * c, d), lambda i: (0, 0)), ANY],
        out_specs=[pl.BlockSpec((t, d_in), lambda i: (rev(i), 0)),
                   small(CONF_K), small(SHORT_K), small(SUB), small(SUB), small(SUB),
                   pl.BlockSpec((SUB, d_in), lambda i: (0, 0))],
        out_shape=[jax.ShapeDtypeStruct((s_len, d_in), BF16),
                   jax.ShapeDtypeStruct((CONF_K, c), F32), jax.ShapeDtypeStruct((SHORT_K, c), F32),
                   jax.ShapeDtypeStruct((SUB, c), F32), jax.ShapeDtypeStruct((SUB, c), F32),
                   jax.ShapeDtypeStruct((SUB, c), F32), jax.ShapeDtypeStruct((SUB, d_in), F32)],
        scratch_shapes=[pltpu.VMEM((t, c), F32), pltpu.VMEM((t, c), F32), pltpu.VMEM((t, 2 * c), F32),
                        pltpu.VMEM((t + HALO, c), F32), pltpu.VMEM((t + HALO3, c), F32),
                        pltpu.VMEM((t, c), F32), pltpu.VMEM((t, c), F32),
                        pltpu.VMEM((CONF_K * SUB, c), F32), pltpu.VMEM((SHORT_K * SUB, c), F32),
                        pltpu.VMEM((CONF_K * SUB, c), F32), pltpu.VMEM((SHORT_K * SUB, c), F32),
                        pltpu.VMEM((SUB - 1, t + HALO, c), F32)],
        compiler_params=_params("arbitrary"), name=name,
    )(dx1, u, ca, cb, wa, lg, lb, wb, w_out, u if dep is None else dep)


def _matmul_tn(a, b, *, name):
    n_p, s_len, k = a.shape
    n = b.shape[1]
    tk = _col_tile(k)
    per = k // tk
    if per > 2:
        def body(a_ref, b_ref, o_ref):
            o_ref[...] = _dot_tn(a_ref[...], b_ref[...]).astype(BF16)

        return pl.pallas_call(
            body, grid=(n_p, per),
            in_specs=[pl.BlockSpec((None, s_len, tk), lambda p, j: (p, 0, j)),
                      _resident((s_len, n), lambda p, j: (0, 0))],
            out_specs=pl.BlockSpec((tk, n), lambda p, j: (p * per + j, 0)),
            out_shape=jax.ShapeDtypeStruct((n_p * k, n), BF16),
            compiler_params=_params("parallel", "parallel"), name=name,
        )(a, b)

    half = s_len // 2

    def body_halves(a_ref, b_ref, o_ref, acc):
        @pl.when(pl.program_id(2) == 0)
        def _():
            acc[...] = _dot_tn(a_ref[...], b_ref[...])

        @pl.when(pl.program_id(2) == 1)
        def _():
            o_ref[...] = (acc[...] + _dot_tn(a_ref[...], b_ref[...])).astype(BF16)

    return pl.pallas_call(
        body_halves, grid=(n_p, per, 2),
        in_specs=[pl.BlockSpec((None, half, tk), lambda p, j, q: (p, q, j)),
                  pl.BlockSpec((half, n), lambda p, j, q: (q, 0))],
        out_specs=pl.BlockSpec((tk, n), lambda p, j, q: (p * per + j, 0)),
        out_shape=jax.ShapeDtypeStruct((n_p * k, n), BF16),
        scratch_shapes=[pltpu.VMEM((tk, n), F32)],
        compiler_params=_params("parallel", "parallel", "arbitrary"), name=name,
    )(a, b)


def _matmul_rmsbwd(dzs, wt, x, g, dx_in, *, name, dep=None):
    s_len, d = x.shape
    n_z, _, nj = dzs.shape
    t = _mm_tile(s_len)

    def body(*refs):
        dz_refs = refs[0:n_z]
        w_refs = refs[n_z:2 * n_z]
        x_ref, g_ref, dxi_ref, _, dx_ref, dxb_ref, dg_ref, dh = refs[2 * n_z:]

        @pl.when(pl.program_id(0) == 0)
        def _():
            dg_ref[...] = jnp.zeros_like(dg_ref)

        def blk(rows):
            xv = x_ref[rows, :]
            r = lax.rsqrt(jnp.mean(xv * xv, axis=-1, keepdims=True) + EPS)
            xn = xv * r
            dhv = dh[rows, :]
            dg_ref[...] += _rows8(dhv * xn)
            dn = dhv * g_ref[...]
            dx = dxi_ref[rows, :] + r * (dn - xn * jnp.mean(dn * xn, axis=-1, keepdims=True))
            dx_ref[rows, :] = dx
            dxb_ref[rows, :] = dx.astype(BF16)

        half = t // 2
        rb = min(128, half)
        for lo in range(0, t, half):
            acc = _dot(dz_refs[0][lo:lo + half, :], w_refs[0][...])
            for q in range(1, n_z):
                acc = acc + _dot(dz_refs[q][lo:lo + half, :], w_refs[q][...])
            dh[lo:lo + half, :] = acc
            for r0 in range(lo, lo + half, rb):
                blk(pl.ds(r0, rb))

    row = pl.BlockSpec((t, d), lambda i: (i, 0))
    in_specs = [pl.BlockSpec((None, t, nj), functools.partial(lambda q, i: (q, i, 0), q)) for q in range(n_z)]
    in_specs += [_resident((nj, d), functools.partial(lambda q, i: (q, 0), q)) for q in range(n_z)]
    in_specs += [row, _resident((1, d), lambda i: (0, 0)), row, ANY]
    return pl.pallas_call(
        body, grid=(s_len // t,), in_specs=in_specs,
        out_specs=[row, row, pl.BlockSpec((SUB, d), lambda i: (0, 0))],
        out_shape=[jax.ShapeDtypeStruct((s_len, d), F32), jax.ShapeDtypeStruct((s_len, d), BF16),
                   jax.ShapeDtypeStruct((SUB, d), F32)],
        scratch_shapes=[pltpu.VMEM((t, d), F32)],
        compiler_params=_params("arbitrary"), name=name,
    )(*([dzs] * n_z), *([wt] * n_z), x, g, dx_in, x if dep is None else dep)


def _row(v):
    return v.reshape(1, -1)


def _layer_fwd(x0, p, tag, dep=None, before_up=None):
    u, h1 = _rms_matmul(x0, _row(p["mix_norm_g"]), p["w_in_t"], _row(p["b_in"]), name=f"in_proj_{tag}", dep=dep)
    ycat, x1, ca, cb = _mix_fwd(u, x0, p["conv_a_w"], _row(p["conv_a_b"]), _row(p["ln_a_g"]), _row(p["ln_a_b"]),
                            p["conv_b_w"], p["w_out"], name=f"mix_fwd_{tag}")
    if before_up is not None:
        before_up(x1)
    uf, h2 = _rms_matmul(x1, _row(p["ffn_norm_g"]), p["w_up_t"], None, name=f"up_proj_{tag}")
    act, x2, cg, cv = _ffn_fwd(uf, x1, p["conv_f_w"], p["w_down"], name=f"ffn_fwd_{tag}")
    return x2, dict(x0=x0, h1=h1, u=u, ca=ca, cb=cb, ycat=ycat, x1=x1, h2=h2, uf=uf, cg=cg, cv=cv, act=act)


def _layer_bwd(dx2, dx2_b, p, saved, tag, ffn_grads, ffn_sent, mix_grads, mix_sent, dep=None):
    d_uf, dwf_g, dwf_v = _ffn_bwd(dx2_b, saved["uf"], saved["cg"], saved["cv"], p["conv_f_w"], p["w_down"],
                                  name=f"ffn_bwd_{tag}", dep=dep)
    g_down = _matmul_tn(saved["act"][None], dx2_b, name=f"dw_down_{tag}")
    g_up = _matmul_tn(d_uf, saved["h2"], name=f"dw_up_{tag}")
    dep_ffn = ffn_grads(dict(w_up=g_up, w_down=g_down), dx2_b)
    dx1, dx1_b, dg2 = _matmul_rmsbwd(d_uf, p["w_up_t"], saved["x1"], _row(p["ffn_norm_g"]), dx2,
                                     name=f"dh_ffn_{tag}", dep=dep_ffn)
    du, dwa, dwb, dba, dlg, dlb, dbin = _mix_bwd(
        dx1_b, saved["u"], saved["ca"], saved["cb"], p["conv_a_w"], _row(p["ln_a_g"]), _row(p["ln_a_b"]),
        p["conv_b_w"], p["w_out"], name=f"mix_bwd_{tag}", dep=ffn_sent(dx1_b))
    g_out = _matmul_tn(saved["ycat"][None], dx1_b, name=f"dw_out_{tag}")
    g_in = _matmul_tn(du[None], saved["h1"], name=f"dw_in_{tag}")
    conv = dict(conv_a_w=dwa, conv_b_w=dwb, conv_f_w=jnp.concatenate([dwf_g, dwf_v], axis=1))
    dep_mix = mix_grads(dict(w_in=g_in, w_out=g_out), conv, dx1_b)
    dx0, dx0_b, dg1 = _matmul_rmsbwd(du[None], p["w_in_t"], saved["x0"], _row(p["mix_norm_g"]), dx1,
                                     name=f"dh_mix_{tag}", dep=dep_mix)
    rep = dict(mix_norm_g=dg1, b_in=dbin, conv_a_b=dba, ln_a_g=dlg, ln_a_b=dlb, ffn_norm_g=dg2)
    return dx0, dx0_b, rep, mix_sent(dx0_b)


def _place():
    return lax.axis_index("x"), lax.axis_index("y"), lax.axis_index("c")


def _all_gather(arrs, *, name):
    n_a = len(arrs)

    def body(*refs):
        ins = refs[0:n_a]
        outs = refs[n_a:2 * n_a]
        send_sems, recv_sems, local_sems = refs[2 * n_a:]
        x, y, c = _place()
        sibling = (x, y, 1 - c)
        chips = [(1 - x, y), (x, 1 - y), (1 - x, 1 - y)]

        def slot(a, px, py, pc):
            return outs[a].at[4 * px + 2 * py + pc]

        def copy(a, k, block, to, src=None):
            return pltpu.make_async_remote_copy(
                src_ref=slot(a, *block) if src is None else src, dst_ref=slot(a, *block),
                send_sem=send_sems.at[a, k], recv_sem=recv_sems.at[a, k],
                device_id=to, device_id_type=MESH)

        me = (x, y, c)
        mine = [pltpu.make_async_copy(ins[a], slot(a, *me), local_sems.at[a]) for a in range(n_a)]
        for cp in mine:
            cp.start()
        started = []
        for a in range(n_a):
            first = [copy(a, 0, me, sibling, src=ins[a])]
            first += [copy(a, 1 + j, me, (*chip, c), src=ins[a]) for j, chip in enumerate(chips)]
            for cp in first:
                cp.start()
            started += first
        for a in range(n_a):
            for j, chip in enumerate(chips):
                copy(a, 1 + j, (*chip, c), me).wait_recv()
                passed = copy(a, 4 + j, (*chip, c), sibling)
                passed.start()
                started.append(passed)
        for a in range(n_a):
            copy(a, 0, sibling, me).wait_recv()
            for j, chip in enumerate(chips):
                copy(a, 4 + j, (*chip, 1 - c), me).wait_recv()
        for cp in started:
            cp.wait_send()
        for cp in mine:
            cp.wait()

    return pl.pallas_call(
        body, in_specs=[ANY] * n_a, out_specs=[ANY] * n_a,
        out_shape=[jax.ShapeDtypeStruct((N_DEV, *a.shape), a.dtype) for a in arrs],
        scratch_shapes=[pltpu.SemaphoreType.DMA((n_a, 7)), pltpu.SemaphoreType.DMA((n_a, 7)),
                        pltpu.SemaphoreType.DMA((n_a,))],
        name=name,
    )(*arrs)


def _row_tile(r, cap):
    for tr in range(min(cap, r) // 16 * 16, 0, -16):
        if r % tr == 0:
            return tr
    return r


def _pair_sum(mines, theirs, where, *, name):
    n_a = len(mines)
    n_chip = mines[0].shape[0]

    def body(where_ref, *refs):
        a_refs = refs[0:n_a]
        b_refs = refs[n_a:2 * n_a]
        p_refs = refs[2 * n_a:3 * n_a]
        l_refs = refs[3 * n_a:4 * n_a]
        q = pl.program_id(0)
        for a in range(n_a):
            p_refs[a][...] = (a_refs[a][...].astype(F32) + b_refs[a][...].astype(F32)).astype(p_refs[a].dtype)

        @pl.when(q == where_ref[1])
        def _():
            for a in range(n_a):
                l_refs[a][...] = p_refs[a][...]

    in_specs, out_p, out_l, shapes = [], [], [], []
    for m in mines:
        _, _, r, c = m.shape
        in_specs.append(pl.BlockSpec((None, None, r, c), lambda q, where_ref: (q, where_ref[0], 0, 0)))
    for m in mines:
        _, _, r, c = m.shape
        in_specs.append(pl.BlockSpec((None, r, c), lambda q, where_ref: (q, 0, 0)))
        out_p.append(pl.BlockSpec((None, r, c), lambda q, where_ref: (q, 0, 0)))
        out_l.append(pl.BlockSpec((None, r, c), lambda q, where_ref: (where_ref[1], 0, 0)))
        shapes.append(jax.ShapeDtypeStruct((n_chip, r, c), m.dtype))
    res = pl.pallas_call(
        body,
        grid_spec=pltpu.PrefetchScalarGridSpec(num_scalar_prefetch=1, grid=(n_chip,), in_specs=in_specs,
                                               out_specs=out_p + out_l),
        out_shape=shapes + shapes,
        compiler_params=_params("arbitrary"), name=name,
    )(where, *mines, *theirs)
    return list(res[:n_a]), list(res[n_a:])


HBM = pl.BlockSpec(memory_space=pltpu.HBM)
SEM = pl.BlockSpec(memory_space=pltpu.SEMAPHORE)
EFFECT = pltpu.SideEffectType.DATAFLOW_SIDE_EFFECTING


def _in_hbm(a):
    return pltpu.with_memory_space_constraint(a, pltpu.HBM)


def _split_start(srcs, lands, plan, n_copies, after, *, name):
    n_s, n_l = len(srcs), len(lands)

    def body(*refs):
        src_refs = refs[0:n_s]
        land_refs = refs[n_s:n_s + n_l]
        send_sems, recv_sems = refs[n_s + n_l + 1], refs[n_s + n_l + 2]
        token = refs[-1]
        for cp in plan(src_refs, land_refs, send_sems, recv_sems):
            cp.start()
        token[...] = jnp.zeros_like(token)

    thru = [pltpu.HBM(a.shape, a.dtype) for a in list(srcs) + list(lands)]
    res = pl.pallas_call(
        body, name=name,
        out_shape=(pltpu.SemaphoreType.DMA((n_copies,)), pltpu.SemaphoreType.DMA((n_copies,)), *thru,
                   jax.ShapeDtypeStruct((SUB, LANES), F32)),
        in_specs=[HBM] * (n_s + n_l) + [ANY],
        out_specs=(SEM, SEM, *([HBM] * (n_s + n_l)), pl.BlockSpec(memory_space=pltpu.VMEM)),
        input_output_aliases={i: 2 + i for i in range(n_s + n_l)},
        compiler_params=pltpu.CompilerParams(has_side_effects=EFFECT),
    )(*[_in_hbm(a) for a in srcs], *[_in_hbm(a) for a in lands], _in_hbm(after))
    return res[0], res[1], list(res[2:2 + n_s]), list(res[2 + n_s:2 + n_s + n_l]), res[-1]


def _split_wait(send_sems, recv_sems, srcs, lands, after, plan, *, name):
    n_s, n_l = len(srcs), len(lands)

    def body(*refs):
        src_refs = refs[0:n_s]
        land_refs = refs[n_s:n_s + n_l]
        send, recv = refs[n_s + n_l], refs[n_s + n_l + 1]
        for cp in plan(src_refs, land_refs, send, recv):
            cp.wait_send()
            cp.wait_recv()

    res = pl.pallas_call(
        body, name=name,
        out_shape=tuple(pltpu.HBM(a.shape, a.dtype) for a in list(srcs) + list(lands)),
        in_specs=[HBM] * (n_s + n_l) + [SEM, SEM, ANY],
        out_specs=tuple([HBM] * (n_s + n_l)),
        input_output_aliases={i: i for i in range(n_s + n_l)},
        compiler_params=pltpu.CompilerParams(has_side_effects=EFFECT),
    )(*srcs, *lands, send_sems, recv_sems, _in_hbm(after))
    return list(res[:n_s]), list(res[n_s:])


def _remote(src, dst, send_sems, recv_sems, k, to):
    return pltpu.make_async_remote_copy(src_ref=src, dst_ref=dst, send_sem=send_sems.at[k], recv_sem=recv_sems.at[k],
                                        device_id=to, device_id_type=MESH)


def _gather_plan_first(src_refs, land_refs, send_sems, recv_sems):
    x, y, c = _place()
    me = 4 * x + 2 * y + c
    peers = [(x, y, 1 - c), (1 - x, y, c), (x, 1 - y, c), (1 - x, 1 - y, c)]
    return [_remote(src, land.at[me], send_sems, recv_sems, 4 * a + k, to)
            for a, (src, land) in enumerate(zip(src_refs, land_refs)) for k, to in enumerate(peers)]


def _gather_plan_second(src_refs, land_refs, send_sems, recv_sems):
    x, y, c = _place()
    chips = [(1 - x, y), (x, 1 - y), (1 - x, 1 - y)]
    out = []
    for a, land in enumerate(land_refs):
        for j, (px, py) in enumerate(chips):
            slot = land.at[4 * px + 2 * py + c]
            out.append(_remote(slot, slot, send_sems, recv_sems, 3 * a + j, (x, y, 1 - c)))
    return out


def _siblings_plan(src_refs, land_refs, send_sems, recv_sems):
    x, y, c = _place()
    return [_remote(src.at[:, 1 - c], land, send_sems, recv_sems, a, (x, y, 1 - c))
            for a, (src, land) in enumerate(zip(src_refs, land_refs))]


def _chips_plan(src_refs, land_refs, send_sems, recv_sems):
    x, y, c = _place()
    my_chip = 2 * x + y
    chips = [(1 - x, y), (x, 1 - y), (1 - x, 1 - y)]
    return [_remote(src.at[2 * px + py], land.at[my_chip], send_sems, recv_sems, 3 * a + j, (px, py, c))
            for a, (src, land) in enumerate(zip(src_refs, land_refs)) for j, (px, py) in enumerate(chips)]


def _gather_landings(shards, me, *, name):
    blank = _unwritten([jax.ShapeDtypeStruct((N_DEV, *s.shape), s.dtype) for s in shards], name=name)
    return [lax.dynamic_update_index_in_dim(b, s, me, 0) for b, s in zip(blank, shards)]


def _adamw_math(g, w, m, v):
    m = ADAM_B1 * m + (1.0 - ADAM_B1) * g
    v = ADAM_B2 * v + (1.0 - ADAM_B2) * (g * g)
    m_hat = m / (1.0 - ADAM_B1 ** ADAM_STEP)
    v_hat = v / (1.0 - ADAM_B2 ** ADAM_STEP)
    delta = -ADAM_LR * (m_hat / (jnp.sqrt(v_hat) + ADAM_EPS) + ADAM_WD * w)
    return delta, m, v


def _adamw_sharded(parts, w, m, v, *, name, dep=None):
    n_layers, r, c = w.shape
    n_chip = parts[0].shape[0]
    tr = _row_tile(r, 384)
    n_i = r // tr

    def body(*refs):
        p_refs = refs[0:n_layers]
        w_ref, m_ref, v_ref, _, g_out, d_out, m_out, v_out = refs[n_layers:]
        layer = pl.program_id(0)
        for l in range(n_layers):
            @pl.when(layer == l)
            def _(l=l):
                g = p_refs[l][0].astype(F32)
                for q in range(1, n_chip):
                    g = g + p_refs[l][q].astype(F32)
                delta, m_new, v_new = _adamw_math(g, w_ref[...], m_ref[...], v_ref[...])
                g_out[...] = g
                d_out[...] = delta
                m_out[...] = m_new
                v_out[...] = v_new

    def part_map(l):
        return lambda layer, i: (0, jnp.where(layer == l, i, jnp.where(layer < l, 0, n_i - 1)), 0)

    blk = pl.BlockSpec((None, tr, c), lambda layer, i: (layer, i, 0))
    return pl.pallas_call(
        body, grid=(n_layers, n_i),
        in_specs=[pl.BlockSpec((n_chip, tr, c), part_map(l)) for l in range(n_layers)] + [blk, blk, blk, ANY],
        out_specs=[blk] * 4, out_shape=[jax.ShapeDtypeStruct((n_layers, r, c), F32)] * 4,
        compiler_params=_params("arbitrary", "arbitrary"), name=name,
    )(*parts, w, m, v, w if dep is None else dep)


def _fold_partials(cols, *, name):
    widths = [c.shape[1] for c in cols]

    def body(*refs):
        o_ref = refs[-1]
        pos = 0
        for ref, width in zip(refs[:-1], widths):
            o_ref[:, pos:pos + width] = jnp.sum(ref[...], axis=0, keepdims=True)
            pos += width

    return pl.pallas_call(body, out_shape=jax.ShapeDtypeStruct((1, sum(widths)), F32), name=name)(*cols)


def _adamw_replicated(parts, names, w, m, v, n_loss, *, name):
    n_dev = parts.shape[0]
    n_layers = w[names[0]].shape[0]
    every = list(names) + ["final_norm_g"]
    n_p = len(every)

    def body(*refs):
        p_ref = refs[0]
        w_refs = dict(zip(every, refs[1:1 + n_p]))
        m_refs = dict(zip(every, refs[1 + n_p:1 + 2 * n_p]))
        v_refs = dict(zip(every, refs[1 + 2 * n_p:1 + 3 * n_p]))
        l_out = refs[1 + 3 * n_p]
        outs = refs[2 + 3 * n_p:]
        o_refs = {n: outs[4 * q:4 * q + 4] for q, n in enumerate(every)}
        acc = p_ref[0]
        for q in range(1, n_dev):
            acc = acc + p_ref[q]
        tot = jnp.sum(acc, axis=0, keepdims=True)
        pos = 0
        where = [(n, l) for l in range(n_layers) for n in names] + [("final_norm_g", 0)]
        for n, l in where:
            width = w_refs[n].shape[1]
            g = tot[:, pos:pos + width]
            pos += width
            row = pl.ds(l, 1)
            delta, m_new, v_new = _adamw_math(g, w_refs[n][row, :], m_refs[n][row, :], v_refs[n][row, :])
            for o, val in zip(o_refs[n], (g, delta, m_new, v_new)):
                o[row, :] = val
        l_out[...] = (0.5 / n_loss) * jnp.sum(tot[:, pos:pos + n_loss], axis=-1, keepdims=True)

    shapes = [jax.ShapeDtypeStruct((1, 1), F32)]
    for n in every:
        shapes += [jax.ShapeDtypeStruct(w[n].shape, F32)] * 4
    res = pl.pallas_call(
        body, out_shape=shapes,
        compiler_params=pltpu.CompilerParams(vmem_limit_bytes=VMEM_LIMIT), name=name,
    )(parts, *[w[n] for n in every], *[m[n] for n in every], *[v[n] for n in every])
    return res[0], {n: res[1 + 4 * q:5 + 4 * q] for q, n in enumerate(every)}


BIG = ("w_in", "w_out", "w_up", "w_down")
COL_SHARDED = ("w_in", "w_up")
CONV = ("conv_a_w", "conv_b_w", "conv_f_w")
REPLICATED = ("mix_norm_g", "b_in", "conv_a_b", "ln_a_g", "ln_a_b", "ffn_norm_g")
KINDS = ("grad", "delta", "m", "v")
FFN_PART = ("w_up", "w_down")
MIX_PART = ("w_in", "w_out")


def _weights_from_gathered(g):
    n_dev, r, c = g.shape
    return g.reshape(n_dev * r, c)


def _slabs_from_full(grad):
    return grad.reshape(N_DEV, grad.shape[0] // N_DEV, grad.shape[1])


def _unwritten(like, *, name):
    return pl.pallas_call(lambda *refs: None, out_specs=[ANY] * len(like), out_shape=list(like), name=name)()


def kernel(x, mix_norm_g, w_in, b_in, conv_a_w, conv_a_b, ln_a_g, ln_a_b, conv_b_w, w_out, ffn_norm_g, w_up, conv_f_w, w_down, final_norm_g, loss_target, m_mix_norm_g, m_w_in, m_b_in, m_conv_a_w, m_conv_a_b, m_ln_a_g, m_ln_a_b, m_conv_b_w, m_w_out, m_ffn_norm_g, m_w_up, m_conv_f_w, m_w_down, m_final_norm_g, v_mix_norm_g, v_w_in, v_b_in, v_conv_a_w, v_conv_a_b, v_ln_a_g, v_ln_a_b, v_conv_b_w, v_w_out, v_ffn_norm_g, v_w_up, v_conv_f_w, v_w_down, v_final_norm_g):
    w = dict(mix_norm_g=mix_norm_g, w_in=w_in, b_in=b_in, conv_a_w=conv_a_w, conv_a_b=conv_a_b, ln_a_g=ln_a_g,
             ln_a_b=ln_a_b, conv_b_w=conv_b_w, w_out=w_out, ffn_norm_g=ffn_norm_g, w_up=w_up, conv_f_w=conv_f_w,
             w_down=w_down, final_norm_g=final_norm_g)
    m = dict(mix_norm_g=m_mix_norm_g, w_in=m_w_in, b_in=m_b_in, conv_a_w=m_conv_a_w, conv_a_b=m_conv_a_b,
             ln_a_g=m_ln_a_g, ln_a_b=m_ln_a_b, conv_b_w=m_conv_b_w, w_out=m_w_out, ffn_norm_g=m_ffn_norm_g,
             w_up=m_w_up, conv_f_w=m_conv_f_w, w_down=m_w_down, final_norm_g=m_final_norm_g)
    v = dict(mix_norm_g=v_mix_norm_g, w_in=v_w_in, b_in=v_b_in, conv_a_w=v_conv_a_w, conv_a_b=v_conv_a_b,
             ln_a_g=v_ln_a_g, ln_a_b=v_ln_a_b, conv_b_w=v_conv_b_w, w_out=v_w_out, ffn_norm_g=v_ffn_norm_g,
             w_up=v_w_up, conv_f_w=v_conv_f_w, w_down=v_w_down, final_norm_g=v_final_norm_g)
    order = list(w)
    n_layers = w_in.shape[0]
    xs = x[0]
    target = loss_target[0]
    flip = lambda a: jnp.transpose(a, (0, 2, 1))
    wt, mt, vt = ({n: flip(d[n]) if n in COL_SHARDED else d[n] for n in BIG} for d in (w, m, v))
    px, py, pc = _place()
    where = jnp.stack([pc, 2 * px + py]).astype(jnp.int32)
    me = 4 * px + 2 * py + pc

    assert BIG == MIX_PART + FFN_PART
    key = lambda n: n + "_t" if n in COL_SHARDED else n
    shard = lambda n, l: wt[n][l].astype(BF16)

    def gather_start(names, l, after, tag):
        shards = [shard(n, l) for n in names]
        lands = _gather_landings(shards, me, name=f"gather_landing_{tag}")
        return _split_start(shards, lands, _gather_plan_first, 4 * len(shards), after, name=f"gather_first_start_{tag}")

    def gather_mid(first, after, tag):
        return _split_wait(first[0], first[1], first[2], first[3], after, _gather_plan_first,
                           name=f"gather_first_wait_{tag}")[1]

    def forward_start(lands, after, tag):
        return _split_start([], lands, _gather_plan_second, 3 * len(lands), after, name=f"gather_second_start_{tag}")

    def forward_finish(second, after, tag):
        return _split_wait(second[0], second[1], [], second[3], after, _gather_plan_second,
                           name=f"gather_second_wait_{tag}")[1]

    gathered = _all_gather([shard(n, 0) for n in MIX_PART] + [w[n] for n in CONV], name="gather_weights_0")
    params = [{n: w[n][l] for n in REPLICATED} for l in range(n_layers)]
    for n, g in zip(CONV, gathered[len(MIX_PART):]):
        n_dev, _, taps, c = g.shape
        full = g.transpose(1, 2, 0, 3).reshape(n_layers, taps, n_dev * c)
        for l in range(n_layers):
            params[l][n] = full[l]
    for n, g in zip(MIX_PART, gathered):
        params[0][key(n)] = _weights_from_gathered(g)
    ffn_first = gather_start(FFN_PART, 0, gathered[0], "0_ffn")
    pending = {}

    h = xs
    saved = []
    for l in range(n_layers):
        nxt = l + 1 if l + 1 < n_layers else None

        def before_up(x1, l=l, nxt=nxt):
            if l == 0:
                second = forward_start(gather_mid(ffn_first, x1, "0_ffn"), x1, "0_ffn")
                after = second[4]
            else:
                second = pending[l]["ffn"]
                after = x1
            if nxt is not None:
                pending[nxt] = dict(first=gather_start(BIG, nxt, after, str(nxt)))
                after = pending[nxt]["first"][4]
            for n, g in zip(FFN_PART, forward_finish(second, after, f"{l}_ffn")):
                params[l][key(n)] = _weights_from_gathered(g)

        h, keep = _layer_fwd(h, params[l], str(l), dep=ffn_first[4] if l == 0 else None, before_up=before_up)
        saved.append(keep)
        if nxt is not None:
            arrived = gather_mid(pending[nxt]["first"], h, str(nxt))
            mix_second = forward_start(arrived[:len(MIX_PART)], h, f"{nxt}_mix")
            pending[nxt]["ffn"] = forward_start(arrived[len(MIX_PART):], mix_second[4], f"{nxt}_ffn")
            for n, g in zip(MIX_PART, forward_finish(mix_second, pending[nxt]["ffn"][4], f"{nxt}_mix")):
                params[nxt][key(n)] = _weights_from_gathered(g)

    def start_siblings(slabs, after, tag):
        mines = [s.reshape(N_CHIP, 2, *s.shape[1:]) for s in slabs]
        lands = _unwritten([jax.ShapeDtypeStruct((N_CHIP, *m.shape[2:]), m.dtype) for m in mines],
                           name=f"reduce_siblings_landing_{tag}")
        return _split_start(mines, lands, _siblings_plan, len(mines), after, name=f"reduce_siblings_start_{tag}")

    def start_chips(sib, after, tag):
        mines, theirs = _split_wait(sib[0], sib[1], sib[2], sib[3], after, _siblings_plan,
                                    name=f"reduce_siblings_wait_{tag}")
        pairs, lands = _pair_sum(mines, theirs, where, name=f"pair_sum_{tag}")
        return _split_start(pairs, lands, _chips_plan, 3 * len(pairs), after, name=f"reduce_chips_start_{tag}")

    def finish_reduce(fly, after, tag):
        return _split_wait(fly[0], fly[1], fly[2], fly[3], after, _chips_plan, name=f"reduce_chips_wait_{tag}")[1]

    loss_sq, dh, dh_b, dgf = _loss_bwd(h, _row(final_norm_g), target, name="loss")
    conv_g = {n: [None] * n_layers for n in CONV}
    rep_g = [None] * n_layers
    siblings = {}
    flights = {}
    token = None
    for l in reversed(range(n_layers)):
        def ffn_grads(g, after, l=l):
            siblings[l, "ffn"] = start_siblings([_slabs_from_full(g[n]) for n in FFN_PART], after, f"{l}_ffn")
            return siblings[l, "ffn"][4]

        def ffn_sent(after, l=l):
            flights[l, "ffn"] = start_chips(siblings[l, "ffn"], after, f"{l}_ffn")
            return flights[l, "ffn"][4]

        def mix_grads(g, conv, after, l=l):
            for n in CONV:
                conv_g[n][l] = conv[n]
            slabs = [_slabs_from_full(g[n]) for n in MIX_PART]
            if l == 0:
                for n in CONV:
                    full = jnp.stack(conv_g[n])
                    _, taps, c = full.shape
                    slabs.append(full.reshape(n_layers, taps, N_DEV, c // N_DEV).transpose(2, 0, 1, 3)
                                 .reshape(N_DEV, n_layers * taps, c // N_DEV))
            siblings[l, "mix"] = start_siblings(slabs, after, f"{l}_mix")
            return siblings[l, "mix"][4]

        def mix_sent(after, l=l):
            flights[l, "mix"] = start_chips(siblings[l, "mix"], after, f"{l}_mix")
            return flights[l, "mix"][4]

        dh, dh_b, rep_g[l], token = _layer_bwd(dh, dh_b, params[l], saved[l], str(l), ffn_grads, ffn_sent,
                                               mix_grads, mix_sent, dep=token)

    sums = {key: finish_reduce(fly, dh, f"{key[0]}_{key[1]}") for key, fly in flights.items() if key != (0, "mix")}
    out = {k: {} for k in KINDS}

    def adamw_big(names, part, dep):
        for q, n in enumerate(names):
            layer_parts = [sums[l, part][q] for l in range(n_layers)]
            res = _adamw_sharded(layer_parts, wt[n], mt[n], vt[n], name=f"adamw_{n}", dep=dep)
            for k, r in zip(KINDS, res):
                out[k][n] = flip(r) if n in COL_SHARDED else r

    adamw_big(FFN_PART, "ffn", token)

    rep_cols = [rep_g[l][n] for l in range(n_layers) for n in REPLICATED] + [dgf, loss_sq]
    rep_all = _all_gather([_fold_partials(rep_cols, name="fold_small")], name="gather_small")[0]
    with_final = lambda d: {**{n: d[n] for n in REPLICATED}, "final_norm_g": _row(d["final_norm_g"])}
    loss, rep_res = _adamw_replicated(rep_all, REPLICATED, with_final(w), with_final(m), with_final(v),
                                      loss_sq.shape[1], name="adamw_small")
    for n, res in rep_res.items():
        for k, r in zip(KINDS, res):
            out[k][n] = r.reshape(w[n].shape)

    last = finish_reduce(flights[0, "mix"], rep_res["b_in"][0], "0_mix")
    sums[0, "mix"] = last[:len(MIX_PART)]
    adamw_big(MIX_PART, "mix", None)
    for n, p in zip(CONV, last[len(MIX_PART):]):
        as_one = lambda a: a.reshape(1, *p.shape[1:])
        for k, r in zip(KINDS, _adamw_sharded([p], as_one(w[n]), as_one(m[n]), as_one(v[n]), name=f"adamw_{n}")):
            out[k][n] = r.reshape(w[n].shape)

    grad_x = dh.reshape(x.shape)
    return (loss.reshape(()), grad_x, *[out["grad"][n] for n in order], *[out["delta"][n] for n in order],
            *[out["m"][n] for n in order], *[out["v"][n] for n in order])
```

```python
import functools

import jax
import jax.numpy as jnp
from jax import lax
from jax.experimental import pallas as pl
from jax.experimental.pallas import tpu as pltpu

F32 = jnp.float32
BF16 = jnp.bfloat16

N_DEV = 8
N_CHIP = 4
D_CONF = 512
CONF_K = 31
SHORT_K = 3
EPS = 1e-6
HALO = 32
HALO3 = 8
HALO3_BLK = 16
LANES = 128
SUB = 8
VMEM_LIMIT = 56 * 1024 * 1024

ADAM_LR = 0.001
ADAM_B1 = 0.9
ADAM_B2 = 0.999
ADAM_EPS = 1e-08
ADAM_WD = 0.01
ADAM_STEP = 10

MESH = pl.DeviceIdType.MESH
ANY = pl.BlockSpec(memory_space=pl.ANY)


def _params(*sem):
    return pltpu.CompilerParams(dimension_semantics=sem, vmem_limit_bytes=VMEM_LIMIT)


def _resident(shape, index_map):
    return pl.BlockSpec(shape, index_map, pipeline_mode=pl.Buffered(1))


def _row_loop(n_rows, rb, fn, unroll=1):
    rb = min(rb, n_rows)

    def body(i, carry):
        fn(pl.ds(pl.multiple_of(i * rb, rb), rb))
        return carry
    lax.fori_loop(0, n_rows // rb, body, 0, unroll=unroll)


def _rows8(v):
    acc = v[0:SUB]
    for k in range(1, v.shape[0] // SUB):
        acc = acc + v[k * SUB:(k + 1) * SUB]
    return acc


def _sigmoid(z):
    return 0.5 * jnp.tanh(0.5 * z) + 0.5


def _dot(a, b):
    return jnp.dot(a, b, preferred_element_type=F32)


def _dot_nt(a, b):
    return lax.dot_general(a, b, (((1,), (1,)), ((), ())), preferred_element_type=F32)


def _dot_tn(a, b):
    return lax.dot_general(a, b, (((0,), (0,)), ((), ())), preferred_element_type=F32)


def _replicate_taps(w_ref, wrep, taps):
    for k in range(taps):
        wrep[pl.ds(k * SUB, SUB), :] = jnp.broadcast_to(w_ref[pl.ds(k, 1), :], (SUB, w_ref.shape[1]))


def _shift_copies(win, shf, lanes):
    span = win.shape[0] - SUB
    for r in range(1, SUB):
        for j0 in range(0, span, 64):
            n = min(64, span - j0)
            shf[r - 1, pl.ds(j0, n), lanes] = win[pl.ds(j0 + r, n), lanes]


def _rows_at(win, shf, off, rb, lanes):
    if shf is None or off % SUB == 0:
        return win[pl.ds(off, rb), lanes]
    return shf[off % SUB - 1, pl.ds(off - off % SUB, rb), lanes]


def _conv_taps(win, wrep, out, *, taps, n_rows, base, width, transposed=False, bias_ref=None, shf=None):
    rb = min(32 if taps > 8 else 64, n_rows)

    def lane_body(cb, carry):
        lanes = pl.ds(pl.multiple_of(cb * LANES, LANES), LANES)
        if shf is not None:
            _shift_copies(win, shf, lanes)
        for r0 in range(0, n_rows, rb):
            acc = None
            for k in range(taps):
                off = (taps - 1 - k) if transposed else (k - (taps - 1))
                wk = jnp.tile(wrep[pl.ds(k * SUB, SUB), lanes], (rb // SUB, 1))
                term = wk * _rows_at(win, shf, base + r0 + off, rb, lanes)
                acc = term if acc is None else acc + term
            if bias_ref is not None:
                acc = acc + bias_ref[:, lanes]
            out[pl.ds(r0, rb), lanes] = acc.astype(out.dtype)
        return carry

    lax.fori_loop(0, width // LANES, lane_body, 0)


def _conv_bwd_taps(win, wrep, x_cur, dx_out, dw_acc, *, taps, n_rows, width, shf=None):
    rb = min(32 if taps > 8 else 64, n_rows)

    def lane_body(cb, carry):
        lanes = pl.ds(pl.multiple_of(cb * LANES, LANES), LANES)
        if shf is not None:
            _shift_copies(win, shf, lanes)
        sums = [None] * taps
        for r0 in range(0, n_rows, rb):
            xv = x_cur[pl.ds(r0, rb), lanes].astype(F32)
            acc = None
            for k in range(taps):
                shifted = _rows_at(win, shf, r0 + taps - 1 - k, rb, lanes)
                term = jnp.tile(wrep[pl.ds(k * SUB, SUB), lanes], (rb // SUB, 1)) * shifted
                acc = term if acc is None else acc + term
                part = _rows8(xv * shifted)
                sums[k] = part if sums[k] is None else sums[k] + part
            dx_out[pl.ds(r0, rb), lanes] = acc.astype(dx_out.dtype)
        for k in range(taps):
            dw_acc[pl.ds(k * SUB, SUB), lanes] += sums[k]
        return carry

    lax.fori_loop(0, width // LANES, lane_body, 0)


def _fold8(acc_ref, taps):
    return jnp.concatenate(
        [jnp.sum(acc_ref[pl.ds(k * SUB, SUB), :], axis=0, keepdims=True) for k in range(taps)], axis=0)


def _seq_tile(s_len):
    return min(512, s_len)


def _mm_tile(s_len):
    return min(512, s_len)


def _ff_chunk(ff):
    best = LANES
    for c in range(LANES, 1408 + 1, LANES):
        if ff % c == 0:
            best = c
    return best


def _col_tile(n):
    for c in (512, 1408, 256, LANES):
        if n % c == 0:
            return c
    return n


def _rms_matmul(x, g, wt, b, *, name, dep=None):
    s_len, d = x.shape
    n = wt.shape[0]
    tm = _mm_tile(s_len)
    cn = _col_tile(n)
    has_bias = b is not None

    def body(*refs):
        x_ref, g_ref, w_ref = refs[0:3]
        b_ref = refs[3] if has_bias else None
        o_ref, h_ref = refs[-2:]

        def blk(rows):
            xv = x_ref[rows, :]
            r = lax.rsqrt(jnp.mean(xv * xv, axis=-1, keepdims=True) + EPS)
            h_ref[rows, :] = ((xv * r) * g_ref[...]).astype(BF16)

        rb = min(128, tm)
        for r0 in range(0, tm, rb):
            blk(pl.ds(r0, rb))
        for j in range(n // cn):
            acc = _dot_nt(h_ref[...], w_ref[j * cn:(j + 1) * cn, :])
            if has_bias:
                acc = acc + b_ref[:, j * cn:(j + 1) * cn]
            o_ref[:, j * cn:(j + 1) * cn] = acc.astype(BF16)

    in_specs = [pl.BlockSpec((tm, d), lambda i: (i, 0)), _resident((1, d), lambda i: (0, 0)),
                _resident((n, d), lambda i: (0, 0))]
    args = [x, g, wt]
    if has_bias:
        in_specs.append(_resident((1, n), lambda i: (0, 0)))
        args.append(b)
    in_specs.append(ANY)
    args.append(x if dep is None else dep)
    return pl.pallas_call(
        body, grid=(s_len // tm,), in_specs=in_specs,
        out_specs=[pl.BlockSpec((tm, n), lambda i: (i, 0)), pl.BlockSpec((tm, d), lambda i: (i, 0))],
        out_shape=[jax.ShapeDtypeStruct((s_len, n), BF16), jax.ShapeDtypeStruct((s_len, d), BF16)],
        compiler_params=_params("parallel"), name=name,
    )(*args)


def _mix_windows(u_ref, uh_ref, gw, pw, first, t):
    c = D_CONF
    uh = uh_ref[...].astype(F32)
    gw[0:HALO, :] = jnp.where(first, 0.0, uh[:, 0:c] * _sigmoid(uh[:, c:2 * c]))
    pw[0:HALO3, :] = jnp.where(first, 0.0, uh[HALO - HALO3:HALO, 3 * c:4 * c] * uh[HALO - HALO3:HALO, 4 * c:5 * c])

    def blk(rows):
        dst = pl.ds(pl.multiple_of(rows.start + HALO, SUB), rows.size)
        gw[dst, :] = u_ref[rows, 0:c].astype(F32) * _sigmoid(u_ref[rows, c:2 * c].astype(F32))
        dst3 = pl.ds(pl.multiple_of(rows.start + HALO3, SUB), rows.size)
        pw[dst3, :] = u_ref[rows, 3 * c:4 * c].astype(F32) * u_ref[rows, 4 * c:5 * c].astype(F32)
    _row_loop(t, 64, blk)


def _mix_fwd(u, x0, wa, ba, lg, lb, wb, w_out, *, name):
    s_len, d_in = u.shape
    d = x0.shape[1]
    c = D_CONF
    t = _seq_tile(s_len)
    per = t // HALO

    def body(u_ref, uh_ref, x0_ref, wa_ref, ba_ref, lg_ref, lb_ref, wb_ref, wo_ref, y_ref, x1_ref, ca, cb,
             gw, pw, wrep_a, wrep_b, shf):
        first = pl.program_id(0) == 0
        _mix_windows(u_ref, uh_ref, gw, pw, first, t)
        _replicate_taps(wa_ref, wrep_a, CONF_K)
        _replicate_taps(wb_ref, wrep_b, SHORT_K)
        _conv_taps(gw, wrep_a, ca, taps=CONF_K, n_rows=t, base=HALO, width=c, bias_ref=ba_ref, shf=shf)
        _conv_taps(pw, wrep_b, cb, taps=SHORT_K, n_rows=t, base=HALO3, width=c)

        def blk(rows):
            cv = ca[rows, :]
            mu = jnp.mean(cv, axis=-1, keepdims=True)
            xc = cv - mu
            var = jnp.mean(xc * xc, axis=-1, keepdims=True)
            ln = (xc * lax.rsqrt(var + EPS)) * lg_ref[...] + lb_ref[...]
            y_ref[rows, 0:c] = (ln * _sigmoid(ln)).astype(BF16)
            y_ref[rows, c:2 * c] = (u_ref[rows, 2 * c:3 * c].astype(F32) * cb[rows, :]).astype(BF16)
        half = t // 2
        rb = min(64, half)
        for lo in range(0, t, half):
            for r0 in range(lo, lo + half, rb):
                blk(pl.ds(r0, rb))
            x1_ref[lo:lo + half, :] = x0_ref[lo:lo + half, :] + _dot(y_ref[lo:lo + half, :], wo_ref[...])

    small = lambda r: _resident((r, c), lambda i: (0, 0))
    return pl.pallas_call(
        body, grid=(s_len // t,),
        in_specs=[pl.BlockSpec((t, d_in), lambda i: (i, 0)),
                  pl.BlockSpec((HALO, d_in), lambda i: (jnp.maximum(i * per - 1, 0), 0)),
                  pl.BlockSpec((t, d), lambda i: (i, 0)),
                  small(CONF_K), small(1), small(1), small(1), small(SHORT_K),
                  _resident((2 * c, d), lambda i: (0, 0))],
        out_specs=[pl.BlockSpec((t, 2 * c), lambda i: (i, 0)), pl.BlockSpec((t, d), lambda i: (i, 0)),
                   pl.BlockSpec((t, c), lambda i: (i, 0)), pl.BlockSpec((t, c), lambda i: (i, 0))],
        out_shape=[jax.ShapeDtypeStruct((s_len, 2 * c), BF16), jax.ShapeDtypeStruct((s_len, d), F32),
                   jax.ShapeDtypeStruct((s_len, c), F32), jax.ShapeDtypeStruct((s_len, c), F32)],
        scratch_shapes=[pltpu.VMEM((HALO + t, c), F32), pltpu.VMEM((HALO3 + t, c), F32),
                        pltpu.VMEM((CONF_K * SUB, c), F32), pltpu.VMEM((SHORT_K * SUB, c), F32),
                        pltpu.VMEM((SUB - 1, HALO + t, c), F32)],
        compiler_params=_params("arbitrary"), name=name,
    )(u, u, x0, wa, ba, lg, lb, wb, w_out)


def _ffn_fwd(uf, x1, wf, w_down, *, name):
    s_len, ff2 = uf.shape
    ff = ff2 // 2
    d = x1.shape[1]
    t = _seq_tile(s_len)
    fc = _ff_chunk(ff)
    nc = ff // fc
    per = t // HALO3_BLK
    half = t // 2
    rb = min(64, half)

    def body(ug_ref, ugh_ref, uv_ref, uvh_ref, x1_ref, wfg_ref, wfv_ref, wd_ref,
             act_ref, x2_ref, cg_ref, cv_ref, gwin, vwin, wrep_g, wrep_v):
        first = pl.program_id(0) == 0
        first_chunk = pl.program_id(1) == 0
        lo8 = HALO3_BLK - HALO3
        gwin[0:HALO3, :] = jnp.where(first, 0.0, ugh_ref[...].astype(F32)[lo8:HALO3_BLK])
        vwin[0:HALO3, :] = jnp.where(first, 0.0, uvh_ref[...].astype(F32)[lo8:HALO3_BLK])
        _replicate_taps(wfg_ref, wrep_g, SHORT_K)
        _replicate_taps(wfv_ref, wrep_v, SHORT_K)
        chunk_rows = pl.ds(pl.multiple_of(pl.program_id(1) * fc, fc), fc)

        def conv(win, wrep, r0, lanes):
            acc = None
            for k in range(SHORT_K):
                wk = jnp.tile(wrep[k * SUB:(k + 1) * SUB, lanes], (rb // SUB, 1))
                off = HALO3 + r0 + k - (SHORT_K - 1)
                term = wk * win[off:off + rb, lanes]
                acc = term if acc is None else acc + term
            return acc

        for lo in range(0, t, half):
            for r0 in range(lo, lo + half, rb):
                gwin[HALO3 + r0:HALO3 + r0 + rb, :] = ug_ref[r0:r0 + rb, :].astype(F32)
                vwin[HALO3 + r0:HALO3 + r0 + rb, :] = uv_ref[r0:r0 + rb, :].astype(F32)
            for cb in range(fc // LANES):
                lanes = slice(cb * LANES, (cb + 1) * LANES)
                for r0 in range(lo, lo + half, rb):
                    gv = conv(gwin, wrep_g, r0, lanes).astype(BF16)
                    vv = conv(vwin, wrep_v, r0, lanes).astype(BF16)
                    cg_ref[r0:r0 + rb, lanes] = gv
                    cv_ref[r0:r0 + rb, lanes] = vv
                    act_ref[r0:r0 + rb, lanes] = (gv * _sigmoid(gv)) * vv
            base = jnp.where(first_chunk, x1_ref[lo:lo + half, :], x2_ref[lo:lo + half, :])
            x2_ref[lo:lo + half, :] = base + _dot(act_ref[lo:lo + half, :], wd_ref[chunk_rows, :])

    halo_map = lambda off: (lambda i, j: (jnp.maximum(i * per - 1, 0), j + off))
    return pl.pallas_call(
        body, grid=(s_len // t, nc),
        in_specs=[pl.BlockSpec((t, fc), lambda i, j: (i, j)), pl.BlockSpec((HALO3_BLK, fc), halo_map(0)),
                  pl.BlockSpec((t, fc), lambda i, j: (i, j + nc)), pl.BlockSpec((HALO3_BLK, fc), halo_map(nc)),
                  pl.BlockSpec((t, d), lambda i, j: (i, 0)),
                  pl.BlockSpec((SHORT_K, fc), lambda i, j: (0, j)),
                  pl.BlockSpec((SHORT_K, fc), lambda i, j: (0, j + nc)),
                  _resident((ff, d), lambda i, j: (0, 0))],
        out_specs=[pl.BlockSpec((t, fc), lambda i, j: (i, j)), pl.BlockSpec((t, d), lambda i, j: (i, 0)),
                   pl.BlockSpec((t, fc), lambda i, j: (i, j)), pl.BlockSpec((t, fc), lambda i, j: (i, j))],
        out_shape=[jax.ShapeDtypeStruct((s_len, ff), BF16), jax.ShapeDtypeStruct((s_len, d), F32),
                   jax.ShapeDtypeStruct((s_len, ff), BF16), jax.ShapeDtypeStruct((s_len, ff), BF16)],
        scratch_shapes=[pltpu.VMEM((HALO3 + t, fc), F32), pltpu.VMEM((HALO3 + t, fc), F32),
                        pltpu.VMEM((SHORT_K * SUB, fc), F32), pltpu.VMEM((SHORT_K * SUB, fc), F32)],
        compiler_params=_params("parallel", "arbitrary"), name=name,
    )(uf, uf, uf, uf, x1, wf, wf, w_down)


def _loss_bwd(x, g, target, *, name):
    s_len, d = x.shape
    t = _seq_tile(s_len)

    def body(x_ref, g_ref, t_ref, l_ref, dx_ref, dxb_ref, dg_ref):
        @pl.when(pl.program_id(0) == 0)
        def _():
            l_ref[...] = jnp.zeros_like(l_ref)
            dg_ref[...] = jnp.zeros_like(dg_ref)

        def blk(rows):
            xv = x_ref[rows, :]
            r = lax.rsqrt(jnp.mean(xv * xv, axis=-1, keepdims=True) + EPS)
            xn = xv * r
            e = xn * g_ref[...] - t_ref[rows, :]
            l_ref[...] += _rows8(e * e)
            dy = e * (1.0 / d)
            dg_ref[...] += _rows8(dy * xn)
            dn = dy * g_ref[...]
            dx = r * (dn - xn * jnp.mean(dn * xn, axis=-1, keepdims=True))
            dx_ref[rows, :] = dx
            dxb_ref[rows, :] = dx.astype(BF16)
        _row_loop(t, 64, blk)

    row = pl.BlockSpec((t, d), lambda i: (i, 0))
    part = pl.BlockSpec((SUB, d), lambda i: (0, 0))
    return pl.pallas_call(
        body, grid=(s_len // t,),
        in_specs=[row, _resident((1, d), lambda i: (0, 0)), row],
        out_specs=[part, row, row, part],
        out_shape=[jax.ShapeDtypeStruct((SUB, d), F32), jax.ShapeDtypeStruct((s_len, d), F32),
                   jax.ShapeDtypeStruct((s_len, d), BF16), jax.ShapeDtypeStruct((SUB, d), F32)],
        compiler_params=_params("arbitrary"), name=name,
    )(x, g, target)


def _ffn_bwd(dx2, uf, cg, cv, wf, w_down, *, name, dep=None):
    s_len, ff2 = uf.shape
    ff = ff2 // 2
    d = dx2.shape[1]
    t = _seq_tile(s_len)
    n_t = s_len // t
    fc = _ff_chunk(ff)
    nc = ff // fc

    def body(dx_ref, ug_ref, uv_ref, cg_ref, cv_ref, wfg_ref, wfv_ref, wd_ref, dep_ref,
             duf_ref, dwg_ref, dwv_ref, dact, dgw, dvw, awg, awv, wrep_g, wrep_v):
        i = pl.program_id(1)

        @pl.when(i == 0)
        def _():
            dgw[t:t + HALO3, :] = jnp.zeros((HALO3, fc), F32)
            dvw[t:t + HALO3, :] = jnp.zeros((HALO3, fc), F32)
            awg[...] = jnp.zeros_like(awg)
            awv[...] = jnp.zeros_like(awv)

        _replicate_taps(wfg_ref, wrep_g, SHORT_K)
        _replicate_taps(wfv_ref, wrep_v, SHORT_K)

        def blk(rows):
            gv = cg_ref[rows, :]
            sg = _sigmoid(gv)
            da = dact[rows, :].astype(BF16)
            dgw[rows, :] = ((da * cv_ref[rows, :]) * (sg * (1.0 + gv * (1.0 - sg)))).astype(F32)
            dvw[rows, :] = (da * (gv * sg)).astype(F32)

        dact[...] = _dot_nt(dx_ref[...], wd_ref[...])
        _row_loop(t, 64, blk)

        _conv_bwd_taps(dgw, wrep_g, ug_ref, duf_ref.at[0], awg, taps=SHORT_K, n_rows=t, width=fc)
        _conv_bwd_taps(dvw, wrep_v, uv_ref, duf_ref.at[1], awv, taps=SHORT_K, n_rows=t, width=fc)
        dgw[t:t + HALO3, :] = dgw[0:HALO3, :]
        dvw[t:t + HALO3, :] = dvw[0:HALO3, :]

        @pl.when(i == n_t - 1)
        def _():
            dwg_ref[...] = _fold8(awg, SHORT_K)
            dwv_ref[...] = _fold8(awv, SHORT_K)

    rev = lambda i: n_t - 1 - i
    gate = pl.BlockSpec((t, fc), lambda j, i: (rev(i), j))
    value = pl.BlockSpec((t, fc), lambda j, i: (rev(i), j + nc))
    return pl.pallas_call(
        body, grid=(nc, n_t),
        in_specs=[pl.BlockSpec((t, d), lambda j, i: (rev(i), 0)), gate, value, gate, gate,
                  pl.BlockSpec((SHORT_K, fc), lambda j, i: (0, j)),
                  pl.BlockSpec((SHORT_K, fc), lambda j, i: (0, j + nc)),
                  pl.BlockSpec((fc, d), lambda j, i: (j, 0)), ANY],
        out_specs=[pl.BlockSpec((2, t, fc), lambda j, i: (0, rev(i), j)),
                   pl.BlockSpec((SHORT_K, fc), lambda j, i: (0, j)), pl.BlockSpec((SHORT_K, fc), lambda j, i: (0, j))],
        out_shape=[jax.ShapeDtypeStruct((2, s_len, ff), BF16),
                   jax.ShapeDtypeStruct((SHORT_K, ff), F32), jax.ShapeDtypeStruct((SHORT_K, ff), F32)],
        scratch_shapes=[pltpu.VMEM((t, fc), F32),
                        pltpu.VMEM((t + HALO3, fc), F32), pltpu.VMEM((t + HALO3, fc), F32),
                        pltpu.VMEM((SHORT_K * SUB, fc), F32), pltpu.VMEM((SHORT_K * SUB, fc), F32),
                        pltpu.VMEM((SHORT_K * SUB, fc), F32), pltpu.VMEM((SHORT_K * SUB, fc), F32)],
        compiler_params=_params("arbitrary", "arbitrary"), name=name,
    )(dx2, uf, uf, cg, cv, wf, wf, w_down, uf if dep is None else dep)


def _mix_bwd(dx1, u, ca, cb, wa, lg, lb, wb, w_out, *, name, dep=None):
    s_len, d_in = u.shape
    d = dx1.shape[1]
    c = D_CONF
    t = _seq_tile(s_len)
    n_t = s_len // t

    def body(dx_ref, u_ref, ca_ref, cb_ref, wa_ref, lg_ref, lb_ref, wb_ref, wo_ref, dep_ref,
             du_ref, dwa_ref, dwb_ref, dba_ref, dlg_ref, dlb_ref, dbin_ref,
             glu, prod, dyc, dcaw, dcbw, dglu, dp, awa, awb, wrep_a, wrep_b, shf):
        i = pl.program_id(0)
        _replicate_taps(wa_ref, wrep_a, CONF_K)
        _replicate_taps(wb_ref, wrep_b, SHORT_K)

        @pl.when(i == 0)
        def _():
            dcaw[t:t + HALO, :] = jnp.zeros((HALO, c), F32)
            dcbw[t:t + HALO3, :] = jnp.zeros((HALO3, c), F32)
            awa[...] = jnp.zeros_like(awa)
            awb[...] = jnp.zeros_like(awb)
            dba_ref[...] = jnp.zeros_like(dba_ref)
            dlg_ref[...] = jnp.zeros_like(dlg_ref)
            dlb_ref[...] = jnp.zeros_like(dlb_ref)
            dbin_ref[...] = jnp.zeros_like(dbin_ref)

        def blk1(rows):
            cv = ca_ref[rows, :]
            mu = jnp.mean(cv, axis=-1, keepdims=True)
            xc = cv - mu
            rstd = lax.rsqrt(jnp.mean(xc * xc, axis=-1, keepdims=True) + EPS)
            nrm = xc * rstd
            ln = nrm * lg_ref[...] + lb_ref[...]
            sg = _sigmoid(ln)
            dln = dyc[rows, 0:c] * (sg * (1.0 + ln * (1.0 - sg)))
            dlg_ref[...] += _rows8(dln * nrm)
            dlb_ref[...] += _rows8(dln)
            dn = dln * lg_ref[...]
            dca = rstd * (dn - jnp.mean(dn, axis=-1, keepdims=True)
                          - nrm * jnp.mean(dn * nrm, axis=-1, keepdims=True))
            dcaw[rows, :] = dca
            dba_ref[...] += _rows8(dca)
            ds = dyc[rows, c:2 * c]
            dgb = ds * cb_ref[rows, :]
            dcbw[rows, :] = ds * u_ref[rows, 2 * c:3 * c].astype(F32)
            du_ref[rows, 2 * c:3 * c] = dgb.astype(BF16)
            dbin_ref[:, 2 * c:3 * c] += _rows8(dgb)
            glu[rows, :] = u_ref[rows, 0:c].astype(F32) * _sigmoid(u_ref[rows, c:2 * c].astype(F32))
            prod[rows, :] = u_ref[rows, 3 * c:4 * c].astype(F32) * u_ref[rows, 4 * c:5 * c].astype(F32)
        half = t // 2
        rb = min(64, half)
        for lo in range(0, t, half):
            dyc[lo:lo + half, :] = _dot_nt(dx_ref[lo:lo + half, :], wo_ref[...])
            for r0 in range(lo, lo + half, rb):
                blk1(pl.ds(r0, rb))

        _conv_bwd_taps(dcaw, wrep_a, glu, dglu, awa, taps=CONF_K, n_rows=t, width=c, shf=shf)
        _conv_bwd_taps(dcbw, wrep_b, prod, dp, awb, taps=SHORT_K, n_rows=t, width=c)
        dcaw[t:t + HALO, :] = dcaw[0:HALO, :]
        dcbw[t:t + HALO3, :] = dcbw[0:HALO3, :]

        def blk2(rows):
            av = u_ref[rows, 0:c].astype(F32)
            sg = _sigmoid(u_ref[rows, c:2 * c].astype(F32))
            dg = dglu[rows, :]
            d_av = dg * sg
            d_ag = (dg * av) * (sg * (1.0 - sg))
            dpv = dp[rows, :]
            d_gc = dpv * u_ref[rows, 4 * c:5 * c].astype(F32)
            d_vs = dpv * u_ref[rows, 3 * c:4 * c].astype(F32)
            du_ref[rows, 0:c] = d_av.astype(BF16)
            du_ref[rows, c:2 * c] = d_ag.astype(BF16)
            du_ref[rows, 3 * c:4 * c] = d_gc.astype(BF16)
            du_ref[rows, 4 * c:5 * c] = d_vs.astype(BF16)
            dbin_ref[:, 0:c] += _rows8(d_av)
            dbin_ref[:, c:2 * c] += _rows8(d_ag)
            dbin_ref[:, 3 * c:4 * c] += _rows8(d_gc)
            dbin_ref[:, 4 * c:5 * c] += _rows8(d_vs)
        _row_loop(t, 64, blk2)

        @pl.when(i == n_t - 1)
        def _():
            dwa_ref[...] = _fold8(awa, CONF_K)
            dwb_ref[...] = _fold8(awb, SHORT_K)

    rev = lambda i: n_t - 1 - i
    small_in = lambda r: _resident((r, c), lambda i: (0, 0))
    small = lambda r: pl.BlockSpec((r, c), lambda i: (0, 0))
    return pl.pallas_call(
        body, grid=(n_t,),
        in_specs=[pl.BlockSpec((t, d), lambda i: (rev(i), 0)),
                  pl.BlockSpec((t, d_in), lambda i: (rev(i), 0)),
                  pl.BlockSpec((t, c), lambda i: (rev(i), 0)), pl.BlockSpec((t, c), lambda i: (rev(i), 0)),
                  small_in(CONF_K), small_in(1), small_in(1), small_in(SHORT_K),
                  _resident((2 * c, d), lambda i: (0, 0)), ANY],
        out_specs=[pl.BlockSpec((t, d_in), lambda i: (rev(i), 0)),
                   small(CONF_K), small(SHORT_K), small(SUB), small(SUB), small(SUB),
                   pl.BlockSpec((SUB, d_in), lambda i: (0, 0))],
        out_shape=[jax.ShapeDtypeStruct((s_len, d_in), BF16),
                   jax.ShapeDtypeStruct((CONF_K, c), F32), jax.ShapeDtypeStruct((SHORT_K, c), F32),
                   jax.ShapeDtypeStruct((SUB, c), F32), jax.ShapeDtypeStruct((SUB, c), F32),
                   jax.ShapeDtypeStruct((SUB, c), F32), jax.ShapeDtypeStruct((SUB, d_in), F32)],
        scratch_shapes=[pltpu.VMEM((t, c), F32), pltpu.VMEM((t, c), F32), pltpu.VMEM((t, 2 * c), F32),
                        pltpu.VMEM((t + HALO, c), F32), pltpu.VMEM((t + HALO3, c), F32),
                        pltpu.VMEM((t, c), F32), pltpu.VMEM((t, c), F32),
                        pltpu.VMEM((CONF_K * SUB, c), F32), pltpu.VMEM((SHORT_K * SUB, c), F32),
                        pltpu.VMEM((CONF_K * SUB, c), F32), pltpu.VMEM((SHORT_K * SUB, c), F32),
                        pltpu.VMEM((SUB - 1, t + HALO, c), F32)],
        compiler_params=_params("arbitrary"), name=name,
    )(dx1, u, ca, cb, wa, lg, lb, wb, w_out, u if dep is None else dep)


def _matmul_tn(a, b, *, name):
    n_p, s_len, k = a.shape
    n = b.shape[1]
    tk = _col_tile(k)
    per = k // tk
    if per > 2:
        def body(a_ref, b_ref, o_ref):
            o_ref[...] = _dot_tn(a_ref[...], b_ref[...]).astype(BF16)

        return pl.pallas_call(
            body, grid=(n_p, per),
            in_specs=[pl.BlockSpec((None, s_len, tk), lambda p, j: (p, 0, j)),
                      _resident((s_len, n), lambda p, j: (0, 0))],
            out_specs=pl.BlockSpec((tk, n), lambda p, j: (p * per + j, 0)),
            out_shape=jax.ShapeDtypeStruct((n_p * k, n), BF16),
            compiler_params=_params("parallel", "parallel"), name=name,
        )(a, b)

    half = s_len // 2

    def body_halves(a_ref, b_ref, o_ref, acc):
        @pl.when(pl.program_id(2) == 0)
        def _():
            acc[...] = _dot_tn(a_ref[...], b_ref[...])

        @pl.when(pl.program_id(2) == 1)
        def _():
            o_ref[...] = (acc[...] + _dot_tn(a_ref[...], b_ref[...])).astype(BF16)

    return pl.pallas_call(
        body_halves, grid=(n_p, per, 2),
        in_specs=[pl.BlockSpec((None, half, tk), lambda p, j, q: (p, q, j)),
                  pl.BlockSpec((half, n), lambda p, j, q: (q, 0))],
        out_specs=pl.BlockSpec((tk, n), lambda p, j, q: (p * per + j, 0)),
        out_shape=jax.ShapeDtypeStruct((n_p * k, n), BF16),
        scratch_shapes=[pltpu.VMEM((tk, n), F32)],
        compiler_params=_params("parallel", "parallel", "arbitrary"), name=name,
    )(a, b)


def _matmul_rmsbwd(dzs, wt, x, g, dx_in, *, name, dep=None):
    s_len, d = x.shape
    n_z, _, nj = dzs.shape
    t = _mm_tile(s_len)

    def body(*refs):
        dz_refs = refs[0:n_z]
        w_refs = refs[n_z:2 * n_z]
        x_ref, g_ref, dxi_ref, _, dx_ref, dxb_ref, dg_ref, dh = refs[2 * n_z:]

        @pl.when(pl.program_id(0) == 0)
        def _():
            dg_ref[...] = jnp.zeros_like(dg_ref)

        def blk(rows):
            xv = x_ref[rows, :]
            r = lax.rsqrt(jnp.mean(xv * xv, axis=-1, keepdims=True) + EPS)
            xn = xv * r
            dhv = dh[rows, :]
            dg_ref[...] += _rows8(dhv * xn)
            dn = dhv * g_ref[...]
            dx = dxi_ref[rows, :] + r * (dn - xn * jnp.mean(dn * xn, axis=-1, keepdims=True))
            dx_ref[rows, :] = dx
            dxb_ref[rows, :] = dx.astype(BF16)

        half = t // 2
        rb = min(128, half)
        for lo in range(0, t, half):
            acc = _dot(dz_refs[0][lo:lo + half, :], w_refs[0][...])
            for q in range(1, n_z):
                acc = acc + _dot(dz_refs[q][lo:lo + half, :], w_refs[q][...])
            dh[lo:lo + half, :] = acc
            for r0 in range(lo, lo + half, rb):
                blk(pl.ds(r0, rb))

    row = pl.BlockSpec((t, d), lambda i: (i, 0))
    in_specs = [pl.BlockSpec((None, t, nj), functools.partial(lambda q, i: (q, i, 0), q)) for q in range(n_z)]
    in_specs += [_resident((nj, d), functools.partial(lambda q, i: (q, 0), q)) for q in range(n_z)]
    in_specs += [row, _resident((1, d), lambda i: (0, 0)), row, ANY]
    return pl.pallas_call(
        body, grid=(s_len // t,), in_specs=in_specs,
        out_specs=[row, row, pl.BlockSpec((SUB, d), lambda i: (0, 0))],
        out_shape=[jax.ShapeDtypeStruct((s_len, d), F32), jax.ShapeDtypeStruct((s_len, d), BF16),
                   jax.ShapeDtypeStruct((SUB, d), F32)],
        scratch_shapes=[pltpu.VMEM((t, d), F32)],
        compiler_params=_params("arbitrary"), name=name,
    )(*([dzs] * n_z), *([wt] * n_z), x, g, dx_in, x if dep is None else dep)


def _row(v):
    return v.reshape(1, -1)


def _layer_fwd(x0, p, tag, dep=None, before_up=None):
    u, h1 = _rms_matmul(x0, _row(p["mix_norm_g"]), p["w_in_t"], _row(p["b_in"]), name=f"in_proj_{tag}", dep=dep)
    ycat, x1, ca, cb = _mix_fwd(u, x0, p["conv_a_w"], _row(p["conv_a_b"]), _row(p["ln_a_g"]), _row(p["ln_a_b"]),
                            p["conv_b_w"], p["w_out"], name=f"mix_fwd_{tag}")
    if before_up is not None:
        before_up(x1)
    uf, h2 = _rms_matmul(x1, _row(p["ffn_norm_g"]), p["w_up_t"], None, name=f"up_proj_{tag}")
    act, x2, cg, cv = _ffn_fwd(uf, x1, p["conv_f_w"], p["w_down"], name=f"ffn_fwd_{tag}")
    return x2, dict(x0=x0, h1=h1, u=u, ca=ca, cb=cb, ycat=ycat, x1=x1, h2=h2, uf=uf, cg=cg, cv=cv, act=act)


def _layer_bwd(dx2, dx2_b, p, saved, tag, ffn_grads, ffn_sent, mix_grads, mix_sent, dep=None):
    d_uf, dwf_g, dwf_v = _ffn_bwd(dx2_b, saved["uf"], saved["cg"], saved["cv"], p["conv_f_w"], p["w_down"],
                                  name=f"ffn_bwd_{tag}", dep=dep)
    g_down = _matmul_tn(saved["act"][None], dx2_b, name=f"dw_down_{tag}")
    g_up = _matmul_tn(d_uf, saved["h2"], name=f"dw_up_{tag}")
    dep_ffn = ffn_grads(dict(w_up=g_up, w_down=g_down), dx2_b)
    dx1, dx1_b, dg2 = _matmul_rmsbwd(d_uf, p["w_up_t"], saved["x1"], _row(p["ffn_norm_g"]), dx2,
                                     name=f"dh_ffn_{tag}", dep=dep_ffn)
    du, dwa, dwb, dba, dlg, dlb, dbin = _mix_bwd(
        dx1_b, saved["u"], saved["ca"], saved["cb"], p["conv_a_w"], _row(p["ln_a_g"]), _row(p["ln_a_b"]),
        p["conv_b_w"], p["w_out"], name=f"mix_bwd_{tag}", dep=ffn_sent(dx1_b))
    g_out = _matmul_tn(saved["ycat"][None], dx1_b, name=f"dw_out_{tag}")
    g_in = _matmul_tn(du[None], saved["h1"], name=f"dw_in_{tag}")
    conv = dict(conv_a_w=dwa, conv_b_w=dwb, conv_f_w=jnp.concatenate([dwf_g, dwf_v], axis=1))
    dep_mix = mix_grads(dict(w_in=g_in, w_out=g_out), conv, dx1_b)
    dx0, dx0_b, dg1 = _matmul_rmsbwd(du[None], p["w_in_t"], saved["x0"], _row(p["mix_norm_g"]), dx1,
                                     name=f"dh_mix_{tag}", dep=dep_mix)
    rep = dict(mix_norm_g=dg1, b_in=dbin, conv_a_b=dba, ln_a_g=dlg, ln_a_b=dlb, ffn_norm_g=dg2)
    return dx0, dx0_b, rep, mix_sent(dx0_b)


def _place():
    return lax.axis_index("x"), lax.axis_index("y"), lax.axis_index("c")


def _all_gather(arrs, *, name):
    n_a = len(arrs)

    def body(*refs):
        ins = refs[0:n_a]
        outs = refs[n_a:2 * n_a]
        send_sems, recv_sems, local_sems = refs[2 * n_a:]
        x, y, c = _place()
        sibling = (x, y, 1 - c)
        chips = [(1 - x, y), (x, 1 - y), (1 - x, 1 - y)]

        def slot(a, px, py, pc):
            return outs[a].at[4 * px + 2 * py + pc]

        def copy(a, k, block, to, src=None):
            return pltpu.make_async_remote_copy(
                src_ref=slot(a, *block) if src is None else src, dst_ref=slot(a, *block),
                send_sem=send_sems.at[a, k], recv_sem=recv_sems.at[a, k],
                device_id=to, device_id_type=MESH)

        me = (x, y, c)
        mine = [pltpu.make_async_copy(ins[a], slot(a, *me), local_sems.at[a]) for a in range(n_a)]
        for cp in mine:
            cp.start()
        started = []
        for a in range(n_a):
            first = [copy(a, 0, me, sibling, src=ins[a])]
            first += [copy(a, 1 + j, me, (*chip, c), src=ins[a]) for j, chip in enumerate(chips)]
            for cp in first:
                cp.start()
            started += first
        for a in range(n_a):
            for j, chip in enumerate(chips):
                copy(a, 1 + j, (*chip, c), me).wait_recv()
                passed = copy(a, 4 + j, (*chip, c), sibling)
                passed.start()
                started.append(passed)
        for a in range(n_a):
            copy(a, 0, sibling, me).wait_recv()
            for j, chip in enumerate(chips):
                copy(a, 4 + j, (*chip, 1 - c), me).wait_recv()
        for cp in started:
            cp.wait_send()
        for cp in mine:
            cp.wait()

    return pl.pallas_call(
        body, in_specs=[ANY] * n_a, out_specs=[ANY] * n_a,
        out_shape=[jax.ShapeDtypeStruct((N_DEV, *a.shape), a.dtype) for a in arrs],
        scratch_shapes=[pltpu.SemaphoreType.DMA((n_a, 7)), pltpu.SemaphoreType.DMA((n_a, 7)),
                        pltpu.SemaphoreType.DMA((n_a,))],
        name=name,
    )(*arrs)


def _row_tile(r, cap):
    for tr in range(min(cap, r) // 16 * 16, 0, -16):
        if r % tr == 0:
            return tr
    return r


def _pair_sum(mines, theirs, where, *, name):
    n_a = len(mines)
    n_chip = mines[0].shape[0]

    def body(where_ref, *refs):
        a_refs = refs[0:n_a]
        b_refs = refs[n_a:2 * n_a]
        p_refs = refs[2 * n_a:3 * n_a]
        l_refs = refs[3 * n_a:4 * n_a]
        q = pl.program_id(0)
        for a in range(n_a):
            p_refs[a][...] = (a_refs[a][...].astype(F32) + b_refs[a][...].astype(F32)).astype(p_refs[a].dtype)

        @pl.when(q == where_ref[1])
        def _():
            for a in range(n_a):
                l_refs[a][...] = p_refs[a][...]

    in_specs, out_p, out_l, shapes = [], [], [], []
    for m in mines:
        _, _, r, c = m.shape
        in_specs.append(pl.BlockSpec((None, None, r, c), lambda q, where_ref: (q, where_ref[0], 0, 0)))
    for m in mines:
        _, _, r, c = m.shape
        in_specs.append(pl.BlockSpec((None, r, c), lambda q, where_ref: (q, 0, 0)))
        out_p.append(pl.BlockSpec((None, r, c), lambda q, where_ref: (q, 0, 0)))
        out_l.append(pl.BlockSpec((None, r, c), lambda q, where_ref: (where_ref[1], 0, 0)))
        shapes.append(jax.ShapeDtypeStruct((n_chip, r, c), m.dtype))
    res = pl.pallas_call(
        body,
        grid_spec=pltpu.PrefetchScalarGridSpec(num_scalar_prefetch=1, grid=(n_chip,), in_specs=in_specs,
                                               out_specs=out_p + out_l),
        out_shape=shapes + shapes,
        compiler_params=_params("arbitrary"), name=name,
    )(where, *mines, *theirs)
    return list(res[:n_a]), list(res[n_a:])


HBM = pl.BlockSpec(memory_space=pltpu.HBM)
SEM = pl.BlockSpec(memory_space=pltpu.SEMAPHORE)
EFFECT = pltpu.SideEffectType.DATAFLOW_SIDE_EFFECTING


def _in_hbm(a):
    return pltpu.with_memory_space_constraint(a, pltpu.HBM)


def _split_start(srcs, lands, plan, n_copies, after, *, name):
    n_s, n_l = len(srcs), len(lands)

    def body(*refs):
        src_refs = refs[0:n_s]
        land_refs = refs[n_s:n_s + n_l]
        send_sems, recv_sems = refs[n_s + n_l + 1], refs[n_s + n_l + 2]
        token = refs[-1]
        for cp in plan(src_refs, land_refs, send_sems, recv_sems):
            cp.start()
        token[...] = jnp.zeros_like(token)

    thru = [pltpu.HBM(a.shape, a.dtype) for a in list(srcs) + list(lands)]
    res = pl.pallas_call(
        body, name=name,
        out_shape=(pltpu.SemaphoreType.DMA((n_copies,)), pltpu.SemaphoreType.DMA((n_copies,)), *thru,
                   jax.ShapeDtypeStruct((SUB, LANES), F32)),
        in_specs=[HBM] * (n_s + n_l) + [ANY],
        out_specs=(SEM, SEM, *([HBM] * (n_s + n_l)), pl.BlockSpec(memory_space=pltpu.VMEM)),
        input_output_aliases={i: 2 + i for i in range(n_s + n_l)},
        compiler_params=pltpu.CompilerParams(has_side_effects=EFFECT),
    )(*[_in_hbm(a) for a in srcs], *[_in_hbm(a) for a in lands], _in_hbm(after))
    return res[0], res[1], list(res[2:2 + n_s]), list(res[2 + n_s:2 + n_s + n_l]), res[-1]


def _split_wait(send_sems, recv_sems, srcs, lands, after, plan, *, name):
    n_s, n_l = len(srcs), len(lands)

    def body(*refs):
        src_refs = refs[0:n_s]
        land_refs = refs[n_s:n_s + n_l]
        send, recv = refs[n_s + n_l], refs[n_s + n_l + 1]
        for cp in plan(src_refs, land_refs, send, recv):
            cp.wait_send()
            cp.wait_recv()

    res = pl.pallas_call(
        body, name=name,
        out_shape=tuple(pltpu.HBM(a.shape, a.dtype) for a in list(srcs) + list(lands)),
        in_specs=[HBM] * (n_s + n_l) + [SEM, SEM, ANY],
        out_specs=tuple([HBM] * (n_s + n_l)),
        input_output_aliases={i: i for i in range(n_s + n_l)},
        compiler_params=pltpu.CompilerParams(has_side_effects=EFFECT),
    )(*srcs, *lands, send_sems, recv_sems, _in_hbm(after))
    return list(res[:n_s]), list(res[n_s:])


def _remote(src, dst, send_sems, recv_sems, k, to):
    return pltpu.make_async_remote_copy(src_ref=src, dst_ref=dst, send_sem=send_sems.at[k], recv_sem=recv_sems.at[k],
                                        device_id=to, device_id_type=MESH)


def _gather_plan_first(src_refs, land_refs, send_sems, recv_sems):
    x, y, c = _place()
    me = 4 * x + 2 * y + c
    peers = [(x, y, 1 - c), (1 - x, y, c), (x, 1 - y, c), (1 - x, 1 - y, c)]
    return [_remote(src, land.at[me], send_sems, recv_sems, 4 * a + k, to)
            for a, (src, land) in enumerate(zip(src_refs, land_refs)) for k, to in enumerate(peers)]


def _gather_plan_second(src_refs, land_refs, send_sems, recv_sems):
    x, y, c = _place()
    chips = [(1 - x, y), (x, 1 - y), (1 - x, 1 - y)]
    out = []
    for a, land in enumerate(land_refs):
        for j, (px, py) in enumerate(chips):
            slot = land.at[4 * px + 2 * py + c]
            out.append(_remote(slot, slot, send_sems, recv_sems, 3 * a + j, (x, y, 1 - c)))
    return out


def _siblings_plan(src_refs, land_refs, send_sems, recv_sems):
    x, y, c = _place()
    return [_remote(src.at[:, 1 - c], land, send_sems, recv_sems, a, (x, y, 1 - c))
            for a, (src, land) in enumerate(zip(src_refs, land_refs))]


def _chips_plan(src_refs, land_refs, send_sems, recv_sems):
    x, y, c = _place()
    my_chip = 2 * x + y
    chips = [(1 - x, y), (x, 1 - y), (1 - x, 1 - y)]
    return [_remote(src.at[2 * px + py], land.at[my_chip], send_sems, recv_sems, 3 * a + j, (px, py, c))
            for a, (src, land) in enumerate(zip(src_refs, land_refs)) for j, (px, py) in enumerate(chips)]


def _gather_landings(shards, me, *, name):
    blank = _unwritten([jax.ShapeDtypeStruct((N_DEV, *s.shape), s.dtype) for s in shards], name=name)
    return [lax.dynamic_update_index_in_dim(b, s, me, 0) for b, s in zip(blank, shards)]


def _adamw_math(g, w, m, v):
    m = ADAM_B1 * m + (1.0 - ADAM_B1) * g
    v = ADAM_B2 * v + (1.0 - ADAM_B2) * (g * g)
    m_hat = m / (1.0 - ADAM_B1 ** ADAM_STEP)
    v_hat = v / (1.0 - ADAM_B2 ** ADAM_STEP)
    delta = -ADAM_LR * (m_hat / (jnp.sqrt(v_hat) + ADAM_EPS) + ADAM_WD * w)
    return delta, m, v


def _adamw_sharded(parts, w, m, v, *, name, dep=None):
    n_layers, r, c = w.shape
    n_chip = parts[0].shape[0]
    tr = _row_tile(r, 384)
    n_i = r // tr

    def body(*refs):
        p_refs = refs[0:n_layers]
        w_ref, m_ref, v_ref, _, g_out, d_out, m_out, v_out = refs[n_layers:]
        layer = pl.program_id(0)
        for l in range(n_layers):
            @pl.when(layer == l)
            def _(l=l):
                g = p_refs[l][0].astype(F32)
                for q in range(1, n_chip):
                    g = g + p_refs[l][q].astype(F32)
                delta, m_new, v_new = _adamw_math(g, w_ref[...], m_ref[...], v_ref[...])
                g_out[...] = g
                d_out[...] = delta
                m_out[...] = m_new
                v_out[...] = v_new

    def part_map(l):
        return lambda layer, i: (0, jnp.where(layer == l, i, jnp.where(layer < l, 0, n_i - 1)), 0)

    blk = pl.BlockSpec((None, tr, c), lambda layer, i: (layer, i, 0))
    return pl.pallas_call(
        body, grid=(n_layers, n_i),
        in_specs=[pl.BlockSpec((n_chip, tr, c), part_map(l)) for l in range(n_layers)] + [blk, blk, blk, ANY],
        out_specs=[blk] * 4, out_shape=[jax.ShapeDtypeStruct((n_layers, r, c), F32)] * 4,
        compiler_params=_params("arbitrary", "arbitrary"), name=name,
    )(*parts, w, m, v, w if dep is None else dep)


def _fold_partials(cols, *, name):
    widths = [c.shape[1] for c in cols]

    def body(*refs):
        o_ref = refs[-1]
        pos = 0
        for ref, width in zip(refs[:-1], widths):
            o_ref[:, pos:pos + width] = jnp.sum(ref[...], axis=0, keepdims=True)
            pos += width

    return pl.pallas_call(body, out_shape=jax.ShapeDtypeStruct((1, sum(widths)), F32), name=name)(*cols)


def _adamw_replicated(parts, names, w, m, v, n_loss, *, name):
    n_dev = parts.shape[0]
    n_layers = w[names[0]].shape[0]
    every = list(names) + ["final_norm_g"]
    n_p = len(every)

    def body(*refs):
        p_ref = refs[0]
        w_refs = dict(zip(every, refs[1:1 + n_p]))
        m_refs = dict(zip(every, refs[1 + n_p:1 + 2 * n_p]))
        v_refs = dict(zip(every, refs[1 + 2 * n_p:1 + 3 * n_p]))
        l_out = refs[1 + 3 * n_p]
        outs = refs[2 + 3 * n_p:]
        o_refs = {n: outs[4 * q:4 * q + 4] for q, n in enumerate(every)}
        acc = p_ref[0]
        for q in range(1, n_dev):
            acc = acc + p_ref[q]
        tot = jnp.sum(acc, axis=0, keepdims=True)
        pos = 0
        where = [(n, l) for l in range(n_layers) for n in names] + [("final_norm_g", 0)]
        for n, l in where:
            width = w_refs[n].shape[1]
            g = tot[:, pos:pos + width]
            pos += width
            row = pl.ds(l, 1)
            delta, m_new, v_new = _adamw_math(g, w_refs[n][row, :], m_refs[n][row, :], v_refs[n][row, :])
            for o, val in zip(o_refs[n], (g, delta, m_new, v_new)):
                o[row, :] = val
        l_out[...] = (0.5 / n_loss) * jnp.sum(tot[:, pos:pos + n_loss], axis=-1, keepdims=True)

    shapes = [jax.ShapeDtypeStruct((1, 1), F32)]
    for n in every:
        shapes += [jax.ShapeDtypeStruct(w[n].shape, F32)] * 4
    res = pl.pallas_call(
        body, out_shape=shapes,
        compiler_params=pltpu.CompilerParams(vmem_limit_bytes=VMEM_LIMIT), name=name,
    )(parts, *[w[n] for n in every], *[m[n] for n in every], *[v[n] for n in every])
    return res[0], {n: res[1 + 4 * q:5 + 4 * q] for q, n in enumerate(every)}


BIG = ("w_in", "w_out", "w_up", "w_down")
COL_SHARDED = ("w_in", "w_up")
CONV = ("conv_a_w", "conv_b_w", "conv_f_w")
REPLICATED = ("mix_norm_g", "b_in", "conv_a_b", "ln_a_g", "ln_a_b", "ffn_norm_g")
KINDS = ("grad", "delta", "m", "v")
FFN_PART = ("w_up", "w_down")
MIX_PART = ("w_in", "w_out")


def _weights_from_gathered(g):
    n_dev, r, c = g.shape
    return g.reshape(n_dev * r, c)


def _slabs_from_full(grad):
    return grad.reshape(N_DEV, grad.shape[0] // N_DEV, grad.shape[1])


def _unwritten(like, *, name):
    return pl.pallas_call(lambda *refs: None, out_specs=[ANY] * len(like), out_shape=list(like), name=name)()


def kernel(x, mix_norm_g, w_in, b_in, conv_a_w, conv_a_b, ln_a_g, ln_a_b, conv_b_w, w_out, ffn_norm_g, w_up, conv_f_w, w_down, final_norm_g, loss_target, m_mix_norm_g, m_w_in, m_b_in, m_conv_a_w, m_conv_a_b, m_ln_a_g, m_ln_a_b, m_conv_b_w, m_w_out, m_ffn_norm_g, m_w_up, m_conv_f_w, m_w_down, m_final_norm_g, v_mix_norm_g, v_w_in, v_b_in, v_conv_a_w, v_conv_a_b, v_ln_a_g, v_ln_a_b, v_conv_b_w, v_w_out, v_ffn_norm_g, v_w_up, v_conv_f_w, v_w_down, v_final_norm_g):
    w = dict(mix_norm_g=mix_norm_g, w_in=w_in, b_in=b_in, conv_a_w=conv_a_w, conv_a_b=conv_a_b, ln_a_g=ln_a_g,
             ln_a_b=ln_a_b, conv_b_w=conv_b_w, w_out=w_out, ffn_norm_g=ffn_norm_g, w_up=w_up, conv_f_w=conv_f_w,
             w_down=w_down, final_norm_g=final_norm_g)
    m = dict(mix_norm_g=m_mix_norm_g, w_in=m_w_in, b_in=m_b_in, conv_a_w=m_conv_a_w, conv_a_b=m_conv_a_b,
             ln_a_g=m_ln_a_g, ln_a_b=m_ln_a_b, conv_b_w=m_conv_b_w, w_out=m_w_out, ffn_norm_g=m_ffn_norm_g,
             w_up=m_w_up, conv_f_w=m_conv_f_w, w_down=m_w_down, final_norm_g=m_final_norm_g)
    v = dict(mix_norm_g=v_mix_norm_g, w_in=v_w_in, b_in=v_b_in, conv_a_w=v_conv_a_w, conv_a_b=v_conv_a_b,
             ln_a_g=v_ln_a_g, ln_a_b=v_ln_a_b, conv_b_w=v_conv_b_w, w_out=v_w_out, ffn_norm_g=v_ffn_norm_g,
             w_up=v_w_up, conv_f_w=v_conv_f_w, w_down=v_w_down, final_norm_g=v_final_norm_g)
    order = list(w)
    n_layers = w_in.shape[0]
    xs = x[0]
    target = loss_target[0]
    flip = lambda a: jnp.transpose(a, (0, 2, 1))
    wt, mt, vt = ({n: flip(d[n]) if n in COL_SHARDED else d[n] for n in BIG} for d in (w, m, v))
    px, py, pc = _place()
    where = jnp.stack([pc, 2 * px + py]).astype(jnp.int32)
    me = 4 * px + 2 * py + pc

    assert BIG == MIX_PART + FFN_PART
    key = lambda n: n + "_t" if n in COL_SHARDED else n
    shard = lambda n, l: wt[n][l].astype(BF16)

    def gather_start(names, l, after, tag):
        shards = [shard(n, l) for n in names]
        lands = _gather_landings(shards, me, name=f"gather_landing_{tag}")
        return _split_start(shards, lands, _gather_plan_first, 4 * len(shards), after, name=f"gather_first_start_{tag}")

    def gather_mid(first, after, tag):
        return _split_wait(first[0], first[1], first[2], first[3], after, _gather_plan_first,
                           name=f"gather_first_wait_{tag}")[1]

    def forward_start(lands, after, tag):
        return _split_start([], lands, _gather_plan_second, 3 * len(lands), after, name=f"gather_second_start_{tag}")

    def forward_finish(second, after, tag):
        return _split_wait(second[0], second[1], [], second[3], after, _gather_plan_second,
                           name=f"gather_second_wait_{tag}")[1]

    gathered = _all_gather([shard(n, 0) for n in MIX_PART] + [w[n] for n in CONV], name="gather_weights_0")
    params = [{n: w[n][l] for n in REPLICATED} for l in range(n_layers)]
    for n, g in zip(CONV, gathered[len(MIX_PART):]):
        n_dev, _, taps, c = g.shape
        full = g.transpose(1, 2, 0, 3).reshape(n_layers, taps, n_dev * c)
        for l in range(n_layers):
            params[l][n] = full[l]
    for n, g in zip(MIX_PART, gathered):
        params[0][key(n)] = _weights_from_gathered(g)
    ffn_first = gather_start(FFN_PART, 0, gathered[0], "0_ffn")
    pending = {}

    h = xs
    saved = []
    for l in range(n_layers):
        nxt = l + 1 if l + 1 < n_layers else None

        def before_up(x1, l=l, nxt=nxt):
            if l == 0:
                second = forward_start(gather_mid(ffn_first, x1, "0_ffn"), x1, "0_ffn")
                after = second[4]
            else:
                second = pending[l]["ffn"]
                after = x1
            if nxt is not None:
                pending[nxt] = dict(first=gather_start(BIG, nxt, after, str(nxt)))
                after = pending[nxt]["first"][4]
            for n, g in zip(FFN_PART, forward_finish(second, after, f"{l}_ffn")):
                params[l][key(n)] = _weights_from_gathered(g)

        h, keep = _layer_fwd(h, params[l], str(l), dep=ffn_first[4] if l == 0 else None, before_up=before_up)
        saved.append(keep)
        if nxt is not None:
            arrived = gather_mid(pending[nxt]["first"], h, str(nxt))
            mix_second = forward_start(arrived[:len(MIX_PART)], h, f"{nxt}_mix")
            pending[nxt]["ffn"] = forward_start(arrived[len(MIX_PART):], mix_second[4], f"{nxt}_ffn")
            for n, g in zip(MIX_PART, forward_finish(mix_second, pending[nxt]["ffn"][4], f"{nxt}_mix")):
                params[nxt][key(n)] = _weights_from_gathered(g)

    def start_siblings(slabs, after, tag):
        mines = [s.reshape(N_CHIP, 2, *s.shape[1:]) for s in slabs]
        lands = _unwritten([jax.ShapeDtypeStruct((N_CHIP, *m.shape[2:]), m.dtype) for m in mines],
                           name=f"reduce_siblings_landing_{tag}")
        return _split_start(mines, lands, _siblings_plan, len(mines), after, name=f"reduce_siblings_start_{tag}")

    def start_chips(sib, after, tag):
        mines, theirs = _split_wait(sib[0], sib[1], sib[2], sib[3], after, _siblings_plan,
                                    name=f"reduce_siblings_wait_{tag}")
        pairs, lands = _pair_sum(mines, theirs, where, name=f"pair_sum_{tag}")
        return _split_start(pairs, lands, _chips_plan, 3 * len(pairs), after, name=f"reduce_chips_start_{tag}")

    def finish_reduce(fly, after, tag):
        return _split_wait(fly[0], fly[1], fly[2], fly[3], after, _chips_plan, name=f"reduce_chips_wait_{tag}")[1]

    loss_sq, dh, dh_b, dgf = _loss_bwd(h, _row(final_norm_g), target, name="loss")
    conv_g = {n: [None] * n_layers for n in CONV}
    rep_g = [None] * n_layers
    siblings = {}
    flights = {}
    token = None
    for l in reversed(range(n_layers)):
        def ffn_grads(g, after, l=l):
            siblings[l, "ffn"] = start_siblings([_slabs_from_full(g[n]) for n in FFN_PART], after, f"{l}_ffn")
            return siblings[l, "ffn"][4]

        def ffn_sent(after, l=l):
            flights[l, "ffn"] = start_chips(siblings[l, "ffn"], after, f"{l}_ffn")
            return flights[l, "ffn"][4]

        def mix_grads(g, conv, after, l=l):
            for n in CONV:
                conv_g[n][l] = conv[n]
            slabs = [_slabs_from_full(g[n]) for n in MIX_PART]
            if l == 0:
                for n in CONV:
                    full = jnp.stack(conv_g[n])
                    _, taps, c = full.shape
                    slabs.append(full.reshape(n_layers, taps, N_DEV, c // N_DEV).transpose(2, 0, 1, 3)
                                 .reshape(N_DEV, n_layers * taps, c // N_DEV))
            siblings[l, "mix"] = start_siblings(slabs, after, f"{l}_mix")
            return siblings[l, "mix"][4]

        def mix_sent(after, l=l):
            flights[l, "mix"] = start_chips(siblings[l, "mix"], after, f"{l}_mix")
            return flights[l, "mix"][4]

        dh, dh_b, rep_g[l], token = _layer_bwd(dh, dh_b, params[l], saved[l], str(l), ffn_grads, ffn_sent,
                                               mix_grads, mix_sent, dep=token)

    sums = {key: finish_reduce(fly, dh, f"{key[0]}_{key[1]}") for key, fly in flights.items() if key != (0, "mix")}
    out = {k: {} for k in KINDS}

    def adamw_big(names, part, dep):
        for q, n in enumerate(names):
            layer_parts = [sums[l, part][q] for l in range(n_layers)]
            res = _adamw_sharded(layer_parts, wt[n], mt[n], vt[n], name=f"adamw_{n}", dep=dep)
            for k, r in zip(KINDS, res):
                out[k][n] = flip(r) if n in COL_SHARDED else r

    adamw_big(FFN_PART, "ffn", token)

    rep_cols = [rep_g[l][n] for l in range(n_layers) for n in REPLICATED] + [dgf, loss_sq]
    rep_all = _all_gather([_fold_partials(rep_cols, name="fold_small")], name="gather_small")[0]
    with_final = lambda d: {**{n: d[n] for n in REPLICATED}, "final_norm_g": _row(d["final_norm_g"])}
    loss, rep_res = _adamw_replicated(rep_all, REPLICATED, with_final(w), with_final(m), with_final(v),
                                      loss_sq.shape[1], name="adamw_small")
    for n, res in rep_res.items():
        for k, r in zip(KINDS, res):
            out[k][n] = r.reshape(w[n].shape)

    last = finish_reduce(flights[0, "mix"], rep_res["b_in"][0], "0_mix")
    sums[0, "mix"] = last[:len(MIX_PART)]
    adamw_big(MIX_PART, "mix", None)
    for n, p in zip(CONV, last[len(MIX_PART):]):
        as_one = lambda a: a.reshape(1, *p.shape[1:])
        for k, r in zip(KINDS, _adamw_sharded([p], as_one(w[n]), as_one(m[n]), as_one(v[n]), name=f"adamw_{n}")):
            out[k][n] = r.reshape(w[n].shape)

    grad_x = dh.reshape(x.shape)
    return (loss.reshape(()), grad_x, *[out["grad"][n] for n in order], *[out["delta"][n] for n in order],
            *[out["m"][n] for n in order], *[out["v"][n] for n in order])
```

```python
import functools

import jax
import jax.numpy as jnp
from jax import lax
from jax.experimental import pallas as pl
from jax.experimental.pallas import tpu as pltpu

F32 = jnp.float32
BF16 = jnp.bfloat16

N_DEV = 8
N_CHIP = 4
D_CONF = 512
CONF_K = 31
SHORT_K = 3
EPS = 1e-6
HALO = 32
HALO3 = 8
HALO3_BLK = 16
LANES = 128
SUB = 8
VMEM_LIMIT = 56 * 1024 * 1024

ADAM_LR = 0.001
ADAM_B1 = 0.9
ADAM_B2 = 0.999
ADAM_EPS = 1e-08
ADAM_WD = 0.01
ADAM_STEP = 10

MESH = pl.DeviceIdType.MESH
ANY = pl.BlockSpec(memory_space=pl.ANY)


def _params(*sem):
    return pltpu.CompilerParams(dimension_semantics=sem, vmem_limit_bytes=VMEM_LIMIT)


def _resident(shape, index_map):
    return pl.BlockSpec(shape, index_map, pipeline_mode=pl.Buffered(1))


def _row_loop(n_rows, rb, fn, unroll=1):
    rb = min(rb, n_rows)

    def body(i, carry):
        fn(pl.ds(pl.multiple_of(i * rb, rb), rb))
        return carry
    lax.fori_loop(0, n_rows // rb, body, 0, unroll=unroll)


def _rows8(v):
    acc = v[0:SUB]
    for k in range(1, v.shape[0] // SUB):
        acc = acc + v[k * SUB:(k + 1) * SUB]
    return acc


def _sigmoid(z):
    return 0.5 * jnp.tanh(0.5 * z) + 0.5


def _dot(a, b):
    return jnp.dot(a, b, preferred_element_type=F32)


def _dot_nt(a, b):
    return lax.dot_general(a, b, (((1,), (1,)), ((), ())), preferred_element_type=F32)


def _dot_tn(a, b):
    return lax.dot_general(a, b, (((0,), (0,)), ((), ())), preferred_element_type=F32)


def _replicate_taps(w_ref, wrep, taps):
    for k in range(taps):
        wrep[pl.ds(k * SUB, SUB), :] = jnp.broadcast_to(w_ref[pl.ds(k, 1), :], (SUB, w_ref.shape[1]))


def _shift_copies(win, shf, lanes):
    span = win.shape[0] - SUB
    for r in range(1, SUB):
        for j0 in range(0, span, 64):
            n = min(64, span - j0)
            shf[r - 1, pl.ds(j0, n), lanes] = win[pl.ds(j0 + r, n), lanes]


def _rows_at(win, shf, off, rb, lanes):
    if shf is None or off % SUB == 0:
        return win[pl.ds(off, rb), lanes]
    return shf[off % SUB - 1, pl.ds(off - off % SUB, rb), lanes]


def _conv_taps(win, wrep, out, *, taps, n_rows, base, width, transposed=False, bias_ref=None, shf=None):
    rb = min(32 if taps > 8 else 64, n_rows)

    def lane_body(cb, carry):
        lanes = pl.ds(pl.multiple_of(cb * LANES, LANES), LANES)
        if shf is not None:
            _shift_copies(win, shf, lanes)
        for r0 in range(0, n_rows, rb):
            acc = None
            for k in range(taps):
                off = (taps - 1 - k) if transposed else (k - (taps - 1))
                wk = jnp.tile(wrep[pl.ds(k * SUB, SUB), lanes], (rb // SUB, 1))
                term = wk * _rows_at(win, shf, base + r0 + off, rb, lanes)
                acc = term if acc is None else acc + term
            if bias_ref is not None:
                acc = acc + bias_ref[:, lanes]
            out[pl.ds(r0, rb), lanes] = acc.astype(out.dtype)
        return carry

    lax.fori_loop(0, width // LANES, lane_body, 0)


def _conv_bwd_taps(win, wrep, x_cur, dx_out, dw_acc, *, taps, n_rows, width, shf=None):
    rb = min(32 if taps > 8 else 64, n_rows)

    def lane_body(cb, carry):
        lanes = pl.ds(pl.multiple_of(cb * LANES, LANES), LANES)
        if shf is not None:
            _shift_copies(win, shf, lanes)
        sums = [None] * taps
        for r0 in range(0, n_rows, rb):
            xv = x_cur[pl.ds(r0, rb), lanes].astype(F32)
            acc = None
            for k in range(taps):
                shifted = _rows_at(win, shf, r0 + taps - 1 - k, rb, lanes)
                term = jnp.tile(wrep[pl.ds(k * SUB, SUB), lanes], (rb // SUB, 1)) * shifted
                acc = term if acc is None else acc + term
                part = _rows8(xv * shifted)
                sums[k] = part if sums[k] is None else sums[k] + part
            dx_out[pl.ds(r0, rb), lanes] = acc.astype(dx_out.dtype)
        for k in range(taps):
            dw_acc[pl.ds(k * SUB, SUB), lanes] += sums[k]
        return carry

    lax.fori_loop(0, width // LANES, lane_body, 0)


def _fold8(acc_ref, taps):
    return jnp.concatenate(
        [jnp.sum(acc_ref[pl.ds(k * SUB, SUB), :], axis=0, keepdims=True) for k in range(taps)], axis=0)


def _seq_tile(s_len):
    return min(512, s_len)


def _mm_tile(s_len):
    return min(512, s_len)


def _ff_chunk(ff):
    best = LANES
    for c in range(LANES, 1408 + 1, LANES):
        if ff % c == 0:
            best = c
    return best


def _col_tile(n):
    for c in (512, 1408, 256, LANES):
        if n % c == 0:
            return c
    return n


def _rms_matmul(x, g, wt, b, *, name, dep=None):
    s_len, d = x.shape
    n = wt.shape[0]
    tm = _mm_tile(s_len)
    cn = _col_tile(n)
    has_bias = b is not None

    def body(*refs):
        x_ref, g_ref, w_ref = refs[0:3]
        b_ref = refs[3] if has_bias else None
        o_ref, h_ref = refs[-2:]

        def blk(rows):
            xv = x_ref[rows, :]
            r = lax.rsqrt(jnp.mean(xv * xv, axis=-1, keepdims=True) + EPS)
            h_ref[rows, :] = ((xv * r) * g_ref[...]).astype(BF16)

        rb = min(128, tm)
        for r0 in range(0, tm, rb):
            blk(pl.ds(r0, rb))
        for j in range(n // cn):
            acc = _dot_nt(h_ref[...], w_ref[j * cn:(j + 1) * cn, :])
            if has_bias:
                acc = acc + b_ref[:, j * cn:(j + 1) * cn]
            o_ref[:, j * cn:(j + 1) * cn] = acc.astype(BF16)

    in_specs = [pl.BlockSpec((tm, d), lambda i: (i, 0)), _resident((1, d), lambda i: (0, 0)),
                _resident((n, d), lambda i: (0, 0))]
    args = [x, g, wt]
    if has_bias:
        in_specs.append(_resident((1, n), lambda i: (0, 0)))
        args.append(b)
    in_specs.append(ANY)
    args.append(x if dep is None else dep)
    return pl.pallas_call(
        body, grid=(s_len // tm,), in_specs=in_specs,
        out_specs=[pl.BlockSpec((tm, n), lambda i: (i, 0)), pl.BlockSpec((tm, d), lambda i: (i, 0))],
        out_shape=[jax.ShapeDtypeStruct((s_len, n), BF16), jax.ShapeDtypeStruct((s_len, d), BF16)],
        compiler_params=_params("parallel"), name=name,
    )(*args)


def _mix_windows(u_ref, uh_ref, gw, pw, first, t):
    c = D_CONF
    uh = uh_ref[...].astype(F32)
    gw[0:HALO, :] = jnp.where(first, 0.0, uh[:, 0:c] * _sigmoid(uh[:, c:2 * c]))
    pw[0:HALO3, :] = jnp.where(first, 0.0, uh[HALO - HALO3:HALO, 3 * c:4 * c] * uh[HALO - HALO3:HALO, 4 * c:5 * c])

    def blk(rows):
        dst = pl.ds(pl.multiple_of(rows.start + HALO, SUB), rows.size)
        gw[dst, :] = u_ref[rows, 0:c].astype(F32) * _sigmoid(u_ref[rows, c:2 * c].astype(F32))
        dst3 = pl.ds(pl.multiple_of(rows.start + HALO3, SUB), rows.size)
        pw[dst3, :] = u_ref[rows, 3 * c:4 * c].astype(F32) * u_ref[rows, 4 * c:5 * c].astype(F32)
    _row_loop(t, 64, blk)


def _mix_fwd(u, x0, wa, ba, lg, lb, wb, w_out, *, name):
    s_len, d_in = u.shape
    d = x0.shape[1]
    c = D_CONF
    t = _seq_tile(s_len)
    per = t // HALO

    def body(u_ref, uh_ref, x0_ref, wa_ref, ba_ref, lg_ref, lb_ref, wb_ref, wo_ref, y_ref, x1_ref, ca, cb,
             gw, pw, wrep_a, wrep_b, shf):
        first = pl.program_id(0) == 0
        _mix_windows(u_ref, uh_ref, gw, pw, first, t)
        _replicate_taps(wa_ref, wrep_a, CONF_K)
        _replicate_taps(wb_ref, wrep_b, SHORT_K)
        _conv_taps(gw, wrep_a, ca, taps=CONF_K, n_rows=t, base=HALO, width=c, bias_ref=ba_ref, shf=shf)
        _conv_taps(pw, wrep_b, cb, taps=SHORT_K, n_rows=t, base=HALO3, width=c)

        def blk(rows):
            cv = ca[rows, :]
            mu = jnp.mean(cv, axis=-1, keepdims=True)
            xc = cv - mu
            var = jnp.mean(xc * xc, axis=-1, keepdims=True)
            ln = (xc * lax.rsqrt(var + EPS)) * lg_ref[...] + lb_ref[...]
            y_ref[rows, 0:c] = (ln * _sigmoid(ln)).astype(BF16)
            y_ref[rows, c:2 * c] = (u_ref[rows, 2 * c:3 * c].astype(F32) * cb[rows, :]).astype(BF16)
        half = t // 2
        rb = min(64, half)
        for lo in range(0, t, half):
            for r0 in range(lo, lo + half, rb):
                blk(pl.ds(r0, rb))
            x1_ref[lo:lo + half, :] = x0_ref[lo:lo + half, :] + _dot(y_ref[lo:lo + half, :], wo_ref[...])

    small = lambda r: _resident((r, c), lambda i: (0, 0))
    return pl.pallas_call(
        body, grid=(s_len // t,),
        in_specs=[pl.BlockSpec((t, d_in), lambda i: (i, 0)),
                  pl.BlockSpec((HALO, d_in), lambda i: (jnp.maximum(i * per - 1, 0), 0)),
                  pl.BlockSpec((t, d), lambda i: (i, 0)),
                  small(CONF_K), small(1), small(1), small(1), small(SHORT_K),
                  _resident((2 * c, d), lambda i: (0, 0))],
        out_specs=[pl.BlockSpec((t, 2 * c), lambda i: (i, 0)), pl.BlockSpec((t, d), lambda i: (i, 0)),
                   pl.BlockSpec((t, c), lambda i: (i, 0)), pl.BlockSpec((t, c), lambda i: (i, 0))],
        out_shape=[jax.ShapeDtypeStruct((s_len, 2 * c), BF16), jax.ShapeDtypeStruct((s_len, d), F32),
                   jax.ShapeDtypeStruct((s_len, c), F32), jax.ShapeDtypeStruct((s_len, c), F32)],
        scratch_shapes=[pltpu.VMEM((HALO + t, c), F32), pltpu.VMEM((HALO3 + t, c), F32),
                        pltpu.VMEM((CONF_K * SUB, c), F32), pltpu.VMEM((SHORT_K * SUB, c), F32),
                        pltpu.VMEM((SUB - 1, HALO + t, c), F32)],
        compiler_params=_params("arbitrary"), name=name,
    )(u, u, x0, wa, ba, lg, lb, wb, w_out)


def _ffn_fwd(uf, x1, wf, w_down, *, name):
    s_len, ff2 = uf.shape
    ff = ff2 // 2
    d = x1.shape[1]
    t = _seq_tile(s_len)
    fc = _ff_chunk(ff)
    nc = ff // fc
    per = t // HALO3_BLK
    half = t // 2
    rb = min(64, half)

    def body(ug_ref, ugh_ref, uv_ref, uvh_ref, x1_ref, wfg_ref, wfv_ref, wd_ref,
             act_ref, x2_ref, cg_ref, cv_ref, gwin, vwin, wrep_g, wrep_v):
        first = pl.program_id(0) == 0
        first_chunk = pl.program_id(1) == 0
        lo8 = HALO3_BLK - HALO3
        gwin[0:HALO3, :] = jnp.where(first, 0.0, ugh_ref[...].astype(F32)[lo8:HALO3_BLK])
        vwin[0:HALO3, :] = jnp.where(first, 0.0, uvh_ref[...].astype(F32)[lo8:HALO3_BLK])
        _replicate_taps(wfg_ref, wrep_g, SHORT_K)
        _replicate_taps(wfv_ref, wrep_v, SHORT_K)
        chunk_rows = pl.ds(pl.multiple_of(pl.program_id(1) * fc, fc), fc)

        def conv(win, wrep, r0, lanes):
            acc = None
            for k in range(SHORT_K):
                wk = jnp.tile(wrep[k * SUB:(k + 1) * SUB, lanes], (rb // SUB, 1))
                off = HALO3 + r0 + k - (SHORT_K - 1)
                term = wk * win[off:off + rb, lanes]
                acc = term if acc is None else acc + term
            return acc

        for lo in range(0, t, half):
            for r0 in range(lo, lo + half, rb):
                gwin[HALO3 + r0:HALO3 + r0 + rb, :] = ug_ref[r0:r0 + rb, :].astype(F32)
                vwin[HALO3 + r0:HALO3 + r0 + rb, :] = uv_ref[r0:r0 + rb, :].astype(F32)
            for cb in range(fc // LANES):
                lanes = slice(cb * LANES, (cb + 1) * LANES)
                for r0 in range(lo, lo + half, rb):
                    gv = conv(gwin, wrep_g, r0, lanes).astype(BF16)
                    vv = conv(vwin, wrep_v, r0, lanes).astype(BF16)
                    cg_ref[r0:r0 + rb, lanes] = gv
                    cv_ref[r0:r0 + rb, lanes] = vv
                    act_ref[r0:r0 + rb, lanes] = (gv * _sigmoid(gv)) * vv
            base = jnp.where(first_chunk, x1_ref[lo:lo + half, :], x2_ref[lo:lo + half, :])
            x2_ref[lo:lo + half, :] = base + _dot(act_ref[lo:lo + half, :], wd_ref[chunk_rows, :])

    halo_map = lambda off: (lambda i, j: (jnp.maximum(i * per - 1, 0), j + off))
    return pl.pallas_call(
        body, grid=(s_len // t, nc),
        in_specs=[pl.BlockSpec((t, fc), lambda i, j: (i, j)), pl.BlockSpec((HALO3_BLK, fc), halo_map(0)),
                  pl.BlockSpec((t, fc), lambda i, j: (i, j + nc)), pl.BlockSpec((HALO3_BLK, fc), halo_map(nc)),
                  pl.BlockSpec((t, d), lambda i, j: (i, 0)),
                  pl.BlockSpec((SHORT_K, fc), lambda i, j: (0, j)),
                  pl.BlockSpec((SHORT_K, fc), lambda i, j: (0, j + nc)),
                  _resident((ff, d), lambda i, j: (0, 0))],
        out_specs=[pl.BlockSpec((t, fc), lambda i, j: (i, j)), pl.BlockSpec((t, d), lambda i, j: (i, 0)),
                   pl.BlockSpec((t, fc), lambda i, j: (i, j)), pl.BlockSpec((t, fc), lambda i, j: (i, j))],
        out_shape=[jax.ShapeDtypeStruct((s_len, ff), BF16), jax.ShapeDtypeStruct((s_len, d), F32),
                   jax.ShapeDtypeStruct((s_len, ff), BF16), jax.ShapeDtypeStruct((s_len, ff), BF16)],
        scratch_shapes=[pltpu.VMEM((HALO3 + t, fc), F32), pltpu.VMEM((HALO3 + t, fc), F32),
                        pltpu.VMEM((SHORT_K * SUB, fc), F32), pltpu.VMEM((SHORT_K * SUB, fc), F32)],
        compiler_params=_params("parallel", "arbitrary"), name=name,
    )(uf, uf, uf, uf, x1, wf, wf, w_down)


def _loss_bwd(x, g, target, *, name):
    s_len, d = x.shape
    t = _seq_tile(s_len)

    def body(x_ref, g_ref, t_ref, l_ref, dx_ref, dxb_ref, dg_ref):
        @pl.when(pl.program_id(0) == 0)
        def _():
            l_ref[...] = jnp.zeros_like(l_ref)
            dg_ref[...] = jnp.zeros_like(dg_ref)

        def blk(rows):
            xv = x_ref[rows, :]
            r = lax.rsqrt(jnp.mean(xv * xv, axis=-1, keepdims=True) + EPS)
            xn = xv * r
            e = xn * g_ref[...] - t_ref[rows, :]
            l_ref[...] += _rows8(e * e)
            dy = e * (1.0 / d)
            dg_ref[...] += _rows8(dy * xn)
            dn = dy * g_ref[...]
            dx = r * (dn - xn * jnp.mean(dn * xn, axis=-1, keepdims=True))
            dx_ref[rows, :] = dx
            dxb_ref[rows, :] = dx.astype(BF16)
        _row_loop(t, 64, blk)

    row = pl.BlockSpec((t, d), lambda i: (i, 0))
    part = pl.BlockSpec((SUB, d), lambda i: (0, 0))
    return pl.pallas_call(
        body, grid=(s_len // t,),
        in_specs=[row, _resident((1, d), lambda i: (0, 0)), row],
        out_specs=[part, row, row, part],
        out_shape=[jax.ShapeDtypeStruct((SUB, d), F32), jax.ShapeDtypeStruct((s_len, d), F32),
                   jax.ShapeDtypeStruct((s_len, d), BF16), jax.ShapeDtypeStruct((SUB, d), F32)],
        compiler_params=_params("arbitrary"), name=name,
    )(x, g, target)


def _ffn_bwd(dx2, uf, cg, cv, wf, w_down, *, name, dep=None):
    s_len, ff2 = uf.shape
    ff = ff2 // 2
    d = dx2.shape[1]
    t = _seq_tile(s_len)
    n_t = s_len // t
    fc = _ff_chunk(ff)
    nc = ff // fc

    def body(dx_ref, ug_ref, uv_ref, cg_ref, cv_ref, wfg_ref, wfv_ref, wd_ref, dep_ref,
             duf_ref, dwg_ref, dwv_ref, dact, dgw, dvw, awg, awv, wrep_g, wrep_v):
        i = pl.program_id(1)

        @pl.when(i == 0)
        def _():
            dgw[t:t + HALO3, :] = jnp.zeros((HALO3, fc), F32)
            dvw[t:t + HALO3, :] = jnp.zeros((HALO3, fc), F32)
            awg[...] = jnp.zeros_like(awg)
            awv[...] = jnp.zeros_like(awv)

        _replicate_taps(wfg_ref, wrep_g, SHORT_K)
        _replicate_taps(wfv_ref, wrep_v, SHORT_K)

        def blk(rows):
            gv = cg_ref[rows, :]
            sg = _sigmoid(gv)
            da = dact[rows, :].astype(BF16)
            dgw[rows, :] = ((da * cv_ref[rows, :]) * (sg * (1.0 + gv * (1.0 - sg)))).astype(F32)
            dvw[rows, :] = (da * (gv * sg)).astype(F32)

        dact[...] = _dot_nt(dx_ref[...], wd_ref[...])
        _row_loop(t, 64, blk)

        _conv_bwd_taps(dgw, wrep_g, ug_ref, duf_ref.at[0], awg, taps=SHORT_K, n_rows=t, width=fc)
        _conv_bwd_taps(dvw, wrep_v, uv_ref, duf_ref.at[1], awv, taps=SHORT_K, n_rows=t, width=fc)
        dgw[t:t + HALO3, :] = dgw[0:HALO3, :]
        dvw[t:t + HALO3, :] = dvw[0:HALO3, :]

        @pl.when(i == n_t - 1)
        def _():
            dwg_ref[...] = _fold8(awg, SHORT_K)
            dwv_ref[...] = _fold8(awv, SHORT_K)

    rev = lambda i: n_t - 1 - i
    gate = pl.BlockSpec((t, fc), lambda j, i: (rev(i), j))
    value = pl.BlockSpec((t, fc), lambda j, i: (rev(i), j + nc))
    return pl.pallas_call(
        body, grid=(nc, n_t),
        in_specs=[pl.BlockSpec((t, d), lambda j, i: (rev(i), 0)), gate, value, gate, gate,
                  pl.BlockSpec((SHORT_K, fc), lambda j, i: (0, j)),
                  pl.BlockSpec((SHORT_K, fc), lambda j, i: (0, j + nc)),
                  pl.BlockSpec((fc, d), lambda j, i: (j, 0)), ANY],
        out_specs=[pl.BlockSpec((2, t, fc), lambda j, i: (0, rev(i), j)),
                   pl.BlockSpec((SHORT_K, fc), lambda j, i: (0, j)), pl.BlockSpec((SHORT_K, fc), lambda j, i: (0, j))],
        out_shape=[jax.ShapeDtypeStruct((2, s_len, ff), BF16),
                   jax.ShapeDtypeStruct((SHORT_K, ff), F32), jax.ShapeDtypeStruct((SHORT_K, ff), F32)],
        scratch_shapes=[pltpu.VMEM((t, fc), F32),
                        pltpu.VMEM((t + HALO3, fc), F32), pltpu.VMEM((t + HALO3, fc), F32),
                        pltpu.VMEM((SHORT_K * SUB, fc), F32), pltpu.VMEM((SHORT_K * SUB, fc), F32),
                        pltpu.VMEM((SHORT_K * SUB, fc), F32), pltpu.VMEM((SHORT_K * SUB, fc), F32)],
        compiler_params=_params("arbitrary", "arbitrary"), name=name,
    )(dx2, uf, uf, cg, cv, wf, wf, w_down, uf if dep is None else dep)


def _mix_bwd(dx1, u, ca, cb, wa, lg, lb, wb, w_out, *, name, dep=None):
    s_len, d_in = u.shape
    d = dx1.shape[1]
    c = D_CONF
    t = _seq_tile(s_len)
    n_t = s_len // t

    def body(dx_ref, u_ref, ca_ref, cb_ref, wa_ref, lg_ref, lb_ref, wb_ref, wo_ref, dep_ref,
             du_ref, dwa_ref, dwb_ref, dba_ref, dlg_ref, dlb_ref, dbin_ref,
             glu, prod, dyc, dcaw, dcbw, dglu, dp, awa, awb, wrep_a, wrep_b, shf):
        i = pl.program_id(0)
        _replicate_taps(wa_ref, wrep_a, CONF_K)
        _replicate_taps(wb_ref, wrep_b, SHORT_K)

        @pl.when(i == 0)
        def _():
            dcaw[t:t + HALO, :] = jnp.zeros((HALO, c), F32)
            dcbw[t:t + HALO3, :] = jnp.zeros((HALO3, c), F32)
            awa[...] = jnp.zeros_like(awa)
            awb[...] = jnp.zeros_like(awb)
            dba_ref[...] = jnp.zeros_like(dba_ref)
            dlg_ref[...] = jnp.zeros_like(dlg_ref)
            dlb_ref[...] = jnp.zeros_like(dlb_ref)
            dbin_ref[...] = jnp.zeros_like(dbin_ref)

        def blk1(rows):
            cv = ca_ref[rows, :]
            mu = jnp.mean(cv, axis=-1, keepdims=True)
            xc = cv - mu
            rstd = lax.rsqrt(jnp.mean(xc * xc, axis=-1, keepdims=True) + EPS)
            nrm = xc * rstd
            ln = nrm * lg_ref[...] + lb_ref[...]
            sg = _sigmoid(ln)
            dln = dyc[rows, 0:c] * (sg * (1.0 + ln * (1.0 - sg)))
            dlg_ref[...] += _rows8(dln * nrm)
            dlb_ref[...] += _rows8(dln)
            dn = dln * lg_ref[...]
            dca = rstd * (dn - jnp.mean(dn, axis=-1, keepdims=True)
                          - nrm * jnp.mean(dn * nrm, axis=-1, keepdims=True))
            dcaw[rows, :] = dca
            dba_ref[...] += _rows8(dca)
            ds = dyc[rows, c:2 * c]
            dgb = ds * cb_ref[rows, :]
            dcbw[rows, :] = ds * u_ref[rows, 2 * c:3 * c].astype(F32)
            du_ref[rows, 2 * c:3 * c] = dgb.astype(BF16)
            dbin_ref[:, 2 * c:3 * c] += _rows8(dgb)
            glu[rows, :] = u_ref[rows, 0:c].astype(F32) * _sigmoid(u_ref[rows, c:2 * c].astype(F32))
            prod[rows, :] = u_ref[rows, 3 * c:4 * c].astype(F32) * u_ref[rows, 4 * c:5 * c].astype(F32)
        half = t // 2
        rb = min(64, half)
        for lo in range(0, t, half):
            dyc[lo:lo + half, :] = _dot_nt(dx_ref[lo:lo + half, :], wo_ref[...])
            for r0 in range(lo, lo + half, rb):
                blk1(pl.ds(r0, rb))

        _conv_bwd_taps(dcaw, wrep_a, glu, dglu, awa, taps=CONF_K, n_rows=t, width=c, shf=shf)
        _conv_bwd_taps(dcbw, wrep_b, prod, dp, awb, taps=SHORT_K, n_rows=t, width=c)
        dcaw[t:t + HALO, :] = dcaw[0:HALO, :]
        dcbw[t:t + HALO3, :] = dcbw[0:HALO3, :]

        def blk2(rows):
            av = u_ref[rows, 0:c].astype(F32)
            sg = _sigmoid(u_ref[rows, c:2 * c].astype(F32))
            dg = dglu[rows, :]
            d_av = dg * sg
            d_ag = (dg * av) * (sg * (1.0 - sg))
            dpv = dp[rows, :]
            d_gc = dpv * u_ref[rows, 4 * c:5 * c].astype(F32)
            d_vs = dpv * u_ref[rows, 3 * c:4 * c].astype(F32)
            du_ref[rows, 0:c] = d_av.astype(BF16)
            du_ref[rows, c:2 * c] = d_ag.astype(BF16)
            du_ref[rows, 3 * c:4 * c] = d_gc.astype(BF16)
            du_ref[rows, 4 * c:5 * c] = d_vs.astype(BF16)
            dbin_ref[:, 0:c] += _rows8(d_av)
            dbin_ref[:, c:2 * c] += _rows8(d_ag)
            dbin_ref[:, 3 * c:4 * c] += _rows8(d_gc)
            dbin_ref[:, 4 * c:5 * c] += _rows8(d_vs)
        _row_loop(t, 64, blk2)

        @pl.when(i == n_t - 1)
        def _():
            dwa_ref[...] = _fold8(awa, CONF_K)
            dwb_ref[...] = _fold8(awb, SHORT_K)

    rev = lambda i: n_t - 1 - i
    small_in = lambda r: _resident((r, c), lambda i: (0, 0))
    small = lambda r: pl.BlockSpec((r, c), lambda i: (0, 0))
    return pl.pallas_call(
        body, grid=(n_t,),
        in_specs=[pl.BlockSpec((t, d), lambda i: (rev(i), 0)),
                  pl.BlockSpec((t, d_in), lambda i: (rev(i), 0)),
                  pl.BlockSpec((t, c), lambda i: (rev(i), 0)), pl.BlockSpec((t, c), lambda i: (rev(i), 0)),
                  small_in(CONF_K), small_in(1), small_in(1), small_in(SHORT_K),
                  _resident((2 * c, d), lambda i: (0, 0)), ANY],
        out_specs=[pl.BlockSpec((t, d_in), lambda i: (rev(i), 0)),
                   small(CONF_K), small(SHORT_K), small(SUB), small(SUB), small(SUB),
                   pl.BlockSpec((SUB, d_in), lambda i: (0, 0))],
        out_shape=[jax.ShapeDtypeStruct((s_len, d_in), BF16),
                   jax.ShapeDtypeStruct((CONF_K, c), F32), jax.ShapeDtypeStruct((SHORT_K, c), F32),
                   jax.ShapeDtypeStruct((SUB, c), F32), jax.ShapeDtypeStruct((SUB, c), F32),
                   jax.ShapeDtypeStruct((SUB, c), F32), jax.ShapeDtypeStruct((SUB, d_in), F32)],
        scratch_shapes=[pltpu.VMEM((t, c), F32), pltpu.VMEM((t, c), F32), pltpu.VMEM((t, 2 * c), F32),
                        pltpu.VMEM((t + HALO, c), F32), pltpu.VMEM((t + HALO3, c), F32),
                        pltpu.VMEM((t, c), F32), pltpu.VMEM((t, c), F32),
                        pltpu.VMEM((CONF_K * SUB, c), F32), pltpu.VMEM((SHORT_K * SUB, c), F32),
                        pltpu.VMEM((CONF_K * SUB, c), F32), pltpu.VMEM((SHORT_K * SUB, c), F32),
                        pltpu.VMEM((SUB - 1, t + HALO, c), F32)],
        compiler_params=_params("arbitrary"), name=name,
    )(dx1, u, ca, cb, wa, lg, lb, wb, w_out, u if dep is None else dep)


def _matmul_tn(a, b, *, name):
    n_p, s_len, k = a.shape
    n = b.shape[1]
    tk = _col_tile(k)
    per = k // tk
    if per > 2:
        def body(a_ref, b_ref, o_ref):
            o_ref[...] = _dot_tn(a_ref[...], b_ref[...]).astype(BF16)

        return pl.pallas_call(
            body, grid=(n_p, per),
            in_specs=[pl.BlockSpec((None, s_len, tk), lambda p, j: (p, 0, j)),
                      _resident((s_len, n), lambda p, j: (0, 0))],
            out_specs=pl.BlockSpec((tk, n), lambda p, j: (p * per + j, 0)),
            out_shape=jax.ShapeDtypeStruct((n_p * k, n), BF16),
            compiler_params=_params("parallel", "parallel"), name=name,
        )(a, b)

    half = s_len // 2

    def body_halves(a_ref, b_ref, o_ref, acc):
        @pl.when(pl.program_id(2) == 0)
        def _():
            acc[...] = _dot_tn(a_ref[...], b_ref[...])

        @pl.when(pl.program_id(2) == 1)
        def _():
            o_ref[...] = (acc[...] + _dot_tn(a_ref[...], b_ref[...])).astype(BF16)

    return pl.pallas_call(
        body_halves, grid=(n_p, per, 2),
        in_specs=[pl.BlockSpec((None, half, tk), lambda p, j, q: (p, q, j)),
                  pl.BlockSpec((half, n), lambda p, j, q: (q, 0))],
        out_specs=pl.BlockSpec((tk, n), lambda p, j, q: (p * per + j, 0)),
        out_shape=jax.ShapeDtypeStruct((n_p * k, n), BF16),
        scratch_shapes=[pltpu.VMEM((tk, n), F32)],
        compiler_params=_params("parallel", "parallel", "arbitrary"), name=name,
    )(a, b)


def _matmul_rmsbwd(dzs, wt, x, g, dx_in, *, name, dep=None):
    s_len, d = x.shape
    n_z, _, nj = dzs.shape
    t = _mm_tile(s_len)

    def body(*refs):
        dz_refs = refs[0:n_z]
        w_refs = refs[n_z:2 * n_z]
        x_ref, g_ref, dxi_ref, _, dx_ref, dxb_ref, dg_ref, dh = refs[2 * n_z:]

        @pl.when(pl.program_id(0) == 0)
        def _():
            dg_ref[...] = jnp.zeros_like(dg_ref)

        def blk(rows):
            xv = x_ref[rows, :]
            r = lax.rsqrt(jnp.mean(xv * xv, axis=-1, keepdims=True) + EPS)
            xn = xv * r
            dhv = dh[rows, :]
            dg_ref[...] += _rows8(dhv * xn)
            dn = dhv * g_ref[...]
            dx = dxi_ref[rows, :] + r * (dn - xn * jnp.mean(dn * xn, axis=-1, keepdims=True))
            dx_ref[rows, :] = dx
            dxb_ref[rows, :] = dx.astype(BF16)

        half = t // 2
        rb = min(128, half)
        for lo in range(0, t, half):
            acc = _dot(dz_refs[0][lo:lo + half, :], w_refs[0][...])
            for q in range(1, n_z):
                acc = acc + _dot(dz_refs[q][lo:lo + half, :], w_refs[q][...])
            dh[lo:lo + half, :] = acc
            for r0 in range(lo, lo + half, rb):
                blk(pl.ds(r0, rb))

    row = pl.BlockSpec((t, d), lambda i: (i, 0))
    in_specs = [pl.BlockSpec((None, t, nj), functools.partial(lambda q, i: (q, i, 0), q)) for q in range(n_z)]
    in_specs += [_resident((nj, d), functools.partial(lambda q, i: (q, 0), q)) for q in range(n_z)]
    in_specs += [row, _resident((1, d), lambda i: (0, 0)), row, ANY]
    return pl.pallas_call(
        body, grid=(s_len // t,), in_specs=in_specs,
        out_specs=[row, row, pl.BlockSpec((SUB, d), lambda i: (0, 0))],
        out_shape=[jax.ShapeDtypeStruct((s_len, d), F32), jax.ShapeDtypeStruct((s_len, d), BF16),
                   jax.ShapeDtypeStruct((SUB, d), F32)],
        scratch_shapes=[pltpu.VMEM((t, d), F32)],
        compiler_params=_params("arbitrary"), name=name,
    )(*([dzs] * n_z), *([wt] * n_z), x, g, dx_in, x if dep is None else dep)


def _row(v):
    return v.reshape(1, -1)


def _layer_fwd(x0, p, tag, dep=None, before_up=None):
    u, h1 = _rms_matmul(x0, _row(p["mix_norm_g"]), p["w_in_t"], _row(p["b_in"]), name=f"in_proj_{tag}", dep=dep)
    ycat, x1, ca, cb = _mix_fwd(u, x0, p["conv_a_w"], _row(p["conv_a_b"]), _row(p["ln_a_g"]), _row(p["ln_a_b"]),
                            p["conv_b_w"], p["w_out"], name=f"mix_fwd_{tag}")
    if before_up is not None:
        before_up(x1)
    uf, h2 = _rms_matmul(x1, _row(p["ffn_norm_g"]), p["w_up_t"], None, name=f"up_proj_{tag}")
    act, x2, cg, cv = _ffn_fwd(uf, x1, p["conv_f_w"], p["w_down"], name=f"ffn_fwd_{tag}")
    return x2, dict(x0=x0, h1=h1, u=u, ca=ca, cb=cb, ycat=ycat, x1=x1, h2=h2, uf=uf, cg=cg, cv=cv, act=act)


def _layer_bwd(dx2, dx2_b, p, saved, tag, ffn_grads, ffn_sent, mix_grads, mix_sent, dep=None):
    d_uf, dwf_g, dwf_v = _ffn_bwd(dx2_b, saved["uf"], saved["cg"], saved["cv"], p["conv_f_w"], p["w_down"],
                                  name=f"ffn_bwd_{tag}", dep=dep)
    g_down = _matmul_tn(saved["act"][None], dx2_b, name=f"dw_down_{tag}")
    g_up = _matmul_tn(d_uf, saved["h2"], name=f"dw_up_{tag}")
    dep_ffn = ffn_grads(dict(w_up=g_up, w_down=g_down), dx2_b)
    dx1, dx1_b, dg2 = _matmul_rmsbwd(d_uf, p["w_up_t"], saved["x1"], _row(p["ffn_norm_g"]), dx2,
                                     name=f"dh_ffn_{tag}", dep=dep_ffn)
    du, dwa, dwb, dba, dlg, dlb, dbin = _mix_bwd(
        dx1_b, saved["u"], saved["ca"], saved["cb"], p["conv_a_w"], _row(p["ln_a_g"]), _row(p["ln_a_b"]),
        p["conv_b_w"], p["w_out"], name=f"mix_bwd_{tag}", dep=ffn_sent(dx1_b))
    g_out = _matmul_tn(saved["ycat"][None], dx1_b, name=f"dw_out_{tag}")
    g_in = _matmul_tn(du[None], saved["h1"], name=f"dw_in_{tag}")
    conv = dict(conv_a_w=dwa, conv_b_w=dwb, conv_f_w=jnp.concatenate([dwf_g, dwf_v], axis=1))
    dep_mix = mix_grads(dict(w_in=g_in, w_out=g_out), conv, dx1_b)
    dx0, dx0_b, dg1 = _matmul_rmsbwd(du[None], p["w_in_t"], saved["x0"], _row(p["mix_norm_g"]), dx1,
                                     name=f"dh_mix_{tag}", dep=dep_mix)
    rep = dict(mix_norm_g=dg1, b_in=dbin, conv_a_b=dba, ln_a_g=dlg, ln_a_b=dlb, ffn_norm_g=dg2)
    return dx0, dx0_b, rep, mix_sent(dx0_b)


def _place():
    return lax.axis_index("x"), lax.axis_index("y"), lax.axis_index("c")


def _all_gather(arrs, *, name):
    n_a = len(arrs)

    def body(*refs):
        ins = refs[0:n_a]
        outs = refs[n_a:2 * n_a]
        send_sems, recv_sems, local_sems = refs[2 * n_a:]
        x, y, c = _place()
        sibling = (x, y, 1 - c)
        chips = [(1 - x, y), (x, 1 - y), (1 - x, 1 - y)]

        def slot(a, px, py, pc):
            return outs[a].at[4 * px + 2 * py + pc]

        def copy(a, k, block, to, src=None):
            return pltpu.make_async_remote_copy(
                src_ref=slot(a, *block) if src is None else src, dst_ref=slot(a, *block),
                send_sem=send_sems.at[a, k], recv_sem=recv_sems.at[a, k],
                device_id=to, device_id_type=MESH)

        me = (x, y, c)
        mine = [pltpu.make_async_copy(ins[a], slot(a, *me), local_sems.at[a]) for a in range(n_a)]
        for cp in mine:
            cp.start()
        started = []
        for a in range(n_a):
            first = [copy(a, 0, me, sibling, src=ins[a])]
            first += [copy(a, 1 + j, me, (*chip, c), src=ins[a]) for j, chip in enumerate(chips)]
            for cp in first:
                cp.start()
            started += first
        for a in range(n_a):
            for j, chip in enumerate(chips):
                copy(a, 1 + j, (*chip, c), me).wait_recv()
                passed = copy(a, 4 + j, (*chip, c), sibling)
                passed.start()
                started.append(passed)
        for a in range(n_a):
            copy(a, 0, sibling, me).wait_recv()
            for j, chip in enumerate(chips):
                copy(a, 4 + j, (*chip, 1 - c), me).wait_recv()
        for cp in started:
            cp.wait_send()
        for cp in mine:
            cp.wait()

    return pl.pallas_call(
        body, in_specs=[ANY] * n_a, out_specs=[ANY] * n_a,
        out_shape=[jax.ShapeDtypeStruct((N_DEV, *a.shape), a.dtype) for a in arrs],
        scratch_shapes=[pltpu.SemaphoreType.DMA((n_a, 7)), pltpu.SemaphoreType.DMA((n_a, 7)),
                        pltpu.SemaphoreType.DMA((n_a,))],
        name=name,
    )(*arrs)


def _row_tile(r, cap):
    for tr in range(min(cap, r) // 16 * 16, 0, -16):
        if r % tr == 0:
            return tr
    return r


def _pair_sum(mines, theirs, where, *, name):
    n_a = len(mines)
    n_chip = mines[0].shape[0]

    def body(where_ref, *refs):
        a_refs = refs[0:n_a]
        b_refs = refs[n_a:2 * n_a]
        p_refs = refs[2 * n_a:3 * n_a]
        l_refs = refs[3 * n_a:4 * n_a]
        q = pl.program_id(0)
        for a in range(n_a):
            p_refs[a][...] = (a_refs[a][...].astype(F32) + b_refs[a][...].astype(F32)).astype(p_refs[a].dtype)

        @pl.when(q == where_ref[1])
        def _():
            for a in range(n_a):
                l_refs[a][...] = p_refs[a][...]

    in_specs, out_p, out_l, shapes = [], [], [], []
    for m in mines:
        _, _, r, c = m.shape
        in_specs.append(pl.BlockSpec((None, None, r, c), lambda q, where_ref: (q, where_ref[0], 0, 0)))
    for m in mines:
        _, _, r, c = m.shape
        in_specs.append(pl.BlockSpec((None, r, c), lambda q, where_ref: (q, 0, 0)))
        out_p.append(pl.BlockSpec((None, r, c), lambda q, where_ref: (q, 0, 0)))
        out_l.append(pl.BlockSpec((None, r, c), lambda q, where_ref: (where_ref[1], 0, 0)))
        shapes.append(jax.ShapeDtypeStruct((n_chip, r, c), m.dtype))
    res = pl.pallas_call(
        body,
        grid_spec=pltpu.PrefetchScalarGridSpec(num_scalar_prefetch=1, grid=(n_chip,), in_specs=in_specs,
                                               out_specs=out_p + out_l),
        out_shape=shapes + shapes,
        compiler_params=_params("arbitrary"), name=name,
    )(where, *mines, *theirs)
    return list(res[:n_a]), list(res[n_a:])


HBM = pl.BlockSpec(memory_space=pltpu.HBM)
SEM = pl.BlockSpec(memory_space=pltpu.SEMAPHORE)
EFFECT = pltpu.SideEffectType.DATAFLOW_SIDE_EFFECTING


def _in_hbm(a):
    return pltpu.with_memory_space_constraint(a, pltpu.HBM)


def _split_start(srcs, lands, plan, n_copies, after, *, name):
    n_s, n_l = len(srcs), len(lands)

    def body(*refs):
        src_refs = refs[0:n_s]
        land_refs = refs[n_s:n_s + n_l]
        send_sems, recv_sems = refs[n_s + n_l + 1], refs[n_s + n_l + 2]
        token = refs[-1]
        for cp in plan(src_refs, land_refs, send_sems, recv_sems):
            cp.start()
        token[...] = jnp.zeros_like(token)

    thru = [pltpu.HBM(a.shape, a.dtype) for a in list(srcs) + list(lands)]
    res = pl.pallas_call(
        body, name=name,
        out_shape=(pltpu.SemaphoreType.DMA((n_copies,)), pltpu.SemaphoreType.DMA((n_copies,)), *thru,
                   jax.ShapeDtypeStruct((SUB, LANES), F32)),
        in_specs=[HBM] * (n_s + n_l) + [ANY],
        out_specs=(SEM, SEM, *([HBM] * (n_s + n_l)), pl.BlockSpec(memory_space=pltpu.VMEM)),
        input_output_aliases={i: 2 + i for i in range(n_s + n_l)},
        compiler_params=pltpu.CompilerParams(has_side_effects=EFFECT),
    )(*[_in_hbm(a) for a in srcs], *[_in_hbm(a) for a in lands], _in_hbm(after))
    return res[0], res[1], list(res[2:2 + n_s]), list(res[2 + n_s:2 + n_s + n_l]), res[-1]


def _split_wait(send_sems, recv_sems, srcs, lands, after, plan, *, name):
    n_s, n_l = len(srcs), len(lands)

    def body(*refs):
        src_refs = refs[0:n_s]
        land_refs = refs[n_s:n_s + n_l]
        send, recv = refs[n_s + n_l], refs[n_s + n_l + 1]
        for cp in plan(src_refs, land_refs, send, recv):
            cp.wait_send()
            cp.wait_recv()

    res = pl.pallas_call(
        body, name=name,
        out_shape=tuple(pltpu.HBM(a.shape, a.dtype) for a in list(srcs) + list(lands)),
        in_specs=[HBM] * (n_s + n_l) + [SEM, SEM, ANY],
        out_specs=tuple([HBM] * (n_s + n_l)),
        input_output_aliases={i: i for i in range(n_s + n_l)},
        compiler_params=pltpu.CompilerParams(has_side_effects=EFFECT),
    )(*srcs, *lands, send_sems, recv_sems, _in_hbm(after))
    return list(res[:n_s]), list(res[n_s:])


def _remote(src, dst, send_sems, recv_sems, k, to):
    return pltpu.make_async_remote_copy(src_ref=src, dst_ref=dst, send_sem=send_sems.at[k], recv_sem=recv_sems.at[k],
                                        device_id=to, device_id_type=MESH)


def _gather_plan_first(src_refs, land_refs, send_sems, recv_sems):
    x, y, c = _place()
    me = 4 * x + 2 * y + c
    peers = [(x, y, 1 - c), (1 - x, y, c), (x, 1 - y, c), (1 - x, 1 - y, c)]
    return [_remote(src, land.at[me], send_sems, recv_sems, 4 * a + k, to)
            for a, (src, land) in enumerate(zip(src_refs, land_refs)) for k, to in enumerate(peers)]


def _gather_plan_second(src_refs, land_refs, send_sems, recv_sems):
    x, y, c = _place()
    chips = [(1 - x, y), (x, 1 - y), (1 - x, 1 - y)]
    out = []
    for a, land in enumerate(land_refs):
        for j, (px, py) in enumerate(chips):
            slot = land.at[4 * px + 2 * py + c]
            out.append(_remote(slot, slot, send_sems, recv_sems, 3 * a + j, (x, y, 1 - c)))
    return out


def _siblings_plan(src_refs, land_refs, send_sems, recv_sems):
    x, y, c = _place()
    return [_remote(src.at[:, 1 - c], land, send_sems, recv_sems, a, (x, y, 1 - c))
            for a, (src, land) in enumerate(zip(src_refs, land_refs))]


def _chips_plan(src_refs, land_refs, send_sems, recv_sems):
    x, y, c = _place()
    my_chip = 2 * x + y
    chips = [(1 - x, y), (x, 1 - y), (1 - x, 1 - y)]
    return [_remote(src.at[2 * px + py], land.at[my_chip], send_sems, recv_sems, 3 * a + j, (px, py, c))
            for a, (src, land) in enumerate(zip(src_refs, land_refs)) for j, (px, py) in enumerate(chips)]


def _gather_landings(shards, me, *, name):
    blank = _unwritten([jax.ShapeDtypeStruct((N_DEV, *s.shape), s.dtype) for s in shards], name=name)
    return [lax.dynamic_update_index_in_dim(b, s, me, 0) for b, s in zip(blank, shards)]


def _adamw_math(g, w, m, v):
    m = ADAM_B1 * m + (1.0 - ADAM_B1) * g
    v = ADAM_B2 * v + (1.0 - ADAM_B2) * (g * g)
    m_hat = m / (1.0 - ADAM_B1 ** ADAM_STEP)
    v_hat = v / (1.0 - ADAM_B2 ** ADAM_STEP)
    delta = -ADAM_LR * (m_hat / (jnp.sqrt(v_hat) + ADAM_EPS) + ADAM_WD * w)
    return delta, m, v


def _adamw_sharded(parts, w, m, v, *, name, dep=None):
    n_layers, r, c = w.shape
    n_chip = parts[0].shape[0]
    tr = _row_tile(r, 384)
    n_i = r // tr

    def body(*refs):
        p_refs = refs[0:n_layers]
        w_ref, m_ref, v_ref, _, g_out, d_out, m_out, v_out = refs[n_layers:]
        layer = pl.program_id(0)
        for l in range(n_layers):
            @pl.when(layer == l)
            def _(l=l):
                g = p_refs[l][0].astype(F32)
                for q in range(1, n_chip):
                    g = g + p_refs[l][q].astype(F32)
                delta, m_new, v_new = _adamw_math(g, w_ref[...], m_ref[...], v_ref[...])
                g_out[...] = g
                d_out[...] = delta
                m_out[...] = m_new
                v_out[...] = v_new

    def part_map(l):
        return lambda layer, i: (0, jnp.where(layer == l, i, jnp.where(layer < l, 0, n_i - 1)), 0)

    blk = pl.BlockSpec((None, tr, c), lambda layer, i: (layer, i, 0))
    return pl.pallas_call(
        body, grid=(n_layers, n_i),
        in_specs=[pl.BlockSpec((n_chip, tr, c), part_map(l)) for l in range(n_layers)] + [blk, blk, blk, ANY],
        out_specs=[blk] * 4, out_shape=[jax.ShapeDtypeStruct((n_layers, r, c), F32)] * 4,
        compiler_params=_params("arbitrary", "arbitrary"), name=name,
    )(*parts, w, m, v, w if dep is None else dep)


def _fold_partials(cols, *, name):
    widths = [c.shape[1] for c in cols]

    def body(*refs):
        o_ref = refs[-1]
        pos = 0
        for ref, width in zip(refs[:-1], widths):
            o_ref[:, pos:pos + width] = jnp.sum(ref[...], axis=0, keepdims=True)
            pos += width

    return pl.pallas_call(body, out_shape=jax.ShapeDtypeStruct((1, sum(widths)), F32), name=name)(*cols)


def _adamw_replicated(parts, names, w, m, v, n_loss, *, name):
    n_dev = parts.shape[0]
    n_layers = w[names[0]].shape[0]
    every = list(names) + ["final_norm_g"]
    n_p = len(every)

    def body(*refs):
        p_ref = refs[0]
        w_refs = dict(zip(every, refs[1:1 + n_p]))
        m_refs = dict(zip(every, refs[1 + n_p:1 + 2 * n_p]))
        v_refs = dict(zip(every, refs[1 + 2 * n_p:1 + 3 * n_p]))
        l_out = refs[1 + 3 * n_p]
        outs = refs[2 + 3 * n_p:]
        o_refs = {n: outs[4 * q:4 * q + 4] for q, n in enumerate(every)}
        acc = p_ref[0]
        for q in range(1, n_dev):
            acc = acc + p_ref[q]
        tot = jnp.sum(acc, axis=0, keepdims=True)
        pos = 0
        where = [(n, l) for l in range(n_layers) for n in names] + [("final_norm_g", 0)]
        for n, l in where:
            width = w_refs[n].shape[1]
            g = tot[:, pos:pos + width]
            pos += width
            row = pl.ds(l, 1)
            delta, m_new, v_new = _adamw_math(g, w_refs[n][row, :], m_refs[n][row, :], v_refs[n][row, :])
            for o, val in zip(o_refs[n], (g, delta, m_new, v_new)):
                o[row, :] = val
        l_out[...] = (0.5 / n_loss) * jnp.sum(tot[:, pos:pos + n_loss], axis=-1, keepdims=True)

    shapes = [jax.ShapeDtypeStruct((1, 1), F32)]
    for n in every:
        shapes += [jax.ShapeDtypeStruct(w[n].shape, F32)] * 4
    res = pl.pallas_call(
        body, out_shape=shapes,
        compiler_params=pltpu.CompilerParams(vmem_limit_bytes=VMEM_LIMIT), name=name,
    )(parts, *[w[n] for n in every], *[m[n] for n in every], *[v[n] for n in every])
    return res[0], {n: res[1 + 4 * q:5 + 4 * q] for q, n in enumerate(every)}


BIG = ("w_in", "w_out", "w_up", "w_down")
COL_SHARDED = ("w_in", "w_up")
CONV = ("conv_a_w", "conv_b_w", "conv_f_w")
REPLICATED = ("mix_norm_g", "b_in", "conv_a_b", "ln_a_g", "ln_a_b", "ffn_norm_g")
KINDS = ("grad", "delta", "m", "v")
FFN_PART = ("w_up", "w_down")
MIX_PART = ("w_in", "w_out")


def _weights_from_gathered(g):
    n_dev, r, c = g.shape
    return g.reshape(n_dev * r, c)


def _slabs_from_full(grad):
    return grad.reshape(N_DEV, grad.shape[0] // N_DEV, grad.shape[1])


def _unwritten(like, *, name):
    return pl.pallas_call(lambda *refs: None, out_specs=[ANY] * len(like), out_shape=list(like), name=name)()


def kernel(x, mix_norm_g, w_in, b_in, conv_a_w, conv_a_b, ln_a_g, ln_a_b, conv_b_w, w_out, ffn_norm_g, w_up, conv_f_w, w_down, final_norm_g, loss_target, m_mix_norm_g, m_w_in, m_b_in, m_conv_a_w, m_conv_a_b, m_ln_a_g, m_ln_a_b, m_conv_b_w, m_w_out, m_ffn_norm_g, m_w_up, m_conv_f_w, m_w_down, m_final_norm_g, v_mix_norm_g, v_w_in, v_b_in, v_conv_a_w, v_conv_a_b, v_ln_a_g, v_ln_a_b, v_conv_b_w, v_w_out, v_ffn_norm_g, v_w_up, v_conv_f_w, v_w_down, v_final_norm_g):
    w = dict(mix_norm_g=mix_norm_g, w_in=w_in, b_in=b_in, conv_a_w=conv_a_w, conv_a_b=conv_a_b, ln_a_g=ln_a_g,
             ln_a_b=ln_a_b, conv_b_w=conv_b_w, w_out=w_out, ffn_norm_g=ffn_norm_g, w_up=w_up, conv_f_w=conv_f_w,
             w_down=w_down, final_norm_g=final_norm_g)
    m = dict(mix_norm_g=m_mix_norm_g, w_in=m_w_in, b_in=m_b_in, conv_a_w=m_conv_a_w, conv_a_b=m_conv_a_b,
             ln_a_g=m_ln_a_g, ln_a_b=m_ln_a_b, conv_b_w=m_conv_b_w, w_out=m_w_out, ffn_norm_g=m_ffn_norm_g,
             w_up=m_w_up, conv_f_w=m_conv_f_w, w_down=m_w_down, final_norm_g=m_final_norm_g)
    v = dict(mix_norm_g=v_mix_norm_g, w_in=v_w_in, b_in=v_b_in, conv_a_w=v_conv_a_w, conv_a_b=v_conv_a_b,
             ln_a_g=v_ln_a_g, ln_a_b=v_ln_a_b, conv_b_w=v_conv_b_w, w_out=v_w_out, ffn_norm_g=v_ffn_norm_g,
             w_up=v_w_up, conv_f_w=v_conv_f_w, w_down=v_w_down, final_norm_g=v_final_norm_g)
    order = list(w)
    n_layers = w_in.shape[0]
    xs = x[0]
    target = loss_target[0]
    flip = lambda a: jnp.transpose(a, (0, 2, 1))
    wt, mt, vt = ({n: flip(d[n]) if n in COL_SHARDED else d[n] for n in BIG} for d in (w, m, v))
    px, py, pc = _place()
    where = jnp.stack([pc, 2 * px + py]).astype(jnp.int32)
    me = 4 * px + 2 * py + pc

    assert BIG == MIX_PART + FFN_PART
    key = lambda n: n + "_t" if n in COL_SHARDED else n
    shard = lambda n, l: wt[n][l].astype(BF16)

    def gather_start(names, l, after, tag, behind=None):
        if behind is None:
            shards = [shard(n, l) for n in names]
        else:
            shards = [s.astype(BF16) for s in lax.optimization_barrier(([wt[n][l] for n in names], behind))[0]]
        lands = _gather_landings(shards, me, name=f"gather_landing_{tag}")
        return _split_start(shards, lands, _gather_plan_first, 4 * len(shards), after, name=f"gather_first_start_{tag}")

    def gather_mid(first, after, tag):
        return _split_wait(first[0], first[1], first[2], first[3], after, _gather_plan_first,
                           name=f"gather_first_wait_{tag}")[1]

    def forward_start(lands, after, tag):
        return _split_start([], lands, _gather_plan_second, 3 * len(lands), after, name=f"gather_second_start_{tag}")

    def forward_finish(second, after, tag):
        return _split_wait(second[0], second[1], [], second[3], after, _gather_plan_second,
                           name=f"gather_second_wait_{tag}")[1]

    gathered = _all_gather([shard(n, 0) for n in MIX_PART] + [w[n] for n in CONV], name="gather_weights_0")
    params = [{n: w[n][l] for n in REPLICATED} for l in range(n_layers)]
    for n, g in zip(CONV, gathered[len(MIX_PART):]):
        n_dev, _, taps, c = g.shape
        full = g.transpose(1, 2, 0, 3).reshape(n_layers, taps, n_dev * c)
        for l in range(n_layers):
            params[l][n] = full[l]
    for n, g in zip(MIX_PART, gathered):
        params[0][key(n)] = _weights_from_gathered(g)
    ffn_first = gather_start(FFN_PART, 0, gathered[0], "0_ffn")
    pending = {}

    h = xs
    saved = []
    for l in range(n_layers):
        nxt = l + 1 if l + 1 < n_layers else None

        def before_up(x1, l=l, nxt=nxt):
            if l == 0:
                second = forward_start(gather_mid(ffn_first, x1, "0_ffn"), x1, "0_ffn")
                after = second[4]
            else:
                second = pending[l]["ffn"]
                after = x1
            if nxt is not None:
                pending[nxt] = dict(first=gather_start(BIG, nxt, after, str(nxt), behind=ffn_first[4]))
                after = pending[nxt]["first"][4]
            for n, g in zip(FFN_PART, forward_finish(second, after, f"{l}_ffn")):
                params[l][key(n)] = _weights_from_gathered(g)

        h, keep = _layer_fwd(h, params[l], str(l), dep=ffn_first[4] if l == 0 else None, before_up=before_up)
        saved.append(keep)
        if nxt is not None:
            arrived = gather_mid(pending[nxt]["first"], h, str(nxt))
            mix_second = forward_start(arrived[:len(MIX_PART)], h, f"{nxt}_mix")
            pending[nxt]["ffn"] = forward_start(arrived[len(MIX_PART):], mix_second[4], f"{nxt}_ffn")
            for n, g in zip(MIX_PART, forward_finish(mix_second, pending[nxt]["ffn"][4], f"{nxt}_mix")):
                params[nxt][key(n)] = _weights_from_gathered(g)

    def start_siblings(slabs, after, tag):
        mines = [s.reshape(N_CHIP, 2, *s.shape[1:]) for s in slabs]
        lands = _unwritten([jax.ShapeDtypeStruct((N_CHIP, *m.shape[2:]), m.dtype) for m in mines],
                           name=f"reduce_siblings_landing_{tag}")
        return _split_start(mines, lands, _siblings_plan, len(mines), after, name=f"reduce_siblings_start_{tag}")

    def start_chips(sib, after, tag):
        mines, theirs = _split_wait(sib[0], sib[1], sib[2], sib[3], after, _siblings_plan,
                                    name=f"reduce_siblings_wait_{tag}")
        pairs, lands = _pair_sum(mines, theirs, where, name=f"pair_sum_{tag}")
        return _split_start(pairs, lands, _chips_plan, 3 * len(pairs), after, name=f"reduce_chips_start_{tag}")

    def finish_reduce(fly, after, tag):
        return _split_wait(fly[0], fly[1], fly[2], fly[3], after, _chips_plan, name=f"reduce_chips_wait_{tag}")[1]

    loss_sq, dh, dh_b, dgf = _loss_bwd(h, _row(final_norm_g), target, name="loss")
    conv_g = {n: [None] * n_layers for n in CONV}
    rep_g = [None] * n_layers
    siblings = {}
    flights = {}
    token = None
    for l in reversed(range(n_layers)):
        def ffn_grads(g, after, l=l):
            siblings[l, "ffn"] = start_siblings([_slabs_from_full(g[n]) for n in FFN_PART], after, f"{l}_ffn")
            return siblings[l, "ffn"][4]

        def ffn_sent(after, l=l):
            flights[l, "ffn"] = start_chips(siblings[l, "ffn"], after, f"{l}_ffn")
            return flights[l, "ffn"][4]

        def mix_grads(g, conv, after, l=l):
            for n in CONV:
                conv_g[n][l] = conv[n]
            slabs = [_slabs_from_full(g[n]) for n in MIX_PART]
            if l == 0:
                for n in CONV:
                    full = jnp.stack(conv_g[n])
                    _, taps, c = full.shape
                    slabs.append(full.reshape(n_layers, taps, N_DEV, c // N_DEV).transpose(2, 0, 1, 3)
                                 .reshape(N_DEV, n_layers * taps, c // N_DEV))
            siblings[l, "mix"] = start_siblings(slabs, after, f"{l}_mix")
            return siblings[l, "mix"][4]

        def mix_sent(after, l=l):
            flights[l, "mix"] = start_chips(siblings[l, "mix"], after, f"{l}_mix")
            return flights[l, "mix"][4]

        dh, dh_b, rep_g[l], token = _layer_bwd(dh, dh_b, params[l], saved[l], str(l), ffn_grads, ffn_sent,
                                               mix_grads, mix_sent, dep=token)

    sums = {key: finish_reduce(fly, dh, f"{key[0]}_{key[1]}") for key, fly in flights.items() if key != (0, "mix")}
    out = {k: {} for k in KINDS}

    def adamw_big(names, part, dep):
        for q, n in enumerate(names):
            layer_parts = [sums[l, part][q] for l in range(n_layers)]
            res = _adamw_sharded(layer_parts, wt[n], mt[n], vt[n], name=f"adamw_{n}", dep=dep)
            for k, r in zip(KINDS, res):
                out[k][n] = flip(r) if n in COL_SHARDED else r

    adamw_big(FFN_PART, "ffn", token)

    rep_cols = [rep_g[l][n] for l in range(n_layers) for n in REPLICATED] + [dgf, loss_sq]
    rep_all = _all_gather([_fold_partials(rep_cols, name="fold_small")], name="gather_small")[0]
    with_final = lambda d: {**{n: d[n] for n in REPLICATED}, "final_norm_g": _row(d["final_norm_g"])}
    loss, rep_res = _adamw_replicated(rep_all, REPLICATED, with_final(w), with_final(m), with_final(v),
                                      loss_sq.shape[1], name="adamw_small")
    for n, res in rep_res.items():
        for k, r in zip(KINDS, res):
            out[k][n] = r.reshape(w[n].shape)

    last = finish_reduce(flights[0, "mix"], rep_res["b_in"][0], "0_mix")
    sums[0, "mix"] = last[:len(MIX_PART)]
    adamw_big(MIX_PART, "mix", None)
    for n, p in zip(CONV, last[len(MIX_PART):]):
        as_one = lambda a: a.reshape(1, *p.shape[1:])
        for k, r in zip(KINDS, _adamw_sharded([p], as_one(w[n]), as_one(m[n]), as_one(v[n]), name=f"adamw_{n}")):
            out[k][n] = r.reshape(w[n].shape)

    grad_x = dh.reshape(x.shape)
    return (loss.reshape(()), grad_x, *[out["grad"][n] for n in order], *[out["delta"][n] for n in order],
            *[out["m"][n] for n in order], *[out["v"][n] for n in order])
```

```python
import functools

import jax
import jax.numpy as jnp
from jax import lax
from jax.experimental import pallas as pl
from jax.experimental.pallas import tpu as pltpu

F32 = jnp.float32
BF16 = jnp.bfloat16

N_DEV = 8
N_CHIP = 4
D_CONF = 512
CONF_K = 31
SHORT_K = 3
EPS = 1e-6
HALO = 32
HALO3 = 8
HALO3_BLK = 16
LANES = 128
SUB = 8
VMEM_LIMIT = 56 * 1024 * 1024

ADAM_LR = 0.001
ADAM_B1 = 0.9
ADAM_B2 = 0.999
ADAM_EPS = 1e-08
ADAM_WD = 0.01
ADAM_STEP = 10

MESH = pl.DeviceIdType.MESH
ANY = pl.BlockSpec(memory_space=pl.ANY)


def _params(*sem):
    return pltpu.CompilerParams(dimension_semantics=sem, vmem_limit_bytes=VMEM_LIMIT)


def _resident(shape, index_map):
    return pl.BlockSpec(shape, index_map, pipeline_mode=pl.Buffered(1))


def _row_loop(n_rows, rb, fn, unroll=1):
    rb = min(rb, n_rows)

    def body(i, carry):
        fn(pl.ds(pl.multiple_of(i * rb, rb), rb))
        return carry
    lax.fori_loop(0, n_rows // rb, body, 0, unroll=unroll)


def _rows8(v):
    acc = v[0:SUB]
    for k in range(1, v.shape[0] // SUB):
        acc = acc + v[k * SUB:(k + 1) * SUB]
    return acc


def _sigmoid(z):
    return 0.5 * jnp.tanh(0.5 * z) + 0.5


def _dot(a, b):
    return jnp.dot(a, b, preferred_element_type=F32)


def _dot_nt(a, b):
    return lax.dot_general(a, b, (((1,), (1,)), ((), ())), preferred_element_type=F32)


def _dot_tn(a, b):
    return lax.dot_general(a, b, (((0,), (0,)), ((), ())), preferred_element_type=F32)


def _replicate_taps(w_ref, wrep, taps):
    for k in range(taps):
        wrep[pl.ds(k * SUB, SUB), :] = jnp.broadcast_to(w_ref[pl.ds(k, 1), :], (SUB, w_ref.shape[1]))


def _shift_copies(win, shf, lanes):
    span = win.shape[0] - SUB
    for r in range(1, SUB):
        for j0 in range(0, span, 64):
            n = min(64, span - j0)
            shf[r - 1, pl.ds(j0, n), lanes] = win[pl.ds(j0 + r, n), lanes]


def _rows_at(win, shf, off, rb, lanes):
    if shf is None or off % SUB == 0:
        return win[pl.ds(off, rb), lanes]
    return shf[off % SUB - 1, pl.ds(off - off % SUB, rb), lanes]


def _conv_taps(win, wrep, out, *, taps, n_rows, base, width, transposed=False, bias_ref=None, shf=None):
    rb = min(32 if taps > 8 else 64, n_rows)

    def lane_body(cb, carry):
        lanes = pl.ds(pl.multiple_of(cb * LANES, LANES), LANES)
        if shf is not None:
            _shift_copies(win, shf, lanes)
        for r0 in range(0, n_rows, rb):
            acc = None
            for k in range(taps):
                off = (taps - 1 - k) if transposed else (k - (taps - 1))
                wk = jnp.tile(wrep[pl.ds(k * SUB, SUB), lanes], (rb // SUB, 1))
                term = wk * _rows_at(win, shf, base + r0 + off, rb, lanes)
                acc = term if acc is None else acc + term
            if bias_ref is not None:
                acc = acc + bias_ref[:, lanes]
            out[pl.ds(r0, rb), lanes] = acc.astype(out.dtype)
        return carry

    lax.fori_loop(0, width // LANES, lane_body, 0)


def _conv_bwd_taps(win, wrep, x_cur, dx_out, dw_acc, *, taps, n_rows, width, shf=None):
    rb = min(32 if taps > 8 else 64, n_rows)

    def lane_body(cb, carry):
        lanes = pl.ds(pl.multiple_of(cb * LANES, LANES), LANES)
        if shf is not None:
            _shift_copies(win, shf, lanes)
        sums = [None] * taps
        for r0 in range(0, n_rows, rb):
            xv = x_cur[pl.ds(r0, rb), lanes].astype(F32)
            acc = None
            for k in range(taps):
                shifted = _rows_at(win, shf, r0 + taps - 1 - k, rb, lanes)
                term = jnp.tile(wrep[pl.ds(k * SUB, SUB), lanes], (rb // SUB, 1)) * shifted
                acc = term if acc is None else acc + term
                part = _rows8(xv * shifted)
                sums[k] = part if sums[k] is None else sums[k] + part
            dx_out[pl.ds(r0, rb), lanes] = acc.astype(dx_out.dtype)
        for k in range(taps):
            dw_acc[pl.ds(k * SUB, SUB), lanes] += sums[k]
        return carry

    lax.fori_loop(0, width // LANES, lane_body, 0)


def _fold8(acc_ref, taps):
    return jnp.concatenate(
        [jnp.sum(acc_ref[pl.ds(k * SUB, SUB), :], axis=0, keepdims=True) for k in range(taps)], axis=0)


def _seq_tile(s_len):
    return min(512, s_len)


def _mm_tile(s_len):
    return min(512, s_len)


def _ff_chunk(ff):
    best = LANES
    for c in range(LANES, 1408 + 1, LANES):
        if ff % c == 0:
            best = c
    return best


def _col_tile(n):
    for c in (512, 1408, 256, LANES):
        if n % c == 0:
            return c
    return n


def _rms_matmul(x, g, wt, b, *, name, dep=None):
    s_len, d = x.shape
    n = wt.shape[0]
    tm = _mm_tile(s_len)
    cn = _col_tile(n)
    has_bias = b is not None

    def body(*refs):
        x_ref, g_ref, w_ref = refs[0:3]
        b_ref = refs[3] if has_bias else None
        o_ref, h_ref = refs[-2:]

        def blk(rows):
            xv = x_ref[rows, :]
            r = lax.rsqrt(jnp.mean(xv * xv, axis=-1, keepdims=True) + EPS)
            h_ref[rows, :] = ((xv * r) * g_ref[...]).astype(BF16)

        rb = min(128, tm)
        for r0 in range(0, tm, rb):
            blk(pl.ds(r0, rb))
        for j in range(n // cn):
            acc = _dot_nt(h_ref[...], w_ref[j * cn:(j + 1) * cn, :])
            if has_bias:
                acc = acc + b_ref[:, j * cn:(j + 1) * cn]
            o_ref[:, j * cn:(j + 1) * cn] = acc.astype(BF16)

    in_specs = [pl.BlockSpec((tm, d), lambda i: (i, 0)), _resident((1, d), lambda i: (0, 0)),
                _resident((n, d), lambda i: (0, 0))]
    args = [x, g, wt]
    if has_bias:
        in_specs.append(_resident((1, n), lambda i: (0, 0)))
        args.append(b)
    in_specs.append(ANY)
    args.append(x if dep is None else dep)
    return pl.pallas_call(
        body, grid=(s_len // tm,), in_specs=in_specs,
        out_specs=[pl.BlockSpec((tm, n), lambda i: (i, 0)), pl.BlockSpec((tm, d), lambda i: (i, 0))],
        out_shape=[jax.ShapeDtypeStruct((s_len, n), BF16), jax.ShapeDtypeStruct((s_len, d), BF16)],
        compiler_params=_params("parallel"), name=name,
    )(*args)


def _mix_windows(u_ref, uh_ref, gw, pw, first, t):
    c = D_CONF
    uh = uh_ref[...].astype(F32)
    gw[0:HALO, :] = jnp.where(first, 0.0, uh[:, 0:c] * _sigmoid(uh[:, c:2 * c]))
    pw[0:HALO3, :] = jnp.where(first, 0.0, uh[HALO - HALO3:HALO, 3 * c:4 * c] * uh[HALO - HALO3:HALO, 4 * c:5 * c])

    def blk(rows):
        dst = pl.ds(pl.multiple_of(rows.start + HALO, SUB), rows.size)
        gw[dst, :] = u_ref[rows, 0:c].astype(F32) * _sigmoid(u_ref[rows, c:2 * c].astype(F32))
        dst3 = pl.ds(pl.multiple_of(rows.start + HALO3, SUB), rows.size)
        pw[dst3, :] = u_ref[rows, 3 * c:4 * c].astype(F32) * u_ref[rows, 4 * c:5 * c].astype(F32)
    _row_loop(t, 64, blk)


def _mix_fwd(u, x0, wa, ba, lg, lb, wb, w_out, *, name):
    s_len, d_in = u.shape
    d = x0.shape[1]
    c = D_CONF
    t = _seq_tile(s_len)
    per = t // HALO

    def body(u_ref, uh_ref, x0_ref, wa_ref, ba_ref, lg_ref, lb_ref, wb_ref, wo_ref, y_ref, x1_ref, ca, cb,
             gw, pw, wrep_a, wrep_b, shf):
        first = pl.program_id(0) == 0
        _mix_windows(u_ref, uh_ref, gw, pw, first, t)
        _replicate_taps(wa_ref, wrep_a, CONF_K)
        _replicate_taps(wb_ref, wrep_b, SHORT_K)
        _conv_taps(gw, wrep_a, ca, taps=CONF_K, n_rows=t, base=HALO, width=c, bias_ref=ba_ref, shf=shf)
        _conv_taps(pw, wrep_b, cb, taps=SHORT_K, n_rows=t, base=HALO3, width=c)

        def blk(rows):
            cv = ca[rows, :]
            mu = jnp.mean(cv, axis=-1, keepdims=True)
            xc = cv - mu
            var = jnp.mean(xc * xc, axis=-1, keepdims=True)
            ln = (xc * lax.rsqrt(var + EPS)) * lg_ref[...] + lb_ref[...]
            y_ref[rows, 0:c] = (ln * _sigmoid(ln)).astype(BF16)
            y_ref[rows, c:2 * c] = (u_ref[rows, 2 * c:3 * c].astype(F32) * cb[rows, :]).astype(BF16)
        half = t // 2
        rb = min(64, half)
        for lo in range(0, t, half):
            for r0 in range(lo, lo + half, rb):
                blk(pl.ds(r0, rb))
            x1_ref[lo:lo + half, :] = x0_ref[lo:lo + half, :] + _dot(y_ref[lo:lo + half, :], wo_ref[...])

    small = lambda r: _resident((r, c), lambda i: (0, 0))
    return pl.pallas_call(
        body, grid=(s_len // t,),
        in_specs=[pl.BlockSpec((t, d_in), lambda i: (i, 0)),
                  pl.BlockSpec((HALO, d_in), lambda i: (jnp.maximum(i * per - 1, 0), 0)),
                  pl.BlockSpec((t, d), lambda i: (i, 0)),
                  small(CONF_K), small(1), small(1), small(1), small(SHORT_K),
                  _resident((2 * c, d), lambda i: (0, 0))],
        out_specs=[pl.BlockSpec((t, 2 * c), lambda i: (i, 0)), pl.BlockSpec((t, d), lambda i: (i, 0)),
                   pl.BlockSpec((t, c), lambda i: (i, 0)), pl.BlockSpec((t, c), lambda i: (i, 0))],
        out_shape=[jax.ShapeDtypeStruct((s_len, 2 * c), BF16), jax.ShapeDtypeStruct((s_len, d), F32),
                   jax.ShapeDtypeStruct((s_len, c), F32), jax.ShapeDtypeStruct((s_len, c), F32)],
        scratch_shapes=[pltpu.VMEM((HALO + t, c), F32), pltpu.VMEM((HALO3 + t, c), F32),
                        pltpu.VMEM((CONF_K * SUB, c), F32), pltpu.VMEM((SHORT_K * SUB, c), F32),
                        pltpu.VMEM((SUB - 1, HALO + t, c), F32)],
        compiler_params=_params("arbitrary"), name=name,
    )(u, u, x0, wa, ba, lg, lb, wb, w_out)


def _ffn_fwd(uf, x1, wf, w_down, *, name):
    s_len, ff2 = uf.shape
    ff = ff2 // 2
    d = x1.shape[1]
    t = _seq_tile(s_len)
    fc = _ff_chunk(ff)
    nc = ff // fc
    per = t // HALO3_BLK
    half = t // 2
    rb = min(64, half)

    def body(ug_ref, ugh_ref, uv_ref, uvh_ref, x1_ref, wfg_ref, wfv_ref, wd_ref,
             act_ref, x2_ref, cg_ref, cv_ref, gwin, vwin, wrep_g, wrep_v):
        first = pl.program_id(0) == 0
        first_chunk = pl.program_id(1) == 0
        lo8 = HALO3_BLK - HALO3
        gwin[0:HALO3, :] = jnp.where(first, 0.0, ugh_ref[...].astype(F32)[lo8:HALO3_BLK])
        vwin[0:HALO3, :] = jnp.where(first, 0.0, uvh_ref[...].astype(F32)[lo8:HALO3_BLK])
        _replicate_taps(wfg_ref, wrep_g, SHORT_K)
        _replicate_taps(wfv_ref, wrep_v, SHORT_K)
        chunk_rows = pl.ds(pl.multiple_of(pl.program_id(1) * fc, fc), fc)

        def conv(win, wrep, r0, lanes):
            acc = None
            for k in range(SHORT_K):
                wk = jnp.tile(wrep[k * SUB:(k + 1) * SUB, lanes], (rb // SUB, 1))
                off = HALO3 + r0 + k - (SHORT_K - 1)
                term = wk * win[off:off + rb, lanes]
                acc = term if acc is None else acc + term
            return acc

        for lo in range(0, t, half):
            for r0 in range(lo, lo + half, rb):
                gwin[HALO3 + r0:HALO3 + r0 + rb, :] = ug_ref[r0:r0 + rb, :].astype(F32)
                vwin[HALO3 + r0:HALO3 + r0 + rb, :] = uv_ref[r0:r0 + rb, :].astype(F32)
            for cb in range(fc // LANES):
                lanes = slice(cb * LANES, (cb + 1) * LANES)
                for r0 in range(lo, lo + half, rb):
                    gv = conv(gwin, wrep_g, r0, lanes).astype(BF16)
                    vv = conv(vwin, wrep_v, r0, lanes).astype(BF16)
                    cg_ref[r0:r0 + rb, lanes] = gv
                    cv_ref[r0:r0 + rb, lanes] = vv
                    act_ref[r0:r0 + rb, lanes] = (gv * _sigmoid(gv)) * vv
            base = jnp.where(first_chunk, x1_ref[lo:lo + half, :], x2_ref[lo:lo + half, :])
            x2_ref[lo:lo + half, :] = base + _dot(act_ref[lo:lo + half, :], wd_ref[chunk_rows, :])

    halo_map = lambda off: (lambda i, j: (jnp.maximum(i * per - 1, 0), j + off))
    return pl.pallas_call(
        body, grid=(s_len // t, nc),
        in_specs=[pl.BlockSpec((t, fc), lambda i, j: (i, j)), pl.BlockSpec((HALO3_BLK, fc), halo_map(0)),
                  pl.BlockSpec((t, fc), lambda i, j: (i, j + nc)), pl.BlockSpec((HALO3_BLK, fc), halo_map(nc)),
                  pl.BlockSpec((t, d), lambda i, j: (i, 0)),
                  pl.BlockSpec((SHORT_K, fc), lambda i, j: (0, j)),
                  pl.BlockSpec((SHORT_K, fc), lambda i, j: (0, j + nc)),
                  _resident((ff, d), lambda i, j: (0, 0))],
        out_specs=[pl.BlockSpec((t, fc), lambda i, j: (i, j)), pl.BlockSpec((t, d), lambda i, j: (i, 0)),
                   pl.BlockSpec((t, fc), lambda i, j: (i, j)), pl.BlockSpec((t, fc), lambda i, j: (i, j))],
        out_shape=[jax.ShapeDtypeStruct((s_len, ff), BF16), jax.ShapeDtypeStruct((s_len, d), F32),
                   jax.ShapeDtypeStruct((s_len, ff), BF16), jax.ShapeDtypeStruct((s_len, ff), BF16)],
        scratch_shapes=[pltpu.VMEM((HALO3 + t, fc), F32), pltpu.VMEM((HALO3 + t, fc), F32),
                        pltpu.VMEM((SHORT_K * SUB, fc), F32), pltpu.VMEM((SHORT_K * SUB, fc), F32)],
        compiler_params=_params("parallel", "arbitrary"), name=name,
    )(uf, uf, uf, uf, x1, wf, wf, w_down)


def _loss_bwd(x, g, target, *, name):
    s_len, d = x.shape
    t = _seq_tile(s_len)

    def body(x_ref, g_ref, t_ref, l_ref, dx_ref, dxb_ref, dg_ref):
        @pl.when(pl.program_id(0) == 0)
        def _():
            l_ref[...] = jnp.zeros_like(l_ref)
            dg_ref[...] = jnp.zeros_like(dg_ref)

        def blk(rows):
            xv = x_ref[rows, :]
            r = lax.rsqrt(jnp.mean(xv * xv, axis=-1, keepdims=True) + EPS)
            xn = xv * r
            e = xn * g_ref[...] - t_ref[rows, :]
            l_ref[...] += _rows8(e * e)
            dy = e * (1.0 / d)
            dg_ref[...] += _rows8(dy * xn)
            dn = dy * g_ref[...]
            dx = r * (dn - xn * jnp.mean(dn * xn, axis=-1, keepdims=True))
            dx_ref[rows, :] = dx
            dxb_ref[rows, :] = dx.astype(BF16)
        _row_loop(t, 64, blk)

    row = pl.BlockSpec((t, d), lambda i: (i, 0))
    part = pl.BlockSpec((SUB, d), lambda i: (0, 0))
    return pl.pallas_call(
        body, grid=(s_len // t,),
        in_specs=[row, _resident((1, d), lambda i: (0, 0)), row],
        out_specs=[part, row, row, part],
        out_shape=[jax.ShapeDtypeStruct((SUB, d), F32), jax.ShapeDtypeStruct((s_len, d), F32),
                   jax.ShapeDtypeStruct((s_len, d), BF16), jax.ShapeDtypeStruct((SUB, d), F32)],
        compiler_params=_params("arbitrary"), name=name,
    )(x, g, target)


def _ffn_bwd(dx2, uf, cg, cv, wf, w_down, *, name, dep=None):
    s_len, ff2 = uf.shape
    ff = ff2 // 2
    d = dx2.shape[1]
    t = _seq_tile(s_len)
    n_t = s_len // t
    fc = _ff_chunk(ff)
    nc = ff // fc

    def body(dx_ref, ug_ref, uv_ref, cg_ref, cv_ref, wfg_ref, wfv_ref, wd_ref, dep_ref,
             duf_ref, dwg_ref, dwv_ref, dact, dgw, dvw, awg, awv, wrep_g, wrep_v):
        i = pl.program_id(1)

        @pl.when(i == 0)
        def _():
            dgw[t:t + HALO3, :] = jnp.zeros((HALO3, fc), F32)
            dvw[t:t + HALO3, :] = jnp.zeros((HALO3, fc), F32)
            awg[...] = jnp.zeros_like(awg)
            awv[...] = jnp.zeros_like(awv)

        _replicate_taps(wfg_ref, wrep_g, SHORT_K)
        _replicate_taps(wfv_ref, wrep_v, SHORT_K)

        def blk(rows):
            gv = cg_ref[rows, :]
            sg = _sigmoid(gv)
            da = dact[rows, :].astype(BF16)
            dgw[rows, :] = ((da * cv_ref[rows, :]) * (sg * (1.0 + gv * (1.0 - sg)))).astype(F32)
            dvw[rows, :] = (da * (gv * sg)).astype(F32)

        dact[...] = _dot_nt(dx_ref[...], wd_ref[...])
        _row_loop(t, 64, blk)

        _conv_bwd_taps(dgw, wrep_g, ug_ref, duf_ref.at[0], awg, taps=SHORT_K, n_rows=t, width=fc)
        _conv_bwd_taps(dvw, wrep_v, uv_ref, duf_ref.at[1], awv, taps=SHORT_K, n_rows=t, width=fc)
        dgw[t:t + HALO3, :] = dgw[0:HALO3, :]
        dvw[t:t + HALO3, :] = dvw[0:HALO3, :]

        @pl.when(i == n_t - 1)
        def _():
            dwg_ref[...] = _fold8(awg, SHORT_K)
            dwv_ref[...] = _fold8(awv, SHORT_K)

    rev = lambda i: n_t - 1 - i
    gate = pl.BlockSpec((t, fc), lambda j, i: (rev(i), j))
    value = pl.BlockSpec((t, fc), lambda j, i: (rev(i), j + nc))
    return pl.pallas_call(
        body, grid=(nc, n_t),
        in_specs=[pl.BlockSpec((t, d), lambda j, i: (rev(i), 0)), gate, value, gate, gate,
                  pl.BlockSpec((SHORT_K, fc), lambda j, i: (0, j)),
                  pl.BlockSpec((SHORT_K, fc), lambda j, i: (0, j + nc)),
                  pl.BlockSpec((fc, d), lambda j, i: (j, 0)), ANY],
        out_specs=[pl.BlockSpec((2, t, fc), lambda j, i: (0, rev(i), j)),
                   pl.BlockSpec((SHORT_K, fc), lambda j, i: (0, j)), pl.BlockSpec((SHORT_K, fc), lambda j, i: (0, j))],
        out_shape=[jax.ShapeDtypeStruct((2, s_len, ff), BF16),
                   jax.ShapeDtypeStruct((SHORT_K, ff), F32), jax.ShapeDtypeStruct((SHORT_K, ff), F32)],
        scratch_shapes=[pltpu.VMEM((t, fc), F32),
                        pltpu.VMEM((t + HALO3, fc), F32), pltpu.VMEM((t + HALO3, fc), F32),
                        pltpu.VMEM((SHORT_K * SUB, fc), F32), pltpu.VMEM((SHORT_K * SUB, fc), F32),
                        pltpu.VMEM((SHORT_K * SUB, fc), F32), pltpu.VMEM((SHORT_K * SUB, fc), F32)],
        compiler_params=_params("arbitrary", "arbitrary"), name=name,
    )(dx2, uf, uf, cg, cv, wf, wf, w_down, uf if dep is None else dep)


def _mix_bwd(dx1, u, ca, cb, wa, lg, lb, wb, w_out, *, name, dep=None):
    s_len, d_in = u.shape
    d = dx1.shape[1]
    c = D_CONF
    t = _seq_tile(s_len)
    n_t = s_len // t

    def body(dx_ref, u_ref, ca_ref, cb_ref, wa_ref, lg_ref, lb_ref, wb_ref, wo_ref, dep_ref,
             du_ref, dwa_ref, dwb_ref, dba_ref, dlg_ref, dlb_ref, dbin_ref,
             glu, prod, dyc, dcaw, dcbw, dglu, dp, awa, awb, wrep_a, wrep_b, shf):
        i = pl.program_id(0)
        _replicate_taps(wa_ref, wrep_a, CONF_K)
        _replicate_taps(wb_ref, wrep_b, SHORT_K)

        @pl.when(i == 0)
        def _():
            dcaw[t:t + HALO, :] = jnp.zeros((HALO, c), F32)
            dcbw[t:t + HALO3, :] = jnp.zeros((HALO3, c), F32)
            awa[...] = jnp.zeros_like(awa)
            awb[...] = jnp.zeros_like(awb)
            dba_ref[...] = jnp.zeros_like(dba_ref)
            dlg_ref[...] = jnp.zeros_like(dlg_ref)
            dlb_ref[...] = jnp.zeros_like(dlb_ref)
            dbin_ref[...] = jnp.zeros_like(dbin_ref)

        def blk1(rows):
            cv = ca_ref[rows, :]
            mu = jnp.mean(cv, axis=-1, keepdims=True)
            xc = cv - mu
            rstd = lax.rsqrt(jnp.mean(xc * xc, axis=-1, keepdims=True) + EPS)
            nrm = xc * rstd
            ln = nrm * lg_ref[...] + lb_ref[...]
            sg = _sigmoid(ln)
            dln = dyc[rows, 0:c] * (sg * (1.0 + ln * (1.0 - sg)))
            dlg_ref[...] += _rows8(dln * nrm)
            dlb_ref[...] += _rows8(dln)
            dn = dln * lg_ref[...]
            dca = rstd * (dn - jnp.mean(dn, axis=-1, keepdims=True)
                          - nrm * jnp.mean(dn * nrm, axis=-1, keepdims=True))
            dcaw[rows, :] = dca
            dba_ref[...] += _rows8(dca)
            ds = dyc[rows, c:2 * c]
            dgb = ds * cb_ref[rows, :]
            dcbw[rows, :] = ds * u_ref[rows, 2 * c:3 * c].astype(F32)
            du_ref[rows, 2 * c:3 * c] = dgb.astype(BF16)
            dbin_ref[:, 2 * c:3 * c] += _rows8(dgb)
            glu[rows, :] = u_ref[rows, 0:c].astype(F32) * _sigmoid(u_ref[rows, c:2 * c].astype(F32))
            prod[rows, :] = u_ref[rows, 3 * c:4 * c].astype(F32) * u_ref[rows, 4 * c:5 * c].astype(F32)
        half = t // 2
        rb = min(64, half)
        for lo in range(0, t, half):
            dyc[lo:lo + half, :] = _dot_nt(dx_ref[lo:lo + half, :], wo_ref[...])
            for r0 in range(lo, lo + half, rb):
                blk1(pl.ds(r0, rb))

        _conv_bwd_taps(dcaw, wrep_a, glu, dglu, awa, taps=CONF_K, n_rows=t, width=c, shf=shf)
        _conv_bwd_taps(dcbw, wrep_b, prod, dp, awb, taps=SHORT_K, n_rows=t, width=c)
        dcaw[t:t + HALO, :] = dcaw[0:HALO, :]
        dcbw[t:t + HALO3, :] = dcbw[0:HALO3, :]

        def blk2(rows):
            av = u_ref[rows, 0:c].astype(F32)
            sg = _sigmoid(u_ref[rows, c:2 * c].astype(F32))
            dg = dglu[rows, :]
            d_av = dg * sg
            d_ag = (dg * av) * (sg * (1.0 - sg))
            dpv = dp[rows, :]
            d_gc = dpv * u_ref[rows, 4 * c:5 * c].astype(F32)
            d_vs = dpv * u_ref[rows, 3 * c:4 * c].astype(F32)
            du_ref[rows, 0:c] = d_av.astype(BF16)
            du_ref[rows, c:2 * c] = d_ag.astype(BF16)
            du_ref[rows, 3 * c:4 * c] = d_gc.astype(BF16)
            du_ref[rows, 4 * c:5 * c] = d_vs.astype(BF16)
            dbin_ref[:, 0:c] += _rows8(d_av)
            dbin_ref[:, c:2 * c] += _rows8(d_ag)
            dbin_ref[:, 3 * c:4 * c] += _rows8(d_gc)
            dbin_ref[:, 4 * c:5 * c] += _rows8(d_vs)
        _row_loop(t, 64, blk2)

        @pl.when(i == n_t - 1)
        def _():
            dwa_ref[...] = _fold8(awa, CONF_K)
            dwb_ref[...] = _fold8(awb, SHORT_K)

    rev = lambda i: n_t - 1 - i
    small_in = lambda r: _resident((r, c), lambda i: (0, 0))
    small = lambda r: pl.BlockSpec((r, c), lambda i: (0, 0))
    return pl.pallas_call(
        body, grid=(n_t,),
        in_specs=[pl.BlockSpec((t, d), lambda i: (rev(i), 0)),
                  pl.BlockSpec((t, d_in), lambda i: (rev(i), 0)),
                  pl.BlockSpec((t, c), lambda i: (rev(i), 0)), pl.BlockSpec((t, c), lambda i: (rev(i), 0)),
                  small_in(CONF_K), small_in(1), small_in(1), small_in(SHORT_K),
                  _resident((2 * c, d), lambda i: (0, 0)), ANY],
        out_specs=[pl.BlockSpec((t, d_in), lambda i: (rev(i), 0)),
                   small(CONF_K), small(SHORT_K), small(SUB), small(SUB), small(SUB),
                   pl.BlockSpec((SUB, d_in), lambda i: (0, 0))],
        out_shape=[jax.ShapeDtypeStruct((s_len, d_in), BF16),
                   jax.ShapeDtypeStruct((CONF_K, c), F32), jax.ShapeDtypeStruct((SHORT_K, c), F32),
                   jax.ShapeDtypeStruct((SUB, c), F32), jax.ShapeDtypeStruct((SUB, c), F32),
                   jax.ShapeDtypeStruct((SUB, c), F32), jax.ShapeDtypeStruct((SUB, d_in), F32)],
        scratch_shapes=[pltpu.VMEM((t, c), F32), pltpu.VMEM((t, c), F32), pltpu.VMEM((t, 2 * c), F32),
                        pltpu.VMEM((t + HALO, c), F32), pltpu.VMEM((t + HALO3, c), F32),
                        pltpu.VMEM((t, c), F32), pltpu.VMEM((t, c), F32),
                        pltpu.VMEM((CONF_K * SUB, c), F32), pltpu.VMEM((SHORT_K * SUB, c), F32),
                        pltpu.VMEM((CONF_K * SUB, c), F32), pltpu.VMEM((SHORT_K * SUB, c), F32),
                        pltpu.VMEM((SUB - 1, t + HALO, c), F32)],
        compiler_params=_params("arbitrary"), name=name,
    )(dx1, u, ca, cb, wa, lg, lb, wb, w_out, u if dep is None else dep)


def _matmul_tn(a, b, *, name):
    n_p, s_len, k = a.shape
    n = b.shape[1]
    tk = _col_tile(k)
    per = k // tk
    if per > 2:
        def body(a_ref, b_ref, o_ref):
            o_ref[...] = _dot_tn(a_ref[...], b_ref[...]).astype(BF16)

        return pl.pallas_call(
            body, grid=(n_p, per),
            in_specs=[pl.BlockSpec((None, s_len, tk), lambda p, j: (p, 0, j)),
                      _resident((s_len, n), lambda p, j: (0, 0))],
            out_specs=pl.BlockSpec((tk, n), lambda p, j: (p * per + j, 0)),
            out_shape=jax.ShapeDtypeStruct((n_p * k, n), BF16),
            compiler_params=_params("parallel", "parallel"), name=name,
        )(a, b)

    half = s_len // 2

    def body_halves(a_ref, b_ref, o_ref, acc):
        @pl.when(pl.program_id(2) == 0)
        def _():
            acc[...] = _dot_tn(a_ref[...], b_ref[...])

        @pl.when(pl.program_id(2) == 1)
        def _():
            o_ref[...] = (acc[...] + _dot_tn(a_ref[...], b_ref[...])).astype(BF16)

    return pl.pallas_call(
        body_halves, grid=(n_p, per, 2),
        in_specs=[pl.BlockSpec((None, half, tk), lambda p, j, q: (p, q, j)),
                  pl.BlockSpec((half, n), lambda p, j, q: (q, 0))],
        out_specs=pl.BlockSpec((tk, n), lambda p, j, q: (p * per + j, 0)),
        out_shape=jax.ShapeDtypeStruct((n_p * k, n), BF16),
        scratch_shapes=[pltpu.VMEM((tk, n), F32)],
        compiler_params=_params("parallel", "parallel", "arbitrary"), name=name,
    )(a, b)


def _matmul_rmsbwd(dzs, wt, x, g, dx_in, *, name, dep=None):
    s_len, d = x.shape
    n_z, _, nj = dzs.shape
    t = _mm_tile(s_len)

    def body(*refs):
        dz_refs = refs[0:n_z]
        w_refs = refs[n_z:2 * n_z]
        x_ref, g_ref, dxi_ref, _, dx_ref, dxb_ref, dg_ref, dh = refs[2 * n_z:]

        @pl.when(pl.program_id(0) == 0)
        def _():
            dg_ref[...] = jnp.zeros_like(dg_ref)

        def blk(rows):
            xv = x_ref[rows, :]
            r = lax.rsqrt(jnp.mean(xv * xv, axis=-1, keepdims=True) + EPS)
            xn = xv * r
            dhv = dh[rows, :]
            dg_ref[...] += _rows8(dhv * xn)
            dn = dhv * g_ref[...]
            dx = dxi_ref[rows, :] + r * (dn - xn * jnp.mean(dn * xn, axis=-1, keepdims=True))
            dx_ref[rows, :] = dx
            dxb_ref[rows, :] = dx.astype(BF16)

        half = t // 2
        rb = min(128, half)
        for lo in range(0, t, half):
            acc = _dot(dz_refs[0][lo:lo + half, :], w_refs[0][...])
            for q in range(1, n_z):
                acc = acc + _dot(dz_refs[q][lo:lo + half, :], w_refs[q][...])
            dh[lo:lo + half, :] = acc
            for r0 in range(lo, lo + half, rb):
                blk(pl.ds(r0, rb))

    row = pl.BlockSpec((t, d), lambda i: (i, 0))
    in_specs = [pl.BlockSpec((None, t, nj), functools.partial(lambda q, i: (q, i, 0), q)) for q in range(n_z)]
    in_specs += [_resident((nj, d), functools.partial(lambda q, i: (q, 0), q)) for q in range(n_z)]
    in_specs += [row, _resident((1, d), lambda i: (0, 0)), row, ANY]
    return pl.pallas_call(
        body, grid=(s_len // t,), in_specs=in_specs,
        out_specs=[row, row, pl.BlockSpec((SUB, d), lambda i: (0, 0))],
        out_shape=[jax.ShapeDtypeStruct((s_len, d), F32), jax.ShapeDtypeStruct((s_len, d), BF16),
                   jax.ShapeDtypeStruct((SUB, d), F32)],
        scratch_shapes=[pltpu.VMEM((t, d), F32)],
        compiler_params=_params("arbitrary"), name=name,
    )(*([dzs] * n_z), *([wt] * n_z), x, g, dx_in, x if dep is None else dep)


def _row(v):
    return v.reshape(1, -1)


def _layer_fwd(x0, p, tag, dep=None, before_up=None):
    u, h1 = _rms_matmul(x0, _row(p["mix_norm_g"]), p["w_in_t"], _row(p["b_in"]), name=f"in_proj_{tag}", dep=dep)
    ycat, x1, ca, cb = _mix_fwd(u, x0, p["conv_a_w"], _row(p["conv_a_b"]), _row(p["ln_a_g"]), _row(p["ln_a_b"]),
                            p["conv_b_w"], p["w_out"], name=f"mix_fwd_{tag}")
    if before_up is not None:
        before_up(x1)
    uf, h2 = _rms_matmul(x1, _row(p["ffn_norm_g"]), p["w_up_t"], None, name=f"up_proj_{tag}")
    act, x2, cg, cv = _ffn_fwd(uf, x1, p["conv_f_w"], p["w_down"], name=f"ffn_fwd_{tag}")
    return x2, dict(x0=x0, h1=h1, u=u, ca=ca, cb=cb, ycat=ycat, x1=x1, h2=h2, uf=uf, cg=cg, cv=cv, act=act)


def _layer_bwd(dx2, dx2_b, p, saved, tag, ffn_grads, ffn_sent, mix_grads, mix_sent, dep=None):
    d_uf, dwf_g, dwf_v = _ffn_bwd(dx2_b, saved["uf"], saved["cg"], saved["cv"], p["conv_f_w"], p["w_down"],
                                  name=f"ffn_bwd_{tag}", dep=dep)
    g_down = _matmul_tn(saved["act"][None], dx2_b, name=f"dw_down_{tag}")
    g_up = _matmul_tn(d_uf, saved["h2"], name=f"dw_up_{tag}")
    dep_ffn = ffn_grads(dict(w_up=g_up, w_down=g_down), dx2_b)
    dx1, dx1_b, dg2 = _matmul_rmsbwd(d_uf, p["w_up_t"], saved["x1"], _row(p["ffn_norm_g"]), dx2,
                                     name=f"dh_ffn_{tag}", dep=dep_ffn)
    du, dwa, dwb, dba, dlg, dlb, dbin = _mix_bwd(
        dx1_b, saved["u"], saved["ca"], saved["cb"], p["conv_a_w"], _row(p["ln_a_g"]), _row(p["ln_a_b"]),
        p["conv_b_w"], p["w_out"], name=f"mix_bwd_{tag}", dep=ffn_sent(dx1_b))
    g_out = _matmul_tn(saved["ycat"][None], dx1_b, name=f"dw_out_{tag}")
    g_in = _matmul_tn(du[None], saved["h1"], name=f"dw_in_{tag}")
    conv = dict(conv_a_w=dwa, conv_b_w=dwb, conv_f_w=jnp.concatenate([dwf_g, dwf_v], axis=1))
    dep_mix = mix_grads(dict(w_in=g_in, w_out=g_out), conv, dx1_b)
    dx0, dx0_b, dg1 = _matmul_rmsbwd(du[None], p["w_in_t"], saved["x0"], _row(p["mix_norm_g"]), dx1,
                                     name=f"dh_mix_{tag}", dep=dep_mix)
    rep = dict(mix_norm_g=dg1, b_in=dbin, conv_a_b=dba, ln_a_g=dlg, ln_a_b=dlb, ffn_norm_g=dg2)
    return dx0, dx0_b, rep, mix_sent(dx0_b)


def _place():
    return lax.axis_index("x"), lax.axis_index("y"), lax.axis_index("c")


def _all_gather(arrs, *, name):
    n_a = len(arrs)

    def body(*refs):
        ins = refs[0:n_a]
        outs = refs[n_a:2 * n_a]
        send_sems, recv_sems, local_sems = refs[2 * n_a:]
        x, y, c = _place()
        sibling = (x, y, 1 - c)
        chips = [(1 - x, y), (x, 1 - y), (1 - x, 1 - y)]

        def slot(a, px, py, pc):
            return outs[a].at[4 * px + 2 * py + pc]

        def copy(a, k, block, to, src=None):
            return pltpu.make_async_remote_copy(
                src_ref=slot(a, *block) if src is None else src, dst_ref=slot(a, *block),
                send_sem=send_sems.at[a, k], recv_sem=recv_sems.at[a, k],
                device_id=to, device_id_type=MESH)

        me = (x, y, c)
        mine = [pltpu.make_async_copy(ins[a], slot(a, *me), local_sems.at[a]) for a in range(n_a)]
        for cp in mine:
            cp.start()
        started = []
        for a in range(n_a):
            first = [copy(a, 0, me, sibling, src=ins[a])]
            first += [copy(a, 1 + j, me, (*chip, c), src=ins[a]) for j, chip in enumerate(chips)]
            for cp in first:
                cp.start()
            started += first
        for a in range(n_a):
            for j, chip in enumerate(chips):
                copy(a, 1 + j, (*chip, c), me).wait_recv()
                passed = copy(a, 4 + j, (*chip, c), sibling)
                passed.start()
                started.append(passed)
        for a in range(n_a):
            copy(a, 0, sibling, me).wait_recv()
            for j, chip in enumerate(chips):
                copy(a, 4 + j, (*chip, 1 - c), me).wait_recv()
        for cp in started:
            cp.wait_send()
        for cp in mine:
            cp.wait()

    return pl.pallas_call(
        body, in_specs=[ANY] * n_a, out_specs=[ANY] * n_a,
        out_shape=[jax.ShapeDtypeStruct((N_DEV, *a.shape), a.dtype) for a in arrs],
        scratch_shapes=[pltpu.SemaphoreType.DMA((n_a, 7)), pltpu.SemaphoreType.DMA((n_a, 7)),
                        pltpu.SemaphoreType.DMA((n_a,))],
        name=name,
    )(*arrs)


def _row_tile(r, cap):
    for tr in range(min(cap, r) // 16 * 16, 0, -16):
        if r % tr == 0:
            return tr
    return r


def _pair_sum(mines, theirs, where, *, name):
    n_a = len(mines)
    n_chip = mines[0].shape[0]

    def body(where_ref, *refs):
        a_refs = refs[0:n_a]
        b_refs = refs[n_a:2 * n_a]
        p_refs = refs[2 * n_a:3 * n_a]
        l_refs = refs[3 * n_a:4 * n_a]
        q = pl.program_id(0)
        for a in range(n_a):
            p_refs[a][...] = (a_refs[a][...].astype(F32) + b_refs[a][...].astype(F32)).astype(p_refs[a].dtype)

        @pl.when(q == where_ref[1])
        def _():
            for a in range(n_a):
                l_refs[a][...] = p_refs[a][...]

    in_specs, out_p, out_l, shapes = [], [], [], []
    for m in mines:
        _, _, r, c = m.shape
        in_specs.append(pl.BlockSpec((None, None, r, c), lambda q, where_ref: (q, where_ref[0], 0, 0)))
    for m in mines:
        _, _, r, c = m.shape
        in_specs.append(pl.BlockSpec((None, r, c), lambda q, where_ref: (q, 0, 0)))
        out_p.append(pl.BlockSpec((None, r, c), lambda q, where_ref: (q, 0, 0)))
        out_l.append(pl.BlockSpec((None, r, c), lambda q, where_ref: (where_ref[1], 0, 0)))
        shapes.append(jax.ShapeDtypeStruct((n_chip, r, c), m.dtype))
    res = pl.pallas_call(
        body,
        grid_spec=pltpu.PrefetchScalarGridSpec(num_scalar_prefetch=1, grid=(n_chip,), in_specs=in_specs,
                                               out_specs=out_p + out_l),
        out_shape=shapes + shapes,
        compiler_params=_params("arbitrary"), name=name,
    )(where, *mines, *theirs)
    return list(res[:n_a]), list(res[n_a:])


HBM = pl.BlockSpec(memory_space=pltpu.HBM)
SEM = pl.BlockSpec(memory_space=pltpu.SEMAPHORE)
EFFECT = pltpu.SideEffectType.DATAFLOW_SIDE_EFFECTING


def _in_hbm(a):
    return pltpu.with_memory_space_constraint(a, pltpu.HBM)


def _split_start(srcs, lands, plan, n_copies, after, *, name):
    n_s, n_l = len(srcs), len(lands)

    def body(*refs):
        src_refs = refs[0:n_s]
        land_refs = refs[n_s:n_s + n_l]
        send_sems, recv_sems = refs[n_s + n_l + 1], refs[n_s + n_l + 2]
        token = refs[-1]
        for cp in plan(src_refs, land_refs, send_sems, recv_sems):
            cp.start()
        token[...] = jnp.zeros_like(token)

    thru = [pltpu.HBM(a.shape, a.dtype) for a in list(srcs) + list(lands)]
    res = pl.pallas_call(
        body, name=name,
        out_shape=(pltpu.SemaphoreType.DMA((n_copies,)), pltpu.SemaphoreType.DMA((n_copies,)), *thru,
                   jax.ShapeDtypeStruct((SUB, LANES), F32)),
        in_specs=[HBM] * (n_s + n_l) + [ANY],
        out_specs=(SEM, SEM, *([HBM] * (n_s + n_l)), pl.BlockSpec(memory_space=pltpu.VMEM)),
        input_output_aliases={i: 2 + i for i in range(n_s + n_l)},
        compiler_params=pltpu.CompilerParams(has_side_effects=EFFECT),
    )(*[_in_hbm(a) for a in srcs], *[_in_hbm(a) for a in lands], _in_hbm(after))
    return res[0], res[1], list(res[2:2 + n_s]), list(res[2 + n_s:2 + n_s + n_l]), res[-1]


def _split_wait(send_sems, recv_sems, srcs, lands, after, plan, *, name):
    n_s, n_l = len(srcs), len(lands)

    def body(*refs):
        src_refs = refs[0:n_s]
        land_refs = refs[n_s:n_s + n_l]
        send, recv = refs[n_s + n_l], refs[n_s + n_l + 1]
        for cp in plan(src_refs, land_refs, send, recv):
            cp.wait_send()
            cp.wait_recv()

    res = pl.pallas_call(
        body, name=name,
        out_shape=tuple(pltpu.HBM(a.shape, a.dtype) for a in list(srcs) + list(lands)),
        in_specs=[HBM] * (n_s + n_l) + [SEM, SEM, ANY],
        out_specs=tuple([HBM] * (n_s + n_l)),
        input_output_aliases={i: i for i in range(n_s + n_l)},
        compiler_params=pltpu.CompilerParams(has_side_effects=EFFECT),
    )(*srcs, *lands, send_sems, recv_sems, _in_hbm(after))
    return list(res[:n_s]), list(res[n_s:])


def _remote(src, dst, send_sems, recv_sems, k, to):
    return pltpu.make_async_remote_copy(src_ref=src, dst_ref=dst, send_sem=send_sems.at[k], recv_sem=recv_sems.at[k],
                                        device_id=to, device_id_type=MESH)


def _gather_plan_first(src_refs, land_refs, send_sems, recv_sems):
    x, y, c = _place()
    me = 4 * x + 2 * y + c
    peers = [(x, y, 1 - c), (1 - x, y, c), (x, 1 - y, c), (1 - x, 1 - y, c)]
    return [_remote(land.at[me], land.at[me], send_sems, recv_sems, 4 * a + k, to)
            for a, land in enumerate(land_refs) for k, to in enumerate(peers)]


def _gather_plan_second(src_refs, land_refs, send_sems, recv_sems):
    x, y, c = _place()
    chips = [(1 - x, y), (x, 1 - y), (1 - x, 1 - y)]
    out = []
    for a, land in enumerate(land_refs):
        for j, (px, py) in enumerate(chips):
            slot = land.at[4 * px + 2 * py + c]
            out.append(_remote(slot, slot, send_sems, recv_sems, 3 * a + j, (x, y, 1 - c)))
    return out


def _siblings_plan(src_refs, land_refs, send_sems, recv_sems):
    x, y, c = _place()
    return [_remote(src.at[:, 1 - c], land, send_sems, recv_sems, a, (x, y, 1 - c))
            for a, (src, land) in enumerate(zip(src_refs, land_refs))]


def _chips_plan(src_refs, land_refs, send_sems, recv_sems):
    x, y, c = _place()
    my_chip = 2 * x + y
    chips = [(1 - x, y), (x, 1 - y), (1 - x, 1 - y)]
    return [_remote(src.at[2 * px + py], land.at[my_chip], send_sems, recv_sems, 3 * a + j, (px, py, c))
            for a, (src, land) in enumerate(zip(src_refs, land_refs)) for j, (px, py) in enumerate(chips)]


def _gather_landings(shards, me, *, name):
    blank = _unwritten([jax.ShapeDtypeStruct((N_DEV, *s.shape), s.dtype) for s in shards], name=name)
    return [lax.dynamic_update_index_in_dim(b, s, me, 0) for b, s in zip(blank, shards)]


def _adamw_math(g, w, m, v):
    m = ADAM_B1 * m + (1.0 - ADAM_B1) * g
    v = ADAM_B2 * v + (1.0 - ADAM_B2) * (g * g)
    m_hat = m / (1.0 - ADAM_B1 ** ADAM_STEP)
    v_hat = v / (1.0 - ADAM_B2 ** ADAM_STEP)
    delta = -ADAM_LR * (m_hat / (jnp.sqrt(v_hat) + ADAM_EPS) + ADAM_WD * w)
    return delta, m, v


def _adamw_sharded(parts, w, m, v, *, name, dep=None):
    n_layers, r, c = w.shape
    n_chip = parts[0].shape[0]
    tr = _row_tile(r, 384)
    n_i = r // tr

    def body(*refs):
        p_refs = refs[0:n_layers]
        w_ref, m_ref, v_ref, _, g_out, d_out, m_out, v_out = refs[n_layers:]
        layer = pl.program_id(0)
        for l in range(n_layers):
            @pl.when(layer == l)
            def _(l=l):
                g = p_refs[l][0].astype(F32)
                for q in range(1, n_chip):
                    g = g + p_refs[l][q].astype(F32)
                delta, m_new, v_new = _adamw_math(g, w_ref[...], m_ref[...], v_ref[...])
                g_out[...] = g
                d_out[...] = delta
                m_out[...] = m_new
                v_out[...] = v_new

    def part_map(l):
        return lambda layer, i: (0, jnp.where(layer == l, i, jnp.where(layer < l, 0, n_i - 1)), 0)

    blk = pl.BlockSpec((None, tr, c), lambda layer, i: (layer, i, 0))
    return pl.pallas_call(
        body, grid=(n_layers, n_i),
        in_specs=[pl.BlockSpec((n_chip, tr, c), part_map(l)) for l in range(n_layers)] + [blk, blk, blk, ANY],
        out_specs=[blk] * 4, out_shape=[jax.ShapeDtypeStruct((n_layers, r, c), F32)] * 4,
        compiler_params=_params("arbitrary", "arbitrary"), name=name,
    )(*parts, w, m, v, w if dep is None else dep)


def _fold_partials(cols, *, name):
    widths = [c.shape[1] for c in cols]

    def body(*refs):
        o_ref = refs[-1]
        pos = 0
        for ref, width in zip(refs[:-1], widths):
            o_ref[:, pos:pos + width] = jnp.sum(ref[...], axis=0, keepdims=True)
            pos += width

    return pl.pallas_call(body, out_shape=jax.ShapeDtypeStruct((1, sum(widths)), F32), name=name)(*cols)


def _adamw_replicated(parts, names, w, m, v, n_loss, *, name):
    n_dev = parts.shape[0]
    n_layers = w[names[0]].shape[0]
    every = list(names) + ["final_norm_g"]
    n_p = len(every)

    def body(*refs):
        p_ref = refs[0]
        w_refs = dict(zip(every, refs[1:1 + n_p]))
        m_refs = dict(zip(every, refs[1 + n_p:1 + 2 * n_p]))
        v_refs = dict(zip(every, refs[1 + 2 * n_p:1 + 3 * n_p]))
        l_out = refs[1 + 3 * n_p]
        outs = refs[2 + 3 * n_p:]
        o_refs = {n: outs[4 * q:4 * q + 4] for q, n in enumerate(every)}
        acc = p_ref[0]
        for q in range(1, n_dev):
            acc = acc + p_ref[q]
        tot = jnp.sum(acc, axis=0, keepdims=True)
        pos = 0
        where = [(n, l) for l in range(n_layers) for n in names] + [("final_norm_g", 0)]
        for n, l in where:
            width = w_refs[n].shape[1]
            g = tot[:, pos:pos + width]
            pos += width
            row = pl.ds(l, 1)
            delta, m_new, v_new = _adamw_math(g, w_refs[n][row, :], m_refs[n][row, :], v_refs[n][row, :])
            for o, val in zip(o_refs[n], (g, delta, m_new, v_new)):
                o[row, :] = val
        l_out[...] = (0.5 / n_loss) * jnp.sum(tot[:, pos:pos + n_loss], axis=-1, keepdims=True)

    shapes = [jax.ShapeDtypeStruct((1, 1), F32)]
    for n in every:
        shapes += [jax.ShapeDtypeStruct(w[n].shape, F32)] * 4
    res = pl.pallas_call(
        body, out_shape=shapes,
        compiler_params=pltpu.CompilerParams(vmem_limit_bytes=VMEM_LIMIT), name=name,
    )(parts, *[w[n] for n in every], *[m[n] for n in every], *[v[n] for n in every])
    return res[0], {n: res[1 + 4 * q:5 + 4 * q] for q, n in enumerate(every)}


BIG = ("w_in", "w_out", "w_up", "w_down")
COL_SHARDED = ("w_in", "w_up")
CONV = ("conv_a_w", "conv_b_w", "conv_f_w")
REPLICATED = ("mix_norm_g", "b_in", "conv_a_b", "ln_a_g", "ln_a_b", "ffn_norm_g")
KINDS = ("grad", "delta", "m", "v")
FFN_PART = ("w_up", "w_down")
MIX_PART = ("w_in", "w_out")


def _weights_from_gathered(g):
    n_dev, r, c = g.shape
    return g.reshape(n_dev * r, c)


def _slabs_from_full(grad):
    return grad.reshape(N_DEV, grad.shape[0] // N_DEV, grad.shape[1])


def _unwritten(like, *, name):
    return pl.pallas_call(lambda *refs: None, out_specs=[ANY] * len(like), out_shape=list(like), name=name)()


def kernel(x, mix_norm_g, w_in, b_in, conv_a_w, conv_a_b, ln_a_g, ln_a_b, conv_b_w, w_out, ffn_norm_g, w_up, conv_f_w, w_down, final_norm_g, loss_target, m_mix_norm_g, m_w_in, m_b_in, m_conv_a_w, m_conv_a_b, m_ln_a_g, m_ln_a_b, m_conv_b_w, m_w_out, m_ffn_norm_g, m_w_up, m_conv_f_w, m_w_down, m_final_norm_g, v_mix_norm_g, v_w_in, v_b_in, v_conv_a_w, v_conv_a_b, v_ln_a_g, v_ln_a_b, v_conv_b_w, v_w_out, v_ffn_norm_g, v_w_up, v_conv_f_w, v_w_down, v_final_norm_g):
    w = dict(mix_norm_g=mix_norm_g, w_in=w_in, b_in=b_in, conv_a_w=conv_a_w, conv_a_b=conv_a_b, ln_a_g=ln_a_g,
             ln_a_b=ln_a_b, conv_b_w=conv_b_w, w_out=w_out, ffn_norm_g=ffn_norm_g, w_up=w_up, conv_f_w=conv_f_w,
             w_down=w_down, final_norm_g=final_norm_g)
    m = dict(mix_norm_g=m_mix_norm_g, w_in=m_w_in, b_in=m_b_in, conv_a_w=m_conv_a_w, conv_a_b=m_conv_a_b,
             ln_a_g=m_ln_a_g, ln_a_b=m_ln_a_b, conv_b_w=m_conv_b_w, w_out=m_w_out, ffn_norm_g=m_ffn_norm_g,
             w_up=m_w_up, conv_f_w=m_conv_f_w, w_down=m_w_down, final_norm_g=m_final_norm_g)
    v = dict(mix_norm_g=v_mix_norm_g, w_in=v_w_in, b_in=v_b_in, conv_a_w=v_conv_a_w, conv_a_b=v_conv_a_b,
             ln_a_g=v_ln_a_g, ln_a_b=v_ln_a_b, conv_b_w=v_conv_b_w, w_out=v_w_out, ffn_norm_g=v_ffn_norm_g,
             w_up=v_w_up, conv_f_w=v_conv_f_w, w_down=v_w_down, final_norm_g=v_final_norm_g)
    order = list(w)
    n_layers = w_in.shape[0]
    xs = x[0]
    target = loss_target[0]
    flip = lambda a: jnp.transpose(a, (0, 2, 1))
    wt, mt, vt = ({n: flip(d[n]) if n in COL_SHARDED else d[n] for n in BIG} for d in (w, m, v))
    px, py, pc = _place()
    where = jnp.stack([pc, 2 * px + py]).astype(jnp.int32)
    me = 4 * px + 2 * py + pc

    assert BIG == MIX_PART + FFN_PART
    key = lambda n: n + "_t" if n in COL_SHARDED else n
    shard = lambda n, l: wt[n][l].astype(BF16)

    def gather_start(names, l, after, tag, behind=None):
        if behind is None:
            shards = [shard(n, l) for n in names]
        else:
            shards = [s.astype(BF16) for s in lax.optimization_barrier(([wt[n][l] for n in names], behind))[0]]
        lands = _gather_landings(shards, me, name=f"gather_landing_{tag}")
        return _split_start([], lands, _gather_plan_first, 4 * len(lands), after, name=f"gather_first_start_{tag}")

    def gather_mid(first, after, tag):
        return _split_wait(first[0], first[1], first[2], first[3], after, _gather_plan_first,
                           name=f"gather_first_wait_{tag}")[1]

    def forward_start(lands, after, tag):
        return _split_start([], lands, _gather_plan_second, 3 * len(lands), after, name=f"gather_second_start_{tag}")

    def forward_finish(second, after, tag):
        return _split_wait(second[0], second[1], [], second[3], after, _gather_plan_second,
                           name=f"gather_second_wait_{tag}")[1]

    gathered = _all_gather([shard(n, 0) for n in MIX_PART] + [w[n] for n in CONV], name="gather_weights_0")
    params = [{n: w[n][l] for n in REPLICATED} for l in range(n_layers)]
    for n, g in zip(CONV, gathered[len(MIX_PART):]):
        n_dev, _, taps, c = g.shape
        full = g.transpose(1, 2, 0, 3).reshape(n_layers, taps, n_dev * c)
        for l in range(n_layers):
            params[l][n] = full[l]
    for n, g in zip(MIX_PART, gathered):
        params[0][key(n)] = _weights_from_gathered(g)
    ffn_first = gather_start(FFN_PART, 0, gathered[0], "0_ffn")
    pending = {}

    h = xs
    saved = []
    for l in range(n_layers):
        nxt = l + 1 if l + 1 < n_layers else None

        def before_up(x1, l=l, nxt=nxt):
            if l == 0:
                second = forward_start(gather_mid(ffn_first, x1, "0_ffn"), x1, "0_ffn")
                after = second[4]
            else:
                second = pending[l]["ffn"]
                after = x1
            if nxt is not None:
                pending[nxt] = dict(first=gather_start(BIG, nxt, after, str(nxt), behind=ffn_first[4]))
                after = pending[nxt]["first"][4]
            for n, g in zip(FFN_PART, forward_finish(second, after, f"{l}_ffn")):
                params[l][key(n)] = _weights_from_gathered(g)

        h, keep = _layer_fwd(h, params[l], str(l), dep=ffn_first[4] if l == 0 else None, before_up=before_up)
        saved.append(keep)
        if nxt is not None:
            arrived = gather_mid(pending[nxt]["first"], h, str(nxt))
            mix_second = forward_start(arrived[:len(MIX_PART)], h, f"{nxt}_mix")
            pending[nxt]["ffn"] = forward_start(arrived[len(MIX_PART):], mix_second[4], f"{nxt}_ffn")
            for n, g in zip(MIX_PART, forward_finish(mix_second, pending[nxt]["ffn"][4], f"{nxt}_mix")):
                params[nxt][key(n)] = _weights_from_gathered(g)

    def start_siblings(slabs, after, tag):
        mines = [s.reshape(N_CHIP, 2, *s.shape[1:]) for s in slabs]
        lands = _unwritten([jax.ShapeDtypeStruct((N_CHIP, *m.shape[2:]), m.dtype) for m in mines],
                           name=f"reduce_siblings_landing_{tag}")
        return _split_start(mines, lands, _siblings_plan, len(mines), after, name=f"reduce_siblings_start_{tag}")

    def start_chips(sib, after, tag):
        mines, theirs = _split_wait(sib[0], sib[1], sib[2], sib[3], after, _siblings_plan,
                                    name=f"reduce_siblings_wait_{tag}")
        pairs, lands = _pair_sum(mines, theirs, where, name=f"pair_sum_{tag}")
        return _split_start(pairs, lands, _chips_plan, 3 * len(pairs), after, name=f"reduce_chips_start_{tag}")

    def finish_reduce(fly, after, tag):
        return _split_wait(fly[0], fly[1], fly[2], fly[3], after, _chips_plan, name=f"reduce_chips_wait_{tag}")[1]

    loss_sq, dh, dh_b, dgf = _loss_bwd(h, _row(final_norm_g), target, name="loss")
    conv_g = {n: [None] * n_layers for n in CONV}
    rep_g = [None] * n_layers
    siblings = {}
    flights = {}
    token = None
    for l in reversed(range(n_layers)):
        def ffn_grads(g, after, l=l):
            siblings[l, "ffn"] = start_siblings([_slabs_from_full(g[n]) for n in FFN_PART], after, f"{l}_ffn")
            return siblings[l, "ffn"][4]

        def ffn_sent(after, l=l):
            flights[l, "ffn"] = start_chips(siblings[l, "ffn"], after, f"{l}_ffn")
            return flights[l, "ffn"][4]

        def mix_grads(g, conv, after, l=l):
            for n in CONV:
                conv_g[n][l] = conv[n]
            slabs = [_slabs_from_full(g[n]) for n in MIX_PART]
            if l == 0:
                for n in CONV:
                    full = jnp.stack(conv_g[n])
                    _, taps, c = full.shape
                    slabs.append(full.reshape(n_layers, taps, N_DEV, c // N_DEV).transpose(2, 0, 1, 3)
                                 .reshape(N_DEV, n_layers * taps, c // N_DEV))
            siblings[l, "mix"] = start_siblings(slabs, after, f"{l}_mix")
            return siblings[l, "mix"][4]

        def mix_sent(after, l=l):
            flights[l, "mix"] = start_chips(siblings[l, "mix"], after, f"{l}_mix")
            return flights[l, "mix"][4]

        dh, dh_b, rep_g[l], token = _layer_bwd(dh, dh_b, params[l], saved[l], str(l), ffn_grads, ffn_sent,
                                               mix_grads, mix_sent, dep=token)

    sums = {key: finish_reduce(fly, dh, f"{key[0]}_{key[1]}") for key, fly in flights.items() if key != (0, "mix")}
    out = {k: {} for k in KINDS}

    def adamw_big(names, part, dep):
        for q, n in enumerate(names):
            layer_parts = [sums[l, part][q] for l in range(n_layers)]
            res = _adamw_sharded(layer_parts, wt[n], mt[n], vt[n], name=f"adamw_{n}", dep=dep)
            for k, r in zip(KINDS, res):
                out[k][n] = flip(r) if n in COL_SHARDED else r

    adamw_big(FFN_PART, "ffn", token)

    rep_cols = [rep_g[l][n] for l in range(n_layers) for n in REPLICATED] + [dgf, loss_sq]
    rep_all = _all_gather([_fold_partials(rep_cols, name="fold_small")], name="gather_small")[0]
    with_final = lambda d: {**{n: d[n] for n in REPLICATED}, "final_norm_g": _row(d["final_norm_g"])}
    loss, rep_res = _adamw_replicated(rep_all, REPLICATED, with_final(w), with_final(m), with_final(v),
                                      loss_sq.shape[1], name="adamw_small")
    for n, res in rep_res.items():
        for k, r in zip(KINDS, res):
            out[k][n] = r.reshape(w[n].shape)

    last = finish_reduce(flights[0, "mix"], rep_res["b_in"][0], "0_mix")
    sums[0, "mix"] = last[:len(MIX_PART)]
    adamw_big(MIX_PART, "mix", None)
    for n, p in zip(CONV, last[len(MIX_PART):]):
        as_one = lambda a: a.reshape(1, *p.shape[1:])
        for k, r in zip(KINDS, _adamw_sharded([p], as_one(w[n]), as_one(m[n]), as_one(v[n]), name=f"adamw_{n}")):
            out[k][n] = r.reshape(w[n].shape)

    grad_x = dh.reshape(x.shape)
    return (loss.reshape(()), grad_x, *[out["grad"][n] for n in order], *[out["delta"][n] for n in order],
            *[out["m"][n] for n in order], *[out["v"][n] for n in order])
```

```python
import functools

import jax
import jax.numpy as jnp
from jax import lax
from jax.experimental import pallas as pl
from jax.experimental.pallas import tpu as pltpu

F32 = jnp.float32
BF16 = jnp.bfloat16

N_DEV = 8
N_CHIP = 4
D_CONF = 512
CONF_K = 31
SHORT_K = 3
EPS = 1e-6
HALO = 32
HALO3 = 8
HALO3_BLK = 16
LANES = 128
SUB = 8
VMEM_LIMIT = 56 * 1024 * 1024

ADAM_LR = 0.001
ADAM_B1 = 0.9
ADAM_B2 = 0.999
ADAM_EPS = 1e-08
ADAM_WD = 0.01
ADAM_STEP = 10

MESH = pl.DeviceIdType.MESH
ANY = pl.BlockSpec(memory_space=pl.ANY)


def _params(*sem):
    return pltpu.CompilerParams(dimension_semantics=sem, vmem_limit_bytes=VMEM_LIMIT)


def _resident(shape, index_map):
    return pl.BlockSpec(shape, index_map, pipeline_mode=pl.Buffered(1))


def _row_loop(n_rows, rb, fn, unroll=1):
    rb = min(rb, n_rows)

    def body(i, carry):
        fn(pl.ds(pl.multiple_of(i * rb, rb), rb))
        return carry
    lax.fori_loop(0, n_rows // rb, body, 0, unroll=unroll)


def _rows8(v):
    acc = v[0:SUB]
    for k in range(1, v.shape[0] // SUB):
        acc = acc + v[k * SUB:(k + 1) * SUB]
    return acc


def _sigmoid(z):
    return 0.5 * jnp.tanh(0.5 * z) + 0.5


def _dot(a, b):
    return jnp.dot(a, b, preferred_element_type=F32)


def _dot_nt(a, b):
    return lax.dot_general(a, b, (((1,), (1,)), ((), ())), preferred_element_type=F32)


def _dot_tn(a, b):
    return lax.dot_general(a, b, (((0,), (0,)), ((), ())), preferred_element_type=F32)


def _replicate_taps(w_ref, wrep, taps):
    for k in range(taps):
        wrep[pl.ds(k * SUB, SUB), :] = jnp.broadcast_to(w_ref[pl.ds(k, 1), :], (SUB, w_ref.shape[1]))


def _shift_copies(win, shf, lanes):
    span = win.shape[0] - SUB
    for r in range(1, SUB):
        for j0 in range(0, span, 64):
            n = min(64, span - j0)
            shf[r - 1, pl.ds(j0, n), lanes] = win[pl.ds(j0 + r, n), lanes]


def _rows_at(win, shf, off, rb, lanes):
    if shf is None or off % SUB == 0:
        return win[pl.ds(off, rb), lanes]
    return shf[off % SUB - 1, pl.ds(off - off % SUB, rb), lanes]


def _conv_taps(win, wrep, out, *, taps, n_rows, base, width, transposed=False, bias_ref=None, shf=None):
    rb = min(32 if taps > 8 else 64, n_rows)

    def lane_body(cb, carry):
        lanes = pl.ds(pl.multiple_of(cb * LANES, LANES), LANES)
        if shf is not None:
            _shift_copies(win, shf, lanes)
        for r0 in range(0, n_rows, rb):
            acc = None
            for k in range(taps):
                off = (taps - 1 - k) if transposed else (k - (taps - 1))
                wk = jnp.tile(wrep[pl.ds(k * SUB, SUB), lanes], (rb // SUB, 1))
                term = wk * _rows_at(win, shf, base + r0 + off, rb, lanes)
                acc = term if acc is None else acc + term
            if bias_ref is not None:
                acc = acc + bias_ref[:, lanes]
            out[pl.ds(r0, rb), lanes] = acc.astype(out.dtype)
        return carry

    lax.fori_loop(0, width // LANES, lane_body, 0)


def _conv_bwd_taps(win, wrep, x_cur, dx_out, dw_acc, *, taps, n_rows, width, shf=None):
    rb = min(32 if taps > 8 else 64, n_rows)

    def lane_body(cb, carry):
        lanes = pl.ds(pl.multiple_of(cb * LANES, LANES), LANES)
        if shf is not None:
            _shift_copies(win, shf, lanes)
        sums = [None] * taps
        for r0 in range(0, n_rows, rb):
            xv = x_cur[pl.ds(r0, rb), lanes].astype(F32)
            acc = None
            for k in range(taps):
                shifted = _rows_at(win, shf, r0 + taps - 1 - k, rb, lanes)
                term = jnp.tile(wrep[pl.ds(k * SUB, SUB), lanes], (rb // SUB, 1)) * shifted
                acc = term if acc is None else acc + term
                part = _rows8(xv * shifted)
                sums[k] = part if sums[k] is None else sums[k] + part
            dx_out[pl.ds(r0, rb), lanes] = acc.astype(dx_out.dtype)
        for k in range(taps):
            dw_acc[pl.ds(k * SUB, SUB), lanes] += sums[k]
        return carry

    lax.fori_loop(0, width // LANES, lane_body, 0)


def _fold8(acc_ref, taps):
    return jnp.concatenate(
        [jnp.sum(acc_ref[pl.ds(k * SUB, SUB), :], axis=0, keepdims=True) for k in range(taps)], axis=0)


def _seq_tile(s_len):
    return min(512, s_len)


def _mm_tile(s_len):
    return min(512, s_len)


def _ff_chunk(ff):
    best = LANES
    for c in range(LANES, 1408 + 1, LANES):
        if ff % c == 0:
            best = c
    return best


def _col_tile(n):
    for c in (512, 1408, 256, LANES):
        if n % c == 0:
            return c
    return n


def _rms_matmul(x, g, wt, b, *, name, dep=None):
    s_len, d = x.shape
    n = wt.shape[0]
    tm = _mm_tile(s_len)
    cn = _col_tile(n)
    has_bias = b is not None

    def body(*refs):
        x_ref, g_ref, w_ref = refs[0:3]
        b_ref = refs[3] if has_bias else None
        o_ref, h_ref = refs[-2:]

        def blk(rows):
            xv = x_ref[rows, :]
            r = lax.rsqrt(jnp.mean(xv * xv, axis=-1, keepdims=True) + EPS)
            h_ref[rows, :] = ((xv * r) * g_ref[...]).astype(BF16)

        rb = min(128, tm)
        for r0 in range(0, tm, rb):
            blk(pl.ds(r0, rb))
        for j in range(n // cn):
            acc = _dot_nt(h_ref[...], w_ref[j * cn:(j + 1) * cn, :])
            if has_bias:
                acc = acc + b_ref[:, j * cn:(j + 1) * cn]
            o_ref[:, j * cn:(j + 1) * cn] = acc.astype(BF16)

    in_specs = [pl.BlockSpec((tm, d), lambda i: (i, 0)), _resident((1, d), lambda i: (0, 0)),
                _resident((n, d), lambda i: (0, 0))]
    args = [x, g, wt]
    if has_bias:
        in_specs.append(_resident((1, n), lambda i: (0, 0)))
        args.append(b)
    in_specs.append(ANY)
    args.append(x if dep is None else dep)
    return pl.pallas_call(
        body, grid=(s_len // tm,), in_specs=in_specs,
        out_specs=[pl.BlockSpec((tm, n), lambda i: (i, 0)), pl.BlockSpec((tm, d), lambda i: (i, 0))],
        out_shape=[jax.ShapeDtypeStruct((s_len, n), BF16), jax.ShapeDtypeStruct((s_len, d), BF16)],
        compiler_params=_params("parallel"), name=name,
    )(*args)


def _mix_windows(u_ref, uh_ref, gw, pw, first, t):
    c = D_CONF
    uh = uh_ref[...].astype(F32)
    gw[0:HALO, :] = jnp.where(first, 0.0, uh[:, 0:c] * _sigmoid(uh[:, c:2 * c]))
    pw[0:HALO3, :] = jnp.where(first, 0.0, uh[HALO - HALO3:HALO, 3 * c:4 * c] * uh[HALO - HALO3:HALO, 4 * c:5 * c])

    def blk(rows):
        dst = pl.ds(pl.multiple_of(rows.start + HALO, SUB), rows.size)
        gw[dst, :] = u_ref[rows, 0:c].astype(F32) * _sigmoid(u_ref[rows, c:2 * c].astype(F32))
        dst3 = pl.ds(pl.multiple_of(rows.start + HALO3, SUB), rows.size)
        pw[dst3, :] = u_ref[rows, 3 * c:4 * c].astype(F32) * u_ref[rows, 4 * c:5 * c].astype(F32)
    _row_loop(t, 64, blk)


def _mix_fwd(u, x0, wa, ba, lg, lb, wb, w_out, *, name):
    s_len, d_in = u.shape
    d = x0.shape[1]
    c = D_CONF
    t = _seq_tile(s_len)
    per = t // HALO

    def body(u_ref, uh_ref, x0_ref, wa_ref, ba_ref, lg_ref, lb_ref, wb_ref, wo_ref, y_ref, x1_ref, ca, cb,
             gw, pw, wrep_a, wrep_b, shf):
        first = pl.program_id(0) == 0
        _mix_windows(u_ref, uh_ref, gw, pw, first, t)
        _replicate_taps(wa_ref, wrep_a, CONF_K)
        _replicate_taps(wb_ref, wrep_b, SHORT_K)
        _conv_taps(gw, wrep_a, ca, taps=CONF_K, n_rows=t, base=HALO, width=c, bias_ref=ba_ref, shf=shf)
        _conv_taps(pw, wrep_b, cb, taps=SHORT_K, n_rows=t, base=HALO3, width=c)

        def blk(rows):
            cv = ca[rows, :]
            mu = jnp.mean(cv, axis=-1, keepdims=True)
            xc = cv - mu
            var = jnp.mean(xc * xc, axis=-1, keepdims=True)
            ln = (xc * lax.rsqrt(var + EPS)) * lg_ref[...] + lb_ref[...]
            y_ref[rows, 0:c] = (ln * _sigmoid(ln)).astype(BF16)
            y_ref[rows, c:2 * c] = (u_ref[rows, 2 * c:3 * c].astype(F32) * cb[rows, :]).astype(BF16)
        half = t // 2
        rb = min(64, half)
        for lo in range(0, t, half):
            for r0 in range(lo, lo + half, rb):
                blk(pl.ds(r0, rb))
            x1_ref[lo:lo + half, :] = x0_ref[lo:lo + half, :] + _dot(y_ref[lo:lo + half, :], wo_ref[...])

    small = lambda r: _resident((r, c), lambda i: (0, 0))
    return pl.pallas_call(
        body, grid=(s_len // t,),
        in_specs=[pl.BlockSpec((t, d_in), lambda i: (i, 0)),
                  pl.BlockSpec((HALO, d_in), lambda i: (jnp.maximum(i * per - 1, 0), 0)),
                  pl.BlockSpec((t, d), lambda i: (i, 0)),
                  small(CONF_K), small(1), small(1), small(1), small(SHORT_K),
                  _resident((2 * c, d), lambda i: (0, 0))],
        out_specs=[pl.BlockSpec((t, 2 * c), lambda i: (i, 0)), pl.BlockSpec((t, d), lambda i: (i, 0)),
                   pl.BlockSpec((t, c), lambda i: (i, 0)), pl.BlockSpec((t, c), lambda i: (i, 0))],
        out_shape=[jax.ShapeDtypeStruct((s_len, 2 * c), BF16), jax.ShapeDtypeStruct((s_len, d), F32),
                   jax.ShapeDtypeStruct((s_len, c), F32), jax.ShapeDtypeStruct((s_len, c), F32)],
        scratch_shapes=[pltpu.VMEM((HALO + t, c), F32), pltpu.VMEM((HALO3 + t, c), F32),
                        pltpu.VMEM((CONF_K * SUB, c), F32), pltpu.VMEM((SHORT_K * SUB, c), F32),
                        pltpu.VMEM((SUB - 1, HALO + t, c), F32)],
        compiler_params=_params("arbitrary"), name=name,
    )(u, u, x0, wa, ba, lg, lb, wb, w_out)


def _ffn_fwd(uf, x1, wf, w_down, *, name):
    s_len, ff2 = uf.shape
    ff = ff2 // 2
    d = x1.shape[1]
    t = _seq_tile(s_len)
    fc = _ff_chunk(ff)
    nc = ff // fc
    per = t // HALO3_BLK
    half = t // 2
    rb = min(64, half)

    def body(ug_ref, ugh_ref, uv_ref, uvh_ref, x1_ref, wfg_ref, wfv_ref, wd_ref,
             act_ref, x2_ref, cg_ref, cv_ref, gwin, vwin, wrep_g, wrep_v):
        first = pl.program_id(0) == 0
        first_chunk = pl.program_id(1) == 0
        lo8 = HALO3_BLK - HALO3
        gwin[0:HALO3, :] = jnp.where(first, 0.0, ugh_ref[...].astype(F32)[lo8:HALO3_BLK])
        vwin[0:HALO3, :] = jnp.where(first, 0.0, uvh_ref[...].astype(F32)[lo8:HALO3_BLK])
        _replicate_taps(wfg_ref, wrep_g, SHORT_K)
        _replicate_taps(wfv_ref, wrep_v, SHORT_K)
        chunk_rows = pl.ds(pl.multiple_of(pl.program_id(1) * fc, fc), fc)

        def conv(win, wrep, r0, lanes):
            acc = None
            for k in range(SHORT_K):
                wk = jnp.tile(wrep[k * SUB:(k + 1) * SUB, lanes], (rb // SUB, 1))
                off = HALO3 + r0 + k - (SHORT_K - 1)
                term = wk * win[off:off + rb, lanes]
                acc = term if acc is None else acc + term
            return acc

        for lo in range(0, t, half):
            for r0 in range(lo, lo + half, rb):
                gwin[HALO3 + r0:HALO3 + r0 + rb, :] = ug_ref[r0:r0 + rb, :].astype(F32)
                vwin[HALO3 + r0:HALO3 + r0 + rb, :] = uv_ref[r0:r0 + rb, :].astype(F32)
            for cb in range(fc // LANES):
                lanes = slice(cb * LANES, (cb + 1) * LANES)
                for r0 in range(lo, lo + half, rb):
                    gv = conv(gwin, wrep_g, r0, lanes).astype(BF16)
                    vv = conv(vwin, wrep_v, r0, lanes).astype(BF16)
                    cg_ref[r0:r0 + rb, lanes] = gv
                    cv_ref[r0:r0 + rb, lanes] = vv
                    act_ref[r0:r0 + rb, lanes] = (gv * _sigmoid(gv)) * vv
            base = jnp.where(first_chunk, x1_ref[lo:lo + half, :], x2_ref[lo:lo + half, :])
            x2_ref[lo:lo + half, :] = base + _dot(act_ref[lo:lo + half, :], wd_ref[chunk_rows, :])

    halo_map = lambda off: (lambda i, j: (jnp.maximum(i * per - 1, 0), j + off))
    return pl.pallas_call(
        body, grid=(s_len // t, nc),
        in_specs=[pl.BlockSpec((t, fc), lambda i, j: (i, j)), pl.BlockSpec((HALO3_BLK, fc), halo_map(0)),
                  pl.BlockSpec((t, fc), lambda i, j: (i, j + nc)), pl.BlockSpec((HALO3_BLK, fc), halo_map(nc)),
                  pl.BlockSpec((t, d), lambda i, j: (i, 0)),
                  pl.BlockSpec((SHORT_K, fc), lambda i, j: (0, j)),
                  pl.BlockSpec((SHORT_K, fc), lambda i, j: (0, j + nc)),
                  _resident((ff, d), lambda i, j: (0, 0))],
        out_specs=[pl.BlockSpec((t, fc), lambda i, j: (i, j)), pl.BlockSpec((t, d), lambda i, j: (i, 0)),
                   pl.BlockSpec((t, fc), lambda i, j: (i, j)), pl.BlockSpec((t, fc), lambda i, j: (i, j))],
        out_shape=[jax.ShapeDtypeStruct((s_len, ff), BF16), jax.ShapeDtypeStruct((s_len, d), F32),
                   jax.ShapeDtypeStruct((s_len, ff), BF16), jax.ShapeDtypeStruct((s_len, ff), BF16)],
        scratch_shapes=[pltpu.VMEM((HALO3 + t, fc), F32), pltpu.VMEM((HALO3 + t, fc), F32),
                        pltpu.VMEM((SHORT_K * SUB, fc), F32), pltpu.VMEM((SHORT_K * SUB, fc), F32)],
        compiler_params=_params("parallel", "arbitrary"), name=name,
    )(uf, uf, uf, uf, x1, wf, wf, w_down)


def _loss_bwd(x, g, target, *, name):
    s_len, d = x.shape
    t = _seq_tile(s_len)

    def body(x_ref, g_ref, t_ref, l_ref, dx_ref, dxb_ref, dg_ref):
        @pl.when(pl.program_id(0) == 0)
        def _():
            l_ref[...] = jnp.zeros_like(l_ref)
            dg_ref[...] = jnp.zeros_like(dg_ref)

        def blk(rows):
            xv = x_ref[rows, :]
            r = lax.rsqrt(jnp.mean(xv * xv, axis=-1, keepdims=True) + EPS)
            xn = xv * r
            e = xn * g_ref[...] - t_ref[rows, :]
            l_ref[...] += _rows8(e * e)
            dy = e * (1.0 / d)
            dg_ref[...] += _rows8(dy * xn)
            dn = dy * g_ref[...]
            dx = r * (dn - xn * jnp.mean(dn * xn, axis=-1, keepdims=True))
            dx_ref[rows, :] = dx
            dxb_ref[rows, :] = dx.astype(BF16)
        _row_loop(t, 64, blk)

    row = pl.BlockSpec((t, d), lambda i: (i, 0))
    part = pl.BlockSpec((SUB, d), lambda i: (0, 0))
    return pl.pallas_call(
        body, grid=(s_len // t,),
        in_specs=[row, _resident((1, d), lambda i: (0, 0)), row],
        out_specs=[part, row, row, part],
        out_shape=[jax.ShapeDtypeStruct((SUB, d), F32), jax.ShapeDtypeStruct((s_len, d), F32),
                   jax.ShapeDtypeStruct((s_len, d), BF16), jax.ShapeDtypeStruct((SUB, d), F32)],
        compiler_params=_params("arbitrary"), name=name,
    )(x, g, target)


def _ffn_bwd(dx2, uf, cg, cv, wf, w_down, *, name, dep=None):
    s_len, ff2 = uf.shape
    ff = ff2 // 2
    d = dx2.shape[1]
    t = _seq_tile(s_len)
    n_t = s_len // t
    fc = _ff_chunk(ff)
    nc = ff // fc

    def body(dx_ref, ug_ref, uv_ref, cg_ref, cv_ref, wfg_ref, wfv_ref, wd_ref, dep_ref,
             duf_ref, dwg_ref, dwv_ref, dact, dgw, dvw, awg, awv, wrep_g, wrep_v):
        i = pl.program_id(1)

        @pl.when(i == 0)
        def _():
            dgw[t:t + HALO3, :] = jnp.zeros((HALO3, fc), F32)
            dvw[t:t + HALO3, :] = jnp.zeros((HALO3, fc), F32)
            awg[...] = jnp.zeros_like(awg)
            awv[...] = jnp.zeros_like(awv)

        _replicate_taps(wfg_ref, wrep_g, SHORT_K)
        _replicate_taps(wfv_ref, wrep_v, SHORT_K)

        def blk(rows):
            gv = cg_ref[rows, :]
            sg = _sigmoid(gv)
            da = dact[rows, :].astype(BF16)
            dgw[rows, :] = ((da * cv_ref[rows, :]) * (sg * (1.0 + gv * (1.0 - sg)))).astype(F32)
            dvw[rows, :] = (da * (gv * sg)).astype(F32)

        dact[...] = _dot_nt(dx_ref[...], wd_ref[...])
        _row_loop(t, 64, blk)

        _conv_bwd_taps(dgw, wrep_g, ug_ref, duf_ref.at[0], awg, taps=SHORT_K, n_rows=t, width=fc)
        _conv_bwd_taps(dvw, wrep_v, uv_ref, duf_ref.at[1], awv, taps=SHORT_K, n_rows=t, width=fc)
        dgw[t:t + HALO3, :] = dgw[0:HALO3, :]
        dvw[t:t + HALO3, :] = dvw[0:HALO3, :]

        @pl.when(i == n_t - 1)
        def _():
            dwg_ref[...] = _fold8(awg, SHORT_K)
            dwv_ref[...] = _fold8(awv, SHORT_K)

    rev = lambda i: n_t - 1 - i
    gate = pl.BlockSpec((t, fc), lambda j, i: (rev(i), j))
    value = pl.BlockSpec((t, fc), lambda j, i: (rev(i), j + nc))
    return pl.pallas_call(
        body, grid=(nc, n_t),
        in_specs=[pl.BlockSpec((t, d), lambda j, i: (rev(i), 0)), gate, value, gate, gate,
                  pl.BlockSpec((SHORT_K, fc), lambda j, i: (0, j)),
                  pl.BlockSpec((SHORT_K, fc), lambda j, i: (0, j + nc)),
                  pl.BlockSpec((fc, d), lambda j, i: (j, 0)), ANY],
        out_specs=[pl.BlockSpec((2, t, fc), lambda j, i: (0, rev(i), j)),
                   pl.BlockSpec((SHORT_K, fc), lambda j, i: (0, j)), pl.BlockSpec((SHORT_K, fc), lambda j, i: (0, j))],
        out_shape=[jax.ShapeDtypeStruct((2, s_len, ff), BF16),
                   jax.ShapeDtypeStruct((SHORT_K, ff), F32), jax.ShapeDtypeStruct((SHORT_K, ff), F32)],
        scratch_shapes=[pltpu.VMEM((t, fc), F32),
                        pltpu.VMEM((t + HALO3, fc), F32), pltpu.VMEM((t + HALO3, fc), F32),
                        pltpu.VMEM((SHORT_K * SUB, fc), F32), pltpu.VMEM((SHORT_K * SUB, fc), F32),
                        pltpu.VMEM((SHORT_K * SUB, fc), F32), pltpu.VMEM((SHORT_K * SUB, fc), F32)],
        compiler_params=_params("arbitrary", "arbitrary"), name=name,
    )(dx2, uf, uf, cg, cv, wf, wf, w_down, uf if dep is None else dep)


def _mix_bwd(dx1, u, ca, cb, wa, lg, lb, wb, w_out, *, name, dep=None):
    s_len, d_in = u.shape
    d = dx1.shape[1]
    c = D_CONF
    t = _seq_tile(s_len)
    n_t = s_len // t

    def body(dx_ref, u_ref, ca_ref, cb_ref, wa_ref, lg_ref, lb_ref, wb_ref, wo_ref, dep_ref,
             du_ref, dwa_ref, dwb_ref, dba_ref, dlg_ref, dlb_ref, dbin_ref,
             glu, prod, dyc, dcaw, dcbw, dglu, dp, awa, awb, wrep_a, wrep_b, shf):
        i = pl.program_id(0)
        _replicate_taps(wa_ref, wrep_a, CONF_K)
        _replicate_taps(wb_ref, wrep_b, SHORT_K)

        @pl.when(i == 0)
        def _():
            dcaw[t:t + HALO, :] = jnp.zeros((HALO, c), F32)
            dcbw[t:t + HALO3, :] = jnp.zeros((HALO3, c), F32)
            awa[...] = jnp.zeros_like(awa)
            awb[...] = jnp.zeros_like(awb)
            dba_ref[...] = jnp.zeros_like(dba_ref)
            dlg_ref[...] = jnp.zeros_like(dlg_ref)
            dlb_ref[...] = jnp.zeros_like(dlb_ref)
            dbin_ref[...] = jnp.zeros_like(dbin_ref)

        def blk1(rows):
            cv = ca_ref[rows, :]
            mu = jnp.mean(cv, axis=-1, keepdims=True)
            xc = cv - mu
            rstd = lax.rsqrt(jnp.mean(xc * xc, axis=-1, keepdims=True) + EPS)
            nrm = xc * rstd
            ln = nrm * lg_ref[...] + lb_ref[...]
            sg = _sigmoid(ln)
            dln = dyc[rows, 0:c] * (sg * (1.0 + ln * (1.0 - sg)))
            dlg_ref[...] += _rows8(dln * nrm)
            dlb_ref[...] += _rows8(dln)
            dn = dln * lg_ref[...]
            dca = rstd * (dn - jnp.mean(dn, axis=-1, keepdims=True)
                          - nrm * jnp.mean(dn * nrm, axis=-1, keepdims=True))
            dcaw[rows, :] = dca
            dba_ref[...] += _rows8(dca)
            ds = dyc[rows, c:2 * c]
            dgb = ds * cb_ref[rows, :]
            dcbw[rows, :] = ds * u_ref[rows, 2 * c:3 * c].astype(F32)
            du_ref[rows, 2 * c:3 * c] = dgb.astype(BF16)
            dbin_ref[:, 2 * c:3 * c] += _rows8(dgb)
            glu[rows, :] = u_ref[rows, 0:c].astype(F32) * _sigmoid(u_ref[rows, c:2 * c].astype(F32))
            prod[rows, :] = u_ref[rows, 3 * c:4 * c].astype(F32) * u_ref[rows, 4 * c:5 * c].astype(F32)
        half = t // 2
        rb = min(64, half)
        for lo in range(0, t, half):
            dyc[lo:lo + half, :] = _dot_nt(dx_ref[lo:lo + half, :], wo_ref[...])
            for r0 in range(lo, lo + half, rb):
                blk1(pl.ds(r0, rb))

        _conv_bwd_taps(dcaw, wrep_a, glu, dglu, awa, taps=CONF_K, n_rows=t, width=c, shf=shf)
        _conv_bwd_taps(dcbw, wrep_b, prod, dp, awb, taps=SHORT_K, n_rows=t, width=c)
        dcaw[t:t + HALO, :] = dcaw[0:HALO, :]
        dcbw[t:t + HALO3, :] = dcbw[0:HALO3, :]

        def blk2(rows):
            av = u_ref[rows, 0:c].astype(F32)
            sg = _sigmoid(u_ref[rows, c:2 * c].astype(F32))
            dg = dglu[rows, :]
            d_av = dg * sg
            d_ag = (dg * av) * (sg * (1.0 - sg))
            dpv = dp[rows, :]
            d_gc = dpv * u_ref[rows, 4 * c:5 * c].astype(F32)
            d_vs = dpv * u_ref[rows, 3 * c:4 * c].astype(F32)
            du_ref[rows, 0:c] = d_av.astype(BF16)
            du_ref[rows, c:2 * c] = d_ag.astype(BF16)
            du_ref[rows, 3 * c:4 * c] = d_gc.astype(BF16)
            du_ref[rows, 4 * c:5 * c] = d_vs.astype(BF16)
            dbin_ref[:, 0:c] += _rows8(d_av)
            dbin_ref[:, c:2 * c] += _rows8(d_ag)
            dbin_ref[:, 3 * c:4 * c] += _rows8(d_gc)
            dbin_ref[:, 4 * c:5 * c] += _rows8(d_vs)
        _row_loop(t, 64, blk2)

        @pl.when(i == n_t - 1)
        def _():
            dwa_ref[...] = _fold8(awa, CONF_K)
            dwb_ref[...] = _fold8(awb, SHORT_K)

    rev = lambda i: n_t - 1 - i
    small_in = lambda r: _resident((r, c), lambda i: (0, 0))
    small = lambda r: pl.BlockSpec((r, c), lambda i: (0, 0))
    return pl.pallas_call(
        body, grid=(n_t,),
        in_specs=[pl.BlockSpec((t, d), lambda i: (rev(i), 0)),
                  pl.BlockSpec((t, d_in), lambda i: (rev(i), 0)),
                  pl.BlockSpec((t, c), lambda i: (rev(i), 0)), pl.BlockSpec((t, c), lambda i: (rev(i), 0)),
                  small_in(CONF_K), small_in(1), small_in(1), small_in(SHORT_K),
                  _resident((2 * c, d), lambda i: (0, 0)), ANY],
        out_specs=[pl.BlockSpec((t, d_in), lambda i: (rev(i), 0)),
                   small(CONF_K), small(SHORT_K), small(SUB), small(SUB), small(SUB),
                   pl.BlockSpec((SUB, d_in), lambda i: (0, 0))],
        out_shape=[jax.ShapeDtypeStruct((s_len, d_in), BF16),
                   jax.ShapeDtypeStruct((CONF_K, c), F32), jax.ShapeDtypeStruct((SHORT_K, c), F32),
                   jax.ShapeDtypeStruct((SUB, c), F32), jax.ShapeDtypeStruct((SUB, c), F32),
                   jax.ShapeDtypeStruct((SUB, c), F32), jax.ShapeDtypeStruct((SUB, d_in), F32)],
        scratch_shapes=[pltpu.VMEM((t, c), F32), pltpu.VMEM((t, c), F32), pltpu.VMEM((t, 2 * c), F32),
                        pltpu.VMEM((t + HALO, c), F32), pltpu.VMEM((t + HALO3, c), F32),
                        pltpu.VMEM((t, c), F32), pltpu.VMEM((t, c), F32),
                        pltpu.VMEM((CONF_K * SUB, c), F32), pltpu.VMEM((SHORT_K * SUB, c), F32),
                        pltpu.VMEM((CONF_K * SUB, c), F32), pltpu.VMEM((SHORT_K * SUB, c), F32),
                        pltpu.VMEM((SUB - 1, t + HALO, c), F32)],
        compiler_params=_params("arbitrary"), name=name,
    )(dx1, u, ca, cb, wa, lg, lb, wb, w_out, u if dep is None else dep)


def _matmul_tn(a, b, *, name):
    n_p, s_len, k = a.shape
    n = b.shape[1]
    tk = _col_tile(k)
    per = k // tk
    if per > 2:
        def body(a_ref, b_ref, o_ref):
            o_ref[...] = _dot_tn(a_ref[...], b_ref[...]).astype(BF16)

        return pl.pallas_call(
            body, grid=(n_p, per),
            in_specs=[pl.BlockSpec((None, s_len, tk), lambda p, j: (p, 0, j)),
                      _resident((s_len, n), lambda p, j: (0, 0))],
            out_specs=pl.BlockSpec((tk, n), lambda p, j: (p * per + j, 0)),
            out_shape=jax.ShapeDtypeStruct((n_p * k, n), BF16),
            compiler_params=_params("parallel", "parallel"), name=name,
        )(a, b)

    half = s_len // 2

    def body_halves(a_ref, b_ref, o_ref, acc):
        @pl.when(pl.program_id(2) == 0)
        def _():
            acc[...] = _dot_tn(a_ref[...], b_ref[...])

        @pl.when(pl.program_id(2) == 1)
        def _():
            o_ref[...] = (acc[...] + _dot_tn(a_ref[...], b_ref[...])).astype(BF16)

    return pl.pallas_call(
        body_halves, grid=(n_p, per, 2),
        in_specs=[pl.BlockSpec((None, half, tk), lambda p, j, q: (p, q, j)),
                  pl.BlockSpec((half, n), lambda p, j, q: (q, 0))],
        out_specs=pl.BlockSpec((tk, n), lambda p, j, q: (p * per + j, 0)),
        out_shape=jax.ShapeDtypeStruct((n_p * k, n), BF16),
        scratch_shapes=[pltpu.VMEM((tk, n), F32)],
        compiler_params=_params("parallel", "parallel", "arbitrary"), name=name,
    )(a, b)


def _matmul_rmsbwd(dzs, wt, x, g, dx_in, *, name, dep=None):
    s_len, d = x.shape
    n_z, _, nj = dzs.shape
    t = _mm_tile(s_len)

    def body(*refs):
        dz_refs = refs[0:n_z]
        w_refs = refs[n_z:2 * n_z]
        x_ref, g_ref, dxi_ref, _, dx_ref, dxb_ref, dg_ref, dh = refs[2 * n_z:]

        @pl.when(pl.program_id(0) == 0)
        def _():
            dg_ref[...] = jnp.zeros_like(dg_ref)

        def blk(rows):
            xv = x_ref[rows, :]
            r = lax.rsqrt(jnp.mean(xv * xv, axis=-1, keepdims=True) + EPS)
            xn = xv * r
            dhv = dh[rows, :]
            dg_ref[...] += _rows8(dhv * xn)
            dn = dhv * g_ref[...]
            dx = dxi_ref[rows, :] + r * (dn - xn * jnp.mean(dn * xn, axis=-1, keepdims=True))
            dx_ref[rows, :] = dx
            dxb_ref[rows, :] = dx.astype(BF16)

        half = t // 2
        rb = min(128, half)
        for lo in range(0, t, half):
            acc = _dot(dz_refs[0][lo:lo + half, :], w_refs[0][...])
            for q in range(1, n_z):
                acc = acc + _dot(dz_refs[q][lo:lo + half, :], w_refs[q][...])
            dh[lo:lo + half, :] = acc
            for r0 in range(lo, lo + half, rb):
                blk(pl.ds(r0, rb))

    row = pl.BlockSpec((t, d), lambda i: (i, 0))
    in_specs = [pl.BlockSpec((None, t, nj), functools.partial(lambda q, i: (q, i, 0), q)) for q in range(n_z)]
    in_specs += [_resident((nj, d), functools.partial(lambda q, i: (q, 0), q)) for q in range(n_z)]
    in_specs += [row, _resident((1, d), lambda i: (0, 0)), row, ANY]
    return pl.pallas_call(
        body, grid=(s_len // t,), in_specs=in_specs,
        out_specs=[row, row, pl.BlockSpec((SUB, d), lambda i: (0, 0))],
        out_shape=[jax.ShapeDtypeStruct((s_len, d), F32), jax.ShapeDtypeStruct((s_len, d), BF16),
                   jax.ShapeDtypeStruct((SUB, d), F32)],
        scratch_shapes=[pltpu.VMEM((t, d), F32)],
        compiler_params=_params("arbitrary"), name=name,
    )(*([dzs] * n_z), *([wt] * n_z), x, g, dx_in, x if dep is None else dep)


def _row(v):
    return v.reshape(1, -1)


def _layer_fwd(x0, p, tag, dep=None, before_up=None):
    u, h1 = _rms_matmul(x0, _row(p["mix_norm_g"]), p["w_in_t"], _row(p["b_in"]), name=f"in_proj_{tag}", dep=dep)
    ycat, x1, ca, cb = _mix_fwd(u, x0, p["conv_a_w"], _row(p["conv_a_b"]), _row(p["ln_a_g"]), _row(p["ln_a_b"]),
                            p["conv_b_w"], p["w_out"], name=f"mix_fwd_{tag}")
    if before_up is not None:
        before_up(x1)
    uf, h2 = _rms_matmul(x1, _row(p["ffn_norm_g"]), p["w_up_t"], None, name=f"up_proj_{tag}")
    act, x2, cg, cv = _ffn_fwd(uf, x1, p["conv_f_w"], p["w_down"], name=f"ffn_fwd_{tag}")
    return x2, dict(x0=x0, h1=h1, u=u, ca=ca, cb=cb, ycat=ycat, x1=x1, h2=h2, uf=uf, cg=cg, cv=cv, act=act)


def _layer_bwd(dx2, dx2_b, p, saved, tag, ffn_grads, ffn_sent, mix_grads, mix_sent, dep=None):
    d_uf, dwf_g, dwf_v = _ffn_bwd(dx2_b, saved["uf"], saved["cg"], saved["cv"], p["conv_f_w"], p["w_down"],
                                  name=f"ffn_bwd_{tag}", dep=dep)
    g_down = _matmul_tn(saved["act"][None], dx2_b, name=f"dw_down_{tag}")
    g_up = _matmul_tn(d_uf, saved["h2"], name=f"dw_up_{tag}")
    dep_ffn = ffn_grads(dict(w_up=g_up, w_down=g_down), dx2_b)
    dx1, dx1_b, dg2 = _matmul_rmsbwd(d_uf, p["w_up_t"], saved["x1"], _row(p["ffn_norm_g"]), dx2,
                                     name=f"dh_ffn_{tag}", dep=dep_ffn)
    du, dwa, dwb, dba, dlg, dlb, dbin = _mix_bwd(
        dx1_b, saved["u"], saved["ca"], saved["cb"], p["conv_a_w"], _row(p["ln_a_g"]), _row(p["ln_a_b"]),
        p["conv_b_w"], p["w_out"], name=f"mix_bwd_{tag}", dep=ffn_sent(dx1_b))
    g_out = _matmul_tn(saved["ycat"][None], dx1_b, name=f"dw_out_{tag}")
    g_in = _matmul_tn(du[None], saved["h1"], name=f"dw_in_{tag}")
    conv = dict(conv_a_w=dwa, conv_b_w=dwb, conv_f_w=jnp.concatenate([dwf_g, dwf_v], axis=1))
    dep_mix = mix_grads(dict(w_in=g_in, w_out=g_out), conv, dx1_b)
    dx0, dx0_b, dg1 = _matmul_rmsbwd(du[None], p["w_in_t"], saved["x0"], _row(p["mix_norm_g"]), dx1,
                                     name=f"dh_mix_{tag}", dep=dep_mix)
    rep = dict(mix_norm_g=dg1, b_in=dbin, conv_a_b=dba, ln_a_g=dlg, ln_a_b=dlb, ffn_norm_g=dg2)
    return dx0, dx0_b, rep, mix_sent(dx0_b)


def _place():
    return lax.axis_index("x"), lax.axis_index("y"), lax.axis_index("c")


def _all_gather(arrs, *, name):
    n_a = len(arrs)

    def body(*refs):
        ins = refs[0:n_a]
        outs = refs[n_a:2 * n_a]
        send_sems, recv_sems, local_sems = refs[2 * n_a:]
        x, y, c = _place()
        sibling = (x, y, 1 - c)
        chips = [(1 - x, y), (x, 1 - y), (1 - x, 1 - y)]

        def slot(a, px, py, pc):
            return outs[a].at[4 * px + 2 * py + pc]

        def copy(a, k, block, to, src=None):
            return pltpu.make_async_remote_copy(
                src_ref=slot(a, *block) if src is None else src, dst_ref=slot(a, *block),
                send_sem=send_sems.at[a, k], recv_sem=recv_sems.at[a, k],
                device_id=to, device_id_type=MESH)

        me = (x, y, c)
        mine = [pltpu.make_async_copy(ins[a], slot(a, *me), local_sems.at[a]) for a in range(n_a)]
        for cp in mine:
            cp.start()
        started = []
        for a in range(n_a):
            first = [copy(a, 0, me, sibling, src=ins[a])]
            first += [copy(a, 1 + j, me, (*chip, c), src=ins[a]) for j, chip in enumerate(chips)]
            for cp in first:
                cp.start()
            started += first
        for a in range(n_a):
            for j, chip in enumerate(chips):
                copy(a, 1 + j, (*chip, c), me).wait_recv()
                passed = copy(a, 4 + j, (*chip, c), sibling)
                passed.start()
                started.append(passed)
        for a in range(n_a):
            copy(a, 0, sibling, me).wait_recv()
            for j, chip in enumerate(chips):
                copy(a, 4 + j, (*chip, 1 - c), me).wait_recv()
        for cp in started:
            cp.wait_send()
        for cp in mine:
            cp.wait()

    return pl.pallas_call(
        body, in_specs=[ANY] * n_a, out_specs=[ANY] * n_a,
        out_shape=[jax.ShapeDtypeStruct((N_DEV, *a.shape), a.dtype) for a in arrs],
        scratch_shapes=[pltpu.SemaphoreType.DMA((n_a, 7)), pltpu.SemaphoreType.DMA((n_a, 7)),
                        pltpu.SemaphoreType.DMA((n_a,))],
        name=name,
    )(*arrs)


def _row_tile(r, cap):
    for tr in range(min(cap, r) // 16 * 16, 0, -16):
        if r % tr == 0:
            return tr
    return r


def _pair_sum(mines, theirs, where, *, name):
    n_a = len(mines)
    n_chip = mines[0].shape[0]

    def body(where_ref, *refs):
        a_refs = refs[0:n_a]
        b_refs = refs[n_a:2 * n_a]
        p_refs = refs[2 * n_a:3 * n_a]
        l_refs = refs[3 * n_a:4 * n_a]
        q = pl.program_id(0)
        for a in range(n_a):
            p_refs[a][...] = (a_refs[a][...].astype(F32) + b_refs[a][...].astype(F32)).astype(p_refs[a].dtype)

        @pl.when(q == where_ref[1])
        def _():
            for a in range(n_a):
                l_refs[a][...] = p_refs[a][...]

    in_specs, out_p, out_l, shapes = [], [], [], []
    for m in mines:
        _, _, r, c = m.shape
        in_specs.append(pl.BlockSpec((None, None, r, c), lambda q, where_ref: (q, where_ref[0], 0, 0)))
    for m in mines:
        _, _, r, c = m.shape
        in_specs.append(pl.BlockSpec((None, r, c), lambda q, where_ref: (q, 0, 0)))
        out_p.append(pl.BlockSpec((None, r, c), lambda q, where_ref: (q, 0, 0)))
        out_l.append(pl.BlockSpec((None, r, c), lambda q, where_ref: (where_ref[1], 0, 0)))
        shapes.append(jax.ShapeDtypeStruct((n_chip, r, c), m.dtype))
    res = pl.pallas_call(
        body,
        grid_spec=pltpu.PrefetchScalarGridSpec(num_scalar_prefetch=1, grid=(n_chip,), in_specs=in_specs,
                                               out_specs=out_p + out_l),
        out_shape=shapes + shapes,
        compiler_params=_params("arbitrary"), name=name,
    )(where, *mines, *theirs)
    return list(res[:n_a]), list(res[n_a:])


HBM = pl.BlockSpec(memory_space=pltpu.HBM)
SEM = pl.BlockSpec(memory_space=pltpu.SEMAPHORE)
EFFECT = pltpu.SideEffectType.DATAFLOW_SIDE_EFFECTING


def _in_hbm(a):
    return pltpu.with_memory_space_constraint(a, pltpu.HBM)


def _split_start(srcs, lands, plan, n_copies, after, *, name):
    n_s, n_l = len(srcs), len(lands)

    def body(*refs):
        src_refs = refs[0:n_s]
        land_refs = refs[n_s:n_s + n_l]
        send_sems, recv_sems = refs[n_s + n_l + 1], refs[n_s + n_l + 2]
        token = refs[-1]
        for cp in plan(src_refs, land_refs, send_sems, recv_sems):
            cp.start()
        token[...] = jnp.zeros_like(token)

    thru = [pltpu.HBM(a.shape, a.dtype) for a in list(srcs) + list(lands)]
    res = pl.pallas_call(
        body, name=name,
        out_shape=(pltpu.SemaphoreType.DMA((n_copies,)), pltpu.SemaphoreType.DMA((n_copies,)), *thru,
                   jax.ShapeDtypeStruct((SUB, LANES), F32)),
        in_specs=[HBM] * (n_s + n_l) + [ANY],
        out_specs=(SEM, SEM, *([HBM] * (n_s + n_l)), pl.BlockSpec(memory_space=pltpu.VMEM)),
        input_output_aliases={i: 2 + i for i in range(n_s + n_l)},
        compiler_params=pltpu.CompilerParams(has_side_effects=EFFECT),
    )(*[_in_hbm(a) for a in srcs], *[_in_hbm(a) for a in lands], _in_hbm(after))
    return res[0], res[1], list(res[2:2 + n_s]), list(res[2 + n_s:2 + n_s + n_l]), res[-1]


def _split_wait(send_sems, recv_sems, srcs, lands, after, plan, *, name):
    n_s, n_l = len(srcs), len(lands)

    def body(*refs):
        src_refs = refs[0:n_s]
        land_refs = refs[n_s:n_s + n_l]
        send, recv = refs[n_s + n_l], refs[n_s + n_l + 1]
        for cp in plan(src_refs, land_refs, send, recv):
            cp.wait_send()
            cp.wait_recv()

    res = pl.pallas_call(
        body, name=name,
        out_shape=tuple(pltpu.HBM(a.shape, a.dtype) for a in list(srcs) + list(lands)),
        in_specs=[HBM] * (n_s + n_l) + [SEM, SEM, ANY],
        out_specs=tuple([HBM] * (n_s + n_l)),
        input_output_aliases={i: i for i in range(n_s + n_l)},
        compiler_params=pltpu.CompilerParams(has_side_effects=EFFECT),
    )(*srcs, *lands, send_sems, recv_sems, _in_hbm(after))
    return list(res[:n_s]), list(res[n_s:])


def _remote(src, dst, send_sems, recv_sems, k, to):
    return pltpu.make_async_remote_copy(src_ref=src, dst_ref=dst, send_sem=send_sems.at[k], recv_sem=recv_sems.at[k],
                                        device_id=to, device_id_type=MESH)


def _gather_plan_first(src_refs, land_refs, send_sems, recv_sems):
    x, y, c = _place()
    me = 4 * x + 2 * y + c
    peers = [(x, y, 1 - c), (1 - x, y, c), (x, 1 - y, c), (1 - x, 1 - y, c)]
    return [_remote(land.at[me], land.at[me], send_sems, recv_sems, 4 * a + k, to)
            for a, land in enumerate(land_refs) for k, to in enumerate(peers)]


def _gather_plan_second(src_refs, land_refs, send_sems, recv_sems):
    x, y, c = _place()
    chips = [(1 - x, y), (x, 1 - y), (1 - x, 1 - y)]
    out = []
    for a, land in enumerate(land_refs):
        for j, (px, py) in enumerate(chips):
            slot = land.at[4 * px + 2 * py + c]
            out.append(_remote(slot, slot, send_sems, recv_sems, 3 * a + j, (x, y, 1 - c)))
    return out


def _siblings_plan(src_refs, land_refs, send_sems, recv_sems):
    x, y, c = _place()
    return [_remote(src.at[:, 1 - c], land, send_sems, recv_sems, a, (x, y, 1 - c))
            for a, (src, land) in enumerate(zip(src_refs, land_refs))]


def _chips_plan(src_refs, land_refs, send_sems, recv_sems):
    x, y, c = _place()
    my_chip = 2 * x + y
    chips = [(1 - x, y), (x, 1 - y), (1 - x, 1 - y)]
    return [_remote(src.at[2 * px + py], land.at[my_chip], send_sems, recv_sems, 3 * a + j, (px, py, c))
            for a, (src, land) in enumerate(zip(src_refs, land_refs)) for j, (px, py) in enumerate(chips)]


def _gather_landings(shards, me, *, name):
    blank = _unwritten([jax.ShapeDtypeStruct((N_DEV, *s.shape), s.dtype) for s in shards], name=name)
    return [lax.dynamic_update_index_in_dim(b, s, me, 0) for b, s in zip(blank, shards)]


def _adamw_math(g, w, m, v):
    m = ADAM_B1 * m + (1.0 - ADAM_B1) * g
    v = ADAM_B2 * v + (1.0 - ADAM_B2) * (g * g)
    m_hat = m / (1.0 - ADAM_B1 ** ADAM_STEP)
    v_hat = v / (1.0 - ADAM_B2 ** ADAM_STEP)
    delta = -ADAM_LR * (m_hat / (jnp.sqrt(v_hat) + ADAM_EPS) + ADAM_WD * w)
    return delta, m, v


def _adamw_sharded(parts, w, m, v, *, name, dep=None):
    n_layers, r, c = w.shape
    n_chip = parts[0].shape[0]
    tr = _row_tile(r, 384)
    n_i = r // tr

    def body(*refs):
        p_refs = refs[0:n_layers]
        w_ref, m_ref, v_ref, _, g_out, d_out, m_out, v_out = refs[n_layers:]
        layer = pl.program_id(0)
        for l in range(n_layers):
            @pl.when(layer == l)
            def _(l=l):
                g = p_refs[l][0].astype(F32)
                for q in range(1, n_chip):
                    g = g + p_refs[l][q].astype(F32)
                delta, m_new, v_new = _adamw_math(g, w_ref[...], m_ref[...], v_ref[...])
                g_out[...] = g
                d_out[...] = delta
                m_out[...] = m_new
                v_out[...] = v_new

    def part_map(l):
        return lambda layer, i: (0, jnp.where(layer == l, i, jnp.where(layer < l, 0, n_i - 1)), 0)

    blk = pl.BlockSpec((None, tr, c), lambda layer, i: (layer, i, 0))
    return pl.pallas_call(
        body, grid=(n_layers, n_i),
        in_specs=[pl.BlockSpec((n_chip, tr, c), part_map(l)) for l in range(n_layers)] + [blk, blk, blk, ANY],
        out_specs=[blk] * 4, out_shape=[jax.ShapeDtypeStruct((n_layers, r, c), F32)] * 4,
        compiler_params=_params("arbitrary", "arbitrary"), name=name,
    )(*parts, w, m, v, w if dep is None else dep)


def _fold_partials(cols, *, name):
    widths = [c.shape[1] for c in cols]

    def body(*refs):
        o_ref = refs[-1]
        pos = 0
        for ref, width in zip(refs[:-1], widths):
            o_ref[:, pos:pos + width] = jnp.sum(ref[...], axis=0, keepdims=True)
            pos += width

    return pl.pallas_call(body, out_shape=jax.ShapeDtypeStruct((1, sum(widths)), F32), name=name)(*cols)


def _adamw_replicated(parts, names, w, m, v, n_loss, *, name):
    n_dev = parts.shape[0]
    n_layers = w[names[0]].shape[0]
    every = list(names) + ["final_norm_g"]
    n_p = len(every)

    def body(*refs):
        p_ref = refs[0]
        w_refs = dict(zip(every, refs[1:1 + n_p]))
        m_refs = dict(zip(every, refs[1 + n_p:1 + 2 * n_p]))
        v_refs = dict(zip(every, refs[1 + 2 * n_p:1 + 3 * n_p]))
        l_out = refs[1 + 3 * n_p]
        outs = refs[2 + 3 * n_p:]
        o_refs = {n: outs[4 * q:4 * q + 4] for q, n in enumerate(every)}
        acc = p_ref[0]
        for q in range(1, n_dev):
            acc = acc + p_ref[q]
        tot = jnp.sum(acc, axis=0, keepdims=True)
        pos = 0
        where = [(n, l) for l in range(n_layers) for n in names] + [("final_norm_g", 0)]
        for n, l in where:
            width = w_refs[n].shape[1]
            g = tot[:, pos:pos + width]
            pos += width
            row = pl.ds(l, 1)
            delta, m_new, v_new = _adamw_math(g, w_refs[n][row, :], m_refs[n][row, :], v_refs[n][row, :])
            for o, val in zip(o_refs[n], (g, delta, m_new, v_new)):
                o[row, :] = val
        l_out[...] = (0.5 / n_loss) * jnp.sum(tot[:, pos:pos + n_loss], axis=-1, keepdims=True)

    shapes = [jax.ShapeDtypeStruct((1, 1), F32)]
    for n in every:
        shapes += [jax.ShapeDtypeStruct(w[n].shape, F32)] * 4
    res = pl.pallas_call(
        body, out_shape=shapes,
        compiler_params=pltpu.CompilerParams(vmem_limit_bytes=VMEM_LIMIT), name=name,
    )(parts, *[w[n] for n in every], *[m[n] for n in every], *[v[n] for n in every])
    return res[0], {n: res[1 + 4 * q:5 + 4 * q] for q, n in enumerate(every)}


BIG = ("w_in", "w_out", "w_up", "w_down")
COL_SHARDED = ("w_in", "w_up")
CONV = ("conv_a_w", "conv_b_w", "conv_f_w")
REPLICATED = ("mix_norm_g", "b_in", "conv_a_b", "ln_a_g", "ln_a_b", "ffn_norm_g")
KINDS = ("grad", "delta", "m", "v")
FFN_PART = ("w_up", "w_down")
MIX_PART = ("w_in", "w_out")


def _weights_from_gathered(g):
    n_dev, r, c = g.shape
    return g.reshape(n_dev * r, c)


def _slabs_from_full(grad):
    return grad.reshape(N_DEV, grad.shape[0] // N_DEV, grad.shape[1])


def _unwritten(like, *, name):
    return pl.pallas_call(lambda *refs: None, out_specs=[ANY] * len(like), out_shape=list(like), name=name)()


def kernel(x, mix_norm_g, w_in, b_in, conv_a_w, conv_a_b, ln_a_g, ln_a_b, conv_b_w, w_out, ffn_norm_g, w_up, conv_f_w, w_down, final_norm_g, loss_target, m_mix_norm_g, m_w_in, m_b_in, m_conv_a_w, m_conv_a_b, m_ln_a_g, m_ln_a_b, m_conv_b_w, m_w_out, m_ffn_norm_g, m_w_up, m_conv_f_w, m_w_down, m_final_norm_g, v_mix_norm_g, v_w_in, v_b_in, v_conv_a_w, v_conv_a_b, v_ln_a_g, v_ln_a_b, v_conv_b_w, v_w_out, v_ffn_norm_g, v_w_up, v_conv_f_w, v_w_down, v_final_norm_g):
    w = dict(mix_norm_g=mix_norm_g, w_in=w_in, b_in=b_in, conv_a_w=conv_a_w, conv_a_b=conv_a_b, ln_a_g=ln_a_g,
             ln_a_b=ln_a_b, conv_b_w=conv_b_w, w_out=w_out, ffn_norm_g=ffn_norm_g, w_up=w_up, conv_f_w=conv_f_w,
             w_down=w_down, final_norm_g=final_norm_g)
    m = dict(mix_norm_g=m_mix_norm_g, w_in=m_w_in, b_in=m_b_in, conv_a_w=m_conv_a_w, conv_a_b=m_conv_a_b,
             ln_a_g=m_ln_a_g, ln_a_b=m_ln_a_b, conv_b_w=m_conv_b_w, w_out=m_w_out, ffn_norm_g=m_ffn_norm_g,
             w_up=m_w_up, conv_f_w=m_conv_f_w, w_down=m_w_down, final_norm_g=m_final_norm_g)
    v = dict(mix_norm_g=v_mix_norm_g, w_in=v_w_in, b_in=v_b_in, conv_a_w=v_conv_a_w, conv_a_b=v_conv_a_b,
             ln_a_g=v_ln_a_g, ln_a_b=v_ln_a_b, conv_b_w=v_conv_b_w, w_out=v_w_out, ffn_norm_g=v_ffn_norm_g,
             w_up=v_w_up, conv_f_w=v_conv_f_w, w_down=v_w_down, final_norm_g=v_final_norm_g)
    order = list(w)
    n_layers = w_in.shape[0]
    xs = x[0]
    target = loss_target[0]
    flip = lambda a: jnp.transpose(a, (0, 2, 1))
    wt, mt, vt = ({n: flip(d[n]) if n in COL_SHARDED else d[n] for n in BIG} for d in (w, m, v))
    px, py, pc = _place()
    where = jnp.stack([pc, 2 * px + py]).astype(jnp.int32)
    me = 4 * px + 2 * py + pc

    assert BIG == MIX_PART + FFN_PART
    key = lambda n: n + "_t" if n in COL_SHARDED else n
    shard = lambda n, l: wt[n][l].astype(BF16)

    def gather_start(names, l, after, tag, behind=None, more=()):
        if behind is None:
            shards = [shard(n, l) for n in names]
        else:
            shards = [s.astype(BF16) for s in lax.optimization_barrier(([wt[n][l] for n in names], behind))[0]]
        lands = _gather_landings(shards + list(more), me, name=f"gather_landing_{tag}")
        return _split_start([], lands, _gather_plan_first, 4 * len(lands), after, name=f"gather_first_start_{tag}")

    def gather_mid(first, after, tag):
        return _split_wait(first[0], first[1], first[2], first[3], after, _gather_plan_first,
                           name=f"gather_first_wait_{tag}")[1]

    def forward_start(lands, after, tag):
        return _split_start([], lands, _gather_plan_second, 3 * len(lands), after, name=f"gather_second_start_{tag}")

    def forward_finish(second, after, tag):
        return _split_wait(second[0], second[1], [], second[3], after, _gather_plan_second,
                           name=f"gather_second_wait_{tag}")[1]

    conv_rows = [w[n].reshape(-1, w[n].shape[2]) for n in CONV]
    mix_first = gather_start(MIX_PART, 0, xs, "0_mix", more=conv_rows)
    ffn_first = gather_start(FFN_PART, 0, mix_first[4], "0_ffn", behind=mix_first[4])
    mix_second = forward_start(gather_mid(mix_first, ffn_first[4], "0_mix"), ffn_first[4], "0_mix")
    gathered = forward_finish(mix_second, mix_second[4], "0_mix")
    params = [{n: w[n][l] for n in REPLICATED} for l in range(n_layers)]
    for n, g in zip(CONV, gathered[len(MIX_PART):]):
        taps = w[n].shape[1]
        full = g.reshape(N_DEV, n_layers, taps, -1).transpose(1, 2, 0, 3).reshape(n_layers, taps, -1)
        for l in range(n_layers):
            params[l][n] = full[l]
    for n, g in zip(MIX_PART, gathered):
        params[0][key(n)] = _weights_from_gathered(g)
    pending = {}

    h = xs
    saved = []
    for l in range(n_layers):
        nxt = l + 1 if l + 1 < n_layers else None

        def before_up(x1, l=l, nxt=nxt):
            if l == 0:
                second = forward_start(gather_mid(ffn_first, x1, "0_ffn"), x1, "0_ffn")
                after = second[4]
            else:
                second = pending[l]["ffn"]
                after = x1
            if nxt is not None:
                pending[nxt] = dict(first=gather_start(BIG, nxt, after, str(nxt), behind=ffn_first[4]))
                after = pending[nxt]["first"][4]
            for n, g in zip(FFN_PART, forward_finish(second, after, f"{l}_ffn")):
                params[l][key(n)] = _weights_from_gathered(g)

        h, keep = _layer_fwd(h, params[l], str(l), dep=ffn_first[4] if l == 0 else None, before_up=before_up)
        saved.append(keep)
        if nxt is not None:
            arrived = gather_mid(pending[nxt]["first"], h, str(nxt))
            mix_second = forward_start(arrived[:len(MIX_PART)], h, f"{nxt}_mix")
            pending[nxt]["ffn"] = forward_start(arrived[len(MIX_PART):], mix_second[4], f"{nxt}_ffn")
            for n, g in zip(MIX_PART, forward_finish(mix_second, pending[nxt]["ffn"][4], f"{nxt}_mix")):
                params[nxt][key(n)] = _weights_from_gathered(g)

    def start_siblings(slabs, after, tag):
        mines = [s.reshape(N_CHIP, 2, *s.shape[1:]) for s in slabs]
        lands = _unwritten([jax.ShapeDtypeStruct((N_CHIP, *m.shape[2:]), m.dtype) for m in mines],
                           name=f"reduce_siblings_landing_{tag}")
        return _split_start(mines, lands, _siblings_plan, len(mines), after, name=f"reduce_siblings_start_{tag}")

    def start_chips(sib, after, tag):
        mines, theirs = _split_wait(sib[0], sib[1], sib[2], sib[3], after, _siblings_plan,
                                    name=f"reduce_siblings_wait_{tag}")
        pairs, lands = _pair_sum(mines, theirs, where, name=f"pair_sum_{tag}")
        return _split_start(pairs, lands, _chips_plan, 3 * len(pairs), after, name=f"reduce_chips_start_{tag}")

    def finish_reduce(fly, after, tag):
        return _split_wait(fly[0], fly[1], fly[2], fly[3], after, _chips_plan, name=f"reduce_chips_wait_{tag}")[1]

    loss_sq, dh, dh_b, dgf = _loss_bwd(h, _row(final_norm_g), target, name="loss")
    conv_g = {n: [None] * n_layers for n in CONV}
    rep_g = [None] * n_layers
    siblings = {}
    flights = {}
    token = None
    for l in reversed(range(n_layers)):
        def ffn_grads(g, after, l=l):
            siblings[l, "ffn"] = start_siblings([_slabs_from_full(g[n]) for n in FFN_PART], after, f"{l}_ffn")
            return siblings[l, "ffn"][4]

        def ffn_sent(after, l=l):
            flights[l, "ffn"] = start_chips(siblings[l, "ffn"], after, f"{l}_ffn")
            return flights[l, "ffn"][4]

        def mix_grads(g, conv, after, l=l):
            for n in CONV:
                conv_g[n][l] = conv[n]
            slabs = [_slabs_from_full(g[n]) for n in MIX_PART]
            if l == 0:
                for n in CONV:
                    full = jnp.stack(conv_g[n])
                    _, taps, c = full.shape
                    slabs.append(full.reshape(n_layers, taps, N_DEV, c // N_DEV).transpose(2, 0, 1, 3)
                                 .reshape(N_DEV, n_layers * taps, c // N_DEV))
            siblings[l, "mix"] = start_siblings(slabs, after, f"{l}_mix")
            return siblings[l, "mix"][4]

        def mix_sent(after, l=l):
            flights[l, "mix"] = start_chips(siblings[l, "mix"], after, f"{l}_mix")
            return flights[l, "mix"][4]

        dh, dh_b, rep_g[l], token = _layer_bwd(dh, dh_b, params[l], saved[l], str(l), ffn_grads, ffn_sent,
                                               mix_grads, mix_sent, dep=token)

    sums = {key: finish_reduce(fly, dh, f"{key[0]}_{key[1]}") for key, fly in flights.items() if key != (0, "mix")}
    out = {k: {} for k in KINDS}

    def adamw_big(names, part, dep):
        for q, n in enumerate(names):
            layer_parts = [sums[l, part][q] for l in range(n_layers)]
            res = _adamw_sharded(layer_parts, wt[n], mt[n], vt[n], name=f"adamw_{n}", dep=dep)
            for k, r in zip(KINDS, res):
                out[k][n] = flip(r) if n in COL_SHARDED else r

    adamw_big(FFN_PART, "ffn", token)

    rep_cols = [rep_g[l][n] for l in range(n_layers) for n in REPLICATED] + [dgf, loss_sq]
    rep_all = _all_gather([_fold_partials(rep_cols, name="fold_small")], name="gather_small")[0]
    with_final = lambda d: {**{n: d[n] for n in REPLICATED}, "final_norm_g": _row(d["final_norm_g"])}
    loss, rep_res = _adamw_replicated(rep_all, REPLICATED, with_final(w), with_final(m), with_final(v),
                                      loss_sq.shape[1], name="adamw_small")
    for n, res in rep_res.items():
        for k, r in zip(KINDS, res):
            out[k][n] = r.reshape(w[n].shape)

    last = finish_reduce(flights[0, "mix"], rep_res["b_in"][0], "0_mix")
    sums[0, "mix"] = last[:len(MIX_PART)]
    adamw_big(MIX_PART, "mix", None)
    for n, p in zip(CONV, last[len(MIX_PART):]):
        as_one = lambda a: a.reshape(1, *p.shape[1:])
        for k, r in zip(KINDS, _adamw_sharded([p], as_one(w[n]), as_one(m[n]), as_one(v[n]), name=f"adamw_{n}")):
            out[k][n] = r.reshape(w[n].shape)

    grad_x = dh.reshape(x.shape)
    return (loss.reshape(()), grad_x, *[out["grad"][n] for n in order], *[out["delta"][n] for n in order],
            *[out["m"][n] for n in order], *[out["v"][n] for n in order])
```

```python
import functools

import jax
import jax.numpy as jnp
from jax import lax
from jax.experimental import pallas as pl
from jax.experimental.pallas import tpu as pltpu

F32 = jnp.float32
BF16 = jnp.bfloat16

N_DEV = 8
N_CHIP = 4
D_CONF = 512
CONF_K = 31
SHORT_K = 3
EPS = 1e-6
HALO = 32
HALO3 = 8
HALO3_BLK = 16
LANES = 128
SUB = 8
VMEM_LIMIT = 56 * 1024 * 1024

ADAM_LR = 0.001
ADAM_B1 = 0.9
ADAM_B2 = 0.999
ADAM_EPS = 1e-08
ADAM_WD = 0.01
ADAM_STEP = 10

MESH = pl.DeviceIdType.MESH
ANY = pl.BlockSpec(memory_space=pl.ANY)


def _params(*sem):
    return pltpu.CompilerParams(dimension_semantics=sem, vmem_limit_bytes=VMEM_LIMIT)


def _resident(shape, index_map):
    return pl.BlockSpec(shape, index_map, pipeline_mode=pl.Buffered(1))


def _row_loop(n_rows, rb, fn, unroll=1):
    rb = min(rb, n_rows)

    def body(i, carry):
        fn(pl.ds(pl.multiple_of(i * rb, rb), rb))
        return carry
    lax.fori_loop(0, n_rows // rb, body, 0, unroll=unroll)


def _rows8(v):
    acc = v[0:SUB]
    for k in range(1, v.shape[0] // SUB):
        acc = acc + v[k * SUB:(k + 1) * SUB]
    return acc


def _sigmoid(z):
    return 0.5 * jnp.tanh(0.5 * z) + 0.5


def _dot(a, b):
    return jnp.dot(a, b, preferred_element_type=F32)


def _dot_nt(a, b):
    return lax.dot_general(a, b, (((1,), (1,)), ((), ())), preferred_element_type=F32)


def _dot_tn(a, b):
    return lax.dot_general(a, b, (((0,), (0,)), ((), ())), preferred_element_type=F32)


def _replicate_taps(w_ref, wrep, taps):
    for k in range(taps):
        wrep[pl.ds(k * SUB, SUB), :] = jnp.broadcast_to(w_ref[pl.ds(k, 1), :], (SUB, w_ref.shape[1]))


def _shift_copies(win, shf, lanes):
    span = win.shape[0] - SUB
    for r in range(1, SUB):
        for j0 in range(0, span, 64):
            n = min(64, span - j0)
            shf[r - 1, pl.ds(j0, n), lanes] = win[pl.ds(j0 + r, n), lanes]


def _rows_at(win, shf, off, rb, lanes):
    if shf is None or off % SUB == 0:
        return win[pl.ds(off, rb), lanes]
    return shf[off % SUB - 1, pl.ds(off - off % SUB, rb), lanes]


def _conv_taps(win, wrep, out, *, taps, n_rows, base, width, transposed=False, bias_ref=None, shf=None):
    rb = min(32 if taps > 8 else 64, n_rows)

    def lane_body(cb, carry):
        lanes = pl.ds(pl.multiple_of(cb * LANES, LANES), LANES)
        if shf is not None:
            _shift_copies(win, shf, lanes)
        for r0 in range(0, n_rows, rb):
            acc = None
            for k in range(taps):
                off = (taps - 1 - k) if transposed else (k - (taps - 1))
                wk = jnp.tile(wrep[pl.ds(k * SUB, SUB), lanes], (rb // SUB, 1))
                term = wk * _rows_at(win, shf, base + r0 + off, rb, lanes)
                acc = term if acc is None else acc + term
            if bias_ref is not None:
                acc = acc + bias_ref[:, lanes]
            out[pl.ds(r0, rb), lanes] = acc.astype(out.dtype)
        return carry

    lax.fori_loop(0, width // LANES, lane_body, 0)


def _conv_bwd_taps(win, wrep, x_cur, dx_out, dw_acc, *, taps, n_rows, width, shf=None):
    rb = min(32 if taps > 8 else 64, n_rows)

    def lane_body(cb, carry):
        lanes = pl.ds(pl.multiple_of(cb * LANES, LANES), LANES)
        if shf is not None:
            _shift_copies(win, shf, lanes)
        sums = [None] * taps
        for r0 in range(0, n_rows, rb):
            xv = x_cur[pl.ds(r0, rb), lanes].astype(F32)
            acc = None
            for k in range(taps):
                shifted = _rows_at(win, shf, r0 + taps - 1 - k, rb, lanes)
                term = jnp.tile(wrep[pl.ds(k * SUB, SUB), lanes], (rb // SUB, 1)) * shifted
                acc = term if acc is None else acc + term
                part = _rows8(xv * shifted)
                sums[k] = part if sums[k] is None else sums[k] + part
            dx_out[pl.ds(r0, rb), lanes] = acc.astype(dx_out.dtype)
        for k in range(taps):
            dw_acc[pl.ds(k * SUB, SUB), lanes] += sums[k]
        return carry

    lax.fori_loop(0, width // LANES, lane_body, 0)


def _fold8(acc_ref, taps):
    return jnp.concatenate(
        [jnp.sum(acc_ref[pl.ds(k * SUB, SUB), :], axis=0, keepdims=True) for k in range(taps)], axis=0)


def _seq_tile(s_len):
    return min(512, s_len)


def _mm_tile(s_len):
    return min(512, s_len)


def _ff_chunk(ff):
    best = LANES
    for c in range(LANES, 1408 + 1, LANES):
        if ff % c == 0:
            best = c
    return best


def _col_tile(n):
    for c in (512, 1408, 256, LANES):
        if n % c == 0:
            return c
    return n


def _rms_matmul(x, g, wt, b, *, name, dep=None):
    s_len, d = x.shape
    n = wt.shape[0]
    tm = _mm_tile(s_len)
    cn = _col_tile(n)
    has_bias = b is not None

    def body(*refs):
        x_ref, g_ref, w_ref = refs[0:3]
        b_ref = refs[3] if has_bias else None
        o_ref, h_ref = refs[-2:]

        def blk(rows):
            xv = x_ref[rows, :]
            r = lax.rsqrt(jnp.mean(xv * xv, axis=-1, keepdims=True) + EPS)
            h_ref[rows, :] = ((xv * r) * g_ref[...]).astype(BF16)

        rb = min(128, tm)
        for r0 in range(0, tm, rb):
            blk(pl.ds(r0, rb))
        for j in range(n // cn):
            acc = _dot_nt(h_ref[...], w_ref[j * cn:(j + 1) * cn, :])
            if has_bias:
                acc = acc + b_ref[:, j * cn:(j + 1) * cn]
            o_ref[:, j * cn:(j + 1) * cn] = acc.astype(BF16)

    in_specs = [pl.BlockSpec((tm, d), lambda i: (i, 0)), _resident((1, d), lambda i: (0, 0)),
                _resident((n, d), lambda i: (0, 0))]
    args = [x, g, wt]
    if has_bias:
        in_specs.append(_resident((1, n), lambda i: (0, 0)))
        args.append(b)
    in_specs.append(ANY)
    args.append(x if dep is None else dep)
    return pl.pallas_call(
        body, grid=(s_len // tm,), in_specs=in_specs,
        out_specs=[pl.BlockSpec((tm, n), lambda i: (i, 0)), pl.BlockSpec((tm, d), lambda i: (i, 0))],
        out_shape=[jax.ShapeDtypeStruct((s_len, n), BF16), jax.ShapeDtypeStruct((s_len, d), BF16)],
        compiler_params=_params("parallel"), name=name,
    )(*args)


def _mix_windows(u_ref, uh_ref, gw, pw, first, t):
    c = D_CONF
    uh = uh_ref[...].astype(F32)
    gw[0:HALO, :] = jnp.where(first, 0.0, uh[:, 0:c] * _sigmoid(uh[:, c:2 * c]))
    pw[0:HALO3, :] = jnp.where(first, 0.0, uh[HALO - HALO3:HALO, 3 * c:4 * c] * uh[HALO - HALO3:HALO, 4 * c:5 * c])

    def blk(rows):
        dst = pl.ds(pl.multiple_of(rows.start + HALO, SUB), rows.size)
        gw[dst, :] = u_ref[rows, 0:c].astype(F32) * _sigmoid(u_ref[rows, c:2 * c].astype(F32))
        dst3 = pl.ds(pl.multiple_of(rows.start + HALO3, SUB), rows.size)
        pw[dst3, :] = u_ref[rows, 3 * c:4 * c].astype(F32) * u_ref[rows, 4 * c:5 * c].astype(F32)
    _row_loop(t, 64, blk)


def _mix_fwd(u, x0, wa, ba, lg, lb, wb, w_out, *, name):
    s_len, d_in = u.shape
    d = x0.shape[1]
    c = D_CONF
    t = _seq_tile(s_len)
    per = t // HALO

    def body(u_ref, uh_ref, x0_ref, wa_ref, ba_ref, lg_ref, lb_ref, wb_ref, wo_ref, y_ref, x1_ref, ca, cb,
             gw, pw, wrep_a, wrep_b, shf):
        first = pl.program_id(0) == 0
        _mix_windows(u_ref, uh_ref, gw, pw, first, t)
        _replicate_taps(wa_ref, wrep_a, CONF_K)
        _replicate_taps(wb_ref, wrep_b, SHORT_K)
        _conv_taps(gw, wrep_a, ca, taps=CONF_K, n_rows=t, base=HALO, width=c, bias_ref=ba_ref, shf=shf)
        _conv_taps(pw, wrep_b, cb, taps=SHORT_K, n_rows=t, base=HALO3, width=c)

        def blk(rows):
            cv = ca[rows, :]
            mu = jnp.mean(cv, axis=-1, keepdims=True)
            xc = cv - mu
            var = jnp.mean(xc * xc, axis=-1, keepdims=True)
            ln = (xc * lax.rsqrt(var + EPS)) * lg_ref[...] + lb_ref[...]
            y_ref[rows, 0:c] = (ln * _sigmoid(ln)).astype(BF16)
            y_ref[rows, c:2 * c] = (u_ref[rows, 2 * c:3 * c].astype(F32) * cb[rows, :]).astype(BF16)
        half = t // 2
        rb = min(64, half)
        for lo in range(0, t, half):
            for r0 in range(lo, lo + half, rb):
                blk(pl.ds(r0, rb))
            x1_ref[lo:lo + half, :] = x0_ref[lo:lo + half, :] + _dot(y_ref[lo:lo + half, :], wo_ref[...])

    small = lambda r: _resident((r, c), lambda i: (0, 0))
    return pl.pallas_call(
        body, grid=(s_len // t,),
        in_specs=[pl.BlockSpec((t, d_in), lambda i: (i, 0)),
                  pl.BlockSpec((HALO, d_in), lambda i: (jnp.maximum(i * per - 1, 0), 0)),
                  pl.BlockSpec((t, d), lambda i: (i, 0)),
                  small(CONF_K), small(1), small(1), small(1), small(SHORT_K),
                  _resident((2 * c, d), lambda i: (0, 0))],
        out_specs=[pl.BlockSpec((t, 2 * c), lambda i: (i, 0)), pl.BlockSpec((t, d), lambda i: (i, 0)),
                   pl.BlockSpec((t, c), lambda i: (i, 0)), pl.BlockSpec((t, c), lambda i: (i, 0))],
        out_shape=[jax.ShapeDtypeStruct((s_len, 2 * c), BF16), jax.ShapeDtypeStruct((s_len, d), F32),
                   jax.ShapeDtypeStruct((s_len, c), F32), jax.ShapeDtypeStruct((s_len, c), F32)],
        scratch_shapes=[pltpu.VMEM((HALO + t, c), F32), pltpu.VMEM((HALO3 + t, c), F32),
                        pltpu.VMEM((CONF_K * SUB, c), F32), pltpu.VMEM((SHORT_K * SUB, c), F32),
                        pltpu.VMEM((SUB - 1, HALO + t, c), F32)],
        compiler_params=_params("arbitrary"), name=name,
    )(u, u, x0, wa, ba, lg, lb, wb, w_out)


def _ffn_fwd(uf, x1, wf, w_down, *, name):
    s_len, ff2 = uf.shape
    ff = ff2 // 2
    d = x1.shape[1]
    t = _seq_tile(s_len)
    fc = _ff_chunk(ff)
    nc = ff // fc
    per = t // HALO3_BLK
    half = t // 2
    rb = min(64, half)

    def body(ug_ref, ugh_ref, uv_ref, uvh_ref, x1_ref, wfg_ref, wfv_ref, wd_ref,
             act_ref, x2_ref, cg_ref, cv_ref, gwin, vwin, wrep_g, wrep_v):
        first = pl.program_id(0) == 0
        first_chunk = pl.program_id(1) == 0
        lo8 = HALO3_BLK - HALO3
        gwin[0:HALO3, :] = jnp.where(first, 0.0, ugh_ref[...].astype(F32)[lo8:HALO3_BLK])
        vwin[0:HALO3, :] = jnp.where(first, 0.0, uvh_ref[...].astype(F32)[lo8:HALO3_BLK])
        _replicate_taps(wfg_ref, wrep_g, SHORT_K)
        _replicate_taps(wfv_ref, wrep_v, SHORT_K)
        chunk_rows = pl.ds(pl.multiple_of(pl.program_id(1) * fc, fc), fc)

        def conv(win, wrep, r0, lanes):
            acc = None
            for k in range(SHORT_K):
                wk = jnp.tile(wrep[k * SUB:(k + 1) * SUB, lanes], (rb // SUB, 1))
                off = HALO3 + r0 + k - (SHORT_K - 1)
                term = wk * win[off:off + rb, lanes]
                acc = term if acc is None else acc + term
            return acc

        for lo in range(0, t, half):
            for r0 in range(lo, lo + half, rb):
                gwin[HALO3 + r0:HALO3 + r0 + rb, :] = ug_ref[r0:r0 + rb, :].astype(F32)
                vwin[HALO3 + r0:HALO3 + r0 + rb, :] = uv_ref[r0:r0 + rb, :].astype(F32)
            for cb in range(fc // LANES):
                lanes = slice(cb * LANES, (cb + 1) * LANES)
                for r0 in range(lo, lo + half, rb):
                    gv = conv(gwin, wrep_g, r0, lanes).astype(BF16)
                    vv = conv(vwin, wrep_v, r0, lanes).astype(BF16)
                    cg_ref[r0:r0 + rb, lanes] = gv
                    cv_ref[r0:r0 + rb, lanes] = vv
                    act_ref[r0:r0 + rb, lanes] = (gv * _sigmoid(gv)) * vv
            base = jnp.where(first_chunk, x1_ref[lo:lo + half, :], x2_ref[lo:lo + half, :])
            x2_ref[lo:lo + half, :] = base + _dot(act_ref[lo:lo + half, :], wd_ref[chunk_rows, :])

    halo_map = lambda off: (lambda i, j: (jnp.maximum(i * per - 1, 0), j + off))
    return pl.pallas_call(
        body, grid=(s_len // t, nc),
        in_specs=[pl.BlockSpec((t, fc), lambda i, j: (i, j)), pl.BlockSpec((HALO3_BLK, fc), halo_map(0)),
                  pl.BlockSpec((t, fc), lambda i, j: (i, j + nc)), pl.BlockSpec((HALO3_BLK, fc), halo_map(nc)),
                  pl.BlockSpec((t, d), lambda i, j: (i, 0)),
                  pl.BlockSpec((SHORT_K, fc), lambda i, j: (0, j)),
                  pl.BlockSpec((SHORT_K, fc), lambda i, j: (0, j + nc)),
                  _resident((ff, d), lambda i, j: (0, 0))],
        out_specs=[pl.BlockSpec((t, fc), lambda i, j: (i, j)), pl.BlockSpec((t, d), lambda i, j: (i, 0)),
                   pl.BlockSpec((t, fc), lambda i, j: (i, j)), pl.BlockSpec((t, fc), lambda i, j: (i, j))],
        out_shape=[jax.ShapeDtypeStruct((s_len, ff), BF16), jax.ShapeDtypeStruct((s_len, d), F32),
                   jax.ShapeDtypeStruct((s_len, ff), BF16), jax.ShapeDtypeStruct((s_len, ff), BF16)],
        scratch_shapes=[pltpu.VMEM((HALO3 + t, fc), F32), pltpu.VMEM((HALO3 + t, fc), F32),
                        pltpu.VMEM((SHORT_K * SUB, fc), F32), pltpu.VMEM((SHORT_K * SUB, fc), F32)],
        compiler_params=_params("parallel", "arbitrary"), name=name,
    )(uf, uf, uf, uf, x1, wf, wf, w_down)


def _loss_bwd(x, g, target, *, name):
    s_len, d = x.shape
    t = _seq_tile(s_len)

    def body(x_ref, g_ref, t_ref, l_ref, dx_ref, dxb_ref, dg_ref):
        @pl.when(pl.program_id(0) == 0)
        def _():
            l_ref[...] = jnp.zeros_like(l_ref)
            dg_ref[...] = jnp.zeros_like(dg_ref)

        def blk(rows):
            xv = x_ref[rows, :]
            r = lax.rsqrt(jnp.mean(xv * xv, axis=-1, keepdims=True) + EPS)
            xn = xv * r
            e = xn * g_ref[...] - t_ref[rows, :]
            l_ref[...] += _rows8(e * e)
            dy = e * (1.0 / d)
            dg_ref[...] += _rows8(dy * xn)
            dn = dy * g_ref[...]
            dx = r * (dn - xn * jnp.mean(dn * xn, axis=-1, keepdims=True))
            dx_ref[rows, :] = dx
            dxb_ref[rows, :] = dx.astype(BF16)
        _row_loop(t, 64, blk)

    row = pl.BlockSpec((t, d), lambda i: (i, 0))
    part = pl.BlockSpec((SUB, d), lambda i: (0, 0))
    return pl.pallas_call(
        body, grid=(s_len // t,),
        in_specs=[row, _resident((1, d), lambda i: (0, 0)), row],
        out_specs=[part, row, row, part],
        out_shape=[jax.ShapeDtypeStruct((SUB, d), F32), jax.ShapeDtypeStruct((s_len, d), F32),
                   jax.ShapeDtypeStruct((s_len, d), BF16), jax.ShapeDtypeStruct((SUB, d), F32)],
        compiler_params=_params("arbitrary"), name=name,
    )(x, g, target)


def _ffn_bwd(dx2, uf, cg, cv, wf, w_down, *, name, dep=None):
    s_len, ff2 = uf.shape
    ff = ff2 // 2
    d = dx2.shape[1]
    t = _seq_tile(s_len)
    n_t = s_len // t
    fc = _ff_chunk(ff)
    nc = ff // fc

    def body(dx_ref, ug_ref, uv_ref, cg_ref, cv_ref, wfg_ref, wfv_ref, wd_ref, dep_ref,
             duf_ref, dwg_ref, dwv_ref, dact, dgw, dvw, awg, awv, wrep_g, wrep_v):
        i = pl.program_id(1)

        @pl.when(i == 0)
        def _():
            dgw[t:t + HALO3, :] = jnp.zeros((HALO3, fc), F32)
            dvw[t:t + HALO3, :] = jnp.zeros((HALO3, fc), F32)
            awg[...] = jnp.zeros_like(awg)
            awv[...] = jnp.zeros_like(awv)

        _replicate_taps(wfg_ref, wrep_g, SHORT_K)
        _replicate_taps(wfv_ref, wrep_v, SHORT_K)

        def blk(rows):
            gv = cg_ref[rows, :]
            sg = _sigmoid(gv)
            da = dact[rows, :].astype(BF16)
            dgw[rows, :] = ((da * cv_ref[rows, :]) * (sg * (1.0 + gv * (1.0 - sg)))).astype(F32)
            dvw[rows, :] = (da * (gv * sg)).astype(F32)

        dact[...] = _dot_nt(dx_ref[...], wd_ref[...])
        _row_loop(t, 64, blk)

        _conv_bwd_taps(dgw, wrep_g, ug_ref, duf_ref.at[0], awg, taps=SHORT_K, n_rows=t, width=fc)
        _conv_bwd_taps(dvw, wrep_v, uv_ref, duf_ref.at[1], awv, taps=SHORT_K, n_rows=t, width=fc)
        dgw[t:t + HALO3, :] = dgw[0:HALO3, :]
        dvw[t:t + HALO3, :] = dvw[0:HALO3, :]

        @pl.when(i == n_t - 1)
        def _():
            dwg_ref[...] = _fold8(awg, SHORT_K)
            dwv_ref[...] = _fold8(awv, SHORT_K)

    rev = lambda i: n_t - 1 - i
    gate = pl.BlockSpec((t, fc), lambda j, i: (rev(i), j))
    value = pl.BlockSpec((t, fc), lambda j, i: (rev(i), j + nc))
    return pl.pallas_call(
        body, grid=(nc, n_t),
        in_specs=[pl.BlockSpec((t, d), lambda j, i: (rev(i), 0)), gate, value, gate, gate,
                  pl.BlockSpec((SHORT_K, fc), lambda j, i: (0, j)),
                  pl.BlockSpec((SHORT_K, fc), lambda j, i: (0, j + nc)),
                  pl.BlockSpec((fc, d), lambda j, i: (j, 0)), ANY],
        out_specs=[pl.BlockSpec((2, t, fc), lambda j, i: (0, rev(i), j)),
                   pl.BlockSpec((SHORT_K, fc), lambda j, i: (0, j)), pl.BlockSpec((SHORT_K, fc), lambda j, i: (0, j))],
        out_shape=[jax.ShapeDtypeStruct((2, s_len, ff), BF16),
                   jax.ShapeDtypeStruct((SHORT_K, ff), F32), jax.ShapeDtypeStruct((SHORT_K, ff), F32)],
        scratch_shapes=[pltpu.VMEM((t, fc), F32),
                        pltpu.VMEM((t + HALO3, fc), F32), pltpu.VMEM((t + HALO3, fc), F32),
                        pltpu.VMEM((SHORT_K * SUB, fc), F32), pltpu.VMEM((SHORT_K * SUB, fc), F32),
                        pltpu.VMEM((SHORT_K * SUB, fc), F32), pltpu.VMEM((SHORT_K * SUB, fc), F32)],
        compiler_params=_params("arbitrary", "arbitrary"), name=name,
    )(dx2, uf, uf, cg, cv, wf, wf, w_down, uf if dep is None else dep)


def _mix_bwd(dx1, u, ca, cb, wa, lg, lb, wb, w_out, *, name, dep=None):
    s_len, d_in = u.shape
    d = dx1.shape[1]
    c = D_CONF
    t = _seq_tile(s_len)
    n_t = s_len // t

    def body(dx_ref, u_ref, ca_ref, cb_ref, wa_ref, lg_ref, lb_ref, wb_ref, wo_ref, dep_ref,
             du_ref, dwa_ref, dwb_ref, dba_ref, dlg_ref, dlb_ref, dbin_ref,
             glu, prod, dyc, dcaw, dcbw, dglu, dp, awa, awb, wrep_a, wrep_b, shf):
        i = pl.program_id(0)
        _replicate_taps(wa_ref, wrep_a, CONF_K)
        _replicate_taps(wb_ref, wrep_b, SHORT_K)

        @pl.when(i == 0)
        def _():
            dcaw[t:t + HALO, :] = jnp.zeros((HALO, c), F32)
            dcbw[t:t + HALO3, :] = jnp.zeros((HALO3, c), F32)
            awa[...] = jnp.zeros_like(awa)
            awb[...] = jnp.zeros_like(awb)
            dba_ref[...] = jnp.zeros_like(dba_ref)
            dlg_ref[...] = jnp.zeros_like(dlg_ref)
            dlb_ref[...] = jnp.zeros_like(dlb_ref)
            dbin_ref[...] = jnp.zeros_like(dbin_ref)

        def blk1(rows):
            cv = ca_ref[rows, :]
            mu = jnp.mean(cv, axis=-1, keepdims=True)
            xc = cv - mu
            rstd = lax.rsqrt(jnp.mean(xc * xc, axis=-1, keepdims=True) + EPS)
            nrm = xc * rstd
            ln = nrm * lg_ref[...] + lb_ref[...]
            sg = _sigmoid(ln)
            dln = dyc[rows, 0:c] * (sg * (1.0 + ln * (1.0 - sg)))
            dlg_ref[...] += _rows8(dln * nrm)
            dlb_ref[...] += _rows8(dln)
            dn = dln * lg_ref[...]
            dca = rstd * (dn - jnp.mean(dn, axis=-1, keepdims=True)
                          - nrm * jnp.mean(dn * nrm, axis=-1, keepdims=True))
            dcaw[rows, :] = dca
            dba_ref[...] += _rows8(dca)
            ds = dyc[rows, c:2 * c]
            dgb = ds * cb_ref[rows, :]
            dcbw[rows, :] = ds * u_ref[rows, 2 * c:3 * c].astype(F32)
            du_ref[rows, 2 * c:3 * c] = dgb.astype(BF16)
            dbin_ref[:, 2 * c:3 * c] += _rows8(dgb)
            glu[rows, :] = u_ref[rows, 0:c].astype(F32) * _sigmoid(u_ref[rows, c:2 * c].astype(F32))
            prod[rows, :] = u_ref[rows, 3 * c:4 * c].astype(F32) * u_ref[rows, 4 * c:5 * c].astype(F32)
        half = t // 2
        rb = min(64, half)
        for lo in range(0, t, half):
            dyc[lo:lo + half, :] = _dot_nt(dx_ref[lo:lo + half, :], wo_ref[...])
            for r0 in range(lo, lo + half, rb):
                blk1(pl.ds(r0, rb))

        _conv_bwd_taps(dcaw, wrep_a, glu, dglu, awa, taps=CONF_K, n_rows=t, width=c, shf=shf)
        _conv_bwd_taps(dcbw, wrep_b, prod, dp, awb, taps=SHORT_K, n_rows=t, width=c)
        dcaw[t:t + HALO, :] = dcaw[0:HALO, :]
        dcbw[t:t + HALO3, :] = dcbw[0:HALO3, :]

        def blk2(rows):
            av = u_ref[rows, 0:c].astype(F32)
            sg = _sigmoid(u_ref[rows, c:2 * c].astype(F32))
            dg = dglu[rows, :]
            d_av = dg * sg
            d_ag = (dg * av) * (sg * (1.0 - sg))
            dpv = dp[rows, :]
            d_gc = dpv * u_ref[rows, 4 * c:5 * c].astype(F32)
            d_vs = dpv * u_ref[rows, 3 * c:4 * c].astype(F32)
            du_ref[rows, 0:c] = d_av.astype(BF16)
            du_ref[rows, c:2 * c] = d_ag.astype(BF16)
            du_ref[rows, 3 * c:4 * c] = d_gc.astype(BF16)
            du_ref[rows, 4 * c:5 * c] = d_vs.astype(BF16)
            dbin_ref[:, 0:c] += _rows8(d_av)
            dbin_ref[:, c:2 * c] += _rows8(d_ag)
            dbin_ref[:, 3 * c:4 * c] += _rows8(d_gc)
            dbin_ref[:, 4 * c:5 * c] += _rows8(d_vs)
        _row_loop(t, 64, blk2)

        @pl.when(i == n_t - 1)
        def _():
            dwa_ref[...] = _fold8(awa, CONF_K)
            dwb_ref[...] = _fold8(awb, SHORT_K)

    rev = lambda i: n_t - 1 - i
    small_in = lambda r: _resident((r, c), lambda i: (0, 0))
    small = lambda r: pl.BlockSpec((r, c), lambda i: (0, 0))
    return pl.pallas_call(
        body, grid=(n_t,),
        in_specs=[pl.BlockSpec((t, d), lambda i: (rev(i), 0)),
                  pl.BlockSpec((t, d_in), lambda i: (rev(i), 0)),
                  pl.BlockSpec((t, c), lambda i: (rev(i), 0)), pl.BlockSpec((t, c), lambda i: (rev(i), 0)),
                  small_in(CONF_K), small_in(1), small_in(1), small_in(SHORT_K),
                  _resident((2 * c, d), lambda i: (0, 0)), ANY],
        out_specs=[pl.BlockSpec((t, d_in), lambda i: (rev(i), 0)),
                   small(CONF_K), small(SHORT_K), small(SUB), small(SUB), small(SUB),
                   pl.BlockSpec((SUB, d_in), lambda i: (0, 0))],
        out_shape=[jax.ShapeDtypeStruct((s_len, d_in), BF16),
                   jax.ShapeDtypeStruct((CONF_K, c), F32), jax.ShapeDtypeStruct((SHORT_K, c), F32),
                   jax.ShapeDtypeStruct((SUB, c), F32), jax.ShapeDtypeStruct((SUB, c), F32),
                   jax.ShapeDtypeStruct((SUB, c), F32), jax.ShapeDtypeStruct((SUB, d_in), F32)],
        scratch_shapes=[pltpu.VMEM((t, c), F32), pltpu.VMEM((t, c), F32), pltpu.VMEM((t, 2 * c), F32),
                        pltpu.VMEM((t + HALO, c), F32), pltpu.VMEM((t + HALO3, c), F32),
                        pltpu.VMEM((t, c), F32), pltpu.VMEM((t, c), F32),
                        pltpu.VMEM((CONF_K * SUB, c), F32), pltpu.VMEM((SHORT_K * SUB, c), F32),
                        pltpu.VMEM((CONF_K * SUB, c), F32), pltpu.VMEM((SHORT_K * SUB, c), F32),
                        pltpu.VMEM((SUB - 1, t + HALO, c), F32)],
        compiler_params=_params("arbitrary"), name=name,
    )(dx1, u, ca, cb, wa, lg, lb, wb, w_out, u if dep is None else dep)


def _matmul_tn(a, b, *, name):
    n_p, s_len, k = a.shape
    n = b.shape[1]
    tk = _col_tile(k)
    per = k // tk
    if per > 2:
        def body(a_ref, b_ref, o_ref):
            o_ref[...] = _dot_tn(a_ref[...], b_ref[...]).astype(BF16)

        return pl.pallas_call(
            body, grid=(n_p, per),
            in_specs=[pl.BlockSpec((None, s_len, tk), lambda p, j: (p, 0, j)),
                      _resident((s_len, n), lambda p, j: (0, 0))],
            out_specs=pl.BlockSpec((tk, n), lambda p, j: (p * per + j, 0)),
            out_shape=jax.ShapeDtypeStruct((n_p * k, n), BF16),
            compiler_params=_params("parallel", "parallel"), name=name,
        )(a, b)

    half = s_len // 2

    def body_halves(a_ref, b_ref, o_ref, acc):
        @pl.when(pl.program_id(2) == 0)
        def _():
            acc[...] = _dot_tn(a_ref[...], b_ref[...])

        @pl.when(pl.program_id(2) == 1)
        def _():
            o_ref[...] = (acc[...] + _dot_tn(a_ref[...], b_ref[...])).astype(BF16)

    return pl.pallas_call(
        body_halves, grid=(n_p, per, 2),
        in_specs=[pl.BlockSpec((None, half, tk), lambda p, j, q: (p, q, j)),
                  pl.BlockSpec((half, n), lambda p, j, q: (q, 0))],
        out_specs=pl.BlockSpec((tk, n), lambda p, j, q: (p * per + j, 0)),
        out_shape=jax.ShapeDtypeStruct((n_p * k, n), BF16),
        scratch_shapes=[pltpu.VMEM((tk, n), F32)],
        compiler_params=_params("parallel", "parallel", "arbitrary"), name=name,
    )(a, b)


def _matmul_rmsbwd(dzs, wt, x, g, dx_in, *, name, dep=None):
    s_len, d = x.shape
    n_z, _, nj = dzs.shape
    t = _mm_tile(s_len)

    def body(*refs):
        dz_refs = refs[0:n_z]
        w_refs = refs[n_z:2 * n_z]
        x_ref, g_ref, dxi_ref, _, dx_ref, dxb_ref, dg_ref, dh = refs[2 * n_z:]

        @pl.when(pl.program_id(0) == 0)
        def _():
            dg_ref[...] = jnp.zeros_like(dg_ref)

        def blk(rows):
            xv = x_ref[rows, :]
            r = lax.rsqrt(jnp.mean(xv * xv, axis=-1, keepdims=True) + EPS)
            xn = xv * r
            dhv = dh[rows, :]
            dg_ref[...] += _rows8(dhv * xn)
            dn = dhv * g_ref[...]
            dx = dxi_ref[rows, :] + r * (dn - xn * jnp.mean(dn * xn, axis=-1, keepdims=True))
            dx_ref[rows, :] = dx
            dxb_ref[rows, :] = dx.astype(BF16)

        half = t // 2
        rb = min(128, half)
        for lo in range(0, t, half):
            acc = _dot(dz_refs[0][lo:lo + half, :], w_refs[0][...])
            for q in range(1, n_z):
                acc = acc + _dot(dz_refs[q][lo:lo + half, :], w_refs[q][...])
            dh[lo:lo + half, :] = acc
            for r0 in range(lo, lo + half, rb):
                blk(pl.ds(r0, rb))

    row = pl.BlockSpec((t, d), lambda i: (i, 0))
    in_specs = [pl.BlockSpec((None, t, nj), functools.partial(lambda q, i: (q, i, 0), q)) for q in range(n_z)]
    in_specs += [_resident((nj, d), functools.partial(lambda q, i: (q, 0), q)) for q in range(n_z)]
    in_specs += [row, _resident((1, d), lambda i: (0, 0)), row, ANY]
    return pl.pallas_call(
        body, grid=(s_len // t,), in_specs=in_specs,
        out_specs=[row, row, pl.BlockSpec((SUB, d), lambda i: (0, 0))],
        out_shape=[jax.ShapeDtypeStruct((s_len, d), F32), jax.ShapeDtypeStruct((s_len, d), BF16),
                   jax.ShapeDtypeStruct((SUB, d), F32)],
        scratch_shapes=[pltpu.VMEM((t, d), F32)],
        compiler_params=_params("arbitrary"), name=name,
    )(*([dzs] * n_z), *([wt] * n_z), x, g, dx_in, x if dep is None else dep)


def _row(v):
    return v.reshape(1, -1)


def _layer_fwd(x0, p, tag, dep=None, before_up=None):
    u, h1 = _rms_matmul(x0, _row(p["mix_norm_g"]), p["w_in_t"], _row(p["b_in"]), name=f"in_proj_{tag}", dep=dep)
    ycat, x1, ca, cb = _mix_fwd(u, x0, p["conv_a_w"], _row(p["conv_a_b"]), _row(p["ln_a_g"]), _row(p["ln_a_b"]),
                            p["conv_b_w"], p["w_out"], name=f"mix_fwd_{tag}")
    if before_up is not None:
        before_up(x1)
    uf, h2 = _rms_matmul(x1, _row(p["ffn_norm_g"]), p["w_up_t"], None, name=f"up_proj_{tag}")
    act, x2, cg, cv = _ffn_fwd(uf, x1, p["conv_f_w"], p["w_down"], name=f"ffn_fwd_{tag}")
    return x2, dict(x0=x0, h1=h1, u=u, ca=ca, cb=cb, ycat=ycat, x1=x1, h2=h2, uf=uf, cg=cg, cv=cv, act=act)


def _layer_bwd(dx2, dx2_b, p, saved, tag, ffn_grads, ffn_sent, mix_grads, mix_sent, dep=None):
    d_uf, dwf_g, dwf_v = _ffn_bwd(dx2_b, saved["uf"], saved["cg"], saved["cv"], p["conv_f_w"], p["w_down"],
                                  name=f"ffn_bwd_{tag}", dep=dep)
    g_down = _matmul_tn(saved["act"][None], dx2_b, name=f"dw_down_{tag}")
    g_up = _matmul_tn(d_uf, saved["h2"], name=f"dw_up_{tag}")
    dep_ffn = ffn_grads(dict(w_up=g_up, w_down=g_down), dx2_b)
    dx1, dx1_b, dg2 = _matmul_rmsbwd(d_uf, p["w_up_t"], saved["x1"], _row(p["ffn_norm_g"]), dx2,
                                     name=f"dh_ffn_{tag}", dep=dep_ffn)
    du, dwa, dwb, dba, dlg, dlb, dbin = _mix_bwd(
        dx1_b, saved["u"], saved["ca"], saved["cb"], p["conv_a_w"], _row(p["ln_a_g"]), _row(p["ln_a_b"]),
        p["conv_b_w"], p["w_out"], name=f"mix_bwd_{tag}", dep=ffn_sent(dx1_b))
    g_out = _matmul_tn(saved["ycat"][None], dx1_b, name=f"dw_out_{tag}")
    g_in = _matmul_tn(du[None], saved["h1"], name=f"dw_in_{tag}")
    conv = dict(conv_a_w=dwa, conv_b_w=dwb, conv_f_w=jnp.concatenate([dwf_g, dwf_v], axis=1))
    dep_mix = mix_grads(dict(w_in=g_in, w_out=g_out), conv, dx1_b)
    dx0, dx0_b, dg1 = _matmul_rmsbwd(du[None], p["w_in_t"], saved["x0"], _row(p["mix_norm_g"]), dx1,
                                     name=f"dh_mix_{tag}", dep=dep_mix)
    rep = dict(mix_norm_g=dg1, b_in=dbin, conv_a_b=dba, ln_a_g=dlg, ln_a_b=dlb, ffn_norm_g=dg2)
    return dx0, dx0_b, rep, mix_sent(dx0_b)


def _place():
    return lax.axis_index("x"), lax.axis_index("y"), lax.axis_index("c")


def _all_gather(arrs, *, name):
    n_a = len(arrs)

    def body(*refs):
        ins = refs[0:n_a]
        outs = refs[n_a:2 * n_a]
        send_sems, recv_sems, local_sems = refs[2 * n_a:]
        x, y, c = _place()
        sibling = (x, y, 1 - c)
        chips = [(1 - x, y), (x, 1 - y), (1 - x, 1 - y)]

        def slot(a, px, py, pc):
            return outs[a].at[4 * px + 2 * py + pc]

        def copy(a, k, block, to, src=None):
            return pltpu.make_async_remote_copy(
                src_ref=slot(a, *block) if src is None else src, dst_ref=slot(a, *block),
                send_sem=send_sems.at[a, k], recv_sem=recv_sems.at[a, k],
                device_id=to, device_id_type=MESH)

        me = (x, y, c)
        mine = [pltpu.make_async_copy(ins[a], slot(a, *me), local_sems.at[a]) for a in range(n_a)]
        for cp in mine:
            cp.start()
        started = []
        for a in range(n_a):
            first = [copy(a, 0, me, sibling, src=ins[a])]
            first += [copy(a, 1 + j, me, (*chip, c), src=ins[a]) for j, chip in enumerate(chips)]
            for cp in first:
                cp.start()
            started += first
        for a in range(n_a):
            for j, chip in enumerate(chips):
                copy(a, 1 + j, (*chip, c), me).wait_recv()
                passed = copy(a, 4 + j, (*chip, c), sibling)
                passed.start()
                started.append(passed)
        for a in range(n_a):
            copy(a, 0, sibling, me).wait_recv()
            for j, chip in enumerate(chips):
                copy(a, 4 + j, (*chip, 1 - c), me).wait_recv()
        for cp in started:
            cp.wait_send()
        for cp in mine:
            cp.wait()

    return pl.pallas_call(
        body, in_specs=[ANY] * n_a, out_specs=[ANY] * n_a,
        out_shape=[jax.ShapeDtypeStruct((N_DEV, *a.shape), a.dtype) for a in arrs],
        scratch_shapes=[pltpu.SemaphoreType.DMA((n_a, 7)), pltpu.SemaphoreType.DMA((n_a, 7)),
                        pltpu.SemaphoreType.DMA((n_a,))],
        name=name,
    )(*arrs)


def _row_tile(r, cap):
    for tr in range(min(cap, r) // 16 * 16, 0, -16):
        if r % tr == 0:
            return tr
    return r


def _pair_sum(mines, theirs, where, *, name):
    n_a = len(mines)
    n_chip = mines[0].shape[0]

    def body(where_ref, *refs):
        a_refs = refs[0:n_a]
        b_refs = refs[n_a:2 * n_a]
        p_refs = refs[2 * n_a:3 * n_a]
        l_refs = refs[3 * n_a:4 * n_a]
        q = pl.program_id(0)
        for a in range(n_a):
            p_refs[a][...] = (a_refs[a][...].astype(F32) + b_refs[a][...].astype(F32)).astype(p_refs[a].dtype)

        @pl.when(q == where_ref[1])
        def _():
            for a in range(n_a):
                l_refs[a][...] = p_refs[a][...]

    in_specs, out_p, out_l, shapes = [], [], [], []
    for m in mines:
        _, _, r, c = m.shape
        in_specs.append(pl.BlockSpec((None, None, r, c), lambda q, where_ref: (q, where_ref[0], 0, 0)))
    for m in mines:
        _, _, r, c = m.shape
        in_specs.append(pl.BlockSpec((None, r, c), lambda q, where_ref: (q, 0, 0)))
        out_p.append(pl.BlockSpec((None, r, c), lambda q, where_ref: (q, 0, 0)))
        out_l.append(pl.BlockSpec((None, r, c), lambda q, where_ref: (where_ref[1], 0, 0)))
        shapes.append(jax.ShapeDtypeStruct((n_chip, r, c), m.dtype))
    res = pl.pallas_call(
        body,
        grid_spec=pltpu.PrefetchScalarGridSpec(num_scalar_prefetch=1, grid=(n_chip,), in_specs=in_specs,
                                               out_specs=out_p + out_l),
        out_shape=shapes + shapes,
        compiler_params=_params("arbitrary"), name=name,
    )(where, *mines, *theirs)
    return list(res[:n_a]), list(res[n_a:])


HBM = pl.BlockSpec(memory_space=pltpu.HBM)
SEM = pl.BlockSpec(memory_space=pltpu.SEMAPHORE)
EFFECT = pltpu.SideEffectType.DATAFLOW_SIDE_EFFECTING


def _in_hbm(a):
    return pltpu.with_memory_space_constraint(a, pltpu.HBM)


def _split_start(srcs, lands, plan, n_copies, after, *, name):
    n_s, n_l = len(srcs), len(lands)

    def body(*refs):
        src_refs = refs[0:n_s]
        land_refs = refs[n_s:n_s + n_l]
        send_sems, recv_sems = refs[n_s + n_l + 1], refs[n_s + n_l + 2]
        token = refs[-1]
        for cp in plan(src_refs, land_refs, send_sems, recv_sems):
            cp.start()
        token[...] = jnp.zeros_like(token)

    thru = [pltpu.HBM(a.shape, a.dtype) for a in list(srcs) + list(lands)]
    res = pl.pallas_call(
        body, name=name,
        out_shape=(pltpu.SemaphoreType.DMA((n_copies,)), pltpu.SemaphoreType.DMA((n_copies,)), *thru,
                   jax.ShapeDtypeStruct((SUB, LANES), F32)),
        in_specs=[HBM] * (n_s + n_l) + [ANY],
        out_specs=(SEM, SEM, *([HBM] * (n_s + n_l)), pl.BlockSpec(memory_space=pltpu.VMEM)),
        input_output_aliases={i: 2 + i for i in range(n_s + n_l)},
        compiler_params=pltpu.CompilerParams(has_side_effects=EFFECT),
    )(*[_in_hbm(a) for a in srcs], *[_in_hbm(a) for a in lands], _in_hbm(after))
    return res[0], res[1], list(res[2:2 + n_s]), list(res[2 + n_s:2 + n_s + n_l]), res[-1]


def _split_wait(send_sems, recv_sems, srcs, lands, after, plan, *, name):
    n_s, n_l = len(srcs), len(lands)

    def body(*refs):
        src_refs = refs[0:n_s]
        land_refs = refs[n_s:n_s + n_l]
        send, recv = refs[n_s + n_l], refs[n_s + n_l + 1]
        for cp in plan(src_refs, land_refs, send, recv):
            cp.wait_send()
            cp.wait_recv()

    res = pl.pallas_call(
        body, name=name,
        out_shape=tuple(pltpu.HBM(a.shape, a.dtype) for a in list(srcs) + list(lands)),
        in_specs=[HBM] * (n_s + n_l) + [SEM, SEM, ANY],
        out_specs=tuple([HBM] * (n_s + n_l)),
        input_output_aliases={i: i for i in range(n_s + n_l)},
        compiler_params=pltpu.CompilerParams(has_side_effects=EFFECT),
    )(*srcs, *lands, send_sems, recv_sems, _in_hbm(after))
    return list(res[:n_s]), list(res[n_s:])


def _remote(src, dst, send_sems, recv_sems, k, to):
    return pltpu.make_async_remote_copy(src_ref=src, dst_ref=dst, send_sem=send_sems.at[k], recv_sem=recv_sems.at[k],
                                        device_id=to, device_id_type=MESH)


def _gather_plan_first(src_refs, land_refs, send_sems, recv_sems):
    x, y, c = _place()
    me = 4 * x + 2 * y + c
    peers = [(x, y, 1 - c), (1 - x, y, c), (x, 1 - y, c), (1 - x, 1 - y, c)]
    return [_remote(land.at[me], land.at[me], send_sems, recv_sems, 4 * a + k, to)
            for a, land in enumerate(land_refs) for k, to in enumerate(peers)]


def _gather_plan_second(src_refs, land_refs, send_sems, recv_sems):
    x, y, c = _place()
    chips = [(1 - x, y), (x, 1 - y), (1 - x, 1 - y)]
    out = []
    for a, land in enumerate(land_refs):
        for j, (px, py) in enumerate(chips):
            slot = land.at[4 * px + 2 * py + c]
            out.append(_remote(slot, slot, send_sems, recv_sems, 3 * a + j, (x, y, 1 - c)))
    return out


def _siblings_plan(src_refs, land_refs, send_sems, recv_sems):
    x, y, c = _place()
    return [_remote(src.at[:, 1 - c], land, send_sems, recv_sems, a, (x, y, 1 - c))
            for a, (src, land) in enumerate(zip(src_refs, land_refs))]


def _chips_plan(src_refs, land_refs, send_sems, recv_sems):
    x, y, c = _place()
    my_chip = 2 * x + y
    chips = [(1 - x, y), (x, 1 - y), (1 - x, 1 - y)]
    return [_remote(src.at[2 * px + py], land.at[my_chip], send_sems, recv_sems, 3 * a + j, (px, py, c))
            for a, (src, land) in enumerate(zip(src_refs, land_refs)) for j, (px, py) in enumerate(chips)]


def _gather_landings(shards, me, *, name):
    blank = _unwritten([jax.ShapeDtypeStruct((N_DEV, *s.shape), s.dtype) for s in shards], name=name)
    return [lax.dynamic_update_index_in_dim(b, s, me, 0) for b, s in zip(blank, shards)]


def _adamw_math(g, w, m, v):
    m = ADAM_B1 * m + (1.0 - ADAM_B1) * g
    v = ADAM_B2 * v + (1.0 - ADAM_B2) * (g * g)
    m_hat = m / (1.0 - ADAM_B1 ** ADAM_STEP)
    v_hat = v / (1.0 - ADAM_B2 ** ADAM_STEP)
    delta = -ADAM_LR * (m_hat / (jnp.sqrt(v_hat) + ADAM_EPS) + ADAM_WD * w)
    return delta, m, v


def _adamw_sharded(parts, w, m, v, *, name, dep=None):
    n_layers, r, c = w.shape
    n_chip = parts[0].shape[0]
    tr = _row_tile(r, 384)
    n_i = r // tr

    def body(*refs):
        p_refs = refs[0:n_layers]
        w_ref, m_ref, v_ref, _, g_out, d_out, m_out, v_out = refs[n_layers:]
        layer = pl.program_id(0)
        for l in range(n_layers):
            @pl.when(layer == l)
            def _(l=l):
                g = p_refs[l][0].astype(F32)
                for q in range(1, n_chip):
                    g = g + p_refs[l][q].astype(F32)
                delta, m_new, v_new = _adamw_math(g, w_ref[...], m_ref[...], v_ref[...])
                g_out[...] = g
                d_out[...] = delta
                m_out[...] = m_new
                v_out[...] = v_new

    def part_map(l):
        return lambda layer, i: (0, jnp.where(layer == l, i, jnp.where(layer < l, 0, n_i - 1)), 0)

    blk = pl.BlockSpec((None, tr, c), lambda layer, i: (layer, i, 0))
    return pl.pallas_call(
        body, grid=(n_layers, n_i),
        in_specs=[pl.BlockSpec((n_chip, tr, c), part_map(l)) for l in range(n_layers)] + [blk, blk, blk, ANY],
        out_specs=[blk] * 4, out_shape=[jax.ShapeDtypeStruct((n_layers, r, c), F32)] * 4,
        compiler_params=_params("arbitrary", "arbitrary"), name=name,
    )(*parts, w, m, v, w if dep is None else dep)


def _fold_partials(cols, *, name):
    widths = [c.shape[1] for c in cols]

    def body(*refs):
        o_ref = refs[-1]
        pos = 0
        for ref, width in zip(refs[:-1], widths):
            o_ref[:, pos:pos + width] = jnp.sum(ref[...], axis=0, keepdims=True)
            pos += width

    return pl.pallas_call(body, out_shape=jax.ShapeDtypeStruct((1, sum(widths)), F32), name=name)(*cols)


def _adamw_replicated(parts, names, w, m, v, n_loss, *, name):
    n_dev = parts.shape[0]
    n_layers = w[names[0]].shape[0]
    every = list(names) + ["final_norm_g"]
    n_p = len(every)

    def body(*refs):
        p_ref = refs[0]
        w_refs = dict(zip(every, refs[1:1 + n_p]))
        m_refs = dict(zip(every, refs[1 + n_p:1 + 2 * n_p]))
        v_refs = dict(zip(every, refs[1 + 2 * n_p:1 + 3 * n_p]))
        l_out = refs[1 + 3 * n_p]
        outs = refs[2 + 3 * n_p:]
        o_refs = {n: outs[4 * q:4 * q + 4] for q, n in enumerate(every)}
        acc = p_ref[0]
        for q in range(1, n_dev):
            acc = acc + p_ref[q]
        tot = jnp.sum(acc, axis=0, keepdims=True)
        pos = 0
        where = [(n, l) for l in range(n_layers) for n in names] + [("final_norm_g", 0)]
        for n, l in where:
            width = w_refs[n].shape[1]
            g = tot[:, pos:pos + width]
            pos += width
            row = pl.ds(l, 1)
            delta, m_new, v_new = _adamw_math(g, w_refs[n][row, :], m_refs[n][row, :], v_refs[n][row, :])
            for o, val in zip(o_refs[n], (g, delta, m_new, v_new)):
                o[row, :] = val
        l_out[...] = (0.5 / n_loss) * jnp.sum(tot[:, pos:pos + n_loss], axis=-1, keepdims=True)

    shapes = [jax.ShapeDtypeStruct((1, 1), F32)]
    for n in every:
        shapes += [jax.ShapeDtypeStruct(w[n].shape, F32)] * 4
    res = pl.pallas_call(
        body, out_shape=shapes,
        compiler_params=pltpu.CompilerParams(vmem_limit_bytes=VMEM_LIMIT), name=name,
    )(parts, *[w[n] for n in every], *[m[n] for n in every], *[v[n] for n in every])
    return res[0], {n: res[1 + 4 * q:5 + 4 * q] for q, n in enumerate(every)}


BIG = ("w_in", "w_out", "w_up", "w_down")
COL_SHARDED = ("w_in", "w_up")
CONV = ("conv_a_w", "conv_b_w", "conv_f_w")
REPLICATED = ("mix_norm_g", "b_in", "conv_a_b", "ln_a_g", "ln_a_b", "ffn_norm_g")
KINDS = ("grad", "delta", "m", "v")
FFN_PART = ("w_up", "w_down")
MIX_PART = ("w_in", "w_out")


def _weights_from_gathered(g):
    n_dev, r, c = g.shape
    return g.reshape(n_dev * r, c)


def _slabs_from_full(grad):
    return grad.reshape(N_DEV, grad.shape[0] // N_DEV, grad.shape[1])


def _unwritten(like, *, name):
    return pl.pallas_call(lambda *refs: None, out_specs=[ANY] * len(like), out_shape=list(like), name=name)()


def kernel(x, mix_norm_g, w_in, b_in, conv_a_w, conv_a_b, ln_a_g, ln_a_b, conv_b_w, w_out, ffn_norm_g, w_up, conv_f_w, w_down, final_norm_g, loss_target, m_mix_norm_g, m_w_in, m_b_in, m_conv_a_w, m_conv_a_b, m_ln_a_g, m_ln_a_b, m_conv_b_w, m_w_out, m_ffn_norm_g, m_w_up, m_conv_f_w, m_w_down, m_final_norm_g, v_mix_norm_g, v_w_in, v_b_in, v_conv_a_w, v_conv_a_b, v_ln_a_g, v_ln_a_b, v_conv_b_w, v_w_out, v_ffn_norm_g, v_w_up, v_conv_f_w, v_w_down, v_final_norm_g):
    w = dict(mix_norm_g=mix_norm_g, w_in=w_in, b_in=b_in, conv_a_w=conv_a_w, conv_a_b=conv_a_b, ln_a_g=ln_a_g,
             ln_a_b=ln_a_b, conv_b_w=conv_b_w, w_out=w_out, ffn_norm_g=ffn_norm_g, w_up=w_up, conv_f_w=conv_f_w,
             w_down=w_down, final_norm_g=final_norm_g)
    m = dict(mix_norm_g=m_mix_norm_g, w_in=m_w_in, b_in=m_b_in, conv_a_w=m_conv_a_w, conv_a_b=m_conv_a_b,
             ln_a_g=m_ln_a_g, ln_a_b=m_ln_a_b, conv_b_w=m_conv_b_w, w_out=m_w_out, ffn_norm_g=m_ffn_norm_g,
             w_up=m_w_up, conv_f_w=m_conv_f_w, w_down=m_w_down, final_norm_g=m_final_norm_g)
    v = dict(mix_norm_g=v_mix_norm_g, w_in=v_w_in, b_in=v_b_in, conv_a_w=v_conv_a_w, conv_a_b=v_conv_a_b,
             ln_a_g=v_ln_a_g, ln_a_b=v_ln_a_b, conv_b_w=v_conv_b_w, w_out=v_w_out, ffn_norm_g=v_ffn_norm_g,
             w_up=v_w_up, conv_f_w=v_conv_f_w, w_down=v_w_down, final_norm_g=v_final_norm_g)
    order = list(w)
    n_layers = w_in.shape[0]
    xs = x[0]
    target = loss_target[0]
    flip = lambda a: jnp.transpose(a, (0, 2, 1))
    wt, mt, vt = ({n: flip(d[n]) if n in COL_SHARDED else d[n] for n in BIG} for d in (w, m, v))
    px, py, pc = _place()
    where = jnp.stack([pc, 2 * px + py]).astype(jnp.int32)
    me = 4 * px + 2 * py + pc

    assert BIG == MIX_PART + FFN_PART
    key = lambda n: n + "_t" if n in COL_SHARDED else n
    shard = lambda n, l: wt[n][l].astype(BF16)

    def gather_lands(names, l, tag, behind=None, more=()):
        if behind is None:
            shards = [shard(n, l) for n in names]
        else:
            shards = [s.astype(BF16) for s in lax.optimization_barrier(([wt[n][l] for n in names], behind))[0]]
        return _gather_landings(shards + list(more), me, name=f"gather_landing_{tag}")

    def gather_go(lands, after, tag):
        return _split_start([], lands, _gather_plan_first, 4 * len(lands), after, name=f"gather_first_start_{tag}")

    def gather_mid(first, after, tag):
        return _split_wait(first[0], first[1], first[2], first[3], after, _gather_plan_first,
                           name=f"gather_first_wait_{tag}")[1]

    def forward_start(lands, after, tag):
        return _split_start([], lands, _gather_plan_second, 3 * len(lands), after, name=f"gather_second_start_{tag}")

    def forward_finish(second, after, tag):
        return _split_wait(second[0], second[1], [], second[3], after, _gather_plan_second,
                           name=f"gather_second_wait_{tag}")[1]

    conv_rows = [w[n].reshape(-1, w[n].shape[2]) for n in CONV]
    mix_first = gather_go(gather_lands(MIX_PART, 0, "0_mix", more=conv_rows), xs, "0_mix")
    ffn_first = gather_go(gather_lands(FFN_PART, 0, "0_ffn", behind=mix_first[4]), mix_first[4], "0_ffn")
    later = {l: lax.optimization_barrier(gather_lands(BIG, l, str(l), behind=ffn_first[4]))
             for l in range(1, n_layers)}
    placed = later[1][-1] if later else ffn_first[4]
    mix_second = forward_start(gather_mid(mix_first, placed, "0_mix"), ffn_first[4], "0_mix")
    gathered = forward_finish(mix_second, mix_second[4], "0_mix")
    params = [{n: w[n][l] for n in REPLICATED} for l in range(n_layers)]
    for n, g in zip(CONV, gathered[len(MIX_PART):]):
        taps = w[n].shape[1]
        full = g.reshape(N_DEV, n_layers, taps, -1).transpose(1, 2, 0, 3).reshape(n_layers, taps, -1)
        for l in range(n_layers):
            params[l][n] = full[l]
    for n, g in zip(MIX_PART, gathered):
        params[0][key(n)] = _weights_from_gathered(g)
    pending = {}

    h = xs
    saved = []
    for l in range(n_layers):
        nxt = l + 1 if l + 1 < n_layers else None

        def before_up(x1, l=l, nxt=nxt):
            if l == 0:
                second = forward_start(gather_mid(ffn_first, x1, "0_ffn"), x1, "0_ffn")
                after = second[4]
            else:
                second = pending[l]["ffn"]
                after = x1
            if nxt is not None:
                pending[nxt] = dict(first=gather_go(later[nxt], after, str(nxt)))
                after = pending[nxt]["first"][4]
            for n, g in zip(FFN_PART, forward_finish(second, after, f"{l}_ffn")):
                params[l][key(n)] = _weights_from_gathered(g)

        h, keep = _layer_fwd(h, params[l], str(l), dep=ffn_first[4] if l == 0 else None, before_up=before_up)
        saved.append(keep)
        if nxt is not None:
            arrived = gather_mid(pending[nxt]["first"], h, str(nxt))
            mix_second = forward_start(arrived[:len(MIX_PART)], h, f"{nxt}_mix")
            pending[nxt]["ffn"] = forward_start(arrived[len(MIX_PART):], mix_second[4], f"{nxt}_ffn")
            for n, g in zip(MIX_PART, forward_finish(mix_second, pending[nxt]["ffn"][4], f"{nxt}_mix")):
                params[nxt][key(n)] = _weights_from_gathered(g)

    def start_siblings(slabs, after, tag):
        mines = [s.reshape(N_CHIP, 2, *s.shape[1:]) for s in slabs]
        lands = _unwritten([jax.ShapeDtypeStruct((N_CHIP, *m.shape[2:]), m.dtype) for m in mines],
                           name=f"reduce_siblings_landing_{tag}")
        return _split_start(mines, lands, _siblings_plan, len(mines), after, name=f"reduce_siblings_start_{tag}")

    def start_chips(sib, after, tag):
        mines, theirs = _split_wait(sib[0], sib[1], sib[2], sib[3], after, _siblings_plan,
                                    name=f"reduce_siblings_wait_{tag}")
        pairs, lands = _pair_sum(mines, theirs, where, name=f"pair_sum_{tag}")
        return _split_start(pairs, lands, _chips_plan, 3 * len(pairs), after, name=f"reduce_chips_start_{tag}")

    def finish_reduce(fly, after, tag):
        return _split_wait(fly[0], fly[1], fly[2], fly[3], after, _chips_plan, name=f"reduce_chips_wait_{tag}")[1]

    loss_sq, dh, dh_b, dgf = _loss_bwd(h, _row(final_norm_g), target, name="loss")
    conv_g = {n: [None] * n_layers for n in CONV}
    rep_g = [None] * n_layers
    siblings = {}
    flights = {}
    token = None
    for l in reversed(range(n_layers)):
        def ffn_grads(g, after, l=l):
            siblings[l, "ffn"] = start_siblings([_slabs_from_full(g[n]) for n in FFN_PART], after, f"{l}_ffn")
            return siblings[l, "ffn"][4]

        def ffn_sent(after, l=l):
            flights[l, "ffn"] = start_chips(siblings[l, "ffn"], after, f"{l}_ffn")
            return flights[l, "ffn"][4]

        def mix_grads(g, conv, after, l=l):
            for n in CONV:
                conv_g[n][l] = conv[n]
            slabs = [_slabs_from_full(g[n]) for n in MIX_PART]
            if l == 0:
                for n in CONV:
                    full = jnp.stack(conv_g[n])
                    _, taps, c = full.shape
                    slabs.append(full.reshape(n_layers, taps, N_DEV, c // N_DEV).transpose(2, 0, 1, 3)
                                 .reshape(N_DEV, n_layers * taps, c // N_DEV))
            siblings[l, "mix"] = start_siblings(slabs, after, f"{l}_mix")
            return siblings[l, "mix"][4]

        def mix_sent(after, l=l):
            flights[l, "mix"] = start_chips(siblings[l, "mix"], after, f"{l}_mix")
            return flights[l, "mix"][4]

        dh, dh_b, rep_g[l], token = _layer_bwd(dh, dh_b, params[l], saved[l], str(l), ffn_grads, ffn_sent,
                                               mix_grads, mix_sent, dep=token)

    sums = {key: finish_reduce(fly, dh, f"{key[0]}_{key[1]}") for key, fly in flights.items() if key != (0, "mix")}
    out = {k: {} for k in KINDS}

    def adamw_big(names, part, dep):
        for q, n in enumerate(names):
            layer_parts = [sums[l, part][q] for l in range(n_layers)]
            res = _adamw_sharded(layer_parts, wt[n], mt[n], vt[n], name=f"adamw_{n}", dep=dep)
            for k, r in zip(KINDS, res):
                out[k][n] = flip(r) if n in COL_SHARDED else r

    adamw_big(FFN_PART, "ffn", token)

    rep_cols = [rep_g[l][n] for l in range(n_layers) for n in REPLICATED] + [dgf, loss_sq]
    rep_all = _all_gather([_fold_partials(rep_cols, name="fold_small")], name="gather_small")[0]
    with_final = lambda d: {**{n: d[n] for n in REPLICATED}, "final_norm_g": _row(d["final_norm_g"])}
    loss, rep_res = _adamw_replicated(rep_all, REPLICATED, with_final(w), with_final(m), with_final(v),
                                      loss_sq.shape[1], name="adamw_small")
    for n, res in rep_res.items():
        for k, r in zip(KINDS, res):
            out[k][n] = r.reshape(w[n].shape)

    last = finish_reduce(flights[0, "mix"], rep_res["b_in"][0], "0_mix")
    sums[0, "mix"] = last[:len(MIX_PART)]
    adamw_big(MIX_PART, "mix", None)
    for n, p in zip(CONV, last[len(MIX_PART):]):
        as_one = lambda a: a.reshape(1, *p.shape[1:])
        for k, r in zip(KINDS, _adamw_sharded([p], as_one(w[n]), as_one(m[n]), as_one(v[n]), name=f"adamw_{n}")):
            out[k][n] = r.reshape(w[n].shape)

    grad_x = dh.reshape(x.shape)
    return (loss.reshape(()), grad_x, *[out["grad"][n] for n in order], *[out["delta"][n] for n in order],
            *[out["m"][n] for n in order], *[out["v"][n] for n in order])
```

```python
import functools

import jax
import jax.numpy as jnp
from jax import lax
from jax.experimental import pallas as pl
from jax.experimental.pallas import tpu as pltpu

F32 = jnp.float32
BF16 = jnp.bfloat16

N_DEV = 8
N_CHIP = 4
D_CONF = 512
CONF_K = 31
SHORT_K = 3
EPS = 1e-6
HALO = 32
HALO3 = 8
HALO3_BLK = 16
LANES = 128
SUB = 8
VMEM_LIMIT = 56 * 1024 * 1024

ADAM_LR = 0.001
ADAM_B1 = 0.9
ADAM_B2 = 0.999
ADAM_EPS = 1e-08
ADAM_WD = 0.01
ADAM_STEP = 10

MESH = pl.DeviceIdType.MESH
ANY = pl.BlockSpec(memory_space=pl.ANY)


def _params(*sem):
    return pltpu.CompilerParams(dimension_semantics=sem, vmem_limit_bytes=VMEM_LIMIT)


def _resident(shape, index_map):
    return pl.BlockSpec(shape, index_map, pipeline_mode=pl.Buffered(1))


def _row_loop(n_rows, rb, fn, unroll=1):
    rb = min(rb, n_rows)

    def body(i, carry):
        fn(pl.ds(pl.multiple_of(i * rb, rb), rb))
        return carry
    lax.fori_loop(0, n_rows // rb, body, 0, unroll=unroll)


def _rows8(v):
    acc = v[0:SUB]
    for k in range(1, v.shape[0] // SUB):
        acc = acc + v[k * SUB:(k + 1) * SUB]
    return acc


def _sigmoid(z):
    return 0.5 * jnp.tanh(0.5 * z) + 0.5


def _dot(a, b):
    return jnp.dot(a, b, preferred_element_type=F32)


def _dot_nt(a, b):
    return lax.dot_general(a, b, (((1,), (1,)), ((), ())), preferred_element_type=F32)


def _dot_tn(a, b):
    return lax.dot_general(a, b, (((0,), (0,)), ((), ())), preferred_element_type=F32)


def _replicate_taps(w_ref, wrep, taps):
    for k in range(taps):
        wrep[pl.ds(k * SUB, SUB), :] = jnp.broadcast_to(w_ref[pl.ds(k, 1), :], (SUB, w_ref.shape[1]))


def _shift_copies(win, shf, lanes):
    span = win.shape[0] - SUB
    for r in range(1, SUB):
        for j0 in range(0, span, 64):
            n = min(64, span - j0)
            shf[r - 1, pl.ds(j0, n), lanes] = win[pl.ds(j0 + r, n), lanes]


def _rows_at(win, shf, off, rb, lanes):
    if shf is None or off % SUB == 0:
        return win[pl.ds(off, rb), lanes]
    return shf[off % SUB - 1, pl.ds(off - off % SUB, rb), lanes]


def _conv_taps(win, wrep, out, *, taps, n_rows, base, width, transposed=False, bias_ref=None, shf=None):
    rb = min(32 if taps > 8 else 64, n_rows)

    def lane_body(cb, carry):
        lanes = pl.ds(pl.multiple_of(cb * LANES, LANES), LANES)
        if shf is not None:
            _shift_copies(win, shf, lanes)
        for r0 in range(0, n_rows, rb):
            acc = None
            for k in range(taps):
                off = (taps - 1 - k) if transposed else (k - (taps - 1))
                wk = jnp.tile(wrep[pl.ds(k * SUB, SUB), lanes], (rb // SUB, 1))
                term = wk * _rows_at(win, shf, base + r0 + off, rb, lanes)
                acc = term if acc is None else acc + term
            if bias_ref is not None:
                acc = acc + bias_ref[:, lanes]
            out[pl.ds(r0, rb), lanes] = acc.astype(out.dtype)
        return carry

    lax.fori_loop(0, width // LANES, lane_body, 0)


def _conv_bwd_taps(win, wrep, x_cur, dx_out, dw_acc, *, taps, n_rows, width, shf=None):
    rb = min(32 if taps > 8 else 64, n_rows)

    def lane_body(cb, carry):
        lanes = pl.ds(pl.multiple_of(cb * LANES, LANES), LANES)
        if shf is not None:
            _shift_copies(win, shf, lanes)
        sums = [None] * taps
        for r0 in range(0, n_rows, rb):
            xv = x_cur[pl.ds(r0, rb), lanes].astype(F32)
            acc = None
            for k in range(taps):
                shifted = _rows_at(win, shf, r0 + taps - 1 - k, rb, lanes)
                term = jnp.tile(wrep[pl.ds(k * SUB, SUB), lanes], (rb // SUB, 1)) * shifted
                acc = term if acc is None else acc + term
                part = _rows8(xv * shifted)
                sums[k] = part if sums[k] is None else sums[k] + part
            dx_out[pl.ds(r0, rb), lanes] = acc.astype(dx_out.dtype)
        for k in range(taps):
            dw_acc[pl.ds(k * SUB, SUB), lanes] += sums[k]
        return carry

    lax.fori_loop(0, width // LANES, lane_body, 0)


def _fold8(acc_ref, taps):
    return jnp.concatenate(
        [jnp.sum(acc_ref[pl.ds(k * SUB, SUB), :], axis=0, keepdims=True) for k in range(taps)], axis=0)


def _seq_tile(s_len):
    return min(512, s_len)


def _mm_tile(s_len):
    return min(512, s_len)


def _ff_chunk(ff):
    best = LANES
    for c in range(LANES, 1408 + 1, LANES):
        if ff % c == 0:
            best = c
    return best


def _col_tile(n):
    for c in (512, 1408, 256, LANES):
        if n % c == 0:
            return c
    return n


def _rms_matmul(x, g, wt, b, *, name, dep=None):
    s_len, d = x.shape
    n = wt.shape[0]
    tm = _mm_tile(s_len)
    cn = _col_tile(n)
    has_bias = b is not None

    def body(*refs):
        x_ref, g_ref, w_ref = refs[0:3]
        b_ref = refs[3] if has_bias else None
        o_ref, h_ref = refs[-2:]

        def blk(rows):
            xv = x_ref[rows, :]
            r = lax.rsqrt(jnp.mean(xv * xv, axis=-1, keepdims=True) + EPS)
            h_ref[rows, :] = ((xv * r) * g_ref[...]).astype(BF16)

        rb = min(128, tm)
        for r0 in range(0, tm, rb):
            blk(pl.ds(r0, rb))
        for j in range(n // cn):
            acc = _dot_nt(h_ref[...], w_ref[j * cn:(j + 1) * cn, :])
            if has_bias:
                acc = acc + b_ref[:, j * cn:(j + 1) * cn]
            o_ref[:, j * cn:(j + 1) * cn] = acc.astype(BF16)

    in_specs = [pl.BlockSpec((tm, d), lambda i: (i, 0)), _resident((1, d), lambda i: (0, 0)),
                _resident((n, d), lambda i: (0, 0))]
    args = [x, g, wt]
    if has_bias:
        in_specs.append(_resident((1, n), lambda i: (0, 0)))
        args.append(b)
    in_specs.append(ANY)
    args.append(x if dep is None else dep)
    return pl.pallas_call(
        body, grid=(s_len // tm,), in_specs=in_specs,
        out_specs=[pl.BlockSpec((tm, n), lambda i: (i, 0)), pl.BlockSpec((tm, d), lambda i: (i, 0))],
        out_shape=[jax.ShapeDtypeStruct((s_len, n), BF16), jax.ShapeDtypeStruct((s_len, d), BF16)],
        compiler_params=_params("parallel"), name=name,
    )(*args)


def _mix_windows(u_ref, uh_ref, gw, pw, first, t):
    c = D_CONF
    uh = uh_ref[...].astype(F32)
    gw[0:HALO, :] = jnp.where(first, 0.0, uh[:, 0:c] * _sigmoid(uh[:, c:2 * c]))
    pw[0:HALO3, :] = jnp.where(first, 0.0, uh[HALO - HALO3:HALO, 3 * c:4 * c] * uh[HALO - HALO3:HALO, 4 * c:5 * c])

    def blk(rows):
        dst = pl.ds(pl.multiple_of(rows.start + HALO, SUB), rows.size)
        gw[dst, :] = u_ref[rows, 0:c].astype(F32) * _sigmoid(u_ref[rows, c:2 * c].astype(F32))
        dst3 = pl.ds(pl.multiple_of(rows.start + HALO3, SUB), rows.size)
        pw[dst3, :] = u_ref[rows, 3 * c:4 * c].astype(F32) * u_ref[rows, 4 * c:5 * c].astype(F32)
    _row_loop(t, 64, blk)


def _mix_fwd(u, x0, wa, ba, lg, lb, wb, w_out, *, name):
    s_len, d_in = u.shape
    d = x0.shape[1]
    c = D_CONF
    t = _seq_tile(s_len)
    per = t // HALO

    def body(u_ref, uh_ref, x0_ref, wa_ref, ba_ref, lg_ref, lb_ref, wb_ref, wo_ref, y_ref, x1_ref, ca, cb,
             gw, pw, wrep_a, wrep_b, shf):
        first = pl.program_id(0) == 0
        _mix_windows(u_ref, uh_ref, gw, pw, first, t)
        _replicate_taps(wa_ref, wrep_a, CONF_K)
        _replicate_taps(wb_ref, wrep_b, SHORT_K)
        _conv_taps(gw, wrep_a, ca, taps=CONF_K, n_rows=t, base=HALO, width=c, bias_ref=ba_ref, shf=shf)
        _conv_taps(pw, wrep_b, cb, taps=SHORT_K, n_rows=t, base=HALO3, width=c)

        def blk(rows):
            cv = ca[rows, :]
            mu = jnp.mean(cv, axis=-1, keepdims=True)
            xc = cv - mu
            var = jnp.mean(xc * xc, axis=-1, keepdims=True)
            ln = (xc * lax.rsqrt(var + EPS)) * lg_ref[...] + lb_ref[...]
            y_ref[rows, 0:c] = (ln * _sigmoid(ln)).astype(BF16)
            y_ref[rows, c:2 * c] = (u_ref[rows, 2 * c:3 * c].astype(F32) * cb[rows, :]).astype(BF16)
        half = t // 2
        rb = min(64, half)
        for lo in range(0, t, half):
            for r0 in range(lo, lo + half, rb):
                blk(pl.ds(r0, rb))
            x1_ref[lo:lo + half, :] = x0_ref[lo:lo + half, :] + _dot(y_ref[lo:lo + half, :], wo_ref[...])

    small = lambda r: _resident((r, c), lambda i: (0, 0))
    return pl.pallas_call(
        body, grid=(s_len // t,),
        in_specs=[pl.BlockSpec((t, d_in), lambda i: (i, 0)),
                  pl.BlockSpec((HALO, d_in), lambda i: (jnp.maximum(i * per - 1, 0), 0)),
                  pl.BlockSpec((t, d), lambda i: (i, 0)),
                  small(CONF_K), small(1), small(1), small(1), small(SHORT_K),
                  _resident((2 * c, d), lambda i: (0, 0))],
        out_specs=[pl.BlockSpec((t, 2 * c), lambda i: (i, 0)), pl.BlockSpec((t, d), lambda i: (i, 0)),
                   pl.BlockSpec((t, c), lambda i: (i, 0)), pl.BlockSpec((t, c), lambda i: (i, 0))],
        out_shape=[jax.ShapeDtypeStruct((s_len, 2 * c), BF16), jax.ShapeDtypeStruct((s_len, d), F32),
                   jax.ShapeDtypeStruct((s_len, c), F32), jax.ShapeDtypeStruct((s_len, c), F32)],
        scratch_shapes=[pltpu.VMEM((HALO + t, c), F32), pltpu.VMEM((HALO3 + t, c), F32),
                        pltpu.VMEM((CONF_K * SUB, c), F32), pltpu.VMEM((SHORT_K * SUB, c), F32),
                        pltpu.VMEM((SUB - 1, HALO + t, c), F32)],
        compiler_params=_params("arbitrary"), name=name,
    )(u, u, x0, wa, ba, lg, lb, wb, w_out)


def _ffn_fwd(uf, x1, wf, w_down, *, name):
    s_len, ff2 = uf.shape
    ff = ff2 // 2
    d = x1.shape[1]
    t = _seq_tile(s_len)
    fc = _ff_chunk(ff)
    nc = ff // fc
    per = t // HALO3_BLK
    half = t // 2
    rb = min(64, half)

    def body(ug_ref, ugh_ref, uv_ref, uvh_ref, x1_ref, wfg_ref, wfv_ref, wd_ref,
             act_ref, x2_ref, cg_ref, cv_ref, gwin, vwin, wrep_g, wrep_v):
        first = pl.program_id(0) == 0
        first_chunk = pl.program_id(1) == 0
        lo8 = HALO3_BLK - HALO3
        gwin[0:HALO3, :] = jnp.where(first, 0.0, ugh_ref[...].astype(F32)[lo8:HALO3_BLK])
        vwin[0:HALO3, :] = jnp.where(first, 0.0, uvh_ref[...].astype(F32)[lo8:HALO3_BLK])
        _replicate_taps(wfg_ref, wrep_g, SHORT_K)
        _replicate_taps(wfv_ref, wrep_v, SHORT_K)
        chunk_rows = pl.ds(pl.multiple_of(pl.program_id(1) * fc, fc), fc)

        def conv(win, wrep, r0, lanes):
            acc = None
            for k in range(SHORT_K):
                wk = jnp.tile(wrep[k * SUB:(k + 1) * SUB, lanes], (rb // SUB, 1))
                off = HALO3 + r0 + k - (SHORT_K - 1)
                term = wk * win[off:off + rb, lanes]
                acc = term if acc is None else acc + term
            return acc

        for lo in range(0, t, half):
            for r0 in range(lo, lo + half, rb):
                gwin[HALO3 + r0:HALO3 + r0 + rb, :] = ug_ref[r0:r0 + rb, :].astype(F32)
                vwin[HALO3 + r0:HALO3 + r0 + rb, :] = uv_ref[r0:r0 + rb, :].astype(F32)
            for cb in range(fc // LANES):
                lanes = slice(cb * LANES, (cb + 1) * LANES)
                for r0 in range(lo, lo + half, rb):
                    gv = conv(gwin, wrep_g, r0, lanes).astype(BF16)
                    vv = conv(vwin, wrep_v, r0, lanes).astype(BF16)
                    cg_ref[r0:r0 + rb, lanes] = gv
                    cv_ref[r0:r0 + rb, lanes] = vv
                    act_ref[r0:r0 + rb, lanes] = (gv * _sigmoid(gv)) * vv
            base = jnp.where(first_chunk, x1_ref[lo:lo + half, :], x2_ref[lo:lo + half, :])
            x2_ref[lo:lo + half, :] = base + _dot(act_ref[lo:lo + half, :], wd_ref[chunk_rows, :])

    halo_map = lambda off: (lambda i, j: (jnp.maximum(i * per - 1, 0), j + off))
    return pl.pallas_call(
        body, grid=(s_len // t, nc),
        in_specs=[pl.BlockSpec((t, fc), lambda i, j: (i, j)), pl.BlockSpec((HALO3_BLK, fc), halo_map(0)),
                  pl.BlockSpec((t, fc), lambda i, j: (i, j + nc)), pl.BlockSpec((HALO3_BLK, fc), halo_map(nc)),
                  pl.BlockSpec((t, d), lambda i, j: (i, 0)),
                  pl.BlockSpec((SHORT_K, fc), lambda i, j: (0, j)),
                  pl.BlockSpec((SHORT_K, fc), lambda i, j: (0, j + nc)),
                  _resident((ff, d), lambda i, j: (0, 0))],
        out_specs=[pl.BlockSpec((t, fc), lambda i, j: (i, j)), pl.BlockSpec((t, d), lambda i, j: (i, 0)),
                   pl.BlockSpec((t, fc), lambda i, j: (i, j)), pl.BlockSpec((t, fc), lambda i, j: (i, j))],
        out_shape=[jax.ShapeDtypeStruct((s_len, ff), BF16), jax.ShapeDtypeStruct((s_len, d), F32),
                   jax.ShapeDtypeStruct((s_len, ff), BF16), jax.ShapeDtypeStruct((s_len, ff), BF16)],
        scratch_shapes=[pltpu.VMEM((HALO3 + t, fc), F32), pltpu.VMEM((HALO3 + t, fc), F32),
                        pltpu.VMEM((SHORT_K * SUB, fc), F32), pltpu.VMEM((SHORT_K * SUB, fc), F32)],
        compiler_params=_params("parallel", "arbitrary"), name=name,
    )(uf, uf, uf, uf, x1, wf, wf, w_down)


def _loss_bwd(x, g, target, *, name):
    s_len, d = x.shape
    t = _seq_tile(s_len)

    def body(x_ref, g_ref, t_ref, l_ref, dx_ref, dxb_ref, dg_ref):
        @pl.when(pl.program_id(0) == 0)
        def _():
            l_ref[...] = jnp.zeros_like(l_ref)
            dg_ref[...] = jnp.zeros_like(dg_ref)

        def blk(rows):
            xv = x_ref[rows, :]
            r = lax.rsqrt(jnp.mean(xv * xv, axis=-1, keepdims=True) + EPS)
            xn = xv * r
            e = xn * g_ref[...] - t_ref[rows, :]
            l_ref[...] += _rows8(e * e)
            dy = e * (1.0 / d)
            dg_ref[...] += _rows8(dy * xn)
            dn = dy * g_ref[...]
            dx = r * (dn - xn * jnp.mean(dn * xn, axis=-1, keepdims=True))
            dx_ref[rows, :] = dx
            dxb_ref[rows, :] = dx.astype(BF16)
        _row_loop(t, 64, blk)

    row = pl.BlockSpec((t, d), lambda i: (i, 0))
    part = pl.BlockSpec((SUB, d), lambda i: (0, 0))
    return pl.pallas_call(
        body, grid=(s_len // t,),
        in_specs=[row, _resident((1, d), lambda i: (0, 0)), row],
        out_specs=[part, row, row, part],
        out_shape=[jax.ShapeDtypeStruct((SUB, d), F32), jax.ShapeDtypeStruct((s_len, d), F32),
                   jax.ShapeDtypeStruct((s_len, d), BF16), jax.ShapeDtypeStruct((SUB, d), F32)],
        compiler_params=_params("arbitrary"), name=name,
    )(x, g, target)


def _ffn_bwd(dx2, uf, cg, cv, wf, w_down, *, name, dep=None):
    s_len, ff2 = uf.shape
    ff = ff2 // 2
    d = dx2.shape[1]
    t = _seq_tile(s_len)
    n_t = s_len // t
    fc = _ff_chunk(ff)
    nc = ff // fc

    def body(dx_ref, ug_ref, uv_ref, cg_ref, cv_ref, wfg_ref, wfv_ref, wd_ref, dep_ref,
             duf_ref, dwg_ref, dwv_ref, dact, dgw, dvw, awg, awv, wrep_g, wrep_v):
        i = pl.program_id(1)

        @pl.when(i == 0)
        def _():
            dgw[t:t + HALO3, :] = jnp.zeros((HALO3, fc), F32)
            dvw[t:t + HALO3, :] = jnp.zeros((HALO3, fc), F32)
            awg[...] = jnp.zeros_like(awg)
            awv[...] = jnp.zeros_like(awv)

        _replicate_taps(wfg_ref, wrep_g, SHORT_K)
        _replicate_taps(wfv_ref, wrep_v, SHORT_K)

        def blk(rows):
            gv = cg_ref[rows, :]
            sg = _sigmoid(gv)
            da = dact[rows, :].astype(BF16)
            dgw[rows, :] = ((da * cv_ref[rows, :]) * (sg * (1.0 + gv * (1.0 - sg)))).astype(F32)
            dvw[rows, :] = (da * (gv * sg)).astype(F32)

        dact[...] = _dot_nt(dx_ref[...], wd_ref[...])
        _row_loop(t, 64, blk)

        _conv_bwd_taps(dgw, wrep_g, ug_ref, duf_ref.at[0], awg, taps=SHORT_K, n_rows=t, width=fc)
        _conv_bwd_taps(dvw, wrep_v, uv_ref, duf_ref.at[1], awv, taps=SHORT_K, n_rows=t, width=fc)
        dgw[t:t + HALO3, :] = dgw[0:HALO3, :]
        dvw[t:t + HALO3, :] = dvw[0:HALO3, :]

        @pl.when(i == n_t - 1)
        def _():
            dwg_ref[...] = _fold8(awg, SHORT_K)
            dwv_ref[...] = _fold8(awv, SHORT_K)

    rev = lambda i: n_t - 1 - i
    gate = pl.BlockSpec((t, fc), lambda j, i: (rev(i), j))
    value = pl.BlockSpec((t, fc), lambda j, i: (rev(i), j + nc))
    return pl.pallas_call(
        body, grid=(nc, n_t),
        in_specs=[pl.BlockSpec((t, d), lambda j, i: (rev(i), 0)), gate, value, gate, gate,
                  pl.BlockSpec((SHORT_K, fc), lambda j, i: (0, j)),
                  pl.BlockSpec((SHORT_K, fc), lambda j, i: (0, j + nc)),
                  pl.BlockSpec((fc, d), lambda j, i: (j, 0)), ANY],
        out_specs=[pl.BlockSpec((2, t, fc), lambda j, i: (0, rev(i), j)),
                   pl.BlockSpec((SHORT_K, fc), lambda j, i: (0, j)), pl.BlockSpec((SHORT_K, fc), lambda j, i: (0, j))],
        out_shape=[jax.ShapeDtypeStruct((2, s_len, ff), BF16),
                   jax.ShapeDtypeStruct((SHORT_K, ff), F32), jax.ShapeDtypeStruct((SHORT_K, ff), F32)],
        scratch_shapes=[pltpu.VMEM((t, fc), F32),
                        pltpu.VMEM((t + HALO3, fc), F32), pltpu.VMEM((t + HALO3, fc), F32),
                        pltpu.VMEM((SHORT_K * SUB, fc), F32), pltpu.VMEM((SHORT_K * SUB, fc), F32),
                        pltpu.VMEM((SHORT_K * SUB, fc), F32), pltpu.VMEM((SHORT_K * SUB, fc), F32)],
        compiler_params=_params("arbitrary", "arbitrary"), name=name,
    )(dx2, uf, uf, cg, cv, wf, wf, w_down, uf if dep is None else dep)


def _mix_bwd(dx1, u, ca, cb, wa, lg, lb, wb, w_out, *, name, dep=None):
    s_len, d_in = u.shape
    d = dx1.shape[1]
    c = D_CONF
    t = _seq_tile(s_len)
    n_t = s_len // t

    def body(dx_ref, u_ref, ca_ref, cb_ref, wa_ref, lg_ref, lb_ref, wb_ref, wo_ref, dep_ref,
             du_ref, dwa_ref, dwb_ref, dba_ref, dlg_ref, dlb_ref, dbin_ref,
             glu, prod, dyc, dcaw, dcbw, dglu, dp, awa, awb, wrep_a, wrep_b, shf):
        i = pl.program_id(0)
        _replicate_taps(wa_ref, wrep_a, CONF_K)
        _replicate_taps(wb_ref, wrep_b, SHORT_K)

        @pl.when(i == 0)
        def _():
            dcaw[t:t + HALO, :] = jnp.zeros((HALO, c), F32)
            dcbw[t:t + HALO3, :] = jnp.zeros((HALO3, c), F32)
            awa[...] = jnp.zeros_like(awa)
            awb[...] = jnp.zeros_like(awb)
            dba_ref[...] = jnp.zeros_like(dba_ref)
            dlg_ref[...] = jnp.zeros_like(dlg_ref)
            dlb_ref[...] = jnp.zeros_like(dlb_ref)
            dbin_ref[...] = jnp.zeros_like(dbin_ref)

        def blk1(rows):
            cv = ca_ref[rows, :]
            mu = jnp.mean(cv, axis=-1, keepdims=True)
            xc = cv - mu
            rstd = lax.rsqrt(jnp.mean(xc * xc, axis=-1, keepdims=True) + EPS)
            nrm = xc * rstd
            ln = nrm * lg_ref[...] + lb_ref[...]
            sg = _sigmoid(ln)
            dln = dyc[rows, 0:c] * (sg * (1.0 + ln * (1.0 - sg)))
            dlg_ref[...] += _rows8(dln * nrm)
            dlb_ref[...] += _rows8(dln)
            dn = dln * lg_ref[...]
            dca = rstd * (dn - jnp.mean(dn, axis=-1, keepdims=True)
                          - nrm * jnp.mean(dn * nrm, axis=-1, keepdims=True))
            dcaw[rows, :] = dca
            dba_ref[...] += _rows8(dca)
            ds = dyc[rows, c:2 * c]
            dgb = ds * cb_ref[rows, :]
            dcbw[rows, :] = ds * u_ref[rows, 2 * c:3 * c].astype(F32)
            du_ref[rows, 2 * c:3 * c] = dgb.astype(BF16)
            dbin_ref[:, 2 * c:3 * c] += _rows8(dgb)
            glu[rows, :] = u_ref[rows, 0:c].astype(F32) * _sigmoid(u_ref[rows, c:2 * c].astype(F32))
            prod[rows, :] = u_ref[rows, 3 * c:4 * c].astype(F32) * u_ref[rows, 4 * c:5 * c].astype(F32)
        half = t // 2
        rb = min(64, half)
        for lo in range(0, t, half):
            dyc[lo:lo + half, :] = _dot_nt(dx_ref[lo:lo + half, :], wo_ref[...])
            for r0 in range(lo, lo + half, rb):
                blk1(pl.ds(r0, rb))

        _conv_bwd_taps(dcaw, wrep_a, glu, dglu, awa, taps=CONF_K, n_rows=t, width=c, shf=shf)
        _conv_bwd_taps(dcbw, wrep_b, prod, dp, awb, taps=SHORT_K, n_rows=t, width=c)
        dcaw[t:t + HALO, :] = dcaw[0:HALO, :]
        dcbw[t:t + HALO3, :] = dcbw[0:HALO3, :]

        def blk2(rows):
            av = u_ref[rows, 0:c].astype(F32)
            sg = _sigmoid(u_ref[rows, c:2 * c].astype(F32))
            dg = dglu[rows, :]
            d_av = dg * sg
            d_ag = (dg * av) * (sg * (1.0 - sg))
            dpv = dp[rows, :]
            d_gc = dpv * u_ref[rows, 4 * c:5 * c].astype(F32)
            d_vs = dpv * u_ref[rows, 3 * c:4 * c].astype(F32)
            du_ref[rows, 0:c] = d_av.astype(BF16)
            du_ref[rows, c:2 * c] = d_ag.astype(BF16)
            du_ref[rows, 3 * c:4 * c] = d_gc.astype(BF16)
            du_ref[rows, 4 * c:5 * c] = d_vs.astype(BF16)
            dbin_ref[:, 0:c] += _rows8(d_av)
            dbin_ref[:, c:2 * c] += _rows8(d_ag)
            dbin_ref[:, 3 * c:4 * c] += _rows8(d_gc)
            dbin_ref[:, 4 * c:5 * c] += _rows8(d_vs)
        _row_loop(t, 64, blk2)

        @pl.when(i == n_t - 1)
        def _():
            dwa_ref[...] = _fold8(awa, CONF_K)
            dwb_ref[...] = _fold8(awb, SHORT_K)

    rev = lambda i: n_t - 1 - i
    small_in = lambda r: _resident((r, c), lambda i: (0, 0))
    small = lambda r: pl.BlockSpec((r, c), lambda i: (0, 0))
    return pl.pallas_call(
        body, grid=(n_t,),
        in_specs=[pl.BlockSpec((t, d), lambda i: (rev(i), 0)),
                  pl.BlockSpec((t, d_in), lambda i: (rev(i), 0)),
                  pl.BlockSpec((t, c), lambda i: (rev(i), 0)), pl.BlockSpec((t, c), lambda i: (rev(i), 0)),
                  small_in(CONF_K), small_in(1), small_in(1), small_in(SHORT_K),
                  _resident((2 * c, d), lambda i: (0, 0)), ANY],
        out_specs=[pl.BlockSpec((t, d_in), lambda i: (rev(i), 0)),
                   small(CONF_K), small(SHORT_K), small(SUB), small(SUB), small(SUB),
                   pl.BlockSpec((SUB, d_in), lambda i: (0, 0))],
        out_shape=[jax.ShapeDtypeStruct((s_len, d_in), BF16),
                   jax.ShapeDtypeStruct((CONF_K, c), F32), jax.ShapeDtypeStruct((SHORT_K, c), F32),
                   jax.ShapeDtypeStruct((SUB, c), F32), jax.ShapeDtypeStruct((SUB, c), F32),
                   jax.ShapeDtypeStruct((SUB, c), F32), jax.ShapeDtypeStruct((SUB, d_in), F32)],
        scratch_shapes=[pltpu.VMEM((t, c), F32), pltpu.VMEM((t, c), F32), pltpu.VMEM((t, 2 * c), F32),
                        pltpu.VMEM((t + HALO, c), F32), pltpu.VMEM((t + HALO3, c), F32),
                        pltpu.VMEM((t, c), F32), pltpu.VMEM((t, c), F32),
                        pltpu.VMEM((CONF_K * SUB, c), F32), pltpu.VMEM((SHORT_K * SUB, c), F32),
                        pltpu.VMEM((CONF_K * SUB, c), F32), pltpu.VMEM((SHORT_K * SUB, c), F32),
                        pltpu.VMEM((SUB - 1, t + HALO, c), F32)],
        compiler_params=_params("arbitrary"), name=name,
    )(dx1, u, ca, cb, wa, lg, lb, wb, w_out, u if dep is None else dep)


def _matmul_tn(a, b, *, name):
    n_p, s_len, k = a.shape
    n = b.shape[1]
    tk = _col_tile(k)
    per = k // tk
    if per > 2:
        def body(a_ref, b_ref, o_ref):
            o_ref[...] = _dot_tn(a_ref[...], b_ref[...]).astype(BF16)

        return pl.pallas_call(
            body, grid=(n_p, per),
            in_specs=[pl.BlockSpec((None, s_len, tk), lambda p, j: (p, 0, j)),
                      _resident((s_len, n), lambda p, j: (0, 0))],
            out_specs=pl.BlockSpec((tk, n), lambda p, j: (p * per + j, 0)),
            out_shape=jax.ShapeDtypeStruct((n_p * k, n), BF16),
            compiler_params=_params("parallel", "parallel"), name=name,
        )(a, b)

    half = s_len // 2

    def body_halves(a_ref, b_ref, o_ref, acc):
        @pl.when(pl.program_id(2) == 0)
        def _():
            acc[...] = _dot_tn(a_ref[...], b_ref[...])

        @pl.when(pl.program_id(2) == 1)
        def _():
            o_ref[...] = (acc[...] + _dot_tn(a_ref[...], b_ref[...])).astype(BF16)

    return pl.pallas_call(
        body_halves, grid=(n_p, per, 2),
        in_specs=[pl.BlockSpec((None, half, tk), lambda p, j, q: (p, q, j)),
                  pl.BlockSpec((half, n), lambda p, j, q: (q, 0))],
        out_specs=pl.BlockSpec((tk, n), lambda p, j, q: (p * per + j, 0)),
        out_shape=jax.ShapeDtypeStruct((n_p * k, n), BF16),
        scratch_shapes=[pltpu.VMEM((tk, n), F32)],
        compiler_params=_params("parallel", "parallel", "arbitrary"), name=name,
    )(a, b)


def _matmul_rmsbwd(dzs, wt, x, g, dx_in, *, name, dep=None):
    s_len, d = x.shape
    n_z, _, nj = dzs.shape
    t = _mm_tile(s_len)

    n_steps = s_len // t
    ring = 3

    def body(*refs):
        dz_hbm = refs[0]
        w_refs = refs[1:1 + n_z]
        x_ref, g_ref, dxi_ref, _, dx_ref, dxb_ref, dg_ref, dh, dz_buf, dz_sems = refs[1 + n_z:]
        i = pl.program_id(0)

        def dz_copy(step, q):
            slot = step % ring
            rows = pl.ds(pl.multiple_of(step * t, t), t)
            return pltpu.make_async_copy(dz_hbm.at[q, rows], dz_buf.at[slot, q], dz_sems.at[slot, q])

        @pl.when(i == 0)
        def _():
            dg_ref[...] = jnp.zeros_like(dg_ref)
            for step in range(min(ring - 1, n_steps)):
                for q in range(n_z):
                    dz_copy(step, q).start()

        @pl.when(i + (ring - 1) < n_steps)
        def _():
            for q in range(n_z):
                dz_copy(i + (ring - 1), q).start()

        for q in range(n_z):
            dz_copy(i, q).wait()
        dz_refs = [dz_buf.at[i % ring, q] for q in range(n_z)]

        def blk(rows):
            xv = x_ref[rows, :]
            r = lax.rsqrt(jnp.mean(xv * xv, axis=-1, keepdims=True) + EPS)
            xn = xv * r
            dhv = dh[rows, :]
            dg_ref[...] += _rows8(dhv * xn)
            dn = dhv * g_ref[...]
            dx = dxi_ref[rows, :] + r * (dn - xn * jnp.mean(dn * xn, axis=-1, keepdims=True))
            dx_ref[rows, :] = dx
            dxb_ref[rows, :] = dx.astype(BF16)

        half = t // 2
        rb = min(128, half)
        for lo in range(0, t, half):
            acc = _dot(dz_refs[0][lo:lo + half, :], w_refs[0][...])
            for q in range(1, n_z):
                acc = acc + _dot(dz_refs[q][lo:lo + half, :], w_refs[q][...])
            dh[lo:lo + half, :] = acc
            for r0 in range(lo, lo + half, rb):
                blk(pl.ds(r0, rb))

    row = pl.BlockSpec((t, d), lambda i: (i, 0))
    in_specs = [ANY] + [_resident((nj, d), functools.partial(lambda q, i: (q, 0), q)) for q in range(n_z)]
    in_specs += [row, _resident((1, d), lambda i: (0, 0)), row, ANY]
    return pl.pallas_call(
        body, grid=(s_len // t,), in_specs=in_specs,
        out_specs=[row, row, pl.BlockSpec((SUB, d), lambda i: (0, 0))],
        out_shape=[jax.ShapeDtypeStruct((s_len, d), F32), jax.ShapeDtypeStruct((s_len, d), BF16),
                   jax.ShapeDtypeStruct((SUB, d), F32)],
        scratch_shapes=[pltpu.VMEM((t, d), F32), pltpu.VMEM((ring, n_z, t, nj), BF16),
                        pltpu.SemaphoreType.DMA((ring, n_z))],
        compiler_params=_params("arbitrary"), name=name,
    )(dzs, *([wt] * n_z), x, g, dx_in, x if dep is None else dep)


def _row(v):
    return v.reshape(1, -1)


def _layer_fwd(x0, p, tag, dep=None, before_up=None):
    u, h1 = _rms_matmul(x0, _row(p["mix_norm_g"]), p["w_in_t"], _row(p["b_in"]), name=f"in_proj_{tag}", dep=dep)
    ycat, x1, ca, cb = _mix_fwd(u, x0, p["conv_a_w"], _row(p["conv_a_b"]), _row(p["ln_a_g"]), _row(p["ln_a_b"]),
                            p["conv_b_w"], p["w_out"], name=f"mix_fwd_{tag}")
    if before_up is not None:
        before_up(x1)
    uf, h2 = _rms_matmul(x1, _row(p["ffn_norm_g"]), p["w_up_t"], None, name=f"up_proj_{tag}")
    act, x2, cg, cv = _ffn_fwd(uf, x1, p["conv_f_w"], p["w_down"], name=f"ffn_fwd_{tag}")
    return x2, dict(x0=x0, h1=h1, u=u, ca=ca, cb=cb, ycat=ycat, x1=x1, h2=h2, uf=uf, cg=cg, cv=cv, act=act)


def _layer_bwd(dx2, dx2_b, p, saved, tag, ffn_grads, ffn_sent, mix_grads, mix_sent, dep=None):
    d_uf, dwf_g, dwf_v = _ffn_bwd(dx2_b, saved["uf"], saved["cg"], saved["cv"], p["conv_f_w"], p["w_down"],
                                  name=f"ffn_bwd_{tag}", dep=dep)
    g_down = _matmul_tn(saved["act"][None], dx2_b, name=f"dw_down_{tag}")
    g_up = _matmul_tn(d_uf, saved["h2"], name=f"dw_up_{tag}")
    dep_ffn = ffn_grads(dict(w_up=g_up, w_down=g_down), dx2_b)
    dx1, dx1_b, dg2 = _matmul_rmsbwd(d_uf, p["w_up_t"], saved["x1"], _row(p["ffn_norm_g"]), dx2,
                                     name=f"dh_ffn_{tag}", dep=dep_ffn)
    du, dwa, dwb, dba, dlg, dlb, dbin = _mix_bwd(
        dx1_b, saved["u"], saved["ca"], saved["cb"], p["conv_a_w"], _row(p["ln_a_g"]), _row(p["ln_a_b"]),
        p["conv_b_w"], p["w_out"], name=f"mix_bwd_{tag}", dep=ffn_sent(dx1_b))
    g_out = _matmul_tn(saved["ycat"][None], dx1_b, name=f"dw_out_{tag}")
    g_in = _matmul_tn(du[None], saved["h1"], name=f"dw_in_{tag}")
    conv = dict(conv_a_w=dwa, conv_b_w=dwb, conv_f_w=jnp.concatenate([dwf_g, dwf_v], axis=1))
    dep_mix = mix_grads(dict(w_in=g_in, w_out=g_out), conv, dx1_b)
    dx0, dx0_b, dg1 = _matmul_rmsbwd(du[None], p["w_in_t"], saved["x0"], _row(p["mix_norm_g"]), dx1,
                                     name=f"dh_mix_{tag}", dep=dep_mix)
    rep = dict(mix_norm_g=dg1, b_in=dbin, conv_a_b=dba, ln_a_g=dlg, ln_a_b=dlb, ffn_norm_g=dg2)
    return dx0, dx0_b, rep, mix_sent(dx0_b)


def _place():
    return lax.axis_index("x"), lax.axis_index("y"), lax.axis_index("c")


def _all_gather(arrs, *, name):
    n_a = len(arrs)

    def body(*refs):
        ins = refs[0:n_a]
        outs = refs[n_a:2 * n_a]
        send_sems, recv_sems, local_sems = refs[2 * n_a:]
        x, y, c = _place()
        sibling = (x, y, 1 - c)
        chips = [(1 - x, y), (x, 1 - y), (1 - x, 1 - y)]

        def slot(a, px, py, pc):
            return outs[a].at[4 * px + 2 * py + pc]

        def copy(a, k, block, to, src=None):
            return pltpu.make_async_remote_copy(
                src_ref=slot(a, *block) if src is None else src, dst_ref=slot(a, *block),
                send_sem=send_sems.at[a, k], recv_sem=recv_sems.at[a, k],
                device_id=to, device_id_type=MESH)

        me = (x, y, c)
        mine = [pltpu.make_async_copy(ins[a], slot(a, *me), local_sems.at[a]) for a in range(n_a)]
        for cp in mine:
            cp.start()
        started = []
        for a in range(n_a):
            first = [copy(a, 0, me, sibling, src=ins[a])]
            first += [copy(a, 1 + j, me, (*chip, c), src=ins[a]) for j, chip in enumerate(chips)]
            for cp in first:
                cp.start()
            started += first
        for a in range(n_a):
            for j, chip in enumerate(chips):
                copy(a, 1 + j, (*chip, c), me).wait_recv()
                passed = copy(a, 4 + j, (*chip, c), sibling)
                passed.start()
                started.append(passed)
        for a in range(n_a):
            copy(a, 0, sibling, me).wait_recv()
            for j, chip in enumerate(chips):
                copy(a, 4 + j, (*chip, 1 - c), me).wait_recv()
        for cp in started:
            cp.wait_send()
        for cp in mine:
            cp.wait()

    return pl.pallas_call(
        body, in_specs=[ANY] * n_a, out_specs=[ANY] * n_a,
        out_shape=[jax.ShapeDtypeStruct((N_DEV, *a.shape), a.dtype) for a in arrs],
        scratch_shapes=[pltpu.SemaphoreType.DMA((n_a, 7)), pltpu.SemaphoreType.DMA((n_a, 7)),
                        pltpu.SemaphoreType.DMA((n_a,))],
        name=name,
    )(*arrs)


def _row_tile(r, cap):
    for tr in range(min(cap, r) // 16 * 16, 0, -16):
        if r % tr == 0:
            return tr
    return r


def _pair_sum(mines, theirs, where, *, name):
    n_a = len(mines)
    n_chip = mines[0].shape[0]

    def body(where_ref, *refs):
        a_refs = refs[0:n_a]
        b_refs = refs[n_a:2 * n_a]
        p_refs = refs[2 * n_a:3 * n_a]
        l_refs = refs[3 * n_a:4 * n_a]
        q = pl.program_id(0)
        for a in range(n_a):
            p_refs[a][...] = (a_refs[a][...].astype(F32) + b_refs[a][...].astype(F32)).astype(p_refs[a].dtype)

        @pl.when(q == where_ref[1])
        def _():
            for a in range(n_a):
                l_refs[a][...] = p_refs[a][...]

    in_specs, out_p, out_l, shapes = [], [], [], []
    for m in mines:
        _, _, r, c = m.shape
        in_specs.append(pl.BlockSpec((None, None, r, c), lambda q, where_ref: (q, where_ref[0], 0, 0)))
    for m in mines:
        _, _, r, c = m.shape
        in_specs.append(pl.BlockSpec((None, r, c), lambda q, where_ref: (q, 0, 0)))
        out_p.append(pl.BlockSpec((None, r, c), lambda q, where_ref: (q, 0, 0)))
        out_l.append(pl.BlockSpec((None, r, c), lambda q, where_ref: (where_ref[1], 0, 0)))
        shapes.append(jax.ShapeDtypeStruct((n_chip, r, c), m.dtype))
    res = pl.pallas_call(
        body,
        grid_spec=pltpu.PrefetchScalarGridSpec(num_scalar_prefetch=1, grid=(n_chip,), in_specs=in_specs,
                                               out_specs=out_p + out_l),
        out_shape=shapes + shapes,
        compiler_params=_params("arbitrary"), name=name,
    )(where, *mines, *theirs)
    return list(res[:n_a]), list(res[n_a:])


HBM = pl.BlockSpec(memory_space=pltpu.HBM)
SEM = pl.BlockSpec(memory_space=pltpu.SEMAPHORE)
EFFECT = pltpu.SideEffectType.DATAFLOW_SIDE_EFFECTING


def _in_hbm(a):
    return pltpu.with_memory_space_constraint(a, pltpu.HBM)


def _split_start(srcs, lands, plan, n_copies, after, *, name):
    n_s, n_l = len(srcs), len(lands)

    def body(*refs):
        src_refs = refs[0:n_s]
        land_refs = refs[n_s:n_s + n_l]
        send_sems, recv_sems = refs[n_s + n_l + 1], refs[n_s + n_l + 2]
        token = refs[-1]
        for cp in plan(src_refs, land_refs, send_sems, recv_sems):
            cp.start()
        token[...] = jnp.zeros_like(token)

    thru = [pltpu.HBM(a.shape, a.dtype) for a in list(srcs) + list(lands)]
    res = pl.pallas_call(
        body, name=name,
        out_shape=(pltpu.SemaphoreType.DMA((n_copies,)), pltpu.SemaphoreType.DMA((n_copies,)), *thru,
                   jax.ShapeDtypeStruct((SUB, LANES), F32)),
        in_specs=[HBM] * (n_s + n_l) + [ANY],
        out_specs=(SEM, SEM, *([HBM] * (n_s + n_l)), pl.BlockSpec(memory_space=pltpu.VMEM)),
        input_output_aliases={i: 2 + i for i in range(n_s + n_l)},
        compiler_params=pltpu.CompilerParams(has_side_effects=EFFECT),
    )(*[_in_hbm(a) for a in srcs], *[_in_hbm(a) for a in lands], _in_hbm(after))
    return res[0], res[1], list(res[2:2 + n_s]), list(res[2 + n_s:2 + n_s + n_l]), res[-1]


def _split_wait(send_sems, recv_sems, srcs, lands, after, plan, *, name):
    n_s, n_l = len(srcs), len(lands)

    def body(*refs):
        src_refs = refs[0:n_s]
        land_refs = refs[n_s:n_s + n_l]
        send, recv = refs[n_s + n_l], refs[n_s + n_l + 1]
        for cp in plan(src_refs, land_refs, send, recv):
            cp.wait_send()
            cp.wait_recv()

    res = pl.pallas_call(
        body, name=name,
        out_shape=tuple(pltpu.HBM(a.shape, a.dtype) for a in list(srcs) + list(lands)),
        in_specs=[HBM] * (n_s + n_l) + [SEM, SEM, ANY],
        out_specs=tuple([HBM] * (n_s + n_l)),
        input_output_aliases={i: i for i in range(n_s + n_l)},
        compiler_params=pltpu.CompilerParams(has_side_effects=EFFECT),
    )(*srcs, *lands, send_sems, recv_sems, _in_hbm(after))
    return list(res[:n_s]), list(res[n_s:])


def _remote(src, dst, send_sems, recv_sems, k, to):
    return pltpu.make_async_remote_copy(src_ref=src, dst_ref=dst, send_sem=send_sems.at[k], recv_sem=recv_sems.at[k],
                                        device_id=to, device_id_type=MESH)


def _gather_plan_first(src_refs, land_refs, send_sems, recv_sems):
    x, y, c = _place()
    me = 4 * x + 2 * y + c
    peers = [(x, y, 1 - c), (1 - x, y, c), (x, 1 - y, c), (1 - x, 1 - y, c)]
    return [_remote(land.at[me], land.at[me], send_sems, recv_sems, 4 * a + k, to)
            for a, land in enumerate(land_refs) for k, to in enumerate(peers)]


def _gather_plan_second(src_refs, land_refs, send_sems, recv_sems):
    x, y, c = _place()
    chips = [(1 - x, y), (x, 1 - y), (1 - x, 1 - y)]
    out = []
    for a, land in enumerate(land_refs):
        for j, (px, py) in enumerate(chips):
            slot = land.at[4 * px + 2 * py + c]
            out.append(_remote(slot, slot, send_sems, recv_sems, 3 * a + j, (x, y, 1 - c)))
    return out


def _siblings_plan(src_refs, land_refs, send_sems, recv_sems):
    x, y, c = _place()
    return [_remote(src.at[:, 1 - c], land, send_sems, recv_sems, a, (x, y, 1 - c))
            for a, (src, land) in enumerate(zip(src_refs, land_refs))]


def _chips_plan(src_refs, land_refs, send_sems, recv_sems):
    x, y, c = _place()
    my_chip = 2 * x + y
    chips = [(1 - x, y), (x, 1 - y), (1 - x, 1 - y)]
    return [_remote(src.at[2 * px + py], land.at[my_chip], send_sems, recv_sems, 3 * a + j, (px, py, c))
            for a, (src, land) in enumerate(zip(src_refs, land_refs)) for j, (px, py) in enumerate(chips)]


def _gather_landings(shards, me, *, name):
    blank = _unwritten([jax.ShapeDtypeStruct((N_DEV, *s.shape), s.dtype) for s in shards], name=name)
    return [lax.dynamic_update_index_in_dim(b, s, me, 0) for b, s in zip(blank, shards)]


def _adamw_math(g, w, m, v):
    m = ADAM_B1 * m + (1.0 - ADAM_B1) * g
    v = ADAM_B2 * v + (1.0 - ADAM_B2) * (g * g)
    m_hat = m / (1.0 - ADAM_B1 ** ADAM_STEP)
    v_hat = v / (1.0 - ADAM_B2 ** ADAM_STEP)
    delta = -ADAM_LR * (m_hat / (jnp.sqrt(v_hat) + ADAM_EPS) + ADAM_WD * w)
    return delta, m, v


def _adamw_sharded(parts, w, m, v, *, name, dep=None):
    n_layers, r, c = w.shape
    n_chip = parts[0].shape[0]
    tr = _row_tile(r, 384)
    n_i = r // tr

    def body(*refs):
        p_refs = refs[0:n_layers]
        w_ref, m_ref, v_ref, _, g_out, d_out, m_out, v_out = refs[n_layers:]
        layer = pl.program_id(0)
        for l in range(n_layers):
            @pl.when(layer == l)
            def _(l=l):
                g = p_refs[l][0].astype(F32)
                for q in range(1, n_chip):
                    g = g + p_refs[l][q].astype(F32)
                delta, m_new, v_new = _adamw_math(g, w_ref[...], m_ref[...], v_ref[...])
                g_out[...] = g
                d_out[...] = delta
                m_out[...] = m_new
                v_out[...] = v_new

    def part_map(l):
        return lambda layer, i: (0, jnp.where(layer == l, i, jnp.where(layer < l, 0, n_i - 1)), 0)

    blk = pl.BlockSpec((None, tr, c), lambda layer, i: (layer, i, 0))
    return pl.pallas_call(
        body, grid=(n_layers, n_i),
        in_specs=[pl.BlockSpec((n_chip, tr, c), part_map(l)) for l in range(n_layers)] + [blk, blk, blk, ANY],
        out_specs=[blk] * 4, out_shape=[jax.ShapeDtypeStruct((n_layers, r, c), F32)] * 4,
        compiler_params=_params("arbitrary", "arbitrary"), name=name,
    )(*parts, w, m, v, w if dep is None else dep)


def _fold_partials(cols, *, name):
    widths = [c.shape[1] for c in cols]

    def body(*refs):
        o_ref = refs[-1]
        pos = 0
        for ref, width in zip(refs[:-1], widths):
            o_ref[:, pos:pos + width] = jnp.sum(ref[...], axis=0, keepdims=True)
            pos += width

    return pl.pallas_call(body, out_shape=jax.ShapeDtypeStruct((1, sum(widths)), F32), name=name)(*cols)


def _adamw_replicated(parts, names, w, m, v, n_loss, *, name):
    n_dev = parts.shape[0]
    n_layers = w[names[0]].shape[0]
    every = list(names) + ["final_norm_g"]
    n_p = len(every)

    def body(*refs):
        p_ref = refs[0]
        w_refs = dict(zip(every, refs[1:1 + n_p]))
        m_refs = dict(zip(every, refs[1 + n_p:1 + 2 * n_p]))
        v_refs = dict(zip(every, refs[1 + 2 * n_p:1 + 3 * n_p]))
        l_out = refs[1 + 3 * n_p]
        outs = refs[2 + 3 * n_p:]
        o_refs = {n: outs[4 * q:4 * q + 4] for q, n in enumerate(every)}
        acc = p_ref[0]
        for q in range(1, n_dev):
            acc = acc + p_ref[q]
        tot = jnp.sum(acc, axis=0, keepdims=True)
        pos = 0
        where = [(n, l) for l in range(n_layers) for n in names] + [("final_norm_g", 0)]
        for n, l in where:
            width = w_refs[n].shape[1]
            g = tot[:, pos:pos + width]
            pos += width
            row = pl.ds(l, 1)
            delta, m_new, v_new = _adamw_math(g, w_refs[n][row, :], m_refs[n][row, :], v_refs[n][row, :])
            for o, val in zip(o_refs[n], (g, delta, m_new, v_new)):
                o[row, :] = val
        l_out[...] = (0.5 / n_loss) * jnp.sum(tot[:, pos:pos + n_loss], axis=-1, keepdims=True)

    shapes = [jax.ShapeDtypeStruct((1, 1), F32)]
    for n in every:
        shapes += [jax.ShapeDtypeStruct(w[n].shape, F32)] * 4
    res = pl.pallas_call(
        body, out_shape=shapes,
        compiler_params=pltpu.CompilerParams(vmem_limit_bytes=VMEM_LIMIT), name=name,
    )(parts, *[w[n] for n in every], *[m[n] for n in every], *[v[n] for n in every])
    return res[0], {n: res[1 + 4 * q:5 + 4 * q] for q, n in enumerate(every)}


BIG = ("w_in", "w_out", "w_up", "w_down")
COL_SHARDED = ("w_in", "w_up")
CONV = ("conv_a_w", "conv_b_w", "conv_f_w")
REPLICATED = ("mix_norm_g", "b_in", "conv_a_b", "ln_a_g", "ln_a_b", "ffn_norm_g")
KINDS = ("grad", "delta", "m", "v")
FFN_PART = ("w_up", "w_down")
MIX_PART = ("w_in", "w_out")


def _weights_from_gathered(g):
    n_dev, r, c = g.shape
    return g.reshape(n_dev * r, c)


def _slabs_from_full(grad):
    return grad.reshape(N_DEV, grad.shape[0] // N_DEV, grad.shape[1])


def _unwritten(like, *, name):
    return pl.pallas_call(lambda *refs: None, out_specs=[ANY] * len(like), out_shape=list(like), name=name)()


def kernel(x, mix_norm_g, w_in, b_in, conv_a_w, conv_a_b, ln_a_g, ln_a_b, conv_b_w, w_out, ffn_norm_g, w_up, conv_f_w, w_down, final_norm_g, loss_target, m_mix_norm_g, m_w_in, m_b_in, m_conv_a_w, m_conv_a_b, m_ln_a_g, m_ln_a_b, m_conv_b_w, m_w_out, m_ffn_norm_g, m_w_up, m_conv_f_w, m_w_down, m_final_norm_g, v_mix_norm_g, v_w_in, v_b_in, v_conv_a_w, v_conv_a_b, v_ln_a_g, v_ln_a_b, v_conv_b_w, v_w_out, v_ffn_norm_g, v_w_up, v_conv_f_w, v_w_down, v_final_norm_g):
    w = dict(mix_norm_g=mix_norm_g, w_in=w_in, b_in=b_in, conv_a_w=conv_a_w, conv_a_b=conv_a_b, ln_a_g=ln_a_g,
             ln_a_b=ln_a_b, conv_b_w=conv_b_w, w_out=w_out, ffn_norm_g=ffn_norm_g, w_up=w_up, conv_f_w=conv_f_w,
             w_down=w_down, final_norm_g=final_norm_g)
    m = dict(mix_norm_g=m_mix_norm_g, w_in=m_w_in, b_in=m_b_in, conv_a_w=m_conv_a_w, conv_a_b=m_conv_a_b,
             ln_a_g=m_ln_a_g, ln_a_b=m_ln_a_b, conv_b_w=m_conv_b_w, w_out=m_w_out, ffn_norm_g=m_ffn_norm_g,
             w_up=m_w_up, conv_f_w=m_conv_f_w, w_down=m_w_down, final_norm_g=m_final_norm_g)
    v = dict(mix_norm_g=v_mix_norm_g, w_in=v_w_in, b_in=v_b_in, conv_a_w=v_conv_a_w, conv_a_b=v_conv_a_b,
             ln_a_g=v_ln_a_g, ln_a_b=v_ln_a_b, conv_b_w=v_conv_b_w, w_out=v_w_out, ffn_norm_g=v_ffn_norm_g,
             w_up=v_w_up, conv_f_w=v_conv_f_w, w_down=v_w_down, final_norm_g=v_final_norm_g)
    order = list(w)
    n_layers = w_in.shape[0]
    xs = x[0]
    target = loss_target[0]
    flip = lambda a: jnp.transpose(a, (0, 2, 1))
    wt, mt, vt = ({n: flip(d[n]) if n in COL_SHARDED else d[n] for n in BIG} for d in (w, m, v))
    px, py, pc = _place()
    where = jnp.stack([pc, 2 * px + py]).astype(jnp.int32)
    me = 4 * px + 2 * py + pc

    assert BIG == MIX_PART + FFN_PART
    key = lambda n: n + "_t" if n in COL_SHARDED else n
    shard = lambda n, l: wt[n][l].astype(BF16)

    def gather_lands(names, l, tag, behind=None, more=()):
        if behind is None:
            shards = [shard(n, l) for n in names]
        else:
            shards = [s.astype(BF16) for s in lax.optimization_barrier(([wt[n][l] for n in names], behind))[0]]
        return _gather_landings(shards + list(more), me, name=f"gather_landing_{tag}")

    def gather_go(lands, after, tag):
        return _split_start([], lands, _gather_plan_first, 4 * len(lands), after, name=f"gather_first_start_{tag}")

    def gather_mid(first, after, tag):
        return _split_wait(first[0], first[1], first[2], first[3], after, _gather_plan_first,
                           name=f"gather_first_wait_{tag}")[1]

    def forward_start(lands, after, tag):
        return _split_start([], lands, _gather_plan_second, 3 * len(lands), after, name=f"gather_second_start_{tag}")

    def forward_finish(second, after, tag):
        return _split_wait(second[0], second[1], [], second[3], after, _gather_plan_second,
                           name=f"gather_second_wait_{tag}")[1]

    conv_rows = [w[n].reshape(-1, w[n].shape[2]) for n in CONV]
    mix_first = gather_go(gather_lands(MIX_PART, 0, "0_mix", more=conv_rows), xs, "0_mix")
    ffn_first = gather_go(gather_lands(FFN_PART, 0, "0_ffn", behind=mix_first[4]), mix_first[4], "0_ffn")
    later = {l: lax.optimization_barrier(gather_lands(BIG, l, str(l), behind=ffn_first[4]))
             for l in range(1, n_layers)}
    placed = later[1][-1] if later else ffn_first[4]
    mix_second = forward_start(gather_mid(mix_first, placed, "0_mix"), ffn_first[4], "0_mix")
    gathered = forward_finish(mix_second, mix_second[4], "0_mix")
    params = [{n: w[n][l] for n in REPLICATED} for l in range(n_layers)]
    for n, g in zip(CONV, gathered[len(MIX_PART):]):
        taps = w[n].shape[1]
        full = g.reshape(N_DEV, n_layers, taps, -1).transpose(1, 2, 0, 3).reshape(n_layers, taps, -1)
        for l in range(n_layers):
            params[l][n] = full[l]
    for n, g in zip(MIX_PART, gathered):
        params[0][key(n)] = _weights_from_gathered(g)
    pending = {}

    h = xs
    saved = []
    for l in range(n_layers):
        nxt = l + 1 if l + 1 < n_layers else None

        def before_up(x1, l=l, nxt=nxt):
            if l == 0:
                second = forward_start(gather_mid(ffn_first, x1, "0_ffn"), x1, "0_ffn")
                after = second[4]
            else:
                second = pending[l]["ffn"]
                after = x1
            if nxt is not None:
                pending[nxt] = dict(first=gather_go(later[nxt], after, str(nxt)))
                after = pending[nxt]["first"][4]
            for n, g in zip(FFN_PART, forward_finish(second, after, f"{l}_ffn")):
                params[l][key(n)] = _weights_from_gathered(g)

        h, keep = _layer_fwd(h, params[l], str(l), dep=ffn_first[4] if l == 0 else None, before_up=before_up)
        saved.append(keep)
        if nxt is not None:
            arrived = gather_mid(pending[nxt]["first"], h, str(nxt))
            mix_second = forward_start(arrived[:len(MIX_PART)], h, f"{nxt}_mix")
            pending[nxt]["ffn"] = forward_start(arrived[len(MIX_PART):], mix_second[4], f"{nxt}_ffn")
            for n, g in zip(MIX_PART, forward_finish(mix_second, pending[nxt]["ffn"][4], f"{nxt}_mix")):
                params[nxt][key(n)] = _weights_from_gathered(g)

    def start_siblings(slabs, after, tag):
        mines = [s.reshape(N_CHIP, 2, *s.shape[1:]) for s in slabs]
        lands = _unwritten([jax.ShapeDtypeStruct((N_CHIP, *m.shape[2:]), m.dtype) for m in mines],
                           name=f"reduce_siblings_landing_{tag}")
        return _split_start(mines, lands, _siblings_plan, len(mines), after, name=f"reduce_siblings_start_{tag}")

    def start_chips(sib, after, tag):
        mines, theirs = _split_wait(sib[0], sib[1], sib[2], sib[3], after, _siblings_plan,
                                    name=f"reduce_siblings_wait_{tag}")
        pairs, lands = _pair_sum(mines, theirs, where, name=f"pair_sum_{tag}")
        return _split_start(pairs, lands, _chips_plan, 3 * len(pairs), after, name=f"reduce_chips_start_{tag}")

    def finish_reduce(fly, after, tag):
        return _split_wait(fly[0], fly[1], fly[2], fly[3], after, _chips_plan, name=f"reduce_chips_wait_{tag}")[1]

    loss_sq, dh, dh_b, dgf = _loss_bwd(h, _row(final_norm_g), target, name="loss")
    conv_g = {n: [None] * n_layers for n in CONV}
    rep_g = [None] * n_layers
    siblings = {}
    flights = {}
    token = None
    for l in reversed(range(n_layers)):
        def ffn_grads(g, after, l=l):
            siblings[l, "ffn"] = start_siblings([_slabs_from_full(g[n]) for n in FFN_PART], after, f"{l}_ffn")
            return siblings[l, "ffn"][4]

        def ffn_sent(after, l=l):
            flights[l, "ffn"] = start_chips(siblings[l, "ffn"], after, f"{l}_ffn")
            return flights[l, "ffn"][4]

        def mix_grads(g, conv, after, l=l):
            for n in CONV:
                conv_g[n][l] = conv[n]
            slabs = [_slabs_from_full(g[n]) for n in MIX_PART]
            if l == 0:
                for n in CONV:
                    full = jnp.stack(conv_g[n])
                    _, taps, c = full.shape
                    slabs.append(full.reshape(n_layers, taps, N_DEV, c // N_DEV).transpose(2, 0, 1, 3)
                                 .reshape(N_DEV, n_layers * taps, c // N_DEV))
            siblings[l, "mix"] = start_siblings(slabs, after, f"{l}_mix")
            return siblings[l, "mix"][4]

        def mix_sent(after, l=l):
            flights[l, "mix"] = start_chips(siblings[l, "mix"], after, f"{l}_mix")
            return flights[l, "mix"][4]

        dh, dh_b, rep_g[l], token = _layer_bwd(dh, dh_b, params[l], saved[l], str(l), ffn_grads, ffn_sent,
                                               mix_grads, mix_sent, dep=token)

    sums = {key: finish_reduce(fly, dh, f"{key[0]}_{key[1]}") for key, fly in flights.items() if key != (0, "mix")}
    out = {k: {} for k in KINDS}

    def adamw_big(names, part, dep):
        for q, n in enumerate(names):
            layer_parts = [sums[l, part][q] for l in range(n_layers)]
            res = _adamw_sharded(layer_parts, wt[n], mt[n], vt[n], name=f"adamw_{n}", dep=dep)
            for k, r in zip(KINDS, res):
                out[k][n] = flip(r) if n in COL_SHARDED else r

    adamw_big(FFN_PART, "ffn", token)

    rep_cols = [rep_g[l][n] for l in range(n_layers) for n in REPLICATED] + [dgf, loss_sq]
    rep_all = _all_gather([_fold_partials(rep_cols, name="fold_small")], name="gather_small")[0]
    with_final = lambda d: {**{n: d[n] for n in REPLICATED}, "final_norm_g": _row(d["final_norm_g"])}
    loss, rep_res = _adamw_replicated(rep_all, REPLICATED, with_final(w), with_final(m), with_final(v),
                                      loss_sq.shape[1], name="adamw_small")
    for n, res in rep_res.items():
        for k, r in zip(KINDS, res):
            out[k][n] = r.reshape(w[n].shape)

    last = finish_reduce(flights[0, "mix"], rep_res["b_in"][0], "0_mix")
    sums[0, "mix"] = last[:len(MIX_PART)]
    adamw_big(MIX_PART, "mix", None)
    for n, p in zip(CONV, last[len(MIX_PART):]):
        as_one = lambda a: a.reshape(1, *p.shape[1:])
        for k, r in zip(KINDS, _adamw_sharded([p], as_one(w[n]), as_one(m[n]), as_one(v[n]), name=f"adamw_{n}")):
            out[k][n] = r.reshape(w[n].shape)

    grad_x = dh.reshape(x.shape)
    return (loss.reshape(()), grad_x, *[out["grad"][n] for n in order], *[out["delta"][n] for n in order],
            *[out["m"][n] for n in order], *[out["v"][n] for n in order])
```
